```python
import math
import jax, jax.numpy as jnp
from jax import lax
import numpy as np

D_MODEL = 2048
BATCH = 8
SEQ = 8192
DEPTH = 1

SWA_Q_HEADS = 16
SWA_KV_HEADS = 2
SWA_HEAD_DIM = 64
SWA_WINDOW = 128
GDN_HEADS = 4
GDN_HEAD_DIM = 128
GDN_CONV = 4
GDN_CHUNK = 64
N_MEM = 256
XA_HEADS = 4
XA_HEAD_DIM = 128
D_FF = 4 * D_MODEL
N_BRANCH = 3
RMS_EPS = 1e-6
L2_EPS = 1e-6

SWA_Q_W = SWA_Q_HEADS * SWA_HEAD_DIM
SWA_KV_W = SWA_KV_HEADS * SWA_HEAD_DIM
GDN_W = GDN_HEADS * GDN_HEAD_DIM
XA_W = XA_HEADS * XA_HEAD_DIM
IN_SPLITS = (SWA_Q_W, SWA_KV_W, SWA_KV_W, 3 * GDN_W, GDN_HEADS, GDN_HEADS, GDN_W, XA_W, N_BRANCH * D_MODEL)
IN_WIDTH = sum(IN_SPLITS)

kernel_name = 'hybrid_swa_sink_gdn_memxattn_relu2_block'


def rms_norm(x, g):
    xf = x.astype(jnp.float32)
    y = xf * lax.rsqrt(jnp.mean(xf * xf, axis=-1, keepdims=True) + RMS_EPS)
    return (y * g.astype(jnp.float32)).astype(x.dtype)


def l2_norm(x):
    return x * lax.rsqrt(jnp.sum(x * x, axis=-1, keepdims=True) + L2_EPS)


def split_cols(t, sizes):
    idx, acc = [], 0
    for s in sizes[:-1]:
        acc += s
        idx.append(acc)
    return jnp.split(t, idx, axis=-1)


def sliding_window_attention(q, k, v, sinks):
    B, S, HQ, hd = q.shape
    HKV = k.shape[2]
    G = HQ // HKV
    W = SWA_WINDOW
    nb = S // W
    qb = q.reshape(B, nb, W, HKV, G, hd)
    kb = k.reshape(B, nb, W, HKV, hd)
    vb = v.reshape(B, nb, W, HKV, hd)

    def with_prev(t):
        prev = jnp.pad(t, ((0, 0), (1, 0), (0, 0), (0, 0), (0, 0)))[:, :-1]
        return jnp.concatenate([prev, t], axis=2)

    kc, vc = with_prev(kb), with_prev(vb)
    s = jnp.einsum('bnqhgd,bnkhd->bnhgqk', qb, kc).astype(jnp.float32) * (hd ** -0.5)
    qi = jnp.arange(W)[:, None]
    kj = jnp.arange(2 * W)[None, :]
    band = (kj > qi) & (kj <= qi + W)
    not_pad = (jnp.arange(nb)[:, None, None] > 0) | (kj >= W)[None]
    mask = band[None] & not_pad
    s = jnp.where(mask[None, :, None, None], s, -jnp.inf)
    sk = sinks.astype(jnp.float32).reshape(HKV, G)[None, None, :, :, None, None]
    m = jnp.maximum(jnp.max(s, axis=-1, keepdims=True), sk)
    p = jnp.exp(s - m)
    denom = jnp.sum(p, axis=-1, keepdims=True) + jnp.exp(sk - m)
    pr = (p / denom).astype(v.dtype)
    o = jnp.einsum('bnhgqk,bnkhd->bnqhgd', pr, vc)
    return o.reshape(B, S, HQ * hd)


def causal_depthwise_conv(x, w):
    K, C = w.shape
    return lax.conv_general_dilated(
        x, w.reshape(K, 1, C), window_strides=(1,), padding=[(K - 1, 0)],
        dimension_numbers=('NWC', 'WIO', 'NWC'), feature_group_count=C)


def chunked_gated_delta_rule(q, k, v, g, beta):
    B, S, H, dk = q.shape
    dv = v.shape[-1]
    C = GDN_CHUNK
    N = S // C

    def to_chunks(t):
        return t.reshape(B, N, C, H, -1).transpose(1, 0, 3, 2, 4)

    qc, kc, vc = to_chunks(q), to_chunks(k), to_chunks(v)
    gc = g.reshape(B, N, C, H).transpose(1, 0, 3, 2)
    bc = beta.reshape(B, N, C, H).transpose(1, 0, 3, 2)
    gcum = jnp.cumsum(gc, axis=-1)
    causal = jnp.tril(jnp.ones((C, C), dtype=bool))
    strict = jnp.tril(jnp.ones((C, C), dtype=bool), k=-1)
    decay = jnp.exp(jnp.where(causal, gcum[..., :, None] - gcum[..., None, :], -jnp.inf))
    kk = jnp.einsum('nbhcd,nbhed->nbhce', kc, kc)
    lower = jnp.where(strict, bc[..., :, None] * kk * decay, 0.0)
    a_mat = jnp.eye(C, dtype=q.dtype) + lower
    rhs = jnp.concatenate([vc * bc[..., None], kc * (bc * jnp.exp(gcum))[..., None]], axis=-1)
    sol = lax.linalg.triangular_solve(a_mat, rhs, left_side=True, lower=True, unit_diagonal=True)
    u, w = sol[..., :dv], sol[..., dv:]
    qk = jnp.einsum('nbhcd,nbhed->nbhce', qc, kc) * decay
    q_dec = qc * jnp.exp(gcum)[..., None]
    k_dec = kc * jnp.exp(gcum[..., -1:] - gcum)[..., None]
    g_last = jnp.exp(gcum[..., -1])

    def step(state, inp):
        qk_i, qd_i, kd_i, u_i, w_i, gl_i = inp
        v_new = u_i - jnp.einsum('bhcd,bhde->bhce', w_i, state)
        o = jnp.einsum('bhcd,bhde->bhce', qd_i, state) + jnp.einsum('bhce,bhef->bhcf', qk_i, v_new)
        state = state * gl_i[..., None, None] + jnp.einsum('bhcd,bhce->bhde', kd_i, v_new)
        return state, o

    state0 = jnp.zeros((B, H, dk, dv), dtype=q.dtype)
    _, o = lax.scan(step, state0, (qk, q_dec, k_dec, u, w, g_last))
    return o.transpose(1, 0, 3, 2, 4).reshape(B, S, H, dv)


def gated_deltanet(qkv, a, b, z, conv_w, a_log, dt_bias, norm_w):
    B, S, _ = qkv.shape
    H, dh = GDN_HEADS, GDN_HEAD_DIM
    f32 = jnp.float32
    qkv = jax.nn.silu(causal_depthwise_conv(qkv, conv_w))
    q, k, v = jnp.split(qkv, 3, axis=-1)
    q = l2_norm(q.reshape(B, S, H, dh).astype(f32)) * (dh ** -0.5)
    k = l2_norm(k.reshape(B, S, H, dh).astype(f32))
    v = v.reshape(B, S, H, dh).astype(f32)
    beta = jax.nn.sigmoid(b.astype(f32))
    g = -jnp.exp(a_log.astype(f32)) * jax.nn.softplus(a.astype(f32) + dt_bias.astype(f32))
    o = chunked_gated_delta_rule(q, k, v, g, beta)
    o = rms_norm(o, norm_w) * jax.nn.silu(z.reshape(B, S, H, dh).astype(f32))
    return o.reshape(B, S, H * dh).astype(qkv.dtype)


def memory_cross_attention(q, mkv):
    B, S, _ = q.shape
    q = q.reshape(B, S, XA_HEADS, XA_HEAD_DIM)
    mk, mv = jnp.split(mkv, 2, axis=-1)
    mk = mk.reshape(B, N_MEM, XA_HEADS, XA_HEAD_DIM)
    mv = mv.reshape(B, N_MEM, XA_HEADS, XA_HEAD_DIM)
    s = jnp.einsum('bshd,bmhd->bhsm', q, mk).astype(jnp.float32) * (XA_HEAD_DIM ** -0.5)
    p = jax.nn.softmax(s, axis=-1).astype(mv.dtype)
    return jnp.einsum('bhsm,bmhd->bshd', p, mv).reshape(B, S, XA_W)


def _fwd_setup_inputs(seed: int = 0) -> dict:
    key = jax.random.key(seed)
    ks = jax.random.split(key, 20)
    f32 = jnp.float32
    L, D = DEPTH, D_MODEL

    def nrm(k, shape, scale):
        return jax.random.normal(k, shape, f32) * scale

    x = nrm(ks[0], (BATCH, SEQ, D), 1.0)
    mem = nrm(ks[1], (BATCH, N_MEM, D), 1.0)
    g_mix = 1.0 + nrm(ks[2], (L, D), 0.02)
    w_in = nrm(ks[3], (L, D, IN_WIDTH), D ** -0.5)
    sinks = nrm(ks[4], (L, SWA_Q_HEADS), 0.5)
    conv_w = nrm(ks[5], (L, GDN_CONV, 3 * GDN_W), GDN_CONV ** -0.5)
    a_log = jnp.log(jax.random.uniform(ks[6], (L, GDN_HEADS), f32, 1.0, 16.0))
    dt = jnp.exp(jax.random.uniform(ks[7], (L, GDN_HEADS), f32, math.log(1e-3), math.log(1e-1)))
    dt_bias = dt + jnp.log(-jnp.expm1(-dt))
    gdn_norm_w = 1.0 + nrm(ks[8], (L, GDN_HEAD_DIM), 0.02)
    g_mem = 1.0 + nrm(ks[9], (L, D), 0.02)
    w_mem_kv = nrm(ks[10], (L, D, 2 * XA_W), D ** -0.5)
    w_swa_up = nrm(ks[11], (L, SWA_Q_W, D), SWA_Q_W ** -0.5)
    w_gdn_up = nrm(ks[12], (L, GDN_W, D), GDN_W ** -0.5)
    w_xa_up = nrm(ks[13], (L, XA_W, D), XA_W ** -0.5)
    w_out = nrm(ks[14], (L, D, D), D ** -0.5)
    g_mlp = 1.0 + nrm(ks[15], (L, D), 0.02)
    w_mlp_in = nrm(ks[16], (L, D, D_FF), D ** -0.5)
    w_mlp_out = nrm(ks[17], (L, D_FF, D), D_FF ** -0.5)
    g_final = 1.0 + nrm(ks[18], (D,), 0.02)
    return {'x': x, 'mem': mem, 'g_mix': g_mix, 'w_in': w_in, 'sinks': sinks, 'conv_w': conv_w,
            'a_log': a_log, 'dt_bias': dt_bias, 'gdn_norm_w': gdn_norm_w, 'g_mem': g_mem,
            'w_mem_kv': w_mem_kv, 'w_swa_up': w_swa_up, 'w_gdn_up': w_gdn_up, 'w_xa_up': w_xa_up,
            'w_out': w_out, 'g_mlp': g_mlp, 'w_mlp_in': w_mlp_in, 'w_mlp_out': w_mlp_out,
            'g_final': g_final}


def _fwd_reference(x, mem, g_mix, w_in, sinks, conv_w, a_log, dt_bias, gdn_norm_w, g_mem, w_mem_kv,
              w_swa_up, w_gdn_up, w_xa_up, w_out, g_mlp, w_mlp_in, w_mlp_out, g_final):
    B, S, D = x.shape
    h = x
    for l in range(DEPTH):
        n = rms_norm(h, g_mix[l])
        p = n @ w_in[l]
        q_a, k_a, v_a, qkv_b, a_b, b_b, z_b, q_c, gate_logits = split_cols(p, IN_SPLITS)
        y_a = sliding_window_attention(
            q_a.reshape(B, S, SWA_Q_HEADS, SWA_HEAD_DIM),
            k_a.reshape(B, S, SWA_KV_HEADS, SWA_HEAD_DIM),
            v_a.reshape(B, S, SWA_KV_HEADS, SWA_HEAD_DIM), sinks[l])
        y_b = gated_deltanet(qkv_b, a_b, b_b, z_b, conv_w[l], a_log[l], dt_bias[l], gdn_norm_w[l])
        mkv = rms_norm(mem, g_mem[l]) @ w_mem_kv[l]
        y_c = memory_cross_attention(q_c, mkv)
        g_a, g_b, g_c = jnp.split(jax.nn.sigmoid(gate_logits), N_BRANCH, axis=-1)
        merged = g_a * (y_a @ w_swa_up[l]) + g_b * (y_b @ w_gdn_up[l]) + g_c * (y_c @ w_xa_up[l])
        h = h + merged @ w_out[l]
        u = rms_norm(h, g_mlp[l]) @ w_mlp_in[l]
        h = h + jnp.square(jax.nn.relu(u)) @ w_mlp_out[l]
    return rms_norm(h, g_final)


import jax as _jax
import jax.numpy as _jnp

TWIN_FORMAT = 'train_step'
FWD_PARAMS = ['x', 'mem', 'g_mix', 'w_in', 'sinks', 'conv_w', 'a_log', 'dt_bias', 'gdn_norm_w', 'g_mem', 'w_mem_kv', 'w_swa_up', 'w_gdn_up', 'w_xa_up', 'w_out', 'g_mlp', 'w_mlp_in', 'w_mlp_out', 'g_final']
TWIN_WEIGHTS = ['g_mix', 'w_in', 'sinks', 'conv_w', 'a_log', 'dt_bias', 'gdn_norm_w', 'g_mem', 'w_mem_kv', 'w_swa_up', 'w_gdn_up', 'w_xa_up', 'w_out', 'g_mlp', 'w_mlp_in', 'w_mlp_out', 'g_final']
TWIN_DIFF_INPUT = 'x'
TWIN_INPUTS = ['x', 'mem', 'g_mix', 'w_in', 'sinks', 'conv_w', 'a_log', 'dt_bias', 'gdn_norm_w', 'g_mem', 'w_mem_kv', 'w_swa_up', 'w_gdn_up', 'w_xa_up', 'w_out', 'g_mlp', 'w_mlp_in', 'w_mlp_out', 'g_final', 'loss_target', 'm_g_mix', 'm_w_in', 'm_sinks', 'm_conv_w', 'm_a_log', 'm_dt_bias', 'm_gdn_norm_w', 'm_g_mem', 'm_w_mem_kv', 'm_w_swa_up', 'm_w_gdn_up', 'm_w_xa_up', 'm_w_out', 'm_g_mlp', 'm_w_mlp_in', 'm_w_mlp_out', 'm_g_final', 'v_g_mix', 'v_w_in', 'v_sinks', 'v_conv_w', 'v_a_log', 'v_dt_bias', 'v_gdn_norm_w', 'v_g_mem', 'v_w_mem_kv', 'v_w_swa_up', 'v_w_gdn_up', 'v_w_xa_up', 'v_w_out', 'v_g_mlp', 'v_w_mlp_in', 'v_w_mlp_out', 'v_g_final']
TWIN_OUTPUTS = ['loss', 'grad_x', 'grad_g_mix', 'grad_w_in', 'grad_sinks', 'grad_conv_w', 'grad_a_log', 'grad_dt_bias', 'grad_gdn_norm_w', 'grad_g_mem', 'grad_w_mem_kv', 'grad_w_swa_up', 'grad_w_gdn_up', 'grad_w_xa_up', 'grad_w_out', 'grad_g_mlp', 'grad_w_mlp_in', 'grad_w_mlp_out', 'grad_g_final', 'delta_g_mix', 'delta_w_in', 'delta_sinks', 'delta_conv_w', 'delta_a_log', 'delta_dt_bias', 'delta_gdn_norm_w', 'delta_g_mem', 'delta_w_mem_kv', 'delta_w_swa_up', 'delta_w_gdn_up', 'delta_w_xa_up', 'delta_w_out', 'delta_g_mlp', 'delta_w_mlp_in', 'delta_w_mlp_out', 'delta_g_final', 'new_m_g_mix', 'new_m_w_in', 'new_m_sinks', 'new_m_conv_w', 'new_m_a_log', 'new_m_dt_bias', 'new_m_gdn_norm_w', 'new_m_g_mem', 'new_m_w_mem_kv', 'new_m_w_swa_up', 'new_m_w_gdn_up', 'new_m_w_xa_up', 'new_m_w_out', 'new_m_g_mlp', 'new_m_w_mlp_in', 'new_m_w_mlp_out', 'new_m_g_final', 'new_v_g_mix', 'new_v_w_in', 'new_v_sinks', 'new_v_conv_w', 'new_v_a_log', 'new_v_dt_bias', 'new_v_gdn_norm_w', 'new_v_g_mem', 'new_v_w_mem_kv', 'new_v_w_swa_up', 'new_v_w_gdn_up', 'new_v_w_xa_up', 'new_v_w_out', 'new_v_g_mlp', 'new_v_w_mlp_in', 'new_v_w_mlp_out', 'new_v_g_final']
TWIN_LEAF_KINDS = {'loss': 'loss', 'grad_x': 'grad_x', 'grad_g_mix': 'grad_w', 'grad_w_in': 'grad_w', 'grad_sinks': 'grad_w', 'grad_conv_w': 'grad_w', 'grad_a_log': 'grad_w', 'grad_dt_bias': 'grad_w', 'grad_gdn_norm_w': 'grad_w', 'grad_g_mem': 'grad_w', 'grad_w_mem_kv': 'grad_w', 'grad_w_swa_up': 'grad_w', 'grad_w_gdn_up': 'grad_w', 'grad_w_xa_up': 'grad_w', 'grad_w_out': 'grad_w', 'grad_g_mlp': 'grad_w', 'grad_w_mlp_in': 'grad_w', 'grad_w_mlp_out': 'grad_w', 'grad_g_final': 'grad_w', 'delta_g_mix': 'delta_w', 'delta_w_in': 'delta_w', 'delta_sinks': 'delta_w', 'delta_conv_w': 'delta_w', 'delta_a_log': 'delta_w', 'delta_dt_bias': 'delta_w', 'delta_gdn_norm_w': 'delta_w', 'delta_g_mem': 'delta_w', 'delta_w_mem_kv': 'delta_w', 'delta_w_swa_up': 'delta_w', 'delta_w_gdn_up': 'delta_w', 'delta_w_xa_up': 'delta_w', 'delta_w_out': 'delta_w', 'delta_g_mlp': 'delta_w', 'delta_w_mlp_in': 'delta_w', 'delta_w_mlp_out': 'delta_w', 'delta_g_final': 'delta_w', 'new_m_g_mix': 'new_m', 'new_m_w_in': 'new_m', 'new_m_sinks': 'new_m', 'new_m_conv_w': 'new_m', 'new_m_a_log': 'new_m', 'new_m_dt_bias': 'new_m', 'new_m_gdn_norm_w': 'new_m', 'new_m_g_mem': 'new_m', 'new_m_w_mem_kv': 'new_m', 'new_m_w_swa_up': 'new_m', 'new_m_w_gdn_up': 'new_m', 'new_m_w_xa_up': 'new_m', 'new_m_w_out': 'new_m', 'new_m_g_mlp': 'new_m', 'new_m_w_mlp_in': 'new_m', 'new_m_w_mlp_out': 'new_m', 'new_m_g_final': 'new_m', 'new_v_g_mix': 'new_v', 'new_v_w_in': 'new_v', 'new_v_sinks': 'new_v', 'new_v_conv_w': 'new_v', 'new_v_a_log': 'new_v', 'new_v_dt_bias': 'new_v', 'new_v_gdn_norm_w': 'new_v', 'new_v_g_mem': 'new_v', 'new_v_w_mem_kv': 'new_v', 'new_v_w_swa_up': 'new_v', 'new_v_w_gdn_up': 'new_v', 'new_v_w_xa_up': 'new_v', 'new_v_w_out': 'new_v', 'new_v_g_mlp': 'new_v', 'new_v_w_mlp_in': 'new_v', 'new_v_w_mlp_out': 'new_v', 'new_v_g_final': 'new_v'}


def _forward(args):
    return _fwd_reference(*[args[k] for k in FWD_PARAMS])


def _output_shape():
    def fwd():
        inp = _fwd_setup_inputs(0)
        return _fwd_reference(*[inp[k] for k in FWD_PARAMS])
    out = _jax.eval_shape(fwd)
    return out.shape, out.dtype

N_MICROBATCH = 1
ADAM_LR = 0.001
ADAM_B1 = 0.9
ADAM_B2 = 0.999
ADAM_EPS = 1e-08
ADAM_WD = 0.01
ADAM_STEP = 10
PER_EXAMPLE_BATCH_AXIS = {'x': 0, 'mem': 0, 'loss_target': 0}
SHARED_INPUTS = []
_WEIGHT_DTYPES = {'g_mix': _jnp.float32, 'w_in': _jnp.float32, 'sinks': _jnp.float32, 'conv_w': _jnp.float32, 'a_log': _jnp.float32, 'dt_bias': _jnp.float32, 'gdn_norm_w': _jnp.float32, 'g_mem': _jnp.float32, 'w_mem_kv': _jnp.float32, 'w_swa_up': _jnp.float32, 'w_gdn_up': _jnp.float32, 'w_xa_up': _jnp.float32, 'w_out': _jnp.float32, 'g_mlp': _jnp.float32, 'w_mlp_in': _jnp.float32, 'w_mlp_out': _jnp.float32, 'g_final': _jnp.float32}
MOMENT_SCALE = {'g_mix': 7.501677e-02, 'w_in': 3.385951e-02, 'sinks': 1.735814e-02, 'conv_w': 7.085770e-02, 'a_log': 7.913221e-01, 'dt_bias': 7.800373e-01, 'gdn_norm_w': 1.785274e-01, 'g_mem': 1.073498e-02, 'w_mem_kv': 1.467182e-02, 'w_swa_up': 1.432564e-02, 'w_gdn_up': 4.266437e-02, 'w_xa_up': 7.389357e-03, 'w_out': 4.415521e-02, 'g_mlp': 1.105810e-01, 'w_mlp_in': 5.488921e-02, 'w_mlp_out': 1.155662e-01, 'g_final': 3.235105e+01}


def _to_microbatches(a, axis):
    t = _jnp.moveaxis(a, axis, 0)
    t = t.reshape((N_MICROBATCH, t.shape[0] // N_MICROBATCH) + t.shape[1:])
    return _jnp.moveaxis(t, 1, axis + 1)


def setup_inputs(seed: int = 0) -> dict:
    inp = _fwd_setup_inputs(seed)
    key = _jax.random.fold_in(_jax.random.key(seed), 7919)
    shape, _ = _output_shape()
    out = dict(inp)
    out["loss_target"] = _jax.random.normal(_jax.random.fold_in(key, 0), shape, _jnp.float32)
    for i, name in enumerate(TWIN_WEIGHTS):
        w = inp[name].astype(_jnp.float32)
        if MOMENT_SCALE is None:
            s = _jnp.sqrt(_jnp.mean(_jnp.square(w)) + 1e-30)
        else:
            s = MOMENT_SCALE[name]
        km, kv = _jax.random.split(_jax.random.fold_in(key, i + 1))
        out[name] = w
        out["m_" + name] = s * _jax.random.normal(km, w.shape, _jnp.float32)
        out["v_" + name] = (s * s) * _jax.random.uniform(kv, w.shape, _jnp.float32, 0.5, 1.5)
    if N_MICROBATCH > 1:
        for name, axis in PER_EXAMPLE_BATCH_AXIS.items():
            out[name] = _to_microbatches(out[name], axis)
    return {'x': out['x'], 'mem': out['mem'], 'g_mix': out['g_mix'], 'w_in': out['w_in'], 'sinks': out['sinks'], 'conv_w': out['conv_w'], 'a_log': out['a_log'], 'dt_bias': out['dt_bias'], 'gdn_norm_w': out['gdn_norm_w'], 'g_mem': out['g_mem'], 'w_mem_kv': out['w_mem_kv'], 'w_swa_up': out['w_swa_up'], 'w_gdn_up': out['w_gdn_up'], 'w_xa_up': out['w_xa_up'], 'w_out': out['w_out'], 'g_mlp': out['g_mlp'], 'w_mlp_in': out['w_mlp_in'], 'w_mlp_out': out['w_mlp_out'], 'g_final': out['g_final'], 'loss_target': out['loss_target'], 'm_g_mix': out['m_g_mix'], 'm_w_in': out['m_w_in'], 'm_sinks': out['m_sinks'], 'm_conv_w': out['m_conv_w'], 'm_a_log': out['m_a_log'], 'm_dt_bias': out['m_dt_bias'], 'm_gdn_norm_w': out['m_gdn_norm_w'], 'm_g_mem': out['m_g_mem'], 'm_w_mem_kv': out['m_w_mem_kv'], 'm_w_swa_up': out['m_w_swa_up'], 'm_w_gdn_up': out['m_w_gdn_up'], 'm_w_xa_up': out['m_w_xa_up'], 'm_w_out': out['m_w_out'], 'm_g_mlp': out['m_g_mlp'], 'm_w_mlp_in': out['m_w_mlp_in'], 'm_w_mlp_out': out['m_w_mlp_out'], 'm_g_final': out['m_g_final'], 'v_g_mix': out['v_g_mix'], 'v_w_in': out['v_w_in'], 'v_sinks': out['v_sinks'], 'v_conv_w': out['v_conv_w'], 'v_a_log': out['v_a_log'], 'v_dt_bias': out['v_dt_bias'], 'v_gdn_norm_w': out['v_gdn_norm_w'], 'v_g_mem': out['v_g_mem'], 'v_w_mem_kv': out['v_w_mem_kv'], 'v_w_swa_up': out['v_w_swa_up'], 'v_w_gdn_up': out['v_w_gdn_up'], 'v_w_xa_up': out['v_w_xa_up'], 'v_w_out': out['v_w_out'], 'v_g_mlp': out['v_g_mlp'], 'v_w_mlp_in': out['v_w_mlp_in'], 'v_w_mlp_out': out['v_w_mlp_out'], 'v_g_final': out['v_g_final']}


def _loss(weights, diff, rest, loss_target):
    with _jax.named_scope("forward"):
        args = {**rest, TWIN_DIFF_INPUT: diff, **{k: w.astype(_WEIGHT_DTYPES[k]) for k, w in weights.items()}}
        y = _forward(args)
    with _jax.named_scope("loss_head"):
        err = _jnp.square(y.astype(_jnp.float32) - loss_target)
        return 0.5 * _jnp.sum(_jnp.mean(err, axis=-1)) if err.ndim else 0.5 * err


def _adamw(w, g, m, v):
    m = ADAM_B1 * m + (1.0 - ADAM_B1) * g
    v = ADAM_B2 * v + (1.0 - ADAM_B2) * _jnp.square(g)
    m_hat = m / (1.0 - ADAM_B1 ** ADAM_STEP)
    v_hat = v / (1.0 - ADAM_B2 ** ADAM_STEP)
    delta = -ADAM_LR * (m_hat / (_jnp.sqrt(v_hat) + ADAM_EPS) + ADAM_WD * w)
    return delta, m, v


def reference(x, mem, g_mix, w_in, sinks, conv_w, a_log, dt_bias, gdn_norm_w, g_mem, w_mem_kv, w_swa_up, w_gdn_up, w_xa_up, w_out, g_mlp, w_mlp_in, w_mlp_out, g_final, loss_target, m_g_mix, m_w_in, m_sinks, m_conv_w, m_a_log, m_dt_bias, m_gdn_norm_w, m_g_mem, m_w_mem_kv, m_w_swa_up, m_w_gdn_up, m_w_xa_up, m_w_out, m_g_mlp, m_w_mlp_in, m_w_mlp_out, m_g_final, v_g_mix, v_w_in, v_sinks, v_conv_w, v_a_log, v_dt_bias, v_gdn_norm_w, v_g_mem, v_w_mem_kv, v_w_swa_up, v_w_gdn_up, v_w_xa_up, v_w_out, v_g_mlp, v_w_mlp_in, v_w_mlp_out, v_g_final):
    given = dict(x=x, mem=mem, g_mix=g_mix, w_in=w_in, sinks=sinks, conv_w=conv_w, a_log=a_log, dt_bias=dt_bias, gdn_norm_w=gdn_norm_w, g_mem=g_mem, w_mem_kv=w_mem_kv, w_swa_up=w_swa_up, w_gdn_up=w_gdn_up, w_xa_up=w_xa_up, w_out=w_out, g_mlp=g_mlp, w_mlp_in=w_mlp_in, w_mlp_out=w_mlp_out, g_final=g_final, loss_target=loss_target, m_g_mix=m_g_mix, m_w_in=m_w_in, m_sinks=m_sinks, m_conv_w=m_conv_w, m_a_log=m_a_log, m_dt_bias=m_dt_bias, m_gdn_norm_w=m_gdn_norm_w, m_g_mem=m_g_mem, m_w_mem_kv=m_w_mem_kv, m_w_swa_up=m_w_swa_up, m_w_gdn_up=m_w_gdn_up, m_w_xa_up=m_w_xa_up, m_w_out=m_w_out, m_g_mlp=m_g_mlp, m_w_mlp_in=m_w_mlp_in, m_w_mlp_out=m_w_mlp_out, m_g_final=m_g_final, v_g_mix=v_g_mix, v_w_in=v_w_in, v_sinks=v_sinks, v_conv_w=v_conv_w, v_a_log=v_a_log, v_dt_bias=v_dt_bias, v_gdn_norm_w=v_gdn_norm_w, v_g_mem=v_g_mem, v_w_mem_kv=v_w_mem_kv, v_w_swa_up=v_w_swa_up, v_w_gdn_up=v_w_gdn_up, v_w_xa_up=v_w_xa_up, v_w_out=v_w_out, v_g_mlp=v_g_mlp, v_w_mlp_in=v_w_mlp_in, v_w_mlp_out=v_w_mlp_out, v_g_final=v_g_final)
    weights = {n: given[n] for n in TWIN_WEIGHTS}
    shared = {n: given[n] for n in SHARED_INPUTS}
    per_example = {n: given[n] for n in ['x', 'mem']}
    grad_fn = _jax.value_and_grad(_loss, argnums=(0, 1))

    def one_microbatch(ex, loss_target):
        ex = dict(ex)
        diff = ex.pop(TWIN_DIFF_INPUT)
        return grad_fn(weights, diff, {**shared, **ex}, loss_target)

    if N_MICROBATCH == 1:
        loss, (grad_w, grad_x) = one_microbatch(per_example, given["loss_target"])
    else:
        def body(carry, xs):
            loss_sum, grad_sum = carry
            l_k, (gw_k, gx_k) = one_microbatch(xs[0], xs[1])
            with _jax.named_scope("update"):
                return (loss_sum + l_k, _jax.tree.map(_jnp.add, grad_sum, gw_k)), gx_k

        init = (_jnp.zeros((), _jnp.float32), _jax.tree.map(_jnp.zeros_like, weights))
        (loss, grad_w), grad_x = _jax.lax.scan(body, init, (per_example, given["loss_target"]))
    with _jax.named_scope("update"):
        delta_w, new_m, new_v = {}, {}, {}
        for n in TWIN_WEIGHTS:
            delta_w[n], new_m[n], new_v[n] = _adamw(weights[n], grad_w[n], given["m_" + n], given["v_" + n])
    return (loss, grad_x, *[grad_w[n] for n in TWIN_WEIGHTS], *[delta_w[n] for n in TWIN_WEIGHTS],
            *[new_m[n] for n in TWIN_WEIGHTS], *[new_v[n] for n in TWIN_WEIGHTS])
```

```python
import functools
import math

import jax
import jax.numpy as jnp
from jax import lax
from jax.experimental import pallas as pl
from jax.experimental.pallas import tpu as pltpu

F32 = jnp.float32
BF16 = jnp.bfloat16
HI = lax.Precision.HIGHEST
MESH = pl.DeviceIdType.MESH

SWA_Q_HEADS = 16
SWA_KV_HEADS = 2
SWA_HEAD_DIM = 64
SWA_WINDOW = 128
GDN_HEADS = 4
GDN_HEAD_DIM = 128
GDN_CONV = 4
GDN_CHUNK = 64
XA_HEADS = 4
XA_HEAD_DIM = 128
RMS_EPS = 1e-6
L2_EPS = 1e-6
ADAM_LR = 0.001
ADAM_B1 = 0.9
ADAM_B2 = 0.999
ADAM_EPS = 1e-08
ADAM_WD = 0.01
ADAM_STEP = 10

LANES = 128
VMEM_LIMIT = 56 * 1024 * 1024

NT = (((1,), (1,)), ((), ()))
TN = (((0,), (0,)), ((), ()))
NN = (((1,), (0,)), ((), ()))


def _cp(sem=None):
    return pltpu.CompilerParams(dimension_semantics=sem, vmem_limit_bytes=VMEM_LIMIT)


def _blk(dim, pref):
    if dim <= pref:
        return dim
    b = (pref // LANES) * LANES
    while dim % b:
        b -= LANES
    assert b > 0, (dim, pref)
    return b


def _dot(a, b, dims=NN, precision=None):
    return lax.dot_general(a, b, dims, precision=precision, preferred_element_type=F32)


def _sigmoid(x):
    return 1.0 / (1.0 + jnp.exp(-x))


def _mm(a, b, *, name, ta=False, tb=False, out_dtypes=(F32,), epilogue=None, extras=(), tm=1024, tn=1024, tk=512):
    (kdim, m) = a.shape if ta else a.shape[::-1]
    (k2, n) = b.shape[::-1] if tb else b.shape
    assert kdim == k2, (a.shape, b.shape, ta, tb)
    tm, tn, tk = _blk(m, tm), _blk(n, tn), _blk(kdim, tk)
    nk = kdim // tk
    a_spec = pl.BlockSpec((tk, tm), lambda i, j, k: (k, i)) if ta else pl.BlockSpec((tm, tk), lambda i, j, k: (i, k))
    b_spec = pl.BlockSpec((tn, tk), lambda i, j, k: (j, k)) if tb else pl.BlockSpec((tk, tn), lambda i, j, k: (k, j))
    o_spec = pl.BlockSpec((tm, tn), lambda i, j, k: (i, j))
    dims = ((((0 if ta else 1),), ((1 if tb else 0),)), ((), ()))
    n_extra, n_out = len(extras), len(out_dtypes)

    def body(*refs):
        a_ref, b_ref = refs[:2]
        extra_refs = refs[2:2 + n_extra]
        out_refs = refs[2 + n_extra:2 + n_extra + n_out]
        acc_ref = refs[-1]
        k = pl.program_id(2)
        part = _dot(a_ref[...].astype(BF16), b_ref[...].astype(BF16), dims)

        @pl.when(k == 0)
        def _():
            acc_ref[...] = part

        @pl.when(k > 0)
        def _():
            acc_ref[...] += part

        @pl.when(k == nk - 1)
        def _():
            acc = acc_ref[...]
            vals = epilogue(acc, *[r[...] for r in extra_refs]) if epilogue is not None else (acc,)
            for r, v in zip(out_refs, vals):
                r[...] = v.astype(r.dtype)

    outs = pl.pallas_call(
        body,
        grid=(m // tm, n // tn, nk),
        in_specs=[a_spec, b_spec] + [o_spec] * n_extra,
        out_specs=[o_spec] * n_out,
        out_shape=[jax.ShapeDtypeStruct((m, n), d) for d in out_dtypes],
        scratch_shapes=[pltpu.VMEM((tm, tn), F32)],
        compiler_params=_cp(("parallel", "parallel", "arbitrary")),
        name=name,
    )(a, b, *extras)
    return outs[0] if n_out == 1 else outs


def _rms_fwd(x, g, *, name, tm=512):
    t, d = x.shape
    tm = _blk(t, tm)

    def body(x_ref, g_ref, n_ref):
        xv = x_ref[...]
        r = lax.rsqrt(jnp.mean(xv * xv, axis=-1, keepdims=True) + RMS_EPS)
        n_ref[...] = (xv * r * g_ref[...]).astype(n_ref.dtype)

    return pl.pallas_call(
        body, grid=(t // tm,),
        in_specs=[pl.BlockSpec((tm, d), lambda i: (i, 0)), pl.BlockSpec((1, d), lambda i: (0, 0))],
        out_specs=pl.BlockSpec((tm, d), lambda i: (i, 0)),
        out_shape=jax.ShapeDtypeStruct((t, d), BF16),
        compiler_params=_cp(("parallel",)), name=name,
    )(x, g)


def _rms_bwd(dn, x, g, dres, *, name, tm=512):
    t, d = x.shape
    tm = _blk(t, tm)

    def body(dn_ref, x_ref, g_ref, dres_ref, dx_ref, dg_ref):
        i = pl.program_id(0)
        xv = x_ref[...]
        r = lax.rsqrt(jnp.mean(xv * xv, axis=-1, keepdims=True) + RMS_EPS)
        xh = xv * r
        dnv = dn_ref[...].astype(F32)
        dxh = dnv * g_ref[...]
        dx_ref[...] = dres_ref[...] + r * (dxh - xh * jnp.mean(dxh * xh, axis=-1, keepdims=True))
        part = jnp.sum(dnv * xh, axis=0, keepdims=True)

        @pl.when(i == 0)
        def _():
            dg_ref[...] = part

        @pl.when(i > 0)
        def _():
            dg_ref[...] += part

    row = pl.BlockSpec((tm, d), lambda i: (i, 0))
    vec = pl.BlockSpec((1, d), lambda i: (0, 0))
    return pl.pallas_call(
        body, grid=(t // tm,),
        in_specs=[row, row, vec, row], out_specs=[row, vec],
        out_shape=[jax.ShapeDtypeStruct((t, d), F32), jax.ShapeDtypeStruct((1, d), F32)],
        compiler_params=_cp(("arbitrary",)), name=name,
    )(dn, x, g, dres)


def _final_loss(h, g, tgt, *, name, tm=512):
    t, d = h.shape
    tm = _blk(t, tm)

    def body(h_ref, g_ref, t_ref, dh_ref, dg_ref, loss_ref):
        i = pl.program_id(0)
        hv = h_ref[...]
        r = lax.rsqrt(jnp.mean(hv * hv, axis=-1, keepdims=True) + RMS_EPS)
        xh = hv * r
        e = xh * g_ref[...] - t_ref[...]
        dy = e * (1.0 / d)
        dxh = dy * g_ref[...]
        dh_ref[...] = r * (dxh - xh * jnp.mean(dxh * xh, axis=-1, keepdims=True))
        dg_part = jnp.sum(dy * xh, axis=0, keepdims=True)
        row_loss = jnp.sum(e * e, axis=-1, keepdims=True) * (0.5 / d)
        loss_part = jnp.sum(row_loss, axis=0, keepdims=True)

        @pl.when(i == 0)
        def _():
            dg_ref[...] = dg_part
            loss_ref[...] = jnp.broadcast_to(loss_part, loss_ref.shape)

        @pl.when(i > 0)
        def _():
            dg_ref[...] += dg_part
            loss_ref[...] += jnp.broadcast_to(loss_part, loss_ref.shape)

    row = pl.BlockSpec((tm, d), lambda i: (i, 0))
    vec = pl.BlockSpec((1, d), lambda i: (0, 0))
    return pl.pallas_call(
        body, grid=(t // tm,),
        in_specs=[row, vec, row], out_specs=[row, vec, pl.BlockSpec((1, LANES), lambda i: (0, 0))],
        out_shape=[jax.ShapeDtypeStruct((t, d), F32), jax.ShapeDtypeStruct((1, d), F32),
                   jax.ShapeDtypeStruct((1, LANES), F32)],
        compiler_params=_cp(("arbitrary",)), name=name,
    )(h, g, tgt)


def _swa_mask(n):
    w = SWA_WINDOW
    qi = lax.broadcasted_iota(jnp.int32, (w, 2 * w), 0)
    kj = lax.broadcasted_iota(jnp.int32, (w, 2 * w), 1)
    return (kj > qi) & (kj <= qi + w) & ((n > 0) | (kj >= w))


def _swa_fwd(q, kv, sinks, *, name):
    t = q.shape[0]
    w, hd, hq, hkv = SWA_WINDOW, SWA_HEAD_DIM, SWA_Q_HEADS, SWA_KV_HEADS
    grp = hq // hkv
    kvw = hkv * hd
    nb = t // w

    def body(q_ref, kvp_ref, kvc_ref, s_ref, o_ref, lse_ref):
        n = pl.program_id(0)
        mask = _swa_mask(n)
        kvcat = jnp.concatenate([kvp_ref[...], kvc_ref[...]], axis=0)
        outs, lses = [], []
        for h in range(hq):
            hk = h // grp
            qh = q_ref[:, h * hd:(h + 1) * hd]
            kh = kvcat[:, hk * hd:(hk + 1) * hd]
            vh = kvcat[:, kvw + hk * hd:kvw + (hk + 1) * hd]
            s = _dot(qh, kh, NT) * (hd ** -0.5)
            s = jnp.where(mask, s, -jnp.inf)
            sk = s_ref[0:1, h:h + 1]
            m = jnp.maximum(jnp.max(s, axis=-1, keepdims=True), sk)
            p = jnp.exp(s - m)
            den = jnp.sum(p, axis=-1, keepdims=True) + jnp.exp(sk - m)
            pr = (p / den).astype(BF16)
            outs.append(_dot(pr, vh))
            lses.append(m + jnp.log(den))
        o_ref[...] = jnp.concatenate(outs, axis=1).astype(o_ref.dtype)
        lse_ref[...] = jnp.concatenate(lses, axis=1)

    return pl.pallas_call(
        body, grid=(nb,),
        in_specs=[pl.BlockSpec((w, hq * hd), lambda i: (i, 0)),
                  pl.BlockSpec((w, 2 * kvw), lambda i: (jnp.maximum(i - 1, 0), 0)),
                  pl.BlockSpec((w, 2 * kvw), lambda i: (i, 0)),
                  pl.BlockSpec((1, hq), lambda i: (0, 0))],
        out_specs=[pl.BlockSpec((w, hq * hd), lambda i: (i, 0)), pl.BlockSpec((w, hq), lambda i: (i, 0))],
        out_shape=[jax.ShapeDtypeStruct((t, hq * hd), BF16), jax.ShapeDtypeStruct((t, hq), F32)],
        compiler_params=_cp(("parallel",)), name=name,
    )(q, kv, kv, sinks)


def _swa_bwd(q, kv, sinks, o, lse, do, *, name):
    t = q.shape[0]
    w, hd, hq, hkv = SWA_WINDOW, SWA_HEAD_DIM, SWA_Q_HEADS, SWA_KV_HEADS
    grp = hq // hkv
    kvw = hkv * hd
    nb = t // w

    def body(q_ref, kvp_ref, kvc_ref, s_ref, o_ref, lse_ref, do_ref, dq_ref, dkv_ref, ds_ref, carry_ref):
        n = pl.program_id(0)

        @pl.when(n == 0)
        def _():
            ds_ref[...] = jnp.zeros_like(ds_ref)
            carry_ref[...] = jnp.zeros_like(carry_ref)

        @pl.when(n < nb)
        def _():
            mask = _swa_mask(n)
            kvcat = jnp.concatenate([kvp_ref[...], kvc_ref[...]], axis=0)
            dqs, dsk = [], []
            dk_acc = [None] * hkv
            dv_acc = [None] * hkv
            for h in range(hq):
                hk = h // grp
                qh = q_ref[:, h * hd:(h + 1) * hd]
                kh = kvcat[:, hk * hd:(hk + 1) * hd]
                vh = kvcat[:, kvw + hk * hd:kvw + (hk + 1) * hd]
                doh = do_ref[:, h * hd:(h + 1) * hd]
                oh = o_ref[:, h * hd:(h + 1) * hd]
                lse_h = lse_ref[:, h:h + 1]
                s = _dot(qh, kh, NT) * (hd ** -0.5)
                p = jnp.exp(jnp.where(mask, s, -jnp.inf) - lse_h)
                delta = jnp.sum(doh.astype(F32) * oh.astype(F32), axis=-1, keepdims=True)
                dp = _dot(doh, vh, NT)
                ds = (p * (dp - delta) * (hd ** -0.5)).astype(BF16)
                dqs.append(_dot(ds, kh))
                dk_h = _dot(ds, qh, TN)
                dv_h = _dot(p.astype(BF16), doh, TN)
                dk_acc[hk] = dk_h if dk_acc[hk] is None else dk_acc[hk] + dk_h
                dv_acc[hk] = dv_h if dv_acc[hk] is None else dv_acc[hk] + dv_h
                sk = s_ref[0:1, h:h + 1]
                dsk.append(jnp.sum(-jnp.exp(sk - lse_h) * delta, axis=0, keepdims=True))
            dq_ref[...] = jnp.concatenate(dqs, axis=1).astype(dq_ref.dtype)
            ds_ref[...] += jnp.concatenate(dsk, axis=1)
            dkv_cat = jnp.concatenate(dk_acc + dv_acc, axis=1)
            dkv_ref[...] = (carry_ref[...] + dkv_cat[:w]).astype(dkv_ref.dtype)
            carry_ref[...] = dkv_cat[w:]

        @pl.when(n == nb)
        def _():
            dkv_ref[...] = carry_ref[...].astype(dkv_ref.dtype)

    cur = lambda i: (jnp.minimum(i, nb - 1), 0)
    prev = lambda i: (jnp.clip(i - 1, 0, nb - 1), 0)
    return pl.pallas_call(
        body, grid=(nb + 1,),
        in_specs=[pl.BlockSpec((w, hq * hd), cur), pl.BlockSpec((w, 2 * kvw), prev), pl.BlockSpec((w, 2 * kvw), cur),
                  pl.BlockSpec((1, hq), lambda i: (0, 0)), pl.BlockSpec((w, hq * hd), cur),
                  pl.BlockSpec((w, hq), cur), pl.BlockSpec((w, hq * hd), cur)],
        out_specs=[pl.BlockSpec((w, hq * hd), cur), pl.BlockSpec((w, 2 * kvw), prev),
                   pl.BlockSpec((1, hq), lambda i: (0, 0))],
        out_shape=[jax.ShapeDtypeStruct((t, hq * hd), BF16), jax.ShapeDtypeStruct((t, 2 * kvw), BF16),
                   jax.ShapeDtypeStruct((1, hq), F32)],
        scratch_shapes=[pltpu.VMEM((w, 2 * kvw), F32)],
        compiler_params=_cp(("arbitrary",)), name=name,
    )(q, kv, kv, sinks, o, lse, do)


def _xa_fwd(q, mkv, *, name, tq=512):
    t, xw = q.shape
    nm = mkv.shape[0]
    hd, nh = XA_HEAD_DIM, XA_HEADS
    tq = _blk(t, tq)

    def body(q_ref, mkv_ref, o_ref):
        outs = []
        for h in range(nh):
            qh = q_ref[:, h * hd:(h + 1) * hd]
            kh = mkv_ref[:, h * hd:(h + 1) * hd]
            vh = mkv_ref[:, xw + h * hd:xw + (h + 1) * hd]
            s = _dot(qh, kh, NT) * (hd ** -0.5)
            p = jnp.exp(s - jnp.max(s, axis=-1, keepdims=True))
            p = p / jnp.sum(p, axis=-1, keepdims=True)
            outs.append(_dot(p.astype(BF16), vh))
        o_ref[...] = jnp.concatenate(outs, axis=1).astype(o_ref.dtype)

    return pl.pallas_call(
        body, grid=(t // tq,),
        in_specs=[pl.BlockSpec((tq, xw), lambda i: (i, 0)), pl.BlockSpec((nm, 2 * xw), lambda i: (0, 0))],
        out_specs=pl.BlockSpec((tq, xw), lambda i: (i, 0)),
        out_shape=jax.ShapeDtypeStruct((t, xw), BF16),
        compiler_params=_cp(("parallel",)), name=name,
    )(q, mkv)


def _xa_bwd(q, mkv, do, *, name, tq=512):
    t, xw = q.shape
    nm = mkv.shape[0]
    hd, nh = XA_HEAD_DIM, XA_HEADS
    tq = _blk(t, tq)

    def body(q_ref, mkv_ref, do_ref, dq_ref, dmkv_ref):
        i = pl.program_id(0)
        dqs, dks, dvs = [], [], []
        for h in range(nh):
            qh = q_ref[:, h * hd:(h + 1) * hd]
            kh = mkv_ref[:, h * hd:(h + 1) * hd]
            vh = mkv_ref[:, xw + h * hd:xw + (h + 1) * hd]
            doh = do_ref[:, h * hd:(h + 1) * hd]
            s = _dot(qh, kh, NT) * (hd ** -0.5)
            p = jnp.exp(s - jnp.max(s, axis=-1, keepdims=True))
            p = p / jnp.sum(p, axis=-1, keepdims=True)
            dp = _dot(doh, vh, NT)
            ds = (p * (dp - jnp.sum(p * dp, axis=-1, keepdims=True)) * (hd ** -0.5)).astype(BF16)
            dqs.append(_dot(ds, kh))
            dks.append(_dot(ds, qh, TN))
            dvs.append(_dot(p.astype(BF16), doh, TN))
        dq_ref[...] = jnp.concatenate(dqs, axis=1).astype(dq_ref.dtype)
        part = jnp.concatenate(dks + dvs, axis=1)

        @pl.when(i == 0)
        def _():
            dmkv_ref[...] = part

        @pl.when(i > 0)
        def _():
            dmkv_ref[...] += part

    row = pl.BlockSpec((tq, xw), lambda i: (i, 0))
    full = pl.BlockSpec((nm, 2 * xw), lambda i: (0, 0))
    return pl.pallas_call(
        body, grid=(t // tq,),
        in_specs=[row, full, row], out_specs=[row, full],
        out_shape=[jax.ShapeDtypeStruct((t, xw), BF16), jax.ShapeDtypeStruct((nm, 2 * xw), F32)],
        compiler_params=_cp(("arbitrary",)), name=name,
    )(q, mkv, do)


def _merge_specs(t, d, ys, ws, tm, tn):
    nj = d // tn
    y_specs = [pl.BlockSpec((tm, y.shape[1]), lambda i, j: (i, 0)) for y in ys]
    w_specs = [pl.BlockSpec((w.shape[0], tn), lambda i, j: (0, j)) for w in ws]
    g_specs = [pl.BlockSpec((tm, tn), functools.partial(lambda i, j, b: (i, j + b * nj), b=b)) for b in range(3)]
    return y_specs, w_specs, g_specs


def _merge_fwd(ys, ws, gates, *, name, tm=512, tn=512):
    t, d = ys[0].shape[0], ws[0].shape[1]
    tm, tn = _blk(t, tm), _blk(d, tn)
    y_specs, w_specs, g_specs = _merge_specs(t, d, ys, ws, tm, tn)

    def body(ya, yb, yc, wa, wb, wc, ga, gb, gc, o_ref):
        acc = None
        for y, w, g in ((ya, wa, ga), (yb, wb, gb), (yc, wc, gc)):
            term = _sigmoid(g[...]) * _dot(y[...], w[...])
            acc = term if acc is None else acc + term
        o_ref[...] = acc.astype(o_ref.dtype)

    return pl.pallas_call(
        body, grid=(t // tm, d // tn),
        in_specs=y_specs + w_specs + g_specs,
        out_specs=pl.BlockSpec((tm, tn), lambda i, j: (i, j)),
        out_shape=jax.ShapeDtypeStruct((t, d), BF16),
        compiler_params=_cp(("parallel", "parallel")), name=name,
    )(*ys, *ws, gates, gates, gates)


def _merge_bwd(ys, ws, gates, dmerged, *, name, tm=512, tn=512):
    t, d = ys[0].shape[0], ws[0].shape[1]
    tm, tn = _blk(t, tm), _blk(d, tn)
    y_specs, w_specs, g_specs = _merge_specs(t, d, ys, ws, tm, tn)
    tile = pl.BlockSpec((tm, tn), lambda i, j: (i, j))

    def body(ya, yb, yc, wa, wb, wc, ga, gb, gc, dm_ref, dua, dub, duc, dga, dgb, dgc):
        dm = dm_ref[...]
        for y, w, g, du, dg in ((ya, wa, ga, dua, dga), (yb, wb, gb, dub, dgb), (yc, wc, gc, duc, dgc)):
            sg = _sigmoid(g[...])
            u = _dot(y[...], w[...])
            du[...] = (dm * sg).astype(du.dtype)
            dg[...] = (dm * u * sg * (1.0 - sg)).astype(dg.dtype)

    return pl.pallas_call(
        body, grid=(t // tm, d // tn),
        in_specs=y_specs + w_specs + g_specs + [tile],
        out_specs=[tile] * 6,
        out_shape=[jax.ShapeDtypeStruct((t, d), BF16)] * 6,
        compiler_params=_cp(("parallel", "parallel")), name=name,
    )(*ys, *ws, gates, gates, gates, dmerged)


def _adamw(w, g, m, v, *, name, tm=256):
    r, c = w.shape
    tm = _blk(r, tm) if r % 8 == 0 else r
    bc1 = 1.0 - ADAM_B1 ** ADAM_STEP
    bc2 = 1.0 - ADAM_B2 ** ADAM_STEP

    def body(w_ref, g_ref, m_ref, v_ref, d_ref, nm_ref, nv_ref):
        gv = g_ref[...]
        nm = ADAM_B1 * m_ref[...] + (1.0 - ADAM_B1) * gv
        nv = ADAM_B2 * v_ref[...] + (1.0 - ADAM_B2) * (gv * gv)
        d_ref[...] = -ADAM_LR * ((nm / bc1) / (jnp.sqrt(nv / bc2) + ADAM_EPS) + ADAM_WD * w_ref[...])
        nm_ref[...] = nm
        nv_ref[...] = nv

    spec = pl.BlockSpec((tm, c), lambda i: (i, 0))
    return pl.pallas_call(
        body, grid=(r // tm,), in_specs=[spec] * 4, out_specs=[spec] * 3,
        out_shape=[jax.ShapeDtypeStruct((r, c), F32)] * 3,
        compiler_params=_cp(("parallel",)), name=name,
    )(w, g, m, v)


HALO = 8


def _shift_down(cur, prev, j):
    if j == 0:
        return cur
    y = pltpu.roll(cur, j, 0)
    row = lax.broadcasted_iota(jnp.int32, (HALO, cur.shape[1]), 0)
    top = jnp.where(row < j, pltpu.roll(prev, j, 0), y[:HALO])
    return jnp.concatenate([top, y[HALO:]], axis=0)


def _shift_up(cur, nxt, j):
    if j == 0:
        return cur
    tm = cur.shape[0]
    y = pltpu.roll(cur, tm - j, 0)
    row = lax.broadcasted_iota(jnp.int32, (HALO, cur.shape[1]), 0)
    bot = jnp.where(row >= HALO - j, pltpu.roll(nxt, HALO - j, 0), y[tm - HALO:])
    return jnp.concatenate([y[:tm - HALO], bot], axis=0)


def _softplus(x):
    return jnp.maximum(x, 0.0) + jnp.log(1.0 + jnp.exp(-jnp.abs(x)))


def _gdn_pre_fwd(qkvb, conv_w, ab, alog_pad, dt_pad, *, name, tm=256):
    t, cw = qkvb.shape
    hd, nh, ck = GDN_HEAD_DIM, GDN_HEADS, GDN_CHUNK
    gw = nh * hd
    tm = _blk(t, tm)
    hb = tm // HALO

    def body(x_ref, xp_ref, w_ref, ab_ref, al_ref, dt_ref, xc_ref, qkvn_ref, aux_ref):
        i = pl.program_id(0)
        cur = x_ref[...]
        prev = jnp.where(i > 0, xp_ref[...], 0.0)
        xc = None
        for tap in range(GDN_CONV):
            term = w_ref[tap:tap + 1, :] * _shift_down(cur, prev, GDN_CONV - 1 - tap)
            xc = term if xc is None else xc + term
        xc_ref[...] = xc
        s = xc * _sigmoid(xc)
        for h in range(2 * nh):
            xh = s[:, h * hd:(h + 1) * hd]
            r = lax.rsqrt(jnp.sum(xh * xh, axis=-1, keepdims=True) + L2_EPS)
            scale = hd ** -0.5 if h < nh else 1.0
            qkvn_ref[:, h * hd:(h + 1) * hd] = xh * (r * scale)
        qkvn_ref[:, 2 * gw:] = s[:, 2 * gw:]
        abv = ab_ref[...]
        lane = lax.broadcasted_iota(jnp.int32, abv.shape, 1)
        g = jnp.where(lane < nh, -jnp.exp(al_ref[...]) * _softplus(abv + dt_ref[...]), 0.0)
        beta = jnp.where((lane >= nh) & (lane < 2 * nh), _sigmoid(abv), 0.0)
        ii = lax.broadcasted_iota(jnp.int32, (tm, tm), 0)
        jj = lax.broadcasted_iota(jnp.int32, (tm, tm), 1)
        tri = jnp.where((ii >= jj) & ((ii ^ jj) < ck), 1.0, 0.0)
        gcum = _dot(tri, g, precision=HI)
        aux_ref[...] = g + beta + pltpu.roll(gcum, 2 * nh, 1)

    row = lambda c: pl.BlockSpec((tm, c), lambda i: (i, 0))
    vec = lambda r, c: pl.BlockSpec((r, c), lambda i: (0, 0))
    return pl.pallas_call(
        body, grid=(t // tm,),
        in_specs=[row(cw), pl.BlockSpec((HALO, cw), lambda i: (jnp.maximum(i * hb - 1, 0), 0)), vec(GDN_CONV, cw),
                  row(LANES), vec(1, LANES), vec(1, LANES)],
        out_specs=[row(cw), row(cw), row(LANES)],
        out_shape=[jax.ShapeDtypeStruct((t, cw), F32), jax.ShapeDtypeStruct((t, cw), F32),
                   jax.ShapeDtypeStruct((t, LANES), F32)],
        compiler_params=_cp(("parallel",)), name=name,
    )(qkvb, qkvb, conv_w, ab, alog_pad, dt_pad)


def _gdn_local(q, k, v, b, gc, gc_row):
    ck = GDN_CHUNK
    ii = lax.broadcasted_iota(jnp.int32, (ck, ck), 0)
    jj = lax.broadcasted_iota(jnp.int32, (ck, ck), 1)
    lower, strict = ii >= jj, ii > jj
    dmat = jnp.exp(jnp.where(lower, gc - gc_row, -jnp.inf))
    kk = _dot(k, k, NT, HI)
    lmat = jnp.where(strict, b * kk * dmat, 0.0)
    tinv = jnp.where(ii == jj, 1.0, 0.0) - lmat
    pw = lmat
    for _ in range(int(math.log2(ck)) - 1):
        pw = _dot(pw, pw, precision=HI)
        tinv = tinv + _dot(tinv, pw, precision=HI)
    gam = jnp.exp(gc)
    u = _dot(tinv, b * v, precision=HI)
    w = _dot(tinv, (b * gam) * k, precision=HI)
    qk = _dot(q, k, NT, HI)
    gl = gc[ck - 1:ck, :]
    return dict(lower=lower, strict=strict, dmat=dmat, kk=kk, tinv=tinv, gam=gam, u=u, w=w, qk=qk,
                mm=qk * dmat, gam_c=jnp.exp(gl), kdec=jnp.exp(gl - gc))


def _gdn_core_fwd(qkvn, aux, aux_t, *, name):
    t = qkvn.shape[0]
    hd, nh, ck = GDN_HEAD_DIM, GDN_HEADS, GDN_CHUNK
    gw = nh * hd
    nc = t // ck

    def body(x_ref, aux_ref, auxt_ref, o_ref, sall_ref, s_ref):
        n = pl.program_id(0)

        @pl.when(n == 0)
        def _():
            s_ref[...] = jnp.zeros_like(s_ref)

        for h in range(nh):
            q = x_ref[:, h * hd:(h + 1) * hd]
            k = x_ref[:, gw + h * hd:gw + (h + 1) * hd]
            v = x_ref[:, 2 * gw + h * hd:2 * gw + (h + 1) * hd]
            b = aux_ref[:, nh + h:nh + h + 1]
            gc = aux_ref[:, 2 * nh + h:2 * nh + h + 1]
            gc_row = auxt_ref[0, 2 * nh + h:2 * nh + h + 1, :]
            lc = _gdn_local(q, k, v, b, gc, gc_row)
            st = s_ref[h]
            sall_ref[0, h] = st
            vn = lc["u"] - _dot(lc["w"], st, precision=HI)
            o_ref[:, h * hd:(h + 1) * hd] = _dot(lc["gam"] * q, st, precision=HI) + _dot(lc["mm"], vn, precision=HI)
            s_ref[h] = lc["gam_c"] * st + _dot(lc["kdec"] * k, vn, TN, HI)

    return pl.pallas_call(
        body, grid=(nc,),
        in_specs=[pl.BlockSpec((ck, 3 * gw), lambda i: (i, 0)), pl.BlockSpec((ck, LANES), lambda i: (i, 0)),
                  pl.BlockSpec((1, 16, ck), lambda i: (i, 0, 0))],
        out_specs=[pl.BlockSpec((ck, gw), lambda i: (i, 0)), pl.BlockSpec((1, nh, hd, hd), lambda i: (i, 0, 0, 0))],
        out_shape=[jax.ShapeDtypeStruct((t, gw), F32), jax.ShapeDtypeStruct((nc, nh, hd, hd), F32)],
        scratch_shapes=[pltpu.VMEM((nh, hd, hd), F32)],
        compiler_params=_cp(("arbitrary",)), name=name,
    )(qkvn, aux, aux_t)


def _gdn_core_bwd(qkvn, aux, aux_t, s_all, do, *, name):
    t = qkvn.shape[0]
    hd, nh, ck = GDN_HEAD_DIM, GDN_HEADS, GDN_CHUNK
    gw = nh * hd
    nc = t // ck

    def body(x_ref, aux_ref, auxt_ref, sall_ref, do_ref, dx_ref, daux_ref, ds_ref):
        n = pl.program_id(0)

        @pl.when(n == 0)
        def _():
            ds_ref[...] = jnp.zeros_like(ds_ref)

        lane = lax.broadcasted_iota(jnp.int32, (ck, LANES), 1)
        rowi = lax.broadcasted_iota(jnp.int32, (ck, 1), 0)
        ones = jnp.ones((ck, LANES), F32)
        dgc_all = jnp.zeros((ck, LANES), F32)
        db_all = jnp.zeros((ck, LANES), F32)
        for h in range(nh):
            q = x_ref[:, h * hd:(h + 1) * hd]
            k = x_ref[:, gw + h * hd:gw + (h + 1) * hd]
            v = x_ref[:, 2 * gw + h * hd:2 * gw + (h + 1) * hd]
            b = aux_ref[:, nh + h:nh + h + 1]
            gc = aux_ref[:, 2 * nh + h:2 * nh + h + 1]
            gc_row = auxt_ref[0, 2 * nh + h:2 * nh + h + 1, :]
            lc = _gdn_local(q, k, v, b, gc, gc_row)
            dmat, kk, tinv, gam, u, w, qk, mm = (lc[key] for key in ("dmat", "kk", "tinv", "gam", "u", "w", "qk", "mm"))
            gam_c, kdec = lc["gam_c"], lc["kdec"]
            st = sall_ref[0, h]
            dsn = ds_ref[h]
            d_o = do_ref[:, h * hd:(h + 1) * hd]
            kd = kdec * k
            vn = u - _dot(w, st, precision=HI)
            dvn = _dot(mm, d_o, TN, HI) + _dot(kd, dsn, precision=HI)
            dm = jnp.where(lc["lower"], _dot(d_o, vn, NT, HI), 0.0)
            dqd = _dot(d_o, st, NT, HI)
            dkd = _dot(vn, dsn, NT, HI)
            dgam_c = jnp.sum(jnp.sum(dsn * st, axis=1, keepdims=True), axis=0, keepdims=True)
            ds_ref[h] = _dot(gam * q, d_o, TN, HI) + gam_c * dsn - _dot(w, dvn, TN, HI)
            dw = -_dot(dvn, st, NT, HI)
            drv = _dot(tinv, dvn, TN, HI)
            drk = _dot(tinv, dw, TN, HI)
            da = jnp.where(lc["strict"], -(_dot(drv, u, NT, HI) + _dot(drk, w, NT, HI)), 0.0)
            rs_rk = jnp.sum(drk * k, axis=-1, keepdims=True)
            db = jnp.sum(drv * v, axis=-1, keepdims=True) + gam * rs_rk + jnp.sum(da * kk * dmat, axis=-1, keepdims=True)
            e_mat = da * dmat * b
            dmd = dm * dmat
            dk = ((b * gam) * drk + _dot(e_mat, k, precision=HI) + _dot(e_mat, k, TN, HI)
                  + _dot(dmd, q, TN, HI) + kdec * dkd)
            dq = _dot(dmd, k, precision=HI) + gam * dqd
            f_mat = da * (b * kk) * dmat + dm * qk * dmat
            e_vec = jnp.sum(dkd * kd, axis=-1, keepdims=True)
            dgc = (b * gam * rs_rk + gam * jnp.sum(dqd * q, axis=-1, keepdims=True)
                   + jnp.sum(f_mat, axis=-1, keepdims=True) - _dot(f_mat, ones, TN, HI)[:, 0:1] - e_vec)
            last = jnp.sum(e_vec, axis=0, keepdims=True) + gam_c * dgam_c
            dgc = dgc + jnp.where(rowi == ck - 1, last, 0.0)
            dx_ref[:, h * hd:(h + 1) * hd] = dq
            dx_ref[:, gw + h * hd:gw + (h + 1) * hd] = dk
            dx_ref[:, 2 * gw + h * hd:2 * gw + (h + 1) * hd] = b * drv
            dgc_all = dgc_all + jnp.where(lane == h, dgc, 0.0)
            db_all = db_all + jnp.where(lane == nh + h, db, 0.0)
        ii = lax.broadcasted_iota(jnp.int32, (ck, ck), 0)
        jj = lax.broadcasted_iota(jnp.int32, (ck, ck), 1)
        daux_ref[...] = _dot(jnp.where(jj >= ii, 1.0, 0.0), dgc_all, precision=HI) + db_all

    rev = lambda i: (nc - 1 - i, 0)
    return pl.pallas_call(
        body, grid=(nc,),
        in_specs=[pl.BlockSpec((ck, 3 * gw), rev), pl.BlockSpec((ck, LANES), rev),
                  pl.BlockSpec((1, 16, ck), lambda i: (nc - 1 - i, 0, 0)),
                  pl.BlockSpec((1, nh, hd, hd), lambda i: (nc - 1 - i, 0, 0, 0)), pl.BlockSpec((ck, gw), rev)],
        out_specs=[pl.BlockSpec((ck, 3 * gw), rev), pl.BlockSpec((ck, LANES), rev)],
        out_shape=[jax.ShapeDtypeStruct((t, 3 * gw), F32), jax.ShapeDtypeStruct((t, LANES), F32)],
        scratch_shapes=[pltpu.VMEM((nh, hd, hd), F32)],
        compiler_params=_cp(("arbitrary",)), name=name,
    )(qkvn, aux, aux_t, s_all, do)


def _gdn_pre_bwd1(xc, dqkvn, daux, ab, alog_pad, dt_pad, *, name, tm=256):
    t, cw = xc.shape
    hd, nh = GDN_HEAD_DIM, GDN_HEADS
    gw = nh * hd
    tm = _blk(t, tm)

    def body(xc_ref, dy_ref, daux_ref, ab_ref, al_ref, dt_ref, dxc_ref, dab_ref, dal_ref, ddt_ref):
        i = pl.program_id(0)
        xc = xc_ref[...]
        sg = _sigmoid(xc)
        s = xc * sg
        dsilu = sg * (1.0 + xc * (1.0 - sg))
        for h in range(2 * nh):
            xh = s[:, h * hd:(h + 1) * hd]
            scale = hd ** -0.5 if h < nh else 1.0
            dyh = dy_ref[:, h * hd:(h + 1) * hd] * scale
            r = lax.rsqrt(jnp.sum(xh * xh, axis=-1, keepdims=True) + L2_EPS)
            dxh = r * dyh - xh * (r * r * r) * jnp.sum(dyh * xh, axis=-1, keepdims=True)
            dxc_ref[:, h * hd:(h + 1) * hd] = dxh * dsilu[:, h * hd:(h + 1) * hd]
        dxc_ref[:, 2 * gw:] = dy_ref[:, 2 * gw:] * dsilu[:, 2 * gw:]
        abv = ab_ref[...]
        dauxv = daux_ref[...]
        lane = lax.broadcasted_iota(jnp.int32, abv.shape, 1)
        is_a = lane < nh
        is_b = (lane >= nh) & (lane < 2 * nh)
        pre = abv + dt_ref[...]
        neg_ea = -jnp.exp(al_ref[...])
        d_a = jnp.where(is_a, dauxv * neg_ea * _sigmoid(pre), 0.0)
        beta = _sigmoid(abv)
        d_b = jnp.where(is_b, dauxv * beta * (1.0 - beta), 0.0)
        dab_ref[...] = (d_a + d_b).astype(dab_ref.dtype)
        dal = jnp.sum(jnp.where(is_a, dauxv * neg_ea * _softplus(pre), 0.0), axis=0, keepdims=True)
        ddt = jnp.sum(d_a, axis=0, keepdims=True)

        @pl.when(i == 0)
        def _():
            dal_ref[...] = dal
            ddt_ref[...] = ddt

        @pl.when(i > 0)
        def _():
            dal_ref[...] += dal
            ddt_ref[...] += ddt

    row = lambda c: pl.BlockSpec((tm, c), lambda i: (i, 0))
    vec = pl.BlockSpec((1, LANES), lambda i: (0, 0))
    return pl.pallas_call(
        body, grid=(t // tm,),
        in_specs=[row(cw), row(cw), row(LANES), row(LANES), vec, vec],
        out_specs=[row(cw), row(LANES), vec, vec],
        out_shape=[jax.ShapeDtypeStruct((t, cw), F32), jax.ShapeDtypeStruct((t, LANES), BF16),
                   jax.ShapeDtypeStruct((1, LANES), F32), jax.ShapeDtypeStruct((1, LANES), F32)],
        compiler_params=_cp(("arbitrary",)), name=name,
    )(xc, dqkvn, daux, ab, alog_pad, dt_pad)


def _gdn_pre_bwd2(dxc, qkvb, conv_w, *, name, tm=256):
    t, cw = dxc.shape
    tm = _blk(t, tm)
    hb = tm // HALO
    nblk = t // tm

    def body(d_ref, dn_ref, x_ref, xp_ref, w_ref, dx_ref, dw_ref):
        i = pl.program_id(0)
        dcur = d_ref[...]
        dnxt = jnp.where(i < nblk - 1, dn_ref[...], 0.0)
        cur = x_ref[...]
        prev = jnp.where(i > 0, xp_ref[...], 0.0)
        dx = None
        dws = []
        for tap in range(GDN_CONV):
            j = GDN_CONV - 1 - tap
            term = w_ref[tap:tap + 1, :] * _shift_up(dcur, dnxt, j)
            dx = term if dx is None else dx + term
            dws.append(jnp.sum(dcur * _shift_down(cur, prev, j), axis=0, keepdims=True))
        dx_ref[...] = dx.astype(dx_ref.dtype)
        dw = jnp.concatenate(dws, axis=0)

        @pl.when(i == 0)
        def _():
            dw_ref[...] = dw

        @pl.when(i > 0)
        def _():
            dw_ref[...] += dw

    row = pl.BlockSpec((tm, cw), lambda i: (i, 0))
    wsp = pl.BlockSpec((GDN_CONV, cw), lambda i: (0, 0))
    return pl.pallas_call(
        body, grid=(nblk,),
        in_specs=[row, pl.BlockSpec((HALO, cw), lambda i: (jnp.minimum((i + 1) * hb, t // HALO - 1), 0)),
                  row, pl.BlockSpec((HALO, cw), lambda i: (jnp.maximum(i * hb - 1, 0), 0)), wsp],
        out_specs=[row, wsp],
        out_shape=[jax.ShapeDtypeStruct((t, cw), BF16), jax.ShapeDtypeStruct((GDN_CONV, cw), F32)],
        compiler_params=_cp(("arbitrary",)), name=name,
    )(dxc, dxc, qkvb, qkvb, conv_w)


def _gdn_post_fwd(o, z, norm_w, *, name, tm=512):
    t, gw = o.shape
    hd, nh = GDN_HEAD_DIM, GDN_HEADS
    tm = _blk(t, tm)

    def body(o_ref, z_ref, w_ref, y_ref):
        zv = z_ref[...]
        sz = zv * _sigmoid(zv)
        for h in range(nh):
            oh = o_ref[:, h * hd:(h + 1) * hd]
            r = lax.rsqrt(jnp.mean(oh * oh, axis=-1, keepdims=True) + RMS_EPS)
            y_ref[:, h * hd:(h + 1) * hd] = (oh * r * w_ref[...] * sz[:, h * hd:(h + 1) * hd]).astype(y_ref.dtype)

    row = pl.BlockSpec((tm, gw), lambda i: (i, 0))
    return pl.pallas_call(
        body, grid=(t // tm,), in_specs=[row, row, pl.BlockSpec((1, hd), lambda i: (0, 0))], out_specs=row,
        out_shape=jax.ShapeDtypeStruct((t, gw), BF16), compiler_params=_cp(("parallel",)), name=name,
    )(o, z, norm_w)


def _gdn_post_bwd(dy, o, z, norm_w, *, name, tm=512):
    t, gw = o.shape
    hd, nh = GDN_HEAD_DIM, GDN_HEADS
    tm = _blk(t, tm)

    def body(dy_ref, o_ref, z_ref, w_ref, do_ref, dz_ref, dw_ref):
        i = pl.program_id(0)
        zv = z_ref[...]
        sg = _sigmoid(zv)
        sz = zv * sg
        dsz = sg * (1.0 + zv * (1.0 - sg))
        dw = None
        for h in range(nh):
            sl = slice(h * hd, (h + 1) * hd)
            oh = o_ref[:, sl]
            dyh = dy_ref[:, sl].astype(F32)
            r = lax.rsqrt(jnp.mean(oh * oh, axis=-1, keepdims=True) + RMS_EPS)
            xh = oh * r
            dz_ref[:, sl] = (dyh * xh * w_ref[...] * dsz[:, sl]).astype(dz_ref.dtype)
            dn = dyh * sz[:, sl]
            dxh = dn * w_ref[...]
            do_ref[:, sl] = r * (dxh - xh * jnp.mean(dxh * xh, axis=-1, keepdims=True))
            part = jnp.sum(dn * xh, axis=0, keepdims=True)
            dw = part if dw is None else dw + part

        @pl.when(i == 0)
        def _():
            dw_ref[...] = dw

        @pl.when(i > 0)
        def _():
            dw_ref[...] += dw

    row = pl.BlockSpec((tm, gw), lambda i: (i, 0))
    vec = pl.BlockSpec((1, hd), lambda i: (0, 0))
    return pl.pallas_call(
        body, grid=(t // tm,), in_specs=[row, row, row, vec], out_specs=[row, row, vec],
        out_shape=[jax.ShapeDtypeStruct((t, gw), F32), jax.ShapeDtypeStruct((t, gw), BF16),
                   jax.ShapeDtypeStruct((1, hd), F32)],
        compiler_params=_cp(("arbitrary",)), name=name,
    )(dy, o, z, norm_w)


def _in_segments(d):
    qw = SWA_Q_HEADS * SWA_HEAD_DIM
    kvw = 2 * SWA_KV_HEADS * SWA_HEAD_DIM
    gw = GDN_HEADS * GDN_HEAD_DIM
    xw = XA_HEADS * XA_HEAD_DIM
    widths = (qw, kvw, 3 * gw, 2 * GDN_HEADS, gw, xw, 3 * d)
    segs, start = [], 0
    for wd in widths:
        segs.append((start, start + wd))
        start += wd
    return segs


def _pad_cols(a, width):
    return jnp.pad(a, ((0, 0), (0, width - a.shape[1])))


def _relu2_epilogue(acc):
    r = jnp.maximum(acc, 0.0)
    return acc, r * r


def _add_epilogue(acc, res):
    return (acc + res,)


def _drelu2_epilogue(acc, u):
    return (acc * (2.0 * jnp.maximum(u.astype(F32), 0.0)),)


def _local_step(x, mem, tgt, wts, small):
    t, d = x.shape
    nh = GDN_HEADS
    segs = _in_segments(d)
    w_in = wts["w_in"]
    ab_seg = 3
    w_seg = [w_in[:, s:e] for s, e in segs]
    w_seg[ab_seg] = _pad_cols(w_seg[ab_seg], LANES)
    wq, wkv, wqkvb, wab, wz, wqc, wg = w_seg
    alog_pad = _pad_cols(small["a_log"], LANES)
    dt_pad = _pad_cols(small["dt_bias"], LANES)

    n = _rms_fwd(x, small["g_mix"], name="rms_mix")
    q_a = _mm(n, wq, out_dtypes=(BF16,), name="in_q_a")
    kv_a = _mm(n, wkv, out_dtypes=(BF16,), name="in_kv_a")
    qkvb = _mm(n, wqkvb, name="in_qkv_b")
    ab = _mm(n, wab, name="in_ab")
    z = _mm(n, wz, name="in_z")
    q_c = _mm(n, wqc, out_dtypes=(BF16,), name="in_q_c")
    gates = _mm(n, wg, name="in_gates")
    y_a, lse = _swa_fwd(q_a, kv_a, small["sinks"], name="swa_fwd")
    xc, qkvn, aux = _gdn_pre_fwd(qkvb, small["conv_w"], ab, alog_pad, dt_pad, name="gdn_pre_fwd")
    aux_t = aux[:, :16].reshape(t // GDN_CHUNK, GDN_CHUNK, 16).transpose(0, 2, 1)
    o_b, s_all = _gdn_core_fwd(qkvn, aux, aux_t, name="gdn_core_fwd")
    y_b = _gdn_post_fwd(o_b, z, small["gdn_norm_w"], name="gdn_post_fwd")
    nmem = _rms_fwd(mem, small["g_mem"], name="rms_mem")
    mkv = _mm(nmem, wts["w_mem_kv"], out_dtypes=(BF16,), name="mem_kv")
    y_c = _xa_fwd(q_c, mkv, name="xa_fwd")
    ys = (y_a, y_b, y_c)
    w_ups = (wts["w_swa_up"], wts["w_gdn_up"], wts["w_xa_up"])
    merged = _merge_fwd(ys, w_ups, gates, name="merge_fwd")
    h1 = _mm(merged, wts["w_out"], extras=(x,), epilogue=_add_epilogue, name="out_proj")
    n2 = _rms_fwd(h1, small["g_mlp"], name="rms_mlp")
    u, act = _mm(n2, wts["w_mlp_in"], out_dtypes=(BF16, BF16), epilogue=_relu2_epilogue, name="mlp_in")
    h2 = _mm(act, wts["w_mlp_out"], extras=(h1,), epilogue=_add_epilogue, name="mlp_out")
    dh2, dg_final, loss = _final_loss(h2, small["g_final"], tgt, name="final_loss")

    grads = {"g_final": dg_final}
    du = _mm(dh2, wts["w_mlp_out"], tb=True, out_dtypes=(BF16,), extras=(u,), epilogue=_drelu2_epilogue, name="d_mlp_act")
    grads["w_mlp_out"] = _mm(act, dh2, ta=True, out_dtypes=(BF16,), name="dw_mlp_out")
    grads["w_mlp_in"] = _mm(n2, du, ta=True, out_dtypes=(BF16,), name="dw_mlp_in")
    dn2 = _mm(du, wts["w_mlp_in"], tb=True, name="d_mlp_in")
    dh1, grads["g_mlp"] = _rms_bwd(dn2, h1, small["g_mlp"], dh2, name="rms_mlp_bwd")
    dmerged = _mm(dh1, wts["w_out"], tb=True, name="d_out_proj")
    grads["w_out"] = _mm(merged, dh1, ta=True, out_dtypes=(BF16,), name="dw_out")
    dus_and_dgates = _merge_bwd(ys, w_ups, gates, dmerged, name="merge_bwd")
    dus, dgates = dus_and_dgates[:3], dus_and_dgates[3:]
    dys = []
    for y, du_i, w_up, key in zip(ys, dus, w_ups, ("w_swa_up", "w_gdn_up", "w_xa_up")):
        dys.append(_mm(du_i, w_up, tb=True, out_dtypes=(BF16,), name="d_" + key))
        grads[key] = _mm(y, du_i, ta=True, out_dtypes=(BF16,), name="dw_" + key[2:])
    dq_a, dkv_a, grads["sinks"] = _swa_bwd(q_a, kv_a, small["sinks"], y_a, lse, dys[0], name="swa_bwd")
    do_b, dz, grads["gdn_norm_w"] = _gdn_post_bwd(dys[1], o_b, z, small["gdn_norm_w"], name="gdn_post_bwd")
    dqkvn, daux = _gdn_core_bwd(qkvn, aux, aux_t, s_all, do_b, name="gdn_core_bwd")
    dxc, dab, dalog, ddt = _gdn_pre_bwd1(xc, dqkvn, daux, ab, alog_pad, dt_pad, name="gdn_pre_bwd1")
    grads["a_log"], grads["dt_bias"] = dalog[:, :nh], ddt[:, :nh]
    dqkvb, grads["conv_w"] = _gdn_pre_bwd2(dxc, qkvb, small["conv_w"], name="gdn_pre_bwd2")
    dq_c, dmkv = _xa_bwd(q_c, mkv, dys[2], name="xa_bwd")
    grads["w_mem_kv"] = _mm(nmem, dmkv, ta=True, out_dtypes=(BF16,), name="dw_mem_kv")
    dnmem = _mm(dmkv, wts["w_mem_kv"], tb=True, name="d_mem_kv")
    _, grads["g_mem"] = _rms_bwd(dnmem, mem, small["g_mem"], jnp.zeros_like(mem), name="rms_mem_bwd")
    pieces = [dq_a, dkv_a, dqkvb, dab, dz, dq_c, *dgates]
    width = sum(p.shape[1] for p in pieces)
    cat_w = -(-width // 1024) * 1024
    dp = jnp.concatenate(pieces + [jnp.zeros((t, cat_w - width), BF16)], axis=1)
    w_cat = jnp.concatenate(w_seg + [jnp.zeros((d, cat_w - width), BF16)], axis=1)
    dn = _mm(dp, w_cat, tb=True, name="d_in_proj")
    dw_cat = _mm(n, dp, ta=True, out_dtypes=(BF16,), name="dw_in")
    ab_lo = segs[ab_seg][0]
    grads["w_in"] = jnp.concatenate(
        [dw_cat[:, :ab_lo + 2 * nh], dw_cat[:, ab_lo + LANES:ab_lo + LANES + w_in.shape[1] - segs[ab_seg][1]]], axis=1)
    dx, grads["g_mix"] = _rms_bwd(dn, x, small["g_mix"], dh1, name="rms_mix_bwd")
    return loss, dx, grads


HBM_SPEC = pl.BlockSpec(memory_space=pltpu.HBM)
VMEM_SPEC = pl.BlockSpec(memory_space=pltpu.VMEM)
N_CHIPS = 4
N_DEV = 8
PACK_ROW_ALIGN = 2048


def _place():
    return lax.axis_index("x"), lax.axis_index("y"), lax.axis_index("c")


def _other_chips(x, y):
    return [(1 - x, y), (x, 1 - y), (1 - x, 1 - y)]


def _all_gather_weights(wp, *, name):
    r = wp.shape[0]
    rh = r // 2

    def body(w_ref, out_ref, send_sems, recv_sems, local_sem):
        x, y, c = _place()
        sibling = (x, y, 1 - c)
        chips = _other_chips(x, y)

        def half(chip, hc):
            return out_ref.at[2 * chip[0] + chip[1], pl.ds(hc * rh, rh), :]

        def copy(k, src, dst, to):
            return pltpu.make_async_remote_copy(src_ref=src, dst_ref=dst, send_sem=send_sems.at[k],
                                                recv_sem=recv_sems.at[k], device_id=to, device_id_type=MESH)

        mine = pltpu.make_async_copy(w_ref, out_ref.at[2 * x + y], local_sem)
        mine.start()
        first = [copy(j, w_ref.at[pl.ds(c * rh, rh), :], half((x, y), c), (*chip, c)) for j, chip in enumerate(chips)]
        for cp in first:
            cp.start()
        passed = [copy(3 + j, half(chip, c), half(chip, c), sibling) for j, chip in enumerate(chips)]
        for j, chip in enumerate(chips):
            copy(j, half(chip, c), half(chip, c), (x, y, c)).wait_recv()
            passed[j].start()
        for j, chip in enumerate(chips):
            copy(3 + j, half(chip, 1 - c), half(chip, 1 - c), (x, y, c)).wait_recv()
        for cp in first + passed:
            cp.wait_send()
        mine.wait()

    return pl.pallas_call(
        body, out_shape=jax.ShapeDtypeStruct((N_CHIPS, r, LANES), wp.dtype),
        in_specs=[HBM_SPEC], out_specs=HBM_SPEC,
        scratch_shapes=[pltpu.SemaphoreType.DMA((6,)), pltpu.SemaphoreType.DMA((6,)), pltpu.SemaphoreType.DMA],
        name=name,
    )(wp)


def _sibling_halves(gp, *, name):
    _, r, _ = gp.shape
    rh = r // 2

    def body(g_ref, own_ref, sib_ref, send_sem, recv_sem, local_sem):
        x, y, c = _place()
        mine = pltpu.make_async_copy(g_ref.at[:, pl.ds(c * rh, rh), :], own_ref, local_sem)
        mine.start()
        swap = pltpu.make_async_remote_copy(src_ref=g_ref.at[:, pl.ds((1 - c) * rh, rh), :], dst_ref=sib_ref,
                                            send_sem=send_sem, recv_sem=recv_sem, device_id=(x, y, 1 - c),
                                            device_id_type=MESH)
        swap.start()
        swap.wait()
        mine.wait()

    shape = jax.ShapeDtypeStruct((N_CHIPS, rh, LANES), gp.dtype)
    return pl.pallas_call(
        body, out_shape=[shape, shape], in_specs=[HBM_SPEC], out_specs=[HBM_SPEC, HBM_SPEC],
        scratch_shapes=[pltpu.SemaphoreType.DMA, pltpu.SemaphoreType.DMA, pltpu.SemaphoreType.DMA],
        name=name,
    )(gp)


def _chip_exchange(s1, *, name):
    _, rh, _ = s1.shape

    def body(s_ref, own_ref, rcv_ref, send_sems, recv_sems, local_sem):
        x, y, c = _place()
        chips = _other_chips(x, y)
        mine = pltpu.make_async_copy(s_ref.at[2 * x + y], own_ref, local_sem)
        mine.start()
        sends = [pltpu.make_async_remote_copy(src_ref=s_ref.at[2 * chip[0] + chip[1]], dst_ref=rcv_ref.at[j],
                                              send_sem=send_sems.at[j], recv_sem=recv_sems.at[j],
                                              device_id=(*chip, c), device_id_type=MESH)
                 for j, chip in enumerate(chips)]
        for cp in sends:
            cp.start()
        for cp in sends:
            cp.wait()
        mine.wait()

    return pl.pallas_call(
        body,
        out_shape=[jax.ShapeDtypeStruct((rh, LANES), s1.dtype), jax.ShapeDtypeStruct((3, rh, LANES), s1.dtype)],
        in_specs=[HBM_SPEC], out_specs=[HBM_SPEC, HBM_SPEC],
        scratch_shapes=[pltpu.SemaphoreType.DMA((3,)), pltpu.SemaphoreType.DMA((3,)), pltpu.SemaphoreType.DMA],
        name=name,
    )(s1)


def _join_halves(g_half, *, name):
    rh = g_half.shape[0]

    def body(g_ref, out_ref, send_sem, recv_sem, local_sem):
        x, y, c = _place()
        rows = out_ref.at[pl.ds(c * rh, rh), :]
        mine = pltpu.make_async_copy(g_ref, rows, local_sem)
        mine.start()
        push = pltpu.make_async_remote_copy(src_ref=g_ref, dst_ref=rows, send_sem=send_sem, recv_sem=recv_sem,
                                            device_id=(x, y, 1 - c), device_id_type=MESH)
        push.start()
        push.wait()
        mine.wait()

    return pl.pallas_call(
        body, out_shape=jax.ShapeDtypeStruct((2 * rh, LANES), g_half.dtype), in_specs=[HBM_SPEC], out_specs=HBM_SPEC,
        scratch_shapes=[pltpu.SemaphoreType.DMA, pltpu.SemaphoreType.DMA, pltpu.SemaphoreType.DMA],
        name=name,
    )(g_half)


def _all_gather_small(blk, *, name):
    r = blk.shape[0]

    def body(b_ref, out_ref, send_sems, recv_sems):
        x, y, c = _place()
        me = 4 * x + 2 * y + c
        out_ref[me] = b_ref[...]
        sends = []
        for k in range(1, N_DEV):
            peer = (x ^ (k >> 2), y ^ ((k >> 1) & 1), c ^ (k & 1))
            sends.append(pltpu.make_async_remote_copy(src_ref=b_ref, dst_ref=out_ref.at[me], send_sem=send_sems.at[k - 1],
                                                      recv_sem=recv_sems.at[k - 1], device_id=peer, device_id_type=MESH))
        for cp in sends:
            cp.start()
        for k in range(1, N_DEV):
            rows = out_ref.at[me ^ k]
            pltpu.make_async_remote_copy(src_ref=rows, dst_ref=rows, send_sem=send_sems.at[k - 1],
                                         recv_sem=recv_sems.at[k - 1], device_id=(x, y, c), device_id_type=MESH).wait_recv()
        for cp in sends:
            cp.wait_send()

    return pl.pallas_call(
        body, out_shape=jax.ShapeDtypeStruct((N_DEV, r, LANES), blk.dtype), in_specs=[VMEM_SPEC], out_specs=VMEM_SPEC,
        scratch_shapes=[pltpu.SemaphoreType.DMA((N_DEV - 1,)), pltpu.SemaphoreType.DMA((N_DEV - 1,))],
        name=name,
    )(blk)


def _sum_rows(parts, out_dtype, *, name, tb=1024):
    rows = parts[0].shape[0]
    tb = _blk(rows, tb)

    def body(*refs):
        acc = refs[0][...].astype(F32)
        for r in refs[1:-1]:
            acc = acc + r[...].astype(F32)
        refs[-1][...] = acc.astype(refs[-1].dtype)

    spec = pl.BlockSpec((tb, LANES), lambda i: (i, 0))
    return pl.pallas_call(
        body, grid=(rows // tb,), in_specs=[spec] * len(parts), out_specs=spec,
        out_shape=jax.ShapeDtypeStruct((rows, LANES), out_dtype), compiler_params=_cp(("parallel",)), name=name,
    )(*parts)


def _reduce_scatter(gp):
    _, r, _ = gp.shape
    rh = r // 2
    own, sib = _sibling_halves(gp, name="rs_sibling_halves")
    s1 = _sum_rows([own.reshape(N_CHIPS * rh, LANES), sib.reshape(N_CHIPS * rh, LANES)], BF16, name="rs_pair_sum")
    own2, rcv = _chip_exchange(s1.reshape(N_CHIPS, rh, LANES), name="rs_chip_exchange")
    g_half = _sum_rows([own2, rcv[0], rcv[1], rcv[2]], F32, name="rs_chip_sum")
    return _join_halves(g_half, name="rs_join_halves")


BIG = (
    ("w_in", 1), ("w_mem_kv", 0), ("w_swa_up", 1), ("w_gdn_up", 1), ("w_xa_up", 1), ("w_out", 0), ("w_mlp_in", 1),
    ("w_mlp_out", 0))
SMALL = ("g_mix", "sinks", "a_log", "dt_bias", "gdn_norm_w", "g_mem", "g_mlp", "g_final")


def _packed_rows(shard_shapes):
    rows = [sh[0] * sh[1] // LANES for sh in shard_shapes]
    total = -(-sum(rows) // PACK_ROW_ALIGN) * PACK_ROW_ALIGN
    return rows, total


def _pack_shards(shards, dtype):
    rows, total = _packed_rows([s.shape for s in shards])
    flat = [s.astype(dtype).reshape(-1, LANES) for s in shards]
    return jnp.concatenate(flat + [jnp.zeros((total - sum(rows), LANES), dtype)], axis=0)


def _unpack_shards(packed, shard_shapes):
    rows, _ = _packed_rows(shard_shapes)
    out, start = [], 0
    for sh, n in zip(shard_shapes, rows):
        out.append(packed[start:start + n].reshape(sh))
        start += n
    return out


def _rows128(a, rows):
    flat = a.reshape(-1)
    return jnp.pad(flat, (0, rows * LANES - flat.shape[0])).reshape(rows, LANES)


def kernel(x, mem, g_mix, w_in, sinks, conv_w, a_log, dt_bias, gdn_norm_w, g_mem, w_mem_kv, w_swa_up, w_gdn_up, w_xa_up, w_out, g_mlp, w_mlp_in, w_mlp_out, g_final, loss_target, m_g_mix, m_w_in, m_sinks, m_conv_w, m_a_log, m_dt_bias, m_gdn_norm_w, m_g_mem, m_w_mem_kv, m_w_swa_up, m_w_gdn_up, m_w_xa_up, m_w_out, m_g_mlp, m_w_mlp_in, m_w_mlp_out, m_g_final, v_g_mix, v_w_in, v_sinks, v_conv_w, v_a_log, v_dt_bias, v_gdn_norm_w, v_g_mem, v_w_mem_kv, v_w_swa_up, v_w_gdn_up, v_w_xa_up, v_w_out, v_g_mlp, v_w_mlp_in, v_w_mlp_out, v_g_final):
    given = dict(locals())
    xi, yi, _ = _place()
    chip = 2 * xi + yi
    d = x.shape[-1]

    shards = [given[k][0] for k, _ in BIG]
    shard_shapes = [s.shape for s in shards]
    full = _all_gather_weights(_pack_shards(shards, BF16), name="ag_weights")
    per_chip = [_unpack_shards(full[s], shard_shapes) for s in range(N_CHIPS)]
    wts = {k: jnp.concatenate([per_chip[s][i] for s in range(N_CHIPS)], axis=ax) for i, (k, ax) in enumerate(BIG)}
    conv_shard = conv_w[0]
    conv_rows = -(-conv_shard.size // (8 * LANES)) * 8
    conv_all = _all_gather_small(_rows128(conv_shard, conv_rows), name="ag_conv")
    conv_full = jnp.concatenate(
        [conv_all[2 * s].reshape(-1)[:conv_shard.size].reshape(conv_shard.shape) for s in range(N_CHIPS)], axis=1)

    small = {k: given[k].reshape(1, -1) for k in SMALL}
    small["conv_w"] = conv_full
    loss_row, dx, grads = _local_step(x[0], mem[0], loss_target[0], wts, small)

    gparts = []
    for s in range(N_CHIPS):
        pieces = []
        for (k, ax), sh in zip(BIG, shard_shapes):
            n = sh[ax]
            pieces.append(lax.slice_in_dim(grads[k], s * n, (s + 1) * n, axis=ax))
        gparts.append(_pack_shards(pieces, BF16))
    g_shard = _unpack_shards(_reduce_scatter(jnp.stack(gparts)), shard_shapes)
    big_grads = {k: g for (k, _), g in zip(BIG, g_shard)}

    layout = [("loss", loss_row[:, :1])] + [(k, grads[k]) for k in SMALL] + [("conv_w", grads["conv_w"])]
    rows = [-(-a.size // LANES) for _, a in layout]
    blk_rows = -(-sum(rows) // 8) * 8
    blk = jnp.concatenate([_rows128(a.astype(F32), n) for (_, a), n in zip(layout, rows)]
                          + [jnp.zeros((blk_rows - sum(rows), LANES), F32)], axis=0)
    gathered = _all_gather_small(blk, name="ag_small_grads")
    reduced = _sum_rows([gathered[i] for i in range(N_DEV)], F32, name="small_grad_sum")
    small_grads, start = {}, 0
    for (k, a), n in zip(layout, rows):
        small_grads[k] = reduced[start:start + n].reshape(-1)[:a.size].reshape(a.shape)
        start += n
    loss = small_grads["loss"].reshape(())
    cw = conv_shard.shape[1]
    conv_grad = lax.dynamic_slice_in_dim(small_grads["conv_w"], chip * cw, cw, axis=1)

    names = ["g_mix", "w_in", "sinks", "conv_w", "a_log", "dt_bias", "gdn_norm_w", "g_mem", "w_mem_kv", "w_swa_up",
             "w_gdn_up", "w_xa_up", "w_out", "g_mlp", "w_mlp_in", "w_mlp_out", "g_final"]
    out_g, out_d, out_m, out_v = [], [], [], []
    for k in names:
        w, m, v = given[k], given["m_" + k], given["v_" + k]
        if k in big_grads:
            g2 = big_grads[k]
        elif k == "conv_w":
            g2 = conv_grad
        else:
            g2 = small_grads[k]
        shape2 = g2.shape
        delta, new_m, new_v = _adamw(w.reshape(shape2), g2, m.reshape(shape2), v.reshape(shape2), name="adamw_" + k)
        out_g.append(g2.reshape(w.shape))
        out_d.append(delta.reshape(w.shape))
        out_m.append(new_m.reshape(w.shape))
        out_v.append(new_v.reshape(w.shape))
    return (loss, dx[None], *out_g, *out_d, *out_m, *out_v)
```

```python
import functools
import math

import jax
import jax.numpy as jnp
from jax import lax
from jax.experimental import pallas as pl
from jax.experimental.pallas import tpu as pltpu

F32 = jnp.float32
BF16 = jnp.bfloat16
HI = lax.Precision.HIGHEST
MESH = pl.DeviceIdType.MESH

SWA_Q_HEADS = 16
SWA_KV_HEADS = 2
SWA_HEAD_DIM = 64
SWA_WINDOW = 128
GDN_HEADS = 4
GDN_HEAD_DIM = 128
GDN_CONV = 4
GDN_CHUNK = 64
XA_HEADS = 4
XA_HEAD_DIM = 128
RMS_EPS = 1e-6
L2_EPS = 1e-6
ADAM_LR = 0.001
ADAM_B1 = 0.9
ADAM_B2 = 0.999
ADAM_EPS = 1e-08
ADAM_WD = 0.01
ADAM_STEP = 10

LANES = 128
N_SHARDS = 4
VMEM_LIMIT = 56 * 1024 * 1024

NT = (((1,), (1,)), ((), ()))
TN = (((0,), (0,)), ((), ()))
NN = (((1,), (0,)), ((), ()))


def _cp(sem=None):
    return pltpu.CompilerParams(dimension_semantics=sem, vmem_limit_bytes=VMEM_LIMIT)


def _blk(dim, pref):
    if dim <= pref:
        return dim
    b = (pref // LANES) * LANES
    while dim % b:
        b -= LANES
    assert b > 0, (dim, pref)
    return b


def _dot(a, b, dims=NN, precision=None):
    return lax.dot_general(a, b, dims, precision=precision, preferred_element_type=F32)


def _sigmoid(x):
    return 1.0 / (1.0 + jnp.exp(-x))


def _mm(a, b, *, name, ta=False, tb=False, out_dtypes=(F32,), epilogue=None, extras=(), tm=1024, tn=1024, tk=512,
        b_sharded=False, out_sharded=False, b_window=None):
    (kdim, m) = a.shape if ta else a.shape[::-1]
    col0 = 0
    n_lim = k_lim = None
    if b_sharded:
        ns, rows_w, per = b.shape
        if tb:
            kb, n, k_lim = ns * per, rows_w, per
        else:
            kb, n, n_lim = rows_w, ns * per, per
    else:
        (kb, n) = b.shape[::-1] if tb else b.shape
        if b_window is not None:
            assert not tb
            col0, n = b_window
    assert kdim == kb, (a.shape, b.shape, ta, tb)
    if out_sharded:
        assert n % N_SHARDS == 0
        n_lim = n // N_SHARDS if n_lim is None else n_lim
        assert n_lim == n // N_SHARDS
    tm, tn, tk = _blk(m, tm), _blk(n_lim or n, tn), _blk(k_lim or kdim, tk)
    assert col0 % tn == 0, (col0, tn)
    nk = kdim // tk
    a_spec = pl.BlockSpec((tk, tm), lambda i, j, k: (k, i)) if ta else pl.BlockSpec((tm, tk), lambda i, j, k: (i, k))
    if b_sharded and tb:
        kpb = k_lim // tk
        b_spec = pl.BlockSpec((None, tn, tk), lambda i, j, k: (k // kpb, j, k % kpb))
    elif b_sharded:
        bpb = n_lim // tn
        b_spec = pl.BlockSpec((None, tk, tn), lambda i, j, k: (j // bpb, k, j % bpb))
    elif tb:
        b_spec = pl.BlockSpec((tn, tk), lambda i, j, k: (j, k))
    else:
        b_spec = pl.BlockSpec((tk, tn), lambda i, j, k: (k, j + col0 // tn))
    x_spec = pl.BlockSpec((tm, tn), lambda i, j, k: (i, j))
    if out_sharded:
        opb = n_lim // tn
        o_spec = pl.BlockSpec((None, tm, tn), lambda i, j, k: (j // opb, i, j % opb))
        out_shape = (N_SHARDS, m, n_lim)
    else:
        o_spec, out_shape = x_spec, (m, n)
    dims = ((((0 if ta else 1),), ((1 if tb else 0),)), ((), ()))
    n_extra, n_out = len(extras), len(out_dtypes)

    def body(*refs):
        a_ref, b_ref = refs[:2]
        extra_refs = refs[2:2 + n_extra]
        out_refs = refs[2 + n_extra:2 + n_extra + n_out]
        acc_ref = refs[-1]
        k = pl.program_id(2)
        part = _dot(a_ref[...].astype(BF16), b_ref[...].astype(BF16), dims)

        @pl.when(k == 0)
        def _():
            acc_ref[...] = part

        @pl.when(k > 0)
        def _():
            acc_ref[...] += part

        @pl.when(k == nk - 1)
        def _():
            acc = acc_ref[...]
            vals = epilogue(acc, *[r[...] for r in extra_refs]) if epilogue is not None else (acc,) * n_out
            assert len(vals) == n_out
            for r, v in zip(out_refs, vals):
                r[...] = v.astype(r.dtype)

    outs = pl.pallas_call(
        body,
        grid=(m // tm, n // tn, nk),
        in_specs=[a_spec, b_spec] + [x_spec] * n_extra,
        out_specs=[o_spec] * n_out,
        out_shape=[jax.ShapeDtypeStruct(out_shape, d) for d in out_dtypes],
        scratch_shapes=[pltpu.VMEM((tm, tn), F32)],
        compiler_params=_cp(("parallel", "parallel", "arbitrary")),
        name=name,
    )(a, b, *extras)
    return outs[0] if n_out == 1 else outs


def _rms_fwd(x, g, *, name, tm=512):
    t, d = x.shape
    tm = _blk(t, tm)

    def body(x_ref, g_ref, n_ref):
        xv = x_ref[...]
        r = lax.rsqrt(jnp.mean(xv * xv, axis=-1, keepdims=True) + RMS_EPS)
        n_ref[...] = (xv * r * g_ref[...]).astype(n_ref.dtype)

    return pl.pallas_call(
        body, grid=(t // tm,),
        in_specs=[pl.BlockSpec((tm, d), lambda i: (i, 0)), pl.BlockSpec((1, d), lambda i: (0, 0))],
        out_specs=pl.BlockSpec((tm, d), lambda i: (i, 0)),
        out_shape=jax.ShapeDtypeStruct((t, d), BF16),
        compiler_params=_cp(("parallel",)), name=name,
    )(x, g)


def _rms_bwd(dn, x, g, dres, *, name, tm=512):
    t, d = x.shape
    tm = _blk(t, tm)

    def body(dn_ref, x_ref, g_ref, dres_ref, dx_ref, dg_ref):
        i = pl.program_id(0)
        xv = x_ref[...]
        r = lax.rsqrt(jnp.mean(xv * xv, axis=-1, keepdims=True) + RMS_EPS)
        xh = xv * r
        dnv = dn_ref[...].astype(F32)
        dxh = dnv * g_ref[...]
        dx_ref[...] = dres_ref[...] + r * (dxh - xh * jnp.mean(dxh * xh, axis=-1, keepdims=True))
        part = jnp.sum(dnv * xh, axis=0, keepdims=True)

        @pl.when(i == 0)
        def _():
            dg_ref[...] = part

        @pl.when(i > 0)
        def _():
            dg_ref[...] += part

    row = pl.BlockSpec((tm, d), lambda i: (i, 0))
    vec = pl.BlockSpec((1, d), lambda i: (0, 0))
    return pl.pallas_call(
        body, grid=(t // tm,),
        in_specs=[row, row, vec, row], out_specs=[row, vec],
        out_shape=[jax.ShapeDtypeStruct((t, d), F32), jax.ShapeDtypeStruct((1, d), F32)],
        compiler_params=_cp(("arbitrary",)), name=name,
    )(dn, x, g, dres)


def _final_loss(h, g, tgt, *, name, tm=512):
    t, d = h.shape
    tm = _blk(t, tm)

    def body(h_ref, g_ref, t_ref, dh_ref, dg_ref, loss_ref):
        i = pl.program_id(0)
        hv = h_ref[...]
        r = lax.rsqrt(jnp.mean(hv * hv, axis=-1, keepdims=True) + RMS_EPS)
        xh = hv * r
        e = xh * g_ref[...] - t_ref[...]
        dy = e * (1.0 / d)
        dxh = dy * g_ref[...]
        dh_ref[...] = r * (dxh - xh * jnp.mean(dxh * xh, axis=-1, keepdims=True))
        dg_part = jnp.sum(dy * xh, axis=0, keepdims=True)
        row_loss = jnp.sum(e * e, axis=-1, keepdims=True) * (0.5 / d)
        loss_part = jnp.sum(row_loss, axis=0, keepdims=True)

        @pl.when(i == 0)
        def _():
            dg_ref[...] = dg_part
            loss_ref[...] = jnp.broadcast_to(loss_part, loss_ref.shape)

        @pl.when(i > 0)
        def _():
            dg_ref[...] += dg_part
            loss_ref[...] += jnp.broadcast_to(loss_part, loss_ref.shape)

    row = pl.BlockSpec((tm, d), lambda i: (i, 0))
    vec = pl.BlockSpec((1, d), lambda i: (0, 0))
    return pl.pallas_call(
        body, grid=(t // tm,),
        in_specs=[row, vec, row], out_specs=[row, vec, pl.BlockSpec((1, LANES), lambda i: (0, 0))],
        out_shape=[jax.ShapeDtypeStruct((t, d), F32), jax.ShapeDtypeStruct((1, d), F32),
                   jax.ShapeDtypeStruct((1, LANES), F32)],
        compiler_params=_cp(("arbitrary",)), name=name,
    )(h, g, tgt)


def _swa_mask(n):
    w = SWA_WINDOW
    qi = lax.broadcasted_iota(jnp.int32, (w, 2 * w), 0)
    kj = lax.broadcasted_iota(jnp.int32, (w, 2 * w), 1)
    return (kj > qi) & (kj <= qi + w) & ((n > 0) | (kj >= w))


def _swa_fwd(q, kv, sinks, *, name):
    t = q.shape[0]
    w, hd, hq, hkv = SWA_WINDOW, SWA_HEAD_DIM, SWA_Q_HEADS, SWA_KV_HEADS
    grp = hq // hkv
    kvw = hkv * hd
    nb = t // w

    def body(q_ref, kvp_ref, kvc_ref, s_ref, o_ref, lse_ref):
        n = pl.program_id(0)
        mask = _swa_mask(n)
        kvcat = jnp.concatenate([kvp_ref[...], kvc_ref[...]], axis=0)
        outs, lses = [], []
        for h in range(hq):
            hk = h // grp
            qh = q_ref[:, h * hd:(h + 1) * hd]
            kh = kvcat[:, hk * hd:(hk + 1) * hd]
            vh = kvcat[:, kvw + hk * hd:kvw + (hk + 1) * hd]
            s = _dot(qh, kh, NT) * (hd ** -0.5)
            s = jnp.where(mask, s, -jnp.inf)
            sk = s_ref[0:1, h:h + 1]
            m = jnp.maximum(jnp.max(s, axis=-1, keepdims=True), sk)
            p = jnp.exp(s - m)
            den = jnp.sum(p, axis=-1, keepdims=True) + jnp.exp(sk - m)
            pr = (p / den).astype(BF16)
            outs.append(_dot(pr, vh))
            lses.append(m + jnp.log(den))
        o_ref[...] = jnp.concatenate(outs, axis=1).astype(o_ref.dtype)
        lse_ref[...] = jnp.concatenate(lses, axis=1)

    return pl.pallas_call(
        body, grid=(nb,),
        in_specs=[pl.BlockSpec((w, hq * hd), lambda i: (i, 0)),
                  pl.BlockSpec((w, 2 * kvw), lambda i: (jnp.maximum(i - 1, 0), 0)),
                  pl.BlockSpec((w, 2 * kvw), lambda i: (i, 0)),
                  pl.BlockSpec((1, hq), lambda i: (0, 0))],
        out_specs=[pl.BlockSpec((w, hq * hd), lambda i: (i, 0)), pl.BlockSpec((w, hq), lambda i: (i, 0))],
        out_shape=[jax.ShapeDtypeStruct((t, hq * hd), BF16), jax.ShapeDtypeStruct((t, hq), F32)],
        compiler_params=_cp(("parallel",)), name=name,
    )(q, kv, kv, sinks)


def _swa_bwd(q, kv, sinks, o, lse, do, *, name):
    t = q.shape[0]
    w, hd, hq, hkv = SWA_WINDOW, SWA_HEAD_DIM, SWA_Q_HEADS, SWA_KV_HEADS
    grp = hq // hkv
    kvw = hkv * hd
    nb = t // w

    def body(q_ref, kvp_ref, kvc_ref, s_ref, o_ref, lse_ref, do_ref, dq_ref, dkv_ref, ds_ref, carry_ref):
        n = pl.program_id(0)

        @pl.when(n == 0)
        def _():
            ds_ref[...] = jnp.zeros_like(ds_ref)
            carry_ref[...] = jnp.zeros_like(carry_ref)

        @pl.when(n < nb)
        def _():
            mask = _swa_mask(n)
            kvcat = jnp.concatenate([kvp_ref[...], kvc_ref[...]], axis=0)
            dqs, dsk = [], []
            dk_acc = [None] * hkv
            dv_acc = [None] * hkv
            for h in range(hq):
                hk = h // grp
                qh = q_ref[:, h * hd:(h + 1) * hd]
                kh = kvcat[:, hk * hd:(hk + 1) * hd]
                vh = kvcat[:, kvw + hk * hd:kvw + (hk + 1) * hd]
                doh = do_ref[:, h * hd:(h + 1) * hd]
                oh = o_ref[:, h * hd:(h + 1) * hd]
                lse_h = lse_ref[:, h:h + 1]
                s = _dot(qh, kh, NT) * (hd ** -0.5)
                p = jnp.exp(jnp.where(mask, s, -jnp.inf) - lse_h)
                delta = jnp.sum(doh.astype(F32) * oh.astype(F32), axis=-1, keepdims=True)
                dp = _dot(doh, vh, NT)
                ds = (p * (dp - delta) * (hd ** -0.5)).astype(BF16)
                dqs.append(_dot(ds, kh))
                dk_h = _dot(ds, qh, TN)
                dv_h = _dot(p.astype(BF16), doh, TN)
                dk_acc[hk] = dk_h if dk_acc[hk] is None else dk_acc[hk] + dk_h
                dv_acc[hk] = dv_h if dv_acc[hk] is None else dv_acc[hk] + dv_h
                sk = s_ref[0:1, h:h + 1]
                dsk.append(jnp.sum(-jnp.exp(sk - lse_h) * delta, axis=0, keepdims=True))
            dq_ref[...] = jnp.concatenate(dqs, axis=1).astype(dq_ref.dtype)
            ds_ref[...] += jnp.concatenate(dsk, axis=1)
            dkv_cat = jnp.concatenate(dk_acc + dv_acc, axis=1)
            dkv_ref[...] = (carry_ref[...] + dkv_cat[:w]).astype(dkv_ref.dtype)
            carry_ref[...] = dkv_cat[w:]

        @pl.when(n == nb)
        def _():
            dkv_ref[...] = carry_ref[...].astype(dkv_ref.dtype)

    cur = lambda i: (jnp.minimum(i, nb - 1), 0)
    prev = lambda i: (jnp.clip(i - 1, 0, nb - 1), 0)
    return pl.pallas_call(
        body, grid=(nb + 1,),
        in_specs=[pl.BlockSpec((w, hq * hd), cur), pl.BlockSpec((w, 2 * kvw), prev), pl.BlockSpec((w, 2 * kvw), cur),
                  pl.BlockSpec((1, hq), lambda i: (0, 0)), pl.BlockSpec((w, hq * hd), cur),
                  pl.BlockSpec((w, hq), cur), pl.BlockSpec((w, hq * hd), cur)],
        out_specs=[pl.BlockSpec((w, hq * hd), cur), pl.BlockSpec((w, 2 * kvw), prev),
                   pl.BlockSpec((1, hq), lambda i: (0, 0))],
        out_shape=[jax.ShapeDtypeStruct((t, hq * hd), BF16), jax.ShapeDtypeStruct((t, 2 * kvw), BF16),
                   jax.ShapeDtypeStruct((1, hq), F32)],
        scratch_shapes=[pltpu.VMEM((w, 2 * kvw), F32)],
        compiler_params=_cp(("arbitrary",)), name=name,
    )(q, kv, kv, sinks, o, lse, do)


def _xa_fwd(q, mkv, *, name, tq=512):
    t, xw = q.shape
    nm = mkv.shape[0]
    hd, nh = XA_HEAD_DIM, XA_HEADS
    tq = _blk(t, tq)

    def body(q_ref, mkv_ref, o_ref):
        outs = []
        for h in range(nh):
            qh = q_ref[:, h * hd:(h + 1) * hd]
            kh = mkv_ref[:, h * hd:(h + 1) * hd]
            vh = mkv_ref[:, xw + h * hd:xw + (h + 1) * hd]
            s = _dot(qh, kh, NT) * (hd ** -0.5)
            p = jnp.exp(s - jnp.max(s, axis=-1, keepdims=True))
            p = p / jnp.sum(p, axis=-1, keepdims=True)
            outs.append(_dot(p.astype(BF16), vh))
        o_ref[...] = jnp.concatenate(outs, axis=1).astype(o_ref.dtype)

    return pl.pallas_call(
        body, grid=(t // tq,),
        in_specs=[pl.BlockSpec((tq, xw), lambda i: (i, 0)), pl.BlockSpec((nm, 2 * xw), lambda i: (0, 0))],
        out_specs=pl.BlockSpec((tq, xw), lambda i: (i, 0)),
        out_shape=jax.ShapeDtypeStruct((t, xw), BF16),
        compiler_params=_cp(("parallel",)), name=name,
    )(q, mkv)


def _xa_bwd(q, mkv, do, *, name, tq=512):
    t, xw = q.shape
    nm = mkv.shape[0]
    hd, nh = XA_HEAD_DIM, XA_HEADS
    tq = _blk(t, tq)

    def body(q_ref, mkv_ref, do_ref, dq_ref, dmkv_ref):
        i = pl.program_id(0)
        dqs, dks, dvs = [], [], []
        for h in range(nh):
            qh = q_ref[:, h * hd:(h + 1) * hd]
            kh = mkv_ref[:, h * hd:(h + 1) * hd]
            vh = mkv_ref[:, xw + h * hd:xw + (h + 1) * hd]
            doh = do_ref[:, h * hd:(h + 1) * hd]
            s = _dot(qh, kh, NT) * (hd ** -0.5)
            p = jnp.exp(s - jnp.max(s, axis=-1, keepdims=True))
            p = p / jnp.sum(p, axis=-1, keepdims=True)
            dp = _dot(doh, vh, NT)
            ds = (p * (dp - jnp.sum(p * dp, axis=-1, keepdims=True)) * (hd ** -0.5)).astype(BF16)
            dqs.append(_dot(ds, kh))
            dks.append(_dot(ds, qh, TN))
            dvs.append(_dot(p.astype(BF16), doh, TN))
        dq_ref[...] = jnp.concatenate(dqs, axis=1).astype(dq_ref.dtype)
        part = jnp.concatenate(dks + dvs, axis=1)

        @pl.when(i == 0)
        def _():
            dmkv_ref[...] = part

        @pl.when(i > 0)
        def _():
            dmkv_ref[...] += part

    row = pl.BlockSpec((tq, xw), lambda i: (i, 0))
    full = pl.BlockSpec((nm, 2 * xw), lambda i: (0, 0))
    return pl.pallas_call(
        body, grid=(t // tq,),
        in_specs=[row, full, row], out_specs=[row, full],
        out_shape=[jax.ShapeDtypeStruct((t, xw), BF16), jax.ShapeDtypeStruct((nm, 2 * xw), F32)],
        compiler_params=_cp(("arbitrary",)), name=name,
    )(q, mkv, do)


def _merge_specs(t, d, ys, ws, tm, tn):
    nj = d // tn
    wpb = ws[0].shape[2] // tn
    y_specs = [pl.BlockSpec((tm, y.shape[1]), lambda i, j: (i, 0)) for y in ys]
    w_specs = [pl.BlockSpec((None, w.shape[1], tn), lambda i, j: (j // wpb, 0, j % wpb)) for w in ws]
    g_specs = [pl.BlockSpec((tm, tn), functools.partial(lambda i, j, b: (i, j + b * nj), b=b)) for b in range(3)]
    return y_specs, w_specs, g_specs


def _merge_fwd(ys, ws, gates, *, name, tm=512, tn=512):
    t, d = ys[0].shape[0], ws[0].shape[0] * ws[0].shape[2]
    tm, tn = _blk(t, tm), _blk(ws[0].shape[2], tn)
    y_specs, w_specs, g_specs = _merge_specs(t, d, ys, ws, tm, tn)

    def body(ya, yb, yc, wa, wb, wc, ga, gb, gc, o_ref):
        acc = None
        for y, w, g in ((ya, wa, ga), (yb, wb, gb), (yc, wc, gc)):
            term = _sigmoid(g[...]) * _dot(y[...], w[...])
            acc = term if acc is None else acc + term
        o_ref[...] = acc.astype(o_ref.dtype)

    return pl.pallas_call(
        body, grid=(t // tm, d // tn),
        in_specs=y_specs + w_specs + g_specs,
        out_specs=pl.BlockSpec((tm, tn), lambda i, j: (i, j)),
        out_shape=jax.ShapeDtypeStruct((t, d), BF16),
        compiler_params=_cp(("parallel", "parallel")), name=name,
    )(*ys, *ws, gates, gates, gates)


def _merge_bwd(ys, ws, gates, dmerged, *, name, tm=512, tn=512):
    t, d = ys[0].shape[0], ws[0].shape[0] * ws[0].shape[2]
    tm, tn = _blk(t, tm), _blk(ws[0].shape[2], tn)
    y_specs, w_specs, g_specs = _merge_specs(t, d, ys, ws, tm, tn)
    tile = pl.BlockSpec((tm, tn), lambda i, j: (i, j))

    def body(ya, yb, yc, wa, wb, wc, ga, gb, gc, dm_ref, dua, dub, duc, dga, dgb, dgc):
        dm = dm_ref[...]
        for y, w, g, du, dg in ((ya, wa, ga, dua, dga), (yb, wb, gb, dub, dgb), (yc, wc, gc, duc, dgc)):
            sg = _sigmoid(g[...])
            u = _dot(y[...], w[...])
            du[...] = (dm * sg).astype(du.dtype)
            dg[...] = (dm * u * sg * (1.0 - sg)).astype(dg.dtype)

    return pl.pallas_call(
        body, grid=(t // tm, d // tn),
        in_specs=y_specs + w_specs + g_specs + [tile],
        out_specs=[tile] * 6,
        out_shape=[jax.ShapeDtypeStruct((t, d), BF16)] * 6,
        compiler_params=_cp(("parallel", "parallel")), name=name,
    )(*ys, *ws, gates, gates, gates, dmerged)


def _adamw(w, g, m, v, *, name, tm=256):
    r, c = w.shape
    tm = _blk(r, tm) if r % 8 == 0 else r
    bc1 = 1.0 - ADAM_B1 ** ADAM_STEP
    bc2 = 1.0 - ADAM_B2 ** ADAM_STEP

    def body(w_ref, g_ref, m_ref, v_ref, d_ref, nm_ref, nv_ref):
        gv = g_ref[...]
        nm = ADAM_B1 * m_ref[...] + (1.0 - ADAM_B1) * gv
        nv = ADAM_B2 * v_ref[...] + (1.0 - ADAM_B2) * (gv * gv)
        d_ref[...] = -ADAM_LR * ((nm / bc1) / (jnp.sqrt(nv / bc2) + ADAM_EPS) + ADAM_WD * w_ref[...])
        nm_ref[...] = nm
        nv_ref[...] = nv

    spec = pl.BlockSpec((tm, c), lambda i: (i, 0))
    return pl.pallas_call(
        body, grid=(r // tm,), in_specs=[spec] * 4, out_specs=[spec] * 3,
        out_shape=[jax.ShapeDtypeStruct((r, c), F32)] * 3,
        compiler_params=_cp(("parallel",)), name=name,
    )(w, g, m, v)


HALO = 8


def _shift_down(cur, prev, j):
    if j == 0:
        return cur
    y = pltpu.roll(cur, j, 0)
    row = lax.broadcasted_iota(jnp.int32, (HALO, cur.shape[1]), 0)
    top = jnp.where(row < j, pltpu.roll(prev, j, 0), y[:HALO])
    return jnp.concatenate([top, y[HALO:]], axis=0)


def _shift_up(cur, nxt, j):
    if j == 0:
        return cur
    tm = cur.shape[0]
    y = pltpu.roll(cur, tm - j, 0)
    row = lax.broadcasted_iota(jnp.int32, (HALO, cur.shape[1]), 0)
    bot = jnp.where(row >= HALO - j, pltpu.roll(nxt, HALO - j, 0), y[tm - HALO:])
    return jnp.concatenate([y[:tm - HALO], bot], axis=0)


def _softplus(x):
    return jnp.maximum(x, 0.0) + jnp.log(1.0 + jnp.exp(-jnp.abs(x)))


def _gdn_pre_fwd(qkvb, conv_w, ab, alog_pad, dt_pad, *, name, ab_blk=0, tm=256):
    t, cw = qkvb.shape
    hd, nh, ck = GDN_HEAD_DIM, GDN_HEADS, GDN_CHUNK
    gw = nh * hd
    tm = _blk(t, tm)
    hb = tm // HALO

    def body(x_ref, xp_ref, w_ref, ab_ref, al_ref, dt_ref, xc_ref, qkvn_ref, aux_ref):
        i = pl.program_id(0)
        cur = x_ref[...]
        prev = jnp.where(i > 0, xp_ref[...], 0.0)
        xc = None
        for tap in range(GDN_CONV):
            term = w_ref[tap:tap + 1, :] * _shift_down(cur, prev, GDN_CONV - 1 - tap)
            xc = term if xc is None else xc + term
        xc_ref[...] = xc
        s = xc * _sigmoid(xc)
        for h in range(2 * nh):
            xh = s[:, h * hd:(h + 1) * hd]
            r = lax.rsqrt(jnp.sum(xh * xh, axis=-1, keepdims=True) + L2_EPS)
            scale = hd ** -0.5 if h < nh else 1.0
            qkvn_ref[:, h * hd:(h + 1) * hd] = xh * (r * scale)
        qkvn_ref[:, 2 * gw:] = s[:, 2 * gw:]
        abv = ab_ref[...]
        lane = lax.broadcasted_iota(jnp.int32, abv.shape, 1)
        g = jnp.where(lane < nh, -jnp.exp(al_ref[...]) * _softplus(abv + dt_ref[...]), 0.0)
        beta = jnp.where((lane >= nh) & (lane < 2 * nh), _sigmoid(abv), 0.0)
        ii = lax.broadcasted_iota(jnp.int32, (tm, tm), 0)
        jj = lax.broadcasted_iota(jnp.int32, (tm, tm), 1)
        tri = jnp.where((ii >= jj) & ((ii ^ jj) < ck), 1.0, 0.0)
        gcum = _dot(tri, g, precision=HI)
        aux_ref[...] = g + beta + pltpu.roll(gcum, 2 * nh, 1)

    row = lambda c: pl.BlockSpec((tm, c), lambda i: (i, 0))
    vec = lambda r, c: pl.BlockSpec((r, c), lambda i: (0, 0))
    return pl.pallas_call(
        body, grid=(t // tm,),
        in_specs=[row(cw), pl.BlockSpec((HALO, cw), lambda i: (jnp.maximum(i * hb - 1, 0), 0)), vec(GDN_CONV, cw),
                  pl.BlockSpec((tm, LANES), lambda i: (i, ab_blk)), vec(1, LANES), vec(1, LANES)],
        out_specs=[row(cw), row(cw), row(LANES)],
        out_shape=[jax.ShapeDtypeStruct((t, cw), F32), jax.ShapeDtypeStruct((t, cw), F32),
                   jax.ShapeDtypeStruct((t, LANES), F32)],
        compiler_params=_cp(("parallel",)), name=name,
    )(qkvb, qkvb, conv_w, ab, alog_pad, dt_pad)


def _gdn_local(q, k, v, b, gc, gc_row):
    ck = GDN_CHUNK
    ii = lax.broadcasted_iota(jnp.int32, (ck, ck), 0)
    jj = lax.broadcasted_iota(jnp.int32, (ck, ck), 1)
    lower, strict = ii >= jj, ii > jj
    dmat = jnp.exp(jnp.where(lower, gc - gc_row, -jnp.inf))
    kk = _dot(k, k, NT, HI)
    lmat = jnp.where(strict, b * kk * dmat, 0.0)
    tinv = jnp.where(ii == jj, 1.0, 0.0) - lmat
    pw = lmat
    for _ in range(int(math.log2(ck)) - 1):
        pw = _dot(pw, pw, precision=HI)
        tinv = tinv + _dot(tinv, pw, precision=HI)
    gam = jnp.exp(gc)
    u = _dot(tinv, b * v, precision=HI)
    w = _dot(tinv, (b * gam) * k, precision=HI)
    qk = _dot(q, k, NT, HI)
    gl = gc[ck - 1:ck, :]
    return dict(lower=lower, strict=strict, dmat=dmat, kk=kk, tinv=tinv, gam=gam, u=u, w=w, qk=qk,
                mm=qk * dmat, gam_c=jnp.exp(gl), kdec=jnp.exp(gl - gc))


def _gdn_core_fwd(qkvn, aux, aux_t, *, name):
    t = qkvn.shape[0]
    hd, nh, ck = GDN_HEAD_DIM, GDN_HEADS, GDN_CHUNK
    gw = nh * hd
    nc = t // ck

    def body(x_ref, aux_ref, auxt_ref, o_ref, sall_ref, s_ref):
        n = pl.program_id(0)

        @pl.when(n == 0)
        def _():
            s_ref[...] = jnp.zeros_like(s_ref)

        for h in range(nh):
            q = x_ref[:, h * hd:(h + 1) * hd]
            k = x_ref[:, gw + h * hd:gw + (h + 1) * hd]
            v = x_ref[:, 2 * gw + h * hd:2 * gw + (h + 1) * hd]
            b = aux_ref[:, nh + h:nh + h + 1]
            gc = aux_ref[:, 2 * nh + h:2 * nh + h + 1]
            gc_row = auxt_ref[0, 2 * nh + h:2 * nh + h + 1, :]
            lc = _gdn_local(q, k, v, b, gc, gc_row)
            st = s_ref[h]
            sall_ref[0, h] = st
            vn = lc["u"] - _dot(lc["w"], st, precision=HI)
            o_ref[:, h * hd:(h + 1) * hd] = _dot(lc["gam"] * q, st, precision=HI) + _dot(lc["mm"], vn, precision=HI)
            s_ref[h] = lc["gam_c"] * st + _dot(lc["kdec"] * k, vn, TN, HI)

    return pl.pallas_call(
        body, grid=(nc,),
        in_specs=[pl.BlockSpec((ck, 3 * gw), lambda i: (i, 0)), pl.BlockSpec((ck, LANES), lambda i: (i, 0)),
                  pl.BlockSpec((1, 16, ck), lambda i: (i, 0, 0))],
        out_specs=[pl.BlockSpec((ck, gw), lambda i: (i, 0)), pl.BlockSpec((1, nh, hd, hd), lambda i: (i, 0, 0, 0))],
        out_shape=[jax.ShapeDtypeStruct((t, gw), F32), jax.ShapeDtypeStruct((nc, nh, hd, hd), F32)],
        scratch_shapes=[pltpu.VMEM((nh, hd, hd), F32)],
        compiler_params=_cp(("arbitrary",)), name=name,
    )(qkvn, aux, aux_t)


def _gdn_core_bwd(qkvn, aux, aux_t, s_all, do, *, name):
    t = qkvn.shape[0]
    hd, nh, ck = GDN_HEAD_DIM, GDN_HEADS, GDN_CHUNK
    gw = nh * hd
    nc = t // ck

    def body(x_ref, aux_ref, auxt_ref, sall_ref, do_ref, dx_ref, daux_ref, ds_ref):
        n = pl.program_id(0)

        @pl.when(n == 0)
        def _():
            ds_ref[...] = jnp.zeros_like(ds_ref)

        lane = lax.broadcasted_iota(jnp.int32, (ck, LANES), 1)
        rowi = lax.broadcasted_iota(jnp.int32, (ck, 1), 0)
        ones = jnp.ones((ck, LANES), F32)
        dgc_all = jnp.zeros((ck, LANES), F32)
        db_all = jnp.zeros((ck, LANES), F32)
        for h in range(nh):
            q = x_ref[:, h * hd:(h + 1) * hd]
            k = x_ref[:, gw + h * hd:gw + (h + 1) * hd]
            v = x_ref[:, 2 * gw + h * hd:2 * gw + (h + 1) * hd]
            b = aux_ref[:, nh + h:nh + h + 1]
            gc = aux_ref[:, 2 * nh + h:2 * nh + h + 1]
            gc_row = auxt_ref[0, 2 * nh + h:2 * nh + h + 1, :]
            lc = _gdn_local(q, k, v, b, gc, gc_row)
            dmat, kk, tinv, gam, u, w, qk, mm = (lc[key] for key in ("dmat", "kk", "tinv", "gam", "u", "w", "qk", "mm"))
            gam_c, kdec = lc["gam_c"], lc["kdec"]
            st = sall_ref[0, h]
            dsn = ds_ref[h]
            d_o = do_ref[:, h * hd:(h + 1) * hd]
            kd = kdec * k
            vn = u - _dot(w, st, precision=HI)
            dvn = _dot(mm, d_o, TN, HI) + _dot(kd, dsn, precision=HI)
            dm = jnp.where(lc["lower"], _dot(d_o, vn, NT, HI), 0.0)
            dqd = _dot(d_o, st, NT, HI)
            dkd = _dot(vn, dsn, NT, HI)
            dgam_c = jnp.sum(jnp.sum(dsn * st, axis=1, keepdims=True), axis=0, keepdims=True)
            ds_ref[h] = _dot(gam * q, d_o, TN, HI) + gam_c * dsn - _dot(w, dvn, TN, HI)
            dw = -_dot(dvn, st, NT, HI)
            drv = _dot(tinv, dvn, TN, HI)
            drk = _dot(tinv, dw, TN, HI)
            da = jnp.where(lc["strict"], -(_dot(drv, u, NT, HI) + _dot(drk, w, NT, HI)), 0.0)
            rs_rk = jnp.sum(drk * k, axis=-1, keepdims=True)
            db = jnp.sum(drv * v, axis=-1, keepdims=True) + gam * rs_rk + jnp.sum(da * kk * dmat, axis=-1, keepdims=True)
            e_mat = da * dmat * b
            dmd = dm * dmat
            dk = ((b * gam) * drk + _dot(e_mat, k, precision=HI) + _dot(e_mat, k, TN, HI)
                  + _dot(dmd, q, TN, HI) + kdec * dkd)
            dq = _dot(dmd, k, precision=HI) + gam * dqd
            f_mat = da * (b * kk) * dmat + dm * qk * dmat
            e_vec = jnp.sum(dkd * kd, axis=-1, keepdims=True)
            dgc = (b * gam * rs_rk + gam * jnp.sum(dqd * q, axis=-1, keepdims=True)
                   + jnp.sum(f_mat, axis=-1, keepdims=True) - _dot(f_mat, ones, TN, HI)[:, 0:1] - e_vec)
            last = jnp.sum(e_vec, axis=0, keepdims=True) + gam_c * dgam_c
            dgc = dgc + jnp.where(rowi == ck - 1, last, 0.0)
            dx_ref[:, h * hd:(h + 1) * hd] = dq
            dx_ref[:, gw + h * hd:gw + (h + 1) * hd] = dk
            dx_ref[:, 2 * gw + h * hd:2 * gw + (h + 1) * hd] = b * drv
            dgc_all = dgc_all + jnp.where(lane == h, dgc, 0.0)
            db_all = db_all + jnp.where(lane == nh + h, db, 0.0)
        ii = lax.broadcasted_iota(jnp.int32, (ck, ck), 0)
        jj = lax.broadcasted_iota(jnp.int32, (ck, ck), 1)
        daux_ref[...] = _dot(jnp.where(jj >= ii, 1.0, 0.0), dgc_all, precision=HI) + db_all

    rev = lambda i: (nc - 1 - i, 0)
    return pl.pallas_call(
        body, grid=(nc,),
        in_specs=[pl.BlockSpec((ck, 3 * gw), rev), pl.BlockSpec((ck, LANES), rev),
                  pl.BlockSpec((1, 16, ck), lambda i: (nc - 1 - i, 0, 0)),
                  pl.BlockSpec((1, nh, hd, hd), lambda i: (nc - 1 - i, 0, 0, 0)), pl.BlockSpec((ck, gw), rev)],
        out_specs=[pl.BlockSpec((ck, 3 * gw), rev), pl.BlockSpec((ck, LANES), rev)],
        out_shape=[jax.ShapeDtypeStruct((t, 3 * gw), F32), jax.ShapeDtypeStruct((t, LANES), F32)],
        scratch_shapes=[pltpu.VMEM((nh, hd, hd), F32)],
        compiler_params=_cp(("arbitrary",)), name=name,
    )(qkvn, aux, aux_t, s_all, do)


def _gdn_pre_bwd1(xc, dqkvn, daux, ab, alog_pad, dt_pad, *, name, ab_blk=0, tm=256):
    t, cw = xc.shape
    hd, nh = GDN_HEAD_DIM, GDN_HEADS
    gw = nh * hd
    tm = _blk(t, tm)

    def body(xc_ref, dy_ref, daux_ref, ab_ref, al_ref, dt_ref, dxc_ref, dab_ref, dal_ref, ddt_ref):
        i = pl.program_id(0)
        xc = xc_ref[...]
        sg = _sigmoid(xc)
        s = xc * sg
        dsilu = sg * (1.0 + xc * (1.0 - sg))
        for h in range(2 * nh):
            xh = s[:, h * hd:(h + 1) * hd]
            scale = hd ** -0.5 if h < nh else 1.0
            dyh = dy_ref[:, h * hd:(h + 1) * hd] * scale
            r = lax.rsqrt(jnp.sum(xh * xh, axis=-1, keepdims=True) + L2_EPS)
            dxh = r * dyh - xh * (r * r * r) * jnp.sum(dyh * xh, axis=-1, keepdims=True)
            dxc_ref[:, h * hd:(h + 1) * hd] = dxh * dsilu[:, h * hd:(h + 1) * hd]
        dxc_ref[:, 2 * gw:] = dy_ref[:, 2 * gw:] * dsilu[:, 2 * gw:]
        abv = ab_ref[...]
        dauxv = daux_ref[...]
        lane = lax.broadcasted_iota(jnp.int32, abv.shape, 1)
        is_a = lane < nh
        is_b = (lane >= nh) & (lane < 2 * nh)
        pre = abv + dt_ref[...]
        neg_ea = -jnp.exp(al_ref[...])
        d_a = jnp.where(is_a, dauxv * neg_ea * _sigmoid(pre), 0.0)
        beta = _sigmoid(abv)
        d_b = jnp.where(is_b, dauxv * beta * (1.0 - beta), 0.0)
        dab_ref[...] = (d_a + d_b).astype(dab_ref.dtype)
        dal = jnp.sum(jnp.where(is_a, dauxv * neg_ea * _softplus(pre), 0.0), axis=0, keepdims=True)
        ddt = jnp.sum(d_a, axis=0, keepdims=True)

        @pl.when(i == 0)
        def _():
            dal_ref[...] = dal
            ddt_ref[...] = ddt

        @pl.when(i > 0)
        def _():
            dal_ref[...] += dal
            ddt_ref[...] += ddt

    row = lambda c: pl.BlockSpec((tm, c), lambda i: (i, 0))
    vec = pl.BlockSpec((1, LANES), lambda i: (0, 0))
    return pl.pallas_call(
        body, grid=(t // tm,),
        in_specs=[row(cw), row(cw), row(LANES), pl.BlockSpec((tm, LANES), lambda i: (i, ab_blk)), vec, vec],
        out_specs=[row(cw), row(LANES), vec, vec],
        out_shape=[jax.ShapeDtypeStruct((t, cw), F32), jax.ShapeDtypeStruct((t, LANES), BF16),
                   jax.ShapeDtypeStruct((1, LANES), F32), jax.ShapeDtypeStruct((1, LANES), F32)],
        compiler_params=_cp(("arbitrary",)), name=name,
    )(xc, dqkvn, daux, ab, alog_pad, dt_pad)


def _gdn_pre_bwd2(dxc, qkvb, conv_w, *, name, tm=256):
    t, cw = dxc.shape
    tm = _blk(t, tm)
    hb = tm // HALO
    nblk = t // tm

    def body(d_ref, dn_ref, x_ref, xp_ref, w_ref, dx_ref, dw_ref):
        i = pl.program_id(0)
        dcur = d_ref[...]
        dnxt = jnp.where(i < nblk - 1, dn_ref[...], 0.0)
        cur = x_ref[...]
        prev = jnp.where(i > 0, xp_ref[...], 0.0)
        dx = None
        dws = []
        for tap in range(GDN_CONV):
            j = GDN_CONV - 1 - tap
            term = w_ref[tap:tap + 1, :] * _shift_up(dcur, dnxt, j)
            dx = term if dx is None else dx + term
            dws.append(jnp.sum(dcur * _shift_down(cur, prev, j), axis=0, keepdims=True))
        dx_ref[...] = dx.astype(dx_ref.dtype)
        dw = jnp.concatenate(dws, axis=0)

        @pl.when(i == 0)
        def _():
            dw_ref[...] = dw

        @pl.when(i > 0)
        def _():
            dw_ref[...] += dw

    row = pl.BlockSpec((tm, cw), lambda i: (i, 0))
    wsp = pl.BlockSpec((GDN_CONV, cw), lambda i: (0, 0))
    return pl.pallas_call(
        body, grid=(nblk,),
        in_specs=[row, pl.BlockSpec((HALO, cw), lambda i: (jnp.minimum((i + 1) * hb, t // HALO - 1), 0)),
                  row, pl.BlockSpec((HALO, cw), lambda i: (jnp.maximum(i * hb - 1, 0), 0)), wsp],
        out_specs=[row, wsp],
        out_shape=[jax.ShapeDtypeStruct((t, cw), BF16), jax.ShapeDtypeStruct((GDN_CONV, cw), F32)],
        compiler_params=_cp(("arbitrary",)), name=name,
    )(dxc, dxc, qkvb, qkvb, conv_w)


def _gdn_post_fwd(o, z, norm_w, *, name, tm=512):
    t, gw = o.shape
    hd, nh = GDN_HEAD_DIM, GDN_HEADS
    tm = _blk(t, tm)

    def body(o_ref, z_ref, w_ref, y_ref):
        zv = z_ref[...]
        sz = zv * _sigmoid(zv)
        for h in range(nh):
            oh = o_ref[:, h * hd:(h + 1) * hd]
            r = lax.rsqrt(jnp.mean(oh * oh, axis=-1, keepdims=True) + RMS_EPS)
            y_ref[:, h * hd:(h + 1) * hd] = (oh * r * w_ref[...] * sz[:, h * hd:(h + 1) * hd]).astype(y_ref.dtype)

    row = pl.BlockSpec((tm, gw), lambda i: (i, 0))
    return pl.pallas_call(
        body, grid=(t // tm,), in_specs=[row, row, pl.BlockSpec((1, hd), lambda i: (0, 0))], out_specs=row,
        out_shape=jax.ShapeDtypeStruct((t, gw), BF16), compiler_params=_cp(("parallel",)), name=name,
    )(o, z, norm_w)


def _gdn_post_bwd(dy, o, z, norm_w, *, name, tm=512):
    t, gw = o.shape
    hd, nh = GDN_HEAD_DIM, GDN_HEADS
    tm = _blk(t, tm)

    def body(dy_ref, o_ref, z_ref, w_ref, do_ref, dz_ref, dw_ref):
        i = pl.program_id(0)
        zv = z_ref[...]
        sg = _sigmoid(zv)
        sz = zv * sg
        dsz = sg * (1.0 + zv * (1.0 - sg))
        dw = None
        for h in range(nh):
            sl = slice(h * hd, (h + 1) * hd)
            oh = o_ref[:, sl]
            dyh = dy_ref[:, sl].astype(F32)
            r = lax.rsqrt(jnp.mean(oh * oh, axis=-1, keepdims=True) + RMS_EPS)
            xh = oh * r
            dz_ref[:, sl] = (dyh * xh * w_ref[...] * dsz[:, sl]).astype(dz_ref.dtype)
            dn = dyh * sz[:, sl]
            dxh = dn * w_ref[...]
            do_ref[:, sl] = r * (dxh - xh * jnp.mean(dxh * xh, axis=-1, keepdims=True))
            part = jnp.sum(dn * xh, axis=0, keepdims=True)
            dw = part if dw is None else dw + part

        @pl.when(i == 0)
        def _():
            dw_ref[...] = dw

        @pl.when(i > 0)
        def _():
            dw_ref[...] += dw

    row = pl.BlockSpec((tm, gw), lambda i: (i, 0))
    vec = pl.BlockSpec((1, hd), lambda i: (0, 0))
    return pl.pallas_call(
        body, grid=(t // tm,), in_specs=[row, row, row, vec], out_specs=[row, row, vec],
        out_shape=[jax.ShapeDtypeStruct((t, gw), F32), jax.ShapeDtypeStruct((t, gw), BF16),
                   jax.ShapeDtypeStruct((1, hd), F32)],
        compiler_params=_cp(("arbitrary",)), name=name,
    )(dy, o, z, norm_w)


IN_NAMES = ("q_a", "kv_a", "qkv_b", "ab", "z", "q_c", "gates")
CAT_NAMES = ("gates", "q_a", "qkv_b", "z", "q_c", "kv_a", "ab")
AB_PAD = 256


def _in_widths(d):
    gw = GDN_HEADS * GDN_HEAD_DIM
    return dict(q_a=SWA_Q_HEADS * SWA_HEAD_DIM, kv_a=2 * SWA_KV_HEADS * SWA_HEAD_DIM, qkv_b=3 * gw, ab=2 * GDN_HEADS,
                z=gw, q_c=XA_HEADS * XA_HEAD_DIM, gates=3 * d)


def _ranges(names, widths):
    out, start = {}, 0
    for k in names:
        out[k] = (start, widths[k])
        start += widths[k]
    return out, start


def _cat_ranges(d):
    widths = dict(_in_widths(d), ab=AB_PAD)
    return _ranges(CAT_NAMES, widths)


def _to_cat(w_in):
    d = w_in.shape[0]
    src, _ = _ranges(IN_NAMES, _in_widths(d))
    cols = []
    for k in CAT_NAMES:
        s, w = src[k]
        cols.append(w_in[:, s:s + w])
    cols.append(jnp.zeros((d, AB_PAD - src["ab"][1]), w_in.dtype))
    return jnp.concatenate(cols, axis=1)


def _from_cat(w_cat):
    d = w_cat.shape[0]
    widths = _in_widths(d)
    cat, _ = _cat_ranges(d)
    return jnp.concatenate([w_cat[:, cat[k][0]:cat[k][0] + widths[k]] for k in IN_NAMES], axis=1)


def _pad_cols(a, width):
    return jnp.pad(a, ((0, 0), (0, width - a.shape[1])))


def _relu2_epilogue(acc):
    r = jnp.maximum(acc, 0.0)
    return acc, r * r


def _add_epilogue(acc, res):
    return (acc + res,)


def _drelu2_epilogue(acc, u):
    return (acc * (2.0 * jnp.maximum(u.astype(F32), 0.0)),)


def _local_step(x, mem, tgt, wts, small):
    t, d = x.shape
    nh = GDN_HEADS
    w_cat = wts["w_cat"]
    cat, cat_w = _cat_ranges(d)
    assert w_cat.shape == (d, cat_w)
    alog_pad = _pad_cols(small["a_log"], LANES)
    dt_pad = _pad_cols(small["dt_bias"], LANES)
    kvw = cat["kv_a"][1]
    assert cat["ab"][0] == cat["kv_a"][0] + kvw
    ab_blk = kvw // LANES

    n = _rms_fwd(x, small["g_mix"], name="rms_mix")
    q_a = _mm(n, w_cat, b_window=cat["q_a"], out_dtypes=(BF16,), name="in_q_a")
    kv_a, ab = _mm(n, w_cat, b_window=(cat["kv_a"][0], kvw + AB_PAD), out_dtypes=(BF16, F32), name="in_kv_ab")
    qkvb = _mm(n, w_cat, b_window=cat["qkv_b"], tn=512, name="in_qkv_b")
    z = _mm(n, w_cat, b_window=cat["z"], name="in_z")
    q_c = _mm(n, w_cat, b_window=cat["q_c"], out_dtypes=(BF16,), name="in_q_c")
    gates = _mm(n, w_cat, b_window=cat["gates"], name="in_gates")
    y_a, lse = _swa_fwd(q_a, kv_a, small["sinks"], name="swa_fwd")
    xc, qkvn, aux = _gdn_pre_fwd(qkvb, small["conv_w"], ab, alog_pad, dt_pad, ab_blk=ab_blk, name="gdn_pre_fwd")
    aux_t = aux[:, :16].reshape(t // GDN_CHUNK, GDN_CHUNK, 16).transpose(0, 2, 1)
    o_b, s_all = _gdn_core_fwd(qkvn, aux, aux_t, name="gdn_core_fwd")
    y_b = _gdn_post_fwd(o_b, z, small["gdn_norm_w"], name="gdn_post_fwd")
    nmem = _rms_fwd(mem, small["g_mem"], name="rms_mem")
    mkv = _mm(nmem, wts["w_mem_kv"], out_dtypes=(BF16,), name="mem_kv")
    y_c = _xa_fwd(q_c, mkv, name="xa_fwd")
    ys = (y_a, y_b, y_c)
    w_ups = (wts["w_swa_up"], wts["w_gdn_up"], wts["w_xa_up"])
    merged = _merge_fwd(ys, w_ups, gates, name="merge_fwd")
    h1 = _mm(merged, wts["w_out"], extras=(x,), epilogue=_add_epilogue, name="out_proj")
    n2 = _rms_fwd(h1, small["g_mlp"], name="rms_mlp")
    u, act = _mm(n2, wts["w_mlp_in"], b_sharded=True, out_dtypes=(BF16, BF16), epilogue=_relu2_epilogue, name="mlp_in")
    h2 = _mm(act, wts["w_mlp_out"], extras=(h1,), epilogue=_add_epilogue, name="mlp_out")
    dh2, dg_final, loss = _final_loss(h2, small["g_final"], tgt, name="final_loss")

    grads = {"g_final": dg_final}
    du = _mm(dh2, wts["w_mlp_out"], tb=True, out_dtypes=(BF16,), extras=(u,), epilogue=_drelu2_epilogue, name="d_mlp_act")
    grads["w_mlp_out"] = _mm(act, dh2, ta=True, out_dtypes=(BF16,), name="dw_mlp_out")
    grads["w_mlp_in"] = _mm(n2, du, ta=True, out_sharded=True, out_dtypes=(BF16,), name="dw_mlp_in")
    dn2 = _mm(du, wts["w_mlp_in"], tb=True, b_sharded=True, name="d_mlp_in")
    dh1, grads["g_mlp"] = _rms_bwd(dn2, h1, small["g_mlp"], dh2, name="rms_mlp_bwd")
    dmerged = _mm(dh1, wts["w_out"], tb=True, name="d_out_proj")
    grads["w_out"] = _mm(merged, dh1, ta=True, out_dtypes=(BF16,), name="dw_out")
    dus_and_dgates = _merge_bwd(ys, w_ups, gates, dmerged, name="merge_bwd")
    dus, dgates = dus_and_dgates[:3], dus_and_dgates[3:]
    dys = []
    for y, du_i, w_up, key in zip(ys, dus, w_ups, ("w_swa_up", "w_gdn_up", "w_xa_up")):
        dys.append(_mm(du_i, w_up, tb=True, b_sharded=True, out_dtypes=(BF16,), name="d_" + key))
        grads[key] = _mm(y, du_i, ta=True, out_sharded=True, out_dtypes=(BF16,), name="dw_" + key[2:])
    dq_a, dkv_a, grads["sinks"] = _swa_bwd(q_a, kv_a, small["sinks"], y_a, lse, dys[0], name="swa_bwd")
    do_b, dz, grads["gdn_norm_w"] = _gdn_post_bwd(dys[1], o_b, z, small["gdn_norm_w"], name="gdn_post_bwd")
    dqkvn, daux = _gdn_core_bwd(qkvn, aux, aux_t, s_all, do_b, name="gdn_core_bwd")
    dxc, dab, dalog, ddt = _gdn_pre_bwd1(xc, dqkvn, daux, ab, alog_pad, dt_pad, ab_blk=ab_blk, name="gdn_pre_bwd1")
    grads["a_log"], grads["dt_bias"] = dalog[:, :nh], ddt[:, :nh]
    dqkvb, grads["conv_w"] = _gdn_pre_bwd2(dxc, qkvb, small["conv_w"], name="gdn_pre_bwd2")
    dq_c, dmkv = _xa_bwd(q_c, mkv, dys[2], name="xa_bwd")
    grads["w_mem_kv"] = _mm(nmem, dmkv, ta=True, out_dtypes=(BF16,), name="dw_mem_kv")
    dnmem = _mm(dmkv, wts["w_mem_kv"], tb=True, name="d_mem_kv")
    _, grads["g_mem"] = _rms_bwd(dnmem, mem, small["g_mem"], jnp.zeros_like(mem), name="rms_mem_bwd")
    dp = jnp.concatenate([*dgates, dq_a, dqkvb, dz, dq_c, dkv_a, dab, jnp.zeros((t, AB_PAD - LANES), BF16)], axis=1)
    dn = _mm(dp, w_cat, tb=True, name="d_in_proj")
    grads["w_cat"] = _mm(n, dp, ta=True, out_dtypes=(BF16,), name="dw_in")
    dx, grads["g_mix"] = _rms_bwd(dn, x, small["g_mix"], dh1, name="rms_mix_bwd")
    return loss, dx, grads


HBM_SPEC = pl.BlockSpec(memory_space=pltpu.HBM)
VMEM_SPEC = pl.BlockSpec(memory_space=pltpu.VMEM)
N_CHIPS = N_SHARDS
N_DEV = 8
DMA_CHUNK_BYTES = 1 << 20


def _place():
    return lax.axis_index("x"), lax.axis_index("y"), lax.axis_index("c")


def _other_chips(x, y):
    return [(1 - x, y), (x, 1 - y), (1 - x, 1 - y)]


def _n_chunks(rows, row_bytes):
    n = 1
    while rows % (2 * n) == 0 and (rows // (2 * n)) % 16 == 0 and (rows // n) * row_bytes > DMA_CHUNK_BYTES:
        n *= 2
    return n


def _sem_scratch(n_remote, n_local):
    return [pltpu.SemaphoreType.DMA((max(n_remote, 1),)), pltpu.SemaphoreType.DMA((max(n_remote, 1),)),
            pltpu.SemaphoreType.DMA((max(n_local, 1),))]


def _all_gather_weights(shards, *, name):
    nw = len(shards)
    plan = []
    for i, s in enumerate(shards):
        rh = s.shape[0] // 2
        nch = _n_chunks(rh, s.shape[1] * s.dtype.itemsize)
        plan += [(i, q * (rh // nch), rh // nch) for q in range(nch)]
    n_ici = 3 * len(plan)

    def body(*refs):
        w_refs, out_refs = refs[:nw], refs[nw:2 * nw]
        send_sems, recv_sems, local_sems = refs[2 * nw:]
        x, y, c = _place()
        sibling = (x, y, 1 - c)
        chips = _other_chips(x, y)

        def copy(k, src, dst, to):
            return pltpu.make_async_remote_copy(src_ref=src, dst_ref=dst, send_sem=send_sems.at[k],
                                                recv_sem=recv_sems.at[k], device_id=to, device_id_type=MESH)

        local = [pltpu.make_async_copy(w_refs[i], out_refs[i].at[2 * x + y], local_sems.at[i]) for i in range(nw)]
        for cp in local:
            cp.start()
        first, arrive, passed, from_sibling = [], [], [], []
        for i, r0, nr in plan:
            rh = shards[i].shape[0] // 2
            mine = pl.ds(c * rh + r0, nr)
            other = pl.ds((1 - c) * rh + r0, nr)
            for chip in chips:
                k = len(first)
                cid = 2 * chip[0] + chip[1]
                first.append(copy(k, w_refs[i].at[mine], out_refs[i].at[2 * x + y, mine], (*chip, c)))
                arrive.append(copy(k, out_refs[i].at[cid, mine], out_refs[i].at[cid, mine], (x, y, c)))
                passed.append(copy(n_ici + k, out_refs[i].at[cid, mine], out_refs[i].at[cid, mine], sibling))
                from_sibling.append(copy(n_ici + k, out_refs[i].at[cid, other], out_refs[i].at[cid, other], (x, y, c)))
        for cp in first:
            cp.start()
        for k in range(n_ici):
            arrive[k].wait_recv()
            passed[k].start()
        for cp in from_sibling:
            cp.wait_recv()
        for cp in first + passed:
            cp.wait_send()
        for cp in local:
            cp.wait()

    return pl.pallas_call(
        body, out_shape=[jax.ShapeDtypeStruct((N_CHIPS, *s.shape), s.dtype) for s in shards],
        in_specs=[HBM_SPEC] * nw, out_specs=[HBM_SPEC] * nw,
        scratch_shapes=_sem_scratch(2 * n_ici, nw), name=name,
    )(*shards)


def _exchange_call(copies_of, n_copies, ins, out_shapes, *, name, aliases=None):
    n_in, n_out = len(ins), len(out_shapes)

    def body(*refs):
        in_refs, out_refs = refs[:n_in], refs[n_in:n_in + n_out]
        send_sems, recv_sems, _ = refs[n_in + n_out:]
        todo = copies_of(in_refs, out_refs, _place())
        assert len(todo) == n_copies
        cps = [pltpu.make_async_remote_copy(src_ref=src, dst_ref=dst, send_sem=send_sems.at[k], recv_sem=recv_sems.at[k],
                                            device_id=to, device_id_type=MESH) for k, (src, dst, to) in enumerate(todo)]
        for cp in cps:
            cp.start()
        for cp in cps:
            cp.wait()

    return pl.pallas_call(
        body, out_shape=out_shapes, in_specs=[HBM_SPEC] * n_in, out_specs=[HBM_SPEC] * n_out,
        scratch_shapes=_sem_scratch(n_copies, 0), input_output_aliases=aliases or {}, name=name,
    )(*ins)


def _half_chunks(arrs, row_axis):
    plan = []
    for i, a in enumerate(arrs):
        rh = a.shape[row_axis] // 2
        row_bytes = a.dtype.itemsize * math.prod(a.shape) // a.shape[row_axis]
        nch = _n_chunks(rh, row_bytes)
        plan += [(i, q * (rh // nch), rh // nch) for q in range(nch)]
    return plan


def _sibling_halves(gs, *, name):
    plan = _half_chunks(gs, 1)

    def copies_of(in_refs, out_refs, place):
        x, y, c = place
        out = []
        for i, r0, nr in plan:
            rh = gs[i].shape[1] // 2
            out.append((in_refs[i].at[:, pl.ds((1 - c) * rh + r0, nr), :], out_refs[i].at[:, pl.ds(r0, nr), :],
                        (x, y, 1 - c)))
        return out

    shapes = [jax.ShapeDtypeStruct((g.shape[0], g.shape[1] // 2, g.shape[2]), g.dtype) for g in gs]
    return _exchange_call(copies_of, len(plan), gs, shapes, name=name)


def _chip_exchange(s1s, *, name):
    plan = _half_chunks([jax.ShapeDtypeStruct((2 * s.shape[1], s.shape[2]), s.dtype) for s in s1s], 0)

    def copies_of(in_refs, out_refs, place):
        x, y, c = place
        out = []
        for i, r0, nr in plan:
            for j, chip in enumerate(_other_chips(x, y)):
                out.append((in_refs[i].at[2 * chip[0] + chip[1], pl.ds(r0, nr), :], out_refs[i].at[j, pl.ds(r0, nr), :],
                            (*chip, c)))
        return out

    shapes = [jax.ShapeDtypeStruct((3, *s.shape[1:]), s.dtype) for s in s1s]
    return _exchange_call(copies_of, 3 * len(plan), s1s, shapes, name=name)


def _join_halves(gs, *, name):
    plan = _half_chunks(gs, 0)

    def copies_of(in_refs, out_refs, place):
        x, y, c = place
        out = []
        for i, r0, nr in plan:
            rows = out_refs[i].at[pl.ds(c * (gs[i].shape[0] // 2) + r0, nr), :]
            out.append((rows, rows, (x, y, 1 - c)))
        return out

    shapes = [jax.ShapeDtypeStruct(g.shape, g.dtype) for g in gs]
    return _exchange_call(copies_of, len(plan), gs, shapes, name=name, aliases={i: i for i in range(len(gs))})


def _row_block(rows, cols):
    tb = rows
    while tb % 32 == 0 and tb * cols * 4 > (2 << 20):
        tb //= 2
    return tb


def _pair_sum(g, sib, core, *, name):
    ns, r, c = g.shape
    rh = r // 2
    tb = _row_block(rh, c)
    nb = rh // tb

    def body(core_ref, g_ref, s_ref, o_ref):
        o_ref[...] = (g_ref[...].astype(F32) + s_ref[...].astype(F32)).astype(o_ref.dtype)

    mine = pl.BlockSpec((None, tb, c), lambda s, i, core_ref: (s, core_ref[0] * nb + i, 0))
    half = pl.BlockSpec((None, tb, c), lambda s, i, core_ref: (s, i, 0))
    return pl.pallas_call(
        body, grid_spec=pltpu.PrefetchScalarGridSpec(num_scalar_prefetch=1, grid=(ns, nb), in_specs=[mine, half],
                                                     out_specs=half),
        out_shape=jax.ShapeDtypeStruct((ns, rh, c), BF16), compiler_params=_cp(("parallel", "parallel")), name=name,
    )(core, g, sib)


def _chip_sum(s1, rcv, where, *, name):
    _, rh, c = s1.shape
    tb = _row_block(rh, c)
    nb = rh // tb

    def body(where_ref, own_ref, r0_ref, r1_ref, r2_ref, o_ref):
        acc = own_ref[...].astype(F32)
        for r in (r0_ref, r1_ref, r2_ref):
            acc = acc + r[...].astype(F32)
        o_ref[...] = acc

    own = pl.BlockSpec((None, tb, c), lambda i, w: (w[1], i, 0))
    got = [pl.BlockSpec((None, tb, c), functools.partial(lambda i, w, j: (j, i, 0), j=j)) for j in range(3)]
    return pl.pallas_call(
        body, grid_spec=pltpu.PrefetchScalarGridSpec(
            num_scalar_prefetch=1, grid=(nb,), in_specs=[own] + got,
            out_specs=pl.BlockSpec((tb, c), lambda i, w: (w[0] * nb + i, 0))),
        out_shape=jax.ShapeDtypeStruct((2 * rh, c), F32), compiler_params=_cp(("parallel",)), name=name,
    )(where, s1, rcv, rcv, rcv)


def _all_gather_small(blk, *, name):
    r = blk.shape[0]

    def body(b_ref, out_ref, send_sems, recv_sems):
        x, y, c = _place()
        me = 4 * x + 2 * y + c
        out_ref[me] = b_ref[...]
        sends = []
        for k in range(1, N_DEV):
            peer = (x ^ (k >> 2), y ^ ((k >> 1) & 1), c ^ (k & 1))
            sends.append(pltpu.make_async_remote_copy(src_ref=b_ref, dst_ref=out_ref.at[me], send_sem=send_sems.at[k - 1],
                                                      recv_sem=recv_sems.at[k - 1], device_id=peer, device_id_type=MESH))
        for cp in sends:
            cp.start()
        for k in range(1, N_DEV):
            rows = out_ref.at[me ^ k]
            pltpu.make_async_remote_copy(src_ref=rows, dst_ref=rows, send_sem=send_sems.at[k - 1],
                                         recv_sem=recv_sems.at[k - 1], device_id=(x, y, c), device_id_type=MESH).wait_recv()
        for cp in sends:
            cp.wait_send()

    return pl.pallas_call(
        body, out_shape=jax.ShapeDtypeStruct((N_DEV, r, LANES), blk.dtype), in_specs=[VMEM_SPEC], out_specs=VMEM_SPEC,
        scratch_shapes=[pltpu.SemaphoreType.DMA((N_DEV - 1,)), pltpu.SemaphoreType.DMA((N_DEV - 1,))],
        name=name,
    )(blk)


def _sum_rows(parts, out_dtype, *, name, tb=1024):
    rows = parts[0].shape[0]
    tb = _blk(rows, tb)

    def body(*refs):
        acc = refs[0][...].astype(F32)
        for r in refs[1:-1]:
            acc = acc + r[...].astype(F32)
        refs[-1][...] = acc.astype(refs[-1].dtype)

    spec = pl.BlockSpec((tb, LANES), lambda i: (i, 0))
    return pl.pallas_call(
        body, grid=(rows // tb,), in_specs=[spec] * len(parts), out_specs=spec,
        out_shape=jax.ShapeDtypeStruct((rows, LANES), out_dtype), compiler_params=_cp(("parallel",)), name=name,
    )(*parts)


def _reduce_scatter(gs, core, where):
    sibs = _sibling_halves(gs, name="rs_sibling_halves")
    s1s = [_pair_sum(g, s, core, name=f"rs_pair_sum_{i}") for i, (g, s) in enumerate(zip(gs, sibs))]
    rcvs = _chip_exchange(s1s, name="rs_chip_exchange")
    halves = [_chip_sum(s1, rcv, where, name=f"rs_chip_sum_{i}") for i, (s1, rcv) in enumerate(zip(s1s, rcvs))]
    return _join_halves(halves, name="rs_join_halves")


BIG = (
    ("w_in", 1), ("w_mem_kv", 0), ("w_swa_up", 1), ("w_gdn_up", 1), ("w_xa_up", 1), ("w_out", 0), ("w_mlp_in", 1),
    ("w_mlp_out", 0))
SMALL = ("g_mix", "sinks", "a_log", "dt_bias", "gdn_norm_w", "g_mem", "g_mlp", "g_final")


def _rows128(a, rows):
    flat = a.reshape(-1)
    return jnp.pad(flat, (0, rows * LANES - flat.shape[0])).reshape(rows, LANES)


def kernel(x, mem, g_mix, w_in, sinks, conv_w, a_log, dt_bias, gdn_norm_w, g_mem, w_mem_kv, w_swa_up, w_gdn_up, w_xa_up, w_out, g_mlp, w_mlp_in, w_mlp_out, g_final, loss_target, m_g_mix, m_w_in, m_sinks, m_conv_w, m_a_log, m_dt_bias, m_gdn_norm_w, m_g_mem, m_w_mem_kv, m_w_swa_up, m_w_gdn_up, m_w_xa_up, m_w_out, m_g_mlp, m_w_mlp_in, m_w_mlp_out, m_g_final, v_g_mix, v_w_in, v_sinks, v_conv_w, v_a_log, v_dt_bias, v_gdn_norm_w, v_g_mem, v_w_mem_kv, v_w_swa_up, v_w_gdn_up, v_w_xa_up, v_w_out, v_g_mlp, v_w_mlp_in, v_w_mlp_out, v_g_final):
    given = dict(locals())
    xi, yi, ci = _place()
    chip = 2 * xi + yi
    core = jnp.reshape(ci, (1,)).astype(jnp.int32)
    where = jnp.stack([ci, chip]).astype(jnp.int32)

    shards = [given[k][0].astype(BF16) for k, _ in BIG]
    gathered = dict(zip([k for k, _ in BIG], _all_gather_weights(shards, name="ag_weights")))
    wts = {}
    for k, ax in BIG:
        g = gathered[k]
        if k == "w_in":
            wts["w_cat"] = _to_cat(jnp.transpose(g, (1, 0, 2)).reshape(g.shape[1], -1))
        elif ax == 0:
            wts[k] = g.reshape(-1, g.shape[2])
        else:
            wts[k] = g
    conv_shard = conv_w[0]
    conv_rows = -(-conv_shard.size // (8 * LANES)) * 8
    conv_all = _all_gather_small(_rows128(conv_shard, conv_rows), name="ag_conv")
    conv_full = jnp.concatenate(
        [conv_all[2 * s].reshape(-1)[:conv_shard.size].reshape(conv_shard.shape) for s in range(N_CHIPS)], axis=1)

    small = {k: given[k].reshape(1, -1) for k in SMALL}
    small["conv_w"] = conv_full
    loss_row, dx, grads = _local_step(x[0], mem[0], loss_target[0], wts, small)

    gs = []
    for (k, ax), sh in zip(BIG, shards):
        if k == "w_in":
            dw_in = _from_cat(grads["w_cat"])
            gs.append(jnp.transpose(dw_in.reshape(dw_in.shape[0], N_CHIPS, -1), (1, 0, 2)))
        elif ax == 0:
            gs.append(grads[k].reshape(N_CHIPS, *sh.shape))
        else:
            gs.append(grads[k])
    big_grads = dict(zip([k for k, _ in BIG], _reduce_scatter(gs, core, where)))

    layout = [("loss", loss_row[:, :1])] + [(k, grads[k]) for k in SMALL] + [("conv_w", grads["conv_w"])]
    rows = [-(-a.size // LANES) for _, a in layout]
    blk_rows = -(-sum(rows) // 8) * 8
    blk = jnp.concatenate([_rows128(a.astype(F32), n) for (_, a), n in zip(layout, rows)]
                          + [jnp.zeros((blk_rows - sum(rows), LANES), F32)], axis=0)
    gathered = _all_gather_small(blk, name="ag_small_grads")
    reduced = _sum_rows([gathered[i] for i in range(N_DEV)], F32, name="small_grad_sum")
    small_grads, start = {}, 0
    for (k, a), n in zip(layout, rows):
        small_grads[k] = reduced[start:start + n].reshape(-1)[:a.size].reshape(a.shape)
        start += n
    loss = small_grads["loss"].reshape(())
    cw = conv_shard.shape[1]
    conv_grad = lax.dynamic_slice_in_dim(small_grads["conv_w"], chip * cw, cw, axis=1)

    names = ["g_mix", "w_in", "sinks", "conv_w", "a_log", "dt_bias", "gdn_norm_w", "g_mem", "w_mem_kv", "w_swa_up",
             "w_gdn_up", "w_xa_up", "w_out", "g_mlp", "w_mlp_in", "w_mlp_out", "g_final"]
    out_g, out_d, out_m, out_v = [], [], [], []
    for k in names:
        w, m, v = given[k], given["m_" + k], given["v_" + k]
        if k in big_grads:
            g2 = big_grads[k]
        elif k == "conv_w":
            g2 = conv_grad
        else:
            g2 = small_grads[k]
        shape2 = g2.shape
        delta, new_m, new_v = _adamw(w.reshape(shape2), g2, m.reshape(shape2), v.reshape(shape2), name="adamw_" + k)
        out_g.append(g2.reshape(w.shape))
        out_d.append(delta.reshape(w.shape))
        out_m.append(new_m.reshape(w.shape))
        out_v.append(new_v.reshape(w.shape))
    return (loss, dx[None], *out_g, *out_d, *out_m, *out_v)
```

```python
import functools
import math

import jax
import jax.numpy as jnp
from jax import lax
from jax.experimental import pallas as pl
from jax.experimental.pallas import tpu as pltpu

F32 = jnp.float32
BF16 = jnp.bfloat16
HI = lax.Precision.HIGHEST
MESH = pl.DeviceIdType.MESH

SWA_Q_HEADS = 16
SWA_KV_HEADS = 2
SWA_HEAD_DIM = 64
SWA_WINDOW = 128
GDN_HEADS = 4
GDN_HEAD_DIM = 128
GDN_CONV = 4
GDN_CHUNK = 64
XA_HEADS = 4
XA_HEAD_DIM = 128
RMS_EPS = 1e-6
L2_EPS = 1e-6
ADAM_LR = 0.001
ADAM_B1 = 0.9
ADAM_B2 = 0.999
ADAM_EPS = 1e-08
ADAM_WD = 0.01
ADAM_STEP = 10

LANES = 128
N_SHARDS = 4
VMEM_LIMIT = 56 * 1024 * 1024

NT = (((1,), (1,)), ((), ()))
TN = (((0,), (0,)), ((), ()))
NN = (((1,), (0,)), ((), ()))


def _cp(sem=None):
    return pltpu.CompilerParams(dimension_semantics=sem, vmem_limit_bytes=VMEM_LIMIT)


def _blk(dim, pref):
    if dim <= pref:
        return dim
    b = (pref // LANES) * LANES
    while dim % b:
        b -= LANES
    assert b > 0, (dim, pref)
    return b


def _dot(a, b, dims=NN, precision=None):
    return lax.dot_general(a, b, dims, precision=precision, preferred_element_type=F32)


def _sigmoid(x):
    return 1.0 / (1.0 + jnp.exp(-x))


MM_TK_BYTES = 4096


def _mm(a, b, *, name, ta=False, tb=False, out_dtypes=(F32,), epilogue=None, extras=(), tm=1024, tn=1024, tk=None,
        b_sharded=False, out_sharded=False, b_window=None):
    (kdim, m) = a.shape if ta else a.shape[::-1]
    col0 = 0
    n_lim = k_lim = None
    if b_sharded:
        ns, rows_w, per = b.shape
        if tb:
            kb, n, k_lim = ns * per, rows_w, per
        else:
            kb, n, n_lim = rows_w, ns * per, per
    else:
        (kb, n) = b.shape[::-1] if tb else b.shape
        if b_window is not None:
            assert not tb
            col0, n = b_window
    assert kdim == kb, (a.shape, b.shape, ta, tb)
    if out_sharded:
        assert n % N_SHARDS == 0
        n_lim = n // N_SHARDS if n_lim is None else n_lim
        assert n_lim == n // N_SHARDS
    if tk is None:
        tk = MM_TK_BYTES // max(a.dtype.itemsize, b.dtype.itemsize)
    tm, tn, tk = _blk(m, tm), _blk(n_lim or n, tn), _blk(k_lim or kdim, tk)
    assert col0 % tn == 0, (col0, tn)
    nk = kdim // tk
    a_spec = pl.BlockSpec((tk, tm), lambda i, j, k: (k, i)) if ta else pl.BlockSpec((tm, tk), lambda i, j, k: (i, k))
    if b_sharded and tb:
        kpb = k_lim // tk
        b_spec = pl.BlockSpec((None, tn, tk), lambda i, j, k: (k // kpb, j, k % kpb))
    elif b_sharded:
        bpb = n_lim // tn
        b_spec = pl.BlockSpec((None, tk, tn), lambda i, j, k: (j // bpb, k, j % bpb))
    elif tb:
        b_spec = pl.BlockSpec((tn, tk), lambda i, j, k: (j, k))
    else:
        b_spec = pl.BlockSpec((tk, tn), lambda i, j, k: (k, j + col0 // tn))
    x_spec = pl.BlockSpec((tm, tn), lambda i, j, k: (i, j))
    if out_sharded:
        opb = n_lim // tn
        o_spec = pl.BlockSpec((None, tm, tn), lambda i, j, k: (j // opb, i, j % opb))
        out_shape = (N_SHARDS, m, n_lim)
    else:
        o_spec, out_shape = x_spec, (m, n)
    dims = ((((0 if ta else 1),), ((1 if tb else 0),)), ((), ()))
    n_extra, n_out = len(extras), len(out_dtypes)

    def body(*refs):
        a_ref, b_ref = refs[:2]
        extra_refs = refs[2:2 + n_extra]
        out_refs = refs[2 + n_extra:2 + n_extra + n_out]
        part = _dot(a_ref[...].astype(BF16), b_ref[...].astype(BF16), dims)

        def finish(acc):
            vals = epilogue(acc, *[r[...] for r in extra_refs]) if epilogue is not None else (acc,) * n_out
            assert len(vals) == n_out
            for r, v in zip(out_refs, vals):
                r[...] = v.astype(r.dtype)

        if nk == 1:
            finish(part)
            return
        acc_ref = refs[-1]
        k = pl.program_id(2)

        @pl.when(k == 0)
        def _():
            acc_ref[...] = part

        @pl.when((k > 0) & (k < nk - 1))
        def _():
            acc_ref[...] += part

        @pl.when(k == nk - 1)
        def _():
            finish(acc_ref[...] + part)

    outs = pl.pallas_call(
        body,
        grid=(m // tm, n // tn, nk),
        in_specs=[a_spec, b_spec] + [x_spec] * n_extra,
        out_specs=[o_spec] * n_out,
        out_shape=[jax.ShapeDtypeStruct(out_shape, d) for d in out_dtypes],
        scratch_shapes=[pltpu.VMEM((tm, tn), F32)] if nk > 1 else [],
        compiler_params=_cp(("parallel", "parallel", "arbitrary")),
        name=name,
    )(a, b, *extras)
    return outs[0] if n_out == 1 else outs


def _rms_fwd(x, g, *, name, tm=512):
    t, d = x.shape
    tm = _blk(t, tm)

    def body(x_ref, g_ref, n_ref):
        xv = x_ref[...]
        r = lax.rsqrt(jnp.mean(xv * xv, axis=-1, keepdims=True) + RMS_EPS)
        n_ref[...] = (xv * r * g_ref[...]).astype(n_ref.dtype)

    return pl.pallas_call(
        body, grid=(t // tm,),
        in_specs=[pl.BlockSpec((tm, d), lambda i: (i, 0)), pl.BlockSpec((1, d), lambda i: (0, 0))],
        out_specs=pl.BlockSpec((tm, d), lambda i: (i, 0)),
        out_shape=jax.ShapeDtypeStruct((t, d), BF16),
        compiler_params=_cp(("parallel",)), name=name,
    )(x, g)


def _rms_bwd(dn, x, g, dres, *, name, tm=512):
    t, d = x.shape
    tm = _blk(t, tm)

    def body(dn_ref, x_ref, g_ref, dres_ref, dx_ref, dg_ref):
        i = pl.program_id(0)
        xv = x_ref[...]
        r = lax.rsqrt(jnp.mean(xv * xv, axis=-1, keepdims=True) + RMS_EPS)
        xh = xv * r
        dnv = dn_ref[...].astype(F32)
        dxh = dnv * g_ref[...]
        dx_ref[...] = dres_ref[...] + r * (dxh - xh * jnp.mean(dxh * xh, axis=-1, keepdims=True))
        part = jnp.sum(dnv * xh, axis=0, keepdims=True)

        @pl.when(i == 0)
        def _():
            dg_ref[...] = part

        @pl.when(i > 0)
        def _():
            dg_ref[...] += part

    row = pl.BlockSpec((tm, d), lambda i: (i, 0))
    vec = pl.BlockSpec((1, d), lambda i: (0, 0))
    return pl.pallas_call(
        body, grid=(t // tm,),
        in_specs=[row, row, vec, row], out_specs=[row, vec],
        out_shape=[jax.ShapeDtypeStruct((t, d), F32), jax.ShapeDtypeStruct((1, d), F32)],
        compiler_params=_cp(("arbitrary",)), name=name,
    )(dn, x, g, dres)


def _final_loss(h, g, tgt, *, name, tm=512):
    t, d = h.shape
    tm = _blk(t, tm)

    def body(h_ref, g_ref, t_ref, dh_ref, dg_ref, loss_ref):
        i = pl.program_id(0)
        hv = h_ref[...]
        r = lax.rsqrt(jnp.mean(hv * hv, axis=-1, keepdims=True) + RMS_EPS)
        xh = hv * r
        e = xh * g_ref[...] - t_ref[...]
        dy = e * (1.0 / d)
        dxh = dy * g_ref[...]
        dh_ref[...] = r * (dxh - xh * jnp.mean(dxh * xh, axis=-1, keepdims=True))
        dg_part = jnp.sum(dy * xh, axis=0, keepdims=True)
        row_loss = jnp.sum(e * e, axis=-1, keepdims=True) * (0.5 / d)
        loss_part = jnp.sum(row_loss, axis=0, keepdims=True)

        @pl.when(i == 0)
        def _():
            dg_ref[...] = dg_part
            loss_ref[...] = jnp.broadcast_to(loss_part, loss_ref.shape)

        @pl.when(i > 0)
        def _():
            dg_ref[...] += dg_part
            loss_ref[...] += jnp.broadcast_to(loss_part, loss_ref.shape)

    row = pl.BlockSpec((tm, d), lambda i: (i, 0))
    vec = pl.BlockSpec((1, d), lambda i: (0, 0))
    return pl.pallas_call(
        body, grid=(t // tm,),
        in_specs=[row, vec, row], out_specs=[row, vec, pl.BlockSpec((1, LANES), lambda i: (0, 0))],
        out_shape=[jax.ShapeDtypeStruct((t, d), F32), jax.ShapeDtypeStruct((1, d), F32),
                   jax.ShapeDtypeStruct((1, LANES), F32)],
        compiler_params=_cp(("arbitrary",)), name=name,
    )(h, g, tgt)


def _swa_mask(n):
    w = SWA_WINDOW
    qi = lax.broadcasted_iota(jnp.int32, (w, 2 * w), 0)
    kj = lax.broadcasted_iota(jnp.int32, (w, 2 * w), 1)
    return (kj > qi) & (kj <= qi + w) & ((n > 0) | (kj >= w))


def _swa_fwd(q, kv, sinks, *, name):
    t = q.shape[0]
    w, hd, hq, hkv = SWA_WINDOW, SWA_HEAD_DIM, SWA_Q_HEADS, SWA_KV_HEADS
    grp = hq // hkv
    kvw = hkv * hd
    nb = t // w

    def body(q_ref, kvp_ref, kvc_ref, s_ref, o_ref, lse_ref):
        n = pl.program_id(0)
        mask = _swa_mask(n)
        kvcat = jnp.concatenate([kvp_ref[...], kvc_ref[...]], axis=0)
        outs, lses = [], []
        for h in range(hq):
            hk = h // grp
            qh = q_ref[:, h * hd:(h + 1) * hd]
            kh = kvcat[:, hk * hd:(hk + 1) * hd]
            vh = kvcat[:, kvw + hk * hd:kvw + (hk + 1) * hd]
            s = _dot(qh, kh, NT) * (hd ** -0.5)
            s = jnp.where(mask, s, -jnp.inf)
            sk = s_ref[0:1, h:h + 1]
            m = jnp.maximum(jnp.max(s, axis=-1, keepdims=True), sk)
            p = jnp.exp(s - m)
            den = jnp.sum(p, axis=-1, keepdims=True) + jnp.exp(sk - m)
            pr = (p / den).astype(BF16)
            outs.append(_dot(pr, vh))
            lses.append(m + jnp.log(den))
        o_ref[...] = jnp.concatenate(outs, axis=1).astype(o_ref.dtype)
        lse_ref[...] = jnp.concatenate(lses, axis=1)

    return pl.pallas_call(
        body, grid=(nb,),
        in_specs=[pl.BlockSpec((w, hq * hd), lambda i: (i, 0)),
                  pl.BlockSpec((w, 2 * kvw), lambda i: (jnp.maximum(i - 1, 0), 0)),
                  pl.BlockSpec((w, 2 * kvw), lambda i: (i, 0)),
                  pl.BlockSpec((1, hq), lambda i: (0, 0))],
        out_specs=[pl.BlockSpec((w, hq * hd), lambda i: (i, 0)), pl.BlockSpec((w, hq), lambda i: (i, 0))],
        out_shape=[jax.ShapeDtypeStruct((t, hq * hd), BF16), jax.ShapeDtypeStruct((t, hq), F32)],
        compiler_params=_cp(("parallel",)), name=name,
    )(q, kv, kv, sinks)


def _swa_bwd(q, kv, sinks, o, lse, do, *, name):
    t = q.shape[0]
    w, hd, hq, hkv = SWA_WINDOW, SWA_HEAD_DIM, SWA_Q_HEADS, SWA_KV_HEADS
    grp = hq // hkv
    kvw = hkv * hd
    nb = t // w

    def body(q_ref, kvp_ref, kvc_ref, s_ref, o_ref, lse_ref, do_ref, dq_ref, dkv_ref, ds_ref, carry_ref):
        n = pl.program_id(0)

        @pl.when(n == 0)
        def _():
            ds_ref[...] = jnp.zeros_like(ds_ref)
            carry_ref[...] = jnp.zeros_like(carry_ref)

        @pl.when(n < nb)
        def _():
            mask = _swa_mask(n)
            kvcat = jnp.concatenate([kvp_ref[...], kvc_ref[...]], axis=0)
            dqs, dsk = [], []
            dk_acc = [None] * hkv
            dv_acc = [None] * hkv
            for h in range(hq):
                hk = h // grp
                qh = q_ref[:, h * hd:(h + 1) * hd]
                kh = kvcat[:, hk * hd:(hk + 1) * hd]
                vh = kvcat[:, kvw + hk * hd:kvw + (hk + 1) * hd]
                doh = do_ref[:, h * hd:(h + 1) * hd]
                oh = o_ref[:, h * hd:(h + 1) * hd]
                lse_h = lse_ref[:, h:h + 1]
                s = _dot(qh, kh, NT) * (hd ** -0.5)
                p = jnp.exp(jnp.where(mask, s, -jnp.inf) - lse_h)
                delta = jnp.sum(doh.astype(F32) * oh.astype(F32), axis=-1, keepdims=True)
                dp = _dot(doh, vh, NT)
                ds = (p * (dp - delta) * (hd ** -0.5)).astype(BF16)
                dqs.append(_dot(ds, kh))
                dk_h = _dot(ds, qh, TN)
                dv_h = _dot(p.astype(BF16), doh, TN)
                dk_acc[hk] = dk_h if dk_acc[hk] is None else dk_acc[hk] + dk_h
                dv_acc[hk] = dv_h if dv_acc[hk] is None else dv_acc[hk] + dv_h
                sk = s_ref[0:1, h:h + 1]
                dsk.append(jnp.sum(-jnp.exp(sk - lse_h) * delta, axis=0, keepdims=True))
            dq_ref[...] = jnp.concatenate(dqs, axis=1).astype(dq_ref.dtype)
            ds_ref[...] += jnp.concatenate(dsk, axis=1)
            dkv_cat = jnp.concatenate(dk_acc + dv_acc, axis=1)
            dkv_ref[...] = (carry_ref[...] + dkv_cat[:w]).astype(dkv_ref.dtype)
            carry_ref[...] = dkv_cat[w:]

        @pl.when(n == nb)
        def _():
            dkv_ref[...] = carry_ref[...].astype(dkv_ref.dtype)

    cur = lambda i: (jnp.minimum(i, nb - 1), 0)
    prev = lambda i: (jnp.clip(i - 1, 0, nb - 1), 0)
    return pl.pallas_call(
        body, grid=(nb + 1,),
        in_specs=[pl.BlockSpec((w, hq * hd), cur), pl.BlockSpec((w, 2 * kvw), prev), pl.BlockSpec((w, 2 * kvw), cur),
                  pl.BlockSpec((1, hq), lambda i: (0, 0)), pl.BlockSpec((w, hq * hd), cur),
                  pl.BlockSpec((w, hq), cur), pl.BlockSpec((w, hq * hd), cur)],
        out_specs=[pl.BlockSpec((w, hq * hd), cur), pl.BlockSpec((w, 2 * kvw), prev),
                   pl.BlockSpec((1, hq), lambda i: (0, 0))],
        out_shape=[jax.ShapeDtypeStruct((t, hq * hd), BF16), jax.ShapeDtypeStruct((t, 2 * kvw), BF16),
                   jax.ShapeDtypeStruct((1, hq), F32)],
        scratch_shapes=[pltpu.VMEM((w, 2 * kvw), F32)],
        compiler_params=_cp(("arbitrary",)), name=name,
    )(q, kv, kv, sinks, o, lse, do)


def _xa_fwd(q, mkv, *, name, tq=512):
    t, xw = q.shape
    nm = mkv.shape[0]
    hd, nh = XA_HEAD_DIM, XA_HEADS
    tq = _blk(t, tq)

    def body(q_ref, mkv_ref, o_ref):
        outs = []
        for h in range(nh):
            qh = q_ref[:, h * hd:(h + 1) * hd]
            kh = mkv_ref[:, h * hd:(h + 1) * hd]
            vh = mkv_ref[:, xw + h * hd:xw + (h + 1) * hd]
            s = _dot(qh, kh, NT) * (hd ** -0.5)
            p = jnp.exp(s - jnp.max(s, axis=-1, keepdims=True))
            p = p / jnp.sum(p, axis=-1, keepdims=True)
            outs.append(_dot(p.astype(BF16), vh))
        o_ref[...] = jnp.concatenate(outs, axis=1).astype(o_ref.dtype)

    return pl.pallas_call(
        body, grid=(t // tq,),
        in_specs=[pl.BlockSpec((tq, xw), lambda i: (i, 0)), pl.BlockSpec((nm, 2 * xw), lambda i: (0, 0))],
        out_specs=pl.BlockSpec((tq, xw), lambda i: (i, 0)),
        out_shape=jax.ShapeDtypeStruct((t, xw), BF16),
        compiler_params=_cp(("parallel",)), name=name,
    )(q, mkv)


def _xa_bwd(q, mkv, do, *, name, tq=512):
    t, xw = q.shape
    nm = mkv.shape[0]
    hd, nh = XA_HEAD_DIM, XA_HEADS
    tq = _blk(t, tq)

    def body(q_ref, mkv_ref, do_ref, dq_ref, dmkv_ref):
        i = pl.program_id(0)
        dqs, dks, dvs = [], [], []
        for h in range(nh):
            qh = q_ref[:, h * hd:(h + 1) * hd]
            kh = mkv_ref[:, h * hd:(h + 1) * hd]
            vh = mkv_ref[:, xw + h * hd:xw + (h + 1) * hd]
            doh = do_ref[:, h * hd:(h + 1) * hd]
            s = _dot(qh, kh, NT) * (hd ** -0.5)
            p = jnp.exp(s - jnp.max(s, axis=-1, keepdims=True))
            p = p / jnp.sum(p, axis=-1, keepdims=True)
            dp = _dot(doh, vh, NT)
            ds = (p * (dp - jnp.sum(p * dp, axis=-1, keepdims=True)) * (hd ** -0.5)).astype(BF16)
            dqs.append(_dot(ds, kh))
            dks.append(_dot(ds, qh, TN))
            dvs.append(_dot(p.astype(BF16), doh, TN))
        dq_ref[...] = jnp.concatenate(dqs, axis=1).astype(dq_ref.dtype)
        part = jnp.concatenate(dks + dvs, axis=1)

        @pl.when(i == 0)
        def _():
            dmkv_ref[...] = part

        @pl.when(i > 0)
        def _():
            dmkv_ref[...] += part

    row = pl.BlockSpec((tq, xw), lambda i: (i, 0))
    full = pl.BlockSpec((nm, 2 * xw), lambda i: (0, 0))
    return pl.pallas_call(
        body, grid=(t // tq,),
        in_specs=[row, full, row], out_specs=[row, full],
        out_shape=[jax.ShapeDtypeStruct((t, xw), BF16), jax.ShapeDtypeStruct((nm, 2 * xw), F32)],
        compiler_params=_cp(("arbitrary",)), name=name,
    )(q, mkv, do)


def _merge_specs(t, d, ys, ws, tm, tn):
    nj = d // tn
    wpb = ws[0].shape[2] // tn
    y_specs = [pl.BlockSpec((tm, y.shape[1]), lambda i, j: (i, 0)) for y in ys]
    w_specs = [pl.BlockSpec((None, w.shape[1], tn), lambda i, j: (j // wpb, 0, j % wpb)) for w in ws]
    g_specs = [pl.BlockSpec((tm, tn), functools.partial(lambda i, j, b: (i, j + b * nj), b=b)) for b in range(3)]
    return y_specs, w_specs, g_specs


def _merge_fwd(ys, ws, gates, *, name, tm=512, tn=512):
    t, d = ys[0].shape[0], ws[0].shape[0] * ws[0].shape[2]
    tm, tn = _blk(t, tm), _blk(ws[0].shape[2], tn)
    y_specs, w_specs, g_specs = _merge_specs(t, d, ys, ws, tm, tn)

    def body(ya, yb, yc, wa, wb, wc, ga, gb, gc, o_ref):
        acc = None
        for y, w, g in ((ya, wa, ga), (yb, wb, gb), (yc, wc, gc)):
            term = _sigmoid(g[...]) * _dot(y[...], w[...])
            acc = term if acc is None else acc + term
        o_ref[...] = acc.astype(o_ref.dtype)

    return pl.pallas_call(
        body, grid=(t // tm, d // tn),
        in_specs=y_specs + w_specs + g_specs,
        out_specs=pl.BlockSpec((tm, tn), lambda i, j: (i, j)),
        out_shape=jax.ShapeDtypeStruct((t, d), BF16),
        compiler_params=_cp(("parallel", "parallel")), name=name,
    )(*ys, *ws, gates, gates, gates)


def _merge_bwd(ys, ws, gates, dmerged, *, name, tm=512, tn=512):
    t, d = ys[0].shape[0], ws[0].shape[0] * ws[0].shape[2]
    tm, tn = _blk(t, tm), _blk(ws[0].shape[2], tn)
    y_specs, w_specs, g_specs = _merge_specs(t, d, ys, ws, tm, tn)
    tile = pl.BlockSpec((tm, tn), lambda i, j: (i, j))

    def body(ya, yb, yc, wa, wb, wc, ga, gb, gc, dm_ref, dua, dub, duc, dga, dgb, dgc):
        dm = dm_ref[...]
        for y, w, g, du, dg in ((ya, wa, ga, dua, dga), (yb, wb, gb, dub, dgb), (yc, wc, gc, duc, dgc)):
            sg = _sigmoid(g[...])
            u = _dot(y[...], w[...])
            du[...] = (dm * sg).astype(du.dtype)
            dg[...] = (dm * u * sg * (1.0 - sg)).astype(dg.dtype)

    return pl.pallas_call(
        body, grid=(t // tm, d // tn),
        in_specs=y_specs + w_specs + g_specs + [tile],
        out_specs=[tile] * 6,
        out_shape=[jax.ShapeDtypeStruct((t, d), BF16)] * 6,
        compiler_params=_cp(("parallel", "parallel")), name=name,
    )(*ys, *ws, gates, gates, gates, dmerged)


def _adamw(w, g, m, v, *, name, tm=256):
    r, c = w.shape
    tm = _blk(r, tm) if r % 8 == 0 else r
    bc1 = 1.0 - ADAM_B1 ** ADAM_STEP
    bc2 = 1.0 - ADAM_B2 ** ADAM_STEP

    def body(w_ref, g_ref, m_ref, v_ref, d_ref, nm_ref, nv_ref):
        gv = g_ref[...]
        nm = ADAM_B1 * m_ref[...] + (1.0 - ADAM_B1) * gv
        nv = ADAM_B2 * v_ref[...] + (1.0 - ADAM_B2) * (gv * gv)
        d_ref[...] = -ADAM_LR * ((nm / bc1) / (jnp.sqrt(nv / bc2) + ADAM_EPS) + ADAM_WD * w_ref[...])
        nm_ref[...] = nm
        nv_ref[...] = nv

    spec = pl.BlockSpec((tm, c), lambda i: (i, 0))
    return pl.pallas_call(
        body, grid=(r // tm,), in_specs=[spec] * 4, out_specs=[spec] * 3,
        out_shape=[jax.ShapeDtypeStruct((r, c), F32)] * 3,
        compiler_params=_cp(("parallel",)), name=name,
    )(w, g, m, v)


HALO = 8


def _shift_down(cur, prev, j):
    if j == 0:
        return cur
    y = pltpu.roll(cur, j, 0)
    row = lax.broadcasted_iota(jnp.int32, (HALO, cur.shape[1]), 0)
    top = jnp.where(row < j, pltpu.roll(prev, j, 0), y[:HALO])
    return jnp.concatenate([top, y[HALO:]], axis=0)


def _shift_up(cur, nxt, j):
    if j == 0:
        return cur
    tm = cur.shape[0]
    y = pltpu.roll(cur, tm - j, 0)
    row = lax.broadcasted_iota(jnp.int32, (HALO, cur.shape[1]), 0)
    bot = jnp.where(row >= HALO - j, pltpu.roll(nxt, HALO - j, 0), y[tm - HALO:])
    return jnp.concatenate([y[:tm - HALO], bot], axis=0)


def _softplus(x):
    return jnp.maximum(x, 0.0) + jnp.log(1.0 + jnp.exp(-jnp.abs(x)))


def _gdn_pre_fwd(qkvb, conv_w, ab, alog_pad, dt_pad, *, name, ab_blk=0, tm=256):
    t, cw = qkvb.shape
    hd, nh, ck = GDN_HEAD_DIM, GDN_HEADS, GDN_CHUNK
    gw = nh * hd
    tm = _blk(t, tm)
    hb = tm // HALO

    def body(x_ref, xp_ref, w_ref, ab_ref, al_ref, dt_ref, xc_ref, qkvn_ref, aux_ref):
        i = pl.program_id(0)
        cur = x_ref[...]
        prev = jnp.where(i > 0, xp_ref[...], 0.0)
        xc = None
        for tap in range(GDN_CONV):
            term = w_ref[tap:tap + 1, :] * _shift_down(cur, prev, GDN_CONV - 1 - tap)
            xc = term if xc is None else xc + term
        xc_ref[...] = xc
        s = xc * _sigmoid(xc)
        for h in range(2 * nh):
            xh = s[:, h * hd:(h + 1) * hd]
            r = lax.rsqrt(jnp.sum(xh * xh, axis=-1, keepdims=True) + L2_EPS)
            scale = hd ** -0.5 if h < nh else 1.0
            qkvn_ref[:, h * hd:(h + 1) * hd] = xh * (r * scale)
        qkvn_ref[:, 2 * gw:] = s[:, 2 * gw:]
        abv = ab_ref[...]
        lane = lax.broadcasted_iota(jnp.int32, abv.shape, 1)
        g = jnp.where(lane < nh, -jnp.exp(al_ref[...]) * _softplus(abv + dt_ref[...]), 0.0)
        beta = jnp.where((lane >= nh) & (lane < 2 * nh), _sigmoid(abv), 0.0)
        ii = lax.broadcasted_iota(jnp.int32, (tm, tm), 0)
        jj = lax.broadcasted_iota(jnp.int32, (tm, tm), 1)
        tri = jnp.where((ii >= jj) & ((ii ^ jj) < ck), 1.0, 0.0)
        gcum = _dot(tri, g, precision=HI)
        aux_ref[...] = g + beta + pltpu.roll(gcum, 2 * nh, 1)

    row = lambda c: pl.BlockSpec((tm, c), lambda i: (i, 0))
    vec = lambda r, c: pl.BlockSpec((r, c), lambda i: (0, 0))
    return pl.pallas_call(
        body, grid=(t // tm,),
        in_specs=[row(cw), pl.BlockSpec((HALO, cw), lambda i: (jnp.maximum(i * hb - 1, 0), 0)), vec(GDN_CONV, cw),
                  pl.BlockSpec((tm, LANES), lambda i: (i, ab_blk)), vec(1, LANES), vec(1, LANES)],
        out_specs=[row(cw), row(cw), row(LANES)],
        out_shape=[jax.ShapeDtypeStruct((t, cw), F32), jax.ShapeDtypeStruct((t, cw), F32),
                   jax.ShapeDtypeStruct((t, LANES), F32)],
        compiler_params=_cp(("parallel",)), name=name,
    )(qkvb, qkvb, conv_w, ab, alog_pad, dt_pad)


GDN_STEP_CHUNKS = 4


def _bdot(a, b, dims=NN):
    return _dot(a.astype(BF16), b.astype(BF16), dims)


def _gdn_local(q, k, b, gc, gc_row):
    ck = GDN_CHUNK
    ii = lax.broadcasted_iota(jnp.int32, (ck, ck), 0)
    jj = lax.broadcasted_iota(jnp.int32, (ck, ck), 1)
    lower, strict = ii >= jj, ii > jj
    dmat = jnp.exp(jnp.where(lower, gc - gc_row, -jnp.inf))
    kk = _bdot(k, k, NT)
    lmat = jnp.where(strict, b * kk * dmat, 0.0)
    tinv = jnp.where(ii == jj, 1.0, 0.0) - lmat
    pw = lmat
    for _ in range(int(math.log2(ck)) - 1):
        pw = _dot(pw, pw, precision=HI)
        tinv = tinv + _dot(tinv, pw, precision=HI)
    gam = jnp.exp(gc)
    qk = _bdot(q, k, NT)
    gl = gc[ck - 1:ck, :]
    return dict(lower=lower, strict=strict, dmat=dmat, kk=kk, tinv=tinv, gam=gam, qk=qk, mm=qk * dmat,
                kdec=jnp.exp(gl - gc))


def _gdn_head_cols(h):
    return slice(h * GDN_HEAD_DIM, (h + 1) * GDN_HEAD_DIM)


def _gdn_chunk_inputs(x_ref, aux_ref, auxt_ref, g, h):
    nh, ck = GDN_HEADS, GDN_CHUNK
    gw = nh * GDN_HEAD_DIM
    rows = slice(g * ck, (g + 1) * ck)
    cols = _gdn_head_cols(h)
    q = x_ref[rows, cols]
    k = x_ref[rows, gw + cols.start:gw + cols.stop]
    v = x_ref[rows, 2 * gw + cols.start:2 * gw + cols.stop]
    b = aux_ref[rows, nh + h:nh + h + 1]
    gc = aux_ref[rows, 2 * nh + h:2 * nh + h + 1]
    gc_row = auxt_ref[g, 2 * nh + h:2 * nh + h + 1, :]
    return q, k, v, b, gc, gc_row


def _gdn_specs(t, widths, *, reverse=False, step_chunks=None):
    rows = (step_chunks or GDN_STEP_CHUNKS) * GDN_CHUNK
    nsteps = t // rows
    idx = (lambda i: (nsteps - 1 - i, 0)) if reverse else (lambda i: (i, 0))
    return [pl.BlockSpec((rows, w), idx) for w in widths]


def _gdn_local_fwd(qkvn, aux, aux_t, *, name):
    t = qkvn.shape[0]
    hd, nh, ck, gs = GDN_HEAD_DIM, GDN_HEADS, GDN_CHUNK, GDN_STEP_CHUNKS
    gw = nh * hd

    def body(x_ref, aux_ref, auxt_ref, u_ref, w_ref, qd_ref, kd_ref, mm_ref, tinv_ref):
        for g in range(gs):
            rows = slice(g * ck, (g + 1) * ck)
            mms, tinvs = [], []
            for h in range(nh):
                q, k, v, b, gc, gc_row = _gdn_chunk_inputs(x_ref, aux_ref, auxt_ref, g, h)
                lc = _gdn_local(q, k, b, gc, gc_row)
                cols = _gdn_head_cols(h)
                u_ref[rows, cols] = _dot(lc["tinv"], b * v, precision=HI)
                w_ref[rows, cols] = _dot(lc["tinv"], (b * lc["gam"]) * k, precision=HI).astype(w_ref.dtype)
                qd_ref[rows, cols] = (lc["gam"] * q).astype(qd_ref.dtype)
                kd_ref[rows, cols] = (lc["kdec"] * k).astype(kd_ref.dtype)
                mms.append(lc["mm"])
                tinvs.append(lc["tinv"])
            mm_ref[rows, :] = jnp.concatenate(mms, axis=1).astype(mm_ref.dtype)
            tinv_ref[rows, :] = jnp.concatenate(tinvs, axis=1)

    sq = nh * ck
    return pl.pallas_call(
        body, grid=(t // (gs * ck),),
        in_specs=_gdn_specs(t, (3 * gw, LANES)) + [pl.BlockSpec((gs, 16, ck), lambda i: (i, 0, 0))],
        out_specs=_gdn_specs(t, (gw, gw, gw, gw, sq, sq)),
        out_shape=[jax.ShapeDtypeStruct((t, gw), F32)] + [jax.ShapeDtypeStruct((t, gw), BF16)] * 3
        + [jax.ShapeDtypeStruct((t, sq), BF16), jax.ShapeDtypeStruct((t, sq), F32)],
        compiler_params=_cp(("parallel",)), name=name,
    )(qkvn, aux, aux_t)


def _gdn_seq_fwd(u, w, qd, kd, mm, aux, *, name):
    t = u.shape[0]
    hd, nh, ck, gs = GDN_HEAD_DIM, GDN_HEADS, GDN_CHUNK, GDN_STEP_CHUNKS
    gw = nh * hd
    sq = nh * ck

    def body(u_ref, w_ref, qd_ref, kd_ref, mm_ref, aux_ref, o_ref, vn_ref, sall_ref, s_ref):
        @pl.when(pl.program_id(0) == 0)
        def _():
            s_ref[...] = jnp.zeros_like(s_ref)

        for g in range(gs):
            rows = slice(g * ck, (g + 1) * ck)
            last = (g + 1) * ck - 1
            for h in range(nh):
                cols = _gdn_head_cols(h)
                st = s_ref[h]
                sall_ref[g, h] = st
                stb = st.astype(BF16)
                vn = u_ref[rows, cols] - _dot(w_ref[rows, cols], stb)
                vnb = vn.astype(BF16)
                vn_ref[rows, cols] = vnb
                o_ref[rows, cols] = _dot(qd_ref[rows, cols], stb) + _dot(mm_ref[rows, h * ck:(h + 1) * ck], vnb)
                gam_c = jnp.exp(aux_ref[last:last + 1, 2 * nh + h:2 * nh + h + 1])
                s_ref[h] = gam_c * st + _dot(kd_ref[rows, cols], vnb, TN)

    return pl.pallas_call(
        body, grid=(t // (gs * ck),),
        in_specs=_gdn_specs(t, (gw, gw, gw, gw, sq, LANES)),
        out_specs=_gdn_specs(t, (gw, gw)) + [pl.BlockSpec((gs, nh, hd, hd), lambda i: (i, 0, 0, 0))],
        out_shape=[jax.ShapeDtypeStruct((t, gw), F32), jax.ShapeDtypeStruct((t, gw), BF16),
                   jax.ShapeDtypeStruct((t // ck, nh, hd, hd), F32)],
        scratch_shapes=[pltpu.VMEM((nh, hd, hd), F32)],
        compiler_params=_cp(("arbitrary",)), name=name,
    )(u, w, qd, kd, mm, aux)


def _gdn_seq_bwd(do, w, qd, kd, mm, vn, s_all, aux, *, name):
    t = do.shape[0]
    hd, nh, ck, gs = GDN_HEAD_DIM, GDN_HEADS, GDN_CHUNK, GDN_STEP_CHUNKS
    gw = nh * hd
    sq = nh * ck
    nsteps = t // (gs * ck)

    def body(do_ref, w_ref, qd_ref, kd_ref, mm_ref, vn_ref, sall_ref, aux_ref, dvn_ref, dqd_ref, dkd_ref, dw_ref,
             dlast_ref, ds_ref):
        @pl.when(pl.program_id(0) == 0)
        def _():
            ds_ref[...] = jnp.zeros_like(ds_ref)

        lane = lax.broadcasted_iota(jnp.int32, (ck, LANES), 1)
        rowi = lax.broadcasted_iota(jnp.int32, (ck, LANES), 0)
        for g in reversed(range(gs)):
            rows = slice(g * ck, (g + 1) * ck)
            last = (g + 1) * ck - 1
            dlast = jnp.zeros((ck, LANES), F32)
            for h in range(nh):
                cols = _gdn_head_cols(h)
                st = sall_ref[g, h]
                stb = st.astype(BF16)
                dsn = ds_ref[h]
                dsb = dsn.astype(BF16)
                dob = do_ref[rows, cols].astype(BF16)
                dvn = _dot(mm_ref[rows, h * ck:(h + 1) * ck], dob, TN) + _dot(kd_ref[rows, cols], dsb)
                dvb = dvn.astype(BF16)
                dvn_ref[rows, cols] = dvn
                dqd_ref[rows, cols] = _dot(dob, stb, NT)
                dkd_ref[rows, cols] = _dot(vn_ref[rows, cols], dsb, NT)
                dw_ref[rows, cols] = -_dot(dvb, stb, NT)
                gam_c = jnp.exp(aux_ref[last:last + 1, 2 * nh + h:2 * nh + h + 1])
                dgam_c = jnp.sum(jnp.sum(dsn * st, axis=1, keepdims=True), axis=0, keepdims=True)
                dlast = dlast + jnp.where((rowi == ck - 1) & (lane == h), gam_c * dgam_c, 0.0)
                ds_ref[h] = _dot(qd_ref[rows, cols], dob, TN) + gam_c * dsn - _dot(w_ref[rows, cols], dvb, TN)
            dlast_ref[rows, :] = dlast

    return pl.pallas_call(
        body, grid=(nsteps,),
        in_specs=_gdn_specs(t, (gw, gw, gw, gw, sq, gw), reverse=True)
        + [pl.BlockSpec((gs, nh, hd, hd), lambda i: (nsteps - 1 - i, 0, 0, 0))] + _gdn_specs(t, (LANES,), reverse=True),
        out_specs=_gdn_specs(t, (gw, gw, gw, gw, LANES), reverse=True),
        out_shape=[jax.ShapeDtypeStruct((t, gw), F32)] * 4 + [jax.ShapeDtypeStruct((t, LANES), F32)],
        scratch_shapes=[pltpu.VMEM((nh, hd, hd), F32)],
        compiler_params=_cp(("arbitrary",)), name=name,
    )(do, w, qd, kd, mm, vn, s_all, aux)


def _gdn_local_bwd(qkvn, aux, aux_t, tinv, u, w, vn, do, dvn, dqd, dkd, dw, dlast, *, name):
    t = qkvn.shape[0]
    hd, nh, ck, gs = GDN_HEAD_DIM, GDN_HEADS, GDN_CHUNK, GDN_STEP_CHUNKS
    gw = nh * hd
    sq = nh * ck

    def body(x_ref, aux_ref, auxt_ref, tinv_ref, u_ref, w_ref, vn_ref, do_ref, dvn_ref, dqd_ref, dkd_ref, dw_ref,
             dlast_ref, dx_ref, daux_ref):
        lane = lax.broadcasted_iota(jnp.int32, (ck, LANES), 1)
        ones = jnp.ones((ck, LANES), F32)
        ii = lax.broadcasted_iota(jnp.int32, (ck, ck), 0)
        jj = lax.broadcasted_iota(jnp.int32, (ck, ck), 1)
        suffix = jnp.where(jj >= ii, 1.0, 0.0)
        for g in range(gs):
            rows = slice(g * ck, (g + 1) * ck)
            dgc_all = dlast_ref[rows, :]
            db_all = jnp.zeros((ck, LANES), F32)
            for h in range(nh):
                cols = _gdn_head_cols(h)
                q, k, v, b, gc, gc_row = _gdn_chunk_inputs(x_ref, aux_ref, auxt_ref, g, h)
                lc = _gdn_local(q, k, b, gc, gc_row)
                dmat, kk, gam, qk, kdec = (lc[key] for key in ("dmat", "kk", "gam", "qk", "kdec"))
                tinv_h = tinv_ref[rows, h * ck:(h + 1) * ck]
                u_h, w_h, vn_h = u_ref[rows, cols], w_ref[rows, cols], vn_ref[rows, cols]
                dqd_h, dkd_h = dqd_ref[rows, cols], dkd_ref[rows, cols]
                dm = jnp.where(lc["lower"], _bdot(do_ref[rows, cols], vn_h, NT), 0.0)
                drv = _dot(tinv_h, dvn_ref[rows, cols], TN, HI)
                drk = _dot(tinv_h, dw_ref[rows, cols], TN, HI)
                da = jnp.where(lc["strict"], -(_bdot(drv, u_h, NT) + _bdot(drk, w_h, NT)), 0.0)
                rs_rk = jnp.sum(drk * k, axis=-1, keepdims=True)
                db = (jnp.sum(drv * v, axis=-1, keepdims=True) + gam * rs_rk
                      + jnp.sum(da * kk * dmat, axis=-1, keepdims=True))
                e_mat = da * dmat * b
                dmd = dm * dmat
                dx_ref[rows, cols] = _bdot(dmd, k) + gam * dqd_h
                dx_ref[rows, gw + cols.start:gw + cols.stop] = (
                    (b * gam) * drk + _bdot(e_mat, k) + _bdot(e_mat, k, TN) + _bdot(dmd, q, TN) + kdec * dkd_h)
                dx_ref[rows, 2 * gw + cols.start:2 * gw + cols.stop] = b * drv
                f_mat = da * (b * kk) * dmat + dm * qk * dmat
                e_vec = jnp.sum(dkd_h * (kdec * k), axis=-1, keepdims=True)
                dgc = (b * gam * rs_rk + gam * jnp.sum(dqd_h * q, axis=-1, keepdims=True)
                       + jnp.sum(f_mat, axis=-1, keepdims=True) - _dot(f_mat, ones, TN, HI)[:, 0:1] - e_vec)
                is_last = lax.broadcasted_iota(jnp.int32, (ck, 1), 0) == ck - 1
                dgc = dgc + jnp.where(is_last, jnp.sum(e_vec, axis=0, keepdims=True), 0.0)
                dgc_all = dgc_all + jnp.where(lane == h, dgc, 0.0)
                db_all = db_all + jnp.where(lane == nh + h, db, 0.0)
            daux_ref[rows, :] = _dot(suffix, dgc_all, precision=HI) + db_all

    return pl.pallas_call(
        body, grid=(t // (gs * ck),),
        in_specs=_gdn_specs(t, (3 * gw, LANES)) + [pl.BlockSpec((gs, 16, ck), lambda i: (i, 0, 0))]
        + _gdn_specs(t, (sq, gw, gw, gw, gw, gw, gw, gw, gw, LANES)),
        out_specs=_gdn_specs(t, (3 * gw, LANES)),
        out_shape=[jax.ShapeDtypeStruct((t, 3 * gw), F32), jax.ShapeDtypeStruct((t, LANES), F32)],
        compiler_params=_cp(("parallel",)), name=name,
    )(qkvn, aux, aux_t, tinv, u, w, vn, do, dvn, dqd, dkd, dw, dlast)


def _gdn_pre_bwd1(xc, dqkvn, daux, ab, alog_pad, dt_pad, *, name, ab_blk=0, tm=256):
    t, cw = xc.shape
    hd, nh = GDN_HEAD_DIM, GDN_HEADS
    gw = nh * hd
    tm = _blk(t, tm)

    def body(xc_ref, dy_ref, daux_ref, ab_ref, al_ref, dt_ref, dxc_ref, dab_ref, dal_ref, ddt_ref):
        i = pl.program_id(0)
        xc = xc_ref[...]
        sg = _sigmoid(xc)
        s = xc * sg
        dsilu = sg * (1.0 + xc * (1.0 - sg))
        for h in range(2 * nh):
            xh = s[:, h * hd:(h + 1) * hd]
            scale = hd ** -0.5 if h < nh else 1.0
            dyh = dy_ref[:, h * hd:(h + 1) * hd] * scale
            r = lax.rsqrt(jnp.sum(xh * xh, axis=-1, keepdims=True) + L2_EPS)
            dxh = r * dyh - xh * (r * r * r) * jnp.sum(dyh * xh, axis=-1, keepdims=True)
            dxc_ref[:, h * hd:(h + 1) * hd] = dxh * dsilu[:, h * hd:(h + 1) * hd]
        dxc_ref[:, 2 * gw:] = dy_ref[:, 2 * gw:] * dsilu[:, 2 * gw:]
        abv = ab_ref[...]
        dauxv = daux_ref[...]
        lane = lax.broadcasted_iota(jnp.int32, abv.shape, 1)
        is_a = lane < nh
        is_b = (lane >= nh) & (lane < 2 * nh)
        pre = abv + dt_ref[...]
        neg_ea = -jnp.exp(al_ref[...])
        d_a = jnp.where(is_a, dauxv * neg_ea * _sigmoid(pre), 0.0)
        beta = _sigmoid(abv)
        d_b = jnp.where(is_b, dauxv * beta * (1.0 - beta), 0.0)
        dab_ref[...] = (d_a + d_b).astype(dab_ref.dtype)
        dal = jnp.sum(jnp.where(is_a, dauxv * neg_ea * _softplus(pre), 0.0), axis=0, keepdims=True)
        ddt = jnp.sum(d_a, axis=0, keepdims=True)

        @pl.when(i == 0)
        def _():
            dal_ref[...] = dal
            ddt_ref[...] = ddt

        @pl.when(i > 0)
        def _():
            dal_ref[...] += dal
            ddt_ref[...] += ddt

    row = lambda c: pl.BlockSpec((tm, c), lambda i: (i, 0))
    vec = pl.BlockSpec((1, LANES), lambda i: (0, 0))
    return pl.pallas_call(
        body, grid=(t // tm,),
        in_specs=[row(cw), row(cw), row(LANES), pl.BlockSpec((tm, LANES), lambda i: (i, ab_blk)), vec, vec],
        out_specs=[row(cw), row(LANES), vec, vec],
        out_shape=[jax.ShapeDtypeStruct((t, cw), F32), jax.ShapeDtypeStruct((t, LANES), BF16),
                   jax.ShapeDtypeStruct((1, LANES), F32), jax.ShapeDtypeStruct((1, LANES), F32)],
        compiler_params=_cp(("arbitrary",)), name=name,
    )(xc, dqkvn, daux, ab, alog_pad, dt_pad)


def _gdn_pre_bwd2(dxc, qkvb, conv_w, *, name, tm=256):
    t, cw = dxc.shape
    tm = _blk(t, tm)
    hb = tm // HALO
    nblk = t // tm

    def body(d_ref, dn_ref, x_ref, xp_ref, w_ref, dx_ref, dw_ref):
        i = pl.program_id(0)
        dcur = d_ref[...]
        dnxt = jnp.where(i < nblk - 1, dn_ref[...], 0.0)
        cur = x_ref[...]
        prev = jnp.where(i > 0, xp_ref[...], 0.0)
        dx = None
        dws = []
        for tap in range(GDN_CONV):
            j = GDN_CONV - 1 - tap
            term = w_ref[tap:tap + 1, :] * _shift_up(dcur, dnxt, j)
            dx = term if dx is None else dx + term
            dws.append(jnp.sum(dcur * _shift_down(cur, prev, j), axis=0, keepdims=True))
        dx_ref[...] = dx.astype(dx_ref.dtype)
        dw = jnp.concatenate(dws, axis=0)

        @pl.when(i == 0)
        def _():
            dw_ref[...] = dw

        @pl.when(i > 0)
        def _():
            dw_ref[...] += dw

    row = pl.BlockSpec((tm, cw), lambda i: (i, 0))
    wsp = pl.BlockSpec((GDN_CONV, cw), lambda i: (0, 0))
    return pl.pallas_call(
        body, grid=(nblk,),
        in_specs=[row, pl.BlockSpec((HALO, cw), lambda i: (jnp.minimum((i + 1) * hb, t // HALO - 1), 0)),
                  row, pl.BlockSpec((HALO, cw), lambda i: (jnp.maximum(i * hb - 1, 0), 0)), wsp],
        out_specs=[row, wsp],
        out_shape=[jax.ShapeDtypeStruct((t, cw), BF16), jax.ShapeDtypeStruct((GDN_CONV, cw), F32)],
        compiler_params=_cp(("arbitrary",)), name=name,
    )(dxc, dxc, qkvb, qkvb, conv_w)


def _gdn_post_fwd(o, z, norm_w, *, name, tm=512):
    t, gw = o.shape
    hd, nh = GDN_HEAD_DIM, GDN_HEADS
    tm = _blk(t, tm)

    def body(o_ref, z_ref, w_ref, y_ref):
        zv = z_ref[...]
        sz = zv * _sigmoid(zv)
        for h in range(nh):
            oh = o_ref[:, h * hd:(h + 1) * hd]
            r = lax.rsqrt(jnp.mean(oh * oh, axis=-1, keepdims=True) + RMS_EPS)
            y_ref[:, h * hd:(h + 1) * hd] = (oh * r * w_ref[...] * sz[:, h * hd:(h + 1) * hd]).astype(y_ref.dtype)

    row = pl.BlockSpec((tm, gw), lambda i: (i, 0))
    return pl.pallas_call(
        body, grid=(t // tm,), in_specs=[row, row, pl.BlockSpec((1, hd), lambda i: (0, 0))], out_specs=row,
        out_shape=jax.ShapeDtypeStruct((t, gw), BF16), compiler_params=_cp(("parallel",)), name=name,
    )(o, z, norm_w)


def _gdn_post_bwd(dy, o, z, norm_w, *, name, tm=512):
    t, gw = o.shape
    hd, nh = GDN_HEAD_DIM, GDN_HEADS
    tm = _blk(t, tm)

    def body(dy_ref, o_ref, z_ref, w_ref, do_ref, dz_ref, dw_ref):
        i = pl.program_id(0)
        zv = z_ref[...]
        sg = _sigmoid(zv)
        sz = zv * sg
        dsz = sg * (1.0 + zv * (1.0 - sg))
        dw = None
        for h in range(nh):
            sl = slice(h * hd, (h + 1) * hd)
            oh = o_ref[:, sl]
            dyh = dy_ref[:, sl].astype(F32)
            r = lax.rsqrt(jnp.mean(oh * oh, axis=-1, keepdims=True) + RMS_EPS)
            xh = oh * r
            dz_ref[:, sl] = (dyh * xh * w_ref[...] * dsz[:, sl]).astype(dz_ref.dtype)
            dn = dyh * sz[:, sl]
            dxh = dn * w_ref[...]
            do_ref[:, sl] = r * (dxh - xh * jnp.mean(dxh * xh, axis=-1, keepdims=True))
            part = jnp.sum(dn * xh, axis=0, keepdims=True)
            dw = part if dw is None else dw + part

        @pl.when(i == 0)
        def _():
            dw_ref[...] = dw

        @pl.when(i > 0)
        def _():
            dw_ref[...] += dw

    row = pl.BlockSpec((tm, gw), lambda i: (i, 0))
    vec = pl.BlockSpec((1, hd), lambda i: (0, 0))
    return pl.pallas_call(
        body, grid=(t // tm,), in_specs=[row, row, row, vec], out_specs=[row, row, vec],
        out_shape=[jax.ShapeDtypeStruct((t, gw), F32), jax.ShapeDtypeStruct((t, gw), BF16),
                   jax.ShapeDtypeStruct((1, hd), F32)],
        compiler_params=_cp(("arbitrary",)), name=name,
    )(dy, o, z, norm_w)


IN_NAMES = ("q_a", "kv_a", "qkv_b", "ab", "z", "q_c", "gates")
CAT_NAMES = ("gates", "q_a", "qkv_b", "z", "q_c", "kv_a", "ab")
AB_PAD = 256


def _in_widths(d):
    gw = GDN_HEADS * GDN_HEAD_DIM
    return dict(q_a=SWA_Q_HEADS * SWA_HEAD_DIM, kv_a=2 * SWA_KV_HEADS * SWA_HEAD_DIM, qkv_b=3 * gw, ab=2 * GDN_HEADS,
                z=gw, q_c=XA_HEADS * XA_HEAD_DIM, gates=3 * d)


def _ranges(names, widths):
    out, start = {}, 0
    for k in names:
        out[k] = (start, widths[k])
        start += widths[k]
    return out, start


def _cat_ranges(d):
    widths = dict(_in_widths(d), ab=AB_PAD)
    return _ranges(CAT_NAMES, widths)


def _to_cat(w_in):
    d = w_in.shape[0]
    src, _ = _ranges(IN_NAMES, _in_widths(d))
    cols = []
    for k in CAT_NAMES:
        s, w = src[k]
        cols.append(w_in[:, s:s + w])
    cols.append(jnp.zeros((d, AB_PAD - src["ab"][1]), w_in.dtype))
    return jnp.concatenate(cols, axis=1)


def _from_cat(w_cat):
    d = w_cat.shape[0]
    widths = _in_widths(d)
    cat, _ = _cat_ranges(d)
    return jnp.concatenate([w_cat[:, cat[k][0]:cat[k][0] + widths[k]] for k in IN_NAMES], axis=1)


def _pad_cols(a, width):
    return jnp.pad(a, ((0, 0), (0, width - a.shape[1])))


def _relu2_epilogue(acc):
    r = jnp.maximum(acc, 0.0)
    return acc, r * r


def _add_epilogue(acc, res):
    return (acc + res,)


def _drelu2_epilogue(acc, u):
    return (acc * (2.0 * jnp.maximum(u.astype(F32), 0.0)),)


def _local_step(x, mem, tgt, wts, small):
    t, d = x.shape
    nh = GDN_HEADS
    w_cat = wts["w_cat"]
    cat, cat_w = _cat_ranges(d)
    assert w_cat.shape == (d, cat_w)
    alog_pad = _pad_cols(small["a_log"], LANES)
    dt_pad = _pad_cols(small["dt_bias"], LANES)
    kvw = cat["kv_a"][1]
    assert cat["ab"][0] == cat["kv_a"][0] + kvw
    ab_blk = kvw // LANES

    n = _rms_fwd(x, small["g_mix"], name="rms_mix")
    q_a = _mm(n, w_cat, b_window=cat["q_a"], out_dtypes=(BF16,), name="in_q_a")
    kv_a, ab = _mm(n, w_cat, b_window=(cat["kv_a"][0], kvw + AB_PAD), out_dtypes=(BF16, F32), name="in_kv_ab")
    qkvb = _mm(n, w_cat, b_window=cat["qkv_b"], tn=512, name="in_qkv_b")
    z = _mm(n, w_cat, b_window=cat["z"], name="in_z")
    q_c = _mm(n, w_cat, b_window=cat["q_c"], out_dtypes=(BF16,), name="in_q_c")
    gates = _mm(n, w_cat, b_window=cat["gates"], name="in_gates")
    y_a, lse = _swa_fwd(q_a, kv_a, small["sinks"], name="swa_fwd")
    xc, qkvn, aux = _gdn_pre_fwd(qkvb, small["conv_w"], ab, alog_pad, dt_pad, ab_blk=ab_blk, name="gdn_pre_fwd")
    aux_t = aux[:, :16].reshape(t // GDN_CHUNK, GDN_CHUNK, 16).transpose(0, 2, 1)
    gdn_u, gdn_w, gdn_qd, gdn_kd, gdn_mm, gdn_tinv = _gdn_local_fwd(qkvn, aux, aux_t, name="gdn_local_fwd")
    o_b, gdn_vn, s_all = _gdn_seq_fwd(gdn_u, gdn_w, gdn_qd, gdn_kd, gdn_mm, aux, name="gdn_seq_fwd")
    y_b = _gdn_post_fwd(o_b, z, small["gdn_norm_w"], name="gdn_post_fwd")
    nmem = _rms_fwd(mem, small["g_mem"], name="rms_mem")
    mkv = _mm(nmem, wts["w_mem_kv"], out_dtypes=(BF16,), name="mem_kv")
    y_c = _xa_fwd(q_c, mkv, name="xa_fwd")
    ys = (y_a, y_b, y_c)
    w_ups = (wts["w_swa_up"], wts["w_gdn_up"], wts["w_xa_up"])
    merged = _merge_fwd(ys, w_ups, gates, name="merge_fwd")
    h1 = _mm(merged, wts["w_out"], extras=(x,), epilogue=_add_epilogue, name="out_proj")
    n2 = _rms_fwd(h1, small["g_mlp"], name="rms_mlp")
    u, act = _mm(n2, wts["w_mlp_in"], b_sharded=True, out_dtypes=(BF16, BF16), epilogue=_relu2_epilogue, name="mlp_in")
    h2 = _mm(act, wts["w_mlp_out"], extras=(h1,), epilogue=_add_epilogue, name="mlp_out")
    dh2, dg_final, loss = _final_loss(h2, small["g_final"], tgt, name="final_loss")

    grads = {"g_final": dg_final}
    du = _mm(dh2, wts["w_mlp_out"], tb=True, out_dtypes=(BF16,), extras=(u,), epilogue=_drelu2_epilogue, name="d_mlp_act")
    grads["w_mlp_out"] = _mm(act, dh2, ta=True, out_dtypes=(BF16,), name="dw_mlp_out")
    grads["w_mlp_in"] = _mm(n2, du, ta=True, out_sharded=True, out_dtypes=(BF16,), name="dw_mlp_in")
    dn2 = _mm(du, wts["w_mlp_in"], tb=True, b_sharded=True, name="d_mlp_in")
    dh1, grads["g_mlp"] = _rms_bwd(dn2, h1, small["g_mlp"], dh2, name="rms_mlp_bwd")
    dmerged = _mm(dh1, wts["w_out"], tb=True, name="d_out_proj")
    grads["w_out"] = _mm(merged, dh1, ta=True, out_dtypes=(BF16,), name="dw_out")
    dus_and_dgates = _merge_bwd(ys, w_ups, gates, dmerged, name="merge_bwd")
    dus, dgates = dus_and_dgates[:3], dus_and_dgates[3:]
    dys = []
    for y, du_i, w_up, key in zip(ys, dus, w_ups, ("w_swa_up", "w_gdn_up", "w_xa_up")):
        dys.append(_mm(du_i, w_up, tb=True, b_sharded=True, out_dtypes=(BF16,), name="d_" + key))
        grads[key] = _mm(y, du_i, ta=True, out_sharded=True, out_dtypes=(BF16,), name="dw_" + key[2:])
    dq_a, dkv_a, grads["sinks"] = _swa_bwd(q_a, kv_a, small["sinks"], y_a, lse, dys[0], name="swa_bwd")
    do_b, dz, grads["gdn_norm_w"] = _gdn_post_bwd(dys[1], o_b, z, small["gdn_norm_w"], name="gdn_post_bwd")
    dvn, dqd, dkd, dw_, dlast = _gdn_seq_bwd(do_b, gdn_w, gdn_qd, gdn_kd, gdn_mm, gdn_vn, s_all, aux, name="gdn_seq_bwd")
    dqkvn, daux = _gdn_local_bwd(qkvn, aux, aux_t, gdn_tinv, gdn_u, gdn_w, gdn_vn, do_b, dvn, dqd, dkd, dw_, dlast,
                                 name="gdn_local_bwd")
    dxc, dab, dalog, ddt = _gdn_pre_bwd1(xc, dqkvn, daux, ab, alog_pad, dt_pad, ab_blk=ab_blk, name="gdn_pre_bwd1")
    grads["a_log"], grads["dt_bias"] = dalog[:, :nh], ddt[:, :nh]
    dqkvb, grads["conv_w"] = _gdn_pre_bwd2(dxc, qkvb, small["conv_w"], name="gdn_pre_bwd2")
    dq_c, dmkv = _xa_bwd(q_c, mkv, dys[2], name="xa_bwd")
    grads["w_mem_kv"] = _mm(nmem, dmkv, ta=True, out_dtypes=(BF16,), name="dw_mem_kv")
    dnmem = _mm(dmkv, wts["w_mem_kv"], tb=True, name="d_mem_kv")
    _, grads["g_mem"] = _rms_bwd(dnmem, mem, small["g_mem"], jnp.zeros_like(mem), name="rms_mem_bwd")
    dp = jnp.concatenate([*dgates, dq_a, dqkvb, dz, dq_c, dkv_a, dab, jnp.zeros((t, AB_PAD - LANES), BF16)], axis=1)
    dn = _mm(dp, w_cat, tb=True, name="d_in_proj")
    grads["w_cat"] = _mm(n, dp, ta=True, out_dtypes=(BF16,), name="dw_in")
    dx, grads["g_mix"] = _rms_bwd(dn, x, small["g_mix"], dh1, name="rms_mix_bwd")
    return loss, dx, grads


HBM_SPEC = pl.BlockSpec(memory_space=pltpu.HBM)
VMEM_SPEC = pl.BlockSpec(memory_space=pltpu.VMEM)
N_CHIPS = N_SHARDS
N_DEV = 8
DMA_CHUNK_BYTES = 1 << 20


def _place():
    return lax.axis_index("x"), lax.axis_index("y"), lax.axis_index("c")


def _other_chips(x, y):
    return [(1 - x, y), (x, 1 - y), (1 - x, 1 - y)]


def _n_chunks(rows, row_bytes):
    n = 1
    while rows % (2 * n) == 0 and (rows // (2 * n)) % 16 == 0 and (rows // n) * row_bytes > DMA_CHUNK_BYTES:
        n *= 2
    return n


def _sem_scratch(n_remote, n_local):
    return [pltpu.SemaphoreType.DMA((max(n_remote, 1),)), pltpu.SemaphoreType.DMA((max(n_remote, 1),)),
            pltpu.SemaphoreType.DMA((max(n_local, 1),))]


def _all_gather_weights(shards, *, name):
    nw = len(shards)
    plan = []
    for i, s in enumerate(shards):
        rh = s.shape[0] // 2
        nch = _n_chunks(rh, s.shape[1] * s.dtype.itemsize)
        plan += [(i, q * (rh // nch), rh // nch) for q in range(nch)]
    n_ici = 3 * len(plan)

    def body(*refs):
        w_refs, out_refs = refs[:nw], refs[nw:2 * nw]
        send_sems, recv_sems, local_sems = refs[2 * nw:]
        x, y, c = _place()
        sibling = (x, y, 1 - c)
        chips = _other_chips(x, y)

        def copy(k, src, dst, to):
            return pltpu.make_async_remote_copy(src_ref=src, dst_ref=dst, send_sem=send_sems.at[k],
                                                recv_sem=recv_sems.at[k], device_id=to, device_id_type=MESH)

        local = [pltpu.make_async_copy(w_refs[i], out_refs[i].at[2 * x + y], local_sems.at[i]) for i in range(nw)]
        for cp in local:
            cp.start()
        first, arrive, passed, from_sibling = [], [], [], []
        for i, r0, nr in plan:
            rh = shards[i].shape[0] // 2
            mine = pl.ds(c * rh + r0, nr)
            other = pl.ds((1 - c) * rh + r0, nr)
            for chip in chips:
                k = len(first)
                cid = 2 * chip[0] + chip[1]
                first.append(copy(k, w_refs[i].at[mine], out_refs[i].at[2 * x + y, mine], (*chip, c)))
                arrive.append(copy(k, out_refs[i].at[cid, mine], out_refs[i].at[cid, mine], (x, y, c)))
                passed.append(copy(n_ici + k, out_refs[i].at[cid, mine], out_refs[i].at[cid, mine], sibling))
                from_sibling.append(copy(n_ici + k, out_refs[i].at[cid, other], out_refs[i].at[cid, other], (x, y, c)))
        for cp in first:
            cp.start()
        for k in range(n_ici):
            arrive[k].wait_recv()
            passed[k].start()
        for cp in from_sibling:
            cp.wait_recv()
        for cp in first + passed:
            cp.wait_send()
        for cp in local:
            cp.wait()

    return pl.pallas_call(
        body, out_shape=[jax.ShapeDtypeStruct((N_CHIPS, *s.shape), s.dtype) for s in shards],
        in_specs=[HBM_SPEC] * nw, out_specs=[HBM_SPEC] * nw,
        scratch_shapes=_sem_scratch(2 * n_ici, nw), name=name,
    )(*shards)


def _exchange_call(copies_of, n_copies, ins, out_shapes, *, name, aliases=None):
    n_in, n_out = len(ins), len(out_shapes)

    def body(*refs):
        in_refs, out_refs = refs[:n_in], refs[n_in:n_in + n_out]
        send_sems, recv_sems, _ = refs[n_in + n_out:]
        todo = copies_of(in_refs, out_refs, _place())
        assert len(todo) == n_copies
        cps = [pltpu.make_async_remote_copy(src_ref=src, dst_ref=dst, send_sem=send_sems.at[k], recv_sem=recv_sems.at[k],
                                            device_id=to, device_id_type=MESH) for k, (src, dst, to) in enumerate(todo)]
        for cp in cps:
            cp.start()
        for cp in cps:
            cp.wait()

    return pl.pallas_call(
        body, out_shape=out_shapes, in_specs=[HBM_SPEC] * n_in, out_specs=[HBM_SPEC] * n_out,
        scratch_shapes=_sem_scratch(n_copies, 0), input_output_aliases=aliases or {}, name=name,
    )(*ins)


def _half_chunks(arrs, row_axis):
    plan = []
    for i, a in enumerate(arrs):
        rh = a.shape[row_axis] // 2
        row_bytes = a.dtype.itemsize * math.prod(a.shape) // a.shape[row_axis]
        nch = _n_chunks(rh, row_bytes)
        plan += [(i, q * (rh // nch), rh // nch) for q in range(nch)]
    return plan


def _sibling_halves(gs, *, name):
    plan = _half_chunks(gs, 1)

    def copies_of(in_refs, out_refs, place):
        x, y, c = place
        out = []
        for i, r0, nr in plan:
            rh = gs[i].shape[1] // 2
            out.append((in_refs[i].at[:, pl.ds((1 - c) * rh + r0, nr), :], out_refs[i].at[:, pl.ds(r0, nr), :],
                        (x, y, 1 - c)))
        return out

    shapes = [jax.ShapeDtypeStruct((g.shape[0], g.shape[1] // 2, g.shape[2]), g.dtype) for g in gs]
    return _exchange_call(copies_of, len(plan), gs, shapes, name=name)


def _chip_exchange(s1s, *, name):
    plan = _half_chunks([jax.ShapeDtypeStruct((2 * s.shape[1], s.shape[2]), s.dtype) for s in s1s], 0)

    def copies_of(in_refs, out_refs, place):
        x, y, c = place
        out = []
        for i, r0, nr in plan:
            for j, chip in enumerate(_other_chips(x, y)):
                out.append((in_refs[i].at[2 * chip[0] + chip[1], pl.ds(r0, nr), :], out_refs[i].at[j, pl.ds(r0, nr), :],
                            (*chip, c)))
        return out

    shapes = [jax.ShapeDtypeStruct((3, *s.shape[1:]), s.dtype) for s in s1s]
    return _exchange_call(copies_of, 3 * len(plan), s1s, shapes, name=name)


def _join_halves(gs, *, name):
    plan = _half_chunks(gs, 0)

    def copies_of(in_refs, out_refs, place):
        x, y, c = place
        out = []
        for i, r0, nr in plan:
            rows = out_refs[i].at[pl.ds(c * (gs[i].shape[0] // 2) + r0, nr), :]
            out.append((rows, rows, (x, y, 1 - c)))
        return out

    shapes = [jax.ShapeDtypeStruct(g.shape, g.dtype) for g in gs]
    return _exchange_call(copies_of, len(plan), gs, shapes, name=name, aliases={i: i for i in range(len(gs))})


def _row_block(rows, cols):
    tb = rows
    while tb % 32 == 0 and tb * cols * 4 > (2 << 20):
        tb //= 2
    return tb


def _pair_sum(g, sib, core, *, name):
    ns, r, c = g.shape
    rh = r // 2
    tb = _row_block(rh, c)
    nb = rh // tb

    def body(core_ref, g_ref, s_ref, o_ref):
        o_ref[...] = (g_ref[...].astype(F32) + s_ref[...].astype(F32)).astype(o_ref.dtype)

    mine = pl.BlockSpec((None, tb, c), lambda s, i, core_ref: (s, core_ref[0] * nb + i, 0))
    half = pl.BlockSpec((None, tb, c), lambda s, i, core_ref: (s, i, 0))
    return pl.pallas_call(
        body, grid_spec=pltpu.PrefetchScalarGridSpec(num_scalar_prefetch=1, grid=(ns, nb), in_specs=[mine, half],
                                                     out_specs=half),
        out_shape=jax.ShapeDtypeStruct((ns, rh, c), BF16), compiler_params=_cp(("parallel", "parallel")), name=name,
    )(core, g, sib)


def _chip_sum(s1, rcv, where, *, name):
    _, rh, c = s1.shape
    tb = _row_block(rh, c)
    nb = rh // tb

    def body(where_ref, own_ref, r0_ref, r1_ref, r2_ref, o_ref):
        acc = own_ref[...].astype(F32)
        for r in (r0_ref, r1_ref, r2_ref):
            acc = acc + r[...].astype(F32)
        o_ref[...] = acc

    own = pl.BlockSpec((None, tb, c), lambda i, w: (w[1], i, 0))
    got = [pl.BlockSpec((None, tb, c), functools.partial(lambda i, w, j: (j, i, 0), j=j)) for j in range(3)]
    return pl.pallas_call(
        body, grid_spec=pltpu.PrefetchScalarGridSpec(
            num_scalar_prefetch=1, grid=(nb,), in_specs=[own] + got,
            out_specs=pl.BlockSpec((tb, c), lambda i, w: (w[0] * nb + i, 0))),
        out_shape=jax.ShapeDtypeStruct((2 * rh, c), F32), compiler_params=_cp(("parallel",)), name=name,
    )(where, s1, rcv, rcv, rcv)


def _all_gather_small(blk, *, name):
    r = blk.shape[0]

    def body(b_ref, out_ref, send_sems, recv_sems):
        x, y, c = _place()
        me = 4 * x + 2 * y + c
        out_ref[me] = b_ref[...]
        sends = []
        for k in range(1, N_DEV):
            peer = (x ^ (k >> 2), y ^ ((k >> 1) & 1), c ^ (k & 1))
            sends.append(pltpu.make_async_remote_copy(src_ref=b_ref, dst_ref=out_ref.at[me], send_sem=send_sems.at[k - 1],
                                                      recv_sem=recv_sems.at[k - 1], device_id=peer, device_id_type=MESH))
        for cp in sends:
            cp.start()
        for k in range(1, N_DEV):
            rows = out_ref.at[me ^ k]
            pltpu.make_async_remote_copy(src_ref=rows, dst_ref=rows, send_sem=send_sems.at[k - 1],
                                         recv_sem=recv_sems.at[k - 1], device_id=(x, y, c), device_id_type=MESH).wait_recv()
        for cp in sends:
            cp.wait_send()

    return pl.pallas_call(
        body, out_shape=jax.ShapeDtypeStruct((N_DEV, r, LANES), blk.dtype), in_specs=[VMEM_SPEC], out_specs=VMEM_SPEC,
        scratch_shapes=[pltpu.SemaphoreType.DMA((N_DEV - 1,)), pltpu.SemaphoreType.DMA((N_DEV - 1,))],
        name=name,
    )(blk)


def _sum_rows(parts, out_dtype, *, name, tb=1024):
    rows = parts[0].shape[0]
    tb = _blk(rows, tb)

    def body(*refs):
        acc = refs[0][...].astype(F32)
        for r in refs[1:-1]:
            acc = acc + r[...].astype(F32)
        refs[-1][...] = acc.astype(refs[-1].dtype)

    spec = pl.BlockSpec((tb, LANES), lambda i: (i, 0))
    return pl.pallas_call(
        body, grid=(rows // tb,), in_specs=[spec] * len(parts), out_specs=spec,
        out_shape=jax.ShapeDtypeStruct((rows, LANES), out_dtype), compiler_params=_cp(("parallel",)), name=name,
    )(*parts)


def _reduce_scatter(gs, core, where):
    sibs = _sibling_halves(gs, name="rs_sibling_halves")
    s1s = [_pair_sum(g, s, core, name=f"rs_pair_sum_{i}") for i, (g, s) in enumerate(zip(gs, sibs))]
    rcvs = _chip_exchange(s1s, name="rs_chip_exchange")
    halves = [_chip_sum(s1, rcv, where, name=f"rs_chip_sum_{i}") for i, (s1, rcv) in enumerate(zip(s1s, rcvs))]
    return _join_halves(halves, name="rs_join_halves")


BIG = (
    ("w_in", 1), ("w_mem_kv", 0), ("w_swa_up", 1), ("w_gdn_up", 1), ("w_xa_up", 1), ("w_out", 0), ("w_mlp_in", 1),
    ("w_mlp_out", 0))
SMALL = ("g_mix", "sinks", "a_log", "dt_bias", "gdn_norm_w", "g_mem", "g_mlp", "g_final")


def _rows128(a, rows):
    flat = a.reshape(-1)
    return jnp.pad(flat, (0, rows * LANES - flat.shape[0])).reshape(rows, LANES)


def kernel(x, mem, g_mix, w_in, sinks, conv_w, a_log, dt_bias, gdn_norm_w, g_mem, w_mem_kv, w_swa_up, w_gdn_up, w_xa_up, w_out, g_mlp, w_mlp_in, w_mlp_out, g_final, loss_target, m_g_mix, m_w_in, m_sinks, m_conv_w, m_a_log, m_dt_bias, m_gdn_norm_w, m_g_mem, m_w_mem_kv, m_w_swa_up, m_w_gdn_up, m_w_xa_up, m_w_out, m_g_mlp, m_w_mlp_in, m_w_mlp_out, m_g_final, v_g_mix, v_w_in, v_sinks, v_conv_w, v_a_log, v_dt_bias, v_gdn_norm_w, v_g_mem, v_w_mem_kv, v_w_swa_up, v_w_gdn_up, v_w_xa_up, v_w_out, v_g_mlp, v_w_mlp_in, v_w_mlp_out, v_g_final):
    given = dict(locals())
    xi, yi, ci = _place()
    chip = 2 * xi + yi
    core = jnp.reshape(ci, (1,)).astype(jnp.int32)
    where = jnp.stack([ci, chip]).astype(jnp.int32)

    shards = [given[k][0].astype(BF16) for k, _ in BIG]
    gathered = dict(zip([k for k, _ in BIG], _all_gather_weights(shards, name="ag_weights")))
    wts = {}
    for k, ax in BIG:
        g = gathered[k]
        if k == "w_in":
            wts["w_cat"] = _to_cat(jnp.transpose(g, (1, 0, 2)).reshape(g.shape[1], -1))
        elif ax == 0:
            wts[k] = g.reshape(-1, g.shape[2])
        else:
            wts[k] = g
    conv_shard = conv_w[0]
    conv_rows = -(-conv_shard.size // (8 * LANES)) * 8
    conv_all = _all_gather_small(_rows128(conv_shard, conv_rows), name="ag_conv")
    conv_full = jnp.concatenate(
        [conv_all[2 * s].reshape(-1)[:conv_shard.size].reshape(conv_shard.shape) for s in range(N_CHIPS)], axis=1)

    small = {k: given[k].reshape(1, -1) for k in SMALL}
    small["conv_w"] = conv_full
    loss_row, dx, grads = _local_step(x[0], mem[0], loss_target[0], wts, small)

    gs = []
    for (k, ax), sh in zip(BIG, shards):
        if k == "w_in":
            dw_in = _from_cat(grads["w_cat"])
            gs.append(jnp.transpose(dw_in.reshape(dw_in.shape[0], N_CHIPS, -1), (1, 0, 2)))
        elif ax == 0:
            gs.append(grads[k].reshape(N_CHIPS, *sh.shape))
        else:
            gs.append(grads[k])
    big_grads = dict(zip([k for k, _ in BIG], _reduce_scatter(gs, core, where)))

    layout = [("loss", loss_row[:, :1])] + [(k, grads[k]) for k in SMALL] + [("conv_w", grads["conv_w"])]
    rows = [-(-a.size // LANES) for _, a in layout]
    blk_rows = -(-sum(rows) // 8) * 8
    blk = jnp.concatenate([_rows128(a.astype(F32), n) for (_, a), n in zip(layout, rows)]
                          + [jnp.zeros((blk_rows - sum(rows), LANES), F32)], axis=0)
    gathered = _all_gather_small(blk, name="ag_small_grads")
    reduced = _sum_rows([gathered[i] for i in range(N_DEV)], F32, name="small_grad_sum")
    small_grads, start = {}, 0
    for (k, a), n in zip(layout, rows):
        small_grads[k] = reduced[start:start + n].reshape(-1)[:a.size].reshape(a.shape)
        start += n
    loss = small_grads["loss"].reshape(())
    cw = conv_shard.shape[1]
    conv_grad = lax.dynamic_slice_in_dim(small_grads["conv_w"], chip * cw, cw, axis=1)

    names = ["g_mix", "w_in", "sinks", "conv_w", "a_log", "dt_bias", "gdn_norm_w", "g_mem", "w_mem_kv", "w_swa_up",
             "w_gdn_up", "w_xa_up", "w_out", "g_mlp", "w_mlp_in", "w_mlp_out", "g_final"]
    out_g, out_d, out_m, out_v = [], [], [], []
    for k in names:
        w, m, v = given[k], given["m_" + k], given["v_" + k]
        if k in big_grads:
            g2 = big_grads[k]
        elif k == "conv_w":
            g2 = conv_grad
        else:
            g2 = small_grads[k]
        shape2 = g2.shape
        delta, new_m, new_v = _adamw(w.reshape(shape2), g2, m.reshape(shape2), v.reshape(shape2), name="adamw_" + k)
        out_g.append(g2.reshape(w.shape))
        out_d.append(delta.reshape(w.shape))
        out_m.append(new_m.reshape(w.shape))
        out_v.append(new_v.reshape(w.shape))
    return (loss, dx[None], *out_g, *out_d, *out_m, *out_v)
```

```python
import functools
import math

import jax
import jax.numpy as jnp
from jax import lax
from jax.experimental import pallas as pl
from jax.experimental.pallas import tpu as pltpu

F32 = jnp.float32
BF16 = jnp.bfloat16
HI = lax.Precision.HIGHEST
MESH = pl.DeviceIdType.MESH

SWA_Q_HEADS = 16
SWA_KV_HEADS = 2
SWA_HEAD_DIM = 64
SWA_WINDOW = 128
GDN_HEADS = 4
GDN_HEAD_DIM = 128
GDN_CONV = 4
GDN_CHUNK = 64
XA_HEADS = 4
XA_HEAD_DIM = 128
RMS_EPS = 1e-6
L2_EPS = 1e-6
ADAM_LR = 0.001
ADAM_B1 = 0.9
ADAM_B2 = 0.999
ADAM_EPS = 1e-08
ADAM_WD = 0.01
ADAM_STEP = 10

LANES = 128
N_SHARDS = 4
VMEM_LIMIT = 56 * 1024 * 1024

NT = (((1,), (1,)), ((), ()))
TN = (((0,), (0,)), ((), ()))
NN = (((1,), (0,)), ((), ()))


def _cp(sem=None):
    return pltpu.CompilerParams(dimension_semantics=sem, vmem_limit_bytes=VMEM_LIMIT)


def _blk(dim, pref):
    if dim <= pref:
        return dim
    b = (pref // LANES) * LANES
    while dim % b:
        b -= LANES
    assert b > 0, (dim, pref)
    return b


def _dot(a, b, dims=NN, precision=None):
    return lax.dot_general(a, b, dims, precision=precision, preferred_element_type=F32)


def _sigmoid(x):
    return 1.0 / (1.0 + jnp.exp(-x))


MM_TK_BYTES = 4096


def _mm(a, b, *, name, ta=False, tb=False, out_dtypes=(F32,), epilogue=None, extras=(), tm=1024, tn=1024, tk=None,
        b_sharded=False, out_sharded=False, b_window=None):
    (kdim, m) = a.shape if ta else a.shape[::-1]
    col0 = 0
    n_lim = k_lim = None
    if b_sharded:
        ns, rows_w, per = b.shape
        if tb:
            kb, n, k_lim = ns * per, rows_w, per
        else:
            kb, n, n_lim = rows_w, ns * per, per
    else:
        (kb, n) = b.shape[::-1] if tb else b.shape
        if b_window is not None:
            assert not tb
            col0, n = b_window
    assert kdim == kb, (a.shape, b.shape, ta, tb)
    if out_sharded:
        assert n % N_SHARDS == 0
        n_lim = n // N_SHARDS if n_lim is None else n_lim
        assert n_lim == n // N_SHARDS
    if tk is None:
        tk = MM_TK_BYTES // max(a.dtype.itemsize, b.dtype.itemsize)
    tm, tn, tk = _blk(m, tm), _blk(n_lim or n, tn), _blk(k_lim or kdim, tk)
    assert col0 % tn == 0, (col0, tn)
    nk = kdim // tk
    a_spec = pl.BlockSpec((tk, tm), lambda i, j, k: (k, i)) if ta else pl.BlockSpec((tm, tk), lambda i, j, k: (i, k))
    if b_sharded and tb:
        kpb = k_lim // tk
        b_spec = pl.BlockSpec((None, tn, tk), lambda i, j, k: (k // kpb, j, k % kpb))
    elif b_sharded:
        bpb = n_lim // tn
        b_spec = pl.BlockSpec((None, tk, tn), lambda i, j, k: (j // bpb, k, j % bpb))
    elif tb:
        b_spec = pl.BlockSpec((tn, tk), lambda i, j, k: (j, k))
    else:
        b_spec = pl.BlockSpec((tk, tn), lambda i, j, k: (k, j + col0 // tn))
    x_spec = pl.BlockSpec((tm, tn), lambda i, j, k: (i, j))
    if out_sharded:
        opb = n_lim // tn
        o_spec = pl.BlockSpec((None, tm, tn), lambda i, j, k: (j // opb, i, j % opb))
        out_shape = (N_SHARDS, m, n_lim)
    else:
        o_spec, out_shape = x_spec, (m, n)
    dims = ((((0 if ta else 1),), ((1 if tb else 0),)), ((), ()))
    n_extra, n_out = len(extras), len(out_dtypes)

    def body(*refs):
        a_ref, b_ref = refs[:2]
        extra_refs = refs[2:2 + n_extra]
        out_refs = refs[2 + n_extra:2 + n_extra + n_out]
        part = _dot(a_ref[...].astype(BF16), b_ref[...].astype(BF16), dims)

        def finish(acc):
            vals = epilogue(acc, *[r[...] for r in extra_refs]) if epilogue is not None else (acc,) * n_out
            assert len(vals) == n_out
            for r, v in zip(out_refs, vals):
                r[...] = v.astype(r.dtype)

        if nk == 1:
            finish(part)
            return
        acc_ref = refs[-1]
        k = pl.program_id(2)

        @pl.when(k == 0)
        def _():
            acc_ref[...] = part

        @pl.when((k > 0) & (k < nk - 1))
        def _():
            acc_ref[...] += part

        @pl.when(k == nk - 1)
        def _():
            finish(acc_ref[...] + part)

    outs = pl.pallas_call(
        body,
        grid=(m // tm, n // tn, nk),
        in_specs=[a_spec, b_spec] + [x_spec] * n_extra,
        out_specs=[o_spec] * n_out,
        out_shape=[jax.ShapeDtypeStruct(out_shape, d) for d in out_dtypes],
        scratch_shapes=[pltpu.VMEM((tm, tn), F32)] if nk > 1 else [],
        compiler_params=_cp(("parallel", "parallel", "arbitrary")),
        name=name,
    )(a, b, *extras)
    return outs[0] if n_out == 1 else outs


def _rms_fwd(x, g, *, name, tm=512):
    t, d = x.shape
    tm = _blk(t, tm)

    def body(x_ref, g_ref, n_ref):
        xv = x_ref[...]
        r = lax.rsqrt(jnp.mean(xv * xv, axis=-1, keepdims=True) + RMS_EPS)
        n_ref[...] = (xv * r * g_ref[...]).astype(n_ref.dtype)

    return pl.pallas_call(
        body, grid=(t // tm,),
        in_specs=[pl.BlockSpec((tm, d), lambda i: (i, 0)), pl.BlockSpec((1, d), lambda i: (0, 0))],
        out_specs=pl.BlockSpec((tm, d), lambda i: (i, 0)),
        out_shape=jax.ShapeDtypeStruct((t, d), BF16),
        compiler_params=_cp(("parallel",)), name=name,
    )(x, g)


def _rms_bwd(dn, x, g, dres, *, name, tm=512):
    t, d = x.shape
    tm = _blk(t, tm)

    def body(dn_ref, x_ref, g_ref, dres_ref, dx_ref, dg_ref):
        i = pl.program_id(0)
        xv = x_ref[...]
        r = lax.rsqrt(jnp.mean(xv * xv, axis=-1, keepdims=True) + RMS_EPS)
        xh = xv * r
        dnv = dn_ref[...].astype(F32)
        dxh = dnv * g_ref[...]
        dx_ref[...] = dres_ref[...] + r * (dxh - xh * jnp.mean(dxh * xh, axis=-1, keepdims=True))
        part = jnp.sum(dnv * xh, axis=0, keepdims=True)

        @pl.when(i == 0)
        def _():
            dg_ref[...] = part

        @pl.when(i > 0)
        def _():
            dg_ref[...] += part

    row = pl.BlockSpec((tm, d), lambda i: (i, 0))
    vec = pl.BlockSpec((1, d), lambda i: (0, 0))
    return pl.pallas_call(
        body, grid=(t // tm,),
        in_specs=[row, row, vec, row], out_specs=[row, vec],
        out_shape=[jax.ShapeDtypeStruct((t, d), F32), jax.ShapeDtypeStruct((1, d), F32)],
        compiler_params=_cp(("arbitrary",)), name=name,
    )(dn, x, g, dres)


def _final_loss(h, g, tgt, *, name, tm=512):
    t, d = h.shape
    tm = _blk(t, tm)

    def body(h_ref, g_ref, t_ref, dh_ref, dg_ref, loss_ref):
        i = pl.program_id(0)
        hv = h_ref[...]
        r = lax.rsqrt(jnp.mean(hv * hv, axis=-1, keepdims=True) + RMS_EPS)
        xh = hv * r
        e = xh * g_ref[...] - t_ref[...]
        dy = e * (1.0 / d)
        dxh = dy * g_ref[...]
        dh_ref[...] = r * (dxh - xh * jnp.mean(dxh * xh, axis=-1, keepdims=True))
        dg_part = jnp.sum(dy * xh, axis=0, keepdims=True)
        row_loss = jnp.sum(e * e, axis=-1, keepdims=True) * (0.5 / d)
        loss_part = jnp.sum(row_loss, axis=0, keepdims=True)

        @pl.when(i == 0)
        def _():
            dg_ref[...] = dg_part
            loss_ref[...] = jnp.broadcast_to(loss_part, loss_ref.shape)

        @pl.when(i > 0)
        def _():
            dg_ref[...] += dg_part
            loss_ref[...] += jnp.broadcast_to(loss_part, loss_ref.shape)

    row = pl.BlockSpec((tm, d), lambda i: (i, 0))
    vec = pl.BlockSpec((1, d), lambda i: (0, 0))
    return pl.pallas_call(
        body, grid=(t // tm,),
        in_specs=[row, vec, row], out_specs=[row, vec, pl.BlockSpec((1, LANES), lambda i: (0, 0))],
        out_shape=[jax.ShapeDtypeStruct((t, d), F32), jax.ShapeDtypeStruct((1, d), F32),
                   jax.ShapeDtypeStruct((1, LANES), F32)],
        compiler_params=_cp(("arbitrary",)), name=name,
    )(h, g, tgt)


def _swa_mask(n, reps):
    w = SWA_WINDOW
    qi = lax.broadcasted_iota(jnp.int32, (reps * w, 2 * w), 0) & (w - 1)
    kj = lax.broadcasted_iota(jnp.int32, (reps * w, 2 * w), 1)
    return (kj > qi) & (kj <= qi + w) & ((n > 0) | (kj >= w))


def _stack_heads(ref, heads, width):
    return jnp.concatenate([ref[:, h * width:(h + 1) * width] for h in heads], axis=0)


def _stack_scalars(ref, heads, rows):
    return jnp.concatenate([jnp.broadcast_to(ref[0:1, h:h + 1], (rows, 1)) for h in heads], axis=0)


def _swa_fwd(q, kv, sinks, *, name):
    t = q.shape[0]
    w, hd, hq, hkv = SWA_WINDOW, SWA_HEAD_DIM, SWA_Q_HEADS, SWA_KV_HEADS
    grp = hq // hkv
    kvw = hkv * hd
    nb = t // w

    def body(q_ref, kvp_ref, kvc_ref, s_ref, o_ref, lse_ref):
        n = pl.program_id(0)
        mask = _swa_mask(n, grp)
        kvcat = jnp.concatenate([kvp_ref[...], kvc_ref[...]], axis=0)
        outs, lses = [], []
        for hk in range(hkv):
            heads = range(hk * grp, (hk + 1) * grp)
            qs = _stack_heads(q_ref, heads, hd)
            kh = kvcat[:, hk * hd:(hk + 1) * hd]
            vh = kvcat[:, kvw + hk * hd:kvw + (hk + 1) * hd]
            sk = _stack_scalars(s_ref, heads, w)
            s = jnp.where(mask, _dot(qs, kh, NT) * (hd ** -0.5), -jnp.inf)
            m = jnp.maximum(jnp.max(s, axis=-1, keepdims=True), sk)
            p = jnp.exp(s - m)
            den = jnp.sum(p, axis=-1, keepdims=True) + jnp.exp(sk - m)
            o = _dot((p * (1.0 / den)).astype(BF16), vh)
            lse = m + jnp.log(den)
            outs += [o[j * w:(j + 1) * w] for j in range(grp)]
            lses += [lse[j * w:(j + 1) * w] for j in range(grp)]
        o_ref[...] = jnp.concatenate(outs, axis=1).astype(o_ref.dtype)
        lse_ref[...] = jnp.concatenate(lses, axis=1)

    return pl.pallas_call(
        body, grid=(nb,),
        in_specs=[pl.BlockSpec((w, hq * hd), lambda i: (i, 0)),
                  pl.BlockSpec((w, 2 * kvw), lambda i: (jnp.maximum(i - 1, 0), 0)),
                  pl.BlockSpec((w, 2 * kvw), lambda i: (i, 0)),
                  pl.BlockSpec((1, hq), lambda i: (0, 0))],
        out_specs=[pl.BlockSpec((w, hq * hd), lambda i: (i, 0)), pl.BlockSpec((w, hq), lambda i: (i, 0))],
        out_shape=[jax.ShapeDtypeStruct((t, hq * hd), BF16), jax.ShapeDtypeStruct((t, hq), F32)],
        compiler_params=_cp(("parallel",)), name=name,
    )(q, kv, kv, sinks)


def _swa_bwd(q, kv, sinks, o, lse, do, *, name):
    t = q.shape[0]
    w, hd, hq, hkv = SWA_WINDOW, SWA_HEAD_DIM, SWA_Q_HEADS, SWA_KV_HEADS
    grp = hq // hkv
    kvw = hkv * hd
    nb = t // w

    def body(q_ref, kvp_ref, kvc_ref, s_ref, o_ref, lse_ref, do_ref, dq_ref, dkv_ref, ds_ref, carry_ref):
        n = pl.program_id(0)

        @pl.when(n == 0)
        def _():
            ds_ref[...] = jnp.zeros_like(ds_ref)
            carry_ref[...] = jnp.zeros_like(carry_ref)

        @pl.when(n < nb)
        def _():
            mask = _swa_mask(n, grp)
            kvcat = jnp.concatenate([kvp_ref[...], kvc_ref[...]], axis=0)
            dqs, dsk, dks, dvs = [], [], [], []
            for hk in range(hkv):
                heads = range(hk * grp, (hk + 1) * grp)
                qs = _stack_heads(q_ref, heads, hd)
                dos = _stack_heads(do_ref, heads, hd)
                os_ = _stack_heads(o_ref, heads, hd)
                lse = _stack_heads(lse_ref, heads, 1)
                sk = _stack_scalars(s_ref, heads, w)
                kh = kvcat[:, hk * hd:(hk + 1) * hd]
                vh = kvcat[:, kvw + hk * hd:kvw + (hk + 1) * hd]
                s = _dot(qs, kh, NT) * (hd ** -0.5)
                p = jnp.exp(jnp.where(mask, s, -jnp.inf) - lse)
                delta = jnp.sum(dos.astype(F32) * os_.astype(F32), axis=-1, keepdims=True)
                ds = (p * (_dot(dos, vh, NT) - delta) * (hd ** -0.5)).astype(BF16)
                dq = _dot(ds, kh)
                dqs += [dq[j * w:(j + 1) * w] for j in range(grp)]
                dks.append(_dot(ds, qs, TN))
                dvs.append(_dot(p.astype(BF16), dos, TN))
                dsink = -jnp.exp(sk - lse) * delta
                dsk += [jnp.sum(dsink[j * w:(j + 1) * w], axis=0, keepdims=True) for j in range(grp)]
            dq_ref[...] = jnp.concatenate(dqs, axis=1).astype(dq_ref.dtype)
            ds_ref[...] += jnp.concatenate(dsk, axis=1)
            dkv_cat = jnp.concatenate(dks + dvs, axis=1)
            dkv_ref[...] = (carry_ref[...] + dkv_cat[:w]).astype(dkv_ref.dtype)
            carry_ref[...] = dkv_cat[w:]

        @pl.when(n == nb)
        def _():
            dkv_ref[...] = carry_ref[...].astype(dkv_ref.dtype)

    cur = lambda i: (jnp.minimum(i, nb - 1), 0)
    prev = lambda i: (jnp.clip(i - 1, 0, nb - 1), 0)
    return pl.pallas_call(
        body, grid=(nb + 1,),
        in_specs=[pl.BlockSpec((w, hq * hd), cur), pl.BlockSpec((w, 2 * kvw), prev), pl.BlockSpec((w, 2 * kvw), cur),
                  pl.BlockSpec((1, hq), lambda i: (0, 0)), pl.BlockSpec((w, hq * hd), cur),
                  pl.BlockSpec((w, hq), cur), pl.BlockSpec((w, hq * hd), cur)],
        out_specs=[pl.BlockSpec((w, hq * hd), cur), pl.BlockSpec((w, 2 * kvw), prev),
                   pl.BlockSpec((1, hq), lambda i: (0, 0))],
        out_shape=[jax.ShapeDtypeStruct((t, hq * hd), BF16), jax.ShapeDtypeStruct((t, 2 * kvw), BF16),
                   jax.ShapeDtypeStruct((1, hq), F32)],
        scratch_shapes=[pltpu.VMEM((w, 2 * kvw), F32)],
        compiler_params=_cp(("arbitrary",)), name=name,
    )(q, kv, kv, sinks, o, lse, do)


def _xa_fwd(q, mkv, *, name, tq=512):
    t, xw = q.shape
    nm = mkv.shape[0]
    hd, nh = XA_HEAD_DIM, XA_HEADS
    tq = _blk(t, tq)

    def body(q_ref, mkv_ref, o_ref):
        outs = []
        for h in range(nh):
            qh = q_ref[:, h * hd:(h + 1) * hd]
            kh = mkv_ref[:, h * hd:(h + 1) * hd]
            vh = mkv_ref[:, xw + h * hd:xw + (h + 1) * hd]
            s = _dot(qh, kh, NT) * (hd ** -0.5)
            p = jnp.exp(s - jnp.max(s, axis=-1, keepdims=True))
            p = p / jnp.sum(p, axis=-1, keepdims=True)
            outs.append(_dot(p.astype(BF16), vh))
        o_ref[...] = jnp.concatenate(outs, axis=1).astype(o_ref.dtype)

    return pl.pallas_call(
        body, grid=(t // tq,),
        in_specs=[pl.BlockSpec((tq, xw), lambda i: (i, 0)), pl.BlockSpec((nm, 2 * xw), lambda i: (0, 0))],
        out_specs=pl.BlockSpec((tq, xw), lambda i: (i, 0)),
        out_shape=jax.ShapeDtypeStruct((t, xw), BF16),
        compiler_params=_cp(("parallel",)), name=name,
    )(q, mkv)


def _xa_bwd(q, mkv, do, *, name, tq=512):
    t, xw = q.shape
    nm = mkv.shape[0]
    hd, nh = XA_HEAD_DIM, XA_HEADS
    tq = _blk(t, tq)

    def body(q_ref, mkv_ref, do_ref, dq_ref, dmkv_ref):
        i = pl.program_id(0)
        dqs, dks, dvs = [], [], []
        for h in range(nh):
            qh = q_ref[:, h * hd:(h + 1) * hd]
            kh = mkv_ref[:, h * hd:(h + 1) * hd]
            vh = mkv_ref[:, xw + h * hd:xw + (h + 1) * hd]
            doh = do_ref[:, h * hd:(h + 1) * hd]
            s = _dot(qh, kh, NT) * (hd ** -0.5)
            p = jnp.exp(s - jnp.max(s, axis=-1, keepdims=True))
            p = p / jnp.sum(p, axis=-1, keepdims=True)
            dp = _dot(doh, vh, NT)
            ds = (p * (dp - jnp.sum(p * dp, axis=-1, keepdims=True)) * (hd ** -0.5)).astype(BF16)
            dqs.append(_dot(ds, kh))
            dks.append(_dot(ds, qh, TN))
            dvs.append(_dot(p.astype(BF16), doh, TN))
        dq_ref[...] = jnp.concatenate(dqs, axis=1).astype(dq_ref.dtype)
        part = jnp.concatenate(dks + dvs, axis=1)

        @pl.when(i == 0)
        def _():
            dmkv_ref[...] = part

        @pl.when(i > 0)
        def _():
            dmkv_ref[...] += part

    row = pl.BlockSpec((tq, xw), lambda i: (i, 0))
    full = pl.BlockSpec((nm, 2 * xw), lambda i: (0, 0))
    return pl.pallas_call(
        body, grid=(t // tq,),
        in_specs=[row, full, row], out_specs=[row, full],
        out_shape=[jax.ShapeDtypeStruct((t, xw), BF16), jax.ShapeDtypeStruct((nm, 2 * xw), F32)],
        compiler_params=_cp(("arbitrary",)), name=name,
    )(q, mkv, do)


def _merge_specs(t, d, ys, ws, tm, tn):
    nj = d // tn
    wpb = ws[0].shape[2] // tn
    y_specs = [pl.BlockSpec((tm, y.shape[1]), lambda i, j: (i, 0)) for y in ys]
    w_specs = [pl.BlockSpec((None, w.shape[1], tn), lambda i, j: (j // wpb, 0, j % wpb)) for w in ws]
    g_specs = [pl.BlockSpec((tm, tn), functools.partial(lambda i, j, b: (i, j + b * nj), b=b)) for b in range(3)]
    return y_specs, w_specs, g_specs


def _merge_fwd(ys, ws, gates, *, name, tm=512, tn=512):
    t, d = ys[0].shape[0], ws[0].shape[0] * ws[0].shape[2]
    tm, tn = _blk(t, tm), _blk(ws[0].shape[2], tn)
    y_specs, w_specs, g_specs = _merge_specs(t, d, ys, ws, tm, tn)

    def body(ya, yb, yc, wa, wb, wc, ga, gb, gc, o_ref):
        acc = None
        for y, w, g in ((ya, wa, ga), (yb, wb, gb), (yc, wc, gc)):
            term = _sigmoid(g[...]) * _dot(y[...], w[...])
            acc = term if acc is None else acc + term
        o_ref[...] = acc.astype(o_ref.dtype)

    return pl.pallas_call(
        body, grid=(t // tm, d // tn),
        in_specs=y_specs + w_specs + g_specs,
        out_specs=pl.BlockSpec((tm, tn), lambda i, j: (i, j)),
        out_shape=jax.ShapeDtypeStruct((t, d), BF16),
        compiler_params=_cp(("parallel", "parallel")), name=name,
    )(*ys, *ws, gates, gates, gates)


def _merge_bwd(ys, ws, gates, dmerged, *, name, tm=512, tn=512):
    t, d = ys[0].shape[0], ws[0].shape[0] * ws[0].shape[2]
    tm, tn = _blk(t, tm), _blk(ws[0].shape[2], tn)
    y_specs, w_specs, g_specs = _merge_specs(t, d, ys, ws, tm, tn)
    tile = pl.BlockSpec((tm, tn), lambda i, j: (i, j))

    def body(ya, yb, yc, wa, wb, wc, ga, gb, gc, dm_ref, dua, dub, duc, dga, dgb, dgc):
        dm = dm_ref[...]
        for y, w, g, du, dg in ((ya, wa, ga, dua, dga), (yb, wb, gb, dub, dgb), (yc, wc, gc, duc, dgc)):
            sg = _sigmoid(g[...])
            u = _dot(y[...], w[...])
            du[...] = (dm * sg).astype(du.dtype)
            dg[...] = (dm * u * sg * (1.0 - sg)).astype(dg.dtype)

    return pl.pallas_call(
        body, grid=(t // tm, d // tn),
        in_specs=y_specs + w_specs + g_specs + [tile],
        out_specs=[tile] * 6,
        out_shape=[jax.ShapeDtypeStruct((t, d), BF16)] * 6,
        compiler_params=_cp(("parallel", "parallel")), name=name,
    )(*ys, *ws, gates, gates, gates, dmerged)


def _adamw(w, g, m, v, *, name, tm=256):
    lead = w.ndim - 2
    assert all(s == 1 for s in w.shape[:lead]) and m.shape == w.shape and v.shape == w.shape
    r, c = w.shape[lead:]
    assert g.shape == (r, c)
    tm = _blk(r, tm) if r % 8 == 0 else r
    bc1 = 1.0 - ADAM_B1 ** ADAM_STEP
    bc2 = 1.0 - ADAM_B2 ** ADAM_STEP

    def body(w_ref, g_ref, m_ref, v_ref, d_ref, nm_ref, nv_ref):
        gv = g_ref[...]
        nm = ADAM_B1 * m_ref[...] + (1.0 - ADAM_B1) * gv
        nv = ADAM_B2 * v_ref[...] + (1.0 - ADAM_B2) * (gv * gv)
        d_ref[...] = -ADAM_LR * ((nm / bc1) / (jnp.sqrt(nv / bc2) + ADAM_EPS) + ADAM_WD * w_ref[...])
        nm_ref[...] = nm
        nv_ref[...] = nv

    spec = pl.BlockSpec((None,) * lead + (tm, c), lambda i: (0,) * lead + (i, 0))
    g_spec = pl.BlockSpec((tm, c), lambda i: (i, 0))
    return pl.pallas_call(
        body, grid=(r // tm,), in_specs=[spec, g_spec, spec, spec], out_specs=[spec] * 3,
        out_shape=[jax.ShapeDtypeStruct(w.shape, F32)] * 3,
        compiler_params=_cp(("parallel",)), name=name,
    )(w, g, m, v)


HALO = 8


def _shift_down(cur, prev, j):
    if j == 0:
        return cur
    y = pltpu.roll(cur, j, 0)
    row = lax.broadcasted_iota(jnp.int32, (HALO, cur.shape[1]), 0)
    top = jnp.where(row < j, pltpu.roll(prev, j, 0), y[:HALO])
    return jnp.concatenate([top, y[HALO:]], axis=0)


def _shift_up(cur, nxt, j):
    if j == 0:
        return cur
    tm = cur.shape[0]
    y = pltpu.roll(cur, tm - j, 0)
    row = lax.broadcasted_iota(jnp.int32, (HALO, cur.shape[1]), 0)
    bot = jnp.where(row >= HALO - j, pltpu.roll(nxt, HALO - j, 0), y[tm - HALO:])
    return jnp.concatenate([y[:tm - HALO], bot], axis=0)


def _softplus(x):
    return jnp.maximum(x, 0.0) + jnp.log(1.0 + jnp.exp(-jnp.abs(x)))


def _gdn_pre_fwd(qkvb, conv_w, ab, alog_pad, dt_pad, *, name, ab_blk=0, tm=256):
    t, cw = qkvb.shape
    hd, nh, ck = GDN_HEAD_DIM, GDN_HEADS, GDN_CHUNK
    gw = nh * hd
    tm = _blk(t, tm)
    hb = tm // HALO

    def body(x_ref, xp_ref, w_ref, ab_ref, al_ref, dt_ref, xc_ref, qkvn_ref, aux_ref):
        i = pl.program_id(0)
        cur = x_ref[...]
        prev = jnp.where(i > 0, xp_ref[...], 0.0)
        xc = None
        for tap in range(GDN_CONV):
            term = w_ref[tap:tap + 1, :] * _shift_down(cur, prev, GDN_CONV - 1 - tap)
            xc = term if xc is None else xc + term
        xc_ref[...] = xc
        s = xc * _sigmoid(xc)
        for h in range(2 * nh):
            xh = s[:, h * hd:(h + 1) * hd]
            r = lax.rsqrt(jnp.sum(xh * xh, axis=-1, keepdims=True) + L2_EPS)
            scale = hd ** -0.5 if h < nh else 1.0
            qkvn_ref[:, h * hd:(h + 1) * hd] = xh * (r * scale)
        qkvn_ref[:, 2 * gw:] = s[:, 2 * gw:]
        abv = ab_ref[...]
        lane = lax.broadcasted_iota(jnp.int32, abv.shape, 1)
        g = jnp.where(lane < nh, -jnp.exp(al_ref[...]) * _softplus(abv + dt_ref[...]), 0.0)
        beta = jnp.where((lane >= nh) & (lane < 2 * nh), _sigmoid(abv), 0.0)
        ii = lax.broadcasted_iota(jnp.int32, (tm, tm), 0)
        jj = lax.broadcasted_iota(jnp.int32, (tm, tm), 1)
        tri = jnp.where((ii >= jj) & ((ii ^ jj) < ck), 1.0, 0.0)
        gcum = _dot(tri, g, precision=HI)
        aux_ref[...] = g + beta + pltpu.roll(gcum, 2 * nh, 1)

    row = lambda c: pl.BlockSpec((tm, c), lambda i: (i, 0))
    vec = lambda r, c: pl.BlockSpec((r, c), lambda i: (0, 0))
    return pl.pallas_call(
        body, grid=(t // tm,),
        in_specs=[row(cw), pl.BlockSpec((HALO, cw), lambda i: (jnp.maximum(i * hb - 1, 0), 0)), vec(GDN_CONV, cw),
                  pl.BlockSpec((tm, LANES), lambda i: (i, ab_blk)), vec(1, LANES), vec(1, LANES)],
        out_specs=[row(cw), row(cw), row(LANES)],
        out_shape=[jax.ShapeDtypeStruct((t, cw), F32), jax.ShapeDtypeStruct((t, cw), F32),
                   jax.ShapeDtypeStruct((t, LANES), F32)],
        compiler_params=_cp(("parallel",)), name=name,
    )(qkvb, qkvb, conv_w, ab, alog_pad, dt_pad)


GDN_STEP_CHUNKS = 4


def _bdot(a, b, dims=NN):
    return _dot(a.astype(BF16), b.astype(BF16), dims)


def _split_bf16(x):
    hi = x.astype(BF16)
    return hi, (x - hi.astype(F32)).astype(BF16)


def _dot3(a, b, dims=NN):
    ah, al = _split_bf16(a)
    bh, bl = _split_bf16(b)
    return _dot(ah, bh, dims) + (_dot(ah, bl, dims) + _dot(al, bh, dims))


def _gdn_local(q, k, b, gc, gc_row):
    ck = GDN_CHUNK
    ii = lax.broadcasted_iota(jnp.int32, (ck, ck), 0)
    jj = lax.broadcasted_iota(jnp.int32, (ck, ck), 1)
    lower, strict = ii >= jj, ii > jj
    dmat = jnp.exp(jnp.where(lower, gc - gc_row, -jnp.inf))
    kk = _bdot(k, k, NT)
    lmat = jnp.where(strict, b * kk * dmat, 0.0)
    tinv = jnp.where(ii == jj, 1.0, 0.0) - lmat
    pw = lmat
    for _ in range(int(math.log2(ck)) - 1):
        pw = _dot3(pw, pw)
        tinv = tinv + _dot3(tinv, pw)
    gam = jnp.exp(gc)
    qk = _bdot(q, k, NT)
    gl = gc[ck - 1:ck, :]
    return dict(lower=lower, strict=strict, dmat=dmat, kk=kk, tinv=tinv, gam=gam, qk=qk, mm=qk * dmat,
                kdec=jnp.exp(gl - gc))


def _gdn_head_cols(h):
    return slice(h * GDN_HEAD_DIM, (h + 1) * GDN_HEAD_DIM)


def _gdn_chunk_inputs(x_ref, aux_ref, auxt_ref, g, h):
    nh, ck = GDN_HEADS, GDN_CHUNK
    gw = nh * GDN_HEAD_DIM
    rows = slice(g * ck, (g + 1) * ck)
    cols = _gdn_head_cols(h)
    q = x_ref[rows, cols]
    k = x_ref[rows, gw + cols.start:gw + cols.stop]
    v = x_ref[rows, 2 * gw + cols.start:2 * gw + cols.stop]
    b = aux_ref[rows, nh + h:nh + h + 1]
    gc = aux_ref[rows, 2 * nh + h:2 * nh + h + 1]
    gc_row = auxt_ref[g, 2 * nh + h:2 * nh + h + 1, :]
    return q, k, v, b, gc, gc_row


def _gdn_specs(t, widths, *, reverse=False, step_chunks=None):
    rows = (step_chunks or GDN_STEP_CHUNKS) * GDN_CHUNK
    nsteps = t // rows
    idx = (lambda i: (nsteps - 1 - i, 0)) if reverse else (lambda i: (i, 0))
    return [pl.BlockSpec((rows, w), idx) for w in widths]


def _gdn_local_fwd(qkvn, aux, aux_t, *, name):
    t = qkvn.shape[0]
    hd, nh, ck, gs = GDN_HEAD_DIM, GDN_HEADS, GDN_CHUNK, GDN_STEP_CHUNKS
    gw = nh * hd

    def body(x_ref, aux_ref, auxt_ref, u_ref, w_ref, qd_ref, kd_ref, mm_ref, tinv_ref):
        for g in range(gs):
            rows = slice(g * ck, (g + 1) * ck)
            mms, tinvs = [], []
            for h in range(nh):
                q, k, v, b, gc, gc_row = _gdn_chunk_inputs(x_ref, aux_ref, auxt_ref, g, h)
                lc = _gdn_local(q, k, b, gc, gc_row)
                cols = _gdn_head_cols(h)
                u_ref[rows, cols] = _dot3(lc["tinv"], b * v)
                w_ref[rows, cols] = _dot3(lc["tinv"], (b * lc["gam"]) * k).astype(w_ref.dtype)
                qd_ref[rows, cols] = (lc["gam"] * q).astype(qd_ref.dtype)
                kd_ref[rows, cols] = (lc["kdec"] * k).astype(kd_ref.dtype)
                mms.append(lc["mm"])
                tinvs.append(lc["tinv"])
            mm_ref[rows, :] = jnp.concatenate(mms, axis=1).astype(mm_ref.dtype)
            tinv_ref[rows, :] = jnp.concatenate(tinvs, axis=1)

    sq = nh * ck
    return pl.pallas_call(
        body, grid=(t // (gs * ck),),
        in_specs=_gdn_specs(t, (3 * gw, LANES)) + [pl.BlockSpec((gs, 16, ck), lambda i: (i, 0, 0))],
        out_specs=_gdn_specs(t, (gw, gw, gw, gw, sq, sq)),
        out_shape=[jax.ShapeDtypeStruct((t, gw), F32)] + [jax.ShapeDtypeStruct((t, gw), BF16)] * 3
        + [jax.ShapeDtypeStruct((t, sq), BF16), jax.ShapeDtypeStruct((t, sq), F32)],
        compiler_params=_cp(("parallel",)), name=name,
    )(qkvn, aux, aux_t)


def _gdn_seq_fwd(u, w, qd, kd, mm, aux, *, name):
    t = u.shape[0]
    hd, nh, ck, gs = GDN_HEAD_DIM, GDN_HEADS, GDN_CHUNK, GDN_STEP_CHUNKS
    gw = nh * hd
    sq = nh * ck

    def body(u_ref, w_ref, qd_ref, kd_ref, mm_ref, aux_ref, o_ref, vn_ref, sall_ref, s_ref):
        @pl.when(pl.program_id(0) == 0)
        def _():
            s_ref[...] = jnp.zeros_like(s_ref)

        for g in range(gs):
            rows = slice(g * ck, (g + 1) * ck)
            last = (g + 1) * ck - 1
            for h in range(nh):
                cols = _gdn_head_cols(h)
                st = s_ref[h]
                sall_ref[g, h] = st
                stb = st.astype(BF16)
                vn = u_ref[rows, cols] - _dot(w_ref[rows, cols], stb)
                vnb = vn.astype(BF16)
                vn_ref[rows, cols] = vnb
                o_ref[rows, cols] = _dot(qd_ref[rows, cols], stb) + _dot(mm_ref[rows, h * ck:(h + 1) * ck], vnb)
                gam_c = jnp.exp(aux_ref[last:last + 1, 2 * nh + h:2 * nh + h + 1])
                s_ref[h] = gam_c * st + _dot(kd_ref[rows, cols], vnb, TN)

    return pl.pallas_call(
        body, grid=(t // (gs * ck),),
        in_specs=_gdn_specs(t, (gw, gw, gw, gw, sq, LANES)),
        out_specs=_gdn_specs(t, (gw, gw)) + [pl.BlockSpec((gs, nh, hd, hd), lambda i: (i, 0, 0, 0))],
        out_shape=[jax.ShapeDtypeStruct((t, gw), F32), jax.ShapeDtypeStruct((t, gw), BF16),
                   jax.ShapeDtypeStruct((t // ck, nh, hd, hd), F32)],
        scratch_shapes=[pltpu.VMEM((nh, hd, hd), F32)],
        compiler_params=_cp(("arbitrary",)), name=name,
    )(u, w, qd, kd, mm, aux)


def _gdn_seq_bwd(do, w, qd, kd, mm, vn, s_all, aux, *, name):
    t = do.shape[0]
    hd, nh, ck, gs = GDN_HEAD_DIM, GDN_HEADS, GDN_CHUNK, GDN_STEP_CHUNKS
    gw = nh * hd
    sq = nh * ck
    nsteps = t // (gs * ck)

    def body(do_ref, w_ref, qd_ref, kd_ref, mm_ref, vn_ref, sall_ref, aux_ref, dvn_ref, dqd_ref, dkd_ref, dw_ref,
             dlast_ref, ds_ref):
        @pl.when(pl.program_id(0) == 0)
        def _():
            ds_ref[...] = jnp.zeros_like(ds_ref)

        lane = lax.broadcasted_iota(jnp.int32, (ck, LANES), 1)
        rowi = lax.broadcasted_iota(jnp.int32, (ck, LANES), 0)
        for g in reversed(range(gs)):
            rows = slice(g * ck, (g + 1) * ck)
            last = (g + 1) * ck - 1
            dlast = jnp.zeros((ck, LANES), F32)
            for h in range(nh):
                cols = _gdn_head_cols(h)
                st = sall_ref[g, h]
                stb = st.astype(BF16)
                dsn = ds_ref[h]
                dsb = dsn.astype(BF16)
                dob = do_ref[rows, cols].astype(BF16)
                dvn = _dot(mm_ref[rows, h * ck:(h + 1) * ck], dob, TN) + _dot(kd_ref[rows, cols], dsb)
                dvb = dvn.astype(BF16)
                dvn_ref[rows, cols] = dvn
                dqd_ref[rows, cols] = _dot(dob, stb, NT)
                dkd_ref[rows, cols] = _dot(vn_ref[rows, cols], dsb, NT)
                dw_ref[rows, cols] = -_dot(dvb, stb, NT)
                gam_c = jnp.exp(aux_ref[last:last + 1, 2 * nh + h:2 * nh + h + 1])
                dgam_c = jnp.sum(jnp.sum(dsn * st, axis=1, keepdims=True), axis=0, keepdims=True)
                dlast = dlast + jnp.where((rowi == ck - 1) & (lane == h), gam_c * dgam_c, 0.0)
                ds_ref[h] = _dot(qd_ref[rows, cols], dob, TN) + gam_c * dsn - _dot(w_ref[rows, cols], dvb, TN)
            dlast_ref[rows, :] = dlast

    return pl.pallas_call(
        body, grid=(nsteps,),
        in_specs=_gdn_specs(t, (gw, gw, gw, gw, sq, gw), reverse=True)
        + [pl.BlockSpec((gs, nh, hd, hd), lambda i: (nsteps - 1 - i, 0, 0, 0))] + _gdn_specs(t, (LANES,), reverse=True),
        out_specs=_gdn_specs(t, (gw, gw, gw, gw, LANES), reverse=True),
        out_shape=[jax.ShapeDtypeStruct((t, gw), F32)] * 4 + [jax.ShapeDtypeStruct((t, LANES), F32)],
        scratch_shapes=[pltpu.VMEM((nh, hd, hd), F32)],
        compiler_params=_cp(("arbitrary",)), name=name,
    )(do, w, qd, kd, mm, vn, s_all, aux)


def _gdn_local_bwd(qkvn, aux, aux_t, tinv, u, w, vn, do, dvn, dqd, dkd, dw, dlast, *, name):
    t = qkvn.shape[0]
    hd, nh, ck, gs = GDN_HEAD_DIM, GDN_HEADS, GDN_CHUNK, GDN_STEP_CHUNKS
    gw = nh * hd
    sq = nh * ck

    def body(x_ref, aux_ref, auxt_ref, tinv_ref, u_ref, w_ref, vn_ref, do_ref, dvn_ref, dqd_ref, dkd_ref, dw_ref,
             dlast_ref, dx_ref, daux_ref):
        lane = lax.broadcasted_iota(jnp.int32, (ck, LANES), 1)
        ones = jnp.ones((ck, LANES), F32)
        ii = lax.broadcasted_iota(jnp.int32, (ck, ck), 0)
        jj = lax.broadcasted_iota(jnp.int32, (ck, ck), 1)
        suffix = jnp.where(jj >= ii, 1.0, 0.0)
        for g in range(gs):
            rows = slice(g * ck, (g + 1) * ck)
            dgc_all = dlast_ref[rows, :]
            db_all = jnp.zeros((ck, LANES), F32)
            for h in range(nh):
                cols = _gdn_head_cols(h)
                q, k, v, b, gc, gc_row = _gdn_chunk_inputs(x_ref, aux_ref, auxt_ref, g, h)
                lc = _gdn_local(q, k, b, gc, gc_row)
                dmat, kk, gam, qk, kdec = (lc[key] for key in ("dmat", "kk", "gam", "qk", "kdec"))
                tinv_h = tinv_ref[rows, h * ck:(h + 1) * ck]
                u_h, w_h, vn_h = u_ref[rows, cols], w_ref[rows, cols], vn_ref[rows, cols]
                dqd_h, dkd_h = dqd_ref[rows, cols], dkd_ref[rows, cols]
                dm = jnp.where(lc["lower"], _bdot(do_ref[rows, cols], vn_h, NT), 0.0)
                drv = _dot3(tinv_h, dvn_ref[rows, cols], TN)
                drk = _dot3(tinv_h, dw_ref[rows, cols], TN)
                da = jnp.where(lc["strict"], -(_bdot(drv, u_h, NT) + _bdot(drk, w_h, NT)), 0.0)
                rs_rk = jnp.sum(drk * k, axis=-1, keepdims=True)
                db = (jnp.sum(drv * v, axis=-1, keepdims=True) + gam * rs_rk
                      + jnp.sum(da * kk * dmat, axis=-1, keepdims=True))
                e_mat = da * dmat * b
                dmd = dm * dmat
                dx_ref[rows, cols] = _bdot(dmd, k) + gam * dqd_h
                dx_ref[rows, gw + cols.start:gw + cols.stop] = (
                    (b * gam) * drk + _bdot(e_mat, k) + _bdot(e_mat, k, TN) + _bdot(dmd, q, TN) + kdec * dkd_h)
                dx_ref[rows, 2 * gw + cols.start:2 * gw + cols.stop] = b * drv
                f_mat = da * (b * kk) * dmat + dm * qk * dmat
                e_vec = jnp.sum(dkd_h * (kdec * k), axis=-1, keepdims=True)
                dgc = (b * gam * rs_rk + gam * jnp.sum(dqd_h * q, axis=-1, keepdims=True)
                       + jnp.sum(f_mat, axis=-1, keepdims=True) - _dot3(f_mat, ones, TN)[:, 0:1] - e_vec)
                is_last = lax.broadcasted_iota(jnp.int32, (ck, 1), 0) == ck - 1
                dgc = dgc + jnp.where(is_last, jnp.sum(e_vec, axis=0, keepdims=True), 0.0)
                dgc_all = dgc_all + jnp.where(lane == h, dgc, 0.0)
                db_all = db_all + jnp.where(lane == nh + h, db, 0.0)
            daux_ref[rows, :] = _dot3(suffix, dgc_all) + db_all

    return pl.pallas_call(
        body, grid=(t // (gs * ck),),
        in_specs=_gdn_specs(t, (3 * gw, LANES)) + [pl.BlockSpec((gs, 16, ck), lambda i: (i, 0, 0))]
        + _gdn_specs(t, (sq, gw, gw, gw, gw, gw, gw, gw, gw, LANES)),
        out_specs=_gdn_specs(t, (3 * gw, LANES)),
        out_shape=[jax.ShapeDtypeStruct((t, 3 * gw), F32), jax.ShapeDtypeStruct((t, LANES), F32)],
        compiler_params=_cp(("parallel",)), name=name,
    )(qkvn, aux, aux_t, tinv, u, w, vn, do, dvn, dqd, dkd, dw, dlast)


def _gdn_pre_bwd1(xc, dqkvn, daux, ab, alog_pad, dt_pad, *, name, ab_blk=0, tm=256):
    t, cw = xc.shape
    hd, nh = GDN_HEAD_DIM, GDN_HEADS
    gw = nh * hd
    tm = _blk(t, tm)

    def body(xc_ref, dy_ref, daux_ref, ab_ref, al_ref, dt_ref, dxc_ref, dab_ref, dal_ref, ddt_ref):
        i = pl.program_id(0)
        xc = xc_ref[...]
        sg = _sigmoid(xc)
        s = xc * sg
        dsilu = sg * (1.0 + xc * (1.0 - sg))
        for h in range(2 * nh):
            xh = s[:, h * hd:(h + 1) * hd]
            scale = hd ** -0.5 if h < nh else 1.0
            dyh = dy_ref[:, h * hd:(h + 1) * hd] * scale
            r = lax.rsqrt(jnp.sum(xh * xh, axis=-1, keepdims=True) + L2_EPS)
            dxh = r * dyh - xh * (r * r * r) * jnp.sum(dyh * xh, axis=-1, keepdims=True)
            dxc_ref[:, h * hd:(h + 1) * hd] = dxh * dsilu[:, h * hd:(h + 1) * hd]
        dxc_ref[:, 2 * gw:] = dy_ref[:, 2 * gw:] * dsilu[:, 2 * gw:]
        abv = ab_ref[...]
        dauxv = daux_ref[...]
        lane = lax.broadcasted_iota(jnp.int32, abv.shape, 1)
        is_a = lane < nh
        is_b = (lane >= nh) & (lane < 2 * nh)
        pre = abv + dt_ref[...]
        neg_ea = -jnp.exp(al_ref[...])
        d_a = jnp.where(is_a, dauxv * neg_ea * _sigmoid(pre), 0.0)
        beta = _sigmoid(abv)
        d_b = jnp.where(is_b, dauxv * beta * (1.0 - beta), 0.0)
        dab_ref[...] = (d_a + d_b).astype(dab_ref.dtype)
        dal = jnp.sum(jnp.where(is_a, dauxv * neg_ea * _softplus(pre), 0.0), axis=0, keepdims=True)
        ddt = jnp.sum(d_a, axis=0, keepdims=True)

        @pl.when(i == 0)
        def _():
            dal_ref[...] = dal
            ddt_ref[...] = ddt

        @pl.when(i > 0)
        def _():
            dal_ref[...] += dal
            ddt_ref[...] += ddt

    row = lambda c: pl.BlockSpec((tm, c), lambda i: (i, 0))
    vec = pl.BlockSpec((1, LANES), lambda i: (0, 0))
    return pl.pallas_call(
        body, grid=(t // tm,),
        in_specs=[row(cw), row(cw), row(LANES), pl.BlockSpec((tm, LANES), lambda i: (i, ab_blk)), vec, vec],
        out_specs=[row(cw), row(LANES), vec, vec],
        out_shape=[jax.ShapeDtypeStruct((t, cw), F32), jax.ShapeDtypeStruct((t, LANES), BF16),
                   jax.ShapeDtypeStruct((1, LANES), F32), jax.ShapeDtypeStruct((1, LANES), F32)],
        compiler_params=_cp(("arbitrary",)), name=name,
    )(xc, dqkvn, daux, ab, alog_pad, dt_pad)


def _gdn_pre_bwd2(dxc, qkvb, conv_w, *, name, tm=256):
    t, cw = dxc.shape
    tm = _blk(t, tm)
    hb = tm // HALO
    nblk = t // tm

    def body(d_ref, dn_ref, x_ref, xp_ref, w_ref, dx_ref, dw_ref):
        i = pl.program_id(0)
        dcur = d_ref[...]
        dnxt = jnp.where(i < nblk - 1, dn_ref[...], 0.0)
        cur = x_ref[...]
        prev = jnp.where(i > 0, xp_ref[...], 0.0)
        dx = None
        dws = []
        for tap in range(GDN_CONV):
            j = GDN_CONV - 1 - tap
            term = w_ref[tap:tap + 1, :] * _shift_up(dcur, dnxt, j)
            dx = term if dx is None else dx + term
            dws.append(jnp.sum(dcur * _shift_down(cur, prev, j), axis=0, keepdims=True))
        dx_ref[...] = dx.astype(dx_ref.dtype)
        dw = jnp.concatenate(dws, axis=0)

        @pl.when(i == 0)
        def _():
            dw_ref[...] = dw

        @pl.when(i > 0)
        def _():
            dw_ref[...] += dw

    row = pl.BlockSpec((tm, cw), lambda i: (i, 0))
    wsp = pl.BlockSpec((GDN_CONV, cw), lambda i: (0, 0))
    return pl.pallas_call(
        body, grid=(nblk,),
        in_specs=[row, pl.BlockSpec((HALO, cw), lambda i: (jnp.minimum((i + 1) * hb, t // HALO - 1), 0)),
                  row, pl.BlockSpec((HALO, cw), lambda i: (jnp.maximum(i * hb - 1, 0), 0)), wsp],
        out_specs=[row, wsp],
        out_shape=[jax.ShapeDtypeStruct((t, cw), BF16), jax.ShapeDtypeStruct((GDN_CONV, cw), F32)],
        compiler_params=_cp(("arbitrary",)), name=name,
    )(dxc, dxc, qkvb, qkvb, conv_w)


def _gdn_post_fwd(o, z, norm_w, *, name, tm=512):
    t, gw = o.shape
    hd, nh = GDN_HEAD_DIM, GDN_HEADS
    tm = _blk(t, tm)

    def body(o_ref, z_ref, w_ref, y_ref):
        zv = z_ref[...]
        sz = zv * _sigmoid(zv)
        for h in range(nh):
            oh = o_ref[:, h * hd:(h + 1) * hd]
            r = lax.rsqrt(jnp.mean(oh * oh, axis=-1, keepdims=True) + RMS_EPS)
            y_ref[:, h * hd:(h + 1) * hd] = (oh * r * w_ref[...] * sz[:, h * hd:(h + 1) * hd]).astype(y_ref.dtype)

    row = pl.BlockSpec((tm, gw), lambda i: (i, 0))
    return pl.pallas_call(
        body, grid=(t // tm,), in_specs=[row, row, pl.BlockSpec((1, hd), lambda i: (0, 0))], out_specs=row,
        out_shape=jax.ShapeDtypeStruct((t, gw), BF16), compiler_params=_cp(("parallel",)), name=name,
    )(o, z, norm_w)


def _gdn_post_bwd(dy, o, z, norm_w, *, name, tm=512):
    t, gw = o.shape
    hd, nh = GDN_HEAD_DIM, GDN_HEADS
    tm = _blk(t, tm)

    def body(dy_ref, o_ref, z_ref, w_ref, do_ref, dz_ref, dw_ref):
        i = pl.program_id(0)
        zv = z_ref[...]
        sg = _sigmoid(zv)
        sz = zv * sg
        dsz = sg * (1.0 + zv * (1.0 - sg))
        dw = None
        for h in range(nh):
            sl = slice(h * hd, (h + 1) * hd)
            oh = o_ref[:, sl]
            dyh = dy_ref[:, sl].astype(F32)
            r = lax.rsqrt(jnp.mean(oh * oh, axis=-1, keepdims=True) + RMS_EPS)
            xh = oh * r
            dz_ref[:, sl] = (dyh * xh * w_ref[...] * dsz[:, sl]).astype(dz_ref.dtype)
            dn = dyh * sz[:, sl]
            dxh = dn * w_ref[...]
            do_ref[:, sl] = r * (dxh - xh * jnp.mean(dxh * xh, axis=-1, keepdims=True))
            part = jnp.sum(dn * xh, axis=0, keepdims=True)
            dw = part if dw is None else dw + part

        @pl.when(i == 0)
        def _():
            dw_ref[...] = dw

        @pl.when(i > 0)
        def _():
            dw_ref[...] += dw

    row = pl.BlockSpec((tm, gw), lambda i: (i, 0))
    vec = pl.BlockSpec((1, hd), lambda i: (0, 0))
    return pl.pallas_call(
        body, grid=(t // tm,), in_specs=[row, row, row, vec], out_specs=[row, row, vec],
        out_shape=[jax.ShapeDtypeStruct((t, gw), F32), jax.ShapeDtypeStruct((t, gw), BF16),
                   jax.ShapeDtypeStruct((1, hd), F32)],
        compiler_params=_cp(("arbitrary",)), name=name,
    )(dy, o, z, norm_w)


IN_NAMES = ("q_a", "kv_a", "qkv_b", "ab", "z", "q_c", "gates")
CAT_NAMES = ("gates", "q_a", "qkv_b", "z", "q_c", "kv_a", "ab")
AB_PAD = 256


def _in_widths(d):
    gw = GDN_HEADS * GDN_HEAD_DIM
    return dict(q_a=SWA_Q_HEADS * SWA_HEAD_DIM, kv_a=2 * SWA_KV_HEADS * SWA_HEAD_DIM, qkv_b=3 * gw, ab=2 * GDN_HEADS,
                z=gw, q_c=XA_HEADS * XA_HEAD_DIM, gates=3 * d)


def _ranges(names, widths):
    out, start = {}, 0
    for k in names:
        out[k] = (start, widths[k])
        start += widths[k]
    return out, start


def _cat_ranges(d):
    widths = dict(_in_widths(d), ab=AB_PAD)
    return _ranges(CAT_NAMES, widths)


def _to_cat(shards):
    ns, d, n = shards.shape
    src, _ = _ranges(IN_NAMES, _in_widths(d))
    cols = []
    for k in CAT_NAMES:
        lo, hi = src[k][0], src[k][0] + src[k][1]
        for s in range(ns):
            a, b = max(lo, s * n), min(hi, (s + 1) * n)
            if a < b:
                cols.append(shards[s][:, a - s * n:b - s * n])
    cols.append(jnp.zeros((d, AB_PAD - src["ab"][1]), shards.dtype))
    return jnp.concatenate(cols, axis=1)


def _from_cat(w_cat):
    d = w_cat.shape[0]
    src, total = _ranges(IN_NAMES, _in_widths(d))
    cat, _ = _cat_ranges(d)
    n = total // N_SHARDS
    shards = []
    for s in range(N_SHARDS):
        pieces = []
        for k in IN_NAMES:
            a, b = max(s * n, src[k][0]), min((s + 1) * n, src[k][0] + src[k][1])
            if a < b:
                pieces.append(w_cat[:, cat[k][0] + a - src[k][0]:cat[k][0] + b - src[k][0]])
        shards.append(jnp.concatenate(pieces, axis=1))
    return jnp.stack(shards)


def _pad_cols(a, width):
    return jnp.pad(a, ((0, 0), (0, width - a.shape[1])))


def _relu2_epilogue(acc):
    r = jnp.maximum(acc, 0.0)
    return acc, r * r


def _add_epilogue(acc, res):
    return (acc + res,)


def _drelu2_epilogue(acc, u):
    return (acc * (2.0 * jnp.maximum(u.astype(F32), 0.0)),)


def _local_step(x, mem, tgt, wts, small):
    t, d = x.shape
    nh = GDN_HEADS
    w_cat = wts["w_cat"]
    cat, cat_w = _cat_ranges(d)
    assert w_cat.shape == (d, cat_w)
    alog_pad = _pad_cols(small["a_log"], LANES)
    dt_pad = _pad_cols(small["dt_bias"], LANES)
    kvw = cat["kv_a"][1]
    assert cat["ab"][0] == cat["kv_a"][0] + kvw
    ab_blk = kvw // LANES

    n = _rms_fwd(x, small["g_mix"], name="rms_mix")
    q_a = _mm(n, w_cat, b_window=cat["q_a"], out_dtypes=(BF16,), name="in_q_a")
    kv_a, ab = _mm(n, w_cat, b_window=(cat["kv_a"][0], kvw + AB_PAD), out_dtypes=(BF16, F32), name="in_kv_ab")
    qkvb = _mm(n, w_cat, b_window=cat["qkv_b"], tn=512, name="in_qkv_b")
    z = _mm(n, w_cat, b_window=cat["z"], name="in_z")
    q_c = _mm(n, w_cat, b_window=cat["q_c"], out_dtypes=(BF16,), name="in_q_c")
    gates = _mm(n, w_cat, b_window=cat["gates"], name="in_gates")
    y_a, lse = _swa_fwd(q_a, kv_a, small["sinks"], name="swa_fwd")
    xc, qkvn, aux = _gdn_pre_fwd(qkvb, small["conv_w"], ab, alog_pad, dt_pad, ab_blk=ab_blk, name="gdn_pre_fwd")
    aux_t = aux[:, :16].reshape(t // GDN_CHUNK, GDN_CHUNK, 16).transpose(0, 2, 1)
    gdn_u, gdn_w, gdn_qd, gdn_kd, gdn_mm, gdn_tinv = _gdn_local_fwd(qkvn, aux, aux_t, name="gdn_local_fwd")
    o_b, gdn_vn, s_all = _gdn_seq_fwd(gdn_u, gdn_w, gdn_qd, gdn_kd, gdn_mm, aux, name="gdn_seq_fwd")
    y_b = _gdn_post_fwd(o_b, z, small["gdn_norm_w"], name="gdn_post_fwd")
    nmem = _rms_fwd(mem, small["g_mem"], name="rms_mem")
    mkv = _mm(nmem, wts["w_mem_kv"], out_dtypes=(BF16,), name="mem_kv")
    y_c = _xa_fwd(q_c, mkv, name="xa_fwd")
    ys = (y_a, y_b, y_c)
    w_ups = (wts["w_swa_up"], wts["w_gdn_up"], wts["w_xa_up"])
    merged = _merge_fwd(ys, w_ups, gates, name="merge_fwd")
    h1 = _mm(merged, wts["w_out"], extras=(x,), epilogue=_add_epilogue, name="out_proj")
    n2 = _rms_fwd(h1, small["g_mlp"], name="rms_mlp")
    u, act = _mm(n2, wts["w_mlp_in"], b_sharded=True, out_dtypes=(BF16, BF16), epilogue=_relu2_epilogue, name="mlp_in")
    h2 = _mm(act, wts["w_mlp_out"], extras=(h1,), epilogue=_add_epilogue, name="mlp_out")
    dh2, dg_final, loss = _final_loss(h2, small["g_final"], tgt, name="final_loss")

    grads = {"g_final": dg_final}
    du = _mm(dh2, wts["w_mlp_out"], tb=True, out_dtypes=(BF16,), extras=(u,), epilogue=_drelu2_epilogue, name="d_mlp_act")
    grads["w_mlp_out"] = _mm(act, dh2, ta=True, out_dtypes=(BF16,), name="dw_mlp_out")
    grads["w_mlp_in"] = _mm(n2, du, ta=True, out_sharded=True, out_dtypes=(BF16,), name="dw_mlp_in")
    dn2 = _mm(du, wts["w_mlp_in"], tb=True, b_sharded=True, name="d_mlp_in")
    dh1, grads["g_mlp"] = _rms_bwd(dn2, h1, small["g_mlp"], dh2, name="rms_mlp_bwd")
    dmerged = _mm(dh1, wts["w_out"], tb=True, name="d_out_proj")
    grads["w_out"] = _mm(merged, dh1, ta=True, out_dtypes=(BF16,), name="dw_out")
    dus_and_dgates = _merge_bwd(ys, w_ups, gates, dmerged, name="merge_bwd")
    dus, dgates = dus_and_dgates[:3], dus_and_dgates[3:]
    dys = []
    for y, du_i, w_up, key in zip(ys, dus, w_ups, ("w_swa_up", "w_gdn_up", "w_xa_up")):
        dys.append(_mm(du_i, w_up, tb=True, b_sharded=True, out_dtypes=(BF16,), name="d_" + key))
        grads[key] = _mm(y, du_i, ta=True, out_sharded=True, out_dtypes=(BF16,), name="dw_" + key[2:])
    dq_a, dkv_a, grads["sinks"] = _swa_bwd(q_a, kv_a, small["sinks"], y_a, lse, dys[0], name="swa_bwd")
    do_b, dz, grads["gdn_norm_w"] = _gdn_post_bwd(dys[1], o_b, z, small["gdn_norm_w"], name="gdn_post_bwd")
    dvn, dqd, dkd, dw_, dlast = _gdn_seq_bwd(do_b, gdn_w, gdn_qd, gdn_kd, gdn_mm, gdn_vn, s_all, aux, name="gdn_seq_bwd")
    dqkvn, daux = _gdn_local_bwd(qkvn, aux, aux_t, gdn_tinv, gdn_u, gdn_w, gdn_vn, do_b, dvn, dqd, dkd, dw_, dlast,
                                 name="gdn_local_bwd")
    dxc, dab, dalog, ddt = _gdn_pre_bwd1(xc, dqkvn, daux, ab, alog_pad, dt_pad, ab_blk=ab_blk, name="gdn_pre_bwd1")
    grads["a_log"], grads["dt_bias"] = dalog[:, :nh], ddt[:, :nh]
    dqkvb, grads["conv_w"] = _gdn_pre_bwd2(dxc, qkvb, small["conv_w"], name="gdn_pre_bwd2")
    dq_c, dmkv = _xa_bwd(q_c, mkv, dys[2], name="xa_bwd")
    grads["w_mem_kv"] = _mm(nmem, dmkv, ta=True, out_dtypes=(BF16,), name="dw_mem_kv")
    dnmem = _mm(dmkv, wts["w_mem_kv"], tb=True, name="d_mem_kv")
    _, grads["g_mem"] = _rms_bwd(dnmem, mem, small["g_mem"], jnp.zeros_like(mem), name="rms_mem_bwd")
    dp = jnp.concatenate([*dgates, dq_a, dqkvb, dz, dq_c, dkv_a, dab, jnp.zeros((t, AB_PAD - LANES), BF16)], axis=1)
    dn = _mm(dp, w_cat, tb=True, name="d_in_proj")
    grads["w_cat"] = _mm(n, dp, ta=True, out_dtypes=(BF16,), name="dw_in")
    dx, grads["g_mix"] = _rms_bwd(dn, x, small["g_mix"], dh1, name="rms_mix_bwd")
    return loss, dx, grads


HBM_SPEC = pl.BlockSpec(memory_space=pltpu.HBM)
VMEM_SPEC = pl.BlockSpec(memory_space=pltpu.VMEM)
N_CHIPS = N_SHARDS
N_DEV = 8
DMA_CHUNK_BYTES = 1 << 20


def _place():
    return lax.axis_index("x"), lax.axis_index("y"), lax.axis_index("c")


def _other_chips(x, y):
    return [(1 - x, y), (x, 1 - y), (1 - x, 1 - y)]


def _n_chunks(rows, row_bytes):
    n = 1
    while rows % (2 * n) == 0 and (rows // (2 * n)) % 16 == 0 and (rows // n) * row_bytes > DMA_CHUNK_BYTES:
        n *= 2
    return n


def _sem_scratch(n_remote, n_local):
    return [pltpu.SemaphoreType.DMA((max(n_remote, 1),)), pltpu.SemaphoreType.DMA((max(n_remote, 1),)),
            pltpu.SemaphoreType.DMA((max(n_local, 1),))]


def _all_gather_weights(shards, *, name):
    nw = len(shards)
    plan = []
    for i, s in enumerate(shards):
        rh = s.shape[0] // 2
        nch = _n_chunks(rh, s.shape[1] * s.dtype.itemsize)
        plan += [(i, q * (rh // nch), rh // nch) for q in range(nch)]
    n_ici = 3 * len(plan)

    def body(*refs):
        w_refs, out_refs = refs[:nw], refs[nw:2 * nw]
        send_sems, recv_sems, local_sems = refs[2 * nw:]
        x, y, c = _place()
        sibling = (x, y, 1 - c)
        chips = _other_chips(x, y)

        def copy(k, src, dst, to):
            return pltpu.make_async_remote_copy(src_ref=src, dst_ref=dst, send_sem=send_sems.at[k],
                                                recv_sem=recv_sems.at[k], device_id=to, device_id_type=MESH)

        local = []
        for i, r0, nr in plan:
            for half in range(2):
                rows = pl.ds(half * (shards[i].shape[0] // 2) + r0, nr)
                local.append(pltpu.make_async_copy(w_refs[i].at[rows], out_refs[i].at[2 * x + y, rows],
                                                   local_sems.at[len(local)]))
        for cp in local:
            cp.start()
        first, arrive, passed, from_sibling = [], [], [], []
        for i, r0, nr in plan:
            rh = shards[i].shape[0] // 2
            mine = pl.ds(c * rh + r0, nr)
            other = pl.ds((1 - c) * rh + r0, nr)
            for chip in chips:
                k = len(first)
                cid = 2 * chip[0] + chip[1]
                first.append(copy(k, w_refs[i].at[mine], out_refs[i].at[2 * x + y, mine], (*chip, c)))
                arrive.append(copy(k, out_refs[i].at[cid, mine], out_refs[i].at[cid, mine], (x, y, c)))
                passed.append(copy(n_ici + k, out_refs[i].at[cid, mine], out_refs[i].at[cid, mine], sibling))
                from_sibling.append(copy(n_ici + k, out_refs[i].at[cid, other], out_refs[i].at[cid, other], (x, y, c)))
        for cp in first:
            cp.start()
        for k in range(n_ici):
            arrive[k].wait_recv()
            passed[k].start()
        for cp in from_sibling:
            cp.wait_recv()
        for cp in first + passed:
            cp.wait_send()
        for cp in local:
            cp.wait()

    return pl.pallas_call(
        body, out_shape=[jax.ShapeDtypeStruct((N_CHIPS, *s.shape), s.dtype) for s in shards],
        in_specs=[HBM_SPEC] * nw, out_specs=[HBM_SPEC] * nw,
        scratch_shapes=_sem_scratch(2 * n_ici, 2 * len(plan)), name=name,
    )(*shards)


def _exchange_call(copies_of, n_copies, ins, out_shapes, *, name, aliases=None):
    n_in, n_out = len(ins), len(out_shapes)

    def body(*refs):
        in_refs, out_refs = refs[:n_in], refs[n_in:n_in + n_out]
        send_sems, recv_sems, _ = refs[n_in + n_out:]
        todo = copies_of(in_refs, out_refs, _place())
        assert len(todo) == n_copies
        cps = [pltpu.make_async_remote_copy(src_ref=src, dst_ref=dst, send_sem=send_sems.at[k], recv_sem=recv_sems.at[k],
                                            device_id=to, device_id_type=MESH) for k, (src, dst, to) in enumerate(todo)]
        for cp in cps:
            cp.start()
        for cp in cps:
            cp.wait()

    return pl.pallas_call(
        body, out_shape=out_shapes, in_specs=[HBM_SPEC] * n_in, out_specs=[HBM_SPEC] * n_out,
        scratch_shapes=_sem_scratch(n_copies, 0), input_output_aliases=aliases or {}, name=name,
    )(*ins)


def _half_chunks(arrs, row_axis):
    plan = []
    for i, a in enumerate(arrs):
        rh = a.shape[row_axis] // 2
        row_bytes = a.dtype.itemsize * math.prod(a.shape) // a.shape[row_axis]
        nch = _n_chunks(rh, row_bytes)
        plan += [(i, q * (rh // nch), rh // nch) for q in range(nch)]
    return plan


def _sibling_halves(gs, *, name):
    plan = _half_chunks(gs, 1)

    def copies_of(in_refs, out_refs, place):
        x, y, c = place
        out = []
        for i, r0, nr in plan:
            rh = gs[i].shape[1] // 2
            out.append((in_refs[i].at[:, pl.ds((1 - c) * rh + r0, nr), :], out_refs[i].at[:, pl.ds(r0, nr), :],
                        (x, y, 1 - c)))
        return out

    shapes = [jax.ShapeDtypeStruct((g.shape[0], g.shape[1] // 2, g.shape[2]), g.dtype) for g in gs]
    return _exchange_call(copies_of, len(plan), gs, shapes, name=name)


def _chip_exchange(s1s, *, name):
    plan = _half_chunks([jax.ShapeDtypeStruct((2 * s.shape[1], s.shape[2]), s.dtype) for s in s1s], 0)

    def copies_of(in_refs, out_refs, place):
        x, y, c = place
        out = []
        for i, r0, nr in plan:
            for j, chip in enumerate(_other_chips(x, y)):
                out.append((in_refs[i].at[2 * chip[0] + chip[1], pl.ds(r0, nr), :], out_refs[i].at[j, pl.ds(r0, nr), :],
                            (*chip, c)))
        return out

    shapes = [jax.ShapeDtypeStruct((3, *s.shape[1:]), s.dtype) for s in s1s]
    return _exchange_call(copies_of, 3 * len(plan), s1s, shapes, name=name)


def _join_halves(gs, *, name):
    plan = _half_chunks(gs, 0)

    def copies_of(in_refs, out_refs, place):
        x, y, c = place
        out = []
        for i, r0, nr in plan:
            rows = out_refs[i].at[pl.ds(c * (gs[i].shape[0] // 2) + r0, nr), :]
            out.append((rows, rows, (x, y, 1 - c)))
        return out

    shapes = [jax.ShapeDtypeStruct(g.shape, g.dtype) for g in gs]
    return _exchange_call(copies_of, len(plan), gs, shapes, name=name, aliases={i: i for i in range(len(gs))})


def _row_block(rows, cols):
    tb = rows
    while tb % 32 == 0 and tb * cols * 4 > (2 << 20):
        tb //= 2
    return tb


def _pair_sum(g, sib, core, *, name):
    ns, r, c = g.shape
    rh = r // 2
    tb = _row_block(rh, c)
    nb = rh // tb

    def body(core_ref, g_ref, s_ref, o_ref):
        o_ref[...] = (g_ref[...].astype(F32) + s_ref[...].astype(F32)).astype(o_ref.dtype)

    mine = pl.BlockSpec((None, tb, c), lambda s, i, core_ref: (s, core_ref[0] * nb + i, 0))
    half = pl.BlockSpec((None, tb, c), lambda s, i, core_ref: (s, i, 0))
    return pl.pallas_call(
        body, grid_spec=pltpu.PrefetchScalarGridSpec(num_scalar_prefetch=1, grid=(ns, nb), in_specs=[mine, half],
                                                     out_specs=half),
        out_shape=jax.ShapeDtypeStruct((ns, rh, c), BF16), compiler_params=_cp(("parallel", "parallel")), name=name,
    )(core, g, sib)


def _chip_sum(s1, rcv, where, *, name):
    _, rh, c = s1.shape
    tb = _row_block(rh, c)
    nb = rh // tb

    def body(where_ref, own_ref, r0_ref, r1_ref, r2_ref, o_ref):
        acc = own_ref[...].astype(F32)
        for r in (r0_ref, r1_ref, r2_ref):
            acc = acc + r[...].astype(F32)
        o_ref[...] = acc

    own = pl.BlockSpec((None, tb, c), lambda i, w: (w[1], i, 0))
    got = [pl.BlockSpec((None, tb, c), functools.partial(lambda i, w, j: (j, i, 0), j=j)) for j in range(3)]
    return pl.pallas_call(
        body, grid_spec=pltpu.PrefetchScalarGridSpec(
            num_scalar_prefetch=1, grid=(nb,), in_specs=[own] + got,
            out_specs=pl.BlockSpec((tb, c), lambda i, w: (w[0] * nb + i, 0))),
        out_shape=jax.ShapeDtypeStruct((2 * rh, c), F32), compiler_params=_cp(("parallel",)), name=name,
    )(where, s1, rcv, rcv, rcv)


def _all_gather_small(blk, *, name):
    r = blk.shape[0]

    def body(b_ref, out_ref, send_sems, recv_sems):
        x, y, c = _place()
        me = 4 * x + 2 * y + c
        out_ref[me] = b_ref[...]
        sends = []
        for k in range(1, N_DEV):
            peer = (x ^ (k >> 2), y ^ ((k >> 1) & 1), c ^ (k & 1))
            sends.append(pltpu.make_async_remote_copy(src_ref=b_ref, dst_ref=out_ref.at[me], send_sem=send_sems.at[k - 1],
                                                      recv_sem=recv_sems.at[k - 1], device_id=peer, device_id_type=MESH))
        for cp in sends:
            cp.start()
        for k in range(1, N_DEV):
            rows = out_ref.at[me ^ k]
            pltpu.make_async_remote_copy(src_ref=rows, dst_ref=rows, send_sem=send_sems.at[k - 1],
                                         recv_sem=recv_sems.at[k - 1], device_id=(x, y, c), device_id_type=MESH).wait_recv()
        for cp in sends:
            cp.wait_send()

    return pl.pallas_call(
        body, out_shape=jax.ShapeDtypeStruct((N_DEV, r, LANES), blk.dtype), in_specs=[VMEM_SPEC], out_specs=VMEM_SPEC,
        scratch_shapes=[pltpu.SemaphoreType.DMA((N_DEV - 1,)), pltpu.SemaphoreType.DMA((N_DEV - 1,))],
        name=name,
    )(blk)


def _sum_rows(parts, out_dtype, *, name, tb=1024):
    rows = parts[0].shape[0]
    tb = _blk(rows, tb)

    def body(*refs):
        acc = refs[0][...].astype(F32)
        for r in refs[1:-1]:
            acc = acc + r[...].astype(F32)
        refs[-1][...] = acc.astype(refs[-1].dtype)

    spec = pl.BlockSpec((tb, LANES), lambda i: (i, 0))
    return pl.pallas_call(
        body, grid=(rows // tb,), in_specs=[spec] * len(parts), out_specs=spec,
        out_shape=jax.ShapeDtypeStruct((rows, LANES), out_dtype), compiler_params=_cp(("parallel",)), name=name,
    )(*parts)


def _reduce_scatter(gs, core, where):
    sibs = _sibling_halves(gs, name="rs_sibling_halves")
    s1s = [_pair_sum(g, s, core, name=f"rs_pair_sum_{i}") for i, (g, s) in enumerate(zip(gs, sibs))]
    rcvs = _chip_exchange(s1s, name="rs_chip_exchange")
    halves = [_chip_sum(s1, rcv, where, name=f"rs_chip_sum_{i}") for i, (s1, rcv) in enumerate(zip(s1s, rcvs))]
    return _join_halves(halves, name="rs_join_halves")


BIG = (
    ("w_in", 1), ("w_mem_kv", 0), ("w_swa_up", 1), ("w_gdn_up", 1), ("w_xa_up", 1), ("w_out", 0), ("w_mlp_in", 1),
    ("w_mlp_out", 0))
SMALL = ("g_mix", "sinks", "a_log", "dt_bias", "gdn_norm_w", "g_mem", "g_mlp", "g_final")


def _rows128(a, rows):
    flat = a.reshape(-1)
    return jnp.pad(flat, (0, rows * LANES - flat.shape[0])).reshape(rows, LANES)


def kernel(x, mem, g_mix, w_in, sinks, conv_w, a_log, dt_bias, gdn_norm_w, g_mem, w_mem_kv, w_swa_up, w_gdn_up, w_xa_up, w_out, g_mlp, w_mlp_in, w_mlp_out, g_final, loss_target, m_g_mix, m_w_in, m_sinks, m_conv_w, m_a_log, m_dt_bias, m_gdn_norm_w, m_g_mem, m_w_mem_kv, m_w_swa_up, m_w_gdn_up, m_w_xa_up, m_w_out, m_g_mlp, m_w_mlp_in, m_w_mlp_out, m_g_final, v_g_mix, v_w_in, v_sinks, v_conv_w, v_a_log, v_dt_bias, v_gdn_norm_w, v_g_mem, v_w_mem_kv, v_w_swa_up, v_w_gdn_up, v_w_xa_up, v_w_out, v_g_mlp, v_w_mlp_in, v_w_mlp_out, v_g_final):
    given = dict(locals())
    xi, yi, ci = _place()
    chip = 2 * xi + yi
    core = jnp.reshape(ci, (1,)).astype(jnp.int32)
    where = jnp.stack([ci, chip]).astype(jnp.int32)

    shards = [given[k][0].astype(BF16) for k, _ in BIG]
    gathered = dict(zip([k for k, _ in BIG], _all_gather_weights(shards, name="ag_weights")))
    wts = {}
    for k, ax in BIG:
        g = gathered[k]
        if k == "w_in":
            wts["w_cat"] = _to_cat(g)
        elif ax == 0:
            wts[k] = g.reshape(-1, g.shape[2])
        else:
            wts[k] = g
    conv_shard = conv_w[0]
    conv_rows = -(-conv_shard.size // (8 * LANES)) * 8
    conv_all = _all_gather_small(_rows128(conv_shard, conv_rows), name="ag_conv")
    conv_full = jnp.concatenate(
        [conv_all[2 * s].reshape(-1)[:conv_shard.size].reshape(conv_shard.shape) for s in range(N_CHIPS)], axis=1)

    small = {k: given[k].reshape(1, -1) for k in SMALL}
    small["conv_w"] = conv_full
    loss_row, dx, grads = _local_step(x[0], mem[0], loss_target[0], wts, small)

    gs = []
    for (k, ax), sh in zip(BIG, shards):
        if k == "w_in":
            gs.append(_from_cat(grads["w_cat"]))
        elif ax == 0:
            gs.append(grads[k].reshape(N_CHIPS, *sh.shape))
        else:
            gs.append(grads[k])
    big_grads = dict(zip([k for k, _ in BIG], _reduce_scatter(gs, core, where)))

    layout = [("loss", loss_row[:, :1])] + [(k, grads[k]) for k in SMALL] + [("conv_w", grads["conv_w"])]
    rows = [-(-a.size // LANES) for _, a in layout]
    blk_rows = -(-sum(rows) // 8) * 8
    blk = jnp.concatenate([_rows128(a.astype(F32), n) for (_, a), n in zip(layout, rows)]
                          + [jnp.zeros((blk_rows - sum(rows), LANES), F32)], axis=0)
    gathered = _all_gather_small(blk, name="ag_small_grads")
    reduced = _sum_rows([gathered[i] for i in range(N_DEV)], F32, name="small_grad_sum")
    small_grads, start = {}, 0
    for (k, a), n in zip(layout, rows):
        small_grads[k] = reduced[start:start + n].reshape(-1)[:a.size].reshape(a.shape)
        start += n
    loss = small_grads["loss"].reshape(())
    cw = conv_shard.shape[1]
    conv_grad = lax.dynamic_slice_in_dim(small_grads["conv_w"], chip * cw, cw, axis=1)

    names = ["g_mix", "w_in", "sinks", "conv_w", "a_log", "dt_bias", "gdn_norm_w", "g_mem", "w_mem_kv", "w_swa_up",
             "w_gdn_up", "w_xa_up", "w_out", "g_mlp", "w_mlp_in", "w_mlp_out", "g_final"]
    out_g, out_d, out_m, out_v = [], [], [], []
    for k in names:
        w, m, v = given[k], given["m_" + k], given["v_" + k]
        if k in big_grads:
            g2 = big_grads[k]
        elif k == "conv_w":
            g2 = conv_grad
        else:
            g2 = small_grads[k]
        as_given = (lambda a: a.reshape(1, -1)) if w.ndim == 1 else (lambda a: a)
        delta, new_m, new_v = _adamw(as_given(w), g2, as_given(m), as_given(v), name="adamw_" + k)
        out_g.append(g2.reshape(w.shape))
        out_d.append(delta.reshape(w.shape))
        out_m.append(new_m.reshape(w.shape))
        out_v.append(new_v.reshape(w.shape))
    return (loss, dx[None], *out_g, *out_d, *out_m, *out_v)
```

```python
import functools
import math
from typing import Callable, NamedTuple

import jax
import jax.numpy as jnp
from jax import lax
from jax.experimental import pallas as pl
from jax.experimental.pallas import tpu as pltpu

F32 = jnp.float32
BF16 = jnp.bfloat16
HI = lax.Precision.HIGHEST
MESH = pl.DeviceIdType.MESH

SWA_Q_HEADS = 16
SWA_KV_HEADS = 2
SWA_HEAD_DIM = 64
SWA_WINDOW = 128
GDN_HEADS = 4
GDN_HEAD_DIM = 128
GDN_CONV = 4
GDN_CHUNK = 64
XA_HEADS = 4
XA_HEAD_DIM = 128
RMS_EPS = 1e-6
L2_EPS = 1e-6
ADAM_LR = 0.001
ADAM_B1 = 0.9
ADAM_B2 = 0.999
ADAM_EPS = 1e-08
ADAM_WD = 0.01
ADAM_STEP = 10

LANES = 128
N_SHARDS = 4
VMEM_LIMIT = 56 * 1024 * 1024

NT = (((1,), (1,)), ((), ()))
TN = (((0,), (0,)), ((), ()))
NN = (((1,), (0,)), ((), ()))


def _cp(sem=None):
    return pltpu.CompilerParams(dimension_semantics=sem, vmem_limit_bytes=VMEM_LIMIT)


def _blk(dim, pref):
    if dim <= pref:
        return dim
    b = (pref // LANES) * LANES
    while dim % b:
        b -= LANES
    assert b > 0, (dim, pref)
    return b


def _dot(a, b, dims=NN, precision=None):
    return lax.dot_general(a, b, dims, precision=precision, preferred_element_type=F32)


def _sigmoid(x):
    return 1.0 / (1.0 + jnp.exp(-x))


MM_TK_BYTES = 4096


def _mm(a, b, *, name, ta=False, tb=False, out_dtypes=(F32,), epilogue=None, extras=(), tm=1024, tn=1024, tk=None,
        b_sharded=False, out_sharded=False, b_window=None, exchange=None):
    (kdim, m) = a.shape if ta else a.shape[::-1]
    col0 = 0
    n_lim = k_lim = None
    if b_sharded:
        ns, rows_w, per = b.shape
        if tb:
            kb, n, k_lim = ns * per, rows_w, per
        else:
            kb, n, n_lim = rows_w, ns * per, per
    else:
        (kb, n) = b.shape[::-1] if tb else b.shape
        if b_window is not None:
            assert not tb
            col0, n = b_window
    assert kdim == kb, (a.shape, b.shape, ta, tb)
    if out_sharded:
        assert n % N_SHARDS == 0
        n_lim = n // N_SHARDS if n_lim is None else n_lim
        assert n_lim == n // N_SHARDS
    if tk is None:
        tk = MM_TK_BYTES // max(a.dtype.itemsize, b.dtype.itemsize)
    tm, tn, tk = _blk(m, tm), _blk(n_lim or n, tn), _blk(k_lim or kdim, tk)
    assert col0 % tn == 0, (col0, tn)
    nk = kdim // tk
    a_spec = pl.BlockSpec((tk, tm), lambda i, j, k: (k, i)) if ta else pl.BlockSpec((tm, tk), lambda i, j, k: (i, k))
    if b_sharded and tb:
        kpb = k_lim // tk
        b_spec = pl.BlockSpec((None, tn, tk), lambda i, j, k: (k // kpb, j, k % kpb))
    elif b_sharded:
        bpb = n_lim // tn
        b_spec = pl.BlockSpec((None, tk, tn), lambda i, j, k: (j // bpb, k, j % bpb))
    elif tb:
        b_spec = pl.BlockSpec((tn, tk), lambda i, j, k: (j, k))
    else:
        b_spec = pl.BlockSpec((tk, tn), lambda i, j, k: (k, j + col0 // tn))
    x_spec = pl.BlockSpec((tm, tn), lambda i, j, k: (i, j))
    if out_sharded:
        opb = n_lim // tn
        o_spec = pl.BlockSpec((None, tm, tn), lambda i, j, k: (j // opb, i, j % opb))
        out_shape = (N_SHARDS, m, n_lim)
    else:
        o_spec, out_shape = x_spec, (m, n)
    dims = ((((0 if ta else 1),), ((1 if tb else 0),)), ((), ()))
    n_extra, n_out = len(extras), len(out_dtypes)

    host = _ExchangeHost(exchange)
    grid = (m // tm, n // tn, nk)

    def body(*refs):
        a_ref, b_ref = refs[:2]
        extra_refs = refs[2:2 + n_extra]
        out_refs = refs[2 + n_extra + host.n_in:2 + n_extra + host.n_in + n_out]
        host.start(refs, 2 + n_extra, 2 + n_extra + host.n_in + n_out, grid)
        part = _dot(a_ref[...].astype(BF16), b_ref[...].astype(BF16), dims)

        def finish(acc):
            vals = epilogue(acc, *[r[...] for r in extra_refs]) if epilogue is not None else (acc,) * n_out
            assert len(vals) == n_out
            for r, v in zip(out_refs, vals):
                r[...] = v.astype(r.dtype)

        if nk == 1:
            finish(part)
        else:
            acc_ref = refs[2 + n_extra + host.n_in + n_out + host.n_out]
            k = pl.program_id(2)

            @pl.when(k == 0)
            def _():
                acc_ref[...] = part

            @pl.when((k > 0) & (k < nk - 1))
            def _():
                acc_ref[...] += part

            @pl.when(k == nk - 1)
            def _():
                finish(acc_ref[...] + part)

        host.wait(refs, 2 + n_extra, 2 + n_extra + host.n_in + n_out, grid)

    outs = pl.pallas_call(
        body,
        grid=grid,
        in_specs=[a_spec, b_spec] + [x_spec] * n_extra + host.in_specs,
        out_specs=[o_spec] * n_out + host.out_specs,
        out_shape=[jax.ShapeDtypeStruct(out_shape, d) for d in out_dtypes] + host.out_shapes,
        scratch_shapes=([pltpu.VMEM((tm, tn), F32)] if nk > 1 else []) + host.scratch,
        compiler_params=_cp(host.semantics(("parallel", "parallel", "arbitrary"))),
        name=name,
    )(a, b, *extras, *host.ins)
    mine, landed = outs[:n_out], list(outs[n_out:])
    mine = mine[0] if n_out == 1 else mine
    return (mine, landed) if exchange is not None else mine


class Exchange(NamedTuple):
    ins: tuple
    out_shapes: tuple
    n_remote: int
    n_local: int
    copies_of: Callable
    aliases: tuple = ()


def _exchange_copies(ex, in_refs, out_refs, sem_refs):
    send_sems, recv_sems, local_sems = sem_refs
    remote, local = ex.copies_of(in_refs, out_refs, _place())
    assert len(remote) == ex.n_remote and len(local) == ex.n_local, (len(remote), len(local))
    cps = [pltpu.make_async_remote_copy(src_ref=src, dst_ref=dst, send_sem=send_sems.at[k], recv_sem=recv_sems.at[k],
                                        device_id=to, device_id_type=MESH) for k, (src, dst, to) in enumerate(remote)]
    cps += [pltpu.make_async_copy(src, dst, local_sems.at[k]) for k, (src, dst) in enumerate(local)]
    return cps


class _ExchangeHost:
    def __init__(self, ex):
        self.ex = ex
        self.ins = list(ex.ins) if ex else []
        self.out_shapes = list(ex.out_shapes) if ex else []
        self.n_in, self.n_out = len(self.ins), len(self.out_shapes)
        self.in_specs = [HBM_SPEC] * self.n_in
        self.out_specs = [HBM_SPEC] * self.n_out
        self.scratch = _sem_scratch(ex.n_remote, ex.n_local) if ex else []
        assert not (ex and ex.aliases)

    def semantics(self, sem):
        return tuple("arbitrary" for _ in sem) if self.ex else sem

    def _refs(self, refs, in_at, out_at):
        return refs[in_at:in_at + self.n_in], refs[out_at:out_at + self.n_out], refs[len(refs) - 3:]

    def _when(self, grid, last):
        cond = None
        for d, size in enumerate(grid):
            c = pl.program_id(d) == (size - 1 if last else 0)
            cond = c if cond is None else cond & c
        return cond

    def start(self, refs, in_at, out_at, grid):
        if self.ex:
            @pl.when(self._when(grid, False))
            def _():
                for cp in _exchange_copies(self.ex, *self._refs(refs, in_at, out_at)):
                    cp.start()

    def wait(self, refs, in_at, out_at, grid):
        if self.ex:
            @pl.when(self._when(grid, True))
            def _():
                for cp in _exchange_copies(self.ex, *self._refs(refs, in_at, out_at)):
                    cp.wait()


def _rms_fwd(x, g, *, name, tm=512):
    t, d = x.shape
    tm = _blk(t, tm)

    def body(x_ref, g_ref, n_ref):
        xv = x_ref[...]
        r = lax.rsqrt(jnp.mean(xv * xv, axis=-1, keepdims=True) + RMS_EPS)
        n_ref[...] = (xv * r * g_ref[...]).astype(n_ref.dtype)

    return pl.pallas_call(
        body, grid=(t // tm,),
        in_specs=[pl.BlockSpec((tm, d), lambda i: (i, 0)), pl.BlockSpec((1, d), lambda i: (0, 0))],
        out_specs=pl.BlockSpec((tm, d), lambda i: (i, 0)),
        out_shape=jax.ShapeDtypeStruct((t, d), BF16),
        compiler_params=_cp(("parallel",)), name=name,
    )(x, g)


def _rms_bwd(dn, x, g, dres, *, name, tm=512):
    t, d = x.shape
    tm = _blk(t, tm)

    def body(dn_ref, x_ref, g_ref, dres_ref, dx_ref, dxb_ref, dg_ref):
        i = pl.program_id(0)
        xv = x_ref[...]
        r = lax.rsqrt(jnp.mean(xv * xv, axis=-1, keepdims=True) + RMS_EPS)
        xh = xv * r
        dnv = dn_ref[...].astype(F32)
        dxh = dnv * g_ref[...]
        dx = dres_ref[...] + r * (dxh - xh * jnp.mean(dxh * xh, axis=-1, keepdims=True))
        dx_ref[...] = dx
        dxb_ref[...] = dx.astype(dxb_ref.dtype)
        part = jnp.sum(dnv * xh, axis=0, keepdims=True)

        @pl.when(i == 0)
        def _():
            dg_ref[...] = part

        @pl.when(i > 0)
        def _():
            dg_ref[...] += part

    row = pl.BlockSpec((tm, d), lambda i: (i, 0))
    vec = pl.BlockSpec((1, d), lambda i: (0, 0))
    return pl.pallas_call(
        body, grid=(t // tm,),
        in_specs=[row, row, vec, row], out_specs=[row, row, vec],
        out_shape=[jax.ShapeDtypeStruct((t, d), F32), jax.ShapeDtypeStruct((t, d), BF16),
                   jax.ShapeDtypeStruct((1, d), F32)],
        compiler_params=_cp(("arbitrary",)), name=name,
    )(dn, x, g, dres)


def _final_loss(h, g, tgt, *, name, tm=512):
    t, d = h.shape
    tm = _blk(t, tm)

    def body(h_ref, g_ref, t_ref, dh_ref, dhb_ref, dg_ref, loss_ref):
        i = pl.program_id(0)
        hv = h_ref[...]
        r = lax.rsqrt(jnp.mean(hv * hv, axis=-1, keepdims=True) + RMS_EPS)
        xh = hv * r
        e = xh * g_ref[...] - t_ref[...]
        dy = e * (1.0 / d)
        dxh = dy * g_ref[...]
        dh = r * (dxh - xh * jnp.mean(dxh * xh, axis=-1, keepdims=True))
        dh_ref[...] = dh
        dhb_ref[...] = dh.astype(dhb_ref.dtype)
        dg_part = jnp.sum(dy * xh, axis=0, keepdims=True)
        row_loss = jnp.sum(e * e, axis=-1, keepdims=True) * (0.5 / d)
        loss_part = jnp.sum(row_loss, axis=0, keepdims=True)

        @pl.when(i == 0)
        def _():
            dg_ref[...] = dg_part
            loss_ref[...] = jnp.broadcast_to(loss_part, loss_ref.shape)

        @pl.when(i > 0)
        def _():
            dg_ref[...] += dg_part
            loss_ref[...] += jnp.broadcast_to(loss_part, loss_ref.shape)

    row = pl.BlockSpec((tm, d), lambda i: (i, 0))
    vec = pl.BlockSpec((1, d), lambda i: (0, 0))
    return pl.pallas_call(
        body, grid=(t // tm,),
        in_specs=[row, vec, row], out_specs=[row, row, vec, pl.BlockSpec((1, LANES), lambda i: (0, 0))],
        out_shape=[jax.ShapeDtypeStruct((t, d), F32), jax.ShapeDtypeStruct((t, d), BF16),
                   jax.ShapeDtypeStruct((1, d), F32), jax.ShapeDtypeStruct((1, LANES), F32)],
        compiler_params=_cp(("arbitrary",)), name=name,
    )(h, g, tgt)


def _swa_mask(n, reps):
    w = SWA_WINDOW
    qi = lax.broadcasted_iota(jnp.int32, (reps * w, 2 * w), 0) & (w - 1)
    kj = lax.broadcasted_iota(jnp.int32, (reps * w, 2 * w), 1)
    return (kj > qi) & (kj <= qi + w) & ((n > 0) | (kj >= w))


def _stack_heads(ref, heads, width):
    return jnp.concatenate([ref[:, h * width:(h + 1) * width] for h in heads], axis=0)


def _stack_scalars(ref, heads, rows):
    return jnp.concatenate([jnp.broadcast_to(ref[0:1, h:h + 1], (rows, 1)) for h in heads], axis=0)


def _swa_fwd(q, kv, sinks, *, name):
    t = q.shape[0]
    w, hd, hq, hkv = SWA_WINDOW, SWA_HEAD_DIM, SWA_Q_HEADS, SWA_KV_HEADS
    grp = hq // hkv
    kvw = hkv * hd
    nb = t // w

    def body(q_ref, kvp_ref, kvc_ref, s_ref, o_ref, lse_ref):
        n = pl.program_id(0)
        mask = _swa_mask(n, grp)
        kvcat = jnp.concatenate([kvp_ref[...], kvc_ref[...]], axis=0)
        outs, lses = [], []
        for hk in range(hkv):
            heads = range(hk * grp, (hk + 1) * grp)
            qs = _stack_heads(q_ref, heads, hd)
            kh = kvcat[:, hk * hd:(hk + 1) * hd]
            vh = kvcat[:, kvw + hk * hd:kvw + (hk + 1) * hd]
            sk = _stack_scalars(s_ref, heads, w)
            s = jnp.where(mask, _dot(qs, kh, NT) * (hd ** -0.5), -jnp.inf)
            m = jnp.maximum(jnp.max(s, axis=-1, keepdims=True), sk)
            p = jnp.exp(s - m)
            den = jnp.sum(p, axis=-1, keepdims=True) + jnp.exp(sk - m)
            o = _dot((p * (1.0 / den)).astype(BF16), vh)
            lse = m + jnp.log(den)
            outs += [o[j * w:(j + 1) * w] for j in range(grp)]
            lses += [lse[j * w:(j + 1) * w] for j in range(grp)]
        o_ref[...] = jnp.concatenate(outs, axis=1).astype(o_ref.dtype)
        lse_ref[...] = jnp.concatenate(lses, axis=1)

    return pl.pallas_call(
        body, grid=(nb,),
        in_specs=[pl.BlockSpec((w, hq * hd), lambda i: (i, 0)),
                  pl.BlockSpec((w, 2 * kvw), lambda i: (jnp.maximum(i - 1, 0), 0)),
                  pl.BlockSpec((w, 2 * kvw), lambda i: (i, 0)),
                  pl.BlockSpec((1, hq), lambda i: (0, 0))],
        out_specs=[pl.BlockSpec((w, hq * hd), lambda i: (i, 0)), pl.BlockSpec((w, hq), lambda i: (i, 0))],
        out_shape=[jax.ShapeDtypeStruct((t, hq * hd), BF16), jax.ShapeDtypeStruct((t, hq), F32)],
        compiler_params=_cp(("parallel",)), name=name,
    )(q, kv, kv, sinks)


def _swa_bwd(q, kv, sinks, o, lse, do, *, name):
    t = q.shape[0]
    w, hd, hq, hkv = SWA_WINDOW, SWA_HEAD_DIM, SWA_Q_HEADS, SWA_KV_HEADS
    grp = hq // hkv
    kvw = hkv * hd
    nb = t // w

    def body(q_ref, kvp_ref, kvc_ref, s_ref, o_ref, lse_ref, do_ref, dq_ref, dkv_ref, ds_ref, carry_ref):
        n = pl.program_id(0)

        @pl.when(n == 0)
        def _():
            ds_ref[...] = jnp.zeros_like(ds_ref)
            carry_ref[...] = jnp.zeros_like(carry_ref)

        @pl.when(n < nb)
        def _():
            mask = _swa_mask(n, grp)
            kvcat = jnp.concatenate([kvp_ref[...], kvc_ref[...]], axis=0)
            dqs, dsk, dks, dvs = [], [], [], []
            for hk in range(hkv):
                heads = range(hk * grp, (hk + 1) * grp)
                qs = _stack_heads(q_ref, heads, hd)
                dos = _stack_heads(do_ref, heads, hd)
                os_ = _stack_heads(o_ref, heads, hd)
                lse = _stack_heads(lse_ref, heads, 1)
                sk = _stack_scalars(s_ref, heads, w)
                kh = kvcat[:, hk * hd:(hk + 1) * hd]
                vh = kvcat[:, kvw + hk * hd:kvw + (hk + 1) * hd]
                s = _dot(qs, kh, NT) * (hd ** -0.5)
                p = jnp.exp(jnp.where(mask, s, -jnp.inf) - lse)
                delta = jnp.sum(dos.astype(F32) * os_.astype(F32), axis=-1, keepdims=True)
                ds = (p * (_dot(dos, vh, NT) - delta) * (hd ** -0.5)).astype(BF16)
                dq = _dot(ds, kh)
                dqs += [dq[j * w:(j + 1) * w] for j in range(grp)]
                dks.append(_dot(ds, qs, TN))
                dvs.append(_dot(p.astype(BF16), dos, TN))
                dsink = -jnp.exp(sk - lse) * delta
                dsk += [jnp.sum(dsink[j * w:(j + 1) * w], axis=0, keepdims=True) for j in range(grp)]
            dq_ref[...] = jnp.concatenate(dqs, axis=1).astype(dq_ref.dtype)
            ds_ref[...] += jnp.concatenate(dsk, axis=1)
            dkv_cat = jnp.concatenate(dks + dvs, axis=1)
            dkv_ref[...] = (carry_ref[...] + dkv_cat[:w]).astype(dkv_ref.dtype)
            carry_ref[...] = dkv_cat[w:]

        @pl.when(n == nb)
        def _():
            dkv_ref[...] = carry_ref[...].astype(dkv_ref.dtype)

    cur = lambda i: (jnp.minimum(i, nb - 1), 0)
    prev = lambda i: (jnp.clip(i - 1, 0, nb - 1), 0)
    return pl.pallas_call(
        body, grid=(nb + 1,),
        in_specs=[pl.BlockSpec((w, hq * hd), cur), pl.BlockSpec((w, 2 * kvw), prev), pl.BlockSpec((w, 2 * kvw), cur),
                  pl.BlockSpec((1, hq), lambda i: (0, 0)), pl.BlockSpec((w, hq * hd), cur),
                  pl.BlockSpec((w, hq), cur), pl.BlockSpec((w, hq * hd), cur)],
        out_specs=[pl.BlockSpec((w, hq * hd), cur), pl.BlockSpec((w, 2 * kvw), prev),
                   pl.BlockSpec((1, hq), lambda i: (0, 0))],
        out_shape=[jax.ShapeDtypeStruct((t, hq * hd), BF16), jax.ShapeDtypeStruct((t, 2 * kvw), BF16),
                   jax.ShapeDtypeStruct((1, hq), F32)],
        scratch_shapes=[pltpu.VMEM((w, 2 * kvw), F32)],
        compiler_params=_cp(("arbitrary",)), name=name,
    )(q, kv, kv, sinks, o, lse, do)


def _xa_fwd(q, mkv, *, name, tq=512):
    t, xw = q.shape
    nm = mkv.shape[0]
    hd, nh = XA_HEAD_DIM, XA_HEADS
    tq = _blk(t, tq)

    def body(q_ref, mkv_ref, o_ref):
        outs = []
        for h in range(nh):
            qh = q_ref[:, h * hd:(h + 1) * hd]
            kh = mkv_ref[:, h * hd:(h + 1) * hd]
            vh = mkv_ref[:, xw + h * hd:xw + (h + 1) * hd]
            s = _dot(qh, kh, NT) * (hd ** -0.5)
            p = jnp.exp(s - jnp.max(s, axis=-1, keepdims=True))
            p = p / jnp.sum(p, axis=-1, keepdims=True)
            outs.append(_dot(p.astype(BF16), vh))
        o_ref[...] = jnp.concatenate(outs, axis=1).astype(o_ref.dtype)

    return pl.pallas_call(
        body, grid=(t // tq,),
        in_specs=[pl.BlockSpec((tq, xw), lambda i: (i, 0)), pl.BlockSpec((nm, 2 * xw), lambda i: (0, 0))],
        out_specs=pl.BlockSpec((tq, xw), lambda i: (i, 0)),
        out_shape=jax.ShapeDtypeStruct((t, xw), BF16),
        compiler_params=_cp(("parallel",)), name=name,
    )(q, mkv)


def _xa_bwd(q, mkv, do, *, name, tq=512):
    t, xw = q.shape
    nm = mkv.shape[0]
    hd, nh = XA_HEAD_DIM, XA_HEADS
    tq = _blk(t, tq)

    def body(q_ref, mkv_ref, do_ref, dq_ref, dmkv_ref):
        i = pl.program_id(0)
        dqs, dks, dvs = [], [], []
        for h in range(nh):
            qh = q_ref[:, h * hd:(h + 1) * hd]
            kh = mkv_ref[:, h * hd:(h + 1) * hd]
            vh = mkv_ref[:, xw + h * hd:xw + (h + 1) * hd]
            doh = do_ref[:, h * hd:(h + 1) * hd]
            s = _dot(qh, kh, NT) * (hd ** -0.5)
            p = jnp.exp(s - jnp.max(s, axis=-1, keepdims=True))
            p = p / jnp.sum(p, axis=-1, keepdims=True)
            dp = _dot(doh, vh, NT)
            ds = (p * (dp - jnp.sum(p * dp, axis=-1, keepdims=True)) * (hd ** -0.5)).astype(BF16)
            dqs.append(_dot(ds, kh))
            dks.append(_dot(ds, qh, TN))
            dvs.append(_dot(p.astype(BF16), doh, TN))
        dq_ref[...] = jnp.concatenate(dqs, axis=1).astype(dq_ref.dtype)
        part = jnp.concatenate(dks + dvs, axis=1)

        @pl.when(i == 0)
        def _():
            dmkv_ref[...] = part

        @pl.when(i > 0)
        def _():
            dmkv_ref[...] += part

    row = pl.BlockSpec((tq, xw), lambda i: (i, 0))
    full = pl.BlockSpec((nm, 2 * xw), lambda i: (0, 0))
    return pl.pallas_call(
        body, grid=(t // tq,),
        in_specs=[row, full, row], out_specs=[row, full],
        out_shape=[jax.ShapeDtypeStruct((t, xw), BF16), jax.ShapeDtypeStruct((nm, 2 * xw), F32)],
        compiler_params=_cp(("arbitrary",)), name=name,
    )(q, mkv, do)


def _merge_specs(t, d, ys, ws, tm, tn):
    nj = d // tn
    wpb = ws[0].shape[2] // tn
    y_specs = [pl.BlockSpec((tm, y.shape[1]), lambda i, j: (i, 0)) for y in ys]
    w_specs = [pl.BlockSpec((None, w.shape[1], tn), lambda i, j: (j // wpb, 0, j % wpb)) for w in ws]
    g_specs = [pl.BlockSpec((tm, tn), functools.partial(lambda i, j, b: (i, j + b * nj), b=b)) for b in range(3)]
    return y_specs, w_specs, g_specs


def _merge_fwd(ys, ws, gates, *, name, tm=512, tn=512):
    t, d = ys[0].shape[0], ws[0].shape[0] * ws[0].shape[2]
    tm, tn = _blk(t, tm), _blk(ws[0].shape[2], tn)
    y_specs, w_specs, g_specs = _merge_specs(t, d, ys, ws, tm, tn)

    def body(ya, yb, yc, wa, wb, wc, ga, gb, gc, o_ref):
        acc = None
        for y, w, g in ((ya, wa, ga), (yb, wb, gb), (yc, wc, gc)):
            term = _sigmoid(g[...]) * _dot(y[...], w[...])
            acc = term if acc is None else acc + term
        o_ref[...] = acc.astype(o_ref.dtype)

    return pl.pallas_call(
        body, grid=(t // tm, d // tn),
        in_specs=y_specs + w_specs + g_specs,
        out_specs=pl.BlockSpec((tm, tn), lambda i, j: (i, j)),
        out_shape=jax.ShapeDtypeStruct((t, d), BF16),
        compiler_params=_cp(("parallel", "parallel")), name=name,
    )(*ys, *ws, gates, gates, gates)


def _merge_bwd(ys, ws, gates, dmerged, *, name, tm=512, tn=512):
    t, d = ys[0].shape[0], ws[0].shape[0] * ws[0].shape[2]
    tm, tn = _blk(t, tm), _blk(ws[0].shape[2], tn)
    y_specs, w_specs, g_specs = _merge_specs(t, d, ys, ws, tm, tn)
    tile = pl.BlockSpec((tm, tn), lambda i, j: (i, j))

    def body(ya, yb, yc, wa, wb, wc, ga, gb, gc, dm_ref, dua, dub, duc, dga, dgb, dgc):
        dm = dm_ref[...]
        for y, w, g, du, dg in ((ya, wa, ga, dua, dga), (yb, wb, gb, dub, dgb), (yc, wc, gc, duc, dgc)):
            sg = _sigmoid(g[...])
            u = _dot(y[...], w[...])
            du[...] = (dm * sg).astype(du.dtype)
            dg[...] = (dm * u * sg * (1.0 - sg)).astype(dg.dtype)

    return pl.pallas_call(
        body, grid=(t // tm, d // tn),
        in_specs=y_specs + w_specs + g_specs + [tile],
        out_specs=[tile] * 6,
        out_shape=[jax.ShapeDtypeStruct((t, d), BF16)] * 6,
        compiler_params=_cp(("parallel", "parallel")), name=name,
    )(*ys, *ws, gates, gates, gates, dmerged)


def _adamw(w, g, m, v, *, name, tm=256):
    lead = w.ndim - 2
    assert all(s == 1 for s in w.shape[:lead]) and m.shape == w.shape and v.shape == w.shape
    r, c = w.shape[lead:]
    assert g.shape == (r, c)
    tm = _blk(r, tm) if r % 8 == 0 else r
    bc1 = 1.0 - ADAM_B1 ** ADAM_STEP
    bc2 = 1.0 - ADAM_B2 ** ADAM_STEP

    def body(w_ref, g_ref, m_ref, v_ref, d_ref, nm_ref, nv_ref):
        gv = g_ref[...]
        nm = ADAM_B1 * m_ref[...] + (1.0 - ADAM_B1) * gv
        nv = ADAM_B2 * v_ref[...] + (1.0 - ADAM_B2) * (gv * gv)
        d_ref[...] = -ADAM_LR * ((nm / bc1) / (jnp.sqrt(nv / bc2) + ADAM_EPS) + ADAM_WD * w_ref[...])
        nm_ref[...] = nm
        nv_ref[...] = nv

    spec = pl.BlockSpec((None,) * lead + (tm, c), lambda i: (0,) * lead + (i, 0))
    g_spec = pl.BlockSpec((tm, c), lambda i: (i, 0))
    return pl.pallas_call(
        body, grid=(r // tm,), in_specs=[spec, g_spec, spec, spec], out_specs=[spec] * 3,
        out_shape=[jax.ShapeDtypeStruct(w.shape, F32)] * 3,
        compiler_params=_cp(("parallel",)), name=name,
    )(w, g, m, v)


HALO = 8


def _shift_down(cur, prev, j):
    if j == 0:
        return cur
    y = pltpu.roll(cur, j, 0)
    row = lax.broadcasted_iota(jnp.int32, (HALO, cur.shape[1]), 0)
    top = jnp.where(row < j, pltpu.roll(prev, j, 0), y[:HALO])
    return jnp.concatenate([top, y[HALO:]], axis=0)


def _shift_up(cur, nxt, j):
    if j == 0:
        return cur
    tm = cur.shape[0]
    y = pltpu.roll(cur, tm - j, 0)
    row = lax.broadcasted_iota(jnp.int32, (HALO, cur.shape[1]), 0)
    bot = jnp.where(row >= HALO - j, pltpu.roll(nxt, HALO - j, 0), y[tm - HALO:])
    return jnp.concatenate([y[:tm - HALO], bot], axis=0)


def _softplus(x):
    return jnp.maximum(x, 0.0) + jnp.log(1.0 + jnp.exp(-jnp.abs(x)))


def _gdn_pre_fwd(qkvb, conv_w, ab, alog_pad, dt_pad, *, name, ab_blk=0, tm=256):
    t, cw = qkvb.shape
    hd, nh, ck = GDN_HEAD_DIM, GDN_HEADS, GDN_CHUNK
    gw = nh * hd
    tm = _blk(t, tm)
    hb = tm // HALO

    def body(x_ref, xp_ref, w_ref, ab_ref, al_ref, dt_ref, xc_ref, qkvn_ref, aux_ref):
        i = pl.program_id(0)
        cur = x_ref[...]
        prev = jnp.where(i > 0, xp_ref[...], 0.0)
        xc = None
        for tap in range(GDN_CONV):
            term = w_ref[tap:tap + 1, :] * _shift_down(cur, prev, GDN_CONV - 1 - tap)
            xc = term if xc is None else xc + term
        xc_ref[...] = xc
        s = xc * _sigmoid(xc)
        for h in range(2 * nh):
            xh = s[:, h * hd:(h + 1) * hd]
            r = lax.rsqrt(jnp.sum(xh * xh, axis=-1, keepdims=True) + L2_EPS)
            scale = hd ** -0.5 if h < nh else 1.0
            qkvn_ref[:, h * hd:(h + 1) * hd] = xh * (r * scale)
        qkvn_ref[:, 2 * gw:] = s[:, 2 * gw:]
        abv = ab_ref[...]
        lane = lax.broadcasted_iota(jnp.int32, abv.shape, 1)
        g = jnp.where(lane < nh, -jnp.exp(al_ref[...]) * _softplus(abv + dt_ref[...]), 0.0)
        beta = jnp.where((lane >= nh) & (lane < 2 * nh), _sigmoid(abv), 0.0)
        ii = lax.broadcasted_iota(jnp.int32, (tm, tm), 0)
        jj = lax.broadcasted_iota(jnp.int32, (tm, tm), 1)
        tri = jnp.where((ii >= jj) & ((ii ^ jj) < ck), 1.0, 0.0)
        gcum = _dot(tri, g, precision=HI)
        aux_ref[...] = g + beta + pltpu.roll(gcum, 2 * nh, 1)

    row = lambda c: pl.BlockSpec((tm, c), lambda i: (i, 0))
    vec = lambda r, c: pl.BlockSpec((r, c), lambda i: (0, 0))
    return pl.pallas_call(
        body, grid=(t // tm,),
        in_specs=[row(cw), pl.BlockSpec((HALO, cw), lambda i: (jnp.maximum(i * hb - 1, 0), 0)), vec(GDN_CONV, cw),
                  pl.BlockSpec((tm, LANES), lambda i: (i, ab_blk)), vec(1, LANES), vec(1, LANES)],
        out_specs=[row(cw), row(cw), row(LANES)],
        out_shape=[jax.ShapeDtypeStruct((t, cw), F32), jax.ShapeDtypeStruct((t, cw), F32),
                   jax.ShapeDtypeStruct((t, LANES), F32)],
        compiler_params=_cp(("parallel",)), name=name,
    )(qkvb, qkvb, conv_w, ab, alog_pad, dt_pad)


GDN_STEP_CHUNKS = 4


def _bdot(a, b, dims=NN):
    return _dot(a.astype(BF16), b.astype(BF16), dims)


def _split_bf16(x):
    hi = x.astype(BF16)
    return hi, (x - hi.astype(F32)).astype(BF16)


def _dot3(a, b, dims=NN):
    ah, al = _split_bf16(a)
    bh, bl = _split_bf16(b)
    return _dot(ah, bh, dims) + (_dot(ah, bl, dims) + _dot(al, bh, dims))


def _gdn_local(q, k, b, gc, gc_row):
    ck = GDN_CHUNK
    ii = lax.broadcasted_iota(jnp.int32, (ck, ck), 0)
    jj = lax.broadcasted_iota(jnp.int32, (ck, ck), 1)
    lower, strict = ii >= jj, ii > jj
    dmat = jnp.exp(jnp.where(lower, gc - gc_row, -jnp.inf))
    kk = _bdot(k, k, NT)
    lmat = jnp.where(strict, b * kk * dmat, 0.0)
    tinv = jnp.where(ii == jj, 1.0, 0.0) - lmat
    pw = lmat
    for _ in range(int(math.log2(ck)) - 1):
        pw = _dot3(pw, pw)
        tinv = tinv + _dot3(tinv, pw)
    gam = jnp.exp(gc)
    qk = _bdot(q, k, NT)
    gl = gc[ck - 1:ck, :]
    return dict(lower=lower, strict=strict, dmat=dmat, kk=kk, tinv=tinv, gam=gam, qk=qk, mm=qk * dmat,
                kdec=jnp.exp(gl - gc))


def _gdn_head_cols(h):
    return slice(h * GDN_HEAD_DIM, (h + 1) * GDN_HEAD_DIM)


def _gdn_chunk_inputs(x_ref, aux_ref, auxt_ref, g, h):
    nh, ck = GDN_HEADS, GDN_CHUNK
    gw = nh * GDN_HEAD_DIM
    rows = slice(g * ck, (g + 1) * ck)
    cols = _gdn_head_cols(h)
    q = x_ref[rows, cols]
    k = x_ref[rows, gw + cols.start:gw + cols.stop]
    v = x_ref[rows, 2 * gw + cols.start:2 * gw + cols.stop]
    b = aux_ref[rows, nh + h:nh + h + 1]
    gc = aux_ref[rows, 2 * nh + h:2 * nh + h + 1]
    gc_row = auxt_ref[g, 2 * nh + h:2 * nh + h + 1, :]
    return q, k, v, b, gc, gc_row


def _gdn_specs(t, widths, *, reverse=False, step_chunks=None):
    rows = (step_chunks or GDN_STEP_CHUNKS) * GDN_CHUNK
    nsteps = t // rows
    idx = (lambda i: (nsteps - 1 - i, 0)) if reverse else (lambda i: (i, 0))
    return [pl.BlockSpec((rows, w), idx) for w in widths]


def _gdn_local_fwd(qkvn, aux, aux_t, *, name, exchange=None):
    t = qkvn.shape[0]
    hd, nh, ck, gs = GDN_HEAD_DIM, GDN_HEADS, GDN_CHUNK, GDN_STEP_CHUNKS
    gw = nh * hd
    host = _ExchangeHost(exchange)
    grid = (t // (gs * ck),)

    def body(*refs):
        x_ref, aux_ref, auxt_ref = refs[:3]
        u_ref, w_ref, qd_ref, kd_ref, mm_ref, tinv_ref = refs[3 + host.n_in:9 + host.n_in]
        host.start(refs, 3, 9 + host.n_in, grid)
        for g in range(gs):
            rows = slice(g * ck, (g + 1) * ck)
            mms, tinvs = [], []
            for h in range(nh):
                q, k, v, b, gc, gc_row = _gdn_chunk_inputs(x_ref, aux_ref, auxt_ref, g, h)
                lc = _gdn_local(q, k, b, gc, gc_row)
                cols = _gdn_head_cols(h)
                u_ref[rows, cols] = _dot3(lc["tinv"], b * v)
                w_ref[rows, cols] = _dot3(lc["tinv"], (b * lc["gam"]) * k).astype(w_ref.dtype)
                qd_ref[rows, cols] = (lc["gam"] * q).astype(qd_ref.dtype)
                kd_ref[rows, cols] = (lc["kdec"] * k).astype(kd_ref.dtype)
                mms.append(lc["mm"])
                tinvs.append(lc["tinv"])
            mm_ref[rows, :] = jnp.concatenate(mms, axis=1).astype(mm_ref.dtype)
            tinv_ref[rows, :] = jnp.concatenate(tinvs, axis=1)
        host.wait(refs, 3, 9 + host.n_in, grid)

    sq = nh * ck
    outs = pl.pallas_call(
        body, grid=grid,
        in_specs=_gdn_specs(t, (3 * gw, LANES)) + [pl.BlockSpec((gs, 16, ck), lambda i: (i, 0, 0))] + host.in_specs,
        out_specs=_gdn_specs(t, (gw, gw, gw, gw, sq, sq)) + host.out_specs,
        out_shape=[jax.ShapeDtypeStruct((t, gw), F32)] + [jax.ShapeDtypeStruct((t, gw), BF16)] * 3
        + [jax.ShapeDtypeStruct((t, sq), BF16), jax.ShapeDtypeStruct((t, sq), F32)] + host.out_shapes,
        scratch_shapes=host.scratch,
        compiler_params=_cp(host.semantics(("parallel",))), name=name,
    )(qkvn, aux, aux_t, *host.ins)
    return (*outs[:6], list(outs[6:])) if exchange is not None else outs


def _gdn_seq_fwd(u, w, qd, kd, mm, aux, *, name):
    t = u.shape[0]
    hd, nh, ck, gs = GDN_HEAD_DIM, GDN_HEADS, GDN_CHUNK, GDN_STEP_CHUNKS
    gw = nh * hd
    sq = nh * ck

    def body(u_ref, w_ref, qd_ref, kd_ref, mm_ref, aux_ref, o_ref, vn_ref, sall_ref, s_ref):
        @pl.when(pl.program_id(0) == 0)
        def _():
            s_ref[...] = jnp.zeros_like(s_ref)

        for g in range(gs):
            rows = slice(g * ck, (g + 1) * ck)
            last = (g + 1) * ck - 1
            for h in range(nh):
                cols = _gdn_head_cols(h)
                st = s_ref[h]
                sall_ref[g, h] = st
                stb = st.astype(BF16)
                vn = u_ref[rows, cols] - _dot(w_ref[rows, cols], stb)
                vnb = vn.astype(BF16)
                vn_ref[rows, cols] = vnb
                o_ref[rows, cols] = _dot(qd_ref[rows, cols], stb) + _dot(mm_ref[rows, h * ck:(h + 1) * ck], vnb)
                gam_c = jnp.exp(aux_ref[last:last + 1, 2 * nh + h:2 * nh + h + 1])
                s_ref[h] = gam_c * st + _dot(kd_ref[rows, cols], vnb, TN)

    return pl.pallas_call(
        body, grid=(t // (gs * ck),),
        in_specs=_gdn_specs(t, (gw, gw, gw, gw, sq, LANES)),
        out_specs=_gdn_specs(t, (gw, gw)) + [pl.BlockSpec((gs, nh, hd, hd), lambda i: (i, 0, 0, 0))],
        out_shape=[jax.ShapeDtypeStruct((t, gw), F32), jax.ShapeDtypeStruct((t, gw), BF16),
                   jax.ShapeDtypeStruct((t // ck, nh, hd, hd), F32)],
        scratch_shapes=[pltpu.VMEM((nh, hd, hd), F32)],
        compiler_params=_cp(("arbitrary",)), name=name,
    )(u, w, qd, kd, mm, aux)


def _gdn_seq_bwd(do, w, qd, kd, mm, vn, s_all, aux, *, name):
    t = do.shape[0]
    hd, nh, ck, gs = GDN_HEAD_DIM, GDN_HEADS, GDN_CHUNK, GDN_STEP_CHUNKS
    gw = nh * hd
    sq = nh * ck
    nsteps = t // (gs * ck)

    def body(do_ref, w_ref, qd_ref, kd_ref, mm_ref, vn_ref, sall_ref, aux_ref, dvn_ref, dqd_ref, dkd_ref, dw_ref,
             dlast_ref, ds_ref):
        @pl.when(pl.program_id(0) == 0)
        def _():
            ds_ref[...] = jnp.zeros_like(ds_ref)

        lane = lax.broadcasted_iota(jnp.int32, (ck, LANES), 1)
        rowi = lax.broadcasted_iota(jnp.int32, (ck, LANES), 0)
        for g in reversed(range(gs)):
            rows = slice(g * ck, (g + 1) * ck)
            last = (g + 1) * ck - 1
            dlast = jnp.zeros((ck, LANES), F32)
            for h in range(nh):
                cols = _gdn_head_cols(h)
                st = sall_ref[g, h]
                stb = st.astype(BF16)
                dsn = ds_ref[h]
                dsb = dsn.astype(BF16)
                dob = do_ref[rows, cols].astype(BF16)
                dvn = _dot(mm_ref[rows, h * ck:(h + 1) * ck], dob, TN) + _dot(kd_ref[rows, cols], dsb)
                dvb = dvn.astype(BF16)
                dvn_ref[rows, cols] = dvn
                dqd_ref[rows, cols] = _dot(dob, stb, NT)
                dkd_ref[rows, cols] = _dot(vn_ref[rows, cols], dsb, NT)
                dw_ref[rows, cols] = -_dot(dvb, stb, NT)
                gam_c = jnp.exp(aux_ref[last:last + 1, 2 * nh + h:2 * nh + h + 1])
                dgam_c = jnp.sum(jnp.sum(dsn * st, axis=1, keepdims=True), axis=0, keepdims=True)
                dlast = dlast + jnp.where((rowi == ck - 1) & (lane == h), gam_c * dgam_c, 0.0)
                ds_ref[h] = _dot(qd_ref[rows, cols], dob, TN) + gam_c * dsn - _dot(w_ref[rows, cols], dvb, TN)
            dlast_ref[rows, :] = dlast

    return pl.pallas_call(
        body, grid=(nsteps,),
        in_specs=_gdn_specs(t, (gw, gw, gw, gw, sq, gw), reverse=True)
        + [pl.BlockSpec((gs, nh, hd, hd), lambda i: (nsteps - 1 - i, 0, 0, 0))] + _gdn_specs(t, (LANES,), reverse=True),
        out_specs=_gdn_specs(t, (gw, gw, gw, gw, LANES), reverse=True),
        out_shape=[jax.ShapeDtypeStruct((t, gw), F32)] * 4 + [jax.ShapeDtypeStruct((t, LANES), F32)],
        scratch_shapes=[pltpu.VMEM((nh, hd, hd), F32)],
        compiler_params=_cp(("arbitrary",)), name=name,
    )(do, w, qd, kd, mm, vn, s_all, aux)


def _gdn_local_bwd(qkvn, aux, aux_t, tinv, u, w, vn, do, dvn, dqd, dkd, dw, dlast, *, name):
    t = qkvn.shape[0]
    hd, nh, ck, gs = GDN_HEAD_DIM, GDN_HEADS, GDN_CHUNK, GDN_STEP_CHUNKS
    gw = nh * hd
    sq = nh * ck

    def body(x_ref, aux_ref, auxt_ref, tinv_ref, u_ref, w_ref, vn_ref, do_ref, dvn_ref, dqd_ref, dkd_ref, dw_ref,
             dlast_ref, dx_ref, daux_ref):
        lane = lax.broadcasted_iota(jnp.int32, (ck, LANES), 1)
        ones = jnp.ones((ck, LANES), F32)
        ii = lax.broadcasted_iota(jnp.int32, (ck, ck), 0)
        jj = lax.broadcasted_iota(jnp.int32, (ck, ck), 1)
        suffix = jnp.where(jj >= ii, 1.0, 0.0)
        for g in range(gs):
            rows = slice(g * ck, (g + 1) * ck)
            dgc_all = dlast_ref[rows, :]
            db_all = jnp.zeros((ck, LANES), F32)
            for h in range(nh):
                cols = _gdn_head_cols(h)
                q, k, v, b, gc, gc_row = _gdn_chunk_inputs(x_ref, aux_ref, auxt_ref, g, h)
                lc = _gdn_local(q, k, b, gc, gc_row)
                dmat, kk, gam, qk, kdec = (lc[key] for key in ("dmat", "kk", "gam", "qk", "kdec"))
                tinv_h = tinv_ref[rows, h * ck:(h + 1) * ck]
                u_h, w_h, vn_h = u_ref[rows, cols], w_ref[rows, cols], vn_ref[rows, cols]
                dqd_h, dkd_h = dqd_ref[rows, cols], dkd_ref[rows, cols]
                dm = jnp.where(lc["lower"], _bdot(do_ref[rows, cols], vn_h, NT), 0.0)
                drv = _dot3(tinv_h, dvn_ref[rows, cols], TN)
                drk = _dot3(tinv_h, dw_ref[rows, cols], TN)
                da = jnp.where(lc["strict"], -(_bdot(drv, u_h, NT) + _bdot(drk, w_h, NT)), 0.0)
                rs_rk = jnp.sum(drk * k, axis=-1, keepdims=True)
                db = (jnp.sum(drv * v, axis=-1, keepdims=True) + gam * rs_rk
                      + jnp.sum(da * kk * dmat, axis=-1, keepdims=True))
                e_mat = da * dmat * b
                dmd = dm * dmat
                dx_ref[rows, cols] = _bdot(dmd, k) + gam * dqd_h
                dx_ref[rows, gw + cols.start:gw + cols.stop] = (
                    (b * gam) * drk + _bdot(e_mat, k) + _bdot(e_mat, k, TN) + _bdot(dmd, q, TN) + kdec * dkd_h)
                dx_ref[rows, 2 * gw + cols.start:2 * gw + cols.stop] = b * drv
                f_mat = da * (b * kk) * dmat + dm * qk * dmat
                e_vec = jnp.sum(dkd_h * (kdec * k), axis=-1, keepdims=True)
                dgc = (b * gam * rs_rk + gam * jnp.sum(dqd_h * q, axis=-1, keepdims=True)
                       + jnp.sum(f_mat, axis=-1, keepdims=True) - _dot3(f_mat, ones, TN)[:, 0:1] - e_vec)
                is_last = lax.broadcasted_iota(jnp.int32, (ck, 1), 0) == ck - 1
                dgc = dgc + jnp.where(is_last, jnp.sum(e_vec, axis=0, keepdims=True), 0.0)
                dgc_all = dgc_all + jnp.where(lane == h, dgc, 0.0)
                db_all = db_all + jnp.where(lane == nh + h, db, 0.0)
            daux_ref[rows, :] = _dot3(suffix, dgc_all) + db_all

    return pl.pallas_call(
        body, grid=(t // (gs * ck),),
        in_specs=_gdn_specs(t, (3 * gw, LANES)) + [pl.BlockSpec((gs, 16, ck), lambda i: (i, 0, 0))]
        + _gdn_specs(t, (sq, gw, gw, gw, gw, gw, gw, gw, gw, LANES)),
        out_specs=_gdn_specs(t, (3 * gw, LANES)),
        out_shape=[jax.ShapeDtypeStruct((t, 3 * gw), F32), jax.ShapeDtypeStruct((t, LANES), F32)],
        compiler_params=_cp(("parallel",)), name=name,
    )(qkvn, aux, aux_t, tinv, u, w, vn, do, dvn, dqd, dkd, dw, dlast)


def _gdn_pre_bwd1(xc, dqkvn, daux, ab, alog_pad, dt_pad, *, name, ab_blk=0, tm=256):
    t, cw = xc.shape
    hd, nh = GDN_HEAD_DIM, GDN_HEADS
    gw = nh * hd
    tm = _blk(t, tm)

    def body(xc_ref, dy_ref, daux_ref, ab_ref, al_ref, dt_ref, dxc_ref, dab_ref, dal_ref, ddt_ref):
        i = pl.program_id(0)
        xc = xc_ref[...]
        sg = _sigmoid(xc)
        s = xc * sg
        dsilu = sg * (1.0 + xc * (1.0 - sg))
        for h in range(2 * nh):
            xh = s[:, h * hd:(h + 1) * hd]
            scale = hd ** -0.5 if h < nh else 1.0
            dyh = dy_ref[:, h * hd:(h + 1) * hd] * scale
            r = lax.rsqrt(jnp.sum(xh * xh, axis=-1, keepdims=True) + L2_EPS)
            dxh = r * dyh - xh * (r * r * r) * jnp.sum(dyh * xh, axis=-1, keepdims=True)
            dxc_ref[:, h * hd:(h + 1) * hd] = dxh * dsilu[:, h * hd:(h + 1) * hd]
        dxc_ref[:, 2 * gw:] = dy_ref[:, 2 * gw:] * dsilu[:, 2 * gw:]
        abv = ab_ref[...]
        dauxv = daux_ref[...]
        lane = lax.broadcasted_iota(jnp.int32, abv.shape, 1)
        is_a = lane < nh
        is_b = (lane >= nh) & (lane < 2 * nh)
        pre = abv + dt_ref[...]
        neg_ea = -jnp.exp(al_ref[...])
        d_a = jnp.where(is_a, dauxv * neg_ea * _sigmoid(pre), 0.0)
        beta = _sigmoid(abv)
        d_b = jnp.where(is_b, dauxv * beta * (1.0 - beta), 0.0)
        dab_ref[...] = (d_a + d_b).astype(dab_ref.dtype)
        dal = jnp.sum(jnp.where(is_a, dauxv * neg_ea * _softplus(pre), 0.0), axis=0, keepdims=True)
        ddt = jnp.sum(d_a, axis=0, keepdims=True)

        @pl.when(i == 0)
        def _():
            dal_ref[...] = dal
            ddt_ref[...] = ddt

        @pl.when(i > 0)
        def _():
            dal_ref[...] += dal
            ddt_ref[...] += ddt

    row = lambda c: pl.BlockSpec((tm, c), lambda i: (i, 0))
    vec = pl.BlockSpec((1, LANES), lambda i: (0, 0))
    return pl.pallas_call(
        body, grid=(t // tm,),
        in_specs=[row(cw), row(cw), row(LANES), pl.BlockSpec((tm, LANES), lambda i: (i, ab_blk)), vec, vec],
        out_specs=[row(cw), row(LANES), vec, vec],
        out_shape=[jax.ShapeDtypeStruct((t, cw), F32), jax.ShapeDtypeStruct((t, LANES), BF16),
                   jax.ShapeDtypeStruct((1, LANES), F32), jax.ShapeDtypeStruct((1, LANES), F32)],
        compiler_params=_cp(("arbitrary",)), name=name,
    )(xc, dqkvn, daux, ab, alog_pad, dt_pad)


def _gdn_pre_bwd2(dxc, qkvb, conv_w, *, name, tm=256):
    t, cw = dxc.shape
    tm = _blk(t, tm)
    hb = tm // HALO
    nblk = t // tm

    def body(d_ref, dn_ref, x_ref, xp_ref, w_ref, dx_ref, dw_ref):
        i = pl.program_id(0)
        dcur = d_ref[...]
        dnxt = jnp.where(i < nblk - 1, dn_ref[...], 0.0)
        cur = x_ref[...]
        prev = jnp.where(i > 0, xp_ref[...], 0.0)
        dx = None
        dws = []
        for tap in range(GDN_CONV):
            j = GDN_CONV - 1 - tap
            term = w_ref[tap:tap + 1, :] * _shift_up(dcur, dnxt, j)
            dx = term if dx is None else dx + term
            dws.append(jnp.sum(dcur * _shift_down(cur, prev, j), axis=0, keepdims=True))
        dx_ref[...] = dx.astype(dx_ref.dtype)
        dw = jnp.concatenate(dws, axis=0)

        @pl.when(i == 0)
        def _():
            dw_ref[...] = dw

        @pl.when(i > 0)
        def _():
            dw_ref[...] += dw

    row = pl.BlockSpec((tm, cw), lambda i: (i, 0))
    wsp = pl.BlockSpec((GDN_CONV, cw), lambda i: (0, 0))
    return pl.pallas_call(
        body, grid=(nblk,),
        in_specs=[row, pl.BlockSpec((HALO, cw), lambda i: (jnp.minimum((i + 1) * hb, t // HALO - 1), 0)),
                  row, pl.BlockSpec((HALO, cw), lambda i: (jnp.maximum(i * hb - 1, 0), 0)), wsp],
        out_specs=[row, wsp],
        out_shape=[jax.ShapeDtypeStruct((t, cw), BF16), jax.ShapeDtypeStruct((GDN_CONV, cw), F32)],
        compiler_params=_cp(("arbitrary",)), name=name,
    )(dxc, dxc, qkvb, qkvb, conv_w)


def _gdn_post_fwd(o, z, norm_w, *, name, tm=512):
    t, gw = o.shape
    hd, nh = GDN_HEAD_DIM, GDN_HEADS
    tm = _blk(t, tm)

    def body(o_ref, z_ref, w_ref, y_ref):
        zv = z_ref[...]
        sz = zv * _sigmoid(zv)
        for h in range(nh):
            oh = o_ref[:, h * hd:(h + 1) * hd]
            r = lax.rsqrt(jnp.mean(oh * oh, axis=-1, keepdims=True) + RMS_EPS)
            y_ref[:, h * hd:(h + 1) * hd] = (oh * r * w_ref[...] * sz[:, h * hd:(h + 1) * hd]).astype(y_ref.dtype)

    row = pl.BlockSpec((tm, gw), lambda i: (i, 0))
    return pl.pallas_call(
        body, grid=(t // tm,), in_specs=[row, row, pl.BlockSpec((1, hd), lambda i: (0, 0))], out_specs=row,
        out_shape=jax.ShapeDtypeStruct((t, gw), BF16), compiler_params=_cp(("parallel",)), name=name,
    )(o, z, norm_w)


def _gdn_post_bwd(dy, o, z, norm_w, *, name, tm=512):
    t, gw = o.shape
    hd, nh = GDN_HEAD_DIM, GDN_HEADS
    tm = _blk(t, tm)

    def body(dy_ref, o_ref, z_ref, w_ref, do_ref, dz_ref, dw_ref):
        i = pl.program_id(0)
        zv = z_ref[...]
        sg = _sigmoid(zv)
        sz = zv * sg
        dsz = sg * (1.0 + zv * (1.0 - sg))
        dw = None
        for h in range(nh):
            sl = slice(h * hd, (h + 1) * hd)
            oh = o_ref[:, sl]
            dyh = dy_ref[:, sl].astype(F32)
            r = lax.rsqrt(jnp.mean(oh * oh, axis=-1, keepdims=True) + RMS_EPS)
            xh = oh * r
            dz_ref[:, sl] = (dyh * xh * w_ref[...] * dsz[:, sl]).astype(dz_ref.dtype)
            dn = dyh * sz[:, sl]
            dxh = dn * w_ref[...]
            do_ref[:, sl] = r * (dxh - xh * jnp.mean(dxh * xh, axis=-1, keepdims=True))
            part = jnp.sum(dn * xh, axis=0, keepdims=True)
            dw = part if dw is None else dw + part

        @pl.when(i == 0)
        def _():
            dw_ref[...] = dw

        @pl.when(i > 0)
        def _():
            dw_ref[...] += dw

    row = pl.BlockSpec((tm, gw), lambda i: (i, 0))
    vec = pl.BlockSpec((1, hd), lambda i: (0, 0))
    return pl.pallas_call(
        body, grid=(t // tm,), in_specs=[row, row, row, vec], out_specs=[row, row, vec],
        out_shape=[jax.ShapeDtypeStruct((t, gw), F32), jax.ShapeDtypeStruct((t, gw), BF16),
                   jax.ShapeDtypeStruct((1, hd), F32)],
        compiler_params=_cp(("arbitrary",)), name=name,
    )(dy, o, z, norm_w)


IN_NAMES = ("q_a", "kv_a", "qkv_b", "ab", "z", "q_c", "gates")
CAT_NAMES = ("gates", "q_a", "qkv_b", "z", "q_c", "kv_a", "ab")
AB_PAD = 256


def _in_widths(d):
    gw = GDN_HEADS * GDN_HEAD_DIM
    return dict(q_a=SWA_Q_HEADS * SWA_HEAD_DIM, kv_a=2 * SWA_KV_HEADS * SWA_HEAD_DIM, qkv_b=3 * gw, ab=2 * GDN_HEADS,
                z=gw, q_c=XA_HEADS * XA_HEAD_DIM, gates=3 * d)


def _ranges(names, widths):
    out, start = {}, 0
    for k in names:
        out[k] = (start, widths[k])
        start += widths[k]
    return out, start


def _cat_ranges(d):
    widths = dict(_in_widths(d), ab=AB_PAD)
    return _ranges(CAT_NAMES, widths)


def _to_cat(shards):
    ns, d, n = shards.shape
    src, _ = _ranges(IN_NAMES, _in_widths(d))
    cols = []
    for k in CAT_NAMES:
        lo, hi = src[k][0], src[k][0] + src[k][1]
        for s in range(ns):
            a, b = max(lo, s * n), min(hi, (s + 1) * n)
            if a < b:
                cols.append(shards[s][:, a - s * n:b - s * n])
    cols.append(jnp.zeros((d, AB_PAD - src["ab"][1]), shards.dtype))
    return jnp.concatenate(cols, axis=1)


def _from_cat(w_cat):
    d = w_cat.shape[0]
    src, total = _ranges(IN_NAMES, _in_widths(d))
    cat, _ = _cat_ranges(d)
    n = total // N_SHARDS
    shards = []
    for s in range(N_SHARDS):
        pieces = []
        for k in IN_NAMES:
            a, b = max(s * n, src[k][0]), min((s + 1) * n, src[k][0] + src[k][1])
            if a < b:
                pieces.append(w_cat[:, cat[k][0] + a - src[k][0]:cat[k][0] + b - src[k][0]])
        shards.append(jnp.concatenate(pieces, axis=1))
    return jnp.stack(shards)


def _pad_cols(a, width):
    return jnp.pad(a, ((0, 0), (0, width - a.shape[1])))


def _relu2_epilogue(acc):
    r = jnp.maximum(acc, 0.0)
    return acc, r * r


def _add_epilogue(acc, res):
    return (acc + res,)


def _drelu2_epilogue(acc, u):
    return (acc * (2.0 * jnp.maximum(u.astype(F32), 0.0)),)


def _local_step(x, mem, tgt, wts, small, comm=None):
    t, d = x.shape
    nh = GDN_HEADS
    w_cat = wts["w_cat"]
    cat, cat_w = _cat_ranges(d)
    assert w_cat.shape == (d, cat_w)
    alog_pad = _pad_cols(small["a_log"], LANES)
    dt_pad = _pad_cols(small["dt_bias"], LANES)
    kvw = cat["kv_a"][1]
    assert cat["ab"][0] == cat["kv_a"][0] + kvw
    ab_blk = kvw // LANES

    n = _rms_fwd(x, small["g_mix"], name="rms_mix")
    q_a = _mm(n, w_cat, b_window=cat["q_a"], out_dtypes=(BF16,), name="in_q_a")
    kv_a, ab = _mm(n, w_cat, b_window=(cat["kv_a"][0], kvw + AB_PAD), out_dtypes=(BF16, F32), name="in_kv_ab")
    qkvb = _mm(n, w_cat, b_window=cat["qkv_b"], tn=512, name="in_qkv_b")
    z = _mm(n, w_cat, b_window=cat["z"], name="in_z")
    q_c = _mm(n, w_cat, b_window=cat["q_c"], out_dtypes=(BF16,), name="in_q_c")
    gates = _mm(n, w_cat, b_window=cat["gates"], name="in_gates")
    y_a, lse = _swa_fwd(q_a, kv_a, small["sinks"], name="swa_fwd")
    xc, qkvn, aux = _gdn_pre_fwd(qkvb, small["conv_w"], ab, alog_pad, dt_pad, ab_blk=ab_blk, name="gdn_pre_fwd")
    aux_t = aux[:, :16].reshape(t // GDN_CHUNK, GDN_CHUNK, 16).transpose(0, 2, 1)
    if comm is None:
        gdn_u, gdn_w, gdn_qd, gdn_kd, gdn_mm, gdn_tinv = _gdn_local_fwd(qkvn, aux, aux_t, name="gdn_local_fwd")
    else:
        gdn_u, gdn_w, gdn_qd, gdn_kd, gdn_mm, gdn_tinv, landed = _gdn_local_fwd(
            qkvn, aux, aux_t, name="gdn_local_fwd", exchange=comm.gather_exchange())
        wts = dict(wts, **comm.gathered(landed))
    o_b, gdn_vn, s_all = _gdn_seq_fwd(gdn_u, gdn_w, gdn_qd, gdn_kd, gdn_mm, aux, name="gdn_seq_fwd")
    y_b = _gdn_post_fwd(o_b, z, small["gdn_norm_w"], name="gdn_post_fwd")
    nmem = _rms_fwd(mem, small["g_mem"], name="rms_mem")
    mkv = _mm(nmem, wts["w_mem_kv"], out_dtypes=(BF16,), name="mem_kv")
    y_c = _xa_fwd(q_c, mkv, name="xa_fwd")
    ys = (y_a, y_b, y_c)
    w_ups = (wts["w_swa_up"], wts["w_gdn_up"], wts["w_xa_up"])
    merged = _merge_fwd(ys, w_ups, gates, name="merge_fwd")
    h1 = _mm(merged, wts["w_out"], extras=(x,), epilogue=_add_epilogue, name="out_proj")
    n2 = _rms_fwd(h1, small["g_mlp"], name="rms_mlp")
    u, act = _mm(n2, wts["w_mlp_in"], b_sharded=True, out_dtypes=(BF16, BF16), epilogue=_relu2_epilogue, name="mlp_in")
    h2 = _mm(act, wts["w_mlp_out"], extras=(h1,), epilogue=_add_epilogue, name="mlp_out")
    dh2, dh2_b, dg_final, loss = _final_loss(h2, small["g_final"], tgt, name="final_loss")

    grads = {"g_final": dg_final}
    du = _mm(dh2_b, wts["w_mlp_out"], tb=True, out_dtypes=(BF16,), extras=(u,), epilogue=_drelu2_epilogue, name="d_mlp_act")
    grads["w_mlp_out"] = _mm(act, dh2_b, ta=True, out_dtypes=(BF16,), name="dw_mlp_out")
    grads["w_mlp_in"] = _mm(n2, du, ta=True, out_sharded=True, out_dtypes=(BF16,), name="dw_mlp_in")
    dn2 = _mm(du, wts["w_mlp_in"], tb=True, b_sharded=True, name="d_mlp_in")
    dh1, dh1_b, grads["g_mlp"] = _rms_bwd(dn2, h1, small["g_mlp"], dh2, name="rms_mlp_bwd")
    dmerged = _mm(dh1_b, wts["w_out"], tb=True, name="d_out_proj")
    grads["w_out"] = _mm(merged, dh1_b, ta=True, out_dtypes=(BF16,), name="dw_out")
    dus_and_dgates = _merge_bwd(ys, w_ups, gates, dmerged, name="merge_bwd")
    dus, dgates = dus_and_dgates[:3], dus_and_dgates[3:]
    dys = []
    for y, du_i, w_up, key in zip(ys, dus, w_ups, ("w_swa_up", "w_gdn_up", "w_xa_up")):
        dys.append(_mm(du_i, w_up, tb=True, b_sharded=True, out_dtypes=(BF16,), name="d_" + key))
        grads[key] = _mm(y, du_i, ta=True, out_sharded=True, out_dtypes=(BF16,), name="dw_" + key[2:])
    dq_a, dkv_a, grads["sinks"] = _swa_bwd(q_a, kv_a, small["sinks"], y_a, lse, dys[0], name="swa_bwd")
    do_b, dz, grads["gdn_norm_w"] = _gdn_post_bwd(dys[1], o_b, z, small["gdn_norm_w"], name="gdn_post_bwd")
    dvn, dqd, dkd, dw_, dlast = _gdn_seq_bwd(do_b, gdn_w, gdn_qd, gdn_kd, gdn_mm, gdn_vn, s_all, aux, name="gdn_seq_bwd")
    dqkvn, daux = _gdn_local_bwd(qkvn, aux, aux_t, gdn_tinv, gdn_u, gdn_w, gdn_vn, do_b, dvn, dqd, dkd, dw_, dlast,
                                 name="gdn_local_bwd")
    dxc, dab, dalog, ddt = _gdn_pre_bwd1(xc, dqkvn, daux, ab, alog_pad, dt_pad, ab_blk=ab_blk, name="gdn_pre_bwd1")
    grads["a_log"], grads["dt_bias"] = dalog[:, :nh], ddt[:, :nh]
    dqkvb, grads["conv_w"] = _gdn_pre_bwd2(dxc, qkvb, small["conv_w"], name="gdn_pre_bwd2")
    dq_c, dmkv = _xa_bwd(q_c, mkv, dys[2], name="xa_bwd")
    grads["w_mem_kv"] = _mm(nmem, dmkv, ta=True, out_dtypes=(BF16,), name="dw_mem_kv")
    dnmem = _mm(dmkv, wts["w_mem_kv"], tb=True, name="d_mem_kv")
    _, _, grads["g_mem"] = _rms_bwd(dnmem, mem, small["g_mem"], jnp.zeros_like(mem), name="rms_mem_bwd")
    dp = jnp.concatenate([*dgates, dq_a, dqkvb, dz, dq_c, dkv_a, dab, jnp.zeros((t, AB_PAD - LANES), BF16)], axis=1)
    if comm is None:
        grads["w_cat"] = _mm(n, dp, ta=True, out_dtypes=(BF16,), name="dw_in")
        dn = _mm(dp, w_cat, tb=True, name="d_in_proj")
    else:
        rest = [k for k in comm.names if k != "w_in"]
        s1_rest = comm.pair_sums([comm.shard_major(k, grads.pop(k)) for k in rest], "rest")
        dw_cat, rcv_rest = _mm(n, dp, ta=True, out_dtypes=(BF16,), name="dw_in", exchange=_chip_exchange(s1_rest))
        s1_in = comm.pair_sums([_from_cat(dw_cat)], "in")
        dn, rcv_in = _mm(dp, w_cat, tb=True, name="d_in_proj", exchange=_chip_exchange(s1_in))
        grads.update(zip(["w_in"] + rest, comm.finish(s1_in + s1_rest, rcv_in + rcv_rest)))
    dx, _, grads["g_mix"] = _rms_bwd(dn, x, small["g_mix"], dh1, name="rms_mix_bwd")
    return loss, dx, grads


HBM_SPEC = pl.BlockSpec(memory_space=pltpu.HBM)
VMEM_SPEC = pl.BlockSpec(memory_space=pltpu.VMEM)
N_CHIPS = N_SHARDS
N_DEV = 8
DMA_CHUNK_BYTES = 1 << 20


def _place():
    return lax.axis_index("x"), lax.axis_index("y"), lax.axis_index("c")


def _other_chips(x, y):
    return [(1 - x, y), (x, 1 - y), (1 - x, 1 - y)]


def _n_chunks(rows, row_bytes):
    n = 1
    while rows % (2 * n) == 0 and (rows // (2 * n)) % 16 == 0 and (rows // n) * row_bytes > DMA_CHUNK_BYTES:
        n *= 2
    return n


def _sem_scratch(n_remote, n_local):
    return [pltpu.SemaphoreType.DMA((max(n_remote, 1),)), pltpu.SemaphoreType.DMA((max(n_remote, 1),)),
            pltpu.SemaphoreType.DMA((max(n_local, 1),))]


def _all_gather_weights(shards, *, name):
    first = _exchange_call(_gather_over_ici(shards), name=name + "_ici")
    return _exchange_call(_gather_pass_on(first), name=name + "_pass")


def _gather_over_ici(shards):
    plan = _half_chunks(shards, 0)

    def copies_of(in_refs, out_refs, place):
        x, y, c = place
        remote, local = [], []
        for i, r0, nr in plan:
            rh = shards[i].shape[0] // 2
            mine = pl.ds(c * rh + r0, nr)
            for chip in _other_chips(x, y):
                remote.append((in_refs[i].at[mine], out_refs[i].at[2 * x + y, mine], (*chip, c)))
            for half in range(2):
                rows = pl.ds(half * rh + r0, nr)
                local.append((in_refs[i].at[rows], out_refs[i].at[2 * x + y, rows]))
        return remote, local

    shapes = tuple(jax.ShapeDtypeStruct((N_CHIPS, *s.shape), s.dtype) for s in shards)
    return Exchange(tuple(shards), shapes, 3 * len(plan), 2 * len(plan), copies_of)


def _gather_pass_on(arrived):
    plan = _half_chunks([jax.ShapeDtypeStruct(a.shape[1:], a.dtype) for a in arrived], 0)

    def copies_of(in_refs, out_refs, place):
        x, y, c = place
        remote = []
        for i, r0, nr in plan:
            mine = pl.ds(c * (arrived[i].shape[1] // 2) + r0, nr)
            for chip in _other_chips(x, y):
                rows = out_refs[i].at[2 * chip[0] + chip[1], mine]
                remote.append((rows, rows, (x, y, 1 - c)))
        return remote, []

    shapes = tuple(jax.ShapeDtypeStruct(a.shape, a.dtype) for a in arrived)
    return Exchange(tuple(arrived), shapes, 3 * len(plan), 0, copies_of, tuple((i, i) for i in range(len(arrived))))


def _exchange_call(ex, *, name):
    n_in, n_out = len(ex.ins), len(ex.out_shapes)

    def body(*refs):
        cps = _exchange_copies(ex, refs[:n_in], refs[n_in:n_in + n_out], refs[n_in + n_out:])
        for cp in cps:
            cp.start()
        for cp in cps:
            cp.wait()

    return pl.pallas_call(
        body, out_shape=list(ex.out_shapes), in_specs=[HBM_SPEC] * n_in, out_specs=[HBM_SPEC] * n_out,
        scratch_shapes=_sem_scratch(ex.n_remote, ex.n_local), input_output_aliases=dict(ex.aliases), name=name,
    )(*ex.ins)


def _half_chunks(arrs, row_axis):
    plan = []
    for i, a in enumerate(arrs):
        rh = a.shape[row_axis] // 2
        row_bytes = a.dtype.itemsize * math.prod(a.shape) // a.shape[row_axis]
        nch = _n_chunks(rh, row_bytes)
        plan += [(i, q * (rh // nch), rh // nch) for q in range(nch)]
    return plan


def _sibling_halves(gs, *, name):
    plan = _half_chunks(gs, 1)

    def copies_of(in_refs, out_refs, place):
        x, y, c = place
        out = []
        for i, r0, nr in plan:
            rh = gs[i].shape[1] // 2
            out.append((in_refs[i].at[:, pl.ds((1 - c) * rh + r0, nr), :], out_refs[i].at[:, pl.ds(r0, nr), :],
                        (x, y, 1 - c)))
        return out, []

    shapes = tuple(jax.ShapeDtypeStruct((g.shape[0], g.shape[1] // 2, g.shape[2]), g.dtype) for g in gs)
    return _exchange_call(Exchange(tuple(gs), shapes, len(plan), 0, copies_of), name=name)


def _chip_exchange(s1s):
    plan = _half_chunks([jax.ShapeDtypeStruct((2 * s.shape[1], s.shape[2]), s.dtype) for s in s1s], 0)

    def copies_of(in_refs, out_refs, place):
        x, y, c = place
        out = []
        for i, r0, nr in plan:
            for j, chip in enumerate(_other_chips(x, y)):
                out.append((in_refs[i].at[2 * chip[0] + chip[1], pl.ds(r0, nr), :], out_refs[i].at[j, pl.ds(r0, nr), :],
                            (*chip, c)))
        return out, []

    shapes = tuple(jax.ShapeDtypeStruct((3, *s.shape[1:]), s.dtype) for s in s1s)
    return Exchange(tuple(s1s), shapes, 3 * len(plan), 0, copies_of)


def _join_halves(gs, *, name):
    plan = _half_chunks(gs, 0)

    def copies_of(in_refs, out_refs, place):
        x, y, c = place
        out = []
        for i, r0, nr in plan:
            rows = out_refs[i].at[pl.ds(c * (gs[i].shape[0] // 2) + r0, nr), :]
            out.append((rows, rows, (x, y, 1 - c)))
        return out, []

    shapes = tuple(jax.ShapeDtypeStruct(g.shape, g.dtype) for g in gs)
    aliases = tuple((i, i) for i in range(len(gs)))
    return _exchange_call(Exchange(tuple(gs), shapes, len(plan), 0, copies_of, aliases), name=name)


def _row_block(rows, cols):
    tb = rows
    while tb % 32 == 0 and tb * cols * 4 > (2 << 20):
        tb //= 2
    return tb


def _pair_sum(g, sib, core, *, name):
    ns, r, c = g.shape
    rh = r // 2
    tb = _row_block(rh, c)
    nb = rh // tb

    def body(core_ref, g_ref, s_ref, o_ref):
        o_ref[...] = (g_ref[...].astype(F32) + s_ref[...].astype(F32)).astype(o_ref.dtype)

    mine = pl.BlockSpec((None, tb, c), lambda s, i, core_ref: (s, core_ref[0] * nb + i, 0))
    half = pl.BlockSpec((None, tb, c), lambda s, i, core_ref: (s, i, 0))
    return pl.pallas_call(
        body, grid_spec=pltpu.PrefetchScalarGridSpec(num_scalar_prefetch=1, grid=(ns, nb), in_specs=[mine, half],
                                                     out_specs=half),
        out_shape=jax.ShapeDtypeStruct((ns, rh, c), BF16), compiler_params=_cp(("parallel", "parallel")), name=name,
    )(core, g, sib)


def _chip_sum(s1, rcv, where, *, name):
    _, rh, c = s1.shape
    tb = _row_block(rh, c)
    nb = rh // tb

    def body(where_ref, own_ref, r0_ref, r1_ref, r2_ref, o_ref):
        acc = own_ref[...].astype(F32)
        for r in (r0_ref, r1_ref, r2_ref):
            acc = acc + r[...].astype(F32)
        o_ref[...] = acc

    own = pl.BlockSpec((None, tb, c), lambda i, w: (w[1], i, 0))
    got = [pl.BlockSpec((None, tb, c), functools.partial(lambda i, w, j: (j, i, 0), j=j)) for j in range(3)]
    return pl.pallas_call(
        body, grid_spec=pltpu.PrefetchScalarGridSpec(
            num_scalar_prefetch=1, grid=(nb,), in_specs=[own] + got,
            out_specs=pl.BlockSpec((tb, c), lambda i, w: (w[0] * nb + i, 0))),
        out_shape=jax.ShapeDtypeStruct((2 * rh, c), F32), compiler_params=_cp(("parallel",)), name=name,
    )(where, s1, rcv, rcv, rcv)


def _all_gather_small(blk, *, name):
    r = blk.shape[0]

    def body(b_ref, out_ref, send_sems, recv_sems):
        x, y, c = _place()
        me = 4 * x + 2 * y + c
        out_ref[me] = b_ref[...]
        sends = []
        for k in range(1, N_DEV):
            peer = (x ^ (k >> 2), y ^ ((k >> 1) & 1), c ^ (k & 1))
            sends.append(pltpu.make_async_remote_copy(src_ref=b_ref, dst_ref=out_ref.at[me], send_sem=send_sems.at[k - 1],
                                                      recv_sem=recv_sems.at[k - 1], device_id=peer, device_id_type=MESH))
        for cp in sends:
            cp.start()
        for k in range(1, N_DEV):
            rows = out_ref.at[me ^ k]
            pltpu.make_async_remote_copy(src_ref=rows, dst_ref=rows, send_sem=send_sems.at[k - 1],
                                         recv_sem=recv_sems.at[k - 1], device_id=(x, y, c), device_id_type=MESH).wait_recv()
        for cp in sends:
            cp.wait_send()

    return pl.pallas_call(
        body, out_shape=jax.ShapeDtypeStruct((N_DEV, r, LANES), blk.dtype), in_specs=[VMEM_SPEC], out_specs=VMEM_SPEC,
        scratch_shapes=[pltpu.SemaphoreType.DMA((N_DEV - 1,)), pltpu.SemaphoreType.DMA((N_DEV - 1,))],
        name=name,
    )(blk)


def _sum_rows(parts, out_dtype, *, name, tb=1024):
    rows = parts[0].shape[0]
    tb = _blk(rows, tb)

    def body(*refs):
        acc = refs[0][...].astype(F32)
        for r in refs[1:-1]:
            acc = acc + r[...].astype(F32)
        refs[-1][...] = acc.astype(refs[-1].dtype)

    spec = pl.BlockSpec((tb, LANES), lambda i: (i, 0))
    return pl.pallas_call(
        body, grid=(rows // tb,), in_specs=[spec] * len(parts), out_specs=spec,
        out_shape=jax.ShapeDtypeStruct((rows, LANES), out_dtype), compiler_params=_cp(("parallel",)), name=name,
    )(*parts)


BIG = (
    ("w_in", 1), ("w_mem_kv", 0), ("w_swa_up", 1), ("w_gdn_up", 1), ("w_xa_up", 1), ("w_out", 0), ("w_mlp_in", 1),
    ("w_mlp_out", 0))


class _Comm:
    def __init__(self, late_shards, core, where):
        self.names = [k for k, _ in BIG]
        self.axis = dict(BIG)
        self.late_names = list(late_shards)
        self.late_shards = [late_shards[k] for k in self.late_names]
        self.core, self.where = core, where

    def gather_exchange(self):
        return _gather_over_ici(self.late_shards)

    def gathered(self, landed):
        whole = _exchange_call(_gather_pass_on(landed), name="ag_rest_pass")
        return {k: (g.reshape(-1, g.shape[2]) if self.axis[k] == 0 else g) for k, g in zip(self.late_names, whole)}

    def shard_major(self, k, grad):
        return grad.reshape(N_CHIPS, -1, grad.shape[-1]) if self.axis[k] == 0 else grad

    def pair_sums(self, gs, tag):
        sibs = _sibling_halves(gs, name=f"rs_sibling_{tag}")
        return [_pair_sum(g, s, self.core, name=f"rs_pair_sum_{tag}{i}") for i, (g, s) in enumerate(zip(gs, sibs))]

    def finish(self, s1s, rcvs):
        halves = [_chip_sum(s1, rcv, self.where, name=f"rs_chip_sum_{i}") for i, (s1, rcv) in enumerate(zip(s1s, rcvs))]
        return _join_halves(halves, name="rs_join_halves")
SMALL = ("g_mix", "sinks", "a_log", "dt_bias", "gdn_norm_w", "g_mem", "g_mlp", "g_final")


def _rows128(a, rows):
    flat = a.reshape(-1)
    return jnp.pad(flat, (0, rows * LANES - flat.shape[0])).reshape(rows, LANES)


def kernel(x, mem, g_mix, w_in, sinks, conv_w, a_log, dt_bias, gdn_norm_w, g_mem, w_mem_kv, w_swa_up, w_gdn_up, w_xa_up, w_out, g_mlp, w_mlp_in, w_mlp_out, g_final, loss_target, m_g_mix, m_w_in, m_sinks, m_conv_w, m_a_log, m_dt_bias, m_gdn_norm_w, m_g_mem, m_w_mem_kv, m_w_swa_up, m_w_gdn_up, m_w_xa_up, m_w_out, m_g_mlp, m_w_mlp_in, m_w_mlp_out, m_g_final, v_g_mix, v_w_in, v_sinks, v_conv_w, v_a_log, v_dt_bias, v_gdn_norm_w, v_g_mem, v_w_mem_kv, v_w_swa_up, v_w_gdn_up, v_w_xa_up, v_w_out, v_g_mlp, v_w_mlp_in, v_w_mlp_out, v_g_final):
    given = dict(locals())
    xi, yi, ci = _place()
    chip = 2 * xi + yi
    core = jnp.reshape(ci, (1,)).astype(jnp.int32)
    where = jnp.stack([ci, chip]).astype(jnp.int32)

    shards = {k: given[k][0].astype(BF16) for k, _ in BIG}
    wts = {"w_cat": _to_cat(_all_gather_weights([shards.pop("w_in")], name="ag_w_in")[0])}
    comm = _Comm(shards, core, where)
    conv_shard = conv_w[0]
    conv_rows = -(-conv_shard.size // (8 * LANES)) * 8
    conv_all = _all_gather_small(_rows128(conv_shard, conv_rows), name="ag_conv")
    conv_full = jnp.concatenate(
        [conv_all[2 * s].reshape(-1)[:conv_shard.size].reshape(conv_shard.shape) for s in range(N_CHIPS)], axis=1)

    small = {k: given[k].reshape(1, -1) for k in SMALL}
    small["conv_w"] = conv_full
    loss_row, dx, grads = _local_step(x[0], mem[0], loss_target[0], wts, small, comm)
    big_grads = {k: grads[k] for k, _ in BIG}

    layout = [("loss", loss_row[:, :1])] + [(k, grads[k]) for k in SMALL] + [("conv_w", grads["conv_w"])]
    rows = [-(-a.size // LANES) for _, a in layout]
    blk_rows = -(-sum(rows) // 8) * 8
    blk = jnp.concatenate([_rows128(a.astype(F32), n) for (_, a), n in zip(layout, rows)]
                          + [jnp.zeros((blk_rows - sum(rows), LANES), F32)], axis=0)
    gathered = _all_gather_small(blk, name="ag_small_grads")
    reduced = _sum_rows([gathered[i] for i in range(N_DEV)], F32, name="small_grad_sum")
    small_grads, start = {}, 0
    for (k, a), n in zip(layout, rows):
        small_grads[k] = reduced[start:start + n].reshape(-1)[:a.size].reshape(a.shape)
        start += n
    loss = small_grads["loss"].reshape(())
    cw = conv_shard.shape[1]
    conv_grad = lax.dynamic_slice_in_dim(small_grads["conv_w"], chip * cw, cw, axis=1)

    names = ["g_mix", "w_in", "sinks", "conv_w", "a_log", "dt_bias", "gdn_norm_w", "g_mem", "w_mem_kv", "w_swa_up",
             "w_gdn_up", "w_xa_up", "w_out", "g_mlp", "w_mlp_in", "w_mlp_out", "g_final"]
    out_g, out_d, out_m, out_v = [], [], [], []
    for k in names:
        w, m, v = given[k], given["m_" + k], given["v_" + k]
        if k in big_grads:
            g2 = big_grads[k]
        elif k == "conv_w":
            g2 = conv_grad
        else:
            g2 = small_grads[k]
        as_given = (lambda a: a.reshape(1, -1)) if w.ndim == 1 else (lambda a: a)
        delta, new_m, new_v = _adamw(as_given(w), g2, as_given(m), as_given(v), name="adamw_" + k)
        out_g.append(g2.reshape(w.shape))
        out_d.append(delta.reshape(w.shape))
        out_m.append(new_m.reshape(w.shape))
        out_v.append(new_v.reshape(w.shape))
    return (loss, dx[None], *out_g, *out_d, *out_m, *out_v)
```

```python
import functools
import math
from typing import Callable, NamedTuple

import jax
import jax.numpy as jnp
from jax import lax
from jax.experimental import pallas as pl
from jax.experimental.pallas import tpu as pltpu

F32 = jnp.float32
BF16 = jnp.bfloat16
HI = lax.Precision.HIGHEST
MESH = pl.DeviceIdType.MESH

SWA_Q_HEADS = 16
SWA_KV_HEADS = 2
SWA_HEAD_DIM = 64
SWA_WINDOW = 128
SWA_SCALE = SWA_HEAD_DIM ** -0.5
assert math.frexp(SWA_SCALE)[0] == 0.5
GDN_HEADS = 4
GDN_HEAD_DIM = 128
GDN_CONV = 4
GDN_CHUNK = 64
XA_HEADS = 4
XA_HEAD_DIM = 128
RMS_EPS = 1e-6
L2_EPS = 1e-6
ADAM_LR = 0.001
ADAM_B1 = 0.9
ADAM_B2 = 0.999
ADAM_EPS = 1e-08
ADAM_WD = 0.01
ADAM_STEP = 10

LANES = 128
N_SHARDS = 4
VMEM_LIMIT = 56 * 1024 * 1024

NT = (((1,), (1,)), ((), ()))
TN = (((0,), (0,)), ((), ()))
NN = (((1,), (0,)), ((), ()))


def _cp(sem=None):
    return pltpu.CompilerParams(dimension_semantics=sem, vmem_limit_bytes=VMEM_LIMIT)


def _blk(dim, pref):
    if dim <= pref:
        return dim
    b = (pref // LANES) * LANES
    while dim % b:
        b -= LANES
    assert b > 0, (dim, pref)
    return b


def _dot(a, b, dims=NN, precision=None):
    return lax.dot_general(a, b, dims, precision=precision, preferred_element_type=F32)


def _sigmoid(x):
    return 1.0 / (1.0 + jnp.exp(-x))


MM_TK_BYTES = 4096


def _mm(a, b, *, name, ta=False, tb=False, out_dtypes=(F32,), epilogue=None, extras=(), tm=1024, tn=1024, tk=None,
        b_sharded=False, out_sharded=False, b_window=None, exchange=None):
    (kdim, m) = a.shape if ta else a.shape[::-1]
    col0 = 0
    n_lim = k_lim = None
    if b_sharded:
        ns, rows_w, per = b.shape
        if tb:
            kb, n, k_lim = ns * per, rows_w, per
        else:
            kb, n, n_lim = rows_w, ns * per, per
    else:
        (kb, n) = b.shape[::-1] if tb else b.shape
        if b_window is not None:
            assert not tb
            col0, n = b_window
    assert kdim == kb, (a.shape, b.shape, ta, tb)
    if out_sharded:
        assert n % N_SHARDS == 0
        n_lim = n // N_SHARDS if n_lim is None else n_lim
        assert n_lim == n // N_SHARDS
    if tk is None:
        tk = MM_TK_BYTES // max(a.dtype.itemsize, b.dtype.itemsize)
    tm, tn, tk = _blk(m, tm), _blk(n_lim or n, tn), _blk(k_lim or kdim, tk)
    assert col0 % tn == 0, (col0, tn)
    nk = kdim // tk
    a_spec = pl.BlockSpec((tk, tm), lambda i, j, k: (k, i)) if ta else pl.BlockSpec((tm, tk), lambda i, j, k: (i, k))
    if b_sharded and tb:
        kpb = k_lim // tk
        b_spec = pl.BlockSpec((None, tn, tk), lambda i, j, k: (k // kpb, j, k % kpb))
    elif b_sharded:
        bpb = n_lim // tn
        b_spec = pl.BlockSpec((None, tk, tn), lambda i, j, k: (j // bpb, k, j % bpb))
    elif tb:
        b_spec = pl.BlockSpec((tn, tk), lambda i, j, k: (j, k))
    else:
        b_spec = pl.BlockSpec((tk, tn), lambda i, j, k: (k, j + col0 // tn))
    x_spec = pl.BlockSpec((tm, tn), lambda i, j, k: (i, j))
    if out_sharded:
        opb = n_lim // tn
        o_spec = pl.BlockSpec((None, tm, tn), lambda i, j, k: (j // opb, i, j % opb))
        out_shape = (N_SHARDS, m, n_lim)
    else:
        o_spec, out_shape = x_spec, (m, n)
    dims = ((((0 if ta else 1),), ((1 if tb else 0),)), ((), ()))
    n_extra, n_out = len(extras), len(out_dtypes)

    host = _ExchangeHost(exchange)
    grid = (m // tm, n // tn, nk)

    def body(*refs):
        a_ref, b_ref = refs[:2]
        extra_refs = refs[2:2 + n_extra]
        out_refs = refs[2 + n_extra + host.n_in:2 + n_extra + host.n_in + n_out]
        host.start(refs, 2 + n_extra, 2 + n_extra + host.n_in + n_out, grid)
        part = _dot(a_ref[...].astype(BF16), b_ref[...].astype(BF16), dims)

        def finish(acc):
            vals = epilogue(acc, *[r[...] for r in extra_refs]) if epilogue is not None else (acc,) * n_out
            assert len(vals) == n_out
            for r, v in zip(out_refs, vals):
                r[...] = v.astype(r.dtype)

        if nk == 1:
            finish(part)
        else:
            acc_ref = refs[2 + n_extra + host.n_in + n_out + host.n_out]
            k = pl.program_id(2)

            @pl.when(k == 0)
            def _():
                acc_ref[...] = part

            @pl.when((k > 0) & (k < nk - 1))
            def _():
                acc_ref[...] += part

            @pl.when(k == nk - 1)
            def _():
                finish(acc_ref[...] + part)

        host.wait(refs, 2 + n_extra, 2 + n_extra + host.n_in + n_out, grid)

    outs = pl.pallas_call(
        body,
        grid=grid,
        in_specs=[a_spec, b_spec] + [x_spec] * n_extra + host.in_specs,
        out_specs=[o_spec] * n_out + host.out_specs,
        out_shape=[jax.ShapeDtypeStruct(out_shape, d) for d in out_dtypes] + host.out_shapes,
        scratch_shapes=([pltpu.VMEM((tm, tn), F32)] if nk > 1 else []) + host.scratch,
        compiler_params=_cp(host.semantics(("parallel", "parallel", "arbitrary"))),
        name=name,
    )(a, b, *extras, *host.ins)
    mine, landed = outs[:n_out], list(outs[n_out:])
    mine = mine[0] if n_out == 1 else mine
    return (mine, landed) if exchange is not None else mine


class Exchange(NamedTuple):
    ins: tuple
    out_shapes: tuple
    n_remote: int
    n_local: int
    copies_of: Callable
    aliases: tuple = ()


def _exchange_copies(ex, in_refs, out_refs, sem_refs):
    send_sems, recv_sems, local_sems = sem_refs
    remote, local = ex.copies_of(in_refs, out_refs, _place())
    assert len(remote) == ex.n_remote and len(local) == ex.n_local, (len(remote), len(local))
    cps = [pltpu.make_async_remote_copy(src_ref=src, dst_ref=dst, send_sem=send_sems.at[k], recv_sem=recv_sems.at[k],
                                        device_id=to, device_id_type=MESH) for k, (src, dst, to) in enumerate(remote)]
    cps += [pltpu.make_async_copy(src, dst, local_sems.at[k]) for k, (src, dst) in enumerate(local)]
    return cps


class _ExchangeHost:
    def __init__(self, ex):
        self.ex = ex
        self.ins = list(ex.ins) if ex else []
        self.out_shapes = list(ex.out_shapes) if ex else []
        self.n_in, self.n_out = len(self.ins), len(self.out_shapes)
        self.in_specs = [HBM_SPEC] * self.n_in
        self.out_specs = [HBM_SPEC] * self.n_out
        self.scratch = _sem_scratch(ex.n_remote, ex.n_local) if ex else []
        assert not (ex and ex.aliases)

    def semantics(self, sem):
        return tuple("arbitrary" for _ in sem) if self.ex else sem

    def _refs(self, refs, in_at, out_at):
        return refs[in_at:in_at + self.n_in], refs[out_at:out_at + self.n_out], refs[len(refs) - 3:]

    def _when(self, grid, last):
        cond = None
        for d, size in enumerate(grid):
            c = pl.program_id(d) == (size - 1 if last else 0)
            cond = c if cond is None else cond & c
        return cond

    def start(self, refs, in_at, out_at, grid):
        if self.ex:
            @pl.when(self._when(grid, False))
            def _():
                for cp in _exchange_copies(self.ex, *self._refs(refs, in_at, out_at)):
                    cp.start()

    def wait(self, refs, in_at, out_at, grid):
        if self.ex:
            @pl.when(self._when(grid, True))
            def _():
                for cp in _exchange_copies(self.ex, *self._refs(refs, in_at, out_at)):
                    cp.wait()


def _rms_fwd(x, g, *, name, tm=512):
    t, d = x.shape
    tm = _blk(t, tm)

    def body(x_ref, g_ref, n_ref):
        xv = x_ref[...]
        r = lax.rsqrt(jnp.mean(xv * xv, axis=-1, keepdims=True) + RMS_EPS)
        n_ref[...] = (xv * r * g_ref[...]).astype(n_ref.dtype)

    return pl.pallas_call(
        body, grid=(t // tm,),
        in_specs=[pl.BlockSpec((tm, d), lambda i: (i, 0)), pl.BlockSpec((1, d), lambda i: (0, 0))],
        out_specs=pl.BlockSpec((tm, d), lambda i: (i, 0)),
        out_shape=jax.ShapeDtypeStruct((t, d), BF16),
        compiler_params=_cp(("parallel",)), name=name,
    )(x, g)


def _rms_bwd(dn, x, g, dres, *, name, tm=512):
    t, d = x.shape
    tm = _blk(t, tm)

    def body(dn_ref, x_ref, g_ref, dres_ref, dx_ref, dxb_ref, dg_ref):
        i = pl.program_id(0)
        xv = x_ref[...]
        r = lax.rsqrt(jnp.mean(xv * xv, axis=-1, keepdims=True) + RMS_EPS)
        xh = xv * r
        dnv = dn_ref[...].astype(F32)
        dxh = dnv * g_ref[...]
        dx = dres_ref[...] + r * (dxh - xh * jnp.mean(dxh * xh, axis=-1, keepdims=True))
        dx_ref[...] = dx
        dxb_ref[...] = dx.astype(dxb_ref.dtype)
        part = jnp.sum(dnv * xh, axis=0, keepdims=True)

        @pl.when(i == 0)
        def _():
            dg_ref[...] = part

        @pl.when(i > 0)
        def _():
            dg_ref[...] += part

    row = pl.BlockSpec((tm, d), lambda i: (i, 0))
    vec = pl.BlockSpec((1, d), lambda i: (0, 0))
    return pl.pallas_call(
        body, grid=(t // tm,),
        in_specs=[row, row, vec, row], out_specs=[row, row, vec],
        out_shape=[jax.ShapeDtypeStruct((t, d), F32), jax.ShapeDtypeStruct((t, d), BF16),
                   jax.ShapeDtypeStruct((1, d), F32)],
        compiler_params=_cp(("arbitrary",)), name=name,
    )(dn, x, g, dres)


def _final_loss(h, g, tgt, *, name, tm=512):
    t, d = h.shape
    tm = _blk(t, tm)

    def body(h_ref, g_ref, t_ref, dh_ref, dhb_ref, dg_ref, loss_ref):
        i = pl.program_id(0)
        hv = h_ref[...]
        r = lax.rsqrt(jnp.mean(hv * hv, axis=-1, keepdims=True) + RMS_EPS)
        xh = hv * r
        e = xh * g_ref[...] - t_ref[...]
        dy = e * (1.0 / d)
        dxh = dy * g_ref[...]
        dh = r * (dxh - xh * jnp.mean(dxh * xh, axis=-1, keepdims=True))
        dh_ref[...] = dh
        dhb_ref[...] = dh.astype(dhb_ref.dtype)
        dg_part = jnp.sum(dy * xh, axis=0, keepdims=True)
        row_loss = jnp.sum(e * e, axis=-1, keepdims=True) * (0.5 / d)
        loss_part = jnp.sum(row_loss, axis=0, keepdims=True)

        @pl.when(i == 0)
        def _():
            dg_ref[...] = dg_part
            loss_ref[...] = jnp.broadcast_to(loss_part, loss_ref.shape)

        @pl.when(i > 0)
        def _():
            dg_ref[...] += dg_part
            loss_ref[...] += jnp.broadcast_to(loss_part, loss_ref.shape)

    row = pl.BlockSpec((tm, d), lambda i: (i, 0))
    vec = pl.BlockSpec((1, d), lambda i: (0, 0))
    return pl.pallas_call(
        body, grid=(t // tm,),
        in_specs=[row, vec, row], out_specs=[row, row, vec, pl.BlockSpec((1, LANES), lambda i: (0, 0))],
        out_shape=[jax.ShapeDtypeStruct((t, d), F32), jax.ShapeDtypeStruct((t, d), BF16),
                   jax.ShapeDtypeStruct((1, d), F32), jax.ShapeDtypeStruct((1, LANES), F32)],
        compiler_params=_cp(("arbitrary",)), name=name,
    )(h, g, tgt)


def _swa_mask(n, reps):
    w = SWA_WINDOW
    qi = lax.broadcasted_iota(jnp.int32, (reps * w, 2 * w), 0) & (w - 1)
    kj = lax.broadcasted_iota(jnp.int32, (reps * w, 2 * w), 1)
    return (kj > qi) & (kj <= qi + w) & ((n > 0) | (kj >= w))


def _stack_heads(ref, heads, width):
    return jnp.concatenate([ref[:, h * width:(h + 1) * width] for h in heads], axis=0)


def _stack_scalars(ref, heads, rows):
    return jnp.concatenate([jnp.broadcast_to(ref[0:1, h:h + 1], (rows, 1)) for h in heads], axis=0)


def _swa_fwd(q, kv, sinks, *, name):
    t = q.shape[0]
    w, hd, hq, hkv = SWA_WINDOW, SWA_HEAD_DIM, SWA_Q_HEADS, SWA_KV_HEADS
    grp = hq // hkv
    kvw = hkv * hd
    nb = t // w

    def body(q_ref, kvp_ref, kvc_ref, s_ref, o_ref, lse_ref):
        n = pl.program_id(0)
        mask = _swa_mask(n, grp)
        kvcat = jnp.concatenate([kvp_ref[...], kvc_ref[...]], axis=0)
        outs, lses = [], []
        for hk in range(hkv):
            heads = range(hk * grp, (hk + 1) * grp)
            qs = _stack_heads(q_ref, heads, hd)
            kh = kvcat[:, hk * hd:(hk + 1) * hd]
            vh = kvcat[:, kvw + hk * hd:kvw + (hk + 1) * hd]
            sk = _stack_scalars(s_ref, heads, w)
            s = jnp.where(mask, _dot(qs * SWA_SCALE, kh, NT), -jnp.inf)
            m = jnp.maximum(jnp.max(s, axis=-1, keepdims=True), sk)
            p = jnp.exp(s - m)
            den = jnp.sum(p, axis=-1, keepdims=True) + jnp.exp(sk - m)
            o = _dot((p * (1.0 / den)).astype(BF16), vh)
            lse = m + jnp.log(den)
            outs += [o[j * w:(j + 1) * w] for j in range(grp)]
            lses += [lse[j * w:(j + 1) * w] for j in range(grp)]
        o_ref[...] = jnp.concatenate(outs, axis=1).astype(o_ref.dtype)
        lse_ref[...] = jnp.concatenate(lses, axis=1)

    return pl.pallas_call(
        body, grid=(nb,),
        in_specs=[pl.BlockSpec((w, hq * hd), lambda i: (i, 0)),
                  pl.BlockSpec((w, 2 * kvw), lambda i: (jnp.maximum(i - 1, 0), 0)),
                  pl.BlockSpec((w, 2 * kvw), lambda i: (i, 0)),
                  pl.BlockSpec((1, hq), lambda i: (0, 0))],
        out_specs=[pl.BlockSpec((w, hq * hd), lambda i: (i, 0)), pl.BlockSpec((w, hq), lambda i: (i, 0))],
        out_shape=[jax.ShapeDtypeStruct((t, hq * hd), BF16), jax.ShapeDtypeStruct((t, hq), F32)],
        compiler_params=_cp(("parallel",)), name=name,
    )(q, kv, kv, sinks)


ANY_SPEC = pl.BlockSpec(memory_space=pl.ANY)


def _swa_bwd(q, kv, sinks, o, lse, do, dp, dp_col, *, name):
    t = q.shape[0]
    w, hd, hq, hkv = SWA_WINDOW, SWA_HEAD_DIM, SWA_Q_HEADS, SWA_KV_HEADS
    grp = hq // hkv
    kvw = hkv * hd
    nb = t // w
    assert dp_col % (hq * hd) == 0
    dq_blk = dp_col // (hq * hd)

    def body(q_ref, kvp_ref, kvc_ref, s_ref, o_ref, lse_ref, do_ref, _, dq_ref, dkv_ref, ds_ref, carry_ref):
        n = pl.program_id(0)

        @pl.when(n == 0)
        def _():
            ds_ref[...] = jnp.zeros_like(ds_ref)
            carry_ref[...] = jnp.zeros_like(carry_ref)

        @pl.when(n < nb)
        def _():
            mask = _swa_mask(n, grp)
            kvcat = jnp.concatenate([kvp_ref[...], kvc_ref[...]], axis=0)
            dqs, dsk, dks, dvs = [], [], [], []
            for hk in range(hkv):
                heads = range(hk * grp, (hk + 1) * grp)
                qs = _stack_heads(q_ref, heads, hd)
                dos = _stack_heads(do_ref, heads, hd)
                os_ = _stack_heads(o_ref, heads, hd)
                lse = _stack_heads(lse_ref, heads, 1)
                sk = _stack_scalars(s_ref, heads, w)
                kh = kvcat[:, hk * hd:(hk + 1) * hd]
                vh = kvcat[:, kvw + hk * hd:kvw + (hk + 1) * hd]
                s = _dot(qs * SWA_SCALE, kh, NT)
                p = jnp.exp(jnp.where(mask, s, -jnp.inf) - lse)
                delta = jnp.sum(dos.astype(F32) * os_.astype(F32), axis=-1, keepdims=True)
                ds = (p * (_dot(dos, vh, NT) - delta) * SWA_SCALE).astype(BF16)
                dq = _dot(ds, kh)
                dqs += [dq[j * w:(j + 1) * w] for j in range(grp)]
                dks.append(_dot(ds, qs, TN))
                dvs.append(_dot(p.astype(BF16), dos, TN))
                dsink = -jnp.exp(sk - lse) * delta
                dsk += [jnp.sum(dsink[j * w:(j + 1) * w], axis=0, keepdims=True) for j in range(grp)]
            dq_ref[...] = jnp.concatenate(dqs, axis=1).astype(dq_ref.dtype)
            ds_ref[...] += jnp.concatenate(dsk, axis=1)
            dkv_cat = jnp.concatenate(dks + dvs, axis=1)
            dkv_ref[...] = (carry_ref[...] + dkv_cat[:w]).astype(dkv_ref.dtype)
            carry_ref[...] = dkv_cat[w:]

        @pl.when(n == nb)
        def _():
            dkv_ref[...] = carry_ref[...].astype(dkv_ref.dtype)

    cur = lambda i: (jnp.minimum(i, nb - 1), 0)
    prev = lambda i: (jnp.clip(i - 1, 0, nb - 1), 0)
    return pl.pallas_call(
        body, grid=(nb + 1,),
        in_specs=[pl.BlockSpec((w, hq * hd), cur), pl.BlockSpec((w, 2 * kvw), prev), pl.BlockSpec((w, 2 * kvw), cur),
                  pl.BlockSpec((1, hq), lambda i: (0, 0)), pl.BlockSpec((w, hq * hd), cur),
                  pl.BlockSpec((w, hq), cur), pl.BlockSpec((w, hq * hd), cur), ANY_SPEC],
        out_specs=[pl.BlockSpec((w, hq * hd), lambda i: (jnp.minimum(i, nb - 1), dq_blk)),
                   pl.BlockSpec((w, 2 * kvw), prev), pl.BlockSpec((1, hq), lambda i: (0, 0))],
        out_shape=[jax.ShapeDtypeStruct(dp.shape, dp.dtype), jax.ShapeDtypeStruct((t, 2 * kvw), BF16),
                   jax.ShapeDtypeStruct((1, hq), F32)],
        scratch_shapes=[pltpu.VMEM((w, 2 * kvw), F32)], input_output_aliases={7: 0},
        compiler_params=_cp(("arbitrary",)), name=name,
    )(q, kv, kv, sinks, o, lse, do, dp)


def _xa_fwd(q, mkv, *, name, tq=512):
    t, xw = q.shape
    nm = mkv.shape[0]
    hd, nh = XA_HEAD_DIM, XA_HEADS
    tq = _blk(t, tq)

    def body(q_ref, mkv_ref, o_ref):
        outs = []
        for h in range(nh):
            qh = q_ref[:, h * hd:(h + 1) * hd]
            kh = mkv_ref[:, h * hd:(h + 1) * hd]
            vh = mkv_ref[:, xw + h * hd:xw + (h + 1) * hd]
            s = _dot(qh, kh, NT) * (hd ** -0.5)
            p = jnp.exp(s - jnp.max(s, axis=-1, keepdims=True))
            p = p / jnp.sum(p, axis=-1, keepdims=True)
            outs.append(_dot(p.astype(BF16), vh))
        o_ref[...] = jnp.concatenate(outs, axis=1).astype(o_ref.dtype)

    return pl.pallas_call(
        body, grid=(t // tq,),
        in_specs=[pl.BlockSpec((tq, xw), lambda i: (i, 0)), pl.BlockSpec((nm, 2 * xw), lambda i: (0, 0))],
        out_specs=pl.BlockSpec((tq, xw), lambda i: (i, 0)),
        out_shape=jax.ShapeDtypeStruct((t, xw), BF16),
        compiler_params=_cp(("parallel",)), name=name,
    )(q, mkv)


def _xa_bwd(q, mkv, do, dp, dp_col, *, name, tq=512):
    t, xw = q.shape
    nm = mkv.shape[0]
    hd, nh = XA_HEAD_DIM, XA_HEADS
    tq = _blk(t, tq)
    assert dp_col % xw == 0

    def body(q_ref, mkv_ref, do_ref, _, dq_ref, dmkv_ref):
        i = pl.program_id(0)
        dqs, dks, dvs = [], [], []
        for h in range(nh):
            qh = q_ref[:, h * hd:(h + 1) * hd]
            kh = mkv_ref[:, h * hd:(h + 1) * hd]
            vh = mkv_ref[:, xw + h * hd:xw + (h + 1) * hd]
            doh = do_ref[:, h * hd:(h + 1) * hd]
            s = _dot(qh, kh, NT) * (hd ** -0.5)
            p = jnp.exp(s - jnp.max(s, axis=-1, keepdims=True))
            p = p / jnp.sum(p, axis=-1, keepdims=True)
            dp = _dot(doh, vh, NT)
            ds = (p * (dp - jnp.sum(p * dp, axis=-1, keepdims=True)) * (hd ** -0.5)).astype(BF16)
            dqs.append(_dot(ds, kh))
            dks.append(_dot(ds, qh, TN))
            dvs.append(_dot(p.astype(BF16), doh, TN))
        dq_ref[...] = jnp.concatenate(dqs, axis=1).astype(dq_ref.dtype)
        part = jnp.concatenate(dks + dvs, axis=1)

        @pl.when(i == 0)
        def _():
            dmkv_ref[...] = part

        @pl.when(i > 0)
        def _():
            dmkv_ref[...] += part

    row = pl.BlockSpec((tq, xw), lambda i: (i, 0))
    full = pl.BlockSpec((nm, 2 * xw), lambda i: (0, 0))
    return pl.pallas_call(
        body, grid=(t // tq,),
        in_specs=[row, full, row, ANY_SPEC],
        out_specs=[pl.BlockSpec((tq, xw), lambda i: (i, dp_col // xw)), full],
        out_shape=[jax.ShapeDtypeStruct(dp.shape, dp.dtype), jax.ShapeDtypeStruct((nm, 2 * xw), F32)],
        input_output_aliases={3: 0}, compiler_params=_cp(("arbitrary",)), name=name,
    )(q, mkv, do, dp)


def _merge_specs(ys, ws, tm):
    y_specs = [pl.BlockSpec((tm, y.shape[1]), lambda i: (i, 0)) for y in ys]
    w_specs = [pl.BlockSpec(w.shape, lambda i: (0, 0, 0)) for w in ws]
    return y_specs, w_specs


def _merge_tiles(ws, tn):
    ns, _, per = ws[0].shape
    tn = _blk(per, tn)
    return tn, [(s, c, s * per + c) for s in range(ns) for c in range(0, per, tn)]


def _merge_fwd(ys, ws, gates, *, name, tm=256, tn=512):
    t, d = ys[0].shape[0], ws[0].shape[0] * ws[0].shape[2]
    tm = _blk(t, tm)
    tn, tiles = _merge_tiles(ws, tn)
    y_specs, w_specs = _merge_specs(ys, ws, tm)

    def body(ya, yb, yc, wa, wb, wc, g_ref, o_ref):
        for s, c, col in tiles:
            acc = None
            for b, (y, w) in enumerate(((ya, wa), (yb, wb), (yc, wc))):
                term = _sigmoid(g_ref[:, b * d + col:b * d + col + tn]) * _dot(y[...], w[s, :, c:c + tn])
                acc = term if acc is None else acc + term
            o_ref[:, col:col + tn] = acc.astype(o_ref.dtype)

    return pl.pallas_call(
        body, grid=(t // tm,),
        in_specs=y_specs + w_specs + [pl.BlockSpec((tm, 3 * d), lambda i: (i, 0))],
        out_specs=pl.BlockSpec((tm, d), lambda i: (i, 0)),
        out_shape=jax.ShapeDtypeStruct((t, d), BF16),
        compiler_params=_cp(("parallel",)), name=name,
    )(*ys, *ws, gates)


def _merge_bwd(ys, ws, gates, dmerged, dp_width, *, name, tm=256, tn=512):
    t, d = ys[0].shape[0], ws[0].shape[0] * ws[0].shape[2]
    tm = _blk(t, tm)
    tn, tiles = _merge_tiles(ws, tn)
    y_specs, w_specs = _merge_specs(ys, ws, tm)
    row = pl.BlockSpec((tm, d), lambda i: (i, 0))
    wide = pl.BlockSpec((tm, 3 * d), lambda i: (i, 0))

    def body(ya, yb, yc, wa, wb, wc, g_ref, dm_ref, dua, dub, duc, dp_ref):
        for s, c, col in tiles:
            dm = dm_ref[:, col:col + tn]
            for b, (y, w, du) in enumerate(((ya, wa, dua), (yb, wb, dub), (yc, wc, duc))):
                sg = _sigmoid(g_ref[:, b * d + col:b * d + col + tn])
                u = _dot(y[...], w[s, :, c:c + tn])
                du[:, col:col + tn] = (dm * sg).astype(du.dtype)
                dp_ref[:, b * d + col:b * d + col + tn] = (dm * u * sg * (1.0 - sg)).astype(dp_ref.dtype)

    return pl.pallas_call(
        body, grid=(t // tm,),
        in_specs=y_specs + w_specs + [wide, row],
        out_specs=[row] * 3 + [wide],
        out_shape=[jax.ShapeDtypeStruct((t, d), BF16)] * 3 + [jax.ShapeDtypeStruct((t, dp_width), BF16)],
        compiler_params=_cp(("parallel",)), name=name,
    )(*ys, *ws, gates, dmerged)


def _adamw(w, g, m, v, *, name, tm=256):
    lead = w.ndim - 2
    assert all(s == 1 for s in w.shape[:lead]) and m.shape == w.shape and v.shape == w.shape
    r, c = w.shape[lead:]
    assert g.shape == (r, c)
    tm = _blk(r, tm) if r % 8 == 0 else r
    bc1 = 1.0 - ADAM_B1 ** ADAM_STEP
    bc2 = 1.0 - ADAM_B2 ** ADAM_STEP

    def body(w_ref, g_ref, m_ref, v_ref, d_ref, nm_ref, nv_ref):
        gv = g_ref[...]
        nm = ADAM_B1 * m_ref[...] + (1.0 - ADAM_B1) * gv
        nv = ADAM_B2 * v_ref[...] + (1.0 - ADAM_B2) * (gv * gv)
        d_ref[...] = -ADAM_LR * ((nm / bc1) / (jnp.sqrt(nv / bc2) + ADAM_EPS) + ADAM_WD * w_ref[...])
        nm_ref[...] = nm
        nv_ref[...] = nv

    spec = pl.BlockSpec((None,) * lead + (tm, c), lambda i: (0,) * lead + (i, 0))
    g_spec = pl.BlockSpec((tm, c), lambda i: (i, 0))
    return pl.pallas_call(
        body, grid=(r // tm,), in_specs=[spec, g_spec, spec, spec], out_specs=[spec] * 3,
        out_shape=[jax.ShapeDtypeStruct(w.shape, F32)] * 3,
        compiler_params=_cp(("parallel",)), name=name,
    )(w, g, m, v)


HALO = 8


def _shift_down(cur, prev, j):
    if j == 0:
        return cur
    y = pltpu.roll(cur, j, 0)
    row = lax.broadcasted_iota(jnp.int32, (HALO, cur.shape[1]), 0)
    top = jnp.where(row < j, pltpu.roll(prev, j, 0), y[:HALO])
    return jnp.concatenate([top, y[HALO:]], axis=0)


def _shift_up(cur, nxt, j):
    if j == 0:
        return cur
    tm = cur.shape[0]
    y = pltpu.roll(cur, tm - j, 0)
    row = lax.broadcasted_iota(jnp.int32, (HALO, cur.shape[1]), 0)
    bot = jnp.where(row >= HALO - j, pltpu.roll(nxt, HALO - j, 0), y[tm - HALO:])
    return jnp.concatenate([y[:tm - HALO], bot], axis=0)


def _softplus(x):
    return jnp.maximum(x, 0.0) + jnp.log(1.0 + jnp.exp(-jnp.abs(x)))


def _gdn_pre_fwd(qkvb, conv_w, ab, alog_pad, dt_pad, *, name, ab_blk=0, tm=256):
    t, cw = qkvb.shape
    hd, nh, ck = GDN_HEAD_DIM, GDN_HEADS, GDN_CHUNK
    gw = nh * hd
    tm = _blk(t, tm)
    hb = tm // HALO

    def body(x_ref, xp_ref, w_ref, ab_ref, al_ref, dt_ref, xc_ref, qkvn_ref, aux_ref):
        i = pl.program_id(0)
        cur = x_ref[...]
        prev = jnp.where(i > 0, xp_ref[...], 0.0)
        xc = None
        for tap in range(GDN_CONV):
            term = w_ref[tap:tap + 1, :] * _shift_down(cur, prev, GDN_CONV - 1 - tap)
            xc = term if xc is None else xc + term
        xc_ref[...] = xc
        s = xc * _sigmoid(xc)
        for h in range(2 * nh):
            xh = s[:, h * hd:(h + 1) * hd]
            r = lax.rsqrt(jnp.sum(xh * xh, axis=-1, keepdims=True) + L2_EPS)
            scale = hd ** -0.5 if h < nh else 1.0
            qkvn_ref[:, h * hd:(h + 1) * hd] = xh * (r * scale)
        qkvn_ref[:, 2 * gw:] = s[:, 2 * gw:]
        abv = ab_ref[...]
        lane = lax.broadcasted_iota(jnp.int32, abv.shape, 1)
        g = jnp.where(lane < nh, -jnp.exp(al_ref[...]) * _softplus(abv + dt_ref[...]), 0.0)
        beta = jnp.where((lane >= nh) & (lane < 2 * nh), _sigmoid(abv), 0.0)
        ii = lax.broadcasted_iota(jnp.int32, (tm, tm), 0)
        jj = lax.broadcasted_iota(jnp.int32, (tm, tm), 1)
        tri = jnp.where((ii >= jj) & ((ii ^ jj) < ck), 1.0, 0.0)
        gcum = _dot(tri, g, precision=HI)
        aux_ref[...] = g + beta + pltpu.roll(gcum, 2 * nh, 1)

    row = lambda c: pl.BlockSpec((tm, c), lambda i: (i, 0))
    vec = lambda r, c: pl.BlockSpec((r, c), lambda i: (0, 0))
    return pl.pallas_call(
        body, grid=(t // tm,),
        in_specs=[row(cw), pl.BlockSpec((HALO, cw), lambda i: (jnp.maximum(i * hb - 1, 0), 0)), vec(GDN_CONV, cw),
                  pl.BlockSpec((tm, LANES), lambda i: (i, ab_blk)), vec(1, LANES), vec(1, LANES)],
        out_specs=[row(cw), row(cw), row(LANES)],
        out_shape=[jax.ShapeDtypeStruct((t, cw), F32), jax.ShapeDtypeStruct((t, cw), F32),
                   jax.ShapeDtypeStruct((t, LANES), F32)],
        compiler_params=_cp(("parallel",)), name=name,
    )(qkvb, qkvb, conv_w, ab, alog_pad, dt_pad)


GDN_STEP_CHUNKS = 4


def _bdot(a, b, dims=NN):
    return _dot(a.astype(BF16), b.astype(BF16), dims)


def _split_bf16(x):
    hi = x.astype(BF16)
    return hi, (x - hi.astype(F32)).astype(BF16)


def _dot3(a, b, dims=NN):
    ah, al = _split_bf16(a)
    bh, bl = _split_bf16(b)
    return _dot(ah, bh, dims) + (_dot(ah, bl, dims) + _dot(al, bh, dims))


def _gdn_local(q, k, b, gc, gc_row):
    ck = GDN_CHUNK
    ii = lax.broadcasted_iota(jnp.int32, (ck, ck), 0)
    jj = lax.broadcasted_iota(jnp.int32, (ck, ck), 1)
    lower, strict = ii >= jj, ii > jj
    dmat = jnp.exp(jnp.where(lower, gc - gc_row, -jnp.inf))
    kk = _bdot(k, k, NT)
    lmat = jnp.where(strict, b * kk * dmat, 0.0)
    tinv = jnp.where(ii == jj, 1.0, 0.0) - lmat
    pw = lmat
    for _ in range(int(math.log2(ck)) - 1):
        pw = _dot3(pw, pw)
        tinv = tinv + _dot3(tinv, pw)
    gam = jnp.exp(gc)
    qk = _bdot(q, k, NT)
    gl = gc[ck - 1:ck, :]
    return dict(lower=lower, strict=strict, dmat=dmat, kk=kk, tinv=tinv, gam=gam, qk=qk, mm=qk * dmat,
                kdec=jnp.exp(gl - gc))


def _gdn_head_cols(h):
    return slice(h * GDN_HEAD_DIM, (h + 1) * GDN_HEAD_DIM)


def _gdn_chunk_inputs(x_ref, aux_ref, auxt_ref, g, h):
    nh, ck = GDN_HEADS, GDN_CHUNK
    gw = nh * GDN_HEAD_DIM
    rows = slice(g * ck, (g + 1) * ck)
    cols = _gdn_head_cols(h)
    q = x_ref[rows, cols]
    k = x_ref[rows, gw + cols.start:gw + cols.stop]
    v = x_ref[rows, 2 * gw + cols.start:2 * gw + cols.stop]
    b = aux_ref[rows, nh + h:nh + h + 1]
    gc = aux_ref[rows, 2 * nh + h:2 * nh + h + 1]
    gc_row = auxt_ref[g, 2 * nh + h:2 * nh + h + 1, :]
    return q, k, v, b, gc, gc_row


def _gdn_specs(t, widths, *, reverse=False, step_chunks=None):
    rows = (step_chunks or GDN_STEP_CHUNKS) * GDN_CHUNK
    nsteps = t // rows
    idx = (lambda i: (nsteps - 1 - i, 0)) if reverse else (lambda i: (i, 0))
    return [pl.BlockSpec((rows, w), idx) for w in widths]


def _gdn_local_fwd(qkvn, aux, aux_t, *, name, exchange=None):
    t = qkvn.shape[0]
    hd, nh, ck, gs = GDN_HEAD_DIM, GDN_HEADS, GDN_CHUNK, GDN_STEP_CHUNKS
    gw = nh * hd
    host = _ExchangeHost(exchange)
    grid = (t // (gs * ck),)

    def body(*refs):
        x_ref, aux_ref, auxt_ref = refs[:3]
        u_ref, w_ref, qd_ref, kd_ref, mm_ref, tinv_ref = refs[3 + host.n_in:9 + host.n_in]
        host.start(refs, 3, 9 + host.n_in, grid)
        for g in range(gs):
            rows = slice(g * ck, (g + 1) * ck)
            mms, tinvs = [], []
            for h in range(nh):
                q, k, v, b, gc, gc_row = _gdn_chunk_inputs(x_ref, aux_ref, auxt_ref, g, h)
                lc = _gdn_local(q, k, b, gc, gc_row)
                cols = _gdn_head_cols(h)
                u_ref[rows, cols] = _dot3(lc["tinv"], b * v)
                w_ref[rows, cols] = _dot3(lc["tinv"], (b * lc["gam"]) * k).astype(w_ref.dtype)
                qd_ref[rows, cols] = (lc["gam"] * q).astype(qd_ref.dtype)
                kd_ref[rows, cols] = (lc["kdec"] * k).astype(kd_ref.dtype)
                mms.append(lc["mm"])
                tinvs.append(lc["tinv"])
            mm_ref[rows, :] = jnp.concatenate(mms, axis=1).astype(mm_ref.dtype)
            tinv_ref[rows, :] = jnp.concatenate(tinvs, axis=1)
        host.wait(refs, 3, 9 + host.n_in, grid)

    sq = nh * ck
    outs = pl.pallas_call(
        body, grid=grid,
        in_specs=_gdn_specs(t, (3 * gw, LANES)) + [pl.BlockSpec((gs, 16, ck), lambda i: (i, 0, 0))] + host.in_specs,
        out_specs=_gdn_specs(t, (gw, gw, gw, gw, sq, sq)) + host.out_specs,
        out_shape=[jax.ShapeDtypeStruct((t, gw), F32)] + [jax.ShapeDtypeStruct((t, gw), BF16)] * 3
        + [jax.ShapeDtypeStruct((t, sq), BF16), jax.ShapeDtypeStruct((t, sq), F32)] + host.out_shapes,
        scratch_shapes=host.scratch,
        compiler_params=_cp(host.semantics(("parallel",))), name=name,
    )(qkvn, aux, aux_t, *host.ins)
    return (*outs[:6], list(outs[6:])) if exchange is not None else outs


def _gdn_seq_fwd(u, w, qd, kd, mm, aux, *, name):
    t = u.shape[0]
    hd, nh, ck, gs = GDN_HEAD_DIM, GDN_HEADS, GDN_CHUNK, GDN_STEP_CHUNKS
    gw = nh * hd
    sq = nh * ck

    def body(u_ref, w_ref, qd_ref, kd_ref, mm_ref, aux_ref, o_ref, vn_ref, sall_ref, s_ref):
        @pl.when(pl.program_id(0) == 0)
        def _():
            s_ref[...] = jnp.zeros_like(s_ref)

        for g in range(gs):
            rows = slice(g * ck, (g + 1) * ck)
            last = (g + 1) * ck - 1
            for h in range(nh):
                cols = _gdn_head_cols(h)
                st = s_ref[h]
                sall_ref[g, h] = st
                stb = st.astype(BF16)
                vn = u_ref[rows, cols] - _dot(w_ref[rows, cols], stb)
                vnb = vn.astype(BF16)
                vn_ref[rows, cols] = vnb
                o_ref[rows, cols] = _dot(qd_ref[rows, cols], stb) + _dot(mm_ref[rows, h * ck:(h + 1) * ck], vnb)
                gam_c = jnp.exp(aux_ref[last:last + 1, 2 * nh + h:2 * nh + h + 1])
                s_ref[h] = gam_c * st + _dot(kd_ref[rows, cols], vnb, TN)

    return pl.pallas_call(
        body, grid=(t // (gs * ck),),
        in_specs=_gdn_specs(t, (gw, gw, gw, gw, sq, LANES)),
        out_specs=_gdn_specs(t, (gw, gw)) + [pl.BlockSpec((gs, nh, hd, hd), lambda i: (i, 0, 0, 0))],
        out_shape=[jax.ShapeDtypeStruct((t, gw), F32), jax.ShapeDtypeStruct((t, gw), BF16),
                   jax.ShapeDtypeStruct((t // ck, nh, hd, hd), F32)],
        scratch_shapes=[pltpu.VMEM((nh, hd, hd), F32)],
        compiler_params=_cp(("arbitrary",)), name=name,
    )(u, w, qd, kd, mm, aux)


def _gdn_seq_bwd(do, w, qd, kd, mm, vn, s_all, aux, *, name):
    t = do.shape[0]
    hd, nh, ck, gs = GDN_HEAD_DIM, GDN_HEADS, GDN_CHUNK, GDN_STEP_CHUNKS
    gw = nh * hd
    sq = nh * ck
    nsteps = t // (gs * ck)

    def body(do_ref, w_ref, qd_ref, kd_ref, mm_ref, vn_ref, sall_ref, aux_ref, dvn_ref, dqd_ref, dkd_ref, dw_ref,
             dlast_ref, ds_ref):
        @pl.when(pl.program_id(0) == 0)
        def _():
            ds_ref[...] = jnp.zeros_like(ds_ref)

        lane = lax.broadcasted_iota(jnp.int32, (ck, LANES), 1)
        rowi = lax.broadcasted_iota(jnp.int32, (ck, LANES), 0)
        for g in reversed(range(gs)):
            rows = slice(g * ck, (g + 1) * ck)
            last = (g + 1) * ck - 1
            dlast = jnp.zeros((ck, LANES), F32)
            for h in range(nh):
                cols = _gdn_head_cols(h)
                st = sall_ref[g, h]
                stb = st.astype(BF16)
                dsn = ds_ref[h]
                dsb = dsn.astype(BF16)
                dob = do_ref[rows, cols].astype(BF16)
                dvn = _dot(mm_ref[rows, h * ck:(h + 1) * ck], dob, TN) + _dot(kd_ref[rows, cols], dsb)
                dvb = dvn.astype(BF16)
                dvn_ref[rows, cols] = dvn
                dqd_ref[rows, cols] = _dot(dob, stb, NT)
                dkd_ref[rows, cols] = _dot(vn_ref[rows, cols], dsb, NT)
                dw_ref[rows, cols] = -_dot(dvb, stb, NT)
                gam_c = jnp.exp(aux_ref[last:last + 1, 2 * nh + h:2 * nh + h + 1])
                dgam_c = jnp.sum(jnp.sum(dsn * st, axis=1, keepdims=True), axis=0, keepdims=True)
                dlast = dlast + jnp.where((rowi == ck - 1) & (lane == h), gam_c * dgam_c, 0.0)
                ds_ref[h] = _dot(qd_ref[rows, cols], dob, TN) + gam_c * dsn - _dot(w_ref[rows, cols], dvb, TN)
            dlast_ref[rows, :] = dlast

    return pl.pallas_call(
        body, grid=(nsteps,),
        in_specs=_gdn_specs(t, (gw, gw, gw, gw, sq, gw), reverse=True)
        + [pl.BlockSpec((gs, nh, hd, hd), lambda i: (nsteps - 1 - i, 0, 0, 0))] + _gdn_specs(t, (LANES,), reverse=True),
        out_specs=_gdn_specs(t, (gw, gw, gw, gw, LANES), reverse=True),
        out_shape=[jax.ShapeDtypeStruct((t, gw), F32)] * 4 + [jax.ShapeDtypeStruct((t, LANES), F32)],
        scratch_shapes=[pltpu.VMEM((nh, hd, hd), F32)],
        compiler_params=_cp(("arbitrary",)), name=name,
    )(do, w, qd, kd, mm, vn, s_all, aux)


def _gdn_local_bwd(qkvn, aux, aux_t, tinv, u, w, vn, do, dvn, dqd, dkd, dw, dlast, *, name):
    t = qkvn.shape[0]
    hd, nh, ck, gs = GDN_HEAD_DIM, GDN_HEADS, GDN_CHUNK, GDN_STEP_CHUNKS
    gw = nh * hd
    sq = nh * ck

    def body(x_ref, aux_ref, auxt_ref, tinv_ref, u_ref, w_ref, vn_ref, do_ref, dvn_ref, dqd_ref, dkd_ref, dw_ref,
             dlast_ref, dx_ref, daux_ref):
        lane = lax.broadcasted_iota(jnp.int32, (ck, LANES), 1)
        ones = jnp.ones((ck, LANES), F32)
        ii = lax.broadcasted_iota(jnp.int32, (ck, ck), 0)
        jj = lax.broadcasted_iota(jnp.int32, (ck, ck), 1)
        suffix = jnp.where(jj >= ii, 1.0, 0.0)
        for g in range(gs):
            rows = slice(g * ck, (g + 1) * ck)
            dgc_all = dlast_ref[rows, :]
            db_all = jnp.zeros((ck, LANES), F32)
            for h in range(nh):
                cols = _gdn_head_cols(h)
                q, k, v, b, gc, gc_row = _gdn_chunk_inputs(x_ref, aux_ref, auxt_ref, g, h)
                lc = _gdn_local(q, k, b, gc, gc_row)
                dmat, kk, gam, qk, kdec = (lc[key] for key in ("dmat", "kk", "gam", "qk", "kdec"))
                tinv_h = tinv_ref[rows, h * ck:(h + 1) * ck]
                u_h, w_h, vn_h = u_ref[rows, cols], w_ref[rows, cols], vn_ref[rows, cols]
                dqd_h, dkd_h = dqd_ref[rows, cols], dkd_ref[rows, cols]
                dm = jnp.where(lc["lower"], _bdot(do_ref[rows, cols], vn_h, NT), 0.0)
                drv = _dot3(tinv_h, dvn_ref[rows, cols], TN)
                drk = _dot3(tinv_h, dw_ref[rows, cols], TN)
                da = jnp.where(lc["strict"], -(_bdot(drv, u_h, NT) + _bdot(drk, w_h, NT)), 0.0)
                rs_rk = jnp.sum(drk * k, axis=-1, keepdims=True)
                db = (jnp.sum(drv * v, axis=-1, keepdims=True) + gam * rs_rk
                      + jnp.sum(da * kk * dmat, axis=-1, keepdims=True))
                e_mat = da * dmat * b
                dmd = dm * dmat
                dx_ref[rows, cols] = _bdot(dmd, k) + gam * dqd_h
                dx_ref[rows, gw + cols.start:gw + cols.stop] = (
                    (b * gam) * drk + _bdot(e_mat, k) + _bdot(e_mat, k, TN) + _bdot(dmd, q, TN) + kdec * dkd_h)
                dx_ref[rows, 2 * gw + cols.start:2 * gw + cols.stop] = b * drv
                f_mat = da * (b * kk) * dmat + dm * qk * dmat
                e_vec = jnp.sum(dkd_h * (kdec * k), axis=-1, keepdims=True)
                dgc = (b * gam * rs_rk + gam * jnp.sum(dqd_h * q, axis=-1, keepdims=True)
                       + jnp.sum(f_mat, axis=-1, keepdims=True) - _dot3(f_mat, ones, TN)[:, 0:1] - e_vec)
                is_last = lax.broadcasted_iota(jnp.int32, (ck, 1), 0) == ck - 1
                dgc = dgc + jnp.where(is_last, jnp.sum(e_vec, axis=0, keepdims=True), 0.0)
                dgc_all = dgc_all + jnp.where(lane == h, dgc, 0.0)
                db_all = db_all + jnp.where(lane == nh + h, db, 0.0)
            daux_ref[rows, :] = _dot3(suffix, dgc_all) + db_all

    return pl.pallas_call(
        body, grid=(t // (gs * ck),),
        in_specs=_gdn_specs(t, (3 * gw, LANES)) + [pl.BlockSpec((gs, 16, ck), lambda i: (i, 0, 0))]
        + _gdn_specs(t, (sq, gw, gw, gw, gw, gw, gw, gw, gw, LANES)),
        out_specs=_gdn_specs(t, (3 * gw, LANES)),
        out_shape=[jax.ShapeDtypeStruct((t, 3 * gw), F32), jax.ShapeDtypeStruct((t, LANES), F32)],
        compiler_params=_cp(("parallel",)), name=name,
    )(qkvn, aux, aux_t, tinv, u, w, vn, do, dvn, dqd, dkd, dw, dlast)


def _gdn_pre_bwd1(xc, dqkvn, daux, ab, alog_pad, dt_pad, dkv, dp, dp_col, *, name, ab_blk=0, tm=256):
    t, cw = xc.shape
    hd, nh = GDN_HEAD_DIM, GDN_HEADS
    gw = nh * hd
    tm = _blk(t, tm)

    kvw = dkv.shape[1]
    seg = kvw + AB_PAD
    assert dp_col % seg == 0

    def body(xc_ref, dy_ref, daux_ref, ab_ref, al_ref, dt_ref, dkv_ref, _, dxc_ref, dab_ref, dal_ref, ddt_ref):
        i = pl.program_id(0)
        xc = xc_ref[...]
        sg = _sigmoid(xc)
        s = xc * sg
        dsilu = sg * (1.0 + xc * (1.0 - sg))
        for h in range(2 * nh):
            xh = s[:, h * hd:(h + 1) * hd]
            scale = hd ** -0.5 if h < nh else 1.0
            dyh = dy_ref[:, h * hd:(h + 1) * hd] * scale
            r = lax.rsqrt(jnp.sum(xh * xh, axis=-1, keepdims=True) + L2_EPS)
            dxh = r * dyh - xh * (r * r * r) * jnp.sum(dyh * xh, axis=-1, keepdims=True)
            dxc_ref[:, h * hd:(h + 1) * hd] = dxh * dsilu[:, h * hd:(h + 1) * hd]
        dxc_ref[:, 2 * gw:] = dy_ref[:, 2 * gw:] * dsilu[:, 2 * gw:]
        abv = ab_ref[...]
        dauxv = daux_ref[...]
        lane = lax.broadcasted_iota(jnp.int32, abv.shape, 1)
        is_a = lane < nh
        is_b = (lane >= nh) & (lane < 2 * nh)
        pre = abv + dt_ref[...]
        neg_ea = -jnp.exp(al_ref[...])
        d_a = jnp.where(is_a, dauxv * neg_ea * _sigmoid(pre), 0.0)
        beta = _sigmoid(abv)
        d_b = jnp.where(is_b, dauxv * beta * (1.0 - beta), 0.0)
        dab_ref[:, :kvw] = dkv_ref[...]
        dab_ref[:, kvw:kvw + LANES] = (d_a + d_b).astype(dab_ref.dtype)
        dab_ref[:, kvw + LANES:] = jnp.zeros((tm, AB_PAD - LANES), dab_ref.dtype)
        dal = jnp.sum(jnp.where(is_a, dauxv * neg_ea * _softplus(pre), 0.0), axis=0, keepdims=True)
        ddt = jnp.sum(d_a, axis=0, keepdims=True)

        @pl.when(i == 0)
        def _():
            dal_ref[...] = dal
            ddt_ref[...] = ddt

        @pl.when(i > 0)
        def _():
            dal_ref[...] += dal
            ddt_ref[...] += ddt

    row = lambda c: pl.BlockSpec((tm, c), lambda i: (i, 0))
    vec = pl.BlockSpec((1, LANES), lambda i: (0, 0))
    return pl.pallas_call(
        body, grid=(t // tm,),
        in_specs=[row(cw), row(cw), row(LANES), pl.BlockSpec((tm, LANES), lambda i: (i, ab_blk)), vec, vec, row(kvw),
                  ANY_SPEC],
        out_specs=[row(cw), pl.BlockSpec((tm, seg), lambda i: (i, dp_col // seg)), vec, vec],
        out_shape=[jax.ShapeDtypeStruct((t, cw), F32), jax.ShapeDtypeStruct(dp.shape, dp.dtype),
                   jax.ShapeDtypeStruct((1, LANES), F32), jax.ShapeDtypeStruct((1, LANES), F32)],
        input_output_aliases={7: 1}, compiler_params=_cp(("arbitrary",)), name=name,
    )(xc, dqkvn, daux, ab, alog_pad, dt_pad, dkv, dp)


def _gdn_pre_bwd2(dxc, qkvb, conv_w, dp, dp_col, *, name, tm=512):
    t, cw = dxc.shape
    tm = _blk(t, tm)
    hb = tm // HALO
    nblk = t // tm
    cg = GDN_HEADS * GDN_HEAD_DIM
    assert cw % cg == 0 and dp_col % cg == 0
    col0 = dp_col // cg

    def body(d_ref, dn_ref, x_ref, xp_ref, w_ref, _, dx_ref, dw_ref):
        i = pl.program_id(1)
        dcur = d_ref[...]
        dnxt = jnp.where(i < nblk - 1, dn_ref[...], 0.0)
        cur = x_ref[...]
        prev = jnp.where(i > 0, xp_ref[...], 0.0)
        dx = None
        dws = []
        for tap in range(GDN_CONV):
            j = GDN_CONV - 1 - tap
            term = w_ref[tap:tap + 1, :] * _shift_up(dcur, dnxt, j)
            dx = term if dx is None else dx + term
            dws.append(jnp.sum(dcur * _shift_down(cur, prev, j), axis=0, keepdims=True))
        dx_ref[...] = dx.astype(dx_ref.dtype)
        dw = jnp.concatenate(dws, axis=0)

        @pl.when(i == 0)
        def _():
            dw_ref[...] = dw

        @pl.when(i > 0)
        def _():
            dw_ref[...] += dw

    row = pl.BlockSpec((tm, cg), lambda c, i: (i, c))
    wsp = pl.BlockSpec((GDN_CONV, cg), lambda c, i: (0, c))
    return pl.pallas_call(
        body, grid=(cw // cg, nblk),
        in_specs=[row, pl.BlockSpec((HALO, cg), lambda c, i: (jnp.minimum((i + 1) * hb, t // HALO - 1), c)),
                  row, pl.BlockSpec((HALO, cg), lambda c, i: (jnp.maximum(i * hb - 1, 0), c)), wsp, ANY_SPEC],
        out_specs=[pl.BlockSpec((tm, cg), lambda c, i: (i, col0 + c)), wsp],
        out_shape=[jax.ShapeDtypeStruct(dp.shape, dp.dtype), jax.ShapeDtypeStruct((GDN_CONV, cw), F32)],
        input_output_aliases={5: 0}, compiler_params=_cp(("arbitrary", "arbitrary")), name=name,
    )(dxc, dxc, qkvb, qkvb, conv_w, dp)


def _gdn_post_fwd(o, z, norm_w, *, name, tm=512):
    t, gw = o.shape
    hd, nh = GDN_HEAD_DIM, GDN_HEADS
    tm = _blk(t, tm)

    def body(o_ref, z_ref, w_ref, y_ref):
        zv = z_ref[...]
        sz = zv * _sigmoid(zv)
        for h in range(nh):
            oh = o_ref[:, h * hd:(h + 1) * hd]
            r = lax.rsqrt(jnp.mean(oh * oh, axis=-1, keepdims=True) + RMS_EPS)
            y_ref[:, h * hd:(h + 1) * hd] = (oh * r * w_ref[...] * sz[:, h * hd:(h + 1) * hd]).astype(y_ref.dtype)

    row = pl.BlockSpec((tm, gw), lambda i: (i, 0))
    return pl.pallas_call(
        body, grid=(t // tm,), in_specs=[row, row, pl.BlockSpec((1, hd), lambda i: (0, 0))], out_specs=row,
        out_shape=jax.ShapeDtypeStruct((t, gw), BF16), compiler_params=_cp(("parallel",)), name=name,
    )(o, z, norm_w)


def _gdn_post_bwd(dy, o, z, norm_w, dp, dp_col, *, name, tm=512):
    t, gw = o.shape
    hd, nh = GDN_HEAD_DIM, GDN_HEADS
    tm = _blk(t, tm)

    def body(dy_ref, o_ref, z_ref, w_ref, _, do_ref, dz_ref, dw_ref):
        i = pl.program_id(0)
        zv = z_ref[...]
        sg = _sigmoid(zv)
        sz = zv * sg
        dsz = sg * (1.0 + zv * (1.0 - sg))
        dw = None
        for h in range(nh):
            sl = slice(h * hd, (h + 1) * hd)
            oh = o_ref[:, sl]
            dyh = dy_ref[:, sl].astype(F32)
            r = lax.rsqrt(jnp.mean(oh * oh, axis=-1, keepdims=True) + RMS_EPS)
            xh = oh * r
            dz_ref[:, sl] = (dyh * xh * w_ref[...] * dsz[:, sl]).astype(dz_ref.dtype)
            dn = dyh * sz[:, sl]
            dxh = dn * w_ref[...]
            do_ref[:, sl] = r * (dxh - xh * jnp.mean(dxh * xh, axis=-1, keepdims=True))
            part = jnp.sum(dn * xh, axis=0, keepdims=True)
            dw = part if dw is None else dw + part

        @pl.when(i == 0)
        def _():
            dw_ref[...] = dw

        @pl.when(i > 0)
        def _():
            dw_ref[...] += dw

    row = pl.BlockSpec((tm, gw), lambda i: (i, 0))
    vec = pl.BlockSpec((1, hd), lambda i: (0, 0))
    return pl.pallas_call(
        body, grid=(t // tm,), in_specs=[row, row, row, vec, ANY_SPEC],
        out_specs=[row, pl.BlockSpec((tm, gw), lambda i: (i, dp_col // gw)), vec],
        out_shape=[jax.ShapeDtypeStruct((t, gw), F32), jax.ShapeDtypeStruct(dp.shape, dp.dtype),
                   jax.ShapeDtypeStruct((1, hd), F32)],
        input_output_aliases={4: 1}, compiler_params=_cp(("arbitrary",)), name=name,
    )(dy, o, z, norm_w, dp)


IN_NAMES = ("q_a", "kv_a", "qkv_b", "ab", "z", "q_c", "gates")
CAT_NAMES = ("gates", "q_a", "qkv_b", "z", "q_c", "kv_a", "ab")
AB_PAD = 256


def _in_widths(d):
    gw = GDN_HEADS * GDN_HEAD_DIM
    return dict(q_a=SWA_Q_HEADS * SWA_HEAD_DIM, kv_a=2 * SWA_KV_HEADS * SWA_HEAD_DIM, qkv_b=3 * gw, ab=2 * GDN_HEADS,
                z=gw, q_c=XA_HEADS * XA_HEAD_DIM, gates=3 * d)


def _ranges(names, widths):
    out, start = {}, 0
    for k in names:
        out[k] = (start, widths[k])
        start += widths[k]
    return out, start


def _cat_ranges(d):
    widths = dict(_in_widths(d), ab=AB_PAD)
    return _ranges(CAT_NAMES, widths)


def _to_cat(shards):
    ns, d, n = shards.shape
    src, _ = _ranges(IN_NAMES, _in_widths(d))
    cols = []
    for k in CAT_NAMES:
        lo, hi = src[k][0], src[k][0] + src[k][1]
        for s in range(ns):
            a, b = max(lo, s * n), min(hi, (s + 1) * n)
            if a < b:
                cols.append(shards[s][:, a - s * n:b - s * n])
    cols.append(jnp.zeros((d, AB_PAD - src["ab"][1]), shards.dtype))
    return jnp.concatenate(cols, axis=1)


def _from_cat(w_cat):
    d = w_cat.shape[0]
    src, total = _ranges(IN_NAMES, _in_widths(d))
    cat, _ = _cat_ranges(d)
    n = total // N_SHARDS
    shards = []
    for s in range(N_SHARDS):
        pieces = []
        for k in IN_NAMES:
            a, b = max(s * n, src[k][0]), min((s + 1) * n, src[k][0] + src[k][1])
            if a < b:
                pieces.append(w_cat[:, cat[k][0] + a - src[k][0]:cat[k][0] + b - src[k][0]])
        shards.append(jnp.concatenate(pieces, axis=1))
    return jnp.stack(shards)


def _pad_cols(a, width):
    return jnp.pad(a, ((0, 0), (0, width - a.shape[1])))


def _relu2_epilogue(acc):
    r = jnp.maximum(acc, 0.0)
    return acc, r * r


def _add_epilogue(acc, res):
    return (acc + res,)


def _drelu2_epilogue(acc, u):
    return (acc * (2.0 * jnp.maximum(u.astype(F32), 0.0)),)


def _local_step(x, mem, tgt, wts, small, comm=None):
    t, d = x.shape
    nh = GDN_HEADS
    w_cat = wts["w_cat"]
    cat, cat_w = _cat_ranges(d)
    assert w_cat.shape == (d, cat_w)
    alog_pad = _pad_cols(small["a_log"], LANES)
    dt_pad = _pad_cols(small["dt_bias"], LANES)
    kvw = cat["kv_a"][1]
    assert cat["ab"][0] == cat["kv_a"][0] + kvw
    ab_blk = kvw // LANES

    n = _rms_fwd(x, small["g_mix"], name="rms_mix")
    q_a = _mm(n, w_cat, b_window=cat["q_a"], out_dtypes=(BF16,), name="in_q_a")
    kv_a, ab = _mm(n, w_cat, b_window=(cat["kv_a"][0], kvw + AB_PAD), out_dtypes=(BF16, F32), name="in_kv_ab")
    qkvb = _mm(n, w_cat, b_window=cat["qkv_b"], tn=512, name="in_qkv_b")
    z = _mm(n, w_cat, b_window=cat["z"], name="in_z")
    q_c = _mm(n, w_cat, b_window=cat["q_c"], out_dtypes=(BF16,), name="in_q_c")
    gates = _mm(n, w_cat, b_window=cat["gates"], name="in_gates")
    y_a, lse = _swa_fwd(q_a, kv_a, small["sinks"], name="swa_fwd")
    xc, qkvn, aux = _gdn_pre_fwd(qkvb, small["conv_w"], ab, alog_pad, dt_pad, ab_blk=ab_blk, name="gdn_pre_fwd")
    aux_t = aux[:, :16].reshape(t // GDN_CHUNK, GDN_CHUNK, 16).transpose(0, 2, 1)
    if comm is None:
        gdn_u, gdn_w, gdn_qd, gdn_kd, gdn_mm, gdn_tinv = _gdn_local_fwd(qkvn, aux, aux_t, name="gdn_local_fwd")
    else:
        gdn_u, gdn_w, gdn_qd, gdn_kd, gdn_mm, gdn_tinv, landed = _gdn_local_fwd(
            qkvn, aux, aux_t, name="gdn_local_fwd", exchange=comm.gather_exchange())
        wts = dict(wts, **comm.gathered(landed))
    o_b, gdn_vn, s_all = _gdn_seq_fwd(gdn_u, gdn_w, gdn_qd, gdn_kd, gdn_mm, aux, name="gdn_seq_fwd")
    y_b = _gdn_post_fwd(o_b, z, small["gdn_norm_w"], name="gdn_post_fwd")
    nmem = _rms_fwd(mem, small["g_mem"], name="rms_mem")
    mkv = _mm(nmem, wts["w_mem_kv"], out_dtypes=(BF16,), name="mem_kv")
    y_c = _xa_fwd(q_c, mkv, name="xa_fwd")
    ys = (y_a, y_b, y_c)
    w_ups = (wts["w_swa_up"], wts["w_gdn_up"], wts["w_xa_up"])
    merged = _merge_fwd(ys, w_ups, gates, name="merge_fwd")
    h1 = _mm(merged, wts["w_out"], extras=(x,), epilogue=_add_epilogue, name="out_proj")
    n2 = _rms_fwd(h1, small["g_mlp"], name="rms_mlp")
    u, act = _mm(n2, wts["w_mlp_in"], b_sharded=True, out_dtypes=(BF16, BF16), epilogue=_relu2_epilogue, name="mlp_in")
    h2 = _mm(act, wts["w_mlp_out"], extras=(h1,), epilogue=_add_epilogue, name="mlp_out")
    dh2, dh2_b, dg_final, loss = _final_loss(h2, small["g_final"], tgt, name="final_loss")

    grads = {"g_final": dg_final}
    du = _mm(dh2_b, wts["w_mlp_out"], tb=True, out_dtypes=(BF16,), extras=(u,), epilogue=_drelu2_epilogue, name="d_mlp_act")
    grads["w_mlp_out"] = _mm(act, dh2_b, ta=True, out_dtypes=(BF16,), name="dw_mlp_out")
    grads["w_mlp_in"] = _mm(n2, du, ta=True, out_sharded=True, out_dtypes=(BF16,), name="dw_mlp_in")
    dn2 = _mm(du, wts["w_mlp_in"], tb=True, b_sharded=True, name="d_mlp_in")
    dh1, dh1_b, grads["g_mlp"] = _rms_bwd(dn2, h1, small["g_mlp"], dh2, name="rms_mlp_bwd")
    dmerged = _mm(dh1_b, wts["w_out"], tb=True, name="d_out_proj")
    grads["w_out"] = _mm(merged, dh1_b, ta=True, out_dtypes=(BF16,), name="dw_out")
    *dus, dp = _merge_bwd(ys, w_ups, gates, dmerged, cat_w, name="merge_bwd")
    dys = []
    for y, du_i, w_up, key in zip(ys, dus, w_ups, ("w_swa_up", "w_gdn_up", "w_xa_up")):
        dys.append(_mm(du_i, w_up, tb=True, b_sharded=True, out_dtypes=(BF16,), name="d_" + key))
        grads[key] = _mm(y, du_i, ta=True, out_sharded=True, out_dtypes=(BF16,), name="dw_" + key[2:])
    dp, dkv_a, grads["sinks"] = _swa_bwd(q_a, kv_a, small["sinks"], y_a, lse, dys[0], dp, cat["q_a"][0], name="swa_bwd")
    do_b, dp, grads["gdn_norm_w"] = _gdn_post_bwd(dys[1], o_b, z, small["gdn_norm_w"], dp, cat["z"][0],
                                                  name="gdn_post_bwd")
    dvn, dqd, dkd, dw_, dlast = _gdn_seq_bwd(do_b, gdn_w, gdn_qd, gdn_kd, gdn_mm, gdn_vn, s_all, aux, name="gdn_seq_bwd")
    dqkvn, daux = _gdn_local_bwd(qkvn, aux, aux_t, gdn_tinv, gdn_u, gdn_w, gdn_vn, do_b, dvn, dqd, dkd, dw_, dlast,
                                 name="gdn_local_bwd")
    dxc, dp, dalog, ddt = _gdn_pre_bwd1(xc, dqkvn, daux, ab, alog_pad, dt_pad, dkv_a, dp, cat["kv_a"][0], ab_blk=ab_blk,
                                        name="gdn_pre_bwd1")
    grads["a_log"], grads["dt_bias"] = dalog[:, :nh], ddt[:, :nh]
    dp, grads["conv_w"] = _gdn_pre_bwd2(dxc, qkvb, small["conv_w"], dp, cat["qkv_b"][0], name="gdn_pre_bwd2")
    dp, dmkv = _xa_bwd(q_c, mkv, dys[2], dp, cat["q_c"][0], name="xa_bwd")
    grads["w_mem_kv"] = _mm(nmem, dmkv, ta=True, out_dtypes=(BF16,), name="dw_mem_kv")
    dnmem = _mm(dmkv, wts["w_mem_kv"], tb=True, name="d_mem_kv")
    _, _, grads["g_mem"] = _rms_bwd(dnmem, mem, small["g_mem"], jnp.zeros_like(mem), name="rms_mem_bwd")
    if comm is None:
        grads["w_cat"] = _mm(n, dp, ta=True, out_dtypes=(BF16,), name="dw_in")
        dn = _mm(dp, w_cat, tb=True, name="d_in_proj")
    else:
        rest = [k for k in comm.names if k != "w_in"]
        s1_rest = comm.pair_sums([comm.shard_major(k, grads.pop(k)) for k in rest], "rest")
        dw_cat, rcv_rest = _mm(n, dp, ta=True, out_dtypes=(BF16,), name="dw_in", exchange=_chip_exchange(s1_rest))
        s1_in = comm.pair_sums([_from_cat(dw_cat)], "in")
        dn, rcv_in = _mm(dp, w_cat, tb=True, name="d_in_proj", exchange=_chip_exchange(s1_in))
        grads.update(zip(["w_in"] + rest, comm.finish(s1_in + s1_rest, rcv_in + rcv_rest)))
    dx, _, grads["g_mix"] = _rms_bwd(dn, x, small["g_mix"], dh1, name="rms_mix_bwd")
    return loss, dx, grads


HBM_SPEC = pl.BlockSpec(memory_space=pltpu.HBM)
VMEM_SPEC = pl.BlockSpec(memory_space=pltpu.VMEM)
N_CHIPS = N_SHARDS
N_DEV = 8
DMA_CHUNK_BYTES = 1 << 20


def _place():
    return lax.axis_index("x"), lax.axis_index("y"), lax.axis_index("c")


def _other_chips(x, y):
    return [(1 - x, y), (x, 1 - y), (1 - x, 1 - y)]


def _n_chunks(rows, row_bytes):
    n = 1
    while rows % (2 * n) == 0 and (rows // (2 * n)) % 16 == 0 and (rows // n) * row_bytes > DMA_CHUNK_BYTES:
        n *= 2
    return n


def _sem_scratch(n_remote, n_local):
    return [pltpu.SemaphoreType.DMA((max(n_remote, 1),)), pltpu.SemaphoreType.DMA((max(n_remote, 1),)),
            pltpu.SemaphoreType.DMA((max(n_local, 1),))]


def _all_gather_weights(shards, *, name):
    first = _exchange_call(_gather_over_ici(shards), name=name + "_ici")
    return _exchange_call(_gather_pass_on(first), name=name + "_pass")


def _gather_over_ici(shards):
    plan = _half_chunks(shards, 0)

    def copies_of(in_refs, out_refs, place):
        x, y, c = place
        remote, local = [], []
        for i, r0, nr in plan:
            rh = shards[i].shape[0] // 2
            mine = pl.ds(c * rh + r0, nr)
            for chip in _other_chips(x, y):
                remote.append((in_refs[i].at[mine], out_refs[i].at[2 * x + y, mine], (*chip, c)))
            for half in range(2):
                rows = pl.ds(half * rh + r0, nr)
                local.append((in_refs[i].at[rows], out_refs[i].at[2 * x + y, rows]))
        return remote, local

    shapes = tuple(jax.ShapeDtypeStruct((N_CHIPS, *s.shape), s.dtype) for s in shards)
    return Exchange(tuple(shards), shapes, 3 * len(plan), 2 * len(plan), copies_of)


def _gather_pass_on(arrived):
    plan = _half_chunks([jax.ShapeDtypeStruct(a.shape[1:], a.dtype) for a in arrived], 0)

    def copies_of(in_refs, out_refs, place):
        x, y, c = place
        remote = []
        for i, r0, nr in plan:
            mine = pl.ds(c * (arrived[i].shape[1] // 2) + r0, nr)
            for chip in _other_chips(x, y):
                rows = out_refs[i].at[2 * chip[0] + chip[1], mine]
                remote.append((rows, rows, (x, y, 1 - c)))
        return remote, []

    shapes = tuple(jax.ShapeDtypeStruct(a.shape, a.dtype) for a in arrived)
    return Exchange(tuple(arrived), shapes, 3 * len(plan), 0, copies_of, tuple((i, i) for i in range(len(arrived))))


def _exchange_call(ex, *, name):
    n_in, n_out = len(ex.ins), len(ex.out_shapes)

    def body(*refs):
        cps = _exchange_copies(ex, refs[:n_in], refs[n_in:n_in + n_out], refs[n_in + n_out:])
        for cp in cps:
            cp.start()
        for cp in cps:
            cp.wait()

    return pl.pallas_call(
        body, out_shape=list(ex.out_shapes), in_specs=[HBM_SPEC] * n_in, out_specs=[HBM_SPEC] * n_out,
        scratch_shapes=_sem_scratch(ex.n_remote, ex.n_local), input_output_aliases=dict(ex.aliases), name=name,
    )(*ex.ins)


def _half_chunks(arrs, row_axis):
    plan = []
    for i, a in enumerate(arrs):
        rh = a.shape[row_axis] // 2
        row_bytes = a.dtype.itemsize * math.prod(a.shape) // a.shape[row_axis]
        nch = _n_chunks(rh, row_bytes)
        plan += [(i, q * (rh // nch), rh // nch) for q in range(nch)]
    return plan


def _sibling_halves(gs, *, name):
    plan = _half_chunks(gs, 1)

    def copies_of(in_refs, out_refs, place):
        x, y, c = place
        out = []
        for i, r0, nr in plan:
            rh = gs[i].shape[1] // 2
            out.append((in_refs[i].at[:, pl.ds((1 - c) * rh + r0, nr), :], out_refs[i].at[:, pl.ds(r0, nr), :],
                        (x, y, 1 - c)))
        return out, []

    shapes = tuple(jax.ShapeDtypeStruct((g.shape[0], g.shape[1] // 2, g.shape[2]), g.dtype) for g in gs)
    return _exchange_call(Exchange(tuple(gs), shapes, len(plan), 0, copies_of), name=name)


def _chip_exchange(s1s):
    plan = _half_chunks([jax.ShapeDtypeStruct((2 * s.shape[1], s.shape[2]), s.dtype) for s in s1s], 0)

    def copies_of(in_refs, out_refs, place):
        x, y, c = place
        out = []
        for i, r0, nr in plan:
            for j, chip in enumerate(_other_chips(x, y)):
                out.append((in_refs[i].at[2 * chip[0] + chip[1], pl.ds(r0, nr), :], out_refs[i].at[j, pl.ds(r0, nr), :],
                            (*chip, c)))
        return out, []

    shapes = tuple(jax.ShapeDtypeStruct((3, *s.shape[1:]), s.dtype) for s in s1s)
    return Exchange(tuple(s1s), shapes, 3 * len(plan), 0, copies_of)


def _join_halves(gs, *, name):
    plan = _half_chunks(gs, 0)

    def copies_of(in_refs, out_refs, place):
        x, y, c = place
        out = []
        for i, r0, nr in plan:
            rows = out_refs[i].at[pl.ds(c * (gs[i].shape[0] // 2) + r0, nr), :]
            out.append((rows, rows, (x, y, 1 - c)))
        return out, []

    shapes = tuple(jax.ShapeDtypeStruct(g.shape, g.dtype) for g in gs)
    aliases = tuple((i, i) for i in range(len(gs)))
    return _exchange_call(Exchange(tuple(gs), shapes, len(plan), 0, copies_of, aliases), name=name)


def _row_block(rows, cols):
    tb = rows
    while tb % 32 == 0 and tb * cols * 4 > (2 << 20):
        tb //= 2
    return tb


def _pair_sum(g, sib, core, *, name):
    ns, r, c = g.shape
    rh = r // 2
    tb = _row_block(rh, c)
    nb = rh // tb

    def body(core_ref, g_ref, s_ref, o_ref):
        o_ref[...] = (g_ref[...].astype(F32) + s_ref[...].astype(F32)).astype(o_ref.dtype)

    mine = pl.BlockSpec((None, tb, c), lambda s, i, core_ref: (s, core_ref[0] * nb + i, 0))
    half = pl.BlockSpec((None, tb, c), lambda s, i, core_ref: (s, i, 0))
    return pl.pallas_call(
        body, grid_spec=pltpu.PrefetchScalarGridSpec(num_scalar_prefetch=1, grid=(ns, nb), in_specs=[mine, half],
                                                     out_specs=half),
        out_shape=jax.ShapeDtypeStruct((ns, rh, c), BF16), compiler_params=_cp(("parallel", "parallel")), name=name,
    )(core, g, sib)


def _chip_sum(s1, rcv, where, *, name):
    _, rh, c = s1.shape
    tb = _row_block(rh, c)
    nb = rh // tb

    def body(where_ref, own_ref, r0_ref, r1_ref, r2_ref, o_ref):
        acc = own_ref[...].astype(F32)
        for r in (r0_ref, r1_ref, r2_ref):
            acc = acc + r[...].astype(F32)
        o_ref[...] = acc

    own = pl.BlockSpec((None, tb, c), lambda i, w: (w[1], i, 0))
    got = [pl.BlockSpec((None, tb, c), functools.partial(lambda i, w, j: (j, i, 0), j=j)) for j in range(3)]
    return pl.pallas_call(
        body, grid_spec=pltpu.PrefetchScalarGridSpec(
            num_scalar_prefetch=1, grid=(nb,), in_specs=[own] + got,
            out_specs=pl.BlockSpec((tb, c), lambda i, w: (w[0] * nb + i, 0))),
        out_shape=jax.ShapeDtypeStruct((2 * rh, c), F32), compiler_params=_cp(("parallel",)), name=name,
    )(where, s1, rcv, rcv, rcv)


def _all_gather_small(blk, *, name):
    r = blk.shape[0]

    def body(b_ref, out_ref, send_sems, recv_sems):
        x, y, c = _place()
        me = 4 * x + 2 * y + c
        out_ref[me] = b_ref[...]
        sends = []
        for k in range(1, N_DEV):
            peer = (x ^ (k >> 2), y ^ ((k >> 1) & 1), c ^ (k & 1))
            sends.append(pltpu.make_async_remote_copy(src_ref=b_ref, dst_ref=out_ref.at[me], send_sem=send_sems.at[k - 1],
                                                      recv_sem=recv_sems.at[k - 1], device_id=peer, device_id_type=MESH))
        for cp in sends:
            cp.start()
        for k in range(1, N_DEV):
            rows = out_ref.at[me ^ k]
            pltpu.make_async_remote_copy(src_ref=rows, dst_ref=rows, send_sem=send_sems.at[k - 1],
                                         recv_sem=recv_sems.at[k - 1], device_id=(x, y, c), device_id_type=MESH).wait_recv()
        for cp in sends:
            cp.wait_send()

    return pl.pallas_call(
        body, out_shape=jax.ShapeDtypeStruct((N_DEV, r, LANES), blk.dtype), in_specs=[VMEM_SPEC], out_specs=VMEM_SPEC,
        scratch_shapes=[pltpu.SemaphoreType.DMA((N_DEV - 1,)), pltpu.SemaphoreType.DMA((N_DEV - 1,))],
        name=name,
    )(blk)


def _sum_rows(parts, out_dtype, *, name, tb=1024):
    rows = parts[0].shape[0]
    tb = _blk(rows, tb)

    def body(*refs):
        acc = refs[0][...].astype(F32)
        for r in refs[1:-1]:
            acc = acc + r[...].astype(F32)
        refs[-1][...] = acc.astype(refs[-1].dtype)

    spec = pl.BlockSpec((tb, LANES), lambda i: (i, 0))
    return pl.pallas_call(
        body, grid=(rows // tb,), in_specs=[spec] * len(parts), out_specs=spec,
        out_shape=jax.ShapeDtypeStruct((rows, LANES), out_dtype), compiler_params=_cp(("parallel",)), name=name,
    )(*parts)


BIG = (
    ("w_in", 1), ("w_mem_kv", 0), ("w_swa_up", 1), ("w_gdn_up", 1), ("w_xa_up", 1), ("w_out", 0), ("w_mlp_in", 1),
    ("w_mlp_out", 0))


class _Comm:
    def __init__(self, late_shards, core, where):
        self.names = [k for k, _ in BIG]
        self.axis = dict(BIG)
        self.late_names = list(late_shards)
        self.late_shards = [late_shards[k] for k in self.late_names]
        self.core, self.where = core, where

    def gather_exchange(self):
        return _gather_over_ici(self.late_shards)

    def gathered(self, landed):
        whole = _exchange_call(_gather_pass_on(landed), name="ag_rest_pass")
        return {k: (g.reshape(-1, g.shape[2]) if self.axis[k] == 0 else g) for k, g in zip(self.late_names, whole)}

    def shard_major(self, k, grad):
        return grad.reshape(N_CHIPS, -1, grad.shape[-1]) if self.axis[k] == 0 else grad

    def pair_sums(self, gs, tag):
        sibs = _sibling_halves(gs, name=f"rs_sibling_{tag}")
        return [_pair_sum(g, s, self.core, name=f"rs_pair_sum_{tag}{i}") for i, (g, s) in enumerate(zip(gs, sibs))]

    def finish(self, s1s, rcvs):
        halves = [_chip_sum(s1, rcv, self.where, name=f"rs_chip_sum_{i}") for i, (s1, rcv) in enumerate(zip(s1s, rcvs))]
        return _join_halves(halves, name="rs_join_halves")
SMALL = ("g_mix", "sinks", "a_log", "dt_bias", "gdn_norm_w", "g_mem", "g_mlp", "g_final")


def _rows128(a, rows):
    flat = a.reshape(-1)
    return jnp.pad(flat, (0, rows * LANES - flat.shape[0])).reshape(rows, LANES)


def kernel(x, mem, g_mix, w_in, sinks, conv_w, a_log, dt_bias, gdn_norm_w, g_mem, w_mem_kv, w_swa_up, w_gdn_up, w_xa_up, w_out, g_mlp, w_mlp_in, w_mlp_out, g_final, loss_target, m_g_mix, m_w_in, m_sinks, m_conv_w, m_a_log, m_dt_bias, m_gdn_norm_w, m_g_mem, m_w_mem_kv, m_w_swa_up, m_w_gdn_up, m_w_xa_up, m_w_out, m_g_mlp, m_w_mlp_in, m_w_mlp_out, m_g_final, v_g_mix, v_w_in, v_sinks, v_conv_w, v_a_log, v_dt_bias, v_gdn_norm_w, v_g_mem, v_w_mem_kv, v_w_swa_up, v_w_gdn_up, v_w_xa_up, v_w_out, v_g_mlp, v_w_mlp_in, v_w_mlp_out, v_g_final):
    given = dict(locals())
    xi, yi, ci = _place()
    chip = 2 * xi + yi
    core = jnp.reshape(ci, (1,)).astype(jnp.int32)
    where = jnp.stack([ci, chip]).astype(jnp.int32)

    shards = {k: given[k][0].astype(BF16) for k, _ in BIG}
    wts = {"w_cat": _to_cat(_all_gather_weights([shards.pop("w_in")], name="ag_w_in")[0])}
    comm = _Comm(shards, core, where)
    conv_shard = conv_w[0]
    conv_rows = -(-conv_shard.size // (8 * LANES)) * 8
    conv_all = _all_gather_small(_rows128(conv_shard, conv_rows), name="ag_conv")
    conv_full = jnp.concatenate(
        [conv_all[2 * s].reshape(-1)[:conv_shard.size].reshape(conv_shard.shape) for s in range(N_CHIPS)], axis=1)

    small = {k: given[k].reshape(1, -1) for k in SMALL}
    small["conv_w"] = conv_full
    loss_row, dx, grads = _local_step(x[0], mem[0], loss_target[0], wts, small, comm)
    big_grads = {k: grads[k] for k, _ in BIG}

    layout = [("loss", loss_row[:, :1])] + [(k, grads[k]) for k in SMALL] + [("conv_w", grads["conv_w"])]
    rows = [-(-a.size // LANES) for _, a in layout]
    blk_rows = -(-sum(rows) // 8) * 8
    blk = jnp.concatenate([_rows128(a.astype(F32), n) for (_, a), n in zip(layout, rows)]
                          + [jnp.zeros((blk_rows - sum(rows), LANES), F32)], axis=0)
    gathered = _all_gather_small(blk, name="ag_small_grads")
    reduced = _sum_rows([gathered[i] for i in range(N_DEV)], F32, name="small_grad_sum")
    small_grads, start = {}, 0
    for (k, a), n in zip(layout, rows):
        small_grads[k] = reduced[start:start + n].reshape(-1)[:a.size].reshape(a.shape)
        start += n
    loss = small_grads["loss"].reshape(())
    cw = conv_shard.shape[1]
    conv_grad = lax.dynamic_slice_in_dim(small_grads["conv_w"], chip * cw, cw, axis=1)

    names = ["g_mix", "w_in", "sinks", "conv_w", "a_log", "dt_bias", "gdn_norm_w", "g_mem", "w_mem_kv", "w_swa_up",
             "w_gdn_up", "w_xa_up", "w_out", "g_mlp", "w_mlp_in", "w_mlp_out", "g_final"]
    out_g, out_d, out_m, out_v = [], [], [], []
    for k in names:
        w, m, v = given[k], given["m_" + k], given["v_" + k]
        if k in big_grads:
            g2 = big_grads[k]
        elif k == "conv_w":
            g2 = conv_grad
        else:
            g2 = small_grads[k]
        as_given = (lambda a: a.reshape(1, -1)) if w.ndim == 1 else (lambda a: a)
        delta, new_m, new_v = _adamw(as_given(w), g2, as_given(m), as_given(v), name="adamw_" + k)
        out_g.append(g2.reshape(w.shape))
        out_d.append(delta.reshape(w.shape))
        out_m.append(new_m.reshape(w.shape))
        out_v.append(new_v.reshape(w.shape))
    return (loss, dx[None], *out_g, *out_d, *out_m, *out_v)
```

```python
import functools
import math
from typing import Callable, NamedTuple

import jax
import jax.numpy as jnp
from jax import lax
from jax.experimental import pallas as pl
from jax.experimental.pallas import tpu as pltpu

F32 = jnp.float32
BF16 = jnp.bfloat16
HI = lax.Precision.HIGHEST
MESH = pl.DeviceIdType.MESH

SWA_Q_HEADS = 16
SWA_KV_HEADS = 2
SWA_HEAD_DIM = 64
SWA_WINDOW = 128
SWA_SCALE = SWA_HEAD_DIM ** -0.5
assert math.frexp(SWA_SCALE)[0] == 0.5
GDN_HEADS = 4
GDN_HEAD_DIM = 128
GDN_CONV = 4
GDN_CHUNK = 64
XA_HEADS = 4
XA_HEAD_DIM = 128
RMS_EPS = 1e-6
L2_EPS = 1e-6
ADAM_LR = 0.001
ADAM_B1 = 0.9
ADAM_B2 = 0.999
ADAM_EPS = 1e-08
ADAM_WD = 0.01
ADAM_STEP = 10

LANES = 128
N_SHARDS = 4
VMEM_LIMIT = 56 * 1024 * 1024

NT = (((1,), (1,)), ((), ()))
TN = (((0,), (0,)), ((), ()))
NN = (((1,), (0,)), ((), ()))


def _cp(sem=None):
    return pltpu.CompilerParams(dimension_semantics=sem, vmem_limit_bytes=VMEM_LIMIT)


def _blk(dim, pref):
    if dim <= pref:
        return dim
    b = (pref // LANES) * LANES
    while dim % b:
        b -= LANES
    assert b > 0, (dim, pref)
    return b


def _dot(a, b, dims=NN, precision=None):
    return lax.dot_general(a, b, dims, precision=precision, preferred_element_type=F32)


def _sigmoid(x):
    return 1.0 / (1.0 + jnp.exp(-x))


MM_TK_BYTES = 4096


def _mm(a, b, *, name, ta=False, tb=False, out_dtypes=(F32,), epilogue=None, extras=(), tm=1024, tn=1024, tk=None,
        b_sharded=False, out_sharded=False, b_window=None, exchange=None):
    (kdim, m) = a.shape if ta else a.shape[::-1]
    col0 = 0
    n_lim = k_lim = None
    if b_sharded:
        ns, rows_w, per = b.shape
        if tb:
            kb, n, k_lim = ns * per, rows_w, per
        else:
            kb, n, n_lim = rows_w, ns * per, per
    else:
        (kb, n) = b.shape[::-1] if tb else b.shape
        if b_window is not None:
            assert not tb
            col0, n = b_window
    assert kdim == kb, (a.shape, b.shape, ta, tb)
    if out_sharded:
        assert n % N_SHARDS == 0
        n_lim = n // N_SHARDS if n_lim is None else n_lim
        assert n_lim == n // N_SHARDS
    if tk is None:
        tk = MM_TK_BYTES // max(a.dtype.itemsize, b.dtype.itemsize)
    tm, tn, tk = _blk(m, tm), _blk(n_lim or n, tn), _blk(k_lim or kdim, tk)
    assert col0 % tn == 0, (col0, tn)
    nk = kdim // tk
    a_spec = pl.BlockSpec((tk, tm), lambda i, j, k: (k, i)) if ta else pl.BlockSpec((tm, tk), lambda i, j, k: (i, k))
    if b_sharded and tb:
        kpb = k_lim // tk
        b_spec = pl.BlockSpec((None, tn, tk), lambda i, j, k: (k // kpb, j, k % kpb))
    elif b_sharded:
        bpb = n_lim // tn
        b_spec = pl.BlockSpec((None, tk, tn), lambda i, j, k: (j // bpb, k, j % bpb))
    elif tb:
        b_spec = pl.BlockSpec((tn, tk), lambda i, j, k: (j, k))
    else:
        b_spec = pl.BlockSpec((tk, tn), lambda i, j, k: (k, j + col0 // tn))
    x_spec = pl.BlockSpec((tm, tn), lambda i, j, k: (i, j))
    if out_sharded:
        opb = n_lim // tn
        o_spec = pl.BlockSpec((None, tm, tn), lambda i, j, k: (j // opb, i, j % opb))
        out_shape = (N_SHARDS, m, n_lim)
    else:
        o_spec, out_shape = x_spec, (m, n)
    dims = ((((0 if ta else 1),), ((1 if tb else 0),)), ((), ()))
    n_extra, n_out = len(extras), len(out_dtypes)

    host = _ExchangeHost(exchange)
    grid = (m // tm, n // tn, nk)

    def body(*refs):
        a_ref, b_ref = refs[:2]
        extra_refs = refs[2:2 + n_extra]
        out_refs = refs[2 + n_extra + host.n_in:2 + n_extra + host.n_in + n_out]
        host.start(refs, 2 + n_extra, 2 + n_extra + host.n_in + n_out, grid)
        part = _dot(a_ref[...].astype(BF16), b_ref[...].astype(BF16), dims)

        def finish(acc):
            vals = epilogue(acc, *[r[...] for r in extra_refs]) if epilogue is not None else (acc,) * n_out
            assert len(vals) == n_out
            for r, v in zip(out_refs, vals):
                r[...] = v.astype(r.dtype)

        if nk == 1:
            finish(part)
        else:
            acc_ref = refs[2 + n_extra + host.n_in + n_out + host.n_out]
            k = pl.program_id(2)

            @pl.when(k == 0)
            def _():
                acc_ref[...] = part

            @pl.when((k > 0) & (k < nk - 1))
            def _():
                acc_ref[...] += part

            @pl.when(k == nk - 1)
            def _():
                finish(acc_ref[...] + part)

        host.wait(refs, 2 + n_extra, 2 + n_extra + host.n_in + n_out, grid)

    outs = pl.pallas_call(
        body,
        grid=grid,
        in_specs=[a_spec, b_spec] + [x_spec] * n_extra + host.in_specs,
        out_specs=[o_spec] * n_out + host.out_specs,
        out_shape=[jax.ShapeDtypeStruct(out_shape, d) for d in out_dtypes] + host.out_shapes,
        scratch_shapes=([pltpu.VMEM((tm, tn), F32)] if nk > 1 else []) + host.scratch,
        compiler_params=_cp(host.semantics(("parallel", "parallel", "arbitrary"))),
        name=name,
    )(a, b, *extras, *host.ins)
    mine, landed = outs[:n_out], list(outs[n_out:])
    mine = mine[0] if n_out == 1 else mine
    return (mine, landed) if exchange is not None else mine


class Exchange(NamedTuple):
    ins: tuple
    out_shapes: tuple
    n_remote: int
    n_local: int
    copies_of: Callable
    aliases: tuple = ()


def _exchange_copies(ex, in_refs, out_refs, sem_refs):
    send_sems, recv_sems, local_sems = sem_refs
    remote, local = ex.copies_of(in_refs, out_refs, _place())
    assert len(remote) == ex.n_remote and len(local) == ex.n_local, (len(remote), len(local))
    cps = [pltpu.make_async_remote_copy(src_ref=src, dst_ref=dst, send_sem=send_sems.at[k], recv_sem=recv_sems.at[k],
                                        device_id=to, device_id_type=MESH) for k, (src, dst, to) in enumerate(remote)]
    cps += [pltpu.make_async_copy(src, dst, local_sems.at[k]) for k, (src, dst) in enumerate(local)]
    return cps


class _ExchangeHost:
    def __init__(self, ex):
        self.ex = ex
        self.ins = list(ex.ins) if ex else []
        self.out_shapes = list(ex.out_shapes) if ex else []
        self.n_in, self.n_out = len(self.ins), len(self.out_shapes)
        self.in_specs = [HBM_SPEC] * self.n_in
        self.out_specs = [HBM_SPEC] * self.n_out
        self.scratch = _sem_scratch(ex.n_remote, ex.n_local) if ex else []
        assert not (ex and ex.aliases)

    def semantics(self, sem):
        return tuple("arbitrary" for _ in sem) if self.ex else sem

    def _refs(self, refs, in_at, out_at):
        return refs[in_at:in_at + self.n_in], refs[out_at:out_at + self.n_out], refs[len(refs) - 3:]

    def _when(self, grid, last):
        cond = None
        for d, size in enumerate(grid):
            c = pl.program_id(d) == (size - 1 if last else 0)
            cond = c if cond is None else cond & c
        return cond

    def start(self, refs, in_at, out_at, grid):
        if self.ex:
            @pl.when(self._when(grid, False))
            def _():
                for cp in _exchange_copies(self.ex, *self._refs(refs, in_at, out_at)):
                    cp.start()

    def wait(self, refs, in_at, out_at, grid):
        if self.ex:
            @pl.when(self._when(grid, True))
            def _():
                for cp in _exchange_copies(self.ex, *self._refs(refs, in_at, out_at)):
                    cp.wait()


def _rms_fwd(x, g, *, name, tm=512):
    t, d = x.shape
    tm = _blk(t, tm)

    def body(x_ref, g_ref, n_ref):
        xv = x_ref[...]
        r = lax.rsqrt(jnp.mean(xv * xv, axis=-1, keepdims=True) + RMS_EPS)
        n_ref[...] = (xv * r * g_ref[...]).astype(n_ref.dtype)

    return pl.pallas_call(
        body, grid=(t // tm,),
        in_specs=[pl.BlockSpec((tm, d), lambda i: (i, 0)), pl.BlockSpec((1, d), lambda i: (0, 0))],
        out_specs=pl.BlockSpec((tm, d), lambda i: (i, 0)),
        out_shape=jax.ShapeDtypeStruct((t, d), BF16),
        compiler_params=_cp(("parallel",)), name=name,
    )(x, g)


def _rms_bwd(dn, x, g, dres, *, name, tm=512):
    t, d = x.shape
    tm = _blk(t, tm)

    def body(dn_ref, x_ref, g_ref, dres_ref, dx_ref, dxb_ref, dg_ref):
        i = pl.program_id(0)
        xv = x_ref[...]
        r = lax.rsqrt(jnp.mean(xv * xv, axis=-1, keepdims=True) + RMS_EPS)
        xh = xv * r
        dnv = dn_ref[...].astype(F32)
        dxh = dnv * g_ref[...]
        dx = dres_ref[...] + r * (dxh - xh * jnp.mean(dxh * xh, axis=-1, keepdims=True))
        dx_ref[...] = dx
        dxb_ref[...] = dx.astype(dxb_ref.dtype)
        part = jnp.sum(dnv * xh, axis=0, keepdims=True)

        @pl.when(i == 0)
        def _():
            dg_ref[...] = part

        @pl.when(i > 0)
        def _():
            dg_ref[...] += part

    row = pl.BlockSpec((tm, d), lambda i: (i, 0))
    vec = pl.BlockSpec((1, d), lambda i: (0, 0))
    return pl.pallas_call(
        body, grid=(t // tm,),
        in_specs=[row, row, vec, row], out_specs=[row, row, vec],
        out_shape=[jax.ShapeDtypeStruct((t, d), F32), jax.ShapeDtypeStruct((t, d), BF16),
                   jax.ShapeDtypeStruct((1, d), F32)],
        compiler_params=_cp(("arbitrary",)), name=name,
    )(dn, x, g, dres)


def _final_loss(h, g, tgt, *, name, tm=512):
    t, d = h.shape
    tm = _blk(t, tm)

    def body(h_ref, g_ref, t_ref, dh_ref, dhb_ref, dg_ref, loss_ref):
        i = pl.program_id(0)
        hv = h_ref[...]
        r = lax.rsqrt(jnp.mean(hv * hv, axis=-1, keepdims=True) + RMS_EPS)
        xh = hv * r
        e = xh * g_ref[...] - t_ref[...]
        dy = e * (1.0 / d)
        dxh = dy * g_ref[...]
        dh = r * (dxh - xh * jnp.mean(dxh * xh, axis=-1, keepdims=True))
        dh_ref[...] = dh
        dhb_ref[...] = dh.astype(dhb_ref.dtype)
        dg_part = jnp.sum(dy * xh, axis=0, keepdims=True)
        row_loss = jnp.sum(e * e, axis=-1, keepdims=True) * (0.5 / d)
        loss_part = jnp.sum(row_loss, axis=0, keepdims=True)

        @pl.when(i == 0)
        def _():
            dg_ref[...] = dg_part
            loss_ref[...] = jnp.broadcast_to(loss_part, loss_ref.shape)

        @pl.when(i > 0)
        def _():
            dg_ref[...] += dg_part
            loss_ref[...] += jnp.broadcast_to(loss_part, loss_ref.shape)

    row = pl.BlockSpec((tm, d), lambda i: (i, 0))
    vec = pl.BlockSpec((1, d), lambda i: (0, 0))
    return pl.pallas_call(
        body, grid=(t // tm,),
        in_specs=[row, vec, row], out_specs=[row, row, vec, pl.BlockSpec((1, LANES), lambda i: (0, 0))],
        out_shape=[jax.ShapeDtypeStruct((t, d), F32), jax.ShapeDtypeStruct((t, d), BF16),
                   jax.ShapeDtypeStruct((1, d), F32), jax.ShapeDtypeStruct((1, LANES), F32)],
        compiler_params=_cp(("arbitrary",)), name=name,
    )(h, g, tgt)


def _swa_mask(n, reps):
    w = SWA_WINDOW
    qi = lax.broadcasted_iota(jnp.int32, (reps * w, 2 * w), 0) & (w - 1)
    kj = lax.broadcasted_iota(jnp.int32, (reps * w, 2 * w), 1)
    return (kj > qi) & (kj <= qi + w) & ((n > 0) | (kj >= w))


def _stack_heads(ref, heads, width):
    return jnp.concatenate([ref[:, h * width:(h + 1) * width] for h in heads], axis=0)


def _stack_scalars(ref, heads, rows):
    return jnp.concatenate([jnp.broadcast_to(ref[0:1, h:h + 1], (rows, 1)) for h in heads], axis=0)


def _swa_fwd(q, kv, sinks, *, name, exchange=None):
    t = q.shape[0]
    w, hd, hq, hkv = SWA_WINDOW, SWA_HEAD_DIM, SWA_Q_HEADS, SWA_KV_HEADS
    grp = hq // hkv
    kvw = hkv * hd
    nb = t // w
    host = _ExchangeHost(exchange)

    def body(*refs):
        q_ref, kvp_ref, kvc_ref, s_ref = refs[:4]
        o_ref, lse_ref = refs[4 + host.n_in:6 + host.n_in]
        host.start(refs, 4, 6 + host.n_in, (nb,))
        n = pl.program_id(0)
        mask = _swa_mask(n, grp)
        kvcat = jnp.concatenate([kvp_ref[...], kvc_ref[...]], axis=0)
        outs, lses = [], []
        for hk in range(hkv):
            heads = range(hk * grp, (hk + 1) * grp)
            qs = _stack_heads(q_ref, heads, hd)
            kh = kvcat[:, hk * hd:(hk + 1) * hd]
            vh = kvcat[:, kvw + hk * hd:kvw + (hk + 1) * hd]
            sk = _stack_scalars(s_ref, heads, w)
            s = jnp.where(mask, _dot(qs * SWA_SCALE, kh, NT), -jnp.inf)
            m = jnp.maximum(jnp.max(s, axis=-1, keepdims=True), sk)
            p = jnp.exp(s - m)
            den = jnp.sum(p, axis=-1, keepdims=True) + jnp.exp(sk - m)
            o = _dot((p * (1.0 / den)).astype(BF16), vh)
            lse = m + jnp.log(den)
            outs += [o[j * w:(j + 1) * w] for j in range(grp)]
            lses += [lse[j * w:(j + 1) * w] for j in range(grp)]
        o_ref[...] = jnp.concatenate(outs, axis=1).astype(o_ref.dtype)
        lse_ref[...] = jnp.concatenate(lses, axis=1)
        host.wait(refs, 4, 6 + host.n_in, (nb,))

    outs = pl.pallas_call(
        body, grid=(nb,),
        in_specs=[pl.BlockSpec((w, hq * hd), lambda i: (i, 0)),
                  pl.BlockSpec((w, 2 * kvw), lambda i: (jnp.maximum(i - 1, 0), 0)),
                  pl.BlockSpec((w, 2 * kvw), lambda i: (i, 0)),
                  pl.BlockSpec((1, hq), lambda i: (0, 0))] + host.in_specs,
        out_specs=[pl.BlockSpec((w, hq * hd), lambda i: (i, 0)), pl.BlockSpec((w, hq), lambda i: (i, 0))] + host.out_specs,
        out_shape=[jax.ShapeDtypeStruct((t, hq * hd), BF16), jax.ShapeDtypeStruct((t, hq), F32)] + host.out_shapes,
        scratch_shapes=host.scratch,
        compiler_params=_cp(host.semantics(("parallel",))), name=name,
    )(q, kv, kv, sinks, *host.ins)
    return (outs[0], outs[1], list(outs[2:])) if exchange is not None else outs


ANY_SPEC = pl.BlockSpec(memory_space=pl.ANY)


def _swa_bwd(q, kv, sinks, o, lse, do, dp, dp_col, *, name):
    t = q.shape[0]
    w, hd, hq, hkv = SWA_WINDOW, SWA_HEAD_DIM, SWA_Q_HEADS, SWA_KV_HEADS
    grp = hq // hkv
    kvw = hkv * hd
    nb = t // w
    assert dp_col % (hq * hd) == 0
    dq_blk = dp_col // (hq * hd)

    def body(q_ref, kvp_ref, kvc_ref, s_ref, o_ref, lse_ref, do_ref, _, dq_ref, dkv_ref, ds_ref, carry_ref):
        n = pl.program_id(0)

        @pl.when(n == 0)
        def _():
            ds_ref[...] = jnp.zeros_like(ds_ref)
            carry_ref[...] = jnp.zeros_like(carry_ref)

        @pl.when(n < nb)
        def _():
            mask = _swa_mask(n, grp)
            kvcat = jnp.concatenate([kvp_ref[...], kvc_ref[...]], axis=0)
            dqs, dsk, dks, dvs = [], [], [], []
            for hk in range(hkv):
                heads = range(hk * grp, (hk + 1) * grp)
                qs = _stack_heads(q_ref, heads, hd)
                dos = _stack_heads(do_ref, heads, hd)
                os_ = _stack_heads(o_ref, heads, hd)
                lse = _stack_heads(lse_ref, heads, 1)
                sk = _stack_scalars(s_ref, heads, w)
                kh = kvcat[:, hk * hd:(hk + 1) * hd]
                vh = kvcat[:, kvw + hk * hd:kvw + (hk + 1) * hd]
                s = _dot(qs * SWA_SCALE, kh, NT)
                p = jnp.exp(jnp.where(mask, s, -jnp.inf) - lse)
                delta = jnp.sum(dos.astype(F32) * os_.astype(F32), axis=-1, keepdims=True)
                ds = (p * (_dot(dos, vh, NT) - delta) * SWA_SCALE).astype(BF16)
                dq = _dot(ds, kh)
                dqs += [dq[j * w:(j + 1) * w] for j in range(grp)]
                dks.append(_dot(ds, qs, TN))
                dvs.append(_dot(p.astype(BF16), dos, TN))
                dsink = -jnp.exp(sk - lse) * delta
                dsk += [jnp.sum(dsink[j * w:(j + 1) * w], axis=0, keepdims=True) for j in range(grp)]
            dq_ref[...] = jnp.concatenate(dqs, axis=1).astype(dq_ref.dtype)
            ds_ref[...] += jnp.concatenate(dsk, axis=1)
            dkv_cat = jnp.concatenate(dks + dvs, axis=1)
            dkv_ref[...] = (carry_ref[...] + dkv_cat[:w]).astype(dkv_ref.dtype)
            carry_ref[...] = dkv_cat[w:]

        @pl.when(n == nb)
        def _():
            dkv_ref[...] = carry_ref[...].astype(dkv_ref.dtype)

    cur = lambda i: (jnp.minimum(i, nb - 1), 0)
    prev = lambda i: (jnp.clip(i - 1, 0, nb - 1), 0)
    return pl.pallas_call(
        body, grid=(nb + 1,),
        in_specs=[pl.BlockSpec((w, hq * hd), cur), pl.BlockSpec((w, 2 * kvw), prev), pl.BlockSpec((w, 2 * kvw), cur),
                  pl.BlockSpec((1, hq), lambda i: (0, 0)), pl.BlockSpec((w, hq * hd), cur),
                  pl.BlockSpec((w, hq), cur), pl.BlockSpec((w, hq * hd), cur), ANY_SPEC],
        out_specs=[pl.BlockSpec((w, hq * hd), lambda i: (jnp.minimum(i, nb - 1), dq_blk)),
                   pl.BlockSpec((w, 2 * kvw), prev), pl.BlockSpec((1, hq), lambda i: (0, 0))],
        out_shape=[jax.ShapeDtypeStruct(dp.shape, dp.dtype), jax.ShapeDtypeStruct((t, 2 * kvw), BF16),
                   jax.ShapeDtypeStruct((1, hq), F32)],
        scratch_shapes=[pltpu.VMEM((w, 2 * kvw), F32)], input_output_aliases={7: 0},
        compiler_params=_cp(("arbitrary",)), name=name,
    )(q, kv, kv, sinks, o, lse, do, dp)


def _xa_fwd(q, mkv, *, name, tq=512):
    t, xw = q.shape
    nm = mkv.shape[0]
    hd, nh = XA_HEAD_DIM, XA_HEADS
    tq = _blk(t, tq)

    def body(q_ref, mkv_ref, o_ref):
        outs = []
        for h in range(nh):
            qh = q_ref[:, h * hd:(h + 1) * hd]
            kh = mkv_ref[:, h * hd:(h + 1) * hd]
            vh = mkv_ref[:, xw + h * hd:xw + (h + 1) * hd]
            s = _dot(qh, kh, NT) * (hd ** -0.5)
            p = jnp.exp(s - jnp.max(s, axis=-1, keepdims=True))
            p = p / jnp.sum(p, axis=-1, keepdims=True)
            outs.append(_dot(p.astype(BF16), vh))
        o_ref[...] = jnp.concatenate(outs, axis=1).astype(o_ref.dtype)

    return pl.pallas_call(
        body, grid=(t // tq,),
        in_specs=[pl.BlockSpec((tq, xw), lambda i: (i, 0)), pl.BlockSpec((nm, 2 * xw), lambda i: (0, 0))],
        out_specs=pl.BlockSpec((tq, xw), lambda i: (i, 0)),
        out_shape=jax.ShapeDtypeStruct((t, xw), BF16),
        compiler_params=_cp(("parallel",)), name=name,
    )(q, mkv)


def _xa_bwd(q, mkv, do, dp, dp_col, *, name, tq=512):
    t, xw = q.shape
    nm = mkv.shape[0]
    hd, nh = XA_HEAD_DIM, XA_HEADS
    tq = _blk(t, tq)
    assert dp_col % xw == 0

    def body(q_ref, mkv_ref, do_ref, _, dq_ref, dmkv_ref):
        i = pl.program_id(0)
        dqs, dks, dvs = [], [], []
        for h in range(nh):
            qh = q_ref[:, h * hd:(h + 1) * hd]
            kh = mkv_ref[:, h * hd:(h + 1) * hd]
            vh = mkv_ref[:, xw + h * hd:xw + (h + 1) * hd]
            doh = do_ref[:, h * hd:(h + 1) * hd]
            s = _dot(qh, kh, NT) * (hd ** -0.5)
            p = jnp.exp(s - jnp.max(s, axis=-1, keepdims=True))
            p = p / jnp.sum(p, axis=-1, keepdims=True)
            dp = _dot(doh, vh, NT)
            ds = (p * (dp - jnp.sum(p * dp, axis=-1, keepdims=True)) * (hd ** -0.5)).astype(BF16)
            dqs.append(_dot(ds, kh))
            dks.append(_dot(ds, qh, TN))
            dvs.append(_dot(p.astype(BF16), doh, TN))
        dq_ref[...] = jnp.concatenate(dqs, axis=1).astype(dq_ref.dtype)
        part = jnp.concatenate(dks + dvs, axis=1)

        @pl.when(i == 0)
        def _():
            dmkv_ref[...] = part

        @pl.when(i > 0)
        def _():
            dmkv_ref[...] += part

    row = pl.BlockSpec((tq, xw), lambda i: (i, 0))
    full = pl.BlockSpec((nm, 2 * xw), lambda i: (0, 0))
    return pl.pallas_call(
        body, grid=(t // tq,),
        in_specs=[row, full, row, ANY_SPEC],
        out_specs=[pl.BlockSpec((tq, xw), lambda i: (i, dp_col // xw)), full],
        out_shape=[jax.ShapeDtypeStruct(dp.shape, dp.dtype), jax.ShapeDtypeStruct((nm, 2 * xw), F32)],
        input_output_aliases={3: 0}, compiler_params=_cp(("arbitrary",)), name=name,
    )(q, mkv, do, dp)


def _merge_specs(ys, ws, tm):
    y_specs = [pl.BlockSpec((tm, y.shape[1]), lambda i: (i, 0)) for y in ys]
    w_specs = [pl.BlockSpec(w.shape, lambda i: (0, 0, 0)) for w in ws]
    return y_specs, w_specs


def _merge_tiles(ws, tn):
    ns, _, per = ws[0].shape
    tn = _blk(per, tn)
    return tn, [(s, c, s * per + c) for s in range(ns) for c in range(0, per, tn)]


def _merge_fwd(ys, ws, gates, *, name, tm=256, tn=512):
    t, d = ys[0].shape[0], ws[0].shape[0] * ws[0].shape[2]
    tm = _blk(t, tm)
    tn, tiles = _merge_tiles(ws, tn)
    y_specs, w_specs = _merge_specs(ys, ws, tm)

    def body(ya, yb, yc, wa, wb, wc, g_ref, o_ref):
        for s, c, col in tiles:
            acc = None
            for b, (y, w) in enumerate(((ya, wa), (yb, wb), (yc, wc))):
                term = _sigmoid(g_ref[:, b * d + col:b * d + col + tn]) * _dot(y[...], w[s, :, c:c + tn])
                acc = term if acc is None else acc + term
            o_ref[:, col:col + tn] = acc.astype(o_ref.dtype)

    return pl.pallas_call(
        body, grid=(t // tm,),
        in_specs=y_specs + w_specs + [pl.BlockSpec((tm, 3 * d), lambda i: (i, 0))],
        out_specs=pl.BlockSpec((tm, d), lambda i: (i, 0)),
        out_shape=jax.ShapeDtypeStruct((t, d), BF16),
        compiler_params=_cp(("parallel",)), name=name,
    )(*ys, *ws, gates)


def _merge_bwd(ys, ws, gates, dmerged, dp_width, *, name, tm=256, tn=512):
    t, d = ys[0].shape[0], ws[0].shape[0] * ws[0].shape[2]
    tm = _blk(t, tm)
    tn, tiles = _merge_tiles(ws, tn)
    y_specs, w_specs = _merge_specs(ys, ws, tm)
    row = pl.BlockSpec((tm, d), lambda i: (i, 0))
    wide = pl.BlockSpec((tm, 3 * d), lambda i: (i, 0))

    def body(ya, yb, yc, wa, wb, wc, g_ref, dm_ref, dua, dub, duc, dp_ref):
        for s, c, col in tiles:
            dm = dm_ref[:, col:col + tn]
            for b, (y, w, du) in enumerate(((ya, wa, dua), (yb, wb, dub), (yc, wc, duc))):
                sg = _sigmoid(g_ref[:, b * d + col:b * d + col + tn])
                u = _dot(y[...], w[s, :, c:c + tn])
                du[:, col:col + tn] = (dm * sg).astype(du.dtype)
                dp_ref[:, b * d + col:b * d + col + tn] = (dm * u * sg * (1.0 - sg)).astype(dp_ref.dtype)

    return pl.pallas_call(
        body, grid=(t // tm,),
        in_specs=y_specs + w_specs + [wide, row],
        out_specs=[row] * 3 + [wide],
        out_shape=[jax.ShapeDtypeStruct((t, d), BF16)] * 3 + [jax.ShapeDtypeStruct((t, dp_width), BF16)],
        compiler_params=_cp(("parallel",)), name=name,
    )(*ys, *ws, gates, dmerged)


def _adamw(w, g, m, v, *, name, tm=256):
    lead = w.ndim - 2
    assert all(s == 1 for s in w.shape[:lead]) and m.shape == w.shape and v.shape == w.shape
    r, c = w.shape[lead:]
    assert g.shape == (r, c)
    tm = _blk(r, tm) if r % 8 == 0 else r
    bc1 = 1.0 - ADAM_B1 ** ADAM_STEP
    bc2 = 1.0 - ADAM_B2 ** ADAM_STEP

    def body(w_ref, g_ref, m_ref, v_ref, d_ref, nm_ref, nv_ref):
        gv = g_ref[...]
        nm = ADAM_B1 * m_ref[...] + (1.0 - ADAM_B1) * gv
        nv = ADAM_B2 * v_ref[...] + (1.0 - ADAM_B2) * (gv * gv)
        d_ref[...] = -ADAM_LR * ((nm / bc1) / (jnp.sqrt(nv / bc2) + ADAM_EPS) + ADAM_WD * w_ref[...])
        nm_ref[...] = nm
        nv_ref[...] = nv

    spec = pl.BlockSpec((None,) * lead + (tm, c), lambda i: (0,) * lead + (i, 0))
    g_spec = pl.BlockSpec((tm, c), lambda i: (i, 0))
    return pl.pallas_call(
        body, grid=(r // tm,), in_specs=[spec, g_spec, spec, spec], out_specs=[spec] * 3,
        out_shape=[jax.ShapeDtypeStruct(w.shape, F32)] * 3,
        compiler_params=_cp(("parallel",)), name=name,
    )(w, g, m, v)


HALO = 8


def _shift_down(cur, prev, j):
    if j == 0:
        return cur
    y = pltpu.roll(cur, j, 0)
    row = lax.broadcasted_iota(jnp.int32, (HALO, cur.shape[1]), 0)
    top = jnp.where(row < j, pltpu.roll(prev, j, 0), y[:HALO])
    return jnp.concatenate([top, y[HALO:]], axis=0)


def _shift_up(cur, nxt, j):
    if j == 0:
        return cur
    tm = cur.shape[0]
    y = pltpu.roll(cur, tm - j, 0)
    row = lax.broadcasted_iota(jnp.int32, (HALO, cur.shape[1]), 0)
    bot = jnp.where(row >= HALO - j, pltpu.roll(nxt, HALO - j, 0), y[tm - HALO:])
    return jnp.concatenate([y[:tm - HALO], bot], axis=0)


def _softplus(x):
    return jnp.maximum(x, 0.0) + jnp.log(1.0 + jnp.exp(-jnp.abs(x)))


def _gdn_pre_fwd(qkvb, conv_w, ab, alog_pad, dt_pad, *, name, ab_blk=0, tm=256):
    t, cw = qkvb.shape
    hd, nh, ck = GDN_HEAD_DIM, GDN_HEADS, GDN_CHUNK
    gw = nh * hd
    tm = _blk(t, tm)
    hb = tm // HALO

    def body(x_ref, xp_ref, w_ref, ab_ref, al_ref, dt_ref, xc_ref, qkvn_ref, aux_ref):
        i = pl.program_id(0)
        cur = x_ref[...]
        prev = jnp.where(i > 0, xp_ref[...], 0.0)
        xc = None
        for tap in range(GDN_CONV):
            term = w_ref[tap:tap + 1, :] * _shift_down(cur, prev, GDN_CONV - 1 - tap)
            xc = term if xc is None else xc + term
        xc_ref[...] = xc
        s = xc * _sigmoid(xc)
        for h in range(2 * nh):
            xh = s[:, h * hd:(h + 1) * hd]
            r = lax.rsqrt(jnp.sum(xh * xh, axis=-1, keepdims=True) + L2_EPS)
            scale = hd ** -0.5 if h < nh else 1.0
            qkvn_ref[:, h * hd:(h + 1) * hd] = xh * (r * scale)
        qkvn_ref[:, 2 * gw:] = s[:, 2 * gw:]
        abv = ab_ref[...]
        lane = lax.broadcasted_iota(jnp.int32, abv.shape, 1)
        g = jnp.where(lane < nh, -jnp.exp(al_ref[...]) * _softplus(abv + dt_ref[...]), 0.0)
        beta = jnp.where((lane >= nh) & (lane < 2 * nh), _sigmoid(abv), 0.0)
        ii = lax.broadcasted_iota(jnp.int32, (tm, tm), 0)
        jj = lax.broadcasted_iota(jnp.int32, (tm, tm), 1)
        tri = jnp.where((ii >= jj) & ((ii ^ jj) < ck), 1.0, 0.0)
        gcum = _dot(tri, g, precision=HI)
        aux_ref[...] = g + beta + pltpu.roll(gcum, 2 * nh, 1)

    row = lambda c: pl.BlockSpec((tm, c), lambda i: (i, 0))
    vec = lambda r, c: pl.BlockSpec((r, c), lambda i: (0, 0))
    return pl.pallas_call(
        body, grid=(t // tm,),
        in_specs=[row(cw), pl.BlockSpec((HALO, cw), lambda i: (jnp.maximum(i * hb - 1, 0), 0)), vec(GDN_CONV, cw),
                  pl.BlockSpec((tm, LANES), lambda i: (i, ab_blk)), vec(1, LANES), vec(1, LANES)],
        out_specs=[row(cw), row(cw), row(LANES)],
        out_shape=[jax.ShapeDtypeStruct((t, cw), F32), jax.ShapeDtypeStruct((t, cw), F32),
                   jax.ShapeDtypeStruct((t, LANES), F32)],
        compiler_params=_cp(("parallel",)), name=name,
    )(qkvb, qkvb, conv_w, ab, alog_pad, dt_pad)


GDN_STEP_CHUNKS = 4
GDN_ILP_CHUNKS = 4
GDN_ILP_CHUNKS_BWD = 4


def _bdot(a, b, dims=NN):
    return _dot(a.astype(BF16), b.astype(BF16), dims)


def _split_bf16(x):
    hi = x.astype(BF16)
    return hi, (x - hi.astype(F32)).astype(BF16)


def _dot3(a, b, dims=NN):
    ah, al = _split_bf16(a)
    bh, bl = _split_bf16(b)
    return _dot(ah, bh, dims) + (_dot(ah, bl, dims) + _dot(al, bh, dims))


def _dot3_many(lhs, rhs, dims=NN):
    sa = [_split_bf16(a) for a in lhs]
    sb = [_split_bf16(b) for b in rhs]
    hh = [_dot(a[0], b[0], dims) for a, b in zip(sa, sb)]
    hl = [_dot(a[0], b[1], dims) for a, b in zip(sa, sb)]
    lh = [_dot(a[1], b[0], dims) for a, b in zip(sa, sb)]
    return [x + (y + z) for x, y, z in zip(hh, hl, lh)]


def _gdn_local(chains, with_inverse):
    ck = GDN_CHUNK
    ii = lax.broadcasted_iota(jnp.int32, (ck, ck), 0)
    jj = lax.broadcasted_iota(jnp.int32, (ck, ck), 1)
    lower, strict = ii >= jj, ii > jj
    dmat = [jnp.exp(jnp.where(lower, gc - gc_row, -jnp.inf)) for _, _, _, gc, gc_row in chains]
    kk = [_bdot(k, k, NT) for _, k, _, _, _ in chains]
    qk = [_bdot(q, k, NT) for q, k, _, _, _ in chains]
    tinv = [None] * len(chains)
    if with_inverse:
        lmat = [jnp.where(strict, c[2] * kk_i * d_i, 0.0) for c, kk_i, d_i in zip(chains, kk, dmat)]
        eye = jnp.where(ii == jj, 1.0, 0.0)
        tinv = [eye - l_i for l_i in lmat]
        pw = lmat
        for _ in range(int(math.log2(ck)) - 1):
            pw = _dot3_many(pw, pw)
            tinv = [t_i + d_i for t_i, d_i in zip(tinv, _dot3_many(tinv, pw))]
    out = []
    for (q, k, b, gc, gc_row), dmat_i, kk_i, qk_i, tinv_i in zip(chains, dmat, kk, qk, tinv):
        gl = gc[ck - 1:ck, :]
        out.append(dict(lower=lower, strict=strict, dmat=dmat_i, kk=kk_i, tinv=tinv_i, gam=jnp.exp(gc), qk=qk_i,
                        mm=qk_i * dmat_i, kdec=jnp.exp(gl - gc)))
    return out


def _gdn_head_cols(h):
    return slice(h * GDN_HEAD_DIM, (h + 1) * GDN_HEAD_DIM)


def _gdn_chunk_inputs(x_ref, aux_ref, auxt_ref, g, h):
    nh, ck = GDN_HEADS, GDN_CHUNK
    gw = nh * GDN_HEAD_DIM
    rows = slice(g * ck, (g + 1) * ck)
    cols = _gdn_head_cols(h)
    q = x_ref[rows, cols]
    k = x_ref[rows, gw + cols.start:gw + cols.stop]
    v = x_ref[rows, 2 * gw + cols.start:2 * gw + cols.stop]
    b = aux_ref[rows, nh + h:nh + h + 1]
    gc = aux_ref[rows, 2 * nh + h:2 * nh + h + 1]
    gc_row = auxt_ref[g, 2 * nh + h:2 * nh + h + 1, :]
    return q, k, v, b, gc, gc_row


def _gdn_specs(t, widths, *, reverse=False, step_chunks=None):
    rows = (step_chunks or GDN_STEP_CHUNKS) * GDN_CHUNK
    nsteps = t // rows
    idx = (lambda i: (nsteps - 1 - i, 0)) if reverse else (lambda i: (i, 0))
    return [pl.BlockSpec((rows, w), idx) for w in widths]


def _gdn_local_fwd(qkvn, aux, aux_t, *, name, exchange=None):
    t = qkvn.shape[0]
    hd, nh, ck, gs = GDN_HEAD_DIM, GDN_HEADS, GDN_CHUNK, GDN_STEP_CHUNKS
    gw = nh * hd
    host = _ExchangeHost(exchange)
    grid = (t // (gs * ck),)

    def body(*refs):
        x_ref, aux_ref, auxt_ref = refs[:3]
        u_ref, w_ref, qd_ref, kd_ref, mm_ref, tinv_ref = refs[3 + host.n_in:9 + host.n_in]
        host.start(refs, 3, 9 + host.n_in, grid)
        for g0 in range(0, gs, GDN_ILP_CHUNKS):
            where = [(g, h) for g in range(g0, g0 + GDN_ILP_CHUNKS) for h in range(nh)]
            ins = [_gdn_chunk_inputs(x_ref, aux_ref, auxt_ref, g, h) for g, h in where]
            lcs = _gdn_local([(q, k, b, gc, gc_row) for q, k, _, b, gc, gc_row in ins], True)
            tinvs = [lc["tinv"] for lc in lcs]
            us = _dot3_many(tinvs, [b * v for _, _, v, b, _, _ in ins])
            ws = _dot3_many(tinvs, [(b * lc["gam"]) * k for (_, k, _, b, _, _), lc in zip(ins, lcs)])
            for i, ((g, h), (q, k, _, _, _, _), lc) in enumerate(zip(where, ins, lcs)):
                rows, cols = slice(g * ck, (g + 1) * ck), _gdn_head_cols(h)
                u_ref[rows, cols] = us[i]
                w_ref[rows, cols] = ws[i].astype(w_ref.dtype)
                qd_ref[rows, cols] = (lc["gam"] * q).astype(qd_ref.dtype)
                kd_ref[rows, cols] = (lc["kdec"] * k).astype(kd_ref.dtype)
            for g in range(g0, g0 + GDN_ILP_CHUNKS):
                rows = slice(g * ck, (g + 1) * ck)
                mine = [lc for (gg, _), lc in zip(where, lcs) if gg == g]
                mm_ref[rows, :] = jnp.concatenate([lc["mm"] for lc in mine], axis=1).astype(mm_ref.dtype)
                tinv_ref[rows, :] = jnp.concatenate([lc["tinv"] for lc in mine], axis=1)
        host.wait(refs, 3, 9 + host.n_in, grid)

    sq = nh * ck
    outs = pl.pallas_call(
        body, grid=grid,
        in_specs=_gdn_specs(t, (3 * gw, LANES)) + [pl.BlockSpec((gs, 16, ck), lambda i: (i, 0, 0))] + host.in_specs,
        out_specs=_gdn_specs(t, (gw, gw, gw, gw, sq, sq)) + host.out_specs,
        out_shape=[jax.ShapeDtypeStruct((t, gw), F32)] + [jax.ShapeDtypeStruct((t, gw), BF16)] * 3
        + [jax.ShapeDtypeStruct((t, sq), BF16), jax.ShapeDtypeStruct((t, sq), F32)] + host.out_shapes,
        scratch_shapes=host.scratch,
        compiler_params=_cp(host.semantics(("parallel",))), name=name,
    )(qkvn, aux, aux_t, *host.ins)
    return (*outs[:6], list(outs[6:])) if exchange is not None else outs


def _gdn_seq_fwd(u, w, qd, kd, mm, aux, *, name):
    t = u.shape[0]
    hd, nh, ck, gs = GDN_HEAD_DIM, GDN_HEADS, GDN_CHUNK, GDN_STEP_CHUNKS
    gw = nh * hd
    sq = nh * ck

    def body(u_ref, w_ref, qd_ref, kd_ref, mm_ref, aux_ref, o_ref, vn_ref, sall_ref, s_ref):
        @pl.when(pl.program_id(0) == 0)
        def _():
            s_ref[...] = jnp.zeros_like(s_ref)

        for g in range(gs):
            rows = slice(g * ck, (g + 1) * ck)
            last = (g + 1) * ck - 1
            for h in range(nh):
                cols = _gdn_head_cols(h)
                st = s_ref[h]
                sall_ref[g, h] = st
                stb = st.astype(BF16)
                vn = u_ref[rows, cols] - _dot(w_ref[rows, cols], stb)
                vnb = vn.astype(BF16)
                vn_ref[rows, cols] = vnb
                o_ref[rows, cols] = _dot(qd_ref[rows, cols], stb) + _dot(mm_ref[rows, h * ck:(h + 1) * ck], vnb)
                gam_c = jnp.exp(aux_ref[last:last + 1, 2 * nh + h:2 * nh + h + 1])
                s_ref[h] = gam_c * st + _dot(kd_ref[rows, cols], vnb, TN)

    return pl.pallas_call(
        body, grid=(t // (gs * ck),),
        in_specs=_gdn_specs(t, (gw, gw, gw, gw, sq, LANES)),
        out_specs=_gdn_specs(t, (gw, gw)) + [pl.BlockSpec((gs, nh, hd, hd), lambda i: (i, 0, 0, 0))],
        out_shape=[jax.ShapeDtypeStruct((t, gw), F32), jax.ShapeDtypeStruct((t, gw), BF16),
                   jax.ShapeDtypeStruct((t // ck, nh, hd, hd), F32)],
        scratch_shapes=[pltpu.VMEM((nh, hd, hd), F32)],
        compiler_params=_cp(("arbitrary",)), name=name,
    )(u, w, qd, kd, mm, aux)


def _gdn_seq_bwd(do, w, qd, kd, mm, vn, s_all, aux, *, name):
    t = do.shape[0]
    hd, nh, ck, gs = GDN_HEAD_DIM, GDN_HEADS, GDN_CHUNK, GDN_STEP_CHUNKS
    gw = nh * hd
    sq = nh * ck
    nsteps = t // (gs * ck)

    def body(do_ref, w_ref, qd_ref, kd_ref, mm_ref, vn_ref, sall_ref, aux_ref, dvn_ref, dqd_ref, dkd_ref, dw_ref,
             dlast_ref, ds_ref):
        @pl.when(pl.program_id(0) == 0)
        def _():
            ds_ref[...] = jnp.zeros_like(ds_ref)

        lane = lax.broadcasted_iota(jnp.int32, (ck, LANES), 1)
        rowi = lax.broadcasted_iota(jnp.int32, (ck, LANES), 0)
        for g in reversed(range(gs)):
            rows = slice(g * ck, (g + 1) * ck)
            last = (g + 1) * ck - 1
            dlast = jnp.zeros((ck, LANES), F32)
            for h in range(nh):
                cols = _gdn_head_cols(h)
                st = sall_ref[g, h]
                stb = st.astype(BF16)
                dsn = ds_ref[h]
                dsb = dsn.astype(BF16)
                dob = do_ref[rows, cols].astype(BF16)
                dvn = _dot(mm_ref[rows, h * ck:(h + 1) * ck], dob, TN) + _dot(kd_ref[rows, cols], dsb)
                dvb = dvn.astype(BF16)
                dvn_ref[rows, cols] = dvn
                dqd_ref[rows, cols] = _dot(dob, stb, NT)
                dkd_ref[rows, cols] = _dot(vn_ref[rows, cols], dsb, NT)
                dw_ref[rows, cols] = -_dot(dvb, stb, NT)
                gam_c = jnp.exp(aux_ref[last:last + 1, 2 * nh + h:2 * nh + h + 1])
                dgam_c = jnp.sum(jnp.sum(dsn * st, axis=1, keepdims=True), axis=0, keepdims=True)
                dlast = dlast + jnp.where((rowi == ck - 1) & (lane == h), gam_c * dgam_c, 0.0)
                ds_ref[h] = _dot(qd_ref[rows, cols], dob, TN) + gam_c * dsn - _dot(w_ref[rows, cols], dvb, TN)
            dlast_ref[rows, :] = dlast

    return pl.pallas_call(
        body, grid=(nsteps,),
        in_specs=_gdn_specs(t, (gw, gw, gw, gw, sq, gw), reverse=True)
        + [pl.BlockSpec((gs, nh, hd, hd), lambda i: (nsteps - 1 - i, 0, 0, 0))] + _gdn_specs(t, (LANES,), reverse=True),
        out_specs=_gdn_specs(t, (gw, gw, gw, gw, LANES), reverse=True),
        out_shape=[jax.ShapeDtypeStruct((t, gw), F32)] * 4 + [jax.ShapeDtypeStruct((t, LANES), F32)],
        scratch_shapes=[pltpu.VMEM((nh, hd, hd), F32)],
        compiler_params=_cp(("arbitrary",)), name=name,
    )(do, w, qd, kd, mm, vn, s_all, aux)


def _gdn_local_bwd(qkvn, aux, aux_t, tinv, u, w, vn, do, dvn, dqd, dkd, dw, dlast, *, name):
    t = qkvn.shape[0]
    hd, nh, ck, gs = GDN_HEAD_DIM, GDN_HEADS, GDN_CHUNK, GDN_STEP_CHUNKS
    gw = nh * hd
    sq = nh * ck

    def body(x_ref, aux_ref, auxt_ref, tinv_ref, u_ref, w_ref, vn_ref, do_ref, dvn_ref, dqd_ref, dkd_ref, dw_ref,
             dlast_ref, dx_ref, daux_ref):
        lane = lax.broadcasted_iota(jnp.int32, (ck, LANES), 1)
        ones = jnp.ones((ck, LANES), F32)
        ii = lax.broadcasted_iota(jnp.int32, (ck, ck), 0)
        jj = lax.broadcasted_iota(jnp.int32, (ck, ck), 1)
        suffix = jnp.where(jj >= ii, 1.0, 0.0)
        for g0 in range(0, gs, GDN_ILP_CHUNKS_BWD):
            where = [(g, h) for g in range(g0, g0 + GDN_ILP_CHUNKS_BWD) for h in range(nh)]
            at = [(slice(g * ck, (g + 1) * ck), _gdn_head_cols(h)) for g, h in where]
            ins = [_gdn_chunk_inputs(x_ref, aux_ref, auxt_ref, g, h) for g, h in where]
            lcs = _gdn_local([(q, k, b, gc, gc_row) for q, k, _, b, gc, gc_row in ins], False)
            tinvs = [tinv_ref[slice(g * ck, (g + 1) * ck), h * ck:(h + 1) * ck] for g, h in where]
            dms = [jnp.where(lc["lower"], _bdot(do_ref[r, c], vn_ref[r, c], NT), 0.0) for lc, (r, c) in zip(lcs, at)]
            drvs = _dot3_many(tinvs, [dvn_ref[r, c] for r, c in at], TN)
            drks = _dot3_many(tinvs, [dw_ref[r, c] for r, c in at], TN)
            das = [jnp.where(lc["strict"], -(_bdot(drv, u_ref[r, c], NT) + _bdot(drk, w_ref[r, c], NT)), 0.0)
                   for lc, (r, c), drv, drk in zip(lcs, at, drvs, drks)]
            f_mats = [da * (i[3] * lc["kk"]) * lc["dmat"] + dm * lc["qk"] * lc["dmat"]
                      for i, lc, da, dm in zip(ins, lcs, das, dms)]
            col_sums = _dot3_many(f_mats, [ones] * len(where), TN)
            dgc_all = {g: dlast_ref[slice(g * ck, (g + 1) * ck), :] for g in range(g0, g0 + GDN_ILP_CHUNKS_BWD)}
            db_all = {g: jnp.zeros((ck, LANES), F32) for g in range(g0, g0 + GDN_ILP_CHUNKS_BWD)}
            for n, ((g, h), (q, k, v, b, _, _), lc, (rows, cols)) in enumerate(zip(where, ins, lcs, at)):
                dmat, kk, gam, kdec = (lc[key] for key in ("dmat", "kk", "gam", "kdec"))
                drv, drk, da, dm = drvs[n], drks[n], das[n], dms[n]
                dqd_h, dkd_h = dqd_ref[rows, cols], dkd_ref[rows, cols]
                rs_rk = jnp.sum(drk * k, axis=-1, keepdims=True)
                db = (jnp.sum(drv * v, axis=-1, keepdims=True) + gam * rs_rk
                      + jnp.sum(da * kk * dmat, axis=-1, keepdims=True))
                e_mat = da * dmat * b
                dmd = dm * dmat
                dx_ref[rows, cols] = _bdot(dmd, k) + gam * dqd_h
                dx_ref[rows, gw + cols.start:gw + cols.stop] = (
                    (b * gam) * drk + _bdot(e_mat, k) + _bdot(e_mat, k, TN) + _bdot(dmd, q, TN) + kdec * dkd_h)
                dx_ref[rows, 2 * gw + cols.start:2 * gw + cols.stop] = b * drv
                e_vec = jnp.sum(dkd_h * (kdec * k), axis=-1, keepdims=True)
                dgc = (b * gam * rs_rk + gam * jnp.sum(dqd_h * q, axis=-1, keepdims=True)
                       + jnp.sum(f_mats[n], axis=-1, keepdims=True) - col_sums[n][:, 0:1] - e_vec)
                is_last = lax.broadcasted_iota(jnp.int32, (ck, 1), 0) == ck - 1
                dgc = dgc + jnp.where(is_last, jnp.sum(e_vec, axis=0, keepdims=True), 0.0)
                dgc_all[g] = dgc_all[g] + jnp.where(lane == h, dgc, 0.0)
                db_all[g] = db_all[g] + jnp.where(lane == nh + h, db, 0.0)
            for g in dgc_all:
                daux_ref[slice(g * ck, (g + 1) * ck), :] = _dot3(suffix, dgc_all[g]) + db_all[g]

    return pl.pallas_call(
        body, grid=(t // (gs * ck),),
        in_specs=_gdn_specs(t, (3 * gw, LANES)) + [pl.BlockSpec((gs, 16, ck), lambda i: (i, 0, 0))]
        + _gdn_specs(t, (sq, gw, gw, gw, gw, gw, gw, gw, gw, LANES)),
        out_specs=_gdn_specs(t, (3 * gw, LANES)),
        out_shape=[jax.ShapeDtypeStruct((t, 3 * gw), F32), jax.ShapeDtypeStruct((t, LANES), F32)],
        compiler_params=_cp(("parallel",)), name=name,
    )(qkvn, aux, aux_t, tinv, u, w, vn, do, dvn, dqd, dkd, dw, dlast)


def _gdn_pre_bwd1(xc, dqkvn, daux, ab, alog_pad, dt_pad, dkv, dp, dp_col, *, name, ab_blk=0, tm=256):
    t, cw = xc.shape
    hd, nh = GDN_HEAD_DIM, GDN_HEADS
    gw = nh * hd
    tm = _blk(t, tm)

    kvw = dkv.shape[1]
    seg = kvw + AB_PAD
    assert dp_col % seg == 0

    def body(xc_ref, dy_ref, daux_ref, ab_ref, al_ref, dt_ref, dkv_ref, _, dxc_ref, dab_ref, dal_ref, ddt_ref):
        i = pl.program_id(0)
        xc = xc_ref[...]
        sg = _sigmoid(xc)
        s = xc * sg
        dsilu = sg * (1.0 + xc * (1.0 - sg))
        for h in range(2 * nh):
            xh = s[:, h * hd:(h + 1) * hd]
            scale = hd ** -0.5 if h < nh else 1.0
            dyh = dy_ref[:, h * hd:(h + 1) * hd] * scale
            r = lax.rsqrt(jnp.sum(xh * xh, axis=-1, keepdims=True) + L2_EPS)
            dxh = r * dyh - xh * (r * r * r) * jnp.sum(dyh * xh, axis=-1, keepdims=True)
            dxc_ref[:, h * hd:(h + 1) * hd] = dxh * dsilu[:, h * hd:(h + 1) * hd]
        dxc_ref[:, 2 * gw:] = dy_ref[:, 2 * gw:] * dsilu[:, 2 * gw:]
        abv = ab_ref[...]
        dauxv = daux_ref[...]
        lane = lax.broadcasted_iota(jnp.int32, abv.shape, 1)
        is_a = lane < nh
        is_b = (lane >= nh) & (lane < 2 * nh)
        pre = abv + dt_ref[...]
        neg_ea = -jnp.exp(al_ref[...])
        d_a = jnp.where(is_a, dauxv * neg_ea * _sigmoid(pre), 0.0)
        beta = _sigmoid(abv)
        d_b = jnp.where(is_b, dauxv * beta * (1.0 - beta), 0.0)
        dab_ref[:, :kvw] = dkv_ref[...]
        dab_ref[:, kvw:kvw + LANES] = (d_a + d_b).astype(dab_ref.dtype)
        dab_ref[:, kvw + LANES:] = jnp.zeros((tm, AB_PAD - LANES), dab_ref.dtype)
        dal = jnp.sum(jnp.where(is_a, dauxv * neg_ea * _softplus(pre), 0.0), axis=0, keepdims=True)
        ddt = jnp.sum(d_a, axis=0, keepdims=True)

        @pl.when(i == 0)
        def _():
            dal_ref[...] = dal
            ddt_ref[...] = ddt

        @pl.when(i > 0)
        def _():
            dal_ref[...] += dal
            ddt_ref[...] += ddt

    row = lambda c: pl.BlockSpec((tm, c), lambda i: (i, 0))
    vec = pl.BlockSpec((1, LANES), lambda i: (0, 0))
    return pl.pallas_call(
        body, grid=(t // tm,),
        in_specs=[row(cw), row(cw), row(LANES), pl.BlockSpec((tm, LANES), lambda i: (i, ab_blk)), vec, vec, row(kvw),
                  ANY_SPEC],
        out_specs=[row(cw), pl.BlockSpec((tm, seg), lambda i: (i, dp_col // seg)), vec, vec],
        out_shape=[jax.ShapeDtypeStruct((t, cw), F32), jax.ShapeDtypeStruct(dp.shape, dp.dtype),
                   jax.ShapeDtypeStruct((1, LANES), F32), jax.ShapeDtypeStruct((1, LANES), F32)],
        input_output_aliases={7: 1}, compiler_params=_cp(("arbitrary",)), name=name,
    )(xc, dqkvn, daux, ab, alog_pad, dt_pad, dkv, dp)


def _gdn_pre_bwd2(dxc, qkvb, conv_w, dp, dp_col, *, name, tm=512):
    t, cw = dxc.shape
    tm = _blk(t, tm)
    hb = tm // HALO
    nblk = t // tm
    cg = GDN_HEADS * GDN_HEAD_DIM
    assert cw % cg == 0 and dp_col % cg == 0
    col0 = dp_col // cg

    def body(d_ref, dn_ref, x_ref, xp_ref, w_ref, _, dx_ref, dw_ref):
        i = pl.program_id(1)
        dcur = d_ref[...]
        dnxt = jnp.where(i < nblk - 1, dn_ref[...], 0.0)
        cur = x_ref[...]
        prev = jnp.where(i > 0, xp_ref[...], 0.0)
        dx = None
        dws = []
        for tap in range(GDN_CONV):
            j = GDN_CONV - 1 - tap
            term = w_ref[tap:tap + 1, :] * _shift_up(dcur, dnxt, j)
            dx = term if dx is None else dx + term
            dws.append(jnp.sum(dcur * _shift_down(cur, prev, j), axis=0, keepdims=True))
        dx_ref[...] = dx.astype(dx_ref.dtype)
        dw = jnp.concatenate(dws, axis=0)

        @pl.when(i == 0)
        def _():
            dw_ref[...] = dw

        @pl.when(i > 0)
        def _():
            dw_ref[...] += dw

    row = pl.BlockSpec((tm, cg), lambda c, i: (i, c))
    wsp = pl.BlockSpec((GDN_CONV, cg), lambda c, i: (0, c))
    return pl.pallas_call(
        body, grid=(cw // cg, nblk),
        in_specs=[row, pl.BlockSpec((HALO, cg), lambda c, i: (jnp.minimum((i + 1) * hb, t // HALO - 1), c)),
                  row, pl.BlockSpec((HALO, cg), lambda c, i: (jnp.maximum(i * hb - 1, 0), c)), wsp, ANY_SPEC],
        out_specs=[pl.BlockSpec((tm, cg), lambda c, i: (i, col0 + c)), wsp],
        out_shape=[jax.ShapeDtypeStruct(dp.shape, dp.dtype), jax.ShapeDtypeStruct((GDN_CONV, cw), F32)],
        input_output_aliases={5: 0}, compiler_params=_cp(("arbitrary", "arbitrary")), name=name,
    )(dxc, dxc, qkvb, qkvb, conv_w, dp)


def _gdn_post_fwd(o, z, norm_w, *, name, tm=512):
    t, gw = o.shape
    hd, nh = GDN_HEAD_DIM, GDN_HEADS
    tm = _blk(t, tm)

    def body(o_ref, z_ref, w_ref, y_ref):
        zv = z_ref[...]
        sz = zv * _sigmoid(zv)
        for h in range(nh):
            oh = o_ref[:, h * hd:(h + 1) * hd]
            r = lax.rsqrt(jnp.mean(oh * oh, axis=-1, keepdims=True) + RMS_EPS)
            y_ref[:, h * hd:(h + 1) * hd] = (oh * r * w_ref[...] * sz[:, h * hd:(h + 1) * hd]).astype(y_ref.dtype)

    row = pl.BlockSpec((tm, gw), lambda i: (i, 0))
    return pl.pallas_call(
        body, grid=(t // tm,), in_specs=[row, row, pl.BlockSpec((1, hd), lambda i: (0, 0))], out_specs=row,
        out_shape=jax.ShapeDtypeStruct((t, gw), BF16), compiler_params=_cp(("parallel",)), name=name,
    )(o, z, norm_w)


def _gdn_post_bwd(dy, o, z, norm_w, dp, dp_col, *, name, tm=512):
    t, gw = o.shape
    hd, nh = GDN_HEAD_DIM, GDN_HEADS
    tm = _blk(t, tm)

    def body(dy_ref, o_ref, z_ref, w_ref, _, do_ref, dz_ref, dw_ref):
        i = pl.program_id(0)
        zv = z_ref[...]
        sg = _sigmoid(zv)
        sz = zv * sg
        dsz = sg * (1.0 + zv * (1.0 - sg))
        dw = None
        for h in range(nh):
            sl = slice(h * hd, (h + 1) * hd)
            oh = o_ref[:, sl]
            dyh = dy_ref[:, sl].astype(F32)
            r = lax.rsqrt(jnp.mean(oh * oh, axis=-1, keepdims=True) + RMS_EPS)
            xh = oh * r
            dz_ref[:, sl] = (dyh * xh * w_ref[...] * dsz[:, sl]).astype(dz_ref.dtype)
            dn = dyh * sz[:, sl]
            dxh = dn * w_ref[...]
            do_ref[:, sl] = r * (dxh - xh * jnp.mean(dxh * xh, axis=-1, keepdims=True))
            part = jnp.sum(dn * xh, axis=0, keepdims=True)
            dw = part if dw is None else dw + part

        @pl.when(i == 0)
        def _():
            dw_ref[...] = dw

        @pl.when(i > 0)
        def _():
            dw_ref[...] += dw

    row = pl.BlockSpec((tm, gw), lambda i: (i, 0))
    vec = pl.BlockSpec((1, hd), lambda i: (0, 0))
    return pl.pallas_call(
        body, grid=(t // tm,), in_specs=[row, row, row, vec, ANY_SPEC],
        out_specs=[row, pl.BlockSpec((tm, gw), lambda i: (i, dp_col // gw)), vec],
        out_shape=[jax.ShapeDtypeStruct((t, gw), F32), jax.ShapeDtypeStruct(dp.shape, dp.dtype),
                   jax.ShapeDtypeStruct((1, hd), F32)],
        input_output_aliases={4: 1}, compiler_params=_cp(("arbitrary",)), name=name,
    )(dy, o, z, norm_w, dp)


IN_NAMES = ("q_a", "kv_a", "qkv_b", "ab", "z", "q_c", "gates")
CAT_NAMES = ("gates", "q_a", "qkv_b", "z", "q_c", "kv_a", "ab")
AB_PAD = 256


def _in_widths(d):
    gw = GDN_HEADS * GDN_HEAD_DIM
    return dict(q_a=SWA_Q_HEADS * SWA_HEAD_DIM, kv_a=2 * SWA_KV_HEADS * SWA_HEAD_DIM, qkv_b=3 * gw, ab=2 * GDN_HEADS,
                z=gw, q_c=XA_HEADS * XA_HEAD_DIM, gates=3 * d)


def _ranges(names, widths):
    out, start = {}, 0
    for k in names:
        out[k] = (start, widths[k])
        start += widths[k]
    return out, start


def _cat_ranges(d):
    widths = dict(_in_widths(d), ab=AB_PAD)
    return _ranges(CAT_NAMES, widths)


def _to_cat(shards):
    ns, d, n = shards.shape
    src, _ = _ranges(IN_NAMES, _in_widths(d))
    cols = []
    for k in CAT_NAMES:
        lo, hi = src[k][0], src[k][0] + src[k][1]
        for s in range(ns):
            a, b = max(lo, s * n), min(hi, (s + 1) * n)
            if a < b:
                cols.append(shards[s][:, a - s * n:b - s * n])
    cols.append(jnp.zeros((d, AB_PAD - src["ab"][1]), shards.dtype))
    return jnp.concatenate(cols, axis=1)


def _from_cat(w_cat):
    d = w_cat.shape[0]
    src, total = _ranges(IN_NAMES, _in_widths(d))
    cat, _ = _cat_ranges(d)
    n = total // N_SHARDS
    shards = []
    for s in range(N_SHARDS):
        pieces = []
        for k in IN_NAMES:
            a, b = max(s * n, src[k][0]), min((s + 1) * n, src[k][0] + src[k][1])
            if a < b:
                pieces.append(w_cat[:, cat[k][0] + a - src[k][0]:cat[k][0] + b - src[k][0]])
        shards.append(jnp.concatenate(pieces, axis=1))
    return jnp.stack(shards)


def _pad_cols(a, width):
    return jnp.pad(a, ((0, 0), (0, width - a.shape[1])))


def _relu2_epilogue(acc):
    r = jnp.maximum(acc, 0.0)
    return acc, r * r


def _add_epilogue(acc, res):
    return (acc + res,)


def _drelu2_epilogue(acc, u):
    return (acc * (2.0 * jnp.maximum(u.astype(F32), 0.0)),)


def _local_step(x, mem, tgt, wts, small, comm=None):
    t, d = x.shape
    nh = GDN_HEADS
    w_cat = wts["w_cat"]
    cat, cat_w = _cat_ranges(d)
    assert w_cat.shape == (d, cat_w)
    alog_pad = _pad_cols(small["a_log"], LANES)
    dt_pad = _pad_cols(small["dt_bias"], LANES)
    kvw = cat["kv_a"][1]
    assert cat["ab"][0] == cat["kv_a"][0] + kvw
    ab_blk = kvw // LANES

    n = _rms_fwd(x, small["g_mix"], name="rms_mix")
    q_a = _mm(n, w_cat, b_window=cat["q_a"], out_dtypes=(BF16,), name="in_q_a")
    kv_a, ab = _mm(n, w_cat, b_window=(cat["kv_a"][0], kvw + AB_PAD), out_dtypes=(BF16, F32), name="in_kv_ab")
    qkvb = _mm(n, w_cat, b_window=cat["qkv_b"], tn=512, name="in_qkv_b")
    z = _mm(n, w_cat, b_window=cat["z"], name="in_z")
    q_c = _mm(n, w_cat, b_window=cat["q_c"], out_dtypes=(BF16,), name="in_q_c")
    gates = _mm(n, w_cat, b_window=cat["gates"], name="in_gates")
    if comm is None:
        y_a, lse = _swa_fwd(q_a, kv_a, small["sinks"], name="swa_fwd")
    else:
        y_a, lse, landed_mlp = _swa_fwd(q_a, kv_a, small["sinks"], name="swa_fwd", exchange=comm.gather_exchange(comm.MLP))
    xc, qkvn, aux = _gdn_pre_fwd(qkvb, small["conv_w"], ab, alog_pad, dt_pad, ab_blk=ab_blk, name="gdn_pre_fwd")
    aux_t = aux[:, :16].reshape(t // GDN_CHUNK, GDN_CHUNK, 16).transpose(0, 2, 1)
    if comm is None:
        gdn_u, gdn_w, gdn_qd, gdn_kd, gdn_mm, gdn_tinv = _gdn_local_fwd(qkvn, aux, aux_t, name="gdn_local_fwd")
    else:
        gdn_u, gdn_w, gdn_qd, gdn_kd, gdn_mm, gdn_tinv, landed_mid = _gdn_local_fwd(
            qkvn, aux, aux_t, name="gdn_local_fwd", exchange=comm.gather_exchange(comm.mid))
        wts = dict(wts, **comm.gathered(comm.mid, landed_mid, "mid"), **comm.gathered(comm.MLP, landed_mlp, "mlp"))
    o_b, gdn_vn, s_all = _gdn_seq_fwd(gdn_u, gdn_w, gdn_qd, gdn_kd, gdn_mm, aux, name="gdn_seq_fwd")
    y_b = _gdn_post_fwd(o_b, z, small["gdn_norm_w"], name="gdn_post_fwd")
    nmem = _rms_fwd(mem, small["g_mem"], name="rms_mem")
    mkv = _mm(nmem, wts["w_mem_kv"], out_dtypes=(BF16,), name="mem_kv")
    y_c = _xa_fwd(q_c, mkv, name="xa_fwd")
    ys = (y_a, y_b, y_c)
    w_ups = (wts["w_swa_up"], wts["w_gdn_up"], wts["w_xa_up"])
    merged = _merge_fwd(ys, w_ups, gates, name="merge_fwd")
    h1 = _mm(merged, wts["w_out"], extras=(x,), epilogue=_add_epilogue, name="out_proj")
    n2 = _rms_fwd(h1, small["g_mlp"], name="rms_mlp")
    u, act = _mm(n2, wts["w_mlp_in"], b_sharded=True, out_dtypes=(BF16, BF16), epilogue=_relu2_epilogue, name="mlp_in")
    h2 = _mm(act, wts["w_mlp_out"], extras=(h1,), epilogue=_add_epilogue, name="mlp_out")
    dh2, dh2_b, dg_final, loss = _final_loss(h2, small["g_final"], tgt, name="final_loss")

    grads = {"g_final": dg_final}
    du = _mm(dh2_b, wts["w_mlp_out"], tb=True, out_dtypes=(BF16,), extras=(u,), epilogue=_drelu2_epilogue, name="d_mlp_act")
    grads["w_mlp_out"] = _mm(act, dh2_b, ta=True, out_dtypes=(BF16,), name="dw_mlp_out")
    grads["w_mlp_in"] = _mm(n2, du, ta=True, out_sharded=True, out_dtypes=(BF16,), name="dw_mlp_in")
    dn2 = _mm(du, wts["w_mlp_in"], tb=True, b_sharded=True, name="d_mlp_in")
    dh1, dh1_b, grads["g_mlp"] = _rms_bwd(dn2, h1, small["g_mlp"], dh2, name="rms_mlp_bwd")
    dmerged = _mm(dh1_b, wts["w_out"], tb=True, name="d_out_proj")
    grads["w_out"] = _mm(merged, dh1_b, ta=True, out_dtypes=(BF16,), name="dw_out")
    *dus, dp = _merge_bwd(ys, w_ups, gates, dmerged, cat_w, name="merge_bwd")
    dys = []
    for y, du_i, w_up, key in zip(ys, dus, w_ups, ("w_swa_up", "w_gdn_up", "w_xa_up")):
        dys.append(_mm(du_i, w_up, tb=True, b_sharded=True, out_dtypes=(BF16,), name="d_" + key))
        grads[key] = _mm(y, du_i, ta=True, out_sharded=True, out_dtypes=(BF16,), name="dw_" + key[2:])
    dp, dkv_a, grads["sinks"] = _swa_bwd(q_a, kv_a, small["sinks"], y_a, lse, dys[0], dp, cat["q_a"][0], name="swa_bwd")
    do_b, dp, grads["gdn_norm_w"] = _gdn_post_bwd(dys[1], o_b, z, small["gdn_norm_w"], dp, cat["z"][0],
                                                  name="gdn_post_bwd")
    dvn, dqd, dkd, dw_, dlast = _gdn_seq_bwd(do_b, gdn_w, gdn_qd, gdn_kd, gdn_mm, gdn_vn, s_all, aux, name="gdn_seq_bwd")
    dqkvn, daux = _gdn_local_bwd(qkvn, aux, aux_t, gdn_tinv, gdn_u, gdn_w, gdn_vn, do_b, dvn, dqd, dkd, dw_, dlast,
                                 name="gdn_local_bwd")
    dxc, dp, dalog, ddt = _gdn_pre_bwd1(xc, dqkvn, daux, ab, alog_pad, dt_pad, dkv_a, dp, cat["kv_a"][0], ab_blk=ab_blk,
                                        name="gdn_pre_bwd1")
    grads["a_log"], grads["dt_bias"] = dalog[:, :nh], ddt[:, :nh]
    dp, grads["conv_w"] = _gdn_pre_bwd2(dxc, qkvb, small["conv_w"], dp, cat["qkv_b"][0], name="gdn_pre_bwd2")
    dp, dmkv = _xa_bwd(q_c, mkv, dys[2], dp, cat["q_c"][0], name="xa_bwd")
    grads["w_mem_kv"] = _mm(nmem, dmkv, ta=True, out_dtypes=(BF16,), name="dw_mem_kv")
    dnmem = _mm(dmkv, wts["w_mem_kv"], tb=True, name="d_mem_kv")
    _, _, grads["g_mem"] = _rms_bwd(dnmem, mem, small["g_mem"], jnp.zeros_like(mem), name="rms_mem_bwd")
    if comm is None:
        grads["w_cat"] = _mm(n, dp, ta=True, out_dtypes=(BF16,), name="dw_in")
        dn = _mm(dp, w_cat, tb=True, name="d_in_proj")
    else:
        s1_mlp = comm.pair_sums([comm.shard_major(k, grads.pop(k)) for k in comm.MLP], "mlp")
        s1_mid = comm.pair_sums([comm.shard_major(k, grads.pop(k)) for k in comm.mid], "mid")
        dw_cat, rcv_mlp = _mm(n, dp, ta=True, out_dtypes=(BF16,), name="dw_in", exchange=_chip_exchange(s1_mlp))
        s1_in = comm.pair_sums([_from_cat(dw_cat)], "in")
        dn, rcv_rest = _mm(dp, w_cat, tb=True, name="d_in_proj", exchange=_chip_exchange(s1_in + s1_mid))
        reduced = comm.finish(s1_in + s1_mid + s1_mlp, rcv_rest + rcv_mlp)
        grads.update(zip(["w_in"] + comm.mid + list(comm.MLP), reduced))
    dx, _, grads["g_mix"] = _rms_bwd(dn, x, small["g_mix"], dh1, name="rms_mix_bwd")
    return loss, dx, grads


HBM_SPEC = pl.BlockSpec(memory_space=pltpu.HBM)
VMEM_SPEC = pl.BlockSpec(memory_space=pltpu.VMEM)
N_CHIPS = N_SHARDS
N_DEV = 8
DMA_CHUNK_BYTES = 1 << 20


def _place():
    return lax.axis_index("x"), lax.axis_index("y"), lax.axis_index("c")


def _other_chips(x, y):
    return [(1 - x, y), (x, 1 - y), (1 - x, 1 - y)]


def _n_chunks(rows, row_bytes):
    n = 1
    while rows % (2 * n) == 0 and (rows // (2 * n)) % 16 == 0 and (rows // n) * row_bytes > DMA_CHUNK_BYTES:
        n *= 2
    return n


def _sem_scratch(n_remote, n_local):
    return [pltpu.SemaphoreType.DMA((max(n_remote, 1),)), pltpu.SemaphoreType.DMA((max(n_remote, 1),)),
            pltpu.SemaphoreType.DMA((max(n_local, 1),))]


def _all_gather_weights(shards, *, name):
    first = _exchange_call(_gather_over_ici(shards), name=name + "_ici")
    return _exchange_call(_gather_pass_on(first), name=name + "_pass")


def _gather_over_ici(shards):
    plan = _half_chunks(shards, 0)

    def copies_of(in_refs, out_refs, place):
        x, y, c = place
        remote, local = [], []
        for i, r0, nr in plan:
            rh = shards[i].shape[0] // 2
            mine = pl.ds(c * rh + r0, nr)
            for chip in _other_chips(x, y):
                remote.append((in_refs[i].at[mine], out_refs[i].at[2 * x + y, mine], (*chip, c)))
            for half in range(2):
                rows = pl.ds(half * rh + r0, nr)
                local.append((in_refs[i].at[rows], out_refs[i].at[2 * x + y, rows]))
        return remote, local

    shapes = tuple(jax.ShapeDtypeStruct((N_CHIPS, *s.shape), s.dtype) for s in shards)
    return Exchange(tuple(shards), shapes, 3 * len(plan), 2 * len(plan), copies_of)


def _gather_pass_on(arrived):
    plan = _half_chunks([jax.ShapeDtypeStruct(a.shape[1:], a.dtype) for a in arrived], 0)

    def copies_of(in_refs, out_refs, place):
        x, y, c = place
        remote = []
        for i, r0, nr in plan:
            mine = pl.ds(c * (arrived[i].shape[1] // 2) + r0, nr)
            for chip in _other_chips(x, y):
                rows = out_refs[i].at[2 * chip[0] + chip[1], mine]
                remote.append((rows, rows, (x, y, 1 - c)))
        return remote, []

    shapes = tuple(jax.ShapeDtypeStruct(a.shape, a.dtype) for a in arrived)
    return Exchange(tuple(arrived), shapes, 3 * len(plan), 0, copies_of, tuple((i, i) for i in range(len(arrived))))


def _exchange_call(ex, *, name):
    n_in, n_out = len(ex.ins), len(ex.out_shapes)

    def body(*refs):
        cps = _exchange_copies(ex, refs[:n_in], refs[n_in:n_in + n_out], refs[n_in + n_out:])
        for cp in cps:
            cp.start()
        for cp in cps:
            cp.wait()

    return pl.pallas_call(
        body, out_shape=list(ex.out_shapes), in_specs=[HBM_SPEC] * n_in, out_specs=[HBM_SPEC] * n_out,
        scratch_shapes=_sem_scratch(ex.n_remote, ex.n_local), input_output_aliases=dict(ex.aliases), name=name,
    )(*ex.ins)


def _half_chunks(arrs, row_axis):
    plan = []
    for i, a in enumerate(arrs):
        rh = a.shape[row_axis] // 2
        row_bytes = a.dtype.itemsize * math.prod(a.shape) // a.shape[row_axis]
        nch = _n_chunks(rh, row_bytes)
        plan += [(i, q * (rh // nch), rh // nch) for q in range(nch)]
    return plan


def _sibling_halves(gs, *, name):
    plan = _half_chunks(gs, 1)

    def copies_of(in_refs, out_refs, place):
        x, y, c = place
        out = []
        for i, r0, nr in plan:
            rh = gs[i].shape[1] // 2
            out.append((in_refs[i].at[:, pl.ds((1 - c) * rh + r0, nr), :], out_refs[i].at[:, pl.ds(r0, nr), :],
                        (x, y, 1 - c)))
        return out, []

    shapes = tuple(jax.ShapeDtypeStruct((g.shape[0], g.shape[1] // 2, g.shape[2]), g.dtype) for g in gs)
    return _exchange_call(Exchange(tuple(gs), shapes, len(plan), 0, copies_of), name=name)


def _chip_exchange(s1s):
    plan = _half_chunks([jax.ShapeDtypeStruct((2 * s.shape[1], s.shape[2]), s.dtype) for s in s1s], 0)

    def copies_of(in_refs, out_refs, place):
        x, y, c = place
        out = []
        for i, r0, nr in plan:
            for j, chip in enumerate(_other_chips(x, y)):
                out.append((in_refs[i].at[2 * chip[0] + chip[1], pl.ds(r0, nr), :], out_refs[i].at[j, pl.ds(r0, nr), :],
                            (*chip, c)))
        return out, []

    shapes = tuple(jax.ShapeDtypeStruct((3, *s.shape[1:]), s.dtype) for s in s1s)
    return Exchange(tuple(s1s), shapes, 3 * len(plan), 0, copies_of)


def _join_halves(gs, *, name):
    plan = _half_chunks(gs, 0)

    def copies_of(in_refs, out_refs, place):
        x, y, c = place
        out = []
        for i, r0, nr in plan:
            rows = out_refs[i].at[pl.ds(c * (gs[i].shape[0] // 2) + r0, nr), :]
            out.append((rows, rows, (x, y, 1 - c)))
        return out, []

    shapes = tuple(jax.ShapeDtypeStruct(g.shape, g.dtype) for g in gs)
    aliases = tuple((i, i) for i in range(len(gs)))
    return _exchange_call(Exchange(tuple(gs), shapes, len(plan), 0, copies_of, aliases), name=name)


def _row_block(rows, cols):
    tb = rows
    while tb % 32 == 0 and tb * cols * 4 > (2 << 20):
        tb //= 2
    return tb


def _pair_sum(g, sib, core, *, name):
    ns, r, c = g.shape
    rh = r // 2
    tb = _row_block(rh, c)
    nb = rh // tb

    def body(core_ref, g_ref, s_ref, o_ref):
        o_ref[...] = (g_ref[...].astype(F32) + s_ref[...].astype(F32)).astype(o_ref.dtype)

    mine = pl.BlockSpec((None, tb, c), lambda s, i, core_ref: (s, core_ref[0] * nb + i, 0))
    half = pl.BlockSpec((None, tb, c), lambda s, i, core_ref: (s, i, 0))
    return pl.pallas_call(
        body, grid_spec=pltpu.PrefetchScalarGridSpec(num_scalar_prefetch=1, grid=(ns, nb), in_specs=[mine, half],
                                                     out_specs=half),
        out_shape=jax.ShapeDtypeStruct((ns, rh, c), BF16), compiler_params=_cp(("parallel", "parallel")), name=name,
    )(core, g, sib)


def _chip_sum(s1, rcv, where, *, name):
    _, rh, c = s1.shape
    tb = _row_block(rh, c)
    nb = rh // tb

    def body(where_ref, own_ref, r0_ref, r1_ref, r2_ref, o_ref):
        acc = own_ref[...].astype(F32)
        for r in (r0_ref, r1_ref, r2_ref):
            acc = acc + r[...].astype(F32)
        o_ref[...] = acc

    own = pl.BlockSpec((None, tb, c), lambda i, w: (w[1], i, 0))
    got = [pl.BlockSpec((None, tb, c), functools.partial(lambda i, w, j: (j, i, 0), j=j)) for j in range(3)]
    return pl.pallas_call(
        body, grid_spec=pltpu.PrefetchScalarGridSpec(
            num_scalar_prefetch=1, grid=(nb,), in_specs=[own] + got,
            out_specs=pl.BlockSpec((tb, c), lambda i, w: (w[0] * nb + i, 0))),
        out_shape=jax.ShapeDtypeStruct((2 * rh, c), F32), compiler_params=_cp(("parallel",)), name=name,
    )(where, s1, rcv, rcv, rcv)


def _all_gather_small(blk, *, name):
    r = blk.shape[0]

    def body(b_ref, out_ref, send_sems, recv_sems):
        x, y, c = _place()
        me = 4 * x + 2 * y + c
        out_ref[me] = b_ref[...]
        sends = []
        for k in range(1, N_DEV):
            peer = (x ^ (k >> 2), y ^ ((k >> 1) & 1), c ^ (k & 1))
            sends.append(pltpu.make_async_remote_copy(src_ref=b_ref, dst_ref=out_ref.at[me], send_sem=send_sems.at[k - 1],
                                                      recv_sem=recv_sems.at[k - 1], device_id=peer, device_id_type=MESH))
        for cp in sends:
            cp.start()
        for k in range(1, N_DEV):
            rows = out_ref.at[me ^ k]
            pltpu.make_async_remote_copy(src_ref=rows, dst_ref=rows, send_sem=send_sems.at[k - 1],
                                         recv_sem=recv_sems.at[k - 1], device_id=(x, y, c), device_id_type=MESH).wait_recv()
        for cp in sends:
            cp.wait_send()

    return pl.pallas_call(
        body, out_shape=jax.ShapeDtypeStruct((N_DEV, r, LANES), blk.dtype), in_specs=[VMEM_SPEC], out_specs=VMEM_SPEC,
        scratch_shapes=[pltpu.SemaphoreType.DMA((N_DEV - 1,)), pltpu.SemaphoreType.DMA((N_DEV - 1,))],
        name=name,
    )(blk)


def _sum_rows(parts, out_dtype, *, name, tb=1024):
    rows = parts[0].shape[0]
    tb = _blk(rows, tb)

    def body(*refs):
        acc = refs[0][...].astype(F32)
        for r in refs[1:-1]:
            acc = acc + r[...].astype(F32)
        refs[-1][...] = acc.astype(refs[-1].dtype)

    spec = pl.BlockSpec((tb, LANES), lambda i: (i, 0))
    return pl.pallas_call(
        body, grid=(rows // tb,), in_specs=[spec] * len(parts), out_specs=spec,
        out_shape=jax.ShapeDtypeStruct((rows, LANES), out_dtype), compiler_params=_cp(("parallel",)), name=name,
    )(*parts)


BIG = (
    ("w_in", 1), ("w_mem_kv", 0), ("w_swa_up", 1), ("w_gdn_up", 1), ("w_xa_up", 1), ("w_out", 0), ("w_mlp_in", 1),
    ("w_mlp_out", 0))


class _Comm:
    MLP = ("w_mlp_in", "w_mlp_out")

    def __init__(self, late_shards, core, where):
        self.axis = dict(BIG)
        self.late_shards = late_shards
        self.mid = [k for k in late_shards if k not in self.MLP]
        self.core, self.where = core, where

    def gather_exchange(self, names):
        return _gather_over_ici([self.late_shards[k] for k in names])

    def gathered(self, names, landed, tag):
        whole = _exchange_call(_gather_pass_on(landed), name=f"ag_{tag}_pass")
        return {k: (g.reshape(-1, g.shape[2]) if self.axis[k] == 0 else g) for k, g in zip(names, whole)}

    def shard_major(self, k, grad):
        return grad.reshape(N_CHIPS, -1, grad.shape[-1]) if self.axis[k] == 0 else grad

    def pair_sums(self, gs, tag):
        sibs = _sibling_halves(gs, name=f"rs_sibling_{tag}")
        return [_pair_sum(g, s, self.core, name=f"rs_pair_sum_{tag}{i}") for i, (g, s) in enumerate(zip(gs, sibs))]

    def finish(self, s1s, rcvs):
        halves = [_chip_sum(s1, rcv, self.where, name=f"rs_chip_sum_{i}") for i, (s1, rcv) in enumerate(zip(s1s, rcvs))]
        return _join_halves(halves, name="rs_join_halves")
SMALL = ("g_mix", "sinks", "a_log", "dt_bias", "gdn_norm_w", "g_mem", "g_mlp", "g_final")


def _rows128(a, rows):
    flat = a.reshape(-1)
    return jnp.pad(flat, (0, rows * LANES - flat.shape[0])).reshape(rows, LANES)


def kernel(x, mem, g_mix, w_in, sinks, conv_w, a_log, dt_bias, gdn_norm_w, g_mem, w_mem_kv, w_swa_up, w_gdn_up, w_xa_up, w_out, g_mlp, w_mlp_in, w_mlp_out, g_final, loss_target, m_g_mix, m_w_in, m_sinks, m_conv_w, m_a_log, m_dt_bias, m_gdn_norm_w, m_g_mem, m_w_mem_kv, m_w_swa_up, m_w_gdn_up, m_w_xa_up, m_w_out, m_g_mlp, m_w_mlp_in, m_w_mlp_out, m_g_final, v_g_mix, v_w_in, v_sinks, v_conv_w, v_a_log, v_dt_bias, v_gdn_norm_w, v_g_mem, v_w_mem_kv, v_w_swa_up, v_w_gdn_up, v_w_xa_up, v_w_out, v_g_mlp, v_w_mlp_in, v_w_mlp_out, v_g_final):
    given = dict(locals())
    xi, yi, ci = _place()
    chip = 2 * xi + yi
    core = jnp.reshape(ci, (1,)).astype(jnp.int32)
    where = jnp.stack([ci, chip]).astype(jnp.int32)

    shards = {k: given[k][0].astype(BF16) for k, _ in BIG}
    wts = {"w_cat": _to_cat(_all_gather_weights([shards.pop("w_in")], name="ag_w_in")[0])}
    comm = _Comm(shards, core, where)
    conv_shard = conv_w[0]
    conv_rows = -(-conv_shard.size // (8 * LANES)) * 8
    conv_all = _all_gather_small(_rows128(conv_shard, conv_rows), name="ag_conv")
    conv_full = jnp.concatenate(
        [conv_all[2 * s].reshape(-1)[:conv_shard.size].reshape(conv_shard.shape) for s in range(N_CHIPS)], axis=1)

    small = {k: given[k].reshape(1, -1) for k in SMALL}
    small["conv_w"] = conv_full
    loss_row, dx, grads = _local_step(x[0], mem[0], loss_target[0], wts, small, comm)
    big_grads = {k: grads[k] for k, _ in BIG}

    layout = [("loss", loss_row[:, :1])] + [(k, grads[k]) for k in SMALL] + [("conv_w", grads["conv_w"])]
    rows = [-(-a.size // LANES) for _, a in layout]
    blk_rows = -(-sum(rows) // 8) * 8
    blk = jnp.concatenate([_rows128(a.astype(F32), n) for (_, a), n in zip(layout, rows)]
                          + [jnp.zeros((blk_rows - sum(rows), LANES), F32)], axis=0)
    gathered = _all_gather_small(blk, name="ag_small_grads")
    reduced = _sum_rows([gathered[i] for i in range(N_DEV)], F32, name="small_grad_sum")
    small_grads, start = {}, 0
    for (k, a), n in zip(layout, rows):
        small_grads[k] = reduced[start:start + n].reshape(-1)[:a.size].reshape(a.shape)
        start += n
    loss = small_grads["loss"].reshape(())
    cw = conv_shard.shape[1]
    conv_grad = lax.dynamic_slice_in_dim(small_grads["conv_w"], chip * cw, cw, axis=1)

    names = ["g_mix", "w_in", "sinks", "conv_w", "a_log", "dt_bias", "gdn_norm_w", "g_mem", "w_mem_kv", "w_swa_up",
             "w_gdn_up", "w_xa_up", "w_out", "g_mlp", "w_mlp_in", "w_mlp_out", "g_final"]
    out_g, out_d, out_m, out_v = [], [], [], []
    for k in names:
        w, m, v = given[k], given["m_" + k], given["v_" + k]
        if k in big_grads:
            g2 = big_grads[k]
        elif k == "conv_w":
            g2 = conv_grad
        else:
            g2 = small_grads[k]
        as_given = (lambda a: a.reshape(1, -1)) if w.ndim == 1 else (lambda a: a)
        delta, new_m, new_v = _adamw(as_given(w), g2, as_given(m), as_given(v), name="adamw_" + k)
        out_g.append(g2.reshape(w.shape))
        out_d.append(delta.reshape(w.shape))
        out_m.append(new_m.reshape(w.shape))
        out_v.append(new_v.reshape(w.shape))
    return (loss, dx[None], *out_g, *out_d, *out_m, *out_v)
```

```python
import functools
import math
from typing import Callable, NamedTuple

import jax
import jax.numpy as jnp
from jax import lax
from jax.experimental import pallas as pl
from jax.experimental.pallas import tpu as pltpu

F32 = jnp.float32
BF16 = jnp.bfloat16
HI = lax.Precision.HIGHEST
MESH = pl.DeviceIdType.MESH

SWA_Q_HEADS = 16
SWA_KV_HEADS = 2
SWA_HEAD_DIM = 64
SWA_WINDOW = 128
SWA_SCALE = SWA_HEAD_DIM ** -0.5
assert math.frexp(SWA_SCALE)[0] == 0.5
GDN_HEADS = 4
GDN_HEAD_DIM = 128
GDN_CONV = 4
GDN_CHUNK = 64
XA_HEADS = 4
XA_HEAD_DIM = 128
RMS_EPS = 1e-6
L2_EPS = 1e-6
ADAM_LR = 0.001
ADAM_B1 = 0.9
ADAM_B2 = 0.999
ADAM_EPS = 1e-08
ADAM_WD = 0.01
ADAM_STEP = 10

LANES = 128
N_SHARDS = 4
VMEM_LIMIT = 56 * 1024 * 1024

NT = (((1,), (1,)), ((), ()))
TN = (((0,), (0,)), ((), ()))
NN = (((1,), (0,)), ((), ()))


def _cp(sem=None):
    return pltpu.CompilerParams(dimension_semantics=sem, vmem_limit_bytes=VMEM_LIMIT)


def _blk(dim, pref):
    if dim <= pref:
        return dim
    b = (pref // LANES) * LANES
    while dim % b:
        b -= LANES
    assert b > 0, (dim, pref)
    return b


def _dot(a, b, dims=NN, precision=None):
    return lax.dot_general(a, b, dims, precision=precision, preferred_element_type=F32)


def _sigmoid(x):
    return 1.0 / (1.0 + jnp.exp(-x))


MM_TK_BYTES = 4096


def _mm(a, b, *, name, ta=False, tb=False, out_dtypes=(F32,), epilogue=None, extras=(), tm=1024, tn=1024, tk=None,
        b_sharded=False, out_sharded=False, b_window=None, exchange=None):
    (kdim, m) = a.shape if ta else a.shape[::-1]
    col0 = 0
    n_lim = k_lim = None
    if b_sharded:
        ns, rows_w, per = b.shape
        if tb:
            kb, n, k_lim = ns * per, rows_w, per
        else:
            kb, n, n_lim = rows_w, ns * per, per
    else:
        (kb, n) = b.shape[::-1] if tb else b.shape
        if b_window is not None:
            assert not tb
            col0, n = b_window
    assert kdim == kb, (a.shape, b.shape, ta, tb)
    if out_sharded:
        assert n % N_SHARDS == 0
        n_lim = n // N_SHARDS if n_lim is None else n_lim
        assert n_lim == n // N_SHARDS
    if tk is None:
        tk = MM_TK_BYTES // max(a.dtype.itemsize, b.dtype.itemsize)
    tm, tn, tk = _blk(m, tm), _blk(n_lim or n, tn), _blk(k_lim or kdim, tk)
    assert col0 % tn == 0, (col0, tn)
    nk = kdim // tk
    a_spec = pl.BlockSpec((tk, tm), lambda i, j, k: (k, i)) if ta else pl.BlockSpec((tm, tk), lambda i, j, k: (i, k))
    if b_sharded and tb:
        kpb = k_lim // tk
        b_spec = pl.BlockSpec((None, tn, tk), lambda i, j, k: (k // kpb, j, k % kpb))
    elif b_sharded:
        bpb = n_lim // tn
        b_spec = pl.BlockSpec((None, tk, tn), lambda i, j, k: (j // bpb, k, j % bpb))
    elif tb:
        b_spec = pl.BlockSpec((tn, tk), lambda i, j, k: (j, k))
    else:
        b_spec = pl.BlockSpec((tk, tn), lambda i, j, k: (k, j + col0 // tn))
    x_spec = pl.BlockSpec((tm, tn), lambda i, j, k: (i, j))
    if out_sharded:
        opb = n_lim // tn
        o_spec = pl.BlockSpec((None, tm, tn), lambda i, j, k: (j // opb, i, j % opb))
        out_shape = (N_SHARDS, m, n_lim)
    else:
        o_spec, out_shape = x_spec, (m, n)
    dims = ((((0 if ta else 1),), ((1 if tb else 0),)), ((), ()))
    n_extra, n_out = len(extras), len(out_dtypes)

    host = _ExchangeHost(exchange)
    grid = (m // tm, n // tn, nk)

    def body(*refs):
        a_ref, b_ref = refs[:2]
        extra_refs = refs[2:2 + n_extra]
        out_refs = refs[2 + n_extra + host.n_in:2 + n_extra + host.n_in + n_out]
        host.start(refs, 2 + n_extra, 2 + n_extra + host.n_in + n_out, grid)
        part = _dot(a_ref[...].astype(BF16), b_ref[...].astype(BF16), dims)

        def finish(acc):
            vals = epilogue(acc, *[r[...] for r in extra_refs]) if epilogue is not None else (acc,) * n_out
            assert len(vals) == n_out
            for r, v in zip(out_refs, vals):
                r[...] = v.astype(r.dtype)

        if nk == 1:
            finish(part)
        else:
            acc_ref = refs[2 + n_extra + host.n_in + n_out + host.n_out]
            k = pl.program_id(2)

            @pl.when(k == 0)
            def _():
                acc_ref[...] = part

            @pl.when((k > 0) & (k < nk - 1))
            def _():
                acc_ref[...] += part

            @pl.when(k == nk - 1)
            def _():
                finish(acc_ref[...] + part)

        host.wait(refs, 2 + n_extra, 2 + n_extra + host.n_in + n_out, grid)

    outs = pl.pallas_call(
        body,
        grid=grid,
        in_specs=[a_spec, b_spec] + [x_spec] * n_extra + host.in_specs,
        out_specs=[o_spec] * n_out + host.out_specs,
        out_shape=[jax.ShapeDtypeStruct(out_shape, d) for d in out_dtypes] + host.out_shapes,
        scratch_shapes=([pltpu.VMEM((tm, tn), F32)] if nk > 1 else []) + host.scratch,
        input_output_aliases=host.aliases(2 + n_extra, n_out),
        compiler_params=_cp(host.semantics(("parallel", "parallel", "arbitrary"))),
        name=name,
    )(a, b, *extras, *host.ins)
    mine, landed = outs[:n_out], list(outs[n_out:])
    mine = mine[0] if n_out == 1 else mine
    return (mine, landed) if exchange is not None else mine


class Exchange(NamedTuple):
    ins: tuple
    out_shapes: tuple
    n_remote: int
    n_local: int
    copies_of: Callable
    aliases: tuple = ()


def _exchange_copies(ex, in_refs, out_refs, sem_refs):
    send_sems, recv_sems, local_sems = sem_refs
    remote, local = ex.copies_of(in_refs, out_refs, _place())
    assert len(remote) == ex.n_remote and len(local) == ex.n_local, (len(remote), len(local))
    cps = [pltpu.make_async_remote_copy(src_ref=src, dst_ref=dst, send_sem=send_sems.at[k], recv_sem=recv_sems.at[k],
                                        device_id=to, device_id_type=MESH) for k, (src, dst, to) in enumerate(remote)]
    cps += [pltpu.make_async_copy(src, dst, local_sems.at[k]) for k, (src, dst) in enumerate(local)]
    return cps


class _ExchangeHost:
    def __init__(self, ex):
        self.ex = ex
        self.ins = list(ex.ins) if ex else []
        self.out_shapes = list(ex.out_shapes) if ex else []
        self.n_in, self.n_out = len(self.ins), len(self.out_shapes)
        self.in_specs = [HBM_SPEC] * self.n_in
        self.out_specs = [HBM_SPEC] * self.n_out
        self.scratch = _sem_scratch(ex.n_remote, ex.n_local) if ex else []

    def semantics(self, sem):
        return tuple("arbitrary" for _ in sem) if self.ex else sem

    def aliases(self, in_at, out_at):
        return {in_at + i: out_at + o for i, o in self.ex.aliases} if self.ex else {}

    def _refs(self, refs, in_at, out_at):
        return refs[in_at:in_at + self.n_in], refs[out_at:out_at + self.n_out], refs[len(refs) - 3:]

    def _when(self, grid, last):
        cond = None
        for d, size in enumerate(grid):
            c = pl.program_id(d) == (size - 1 if last else 0)
            cond = c if cond is None else cond & c
        return cond

    def start(self, refs, in_at, out_at, grid):
        if self.ex:
            @pl.when(self._when(grid, False))
            def _():
                for cp in _exchange_copies(self.ex, *self._refs(refs, in_at, out_at)):
                    cp.start()

    def wait(self, refs, in_at, out_at, grid):
        if self.ex:
            @pl.when(self._when(grid, True))
            def _():
                for cp in _exchange_copies(self.ex, *self._refs(refs, in_at, out_at)):
                    cp.wait()


def _rms_fwd(x, g, *, name, tm=512):
    t, d = x.shape
    tm = _blk(t, tm)

    def body(x_ref, g_ref, n_ref):
        xv = x_ref[...]
        r = lax.rsqrt(jnp.mean(xv * xv, axis=-1, keepdims=True) + RMS_EPS)
        n_ref[...] = (xv * r * g_ref[...]).astype(n_ref.dtype)

    return pl.pallas_call(
        body, grid=(t // tm,),
        in_specs=[pl.BlockSpec((tm, d), lambda i: (i, 0)), pl.BlockSpec((1, d), lambda i: (0, 0))],
        out_specs=pl.BlockSpec((tm, d), lambda i: (i, 0)),
        out_shape=jax.ShapeDtypeStruct((t, d), BF16),
        compiler_params=_cp(("parallel",)), name=name,
    )(x, g)


def _rms_bwd(dn, x, g, dres, *, name, tm=512):
    t, d = x.shape
    tm = _blk(t, tm)

    def body(dn_ref, x_ref, g_ref, dres_ref, dx_ref, dxb_ref, dg_ref):
        i = pl.program_id(0)
        xv = x_ref[...]
        r = lax.rsqrt(jnp.mean(xv * xv, axis=-1, keepdims=True) + RMS_EPS)
        xh = xv * r
        dnv = dn_ref[...].astype(F32)
        dxh = dnv * g_ref[...]
        dx = dres_ref[...] + r * (dxh - xh * jnp.mean(dxh * xh, axis=-1, keepdims=True))
        dx_ref[...] = dx
        dxb_ref[...] = dx.astype(dxb_ref.dtype)
        part = jnp.sum(dnv * xh, axis=0, keepdims=True)

        @pl.when(i == 0)
        def _():
            dg_ref[...] = part

        @pl.when(i > 0)
        def _():
            dg_ref[...] += part

    row = pl.BlockSpec((tm, d), lambda i: (i, 0))
    vec = pl.BlockSpec((1, d), lambda i: (0, 0))
    return pl.pallas_call(
        body, grid=(t // tm,),
        in_specs=[row, row, vec, row], out_specs=[row, row, vec],
        out_shape=[jax.ShapeDtypeStruct((t, d), F32), jax.ShapeDtypeStruct((t, d), BF16),
                   jax.ShapeDtypeStruct((1, d), F32)],
        compiler_params=_cp(("arbitrary",)), name=name,
    )(dn, x, g, dres)


def _final_loss(h, g, tgt, *, name, tm=512):
    t, d = h.shape
    tm = _blk(t, tm)

    def body(h_ref, g_ref, t_ref, dh_ref, dhb_ref, dg_ref, loss_ref):
        i = pl.program_id(0)
        hv = h_ref[...]
        r = lax.rsqrt(jnp.mean(hv * hv, axis=-1, keepdims=True) + RMS_EPS)
        xh = hv * r
        e = xh * g_ref[...] - t_ref[...]
        dy = e * (1.0 / d)
        dxh = dy * g_ref[...]
        dh = r * (dxh - xh * jnp.mean(dxh * xh, axis=-1, keepdims=True))
        dh_ref[...] = dh
        dhb_ref[...] = dh.astype(dhb_ref.dtype)
        dg_part = jnp.sum(dy * xh, axis=0, keepdims=True)
        row_loss = jnp.sum(e * e, axis=-1, keepdims=True) * (0.5 / d)
        loss_part = jnp.sum(row_loss, axis=0, keepdims=True)

        @pl.when(i == 0)
        def _():
            dg_ref[...] = dg_part
            loss_ref[...] = jnp.broadcast_to(loss_part, loss_ref.shape)

        @pl.when(i > 0)
        def _():
            dg_ref[...] += dg_part
            loss_ref[...] += jnp.broadcast_to(loss_part, loss_ref.shape)

    row = pl.BlockSpec((tm, d), lambda i: (i, 0))
    vec = pl.BlockSpec((1, d), lambda i: (0, 0))
    return pl.pallas_call(
        body, grid=(t // tm,),
        in_specs=[row, vec, row], out_specs=[row, row, vec, pl.BlockSpec((1, LANES), lambda i: (0, 0))],
        out_shape=[jax.ShapeDtypeStruct((t, d), F32), jax.ShapeDtypeStruct((t, d), BF16),
                   jax.ShapeDtypeStruct((1, d), F32), jax.ShapeDtypeStruct((1, LANES), F32)],
        compiler_params=_cp(("arbitrary",)), name=name,
    )(h, g, tgt)


def _swa_mask(n, reps):
    w = SWA_WINDOW
    qi = lax.broadcasted_iota(jnp.int32, (reps * w, 2 * w), 0) & (w - 1)
    kj = lax.broadcasted_iota(jnp.int32, (reps * w, 2 * w), 1)
    return (kj > qi) & (kj <= qi + w) & ((n > 0) | (kj >= w))


def _stack_heads(ref, heads, width):
    return jnp.concatenate([ref[:, h * width:(h + 1) * width] for h in heads], axis=0)


def _stack_scalars(ref, heads, rows):
    return jnp.concatenate([jnp.broadcast_to(ref[0:1, h:h + 1], (rows, 1)) for h in heads], axis=0)


def _swa_fwd(q, kv, sinks, *, name, exchange=None):
    t = q.shape[0]
    w, hd, hq, hkv = SWA_WINDOW, SWA_HEAD_DIM, SWA_Q_HEADS, SWA_KV_HEADS
    grp = hq // hkv
    kvw = hkv * hd
    nb = t // w
    host = _ExchangeHost(exchange)
    assert not (exchange and exchange.aliases)

    def body(*refs):
        q_ref, kvp_ref, kvc_ref, s_ref = refs[:4]
        o_ref, lse_ref = refs[4 + host.n_in:6 + host.n_in]
        host.start(refs, 4, 6 + host.n_in, (nb,))
        n = pl.program_id(0)
        mask = _swa_mask(n, grp)
        kvcat = jnp.concatenate([kvp_ref[...], kvc_ref[...]], axis=0)
        outs, lses = [], []
        for hk in range(hkv):
            heads = range(hk * grp, (hk + 1) * grp)
            qs = _stack_heads(q_ref, heads, hd)
            kh = kvcat[:, hk * hd:(hk + 1) * hd]
            vh = kvcat[:, kvw + hk * hd:kvw + (hk + 1) * hd]
            sk = _stack_scalars(s_ref, heads, w)
            s = jnp.where(mask, _dot(qs * SWA_SCALE, kh, NT), -jnp.inf)
            m = jnp.maximum(jnp.max(s, axis=-1, keepdims=True), sk)
            p = jnp.exp(s - m)
            den = jnp.sum(p, axis=-1, keepdims=True) + jnp.exp(sk - m)
            o = _dot((p * (1.0 / den)).astype(BF16), vh)
            lse = m + jnp.log(den)
            outs += [o[j * w:(j + 1) * w] for j in range(grp)]
            lses += [lse[j * w:(j + 1) * w] for j in range(grp)]
        o_ref[...] = jnp.concatenate(outs, axis=1).astype(o_ref.dtype)
        lse_ref[...] = jnp.concatenate(lses, axis=1)
        host.wait(refs, 4, 6 + host.n_in, (nb,))

    outs = pl.pallas_call(
        body, grid=(nb,),
        in_specs=[pl.BlockSpec((w, hq * hd), lambda i: (i, 0)),
                  pl.BlockSpec((w, 2 * kvw), lambda i: (jnp.maximum(i - 1, 0), 0)),
                  pl.BlockSpec((w, 2 * kvw), lambda i: (i, 0)),
                  pl.BlockSpec((1, hq), lambda i: (0, 0))] + host.in_specs,
        out_specs=[pl.BlockSpec((w, hq * hd), lambda i: (i, 0)), pl.BlockSpec((w, hq), lambda i: (i, 0))] + host.out_specs,
        out_shape=[jax.ShapeDtypeStruct((t, hq * hd), BF16), jax.ShapeDtypeStruct((t, hq), F32)] + host.out_shapes,
        scratch_shapes=host.scratch,
        compiler_params=_cp(host.semantics(("parallel",))), name=name,
    )(q, kv, kv, sinks, *host.ins)
    return (outs[0], outs[1], list(outs[2:])) if exchange is not None else outs


ANY_SPEC = pl.BlockSpec(memory_space=pl.ANY)


def _swa_bwd(q, kv, sinks, o, lse, do, dp, dp_col, *, name):
    t = q.shape[0]
    w, hd, hq, hkv = SWA_WINDOW, SWA_HEAD_DIM, SWA_Q_HEADS, SWA_KV_HEADS
    grp = hq // hkv
    kvw = hkv * hd
    nb = t // w
    assert dp_col % (hq * hd) == 0
    dq_blk = dp_col // (hq * hd)

    def body(q_ref, kvp_ref, kvc_ref, s_ref, o_ref, lse_ref, do_ref, _, dq_ref, dkv_ref, ds_ref, carry_ref):
        n = pl.program_id(0)

        @pl.when(n == 0)
        def _():
            ds_ref[...] = jnp.zeros_like(ds_ref)
            carry_ref[...] = jnp.zeros_like(carry_ref)

        @pl.when(n < nb)
        def _():
            mask = _swa_mask(n, grp)
            kvcat = jnp.concatenate([kvp_ref[...], kvc_ref[...]], axis=0)
            dqs, dsk, dks, dvs = [], [], [], []
            for hk in range(hkv):
                heads = range(hk * grp, (hk + 1) * grp)
                qs = _stack_heads(q_ref, heads, hd)
                dos = _stack_heads(do_ref, heads, hd)
                os_ = _stack_heads(o_ref, heads, hd)
                lse = _stack_heads(lse_ref, heads, 1)
                sk = _stack_scalars(s_ref, heads, w)
                kh = kvcat[:, hk * hd:(hk + 1) * hd]
                vh = kvcat[:, kvw + hk * hd:kvw + (hk + 1) * hd]
                s = _dot(qs * SWA_SCALE, kh, NT)
                p = jnp.exp(jnp.where(mask, s, -jnp.inf) - lse)
                delta = jnp.sum(dos.astype(F32) * os_.astype(F32), axis=-1, keepdims=True)
                ds = (p * (_dot(dos, vh, NT) - delta) * SWA_SCALE).astype(BF16)
                dq = _dot(ds, kh)
                dqs += [dq[j * w:(j + 1) * w] for j in range(grp)]
                dks.append(_dot(ds, qs, TN))
                dvs.append(_dot(p.astype(BF16), dos, TN))
                dsink = -jnp.exp(sk - lse) * delta
                dsk += [jnp.sum(dsink[j * w:(j + 1) * w], axis=0, keepdims=True) for j in range(grp)]
            dq_ref[...] = jnp.concatenate(dqs, axis=1).astype(dq_ref.dtype)
            ds_ref[...] += jnp.concatenate(dsk, axis=1)
            dkv_cat = jnp.concatenate(dks + dvs, axis=1)
            dkv_ref[...] = (carry_ref[...] + dkv_cat[:w]).astype(dkv_ref.dtype)
            carry_ref[...] = dkv_cat[w:]

        @pl.when(n == nb)
        def _():
            dkv_ref[...] = carry_ref[...].astype(dkv_ref.dtype)

    cur = lambda i: (jnp.minimum(i, nb - 1), 0)
    prev = lambda i: (jnp.clip(i - 1, 0, nb - 1), 0)
    return pl.pallas_call(
        body, grid=(nb + 1,),
        in_specs=[pl.BlockSpec((w, hq * hd), cur), pl.BlockSpec((w, 2 * kvw), prev), pl.BlockSpec((w, 2 * kvw), cur),
                  pl.BlockSpec((1, hq), lambda i: (0, 0)), pl.BlockSpec((w, hq * hd), cur),
                  pl.BlockSpec((w, hq), cur), pl.BlockSpec((w, hq * hd), cur), ANY_SPEC],
        out_specs=[pl.BlockSpec((w, hq * hd), lambda i: (jnp.minimum(i, nb - 1), dq_blk)),
                   pl.BlockSpec((w, 2 * kvw), prev), pl.BlockSpec((1, hq), lambda i: (0, 0))],
        out_shape=[jax.ShapeDtypeStruct(dp.shape, dp.dtype), jax.ShapeDtypeStruct((t, 2 * kvw), BF16),
                   jax.ShapeDtypeStruct((1, hq), F32)],
        scratch_shapes=[pltpu.VMEM((w, 2 * kvw), F32)], input_output_aliases={7: 0},
        compiler_params=_cp(("arbitrary",)), name=name,
    )(q, kv, kv, sinks, o, lse, do, dp)


def _xa_fwd(q, mkv, *, name, tq=512):
    t, xw = q.shape
    nm = mkv.shape[0]
    hd, nh = XA_HEAD_DIM, XA_HEADS
    tq = _blk(t, tq)

    def body(q_ref, mkv_ref, o_ref):
        outs = []
        for h in range(nh):
            qh = q_ref[:, h * hd:(h + 1) * hd]
            kh = mkv_ref[:, h * hd:(h + 1) * hd]
            vh = mkv_ref[:, xw + h * hd:xw + (h + 1) * hd]
            s = _dot(qh, kh, NT) * (hd ** -0.5)
            p = jnp.exp(s - jnp.max(s, axis=-1, keepdims=True))
            p = p / jnp.sum(p, axis=-1, keepdims=True)
            outs.append(_dot(p.astype(BF16), vh))
        o_ref[...] = jnp.concatenate(outs, axis=1).astype(o_ref.dtype)

    return pl.pallas_call(
        body, grid=(t // tq,),
        in_specs=[pl.BlockSpec((tq, xw), lambda i: (i, 0)), pl.BlockSpec((nm, 2 * xw), lambda i: (0, 0))],
        out_specs=pl.BlockSpec((tq, xw), lambda i: (i, 0)),
        out_shape=jax.ShapeDtypeStruct((t, xw), BF16),
        compiler_params=_cp(("parallel",)), name=name,
    )(q, mkv)


def _xa_bwd(q, mkv, do, dp, dp_col, *, name, tq=512):
    t, xw = q.shape
    nm = mkv.shape[0]
    hd, nh = XA_HEAD_DIM, XA_HEADS
    tq = _blk(t, tq)
    assert dp_col % xw == 0

    def body(q_ref, mkv_ref, do_ref, _, dq_ref, dmkv_ref):
        i = pl.program_id(0)
        dqs, dks, dvs = [], [], []
        for h in range(nh):
            qh = q_ref[:, h * hd:(h + 1) * hd]
            kh = mkv_ref[:, h * hd:(h + 1) * hd]
            vh = mkv_ref[:, xw + h * hd:xw + (h + 1) * hd]
            doh = do_ref[:, h * hd:(h + 1) * hd]
            s = _dot(qh, kh, NT) * (hd ** -0.5)
            p = jnp.exp(s - jnp.max(s, axis=-1, keepdims=True))
            p = p / jnp.sum(p, axis=-1, keepdims=True)
            dp = _dot(doh, vh, NT)
            ds = (p * (dp - jnp.sum(p * dp, axis=-1, keepdims=True)) * (hd ** -0.5)).astype(BF16)
            dqs.append(_dot(ds, kh))
            dks.append(_dot(ds, qh, TN))
            dvs.append(_dot(p.astype(BF16), doh, TN))
        dq_ref[...] = jnp.concatenate(dqs, axis=1).astype(dq_ref.dtype)
        part = jnp.concatenate(dks + dvs, axis=1)

        @pl.when(i == 0)
        def _():
            dmkv_ref[...] = part

        @pl.when(i > 0)
        def _():
            dmkv_ref[...] += part

    row = pl.BlockSpec((tq, xw), lambda i: (i, 0))
    full = pl.BlockSpec((nm, 2 * xw), lambda i: (0, 0))
    return pl.pallas_call(
        body, grid=(t // tq,),
        in_specs=[row, full, row, ANY_SPEC],
        out_specs=[pl.BlockSpec((tq, xw), lambda i: (i, dp_col // xw)), full],
        out_shape=[jax.ShapeDtypeStruct(dp.shape, dp.dtype), jax.ShapeDtypeStruct((nm, 2 * xw), F32)],
        input_output_aliases={3: 0}, compiler_params=_cp(("arbitrary",)), name=name,
    )(q, mkv, do, dp)


def _merge_specs(ys, ws, tm):
    y_specs = [pl.BlockSpec((tm, y.shape[1]), lambda i: (i, 0)) for y in ys]
    w_specs = [pl.BlockSpec(w.shape, lambda i: (0, 0, 0)) for w in ws]
    return y_specs, w_specs


def _merge_tiles(ws, tn):
    ns, _, per = ws[0].shape
    tn = _blk(per, tn)
    return tn, [(s, c, s * per + c) for s in range(ns) for c in range(0, per, tn)]


def _merge_fwd(ys, ws, gates, *, name, tm=256, tn=512):
    t, d = ys[0].shape[0], ws[0].shape[0] * ws[0].shape[2]
    tm = _blk(t, tm)
    tn, tiles = _merge_tiles(ws, tn)
    y_specs, w_specs = _merge_specs(ys, ws, tm)

    def body(ya, yb, yc, wa, wb, wc, g_ref, o_ref):
        for s, c, col in tiles:
            acc = None
            for b, (y, w) in enumerate(((ya, wa), (yb, wb), (yc, wc))):
                term = _sigmoid(g_ref[:, b * d + col:b * d + col + tn]) * _dot(y[...], w[s, :, c:c + tn])
                acc = term if acc is None else acc + term
            o_ref[:, col:col + tn] = acc.astype(o_ref.dtype)

    return pl.pallas_call(
        body, grid=(t // tm,),
        in_specs=y_specs + w_specs + [pl.BlockSpec((tm, 3 * d), lambda i: (i, 0))],
        out_specs=pl.BlockSpec((tm, d), lambda i: (i, 0)),
        out_shape=jax.ShapeDtypeStruct((t, d), BF16),
        compiler_params=_cp(("parallel",)), name=name,
    )(*ys, *ws, gates)


def _merge_bwd(ys, ws, gates, dmerged, dp_width, *, name, tm=256, tn=512):
    t, d = ys[0].shape[0], ws[0].shape[0] * ws[0].shape[2]
    tm = _blk(t, tm)
    tn, tiles = _merge_tiles(ws, tn)
    y_specs, w_specs = _merge_specs(ys, ws, tm)
    row = pl.BlockSpec((tm, d), lambda i: (i, 0))
    wide = pl.BlockSpec((tm, 3 * d), lambda i: (i, 0))

    def body(ya, yb, yc, wa, wb, wc, g_ref, dm_ref, dua, dub, duc, dp_ref):
        for s, c, col in tiles:
            dm = dm_ref[:, col:col + tn]
            for b, (y, w, du) in enumerate(((ya, wa, dua), (yb, wb, dub), (yc, wc, duc))):
                sg = _sigmoid(g_ref[:, b * d + col:b * d + col + tn])
                u = _dot(y[...], w[s, :, c:c + tn])
                du[:, col:col + tn] = (dm * sg).astype(du.dtype)
                dp_ref[:, b * d + col:b * d + col + tn] = (dm * u * sg * (1.0 - sg)).astype(dp_ref.dtype)

    return pl.pallas_call(
        body, grid=(t // tm,),
        in_specs=y_specs + w_specs + [wide, row],
        out_specs=[row] * 3 + [wide],
        out_shape=[jax.ShapeDtypeStruct((t, d), BF16)] * 3 + [jax.ShapeDtypeStruct((t, dp_width), BF16)],
        compiler_params=_cp(("parallel",)), name=name,
    )(*ys, *ws, gates, dmerged)


def _adamw(w, g, m, v, *, name, tm=256):
    lead = w.ndim - 2
    assert all(s == 1 for s in w.shape[:lead]) and m.shape == w.shape and v.shape == w.shape
    r, c = w.shape[lead:]
    assert g.shape == (r, c)
    tm = _blk(r, tm) if r % 8 == 0 else r
    bc1 = 1.0 - ADAM_B1 ** ADAM_STEP
    bc2 = 1.0 - ADAM_B2 ** ADAM_STEP

    def body(w_ref, g_ref, m_ref, v_ref, d_ref, nm_ref, nv_ref):
        gv = g_ref[...]
        nm = ADAM_B1 * m_ref[...] + (1.0 - ADAM_B1) * gv
        nv = ADAM_B2 * v_ref[...] + (1.0 - ADAM_B2) * (gv * gv)
        d_ref[...] = -ADAM_LR * ((nm / bc1) / (jnp.sqrt(nv / bc2) + ADAM_EPS) + ADAM_WD * w_ref[...])
        nm_ref[...] = nm
        nv_ref[...] = nv

    spec = pl.BlockSpec((None,) * lead + (tm, c), lambda i: (0,) * lead + (i, 0))
    g_spec = pl.BlockSpec((tm, c), lambda i: (i, 0))
    return pl.pallas_call(
        body, grid=(r // tm,), in_specs=[spec, g_spec, spec, spec], out_specs=[spec] * 3,
        out_shape=[jax.ShapeDtypeStruct(w.shape, F32)] * 3,
        compiler_params=_cp(("parallel",)), name=name,
    )(w, g, m, v)


HALO = 8


def _shift_down(cur, prev, j):
    if j == 0:
        return cur
    y = pltpu.roll(cur, j, 0)
    row = lax.broadcasted_iota(jnp.int32, (HALO, cur.shape[1]), 0)
    top = jnp.where(row < j, pltpu.roll(prev, j, 0), y[:HALO])
    return jnp.concatenate([top, y[HALO:]], axis=0)


def _shift_up(cur, nxt, j):
    if j == 0:
        return cur
    tm = cur.shape[0]
    y = pltpu.roll(cur, tm - j, 0)
    row = lax.broadcasted_iota(jnp.int32, (HALO, cur.shape[1]), 0)
    bot = jnp.where(row >= HALO - j, pltpu.roll(nxt, HALO - j, 0), y[tm - HALO:])
    return jnp.concatenate([y[:tm - HALO], bot], axis=0)


def _softplus(x):
    return jnp.maximum(x, 0.0) + jnp.log(1.0 + jnp.exp(-jnp.abs(x)))


def _gdn_pre_fwd(qkvb, conv_w, ab, alog_pad, dt_pad, *, name, ab_blk=0, tm=256):
    t, cw = qkvb.shape
    hd, nh, ck = GDN_HEAD_DIM, GDN_HEADS, GDN_CHUNK
    gw = nh * hd
    tm = _blk(t, tm)
    hb = tm // HALO

    def body(x_ref, xp_ref, w_ref, ab_ref, al_ref, dt_ref, xc_ref, qkvn_ref, aux_ref):
        i = pl.program_id(0)
        cur = x_ref[...]
        prev = jnp.where(i > 0, xp_ref[...], 0.0)
        xc = None
        for tap in range(GDN_CONV):
            term = w_ref[tap:tap + 1, :] * _shift_down(cur, prev, GDN_CONV - 1 - tap)
            xc = term if xc is None else xc + term
        xc_ref[...] = xc
        s = xc * _sigmoid(xc)
        for h in range(2 * nh):
            xh = s[:, h * hd:(h + 1) * hd]
            r = lax.rsqrt(jnp.sum(xh * xh, axis=-1, keepdims=True) + L2_EPS)
            scale = hd ** -0.5 if h < nh else 1.0
            qkvn_ref[:, h * hd:(h + 1) * hd] = xh * (r * scale)
        qkvn_ref[:, 2 * gw:] = s[:, 2 * gw:]
        abv = ab_ref[...]
        lane = lax.broadcasted_iota(jnp.int32, abv.shape, 1)
        g = jnp.where(lane < nh, -jnp.exp(al_ref[...]) * _softplus(abv + dt_ref[...]), 0.0)
        beta = jnp.where((lane >= nh) & (lane < 2 * nh), _sigmoid(abv), 0.0)
        ii = lax.broadcasted_iota(jnp.int32, (tm, tm), 0)
        jj = lax.broadcasted_iota(jnp.int32, (tm, tm), 1)
        tri = jnp.where((ii >= jj) & ((ii ^ jj) < ck), 1.0, 0.0)
        gcum = _dot(tri, g, precision=HI)
        aux_ref[...] = g + beta + pltpu.roll(gcum, 2 * nh, 1)

    row = lambda c: pl.BlockSpec((tm, c), lambda i: (i, 0))
    vec = lambda r, c: pl.BlockSpec((r, c), lambda i: (0, 0))
    return pl.pallas_call(
        body, grid=(t // tm,),
        in_specs=[row(cw), pl.BlockSpec((HALO, cw), lambda i: (jnp.maximum(i * hb - 1, 0), 0)), vec(GDN_CONV, cw),
                  pl.BlockSpec((tm, LANES), lambda i: (i, ab_blk)), vec(1, LANES), vec(1, LANES)],
        out_specs=[row(cw), row(cw), row(LANES)],
        out_shape=[jax.ShapeDtypeStruct((t, cw), F32), jax.ShapeDtypeStruct((t, cw), F32),
                   jax.ShapeDtypeStruct((t, LANES), F32)],
        compiler_params=_cp(("parallel",)), name=name,
    )(qkvb, qkvb, conv_w, ab, alog_pad, dt_pad)


GDN_STEP_CHUNKS = 4
GDN_ILP_CHUNKS = 4
GDN_ILP_CHUNKS_BWD = 4


def _bdot(a, b, dims=NN):
    return _dot(a.astype(BF16), b.astype(BF16), dims)


def _split_bf16(x):
    hi = x.astype(BF16)
    return hi, (x - hi.astype(F32)).astype(BF16)


def _dot3(a, b, dims=NN):
    ah, al = _split_bf16(a)
    bh, bl = _split_bf16(b)
    return _dot(ah, bh, dims) + (_dot(ah, bl, dims) + _dot(al, bh, dims))


def _dot3_many(lhs, rhs, dims=NN):
    sa = [_split_bf16(a) for a in lhs]
    sb = [_split_bf16(b) for b in rhs]
    hh = [_dot(a[0], b[0], dims) for a, b in zip(sa, sb)]
    hl = [_dot(a[0], b[1], dims) for a, b in zip(sa, sb)]
    lh = [_dot(a[1], b[0], dims) for a, b in zip(sa, sb)]
    return [x + (y + z) for x, y, z in zip(hh, hl, lh)]


def _gdn_local(chains, with_inverse):
    ck = GDN_CHUNK
    ii = lax.broadcasted_iota(jnp.int32, (ck, ck), 0)
    jj = lax.broadcasted_iota(jnp.int32, (ck, ck), 1)
    lower, strict = ii >= jj, ii > jj
    dmat = [jnp.exp(jnp.where(lower, gc - gc_row, -jnp.inf)) for _, _, _, gc, gc_row in chains]
    kk = [_bdot(k, k, NT) for _, k, _, _, _ in chains]
    qk = [_bdot(q, k, NT) for q, k, _, _, _ in chains]
    tinv = [None] * len(chains)
    if with_inverse:
        lmat = [jnp.where(strict, c[2] * kk_i * d_i, 0.0) for c, kk_i, d_i in zip(chains, kk, dmat)]
        eye = jnp.where(ii == jj, 1.0, 0.0)
        tinv = [eye - l_i for l_i in lmat]
        pw = lmat
        for _ in range(int(math.log2(ck)) - 1):
            pw = _dot3_many(pw, pw)
            tinv = [t_i + d_i for t_i, d_i in zip(tinv, _dot3_many(tinv, pw))]
    out = []
    for (q, k, b, gc, gc_row), dmat_i, kk_i, qk_i, tinv_i in zip(chains, dmat, kk, qk, tinv):
        gl = gc[ck - 1:ck, :]
        out.append(dict(lower=lower, strict=strict, dmat=dmat_i, kk=kk_i, tinv=tinv_i, gam=jnp.exp(gc), qk=qk_i,
                        mm=qk_i * dmat_i, kdec=jnp.exp(gl - gc)))
    return out


def _gdn_head_cols(h):
    return slice(h * GDN_HEAD_DIM, (h + 1) * GDN_HEAD_DIM)


def _gdn_chunk_inputs(x_ref, aux_ref, auxt_ref, g, h):
    nh, ck = GDN_HEADS, GDN_CHUNK
    gw = nh * GDN_HEAD_DIM
    rows = slice(g * ck, (g + 1) * ck)
    cols = _gdn_head_cols(h)
    q = x_ref[rows, cols]
    k = x_ref[rows, gw + cols.start:gw + cols.stop]
    v = x_ref[rows, 2 * gw + cols.start:2 * gw + cols.stop]
    b = aux_ref[rows, nh + h:nh + h + 1]
    gc = aux_ref[rows, 2 * nh + h:2 * nh + h + 1]
    gc_row = auxt_ref[g, 2 * nh + h:2 * nh + h + 1, :]
    return q, k, v, b, gc, gc_row


def _gdn_specs(t, widths, *, reverse=False, step_chunks=None):
    rows = (step_chunks or GDN_STEP_CHUNKS) * GDN_CHUNK
    nsteps = t // rows
    idx = (lambda i: (nsteps - 1 - i, 0)) if reverse else (lambda i: (i, 0))
    return [pl.BlockSpec((rows, w), idx) for w in widths]


def _gdn_local_fwd(qkvn, aux, aux_t, *, name, exchange=None):
    t = qkvn.shape[0]
    hd, nh, ck, gs = GDN_HEAD_DIM, GDN_HEADS, GDN_CHUNK, GDN_STEP_CHUNKS
    gw = nh * hd
    host = _ExchangeHost(exchange)
    assert not (exchange and exchange.aliases)
    grid = (t // (gs * ck),)

    def body(*refs):
        x_ref, aux_ref, auxt_ref = refs[:3]
        u_ref, w_ref, qd_ref, kd_ref, mm_ref, tinv_ref = refs[3 + host.n_in:9 + host.n_in]
        host.start(refs, 3, 9 + host.n_in, grid)
        for g0 in range(0, gs, GDN_ILP_CHUNKS):
            where = [(g, h) for g in range(g0, g0 + GDN_ILP_CHUNKS) for h in range(nh)]
            ins = [_gdn_chunk_inputs(x_ref, aux_ref, auxt_ref, g, h) for g, h in where]
            lcs = _gdn_local([(q, k, b, gc, gc_row) for q, k, _, b, gc, gc_row in ins], True)
            tinvs = [lc["tinv"] for lc in lcs]
            us = _dot3_many(tinvs, [b * v for _, _, v, b, _, _ in ins])
            ws = _dot3_many(tinvs, [(b * lc["gam"]) * k for (_, k, _, b, _, _), lc in zip(ins, lcs)])
            for i, ((g, h), (q, k, _, _, _, _), lc) in enumerate(zip(where, ins, lcs)):
                rows, cols = slice(g * ck, (g + 1) * ck), _gdn_head_cols(h)
                u_ref[rows, cols] = us[i]
                w_ref[rows, cols] = ws[i].astype(w_ref.dtype)
                qd_ref[rows, cols] = (lc["gam"] * q).astype(qd_ref.dtype)
                kd_ref[rows, cols] = (lc["kdec"] * k).astype(kd_ref.dtype)
            for g in range(g0, g0 + GDN_ILP_CHUNKS):
                rows = slice(g * ck, (g + 1) * ck)
                mine = [lc for (gg, _), lc in zip(where, lcs) if gg == g]
                mm_ref[rows, :] = jnp.concatenate([lc["mm"] for lc in mine], axis=1).astype(mm_ref.dtype)
                tinv_ref[rows, :] = jnp.concatenate([lc["tinv"] for lc in mine], axis=1)
        host.wait(refs, 3, 9 + host.n_in, grid)

    sq = nh * ck
    outs = pl.pallas_call(
        body, grid=grid,
        in_specs=_gdn_specs(t, (3 * gw, LANES)) + [pl.BlockSpec((gs, 16, ck), lambda i: (i, 0, 0))] + host.in_specs,
        out_specs=_gdn_specs(t, (gw, gw, gw, gw, sq, sq)) + host.out_specs,
        out_shape=[jax.ShapeDtypeStruct((t, gw), F32)] + [jax.ShapeDtypeStruct((t, gw), BF16)] * 3
        + [jax.ShapeDtypeStruct((t, sq), BF16), jax.ShapeDtypeStruct((t, sq), F32)] + host.out_shapes,
        scratch_shapes=host.scratch,
        compiler_params=_cp(host.semantics(("parallel",))), name=name,
    )(qkvn, aux, aux_t, *host.ins)
    return (*outs[:6], list(outs[6:])) if exchange is not None else outs


def _gdn_seq_fwd(u, w, qd, kd, mm, aux, *, name):
    t = u.shape[0]
    hd, nh, ck, gs = GDN_HEAD_DIM, GDN_HEADS, GDN_CHUNK, GDN_STEP_CHUNKS
    gw = nh * hd
    sq = nh * ck

    def body(u_ref, w_ref, qd_ref, kd_ref, mm_ref, aux_ref, o_ref, vn_ref, sall_ref, s_ref):
        @pl.when(pl.program_id(0) == 0)
        def _():
            s_ref[...] = jnp.zeros_like(s_ref)

        for g in range(gs):
            rows = slice(g * ck, (g + 1) * ck)
            last = (g + 1) * ck - 1
            for h in range(nh):
                cols = _gdn_head_cols(h)
                st = s_ref[h]
                sall_ref[g, h] = st
                stb = st.astype(BF16)
                vn = u_ref[rows, cols] - _dot(w_ref[rows, cols], stb)
                vnb = vn.astype(BF16)
                vn_ref[rows, cols] = vnb
                o_ref[rows, cols] = _dot(qd_ref[rows, cols], stb) + _dot(mm_ref[rows, h * ck:(h + 1) * ck], vnb)
                gam_c = jnp.exp(aux_ref[last:last + 1, 2 * nh + h:2 * nh + h + 1])
                s_ref[h] = gam_c * st + _dot(kd_ref[rows, cols], vnb, TN)

    return pl.pallas_call(
        body, grid=(t // (gs * ck),),
        in_specs=_gdn_specs(t, (gw, gw, gw, gw, sq, LANES)),
        out_specs=_gdn_specs(t, (gw, gw)) + [pl.BlockSpec((gs, nh, hd, hd), lambda i: (i, 0, 0, 0))],
        out_shape=[jax.ShapeDtypeStruct((t, gw), F32), jax.ShapeDtypeStruct((t, gw), BF16),
                   jax.ShapeDtypeStruct((t // ck, nh, hd, hd), F32)],
        scratch_shapes=[pltpu.VMEM((nh, hd, hd), F32)],
        compiler_params=_cp(("arbitrary",)), name=name,
    )(u, w, qd, kd, mm, aux)


def _gdn_seq_bwd(do, w, qd, kd, mm, vn, s_all, aux, *, name):
    t = do.shape[0]
    hd, nh, ck, gs = GDN_HEAD_DIM, GDN_HEADS, GDN_CHUNK, GDN_STEP_CHUNKS
    gw = nh * hd
    sq = nh * ck
    nsteps = t // (gs * ck)

    def body(do_ref, w_ref, qd_ref, kd_ref, mm_ref, vn_ref, sall_ref, aux_ref, dvn_ref, dqd_ref, dkd_ref, dw_ref,
             dlast_ref, ds_ref):
        @pl.when(pl.program_id(0) == 0)
        def _():
            ds_ref[...] = jnp.zeros_like(ds_ref)

        lane = lax.broadcasted_iota(jnp.int32, (ck, LANES), 1)
        rowi = lax.broadcasted_iota(jnp.int32, (ck, LANES), 0)
        for g in reversed(range(gs)):
            rows = slice(g * ck, (g + 1) * ck)
            last = (g + 1) * ck - 1
            dlast = jnp.zeros((ck, LANES), F32)
            for h in range(nh):
                cols = _gdn_head_cols(h)
                st = sall_ref[g, h]
                stb = st.astype(BF16)
                dsn = ds_ref[h]
                dsb = dsn.astype(BF16)
                dob = do_ref[rows, cols].astype(BF16)
                dvn = _dot(mm_ref[rows, h * ck:(h + 1) * ck], dob, TN) + _dot(kd_ref[rows, cols], dsb)
                dvb = dvn.astype(BF16)
                dvn_ref[rows, cols] = dvn
                dqd_ref[rows, cols] = _dot(dob, stb, NT)
                dkd_ref[rows, cols] = _dot(vn_ref[rows, cols], dsb, NT)
                dw_ref[rows, cols] = -_dot(dvb, stb, NT)
                gam_c = jnp.exp(aux_ref[last:last + 1, 2 * nh + h:2 * nh + h + 1])
                dgam_c = jnp.sum(jnp.sum(dsn * st, axis=1, keepdims=True), axis=0, keepdims=True)
                dlast = dlast + jnp.where((rowi == ck - 1) & (lane == h), gam_c * dgam_c, 0.0)
                ds_ref[h] = _dot(qd_ref[rows, cols], dob, TN) + gam_c * dsn - _dot(w_ref[rows, cols], dvb, TN)
            dlast_ref[rows, :] = dlast

    return pl.pallas_call(
        body, grid=(nsteps,),
        in_specs=_gdn_specs(t, (gw, gw, gw, gw, sq, gw), reverse=True)
        + [pl.BlockSpec((gs, nh, hd, hd), lambda i: (nsteps - 1 - i, 0, 0, 0))] + _gdn_specs(t, (LANES,), reverse=True),
        out_specs=_gdn_specs(t, (gw, gw, gw, gw, LANES), reverse=True),
        out_shape=[jax.ShapeDtypeStruct((t, gw), F32)] * 4 + [jax.ShapeDtypeStruct((t, LANES), F32)],
        scratch_shapes=[pltpu.VMEM((nh, hd, hd), F32)],
        compiler_params=_cp(("arbitrary",)), name=name,
    )(do, w, qd, kd, mm, vn, s_all, aux)


def _gdn_local_bwd(qkvn, aux, aux_t, tinv, u, w, vn, do, dvn, dqd, dkd, dw, dlast, *, name):
    t = qkvn.shape[0]
    hd, nh, ck, gs = GDN_HEAD_DIM, GDN_HEADS, GDN_CHUNK, GDN_STEP_CHUNKS
    gw = nh * hd
    sq = nh * ck

    def body(x_ref, aux_ref, auxt_ref, tinv_ref, u_ref, w_ref, vn_ref, do_ref, dvn_ref, dqd_ref, dkd_ref, dw_ref,
             dlast_ref, dx_ref, daux_ref):
        lane = lax.broadcasted_iota(jnp.int32, (ck, LANES), 1)
        ones = jnp.ones((ck, LANES), F32)
        ii = lax.broadcasted_iota(jnp.int32, (ck, ck), 0)
        jj = lax.broadcasted_iota(jnp.int32, (ck, ck), 1)
        suffix = jnp.where(jj >= ii, 1.0, 0.0)
        for g0 in range(0, gs, GDN_ILP_CHUNKS_BWD):
            where = [(g, h) for g in range(g0, g0 + GDN_ILP_CHUNKS_BWD) for h in range(nh)]
            at = [(slice(g * ck, (g + 1) * ck), _gdn_head_cols(h)) for g, h in where]
            ins = [_gdn_chunk_inputs(x_ref, aux_ref, auxt_ref, g, h) for g, h in where]
            lcs = _gdn_local([(q, k, b, gc, gc_row) for q, k, _, b, gc, gc_row in ins], False)
            tinvs = [tinv_ref[slice(g * ck, (g + 1) * ck), h * ck:(h + 1) * ck] for g, h in where]
            dms = [jnp.where(lc["lower"], _bdot(do_ref[r, c], vn_ref[r, c], NT), 0.0) for lc, (r, c) in zip(lcs, at)]
            drvs = _dot3_many(tinvs, [dvn_ref[r, c] for r, c in at], TN)
            drks = _dot3_many(tinvs, [dw_ref[r, c] for r, c in at], TN)
            das = [jnp.where(lc["strict"], -(_bdot(drv, u_ref[r, c], NT) + _bdot(drk, w_ref[r, c], NT)), 0.0)
                   for lc, (r, c), drv, drk in zip(lcs, at, drvs, drks)]
            f_mats = [da * (i[3] * lc["kk"]) * lc["dmat"] + dm * lc["qk"] * lc["dmat"]
                      for i, lc, da, dm in zip(ins, lcs, das, dms)]
            col_sums = _dot3_many(f_mats, [ones] * len(where), TN)
            dgc_all = {g: dlast_ref[slice(g * ck, (g + 1) * ck), :] for g in range(g0, g0 + GDN_ILP_CHUNKS_BWD)}
            db_all = {g: jnp.zeros((ck, LANES), F32) for g in range(g0, g0 + GDN_ILP_CHUNKS_BWD)}
            e_mats = [da * lc["dmat"] * i[3] for i, lc, da in zip(ins, lcs, das)]
            dmds = [dm * lc["dmat"] for lc, dm in zip(lcs, dms)]
            dq_mm = [_bdot(dmd, i[1]) for i, dmd in zip(ins, dmds)]
            dk_mm = [_bdot(e, i[1]) + _bdot(e, i[1], TN) + _bdot(dmd, i[0], TN) for i, e, dmd in zip(ins, e_mats, dmds)]
            for n, ((g, h), (q, k, v, b, _, _), lc, (rows, cols)) in enumerate(zip(where, ins, lcs, at)):
                dmat, kk, gam, kdec = (lc[key] for key in ("dmat", "kk", "gam", "kdec"))
                drv, drk, da = drvs[n], drks[n], das[n]
                dqd_h, dkd_h = dqd_ref[rows, cols], dkd_ref[rows, cols]
                rs_rk = jnp.sum(drk * k, axis=-1, keepdims=True)
                db = (jnp.sum(drv * v, axis=-1, keepdims=True) + gam * rs_rk
                      + jnp.sum(da * kk * dmat, axis=-1, keepdims=True))
                dx_ref[rows, cols] = dq_mm[n] + gam * dqd_h
                dx_ref[rows, gw + cols.start:gw + cols.stop] = (b * gam) * drk + dk_mm[n] + kdec * dkd_h
                dx_ref[rows, 2 * gw + cols.start:2 * gw + cols.stop] = b * drv
                e_vec = jnp.sum(dkd_h * (kdec * k), axis=-1, keepdims=True)
                dgc = (b * gam * rs_rk + gam * jnp.sum(dqd_h * q, axis=-1, keepdims=True)
                       + jnp.sum(f_mats[n], axis=-1, keepdims=True) - col_sums[n][:, 0:1] - e_vec)
                is_last = lax.broadcasted_iota(jnp.int32, (ck, 1), 0) == ck - 1
                dgc = dgc + jnp.where(is_last, jnp.sum(e_vec, axis=0, keepdims=True), 0.0)
                dgc_all[g] = dgc_all[g] + jnp.where(lane == h, dgc, 0.0)
                db_all[g] = db_all[g] + jnp.where(lane == nh + h, db, 0.0)
            for g in dgc_all:
                daux_ref[slice(g * ck, (g + 1) * ck), :] = _dot3(suffix, dgc_all[g]) + db_all[g]

    return pl.pallas_call(
        body, grid=(t // (gs * ck),),
        in_specs=_gdn_specs(t, (3 * gw, LANES)) + [pl.BlockSpec((gs, 16, ck), lambda i: (i, 0, 0))]
        + _gdn_specs(t, (sq, gw, gw, gw, gw, gw, gw, gw, gw, LANES)),
        out_specs=_gdn_specs(t, (3 * gw, LANES)),
        out_shape=[jax.ShapeDtypeStruct((t, 3 * gw), F32), jax.ShapeDtypeStruct((t, LANES), F32)],
        compiler_params=_cp(("parallel",)), name=name,
    )(qkvn, aux, aux_t, tinv, u, w, vn, do, dvn, dqd, dkd, dw, dlast)


def _gdn_pre_bwd1(xc, dqkvn, daux, ab, alog_pad, dt_pad, dkv, dp, dp_col, *, name, ab_blk=0, tm=256):
    t, cw = xc.shape
    hd, nh = GDN_HEAD_DIM, GDN_HEADS
    gw = nh * hd
    tm = _blk(t, tm)

    kvw = dkv.shape[1]
    seg = kvw + AB_PAD
    assert dp_col % seg == 0

    def body(xc_ref, dy_ref, daux_ref, ab_ref, al_ref, dt_ref, dkv_ref, _, dxc_ref, dab_ref, dal_ref, ddt_ref):
        i = pl.program_id(0)
        xc = xc_ref[...]
        sg = _sigmoid(xc)
        s = xc * sg
        dsilu = sg * (1.0 + xc * (1.0 - sg))
        for h in range(2 * nh):
            xh = s[:, h * hd:(h + 1) * hd]
            scale = hd ** -0.5 if h < nh else 1.0
            dyh = dy_ref[:, h * hd:(h + 1) * hd] * scale
            r = lax.rsqrt(jnp.sum(xh * xh, axis=-1, keepdims=True) + L2_EPS)
            dxh = r * dyh - xh * (r * r * r) * jnp.sum(dyh * xh, axis=-1, keepdims=True)
            dxc_ref[:, h * hd:(h + 1) * hd] = dxh * dsilu[:, h * hd:(h + 1) * hd]
        dxc_ref[:, 2 * gw:] = dy_ref[:, 2 * gw:] * dsilu[:, 2 * gw:]
        abv = ab_ref[...]
        dauxv = daux_ref[...]
        lane = lax.broadcasted_iota(jnp.int32, abv.shape, 1)
        is_a = lane < nh
        is_b = (lane >= nh) & (lane < 2 * nh)
        pre = abv + dt_ref[...]
        neg_ea = -jnp.exp(al_ref[...])
        d_a = jnp.where(is_a, dauxv * neg_ea * _sigmoid(pre), 0.0)
        beta = _sigmoid(abv)
        d_b = jnp.where(is_b, dauxv * beta * (1.0 - beta), 0.0)
        dab_ref[:, :kvw] = dkv_ref[...]
        dab_ref[:, kvw:kvw + LANES] = (d_a + d_b).astype(dab_ref.dtype)
        dab_ref[:, kvw + LANES:] = jnp.zeros((tm, AB_PAD - LANES), dab_ref.dtype)
        dal = jnp.sum(jnp.where(is_a, dauxv * neg_ea * _softplus(pre), 0.0), axis=0, keepdims=True)
        ddt = jnp.sum(d_a, axis=0, keepdims=True)

        @pl.when(i == 0)
        def _():
            dal_ref[...] = dal
            ddt_ref[...] = ddt

        @pl.when(i > 0)
        def _():
            dal_ref[...] += dal
            ddt_ref[...] += ddt

    row = lambda c: pl.BlockSpec((tm, c), lambda i: (i, 0))
    vec = pl.BlockSpec((1, LANES), lambda i: (0, 0))
    return pl.pallas_call(
        body, grid=(t // tm,),
        in_specs=[row(cw), row(cw), row(LANES), pl.BlockSpec((tm, LANES), lambda i: (i, ab_blk)), vec, vec, row(kvw),
                  ANY_SPEC],
        out_specs=[row(cw), pl.BlockSpec((tm, seg), lambda i: (i, dp_col // seg)), vec, vec],
        out_shape=[jax.ShapeDtypeStruct((t, cw), F32), jax.ShapeDtypeStruct(dp.shape, dp.dtype),
                   jax.ShapeDtypeStruct((1, LANES), F32), jax.ShapeDtypeStruct((1, LANES), F32)],
        input_output_aliases={7: 1}, compiler_params=_cp(("arbitrary",)), name=name,
    )(xc, dqkvn, daux, ab, alog_pad, dt_pad, dkv, dp)


def _gdn_pre_bwd2(dxc, qkvb, conv_w, dp, dp_col, *, name, tm=512):
    t, cw = dxc.shape
    tm = _blk(t, tm)
    hb = tm // HALO
    nblk = t // tm
    cg = GDN_HEADS * GDN_HEAD_DIM
    assert cw % cg == 0 and dp_col % cg == 0
    col0 = dp_col // cg

    def body(d_ref, dn_ref, x_ref, xp_ref, w_ref, _, dx_ref, dw_ref):
        i = pl.program_id(1)
        dcur = d_ref[...]
        dnxt = jnp.where(i < nblk - 1, dn_ref[...], 0.0)
        cur = x_ref[...]
        prev = jnp.where(i > 0, xp_ref[...], 0.0)
        dx = None
        dws = []
        for tap in range(GDN_CONV):
            j = GDN_CONV - 1 - tap
            term = w_ref[tap:tap + 1, :] * _shift_up(dcur, dnxt, j)
            dx = term if dx is None else dx + term
            dws.append(jnp.sum(dcur * _shift_down(cur, prev, j), axis=0, keepdims=True))
        dx_ref[...] = dx.astype(dx_ref.dtype)
        dw = jnp.concatenate(dws, axis=0)

        @pl.when(i == 0)
        def _():
            dw_ref[...] = dw

        @pl.when(i > 0)
        def _():
            dw_ref[...] += dw

    row = pl.BlockSpec((tm, cg), lambda c, i: (i, c))
    wsp = pl.BlockSpec((GDN_CONV, cg), lambda c, i: (0, c))
    return pl.pallas_call(
        body, grid=(cw // cg, nblk),
        in_specs=[row, pl.BlockSpec((HALO, cg), lambda c, i: (jnp.minimum((i + 1) * hb, t // HALO - 1), c)),
                  row, pl.BlockSpec((HALO, cg), lambda c, i: (jnp.maximum(i * hb - 1, 0), c)), wsp, ANY_SPEC],
        out_specs=[pl.BlockSpec((tm, cg), lambda c, i: (i, col0 + c)), wsp],
        out_shape=[jax.ShapeDtypeStruct(dp.shape, dp.dtype), jax.ShapeDtypeStruct((GDN_CONV, cw), F32)],
        input_output_aliases={5: 0}, compiler_params=_cp(("arbitrary", "arbitrary")), name=name,
    )(dxc, dxc, qkvb, qkvb, conv_w, dp)


def _gdn_post_fwd(o, z, norm_w, *, name, tm=512):
    t, gw = o.shape
    hd, nh = GDN_HEAD_DIM, GDN_HEADS
    tm = _blk(t, tm)

    def body(o_ref, z_ref, w_ref, y_ref):
        zv = z_ref[...]
        sz = zv * _sigmoid(zv)
        for h in range(nh):
            oh = o_ref[:, h * hd:(h + 1) * hd]
            r = lax.rsqrt(jnp.mean(oh * oh, axis=-1, keepdims=True) + RMS_EPS)
            y_ref[:, h * hd:(h + 1) * hd] = (oh * r * w_ref[...] * sz[:, h * hd:(h + 1) * hd]).astype(y_ref.dtype)

    row = pl.BlockSpec((tm, gw), lambda i: (i, 0))
    return pl.pallas_call(
        body, grid=(t // tm,), in_specs=[row, row, pl.BlockSpec((1, hd), lambda i: (0, 0))], out_specs=row,
        out_shape=jax.ShapeDtypeStruct((t, gw), BF16), compiler_params=_cp(("parallel",)), name=name,
    )(o, z, norm_w)


def _gdn_post_bwd(dy, o, z, norm_w, dp, dp_col, *, name, tm=512):
    t, gw = o.shape
    hd, nh = GDN_HEAD_DIM, GDN_HEADS
    tm = _blk(t, tm)

    def body(dy_ref, o_ref, z_ref, w_ref, _, do_ref, dz_ref, dw_ref):
        i = pl.program_id(0)
        zv = z_ref[...]
        sg = _sigmoid(zv)
        sz = zv * sg
        dsz = sg * (1.0 + zv * (1.0 - sg))
        dw = None
        for h in range(nh):
            sl = slice(h * hd, (h + 1) * hd)
            oh = o_ref[:, sl]
            dyh = dy_ref[:, sl].astype(F32)
            r = lax.rsqrt(jnp.mean(oh * oh, axis=-1, keepdims=True) + RMS_EPS)
            xh = oh * r
            dz_ref[:, sl] = (dyh * xh * w_ref[...] * dsz[:, sl]).astype(dz_ref.dtype)
            dn = dyh * sz[:, sl]
            dxh = dn * w_ref[...]
            do_ref[:, sl] = r * (dxh - xh * jnp.mean(dxh * xh, axis=-1, keepdims=True))
            part = jnp.sum(dn * xh, axis=0, keepdims=True)
            dw = part if dw is None else dw + part

        @pl.when(i == 0)
        def _():
            dw_ref[...] = dw

        @pl.when(i > 0)
        def _():
            dw_ref[...] += dw

    row = pl.BlockSpec((tm, gw), lambda i: (i, 0))
    vec = pl.BlockSpec((1, hd), lambda i: (0, 0))
    return pl.pallas_call(
        body, grid=(t // tm,), in_specs=[row, row, row, vec, ANY_SPEC],
        out_specs=[row, pl.BlockSpec((tm, gw), lambda i: (i, dp_col // gw)), vec],
        out_shape=[jax.ShapeDtypeStruct((t, gw), F32), jax.ShapeDtypeStruct(dp.shape, dp.dtype),
                   jax.ShapeDtypeStruct((1, hd), F32)],
        input_output_aliases={4: 1}, compiler_params=_cp(("arbitrary",)), name=name,
    )(dy, o, z, norm_w, dp)


IN_NAMES = ("q_a", "kv_a", "qkv_b", "ab", "z", "q_c", "gates")
CAT_NAMES = ("gates", "q_a", "qkv_b", "z", "q_c", "kv_a", "ab")
AB_PAD = 256


def _in_widths(d):
    gw = GDN_HEADS * GDN_HEAD_DIM
    return dict(q_a=SWA_Q_HEADS * SWA_HEAD_DIM, kv_a=2 * SWA_KV_HEADS * SWA_HEAD_DIM, qkv_b=3 * gw, ab=2 * GDN_HEADS,
                z=gw, q_c=XA_HEADS * XA_HEAD_DIM, gates=3 * d)


def _ranges(names, widths):
    out, start = {}, 0
    for k in names:
        out[k] = (start, widths[k])
        start += widths[k]
    return out, start


def _cat_ranges(d):
    widths = dict(_in_widths(d), ab=AB_PAD)
    return _ranges(CAT_NAMES, widths)


def _to_cat(shards):
    ns, d, n = shards.shape
    src, _ = _ranges(IN_NAMES, _in_widths(d))
    cols = []
    for k in CAT_NAMES:
        lo, hi = src[k][0], src[k][0] + src[k][1]
        for s in range(ns):
            a, b = max(lo, s * n), min(hi, (s + 1) * n)
            if a < b:
                cols.append(shards[s][:, a - s * n:b - s * n])
    cols.append(jnp.zeros((d, AB_PAD - src["ab"][1]), shards.dtype))
    return jnp.concatenate(cols, axis=1)


def _from_cat(w_cat):
    d = w_cat.shape[0]
    src, total = _ranges(IN_NAMES, _in_widths(d))
    cat, _ = _cat_ranges(d)
    n = total // N_SHARDS
    shards = []
    for s in range(N_SHARDS):
        pieces = []
        for k in IN_NAMES:
            a, b = max(s * n, src[k][0]), min((s + 1) * n, src[k][0] + src[k][1])
            if a < b:
                pieces.append(w_cat[:, cat[k][0] + a - src[k][0]:cat[k][0] + b - src[k][0]])
        shards.append(jnp.concatenate(pieces, axis=1))
    return jnp.stack(shards)


def _pad_cols(a, width):
    return jnp.pad(a, ((0, 0), (0, width - a.shape[1])))


def _relu2_epilogue(acc):
    r = jnp.maximum(acc, 0.0)
    return acc, r * r


def _add_epilogue(acc, res):
    return (acc + res,)


def _drelu2_epilogue(acc, u):
    return (acc * (2.0 * jnp.maximum(u.astype(F32), 0.0)),)


def _local_step(x, mem, tgt, wts, small, comm=None):
    t, d = x.shape
    nh = GDN_HEADS
    w_cat = wts["w_cat"]
    cat, cat_w = _cat_ranges(d)
    assert w_cat.shape == (d, cat_w)
    alog_pad = _pad_cols(small["a_log"], LANES)
    dt_pad = _pad_cols(small["dt_bias"], LANES)
    kvw = cat["kv_a"][1]
    assert cat["ab"][0] == cat["kv_a"][0] + kvw
    ab_blk = kvw // LANES

    n = _rms_fwd(x, small["g_mix"], name="rms_mix")
    q_a = _mm(n, w_cat, b_window=cat["q_a"], out_dtypes=(BF16,), name="in_q_a")
    kv_a, ab = _mm(n, w_cat, b_window=(cat["kv_a"][0], kvw + AB_PAD), out_dtypes=(BF16, F32), name="in_kv_ab")
    qkvb = _mm(n, w_cat, b_window=cat["qkv_b"], tn=512, name="in_qkv_b")
    z = _mm(n, w_cat, b_window=cat["z"], name="in_z")
    q_c = _mm(n, w_cat, b_window=cat["q_c"], out_dtypes=(BF16,), name="in_q_c")
    if comm is None:
        gates = _mm(n, w_cat, b_window=cat["gates"], name="in_gates")
        y_a, lse = _swa_fwd(q_a, kv_a, small["sinks"], name="swa_fwd")
    else:
        gates, landed_mlp = _mm(n, w_cat, b_window=cat["gates"], name="in_gates",
                                exchange=comm.gather_exchange(comm.MLP[1:]))
        y_a, lse, landed = _swa_fwd(q_a, kv_a, small["sinks"], name="swa_fwd", exchange=comm.gather_exchange(comm.MLP[:1]))
        landed_mlp = landed + landed_mlp
    xc, qkvn, aux = _gdn_pre_fwd(qkvb, small["conv_w"], ab, alog_pad, dt_pad, ab_blk=ab_blk, name="gdn_pre_fwd")
    aux_t = aux[:, :16].reshape(t // GDN_CHUNK, GDN_CHUNK, 16).transpose(0, 2, 1)
    if comm is None:
        gdn_u, gdn_w, gdn_qd, gdn_kd, gdn_mm, gdn_tinv = _gdn_local_fwd(qkvn, aux, aux_t, name="gdn_local_fwd")
    else:
        gdn_u, gdn_w, gdn_qd, gdn_kd, gdn_mm, gdn_tinv, landed_mid = _gdn_local_fwd(
            qkvn, aux, aux_t, name="gdn_local_fwd", exchange=comm.gather_exchange(comm.mid))
        wts = dict(wts, **comm.gathered(comm.mid, landed_mid, "mid"))
    o_b, gdn_vn, s_all = _gdn_seq_fwd(gdn_u, gdn_w, gdn_qd, gdn_kd, gdn_mm, aux, name="gdn_seq_fwd")
    y_b = _gdn_post_fwd(o_b, z, small["gdn_norm_w"], name="gdn_post_fwd")
    nmem = _rms_fwd(mem, small["g_mem"], name="rms_mem")
    mkv = _mm(nmem, wts["w_mem_kv"], out_dtypes=(BF16,), name="mem_kv")
    y_c = _xa_fwd(q_c, mkv, name="xa_fwd")
    ys = (y_a, y_b, y_c)
    w_ups = (wts["w_swa_up"], wts["w_gdn_up"], wts["w_xa_up"])
    merged = _merge_fwd(ys, w_ups, gates, name="merge_fwd")
    if comm is None:
        h1 = _mm(merged, wts["w_out"], extras=(x,), epilogue=_add_epilogue, name="out_proj")
    else:
        h1, whole = _mm(merged, wts["w_out"], extras=(x,), epilogue=_add_epilogue, name="out_proj",
                        exchange=_gather_pass_on(landed_mlp))
        wts = dict(wts, **comm.as_weights(comm.MLP, whole))
    n2 = _rms_fwd(h1, small["g_mlp"], name="rms_mlp")
    u, act = _mm(n2, wts["w_mlp_in"], b_sharded=True, out_dtypes=(BF16, BF16), epilogue=_relu2_epilogue, name="mlp_in")
    h2 = _mm(act, wts["w_mlp_out"], extras=(h1,), epilogue=_add_epilogue, name="mlp_out")
    dh2, dh2_b, dg_final, loss = _final_loss(h2, small["g_final"], tgt, name="final_loss")

    grads = {"g_final": dg_final}
    du = _mm(dh2_b, wts["w_mlp_out"], tb=True, out_dtypes=(BF16,), extras=(u,), epilogue=_drelu2_epilogue, name="d_mlp_act")
    grads["w_mlp_out"] = _mm(act, dh2_b, ta=True, out_dtypes=(BF16,), name="dw_mlp_out")
    grads["w_mlp_in"] = _mm(n2, du, ta=True, out_sharded=True, out_dtypes=(BF16,), name="dw_mlp_in")
    if comm is None:
        dn2 = _mm(du, wts["w_mlp_in"], tb=True, b_sharded=True, name="d_mlp_in")
    else:
        g_mlp = [comm.shard_major(k, grads.pop(k)) for k in comm.MLP]
        dn2, sib_mlp = _mm(du, wts["w_mlp_in"], tb=True, b_sharded=True, name="d_mlp_in", exchange=_sibling_halves(g_mlp))
        s1_mlp = comm.pair_sums(g_mlp, "mlp", sib_mlp)
    dh1, dh1_b, grads["g_mlp"] = _rms_bwd(dn2, h1, small["g_mlp"], dh2, name="rms_mlp_bwd")
    dmerged = _mm(dh1_b, wts["w_out"], tb=True, name="d_out_proj")
    grads["w_out"] = _mm(merged, dh1_b, ta=True, out_dtypes=(BF16,), name="dw_out")
    *dus, dp = _merge_bwd(ys, w_ups, gates, dmerged, cat_w, name="merge_bwd")
    dys = []
    for y, du_i, w_up, key in zip(ys, dus, w_ups, ("w_swa_up", "w_gdn_up", "w_xa_up")):
        dys.append(_mm(du_i, w_up, tb=True, b_sharded=True, out_dtypes=(BF16,), name="d_" + key))
        grads[key] = _mm(y, du_i, ta=True, out_sharded=True, out_dtypes=(BF16,), name="dw_" + key[2:])
    dp, dkv_a, grads["sinks"] = _swa_bwd(q_a, kv_a, small["sinks"], y_a, lse, dys[0], dp, cat["q_a"][0], name="swa_bwd")
    do_b, dp, grads["gdn_norm_w"] = _gdn_post_bwd(dys[1], o_b, z, small["gdn_norm_w"], dp, cat["z"][0],
                                                  name="gdn_post_bwd")
    dvn, dqd, dkd, dw_, dlast = _gdn_seq_bwd(do_b, gdn_w, gdn_qd, gdn_kd, gdn_mm, gdn_vn, s_all, aux, name="gdn_seq_bwd")
    dqkvn, daux = _gdn_local_bwd(qkvn, aux, aux_t, gdn_tinv, gdn_u, gdn_w, gdn_vn, do_b, dvn, dqd, dkd, dw_, dlast,
                                 name="gdn_local_bwd")
    dxc, dp, dalog, ddt = _gdn_pre_bwd1(xc, dqkvn, daux, ab, alog_pad, dt_pad, dkv_a, dp, cat["kv_a"][0], ab_blk=ab_blk,
                                        name="gdn_pre_bwd1")
    grads["a_log"], grads["dt_bias"] = dalog[:, :nh], ddt[:, :nh]
    dp, grads["conv_w"] = _gdn_pre_bwd2(dxc, qkvb, small["conv_w"], dp, cat["qkv_b"][0], name="gdn_pre_bwd2")
    dp, dmkv = _xa_bwd(q_c, mkv, dys[2], dp, cat["q_c"][0], name="xa_bwd")
    grads["w_mem_kv"] = _mm(nmem, dmkv, ta=True, out_dtypes=(BF16,), name="dw_mem_kv")
    dnmem = _mm(dmkv, wts["w_mem_kv"], tb=True, name="d_mem_kv")
    _, _, grads["g_mem"] = _rms_bwd(dnmem, mem, small["g_mem"], jnp.zeros_like(mem), name="rms_mem_bwd")
    if comm is None:
        grads["w_cat"] = _mm(n, dp, ta=True, out_dtypes=(BF16,), name="dw_in")
        dn = _mm(dp, w_cat, tb=True, name="d_in_proj")
    else:
        s1_mid = comm.pair_sums([comm.shard_major(k, grads.pop(k)) for k in comm.mid], "mid")
        dw_cat, rcv_mlp = _mm(n, dp, ta=True, out_dtypes=(BF16,), name="dw_in", exchange=_chip_exchange(s1_mlp))
        s1_in = comm.pair_sums([_from_cat(dw_cat)], "in")
        dn, rcv_rest = _mm(dp, w_cat, tb=True, name="d_in_proj", exchange=_chip_exchange(s1_in + s1_mid))
        reduced = comm.finish(s1_in + s1_mid + s1_mlp, rcv_rest + rcv_mlp)
        grads.update(zip(["w_in"] + comm.mid + list(comm.MLP), reduced))
    dx, _, grads["g_mix"] = _rms_bwd(dn, x, small["g_mix"], dh1, name="rms_mix_bwd")
    return loss, dx, grads


HBM_SPEC = pl.BlockSpec(memory_space=pltpu.HBM)
VMEM_SPEC = pl.BlockSpec(memory_space=pltpu.VMEM)
N_CHIPS = N_SHARDS
N_DEV = 8
DMA_CHUNK_BYTES = 1 << 20


def _place():
    return lax.axis_index("x"), lax.axis_index("y"), lax.axis_index("c")


def _other_chips(x, y):
    return [(1 - x, y), (x, 1 - y), (1 - x, 1 - y)]


def _n_chunks(rows, row_bytes):
    n = 1
    while rows % (2 * n) == 0 and (rows // (2 * n)) % 16 == 0 and (rows // n) * row_bytes > DMA_CHUNK_BYTES:
        n *= 2
    return n


def _sem_scratch(n_remote, n_local):
    return [pltpu.SemaphoreType.DMA((max(n_remote, 1),)), pltpu.SemaphoreType.DMA((max(n_remote, 1),)),
            pltpu.SemaphoreType.DMA((max(n_local, 1),))]


def _all_gather_weights(shards, *, name):
    first = _exchange_call(_gather_over_ici(shards), name=name + "_ici")
    return _exchange_call(_gather_pass_on(first), name=name + "_pass")


def _gather_over_ici(shards):
    plan = _half_chunks(shards, 0)

    def copies_of(in_refs, out_refs, place):
        x, y, c = place
        remote, local = [], []
        for i, r0, nr in plan:
            rh = shards[i].shape[0] // 2
            mine = pl.ds(c * rh + r0, nr)
            for chip in _other_chips(x, y):
                remote.append((in_refs[i].at[mine], out_refs[i].at[2 * x + y, mine], (*chip, c)))
            for half in range(2):
                rows = pl.ds(half * rh + r0, nr)
                local.append((in_refs[i].at[rows], out_refs[i].at[2 * x + y, rows]))
        return remote, local

    shapes = tuple(jax.ShapeDtypeStruct((N_CHIPS, *s.shape), s.dtype) for s in shards)
    return Exchange(tuple(shards), shapes, 3 * len(plan), 2 * len(plan), copies_of)


def _gather_pass_on(arrived):
    plan = _half_chunks([jax.ShapeDtypeStruct(a.shape[1:], a.dtype) for a in arrived], 0)

    def copies_of(in_refs, out_refs, place):
        x, y, c = place
        remote = []
        for i, r0, nr in plan:
            mine = pl.ds(c * (arrived[i].shape[1] // 2) + r0, nr)
            for chip in _other_chips(x, y):
                rows = out_refs[i].at[2 * chip[0] + chip[1], mine]
                remote.append((rows, rows, (x, y, 1 - c)))
        return remote, []

    shapes = tuple(jax.ShapeDtypeStruct(a.shape, a.dtype) for a in arrived)
    return Exchange(tuple(arrived), shapes, 3 * len(plan), 0, copies_of, tuple((i, i) for i in range(len(arrived))))


def _exchange_call(ex, *, name):
    n_in, n_out = len(ex.ins), len(ex.out_shapes)

    def body(*refs):
        cps = _exchange_copies(ex, refs[:n_in], refs[n_in:n_in + n_out], refs[n_in + n_out:])
        for cp in cps:
            cp.start()
        for cp in cps:
            cp.wait()

    return pl.pallas_call(
        body, out_shape=list(ex.out_shapes), in_specs=[HBM_SPEC] * n_in, out_specs=[HBM_SPEC] * n_out,
        scratch_shapes=_sem_scratch(ex.n_remote, ex.n_local), input_output_aliases=dict(ex.aliases), name=name,
    )(*ex.ins)


def _half_chunks(arrs, row_axis):
    plan = []
    for i, a in enumerate(arrs):
        rh = a.shape[row_axis] // 2
        row_bytes = a.dtype.itemsize * math.prod(a.shape) // a.shape[row_axis]
        nch = _n_chunks(rh, row_bytes)
        plan += [(i, q * (rh // nch), rh // nch) for q in range(nch)]
    return plan


def _sibling_halves(gs):
    plan = _half_chunks(gs, 1)

    def copies_of(in_refs, out_refs, place):
        x, y, c = place
        out = []
        for i, r0, nr in plan:
            rh = gs[i].shape[1] // 2
            out.append((in_refs[i].at[:, pl.ds((1 - c) * rh + r0, nr), :], out_refs[i].at[:, pl.ds(r0, nr), :],
                        (x, y, 1 - c)))
        return out, []

    shapes = tuple(jax.ShapeDtypeStruct((g.shape[0], g.shape[1] // 2, g.shape[2]), g.dtype) for g in gs)
    return Exchange(tuple(gs), shapes, len(plan), 0, copies_of)


def _chip_exchange(s1s):
    plan = _half_chunks([jax.ShapeDtypeStruct((2 * s.shape[1], s.shape[2]), s.dtype) for s in s1s], 0)

    def copies_of(in_refs, out_refs, place):
        x, y, c = place
        out = []
        for i, r0, nr in plan:
            for j, chip in enumerate(_other_chips(x, y)):
                out.append((in_refs[i].at[2 * chip[0] + chip[1], pl.ds(r0, nr), :], out_refs[i].at[j, pl.ds(r0, nr), :],
                            (*chip, c)))
        return out, []

    shapes = tuple(jax.ShapeDtypeStruct((3, *s.shape[1:]), s.dtype) for s in s1s)
    return Exchange(tuple(s1s), shapes, 3 * len(plan), 0, copies_of)


def _join_halves(gs, *, name):
    plan = _half_chunks(gs, 0)

    def copies_of(in_refs, out_refs, place):
        x, y, c = place
        out = []
        for i, r0, nr in plan:
            rows = out_refs[i].at[pl.ds(c * (gs[i].shape[0] // 2) + r0, nr), :]
            out.append((rows, rows, (x, y, 1 - c)))
        return out, []

    shapes = tuple(jax.ShapeDtypeStruct(g.shape, g.dtype) for g in gs)
    aliases = tuple((i, i) for i in range(len(gs)))
    return _exchange_call(Exchange(tuple(gs), shapes, len(plan), 0, copies_of, aliases), name=name)


def _row_block(rows, cols):
    tb = rows
    while tb % 32 == 0 and tb * cols * 4 > (2 << 20):
        tb //= 2
    return tb


def _pair_sum(g, sib, core, *, name):
    ns, r, c = g.shape
    rh = r // 2
    tb = _row_block(rh, c)
    nb = rh // tb

    def body(core_ref, g_ref, s_ref, o_ref):
        o_ref[...] = (g_ref[...].astype(F32) + s_ref[...].astype(F32)).astype(o_ref.dtype)

    mine = pl.BlockSpec((None, tb, c), lambda s, i, core_ref: (s, core_ref[0] * nb + i, 0))
    half = pl.BlockSpec((None, tb, c), lambda s, i, core_ref: (s, i, 0))
    return pl.pallas_call(
        body, grid_spec=pltpu.PrefetchScalarGridSpec(num_scalar_prefetch=1, grid=(ns, nb), in_specs=[mine, half],
                                                     out_specs=half),
        out_shape=jax.ShapeDtypeStruct((ns, rh, c), BF16), compiler_params=_cp(("parallel", "parallel")), name=name,
    )(core, g, sib)


def _chip_sum(s1, rcv, where, *, name):
    _, rh, c = s1.shape
    tb = _row_block(rh, c)
    nb = rh // tb

    def body(where_ref, own_ref, r0_ref, r1_ref, r2_ref, o_ref):
        acc = own_ref[...].astype(F32)
        for r in (r0_ref, r1_ref, r2_ref):
            acc = acc + r[...].astype(F32)
        o_ref[...] = acc

    own = pl.BlockSpec((None, tb, c), lambda i, w: (w[1], i, 0))
    got = [pl.BlockSpec((None, tb, c), functools.partial(lambda i, w, j: (j, i, 0), j=j)) for j in range(3)]
    return pl.pallas_call(
        body, grid_spec=pltpu.PrefetchScalarGridSpec(
            num_scalar_prefetch=1, grid=(nb,), in_specs=[own] + got,
            out_specs=pl.BlockSpec((tb, c), lambda i, w: (w[0] * nb + i, 0))),
        out_shape=jax.ShapeDtypeStruct((2 * rh, c), F32), compiler_params=_cp(("parallel",)), name=name,
    )(where, s1, rcv, rcv, rcv)


def _all_gather_small(blk, *, name):
    r = blk.shape[0]

    def body(b_ref, out_ref, send_sems, recv_sems):
        x, y, c = _place()
        me = 4 * x + 2 * y + c
        out_ref[me] = b_ref[...]
        sends = []
        for k in range(1, N_DEV):
            peer = (x ^ (k >> 2), y ^ ((k >> 1) & 1), c ^ (k & 1))
            sends.append(pltpu.make_async_remote_copy(src_ref=b_ref, dst_ref=out_ref.at[me], send_sem=send_sems.at[k - 1],
                                                      recv_sem=recv_sems.at[k - 1], device_id=peer, device_id_type=MESH))
        for cp in sends:
            cp.start()
        for k in range(1, N_DEV):
            rows = out_ref.at[me ^ k]
            pltpu.make_async_remote_copy(src_ref=rows, dst_ref=rows, send_sem=send_sems.at[k - 1],
                                         recv_sem=recv_sems.at[k - 1], device_id=(x, y, c), device_id_type=MESH).wait_recv()
        for cp in sends:
            cp.wait_send()

    return pl.pallas_call(
        body, out_shape=jax.ShapeDtypeStruct((N_DEV, r, LANES), blk.dtype), in_specs=[VMEM_SPEC], out_specs=VMEM_SPEC,
        scratch_shapes=[pltpu.SemaphoreType.DMA((N_DEV - 1,)), pltpu.SemaphoreType.DMA((N_DEV - 1,))],
        name=name,
    )(blk)


def _sum_rows(parts, out_dtype, *, name, tb=1024):
    rows = parts[0].shape[0]
    tb = _blk(rows, tb)

    def body(*refs):
        acc = refs[0][...].astype(F32)
        for r in refs[1:-1]:
            acc = acc + r[...].astype(F32)
        refs[-1][...] = acc.astype(refs[-1].dtype)

    spec = pl.BlockSpec((tb, LANES), lambda i: (i, 0))
    return pl.pallas_call(
        body, grid=(rows // tb,), in_specs=[spec] * len(parts), out_specs=spec,
        out_shape=jax.ShapeDtypeStruct((rows, LANES), out_dtype), compiler_params=_cp(("parallel",)), name=name,
    )(*parts)


BIG = (
    ("w_in", 1), ("w_mem_kv", 0), ("w_swa_up", 1), ("w_gdn_up", 1), ("w_xa_up", 1), ("w_out", 0), ("w_mlp_in", 1),
    ("w_mlp_out", 0))


class _Comm:
    MLP = ("w_mlp_in", "w_mlp_out")

    def __init__(self, late_shards, core, where):
        self.axis = dict(BIG)
        self.late_shards = late_shards
        self.mid = [k for k in late_shards if k not in self.MLP]
        self.core, self.where = core, where

    def gather_exchange(self, names):
        return _gather_over_ici([self.late_shards[k] for k in names])

    def as_weights(self, names, whole):
        return {k: (g.reshape(-1, g.shape[2]) if self.axis[k] == 0 else g) for k, g in zip(names, whole)}

    def gathered(self, names, landed, tag):
        return self.as_weights(names, _exchange_call(_gather_pass_on(landed), name=f"ag_{tag}_pass"))

    def shard_major(self, k, grad):
        return grad.reshape(N_CHIPS, -1, grad.shape[-1]) if self.axis[k] == 0 else grad

    def pair_sums(self, gs, tag, sibs=None):
        if sibs is None:
            sibs = _exchange_call(_sibling_halves(gs), name=f"rs_sibling_{tag}")
        return [_pair_sum(g, s, self.core, name=f"rs_pair_sum_{tag}{i}") for i, (g, s) in enumerate(zip(gs, sibs))]

    def finish(self, s1s, rcvs):
        halves = [_chip_sum(s1, rcv, self.where, name=f"rs_chip_sum_{i}") for i, (s1, rcv) in enumerate(zip(s1s, rcvs))]
        return _join_halves(halves, name="rs_join_halves")
SMALL = ("g_mix", "sinks", "a_log", "dt_bias", "gdn_norm_w", "g_mem", "g_mlp", "g_final")


def _rows128(a, rows):
    flat = a.reshape(-1)
    return jnp.pad(flat, (0, rows * LANES - flat.shape[0])).reshape(rows, LANES)


def kernel(x, mem, g_mix, w_in, sinks, conv_w, a_log, dt_bias, gdn_norm_w, g_mem, w_mem_kv, w_swa_up, w_gdn_up, w_xa_up, w_out, g_mlp, w_mlp_in, w_mlp_out, g_final, loss_target, m_g_mix, m_w_in, m_sinks, m_conv_w, m_a_log, m_dt_bias, m_gdn_norm_w, m_g_mem, m_w_mem_kv, m_w_swa_up, m_w_gdn_up, m_w_xa_up, m_w_out, m_g_mlp, m_w_mlp_in, m_w_mlp_out, m_g_final, v_g_mix, v_w_in, v_sinks, v_conv_w, v_a_log, v_dt_bias, v_gdn_norm_w, v_g_mem, v_w_mem_kv, v_w_swa_up, v_w_gdn_up, v_w_xa_up, v_w_out, v_g_mlp, v_w_mlp_in, v_w_mlp_out, v_g_final):
    given = dict(locals())
    xi, yi, ci = _place()
    chip = 2 * xi + yi
    core = jnp.reshape(ci, (1,)).astype(jnp.int32)
    where = jnp.stack([ci, chip]).astype(jnp.int32)

    shards = {k: given[k][0].astype(BF16) for k, _ in BIG}
    wts = {"w_cat": _to_cat(_all_gather_weights([shards.pop("w_in")], name="ag_w_in")[0])}
    comm = _Comm(shards, core, where)
    conv_shard = conv_w[0]
    conv_rows = -(-conv_shard.size // (8 * LANES)) * 8
    conv_all = _all_gather_small(_rows128(conv_shard, conv_rows), name="ag_conv")
    conv_full = jnp.concatenate(
        [conv_all[2 * s].reshape(-1)[:conv_shard.size].reshape(conv_shard.shape) for s in range(N_CHIPS)], axis=1)

    small = {k: given[k].reshape(1, -1) for k in SMALL}
    small["conv_w"] = conv_full
    loss_row, dx, grads = _local_step(x[0], mem[0], loss_target[0], wts, small, comm)
    big_grads = {k: grads[k] for k, _ in BIG}

    layout = [("loss", loss_row[:, :1])] + [(k, grads[k]) for k in SMALL] + [("conv_w", grads["conv_w"])]
    rows = [-(-a.size // LANES) for _, a in layout]
    blk_rows = -(-sum(rows) // 8) * 8
    blk = jnp.concatenate([_rows128(a.astype(F32), n) for (_, a), n in zip(layout, rows)]
                          + [jnp.zeros((blk_rows - sum(rows), LANES), F32)], axis=0)
    gathered = _all_gather_small(blk, name="ag_small_grads")
    reduced = _sum_rows([gathered[i] for i in range(N_DEV)], F32, name="small_grad_sum")
    small_grads, start = {}, 0
    for (k, a), n in zip(layout, rows):
        small_grads[k] = reduced[start:start + n].reshape(-1)[:a.size].reshape(a.shape)
        start += n
    loss = small_grads["loss"].reshape(())
    cw = conv_shard.shape[1]
    conv_grad = lax.dynamic_slice_in_dim(small_grads["conv_w"], chip * cw, cw, axis=1)

    names = ["g_mix", "w_in", "sinks", "conv_w", "a_log", "dt_bias", "gdn_norm_w", "g_mem", "w_mem_kv", "w_swa_up",
             "w_gdn_up", "w_xa_up", "w_out", "g_mlp", "w_mlp_in", "w_mlp_out", "g_final"]
    out_g, out_d, out_m, out_v = [], [], [], []
    for k in names:
        w, m, v = given[k], given["m_" + k], given["v_" + k]
        if k in big_grads:
            g2 = big_grads[k]
        elif k == "conv_w":
            g2 = conv_grad
        else:
            g2 = small_grads[k]
        as_given = (lambda a: a.reshape(1, -1)) if w.ndim == 1 else (lambda a: a)
        delta, new_m, new_v = _adamw(as_given(w), g2, as_given(m), as_given(v), name="adamw_" + k)
        out_g.append(g2.reshape(w.shape))
        out_d.append(delta.reshape(w.shape))
        out_m.append(new_m.reshape(w.shape))
        out_v.append(new_v.reshape(w.shape))
    return (loss, dx[None], *out_g, *out_d, *out_m, *out_v)
```

```python
import functools
import math
from typing import Callable, NamedTuple

import jax
import jax.numpy as jnp
from jax import lax
from jax.experimental import pallas as pl
from jax.experimental.pallas import tpu as pltpu

F32 = jnp.float32
BF16 = jnp.bfloat16
HI = lax.Precision.HIGHEST
MESH = pl.DeviceIdType.MESH

SWA_Q_HEADS = 16
SWA_KV_HEADS = 2
SWA_HEAD_DIM = 64
SWA_WINDOW = 128
SWA_SCALE = SWA_HEAD_DIM ** -0.5
assert math.frexp(SWA_SCALE)[0] == 0.5
GDN_HEADS = 4
GDN_HEAD_DIM = 128
GDN_CONV = 4
GDN_CHUNK = 64
XA_HEADS = 4
XA_HEAD_DIM = 128
RMS_EPS = 1e-6
L2_EPS = 1e-6
ADAM_LR = 0.001
ADAM_B1 = 0.9
ADAM_B2 = 0.999
ADAM_EPS = 1e-08
ADAM_WD = 0.01
ADAM_STEP = 10

LANES = 128
N_SHARDS = 4
VMEM_LIMIT = 56 * 1024 * 1024

NT = (((1,), (1,)), ((), ()))
TN = (((0,), (0,)), ((), ()))
NN = (((1,), (0,)), ((), ()))


def _cp(sem=None):
    return pltpu.CompilerParams(dimension_semantics=sem, vmem_limit_bytes=VMEM_LIMIT)


def _blk(dim, pref):
    if dim <= pref:
        return dim
    b = (pref // LANES) * LANES
    while dim % b:
        b -= LANES
    assert b > 0, (dim, pref)
    return b


def _dot(a, b, dims=NN, precision=None):
    return lax.dot_general(a, b, dims, precision=precision, preferred_element_type=F32)


def _sigmoid(x):
    return 1.0 / (1.0 + jnp.exp(-x))


MM_TK_BYTES = 4096


def _mm(a, b, *, name, ta=False, tb=False, out_dtypes=(F32,), epilogue=None, extras=(), tm=1024, tn=1024, tk=None,
        b_sharded=False, out_sharded=False, b_window=None, exchange=None):
    (kdim, m) = a.shape if ta else a.shape[::-1]
    col0 = 0
    n_lim = k_lim = None
    if b_sharded:
        ns, rows_w, per = b.shape
        if tb:
            kb, n, k_lim = ns * per, rows_w, per
        else:
            kb, n, n_lim = rows_w, ns * per, per
    else:
        (kb, n) = b.shape[::-1] if tb else b.shape
        if b_window is not None:
            assert not tb
            col0, n = b_window
    assert kdim == kb, (a.shape, b.shape, ta, tb)
    if out_sharded:
        assert n % N_SHARDS == 0
        n_lim = n // N_SHARDS if n_lim is None else n_lim
        assert n_lim == n // N_SHARDS
    if tk is None:
        tk = MM_TK_BYTES // max(a.dtype.itemsize, b.dtype.itemsize)
    tm, tn, tk = _blk(m, tm), _blk(n_lim or n, tn), _blk(k_lim or kdim, tk)
    assert col0 % tn == 0, (col0, tn)
    nk = kdim // tk
    a_spec = pl.BlockSpec((tk, tm), lambda i, j, k: (k, i)) if ta else pl.BlockSpec((tm, tk), lambda i, j, k: (i, k))
    if b_sharded and tb:
        kpb = k_lim // tk
        b_spec = pl.BlockSpec((None, tn, tk), lambda i, j, k: (k // kpb, j, k % kpb))
    elif b_sharded:
        bpb = n_lim // tn
        b_spec = pl.BlockSpec((None, tk, tn), lambda i, j, k: (j // bpb, k, j % bpb))
    elif tb:
        b_spec = pl.BlockSpec((tn, tk), lambda i, j, k: (j, k))
    else:
        b_spec = pl.BlockSpec((tk, tn), lambda i, j, k: (k, j + col0 // tn))
    x_spec = pl.BlockSpec((tm, tn), lambda i, j, k: (i, j))
    if out_sharded:
        opb = n_lim // tn
        o_spec = pl.BlockSpec((None, tm, tn), lambda i, j, k: (j // opb, i, j % opb))
        out_shape = (N_SHARDS, m, n_lim)
    else:
        o_spec, out_shape = x_spec, (m, n)
    dims = ((((0 if ta else 1),), ((1 if tb else 0),)), ((), ()))
    n_extra, n_out = len(extras), len(out_dtypes)

    host = _ExchangeHost(exchange)
    grid = (m // tm, n // tn, nk)

    def body(*refs):
        a_ref, b_ref = refs[:2]
        extra_refs = refs[2:2 + n_extra]
        out_refs = refs[2 + n_extra + host.n_in:2 + n_extra + host.n_in + n_out]
        host.start(refs, 2 + n_extra, 2 + n_extra + host.n_in + n_out, grid)
        part = _dot(a_ref[...].astype(BF16), b_ref[...].astype(BF16), dims)

        def finish(acc):
            vals = epilogue(acc, *[r[...] for r in extra_refs]) if epilogue is not None else (acc,) * n_out
            assert len(vals) == n_out
            for r, v in zip(out_refs, vals):
                r[...] = v.astype(r.dtype)

        if nk == 1:
            finish(part)
        else:
            acc_ref = refs[2 + n_extra + host.n_in + n_out + host.n_out]
            k = pl.program_id(2)

            @pl.when(k == 0)
            def _():
                acc_ref[...] = part

            @pl.when((k > 0) & (k < nk - 1))
            def _():
                acc_ref[...] += part

            @pl.when(k == nk - 1)
            def _():
                finish(acc_ref[...] + part)

        host.wait(refs, 2 + n_extra, 2 + n_extra + host.n_in + n_out, grid)

    outs = pl.pallas_call(
        body,
        grid=grid,
        in_specs=[a_spec, b_spec] + [x_spec] * n_extra + host.in_specs,
        out_specs=[o_spec] * n_out + host.out_specs,
        out_shape=[jax.ShapeDtypeStruct(out_shape, d) for d in out_dtypes] + host.out_shapes,
        scratch_shapes=([pltpu.VMEM((tm, tn), F32)] if nk > 1 else []) + host.scratch,
        input_output_aliases=host.aliases(2 + n_extra, n_out),
        compiler_params=_cp(host.semantics(("parallel", "parallel", "arbitrary"))),
        name=name,
    )(a, b, *extras, *host.ins)
    mine, landed = outs[:n_out], list(outs[n_out:])
    mine = mine[0] if n_out == 1 else mine
    return (mine, landed) if exchange is not None else mine


class Exchange(NamedTuple):
    ins: tuple
    out_shapes: tuple
    n_remote: int
    n_local: int
    copies_of: Callable
    aliases: tuple = ()


def _exchange_copies(ex, in_refs, out_refs, sem_refs):
    send_sems, recv_sems, local_sems = sem_refs
    remote, local = ex.copies_of(in_refs, out_refs, _place())
    assert len(remote) == ex.n_remote and len(local) == ex.n_local, (len(remote), len(local))
    cps = [pltpu.make_async_remote_copy(src_ref=src, dst_ref=dst, send_sem=send_sems.at[k], recv_sem=recv_sems.at[k],
                                        device_id=to, device_id_type=MESH) for k, (src, dst, to) in enumerate(remote)]
    cps += [pltpu.make_async_copy(src, dst, local_sems.at[k]) for k, (src, dst) in enumerate(local)]
    return cps


class _ExchangeHost:
    def __init__(self, ex):
        self.ex = ex
        self.ins = list(ex.ins) if ex else []
        self.out_shapes = list(ex.out_shapes) if ex else []
        self.n_in, self.n_out = len(self.ins), len(self.out_shapes)
        self.in_specs = [HBM_SPEC] * self.n_in
        self.out_specs = [HBM_SPEC] * self.n_out
        self.scratch = _sem_scratch(ex.n_remote, ex.n_local) if ex else []

    def semantics(self, sem):
        return tuple("arbitrary" for _ in sem) if self.ex else sem

    def aliases(self, in_at, out_at):
        return {in_at + i: out_at + o for i, o in self.ex.aliases} if self.ex else {}

    def _refs(self, refs, in_at, out_at):
        return refs[in_at:in_at + self.n_in], refs[out_at:out_at + self.n_out], refs[len(refs) - 3:]

    def _when(self, grid, last):
        cond = None
        for d, size in enumerate(grid):
            c = pl.program_id(d) == (size - 1 if last else 0)
            cond = c if cond is None else cond & c
        return cond

    def start(self, refs, in_at, out_at, grid):
        if self.ex:
            @pl.when(self._when(grid, False))
            def _():
                for cp in _exchange_copies(self.ex, *self._refs(refs, in_at, out_at)):
                    cp.start()

    def wait(self, refs, in_at, out_at, grid):
        if self.ex:
            @pl.when(self._when(grid, True))
            def _():
                for cp in _exchange_copies(self.ex, *self._refs(refs, in_at, out_at)):
                    cp.wait()


def _rms_fwd(x, g, *, name, tm=512):
    t, d = x.shape
    tm = _blk(t, tm)

    def body(x_ref, g_ref, n_ref):
        xv = x_ref[...]
        r = lax.rsqrt(jnp.mean(xv * xv, axis=-1, keepdims=True) + RMS_EPS)
        n_ref[...] = (xv * r * g_ref[...]).astype(n_ref.dtype)

    return pl.pallas_call(
        body, grid=(t // tm,),
        in_specs=[pl.BlockSpec((tm, d), lambda i: (i, 0)), pl.BlockSpec((1, d), lambda i: (0, 0))],
        out_specs=pl.BlockSpec((tm, d), lambda i: (i, 0)),
        out_shape=jax.ShapeDtypeStruct((t, d), BF16),
        compiler_params=_cp(("parallel",)), name=name,
    )(x, g)


def _rms_bwd(dn, x, g, dres, *, name, tm=512):
    t, d = x.shape
    tm = _blk(t, tm)

    def body(dn_ref, x_ref, g_ref, dres_ref, dx_ref, dxb_ref, dg_ref):
        i = pl.program_id(0)
        xv = x_ref[...]
        r = lax.rsqrt(jnp.mean(xv * xv, axis=-1, keepdims=True) + RMS_EPS)
        xh = xv * r
        dnv = dn_ref[...].astype(F32)
        dxh = dnv * g_ref[...]
        dx = dres_ref[...] + r * (dxh - xh * jnp.mean(dxh * xh, axis=-1, keepdims=True))
        dx_ref[...] = dx
        dxb_ref[...] = dx.astype(dxb_ref.dtype)
        part = jnp.sum(dnv * xh, axis=0, keepdims=True)

        @pl.when(i == 0)
        def _():
            dg_ref[...] = part

        @pl.when(i > 0)
        def _():
            dg_ref[...] += part

    row = pl.BlockSpec((tm, d), lambda i: (i, 0))
    vec = pl.BlockSpec((1, d), lambda i: (0, 0))
    return pl.pallas_call(
        body, grid=(t // tm,),
        in_specs=[row, row, vec, row], out_specs=[row, row, vec],
        out_shape=[jax.ShapeDtypeStruct((t, d), F32), jax.ShapeDtypeStruct((t, d), BF16),
                   jax.ShapeDtypeStruct((1, d), F32)],
        compiler_params=_cp(("arbitrary",)), name=name,
    )(dn, x, g, dres)


def _final_loss(h, g, tgt, *, name, tm=512):
    t, d = h.shape
    tm = _blk(t, tm)

    def body(h_ref, g_ref, t_ref, dh_ref, dhb_ref, dg_ref, loss_ref):
        i = pl.program_id(0)
        hv = h_ref[...]
        r = lax.rsqrt(jnp.mean(hv * hv, axis=-1, keepdims=True) + RMS_EPS)
        xh = hv * r
        e = xh * g_ref[...] - t_ref[...]
        dy = e * (1.0 / d)
        dxh = dy * g_ref[...]
        dh = r * (dxh - xh * jnp.mean(dxh * xh, axis=-1, keepdims=True))
        dh_ref[...] = dh
        dhb_ref[...] = dh.astype(dhb_ref.dtype)
        dg_part = jnp.sum(dy * xh, axis=0, keepdims=True)
        row_loss = jnp.sum(e * e, axis=-1, keepdims=True) * (0.5 / d)
        loss_part = jnp.sum(row_loss, axis=0, keepdims=True)

        @pl.when(i == 0)
        def _():
            dg_ref[...] = dg_part
            loss_ref[...] = jnp.broadcast_to(loss_part, loss_ref.shape)

        @pl.when(i > 0)
        def _():
            dg_ref[...] += dg_part
            loss_ref[...] += jnp.broadcast_to(loss_part, loss_ref.shape)

    row = pl.BlockSpec((tm, d), lambda i: (i, 0))
    vec = pl.BlockSpec((1, d), lambda i: (0, 0))
    return pl.pallas_call(
        body, grid=(t // tm,),
        in_specs=[row, vec, row], out_specs=[row, row, vec, pl.BlockSpec((1, LANES), lambda i: (0, 0))],
        out_shape=[jax.ShapeDtypeStruct((t, d), F32), jax.ShapeDtypeStruct((t, d), BF16),
                   jax.ShapeDtypeStruct((1, d), F32), jax.ShapeDtypeStruct((1, LANES), F32)],
        compiler_params=_cp(("arbitrary",)), name=name,
    )(h, g, tgt)


def _swa_mask(n, reps):
    w = SWA_WINDOW
    qi = lax.broadcasted_iota(jnp.int32, (reps * w, 2 * w), 0) & (w - 1)
    kj = lax.broadcasted_iota(jnp.int32, (reps * w, 2 * w), 1)
    return (kj > qi) & (kj <= qi + w) & ((n > 0) | (kj >= w))


def _stack_heads(ref, heads, width):
    return jnp.concatenate([ref[:, h * width:(h + 1) * width] for h in heads], axis=0)


def _stack_scalars(ref, heads, rows):
    return jnp.concatenate([jnp.broadcast_to(ref[0:1, h:h + 1], (rows, 1)) for h in heads], axis=0)


def _swa_fwd(q, kv, sinks, *, name, exchange=None):
    t = q.shape[0]
    w, hd, hq, hkv = SWA_WINDOW, SWA_HEAD_DIM, SWA_Q_HEADS, SWA_KV_HEADS
    grp = hq // hkv
    kvw = hkv * hd
    nb = t // w
    host = _ExchangeHost(exchange)
    assert not (exchange and exchange.aliases)

    def body(*refs):
        q_ref, kvp_ref, kvc_ref, s_ref = refs[:4]
        o_ref, lse_ref = refs[4 + host.n_in:6 + host.n_in]
        host.start(refs, 4, 6 + host.n_in, (nb,))
        n = pl.program_id(0)
        mask = _swa_mask(n, grp)
        kvcat = jnp.concatenate([kvp_ref[...], kvc_ref[...]], axis=0)
        outs, lses = [], []
        for hk in range(hkv):
            heads = range(hk * grp, (hk + 1) * grp)
            qs = _stack_heads(q_ref, heads, hd)
            kh = kvcat[:, hk * hd:(hk + 1) * hd]
            vh = kvcat[:, kvw + hk * hd:kvw + (hk + 1) * hd]
            sk = _stack_scalars(s_ref, heads, w)
            s = jnp.where(mask, _dot(qs * SWA_SCALE, kh, NT), -jnp.inf)
            m = jnp.maximum(jnp.max(s, axis=-1, keepdims=True), sk)
            p = jnp.exp(s - m)
            den = jnp.sum(p, axis=-1, keepdims=True) + jnp.exp(sk - m)
            o = _dot((p * (1.0 / den)).astype(BF16), vh)
            lse = m + jnp.log(den)
            outs += [o[j * w:(j + 1) * w] for j in range(grp)]
            lses += [lse[j * w:(j + 1) * w] for j in range(grp)]
        o_ref[...] = jnp.concatenate(outs, axis=1).astype(o_ref.dtype)
        lse_ref[...] = jnp.concatenate(lses, axis=1)
        host.wait(refs, 4, 6 + host.n_in, (nb,))

    outs = pl.pallas_call(
        body, grid=(nb,),
        in_specs=[pl.BlockSpec((w, hq * hd), lambda i: (i, 0)),
                  pl.BlockSpec((w, 2 * kvw), lambda i: (jnp.maximum(i - 1, 0), 0)),
                  pl.BlockSpec((w, 2 * kvw), lambda i: (i, 0)),
                  pl.BlockSpec((1, hq), lambda i: (0, 0))] + host.in_specs,
        out_specs=[pl.BlockSpec((w, hq * hd), lambda i: (i, 0)), pl.BlockSpec((w, hq), lambda i: (i, 0))] + host.out_specs,
        out_shape=[jax.ShapeDtypeStruct((t, hq * hd), BF16), jax.ShapeDtypeStruct((t, hq), F32)] + host.out_shapes,
        scratch_shapes=host.scratch,
        compiler_params=_cp(host.semantics(("parallel",))), name=name,
    )(q, kv, kv, sinks, *host.ins)
    return (outs[0], outs[1], list(outs[2:])) if exchange is not None else outs


ANY_SPEC = pl.BlockSpec(memory_space=pl.ANY)


def _swa_bwd(q, kv, sinks, o, lse, do, dp, dp_col, *, name):
    t = q.shape[0]
    w, hd, hq, hkv = SWA_WINDOW, SWA_HEAD_DIM, SWA_Q_HEADS, SWA_KV_HEADS
    grp = hq // hkv
    kvw = hkv * hd
    nb = t // w
    assert dp_col % (hq * hd) == 0
    dq_blk = dp_col // (hq * hd)

    def body(q_ref, kvp_ref, kvc_ref, s_ref, o_ref, lse_ref, do_ref, _, dq_ref, dkv_ref, ds_ref, carry_ref):
        n = pl.program_id(0)

        @pl.when(n == 0)
        def _():
            ds_ref[...] = jnp.zeros_like(ds_ref)
            carry_ref[...] = jnp.zeros_like(carry_ref)

        @pl.when(n < nb)
        def _():
            mask = _swa_mask(n, grp)
            kvcat = jnp.concatenate([kvp_ref[...], kvc_ref[...]], axis=0)
            dqs, dsk, dks, dvs = [], [], [], []
            for hk in range(hkv):
                heads = range(hk * grp, (hk + 1) * grp)
                qs = _stack_heads(q_ref, heads, hd)
                dos = _stack_heads(do_ref, heads, hd)
                os_ = _stack_heads(o_ref, heads, hd)
                lse = _stack_heads(lse_ref, heads, 1)
                sk = _stack_scalars(s_ref, heads, w)
                kh = kvcat[:, hk * hd:(hk + 1) * hd]
                vh = kvcat[:, kvw + hk * hd:kvw + (hk + 1) * hd]
                s = _dot(qs * SWA_SCALE, kh, NT)
                p = jnp.exp(jnp.where(mask, s, -jnp.inf) - lse)
                delta = jnp.sum(dos.astype(F32) * os_.astype(F32), axis=-1, keepdims=True)
                ds = (p * (_dot(dos, vh, NT) - delta) * SWA_SCALE).astype(BF16)
                dq = _dot(ds, kh)
                dqs += [dq[j * w:(j + 1) * w] for j in range(grp)]
                dks.append(_dot(ds, qs, TN))
                dvs.append(_dot(p.astype(BF16), dos, TN))
                dsink = -jnp.exp(sk - lse) * delta
                dsk += [jnp.sum(dsink[j * w:(j + 1) * w], axis=0, keepdims=True) for j in range(grp)]
            dq_ref[...] = jnp.concatenate(dqs, axis=1).astype(dq_ref.dtype)
            ds_ref[...] += jnp.concatenate(dsk, axis=1)
            dkv_cat = jnp.concatenate(dks + dvs, axis=1)
            dkv_ref[...] = (carry_ref[...] + dkv_cat[:w]).astype(dkv_ref.dtype)
            carry_ref[...] = dkv_cat[w:]

        @pl.when(n == nb)
        def _():
            dkv_ref[...] = carry_ref[...].astype(dkv_ref.dtype)

    cur = lambda i: (jnp.minimum(i, nb - 1), 0)
    prev = lambda i: (jnp.clip(i - 1, 0, nb - 1), 0)
    return pl.pallas_call(
        body, grid=(nb + 1,),
        in_specs=[pl.BlockSpec((w, hq * hd), cur), pl.BlockSpec((w, 2 * kvw), prev), pl.BlockSpec((w, 2 * kvw), cur),
                  pl.BlockSpec((1, hq), lambda i: (0, 0)), pl.BlockSpec((w, hq * hd), cur),
                  pl.BlockSpec((w, hq), cur), pl.BlockSpec((w, hq * hd), cur), ANY_SPEC],
        out_specs=[pl.BlockSpec((w, hq * hd), lambda i: (jnp.minimum(i, nb - 1), dq_blk)),
                   pl.BlockSpec((w, 2 * kvw), prev), pl.BlockSpec((1, hq), lambda i: (0, 0))],
        out_shape=[jax.ShapeDtypeStruct(dp.shape, dp.dtype), jax.ShapeDtypeStruct((t, 2 * kvw), BF16),
                   jax.ShapeDtypeStruct((1, hq), F32)],
        scratch_shapes=[pltpu.VMEM((w, 2 * kvw), F32)], input_output_aliases={7: 0},
        compiler_params=_cp(("arbitrary",)), name=name,
    )(q, kv, kv, sinks, o, lse, do, dp)


def _xa_fwd(q, mkv, *, name, tq=512):
    t, xw = q.shape
    nm = mkv.shape[0]
    hd, nh = XA_HEAD_DIM, XA_HEADS
    tq = _blk(t, tq)

    def body(q_ref, mkv_ref, o_ref):
        outs = []
        for h in range(nh):
            qh = q_ref[:, h * hd:(h + 1) * hd]
            kh = mkv_ref[:, h * hd:(h + 1) * hd]
            vh = mkv_ref[:, xw + h * hd:xw + (h + 1) * hd]
            s = _dot(qh, kh, NT) * (hd ** -0.5)
            p = jnp.exp(s - jnp.max(s, axis=-1, keepdims=True))
            p = p / jnp.sum(p, axis=-1, keepdims=True)
            outs.append(_dot(p.astype(BF16), vh))
        o_ref[...] = jnp.concatenate(outs, axis=1).astype(o_ref.dtype)

    return pl.pallas_call(
        body, grid=(t // tq,),
        in_specs=[pl.BlockSpec((tq, xw), lambda i: (i, 0)), pl.BlockSpec((nm, 2 * xw), lambda i: (0, 0))],
        out_specs=pl.BlockSpec((tq, xw), lambda i: (i, 0)),
        out_shape=jax.ShapeDtypeStruct((t, xw), BF16),
        compiler_params=_cp(("parallel",)), name=name,
    )(q, mkv)


def _xa_bwd(q, mkv, do, dp, dp_col, *, name, tq=512):
    t, xw = q.shape
    nm = mkv.shape[0]
    hd, nh = XA_HEAD_DIM, XA_HEADS
    tq = _blk(t, tq)
    assert dp_col % xw == 0

    def body(q_ref, mkv_ref, do_ref, _, dq_ref, dmkv_ref):
        i = pl.program_id(0)
        dqs, dks, dvs = [], [], []
        for h in range(nh):
            qh = q_ref[:, h * hd:(h + 1) * hd]
            kh = mkv_ref[:, h * hd:(h + 1) * hd]
            vh = mkv_ref[:, xw + h * hd:xw + (h + 1) * hd]
            doh = do_ref[:, h * hd:(h + 1) * hd]
            s = _dot(qh, kh, NT) * (hd ** -0.5)
            p = jnp.exp(s - jnp.max(s, axis=-1, keepdims=True))
            p = p / jnp.sum(p, axis=-1, keepdims=True)
            dp = _dot(doh, vh, NT)
            ds = (p * (dp - jnp.sum(p * dp, axis=-1, keepdims=True)) * (hd ** -0.5)).astype(BF16)
            dqs.append(_dot(ds, kh))
            dks.append(_dot(ds, qh, TN))
            dvs.append(_dot(p.astype(BF16), doh, TN))
        dq_ref[...] = jnp.concatenate(dqs, axis=1).astype(dq_ref.dtype)
        part = jnp.concatenate(dks + dvs, axis=1)

        @pl.when(i == 0)
        def _():
            dmkv_ref[...] = part

        @pl.when(i > 0)
        def _():
            dmkv_ref[...] += part

    row = pl.BlockSpec((tq, xw), lambda i: (i, 0))
    full = pl.BlockSpec((nm, 2 * xw), lambda i: (0, 0))
    return pl.pallas_call(
        body, grid=(t // tq,),
        in_specs=[row, full, row, ANY_SPEC],
        out_specs=[pl.BlockSpec((tq, xw), lambda i: (i, dp_col // xw)), full],
        out_shape=[jax.ShapeDtypeStruct(dp.shape, dp.dtype), jax.ShapeDtypeStruct((nm, 2 * xw), F32)],
        input_output_aliases={3: 0}, compiler_params=_cp(("arbitrary",)), name=name,
    )(q, mkv, do, dp)


def _merge_specs(ys, ws, tm):
    y_specs = [pl.BlockSpec((tm, y.shape[1]), lambda i: (i, 0)) for y in ys]
    w_specs = [pl.BlockSpec(w.shape, lambda i: (0, 0, 0)) for w in ws]
    return y_specs, w_specs


def _merge_tiles(ws, tn):
    ns, _, per = ws[0].shape
    tn = _blk(per, tn)
    return tn, [(s, c, s * per + c) for s in range(ns) for c in range(0, per, tn)]


def _merge_fwd(ys, ws, gates, *, name, tm=256, tn=512):
    t, d = ys[0].shape[0], ws[0].shape[0] * ws[0].shape[2]
    tm = _blk(t, tm)
    tn, tiles = _merge_tiles(ws, tn)
    y_specs, w_specs = _merge_specs(ys, ws, tm)

    def body(ya, yb, yc, wa, wb, wc, g_ref, o_ref):
        for s, c, col in tiles:
            acc = None
            for b, (y, w) in enumerate(((ya, wa), (yb, wb), (yc, wc))):
                term = _sigmoid(g_ref[:, b * d + col:b * d + col + tn]) * _dot(y[...], w[s, :, c:c + tn])
                acc = term if acc is None else acc + term
            o_ref[:, col:col + tn] = acc.astype(o_ref.dtype)

    return pl.pallas_call(
        body, grid=(t // tm,),
        in_specs=y_specs + w_specs + [pl.BlockSpec((tm, 3 * d), lambda i: (i, 0))],
        out_specs=pl.BlockSpec((tm, d), lambda i: (i, 0)),
        out_shape=jax.ShapeDtypeStruct((t, d), BF16),
        compiler_params=_cp(("parallel",)), name=name,
    )(*ys, *ws, gates)


def _merge_bwd(ys, ws, gates, dmerged, dp_width, *, name, tm=256, tn=512):
    t, d = ys[0].shape[0], ws[0].shape[0] * ws[0].shape[2]
    tm = _blk(t, tm)
    tn, tiles = _merge_tiles(ws, tn)
    y_specs, w_specs = _merge_specs(ys, ws, tm)
    row = pl.BlockSpec((tm, d), lambda i: (i, 0))
    wide = pl.BlockSpec((tm, 3 * d), lambda i: (i, 0))

    def body(ya, yb, yc, wa, wb, wc, g_ref, dm_ref, dua, dub, duc, dp_ref):
        for s, c, col in tiles:
            dm = dm_ref[:, col:col + tn]
            for b, (y, w, du) in enumerate(((ya, wa, dua), (yb, wb, dub), (yc, wc, duc))):
                sg = _sigmoid(g_ref[:, b * d + col:b * d + col + tn])
                u = _dot(y[...], w[s, :, c:c + tn])
                du[:, col:col + tn] = (dm * sg).astype(du.dtype)
                dp_ref[:, b * d + col:b * d + col + tn] = (dm * u * sg * (1.0 - sg)).astype(dp_ref.dtype)

    return pl.pallas_call(
        body, grid=(t // tm,),
        in_specs=y_specs + w_specs + [wide, row],
        out_specs=[row] * 3 + [wide],
        out_shape=[jax.ShapeDtypeStruct((t, d), BF16)] * 3 + [jax.ShapeDtypeStruct((t, dp_width), BF16)],
        compiler_params=_cp(("parallel",)), name=name,
    )(*ys, *ws, gates, dmerged)


def _adamw(w, g, m, v, *, name, tm=256):
    lead = w.ndim - 2
    assert all(s == 1 for s in w.shape[:lead]) and m.shape == w.shape and v.shape == w.shape
    r, c = w.shape[lead:]
    assert g.shape == (r, c)
    tm = _blk(r, tm) if r % 8 == 0 else r
    tc = c if tm * c * 4 <= (4 << 20) else _blk(c, 256)
    ncb = c // tc
    bc1 = 1.0 - ADAM_B1 ** ADAM_STEP
    bc2 = 1.0 - ADAM_B2 ** ADAM_STEP

    def body(w_ref, g_ref, m_ref, v_ref, d_ref, nm_ref, nv_ref):
        gv = g_ref[...]
        nm = ADAM_B1 * m_ref[...] + (1.0 - ADAM_B1) * gv
        nv = ADAM_B2 * v_ref[...] + (1.0 - ADAM_B2) * (gv * gv)
        d_ref[...] = -ADAM_LR * ((nm / bc1) / (jnp.sqrt(nv / bc2) + ADAM_EPS) + ADAM_WD * w_ref[...])
        nm_ref[...] = nm
        nv_ref[...] = nv

    spec = pl.BlockSpec((None,) * lead + (tm, tc), lambda i: (0,) * lead + (i // ncb, i % ncb))
    g_spec = pl.BlockSpec((tm, tc), lambda i: (i // ncb, i % ncb))
    return pl.pallas_call(
        body, grid=(r // tm * ncb,), in_specs=[spec, g_spec, spec, spec], out_specs=[spec] * 3,
        out_shape=[jax.ShapeDtypeStruct(w.shape, F32)] * 3,
        compiler_params=_cp(("parallel",)), name=name,
    )(w, g, m, v)


HALO = 8


def _shift_down(cur, prev, j):
    if j == 0:
        return cur
    y = pltpu.roll(cur, j, 0)
    row = lax.broadcasted_iota(jnp.int32, (HALO, cur.shape[1]), 0)
    top = jnp.where(row < j, pltpu.roll(prev, j, 0), y[:HALO])
    return jnp.concatenate([top, y[HALO:]], axis=0)


def _shift_up(cur, nxt, j):
    if j == 0:
        return cur
    tm = cur.shape[0]
    y = pltpu.roll(cur, tm - j, 0)
    row = lax.broadcasted_iota(jnp.int32, (HALO, cur.shape[1]), 0)
    bot = jnp.where(row >= HALO - j, pltpu.roll(nxt, HALO - j, 0), y[tm - HALO:])
    return jnp.concatenate([y[:tm - HALO], bot], axis=0)


def _softplus(x):
    return jnp.maximum(x, 0.0) + jnp.log(1.0 + jnp.exp(-jnp.abs(x)))


def _gdn_pre_fwd(qkvb, conv_w, ab, alog_pad, dt_pad, *, name, ab_blk=0, tm=256):
    t, cw = qkvb.shape
    hd, nh, ck = GDN_HEAD_DIM, GDN_HEADS, GDN_CHUNK
    gw = nh * hd
    tm = _blk(t, tm)
    hb = tm // HALO

    def body(x_ref, xp_ref, w_ref, ab_ref, al_ref, dt_ref, xc_ref, qkvn_ref, aux_ref):
        i = pl.program_id(0)
        cur = x_ref[...]
        prev = jnp.where(i > 0, xp_ref[...], 0.0)
        xc = None
        for tap in range(GDN_CONV):
            term = w_ref[tap:tap + 1, :] * _shift_down(cur, prev, GDN_CONV - 1 - tap)
            xc = term if xc is None else xc + term
        xc_ref[...] = xc
        s = xc * _sigmoid(xc)
        for h in range(2 * nh):
            xh = s[:, h * hd:(h + 1) * hd]
            r = lax.rsqrt(jnp.sum(xh * xh, axis=-1, keepdims=True) + L2_EPS)
            scale = hd ** -0.5 if h < nh else 1.0
            qkvn_ref[:, h * hd:(h + 1) * hd] = xh * (r * scale)
        qkvn_ref[:, 2 * gw:] = s[:, 2 * gw:]
        abv = ab_ref[...]
        lane = lax.broadcasted_iota(jnp.int32, abv.shape, 1)
        g = jnp.where(lane < nh, -jnp.exp(al_ref[...]) * _softplus(abv + dt_ref[...]), 0.0)
        beta = jnp.where((lane >= nh) & (lane < 2 * nh), _sigmoid(abv), 0.0)
        ii = lax.broadcasted_iota(jnp.int32, (tm, tm), 0)
        jj = lax.broadcasted_iota(jnp.int32, (tm, tm), 1)
        tri = jnp.where((ii >= jj) & ((ii ^ jj) < ck), 1.0, 0.0)
        gcum = _dot(tri, g, precision=HI)
        aux_ref[...] = g + beta + pltpu.roll(gcum, 2 * nh, 1)

    row = lambda c: pl.BlockSpec((tm, c), lambda i: (i, 0))
    vec = lambda r, c: pl.BlockSpec((r, c), lambda i: (0, 0))
    return pl.pallas_call(
        body, grid=(t // tm,),
        in_specs=[row(cw), pl.BlockSpec((HALO, cw), lambda i: (jnp.maximum(i * hb - 1, 0), 0)), vec(GDN_CONV, cw),
                  pl.BlockSpec((tm, LANES), lambda i: (i, ab_blk)), vec(1, LANES), vec(1, LANES)],
        out_specs=[row(cw), row(cw), row(LANES)],
        out_shape=[jax.ShapeDtypeStruct((t, cw), F32), jax.ShapeDtypeStruct((t, cw), F32),
                   jax.ShapeDtypeStruct((t, LANES), F32)],
        compiler_params=_cp(("parallel",)), name=name,
    )(qkvb, qkvb, conv_w, ab, alog_pad, dt_pad)


GDN_STEP_CHUNKS = 4
GDN_ILP_CHUNKS = 4
GDN_ILP_CHUNKS_BWD = 4


def _bdot(a, b, dims=NN):
    return _dot(a.astype(BF16), b.astype(BF16), dims)


def _split_bf16(x):
    hi = x.astype(BF16)
    return hi, (x - hi.astype(F32)).astype(BF16)


def _dot3(a, b, dims=NN):
    ah, al = _split_bf16(a)
    bh, bl = _split_bf16(b)
    return _dot(ah, bh, dims) + (_dot(ah, bl, dims) + _dot(al, bh, dims))


def _dot3_many(lhs, rhs, dims=NN):
    sa = [_split_bf16(a) for a in lhs]
    sb = [_split_bf16(b) for b in rhs]
    hh = [_dot(a[0], b[0], dims) for a, b in zip(sa, sb)]
    hl = [_dot(a[0], b[1], dims) for a, b in zip(sa, sb)]
    lh = [_dot(a[1], b[0], dims) for a, b in zip(sa, sb)]
    return [x + (y + z) for x, y, z in zip(hh, hl, lh)]


def _gdn_local(chains, with_inverse):
    ck = GDN_CHUNK
    ii = lax.broadcasted_iota(jnp.int32, (ck, ck), 0)
    jj = lax.broadcasted_iota(jnp.int32, (ck, ck), 1)
    lower, strict = ii >= jj, ii > jj
    dmat = [jnp.exp(jnp.where(lower, gc - gc_row, -jnp.inf)) for _, _, _, gc, gc_row in chains]
    kk = [_bdot(k, k, NT) for _, k, _, _, _ in chains]
    qk = [_bdot(q, k, NT) for q, k, _, _, _ in chains]
    tinv = [None] * len(chains)
    if with_inverse:
        lmat = [jnp.where(strict, c[2] * kk_i * d_i, 0.0) for c, kk_i, d_i in zip(chains, kk, dmat)]
        eye = jnp.where(ii == jj, 1.0, 0.0)
        tinv = [eye - l_i for l_i in lmat]
        pw = lmat
        for _ in range(int(math.log2(ck)) - 1):
            pw = _dot3_many(pw, pw)
            tinv = [t_i + d_i for t_i, d_i in zip(tinv, _dot3_many(tinv, pw))]
    out = []
    for (q, k, b, gc, gc_row), dmat_i, kk_i, qk_i, tinv_i in zip(chains, dmat, kk, qk, tinv):
        gl = gc[ck - 1:ck, :]
        out.append(dict(lower=lower, strict=strict, dmat=dmat_i, kk=kk_i, tinv=tinv_i, gam=jnp.exp(gc), qk=qk_i,
                        mm=qk_i * dmat_i, kdec=jnp.exp(gl - gc)))
    return out


def _gdn_head_cols(h):
    return slice(h * GDN_HEAD_DIM, (h + 1) * GDN_HEAD_DIM)


def _gdn_chunk_inputs(x_ref, aux_ref, auxt_ref, g, h):
    nh, ck = GDN_HEADS, GDN_CHUNK
    gw = nh * GDN_HEAD_DIM
    rows = slice(g * ck, (g + 1) * ck)
    cols = _gdn_head_cols(h)
    q = x_ref[rows, cols]
    k = x_ref[rows, gw + cols.start:gw + cols.stop]
    v = x_ref[rows, 2 * gw + cols.start:2 * gw + cols.stop]
    b = aux_ref[rows, nh + h:nh + h + 1]
    gc = aux_ref[rows, 2 * nh + h:2 * nh + h + 1]
    gc_row = auxt_ref[g, 2 * nh + h:2 * nh + h + 1, :]
    return q, k, v, b, gc, gc_row


def _gdn_specs(t, widths, *, reverse=False, step_chunks=None):
    rows = (step_chunks or GDN_STEP_CHUNKS) * GDN_CHUNK
    nsteps = t // rows
    idx = (lambda i: (nsteps - 1 - i, 0)) if reverse else (lambda i: (i, 0))
    return [pl.BlockSpec((rows, w), idx) for w in widths]


def _gdn_local_fwd(qkvn, aux, aux_t, *, name, exchange=None):
    t = qkvn.shape[0]
    hd, nh, ck, gs = GDN_HEAD_DIM, GDN_HEADS, GDN_CHUNK, GDN_STEP_CHUNKS
    gw = nh * hd
    host = _ExchangeHost(exchange)
    assert not (exchange and exchange.aliases)
    grid = (t // (gs * ck),)

    def body(*refs):
        x_ref, aux_ref, auxt_ref = refs[:3]
        u_ref, w_ref, qd_ref, kd_ref, mm_ref, tinv_ref = refs[3 + host.n_in:9 + host.n_in]
        host.start(refs, 3, 9 + host.n_in, grid)
        for g0 in range(0, gs, GDN_ILP_CHUNKS):
            where = [(g, h) for g in range(g0, g0 + GDN_ILP_CHUNKS) for h in range(nh)]
            ins = [_gdn_chunk_inputs(x_ref, aux_ref, auxt_ref, g, h) for g, h in where]
            lcs = _gdn_local([(q, k, b, gc, gc_row) for q, k, _, b, gc, gc_row in ins], True)
            tinvs = [lc["tinv"] for lc in lcs]
            us = _dot3_many(tinvs, [b * v for _, _, v, b, _, _ in ins])
            ws = _dot3_many(tinvs, [(b * lc["gam"]) * k for (_, k, _, b, _, _), lc in zip(ins, lcs)])
            for i, ((g, h), (q, k, _, _, _, _), lc) in enumerate(zip(where, ins, lcs)):
                rows, cols = slice(g * ck, (g + 1) * ck), _gdn_head_cols(h)
                u_ref[rows, cols] = us[i]
                w_ref[rows, cols] = ws[i].astype(w_ref.dtype)
                qd_ref[rows, cols] = (lc["gam"] * q).astype(qd_ref.dtype)
                kd_ref[rows, cols] = (lc["kdec"] * k).astype(kd_ref.dtype)
            for g in range(g0, g0 + GDN_ILP_CHUNKS):
                rows = slice(g * ck, (g + 1) * ck)
                mine = [lc for (gg, _), lc in zip(where, lcs) if gg == g]
                mm_ref[rows, :] = jnp.concatenate([lc["mm"] for lc in mine], axis=1).astype(mm_ref.dtype)
                tinv_ref[rows, :] = jnp.concatenate([lc["tinv"] for lc in mine], axis=1)
        host.wait(refs, 3, 9 + host.n_in, grid)

    sq = nh * ck
    outs = pl.pallas_call(
        body, grid=grid,
        in_specs=_gdn_specs(t, (3 * gw, LANES)) + [pl.BlockSpec((gs, 16, ck), lambda i: (i, 0, 0))] + host.in_specs,
        out_specs=_gdn_specs(t, (gw, gw, gw, gw, sq, sq)) + host.out_specs,
        out_shape=[jax.ShapeDtypeStruct((t, gw), F32)] + [jax.ShapeDtypeStruct((t, gw), BF16)] * 3
        + [jax.ShapeDtypeStruct((t, sq), BF16), jax.ShapeDtypeStruct((t, sq), F32)] + host.out_shapes,
        scratch_shapes=host.scratch,
        compiler_params=_cp(host.semantics(("parallel",))), name=name,
    )(qkvn, aux, aux_t, *host.ins)
    return (*outs[:6], list(outs[6:])) if exchange is not None else outs


def _gdn_seq_fwd(u, w, qd, kd, mm, aux, *, name):
    t = u.shape[0]
    hd, nh, ck, gs = GDN_HEAD_DIM, GDN_HEADS, GDN_CHUNK, GDN_STEP_CHUNKS
    gw = nh * hd
    sq = nh * ck

    def body(u_ref, w_ref, qd_ref, kd_ref, mm_ref, aux_ref, o_ref, vn_ref, sall_ref, s_ref):
        @pl.when(pl.program_id(0) == 0)
        def _():
            s_ref[...] = jnp.zeros_like(s_ref)

        heads = range(nh)
        hcols = [_gdn_head_cols(h) for h in heads]
        sts = [s_ref[h] for h in heads]
        for g in range(gs):
            rows = slice(g * ck, (g + 1) * ck)
            last = (g + 1) * ck - 1
            for h in heads:
                sall_ref[g, h] = sts[h]
            stbs = [st.astype(BF16) for st in sts]
            w_s = [_dot(w_ref[rows, c], stb) for c, stb in zip(hcols, stbs)]
            q_s = [_dot(qd_ref[rows, c], stb) for c, stb in zip(hcols, stbs)]
            vnbs = [(u_ref[rows, c] - ws).astype(BF16) for c, ws in zip(hcols, w_s)]
            m_v = [_dot(mm_ref[rows, h * ck:(h + 1) * ck], vnbs[h]) for h in heads]
            k_v = [_dot(kd_ref[rows, c], vnb, TN) for c, vnb in zip(hcols, vnbs)]
            for h, c in zip(heads, hcols):
                vn_ref[rows, c] = vnbs[h]
                o_ref[rows, c] = q_s[h] + m_v[h]
            gam_c = [jnp.exp(aux_ref[last:last + 1, 2 * nh + h:2 * nh + h + 1]) for h in heads]
            sts = [gam_c[h] * sts[h] + k_v[h] for h in heads]
        for h in heads:
            s_ref[h] = sts[h]

    return pl.pallas_call(
        body, grid=(t // (gs * ck),),
        in_specs=_gdn_specs(t, (gw, gw, gw, gw, sq, LANES)),
        out_specs=_gdn_specs(t, (gw, gw)) + [pl.BlockSpec((gs, nh, hd, hd), lambda i: (i, 0, 0, 0))],
        out_shape=[jax.ShapeDtypeStruct((t, gw), F32), jax.ShapeDtypeStruct((t, gw), BF16),
                   jax.ShapeDtypeStruct((t // ck, nh, hd, hd), F32)],
        scratch_shapes=[pltpu.VMEM((nh, hd, hd), F32)],
        compiler_params=_cp(("arbitrary",)), name=name,
    )(u, w, qd, kd, mm, aux)


def _gdn_seq_bwd(do, w, qd, kd, mm, vn, s_all, aux, *, name):
    t = do.shape[0]
    hd, nh, ck, gs = GDN_HEAD_DIM, GDN_HEADS, GDN_CHUNK, GDN_STEP_CHUNKS
    gw = nh * hd
    sq = nh * ck
    nsteps = t // (gs * ck)

    def body(do_ref, w_ref, qd_ref, kd_ref, mm_ref, vn_ref, sall_ref, aux_ref, dvn_ref, dqd_ref, dkd_ref, dw_ref,
             dlast_ref, ds_ref):
        @pl.when(pl.program_id(0) == 0)
        def _():
            ds_ref[...] = jnp.zeros_like(ds_ref)

        lane = lax.broadcasted_iota(jnp.int32, (ck, LANES), 1)
        rowi = lax.broadcasted_iota(jnp.int32, (ck, LANES), 0)
        heads = range(nh)
        hcols = [_gdn_head_cols(h) for h in heads]
        dsns = [ds_ref[h] for h in heads]
        for g in reversed(range(gs)):
            rows = slice(g * ck, (g + 1) * ck)
            last = (g + 1) * ck - 1
            sts = [sall_ref[g, h] for h in heads]
            stbs = [st.astype(BF16) for st in sts]
            dsbs = [dsn.astype(BF16) for dsn in dsns]
            dobs = [do_ref[rows, c].astype(BF16) for c in hcols]
            dvns = [_dot(mm_ref[rows, h * ck:(h + 1) * ck], dobs[h], TN) + _dot(kd_ref[rows, hcols[h]], dsbs[h])
                    for h in heads]
            dqds = [_dot(dob, stb, NT) for dob, stb in zip(dobs, stbs)]
            dkds = [_dot(vn_ref[rows, c], dsb, NT) for c, dsb in zip(hcols, dsbs)]
            q_o = [_dot(qd_ref[rows, c], dob, TN) for c, dob in zip(hcols, dobs)]
            dvbs = [dvn.astype(BF16) for dvn in dvns]
            dws = [_dot(dvb, stb, NT) for dvb, stb in zip(dvbs, stbs)]
            w_v = [_dot(w_ref[rows, c], dvb, TN) for c, dvb in zip(hcols, dvbs)]
            gam_c = [jnp.exp(aux_ref[last:last + 1, 2 * nh + h:2 * nh + h + 1]) for h in heads]
            dlast = jnp.zeros((ck, LANES), F32)
            for h, c in zip(heads, hcols):
                dvn_ref[rows, c] = dvns[h]
                dqd_ref[rows, c] = dqds[h]
                dkd_ref[rows, c] = dkds[h]
                dw_ref[rows, c] = -dws[h]
                dgam_c = jnp.sum(jnp.sum(dsns[h] * sts[h], axis=1, keepdims=True), axis=0, keepdims=True)
                dlast = dlast + jnp.where((rowi == ck - 1) & (lane == h), gam_c[h] * dgam_c, 0.0)
            dlast_ref[rows, :] = dlast
            dsns = [q_o[h] + gam_c[h] * dsns[h] - w_v[h] for h in heads]
        for h in heads:
            ds_ref[h] = dsns[h]

    return pl.pallas_call(
        body, grid=(nsteps,),
        in_specs=_gdn_specs(t, (gw, gw, gw, gw, sq, gw), reverse=True)
        + [pl.BlockSpec((gs, nh, hd, hd), lambda i: (nsteps - 1 - i, 0, 0, 0))] + _gdn_specs(t, (LANES,), reverse=True),
        out_specs=_gdn_specs(t, (gw, gw, gw, gw, LANES), reverse=True),
        out_shape=[jax.ShapeDtypeStruct((t, gw), F32)] * 4 + [jax.ShapeDtypeStruct((t, LANES), F32)],
        scratch_shapes=[pltpu.VMEM((nh, hd, hd), F32)],
        compiler_params=_cp(("arbitrary",)), name=name,
    )(do, w, qd, kd, mm, vn, s_all, aux)


def _gdn_local_bwd(qkvn, aux, aux_t, tinv, u, w, vn, do, dvn, dqd, dkd, dw, dlast, *, name):
    t = qkvn.shape[0]
    hd, nh, ck, gs = GDN_HEAD_DIM, GDN_HEADS, GDN_CHUNK, GDN_STEP_CHUNKS
    gw = nh * hd
    sq = nh * ck

    def body(x_ref, aux_ref, auxt_ref, tinv_ref, u_ref, w_ref, vn_ref, do_ref, dvn_ref, dqd_ref, dkd_ref, dw_ref,
             dlast_ref, dx_ref, daux_ref):
        lane = lax.broadcasted_iota(jnp.int32, (ck, LANES), 1)
        ones = jnp.ones((ck, LANES), F32)
        ii = lax.broadcasted_iota(jnp.int32, (ck, ck), 0)
        jj = lax.broadcasted_iota(jnp.int32, (ck, ck), 1)
        suffix = jnp.where(jj >= ii, 1.0, 0.0)
        for g0 in range(0, gs, GDN_ILP_CHUNKS_BWD):
            where = [(g, h) for g in range(g0, g0 + GDN_ILP_CHUNKS_BWD) for h in range(nh)]
            at = [(slice(g * ck, (g + 1) * ck), _gdn_head_cols(h)) for g, h in where]
            ins = [_gdn_chunk_inputs(x_ref, aux_ref, auxt_ref, g, h) for g, h in where]
            lcs = _gdn_local([(q, k, b, gc, gc_row) for q, k, _, b, gc, gc_row in ins], False)
            tinvs = [tinv_ref[slice(g * ck, (g + 1) * ck), h * ck:(h + 1) * ck] for g, h in where]
            dms = [jnp.where(lc["lower"], _bdot(do_ref[r, c], vn_ref[r, c], NT), 0.0) for lc, (r, c) in zip(lcs, at)]
            drvs = _dot3_many(tinvs, [dvn_ref[r, c] for r, c in at], TN)
            drks = _dot3_many(tinvs, [dw_ref[r, c] for r, c in at], TN)
            das = [jnp.where(lc["strict"], -(_bdot(drv, u_ref[r, c], NT) + _bdot(drk, w_ref[r, c], NT)), 0.0)
                   for lc, (r, c), drv, drk in zip(lcs, at, drvs, drks)]
            f_mats = [da * (i[3] * lc["kk"]) * lc["dmat"] + dm * lc["qk"] * lc["dmat"]
                      for i, lc, da, dm in zip(ins, lcs, das, dms)]
            col_sums = _dot3_many(f_mats, [ones] * len(where), TN)
            dgc_all = {g: dlast_ref[slice(g * ck, (g + 1) * ck), :] for g in range(g0, g0 + GDN_ILP_CHUNKS_BWD)}
            db_all = {g: jnp.zeros((ck, LANES), F32) for g in range(g0, g0 + GDN_ILP_CHUNKS_BWD)}
            e_mats = [da * lc["dmat"] * i[3] for i, lc, da in zip(ins, lcs, das)]
            dmds = [dm * lc["dmat"] for lc, dm in zip(lcs, dms)]
            dq_mm = [_bdot(dmd, i[1]) for i, dmd in zip(ins, dmds)]
            dk_mm = [_bdot(e, i[1]) + _bdot(e, i[1], TN) + _bdot(dmd, i[0], TN) for i, e, dmd in zip(ins, e_mats, dmds)]
            for n, ((g, h), (q, k, v, b, _, _), lc, (rows, cols)) in enumerate(zip(where, ins, lcs, at)):
                dmat, kk, gam, kdec = (lc[key] for key in ("dmat", "kk", "gam", "kdec"))
                drv, drk, da = drvs[n], drks[n], das[n]
                dqd_h, dkd_h = dqd_ref[rows, cols], dkd_ref[rows, cols]
                rs_rk = jnp.sum(drk * k, axis=-1, keepdims=True)
                db = (jnp.sum(drv * v, axis=-1, keepdims=True) + gam * rs_rk
                      + jnp.sum(da * kk * dmat, axis=-1, keepdims=True))
                dx_ref[rows, cols] = dq_mm[n] + gam * dqd_h
                dx_ref[rows, gw + cols.start:gw + cols.stop] = (b * gam) * drk + dk_mm[n] + kdec * dkd_h
                dx_ref[rows, 2 * gw + cols.start:2 * gw + cols.stop] = b * drv
                e_vec = jnp.sum(dkd_h * (kdec * k), axis=-1, keepdims=True)
                dgc = (b * gam * rs_rk + gam * jnp.sum(dqd_h * q, axis=-1, keepdims=True)
                       + jnp.sum(f_mats[n], axis=-1, keepdims=True) - col_sums[n][:, 0:1] - e_vec)
                is_last = lax.broadcasted_iota(jnp.int32, (ck, 1), 0) == ck - 1
                dgc = dgc + jnp.where(is_last, jnp.sum(e_vec, axis=0, keepdims=True), 0.0)
                dgc_all[g] = dgc_all[g] + jnp.where(lane == h, dgc, 0.0)
                db_all[g] = db_all[g] + jnp.where(lane == nh + h, db, 0.0)
            for g in dgc_all:
                daux_ref[slice(g * ck, (g + 1) * ck), :] = _dot3(suffix, dgc_all[g]) + db_all[g]

    return pl.pallas_call(
        body, grid=(t // (gs * ck),),
        in_specs=_gdn_specs(t, (3 * gw, LANES)) + [pl.BlockSpec((gs, 16, ck), lambda i: (i, 0, 0))]
        + _gdn_specs(t, (sq, gw, gw, gw, gw, gw, gw, gw, gw, LANES)),
        out_specs=_gdn_specs(t, (3 * gw, LANES)),
        out_shape=[jax.ShapeDtypeStruct((t, 3 * gw), F32), jax.ShapeDtypeStruct((t, LANES), F32)],
        compiler_params=_cp(("parallel",)), name=name,
    )(qkvn, aux, aux_t, tinv, u, w, vn, do, dvn, dqd, dkd, dw, dlast)


def _gdn_pre_bwd1(xc, dqkvn, daux, ab, alog_pad, dt_pad, dkv, dp, dp_col, *, name, ab_blk=0, tm=256):
    t, cw = xc.shape
    hd, nh = GDN_HEAD_DIM, GDN_HEADS
    gw = nh * hd
    tm = _blk(t, tm)

    kvw = dkv.shape[1]
    seg = kvw + AB_PAD
    assert dp_col % seg == 0

    def body(xc_ref, dy_ref, daux_ref, ab_ref, al_ref, dt_ref, dkv_ref, _, dxc_ref, dab_ref, dal_ref, ddt_ref):
        i = pl.program_id(0)
        xc = xc_ref[...]
        sg = _sigmoid(xc)
        s = xc * sg
        dsilu = sg * (1.0 + xc * (1.0 - sg))
        for h in range(2 * nh):
            xh = s[:, h * hd:(h + 1) * hd]
            scale = hd ** -0.5 if h < nh else 1.0
            dyh = dy_ref[:, h * hd:(h + 1) * hd] * scale
            r = lax.rsqrt(jnp.sum(xh * xh, axis=-1, keepdims=True) + L2_EPS)
            dxh = r * dyh - xh * (r * r * r) * jnp.sum(dyh * xh, axis=-1, keepdims=True)
            dxc_ref[:, h * hd:(h + 1) * hd] = dxh * dsilu[:, h * hd:(h + 1) * hd]
        dxc_ref[:, 2 * gw:] = dy_ref[:, 2 * gw:] * dsilu[:, 2 * gw:]
        abv = ab_ref[...]
        dauxv = daux_ref[...]
        lane = lax.broadcasted_iota(jnp.int32, abv.shape, 1)
        is_a = lane < nh
        is_b = (lane >= nh) & (lane < 2 * nh)
        pre = abv + dt_ref[...]
        neg_ea = -jnp.exp(al_ref[...])
        d_a = jnp.where(is_a, dauxv * neg_ea * _sigmoid(pre), 0.0)
        beta = _sigmoid(abv)
        d_b = jnp.where(is_b, dauxv * beta * (1.0 - beta), 0.0)
        dab_ref[:, :kvw] = dkv_ref[...]
        dab_ref[:, kvw:kvw + LANES] = (d_a + d_b).astype(dab_ref.dtype)
        dab_ref[:, kvw + LANES:] = jnp.zeros((tm, AB_PAD - LANES), dab_ref.dtype)
        dal = jnp.sum(jnp.where(is_a, dauxv * neg_ea * _softplus(pre), 0.0), axis=0, keepdims=True)
        ddt = jnp.sum(d_a, axis=0, keepdims=True)

        @pl.when(i == 0)
        def _():
            dal_ref[...] = dal
            ddt_ref[...] = ddt

        @pl.when(i > 0)
        def _():
            dal_ref[...] += dal
            ddt_ref[...] += ddt

    row = lambda c: pl.BlockSpec((tm, c), lambda i: (i, 0))
    vec = pl.BlockSpec((1, LANES), lambda i: (0, 0))
    return pl.pallas_call(
        body, grid=(t // tm,),
        in_specs=[row(cw), row(cw), row(LANES), pl.BlockSpec((tm, LANES), lambda i: (i, ab_blk)), vec, vec, row(kvw),
                  ANY_SPEC],
        out_specs=[row(cw), pl.BlockSpec((tm, seg), lambda i: (i, dp_col // seg)), vec, vec],
        out_shape=[jax.ShapeDtypeStruct((t, cw), F32), jax.ShapeDtypeStruct(dp.shape, dp.dtype),
                   jax.ShapeDtypeStruct((1, LANES), F32), jax.ShapeDtypeStruct((1, LANES), F32)],
        input_output_aliases={7: 1}, compiler_params=_cp(("arbitrary",)), name=name,
    )(xc, dqkvn, daux, ab, alog_pad, dt_pad, dkv, dp)


def _gdn_pre_bwd2(dxc, qkvb, conv_w, dp, dp_col, *, name, tm=512):
    t, cw = dxc.shape
    tm = _blk(t, tm)
    hb = tm // HALO
    nblk = t // tm
    cg = GDN_HEADS * GDN_HEAD_DIM
    assert cw % cg == 0 and dp_col % cg == 0
    col0 = dp_col // cg

    def body(d_ref, dn_ref, x_ref, xp_ref, w_ref, _, dx_ref, dw_ref):
        i = pl.program_id(1)
        dcur = d_ref[...]
        dnxt = jnp.where(i < nblk - 1, dn_ref[...], 0.0)
        cur = x_ref[...]
        prev = jnp.where(i > 0, xp_ref[...], 0.0)
        dx = None
        dws = []
        for tap in range(GDN_CONV):
            j = GDN_CONV - 1 - tap
            term = w_ref[tap:tap + 1, :] * _shift_up(dcur, dnxt, j)
            dx = term if dx is None else dx + term
            dws.append(jnp.sum(dcur * _shift_down(cur, prev, j), axis=0, keepdims=True))
        dx_ref[...] = dx.astype(dx_ref.dtype)
        dw = jnp.concatenate(dws, axis=0)

        @pl.when(i == 0)
        def _():
            dw_ref[...] = dw

        @pl.when(i > 0)
        def _():
            dw_ref[...] += dw

    row = pl.BlockSpec((tm, cg), lambda c, i: (i, c))
    wsp = pl.BlockSpec((GDN_CONV, cg), lambda c, i: (0, c))
    return pl.pallas_call(
        body, grid=(cw // cg, nblk),
        in_specs=[row, pl.BlockSpec((HALO, cg), lambda c, i: (jnp.minimum((i + 1) * hb, t // HALO - 1), c)),
                  row, pl.BlockSpec((HALO, cg), lambda c, i: (jnp.maximum(i * hb - 1, 0), c)), wsp, ANY_SPEC],
        out_specs=[pl.BlockSpec((tm, cg), lambda c, i: (i, col0 + c)), wsp],
        out_shape=[jax.ShapeDtypeStruct(dp.shape, dp.dtype), jax.ShapeDtypeStruct((GDN_CONV, cw), F32)],
        input_output_aliases={5: 0}, compiler_params=_cp(("arbitrary", "arbitrary")), name=name,
    )(dxc, dxc, qkvb, qkvb, conv_w, dp)


def _gdn_post_fwd(o, z, norm_w, *, name, tm=512):
    t, gw = o.shape
    hd, nh = GDN_HEAD_DIM, GDN_HEADS
    tm = _blk(t, tm)

    def body(o_ref, z_ref, w_ref, y_ref):
        zv = z_ref[...]
        sz = zv * _sigmoid(zv)
        for h in range(nh):
            oh = o_ref[:, h * hd:(h + 1) * hd]
            r = lax.rsqrt(jnp.mean(oh * oh, axis=-1, keepdims=True) + RMS_EPS)
            y_ref[:, h * hd:(h + 1) * hd] = (oh * r * w_ref[...] * sz[:, h * hd:(h + 1) * hd]).astype(y_ref.dtype)

    row = pl.BlockSpec((tm, gw), lambda i: (i, 0))
    return pl.pallas_call(
        body, grid=(t // tm,), in_specs=[row, row, pl.BlockSpec((1, hd), lambda i: (0, 0))], out_specs=row,
        out_shape=jax.ShapeDtypeStruct((t, gw), BF16), compiler_params=_cp(("parallel",)), name=name,
    )(o, z, norm_w)


def _gdn_post_bwd(dy, o, z, norm_w, dp, dp_col, *, name, tm=512):
    t, gw = o.shape
    hd, nh = GDN_HEAD_DIM, GDN_HEADS
    tm = _blk(t, tm)

    def body(dy_ref, o_ref, z_ref, w_ref, _, do_ref, dz_ref, dw_ref):
        i = pl.program_id(0)
        zv = z_ref[...]
        sg = _sigmoid(zv)
        sz = zv * sg
        dsz = sg * (1.0 + zv * (1.0 - sg))
        dw = None
        for h in range(nh):
            sl = slice(h * hd, (h + 1) * hd)
            oh = o_ref[:, sl]
            dyh = dy_ref[:, sl].astype(F32)
            r = lax.rsqrt(jnp.mean(oh * oh, axis=-1, keepdims=True) + RMS_EPS)
            xh = oh * r
            dz_ref[:, sl] = (dyh * xh * w_ref[...] * dsz[:, sl]).astype(dz_ref.dtype)
            dn = dyh * sz[:, sl]
            dxh = dn * w_ref[...]
            do_ref[:, sl] = r * (dxh - xh * jnp.mean(dxh * xh, axis=-1, keepdims=True))
            part = jnp.sum(dn * xh, axis=0, keepdims=True)
            dw = part if dw is None else dw + part

        @pl.when(i == 0)
        def _():
            dw_ref[...] = dw

        @pl.when(i > 0)
        def _():
            dw_ref[...] += dw

    row = pl.BlockSpec((tm, gw), lambda i: (i, 0))
    vec = pl.BlockSpec((1, hd), lambda i: (0, 0))
    return pl.pallas_call(
        body, grid=(t // tm,), in_specs=[row, row, row, vec, ANY_SPEC],
        out_specs=[row, pl.BlockSpec((tm, gw), lambda i: (i, dp_col // gw)), vec],
        out_shape=[jax.ShapeDtypeStruct((t, gw), F32), jax.ShapeDtypeStruct(dp.shape, dp.dtype),
                   jax.ShapeDtypeStruct((1, hd), F32)],
        input_output_aliases={4: 1}, compiler_params=_cp(("arbitrary",)), name=name,
    )(dy, o, z, norm_w, dp)


IN_NAMES = ("q_a", "kv_a", "qkv_b", "ab", "z", "q_c", "gates")
CAT_NAMES = ("gates", "q_a", "qkv_b", "z", "q_c", "kv_a", "ab")
AB_PAD = 256


def _in_widths(d):
    gw = GDN_HEADS * GDN_HEAD_DIM
    return dict(q_a=SWA_Q_HEADS * SWA_HEAD_DIM, kv_a=2 * SWA_KV_HEADS * SWA_HEAD_DIM, qkv_b=3 * gw, ab=2 * GDN_HEADS,
                z=gw, q_c=XA_HEADS * XA_HEAD_DIM, gates=3 * d)


def _ranges(names, widths):
    out, start = {}, 0
    for k in names:
        out[k] = (start, widths[k])
        start += widths[k]
    return out, start


def _cat_ranges(d):
    widths = dict(_in_widths(d), ab=AB_PAD)
    return _ranges(CAT_NAMES, widths)


def _to_cat(shards):
    ns, d, n = shards.shape
    src, _ = _ranges(IN_NAMES, _in_widths(d))
    cols = []
    for k in CAT_NAMES:
        lo, hi = src[k][0], src[k][0] + src[k][1]
        for s in range(ns):
            a, b = max(lo, s * n), min(hi, (s + 1) * n)
            if a < b:
                cols.append(shards[s][:, a - s * n:b - s * n])
    cols.append(jnp.zeros((d, AB_PAD - src["ab"][1]), shards.dtype))
    return jnp.concatenate(cols, axis=1)


def _from_cat(w_cat):
    d = w_cat.shape[0]
    src, total = _ranges(IN_NAMES, _in_widths(d))
    cat, _ = _cat_ranges(d)
    n = total // N_SHARDS
    shards = []
    for s in range(N_SHARDS):
        pieces = []
        for k in IN_NAMES:
            a, b = max(s * n, src[k][0]), min((s + 1) * n, src[k][0] + src[k][1])
            if a < b:
                pieces.append(w_cat[:, cat[k][0] + a - src[k][0]:cat[k][0] + b - src[k][0]])
        shards.append(jnp.concatenate(pieces, axis=1))
    return jnp.stack(shards)


def _pad_cols(a, width):
    return jnp.pad(a, ((0, 0), (0, width - a.shape[1])))


def _relu2_epilogue(acc):
    r = jnp.maximum(acc, 0.0)
    return acc, r * r


def _add_epilogue(acc, res):
    return (acc + res,)


def _drelu2_epilogue(acc, u):
    return (acc * (2.0 * jnp.maximum(u.astype(F32), 0.0)),)


def _local_step(x, mem, tgt, wts, small, comm=None):
    t, d = x.shape
    nh = GDN_HEADS
    w_cat = wts["w_cat"]
    cat, cat_w = _cat_ranges(d)
    assert w_cat.shape == (d, cat_w)
    alog_pad = _pad_cols(small["a_log"], LANES)
    dt_pad = _pad_cols(small["dt_bias"], LANES)
    kvw = cat["kv_a"][1]
    assert cat["ab"][0] == cat["kv_a"][0] + kvw
    ab_blk = kvw // LANES

    n = _rms_fwd(x, small["g_mix"], name="rms_mix")
    q_a = _mm(n, w_cat, b_window=cat["q_a"], out_dtypes=(BF16,), name="in_q_a")
    kv_a, ab = _mm(n, w_cat, b_window=(cat["kv_a"][0], kvw + AB_PAD), out_dtypes=(BF16, F32), name="in_kv_ab")
    qkvb = _mm(n, w_cat, b_window=cat["qkv_b"], tn=512, name="in_qkv_b")
    z = _mm(n, w_cat, b_window=cat["z"], name="in_z")
    q_c = _mm(n, w_cat, b_window=cat["q_c"], out_dtypes=(BF16,), name="in_q_c")
    if comm is None:
        gates = _mm(n, w_cat, b_window=cat["gates"], name="in_gates")
        y_a, lse = _swa_fwd(q_a, kv_a, small["sinks"], name="swa_fwd")
    else:
        gates, landed_mlp = _mm(n, w_cat, b_window=cat["gates"], name="in_gates",
                                exchange=comm.gather_exchange(comm.MLP[1:]))
        y_a, lse, landed = _swa_fwd(q_a, kv_a, small["sinks"], name="swa_fwd", exchange=comm.gather_exchange(comm.MLP[:1]))
        landed_mlp = landed + landed_mlp
    xc, qkvn, aux = _gdn_pre_fwd(qkvb, small["conv_w"], ab, alog_pad, dt_pad, ab_blk=ab_blk, name="gdn_pre_fwd")
    aux_t = aux[:, :16].reshape(t // GDN_CHUNK, GDN_CHUNK, 16).transpose(0, 2, 1)
    if comm is None:
        gdn_u, gdn_w, gdn_qd, gdn_kd, gdn_mm, gdn_tinv = _gdn_local_fwd(qkvn, aux, aux_t, name="gdn_local_fwd")
    else:
        gdn_u, gdn_w, gdn_qd, gdn_kd, gdn_mm, gdn_tinv, landed_mid = _gdn_local_fwd(
            qkvn, aux, aux_t, name="gdn_local_fwd", exchange=comm.gather_exchange(comm.mid))
        wts = dict(wts, **comm.gathered(comm.mid, landed_mid, "mid"))
    o_b, gdn_vn, s_all = _gdn_seq_fwd(gdn_u, gdn_w, gdn_qd, gdn_kd, gdn_mm, aux, name="gdn_seq_fwd")
    y_b = _gdn_post_fwd(o_b, z, small["gdn_norm_w"], name="gdn_post_fwd")
    nmem = _rms_fwd(mem, small["g_mem"], name="rms_mem")
    mkv = _mm(nmem, wts["w_mem_kv"], out_dtypes=(BF16,), name="mem_kv")
    y_c = _xa_fwd(q_c, mkv, name="xa_fwd")
    ys = (y_a, y_b, y_c)
    w_ups = (wts["w_swa_up"], wts["w_gdn_up"], wts["w_xa_up"])
    merged = _merge_fwd(ys, w_ups, gates, name="merge_fwd")
    if comm is None:
        h1 = _mm(merged, wts["w_out"], extras=(x,), epilogue=_add_epilogue, name="out_proj")
    else:
        h1, whole = _mm(merged, wts["w_out"], extras=(x,), epilogue=_add_epilogue, name="out_proj",
                        exchange=_gather_pass_on(landed_mlp))
        wts = dict(wts, **comm.as_weights(comm.MLP, whole))
    n2 = _rms_fwd(h1, small["g_mlp"], name="rms_mlp")
    u, act = _mm(n2, wts["w_mlp_in"], b_sharded=True, out_dtypes=(BF16, BF16), epilogue=_relu2_epilogue, name="mlp_in")
    h2 = _mm(act, wts["w_mlp_out"], extras=(h1,), epilogue=_add_epilogue, name="mlp_out")
    dh2, dh2_b, dg_final, loss = _final_loss(h2, small["g_final"], tgt, name="final_loss")

    grads = {"g_final": dg_final}
    du = _mm(dh2_b, wts["w_mlp_out"], tb=True, out_dtypes=(BF16,), extras=(u,), epilogue=_drelu2_epilogue, name="d_mlp_act")
    grads["w_mlp_out"] = _mm(act, dh2_b, ta=True, out_dtypes=(BF16,), name="dw_mlp_out")
    grads["w_mlp_in"] = _mm(n2, du, ta=True, out_sharded=True, out_dtypes=(BF16,), name="dw_mlp_in")
    if comm is None:
        dn2 = _mm(du, wts["w_mlp_in"], tb=True, b_sharded=True, name="d_mlp_in")
    else:
        g_mlp = [comm.shard_major(k, grads.pop(k)) for k in comm.MLP]
        dn2, sib_mlp = _mm(du, wts["w_mlp_in"], tb=True, b_sharded=True, name="d_mlp_in", exchange=_sibling_halves(g_mlp))
        s1_mlp = comm.pair_sums(g_mlp, "mlp", sib_mlp)
    dh1, dh1_b, grads["g_mlp"] = _rms_bwd(dn2, h1, small["g_mlp"], dh2, name="rms_mlp_bwd")
    dmerged = _mm(dh1_b, wts["w_out"], tb=True, name="d_out_proj")
    grads["w_out"] = _mm(merged, dh1_b, ta=True, out_dtypes=(BF16,), name="dw_out")
    *dus, dp = _merge_bwd(ys, w_ups, gates, dmerged, cat_w, name="merge_bwd")
    dys = []
    for y, du_i, w_up, key in zip(ys, dus, w_ups, ("w_swa_up", "w_gdn_up", "w_xa_up")):
        dys.append(_mm(du_i, w_up, tb=True, b_sharded=True, out_dtypes=(BF16,), name="d_" + key))
        grads[key] = _mm(y, du_i, ta=True, out_sharded=True, out_dtypes=(BF16,), name="dw_" + key[2:])
    dp, dkv_a, grads["sinks"] = _swa_bwd(q_a, kv_a, small["sinks"], y_a, lse, dys[0], dp, cat["q_a"][0], name="swa_bwd")
    do_b, dp, grads["gdn_norm_w"] = _gdn_post_bwd(dys[1], o_b, z, small["gdn_norm_w"], dp, cat["z"][0],
                                                  name="gdn_post_bwd")
    dvn, dqd, dkd, dw_, dlast = _gdn_seq_bwd(do_b, gdn_w, gdn_qd, gdn_kd, gdn_mm, gdn_vn, s_all, aux, name="gdn_seq_bwd")
    dqkvn, daux = _gdn_local_bwd(qkvn, aux, aux_t, gdn_tinv, gdn_u, gdn_w, gdn_vn, do_b, dvn, dqd, dkd, dw_, dlast,
                                 name="gdn_local_bwd")
    dxc, dp, dalog, ddt = _gdn_pre_bwd1(xc, dqkvn, daux, ab, alog_pad, dt_pad, dkv_a, dp, cat["kv_a"][0], ab_blk=ab_blk,
                                        name="gdn_pre_bwd1")
    grads["a_log"], grads["dt_bias"] = dalog[:, :nh], ddt[:, :nh]
    dp, grads["conv_w"] = _gdn_pre_bwd2(dxc, qkvb, small["conv_w"], dp, cat["qkv_b"][0], name="gdn_pre_bwd2")
    dp, dmkv = _xa_bwd(q_c, mkv, dys[2], dp, cat["q_c"][0], name="xa_bwd")
    grads["w_mem_kv"] = _mm(nmem, dmkv, ta=True, out_dtypes=(BF16,), name="dw_mem_kv")
    dnmem = _mm(dmkv, wts["w_mem_kv"], tb=True, name="d_mem_kv")
    _, _, grads["g_mem"] = _rms_bwd(dnmem, mem, small["g_mem"], jnp.zeros_like(mem), name="rms_mem_bwd")
    if comm is None:
        grads["w_cat"] = _mm(n, dp, ta=True, out_dtypes=(BF16,), name="dw_in")
        dn = _mm(dp, w_cat, tb=True, name="d_in_proj")
    else:
        s1_mid = comm.pair_sums([comm.shard_major(k, grads.pop(k)) for k in comm.mid], "mid")
        dw_cat, rcv_mlp = _mm(n, dp, ta=True, out_dtypes=(BF16,), name="dw_in", exchange=_chip_exchange(s1_mlp))
        s1_in = comm.pair_sums([_from_cat(dw_cat)], "in")
        dn, rcv_rest = _mm(dp, w_cat, tb=True, name="d_in_proj", exchange=_chip_exchange(s1_in + s1_mid))
        reduced = comm.finish(s1_in + s1_mid + s1_mlp, rcv_rest + rcv_mlp)
        grads.update(zip(["w_in"] + comm.mid + list(comm.MLP), reduced))
    dx, _, grads["g_mix"] = _rms_bwd(dn, x, small["g_mix"], dh1, name="rms_mix_bwd")
    return loss, dx, grads


HBM_SPEC = pl.BlockSpec(memory_space=pltpu.HBM)
VMEM_SPEC = pl.BlockSpec(memory_space=pltpu.VMEM)
N_CHIPS = N_SHARDS
N_DEV = 8
DMA_CHUNK_BYTES = 1 << 20


def _place():
    return lax.axis_index("x"), lax.axis_index("y"), lax.axis_index("c")


def _other_chips(x, y):
    return [(1 - x, y), (x, 1 - y), (1 - x, 1 - y)]


def _n_chunks(rows, row_bytes):
    n = 1
    while rows % (2 * n) == 0 and (rows // (2 * n)) % 16 == 0 and (rows // n) * row_bytes > DMA_CHUNK_BYTES:
        n *= 2
    return n


def _sem_scratch(n_remote, n_local):
    return [pltpu.SemaphoreType.DMA((max(n_remote, 1),)), pltpu.SemaphoreType.DMA((max(n_remote, 1),)),
            pltpu.SemaphoreType.DMA((max(n_local, 1),))]


def _all_gather_weights(shards, *, name):
    first = _exchange_call(_gather_over_ici(shards), name=name + "_ici")
    return _exchange_call(_gather_pass_on(first), name=name + "_pass")


def _gather_over_ici(shards):
    plan = _half_chunks(shards, 0)

    def copies_of(in_refs, out_refs, place):
        x, y, c = place
        remote, local = [], []
        for i, r0, nr in plan:
            rh = shards[i].shape[0] // 2
            mine = pl.ds(c * rh + r0, nr)
            for chip in _other_chips(x, y):
                remote.append((in_refs[i].at[mine], out_refs[i].at[2 * x + y, mine], (*chip, c)))
            for half in range(2):
                rows = pl.ds(half * rh + r0, nr)
                local.append((in_refs[i].at[rows], out_refs[i].at[2 * x + y, rows]))
        return remote, local

    shapes = tuple(jax.ShapeDtypeStruct((N_CHIPS, *s.shape), s.dtype) for s in shards)
    return Exchange(tuple(shards), shapes, 3 * len(plan), 2 * len(plan), copies_of)


def _gather_pass_on(arrived):
    plan = _half_chunks([jax.ShapeDtypeStruct(a.shape[1:], a.dtype) for a in arrived], 0)

    def copies_of(in_refs, out_refs, place):
        x, y, c = place
        remote = []
        for i, r0, nr in plan:
            mine = pl.ds(c * (arrived[i].shape[1] // 2) + r0, nr)
            for chip in _other_chips(x, y):
                rows = out_refs[i].at[2 * chip[0] + chip[1], mine]
                remote.append((rows, rows, (x, y, 1 - c)))
        return remote, []

    shapes = tuple(jax.ShapeDtypeStruct(a.shape, a.dtype) for a in arrived)
    return Exchange(tuple(arrived), shapes, 3 * len(plan), 0, copies_of, tuple((i, i) for i in range(len(arrived))))


def _exchange_call(ex, *, name):
    n_in, n_out = len(ex.ins), len(ex.out_shapes)

    def body(*refs):
        cps = _exchange_copies(ex, refs[:n_in], refs[n_in:n_in + n_out], refs[n_in + n_out:])
        for cp in cps:
            cp.start()
        for cp in cps:
            cp.wait()

    return pl.pallas_call(
        body, out_shape=list(ex.out_shapes), in_specs=[HBM_SPEC] * n_in, out_specs=[HBM_SPEC] * n_out,
        scratch_shapes=_sem_scratch(ex.n_remote, ex.n_local), input_output_aliases=dict(ex.aliases), name=name,
    )(*ex.ins)


def _half_chunks(arrs, row_axis):
    plan = []
    for i, a in enumerate(arrs):
        rh = a.shape[row_axis] // 2
        row_bytes = a.dtype.itemsize * math.prod(a.shape) // a.shape[row_axis]
        nch = _n_chunks(rh, row_bytes)
        plan += [(i, q * (rh // nch), rh // nch) for q in range(nch)]
    return plan


def _sibling_halves(gs):
    plan = _half_chunks(gs, 1)

    def copies_of(in_refs, out_refs, place):
        x, y, c = place
        out = []
        for i, r0, nr in plan:
            rh = gs[i].shape[1] // 2
            out.append((in_refs[i].at[:, pl.ds((1 - c) * rh + r0, nr), :], out_refs[i].at[:, pl.ds(r0, nr), :],
                        (x, y, 1 - c)))
        return out, []

    shapes = tuple(jax.ShapeDtypeStruct((g.shape[0], g.shape[1] // 2, g.shape[2]), g.dtype) for g in gs)
    return Exchange(tuple(gs), shapes, len(plan), 0, copies_of)


def _chip_exchange(s1s):
    plan = _half_chunks([jax.ShapeDtypeStruct((2 * s.shape[1], s.shape[2]), s.dtype) for s in s1s], 0)

    def copies_of(in_refs, out_refs, place):
        x, y, c = place
        out = []
        for i, r0, nr in plan:
            for j, chip in enumerate(_other_chips(x, y)):
                out.append((in_refs[i].at[2 * chip[0] + chip[1], pl.ds(r0, nr), :], out_refs[i].at[j, pl.ds(r0, nr), :],
                            (*chip, c)))
        return out, []

    shapes = tuple(jax.ShapeDtypeStruct((3, *s.shape[1:]), s.dtype) for s in s1s)
    return Exchange(tuple(s1s), shapes, 3 * len(plan), 0, copies_of)


def _join_halves(gs, *, name):
    plan = _half_chunks(gs, 0)

    def copies_of(in_refs, out_refs, place):
        x, y, c = place
        out = []
        for i, r0, nr in plan:
            rows = out_refs[i].at[pl.ds(c * (gs[i].shape[0] // 2) + r0, nr), :]
            out.append((rows, rows, (x, y, 1 - c)))
        return out, []

    shapes = tuple(jax.ShapeDtypeStruct(g.shape, g.dtype) for g in gs)
    aliases = tuple((i, i) for i in range(len(gs)))
    return _exchange_call(Exchange(tuple(gs), shapes, len(plan), 0, copies_of, aliases), name=name)


def _row_block(rows, cols):
    tb = rows
    while tb % 32 == 0 and tb * cols * 4 > (2 << 20):
        tb //= 2
    return tb


def _pair_sum(g, sib, core, *, name):
    ns, r, c = g.shape
    rh = r // 2
    tb = _row_block(rh, c)
    nb = rh // tb

    def body(core_ref, g_ref, s_ref, o_ref):
        o_ref[...] = (g_ref[...].astype(F32) + s_ref[...].astype(F32)).astype(o_ref.dtype)

    mine = pl.BlockSpec((None, tb, c), lambda s, i, core_ref: (s, core_ref[0] * nb + i, 0))
    half = pl.BlockSpec((None, tb, c), lambda s, i, core_ref: (s, i, 0))
    return pl.pallas_call(
        body, grid_spec=pltpu.PrefetchScalarGridSpec(num_scalar_prefetch=1, grid=(ns, nb), in_specs=[mine, half],
                                                     out_specs=half),
        out_shape=jax.ShapeDtypeStruct((ns, rh, c), BF16), compiler_params=_cp(("parallel", "parallel")), name=name,
    )(core, g, sib)


def _chip_sum(s1, rcv, where, *, name):
    _, rh, c = s1.shape
    tb = _row_block(rh, c)
    nb = rh // tb

    def body(where_ref, own_ref, r0_ref, r1_ref, r2_ref, o_ref):
        acc = own_ref[...].astype(F32)
        for r in (r0_ref, r1_ref, r2_ref):
            acc = acc + r[...].astype(F32)
        o_ref[...] = acc

    own = pl.BlockSpec((None, tb, c), lambda i, w: (w[1], i, 0))
    got = [pl.BlockSpec((None, tb, c), functools.partial(lambda i, w, j: (j, i, 0), j=j)) for j in range(3)]
    return pl.pallas_call(
        body, grid_spec=pltpu.PrefetchScalarGridSpec(
            num_scalar_prefetch=1, grid=(nb,), in_specs=[own] + got,
            out_specs=pl.BlockSpec((tb, c), lambda i, w: (w[0] * nb + i, 0))),
        out_shape=jax.ShapeDtypeStruct((2 * rh, c), F32), compiler_params=_cp(("parallel",)), name=name,
    )(where, s1, rcv, rcv, rcv)


def _all_gather_small(blk, *, name):
    r = blk.shape[0]

    def body(b_ref, out_ref, send_sems, recv_sems):
        x, y, c = _place()
        me = 4 * x + 2 * y + c
        out_ref[me] = b_ref[...]
        sends = []
        for k in range(1, N_DEV):
            peer = (x ^ (k >> 2), y ^ ((k >> 1) & 1), c ^ (k & 1))
            sends.append(pltpu.make_async_remote_copy(src_ref=b_ref, dst_ref=out_ref.at[me], send_sem=send_sems.at[k - 1],
                                                      recv_sem=recv_sems.at[k - 1], device_id=peer, device_id_type=MESH))
        for cp in sends:
            cp.start()
        for k in range(1, N_DEV):
            rows = out_ref.at[me ^ k]
            pltpu.make_async_remote_copy(src_ref=rows, dst_ref=rows, send_sem=send_sems.at[k - 1],
                                         recv_sem=recv_sems.at[k - 1], device_id=(x, y, c), device_id_type=MESH).wait_recv()
        for cp in sends:
            cp.wait_send()

    return pl.pallas_call(
        body, out_shape=jax.ShapeDtypeStruct((N_DEV, r, LANES), blk.dtype), in_specs=[VMEM_SPEC], out_specs=VMEM_SPEC,
        scratch_shapes=[pltpu.SemaphoreType.DMA((N_DEV - 1,)), pltpu.SemaphoreType.DMA((N_DEV - 1,))],
        name=name,
    )(blk)


def _sum_rows(parts, out_dtype, *, name, tb=1024):
    rows = parts[0].shape[0]
    tb = _blk(rows, tb)

    def body(*refs):
        acc = refs[0][...].astype(F32)
        for r in refs[1:-1]:
            acc = acc + r[...].astype(F32)
        refs[-1][...] = acc.astype(refs[-1].dtype)

    spec = pl.BlockSpec((tb, LANES), lambda i: (i, 0))
    return pl.pallas_call(
        body, grid=(rows // tb,), in_specs=[spec] * len(parts), out_specs=spec,
        out_shape=jax.ShapeDtypeStruct((rows, LANES), out_dtype), compiler_params=_cp(("parallel",)), name=name,
    )(*parts)


BIG = (
    ("w_in", 1), ("w_mem_kv", 0), ("w_swa_up", 1), ("w_gdn_up", 1), ("w_xa_up", 1), ("w_out", 0), ("w_mlp_in", 1),
    ("w_mlp_out", 0))


class _Comm:
    MLP = ("w_mlp_in", "w_mlp_out")

    def __init__(self, late_shards, core, where):
        self.axis = dict(BIG)
        self.late_shards = late_shards
        self.mid = [k for k in late_shards if k not in self.MLP]
        self.core, self.where = core, where

    def gather_exchange(self, names):
        return _gather_over_ici([self.late_shards[k] for k in names])

    def as_weights(self, names, whole):
        return {k: (g.reshape(-1, g.shape[2]) if self.axis[k] == 0 else g) for k, g in zip(names, whole)}

    def gathered(self, names, landed, tag):
        return self.as_weights(names, _exchange_call(_gather_pass_on(landed), name=f"ag_{tag}_pass"))

    def shard_major(self, k, grad):
        return grad.reshape(N_CHIPS, -1, grad.shape[-1]) if self.axis[k] == 0 else grad

    def pair_sums(self, gs, tag, sibs=None):
        if sibs is None:
            sibs = _exchange_call(_sibling_halves(gs), name=f"rs_sibling_{tag}")
        return [_pair_sum(g, s, self.core, name=f"rs_pair_sum_{tag}{i}") for i, (g, s) in enumerate(zip(gs, sibs))]

    def finish(self, s1s, rcvs):
        halves = [_chip_sum(s1, rcv, self.where, name=f"rs_chip_sum_{i}") for i, (s1, rcv) in enumerate(zip(s1s, rcvs))]
        return _join_halves(halves, name="rs_join_halves")
SMALL = ("g_mix", "sinks", "a_log", "dt_bias", "gdn_norm_w", "g_mem", "g_mlp", "g_final")


def _rows128(a, rows):
    flat = a.reshape(-1)
    return jnp.pad(flat, (0, rows * LANES - flat.shape[0])).reshape(rows, LANES)


def kernel(x, mem, g_mix, w_in, sinks, conv_w, a_log, dt_bias, gdn_norm_w, g_mem, w_mem_kv, w_swa_up, w_gdn_up, w_xa_up, w_out, g_mlp, w_mlp_in, w_mlp_out, g_final, loss_target, m_g_mix, m_w_in, m_sinks, m_conv_w, m_a_log, m_dt_bias, m_gdn_norm_w, m_g_mem, m_w_mem_kv, m_w_swa_up, m_w_gdn_up, m_w_xa_up, m_w_out, m_g_mlp, m_w_mlp_in, m_w_mlp_out, m_g_final, v_g_mix, v_w_in, v_sinks, v_conv_w, v_a_log, v_dt_bias, v_gdn_norm_w, v_g_mem, v_w_mem_kv, v_w_swa_up, v_w_gdn_up, v_w_xa_up, v_w_out, v_g_mlp, v_w_mlp_in, v_w_mlp_out, v_g_final):
    given = dict(locals())
    xi, yi, ci = _place()
    chip = 2 * xi + yi
    core = jnp.reshape(ci, (1,)).astype(jnp.int32)
    where = jnp.stack([ci, chip]).astype(jnp.int32)

    shards = {k: given[k][0].astype(BF16) for k, _ in BIG}
    wts = {"w_cat": _to_cat(_all_gather_weights([shards.pop("w_in")], name="ag_w_in")[0])}
    comm = _Comm(shards, core, where)
    conv_shard = conv_w[0]
    conv_rows = -(-conv_shard.size // (8 * LANES)) * 8
    conv_all = _all_gather_small(_rows128(conv_shard, conv_rows), name="ag_conv")
    conv_full = jnp.concatenate(
        [conv_all[2 * s].reshape(-1)[:conv_shard.size].reshape(conv_shard.shape) for s in range(N_CHIPS)], axis=1)

    small = {k: given[k].reshape(1, -1) for k in SMALL}
    small["conv_w"] = conv_full
    loss_row, dx, grads = _local_step(x[0], mem[0], loss_target[0], wts, small, comm)
    big_grads = {k: grads[k] for k, _ in BIG}

    layout = [("loss", loss_row[:, :1])] + [(k, grads[k]) for k in SMALL] + [("conv_w", grads["conv_w"])]
    rows = [-(-a.size // LANES) for _, a in layout]
    blk_rows = -(-sum(rows) // 8) * 8
    blk = jnp.concatenate([_rows128(a.astype(F32), n) for (_, a), n in zip(layout, rows)]
                          + [jnp.zeros((blk_rows - sum(rows), LANES), F32)], axis=0)
    gathered = _all_gather_small(blk, name="ag_small_grads")
    reduced = _sum_rows([gathered[i] for i in range(N_DEV)], F32, name="small_grad_sum")
    small_grads, start = {}, 0
    for (k, a), n in zip(layout, rows):
        small_grads[k] = reduced[start:start + n].reshape(-1)[:a.size].reshape(a.shape)
        start += n
    loss = small_grads["loss"].reshape(())
    cw = conv_shard.shape[1]
    conv_grad = lax.dynamic_slice_in_dim(small_grads["conv_w"], chip * cw, cw, axis=1)

    names = ["g_mix", "w_in", "sinks", "conv_w", "a_log", "dt_bias", "gdn_norm_w", "g_mem", "w_mem_kv", "w_swa_up",
             "w_gdn_up", "w_xa_up", "w_out", "g_mlp", "w_mlp_in", "w_mlp_out", "g_final"]
    out_g, out_d, out_m, out_v = [], [], [], []
    for k in names:
        w, m, v = given[k], given["m_" + k], given["v_" + k]
        if k in big_grads:
            g2 = big_grads[k]
        elif k == "conv_w":
            g2 = conv_grad
        else:
            g2 = small_grads[k]
        as_given = (lambda a: a.reshape(1, -1)) if w.ndim == 1 else (lambda a: a)
        if w.shape[-1] % LANES and w.shape[-1] > LANES:
            tr = lambda a: jnp.swapaxes(a, -1, -2)
            delta, new_m, new_v = (tr(a) for a in _adamw(tr(w), tr(g2), tr(m), tr(v), name="adamw_" + k))
        else:
            delta, new_m, new_v = _adamw(as_given(w), g2, as_given(m), as_given(v), name="adamw_" + k)
        out_g.append(g2.reshape(w.shape))
        out_d.append(delta.reshape(w.shape))
        out_m.append(new_m.reshape(w.shape))
        out_v.append(new_v.reshape(w.shape))
    return (loss, dx[None], *out_g, *out_d, *out_m, *out_v)
```

```python
import functools
import math
from typing import Callable, NamedTuple

import jax
import jax.numpy as jnp
from jax import lax
from jax.experimental import pallas as pl
from jax.experimental.pallas import tpu as pltpu

F32 = jnp.float32
BF16 = jnp.bfloat16
HI = lax.Precision.HIGHEST
MESH = pl.DeviceIdType.MESH

SWA_Q_HEADS = 16
SWA_KV_HEADS = 2
SWA_HEAD_DIM = 64
SWA_WINDOW = 128
SWA_SCALE = SWA_HEAD_DIM ** -0.5
assert math.frexp(SWA_SCALE)[0] == 0.5
GDN_HEADS = 4
GDN_HEAD_DIM = 128
GDN_CONV = 4
GDN_CHUNK = 64
XA_HEADS = 4
XA_HEAD_DIM = 128
RMS_EPS = 1e-6
L2_EPS = 1e-6
ADAM_LR = 0.001
ADAM_B1 = 0.9
ADAM_B2 = 0.999
ADAM_EPS = 1e-08
ADAM_WD = 0.01
ADAM_STEP = 10

LANES = 128
N_SHARDS = 4
VMEM_LIMIT = 56 * 1024 * 1024

NT = (((1,), (1,)), ((), ()))
TN = (((0,), (0,)), ((), ()))
NN = (((1,), (0,)), ((), ()))


def _cp(sem=None):
    return pltpu.CompilerParams(dimension_semantics=sem, vmem_limit_bytes=VMEM_LIMIT)


def _blk(dim, pref):
    if dim <= pref:
        return dim
    b = (pref // LANES) * LANES
    while dim % b:
        b -= LANES
    assert b > 0, (dim, pref)
    return b


def _dot(a, b, dims=NN, precision=None):
    return lax.dot_general(a, b, dims, precision=precision, preferred_element_type=F32)


def _sigmoid(x):
    return 1.0 / (1.0 + jnp.exp(-x))


MM_TK_BYTES = 4096


def _mm(a, b, *, name, ta=False, tb=False, out_dtypes=(F32,), epilogue=None, extras=(), tm=1024, tn=1024, tk=None,
        b_sharded=False, out_sharded=False, b_window=None, exchange=None):
    (kdim, m) = a.shape if ta else a.shape[::-1]
    col0 = 0
    n_lim = k_lim = None
    if b_sharded:
        ns, rows_w, per = b.shape
        if tb:
            kb, n, k_lim = ns * per, rows_w, per
        else:
            kb, n, n_lim = rows_w, ns * per, per
    else:
        (kb, n) = b.shape[::-1] if tb else b.shape
        if b_window is not None:
            assert not tb
            col0, n = b_window
    assert kdim == kb, (a.shape, b.shape, ta, tb)
    if out_sharded:
        assert n % N_SHARDS == 0
        n_lim = n // N_SHARDS if n_lim is None else n_lim
        assert n_lim == n // N_SHARDS
    if tk is None:
        tk = MM_TK_BYTES // max(a.dtype.itemsize, b.dtype.itemsize)
    tm, tn, tk = _blk(m, tm), _blk(n_lim or n, tn), _blk(k_lim or kdim, tk)
    assert col0 % tn == 0, (col0, tn)
    nk = kdim // tk
    a_spec = pl.BlockSpec((tk, tm), lambda i, j, k: (k, i)) if ta else pl.BlockSpec((tm, tk), lambda i, j, k: (i, k))
    if b_sharded and tb:
        kpb = k_lim // tk
        b_spec = pl.BlockSpec((None, tn, tk), lambda i, j, k: (k // kpb, j, k % kpb))
    elif b_sharded:
        bpb = n_lim // tn
        b_spec = pl.BlockSpec((None, tk, tn), lambda i, j, k: (j // bpb, k, j % bpb))
    elif tb:
        b_spec = pl.BlockSpec((tn, tk), lambda i, j, k: (j, k))
    else:
        b_spec = pl.BlockSpec((tk, tn), lambda i, j, k: (k, j + col0 // tn))
    x_spec = pl.BlockSpec((tm, tn), lambda i, j, k: (i, j))
    if out_sharded:
        opb = n_lim // tn
        o_spec = pl.BlockSpec((None, tm, tn), lambda i, j, k: (j // opb, i, j % opb))
        out_shape = (N_SHARDS, m, n_lim)
    else:
        o_spec, out_shape = x_spec, (m, n)
    dims = ((((0 if ta else 1),), ((1 if tb else 0),)), ((), ()))
    n_extra, n_out = len(extras), len(out_dtypes)

    host = _ExchangeHost(exchange)
    grid = (m // tm, n // tn, nk)

    def body(*refs):
        a_ref, b_ref = refs[:2]
        extra_refs = refs[2:2 + n_extra]
        out_refs = refs[2 + n_extra + host.n_in:2 + n_extra + host.n_in + n_out]
        host.start(refs, 2 + n_extra, 2 + n_extra + host.n_in + n_out, grid)
        part = _dot(a_ref[...].astype(BF16), b_ref[...].astype(BF16), dims)

        def finish(acc):
            vals = epilogue(acc, *[r[...] for r in extra_refs]) if epilogue is not None else (acc,) * n_out
            assert len(vals) == n_out
            for r, v in zip(out_refs, vals):
                r[...] = v.astype(r.dtype)

        if nk == 1:
            finish(part)
        else:
            acc_ref = refs[2 + n_extra + host.n_in + n_out + host.n_out]
            k = pl.program_id(2)

            @pl.when(k == 0)
            def _():
                acc_ref[...] = part

            @pl.when((k > 0) & (k < nk - 1))
            def _():
                acc_ref[...] += part

            @pl.when(k == nk - 1)
            def _():
                finish(acc_ref[...] + part)

        host.wait(refs, 2 + n_extra, 2 + n_extra + host.n_in + n_out, grid)

    outs = pl.pallas_call(
        body,
        grid=grid,
        in_specs=[a_spec, b_spec] + [x_spec] * n_extra + host.in_specs,
        out_specs=[o_spec] * n_out + host.out_specs,
        out_shape=[jax.ShapeDtypeStruct(out_shape, d) for d in out_dtypes] + host.out_shapes,
        scratch_shapes=([pltpu.VMEM((tm, tn), F32)] if nk > 1 else []) + host.scratch,
        input_output_aliases=host.aliases(2 + n_extra, n_out),
        compiler_params=_cp(host.semantics(("parallel", "parallel", "arbitrary"))),
        name=name,
    )(a, b, *extras, *host.ins)
    mine, landed = outs[:n_out], list(outs[n_out:])
    mine = mine[0] if n_out == 1 else mine
    return (mine, landed) if exchange is not None else mine


class Exchange(NamedTuple):
    ins: tuple
    out_shapes: tuple
    n_remote: int
    n_local: int
    copies_of: Callable
    aliases: tuple = ()


def _exchange_copies(ex, in_refs, out_refs, sem_refs):
    send_sems, recv_sems, local_sems = sem_refs
    remote, local = ex.copies_of(in_refs, out_refs, _place())
    assert len(remote) == ex.n_remote and len(local) == ex.n_local, (len(remote), len(local))
    cps = [pltpu.make_async_remote_copy(src_ref=src, dst_ref=dst, send_sem=send_sems.at[k], recv_sem=recv_sems.at[k],
                                        device_id=to, device_id_type=MESH) for k, (src, dst, to) in enumerate(remote)]
    cps += [pltpu.make_async_copy(src, dst, local_sems.at[k]) for k, (src, dst) in enumerate(local)]
    return cps


class _ExchangeHost:
    def __init__(self, ex):
        self.ex = ex
        self.ins = list(ex.ins) if ex else []
        self.out_shapes = list(ex.out_shapes) if ex else []
        self.n_in, self.n_out = len(self.ins), len(self.out_shapes)
        self.in_specs = [HBM_SPEC] * self.n_in
        self.out_specs = [HBM_SPEC] * self.n_out
        self.scratch = _sem_scratch(ex.n_remote, ex.n_local) if ex else []

    def semantics(self, sem):
        return tuple("arbitrary" for _ in sem) if self.ex else sem

    def aliases(self, in_at, out_at):
        return {in_at + i: out_at + o for i, o in self.ex.aliases} if self.ex else {}

    def _refs(self, refs, in_at, out_at):
        return refs[in_at:in_at + self.n_in], refs[out_at:out_at + self.n_out], refs[len(refs) - 3:]

    def _when(self, grid, last):
        cond = None
        for d, size in enumerate(grid):
            c = pl.program_id(d) == (size - 1 if last else 0)
            cond = c if cond is None else cond & c
        return cond

    def start(self, refs, in_at, out_at, grid):
        if self.ex:
            @pl.when(self._when(grid, False))
            def _():
                for cp in _exchange_copies(self.ex, *self._refs(refs, in_at, out_at)):
                    cp.start()

    def wait(self, refs, in_at, out_at, grid):
        if self.ex:
            @pl.when(self._when(grid, True))
            def _():
                for cp in _exchange_copies(self.ex, *self._refs(refs, in_at, out_at)):
                    cp.wait()


def _rms_fwd(x, g, *, name, tm=512):
    t, d = x.shape
    tm = _blk(t, tm)

    def body(x_ref, g_ref, n_ref):
        xv = x_ref[...]
        r = lax.rsqrt(jnp.mean(xv * xv, axis=-1, keepdims=True) + RMS_EPS)
        n_ref[...] = (xv * r * g_ref[...]).astype(n_ref.dtype)

    return pl.pallas_call(
        body, grid=(t // tm,),
        in_specs=[pl.BlockSpec((tm, d), lambda i: (i, 0)), pl.BlockSpec((1, d), lambda i: (0, 0))],
        out_specs=pl.BlockSpec((tm, d), lambda i: (i, 0)),
        out_shape=jax.ShapeDtypeStruct((t, d), BF16),
        compiler_params=_cp(("parallel",)), name=name,
    )(x, g)


def _rms_bwd(dn, x, g, dres, *, name, tm=512):
    t, d = x.shape
    tm = _blk(t, tm)

    def body(dn_ref, x_ref, g_ref, dres_ref, dx_ref, dxb_ref, dg_ref):
        i = pl.program_id(0)
        xv = x_ref[...]
        r = lax.rsqrt(jnp.mean(xv * xv, axis=-1, keepdims=True) + RMS_EPS)
        xh = xv * r
        dnv = dn_ref[...].astype(F32)
        dxh = dnv * g_ref[...]
        dx = dres_ref[...] + r * (dxh - xh * jnp.mean(dxh * xh, axis=-1, keepdims=True))
        dx_ref[...] = dx
        dxb_ref[...] = dx.astype(dxb_ref.dtype)
        part = jnp.sum(dnv * xh, axis=0, keepdims=True)

        @pl.when(i == 0)
        def _():
            dg_ref[...] = part

        @pl.when(i > 0)
        def _():
            dg_ref[...] += part

    row = pl.BlockSpec((tm, d), lambda i: (i, 0))
    vec = pl.BlockSpec((1, d), lambda i: (0, 0))
    return pl.pallas_call(
        body, grid=(t // tm,),
        in_specs=[row, row, vec, row], out_specs=[row, row, vec],
        out_shape=[jax.ShapeDtypeStruct((t, d), F32), jax.ShapeDtypeStruct((t, d), BF16),
                   jax.ShapeDtypeStruct((1, d), F32)],
        compiler_params=_cp(("arbitrary",)), name=name,
    )(dn, x, g, dres)


def _final_loss(h, g, tgt, *, name, tm=512):
    t, d = h.shape
    tm = _blk(t, tm)

    def body(h_ref, g_ref, t_ref, dh_ref, dhb_ref, dg_ref, loss_ref):
        i = pl.program_id(0)
        hv = h_ref[...]
        r = lax.rsqrt(jnp.mean(hv * hv, axis=-1, keepdims=True) + RMS_EPS)
        xh = hv * r
        e = xh * g_ref[...] - t_ref[...]
        dy = e * (1.0 / d)
        dxh = dy * g_ref[...]
        dh = r * (dxh - xh * jnp.mean(dxh * xh, axis=-1, keepdims=True))
        dh_ref[...] = dh
        dhb_ref[...] = dh.astype(dhb_ref.dtype)
        dg_part = jnp.sum(dy * xh, axis=0, keepdims=True)
        row_loss = jnp.sum(e * e, axis=-1, keepdims=True) * (0.5 / d)
        loss_part = jnp.sum(row_loss, axis=0, keepdims=True)

        @pl.when(i == 0)
        def _():
            dg_ref[...] = dg_part
            loss_ref[...] = jnp.broadcast_to(loss_part, loss_ref.shape)

        @pl.when(i > 0)
        def _():
            dg_ref[...] += dg_part
            loss_ref[...] += jnp.broadcast_to(loss_part, loss_ref.shape)

    row = pl.BlockSpec((tm, d), lambda i: (i, 0))
    vec = pl.BlockSpec((1, d), lambda i: (0, 0))
    return pl.pallas_call(
        body, grid=(t // tm,),
        in_specs=[row, vec, row], out_specs=[row, row, vec, pl.BlockSpec((1, LANES), lambda i: (0, 0))],
        out_shape=[jax.ShapeDtypeStruct((t, d), F32), jax.ShapeDtypeStruct((t, d), BF16),
                   jax.ShapeDtypeStruct((1, d), F32), jax.ShapeDtypeStruct((1, LANES), F32)],
        compiler_params=_cp(("arbitrary",)), name=name,
    )(h, g, tgt)


SWA_SUB = 64


def _swa_mask(n, rows, row0):
    w = SWA_WINDOW
    qi = (lax.broadcasted_iota(jnp.int32, (rows, 2 * w), 0) + row0) & (w - 1)
    kj = lax.broadcasted_iota(jnp.int32, (rows, 2 * w), 1)
    return (kj > qi) & (kj <= qi + w) & ((n > 0) | (kj >= w))


def _stack_heads(ref, heads, width):
    return jnp.concatenate([ref[:, h * width:(h + 1) * width] for h in heads], axis=0)


def _stack_scalars(ref, heads, rows):
    return jnp.concatenate([jnp.broadcast_to(ref[0:1, h:h + 1], (rows, 1)) for h in heads], axis=0)


def _swa_fwd(q, kv, sinks, *, name, exchange=None):
    t = q.shape[0]
    w, hd, hq, hkv = SWA_WINDOW, SWA_HEAD_DIM, SWA_Q_HEADS, SWA_KV_HEADS
    grp = hq // hkv
    kvw = hkv * hd
    nb = t // w
    host = _ExchangeHost(exchange)
    assert not (exchange and exchange.aliases)

    def body(*refs):
        q_ref, kvp_ref, kvc_ref, s_ref = refs[:4]
        o_ref, lse_ref = refs[4 + host.n_in:6 + host.n_in]
        host.start(refs, 4, 6 + host.n_in, (nb,))
        n = pl.program_id(0)
        mask = _swa_mask(n, grp * w, 0)
        kvcat = jnp.concatenate([kvp_ref[...], kvc_ref[...]], axis=0)
        outs, lses = [], []
        for hk in range(hkv):
            heads = range(hk * grp, (hk + 1) * grp)
            qs = _stack_heads(q_ref, heads, hd)
            kh = kvcat[:, hk * hd:(hk + 1) * hd]
            vh = kvcat[:, kvw + hk * hd:kvw + (hk + 1) * hd]
            sk = _stack_scalars(s_ref, heads, w)
            s = jnp.where(mask, _dot(qs * SWA_SCALE, kh, NT), -jnp.inf)
            m = jnp.maximum(jnp.max(s, axis=-1, keepdims=True), sk)
            p = jnp.exp(s - m)
            den = jnp.sum(p, axis=-1, keepdims=True) + jnp.exp(sk - m)
            o = _dot((p * (1.0 / den)).astype(BF16), vh)
            lse = m + jnp.log(den)
            outs += [o[j * w:(j + 1) * w] for j in range(grp)]
            lses += [lse[j * w:(j + 1) * w] for j in range(grp)]
        o_ref[...] = jnp.concatenate(outs, axis=1).astype(o_ref.dtype)
        lse_ref[...] = jnp.concatenate(lses, axis=1)
        host.wait(refs, 4, 6 + host.n_in, (nb,))

    outs = pl.pallas_call(
        body, grid=(nb,),
        in_specs=[pl.BlockSpec((w, hq * hd), lambda i: (i, 0)),
                  pl.BlockSpec((w, 2 * kvw), lambda i: (jnp.maximum(i - 1, 0), 0)),
                  pl.BlockSpec((w, 2 * kvw), lambda i: (i, 0)),
                  pl.BlockSpec((1, hq), lambda i: (0, 0))] + host.in_specs,
        out_specs=[pl.BlockSpec((w, hq * hd), lambda i: (i, 0)), pl.BlockSpec((w, hq), lambda i: (i, 0))] + host.out_specs,
        out_shape=[jax.ShapeDtypeStruct((t, hq * hd), BF16), jax.ShapeDtypeStruct((t, hq), F32)] + host.out_shapes,
        scratch_shapes=host.scratch,
        compiler_params=_cp(host.semantics(("parallel",))), name=name,
    )(q, kv, kv, sinks, *host.ins)
    return (outs[0], outs[1], list(outs[2:])) if exchange is not None else outs


ANY_SPEC = pl.BlockSpec(memory_space=pl.ANY)


def _swa_bwd(q, kv, sinks, o, lse, do, dp, dp_col, *, name):
    t = q.shape[0]
    w, hd, hq, hkv = SWA_WINDOW, SWA_HEAD_DIM, SWA_Q_HEADS, SWA_KV_HEADS
    grp = hq // hkv
    kvw = hkv * hd
    nb = t // w
    assert dp_col % (hq * hd) == 0
    dq_blk = dp_col // (hq * hd)

    def body(q_ref, kvp_ref, kvc_ref, s_ref, o_ref, lse_ref, do_ref, _, dq_ref, dkv_ref, ds_ref, carry_ref, s_scr, dp_scr,
             p_scr, ds_scr):
        n = pl.program_id(0)

        @pl.when(n == 0)
        def _():
            ds_ref[...] = jnp.zeros_like(ds_ref)
            carry_ref[...] = jnp.zeros_like(carry_ref)

        @pl.when(n < nb)
        def _():
            kvcat = jnp.concatenate([kvp_ref[...], kvc_ref[...]], axis=0)
            dqs, dsk, dks, dvs = [], [], [], []
            for hk in range(hkv):
                heads = range(hk * grp, (hk + 1) * grp)
                qs = _stack_heads(q_ref, heads, hd)
                dos = _stack_heads(do_ref, heads, hd)
                os_ = _stack_heads(o_ref, heads, hd)
                lse = _stack_heads(lse_ref, heads, 1)
                kh = kvcat[:, hk * hd:(hk + 1) * hd]
                vh = kvcat[:, kvw + hk * hd:kvw + (hk + 1) * hd]
                delta = jnp.sum(dos.astype(F32) * os_.astype(F32), axis=-1, keepdims=True)
                s_scr[...] = _dot(qs * SWA_SCALE, kh, NT)
                dp_scr[...] = _dot(dos, vh, NT)
                for r0 in range(0, grp * w, SWA_SUB):
                    rows = slice(r0, r0 + SWA_SUB)
                    p = jnp.exp(jnp.where(_swa_mask(n, SWA_SUB, r0 % w), s_scr[rows, :], -jnp.inf) - lse[rows])
                    p_scr[rows, :] = p.astype(p_scr.dtype)
                    ds_scr[rows, :] = (p * (dp_scr[rows, :] - delta[rows]) * SWA_SCALE).astype(ds_scr.dtype)
                ds = ds_scr[...]
                dq = _dot(ds, kh)
                dqs += [dq[j * w:(j + 1) * w] for j in range(grp)]
                dks.append(_dot(ds, qs, TN))
                dvs.append(_dot(p_scr[...], dos, TN))
                dsink = -jnp.exp(_stack_scalars(s_ref, heads, w) - lse) * delta
                dsk += [jnp.sum(dsink[j * w:(j + 1) * w], axis=0, keepdims=True) for j in range(grp)]
            dq_ref[...] = jnp.concatenate(dqs, axis=1).astype(dq_ref.dtype)
            ds_ref[...] += jnp.concatenate(dsk, axis=1)
            dkv_cat = jnp.concatenate(dks + dvs, axis=1)
            dkv_ref[...] = (carry_ref[...] + dkv_cat[:w]).astype(dkv_ref.dtype)
            carry_ref[...] = dkv_cat[w:]

        @pl.when(n == nb)
        def _():
            dkv_ref[...] = carry_ref[...].astype(dkv_ref.dtype)

    cur = lambda i: (jnp.minimum(i, nb - 1), 0)
    prev = lambda i: (jnp.clip(i - 1, 0, nb - 1), 0)
    return pl.pallas_call(
        body, grid=(nb + 1,),
        in_specs=[pl.BlockSpec((w, hq * hd), cur), pl.BlockSpec((w, 2 * kvw), prev), pl.BlockSpec((w, 2 * kvw), cur),
                  pl.BlockSpec((1, hq), lambda i: (0, 0)), pl.BlockSpec((w, hq * hd), cur),
                  pl.BlockSpec((w, hq), cur), pl.BlockSpec((w, hq * hd), cur), ANY_SPEC],
        out_specs=[pl.BlockSpec((w, hq * hd), lambda i: (jnp.minimum(i, nb - 1), dq_blk)),
                   pl.BlockSpec((w, 2 * kvw), prev), pl.BlockSpec((1, hq), lambda i: (0, 0))],
        out_shape=[jax.ShapeDtypeStruct(dp.shape, dp.dtype), jax.ShapeDtypeStruct((t, 2 * kvw), BF16),
                   jax.ShapeDtypeStruct((1, hq), F32)],
        scratch_shapes=[pltpu.VMEM((w, 2 * kvw), F32)] + [pltpu.VMEM((grp * w, 2 * w), dt) for dt in (F32, F32, BF16, BF16)],
        input_output_aliases={7: 0},
        compiler_params=_cp(("arbitrary",)), name=name,
    )(q, kv, kv, sinks, o, lse, do, dp)


def _xa_fwd(q, mkv, *, name, tq=512):
    t, xw = q.shape
    nm = mkv.shape[0]
    hd, nh = XA_HEAD_DIM, XA_HEADS
    tq = _blk(t, tq)

    def body(q_ref, mkv_ref, o_ref):
        outs = []
        for h in range(nh):
            qh = q_ref[:, h * hd:(h + 1) * hd]
            kh = mkv_ref[:, h * hd:(h + 1) * hd]
            vh = mkv_ref[:, xw + h * hd:xw + (h + 1) * hd]
            s = _dot(qh, kh, NT) * (hd ** -0.5)
            p = jnp.exp(s - jnp.max(s, axis=-1, keepdims=True))
            p = p / jnp.sum(p, axis=-1, keepdims=True)
            outs.append(_dot(p.astype(BF16), vh))
        o_ref[...] = jnp.concatenate(outs, axis=1).astype(o_ref.dtype)

    return pl.pallas_call(
        body, grid=(t // tq,),
        in_specs=[pl.BlockSpec((tq, xw), lambda i: (i, 0)), pl.BlockSpec((nm, 2 * xw), lambda i: (0, 0))],
        out_specs=pl.BlockSpec((tq, xw), lambda i: (i, 0)),
        out_shape=jax.ShapeDtypeStruct((t, xw), BF16),
        compiler_params=_cp(("parallel",)), name=name,
    )(q, mkv)


def _xa_bwd(q, mkv, do, dp, dp_col, *, name, tq=512):
    t, xw = q.shape
    nm = mkv.shape[0]
    hd, nh = XA_HEAD_DIM, XA_HEADS
    tq = _blk(t, tq)
    assert dp_col % xw == 0

    def body(q_ref, mkv_ref, do_ref, _, dq_ref, dmkv_ref):
        i = pl.program_id(0)
        dqs, dks, dvs = [], [], []
        for h in range(nh):
            qh = q_ref[:, h * hd:(h + 1) * hd]
            kh = mkv_ref[:, h * hd:(h + 1) * hd]
            vh = mkv_ref[:, xw + h * hd:xw + (h + 1) * hd]
            doh = do_ref[:, h * hd:(h + 1) * hd]
            s = _dot(qh, kh, NT) * (hd ** -0.5)
            p = jnp.exp(s - jnp.max(s, axis=-1, keepdims=True))
            p = p / jnp.sum(p, axis=-1, keepdims=True)
            dp = _dot(doh, vh, NT)
            ds = (p * (dp - jnp.sum(p * dp, axis=-1, keepdims=True)) * (hd ** -0.5)).astype(BF16)
            dqs.append(_dot(ds, kh))
            dks.append(_dot(ds, qh, TN))
            dvs.append(_dot(p.astype(BF16), doh, TN))
        dq_ref[...] = jnp.concatenate(dqs, axis=1).astype(dq_ref.dtype)
        part = jnp.concatenate(dks + dvs, axis=1)

        @pl.when(i == 0)
        def _():
            dmkv_ref[...] = part

        @pl.when(i > 0)
        def _():
            dmkv_ref[...] += part

    row = pl.BlockSpec((tq, xw), lambda i: (i, 0))
    full = pl.BlockSpec((nm, 2 * xw), lambda i: (0, 0))
    return pl.pallas_call(
        body, grid=(t // tq,),
        in_specs=[row, full, row, ANY_SPEC],
        out_specs=[pl.BlockSpec((tq, xw), lambda i: (i, dp_col // xw)), full],
        out_shape=[jax.ShapeDtypeStruct(dp.shape, dp.dtype), jax.ShapeDtypeStruct((nm, 2 * xw), F32)],
        input_output_aliases={3: 0}, compiler_params=_cp(("arbitrary",)), name=name,
    )(q, mkv, do, dp)


def _merge_specs(ys, ws, tm):
    y_specs = [pl.BlockSpec((tm, y.shape[1]), lambda i: (i, 0)) for y in ys]
    w_specs = [pl.BlockSpec(w.shape, lambda i: (0, 0, 0)) for w in ws]
    return y_specs, w_specs


def _merge_tiles(ws, tn):
    ns, _, per = ws[0].shape
    tn = _blk(per, tn)
    return tn, [(s, c, s * per + c) for s in range(ns) for c in range(0, per, tn)]


def _merge_fwd(ys, ws, gates, *, name, tm=256, tn=512):
    t, d = ys[0].shape[0], ws[0].shape[0] * ws[0].shape[2]
    tm = _blk(t, tm)
    tn, tiles = _merge_tiles(ws, tn)
    y_specs, w_specs = _merge_specs(ys, ws, tm)

    def body(ya, yb, yc, wa, wb, wc, g_ref, o_ref):
        for s, c, col in tiles:
            acc = None
            for b, (y, w) in enumerate(((ya, wa), (yb, wb), (yc, wc))):
                term = _sigmoid(g_ref[:, b * d + col:b * d + col + tn]) * _dot(y[...], w[s, :, c:c + tn])
                acc = term if acc is None else acc + term
            o_ref[:, col:col + tn] = acc.astype(o_ref.dtype)

    return pl.pallas_call(
        body, grid=(t // tm,),
        in_specs=y_specs + w_specs + [pl.BlockSpec((tm, 3 * d), lambda i: (i, 0))],
        out_specs=pl.BlockSpec((tm, d), lambda i: (i, 0)),
        out_shape=jax.ShapeDtypeStruct((t, d), BF16),
        compiler_params=_cp(("parallel",)), name=name,
    )(*ys, *ws, gates)


def _merge_bwd(ys, ws, gates, dmerged, dp_width, *, name, tm=256, tn=512):
    t, d = ys[0].shape[0], ws[0].shape[0] * ws[0].shape[2]
    tm = _blk(t, tm)
    tn, tiles = _merge_tiles(ws, tn)
    y_specs, w_specs = _merge_specs(ys, ws, tm)
    row = pl.BlockSpec((tm, d), lambda i: (i, 0))
    wide = pl.BlockSpec((tm, 3 * d), lambda i: (i, 0))

    def body(ya, yb, yc, wa, wb, wc, g_ref, dm_ref, dua, dub, duc, dp_ref):
        for s, c, col in tiles:
            dm = dm_ref[:, col:col + tn]
            for b, (y, w, du) in enumerate(((ya, wa, dua), (yb, wb, dub), (yc, wc, duc))):
                sg = _sigmoid(g_ref[:, b * d + col:b * d + col + tn])
                u = _dot(y[...], w[s, :, c:c + tn])
                du[:, col:col + tn] = (dm * sg).astype(du.dtype)
                dp_ref[:, b * d + col:b * d + col + tn] = (dm * u * sg * (1.0 - sg)).astype(dp_ref.dtype)

    return pl.pallas_call(
        body, grid=(t // tm,),
        in_specs=y_specs + w_specs + [wide, row],
        out_specs=[row] * 3 + [wide],
        out_shape=[jax.ShapeDtypeStruct((t, d), BF16)] * 3 + [jax.ShapeDtypeStruct((t, dp_width), BF16)],
        compiler_params=_cp(("parallel",)), name=name,
    )(*ys, *ws, gates, dmerged)


def _adamw(w, g, m, v, *, name, tm=256):
    lead = w.ndim - 2
    assert all(s == 1 for s in w.shape[:lead]) and m.shape == w.shape and v.shape == w.shape
    r, c = w.shape[lead:]
    assert g.shape == (r, c)
    tm = _blk(r, tm) if r % 8 == 0 else r
    tc = c if tm * c * 4 <= (4 << 20) else _blk(c, 256)
    ncb = c // tc
    bc1 = 1.0 - ADAM_B1 ** ADAM_STEP
    bc2 = 1.0 - ADAM_B2 ** ADAM_STEP

    def body(w_ref, g_ref, m_ref, v_ref, d_ref, nm_ref, nv_ref):
        gv = g_ref[...]
        nm = ADAM_B1 * m_ref[...] + (1.0 - ADAM_B1) * gv
        nv = ADAM_B2 * v_ref[...] + (1.0 - ADAM_B2) * (gv * gv)
        d_ref[...] = -ADAM_LR * ((nm / bc1) / (jnp.sqrt(nv / bc2) + ADAM_EPS) + ADAM_WD * w_ref[...])
        nm_ref[...] = nm
        nv_ref[...] = nv

    spec = pl.BlockSpec((None,) * lead + (tm, tc), lambda i: (0,) * lead + (i // ncb, i % ncb))
    g_spec = pl.BlockSpec((tm, tc), lambda i: (i // ncb, i % ncb))
    return pl.pallas_call(
        body, grid=(r // tm * ncb,), in_specs=[spec, g_spec, spec, spec], out_specs=[spec] * 3,
        out_shape=[jax.ShapeDtypeStruct(w.shape, F32)] * 3,
        compiler_params=_cp(("parallel",)), name=name,
    )(w, g, m, v)


HALO = 8


def _shift_down(cur, prev, j):
    if j == 0:
        return cur
    y = pltpu.roll(cur, j, 0)
    row = lax.broadcasted_iota(jnp.int32, (HALO, cur.shape[1]), 0)
    top = jnp.where(row < j, pltpu.roll(prev, j, 0), y[:HALO])
    return jnp.concatenate([top, y[HALO:]], axis=0)


def _shift_up(cur, nxt, j):
    if j == 0:
        return cur
    tm = cur.shape[0]
    y = pltpu.roll(cur, tm - j, 0)
    row = lax.broadcasted_iota(jnp.int32, (HALO, cur.shape[1]), 0)
    bot = jnp.where(row >= HALO - j, pltpu.roll(nxt, HALO - j, 0), y[tm - HALO:])
    return jnp.concatenate([y[:tm - HALO], bot], axis=0)


def _softplus(x):
    return jnp.maximum(x, 0.0) + jnp.log(1.0 + jnp.exp(-jnp.abs(x)))


def _gdn_pre_fwd(qkvb, conv_w, ab, alog_pad, dt_pad, *, name, ab_blk=0, tm=256):
    t, cw = qkvb.shape
    hd, nh, ck = GDN_HEAD_DIM, GDN_HEADS, GDN_CHUNK
    gw = nh * hd
    tm = _blk(t, tm)
    hb = tm // HALO

    def body(x_ref, xp_ref, w_ref, ab_ref, al_ref, dt_ref, xc_ref, qkvn_ref, aux_ref):
        i = pl.program_id(0)
        cur = x_ref[...]
        prev = jnp.where(i > 0, xp_ref[...], 0.0)
        xc = None
        for tap in range(GDN_CONV):
            term = w_ref[tap:tap + 1, :] * _shift_down(cur, prev, GDN_CONV - 1 - tap)
            xc = term if xc is None else xc + term
        xc_ref[...] = xc
        s = xc * _sigmoid(xc)
        for h in range(2 * nh):
            xh = s[:, h * hd:(h + 1) * hd]
            r = lax.rsqrt(jnp.sum(xh * xh, axis=-1, keepdims=True) + L2_EPS)
            scale = hd ** -0.5 if h < nh else 1.0
            qkvn_ref[:, h * hd:(h + 1) * hd] = xh * (r * scale)
        qkvn_ref[:, 2 * gw:] = s[:, 2 * gw:]
        abv = ab_ref[...]
        lane = lax.broadcasted_iota(jnp.int32, abv.shape, 1)
        g = jnp.where(lane < nh, -jnp.exp(al_ref[...]) * _softplus(abv + dt_ref[...]), 0.0)
        beta = jnp.where((lane >= nh) & (lane < 2 * nh), _sigmoid(abv), 0.0)
        ii = lax.broadcasted_iota(jnp.int32, (tm, tm), 0)
        jj = lax.broadcasted_iota(jnp.int32, (tm, tm), 1)
        tri = jnp.where((ii >= jj) & ((ii ^ jj) < ck), 1.0, 0.0)
        gcum = _dot(tri, g, precision=HI)
        aux_ref[...] = g + beta + pltpu.roll(gcum, 2 * nh, 1)

    row = lambda c: pl.BlockSpec((tm, c), lambda i: (i, 0))
    vec = lambda r, c: pl.BlockSpec((r, c), lambda i: (0, 0))
    return pl.pallas_call(
        body, grid=(t // tm,),
        in_specs=[row(cw), pl.BlockSpec((HALO, cw), lambda i: (jnp.maximum(i * hb - 1, 0), 0)), vec(GDN_CONV, cw),
                  pl.BlockSpec((tm, LANES), lambda i: (i, ab_blk)), vec(1, LANES), vec(1, LANES)],
        out_specs=[row(cw), row(cw), row(LANES)],
        out_shape=[jax.ShapeDtypeStruct((t, cw), F32), jax.ShapeDtypeStruct((t, cw), F32),
                   jax.ShapeDtypeStruct((t, LANES), F32)],
        compiler_params=_cp(("parallel",)), name=name,
    )(qkvb, qkvb, conv_w, ab, alog_pad, dt_pad)


GDN_STEP_CHUNKS = 4
GDN_ILP_CHUNKS = 4
GDN_ILP_CHUNKS_BWD = 4


def _bdot(a, b, dims=NN):
    return _dot(a.astype(BF16), b.astype(BF16), dims)


def _split_bf16(x):
    hi = x.astype(BF16)
    return hi, (x - hi.astype(F32)).astype(BF16)


def _dot3(a, b, dims=NN):
    ah, al = _split_bf16(a)
    bh, bl = _split_bf16(b)
    return _dot(ah, bh, dims) + (_dot(ah, bl, dims) + _dot(al, bh, dims))


def _dot3_many(lhs, rhs, dims=NN):
    sa = [_split_bf16(a) for a in lhs]
    sb = [_split_bf16(b) for b in rhs]
    hh = [_dot(a[0], b[0], dims) for a, b in zip(sa, sb)]
    hl = [_dot(a[0], b[1], dims) for a, b in zip(sa, sb)]
    lh = [_dot(a[1], b[0], dims) for a, b in zip(sa, sb)]
    return [x + (y + z) for x, y, z in zip(hh, hl, lh)]


def _gdn_local(chains, with_inverse):
    ck = GDN_CHUNK
    ii = lax.broadcasted_iota(jnp.int32, (ck, ck), 0)
    jj = lax.broadcasted_iota(jnp.int32, (ck, ck), 1)
    lower, strict = ii >= jj, ii > jj
    dmat = [jnp.exp(jnp.where(lower, gc - gc_row, -jnp.inf)) for _, _, _, gc, gc_row in chains]
    kk = [_bdot(k, k, NT) for _, k, _, _, _ in chains]
    qk = [_bdot(q, k, NT) for q, k, _, _, _ in chains]
    tinv = [None] * len(chains)
    if with_inverse:
        lmat = [jnp.where(strict, c[2] * kk_i * d_i, 0.0) for c, kk_i, d_i in zip(chains, kk, dmat)]
        eye = jnp.where(ii == jj, 1.0, 0.0)
        tinv = [eye - l_i for l_i in lmat]
        pw = lmat
        for _ in range(int(math.log2(ck)) - 1):
            pw = _dot3_many(pw, pw)
            tinv = [t_i + d_i for t_i, d_i in zip(tinv, _dot3_many(tinv, pw))]
    out = []
    for (q, k, b, gc, gc_row), dmat_i, kk_i, qk_i, tinv_i in zip(chains, dmat, kk, qk, tinv):
        gl = gc[ck - 1:ck, :]
        out.append(dict(lower=lower, strict=strict, dmat=dmat_i, kk=kk_i, tinv=tinv_i, gam=jnp.exp(gc), qk=qk_i,
                        mm=qk_i * dmat_i, kdec=jnp.exp(gl - gc)))
    return out


def _gdn_head_cols(h):
    return slice(h * GDN_HEAD_DIM, (h + 1) * GDN_HEAD_DIM)


def _gdn_chunk_inputs(x_ref, aux_ref, auxt_ref, g, h):
    nh, ck = GDN_HEADS, GDN_CHUNK
    gw = nh * GDN_HEAD_DIM
    rows = slice(g * ck, (g + 1) * ck)
    cols = _gdn_head_cols(h)
    q = x_ref[rows, cols]
    k = x_ref[rows, gw + cols.start:gw + cols.stop]
    v = x_ref[rows, 2 * gw + cols.start:2 * gw + cols.stop]
    b = aux_ref[rows, nh + h:nh + h + 1]
    gc = aux_ref[rows, 2 * nh + h:2 * nh + h + 1]
    gc_row = auxt_ref[g, 2 * nh + h:2 * nh + h + 1, :]
    return q, k, v, b, gc, gc_row


def _gdn_specs(t, widths, *, reverse=False, step_chunks=None):
    rows = (step_chunks or GDN_STEP_CHUNKS) * GDN_CHUNK
    nsteps = t // rows
    idx = (lambda i: (nsteps - 1 - i, 0)) if reverse else (lambda i: (i, 0))
    return [pl.BlockSpec((rows, w), idx) for w in widths]


def _gdn_local_fwd(qkvn, aux, aux_t, *, name, exchange=None):
    t = qkvn.shape[0]
    hd, nh, ck, gs = GDN_HEAD_DIM, GDN_HEADS, GDN_CHUNK, GDN_STEP_CHUNKS
    gw = nh * hd
    host = _ExchangeHost(exchange)
    assert not (exchange and exchange.aliases)
    grid = (t // (gs * ck),)

    def body(*refs):
        x_ref, aux_ref, auxt_ref = refs[:3]
        u_ref, w_ref, qd_ref, kd_ref, mm_ref, tinv_ref = refs[3 + host.n_in:9 + host.n_in]
        host.start(refs, 3, 9 + host.n_in, grid)
        for g0 in range(0, gs, GDN_ILP_CHUNKS):
            where = [(g, h) for g in range(g0, g0 + GDN_ILP_CHUNKS) for h in range(nh)]
            ins = [_gdn_chunk_inputs(x_ref, aux_ref, auxt_ref, g, h) for g, h in where]
            lcs = _gdn_local([(q, k, b, gc, gc_row) for q, k, _, b, gc, gc_row in ins], True)
            tinvs = [lc["tinv"] for lc in lcs]
            us = _dot3_many(tinvs, [b * v for _, _, v, b, _, _ in ins])
            ws = _dot3_many(tinvs, [(b * lc["gam"]) * k for (_, k, _, b, _, _), lc in zip(ins, lcs)])
            for i, ((g, h), (q, k, _, _, _, _), lc) in enumerate(zip(where, ins, lcs)):
                rows, cols = slice(g * ck, (g + 1) * ck), _gdn_head_cols(h)
                u_ref[rows, cols] = us[i]
                w_ref[rows, cols] = ws[i].astype(w_ref.dtype)
                qd_ref[rows, cols] = (lc["gam"] * q).astype(qd_ref.dtype)
                kd_ref[rows, cols] = (lc["kdec"] * k).astype(kd_ref.dtype)
            for g in range(g0, g0 + GDN_ILP_CHUNKS):
                rows = slice(g * ck, (g + 1) * ck)
                mine = [lc for (gg, _), lc in zip(where, lcs) if gg == g]
                mm_ref[rows, :] = jnp.concatenate([lc["mm"] for lc in mine], axis=1).astype(mm_ref.dtype)
                tinv_ref[rows, :] = jnp.concatenate([lc["tinv"] for lc in mine], axis=1)
        host.wait(refs, 3, 9 + host.n_in, grid)

    sq = nh * ck
    outs = pl.pallas_call(
        body, grid=grid,
        in_specs=_gdn_specs(t, (3 * gw, LANES)) + [pl.BlockSpec((gs, 16, ck), lambda i: (i, 0, 0))] + host.in_specs,
        out_specs=_gdn_specs(t, (gw, gw, gw, gw, sq, sq)) + host.out_specs,
        out_shape=[jax.ShapeDtypeStruct((t, gw), F32)] + [jax.ShapeDtypeStruct((t, gw), BF16)] * 3
        + [jax.ShapeDtypeStruct((t, sq), BF16), jax.ShapeDtypeStruct((t, sq), F32)] + host.out_shapes,
        scratch_shapes=host.scratch,
        compiler_params=_cp(host.semantics(("parallel",))), name=name,
    )(qkvn, aux, aux_t, *host.ins)
    return (*outs[:6], list(outs[6:])) if exchange is not None else outs


def _gdn_seq_fwd(u, w, qd, kd, mm, aux, *, name):
    t = u.shape[0]
    hd, nh, ck, gs = GDN_HEAD_DIM, GDN_HEADS, GDN_CHUNK, GDN_STEP_CHUNKS
    gw = nh * hd
    sq = nh * ck

    def body(u_ref, w_ref, qd_ref, kd_ref, mm_ref, aux_ref, o_ref, vn_ref, sall_ref, s_ref):
        @pl.when(pl.program_id(0) == 0)
        def _():
            s_ref[...] = jnp.zeros_like(s_ref)

        heads = range(nh)
        hcols = [_gdn_head_cols(h) for h in heads]
        sts = [s_ref[h] for h in heads]
        for g in range(gs):
            rows = slice(g * ck, (g + 1) * ck)
            last = (g + 1) * ck - 1
            for h in heads:
                sall_ref[g, h] = sts[h]
            stbs = [st.astype(BF16) for st in sts]
            w_s = [_dot(w_ref[rows, c], stb) for c, stb in zip(hcols, stbs)]
            q_s = [_dot(qd_ref[rows, c], stb) for c, stb in zip(hcols, stbs)]
            vnbs = [(u_ref[rows, c] - ws).astype(BF16) for c, ws in zip(hcols, w_s)]
            m_v = [_dot(mm_ref[rows, h * ck:(h + 1) * ck], vnbs[h]) for h in heads]
            k_v = [_dot(kd_ref[rows, c], vnb, TN) for c, vnb in zip(hcols, vnbs)]
            for h, c in zip(heads, hcols):
                vn_ref[rows, c] = vnbs[h]
                o_ref[rows, c] = q_s[h] + m_v[h]
            gam_c = [jnp.exp(aux_ref[last:last + 1, 2 * nh + h:2 * nh + h + 1]) for h in heads]
            sts = [gam_c[h] * sts[h] + k_v[h] for h in heads]
        for h in heads:
            s_ref[h] = sts[h]

    return pl.pallas_call(
        body, grid=(t // (gs * ck),),
        in_specs=_gdn_specs(t, (gw, gw, gw, gw, sq, LANES)),
        out_specs=_gdn_specs(t, (gw, gw)) + [pl.BlockSpec((gs, nh, hd, hd), lambda i: (i, 0, 0, 0))],
        out_shape=[jax.ShapeDtypeStruct((t, gw), F32), jax.ShapeDtypeStruct((t, gw), BF16),
                   jax.ShapeDtypeStruct((t // ck, nh, hd, hd), F32)],
        scratch_shapes=[pltpu.VMEM((nh, hd, hd), F32)],
        compiler_params=_cp(("arbitrary",)), name=name,
    )(u, w, qd, kd, mm, aux)


def _gdn_seq_bwd(do, w, qd, kd, mm, vn, s_all, aux, *, name):
    t = do.shape[0]
    hd, nh, ck, gs = GDN_HEAD_DIM, GDN_HEADS, GDN_CHUNK, GDN_STEP_CHUNKS
    gw = nh * hd
    sq = nh * ck
    nsteps = t // (gs * ck)

    def body(do_ref, w_ref, qd_ref, kd_ref, mm_ref, vn_ref, sall_ref, aux_ref, dvn_ref, dqd_ref, dkd_ref, dw_ref,
             dlast_ref, ds_ref):
        @pl.when(pl.program_id(0) == 0)
        def _():
            ds_ref[...] = jnp.zeros_like(ds_ref)

        lane = lax.broadcasted_iota(jnp.int32, (ck, LANES), 1)
        rowi = lax.broadcasted_iota(jnp.int32, (ck, LANES), 0)
        heads = range(nh)
        hcols = [_gdn_head_cols(h) for h in heads]
        dsns = [ds_ref[h] for h in heads]
        for g in reversed(range(gs)):
            rows = slice(g * ck, (g + 1) * ck)
            last = (g + 1) * ck - 1
            sts = [sall_ref[g, h] for h in heads]
            stbs = [st.astype(BF16) for st in sts]
            dsbs = [dsn.astype(BF16) for dsn in dsns]
            dobs = [do_ref[rows, c].astype(BF16) for c in hcols]
            dvns = [_dot(mm_ref[rows, h * ck:(h + 1) * ck], dobs[h], TN) + _dot(kd_ref[rows, hcols[h]], dsbs[h])
                    for h in heads]
            dqds = [_dot(dob, stb, NT) for dob, stb in zip(dobs, stbs)]
            dkds = [_dot(vn_ref[rows, c], dsb, NT) for c, dsb in zip(hcols, dsbs)]
            q_o = [_dot(qd_ref[rows, c], dob, TN) for c, dob in zip(hcols, dobs)]
            dvbs = [dvn.astype(BF16) for dvn in dvns]
            dws = [_dot(dvb, stb, NT) for dvb, stb in zip(dvbs, stbs)]
            w_v = [_dot(w_ref[rows, c], dvb, TN) for c, dvb in zip(hcols, dvbs)]
            gam_c = [jnp.exp(aux_ref[last:last + 1, 2 * nh + h:2 * nh + h + 1]) for h in heads]
            dlast = jnp.zeros((ck, LANES), F32)
            for h, c in zip(heads, hcols):
                dvn_ref[rows, c] = dvns[h]
                dqd_ref[rows, c] = dqds[h]
                dkd_ref[rows, c] = dkds[h]
                dw_ref[rows, c] = -dws[h]
                dgam_c = jnp.sum(jnp.sum(dsns[h] * sts[h], axis=1, keepdims=True), axis=0, keepdims=True)
                dlast = dlast + jnp.where((rowi == ck - 1) & (lane == h), gam_c[h] * dgam_c, 0.0)
            dlast_ref[rows, :] = dlast
            dsns = [q_o[h] + gam_c[h] * dsns[h] - w_v[h] for h in heads]
        for h in heads:
            ds_ref[h] = dsns[h]

    return pl.pallas_call(
        body, grid=(nsteps,),
        in_specs=_gdn_specs(t, (gw, gw, gw, gw, sq, gw), reverse=True)
        + [pl.BlockSpec((gs, nh, hd, hd), lambda i: (nsteps - 1 - i, 0, 0, 0))] + _gdn_specs(t, (LANES,), reverse=True),
        out_specs=_gdn_specs(t, (gw, gw, gw, gw, LANES), reverse=True),
        out_shape=[jax.ShapeDtypeStruct((t, gw), F32)] * 4 + [jax.ShapeDtypeStruct((t, LANES), F32)],
        scratch_shapes=[pltpu.VMEM((nh, hd, hd), F32)],
        compiler_params=_cp(("arbitrary",)), name=name,
    )(do, w, qd, kd, mm, vn, s_all, aux)


def _gdn_local_bwd(qkvn, aux, aux_t, tinv, u, w, vn, do, dvn, dqd, dkd, dw, dlast, *, name):
    t = qkvn.shape[0]
    hd, nh, ck, gs = GDN_HEAD_DIM, GDN_HEADS, GDN_CHUNK, GDN_STEP_CHUNKS
    gw = nh * hd
    sq = nh * ck

    def body(x_ref, aux_ref, auxt_ref, tinv_ref, u_ref, w_ref, vn_ref, do_ref, dvn_ref, dqd_ref, dkd_ref, dw_ref,
             dlast_ref, dx_ref, daux_ref):
        lane = lax.broadcasted_iota(jnp.int32, (ck, LANES), 1)
        ones = jnp.ones((ck, LANES), F32)
        ii = lax.broadcasted_iota(jnp.int32, (ck, ck), 0)
        jj = lax.broadcasted_iota(jnp.int32, (ck, ck), 1)
        suffix = jnp.where(jj >= ii, 1.0, 0.0)
        for g0 in range(0, gs, GDN_ILP_CHUNKS_BWD):
            where = [(g, h) for g in range(g0, g0 + GDN_ILP_CHUNKS_BWD) for h in range(nh)]
            at = [(slice(g * ck, (g + 1) * ck), _gdn_head_cols(h)) for g, h in where]
            ins = [_gdn_chunk_inputs(x_ref, aux_ref, auxt_ref, g, h) for g, h in where]
            lcs = _gdn_local([(q, k, b, gc, gc_row) for q, k, _, b, gc, gc_row in ins], False)
            tinvs = [tinv_ref[slice(g * ck, (g + 1) * ck), h * ck:(h + 1) * ck] for g, h in where]
            dms = [jnp.where(lc["lower"], _bdot(do_ref[r, c], vn_ref[r, c], NT), 0.0) for lc, (r, c) in zip(lcs, at)]
            drvs = _dot3_many(tinvs, [dvn_ref[r, c] for r, c in at], TN)
            drks = _dot3_many(tinvs, [dw_ref[r, c] for r, c in at], TN)
            das = [jnp.where(lc["strict"], -(_bdot(drv, u_ref[r, c], NT) + _bdot(drk, w_ref[r, c], NT)), 0.0)
                   for lc, (r, c), drv, drk in zip(lcs, at, drvs, drks)]
            f_mats = [da * (i[3] * lc["kk"]) * lc["dmat"] + dm * lc["qk"] * lc["dmat"]
                      for i, lc, da, dm in zip(ins, lcs, das, dms)]
            col_sums = _dot3_many(f_mats, [ones] * len(where), TN)
            dgc_all = {g: dlast_ref[slice(g * ck, (g + 1) * ck), :] for g in range(g0, g0 + GDN_ILP_CHUNKS_BWD)}
            db_all = {g: jnp.zeros((ck, LANES), F32) for g in range(g0, g0 + GDN_ILP_CHUNKS_BWD)}
            e_mats = [da * lc["dmat"] * i[3] for i, lc, da in zip(ins, lcs, das)]
            dmds = [dm * lc["dmat"] for lc, dm in zip(lcs, dms)]
            dq_mm = [_bdot(dmd, i[1]) for i, dmd in zip(ins, dmds)]
            dk_mm = [_bdot(e, i[1]) + _bdot(e, i[1], TN) + _bdot(dmd, i[0], TN) for i, e, dmd in zip(ins, e_mats, dmds)]
            for n, ((g, h), (q, k, v, b, _, _), lc, (rows, cols)) in enumerate(zip(where, ins, lcs, at)):
                dmat, kk, gam, kdec = (lc[key] for key in ("dmat", "kk", "gam", "kdec"))
                drv, drk, da = drvs[n], drks[n], das[n]
                dqd_h, dkd_h = dqd_ref[rows, cols], dkd_ref[rows, cols]
                rs_rk = jnp.sum(drk * k, axis=-1, keepdims=True)
                db = (jnp.sum(drv * v, axis=-1, keepdims=True) + gam * rs_rk
                      + jnp.sum(da * kk * dmat, axis=-1, keepdims=True))
                dx_ref[rows, cols] = dq_mm[n] + gam * dqd_h
                dx_ref[rows, gw + cols.start:gw + cols.stop] = (b * gam) * drk + dk_mm[n] + kdec * dkd_h
                dx_ref[rows, 2 * gw + cols.start:2 * gw + cols.stop] = b * drv
                e_vec = jnp.sum(dkd_h * (kdec * k), axis=-1, keepdims=True)
                dgc = (b * gam * rs_rk + gam * jnp.sum(dqd_h * q, axis=-1, keepdims=True)
                       + jnp.sum(f_mats[n], axis=-1, keepdims=True) - col_sums[n][:, 0:1] - e_vec)
                is_last = lax.broadcasted_iota(jnp.int32, (ck, 1), 0) == ck - 1
                dgc = dgc + jnp.where(is_last, jnp.sum(e_vec, axis=0, keepdims=True), 0.0)
                dgc_all[g] = dgc_all[g] + jnp.where(lane == h, dgc, 0.0)
                db_all[g] = db_all[g] + jnp.where(lane == nh + h, db, 0.0)
            for g in dgc_all:
                daux_ref[slice(g * ck, (g + 1) * ck), :] = _dot3(suffix, dgc_all[g]) + db_all[g]

    return pl.pallas_call(
        body, grid=(t // (gs * ck),),
        in_specs=_gdn_specs(t, (3 * gw, LANES)) + [pl.BlockSpec((gs, 16, ck), lambda i: (i, 0, 0))]
        + _gdn_specs(t, (sq, gw, gw, gw, gw, gw, gw, gw, gw, LANES)),
        out_specs=_gdn_specs(t, (3 * gw, LANES)),
        out_shape=[jax.ShapeDtypeStruct((t, 3 * gw), F32), jax.ShapeDtypeStruct((t, LANES), F32)],
        compiler_params=_cp(("parallel",)), name=name,
    )(qkvn, aux, aux_t, tinv, u, w, vn, do, dvn, dqd, dkd, dw, dlast)


def _gdn_pre_bwd1(xc, dqkvn, daux, ab, alog_pad, dt_pad, dkv, dp, dp_col, *, name, ab_blk=0, tm=256):
    t, cw = xc.shape
    hd, nh = GDN_HEAD_DIM, GDN_HEADS
    gw = nh * hd
    tm = _blk(t, tm)

    kvw = dkv.shape[1]
    seg = kvw + AB_PAD
    assert dp_col % seg == 0

    def body(xc_ref, dy_ref, daux_ref, ab_ref, al_ref, dt_ref, dkv_ref, _, dxc_ref, dab_ref, dal_ref, ddt_ref):
        i = pl.program_id(0)
        xc = xc_ref[...]
        sg = _sigmoid(xc)
        s = xc * sg
        dsilu = sg * (1.0 + xc * (1.0 - sg))
        for h in range(2 * nh):
            xh = s[:, h * hd:(h + 1) * hd]
            scale = hd ** -0.5 if h < nh else 1.0
            dyh = dy_ref[:, h * hd:(h + 1) * hd] * scale
            r = lax.rsqrt(jnp.sum(xh * xh, axis=-1, keepdims=True) + L2_EPS)
            dxh = r * dyh - xh * (r * r * r) * jnp.sum(dyh * xh, axis=-1, keepdims=True)
            dxc_ref[:, h * hd:(h + 1) * hd] = dxh * dsilu[:, h * hd:(h + 1) * hd]
        dxc_ref[:, 2 * gw:] = dy_ref[:, 2 * gw:] * dsilu[:, 2 * gw:]
        abv = ab_ref[...]
        dauxv = daux_ref[...]
        lane = lax.broadcasted_iota(jnp.int32, abv.shape, 1)
        is_a = lane < nh
        is_b = (lane >= nh) & (lane < 2 * nh)
        pre = abv + dt_ref[...]
        neg_ea = -jnp.exp(al_ref[...])
        d_a = jnp.where(is_a, dauxv * neg_ea * _sigmoid(pre), 0.0)
        beta = _sigmoid(abv)
        d_b = jnp.where(is_b, dauxv * beta * (1.0 - beta), 0.0)
        dab_ref[:, :kvw] = dkv_ref[...]
        dab_ref[:, kvw:kvw + LANES] = (d_a + d_b).astype(dab_ref.dtype)
        dab_ref[:, kvw + LANES:] = jnp.zeros((tm, AB_PAD - LANES), dab_ref.dtype)
        dal = jnp.sum(jnp.where(is_a, dauxv * neg_ea * _softplus(pre), 0.0), axis=0, keepdims=True)
        ddt = jnp.sum(d_a, axis=0, keepdims=True)

        @pl.when(i == 0)
        def _():
            dal_ref[...] = dal
            ddt_ref[...] = ddt

        @pl.when(i > 0)
        def _():
            dal_ref[...] += dal
            ddt_ref[...] += ddt

    row = lambda c: pl.BlockSpec((tm, c), lambda i: (i, 0))
    vec = pl.BlockSpec((1, LANES), lambda i: (0, 0))
    return pl.pallas_call(
        body, grid=(t // tm,),
        in_specs=[row(cw), row(cw), row(LANES), pl.BlockSpec((tm, LANES), lambda i: (i, ab_blk)), vec, vec, row(kvw),
                  ANY_SPEC],
        out_specs=[row(cw), pl.BlockSpec((tm, seg), lambda i: (i, dp_col // seg)), vec, vec],
        out_shape=[jax.ShapeDtypeStruct((t, cw), F32), jax.ShapeDtypeStruct(dp.shape, dp.dtype),
                   jax.ShapeDtypeStruct((1, LANES), F32), jax.ShapeDtypeStruct((1, LANES), F32)],
        input_output_aliases={7: 1}, compiler_params=_cp(("arbitrary",)), name=name,
    )(xc, dqkvn, daux, ab, alog_pad, dt_pad, dkv, dp)


def _gdn_pre_bwd2(dxc, qkvb, conv_w, dp, dp_col, *, name, tm=512):
    t, cw = dxc.shape
    tm = _blk(t, tm)
    hb = tm // HALO
    nblk = t // tm
    cg = GDN_HEADS * GDN_HEAD_DIM
    assert cw % cg == 0 and dp_col % cg == 0
    col0 = dp_col // cg

    def body(d_ref, dn_ref, x_ref, xp_ref, w_ref, _, dx_ref, dw_ref):
        i = pl.program_id(1)
        dcur = d_ref[...]
        dnxt = jnp.where(i < nblk - 1, dn_ref[...], 0.0)
        cur = x_ref[...]
        prev = jnp.where(i > 0, xp_ref[...], 0.0)
        dx = None
        dws = []
        for tap in range(GDN_CONV):
            j = GDN_CONV - 1 - tap
            term = w_ref[tap:tap + 1, :] * _shift_up(dcur, dnxt, j)
            dx = term if dx is None else dx + term
            dws.append(jnp.sum(dcur * _shift_down(cur, prev, j), axis=0, keepdims=True))
        dx_ref[...] = dx.astype(dx_ref.dtype)
        dw = jnp.concatenate(dws, axis=0)

        @pl.when(i == 0)
        def _():
            dw_ref[...] = dw

        @pl.when(i > 0)
        def _():
            dw_ref[...] += dw

    row = pl.BlockSpec((tm, cg), lambda c, i: (i, c))
    wsp = pl.BlockSpec((GDN_CONV, cg), lambda c, i: (0, c))
    return pl.pallas_call(
        body, grid=(cw // cg, nblk),
        in_specs=[row, pl.BlockSpec((HALO, cg), lambda c, i: (jnp.minimum((i + 1) * hb, t // HALO - 1), c)),
                  row, pl.BlockSpec((HALO, cg), lambda c, i: (jnp.maximum(i * hb - 1, 0), c)), wsp, ANY_SPEC],
        out_specs=[pl.BlockSpec((tm, cg), lambda c, i: (i, col0 + c)), wsp],
        out_shape=[jax.ShapeDtypeStruct(dp.shape, dp.dtype), jax.ShapeDtypeStruct((GDN_CONV, cw), F32)],
        input_output_aliases={5: 0}, compiler_params=_cp(("arbitrary", "arbitrary")), name=name,
    )(dxc, dxc, qkvb, qkvb, conv_w, dp)


def _gdn_post_fwd(o, z, norm_w, *, name, tm=512):
    t, gw = o.shape
    hd, nh = GDN_HEAD_DIM, GDN_HEADS
    tm = _blk(t, tm)

    def body(o_ref, z_ref, w_ref, y_ref):
        zv = z_ref[...]
        sz = zv * _sigmoid(zv)
        for h in range(nh):
            oh = o_ref[:, h * hd:(h + 1) * hd]
            r = lax.rsqrt(jnp.mean(oh * oh, axis=-1, keepdims=True) + RMS_EPS)
            y_ref[:, h * hd:(h + 1) * hd] = (oh * r * w_ref[...] * sz[:, h * hd:(h + 1) * hd]).astype(y_ref.dtype)

    row = pl.BlockSpec((tm, gw), lambda i: (i, 0))
    return pl.pallas_call(
        body, grid=(t // tm,), in_specs=[row, row, pl.BlockSpec((1, hd), lambda i: (0, 0))], out_specs=row,
        out_shape=jax.ShapeDtypeStruct((t, gw), BF16), compiler_params=_cp(("parallel",)), name=name,
    )(o, z, norm_w)


def _gdn_post_bwd(dy, o, z, norm_w, dp, dp_col, *, name, tm=512):
    t, gw = o.shape
    hd, nh = GDN_HEAD_DIM, GDN_HEADS
    tm = _blk(t, tm)

    def body(dy_ref, o_ref, z_ref, w_ref, _, do_ref, dz_ref, dw_ref):
        i = pl.program_id(0)
        zv = z_ref[...]
        sg = _sigmoid(zv)
        sz = zv * sg
        dsz = sg * (1.0 + zv * (1.0 - sg))
        dw = None
        for h in range(nh):
            sl = slice(h * hd, (h + 1) * hd)
            oh = o_ref[:, sl]
            dyh = dy_ref[:, sl].astype(F32)
            r = lax.rsqrt(jnp.mean(oh * oh, axis=-1, keepdims=True) + RMS_EPS)
            xh = oh * r
            dz_ref[:, sl] = (dyh * xh * w_ref[...] * dsz[:, sl]).astype(dz_ref.dtype)
            dn = dyh * sz[:, sl]
            dxh = dn * w_ref[...]
            do_ref[:, sl] = r * (dxh - xh * jnp.mean(dxh * xh, axis=-1, keepdims=True))
            part = jnp.sum(dn * xh, axis=0, keepdims=True)
            dw = part if dw is None else dw + part

        @pl.when(i == 0)
        def _():
            dw_ref[...] = dw

        @pl.when(i > 0)
        def _():
            dw_ref[...] += dw

    row = pl.BlockSpec((tm, gw), lambda i: (i, 0))
    vec = pl.BlockSpec((1, hd), lambda i: (0, 0))
    return pl.pallas_call(
        body, grid=(t // tm,), in_specs=[row, row, row, vec, ANY_SPEC],
        out_specs=[row, pl.BlockSpec((tm, gw), lambda i: (i, dp_col // gw)), vec],
        out_shape=[jax.ShapeDtypeStruct((t, gw), F32), jax.ShapeDtypeStruct(dp.shape, dp.dtype),
                   jax.ShapeDtypeStruct((1, hd), F32)],
        input_output_aliases={4: 1}, compiler_params=_cp(("arbitrary",)), name=name,
    )(dy, o, z, norm_w, dp)


IN_NAMES = ("q_a", "kv_a", "qkv_b", "ab", "z", "q_c", "gates")
CAT_NAMES = ("gates", "q_a", "qkv_b", "z", "q_c", "kv_a", "ab")
AB_PAD = 256


def _in_widths(d):
    gw = GDN_HEADS * GDN_HEAD_DIM
    return dict(q_a=SWA_Q_HEADS * SWA_HEAD_DIM, kv_a=2 * SWA_KV_HEADS * SWA_HEAD_DIM, qkv_b=3 * gw, ab=2 * GDN_HEADS,
                z=gw, q_c=XA_HEADS * XA_HEAD_DIM, gates=3 * d)


def _ranges(names, widths):
    out, start = {}, 0
    for k in names:
        out[k] = (start, widths[k])
        start += widths[k]
    return out, start


def _cat_ranges(d):
    widths = dict(_in_widths(d), ab=AB_PAD)
    return _ranges(CAT_NAMES, widths)


def _to_cat(shards):
    ns, d, n = shards.shape
    src, _ = _ranges(IN_NAMES, _in_widths(d))
    cols = []
    for k in CAT_NAMES:
        lo, hi = src[k][0], src[k][0] + src[k][1]
        for s in range(ns):
            a, b = max(lo, s * n), min(hi, (s + 1) * n)
            if a < b:
                cols.append(shards[s][:, a - s * n:b - s * n])
    cols.append(jnp.zeros((d, AB_PAD - src["ab"][1]), shards.dtype))
    return jnp.concatenate(cols, axis=1)


def _from_cat(w_cat):
    d = w_cat.shape[0]
    src, total = _ranges(IN_NAMES, _in_widths(d))
    cat, _ = _cat_ranges(d)
    n = total // N_SHARDS
    shards = []
    for s in range(N_SHARDS):
        pieces = []
        for k in IN_NAMES:
            a, b = max(s * n, src[k][0]), min((s + 1) * n, src[k][0] + src[k][1])
            if a < b:
                pieces.append(w_cat[:, cat[k][0] + a - src[k][0]:cat[k][0] + b - src[k][0]])
        shards.append(jnp.concatenate(pieces, axis=1))
    return jnp.stack(shards)


def _pad_cols(a, width):
    return jnp.pad(a, ((0, 0), (0, width - a.shape[1])))


def _relu2_epilogue(acc):
    r = jnp.maximum(acc, 0.0)
    return acc, r * r


def _add_epilogue(acc, res):
    return (acc + res,)


def _drelu2_epilogue(acc, u):
    return (acc * (2.0 * jnp.maximum(u.astype(F32), 0.0)),)


def _local_step(x, mem, tgt, wts, small, comm=None):
    t, d = x.shape
    nh = GDN_HEADS
    w_cat = wts["w_cat"]
    cat, cat_w = _cat_ranges(d)
    assert w_cat.shape == (d, cat_w)
    alog_pad = _pad_cols(small["a_log"], LANES)
    dt_pad = _pad_cols(small["dt_bias"], LANES)
    kvw = cat["kv_a"][1]
    assert cat["ab"][0] == cat["kv_a"][0] + kvw
    ab_blk = kvw // LANES

    n = _rms_fwd(x, small["g_mix"], name="rms_mix")
    q_a = _mm(n, w_cat, b_window=cat["q_a"], out_dtypes=(BF16,), name="in_q_a")
    kv_a, ab = _mm(n, w_cat, b_window=(cat["kv_a"][0], kvw + AB_PAD), out_dtypes=(BF16, F32), name="in_kv_ab")
    qkvb = _mm(n, w_cat, b_window=cat["qkv_b"], tn=512, name="in_qkv_b")
    z = _mm(n, w_cat, b_window=cat["z"], name="in_z")
    q_c = _mm(n, w_cat, b_window=cat["q_c"], out_dtypes=(BF16,), name="in_q_c")
    if comm is None:
        gates = _mm(n, w_cat, b_window=cat["gates"], name="in_gates")
        y_a, lse = _swa_fwd(q_a, kv_a, small["sinks"], name="swa_fwd")
    else:
        gates, landed_mlp = _mm(n, w_cat, b_window=cat["gates"], name="in_gates",
                                exchange=comm.gather_exchange(comm.MLP[1:]))
        y_a, lse, landed = _swa_fwd(q_a, kv_a, small["sinks"], name="swa_fwd", exchange=comm.gather_exchange(comm.MLP[:1]))
        landed_mlp = landed + landed_mlp
    xc, qkvn, aux = _gdn_pre_fwd(qkvb, small["conv_w"], ab, alog_pad, dt_pad, ab_blk=ab_blk, name="gdn_pre_fwd")
    aux_t = aux[:, :16].reshape(t // GDN_CHUNK, GDN_CHUNK, 16).transpose(0, 2, 1)
    if comm is None:
        gdn_u, gdn_w, gdn_qd, gdn_kd, gdn_mm, gdn_tinv = _gdn_local_fwd(qkvn, aux, aux_t, name="gdn_local_fwd")
    else:
        gdn_u, gdn_w, gdn_qd, gdn_kd, gdn_mm, gdn_tinv, landed_mid = _gdn_local_fwd(
            qkvn, aux, aux_t, name="gdn_local_fwd", exchange=comm.gather_exchange(comm.mid))
        wts = dict(wts, **comm.gathered(comm.mid, landed_mid, "mid"))
    o_b, gdn_vn, s_all = _gdn_seq_fwd(gdn_u, gdn_w, gdn_qd, gdn_kd, gdn_mm, aux, name="gdn_seq_fwd")
    y_b = _gdn_post_fwd(o_b, z, small["gdn_norm_w"], name="gdn_post_fwd")
    nmem = _rms_fwd(mem, small["g_mem"], name="rms_mem")
    mkv = _mm(nmem, wts["w_mem_kv"], out_dtypes=(BF16,), name="mem_kv")
    y_c = _xa_fwd(q_c, mkv, name="xa_fwd")
    ys = (y_a, y_b, y_c)
    w_ups = (wts["w_swa_up"], wts["w_gdn_up"], wts["w_xa_up"])
    merged = _merge_fwd(ys, w_ups, gates, name="merge_fwd")
    if comm is None:
        h1 = _mm(merged, wts["w_out"], extras=(x,), epilogue=_add_epilogue, name="out_proj")
    else:
        h1, whole = _mm(merged, wts["w_out"], extras=(x,), epilogue=_add_epilogue, name="out_proj",
                        exchange=_gather_pass_on(landed_mlp))
        wts = dict(wts, **comm.as_weights(comm.MLP, whole))
    n2 = _rms_fwd(h1, small["g_mlp"], name="rms_mlp")
    u, act = _mm(n2, wts["w_mlp_in"], b_sharded=True, out_dtypes=(BF16, BF16), epilogue=_relu2_epilogue, name="mlp_in")
    h2 = _mm(act, wts["w_mlp_out"], extras=(h1,), epilogue=_add_epilogue, name="mlp_out")
    dh2, dh2_b, dg_final, loss = _final_loss(h2, small["g_final"], tgt, name="final_loss")

    grads = {"g_final": dg_final}
    du = _mm(dh2_b, wts["w_mlp_out"], tb=True, out_dtypes=(BF16,), extras=(u,), epilogue=_drelu2_epilogue, name="d_mlp_act")
    grads["w_mlp_out"] = _mm(act, dh2_b, ta=True, out_dtypes=(BF16,), name="dw_mlp_out")
    grads["w_mlp_in"] = _mm(n2, du, ta=True, out_sharded=True, out_dtypes=(BF16,), name="dw_mlp_in")
    if comm is None:
        dn2 = _mm(du, wts["w_mlp_in"], tb=True, b_sharded=True, name="d_mlp_in")
    else:
        g_mlp = [comm.shard_major(k, grads.pop(k)) for k in comm.MLP]
        dn2, sib_mlp = _mm(du, wts["w_mlp_in"], tb=True, b_sharded=True, name="d_mlp_in", exchange=_sibling_halves(g_mlp))
        s1_mlp = comm.pair_sums(g_mlp, "mlp", sib_mlp)
    dh1, dh1_b, grads["g_mlp"] = _rms_bwd(dn2, h1, small["g_mlp"], dh2, name="rms_mlp_bwd")
    dmerged = _mm(dh1_b, wts["w_out"], tb=True, name="d_out_proj")
    grads["w_out"] = _mm(merged, dh1_b, ta=True, out_dtypes=(BF16,), name="dw_out")
    *dus, dp = _merge_bwd(ys, w_ups, gates, dmerged, cat_w, name="merge_bwd")
    dys = []
    for y, du_i, w_up, key in zip(ys, dus, w_ups, ("w_swa_up", "w_gdn_up", "w_xa_up")):
        dys.append(_mm(du_i, w_up, tb=True, b_sharded=True, out_dtypes=(BF16,), name="d_" + key))
        grads[key] = _mm(y, du_i, ta=True, out_sharded=True, out_dtypes=(BF16,), name="dw_" + key[2:])
    dp, dkv_a, grads["sinks"] = _swa_bwd(q_a, kv_a, small["sinks"], y_a, lse, dys[0], dp, cat["q_a"][0], name="swa_bwd")
    do_b, dp, grads["gdn_norm_w"] = _gdn_post_bwd(dys[1], o_b, z, small["gdn_norm_w"], dp, cat["z"][0],
                                                  name="gdn_post_bwd")
    dvn, dqd, dkd, dw_, dlast = _gdn_seq_bwd(do_b, gdn_w, gdn_qd, gdn_kd, gdn_mm, gdn_vn, s_all, aux, name="gdn_seq_bwd")
    dqkvn, daux = _gdn_local_bwd(qkvn, aux, aux_t, gdn_tinv, gdn_u, gdn_w, gdn_vn, do_b, dvn, dqd, dkd, dw_, dlast,
                                 name="gdn_local_bwd")
    dxc, dp, dalog, ddt = _gdn_pre_bwd1(xc, dqkvn, daux, ab, alog_pad, dt_pad, dkv_a, dp, cat["kv_a"][0], ab_blk=ab_blk,
                                        name="gdn_pre_bwd1")
    grads["a_log"], grads["dt_bias"] = dalog[:, :nh], ddt[:, :nh]
    dp, grads["conv_w"] = _gdn_pre_bwd2(dxc, qkvb, small["conv_w"], dp, cat["qkv_b"][0], name="gdn_pre_bwd2")
    dp, dmkv = _xa_bwd(q_c, mkv, dys[2], dp, cat["q_c"][0], name="xa_bwd")
    grads["w_mem_kv"] = _mm(nmem, dmkv, ta=True, out_dtypes=(BF16,), name="dw_mem_kv")
    dnmem = _mm(dmkv, wts["w_mem_kv"], tb=True, name="d_mem_kv")
    _, _, grads["g_mem"] = _rms_bwd(dnmem, mem, small["g_mem"], jnp.zeros_like(mem), name="rms_mem_bwd")
    if comm is None:
        grads["w_cat"] = _mm(n, dp, ta=True, out_dtypes=(BF16,), name="dw_in")
        dn = _mm(dp, w_cat, tb=True, name="d_in_proj")
    else:
        s1_mid = comm.pair_sums([comm.shard_major(k, grads.pop(k)) for k in comm.mid], "mid")
        dw_cat, rcv_mlp = _mm(n, dp, ta=True, out_dtypes=(BF16,), name="dw_in", exchange=_chip_exchange(s1_mlp))
        s1_in = comm.pair_sums([_from_cat(dw_cat)], "in")
        dn, rcv_rest = _mm(dp, w_cat, tb=True, name="d_in_proj", exchange=_chip_exchange(s1_in + s1_mid))
        reduced = comm.finish(s1_in + s1_mid + s1_mlp, rcv_rest + rcv_mlp)
        grads.update(zip(["w_in"] + comm.mid + list(comm.MLP), reduced))
    dx, _, grads["g_mix"] = _rms_bwd(dn, x, small["g_mix"], dh1, name="rms_mix_bwd")
    return loss, dx, grads


HBM_SPEC = pl.BlockSpec(memory_space=pltpu.HBM)
VMEM_SPEC = pl.BlockSpec(memory_space=pltpu.VMEM)
N_CHIPS = N_SHARDS
N_DEV = 8
DMA_CHUNK_BYTES = 1 << 20


def _place():
    return lax.axis_index("x"), lax.axis_index("y"), lax.axis_index("c")


def _other_chips(x, y):
    return [(1 - x, y), (x, 1 - y), (1 - x, 1 - y)]


def _n_chunks(rows, row_bytes):
    n = 1
    while rows % (2 * n) == 0 and (rows // (2 * n)) % 16 == 0 and (rows // n) * row_bytes > DMA_CHUNK_BYTES:
        n *= 2
    return n


def _sem_scratch(n_remote, n_local):
    return [pltpu.SemaphoreType.DMA((max(n_remote, 1),)), pltpu.SemaphoreType.DMA((max(n_remote, 1),)),
            pltpu.SemaphoreType.DMA((max(n_local, 1),))]


def _all_gather_weights(shards, *, name):
    first = _exchange_call(_gather_over_ici(shards), name=name + "_ici")
    return _exchange_call(_gather_pass_on(first), name=name + "_pass")


def _gather_over_ici(shards):
    plan = _half_chunks(shards, 0)

    def copies_of(in_refs, out_refs, place):
        x, y, c = place
        remote, local = [], []
        for i, r0, nr in plan:
            rh = shards[i].shape[0] // 2
            mine = pl.ds(c * rh + r0, nr)
            for chip in _other_chips(x, y):
                remote.append((in_refs[i].at[mine], out_refs[i].at[2 * x + y, mine], (*chip, c)))
            for half in range(2):
                rows = pl.ds(half * rh + r0, nr)
                local.append((in_refs[i].at[rows], out_refs[i].at[2 * x + y, rows]))
        return remote, local

    shapes = tuple(jax.ShapeDtypeStruct((N_CHIPS, *s.shape), s.dtype) for s in shards)
    return Exchange(tuple(shards), shapes, 3 * len(plan), 2 * len(plan), copies_of)


def _gather_pass_on(arrived):
    plan = _half_chunks([jax.ShapeDtypeStruct(a.shape[1:], a.dtype) for a in arrived], 0)

    def copies_of(in_refs, out_refs, place):
        x, y, c = place
        remote = []
        for i, r0, nr in plan:
            mine = pl.ds(c * (arrived[i].shape[1] // 2) + r0, nr)
            for chip in _other_chips(x, y):
                rows = out_refs[i].at[2 * chip[0] + chip[1], mine]
                remote.append((rows, rows, (x, y, 1 - c)))
        return remote, []

    shapes = tuple(jax.ShapeDtypeStruct(a.shape, a.dtype) for a in arrived)
    return Exchange(tuple(arrived), shapes, 3 * len(plan), 0, copies_of, tuple((i, i) for i in range(len(arrived))))


def _exchange_call(ex, *, name):
    n_in, n_out = len(ex.ins), len(ex.out_shapes)

    def body(*refs):
        cps = _exchange_copies(ex, refs[:n_in], refs[n_in:n_in + n_out], refs[n_in + n_out:])
        for cp in cps:
            cp.start()
        for cp in cps:
            cp.wait()

    return pl.pallas_call(
        body, out_shape=list(ex.out_shapes), in_specs=[HBM_SPEC] * n_in, out_specs=[HBM_SPEC] * n_out,
        scratch_shapes=_sem_scratch(ex.n_remote, ex.n_local), input_output_aliases=dict(ex.aliases), name=name,
    )(*ex.ins)


def _half_chunks(arrs, row_axis):
    plan = []
    for i, a in enumerate(arrs):
        rh = a.shape[row_axis] // 2
        row_bytes = a.dtype.itemsize * math.prod(a.shape) // a.shape[row_axis]
        nch = _n_chunks(rh, row_bytes)
        plan += [(i, q * (rh // nch), rh // nch) for q in range(nch)]
    return plan


def _sibling_halves(gs):
    plan = _half_chunks(gs, 1)

    def copies_of(in_refs, out_refs, place):
        x, y, c = place
        out = []
        for i, r0, nr in plan:
            rh = gs[i].shape[1] // 2
            out.append((in_refs[i].at[:, pl.ds((1 - c) * rh + r0, nr), :], out_refs[i].at[:, pl.ds(r0, nr), :],
                        (x, y, 1 - c)))
        return out, []

    shapes = tuple(jax.ShapeDtypeStruct((g.shape[0], g.shape[1] // 2, g.shape[2]), g.dtype) for g in gs)
    return Exchange(tuple(gs), shapes, len(plan), 0, copies_of)


def _chip_exchange(s1s):
    plan = _half_chunks([jax.ShapeDtypeStruct((2 * s.shape[1], s.shape[2]), s.dtype) for s in s1s], 0)

    def copies_of(in_refs, out_refs, place):
        x, y, c = place
        out = []
        for i, r0, nr in plan:
            for j, chip in enumerate(_other_chips(x, y)):
                out.append((in_refs[i].at[2 * chip[0] + chip[1], pl.ds(r0, nr), :], out_refs[i].at[j, pl.ds(r0, nr), :],
                            (*chip, c)))
        return out, []

    shapes = tuple(jax.ShapeDtypeStruct((3, *s.shape[1:]), s.dtype) for s in s1s)
    return Exchange(tuple(s1s), shapes, 3 * len(plan), 0, copies_of)


def _join_halves(gs, *, name):
    plan = _half_chunks(gs, 0)

    def copies_of(in_refs, out_refs, place):
        x, y, c = place
        out = []
        for i, r0, nr in plan:
            rows = out_refs[i].at[pl.ds(c * (gs[i].shape[0] // 2) + r0, nr), :]
            out.append((rows, rows, (x, y, 1 - c)))
        return out, []

    shapes = tuple(jax.ShapeDtypeStruct(g.shape, g.dtype) for g in gs)
    aliases = tuple((i, i) for i in range(len(gs)))
    return _exchange_call(Exchange(tuple(gs), shapes, len(plan), 0, copies_of, aliases), name=name)


def _row_block(rows, cols):
    tb = rows
    while tb % 32 == 0 and tb * cols * 4 > (2 << 20):
        tb //= 2
    return tb


def _pair_sum(g, sib, core, *, name):
    ns, r, c = g.shape
    rh = r // 2
    tb = _row_block(rh, c)
    nb = rh // tb

    def body(core_ref, g_ref, s_ref, o_ref):
        o_ref[...] = (g_ref[...].astype(F32) + s_ref[...].astype(F32)).astype(o_ref.dtype)

    mine = pl.BlockSpec((None, tb, c), lambda s, i, core_ref: (s, core_ref[0] * nb + i, 0))
    half = pl.BlockSpec((None, tb, c), lambda s, i, core_ref: (s, i, 0))
    return pl.pallas_call(
        body, grid_spec=pltpu.PrefetchScalarGridSpec(num_scalar_prefetch=1, grid=(ns, nb), in_specs=[mine, half],
                                                     out_specs=half),
        out_shape=jax.ShapeDtypeStruct((ns, rh, c), BF16), compiler_params=_cp(("parallel", "parallel")), name=name,
    )(core, g, sib)


def _chip_sum(s1, rcv, where, *, name):
    _, rh, c = s1.shape
    tb = _row_block(rh, c)
    nb = rh // tb

    def body(where_ref, own_ref, r0_ref, r1_ref, r2_ref, o_ref):
        acc = own_ref[...].astype(F32)
        for r in (r0_ref, r1_ref, r2_ref):
            acc = acc + r[...].astype(F32)
        o_ref[...] = acc

    own = pl.BlockSpec((None, tb, c), lambda i, w: (w[1], i, 0))
    got = [pl.BlockSpec((None, tb, c), functools.partial(lambda i, w, j: (j, i, 0), j=j)) for j in range(3)]
    return pl.pallas_call(
        body, grid_spec=pltpu.PrefetchScalarGridSpec(
            num_scalar_prefetch=1, grid=(nb,), in_specs=[own] + got,
            out_specs=pl.BlockSpec((tb, c), lambda i, w: (w[0] * nb + i, 0))),
        out_shape=jax.ShapeDtypeStruct((2 * rh, c), F32), compiler_params=_cp(("parallel",)), name=name,
    )(where, s1, rcv, rcv, rcv)


def _all_gather_small(blk, *, name):
    r = blk.shape[0]

    def body(b_ref, out_ref, send_sems, recv_sems):
        x, y, c = _place()
        me = 4 * x + 2 * y + c
        out_ref[me] = b_ref[...]
        sends = []
        for k in range(1, N_DEV):
            peer = (x ^ (k >> 2), y ^ ((k >> 1) & 1), c ^ (k & 1))
            sends.append(pltpu.make_async_remote_copy(src_ref=b_ref, dst_ref=out_ref.at[me], send_sem=send_sems.at[k - 1],
                                                      recv_sem=recv_sems.at[k - 1], device_id=peer, device_id_type=MESH))
        for cp in sends:
            cp.start()
        for k in range(1, N_DEV):
            rows = out_ref.at[me ^ k]
            pltpu.make_async_remote_copy(src_ref=rows, dst_ref=rows, send_sem=send_sems.at[k - 1],
                                         recv_sem=recv_sems.at[k - 1], device_id=(x, y, c), device_id_type=MESH).wait_recv()
        for cp in sends:
            cp.wait_send()

    return pl.pallas_call(
        body, out_shape=jax.ShapeDtypeStruct((N_DEV, r, LANES), blk.dtype), in_specs=[VMEM_SPEC], out_specs=VMEM_SPEC,
        scratch_shapes=[pltpu.SemaphoreType.DMA((N_DEV - 1,)), pltpu.SemaphoreType.DMA((N_DEV - 1,))],
        name=name,
    )(blk)


def _sum_rows(parts, out_dtype, *, name, tb=1024):
    rows = parts[0].shape[0]
    tb = _blk(rows, tb)

    def body(*refs):
        acc = refs[0][...].astype(F32)
        for r in refs[1:-1]:
            acc = acc + r[...].astype(F32)
        refs[-1][...] = acc.astype(refs[-1].dtype)

    spec = pl.BlockSpec((tb, LANES), lambda i: (i, 0))
    return pl.pallas_call(
        body, grid=(rows // tb,), in_specs=[spec] * len(parts), out_specs=spec,
        out_shape=jax.ShapeDtypeStruct((rows, LANES), out_dtype), compiler_params=_cp(("parallel",)), name=name,
    )(*parts)


BIG = (
    ("w_in", 1), ("w_mem_kv", 0), ("w_swa_up", 1), ("w_gdn_up", 1), ("w_xa_up", 1), ("w_out", 0), ("w_mlp_in", 1),
    ("w_mlp_out", 0))


class _Comm:
    MLP = ("w_mlp_in", "w_mlp_out")

    def __init__(self, late_shards, core, where):
        self.axis = dict(BIG)
        self.late_shards = late_shards
        self.mid = [k for k in late_shards if k not in self.MLP]
        self.core, self.where = core, where

    def gather_exchange(self, names):
        return _gather_over_ici([self.late_shards[k] for k in names])

    def as_weights(self, names, whole):
        return {k: (g.reshape(-1, g.shape[2]) if self.axis[k] == 0 else g) for k, g in zip(names, whole)}

    def gathered(self, names, landed, tag):
        return self.as_weights(names, _exchange_call(_gather_pass_on(landed), name=f"ag_{tag}_pass"))

    def shard_major(self, k, grad):
        return grad.reshape(N_CHIPS, -1, grad.shape[-1]) if self.axis[k] == 0 else grad

    def pair_sums(self, gs, tag, sibs=None):
        if sibs is None:
            sibs = _exchange_call(_sibling_halves(gs), name=f"rs_sibling_{tag}")
        return [_pair_sum(g, s, self.core, name=f"rs_pair_sum_{tag}{i}") for i, (g, s) in enumerate(zip(gs, sibs))]

    def finish(self, s1s, rcvs):
        halves = [_chip_sum(s1, rcv, self.where, name=f"rs_chip_sum_{i}") for i, (s1, rcv) in enumerate(zip(s1s, rcvs))]
        return _join_halves(halves, name="rs_join_halves")
SMALL = ("g_mix", "sinks", "a_log", "dt_bias", "gdn_norm_w", "g_mem", "g_mlp", "g_final")


def _rows128(a, rows):
    flat = a.reshape(-1)
    return jnp.pad(flat, (0, rows * LANES - flat.shape[0])).reshape(rows, LANES)


def kernel(x, mem, g_mix, w_in, sinks, conv_w, a_log, dt_bias, gdn_norm_w, g_mem, w_mem_kv, w_swa_up, w_gdn_up, w_xa_up, w_out, g_mlp, w_mlp_in, w_mlp_out, g_final, loss_target, m_g_mix, m_w_in, m_sinks, m_conv_w, m_a_log, m_dt_bias, m_gdn_norm_w, m_g_mem, m_w_mem_kv, m_w_swa_up, m_w_gdn_up, m_w_xa_up, m_w_out, m_g_mlp, m_w_mlp_in, m_w_mlp_out, m_g_final, v_g_mix, v_w_in, v_sinks, v_conv_w, v_a_log, v_dt_bias, v_gdn_norm_w, v_g_mem, v_w_mem_kv, v_w_swa_up, v_w_gdn_up, v_w_xa_up, v_w_out, v_g_mlp, v_w_mlp_in, v_w_mlp_out, v_g_final):
    given = dict(locals())
    xi, yi, ci = _place()
    chip = 2 * xi + yi
    core = jnp.reshape(ci, (1,)).astype(jnp.int32)
    where = jnp.stack([ci, chip]).astype(jnp.int32)

    shards = {k: given[k][0].astype(BF16) for k, _ in BIG}
    wts = {"w_cat": _to_cat(_all_gather_weights([shards.pop("w_in")], name="ag_w_in")[0])}
    comm = _Comm(shards, core, where)
    conv_shard = conv_w[0]
    conv_rows = -(-conv_shard.size // (8 * LANES)) * 8
    conv_all = _all_gather_small(_rows128(conv_shard, conv_rows), name="ag_conv")
    conv_full = jnp.concatenate(
        [conv_all[2 * s].reshape(-1)[:conv_shard.size].reshape(conv_shard.shape) for s in range(N_CHIPS)], axis=1)

    small = {k: given[k].reshape(1, -1) for k in SMALL}
    small["conv_w"] = conv_full
    loss_row, dx, grads = _local_step(x[0], mem[0], loss_target[0], wts, small, comm)
    big_grads = {k: grads[k] for k, _ in BIG}

    layout = [("loss", loss_row[:, :1])] + [(k, grads[k]) for k in SMALL] + [("conv_w", grads["conv_w"])]
    rows = [-(-a.size // LANES) for _, a in layout]
    blk_rows = -(-sum(rows) // 8) * 8
    blk = jnp.concatenate([_rows128(a.astype(F32), n) for (_, a), n in zip(layout, rows)]
                          + [jnp.zeros((blk_rows - sum(rows), LANES), F32)], axis=0)
    gathered = _all_gather_small(blk, name="ag_small_grads")
    reduced = _sum_rows([gathered[i] for i in range(N_DEV)], F32, name="small_grad_sum")
    small_grads, start = {}, 0
    for (k, a), n in zip(layout, rows):
        small_grads[k] = reduced[start:start + n].reshape(-1)[:a.size].reshape(a.shape)
        start += n
    loss = small_grads["loss"].reshape(())
    cw = conv_shard.shape[1]
    conv_grad = lax.dynamic_slice_in_dim(small_grads["conv_w"], chip * cw, cw, axis=1)

    names = ["g_mix", "w_in", "sinks", "conv_w", "a_log", "dt_bias", "gdn_norm_w", "g_mem", "w_mem_kv", "w_swa_up",
             "w_gdn_up", "w_xa_up", "w_out", "g_mlp", "w_mlp_in", "w_mlp_out", "g_final"]
    out_g, out_d, out_m, out_v = [], [], [], []
    for k in names:
        w, m, v = given[k], given["m_" + k], given["v_" + k]
        if k in big_grads:
            g2 = big_grads[k]
        elif k == "conv_w":
            g2 = conv_grad
        else:
            g2 = small_grads[k]
        as_given = (lambda a: a.reshape(1, -1)) if w.ndim == 1 else (lambda a: a)
        if w.shape[-1] % LANES and w.shape[-1] > LANES:
            tr = lambda a: jnp.swapaxes(a, -1, -2)
            delta, new_m, new_v = (tr(a) for a in _adamw(tr(w), tr(g2), tr(m), tr(v), name="adamw_" + k))
        else:
            delta, new_m, new_v = _adamw(as_given(w), g2, as_given(m), as_given(v), name="adamw_" + k)
        out_g.append(g2.reshape(w.shape))
        out_d.append(delta.reshape(w.shape))
        out_m.append(new_m.reshape(w.shape))
        out_v.append(new_v.reshape(w.shape))
    return (loss, dx[None], *out_g, *out_d, *out_m, *out_v)
```

```python
import functools
import math
from typing import Callable, NamedTuple

import jax
import jax.numpy as jnp
from jax import lax
from jax.experimental import pallas as pl
from jax.experimental.pallas import tpu as pltpu

F32 = jnp.float32
BF16 = jnp.bfloat16
HI = lax.Precision.HIGHEST
MESH = pl.DeviceIdType.MESH

SWA_Q_HEADS = 16
SWA_KV_HEADS = 2
SWA_HEAD_DIM = 64
SWA_WINDOW = 128
SWA_SCALE = SWA_HEAD_DIM ** -0.5
assert math.frexp(SWA_SCALE)[0] == 0.5
GDN_HEADS = 4
GDN_HEAD_DIM = 128
GDN_CONV = 4
GDN_CHUNK = 64
XA_HEADS = 4
XA_HEAD_DIM = 128
RMS_EPS = 1e-6
L2_EPS = 1e-6
ADAM_LR = 0.001
ADAM_B1 = 0.9
ADAM_B2 = 0.999
ADAM_EPS = 1e-08
ADAM_WD = 0.01
ADAM_STEP = 10

LANES = 128
N_SHARDS = 4
VMEM_LIMIT = 56 * 1024 * 1024

NT = (((1,), (1,)), ((), ()))
TN = (((0,), (0,)), ((), ()))
NN = (((1,), (0,)), ((), ()))


def _cp(sem=None):
    return pltpu.CompilerParams(dimension_semantics=sem, vmem_limit_bytes=VMEM_LIMIT)


def _blk(dim, pref):
    if dim <= pref:
        return dim
    b = (pref // LANES) * LANES
    while dim % b:
        b -= LANES
    assert b > 0, (dim, pref)
    return b


def _dot(a, b, dims=NN, precision=None):
    return lax.dot_general(a, b, dims, precision=precision, preferred_element_type=F32)


def _sigmoid(x):
    return 0.5 * jnp.tanh(0.5 * x) + 0.5


MM_TK_BYTES = 4096


def _mm(a, b, *, name, ta=False, tb=False, out_dtypes=(F32,), epilogue=None, extras=(), tm=1024, tn=1024, tk=None,
        b_sharded=False, out_sharded=False, b_window=None, exchange=None):
    (kdim, m) = a.shape if ta else a.shape[::-1]
    col0 = 0
    n_lim = k_lim = None
    if b_sharded:
        ns, rows_w, per = b.shape
        if tb:
            kb, n, k_lim = ns * per, rows_w, per
        else:
            kb, n, n_lim = rows_w, ns * per, per
    else:
        (kb, n) = b.shape[::-1] if tb else b.shape
        if b_window is not None:
            assert not tb
            col0, n = b_window
    assert kdim == kb, (a.shape, b.shape, ta, tb)
    if out_sharded:
        assert n % N_SHARDS == 0
        n_lim = n // N_SHARDS if n_lim is None else n_lim
        assert n_lim == n // N_SHARDS
    if tk is None:
        tk = MM_TK_BYTES // max(a.dtype.itemsize, b.dtype.itemsize)
    tm, tn, tk = _blk(m, tm), _blk(n_lim or n, tn), _blk(k_lim or kdim, tk)
    assert col0 % tn == 0, (col0, tn)
    nk = kdim // tk
    a_spec = pl.BlockSpec((tk, tm), lambda i, j, k: (k, i)) if ta else pl.BlockSpec((tm, tk), lambda i, j, k: (i, k))
    if b_sharded and tb:
        kpb = k_lim // tk
        b_spec = pl.BlockSpec((None, tn, tk), lambda i, j, k: (k // kpb, j, k % kpb))
    elif b_sharded:
        bpb = n_lim // tn
        b_spec = pl.BlockSpec((None, tk, tn), lambda i, j, k: (j // bpb, k, j % bpb))
    elif tb:
        b_spec = pl.BlockSpec((tn, tk), lambda i, j, k: (j, k))
    else:
        b_spec = pl.BlockSpec((tk, tn), lambda i, j, k: (k, j + col0 // tn))
    x_spec = pl.BlockSpec((tm, tn), lambda i, j, k: (i, j))
    if out_sharded:
        opb = n_lim // tn
        o_spec = pl.BlockSpec((None, tm, tn), lambda i, j, k: (j // opb, i, j % opb))
        out_shape = (N_SHARDS, m, n_lim)
    else:
        o_spec, out_shape = x_spec, (m, n)
    dims = ((((0 if ta else 1),), ((1 if tb else 0),)), ((), ()))
    n_extra, n_out = len(extras), len(out_dtypes)

    host = _ExchangeHost(exchange)
    grid = (m // tm, n // tn, nk)

    def body(*refs):
        a_ref, b_ref = refs[:2]
        extra_refs = refs[2:2 + n_extra]
        out_refs = refs[2 + n_extra + host.n_in:2 + n_extra + host.n_in + n_out]
        host.start(refs, 2 + n_extra, 2 + n_extra + host.n_in + n_out, grid)
        part = _dot(a_ref[...].astype(BF16), b_ref[...].astype(BF16), dims)

        def finish(acc):
            vals = epilogue(acc, *[r[...] for r in extra_refs]) if epilogue is not None else (acc,) * n_out
            assert len(vals) == n_out
            for r, v in zip(out_refs, vals):
                r[...] = v.astype(r.dtype)

        if nk == 1:
            finish(part)
        else:
            acc_ref = refs[2 + n_extra + host.n_in + n_out + host.n_out]
            k = pl.program_id(2)

            @pl.when(k == 0)
            def _():
                acc_ref[...] = part

            @pl.when((k > 0) & (k < nk - 1))
            def _():
                acc_ref[...] += part

            @pl.when(k == nk - 1)
            def _():
                finish(acc_ref[...] + part)

        host.wait(refs, 2 + n_extra, 2 + n_extra + host.n_in + n_out, grid)

    outs = pl.pallas_call(
        body,
        grid=grid,
        in_specs=[a_spec, b_spec] + [x_spec] * n_extra + host.in_specs,
        out_specs=[o_spec] * n_out + host.out_specs,
        out_shape=[jax.ShapeDtypeStruct(out_shape, d) for d in out_dtypes] + host.out_shapes,
        scratch_shapes=([pltpu.VMEM((tm, tn), F32)] if nk > 1 else []) + host.scratch,
        input_output_aliases=host.aliases(2 + n_extra, n_out),
        compiler_params=_cp(host.semantics(("parallel", "parallel", "arbitrary"))),
        name=name,
    )(a, b, *extras, *host.ins)
    mine, landed = outs[:n_out], list(outs[n_out:])
    mine = mine[0] if n_out == 1 else mine
    return (mine, landed) if exchange is not None else mine


class Exchange(NamedTuple):
    ins: tuple
    out_shapes: tuple
    n_remote: int
    n_local: int
    copies_of: Callable
    aliases: tuple = ()


def _exchange_copies(ex, in_refs, out_refs, sem_refs):
    send_sems, recv_sems, local_sems = sem_refs
    remote, local = ex.copies_of(in_refs, out_refs, _place())
    assert len(remote) == ex.n_remote and len(local) == ex.n_local, (len(remote), len(local))
    cps = [pltpu.make_async_remote_copy(src_ref=src, dst_ref=dst, send_sem=send_sems.at[k], recv_sem=recv_sems.at[k],
                                        device_id=to, device_id_type=MESH) for k, (src, dst, to) in enumerate(remote)]
    cps += [pltpu.make_async_copy(src, dst, local_sems.at[k]) for k, (src, dst) in enumerate(local)]
    return cps


class _ExchangeHost:
    def __init__(self, ex):
        self.ex = ex
        self.ins = list(ex.ins) if ex else []
        self.out_shapes = list(ex.out_shapes) if ex else []
        self.n_in, self.n_out = len(self.ins), len(self.out_shapes)
        self.in_specs = [HBM_SPEC] * self.n_in
        self.out_specs = [HBM_SPEC] * self.n_out
        self.scratch = _sem_scratch(ex.n_remote, ex.n_local) if ex else []

    def semantics(self, sem):
        return tuple("arbitrary" for _ in sem) if self.ex else sem

    def aliases(self, in_at, out_at):
        return {in_at + i: out_at + o for i, o in self.ex.aliases} if self.ex else {}

    def _refs(self, refs, in_at, out_at):
        return refs[in_at:in_at + self.n_in], refs[out_at:out_at + self.n_out], refs[len(refs) - 3:]

    def _when(self, grid, last):
        cond = None
        for d, size in enumerate(grid):
            c = pl.program_id(d) == (size - 1 if last else 0)
            cond = c if cond is None else cond & c
        return cond

    def start(self, refs, in_at, out_at, grid):
        if self.ex:
            @pl.when(self._when(grid, False))
            def _():
                for cp in _exchange_copies(self.ex, *self._refs(refs, in_at, out_at)):
                    cp.start()

    def wait(self, refs, in_at, out_at, grid):
        if self.ex:
            @pl.when(self._when(grid, True))
            def _():
                for cp in _exchange_copies(self.ex, *self._refs(refs, in_at, out_at)):
                    cp.wait()


def _rms_fwd(x, g, *, name, tm=512, exchange=None):
    t, d = x.shape
    tm = _blk(t, tm)
    host = _ExchangeHost(exchange)
    grid = (t // tm,)

    def body(*refs):
        x_ref, g_ref, n_ref = refs[0], refs[1], refs[2 + host.n_in]
        host.start(refs, 2, 3 + host.n_in, grid)
        xv = x_ref[...]
        r = lax.rsqrt(jnp.mean(xv * xv, axis=-1, keepdims=True) + RMS_EPS)
        n_ref[...] = (xv * r * g_ref[...]).astype(n_ref.dtype)
        host.wait(refs, 2, 3 + host.n_in, grid)

    outs = pl.pallas_call(
        body, grid=grid,
        in_specs=[pl.BlockSpec((tm, d), lambda i: (i, 0)), pl.BlockSpec((1, d), lambda i: (0, 0))] + host.in_specs,
        out_specs=[pl.BlockSpec((tm, d), lambda i: (i, 0))] + host.out_specs,
        out_shape=[jax.ShapeDtypeStruct((t, d), BF16)] + host.out_shapes,
        scratch_shapes=host.scratch, input_output_aliases=host.aliases(2, 1),
        compiler_params=_cp(host.semantics(("parallel",))), name=name,
    )(x, g, *host.ins)
    return (outs[0], list(outs[1:])) if exchange is not None else outs[0]


def _rms_bwd(dn, x, g, dres, *, name, tm=512, exchange=None):
    t, d = x.shape
    tm = _blk(t, tm)
    host = _ExchangeHost(exchange)
    grid = (t // tm,)

    def body(*refs):
        dn_ref, x_ref, g_ref, dres_ref = refs[:4]
        dx_ref, dxb_ref, dg_ref = refs[4 + host.n_in:7 + host.n_in]
        host.start(refs, 4, 7 + host.n_in, grid)
        i = pl.program_id(0)
        xv = x_ref[...]
        r = lax.rsqrt(jnp.mean(xv * xv, axis=-1, keepdims=True) + RMS_EPS)
        xh = xv * r
        dnv = dn_ref[...].astype(F32)
        dxh = dnv * g_ref[...]
        dx = dres_ref[...] + r * (dxh - xh * jnp.mean(dxh * xh, axis=-1, keepdims=True))
        dx_ref[...] = dx
        dxb_ref[...] = dx.astype(dxb_ref.dtype)
        part = jnp.sum(dnv * xh, axis=0, keepdims=True)

        @pl.when(i == 0)
        def _():
            dg_ref[...] = part

        @pl.when(i > 0)
        def _():
            dg_ref[...] += part

        host.wait(refs, 4, 7 + host.n_in, grid)

    row = pl.BlockSpec((tm, d), lambda i: (i, 0))
    vec = pl.BlockSpec((1, d), lambda i: (0, 0))
    outs = pl.pallas_call(
        body, grid=grid,
        in_specs=[row, row, vec, row] + host.in_specs, out_specs=[row, row, vec] + host.out_specs,
        out_shape=[jax.ShapeDtypeStruct((t, d), F32), jax.ShapeDtypeStruct((t, d), BF16),
                   jax.ShapeDtypeStruct((1, d), F32)] + host.out_shapes,
        scratch_shapes=host.scratch, input_output_aliases=host.aliases(4, 3),
        compiler_params=_cp(("arbitrary",)), name=name,
    )(dn, x, g, dres, *host.ins)
    return (*outs[:3], list(outs[3:])) if exchange is not None else outs


def _final_loss(h, g, tgt, *, name, tm=512):
    t, d = h.shape
    tm = _blk(t, tm)

    def body(h_ref, g_ref, t_ref, dh_ref, dhb_ref, dg_ref, loss_ref):
        i = pl.program_id(0)
        hv = h_ref[...]
        r = lax.rsqrt(jnp.mean(hv * hv, axis=-1, keepdims=True) + RMS_EPS)
        xh = hv * r
        e = xh * g_ref[...] - t_ref[...]
        dy = e * (1.0 / d)
        dxh = dy * g_ref[...]
        dh = r * (dxh - xh * jnp.mean(dxh * xh, axis=-1, keepdims=True))
        dh_ref[...] = dh
        dhb_ref[...] = dh.astype(dhb_ref.dtype)
        dg_part = jnp.sum(dy * xh, axis=0, keepdims=True)
        row_loss = jnp.sum(e * e, axis=-1, keepdims=True) * (0.5 / d)
        loss_part = jnp.sum(row_loss, axis=0, keepdims=True)

        @pl.when(i == 0)
        def _():
            dg_ref[...] = dg_part
            loss_ref[...] = jnp.broadcast_to(loss_part, loss_ref.shape)

        @pl.when(i > 0)
        def _():
            dg_ref[...] += dg_part
            loss_ref[...] += jnp.broadcast_to(loss_part, loss_ref.shape)

    row = pl.BlockSpec((tm, d), lambda i: (i, 0))
    vec = pl.BlockSpec((1, d), lambda i: (0, 0))
    return pl.pallas_call(
        body, grid=(t // tm,),
        in_specs=[row, vec, row], out_specs=[row, row, vec, pl.BlockSpec((1, LANES), lambda i: (0, 0))],
        out_shape=[jax.ShapeDtypeStruct((t, d), F32), jax.ShapeDtypeStruct((t, d), BF16),
                   jax.ShapeDtypeStruct((1, d), F32), jax.ShapeDtypeStruct((1, LANES), F32)],
        compiler_params=_cp(("arbitrary",)), name=name,
    )(h, g, tgt)


SWA_SUB = 64


def _swa_mask(n, rows, row0):
    w = SWA_WINDOW
    qi = (lax.broadcasted_iota(jnp.int32, (rows, 2 * w), 0) + row0) & (w - 1)
    kj = lax.broadcasted_iota(jnp.int32, (rows, 2 * w), 1)
    return (kj > qi) & (kj <= qi + w) & ((n > 0) | (kj >= w))


def _stack_heads(ref, heads, width):
    return jnp.concatenate([ref[:, h * width:(h + 1) * width] for h in heads], axis=0)


def _stack_scalars(ref, heads, rows):
    return jnp.concatenate([jnp.broadcast_to(ref[0:1, h:h + 1], (rows, 1)) for h in heads], axis=0)


def _swa_fwd(q, kv, sinks, *, name, exchange=None):
    t = q.shape[0]
    w, hd, hq, hkv = SWA_WINDOW, SWA_HEAD_DIM, SWA_Q_HEADS, SWA_KV_HEADS
    grp = hq // hkv
    kvw = hkv * hd
    nb = t // w
    host = _ExchangeHost(exchange)
    assert not (exchange and exchange.aliases)

    def body(*refs):
        q_ref, kvp_ref, kvc_ref, s_ref = refs[:4]
        o_ref, lse_ref = refs[4 + host.n_in:6 + host.n_in]
        host.start(refs, 4, 6 + host.n_in, (nb,))
        n = pl.program_id(0)
        mask = _swa_mask(n, grp * w, 0)
        kvcat = jnp.concatenate([kvp_ref[...], kvc_ref[...]], axis=0)
        outs, lses = [], []
        for hk in range(hkv):
            heads = range(hk * grp, (hk + 1) * grp)
            qs = _stack_heads(q_ref, heads, hd)
            kh = kvcat[:, hk * hd:(hk + 1) * hd]
            vh = kvcat[:, kvw + hk * hd:kvw + (hk + 1) * hd]
            sk = _stack_scalars(s_ref, heads, w)
            s = jnp.where(mask, _dot(qs * SWA_SCALE, kh, NT), -jnp.inf)
            m = jnp.maximum(jnp.max(s, axis=-1, keepdims=True), sk)
            p = jnp.exp(s - m)
            den = jnp.sum(p, axis=-1, keepdims=True) + jnp.exp(sk - m)
            o = _dot((p * (1.0 / den)).astype(BF16), vh)
            lse = m + jnp.log(den)
            outs += [o[j * w:(j + 1) * w] for j in range(grp)]
            lses += [lse[j * w:(j + 1) * w] for j in range(grp)]
        o_ref[...] = jnp.concatenate(outs, axis=1).astype(o_ref.dtype)
        lse_ref[...] = jnp.concatenate(lses, axis=1)
        host.wait(refs, 4, 6 + host.n_in, (nb,))

    outs = pl.pallas_call(
        body, grid=(nb,),
        in_specs=[pl.BlockSpec((w, hq * hd), lambda i: (i, 0)),
                  pl.BlockSpec((w, 2 * kvw), lambda i: (jnp.maximum(i - 1, 0), 0)),
                  pl.BlockSpec((w, 2 * kvw), lambda i: (i, 0)),
                  pl.BlockSpec((1, hq), lambda i: (0, 0))] + host.in_specs,
        out_specs=[pl.BlockSpec((w, hq * hd), lambda i: (i, 0)), pl.BlockSpec((w, hq), lambda i: (i, 0))] + host.out_specs,
        out_shape=[jax.ShapeDtypeStruct((t, hq * hd), BF16), jax.ShapeDtypeStruct((t, hq), F32)] + host.out_shapes,
        scratch_shapes=host.scratch,
        compiler_params=_cp(host.semantics(("parallel",))), name=name,
    )(q, kv, kv, sinks, *host.ins)
    return (outs[0], outs[1], list(outs[2:])) if exchange is not None else outs


ANY_SPEC = pl.BlockSpec(memory_space=pl.ANY)


def _swa_bwd(q, kv, sinks, o, lse, do, dp, dp_col, *, name):
    t = q.shape[0]
    w, hd, hq, hkv = SWA_WINDOW, SWA_HEAD_DIM, SWA_Q_HEADS, SWA_KV_HEADS
    grp = hq // hkv
    kvw = hkv * hd
    nb = t // w
    assert dp_col % (hq * hd) == 0
    dq_blk = dp_col // (hq * hd)

    def body(q_ref, kvp_ref, kvc_ref, s_ref, o_ref, lse_ref, do_ref, _, dq_ref, dkv_ref, ds_ref, carry_ref, s_scr, dp_scr,
             p_scr, ds_scr):
        n = pl.program_id(0)

        @pl.when(n == 0)
        def _():
            ds_ref[...] = jnp.zeros_like(ds_ref)
            carry_ref[...] = jnp.zeros_like(carry_ref)

        @pl.when(n < nb)
        def _():
            kvcat = jnp.concatenate([kvp_ref[...], kvc_ref[...]], axis=0)
            dqs, dsk, dks, dvs = [], [], [], []
            for hk in range(hkv):
                heads = range(hk * grp, (hk + 1) * grp)
                qs = _stack_heads(q_ref, heads, hd)
                dos = _stack_heads(do_ref, heads, hd)
                os_ = _stack_heads(o_ref, heads, hd)
                lse = _stack_heads(lse_ref, heads, 1)
                kh = kvcat[:, hk * hd:(hk + 1) * hd]
                vh = kvcat[:, kvw + hk * hd:kvw + (hk + 1) * hd]
                delta = jnp.sum(dos.astype(F32) * os_.astype(F32), axis=-1, keepdims=True)
                s_scr[...] = _dot(qs * SWA_SCALE, kh, NT)
                dp_scr[...] = _dot(dos, vh, NT)
                for r0 in range(0, grp * w, SWA_SUB):
                    rows = slice(r0, r0 + SWA_SUB)
                    p = jnp.exp(jnp.where(_swa_mask(n, SWA_SUB, r0 % w), s_scr[rows, :], -jnp.inf) - lse[rows])
                    p_scr[rows, :] = p.astype(p_scr.dtype)
                    ds_scr[rows, :] = (p * (dp_scr[rows, :] - delta[rows]) * SWA_SCALE).astype(ds_scr.dtype)
                ds = ds_scr[...]
                dq = _dot(ds, kh)
                dqs += [dq[j * w:(j + 1) * w] for j in range(grp)]
                dks.append(_dot(ds, qs, TN))
                dvs.append(_dot(p_scr[...], dos, TN))
                dsink = -jnp.exp(_stack_scalars(s_ref, heads, w) - lse) * delta
                dsk += [jnp.sum(dsink[j * w:(j + 1) * w], axis=0, keepdims=True) for j in range(grp)]
            dq_ref[...] = jnp.concatenate(dqs, axis=1).astype(dq_ref.dtype)
            ds_ref[...] += jnp.concatenate(dsk, axis=1)
            dkv_cat = jnp.concatenate(dks + dvs, axis=1)
            dkv_ref[...] = (carry_ref[...] + dkv_cat[:w]).astype(dkv_ref.dtype)
            carry_ref[...] = dkv_cat[w:]

        @pl.when(n == nb)
        def _():
            dkv_ref[...] = carry_ref[...].astype(dkv_ref.dtype)

    cur = lambda i: (jnp.minimum(i, nb - 1), 0)
    prev = lambda i: (jnp.clip(i - 1, 0, nb - 1), 0)
    return pl.pallas_call(
        body, grid=(nb + 1,),
        in_specs=[pl.BlockSpec((w, hq * hd), cur), pl.BlockSpec((w, 2 * kvw), prev), pl.BlockSpec((w, 2 * kvw), cur),
                  pl.BlockSpec((1, hq), lambda i: (0, 0)), pl.BlockSpec((w, hq * hd), cur),
                  pl.BlockSpec((w, hq), cur), pl.BlockSpec((w, hq * hd), cur), ANY_SPEC],
        out_specs=[pl.BlockSpec((w, hq * hd), lambda i: (jnp.minimum(i, nb - 1), dq_blk)),
                   pl.BlockSpec((w, 2 * kvw), prev), pl.BlockSpec((1, hq), lambda i: (0, 0))],
        out_shape=[jax.ShapeDtypeStruct(dp.shape, dp.dtype), jax.ShapeDtypeStruct((t, 2 * kvw), BF16),
                   jax.ShapeDtypeStruct((1, hq), F32)],
        scratch_shapes=[pltpu.VMEM((w, 2 * kvw), F32)] + [pltpu.VMEM((grp * w, 2 * w), dt) for dt in (F32, F32, BF16, BF16)],
        input_output_aliases={7: 0},
        compiler_params=_cp(("arbitrary",)), name=name,
    )(q, kv, kv, sinks, o, lse, do, dp)


def _xa_fwd(q, mkv, *, name, tq=512):
    t, xw = q.shape
    nm = mkv.shape[0]
    hd, nh = XA_HEAD_DIM, XA_HEADS
    tq = _blk(t, tq)

    def body(q_ref, mkv_ref, o_ref):
        outs = []
        for h in range(nh):
            qh = q_ref[:, h * hd:(h + 1) * hd]
            kh = mkv_ref[:, h * hd:(h + 1) * hd]
            vh = mkv_ref[:, xw + h * hd:xw + (h + 1) * hd]
            s = _dot(qh, kh, NT) * (hd ** -0.5)
            p = jnp.exp(s - jnp.max(s, axis=-1, keepdims=True))
            p = p * (1.0 / jnp.sum(p, axis=-1, keepdims=True))
            outs.append(_dot(p.astype(BF16), vh))
        o_ref[...] = jnp.concatenate(outs, axis=1).astype(o_ref.dtype)

    return pl.pallas_call(
        body, grid=(t // tq,),
        in_specs=[pl.BlockSpec((tq, xw), lambda i: (i, 0)), pl.BlockSpec((nm, 2 * xw), lambda i: (0, 0))],
        out_specs=pl.BlockSpec((tq, xw), lambda i: (i, 0)),
        out_shape=jax.ShapeDtypeStruct((t, xw), BF16),
        compiler_params=_cp(("parallel",)), name=name,
    )(q, mkv)


def _xa_bwd(q, mkv, do, dp, dp_col, *, name, tq=512):
    t, xw = q.shape
    nm = mkv.shape[0]
    hd, nh = XA_HEAD_DIM, XA_HEADS
    tq = _blk(t, tq)
    assert dp_col % xw == 0

    def body(q_ref, mkv_ref, do_ref, _, dq_ref, dmkv_ref):
        i = pl.program_id(0)
        dqs, dks, dvs = [], [], []
        for h in range(nh):
            qh = q_ref[:, h * hd:(h + 1) * hd]
            kh = mkv_ref[:, h * hd:(h + 1) * hd]
            vh = mkv_ref[:, xw + h * hd:xw + (h + 1) * hd]
            doh = do_ref[:, h * hd:(h + 1) * hd]
            s = _dot(qh, kh, NT) * (hd ** -0.5)
            p = jnp.exp(s - jnp.max(s, axis=-1, keepdims=True))
            p = p * (1.0 / jnp.sum(p, axis=-1, keepdims=True))
            dp = _dot(doh, vh, NT)
            ds = (p * (dp - jnp.sum(p * dp, axis=-1, keepdims=True)) * (hd ** -0.5)).astype(BF16)
            dqs.append(_dot(ds, kh))
            dks.append(_dot(ds, qh, TN))
            dvs.append(_dot(p.astype(BF16), doh, TN))
        dq_ref[...] = jnp.concatenate(dqs, axis=1).astype(dq_ref.dtype)
        part = jnp.concatenate(dks + dvs, axis=1)

        @pl.when(i == 0)
        def _():
            dmkv_ref[...] = part

        @pl.when(i > 0)
        def _():
            dmkv_ref[...] += part

    row = pl.BlockSpec((tq, xw), lambda i: (i, 0))
    full = pl.BlockSpec((nm, 2 * xw), lambda i: (0, 0))
    return pl.pallas_call(
        body, grid=(t // tq,),
        in_specs=[row, full, row, ANY_SPEC],
        out_specs=[pl.BlockSpec((tq, xw), lambda i: (i, dp_col // xw)), full],
        out_shape=[jax.ShapeDtypeStruct(dp.shape, dp.dtype), jax.ShapeDtypeStruct((nm, 2 * xw), F32)],
        input_output_aliases={3: 0}, compiler_params=_cp(("arbitrary",)), name=name,
    )(q, mkv, do, dp)


def _merge_specs(ys, ws, tm):
    y_specs = [pl.BlockSpec((tm, y.shape[1]), lambda i: (i, 0)) for y in ys]
    w_specs = [pl.BlockSpec(w.shape, lambda i: (0, 0, 0)) for w in ws]
    return y_specs, w_specs


def _merge_tiles(ws, tn):
    ns, _, per = ws[0].shape
    tn = _blk(per, tn)
    return tn, [(s, c, s * per + c) for s in range(ns) for c in range(0, per, tn)]


def _merge_fwd(ys, ws, gates, *, name, tm=256, tn=512):
    t, d = ys[0].shape[0], ws[0].shape[0] * ws[0].shape[2]
    tm = _blk(t, tm)
    tn, tiles = _merge_tiles(ws, tn)
    y_specs, w_specs = _merge_specs(ys, ws, tm)

    def body(ya, yb, yc, wa, wb, wc, g_ref, o_ref):
        for s, c, col in tiles:
            acc = None
            for b, (y, w) in enumerate(((ya, wa), (yb, wb), (yc, wc))):
                term = _sigmoid(g_ref[:, b * d + col:b * d + col + tn]) * _dot(y[...], w[s, :, c:c + tn])
                acc = term if acc is None else acc + term
            o_ref[:, col:col + tn] = acc.astype(o_ref.dtype)

    return pl.pallas_call(
        body, grid=(t // tm,),
        in_specs=y_specs + w_specs + [pl.BlockSpec((tm, 3 * d), lambda i: (i, 0))],
        out_specs=pl.BlockSpec((tm, d), lambda i: (i, 0)),
        out_shape=jax.ShapeDtypeStruct((t, d), BF16),
        compiler_params=_cp(("parallel",)), name=name,
    )(*ys, *ws, gates)


def _merge_bwd(ys, ws, gates, dmerged, dp_width, *, name, tm=256, tn=512):
    t, d = ys[0].shape[0], ws[0].shape[0] * ws[0].shape[2]
    tm = _blk(t, tm)
    tn, tiles = _merge_tiles(ws, tn)
    y_specs, w_specs = _merge_specs(ys, ws, tm)
    row = pl.BlockSpec((tm, d), lambda i: (i, 0))
    wide = pl.BlockSpec((tm, 3 * d), lambda i: (i, 0))

    def body(ya, yb, yc, wa, wb, wc, g_ref, dm_ref, dua, dub, duc, dp_ref):
        for s, c, col in tiles:
            dm = dm_ref[:, col:col + tn]
            for b, (y, w, du) in enumerate(((ya, wa, dua), (yb, wb, dub), (yc, wc, duc))):
                sg = _sigmoid(g_ref[:, b * d + col:b * d + col + tn])
                u = _dot(y[...], w[s, :, c:c + tn])
                du[:, col:col + tn] = (dm * sg).astype(du.dtype)
                dp_ref[:, b * d + col:b * d + col + tn] = (dm * u * sg * (1.0 - sg)).astype(dp_ref.dtype)

    return pl.pallas_call(
        body, grid=(t // tm,),
        in_specs=y_specs + w_specs + [wide, row],
        out_specs=[row] * 3 + [wide],
        out_shape=[jax.ShapeDtypeStruct((t, d), BF16)] * 3 + [jax.ShapeDtypeStruct((t, dp_width), BF16)],
        compiler_params=_cp(("parallel",)), name=name,
    )(*ys, *ws, gates, dmerged)


def _adamw(w, g, m, v, *, name, tm=256):
    lead = w.ndim - 2
    assert all(s == 1 for s in w.shape[:lead]) and m.shape == w.shape and v.shape == w.shape
    r, c = w.shape[lead:]
    assert g.shape == (r, c)
    tm = _blk(r, tm) if r % 8 == 0 else r
    tc = c if tm * c * 4 <= (4 << 20) else _blk(c, 256)
    ncb = c // tc
    bc1 = 1.0 - ADAM_B1 ** ADAM_STEP
    bc2 = 1.0 - ADAM_B2 ** ADAM_STEP

    def body(w_ref, g_ref, m_ref, v_ref, d_ref, nm_ref, nv_ref):
        gv = g_ref[...]
        nm = ADAM_B1 * m_ref[...] + (1.0 - ADAM_B1) * gv
        nv = ADAM_B2 * v_ref[...] + (1.0 - ADAM_B2) * (gv * gv)
        d_ref[...] = -ADAM_LR * ((nm / bc1) / (jnp.sqrt(nv / bc2) + ADAM_EPS) + ADAM_WD * w_ref[...])
        nm_ref[...] = nm
        nv_ref[...] = nv

    spec = pl.BlockSpec((None,) * lead + (tm, tc), lambda i: (0,) * lead + (i // ncb, i % ncb))
    g_spec = pl.BlockSpec((tm, tc), lambda i: (i // ncb, i % ncb))
    return pl.pallas_call(
        body, grid=(r // tm * ncb,), in_specs=[spec, g_spec, spec, spec], out_specs=[spec] * 3,
        out_shape=[jax.ShapeDtypeStruct(w.shape, F32)] * 3,
        compiler_params=_cp(("parallel",)), name=name,
    )(w, g, m, v)


HALO = 8


def _shift_down(cur, prev, j):
    if j == 0:
        return cur
    y = pltpu.roll(cur, j, 0)
    row = lax.broadcasted_iota(jnp.int32, (HALO, cur.shape[1]), 0)
    top = jnp.where(row < j, pltpu.roll(prev, j, 0), y[:HALO])
    return jnp.concatenate([top, y[HALO:]], axis=0)


def _shift_up(cur, nxt, j):
    if j == 0:
        return cur
    tm = cur.shape[0]
    y = pltpu.roll(cur, tm - j, 0)
    row = lax.broadcasted_iota(jnp.int32, (HALO, cur.shape[1]), 0)
    bot = jnp.where(row >= HALO - j, pltpu.roll(nxt, HALO - j, 0), y[tm - HALO:])
    return jnp.concatenate([y[:tm - HALO], bot], axis=0)


def _softplus(x):
    return jnp.maximum(x, 0.0) + jnp.log(1.0 + jnp.exp(-jnp.abs(x)))


def _gdn_pre_fwd(qkvb, conv_w, ab, alog_pad, dt_pad, *, name, ab_blk=0, tm=256):
    t, cw = qkvb.shape
    hd, nh, ck = GDN_HEAD_DIM, GDN_HEADS, GDN_CHUNK
    gw = nh * hd
    tm = _blk(t, tm)
    hb = tm // HALO

    def body(x_ref, xp_ref, w_ref, ab_ref, al_ref, dt_ref, xc_ref, qkvn_ref, aux_ref):
        i = pl.program_id(0)
        cur = x_ref[...]
        prev = jnp.where(i > 0, xp_ref[...], 0.0)
        xc = None
        for tap in range(GDN_CONV):
            term = w_ref[tap:tap + 1, :] * _shift_down(cur, prev, GDN_CONV - 1 - tap)
            xc = term if xc is None else xc + term
        xc_ref[...] = xc
        s = xc * _sigmoid(xc)
        for h in range(2 * nh):
            xh = s[:, h * hd:(h + 1) * hd]
            r = lax.rsqrt(jnp.sum(xh * xh, axis=-1, keepdims=True) + L2_EPS)
            scale = hd ** -0.5 if h < nh else 1.0
            qkvn_ref[:, h * hd:(h + 1) * hd] = xh * (r * scale)
        qkvn_ref[:, 2 * gw:] = s[:, 2 * gw:]
        abv = ab_ref[...]
        lane = lax.broadcasted_iota(jnp.int32, abv.shape, 1)
        g = jnp.where(lane < nh, -jnp.exp(al_ref[...]) * _softplus(abv + dt_ref[...]), 0.0)
        beta = jnp.where((lane >= nh) & (lane < 2 * nh), _sigmoid(abv), 0.0)
        ii = lax.broadcasted_iota(jnp.int32, (tm, tm), 0)
        jj = lax.broadcasted_iota(jnp.int32, (tm, tm), 1)
        tri = jnp.where((ii >= jj) & ((ii ^ jj) < ck), 1.0, 0.0)
        gcum = _dot(tri, g, precision=HI)
        aux_ref[...] = g + beta + pltpu.roll(gcum, 2 * nh, 1)

    row = lambda c: pl.BlockSpec((tm, c), lambda i: (i, 0))
    vec = lambda r, c: pl.BlockSpec((r, c), lambda i: (0, 0))
    return pl.pallas_call(
        body, grid=(t // tm,),
        in_specs=[row(cw), pl.BlockSpec((HALO, cw), lambda i: (jnp.maximum(i * hb - 1, 0), 0)), vec(GDN_CONV, cw),
                  pl.BlockSpec((tm, LANES), lambda i: (i, ab_blk)), vec(1, LANES), vec(1, LANES)],
        out_specs=[row(cw), row(cw), row(LANES)],
        out_shape=[jax.ShapeDtypeStruct((t, cw), F32), jax.ShapeDtypeStruct((t, cw), F32),
                   jax.ShapeDtypeStruct((t, LANES), F32)],
        compiler_params=_cp(("parallel",)), name=name,
    )(qkvb, qkvb, conv_w, ab, alog_pad, dt_pad)


GDN_STEP_CHUNKS = 4
GDN_ILP_CHUNKS = 4
GDN_ILP_CHUNKS_BWD = 4


def _bdot(a, b, dims=NN):
    return _dot(a.astype(BF16), b.astype(BF16), dims)


def _split_bf16(x):
    hi = x.astype(BF16)
    return hi, (x - hi.astype(F32)).astype(BF16)


def _dot3(a, b, dims=NN):
    ah, al = _split_bf16(a)
    bh, bl = _split_bf16(b)
    return _dot(ah, bh, dims) + (_dot(ah, bl, dims) + _dot(al, bh, dims))


def _dot3_many(lhs, rhs, dims=NN):
    sa = [_split_bf16(a) for a in lhs]
    sb = [_split_bf16(b) for b in rhs]
    hh = [_dot(a[0], b[0], dims) for a, b in zip(sa, sb)]
    hl = [_dot(a[0], b[1], dims) for a, b in zip(sa, sb)]
    lh = [_dot(a[1], b[0], dims) for a, b in zip(sa, sb)]
    return [x + (y + z) for x, y, z in zip(hh, hl, lh)]


def _gdn_local(chains, with_inverse):
    ck = GDN_CHUNK
    ii = lax.broadcasted_iota(jnp.int32, (ck, ck), 0)
    jj = lax.broadcasted_iota(jnp.int32, (ck, ck), 1)
    lower, strict = ii >= jj, ii > jj
    dmat = [jnp.exp(jnp.where(lower, gc - gc_row, -jnp.inf)) for _, _, _, gc, gc_row in chains]
    kk = [_bdot(k, k, NT) for _, k, _, _, _ in chains]
    qk = [_bdot(q, k, NT) for q, k, _, _, _ in chains]
    tinv = [None] * len(chains)
    if with_inverse:
        lmat = [jnp.where(strict, c[2] * kk_i * d_i, 0.0) for c, kk_i, d_i in zip(chains, kk, dmat)]
        eye = jnp.where(ii == jj, 1.0, 0.0)
        tinv = [eye - l_i for l_i in lmat]
        pw = lmat
        for _ in range(int(math.log2(ck)) - 1):
            pw = _dot3_many(pw, pw)
            tinv = [t_i + d_i for t_i, d_i in zip(tinv, _dot3_many(tinv, pw))]
    out = []
    for (q, k, b, gc, gc_row), dmat_i, kk_i, qk_i, tinv_i in zip(chains, dmat, kk, qk, tinv):
        gl = gc[ck - 1:ck, :]
        out.append(dict(lower=lower, strict=strict, dmat=dmat_i, kk=kk_i, tinv=tinv_i, gam=jnp.exp(gc), qk=qk_i,
                        mm=qk_i * dmat_i, kdec=jnp.exp(gl - gc)))
    return out


def _gdn_head_cols(h):
    return slice(h * GDN_HEAD_DIM, (h + 1) * GDN_HEAD_DIM)


def _gdn_chunk_inputs(x_ref, aux_ref, auxt_ref, g, h):
    nh, ck = GDN_HEADS, GDN_CHUNK
    gw = nh * GDN_HEAD_DIM
    rows = slice(g * ck, (g + 1) * ck)
    cols = _gdn_head_cols(h)
    q = x_ref[rows, cols]
    k = x_ref[rows, gw + cols.start:gw + cols.stop]
    v = x_ref[rows, 2 * gw + cols.start:2 * gw + cols.stop]
    b = aux_ref[rows, nh + h:nh + h + 1]
    gc = aux_ref[rows, 2 * nh + h:2 * nh + h + 1]
    gc_row = auxt_ref[g, 2 * nh + h:2 * nh + h + 1, :]
    return q, k, v, b, gc, gc_row


def _gdn_specs(t, widths, *, reverse=False, step_chunks=None):
    rows = (step_chunks or GDN_STEP_CHUNKS) * GDN_CHUNK
    nsteps = t // rows
    idx = (lambda i: (nsteps - 1 - i, 0)) if reverse else (lambda i: (i, 0))
    return [pl.BlockSpec((rows, w), idx) for w in widths]


def _gdn_local_fwd(qkvn, aux, aux_t, *, name, exchange=None):
    t = qkvn.shape[0]
    hd, nh, ck, gs = GDN_HEAD_DIM, GDN_HEADS, GDN_CHUNK, GDN_STEP_CHUNKS
    gw = nh * hd
    host = _ExchangeHost(exchange)
    assert not (exchange and exchange.aliases)
    grid = (t // (gs * ck),)

    def body(*refs):
        x_ref, aux_ref, auxt_ref = refs[:3]
        u_ref, w_ref, qd_ref, kd_ref, mm_ref, tinv_ref = refs[3 + host.n_in:9 + host.n_in]
        host.start(refs, 3, 9 + host.n_in, grid)
        for g0 in range(0, gs, GDN_ILP_CHUNKS):
            where = [(g, h) for g in range(g0, g0 + GDN_ILP_CHUNKS) for h in range(nh)]
            ins = [_gdn_chunk_inputs(x_ref, aux_ref, auxt_ref, g, h) for g, h in where]
            lcs = _gdn_local([(q, k, b, gc, gc_row) for q, k, _, b, gc, gc_row in ins], True)
            tinvs = [lc["tinv"] for lc in lcs]
            us = _dot3_many(tinvs, [b * v for _, _, v, b, _, _ in ins])
            ws = _dot3_many(tinvs, [(b * lc["gam"]) * k for (_, k, _, b, _, _), lc in zip(ins, lcs)])
            for i, ((g, h), (q, k, _, _, _, _), lc) in enumerate(zip(where, ins, lcs)):
                rows, cols = slice(g * ck, (g + 1) * ck), _gdn_head_cols(h)
                u_ref[rows, cols] = us[i]
                w_ref[rows, cols] = ws[i].astype(w_ref.dtype)
                qd_ref[rows, cols] = (lc["gam"] * q).astype(qd_ref.dtype)
                kd_ref[rows, cols] = (lc["kdec"] * k).astype(kd_ref.dtype)
            for g in range(g0, g0 + GDN_ILP_CHUNKS):
                rows = slice(g * ck, (g + 1) * ck)
                mine = [lc for (gg, _), lc in zip(where, lcs) if gg == g]
                mm_ref[rows, :] = jnp.concatenate([lc["mm"] for lc in mine], axis=1).astype(mm_ref.dtype)
                tinv_ref[rows, :] = jnp.concatenate([lc["tinv"] for lc in mine], axis=1)
        host.wait(refs, 3, 9 + host.n_in, grid)

    sq = nh * ck
    outs = pl.pallas_call(
        body, grid=grid,
        in_specs=_gdn_specs(t, (3 * gw, LANES)) + [pl.BlockSpec((gs, 16, ck), lambda i: (i, 0, 0))] + host.in_specs,
        out_specs=_gdn_specs(t, (gw, gw, gw, gw, sq, sq)) + host.out_specs,
        out_shape=[jax.ShapeDtypeStruct((t, gw), F32)] + [jax.ShapeDtypeStruct((t, gw), BF16)] * 3
        + [jax.ShapeDtypeStruct((t, sq), BF16), jax.ShapeDtypeStruct((t, sq), F32)] + host.out_shapes,
        scratch_shapes=host.scratch,
        compiler_params=_cp(host.semantics(("parallel",))), name=name,
    )(qkvn, aux, aux_t, *host.ins)
    return (*outs[:6], list(outs[6:])) if exchange is not None else outs


def _gdn_seq_fwd(u, w, qd, kd, mm, aux, *, name):
    t = u.shape[0]
    hd, nh, ck, gs = GDN_HEAD_DIM, GDN_HEADS, GDN_CHUNK, GDN_STEP_CHUNKS
    gw = nh * hd
    sq = nh * ck

    def body(u_ref, w_ref, qd_ref, kd_ref, mm_ref, aux_ref, o_ref, vn_ref, sall_ref, s_ref):
        @pl.when(pl.program_id(0) == 0)
        def _():
            s_ref[...] = jnp.zeros_like(s_ref)

        heads = range(nh)
        hcols = [_gdn_head_cols(h) for h in heads]
        sts = [s_ref[h] for h in heads]
        for g in range(gs):
            rows = slice(g * ck, (g + 1) * ck)
            last = (g + 1) * ck - 1
            for h in heads:
                sall_ref[g, h] = sts[h]
            stbs = [st.astype(BF16) for st in sts]
            w_s = [_dot(w_ref[rows, c], stb) for c, stb in zip(hcols, stbs)]
            q_s = [_dot(qd_ref[rows, c], stb) for c, stb in zip(hcols, stbs)]
            vnbs = [(u_ref[rows, c] - ws).astype(BF16) for c, ws in zip(hcols, w_s)]
            m_v = [_dot(mm_ref[rows, h * ck:(h + 1) * ck], vnbs[h]) for h in heads]
            k_v = [_dot(kd_ref[rows, c], vnb, TN) for c, vnb in zip(hcols, vnbs)]
            for h, c in zip(heads, hcols):
                vn_ref[rows, c] = vnbs[h]
                o_ref[rows, c] = q_s[h] + m_v[h]
            gam_c = [jnp.exp(aux_ref[last:last + 1, 2 * nh + h:2 * nh + h + 1]) for h in heads]
            sts = [gam_c[h] * sts[h] + k_v[h] for h in heads]
        for h in heads:
            s_ref[h] = sts[h]

    return pl.pallas_call(
        body, grid=(t // (gs * ck),),
        in_specs=_gdn_specs(t, (gw, gw, gw, gw, sq, LANES)),
        out_specs=_gdn_specs(t, (gw, gw)) + [pl.BlockSpec((gs, nh, hd, hd), lambda i: (i, 0, 0, 0))],
        out_shape=[jax.ShapeDtypeStruct((t, gw), F32), jax.ShapeDtypeStruct((t, gw), BF16),
                   jax.ShapeDtypeStruct((t // ck, nh, hd, hd), F32)],
        scratch_shapes=[pltpu.VMEM((nh, hd, hd), F32)],
        compiler_params=_cp(("arbitrary",)), name=name,
    )(u, w, qd, kd, mm, aux)


def _gdn_seq_bwd(do, w, qd, kd, mm, vn, s_all, aux, *, name):
    t = do.shape[0]
    hd, nh, ck, gs = GDN_HEAD_DIM, GDN_HEADS, GDN_CHUNK, GDN_STEP_CHUNKS
    gw = nh * hd
    sq = nh * ck
    nsteps = t // (gs * ck)

    def body(do_ref, w_ref, qd_ref, kd_ref, mm_ref, vn_ref, sall_ref, aux_ref, dvn_ref, dqd_ref, dkd_ref, dw_ref,
             dlast_ref, ds_ref):
        @pl.when(pl.program_id(0) == 0)
        def _():
            ds_ref[...] = jnp.zeros_like(ds_ref)

        lane = lax.broadcasted_iota(jnp.int32, (ck, LANES), 1)
        rowi = lax.broadcasted_iota(jnp.int32, (ck, LANES), 0)
        heads = range(nh)
        hcols = [_gdn_head_cols(h) for h in heads]
        dsns = [ds_ref[h] for h in heads]
        for g in reversed(range(gs)):
            rows = slice(g * ck, (g + 1) * ck)
            last = (g + 1) * ck - 1
            sts = [sall_ref[g, h] for h in heads]
            stbs = [st.astype(BF16) for st in sts]
            dsbs = [dsn.astype(BF16) for dsn in dsns]
            dobs = [do_ref[rows, c].astype(BF16) for c in hcols]
            dvns = [_dot(mm_ref[rows, h * ck:(h + 1) * ck], dobs[h], TN) + _dot(kd_ref[rows, hcols[h]], dsbs[h])
                    for h in heads]
            dqds = [_dot(dob, stb, NT) for dob, stb in zip(dobs, stbs)]
            dkds = [_dot(vn_ref[rows, c], dsb, NT) for c, dsb in zip(hcols, dsbs)]
            q_o = [_dot(qd_ref[rows, c], dob, TN) for c, dob in zip(hcols, dobs)]
            dvbs = [dvn.astype(BF16) for dvn in dvns]
            dws = [_dot(dvb, stb, NT) for dvb, stb in zip(dvbs, stbs)]
            w_v = [_dot(w_ref[rows, c], dvb, TN) for c, dvb in zip(hcols, dvbs)]
            gam_c = [jnp.exp(aux_ref[last:last + 1, 2 * nh + h:2 * nh + h + 1]) for h in heads]
            dlast = jnp.zeros((ck, LANES), F32)
            for h, c in zip(heads, hcols):
                dvn_ref[rows, c] = dvns[h]
                dqd_ref[rows, c] = dqds[h]
                dkd_ref[rows, c] = dkds[h]
                dw_ref[rows, c] = -dws[h]
                dgam_c = jnp.sum(jnp.sum(dsns[h] * sts[h], axis=1, keepdims=True), axis=0, keepdims=True)
                dlast = dlast + jnp.where((rowi == ck - 1) & (lane == h), gam_c[h] * dgam_c, 0.0)
            dlast_ref[rows, :] = dlast
            dsns = [q_o[h] + gam_c[h] * dsns[h] - w_v[h] for h in heads]
        for h in heads:
            ds_ref[h] = dsns[h]

    return pl.pallas_call(
        body, grid=(nsteps,),
        in_specs=_gdn_specs(t, (gw, gw, gw, gw, sq, gw), reverse=True)
        + [pl.BlockSpec((gs, nh, hd, hd), lambda i: (nsteps - 1 - i, 0, 0, 0))] + _gdn_specs(t, (LANES,), reverse=True),
        out_specs=_gdn_specs(t, (gw, gw, gw, gw, LANES), reverse=True),
        out_shape=[jax.ShapeDtypeStruct((t, gw), F32)] * 4 + [jax.ShapeDtypeStruct((t, LANES), F32)],
        scratch_shapes=[pltpu.VMEM((nh, hd, hd), F32)],
        compiler_params=_cp(("arbitrary",)), name=name,
    )(do, w, qd, kd, mm, vn, s_all, aux)


def _gdn_local_bwd(qkvn, aux, aux_t, tinv, u, w, vn, do, dvn, dqd, dkd, dw, dlast, *, name):
    t = qkvn.shape[0]
    hd, nh, ck, gs = GDN_HEAD_DIM, GDN_HEADS, GDN_CHUNK, GDN_STEP_CHUNKS
    gw = nh * hd
    sq = nh * ck

    def body(x_ref, aux_ref, auxt_ref, tinv_ref, u_ref, w_ref, vn_ref, do_ref, dvn_ref, dqd_ref, dkd_ref, dw_ref,
             dlast_ref, dx_ref, daux_ref):
        lane = lax.broadcasted_iota(jnp.int32, (ck, LANES), 1)
        ones = jnp.ones((ck, LANES), F32)
        ii = lax.broadcasted_iota(jnp.int32, (ck, ck), 0)
        jj = lax.broadcasted_iota(jnp.int32, (ck, ck), 1)
        suffix = jnp.where(jj >= ii, 1.0, 0.0)
        for g0 in range(0, gs, GDN_ILP_CHUNKS_BWD):
            where = [(g, h) for g in range(g0, g0 + GDN_ILP_CHUNKS_BWD) for h in range(nh)]
            at = [(slice(g * ck, (g + 1) * ck), _gdn_head_cols(h)) for g, h in where]
            ins = [_gdn_chunk_inputs(x_ref, aux_ref, auxt_ref, g, h) for g, h in where]
            lcs = _gdn_local([(q, k, b, gc, gc_row) for q, k, _, b, gc, gc_row in ins], False)
            tinvs = [tinv_ref[slice(g * ck, (g + 1) * ck), h * ck:(h + 1) * ck] for g, h in where]
            dms = [jnp.where(lc["lower"], _bdot(do_ref[r, c], vn_ref[r, c], NT), 0.0) for lc, (r, c) in zip(lcs, at)]
            drvs = _dot3_many(tinvs, [dvn_ref[r, c] for r, c in at], TN)
            drks = _dot3_many(tinvs, [dw_ref[r, c] for r, c in at], TN)
            das = [jnp.where(lc["strict"], -(_bdot(drv, u_ref[r, c], NT) + _bdot(drk, w_ref[r, c], NT)), 0.0)
                   for lc, (r, c), drv, drk in zip(lcs, at, drvs, drks)]
            f_mats = [da * (i[3] * lc["kk"]) * lc["dmat"] + dm * lc["qk"] * lc["dmat"]
                      for i, lc, da, dm in zip(ins, lcs, das, dms)]
            col_sums = _dot3_many(f_mats, [ones] * len(where), TN)
            dgc_all = {g: dlast_ref[slice(g * ck, (g + 1) * ck), :] for g in range(g0, g0 + GDN_ILP_CHUNKS_BWD)}
            db_all = {g: jnp.zeros((ck, LANES), F32) for g in range(g0, g0 + GDN_ILP_CHUNKS_BWD)}
            e_mats = [da * lc["dmat"] * i[3] for i, lc, da in zip(ins, lcs, das)]
            dmds = [dm * lc["dmat"] for lc, dm in zip(lcs, dms)]
            dq_mm = [_bdot(dmd, i[1]) for i, dmd in zip(ins, dmds)]
            dk_mm = [_bdot(e, i[1]) + _bdot(e, i[1], TN) + _bdot(dmd, i[0], TN) for i, e, dmd in zip(ins, e_mats, dmds)]
            for n, ((g, h), (q, k, v, b, _, _), lc, (rows, cols)) in enumerate(zip(where, ins, lcs, at)):
                dmat, kk, gam, kdec = (lc[key] for key in ("dmat", "kk", "gam", "kdec"))
                drv, drk, da = drvs[n], drks[n], das[n]
                dqd_h, dkd_h = dqd_ref[rows, cols], dkd_ref[rows, cols]
                rs_rk = jnp.sum(drk * k, axis=-1, keepdims=True)
                db = (jnp.sum(drv * v, axis=-1, keepdims=True) + gam * rs_rk
                      + jnp.sum(da * kk * dmat, axis=-1, keepdims=True))
                dx_ref[rows, cols] = dq_mm[n] + gam * dqd_h
                dx_ref[rows, gw + cols.start:gw + cols.stop] = (b * gam) * drk + dk_mm[n] + kdec * dkd_h
                dx_ref[rows, 2 * gw + cols.start:2 * gw + cols.stop] = b * drv
                e_vec = jnp.sum(dkd_h * (kdec * k), axis=-1, keepdims=True)
                dgc = (b * gam * rs_rk + gam * jnp.sum(dqd_h * q, axis=-1, keepdims=True)
                       + jnp.sum(f_mats[n], axis=-1, keepdims=True) - col_sums[n][:, 0:1] - e_vec)
                is_last = lax.broadcasted_iota(jnp.int32, (ck, 1), 0) == ck - 1
                dgc = dgc + jnp.where(is_last, jnp.sum(e_vec, axis=0, keepdims=True), 0.0)
                dgc_all[g] = dgc_all[g] + jnp.where(lane == h, dgc, 0.0)
                db_all[g] = db_all[g] + jnp.where(lane == nh + h, db, 0.0)
            for g in dgc_all:
                daux_ref[slice(g * ck, (g + 1) * ck), :] = _dot3(suffix, dgc_all[g]) + db_all[g]

    return pl.pallas_call(
        body, grid=(t // (gs * ck),),
        in_specs=_gdn_specs(t, (3 * gw, LANES)) + [pl.BlockSpec((gs, 16, ck), lambda i: (i, 0, 0))]
        + _gdn_specs(t, (sq, gw, gw, gw, gw, gw, gw, gw, gw, LANES)),
        out_specs=_gdn_specs(t, (3 * gw, LANES)),
        out_shape=[jax.ShapeDtypeStruct((t, 3 * gw), F32), jax.ShapeDtypeStruct((t, LANES), F32)],
        compiler_params=_cp(("parallel",)), name=name,
    )(qkvn, aux, aux_t, tinv, u, w, vn, do, dvn, dqd, dkd, dw, dlast)


def _gdn_pre_bwd1(xc, dqkvn, daux, ab, alog_pad, dt_pad, dkv, dp, dp_col, *, name, ab_blk=0, tm=256):
    t, cw = xc.shape
    hd, nh = GDN_HEAD_DIM, GDN_HEADS
    gw = nh * hd
    tm = _blk(t, tm)

    kvw = dkv.shape[1]
    seg = kvw + AB_PAD
    assert dp_col % seg == 0

    def body(xc_ref, dy_ref, daux_ref, ab_ref, al_ref, dt_ref, dkv_ref, _, dxc_ref, dab_ref, dal_ref, ddt_ref):
        i = pl.program_id(0)
        xc = xc_ref[...]
        sg = _sigmoid(xc)
        s = xc * sg
        dsilu = sg * (1.0 + xc * (1.0 - sg))
        for h in range(2 * nh):
            xh = s[:, h * hd:(h + 1) * hd]
            scale = hd ** -0.5 if h < nh else 1.0
            dyh = dy_ref[:, h * hd:(h + 1) * hd] * scale
            r = lax.rsqrt(jnp.sum(xh * xh, axis=-1, keepdims=True) + L2_EPS)
            dxh = r * dyh - xh * (r * r * r) * jnp.sum(dyh * xh, axis=-1, keepdims=True)
            dxc_ref[:, h * hd:(h + 1) * hd] = dxh * dsilu[:, h * hd:(h + 1) * hd]
        dxc_ref[:, 2 * gw:] = dy_ref[:, 2 * gw:] * dsilu[:, 2 * gw:]
        abv = ab_ref[...]
        dauxv = daux_ref[...]
        lane = lax.broadcasted_iota(jnp.int32, abv.shape, 1)
        is_a = lane < nh
        is_b = (lane >= nh) & (lane < 2 * nh)
        pre = abv + dt_ref[...]
        neg_ea = -jnp.exp(al_ref[...])
        d_a = jnp.where(is_a, dauxv * neg_ea * _sigmoid(pre), 0.0)
        beta = _sigmoid(abv)
        d_b = jnp.where(is_b, dauxv * beta * (1.0 - beta), 0.0)
        dab_ref[:, :kvw] = dkv_ref[...]
        dab_ref[:, kvw:kvw + LANES] = (d_a + d_b).astype(dab_ref.dtype)
        dab_ref[:, kvw + LANES:] = jnp.zeros((tm, AB_PAD - LANES), dab_ref.dtype)
        dal = jnp.sum(jnp.where(is_a, dauxv * neg_ea * _softplus(pre), 0.0), axis=0, keepdims=True)
        ddt = jnp.sum(d_a, axis=0, keepdims=True)

        @pl.when(i == 0)
        def _():
            dal_ref[...] = dal
            ddt_ref[...] = ddt

        @pl.when(i > 0)
        def _():
            dal_ref[...] += dal
            ddt_ref[...] += ddt

    row = lambda c: pl.BlockSpec((tm, c), lambda i: (i, 0))
    vec = pl.BlockSpec((1, LANES), lambda i: (0, 0))
    return pl.pallas_call(
        body, grid=(t // tm,),
        in_specs=[row(cw), row(cw), row(LANES), pl.BlockSpec((tm, LANES), lambda i: (i, ab_blk)), vec, vec, row(kvw),
                  ANY_SPEC],
        out_specs=[row(cw), pl.BlockSpec((tm, seg), lambda i: (i, dp_col // seg)), vec, vec],
        out_shape=[jax.ShapeDtypeStruct((t, cw), F32), jax.ShapeDtypeStruct(dp.shape, dp.dtype),
                   jax.ShapeDtypeStruct((1, LANES), F32), jax.ShapeDtypeStruct((1, LANES), F32)],
        input_output_aliases={7: 1}, compiler_params=_cp(("arbitrary",)), name=name,
    )(xc, dqkvn, daux, ab, alog_pad, dt_pad, dkv, dp)


def _gdn_pre_bwd2(dxc, qkvb, conv_w, dp, dp_col, *, name, tm=512):
    t, cw = dxc.shape
    tm = _blk(t, tm)
    hb = tm // HALO
    nblk = t // tm
    cg = GDN_HEADS * GDN_HEAD_DIM
    assert cw % cg == 0 and dp_col % cg == 0
    col0 = dp_col // cg

    def body(d_ref, dn_ref, x_ref, xp_ref, w_ref, _, dx_ref, dw_ref):
        i = pl.program_id(1)
        dcur = d_ref[...]
        dnxt = jnp.where(i < nblk - 1, dn_ref[...], 0.0)
        cur = x_ref[...]
        prev = jnp.where(i > 0, xp_ref[...], 0.0)
        dx = None
        dws = []
        for tap in range(GDN_CONV):
            j = GDN_CONV - 1 - tap
            term = w_ref[tap:tap + 1, :] * _shift_up(dcur, dnxt, j)
            dx = term if dx is None else dx + term
            dws.append(jnp.sum(dcur * _shift_down(cur, prev, j), axis=0, keepdims=True))
        dx_ref[...] = dx.astype(dx_ref.dtype)
        dw = jnp.concatenate(dws, axis=0)

        @pl.when(i == 0)
        def _():
            dw_ref[...] = dw

        @pl.when(i > 0)
        def _():
            dw_ref[...] += dw

    row = pl.BlockSpec((tm, cg), lambda c, i: (i, c))
    wsp = pl.BlockSpec((GDN_CONV, cg), lambda c, i: (0, c))
    return pl.pallas_call(
        body, grid=(cw // cg, nblk),
        in_specs=[row, pl.BlockSpec((HALO, cg), lambda c, i: (jnp.minimum((i + 1) * hb, t // HALO - 1), c)),
                  row, pl.BlockSpec((HALO, cg), lambda c, i: (jnp.maximum(i * hb - 1, 0), c)), wsp, ANY_SPEC],
        out_specs=[pl.BlockSpec((tm, cg), lambda c, i: (i, col0 + c)), wsp],
        out_shape=[jax.ShapeDtypeStruct(dp.shape, dp.dtype), jax.ShapeDtypeStruct((GDN_CONV, cw), F32)],
        input_output_aliases={5: 0}, compiler_params=_cp(("arbitrary", "arbitrary")), name=name,
    )(dxc, dxc, qkvb, qkvb, conv_w, dp)


def _gdn_post_fwd(o, z, norm_w, *, name, tm=512):
    t, gw = o.shape
    hd, nh = GDN_HEAD_DIM, GDN_HEADS
    tm = _blk(t, tm)

    def body(o_ref, z_ref, w_ref, y_ref):
        zv = z_ref[...]
        sz = zv * _sigmoid(zv)
        for h in range(nh):
            oh = o_ref[:, h * hd:(h + 1) * hd]
            r = lax.rsqrt(jnp.mean(oh * oh, axis=-1, keepdims=True) + RMS_EPS)
            y_ref[:, h * hd:(h + 1) * hd] = (oh * r * w_ref[...] * sz[:, h * hd:(h + 1) * hd]).astype(y_ref.dtype)

    row = pl.BlockSpec((tm, gw), lambda i: (i, 0))
    return pl.pallas_call(
        body, grid=(t // tm,), in_specs=[row, row, pl.BlockSpec((1, hd), lambda i: (0, 0))], out_specs=row,
        out_shape=jax.ShapeDtypeStruct((t, gw), BF16), compiler_params=_cp(("parallel",)), name=name,
    )(o, z, norm_w)


def _gdn_post_bwd(dy, o, z, norm_w, dp, dp_col, *, name, tm=512):
    t, gw = o.shape
    hd, nh = GDN_HEAD_DIM, GDN_HEADS
    tm = _blk(t, tm)

    def body(dy_ref, o_ref, z_ref, w_ref, _, do_ref, dz_ref, dw_ref):
        i = pl.program_id(0)
        zv = z_ref[...]
        sg = _sigmoid(zv)
        sz = zv * sg
        dsz = sg * (1.0 + zv * (1.0 - sg))
        dw = None
        for h in range(nh):
            sl = slice(h * hd, (h + 1) * hd)
            oh = o_ref[:, sl]
            dyh = dy_ref[:, sl].astype(F32)
            r = lax.rsqrt(jnp.mean(oh * oh, axis=-1, keepdims=True) + RMS_EPS)
            xh = oh * r
            dz_ref[:, sl] = (dyh * xh * w_ref[...] * dsz[:, sl]).astype(dz_ref.dtype)
            dn = dyh * sz[:, sl]
            dxh = dn * w_ref[...]
            do_ref[:, sl] = r * (dxh - xh * jnp.mean(dxh * xh, axis=-1, keepdims=True))
            part = jnp.sum(dn * xh, axis=0, keepdims=True)
            dw = part if dw is None else dw + part

        @pl.when(i == 0)
        def _():
            dw_ref[...] = dw

        @pl.when(i > 0)
        def _():
            dw_ref[...] += dw

    row = pl.BlockSpec((tm, gw), lambda i: (i, 0))
    vec = pl.BlockSpec((1, hd), lambda i: (0, 0))
    return pl.pallas_call(
        body, grid=(t // tm,), in_specs=[row, row, row, vec, ANY_SPEC],
        out_specs=[row, pl.BlockSpec((tm, gw), lambda i: (i, dp_col // gw)), vec],
        out_shape=[jax.ShapeDtypeStruct((t, gw), F32), jax.ShapeDtypeStruct(dp.shape, dp.dtype),
                   jax.ShapeDtypeStruct((1, hd), F32)],
        input_output_aliases={4: 1}, compiler_params=_cp(("arbitrary",)), name=name,
    )(dy, o, z, norm_w, dp)


IN_NAMES = ("q_a", "kv_a", "qkv_b", "ab", "z", "q_c", "gates")
CAT_NAMES = ("gates", "q_a", "qkv_b", "z", "q_c", "kv_a", "ab")
AB_PAD = 256


def _in_widths(d):
    gw = GDN_HEADS * GDN_HEAD_DIM
    return dict(q_a=SWA_Q_HEADS * SWA_HEAD_DIM, kv_a=2 * SWA_KV_HEADS * SWA_HEAD_DIM, qkv_b=3 * gw, ab=2 * GDN_HEADS,
                z=gw, q_c=XA_HEADS * XA_HEAD_DIM, gates=3 * d)


def _ranges(names, widths):
    out, start = {}, 0
    for k in names:
        out[k] = (start, widths[k])
        start += widths[k]
    return out, start


def _cat_ranges(d):
    widths = dict(_in_widths(d), ab=AB_PAD)
    return _ranges(CAT_NAMES, widths)


def _to_cat(shards):
    ns, d, n = shards.shape
    src, _ = _ranges(IN_NAMES, _in_widths(d))
    cols = []
    for k in CAT_NAMES:
        lo, hi = src[k][0], src[k][0] + src[k][1]
        for s in range(ns):
            a, b = max(lo, s * n), min(hi, (s + 1) * n)
            if a < b:
                cols.append(shards[s][:, a - s * n:b - s * n])
    cols.append(jnp.zeros((d, AB_PAD - src["ab"][1]), shards.dtype))
    return jnp.concatenate(cols, axis=1)


def _from_cat(w_cat):
    d = w_cat.shape[0]
    src, total = _ranges(IN_NAMES, _in_widths(d))
    cat, _ = _cat_ranges(d)
    n = total // N_SHARDS
    shards = []
    for s in range(N_SHARDS):
        pieces = []
        for k in IN_NAMES:
            a, b = max(s * n, src[k][0]), min((s + 1) * n, src[k][0] + src[k][1])
            if a < b:
                pieces.append(w_cat[:, cat[k][0] + a - src[k][0]:cat[k][0] + b - src[k][0]])
        shards.append(jnp.concatenate(pieces, axis=1))
    return jnp.stack(shards)


def _pad_cols(a, width):
    return jnp.pad(a, ((0, 0), (0, width - a.shape[1])))


def _relu2_epilogue(acc):
    r = jnp.maximum(acc, 0.0)
    return acc, r * r


def _add_epilogue(acc, res):
    return (acc + res,)


def _drelu2_epilogue(acc, u):
    return (acc * (2.0 * jnp.maximum(u.astype(F32), 0.0)),)


def _local_step(x, mem, tgt, wts, small, comm=None):
    t, d = x.shape
    nh = GDN_HEADS
    cat, cat_w = _cat_ranges(d)
    alog_pad = _pad_cols(small["a_log"], LANES)
    dt_pad = _pad_cols(small["dt_bias"], LANES)
    kvw = cat["kv_a"][1]
    assert cat["ab"][0] == cat["kv_a"][0] + kvw
    ab_blk = kvw // LANES

    if comm is None:
        n = _rms_fwd(x, small["g_mix"], name="rms_mix")
        w_cat = wts["w_cat"]
    else:
        n, landed = _rms_fwd(x, small["g_mix"], name="rms_mix", exchange=comm.gather_exchange(["w_in"]))
        w_cat = _to_cat(_exchange_call(_gather_pass_on(landed), name="ag_w_in_pass")[0])
    assert w_cat.shape == (d, cat_w)
    q_a = _mm(n, w_cat, b_window=cat["q_a"], out_dtypes=(BF16,), name="in_q_a")
    kv_a, ab = _mm(n, w_cat, b_window=(cat["kv_a"][0], kvw + AB_PAD), out_dtypes=(BF16, F32), name="in_kv_ab")
    qkvb = _mm(n, w_cat, b_window=cat["qkv_b"], tn=512, name="in_qkv_b")
    z = _mm(n, w_cat, b_window=cat["z"], name="in_z")
    q_c = _mm(n, w_cat, b_window=cat["q_c"], out_dtypes=(BF16,), name="in_q_c")
    if comm is None:
        gates = _mm(n, w_cat, b_window=cat["gates"], name="in_gates")
        y_a, lse = _swa_fwd(q_a, kv_a, small["sinks"], name="swa_fwd")
    else:
        gates, landed_mlp = _mm(n, w_cat, b_window=cat["gates"], name="in_gates",
                                exchange=comm.gather_exchange(comm.MLP[1:]))
        y_a, lse, landed = _swa_fwd(q_a, kv_a, small["sinks"], name="swa_fwd", exchange=comm.gather_exchange(comm.MLP[:1]))
        landed_mlp = landed + landed_mlp
    xc, qkvn, aux = _gdn_pre_fwd(qkvb, small["conv_w"], ab, alog_pad, dt_pad, ab_blk=ab_blk, name="gdn_pre_fwd")
    aux_t = aux[:, :16].reshape(t // GDN_CHUNK, GDN_CHUNK, 16).transpose(0, 2, 1)
    if comm is None:
        gdn_u, gdn_w, gdn_qd, gdn_kd, gdn_mm, gdn_tinv = _gdn_local_fwd(qkvn, aux, aux_t, name="gdn_local_fwd")
    else:
        gdn_u, gdn_w, gdn_qd, gdn_kd, gdn_mm, gdn_tinv, landed_mid = _gdn_local_fwd(
            qkvn, aux, aux_t, name="gdn_local_fwd", exchange=comm.gather_exchange(comm.mid))
        wts = dict(wts, **comm.gathered(comm.mid, landed_mid, "mid"))
    o_b, gdn_vn, s_all = _gdn_seq_fwd(gdn_u, gdn_w, gdn_qd, gdn_kd, gdn_mm, aux, name="gdn_seq_fwd")
    y_b = _gdn_post_fwd(o_b, z, small["gdn_norm_w"], name="gdn_post_fwd")
    nmem = _rms_fwd(mem, small["g_mem"], name="rms_mem")
    mkv = _mm(nmem, wts["w_mem_kv"], out_dtypes=(BF16,), name="mem_kv")
    y_c = _xa_fwd(q_c, mkv, name="xa_fwd")
    ys = (y_a, y_b, y_c)
    w_ups = (wts["w_swa_up"], wts["w_gdn_up"], wts["w_xa_up"])
    merged = _merge_fwd(ys, w_ups, gates, name="merge_fwd")
    if comm is None:
        h1 = _mm(merged, wts["w_out"], extras=(x,), epilogue=_add_epilogue, name="out_proj")
    else:
        h1, whole = _mm(merged, wts["w_out"], extras=(x,), epilogue=_add_epilogue, name="out_proj",
                        exchange=_gather_pass_on(landed_mlp))
        wts = dict(wts, **comm.as_weights(comm.MLP, whole))
    n2 = _rms_fwd(h1, small["g_mlp"], name="rms_mlp")
    u, act = _mm(n2, wts["w_mlp_in"], b_sharded=True, out_dtypes=(BF16, BF16), epilogue=_relu2_epilogue, name="mlp_in")
    h2 = _mm(act, wts["w_mlp_out"], extras=(h1,), epilogue=_add_epilogue, name="mlp_out")
    dh2, dh2_b, dg_final, loss = _final_loss(h2, small["g_final"], tgt, name="final_loss")

    grads = {"g_final": dg_final}
    du = _mm(dh2_b, wts["w_mlp_out"], tb=True, out_dtypes=(BF16,), extras=(u,), epilogue=_drelu2_epilogue, name="d_mlp_act")
    grads["w_mlp_out"] = _mm(act, dh2_b, ta=True, out_dtypes=(BF16,), name="dw_mlp_out")
    grads["w_mlp_in"] = _mm(n2, du, ta=True, out_sharded=True, out_dtypes=(BF16,), name="dw_mlp_in")
    if comm is None:
        dn2 = _mm(du, wts["w_mlp_in"], tb=True, b_sharded=True, name="d_mlp_in")
    else:
        g_mlp = [comm.shard_major(k, grads.pop(k)) for k in comm.MLP]
        dn2, sib_mlp = _mm(du, wts["w_mlp_in"], tb=True, b_sharded=True, name="d_mlp_in", exchange=_sibling_halves(g_mlp))
        s1_mlp = comm.pair_sums(g_mlp, "mlp", sib_mlp)
    dh1, dh1_b, grads["g_mlp"] = _rms_bwd(dn2, h1, small["g_mlp"], dh2, name="rms_mlp_bwd")
    dmerged = _mm(dh1_b, wts["w_out"], tb=True, name="d_out_proj")
    grads["w_out"] = _mm(merged, dh1_b, ta=True, out_dtypes=(BF16,), name="dw_out")
    *dus, dp = _merge_bwd(ys, w_ups, gates, dmerged, cat_w, name="merge_bwd")
    dys = []
    for y, du_i, w_up, key in zip(ys, dus, w_ups, ("w_swa_up", "w_gdn_up", "w_xa_up")):
        dys.append(_mm(du_i, w_up, tb=True, b_sharded=True, out_dtypes=(BF16,), name="d_" + key))
        grads[key] = _mm(y, du_i, ta=True, out_sharded=True, out_dtypes=(BF16,), name="dw_" + key[2:])
    dp, dkv_a, grads["sinks"] = _swa_bwd(q_a, kv_a, small["sinks"], y_a, lse, dys[0], dp, cat["q_a"][0], name="swa_bwd")
    do_b, dp, grads["gdn_norm_w"] = _gdn_post_bwd(dys[1], o_b, z, small["gdn_norm_w"], dp, cat["z"][0],
                                                  name="gdn_post_bwd")
    dvn, dqd, dkd, dw_, dlast = _gdn_seq_bwd(do_b, gdn_w, gdn_qd, gdn_kd, gdn_mm, gdn_vn, s_all, aux, name="gdn_seq_bwd")
    dqkvn, daux = _gdn_local_bwd(qkvn, aux, aux_t, gdn_tinv, gdn_u, gdn_w, gdn_vn, do_b, dvn, dqd, dkd, dw_, dlast,
                                 name="gdn_local_bwd")
    dxc, dp, dalog, ddt = _gdn_pre_bwd1(xc, dqkvn, daux, ab, alog_pad, dt_pad, dkv_a, dp, cat["kv_a"][0], ab_blk=ab_blk,
                                        name="gdn_pre_bwd1")
    grads["a_log"], grads["dt_bias"] = dalog[:, :nh], ddt[:, :nh]
    dp, grads["conv_w"] = _gdn_pre_bwd2(dxc, qkvb, small["conv_w"], dp, cat["qkv_b"][0], name="gdn_pre_bwd2")
    dp, dmkv = _xa_bwd(q_c, mkv, dys[2], dp, cat["q_c"][0], name="xa_bwd")
    grads["w_mem_kv"] = _mm(nmem, dmkv, ta=True, out_dtypes=(BF16,), name="dw_mem_kv")
    dnmem = _mm(dmkv, wts["w_mem_kv"], tb=True, name="d_mem_kv")
    _, _, grads["g_mem"] = _rms_bwd(dnmem, mem, small["g_mem"], jnp.zeros_like(mem), name="rms_mem_bwd")
    if comm is None:
        grads["w_cat"] = _mm(n, dp, ta=True, out_dtypes=(BF16,), name="dw_in")
        dn = _mm(dp, w_cat, tb=True, name="d_in_proj")
    else:
        s1_mid = comm.pair_sums([comm.shard_major(k, grads.pop(k)) for k in comm.mid], "mid")
        dw_cat, rcv_mlp = _mm(n, dp, ta=True, out_dtypes=(BF16,), name="dw_in", exchange=_chip_exchange(s1_mlp))
        s1_in = comm.pair_sums([_from_cat(dw_cat)], "in")
        dn, rcv_rest = _mm(dp, w_cat, tb=True, name="d_in_proj", exchange=_chip_exchange(s1_in + s1_mid))
        halves = comm.chip_sums(s1_in + s1_mid + s1_mlp, rcv_rest + rcv_mlp)
    if comm is None:
        dx, _, grads["g_mix"] = _rms_bwd(dn, x, small["g_mix"], dh1, name="rms_mix_bwd")
    else:
        dx, _, grads["g_mix"], reduced = _rms_bwd(dn, x, small["g_mix"], dh1, name="rms_mix_bwd",
                                                  exchange=_join_halves(halves))
        grads.update(zip(["w_in"] + comm.mid + list(comm.MLP), reduced))
    return loss, dx, grads


HBM_SPEC = pl.BlockSpec(memory_space=pltpu.HBM)
VMEM_SPEC = pl.BlockSpec(memory_space=pltpu.VMEM)
N_CHIPS = N_SHARDS
N_DEV = 8
DMA_CHUNK_BYTES = 1 << 20


def _place():
    return lax.axis_index("x"), lax.axis_index("y"), lax.axis_index("c")


def _other_chips(x, y):
    return [(1 - x, y), (x, 1 - y), (1 - x, 1 - y)]


def _n_chunks(rows, row_bytes):
    n = 1
    while rows % (2 * n) == 0 and (rows // (2 * n)) % 16 == 0 and (rows // n) * row_bytes > DMA_CHUNK_BYTES:
        n *= 2
    return n


def _sem_scratch(n_remote, n_local):
    return [pltpu.SemaphoreType.DMA((max(n_remote, 1),)), pltpu.SemaphoreType.DMA((max(n_remote, 1),)),
            pltpu.SemaphoreType.DMA((max(n_local, 1),))]


def _gather_over_ici(shards):
    plan = _half_chunks(shards, 0)

    def copies_of(in_refs, out_refs, place):
        x, y, c = place
        remote, local = [], []
        for i, r0, nr in plan:
            rh = shards[i].shape[0] // 2
            mine = pl.ds(c * rh + r0, nr)
            for chip in _other_chips(x, y):
                remote.append((in_refs[i].at[mine], out_refs[i].at[2 * x + y, mine], (*chip, c)))
            for half in range(2):
                rows = pl.ds(half * rh + r0, nr)
                local.append((in_refs[i].at[rows], out_refs[i].at[2 * x + y, rows]))
        return remote, local

    shapes = tuple(jax.ShapeDtypeStruct((N_CHIPS, *s.shape), s.dtype) for s in shards)
    return Exchange(tuple(shards), shapes, 3 * len(plan), 2 * len(plan), copies_of)


def _gather_pass_on(arrived):
    plan = _half_chunks([jax.ShapeDtypeStruct(a.shape[1:], a.dtype) for a in arrived], 0)

    def copies_of(in_refs, out_refs, place):
        x, y, c = place
        remote = []
        for i, r0, nr in plan:
            mine = pl.ds(c * (arrived[i].shape[1] // 2) + r0, nr)
            for chip in _other_chips(x, y):
                rows = out_refs[i].at[2 * chip[0] + chip[1], mine]
                remote.append((rows, rows, (x, y, 1 - c)))
        return remote, []

    shapes = tuple(jax.ShapeDtypeStruct(a.shape, a.dtype) for a in arrived)
    return Exchange(tuple(arrived), shapes, 3 * len(plan), 0, copies_of, tuple((i, i) for i in range(len(arrived))))


def _exchange_call(ex, *, name):
    n_in, n_out = len(ex.ins), len(ex.out_shapes)

    def body(*refs):
        cps = _exchange_copies(ex, refs[:n_in], refs[n_in:n_in + n_out], refs[n_in + n_out:])
        for cp in cps:
            cp.start()
        for cp in cps:
            cp.wait()

    return pl.pallas_call(
        body, out_shape=list(ex.out_shapes), in_specs=[HBM_SPEC] * n_in, out_specs=[HBM_SPEC] * n_out,
        scratch_shapes=_sem_scratch(ex.n_remote, ex.n_local), input_output_aliases=dict(ex.aliases), name=name,
    )(*ex.ins)


def _half_chunks(arrs, row_axis):
    plan = []
    for i, a in enumerate(arrs):
        rh = a.shape[row_axis] // 2
        row_bytes = a.dtype.itemsize * math.prod(a.shape) // a.shape[row_axis]
        nch = _n_chunks(rh, row_bytes)
        plan += [(i, q * (rh // nch), rh // nch) for q in range(nch)]
    return plan


def _sibling_halves(gs):
    plan = _half_chunks(gs, 1)

    def copies_of(in_refs, out_refs, place):
        x, y, c = place
        out = []
        for i, r0, nr in plan:
            rh = gs[i].shape[1] // 2
            out.append((in_refs[i].at[:, pl.ds((1 - c) * rh + r0, nr), :], out_refs[i].at[:, pl.ds(r0, nr), :],
                        (x, y, 1 - c)))
        return out, []

    shapes = tuple(jax.ShapeDtypeStruct((g.shape[0], g.shape[1] // 2, g.shape[2]), g.dtype) for g in gs)
    return Exchange(tuple(gs), shapes, len(plan), 0, copies_of)


def _chip_exchange(s1s):
    plan = _half_chunks([jax.ShapeDtypeStruct((2 * s.shape[1], s.shape[2]), s.dtype) for s in s1s], 0)

    def copies_of(in_refs, out_refs, place):
        x, y, c = place
        out = []
        for i, r0, nr in plan:
            for j, chip in enumerate(_other_chips(x, y)):
                out.append((in_refs[i].at[2 * chip[0] + chip[1], pl.ds(r0, nr), :], out_refs[i].at[j, pl.ds(r0, nr), :],
                            (*chip, c)))
        return out, []

    shapes = tuple(jax.ShapeDtypeStruct((3, *s.shape[1:]), s.dtype) for s in s1s)
    return Exchange(tuple(s1s), shapes, 3 * len(plan), 0, copies_of)


def _join_halves(gs):
    plan = _half_chunks(gs, 0)

    def copies_of(in_refs, out_refs, place):
        x, y, c = place
        out = []
        for i, r0, nr in plan:
            rows = out_refs[i].at[pl.ds(c * (gs[i].shape[0] // 2) + r0, nr), :]
            out.append((rows, rows, (x, y, 1 - c)))
        return out, []

    shapes = tuple(jax.ShapeDtypeStruct(g.shape, g.dtype) for g in gs)
    aliases = tuple((i, i) for i in range(len(gs)))
    return Exchange(tuple(gs), shapes, len(plan), 0, copies_of, aliases)


def _row_block(rows, cols):
    tb = rows
    while tb % 32 == 0 and tb * cols * 4 > (2 << 20):
        tb //= 2
    return tb


def _pair_sum(g, sib, core, *, name):
    ns, r, c = g.shape
    rh = r // 2
    tb = _row_block(rh, c)
    nb = rh // tb

    def body(core_ref, g_ref, s_ref, o_ref):
        o_ref[...] = (g_ref[...].astype(F32) + s_ref[...].astype(F32)).astype(o_ref.dtype)

    mine = pl.BlockSpec((None, tb, c), lambda s, i, core_ref: (s, core_ref[0] * nb + i, 0))
    half = pl.BlockSpec((None, tb, c), lambda s, i, core_ref: (s, i, 0))
    return pl.pallas_call(
        body, grid_spec=pltpu.PrefetchScalarGridSpec(num_scalar_prefetch=1, grid=(ns, nb), in_specs=[mine, half],
                                                     out_specs=half),
        out_shape=jax.ShapeDtypeStruct((ns, rh, c), BF16), compiler_params=_cp(("parallel", "parallel")), name=name,
    )(core, g, sib)


def _chip_sum(s1, rcv, where, *, name):
    _, rh, c = s1.shape
    tb = _row_block(rh, c)
    nb = rh // tb

    def body(where_ref, own_ref, r0_ref, r1_ref, r2_ref, o_ref):
        acc = own_ref[...].astype(F32)
        for r in (r0_ref, r1_ref, r2_ref):
            acc = acc + r[...].astype(F32)
        o_ref[...] = acc

    own = pl.BlockSpec((None, tb, c), lambda i, w: (w[1], i, 0))
    got = [pl.BlockSpec((None, tb, c), functools.partial(lambda i, w, j: (j, i, 0), j=j)) for j in range(3)]
    return pl.pallas_call(
        body, grid_spec=pltpu.PrefetchScalarGridSpec(
            num_scalar_prefetch=1, grid=(nb,), in_specs=[own] + got,
            out_specs=pl.BlockSpec((tb, c), lambda i, w: (w[0] * nb + i, 0))),
        out_shape=jax.ShapeDtypeStruct((2 * rh, c), F32), compiler_params=_cp(("parallel",)), name=name,
    )(where, s1, rcv, rcv, rcv)


def _all_gather_small(blk, *, name):
    r = blk.shape[0]

    def body(b_ref, out_ref, send_sems, recv_sems):
        x, y, c = _place()
        me = 4 * x + 2 * y + c
        out_ref[me] = b_ref[...]
        sends = []
        for k in range(1, N_DEV):
            peer = (x ^ (k >> 2), y ^ ((k >> 1) & 1), c ^ (k & 1))
            sends.append(pltpu.make_async_remote_copy(src_ref=b_ref, dst_ref=out_ref.at[me], send_sem=send_sems.at[k - 1],
                                                      recv_sem=recv_sems.at[k - 1], device_id=peer, device_id_type=MESH))
        for cp in sends:
            cp.start()
        for k in range(1, N_DEV):
            rows = out_ref.at[me ^ k]
            pltpu.make_async_remote_copy(src_ref=rows, dst_ref=rows, send_sem=send_sems.at[k - 1],
                                         recv_sem=recv_sems.at[k - 1], device_id=(x, y, c), device_id_type=MESH).wait_recv()
        for cp in sends:
            cp.wait_send()

    return pl.pallas_call(
        body, out_shape=jax.ShapeDtypeStruct((N_DEV, r, LANES), blk.dtype), in_specs=[VMEM_SPEC], out_specs=VMEM_SPEC,
        scratch_shapes=[pltpu.SemaphoreType.DMA((N_DEV - 1,)), pltpu.SemaphoreType.DMA((N_DEV - 1,))],
        name=name,
    )(blk)


def _sum_rows(parts, out_dtype, *, name, tb=1024):
    rows = parts[0].shape[0]
    tb = _blk(rows, tb)

    def body(*refs):
        acc = refs[0][...].astype(F32)
        for r in refs[1:-1]:
            acc = acc + r[...].astype(F32)
        refs[-1][...] = acc.astype(refs[-1].dtype)

    spec = pl.BlockSpec((tb, LANES), lambda i: (i, 0))
    return pl.pallas_call(
        body, grid=(rows // tb,), in_specs=[spec] * len(parts), out_specs=spec,
        out_shape=jax.ShapeDtypeStruct((rows, LANES), out_dtype), compiler_params=_cp(("parallel",)), name=name,
    )(*parts)


BIG = (
    ("w_in", 1), ("w_mem_kv", 0), ("w_swa_up", 1), ("w_gdn_up", 1), ("w_xa_up", 1), ("w_out", 0), ("w_mlp_in", 1),
    ("w_mlp_out", 0))


class _Comm:
    MLP = ("w_mlp_in", "w_mlp_out")

    def __init__(self, late_shards, core, where):
        self.axis = dict(BIG)
        self.late_shards = late_shards
        self.mid = [k for k in late_shards if k not in self.MLP and k != "w_in"]
        self.core, self.where = core, where

    def gather_exchange(self, names):
        return _gather_over_ici([self.late_shards[k] for k in names])

    def as_weights(self, names, whole):
        return {k: (g.reshape(-1, g.shape[2]) if self.axis[k] == 0 else g) for k, g in zip(names, whole)}

    def gathered(self, names, landed, tag):
        return self.as_weights(names, _exchange_call(_gather_pass_on(landed), name=f"ag_{tag}_pass"))

    def shard_major(self, k, grad):
        return grad.reshape(N_CHIPS, -1, grad.shape[-1]) if self.axis[k] == 0 else grad

    def pair_sums(self, gs, tag, sibs=None):
        if sibs is None:
            sibs = _exchange_call(_sibling_halves(gs), name=f"rs_sibling_{tag}")
        return [_pair_sum(g, s, self.core, name=f"rs_pair_sum_{tag}{i}") for i, (g, s) in enumerate(zip(gs, sibs))]

    def chip_sums(self, s1s, rcvs):
        return [_chip_sum(s1, rcv, self.where, name=f"rs_chip_sum_{i}") for i, (s1, rcv) in enumerate(zip(s1s, rcvs))]
SMALL = ("g_mix", "sinks", "a_log", "dt_bias", "gdn_norm_w", "g_mem", "g_mlp", "g_final")


def _rows128(a, rows):
    flat = a.reshape(-1)
    return jnp.pad(flat, (0, rows * LANES - flat.shape[0])).reshape(rows, LANES)


def kernel(x, mem, g_mix, w_in, sinks, conv_w, a_log, dt_bias, gdn_norm_w, g_mem, w_mem_kv, w_swa_up, w_gdn_up, w_xa_up, w_out, g_mlp, w_mlp_in, w_mlp_out, g_final, loss_target, m_g_mix, m_w_in, m_sinks, m_conv_w, m_a_log, m_dt_bias, m_gdn_norm_w, m_g_mem, m_w_mem_kv, m_w_swa_up, m_w_gdn_up, m_w_xa_up, m_w_out, m_g_mlp, m_w_mlp_in, m_w_mlp_out, m_g_final, v_g_mix, v_w_in, v_sinks, v_conv_w, v_a_log, v_dt_bias, v_gdn_norm_w, v_g_mem, v_w_mem_kv, v_w_swa_up, v_w_gdn_up, v_w_xa_up, v_w_out, v_g_mlp, v_w_mlp_in, v_w_mlp_out, v_g_final):
    given = dict(locals())
    xi, yi, ci = _place()
    chip = 2 * xi + yi
    core = jnp.reshape(ci, (1,)).astype(jnp.int32)
    where = jnp.stack([ci, chip]).astype(jnp.int32)

    comm = _Comm({k: given[k][0].astype(BF16) for k, _ in BIG}, core, where)
    wts = {}
    conv_shard = conv_w[0]
    conv_rows = -(-conv_shard.size // (8 * LANES)) * 8
    conv_all = _all_gather_small(_rows128(conv_shard, conv_rows), name="ag_conv")
    conv_full = jnp.concatenate(
        [conv_all[2 * s].reshape(-1)[:conv_shard.size].reshape(conv_shard.shape) for s in range(N_CHIPS)], axis=1)

    small = {k: given[k].reshape(1, -1) for k in SMALL}
    small["conv_w"] = conv_full
    loss_row, dx, grads = _local_step(x[0], mem[0], loss_target[0], wts, small, comm)
    big_grads = {k: grads[k] for k, _ in BIG}

    layout = [("loss", loss_row[:, :1])] + [(k, grads[k]) for k in SMALL] + [("conv_w", grads["conv_w"])]
    rows = [-(-a.size // LANES) for _, a in layout]
    blk_rows = -(-sum(rows) // 8) * 8
    blk = jnp.concatenate([_rows128(a.astype(F32), n) for (_, a), n in zip(layout, rows)]
                          + [jnp.zeros((blk_rows - sum(rows), LANES), F32)], axis=0)
    gathered = _all_gather_small(blk, name="ag_small_grads")
    reduced = _sum_rows([gathered[i] for i in range(N_DEV)], F32, name="small_grad_sum")
    small_grads, start = {}, 0
    for (k, a), n in zip(layout, rows):
        small_grads[k] = reduced[start:start + n].reshape(-1)[:a.size].reshape(a.shape)
        start += n
    loss = small_grads["loss"].reshape(())
    cw = conv_shard.shape[1]
    conv_grad = lax.dynamic_slice_in_dim(small_grads["conv_w"], chip * cw, cw, axis=1)

    names = ["g_mix", "w_in", "sinks", "conv_w", "a_log", "dt_bias", "gdn_norm_w", "g_mem", "w_mem_kv", "w_swa_up",
             "w_gdn_up", "w_xa_up", "w_out", "g_mlp", "w_mlp_in", "w_mlp_out", "g_final"]
    out_g, out_d, out_m, out_v = [], [], [], []
    for k in names:
        w, m, v = given[k], given["m_" + k], given["v_" + k]
        if k in big_grads:
            g2 = big_grads[k]
        elif k == "conv_w":
            g2 = conv_grad
        else:
            g2 = small_grads[k]
        as_given = (lambda a: a.reshape(1, -1)) if w.ndim == 1 else (lambda a: a)
        if w.shape[-1] % LANES and w.shape[-1] > LANES:
            tr = lambda a: jnp.swapaxes(a, -1, -2)
            delta, new_m, new_v = (tr(a) for a in _adamw(tr(w), tr(g2), tr(m), tr(v), name="adamw_" + k))
        else:
            delta, new_m, new_v = _adamw(as_given(w), g2, as_given(m), as_given(v), name="adamw_" + k)
        out_g.append(g2.reshape(w.shape))
        out_d.append(delta.reshape(w.shape))
        out_m.append(new_m.reshape(w.shape))
        out_v.append(new_v.reshape(w.shape))
    return (loss, dx[None], *out_g, *out_d, *out_m, *out_v)
```

```python
import functools
import math
from typing import Callable, NamedTuple

import jax
import jax.numpy as jnp
from jax import lax
from jax.experimental import pallas as pl
from jax.experimental.pallas import tpu as pltpu

F32 = jnp.float32
BF16 = jnp.bfloat16
HI = lax.Precision.HIGHEST
MESH = pl.DeviceIdType.MESH

SWA_Q_HEADS = 16
SWA_KV_HEADS = 2
SWA_HEAD_DIM = 64
SWA_WINDOW = 128
SWA_SCALE = SWA_HEAD_DIM ** -0.5
assert math.frexp(SWA_SCALE)[0] == 0.5
GDN_HEADS = 4
GDN_HEAD_DIM = 128
GDN_CONV = 4
GDN_CHUNK = 64
XA_HEADS = 4
XA_HEAD_DIM = 128
RMS_EPS = 1e-6
L2_EPS = 1e-6
ADAM_LR = 0.001
ADAM_B1 = 0.9
ADAM_B2 = 0.999
ADAM_EPS = 1e-08
ADAM_WD = 0.01
ADAM_STEP = 10

LANES = 128
N_SHARDS = 4
VMEM_LIMIT = 56 * 1024 * 1024

NT = (((1,), (1,)), ((), ()))
TN = (((0,), (0,)), ((), ()))
NN = (((1,), (0,)), ((), ()))


def _cp(sem=None):
    return pltpu.CompilerParams(dimension_semantics=sem, vmem_limit_bytes=VMEM_LIMIT)


def _blk(dim, pref):
    if dim <= pref:
        return dim
    b = (pref // LANES) * LANES
    while dim % b:
        b -= LANES
    assert b > 0, (dim, pref)
    return b


def _dot(a, b, dims=NN, precision=None):
    return lax.dot_general(a, b, dims, precision=precision, preferred_element_type=F32)


def _sigmoid(x):
    return 0.5 * jnp.tanh(0.5 * x) + 0.5


MM_TK_BYTES = 4096


def _mm(a, b, *, name, ta=False, tb=False, out_dtypes=(F32,), epilogue=None, extras=(), tm=1024, tn=1024, tk=None,
        b_sharded=False, out_sharded=False, b_window=None, exchange=None):
    (kdim, m) = a.shape if ta else a.shape[::-1]
    col0 = 0
    n_lim = k_lim = None
    if b_sharded:
        ns, rows_w, per = b.shape
        if tb:
            kb, n, k_lim = ns * per, rows_w, per
        else:
            kb, n, n_lim = rows_w, ns * per, per
    else:
        (kb, n) = b.shape[::-1] if tb else b.shape
        if b_window is not None:
            assert not tb
            col0, n = b_window
    assert kdim == kb, (a.shape, b.shape, ta, tb)
    if out_sharded:
        assert n % N_SHARDS == 0
        n_lim = n // N_SHARDS if n_lim is None else n_lim
        assert n_lim == n // N_SHARDS
    if tk is None:
        tk = MM_TK_BYTES // max(a.dtype.itemsize, b.dtype.itemsize)
    tm, tn, tk = _blk(m, tm), _blk(n_lim or n, tn), _blk(k_lim or kdim, tk)
    assert col0 % tn == 0, (col0, tn)
    nk = kdim // tk
    a_spec = pl.BlockSpec((tk, tm), lambda i, j, k: (k, i)) if ta else pl.BlockSpec((tm, tk), lambda i, j, k: (i, k))
    if b_sharded and tb:
        kpb = k_lim // tk
        b_spec = pl.BlockSpec((None, tn, tk), lambda i, j, k: (k // kpb, j, k % kpb))
    elif b_sharded:
        bpb = n_lim // tn
        b_spec = pl.BlockSpec((None, tk, tn), lambda i, j, k: (j // bpb, k, j % bpb))
    elif tb:
        b_spec = pl.BlockSpec((tn, tk), lambda i, j, k: (j, k))
    else:
        b_spec = pl.BlockSpec((tk, tn), lambda i, j, k: (k, j + col0 // tn))
    x_spec = pl.BlockSpec((tm, tn), lambda i, j, k: (i, j))
    if out_sharded:
        opb = n_lim // tn
        o_spec = pl.BlockSpec((None, tm, tn), lambda i, j, k: (j // opb, i, j % opb))
        out_shape = (N_SHARDS, m, n_lim)
    else:
        o_spec, out_shape = x_spec, (m, n)
    dims = ((((0 if ta else 1),), ((1 if tb else 0),)), ((), ()))
    n_extra, n_out = len(extras), len(out_dtypes)

    host = _ExchangeHost(exchange)
    grid = (m // tm, n // tn, nk)

    def body(*refs):
        a_ref, b_ref = refs[:2]
        extra_refs = refs[2:2 + n_extra]
        out_refs = refs[2 + n_extra + host.n_in:2 + n_extra + host.n_in + n_out]
        host.start(refs, 2 + n_extra, 2 + n_extra + host.n_in + n_out, grid)
        part = _dot(a_ref[...].astype(BF16), b_ref[...].astype(BF16), dims)

        def finish(acc):
            vals = epilogue(acc, *[r[...] for r in extra_refs]) if epilogue is not None else (acc,) * n_out
            assert len(vals) == n_out
            for r, v in zip(out_refs, vals):
                r[...] = v.astype(r.dtype)

        if nk == 1:
            finish(part)
        else:
            acc_ref = refs[2 + n_extra + host.n_in + n_out + host.n_out]
            k = pl.program_id(2)

            @pl.when(k == 0)
            def _():
                acc_ref[...] = part

            @pl.when((k > 0) & (k < nk - 1))
            def _():
                acc_ref[...] += part

            @pl.when(k == nk - 1)
            def _():
                finish(acc_ref[...] + part)

        host.wait(refs, 2 + n_extra, 2 + n_extra + host.n_in + n_out, grid)

    outs = pl.pallas_call(
        body,
        grid=grid,
        in_specs=[a_spec, b_spec] + [x_spec] * n_extra + host.in_specs,
        out_specs=[o_spec] * n_out + host.out_specs,
        out_shape=[jax.ShapeDtypeStruct(out_shape, d) for d in out_dtypes] + host.out_shapes,
        scratch_shapes=([pltpu.VMEM((tm, tn), F32)] if nk > 1 else []) + host.scratch,
        input_output_aliases=host.aliases(2 + n_extra, n_out),
        compiler_params=_cp(host.semantics(("parallel", "parallel", "arbitrary"))),
        name=name,
    )(a, b, *extras, *host.ins)
    mine, landed = outs[:n_out], list(outs[n_out:])
    mine = mine[0] if n_out == 1 else mine
    return (mine, landed) if exchange is not None else mine


class Exchange(NamedTuple):
    ins: tuple
    out_shapes: tuple
    n_remote: int
    n_local: int
    copies_of: Callable
    aliases: tuple = ()


def _exchange_copies(ex, in_refs, out_refs, sem_refs):
    send_sems, recv_sems, local_sems = sem_refs
    remote, local = ex.copies_of(in_refs, out_refs, _place())
    assert len(remote) == ex.n_remote and len(local) == ex.n_local, (len(remote), len(local))
    cps = [pltpu.make_async_remote_copy(src_ref=src, dst_ref=dst, send_sem=send_sems.at[k], recv_sem=recv_sems.at[k],
                                        device_id=to, device_id_type=MESH) for k, (src, dst, to) in enumerate(remote)]
    cps += [pltpu.make_async_copy(src, dst, local_sems.at[k]) for k, (src, dst) in enumerate(local)]
    return cps


class _ExchangeHost:
    def __init__(self, ex):
        self.ex = ex
        self.ins = list(ex.ins) if ex else []
        self.out_shapes = list(ex.out_shapes) if ex else []
        self.n_in, self.n_out = len(self.ins), len(self.out_shapes)
        self.in_specs = [HBM_SPEC] * self.n_in
        self.out_specs = [HBM_SPEC] * self.n_out
        self.scratch = _sem_scratch(ex.n_remote, ex.n_local) if ex else []

    def semantics(self, sem):
        return tuple("arbitrary" for _ in sem) if self.ex else sem

    def aliases(self, in_at, out_at):
        return {in_at + i: out_at + o for i, o in self.ex.aliases} if self.ex else {}

    def _refs(self, refs, in_at, out_at):
        return refs[in_at:in_at + self.n_in], refs[out_at:out_at + self.n_out], refs[len(refs) - 3:]

    def _when(self, grid, last):
        cond = None
        for d, size in enumerate(grid):
            c = pl.program_id(d) == (size - 1 if last else 0)
            cond = c if cond is None else cond & c
        return cond

    def start(self, refs, in_at, out_at, grid):
        if self.ex:
            @pl.when(self._when(grid, False))
            def _():
                for cp in _exchange_copies(self.ex, *self._refs(refs, in_at, out_at)):
                    cp.start()

    def wait(self, refs, in_at, out_at, grid):
        if self.ex:
            @pl.when(self._when(grid, True))
            def _():
                for cp in _exchange_copies(self.ex, *self._refs(refs, in_at, out_at)):
                    cp.wait()


def _rms_fwd(x, g, *, name, tm=512, exchange=None):
    t, d = x.shape
    tm = _blk(t, tm)
    host = _ExchangeHost(exchange)
    grid = (t // tm,)

    def body(*refs):
        x_ref, g_ref, n_ref = refs[0], refs[1], refs[2 + host.n_in]
        host.start(refs, 2, 3 + host.n_in, grid)
        xv = x_ref[...]
        r = lax.rsqrt(jnp.mean(xv * xv, axis=-1, keepdims=True) + RMS_EPS)
        n_ref[...] = (xv * r * g_ref[...]).astype(n_ref.dtype)
        host.wait(refs, 2, 3 + host.n_in, grid)

    outs = pl.pallas_call(
        body, grid=grid,
        in_specs=[pl.BlockSpec((tm, d), lambda i: (i, 0)), pl.BlockSpec((1, d), lambda i: (0, 0))] + host.in_specs,
        out_specs=[pl.BlockSpec((tm, d), lambda i: (i, 0))] + host.out_specs,
        out_shape=[jax.ShapeDtypeStruct((t, d), BF16)] + host.out_shapes,
        scratch_shapes=host.scratch, input_output_aliases=host.aliases(2, 1),
        compiler_params=_cp(host.semantics(("parallel",))), name=name,
    )(x, g, *host.ins)
    return (outs[0], list(outs[1:])) if exchange is not None else outs[0]


def _rms_bwd(dn, x, g, dres, *, name, tm=512, exchange=None):
    t, d = x.shape
    tm = _blk(t, tm)
    host = _ExchangeHost(exchange)
    grid = (t // tm,)

    def body(*refs):
        dn_ref, x_ref, g_ref, dres_ref = refs[:4]
        dx_ref, dxb_ref, dg_ref = refs[4 + host.n_in:7 + host.n_in]
        host.start(refs, 4, 7 + host.n_in, grid)
        i = pl.program_id(0)
        xv = x_ref[...]
        r = lax.rsqrt(jnp.mean(xv * xv, axis=-1, keepdims=True) + RMS_EPS)
        xh = xv * r
        dnv = dn_ref[...].astype(F32)
        dxh = dnv * g_ref[...]
        dx = dres_ref[...] + r * (dxh - xh * jnp.mean(dxh * xh, axis=-1, keepdims=True))
        dx_ref[...] = dx
        dxb_ref[...] = dx.astype(dxb_ref.dtype)
        part = jnp.sum(dnv * xh, axis=0, keepdims=True)

        @pl.when(i == 0)
        def _():
            dg_ref[...] = part

        @pl.when(i > 0)
        def _():
            dg_ref[...] += part

        host.wait(refs, 4, 7 + host.n_in, grid)

    row = pl.BlockSpec((tm, d), lambda i: (i, 0))
    vec = pl.BlockSpec((1, d), lambda i: (0, 0))
    outs = pl.pallas_call(
        body, grid=grid,
        in_specs=[row, row, vec, row] + host.in_specs, out_specs=[row, row, vec] + host.out_specs,
        out_shape=[jax.ShapeDtypeStruct((t, d), F32), jax.ShapeDtypeStruct((t, d), BF16),
                   jax.ShapeDtypeStruct((1, d), F32)] + host.out_shapes,
        scratch_shapes=host.scratch, input_output_aliases=host.aliases(4, 3),
        compiler_params=_cp(("arbitrary",)), name=name,
    )(dn, x, g, dres, *host.ins)
    return (*outs[:3], list(outs[3:])) if exchange is not None else outs


def _final_loss(h, g, tgt, *, name, tm=512):
    t, d = h.shape
    tm = _blk(t, tm)

    def body(h_ref, g_ref, t_ref, dh_ref, dhb_ref, dg_ref, loss_ref):
        i = pl.program_id(0)
        hv = h_ref[...]
        r = lax.rsqrt(jnp.mean(hv * hv, axis=-1, keepdims=True) + RMS_EPS)
        xh = hv * r
        e = xh * g_ref[...] - t_ref[...]
        dy = e * (1.0 / d)
        dxh = dy * g_ref[...]
        dh = r * (dxh - xh * jnp.mean(dxh * xh, axis=-1, keepdims=True))
        dh_ref[...] = dh
        dhb_ref[...] = dh.astype(dhb_ref.dtype)
        dg_part = jnp.sum(dy * xh, axis=0, keepdims=True)
        row_loss = jnp.sum(e * e, axis=-1, keepdims=True) * (0.5 / d)
        loss_part = jnp.sum(row_loss, axis=0, keepdims=True)

        @pl.when(i == 0)
        def _():
            dg_ref[...] = dg_part
            loss_ref[...] = jnp.broadcast_to(loss_part, loss_ref.shape)

        @pl.when(i > 0)
        def _():
            dg_ref[...] += dg_part
            loss_ref[...] += jnp.broadcast_to(loss_part, loss_ref.shape)

    row = pl.BlockSpec((tm, d), lambda i: (i, 0))
    vec = pl.BlockSpec((1, d), lambda i: (0, 0))
    return pl.pallas_call(
        body, grid=(t // tm,),
        in_specs=[row, vec, row], out_specs=[row, row, vec, pl.BlockSpec((1, LANES), lambda i: (0, 0))],
        out_shape=[jax.ShapeDtypeStruct((t, d), F32), jax.ShapeDtypeStruct((t, d), BF16),
                   jax.ShapeDtypeStruct((1, d), F32), jax.ShapeDtypeStruct((1, LANES), F32)],
        compiler_params=_cp(("arbitrary",)), name=name,
    )(h, g, tgt)


SWA_SUB = 64


def _swa_mask(n, rows, row0):
    w = SWA_WINDOW
    qi = (lax.broadcasted_iota(jnp.int32, (rows, 2 * w), 0) + row0) & (w - 1)
    kj = lax.broadcasted_iota(jnp.int32, (rows, 2 * w), 1)
    return (kj > qi) & (kj <= qi + w) & ((n > 0) | (kj >= w))


def _stack_heads(ref, heads, width):
    return jnp.concatenate([ref[:, h * width:(h + 1) * width] for h in heads], axis=0)


def _stack_scalars(ref, heads, rows):
    return jnp.concatenate([jnp.broadcast_to(ref[0:1, h:h + 1], (rows, 1)) for h in heads], axis=0)


def _swa_fwd(q, kv, sinks, *, name, exchange=None):
    t = q.shape[0]
    w, hd, hq, hkv = SWA_WINDOW, SWA_HEAD_DIM, SWA_Q_HEADS, SWA_KV_HEADS
    grp = hq // hkv
    kvw = hkv * hd
    nb = t // w
    host = _ExchangeHost(exchange)
    assert not (exchange and exchange.aliases)

    def body(*refs):
        q_ref, kvp_ref, kvc_ref, s_ref = refs[:4]
        o_ref, lse_ref = refs[4 + host.n_in:6 + host.n_in]
        host.start(refs, 4, 6 + host.n_in, (nb,))
        n = pl.program_id(0)
        mask = _swa_mask(n, grp * w, 0)
        kvcat = jnp.concatenate([kvp_ref[...], kvc_ref[...]], axis=0)
        outs, lses = [], []
        for hk in range(hkv):
            heads = range(hk * grp, (hk + 1) * grp)
            qs = _stack_heads(q_ref, heads, hd)
            kh = kvcat[:, hk * hd:(hk + 1) * hd]
            vh = kvcat[:, kvw + hk * hd:kvw + (hk + 1) * hd]
            sk = _stack_scalars(s_ref, heads, w)
            s = jnp.where(mask, _dot(qs * SWA_SCALE, kh, NT), -jnp.inf)
            m = jnp.maximum(jnp.max(s, axis=-1, keepdims=True), sk)
            p = jnp.exp(s - m)
            den = jnp.sum(p, axis=-1, keepdims=True) + jnp.exp(sk - m)
            o = _dot((p * (1.0 / den)).astype(BF16), vh)
            lse = m + jnp.log(den)
            outs += [o[j * w:(j + 1) * w] for j in range(grp)]
            lses += [lse[j * w:(j + 1) * w] for j in range(grp)]
        o_ref[...] = jnp.concatenate(outs, axis=1).astype(o_ref.dtype)
        lse_ref[...] = jnp.concatenate(lses, axis=1)
        host.wait(refs, 4, 6 + host.n_in, (nb,))

    outs = pl.pallas_call(
        body, grid=(nb,),
        in_specs=[pl.BlockSpec((w, hq * hd), lambda i: (i, 0)),
                  pl.BlockSpec((w, 2 * kvw), lambda i: (jnp.maximum(i - 1, 0), 0)),
                  pl.BlockSpec((w, 2 * kvw), lambda i: (i, 0)),
                  pl.BlockSpec((1, hq), lambda i: (0, 0))] + host.in_specs,
        out_specs=[pl.BlockSpec((w, hq * hd), lambda i: (i, 0)), pl.BlockSpec((w, hq), lambda i: (i, 0))] + host.out_specs,
        out_shape=[jax.ShapeDtypeStruct((t, hq * hd), BF16), jax.ShapeDtypeStruct((t, hq), F32)] + host.out_shapes,
        scratch_shapes=host.scratch,
        compiler_params=_cp(host.semantics(("parallel",))), name=name,
    )(q, kv, kv, sinks, *host.ins)
    return (outs[0], outs[1], list(outs[2:])) if exchange is not None else outs


ANY_SPEC = pl.BlockSpec(memory_space=pl.ANY)


def _swa_bwd(q, kv, sinks, o, lse, do, dp, dp_col, *, name):
    t = q.shape[0]
    w, hd, hq, hkv = SWA_WINDOW, SWA_HEAD_DIM, SWA_Q_HEADS, SWA_KV_HEADS
    grp = hq // hkv
    kvw = hkv * hd
    nb = t // w
    assert dp_col % (hq * hd) == 0
    dq_blk = dp_col // (hq * hd)

    def body(q_ref, kvp_ref, kvc_ref, s_ref, o_ref, lse_ref, do_ref, _, dq_ref, dkv_ref, ds_ref, carry_ref, s_scr, dp_scr,
             p_scr, ds_scr):
        n = pl.program_id(0)

        @pl.when(n == 0)
        def _():
            ds_ref[...] = jnp.zeros_like(ds_ref)
            carry_ref[...] = jnp.zeros_like(carry_ref)

        @pl.when(n < nb)
        def _():
            kvcat = jnp.concatenate([kvp_ref[...], kvc_ref[...]], axis=0)
            dqs, dsk, dks, dvs = [], [], [], []
            for hk in range(hkv):
                heads = range(hk * grp, (hk + 1) * grp)
                qs = _stack_heads(q_ref, heads, hd)
                dos = _stack_heads(do_ref, heads, hd)
                os_ = _stack_heads(o_ref, heads, hd)
                lse = _stack_heads(lse_ref, heads, 1)
                kh = kvcat[:, hk * hd:(hk + 1) * hd]
                vh = kvcat[:, kvw + hk * hd:kvw + (hk + 1) * hd]
                delta = jnp.sum(dos.astype(F32) * os_.astype(F32), axis=-1, keepdims=True)
                s_scr[...] = _dot(qs * SWA_SCALE, kh, NT)
                dp_scr[...] = _dot(dos, vh, NT)
                for r0 in range(0, grp * w, SWA_SUB):
                    rows = slice(r0, r0 + SWA_SUB)
                    p = jnp.exp(jnp.where(_swa_mask(n, SWA_SUB, r0 % w), s_scr[rows, :], -jnp.inf) - lse[rows])
                    p_scr[rows, :] = p.astype(p_scr.dtype)
                    ds_scr[rows, :] = (p * (dp_scr[rows, :] - delta[rows]) * SWA_SCALE).astype(ds_scr.dtype)
                ds = ds_scr[...]
                dq = _dot(ds, kh)
                dqs += [dq[j * w:(j + 1) * w] for j in range(grp)]
                dks.append(_dot(ds, qs, TN))
                dvs.append(_dot(p_scr[...], dos, TN))
                dsink = -jnp.exp(_stack_scalars(s_ref, heads, w) - lse) * delta
                dsk += [jnp.sum(dsink[j * w:(j + 1) * w], axis=0, keepdims=True) for j in range(grp)]
            dq_ref[...] = jnp.concatenate(dqs, axis=1).astype(dq_ref.dtype)
            ds_ref[...] += jnp.concatenate(dsk, axis=1)
            dkv_cat = jnp.concatenate(dks + dvs, axis=1)
            dkv_ref[...] = (carry_ref[...] + dkv_cat[:w]).astype(dkv_ref.dtype)
            carry_ref[...] = dkv_cat[w:]

        @pl.when(n == nb)
        def _():
            dkv_ref[...] = carry_ref[...].astype(dkv_ref.dtype)

    cur = lambda i: (jnp.minimum(i, nb - 1), 0)
    prev = lambda i: (jnp.clip(i - 1, 0, nb - 1), 0)
    return pl.pallas_call(
        body, grid=(nb + 1,),
        in_specs=[pl.BlockSpec((w, hq * hd), cur), pl.BlockSpec((w, 2 * kvw), prev), pl.BlockSpec((w, 2 * kvw), cur),
                  pl.BlockSpec((1, hq), lambda i: (0, 0)), pl.BlockSpec((w, hq * hd), cur),
                  pl.BlockSpec((w, hq), cur), pl.BlockSpec((w, hq * hd), cur), ANY_SPEC],
        out_specs=[pl.BlockSpec((w, hq * hd), lambda i: (jnp.minimum(i, nb - 1), dq_blk)),
                   pl.BlockSpec((w, 2 * kvw), prev), pl.BlockSpec((1, hq), lambda i: (0, 0))],
        out_shape=[jax.ShapeDtypeStruct(dp.shape, dp.dtype), jax.ShapeDtypeStruct((t, 2 * kvw), BF16),
                   jax.ShapeDtypeStruct((1, hq), F32)],
        scratch_shapes=[pltpu.VMEM((w, 2 * kvw), F32)] + [pltpu.VMEM((grp * w, 2 * w), dt) for dt in (F32, F32, BF16, BF16)],
        input_output_aliases={7: 0},
        compiler_params=_cp(("arbitrary",)), name=name,
    )(q, kv, kv, sinks, o, lse, do, dp)


def _xa_fwd(q, mkv, *, name, tq=512):
    t, xw = q.shape
    nm = mkv.shape[0]
    hd, nh = XA_HEAD_DIM, XA_HEADS
    tq = _blk(t, tq)

    def body(q_ref, mkv_ref, o_ref):
        outs = []
        for h in range(nh):
            qh = q_ref[:, h * hd:(h + 1) * hd]
            kh = mkv_ref[:, h * hd:(h + 1) * hd]
            vh = mkv_ref[:, xw + h * hd:xw + (h + 1) * hd]
            s = _dot(qh, kh, NT) * (hd ** -0.5)
            p = jnp.exp(s - jnp.max(s, axis=-1, keepdims=True))
            p = p * (1.0 / jnp.sum(p, axis=-1, keepdims=True))
            outs.append(_dot(p.astype(BF16), vh))
        o_ref[...] = jnp.concatenate(outs, axis=1).astype(o_ref.dtype)

    return pl.pallas_call(
        body, grid=(t // tq,),
        in_specs=[pl.BlockSpec((tq, xw), lambda i: (i, 0)), pl.BlockSpec((nm, 2 * xw), lambda i: (0, 0))],
        out_specs=pl.BlockSpec((tq, xw), lambda i: (i, 0)),
        out_shape=jax.ShapeDtypeStruct((t, xw), BF16),
        compiler_params=_cp(("parallel",)), name=name,
    )(q, mkv)


def _xa_bwd(q, mkv, do, dp, dp_col, *, name, tq=512):
    t, xw = q.shape
    nm = mkv.shape[0]
    hd, nh = XA_HEAD_DIM, XA_HEADS
    tq = _blk(t, tq)
    assert dp_col % xw == 0

    def body(q_ref, mkv_ref, do_ref, _, dq_ref, dmkv_ref):
        i = pl.program_id(0)
        dqs, dks, dvs = [], [], []
        for h in range(nh):
            qh = q_ref[:, h * hd:(h + 1) * hd]
            kh = mkv_ref[:, h * hd:(h + 1) * hd]
            vh = mkv_ref[:, xw + h * hd:xw + (h + 1) * hd]
            doh = do_ref[:, h * hd:(h + 1) * hd]
            s = _dot(qh, kh, NT) * (hd ** -0.5)
            p = jnp.exp(s - jnp.max(s, axis=-1, keepdims=True))
            p = p * (1.0 / jnp.sum(p, axis=-1, keepdims=True))
            dp = _dot(doh, vh, NT)
            ds = (p * (dp - jnp.sum(p * dp, axis=-1, keepdims=True)) * (hd ** -0.5)).astype(BF16)
            dqs.append(_dot(ds, kh))
            dks.append(_dot(ds, qh, TN))
            dvs.append(_dot(p.astype(BF16), doh, TN))
        dq_ref[...] = jnp.concatenate(dqs, axis=1).astype(dq_ref.dtype)
        part = jnp.concatenate(dks + dvs, axis=1)

        @pl.when(i == 0)
        def _():
            dmkv_ref[...] = part

        @pl.when(i > 0)
        def _():
            dmkv_ref[...] += part

    row = pl.BlockSpec((tq, xw), lambda i: (i, 0))
    full = pl.BlockSpec((nm, 2 * xw), lambda i: (0, 0))
    return pl.pallas_call(
        body, grid=(t // tq,),
        in_specs=[row, full, row, ANY_SPEC],
        out_specs=[pl.BlockSpec((tq, xw), lambda i: (i, dp_col // xw)), full],
        out_shape=[jax.ShapeDtypeStruct(dp.shape, dp.dtype), jax.ShapeDtypeStruct((nm, 2 * xw), F32)],
        input_output_aliases={3: 0}, compiler_params=_cp(("arbitrary",)), name=name,
    )(q, mkv, do, dp)


def _merge_specs(ys, ws, tm):
    y_specs = [pl.BlockSpec((tm, y.shape[1]), lambda i: (i, 0)) for y in ys]
    w_specs = [pl.BlockSpec(w.shape, lambda i: (0, 0, 0)) for w in ws]
    return y_specs, w_specs


def _merge_tiles(ws, tn):
    ns, _, per = ws[0].shape
    tn = _blk(per, tn)
    return tn, [(s, c, s * per + c) for s in range(ns) for c in range(0, per, tn)]


def _merge_fwd(ys, ws, gates, *, name, tm=256, tn=512):
    t, d = ys[0].shape[0], ws[0].shape[0] * ws[0].shape[2]
    tm = _blk(t, tm)
    tn, tiles = _merge_tiles(ws, tn)
    y_specs, w_specs = _merge_specs(ys, ws, tm)

    def body(ya, yb, yc, wa, wb, wc, g_ref, o_ref):
        for s, c, col in tiles:
            acc = None
            for b, (y, w) in enumerate(((ya, wa), (yb, wb), (yc, wc))):
                term = _sigmoid(g_ref[:, b * d + col:b * d + col + tn]) * _dot(y[...], w[s, :, c:c + tn])
                acc = term if acc is None else acc + term
            o_ref[:, col:col + tn] = acc.astype(o_ref.dtype)

    return pl.pallas_call(
        body, grid=(t // tm,),
        in_specs=y_specs + w_specs + [pl.BlockSpec((tm, 3 * d), lambda i: (i, 0))],
        out_specs=pl.BlockSpec((tm, d), lambda i: (i, 0)),
        out_shape=jax.ShapeDtypeStruct((t, d), BF16),
        compiler_params=_cp(("parallel",)), name=name,
    )(*ys, *ws, gates)


def _merge_bwd(ys, ws, gates, dmerged, dp_width, *, name, tm=256, tn=512):
    t, d = ys[0].shape[0], ws[0].shape[0] * ws[0].shape[2]
    tm = _blk(t, tm)
    tn, tiles = _merge_tiles(ws, tn)
    y_specs, w_specs = _merge_specs(ys, ws, tm)
    row = pl.BlockSpec((tm, d), lambda i: (i, 0))
    wide = pl.BlockSpec((tm, 3 * d), lambda i: (i, 0))

    def body(ya, yb, yc, wa, wb, wc, g_ref, dm_ref, dua, dub, duc, dp_ref):
        for s, c, col in tiles:
            dm = dm_ref[:, col:col + tn]
            for b, (y, w, du) in enumerate(((ya, wa, dua), (yb, wb, dub), (yc, wc, duc))):
                sg = _sigmoid(g_ref[:, b * d + col:b * d + col + tn])
                u = _dot(y[...], w[s, :, c:c + tn])
                du[:, col:col + tn] = (dm * sg).astype(du.dtype)
                dp_ref[:, b * d + col:b * d + col + tn] = (dm * u * sg * (1.0 - sg)).astype(dp_ref.dtype)

    return pl.pallas_call(
        body, grid=(t // tm,),
        in_specs=y_specs + w_specs + [wide, row],
        out_specs=[row] * 3 + [wide],
        out_shape=[jax.ShapeDtypeStruct((t, d), BF16)] * 3 + [jax.ShapeDtypeStruct((t, dp_width), BF16)],
        compiler_params=_cp(("parallel",)), name=name,
    )(*ys, *ws, gates, dmerged)


def _adamw(w, g, m, v, *, name, tm=256):
    lead = w.ndim - 2
    assert all(s == 1 for s in w.shape[:lead]) and m.shape == w.shape and v.shape == w.shape
    r, c = w.shape[lead:]
    assert g.shape == (r, c)
    tm = _blk(r, tm) if r % 8 == 0 else r
    tc = c if tm * c * 4 <= (4 << 20) else _blk(c, 256)
    ncb = c // tc
    bc1 = 1.0 - ADAM_B1 ** ADAM_STEP
    bc2 = 1.0 - ADAM_B2 ** ADAM_STEP

    def body(w_ref, g_ref, m_ref, v_ref, d_ref, nm_ref, nv_ref):
        gv = g_ref[...]
        nm = ADAM_B1 * m_ref[...] + (1.0 - ADAM_B1) * gv
        nv = ADAM_B2 * v_ref[...] + (1.0 - ADAM_B2) * (gv * gv)
        d_ref[...] = -ADAM_LR * ((nm / bc1) / (jnp.sqrt(nv / bc2) + ADAM_EPS) + ADAM_WD * w_ref[...])
        nm_ref[...] = nm
        nv_ref[...] = nv

    spec = pl.BlockSpec((None,) * lead + (tm, tc), lambda i: (0,) * lead + (i // ncb, i % ncb))
    g_spec = pl.BlockSpec((tm, tc), lambda i: (i // ncb, i % ncb))
    return pl.pallas_call(
        body, grid=(r // tm * ncb,), in_specs=[spec, g_spec, spec, spec], out_specs=[spec] * 3,
        out_shape=[jax.ShapeDtypeStruct(w.shape, F32)] * 3,
        compiler_params=_cp(("parallel",)), name=name,
    )(w, g, m, v)


HALO = 8


def _shift_down(cur, prev, j):
    if j == 0:
        return cur
    y = pltpu.roll(cur, j, 0)
    row = lax.broadcasted_iota(jnp.int32, (HALO, cur.shape[1]), 0)
    top = jnp.where(row < j, pltpu.roll(prev, j, 0), y[:HALO])
    return jnp.concatenate([top, y[HALO:]], axis=0)


def _shift_up(cur, nxt, j):
    if j == 0:
        return cur
    tm = cur.shape[0]
    y = pltpu.roll(cur, tm - j, 0)
    row = lax.broadcasted_iota(jnp.int32, (HALO, cur.shape[1]), 0)
    bot = jnp.where(row >= HALO - j, pltpu.roll(nxt, HALO - j, 0), y[tm - HALO:])
    return jnp.concatenate([y[:tm - HALO], bot], axis=0)


def _softplus(x):
    return jnp.maximum(x, 0.0) + jnp.log(1.0 + jnp.exp(-jnp.abs(x)))


def _gdn_pre_fwd(qkvb, conv_w, ab, alog_pad, dt_pad, *, name, ab_blk=0, tm=256):
    t, cw = qkvb.shape
    hd, nh, ck = GDN_HEAD_DIM, GDN_HEADS, GDN_CHUNK
    gw = nh * hd
    tm = _blk(t, tm)
    hb = tm // HALO

    def body(x_ref, xp_ref, w_ref, ab_ref, al_ref, dt_ref, xc_ref, qkvn_ref, aux_ref):
        i = pl.program_id(0)
        cur = x_ref[...]
        prev = jnp.where(i > 0, xp_ref[...], 0.0)
        xc = None
        for tap in range(GDN_CONV):
            term = w_ref[tap:tap + 1, :] * _shift_down(cur, prev, GDN_CONV - 1 - tap)
            xc = term if xc is None else xc + term
        xc_ref[...] = xc
        s = xc * _sigmoid(xc)
        for h in range(2 * nh):
            xh = s[:, h * hd:(h + 1) * hd]
            r = lax.rsqrt(jnp.sum(xh * xh, axis=-1, keepdims=True) + L2_EPS)
            scale = hd ** -0.5 if h < nh else 1.0
            qkvn_ref[:, h * hd:(h + 1) * hd] = xh * (r * scale)
        qkvn_ref[:, 2 * gw:] = s[:, 2 * gw:]
        abv = ab_ref[...]
        lane = lax.broadcasted_iota(jnp.int32, abv.shape, 1)
        g = jnp.where(lane < nh, -jnp.exp(al_ref[...]) * _softplus(abv + dt_ref[...]), 0.0)
        beta = jnp.where((lane >= nh) & (lane < 2 * nh), _sigmoid(abv), 0.0)
        ii = lax.broadcasted_iota(jnp.int32, (tm, tm), 0)
        jj = lax.broadcasted_iota(jnp.int32, (tm, tm), 1)
        tri = jnp.where((ii >= jj) & ((ii ^ jj) < ck), 1.0, 0.0)
        gcum = _dot(tri, g, precision=HI)
        aux_ref[...] = g + beta + pltpu.roll(gcum, 2 * nh, 1)

    row = lambda c: pl.BlockSpec((tm, c), lambda i: (i, 0))
    vec = lambda r, c: pl.BlockSpec((r, c), lambda i: (0, 0))
    return pl.pallas_call(
        body, grid=(t // tm,),
        in_specs=[row(cw), pl.BlockSpec((HALO, cw), lambda i: (jnp.maximum(i * hb - 1, 0), 0)), vec(GDN_CONV, cw),
                  pl.BlockSpec((tm, LANES), lambda i: (i, ab_blk)), vec(1, LANES), vec(1, LANES)],
        out_specs=[row(cw), row(cw), row(LANES)],
        out_shape=[jax.ShapeDtypeStruct((t, cw), F32), jax.ShapeDtypeStruct((t, cw), F32),
                   jax.ShapeDtypeStruct((t, LANES), F32)],
        compiler_params=_cp(("parallel",)), name=name,
    )(qkvb, qkvb, conv_w, ab, alog_pad, dt_pad)


GDN_STEP_CHUNKS = 4
GDN_ILP_CHUNKS = 4
GDN_ILP_CHUNKS_BWD = 4


def _bdot(a, b, dims=NN):
    return _dot(a.astype(BF16), b.astype(BF16), dims)


def _split_bf16(x):
    hi = x.astype(BF16)
    return hi, (x - hi.astype(F32)).astype(BF16)


def _dot3(a, b, dims=NN):
    ah, al = _split_bf16(a)
    bh, bl = _split_bf16(b)
    return _dot(ah, bh, dims) + (_dot(ah, bl, dims) + _dot(al, bh, dims))


def _dot3_many(lhs, rhs, dims=NN):
    sa = [_split_bf16(a) for a in lhs]
    sb = [_split_bf16(b) for b in rhs]
    hh = [_dot(a[0], b[0], dims) for a, b in zip(sa, sb)]
    hl = [_dot(a[0], b[1], dims) for a, b in zip(sa, sb)]
    lh = [_dot(a[1], b[0], dims) for a, b in zip(sa, sb)]
    return [x + (y + z) for x, y, z in zip(hh, hl, lh)]


def _gdn_local(chains, with_inverse):
    ck = GDN_CHUNK
    ii = lax.broadcasted_iota(jnp.int32, (ck, ck), 0)
    jj = lax.broadcasted_iota(jnp.int32, (ck, ck), 1)
    lower, strict = ii >= jj, ii > jj
    dmat = [jnp.exp(jnp.where(lower, gc - gc_row, -jnp.inf)) for _, _, _, gc, gc_row in chains]
    kk = [_bdot(k, k, NT) for _, k, _, _, _ in chains]
    qk = [_bdot(q, k, NT) for q, k, _, _, _ in chains]
    tinv = [None] * len(chains)
    if with_inverse:
        lmat = [jnp.where(strict, c[2] * kk_i * d_i, 0.0) for c, kk_i, d_i in zip(chains, kk, dmat)]
        eye = jnp.where(ii == jj, 1.0, 0.0)
        tinv = [eye - l_i for l_i in lmat]
        pw = lmat
        for _ in range(int(math.log2(ck)) - 1):
            pw = _dot3_many(pw, pw)
            tinv = [t_i + d_i for t_i, d_i in zip(tinv, _dot3_many(tinv, pw))]
    out = []
    for (q, k, b, gc, gc_row), dmat_i, kk_i, qk_i, tinv_i in zip(chains, dmat, kk, qk, tinv):
        gl = gc[ck - 1:ck, :]
        out.append(dict(lower=lower, strict=strict, dmat=dmat_i, kk=kk_i, tinv=tinv_i, gam=jnp.exp(gc), qk=qk_i,
                        mm=qk_i * dmat_i, kdec=jnp.exp(gl - gc)))
    return out


def _gdn_head_cols(h):
    return slice(h * GDN_HEAD_DIM, (h + 1) * GDN_HEAD_DIM)


def _gdn_chunk_inputs(x_ref, aux_ref, auxt_ref, g, h):
    nh, ck = GDN_HEADS, GDN_CHUNK
    gw = nh * GDN_HEAD_DIM
    rows = slice(g * ck, (g + 1) * ck)
    cols = _gdn_head_cols(h)
    q = x_ref[rows, cols]
    k = x_ref[rows, gw + cols.start:gw + cols.stop]
    v = x_ref[rows, 2 * gw + cols.start:2 * gw + cols.stop]
    b = aux_ref[rows, nh + h:nh + h + 1]
    gc = aux_ref[rows, 2 * nh + h:2 * nh + h + 1]
    gc_row = auxt_ref[g, 2 * nh + h:2 * nh + h + 1, :]
    return q, k, v, b, gc, gc_row


def _gdn_specs(t, widths, *, reverse=False, step_chunks=None):
    rows = (step_chunks or GDN_STEP_CHUNKS) * GDN_CHUNK
    nsteps = t // rows
    idx = (lambda i: (nsteps - 1 - i, 0)) if reverse else (lambda i: (i, 0))
    return [pl.BlockSpec((rows, w), idx) for w in widths]


def _gdn_local_fwd(qkvn, aux, aux_t, *, name, exchange=None):
    t = qkvn.shape[0]
    hd, nh, ck, gs = GDN_HEAD_DIM, GDN_HEADS, GDN_CHUNK, GDN_STEP_CHUNKS
    gw = nh * hd
    host = _ExchangeHost(exchange)
    assert not (exchange and exchange.aliases)
    grid = (t // (gs * ck),)

    def body(*refs):
        x_ref, aux_ref, auxt_ref = refs[:3]
        u_ref, w_ref, qd_ref, kd_ref, mm_ref, tinv_ref = refs[3 + host.n_in:9 + host.n_in]
        host.start(refs, 3, 9 + host.n_in, grid)
        for g0 in range(0, gs, GDN_ILP_CHUNKS):
            where = [(g, h) for g in range(g0, g0 + GDN_ILP_CHUNKS) for h in range(nh)]
            ins = [_gdn_chunk_inputs(x_ref, aux_ref, auxt_ref, g, h) for g, h in where]
            lcs = _gdn_local([(q, k, b, gc, gc_row) for q, k, _, b, gc, gc_row in ins], True)
            tinvs = [lc["tinv"] for lc in lcs]
            us = _dot3_many(tinvs, [b * v for _, _, v, b, _, _ in ins])
            ws = _dot3_many(tinvs, [(b * lc["gam"]) * k for (_, k, _, b, _, _), lc in zip(ins, lcs)])
            for i, ((g, h), (q, k, _, _, _, _), lc) in enumerate(zip(where, ins, lcs)):
                rows, cols = slice(g * ck, (g + 1) * ck), _gdn_head_cols(h)
                u_ref[rows, cols] = us[i]
                w_ref[rows, cols] = ws[i].astype(w_ref.dtype)
                qd_ref[rows, cols] = (lc["gam"] * q).astype(qd_ref.dtype)
                kd_ref[rows, cols] = (lc["kdec"] * k).astype(kd_ref.dtype)
            for g in range(g0, g0 + GDN_ILP_CHUNKS):
                rows = slice(g * ck, (g + 1) * ck)
                mine = [lc for (gg, _), lc in zip(where, lcs) if gg == g]
                mm_ref[rows, :] = jnp.concatenate([lc["mm"] for lc in mine], axis=1).astype(mm_ref.dtype)
                tinv_ref[rows, :] = jnp.concatenate([lc["tinv"] for lc in mine], axis=1)
        host.wait(refs, 3, 9 + host.n_in, grid)

    sq = nh * ck
    outs = pl.pallas_call(
        body, grid=grid,
        in_specs=_gdn_specs(t, (3 * gw, LANES)) + [pl.BlockSpec((gs, 16, ck), lambda i: (i, 0, 0))] + host.in_specs,
        out_specs=_gdn_specs(t, (gw, gw, gw, gw, sq, sq)) + host.out_specs,
        out_shape=[jax.ShapeDtypeStruct((t, gw), F32)] + [jax.ShapeDtypeStruct((t, gw), BF16)] * 3
        + [jax.ShapeDtypeStruct((t, sq), BF16), jax.ShapeDtypeStruct((t, sq), F32)] + host.out_shapes,
        scratch_shapes=host.scratch,
        compiler_params=_cp(host.semantics(("parallel",))), name=name,
    )(qkvn, aux, aux_t, *host.ins)
    return (*outs[:6], list(outs[6:])) if exchange is not None else outs


def _gdn_seq_fwd(u, w, qd, kd, mm, aux, *, name):
    t = u.shape[0]
    hd, nh, ck, gs = GDN_HEAD_DIM, GDN_HEADS, GDN_CHUNK, GDN_STEP_CHUNKS
    gw = nh * hd
    sq = nh * ck

    def body(u_ref, w_ref, qd_ref, kd_ref, mm_ref, aux_ref, o_ref, vn_ref, sall_ref, s_ref):
        @pl.when(pl.program_id(0) == 0)
        def _():
            s_ref[...] = jnp.zeros_like(s_ref)

        heads = range(nh)
        hcols = [_gdn_head_cols(h) for h in heads]
        sts = [s_ref[h] for h in heads]
        for g in range(gs):
            rows = slice(g * ck, (g + 1) * ck)
            last = (g + 1) * ck - 1
            for h in heads:
                sall_ref[g, h] = sts[h]
            stbs = [st.astype(BF16) for st in sts]
            w_s = [_dot(w_ref[rows, c], stb) for c, stb in zip(hcols, stbs)]
            q_s = [_dot(qd_ref[rows, c], stb) for c, stb in zip(hcols, stbs)]
            vnbs = [(u_ref[rows, c] - ws).astype(BF16) for c, ws in zip(hcols, w_s)]
            m_v = [_dot(mm_ref[rows, h * ck:(h + 1) * ck], vnbs[h]) for h in heads]
            k_v = [_dot(kd_ref[rows, c], vnb, TN) for c, vnb in zip(hcols, vnbs)]
            for h, c in zip(heads, hcols):
                vn_ref[rows, c] = vnbs[h]
                o_ref[rows, c] = q_s[h] + m_v[h]
            gam_c = [jnp.exp(aux_ref[last:last + 1, 2 * nh + h:2 * nh + h + 1]) for h in heads]
            sts = [gam_c[h] * sts[h] + k_v[h] for h in heads]
        for h in heads:
            s_ref[h] = sts[h]

    return pl.pallas_call(
        body, grid=(t // (gs * ck),),
        in_specs=_gdn_specs(t, (gw, gw, gw, gw, sq, LANES)),
        out_specs=_gdn_specs(t, (gw, gw)) + [pl.BlockSpec((gs, nh, hd, hd), lambda i: (i, 0, 0, 0))],
        out_shape=[jax.ShapeDtypeStruct((t, gw), F32), jax.ShapeDtypeStruct((t, gw), BF16),
                   jax.ShapeDtypeStruct((t // ck, nh, hd, hd), F32)],
        scratch_shapes=[pltpu.VMEM((nh, hd, hd), F32)],
        compiler_params=_cp(("arbitrary",)), name=name,
    )(u, w, qd, kd, mm, aux)


def _gdn_seq_bwd(do, w, qd, kd, mm, vn, s_all, aux, *, name):
    t = do.shape[0]
    hd, nh, ck, gs = GDN_HEAD_DIM, GDN_HEADS, GDN_CHUNK, GDN_STEP_CHUNKS
    gw = nh * hd
    sq = nh * ck
    nsteps = t // (gs * ck)

    def body(do_ref, w_ref, qd_ref, kd_ref, mm_ref, vn_ref, sall_ref, aux_ref, dvn_ref, dqd_ref, dkd_ref, dw_ref,
             dlast_ref, ds_ref):
        @pl.when(pl.program_id(0) == 0)
        def _():
            ds_ref[...] = jnp.zeros_like(ds_ref)

        lane = lax.broadcasted_iota(jnp.int32, (ck, LANES), 1)
        rowi = lax.broadcasted_iota(jnp.int32, (ck, LANES), 0)
        heads = range(nh)
        hcols = [_gdn_head_cols(h) for h in heads]
        dsns = [ds_ref[h] for h in heads]
        for g in reversed(range(gs)):
            rows = slice(g * ck, (g + 1) * ck)
            last = (g + 1) * ck - 1
            sts = [sall_ref[g, h] for h in heads]
            stbs = [st.astype(BF16) for st in sts]
            dsbs = [dsn.astype(BF16) for dsn in dsns]
            dobs = [do_ref[rows, c].astype(BF16) for c in hcols]
            dvns = [_dot(mm_ref[rows, h * ck:(h + 1) * ck], dobs[h], TN) + _dot(kd_ref[rows, hcols[h]], dsbs[h])
                    for h in heads]
            dqds = [_dot(dob, stb, NT) for dob, stb in zip(dobs, stbs)]
            dkds = [_dot(vn_ref[rows, c], dsb, NT) for c, dsb in zip(hcols, dsbs)]
            q_o = [_dot(qd_ref[rows, c], dob, TN) for c, dob in zip(hcols, dobs)]
            dvbs = [dvn.astype(BF16) for dvn in dvns]
            dws = [_dot(dvb, stb, NT) for dvb, stb in zip(dvbs, stbs)]
            w_v = [_dot(w_ref[rows, c], dvb, TN) for c, dvb in zip(hcols, dvbs)]
            gam_c = [jnp.exp(aux_ref[last:last + 1, 2 * nh + h:2 * nh + h + 1]) for h in heads]
            dlast = jnp.zeros((ck, LANES), F32)
            for h, c in zip(heads, hcols):
                dvn_ref[rows, c] = dvns[h]
                dqd_ref[rows, c] = dqds[h]
                dkd_ref[rows, c] = dkds[h]
                dw_ref[rows, c] = -dws[h]
                dgam_c = jnp.sum(jnp.sum(dsns[h] * sts[h], axis=1, keepdims=True), axis=0, keepdims=True)
                dlast = dlast + jnp.where((rowi == ck - 1) & (lane == h), gam_c[h] * dgam_c, 0.0)
            dlast_ref[rows, :] = dlast
            dsns = [q_o[h] + gam_c[h] * dsns[h] - w_v[h] for h in heads]
        for h in heads:
            ds_ref[h] = dsns[h]

    return pl.pallas_call(
        body, grid=(nsteps,),
        in_specs=_gdn_specs(t, (gw, gw, gw, gw, sq, gw), reverse=True)
        + [pl.BlockSpec((gs, nh, hd, hd), lambda i: (nsteps - 1 - i, 0, 0, 0))] + _gdn_specs(t, (LANES,), reverse=True),
        out_specs=_gdn_specs(t, (gw, gw, gw, gw, LANES), reverse=True),
        out_shape=[jax.ShapeDtypeStruct((t, gw), F32)] * 4 + [jax.ShapeDtypeStruct((t, LANES), F32)],
        scratch_shapes=[pltpu.VMEM((nh, hd, hd), F32)],
        compiler_params=_cp(("arbitrary",)), name=name,
    )(do, w, qd, kd, mm, vn, s_all, aux)


def _gdn_local_bwd(qkvn, aux, aux_t, tinv, u, w, vn, do, dvn, dqd, dkd, dw, dlast, *, name):
    t = qkvn.shape[0]
    hd, nh, ck, gs = GDN_HEAD_DIM, GDN_HEADS, GDN_CHUNK, GDN_STEP_CHUNKS
    gw = nh * hd
    sq = nh * ck

    def body(x_ref, aux_ref, auxt_ref, tinv_ref, u_ref, w_ref, vn_ref, do_ref, dvn_ref, dqd_ref, dkd_ref, dw_ref,
             dlast_ref, dx_ref, daux_ref):
        lane = lax.broadcasted_iota(jnp.int32, (ck, LANES), 1)
        ones = jnp.ones((ck, LANES), F32)
        ii = lax.broadcasted_iota(jnp.int32, (ck, ck), 0)
        jj = lax.broadcasted_iota(jnp.int32, (ck, ck), 1)
        suffix = jnp.where(jj >= ii, 1.0, 0.0)
        for g0 in range(0, gs, GDN_ILP_CHUNKS_BWD):
            where = [(g, h) for g in range(g0, g0 + GDN_ILP_CHUNKS_BWD) for h in range(nh)]
            at = [(slice(g * ck, (g + 1) * ck), _gdn_head_cols(h)) for g, h in where]
            ins = [_gdn_chunk_inputs(x_ref, aux_ref, auxt_ref, g, h) for g, h in where]
            lcs = _gdn_local([(q, k, b, gc, gc_row) for q, k, _, b, gc, gc_row in ins], False)
            tinvs = [tinv_ref[slice(g * ck, (g + 1) * ck), h * ck:(h + 1) * ck] for g, h in where]
            dms = [jnp.where(lc["lower"], _bdot(do_ref[r, c], vn_ref[r, c], NT), 0.0) for lc, (r, c) in zip(lcs, at)]
            drvs = _dot3_many(tinvs, [dvn_ref[r, c] for r, c in at], TN)
            drks = _dot3_many(tinvs, [dw_ref[r, c] for r, c in at], TN)
            das = [jnp.where(lc["strict"], -(_bdot(drv, u_ref[r, c], NT) + _bdot(drk, w_ref[r, c], NT)), 0.0)
                   for lc, (r, c), drv, drk in zip(lcs, at, drvs, drks)]
            f_mats = [da * (i[3] * lc["kk"]) * lc["dmat"] + dm * lc["qk"] * lc["dmat"]
                      for i, lc, da, dm in zip(ins, lcs, das, dms)]
            col_sums = _dot3_many(f_mats, [ones] * len(where), TN)
            dgc_all = {g: dlast_ref[slice(g * ck, (g + 1) * ck), :] for g in range(g0, g0 + GDN_ILP_CHUNKS_BWD)}
            db_all = {g: jnp.zeros((ck, LANES), F32) for g in range(g0, g0 + GDN_ILP_CHUNKS_BWD)}
            e_mats = [da * lc["dmat"] * i[3] for i, lc, da in zip(ins, lcs, das)]
            dmds = [dm * lc["dmat"] for lc, dm in zip(lcs, dms)]
            dq_mm = [_bdot(dmd, i[1]) for i, dmd in zip(ins, dmds)]
            dk_mm = [_bdot(e, i[1]) + _bdot(e, i[1], TN) + _bdot(dmd, i[0], TN) for i, e, dmd in zip(ins, e_mats, dmds)]
            for n, ((g, h), (q, k, v, b, _, _), lc, (rows, cols)) in enumerate(zip(where, ins, lcs, at)):
                dmat, kk, gam, kdec = (lc[key] for key in ("dmat", "kk", "gam", "kdec"))
                drv, drk, da = drvs[n], drks[n], das[n]
                dqd_h, dkd_h = dqd_ref[rows, cols], dkd_ref[rows, cols]
                rs_rk = jnp.sum(drk * k, axis=-1, keepdims=True)
                db = (jnp.sum(drv * v, axis=-1, keepdims=True) + gam * rs_rk
                      + jnp.sum(da * kk * dmat, axis=-1, keepdims=True))
                dx_ref[rows, cols] = dq_mm[n] + gam * dqd_h
                dx_ref[rows, gw + cols.start:gw + cols.stop] = (b * gam) * drk + dk_mm[n] + kdec * dkd_h
                dx_ref[rows, 2 * gw + cols.start:2 * gw + cols.stop] = b * drv
                e_vec = jnp.sum(dkd_h * (kdec * k), axis=-1, keepdims=True)
                dgc = (b * gam * rs_rk + gam * jnp.sum(dqd_h * q, axis=-1, keepdims=True)
                       + jnp.sum(f_mats[n], axis=-1, keepdims=True) - col_sums[n][:, 0:1] - e_vec)
                is_last = lax.broadcasted_iota(jnp.int32, (ck, 1), 0) == ck - 1
                dgc = dgc + jnp.where(is_last, jnp.sum(e_vec, axis=0, keepdims=True), 0.0)
                dgc_all[g] = dgc_all[g] + jnp.where(lane == h, dgc, 0.0)
                db_all[g] = db_all[g] + jnp.where(lane == nh + h, db, 0.0)
            for g in dgc_all:
                daux_ref[slice(g * ck, (g + 1) * ck), :] = _dot3(suffix, dgc_all[g]) + db_all[g]

    return pl.pallas_call(
        body, grid=(t // (gs * ck),),
        in_specs=_gdn_specs(t, (3 * gw, LANES)) + [pl.BlockSpec((gs, 16, ck), lambda i: (i, 0, 0))]
        + _gdn_specs(t, (sq, gw, gw, gw, gw, gw, gw, gw, gw, LANES)),
        out_specs=_gdn_specs(t, (3 * gw, LANES)),
        out_shape=[jax.ShapeDtypeStruct((t, 3 * gw), F32), jax.ShapeDtypeStruct((t, LANES), F32)],
        compiler_params=_cp(("parallel",)), name=name,
    )(qkvn, aux, aux_t, tinv, u, w, vn, do, dvn, dqd, dkd, dw, dlast)


def _gdn_pre_bwd1(xc, dqkvn, daux, ab, alog_pad, dt_pad, dkv, dp, dp_col, *, name, ab_blk=0, tm=256):
    t, cw = xc.shape
    hd, nh = GDN_HEAD_DIM, GDN_HEADS
    gw = nh * hd
    tm = _blk(t, tm)

    kvw = dkv.shape[1]
    seg = kvw + AB_PAD
    assert dp_col % seg == 0

    def body(xc_ref, dy_ref, daux_ref, ab_ref, al_ref, dt_ref, dkv_ref, _, dxc_ref, dab_ref, dal_ref, ddt_ref):
        i = pl.program_id(0)
        xc = xc_ref[...]
        sg = _sigmoid(xc)
        s = xc * sg
        dsilu = sg * (1.0 + xc * (1.0 - sg))
        for h in range(2 * nh):
            xh = s[:, h * hd:(h + 1) * hd]
            scale = hd ** -0.5 if h < nh else 1.0
            dyh = dy_ref[:, h * hd:(h + 1) * hd] * scale
            r = lax.rsqrt(jnp.sum(xh * xh, axis=-1, keepdims=True) + L2_EPS)
            dxh = r * dyh - xh * (r * r * r) * jnp.sum(dyh * xh, axis=-1, keepdims=True)
            dxc_ref[:, h * hd:(h + 1) * hd] = dxh * dsilu[:, h * hd:(h + 1) * hd]
        dxc_ref[:, 2 * gw:] = dy_ref[:, 2 * gw:] * dsilu[:, 2 * gw:]
        abv = ab_ref[...]
        dauxv = daux_ref[...]
        lane = lax.broadcasted_iota(jnp.int32, abv.shape, 1)
        is_a = lane < nh
        is_b = (lane >= nh) & (lane < 2 * nh)
        pre = abv + dt_ref[...]
        neg_ea = -jnp.exp(al_ref[...])
        d_a = jnp.where(is_a, dauxv * neg_ea * _sigmoid(pre), 0.0)
        beta = _sigmoid(abv)
        d_b = jnp.where(is_b, dauxv * beta * (1.0 - beta), 0.0)
        dab_ref[:, :kvw] = dkv_ref[...]
        dab_ref[:, kvw:kvw + LANES] = (d_a + d_b).astype(dab_ref.dtype)
        dab_ref[:, kvw + LANES:] = jnp.zeros((tm, AB_PAD - LANES), dab_ref.dtype)
        dal = jnp.sum(jnp.where(is_a, dauxv * neg_ea * _softplus(pre), 0.0), axis=0, keepdims=True)
        ddt = jnp.sum(d_a, axis=0, keepdims=True)

        @pl.when(i == 0)
        def _():
            dal_ref[...] = dal
            ddt_ref[...] = ddt

        @pl.when(i > 0)
        def _():
            dal_ref[...] += dal
            ddt_ref[...] += ddt

    row = lambda c: pl.BlockSpec((tm, c), lambda i: (i, 0))
    vec = pl.BlockSpec((1, LANES), lambda i: (0, 0))
    return pl.pallas_call(
        body, grid=(t // tm,),
        in_specs=[row(cw), row(cw), row(LANES), pl.BlockSpec((tm, LANES), lambda i: (i, ab_blk)), vec, vec, row(kvw),
                  ANY_SPEC],
        out_specs=[row(cw), pl.BlockSpec((tm, seg), lambda i: (i, dp_col // seg)), vec, vec],
        out_shape=[jax.ShapeDtypeStruct((t, cw), F32), jax.ShapeDtypeStruct(dp.shape, dp.dtype),
                   jax.ShapeDtypeStruct((1, LANES), F32), jax.ShapeDtypeStruct((1, LANES), F32)],
        input_output_aliases={7: 1}, compiler_params=_cp(("arbitrary",)), name=name,
    )(xc, dqkvn, daux, ab, alog_pad, dt_pad, dkv, dp)


def _gdn_pre_bwd2(dxc, qkvb, conv_w, dp, dp_col, *, name, tm=512):
    t, cw = dxc.shape
    tm = _blk(t, tm)
    hb = tm // HALO
    nblk = t // tm
    cg = GDN_HEADS * GDN_HEAD_DIM
    assert cw % cg == 0 and dp_col % cg == 0
    col0 = dp_col // cg

    def body(d_ref, dn_ref, x_ref, xp_ref, w_ref, _, dx_ref, dw_ref):
        i = pl.program_id(1)
        dcur = d_ref[...]
        dnxt = jnp.where(i < nblk - 1, dn_ref[...], 0.0)
        cur = x_ref[...]
        prev = jnp.where(i > 0, xp_ref[...], 0.0)
        dx = None
        dws = []
        for tap in range(GDN_CONV):
            j = GDN_CONV - 1 - tap
            term = w_ref[tap:tap + 1, :] * _shift_up(dcur, dnxt, j)
            dx = term if dx is None else dx + term
            dws.append(jnp.sum(dcur * _shift_down(cur, prev, j), axis=0, keepdims=True))
        dx_ref[...] = dx.astype(dx_ref.dtype)
        dw = jnp.concatenate(dws, axis=0)

        @pl.when(i == 0)
        def _():
            dw_ref[...] = dw

        @pl.when(i > 0)
        def _():
            dw_ref[...] += dw

    row = pl.BlockSpec((tm, cg), lambda c, i: (i, c))
    wsp = pl.BlockSpec((GDN_CONV, cg), lambda c, i: (0, c))
    return pl.pallas_call(
        body, grid=(cw // cg, nblk),
        in_specs=[row, pl.BlockSpec((HALO, cg), lambda c, i: (jnp.minimum((i + 1) * hb, t // HALO - 1), c)),
                  row, pl.BlockSpec((HALO, cg), lambda c, i: (jnp.maximum(i * hb - 1, 0), c)), wsp, ANY_SPEC],
        out_specs=[pl.BlockSpec((tm, cg), lambda c, i: (i, col0 + c)), wsp],
        out_shape=[jax.ShapeDtypeStruct(dp.shape, dp.dtype), jax.ShapeDtypeStruct((GDN_CONV, cw), F32)],
        input_output_aliases={5: 0}, compiler_params=_cp(("arbitrary", "arbitrary")), name=name,
    )(dxc, dxc, qkvb, qkvb, conv_w, dp)


def _gdn_post_fwd(o, z, norm_w, *, name, tm=512):
    t, gw = o.shape
    hd, nh = GDN_HEAD_DIM, GDN_HEADS
    tm = _blk(t, tm)

    def body(o_ref, z_ref, w_ref, y_ref):
        zv = z_ref[...]
        sz = zv * _sigmoid(zv)
        for h in range(nh):
            oh = o_ref[:, h * hd:(h + 1) * hd]
            r = lax.rsqrt(jnp.mean(oh * oh, axis=-1, keepdims=True) + RMS_EPS)
            y_ref[:, h * hd:(h + 1) * hd] = (oh * r * w_ref[...] * sz[:, h * hd:(h + 1) * hd]).astype(y_ref.dtype)

    row = pl.BlockSpec((tm, gw), lambda i: (i, 0))
    return pl.pallas_call(
        body, grid=(t // tm,), in_specs=[row, row, pl.BlockSpec((1, hd), lambda i: (0, 0))], out_specs=row,
        out_shape=jax.ShapeDtypeStruct((t, gw), BF16), compiler_params=_cp(("parallel",)), name=name,
    )(o, z, norm_w)


def _gdn_post_bwd(dy, o, z, norm_w, dp, dp_col, *, name, tm=512):
    t, gw = o.shape
    hd, nh = GDN_HEAD_DIM, GDN_HEADS
    tm = _blk(t, tm)

    def body(dy_ref, o_ref, z_ref, w_ref, _, do_ref, dz_ref, dw_ref):
        i = pl.program_id(0)
        zv = z_ref[...]
        sg = _sigmoid(zv)
        sz = zv * sg
        dsz = sg * (1.0 + zv * (1.0 - sg))
        dw = None
        for h in range(nh):
            sl = slice(h * hd, (h + 1) * hd)
            oh = o_ref[:, sl]
            dyh = dy_ref[:, sl].astype(F32)
            r = lax.rsqrt(jnp.mean(oh * oh, axis=-1, keepdims=True) + RMS_EPS)
            xh = oh * r
            dz_ref[:, sl] = (dyh * xh * w_ref[...] * dsz[:, sl]).astype(dz_ref.dtype)
            dn = dyh * sz[:, sl]
            dxh = dn * w_ref[...]
            do_ref[:, sl] = r * (dxh - xh * jnp.mean(dxh * xh, axis=-1, keepdims=True))
            part = jnp.sum(dn * xh, axis=0, keepdims=True)
            dw = part if dw is None else dw + part

        @pl.when(i == 0)
        def _():
            dw_ref[...] = dw

        @pl.when(i > 0)
        def _():
            dw_ref[...] += dw

    row = pl.BlockSpec((tm, gw), lambda i: (i, 0))
    vec = pl.BlockSpec((1, hd), lambda i: (0, 0))
    return pl.pallas_call(
        body, grid=(t // tm,), in_specs=[row, row, row, vec, ANY_SPEC],
        out_specs=[row, pl.BlockSpec((tm, gw), lambda i: (i, dp_col // gw)), vec],
        out_shape=[jax.ShapeDtypeStruct((t, gw), F32), jax.ShapeDtypeStruct(dp.shape, dp.dtype),
                   jax.ShapeDtypeStruct((1, hd), F32)],
        input_output_aliases={4: 1}, compiler_params=_cp(("arbitrary",)), name=name,
    )(dy, o, z, norm_w, dp)


IN_NAMES = ("q_a", "kv_a", "qkv_b", "ab", "z", "q_c", "gates")
CAT_NAMES = ("gates", "q_a", "qkv_b", "z", "q_c", "kv_a", "ab")
AB_PAD = 256


def _in_widths(d):
    gw = GDN_HEADS * GDN_HEAD_DIM
    return dict(q_a=SWA_Q_HEADS * SWA_HEAD_DIM, kv_a=2 * SWA_KV_HEADS * SWA_HEAD_DIM, qkv_b=3 * gw, ab=2 * GDN_HEADS,
                z=gw, q_c=XA_HEADS * XA_HEAD_DIM, gates=3 * d)


def _ranges(names, widths):
    out, start = {}, 0
    for k in names:
        out[k] = (start, widths[k])
        start += widths[k]
    return out, start


def _cat_ranges(d):
    widths = dict(_in_widths(d), ab=AB_PAD)
    return _ranges(CAT_NAMES, widths)


def _to_cat(shards, *, name="to_cat", tm=256):
    ns, d, n = shards.shape
    src, _ = _ranges(IN_NAMES, _in_widths(d))
    _, cat_w = _cat_ranges(d)
    pieces = []
    for k in CAT_NAMES:
        lo, hi = src[k][0], src[k][0] + src[k][1]
        for s in range(ns):
            a, b = max(lo, s * n), min(hi, (s + 1) * n)
            if a < b:
                pieces.append((s, a - s * n, b - s * n))
    tm = _blk(d, tm)

    def body(s_ref, o_ref):
        cols = [s_ref[s, :, a:b] for s, a, b in pieces]
        cols.append(jnp.zeros((tm, AB_PAD - src["ab"][1]), o_ref.dtype))
        o_ref[...] = jnp.concatenate(cols, axis=1)

    return pl.pallas_call(
        body, grid=(d // tm,),
        in_specs=[pl.BlockSpec((ns, tm, n), lambda i: (0, i, 0))],
        out_specs=pl.BlockSpec((tm, cat_w), lambda i: (i, 0)),
        out_shape=jax.ShapeDtypeStruct((d, cat_w), shards.dtype),
        compiler_params=_cp(("parallel",)), name=name,
    )(shards)


def _from_cat(w_cat, *, name="from_cat", tm=256):
    d, cat_w = w_cat.shape
    src, total = _ranges(IN_NAMES, _in_widths(d))
    cat, _ = _cat_ranges(d)
    n = total // N_SHARDS
    pieces = []
    for s in range(N_SHARDS):
        pieces.append([])
        for k in IN_NAMES:
            a, b = max(s * n, src[k][0]), min((s + 1) * n, src[k][0] + src[k][1])
            if a < b:
                pieces[s].append((cat[k][0] + a - src[k][0], cat[k][0] + b - src[k][0]))
    tm = _blk(d, tm)

    def body(c_ref, o_ref):
        for s in range(N_SHARDS):
            o_ref[s] = jnp.concatenate([c_ref[:, a:b] for a, b in pieces[s]], axis=1)

    return pl.pallas_call(
        body, grid=(d // tm,),
        in_specs=[pl.BlockSpec((tm, cat_w), lambda i: (i, 0))],
        out_specs=pl.BlockSpec((N_SHARDS, tm, n), lambda i: (0, i, 0)),
        out_shape=jax.ShapeDtypeStruct((N_SHARDS, d, n), w_cat.dtype),
        compiler_params=_cp(("parallel",)), name=name,
    )(w_cat)


def _pad_cols(a, width):
    return jnp.pad(a, ((0, 0), (0, width - a.shape[1])))


def _relu2_epilogue(acc):
    r = jnp.maximum(acc, 0.0)
    return acc, r * r


def _add_epilogue(acc, res):
    return (acc + res,)


def _drelu2_epilogue(acc, u):
    return (acc * (2.0 * jnp.maximum(u.astype(F32), 0.0)),)


def _local_step(x, mem, tgt, wts, small, comm=None):
    t, d = x.shape
    nh = GDN_HEADS
    cat, cat_w = _cat_ranges(d)
    alog_pad = _pad_cols(small["a_log"], LANES)
    dt_pad = _pad_cols(small["dt_bias"], LANES)
    kvw = cat["kv_a"][1]
    assert cat["ab"][0] == cat["kv_a"][0] + kvw
    ab_blk = kvw // LANES

    if comm is None:
        n = _rms_fwd(x, small["g_mix"], name="rms_mix")
        w_cat = wts["w_cat"]
    else:
        n, landed = _rms_fwd(x, small["g_mix"], name="rms_mix", exchange=comm.gather_exchange(["w_in"]))
        w_cat = _to_cat(_exchange_call(_gather_pass_on(landed), name="ag_w_in_pass")[0])
    assert w_cat.shape == (d, cat_w)
    q_a = _mm(n, w_cat, b_window=cat["q_a"], out_dtypes=(BF16,), name="in_q_a")
    kv_a, ab = _mm(n, w_cat, b_window=(cat["kv_a"][0], kvw + AB_PAD), out_dtypes=(BF16, F32), name="in_kv_ab")
    qkvb = _mm(n, w_cat, b_window=cat["qkv_b"], tn=512, name="in_qkv_b")
    z = _mm(n, w_cat, b_window=cat["z"], name="in_z")
    q_c = _mm(n, w_cat, b_window=cat["q_c"], out_dtypes=(BF16,), name="in_q_c")
    if comm is None:
        gates = _mm(n, w_cat, b_window=cat["gates"], name="in_gates")
        y_a, lse = _swa_fwd(q_a, kv_a, small["sinks"], name="swa_fwd")
    else:
        gates, landed_mlp = _mm(n, w_cat, b_window=cat["gates"], name="in_gates",
                                exchange=comm.gather_exchange(comm.MLP[1:]))
        y_a, lse, landed = _swa_fwd(q_a, kv_a, small["sinks"], name="swa_fwd", exchange=comm.gather_exchange(comm.MLP[:1]))
        landed_mlp = landed + landed_mlp
    xc, qkvn, aux = _gdn_pre_fwd(qkvb, small["conv_w"], ab, alog_pad, dt_pad, ab_blk=ab_blk, name="gdn_pre_fwd")
    aux_t = aux[:, :16].reshape(t // GDN_CHUNK, GDN_CHUNK, 16).transpose(0, 2, 1)
    if comm is None:
        gdn_u, gdn_w, gdn_qd, gdn_kd, gdn_mm, gdn_tinv = _gdn_local_fwd(qkvn, aux, aux_t, name="gdn_local_fwd")
    else:
        gdn_u, gdn_w, gdn_qd, gdn_kd, gdn_mm, gdn_tinv, landed_mid = _gdn_local_fwd(
            qkvn, aux, aux_t, name="gdn_local_fwd", exchange=comm.gather_exchange(comm.mid))
        wts = dict(wts, **comm.gathered(comm.mid, landed_mid, "mid"))
    o_b, gdn_vn, s_all = _gdn_seq_fwd(gdn_u, gdn_w, gdn_qd, gdn_kd, gdn_mm, aux, name="gdn_seq_fwd")
    y_b = _gdn_post_fwd(o_b, z, small["gdn_norm_w"], name="gdn_post_fwd")
    nmem = _rms_fwd(mem, small["g_mem"], name="rms_mem")
    mkv = _mm(nmem, wts["w_mem_kv"], out_dtypes=(BF16,), name="mem_kv")
    y_c = _xa_fwd(q_c, mkv, name="xa_fwd")
    ys = (y_a, y_b, y_c)
    w_ups = (wts["w_swa_up"], wts["w_gdn_up"], wts["w_xa_up"])
    merged = _merge_fwd(ys, w_ups, gates, name="merge_fwd")
    if comm is None:
        h1 = _mm(merged, wts["w_out"], extras=(x,), epilogue=_add_epilogue, name="out_proj")
    else:
        h1, whole = _mm(merged, wts["w_out"], extras=(x,), epilogue=_add_epilogue, name="out_proj",
                        exchange=_gather_pass_on(landed_mlp))
        wts = dict(wts, **comm.as_weights(comm.MLP, whole))
    n2 = _rms_fwd(h1, small["g_mlp"], name="rms_mlp")
    u, act = _mm(n2, wts["w_mlp_in"], b_sharded=True, out_dtypes=(BF16, BF16), epilogue=_relu2_epilogue, name="mlp_in")
    h2 = _mm(act, wts["w_mlp_out"], extras=(h1,), epilogue=_add_epilogue, name="mlp_out")
    dh2, dh2_b, dg_final, loss = _final_loss(h2, small["g_final"], tgt, name="final_loss")

    grads = {"g_final": dg_final}
    du = _mm(dh2_b, wts["w_mlp_out"], tb=True, out_dtypes=(BF16,), extras=(u,), epilogue=_drelu2_epilogue, name="d_mlp_act")
    grads["w_mlp_out"] = _mm(act, dh2_b, ta=True, out_dtypes=(BF16,), name="dw_mlp_out")
    grads["w_mlp_in"] = _mm(n2, du, ta=True, out_sharded=True, out_dtypes=(BF16,), name="dw_mlp_in")
    if comm is None:
        dn2 = _mm(du, wts["w_mlp_in"], tb=True, b_sharded=True, name="d_mlp_in")
    else:
        g_mlp = [comm.shard_major(k, grads.pop(k)) for k in comm.MLP]
        dn2, sib_mlp = _mm(du, wts["w_mlp_in"], tb=True, b_sharded=True, name="d_mlp_in", exchange=_sibling_halves(g_mlp))
        s1_mlp = comm.pair_sums(g_mlp, "mlp", sib_mlp)
    dh1, dh1_b, grads["g_mlp"] = _rms_bwd(dn2, h1, small["g_mlp"], dh2, name="rms_mlp_bwd")
    dmerged = _mm(dh1_b, wts["w_out"], tb=True, name="d_out_proj")
    grads["w_out"] = _mm(merged, dh1_b, ta=True, out_dtypes=(BF16,), name="dw_out")
    *dus, dp = _merge_bwd(ys, w_ups, gates, dmerged, cat_w, name="merge_bwd")
    dys = []
    for y, du_i, w_up, key in zip(ys, dus, w_ups, ("w_swa_up", "w_gdn_up", "w_xa_up")):
        dys.append(_mm(du_i, w_up, tb=True, b_sharded=True, out_dtypes=(BF16,), name="d_" + key))
        grads[key] = _mm(y, du_i, ta=True, out_sharded=True, out_dtypes=(BF16,), name="dw_" + key[2:])
    dp, dkv_a, grads["sinks"] = _swa_bwd(q_a, kv_a, small["sinks"], y_a, lse, dys[0], dp, cat["q_a"][0], name="swa_bwd")
    do_b, dp, grads["gdn_norm_w"] = _gdn_post_bwd(dys[1], o_b, z, small["gdn_norm_w"], dp, cat["z"][0],
                                                  name="gdn_post_bwd")
    dvn, dqd, dkd, dw_, dlast = _gdn_seq_bwd(do_b, gdn_w, gdn_qd, gdn_kd, gdn_mm, gdn_vn, s_all, aux, name="gdn_seq_bwd")
    dqkvn, daux = _gdn_local_bwd(qkvn, aux, aux_t, gdn_tinv, gdn_u, gdn_w, gdn_vn, do_b, dvn, dqd, dkd, dw_, dlast,
                                 name="gdn_local_bwd")
    dxc, dp, dalog, ddt = _gdn_pre_bwd1(xc, dqkvn, daux, ab, alog_pad, dt_pad, dkv_a, dp, cat["kv_a"][0], ab_blk=ab_blk,
                                        name="gdn_pre_bwd1")
    grads["a_log"], grads["dt_bias"] = dalog[:, :nh], ddt[:, :nh]
    dp, grads["conv_w"] = _gdn_pre_bwd2(dxc, qkvb, small["conv_w"], dp, cat["qkv_b"][0], name="gdn_pre_bwd2")
    dp, dmkv = _xa_bwd(q_c, mkv, dys[2], dp, cat["q_c"][0], name="xa_bwd")
    grads["w_mem_kv"] = _mm(nmem, dmkv, ta=True, out_dtypes=(BF16,), name="dw_mem_kv")
    dnmem = _mm(dmkv, wts["w_mem_kv"], tb=True, name="d_mem_kv")
    _, _, grads["g_mem"] = _rms_bwd(dnmem, mem, small["g_mem"], jnp.zeros_like(mem), name="rms_mem_bwd")
    if comm is None:
        grads["w_cat"] = _mm(n, dp, ta=True, out_dtypes=(BF16,), name="dw_in")
        dn = _mm(dp, w_cat, tb=True, name="d_in_proj")
    else:
        s1_mid = comm.pair_sums([comm.shard_major(k, grads.pop(k)) for k in comm.mid], "mid")
        dw_cat, rcv_mlp = _mm(n, dp, ta=True, out_dtypes=(BF16,), name="dw_in", exchange=_chip_exchange(s1_mlp))
        s1_in = comm.pair_sums([_from_cat(dw_cat)], "in")
        dn, rcv_rest = _mm(dp, w_cat, tb=True, name="d_in_proj", exchange=_chip_exchange(s1_in + s1_mid))
        halves = comm.chip_sums(s1_in + s1_mid + s1_mlp, rcv_rest + rcv_mlp)
    if comm is None:
        dx, _, grads["g_mix"] = _rms_bwd(dn, x, small["g_mix"], dh1, name="rms_mix_bwd")
    else:
        dx, _, grads["g_mix"], reduced = _rms_bwd(dn, x, small["g_mix"], dh1, name="rms_mix_bwd",
                                                  exchange=_join_halves(halves))
        grads.update(zip(["w_in"] + comm.mid + list(comm.MLP), reduced))
    return loss, dx, grads


HBM_SPEC = pl.BlockSpec(memory_space=pltpu.HBM)
VMEM_SPEC = pl.BlockSpec(memory_space=pltpu.VMEM)
N_CHIPS = N_SHARDS
N_DEV = 8
DMA_CHUNK_BYTES = 1 << 20


def _place():
    return lax.axis_index("x"), lax.axis_index("y"), lax.axis_index("c")


def _other_chips(x, y):
    return [(1 - x, y), (x, 1 - y), (1 - x, 1 - y)]


def _n_chunks(rows, row_bytes):
    n = 1
    while rows % (2 * n) == 0 and (rows // (2 * n)) % 16 == 0 and (rows // n) * row_bytes > DMA_CHUNK_BYTES:
        n *= 2
    return n


def _sem_scratch(n_remote, n_local):
    return [pltpu.SemaphoreType.DMA((max(n_remote, 1),)), pltpu.SemaphoreType.DMA((max(n_remote, 1),)),
            pltpu.SemaphoreType.DMA((max(n_local, 1),))]


def _gather_over_ici(shards):
    plan = _half_chunks(shards, 0)

    def copies_of(in_refs, out_refs, place):
        x, y, c = place
        remote, local = [], []
        for i, r0, nr in plan:
            rh = shards[i].shape[0] // 2
            mine = pl.ds(c * rh + r0, nr)
            for chip in _other_chips(x, y):
                remote.append((in_refs[i].at[mine], out_refs[i].at[2 * x + y, mine], (*chip, c)))
            for half in range(2):
                rows = pl.ds(half * rh + r0, nr)
                local.append((in_refs[i].at[rows], out_refs[i].at[2 * x + y, rows]))
        return remote, local

    shapes = tuple(jax.ShapeDtypeStruct((N_CHIPS, *s.shape), s.dtype) for s in shards)
    return Exchange(tuple(shards), shapes, 3 * len(plan), 2 * len(plan), copies_of)


def _gather_pass_on(arrived):
    plan = _half_chunks([jax.ShapeDtypeStruct(a.shape[1:], a.dtype) for a in arrived], 0)

    def copies_of(in_refs, out_refs, place):
        x, y, c = place
        remote = []
        for i, r0, nr in plan:
            mine = pl.ds(c * (arrived[i].shape[1] // 2) + r0, nr)
            for chip in _other_chips(x, y):
                rows = out_refs[i].at[2 * chip[0] + chip[1], mine]
                remote.append((rows, rows, (x, y, 1 - c)))
        return remote, []

    shapes = tuple(jax.ShapeDtypeStruct(a.shape, a.dtype) for a in arrived)
    return Exchange(tuple(arrived), shapes, 3 * len(plan), 0, copies_of, tuple((i, i) for i in range(len(arrived))))


def _exchange_call(ex, *, name):
    n_in, n_out = len(ex.ins), len(ex.out_shapes)

    def body(*refs):
        cps = _exchange_copies(ex, refs[:n_in], refs[n_in:n_in + n_out], refs[n_in + n_out:])
        for cp in cps:
            cp.start()
        for cp in cps:
            cp.wait()

    return pl.pallas_call(
        body, out_shape=list(ex.out_shapes), in_specs=[HBM_SPEC] * n_in, out_specs=[HBM_SPEC] * n_out,
        scratch_shapes=_sem_scratch(ex.n_remote, ex.n_local), input_output_aliases=dict(ex.aliases), name=name,
    )(*ex.ins)


def _half_chunks(arrs, row_axis):
    plan = []
    for i, a in enumerate(arrs):
        rh = a.shape[row_axis] // 2
        row_bytes = a.dtype.itemsize * math.prod(a.shape) // a.shape[row_axis]
        nch = _n_chunks(rh, row_bytes)
        plan += [(i, q * (rh // nch), rh // nch) for q in range(nch)]
    return plan


def _sibling_halves(gs):
    plan = _half_chunks(gs, 1)

    def copies_of(in_refs, out_refs, place):
        x, y, c = place
        out = []
        for i, r0, nr in plan:
            rh = gs[i].shape[1] // 2
            out.append((in_refs[i].at[:, pl.ds((1 - c) * rh + r0, nr), :], out_refs[i].at[:, pl.ds(r0, nr), :],
                        (x, y, 1 - c)))
        return out, []

    shapes = tuple(jax.ShapeDtypeStruct((g.shape[0], g.shape[1] // 2, g.shape[2]), g.dtype) for g in gs)
    return Exchange(tuple(gs), shapes, len(plan), 0, copies_of)


def _chip_exchange(s1s):
    plan = _half_chunks([jax.ShapeDtypeStruct((2 * s.shape[1], s.shape[2]), s.dtype) for s in s1s], 0)

    def copies_of(in_refs, out_refs, place):
        x, y, c = place
        out = []
        for i, r0, nr in plan:
            for j, chip in enumerate(_other_chips(x, y)):
                out.append((in_refs[i].at[2 * chip[0] + chip[1], pl.ds(r0, nr), :], out_refs[i].at[j, pl.ds(r0, nr), :],
                            (*chip, c)))
        return out, []

    shapes = tuple(jax.ShapeDtypeStruct((3, *s.shape[1:]), s.dtype) for s in s1s)
    return Exchange(tuple(s1s), shapes, 3 * len(plan), 0, copies_of)


def _join_halves(gs):
    plan = _half_chunks(gs, 0)

    def copies_of(in_refs, out_refs, place):
        x, y, c = place
        out = []
        for i, r0, nr in plan:
            rows = out_refs[i].at[pl.ds(c * (gs[i].shape[0] // 2) + r0, nr), :]
            out.append((rows, rows, (x, y, 1 - c)))
        return out, []

    shapes = tuple(jax.ShapeDtypeStruct(g.shape, g.dtype) for g in gs)
    aliases = tuple((i, i) for i in range(len(gs)))
    return Exchange(tuple(gs), shapes, len(plan), 0, copies_of, aliases)


def _row_block(rows, cols):
    tb = rows
    while tb % 32 == 0 and tb * cols * 4 > (2 << 20):
        tb //= 2
    return tb


def _pair_sum(g, sib, core, *, name):
    ns, r, c = g.shape
    rh = r // 2
    tb = _row_block(rh, c)
    nb = rh // tb

    def body(core_ref, g_ref, s_ref, o_ref):
        o_ref[...] = (g_ref[...].astype(F32) + s_ref[...].astype(F32)).astype(o_ref.dtype)

    mine = pl.BlockSpec((None, tb, c), lambda s, i, core_ref: (s, core_ref[0] * nb + i, 0))
    half = pl.BlockSpec((None, tb, c), lambda s, i, core_ref: (s, i, 0))
    return pl.pallas_call(
        body, grid_spec=pltpu.PrefetchScalarGridSpec(num_scalar_prefetch=1, grid=(ns, nb), in_specs=[mine, half],
                                                     out_specs=half),
        out_shape=jax.ShapeDtypeStruct((ns, rh, c), BF16), compiler_params=_cp(("parallel", "parallel")), name=name,
    )(core, g, sib)


def _chip_sum(s1, rcv, where, *, name):
    _, rh, c = s1.shape
    tb = _row_block(rh, c)
    nb = rh // tb

    def body(where_ref, own_ref, r0_ref, r1_ref, r2_ref, o_ref):
        acc = own_ref[...].astype(F32)
        for r in (r0_ref, r1_ref, r2_ref):
            acc = acc + r[...].astype(F32)
        o_ref[...] = acc

    own = pl.BlockSpec((None, tb, c), lambda i, w: (w[1], i, 0))
    got = [pl.BlockSpec((None, tb, c), functools.partial(lambda i, w, j: (j, i, 0), j=j)) for j in range(3)]
    return pl.pallas_call(
        body, grid_spec=pltpu.PrefetchScalarGridSpec(
            num_scalar_prefetch=1, grid=(nb,), in_specs=[own] + got,
            out_specs=pl.BlockSpec((tb, c), lambda i, w: (w[0] * nb + i, 0))),
        out_shape=jax.ShapeDtypeStruct((2 * rh, c), F32), compiler_params=_cp(("parallel",)), name=name,
    )(where, s1, rcv, rcv, rcv)


def _all_gather_small(blk, *, name):
    r = blk.shape[0]

    def body(b_ref, out_ref, send_sems, recv_sems):
        x, y, c = _place()
        me = 4 * x + 2 * y + c
        out_ref[me] = b_ref[...]
        sends = []
        for k in range(1, N_DEV):
            peer = (x ^ (k >> 2), y ^ ((k >> 1) & 1), c ^ (k & 1))
            sends.append(pltpu.make_async_remote_copy(src_ref=b_ref, dst_ref=out_ref.at[me], send_sem=send_sems.at[k - 1],
                                                      recv_sem=recv_sems.at[k - 1], device_id=peer, device_id_type=MESH))
        for cp in sends:
            cp.start()
        for k in range(1, N_DEV):
            rows = out_ref.at[me ^ k]
            pltpu.make_async_remote_copy(src_ref=rows, dst_ref=rows, send_sem=send_sems.at[k - 1],
                                         recv_sem=recv_sems.at[k - 1], device_id=(x, y, c), device_id_type=MESH).wait_recv()
        for cp in sends:
            cp.wait_send()

    return pl.pallas_call(
        body, out_shape=jax.ShapeDtypeStruct((N_DEV, r, LANES), blk.dtype), in_specs=[VMEM_SPEC], out_specs=VMEM_SPEC,
        scratch_shapes=[pltpu.SemaphoreType.DMA((N_DEV - 1,)), pltpu.SemaphoreType.DMA((N_DEV - 1,))],
        name=name,
    )(blk)


def _sum_rows(parts, out_dtype, *, name, tb=1024):
    rows = parts[0].shape[0]
    tb = _blk(rows, tb)

    def body(*refs):
        acc = refs[0][...].astype(F32)
        for r in refs[1:-1]:
            acc = acc + r[...].astype(F32)
        refs[-1][...] = acc.astype(refs[-1].dtype)

    spec = pl.BlockSpec((tb, LANES), lambda i: (i, 0))
    return pl.pallas_call(
        body, grid=(rows // tb,), in_specs=[spec] * len(parts), out_specs=spec,
        out_shape=jax.ShapeDtypeStruct((rows, LANES), out_dtype), compiler_params=_cp(("parallel",)), name=name,
    )(*parts)


BIG = (
    ("w_in", 1), ("w_mem_kv", 0), ("w_swa_up", 1), ("w_gdn_up", 1), ("w_xa_up", 1), ("w_out", 0), ("w_mlp_in", 1),
    ("w_mlp_out", 0))


class _Comm:
    MLP = ("w_mlp_in", "w_mlp_out")

    def __init__(self, late_shards, core, where):
        self.axis = dict(BIG)
        self.late_shards = late_shards
        self.mid = [k for k in late_shards if k not in self.MLP and k != "w_in"]
        self.core, self.where = core, where

    def gather_exchange(self, names):
        return _gather_over_ici([self.late_shards[k] for k in names])

    def as_weights(self, names, whole):
        return {k: (g.reshape(-1, g.shape[2]) if self.axis[k] == 0 else g) for k, g in zip(names, whole)}

    def gathered(self, names, landed, tag):
        return self.as_weights(names, _exchange_call(_gather_pass_on(landed), name=f"ag_{tag}_pass"))

    def shard_major(self, k, grad):
        return grad.reshape(N_CHIPS, -1, grad.shape[-1]) if self.axis[k] == 0 else grad

    def pair_sums(self, gs, tag, sibs=None):
        if sibs is None:
            sibs = _exchange_call(_sibling_halves(gs), name=f"rs_sibling_{tag}")
        return [_pair_sum(g, s, self.core, name=f"rs_pair_sum_{tag}{i}") for i, (g, s) in enumerate(zip(gs, sibs))]

    def chip_sums(self, s1s, rcvs):
        return [_chip_sum(s1, rcv, self.where, name=f"rs_chip_sum_{i}") for i, (s1, rcv) in enumerate(zip(s1s, rcvs))]
SMALL = ("g_mix", "sinks", "a_log", "dt_bias", "gdn_norm_w", "g_mem", "g_mlp", "g_final")


def _rows128(a, rows):
    flat = a.reshape(-1)
    return jnp.pad(flat, (0, rows * LANES - flat.shape[0])).reshape(rows, LANES)


def kernel(x, mem, g_mix, w_in, sinks, conv_w, a_log, dt_bias, gdn_norm_w, g_mem, w_mem_kv, w_swa_up, w_gdn_up, w_xa_up, w_out, g_mlp, w_mlp_in, w_mlp_out, g_final, loss_target, m_g_mix, m_w_in, m_sinks, m_conv_w, m_a_log, m_dt_bias, m_gdn_norm_w, m_g_mem, m_w_mem_kv, m_w_swa_up, m_w_gdn_up, m_w_xa_up, m_w_out, m_g_mlp, m_w_mlp_in, m_w_mlp_out, m_g_final, v_g_mix, v_w_in, v_sinks, v_conv_w, v_a_log, v_dt_bias, v_gdn_norm_w, v_g_mem, v_w_mem_kv, v_w_swa_up, v_w_gdn_up, v_w_xa_up, v_w_out, v_g_mlp, v_w_mlp_in, v_w_mlp_out, v_g_final):
    given = dict(locals())
    xi, yi, ci = _place()
    chip = 2 * xi + yi
    core = jnp.reshape(ci, (1,)).astype(jnp.int32)
    where = jnp.stack([ci, chip]).astype(jnp.int32)

    comm = _Comm({k: given[k][0].astype(BF16) for k, _ in BIG}, core, where)
    wts = {}
    conv_shard = conv_w[0]
    conv_rows = -(-conv_shard.size // (8 * LANES)) * 8
    conv_all = _all_gather_small(_rows128(conv_shard, conv_rows), name="ag_conv")
    conv_full = jnp.concatenate(
        [conv_all[2 * s].reshape(-1)[:conv_shard.size].reshape(conv_shard.shape) for s in range(N_CHIPS)], axis=1)

    small = {k: given[k].reshape(1, -1) for k in SMALL}
    small["conv_w"] = conv_full
    loss_row, dx, grads = _local_step(x[0], mem[0], loss_target[0], wts, small, comm)
    big_grads = {k: grads[k] for k, _ in BIG}

    layout = [("loss", loss_row[:, :1])] + [(k, grads[k]) for k in SMALL] + [("conv_w", grads["conv_w"])]
    rows = [-(-a.size // LANES) for _, a in layout]
    blk_rows = -(-sum(rows) // 8) * 8
    blk = jnp.concatenate([_rows128(a.astype(F32), n) for (_, a), n in zip(layout, rows)]
                          + [jnp.zeros((blk_rows - sum(rows), LANES), F32)], axis=0)
    gathered = _all_gather_small(blk, name="ag_small_grads")
    reduced = _sum_rows([gathered[i] for i in range(N_DEV)], F32, name="small_grad_sum")
    small_grads, start = {}, 0
    for (k, a), n in zip(layout, rows):
        small_grads[k] = reduced[start:start + n].reshape(-1)[:a.size].reshape(a.shape)
        start += n
    loss = small_grads["loss"].reshape(())
    cw = conv_shard.shape[1]
    conv_grad = lax.dynamic_slice_in_dim(small_grads["conv_w"], chip * cw, cw, axis=1)

    names = ["g_mix", "w_in", "sinks", "conv_w", "a_log", "dt_bias", "gdn_norm_w", "g_mem", "w_mem_kv", "w_swa_up",
             "w_gdn_up", "w_xa_up", "w_out", "g_mlp", "w_mlp_in", "w_mlp_out", "g_final"]
    out_g, out_d, out_m, out_v = [], [], [], []
    for k in names:
        w, m, v = given[k], given["m_" + k], given["v_" + k]
        if k in big_grads:
            g2 = big_grads[k]
        elif k == "conv_w":
            g2 = conv_grad
        else:
            g2 = small_grads[k]
        as_given = (lambda a: a.reshape(1, -1)) if w.ndim == 1 else (lambda a: a)
        if w.shape[-1] % LANES and w.shape[-1] > LANES:
            tr = lambda a: jnp.swapaxes(a, -1, -2)
            delta, new_m, new_v = (tr(a) for a in _adamw(tr(w), tr(g2), tr(m), tr(v), name="adamw_" + k))
        else:
            delta, new_m, new_v = _adamw(as_given(w), g2, as_given(m), as_given(v), name="adamw_" + k)
        out_g.append(g2.reshape(w.shape))
        out_d.append(delta.reshape(w.shape))
        out_m.append(new_m.reshape(w.shape))
        out_v.append(new_v.reshape(w.shape))
    return (loss, dx[None], *out_g, *out_d, *out_m, *out_v)
```

```python
import functools
import math
from typing import Callable, NamedTuple

import jax
import jax.numpy as jnp
from jax import lax
from jax.experimental import pallas as pl
from jax.experimental.pallas import tpu as pltpu

F32 = jnp.float32
BF16 = jnp.bfloat16
HI = lax.Precision.HIGHEST
MESH = pl.DeviceIdType.MESH

SWA_Q_HEADS = 16
SWA_KV_HEADS = 2
SWA_HEAD_DIM = 64
SWA_WINDOW = 128
SWA_SCALE = SWA_HEAD_DIM ** -0.5
assert math.frexp(SWA_SCALE)[0] == 0.5
GDN_HEADS = 4
GDN_HEAD_DIM = 128
GDN_CONV = 4
GDN_CHUNK = 64
XA_HEADS = 4
XA_HEAD_DIM = 128
RMS_EPS = 1e-6
L2_EPS = 1e-6
ADAM_LR = 0.001
ADAM_B1 = 0.9
ADAM_B2 = 0.999
ADAM_EPS = 1e-08
ADAM_WD = 0.01
ADAM_STEP = 10

LANES = 128
N_SHARDS = 4
VMEM_LIMIT = 56 * 1024 * 1024

NT = (((1,), (1,)), ((), ()))
TN = (((0,), (0,)), ((), ()))
NN = (((1,), (0,)), ((), ()))


def _cp(sem=None):
    return pltpu.CompilerParams(dimension_semantics=sem, vmem_limit_bytes=VMEM_LIMIT)


def _blk(dim, pref):
    if dim <= pref:
        return dim
    b = (pref // LANES) * LANES
    while dim % b:
        b -= LANES
    assert b > 0, (dim, pref)
    return b


def _dot(a, b, dims=NN, precision=None):
    return lax.dot_general(a, b, dims, precision=precision, preferred_element_type=F32)


def _sigmoid(x):
    return 0.5 * jnp.tanh(0.5 * x) + 0.5


MM_TK_BYTES = 4096


def _mm(a, b, *, name, ta=False, tb=False, out_dtypes=(F32,), epilogue=None, extras=(), tm=1024, tn=1024, tk=None,
        b_sharded=False, out_sharded=False, b_window=None, exchange=None):
    (kdim, m) = a.shape if ta else a.shape[::-1]
    col0 = 0
    n_lim = k_lim = None
    if b_sharded:
        ns, rows_w, per = b.shape
        if tb:
            kb, n, k_lim = ns * per, rows_w, per
        else:
            kb, n, n_lim = rows_w, ns * per, per
    else:
        (kb, n) = b.shape[::-1] if tb else b.shape
        if b_window is not None:
            assert not tb
            col0, n = b_window
    assert kdim == kb, (a.shape, b.shape, ta, tb)
    if out_sharded:
        assert n % N_SHARDS == 0
        n_lim = n // N_SHARDS if n_lim is None else n_lim
        assert n_lim == n // N_SHARDS
    if tk is None:
        tk = MM_TK_BYTES // max(a.dtype.itemsize, b.dtype.itemsize)
    tm, tn, tk = _blk(m, tm), _blk(n_lim or n, tn), _blk(k_lim or kdim, tk)
    assert col0 % tn == 0, (col0, tn)
    nk = kdim // tk
    a_spec = pl.BlockSpec((tk, tm), lambda i, j, k: (k, i)) if ta else pl.BlockSpec((tm, tk), lambda i, j, k: (i, k))
    if b_sharded and tb:
        kpb = k_lim // tk
        b_spec = pl.BlockSpec((None, tn, tk), lambda i, j, k: (k // kpb, j, k % kpb))
    elif b_sharded:
        bpb = n_lim // tn
        b_spec = pl.BlockSpec((None, tk, tn), lambda i, j, k: (j // bpb, k, j % bpb))
    elif tb:
        b_spec = pl.BlockSpec((tn, tk), lambda i, j, k: (j, k))
    else:
        b_spec = pl.BlockSpec((tk, tn), lambda i, j, k: (k, j + col0 // tn))
    x_spec = pl.BlockSpec((tm, tn), lambda i, j, k: (i, j))
    if out_sharded:
        opb = n_lim // tn
        o_spec = pl.BlockSpec((None, tm, tn), lambda i, j, k: (j // opb, i, j % opb))
        out_shape = (N_SHARDS, m, n_lim)
    else:
        o_spec, out_shape = x_spec, (m, n)
    dims = ((((0 if ta else 1),), ((1 if tb else 0),)), ((), ()))
    n_extra, n_out = len(extras), len(out_dtypes)

    host = _ExchangeHost(exchange)
    grid = (m // tm, n // tn, nk)

    def body(*refs):
        a_ref, b_ref = refs[:2]
        extra_refs = refs[2:2 + n_extra]
        out_refs = refs[2 + n_extra + host.n_in:2 + n_extra + host.n_in + n_out]
        host.start(refs, 2 + n_extra, 2 + n_extra + host.n_in + n_out, grid)
        part = _dot(a_ref[...].astype(BF16), b_ref[...].astype(BF16), dims)

        def finish(acc):
            vals = epilogue(acc, *[r[...] for r in extra_refs]) if epilogue is not None else (acc,) * n_out
            assert len(vals) == n_out
            for r, v in zip(out_refs, vals):
                r[...] = v.astype(r.dtype)

        if nk == 1:
            finish(part)
        else:
            acc_ref = refs[2 + n_extra + host.n_in + n_out + host.n_out]
            k = pl.program_id(2)

            @pl.when(k == 0)
            def _():
                acc_ref[...] = part

            @pl.when((k > 0) & (k < nk - 1))
            def _():
                acc_ref[...] += part

            @pl.when(k == nk - 1)
            def _():
                finish(acc_ref[...] + part)

        host.wait(refs, 2 + n_extra, 2 + n_extra + host.n_in + n_out, grid)

    outs = pl.pallas_call(
        body,
        grid=grid,
        in_specs=[a_spec, b_spec] + [x_spec] * n_extra + host.in_specs,
        out_specs=[o_spec] * n_out + host.out_specs,
        out_shape=[jax.ShapeDtypeStruct(out_shape, d) for d in out_dtypes] + host.out_shapes,
        scratch_shapes=([pltpu.VMEM((tm, tn), F32)] if nk > 1 else []) + host.scratch,
        input_output_aliases=host.aliases(2 + n_extra, n_out),
        compiler_params=_cp(host.semantics(("parallel", "parallel", "arbitrary"))),
        name=name,
    )(a, b, *extras, *host.ins)
    mine, landed = outs[:n_out], list(outs[n_out:])
    mine = mine[0] if n_out == 1 else mine
    return (mine, landed) if exchange is not None else mine


class Exchange(NamedTuple):
    ins: tuple
    out_shapes: tuple
    n_remote: int
    n_local: int
    copies_of: Callable
    aliases: tuple = ()


def _exchange_copies(ex, in_refs, out_refs, sem_refs):
    send_sems, recv_sems, local_sems = sem_refs
    remote, local = ex.copies_of(in_refs, out_refs, _place())
    assert len(remote) == ex.n_remote and len(local) == ex.n_local, (len(remote), len(local))
    cps = [pltpu.make_async_remote_copy(src_ref=src, dst_ref=dst, send_sem=send_sems.at[k], recv_sem=recv_sems.at[k],
                                        device_id=to, device_id_type=MESH) for k, (src, dst, to) in enumerate(remote)]
    cps += [pltpu.make_async_copy(src, dst, local_sems.at[k]) for k, (src, dst) in enumerate(local)]
    return cps


class _ExchangeHost:
    def __init__(self, ex):
        self.ex = ex
        self.ins = list(ex.ins) if ex else []
        self.out_shapes = list(ex.out_shapes) if ex else []
        self.n_in, self.n_out = len(self.ins), len(self.out_shapes)
        self.in_specs = [HBM_SPEC] * self.n_in
        self.out_specs = [HBM_SPEC] * self.n_out
        self.scratch = _sem_scratch(ex.n_remote, ex.n_local) if ex else []

    def semantics(self, sem):
        return tuple("arbitrary" for _ in sem) if self.ex else sem

    def aliases(self, in_at, out_at):
        return {in_at + i: out_at + o for i, o in self.ex.aliases} if self.ex else {}

    def _refs(self, refs, in_at, out_at):
        return refs[in_at:in_at + self.n_in], refs[out_at:out_at + self.n_out], refs[len(refs) - 3:]

    def _when(self, grid, last):
        cond = None
        for d, size in enumerate(grid):
            c = pl.program_id(d) == (size - 1 if last else 0)
            cond = c if cond is None else cond & c
        return cond

    def start(self, refs, in_at, out_at, grid):
        if self.ex:
            @pl.when(self._when(grid, False))
            def _():
                for cp in _exchange_copies(self.ex, *self._refs(refs, in_at, out_at)):
                    cp.start()

    def wait(self, refs, in_at, out_at, grid):
        if self.ex:
            @pl.when(self._when(grid, True))
            def _():
                for cp in _exchange_copies(self.ex, *self._refs(refs, in_at, out_at)):
                    cp.wait()


def _rms_fwd(x, g, *, name, tm=512, exchange=None):
    t, d = x.shape
    tm = _blk(t, tm)
    host = _ExchangeHost(exchange)
    grid = (t // tm,)

    def body(*refs):
        x_ref, g_ref, n_ref = refs[0], refs[1], refs[2 + host.n_in]
        host.start(refs, 2, 3 + host.n_in, grid)
        xv = x_ref[...]
        r = lax.rsqrt(jnp.mean(xv * xv, axis=-1, keepdims=True) + RMS_EPS)
        n_ref[...] = (xv * r * g_ref[...]).astype(n_ref.dtype)
        host.wait(refs, 2, 3 + host.n_in, grid)

    outs = pl.pallas_call(
        body, grid=grid,
        in_specs=[pl.BlockSpec((tm, d), lambda i: (i, 0)), pl.BlockSpec((1, d), lambda i: (0, 0))] + host.in_specs,
        out_specs=[pl.BlockSpec((tm, d), lambda i: (i, 0))] + host.out_specs,
        out_shape=[jax.ShapeDtypeStruct((t, d), BF16)] + host.out_shapes,
        scratch_shapes=host.scratch, input_output_aliases=host.aliases(2, 1),
        compiler_params=_cp(host.semantics(("parallel",))), name=name,
    )(x, g, *host.ins)
    return (outs[0], list(outs[1:])) if exchange is not None else outs[0]


def _rms_bwd(dn, x, g, dres, *, name, tm=512):
    t, d = x.shape
    tm = _blk(t, tm)

    def body(dn_ref, x_ref, g_ref, dres_ref, dx_ref, dxb_ref, dg_ref):
        i = pl.program_id(0)
        xv = x_ref[...]
        r = lax.rsqrt(jnp.mean(xv * xv, axis=-1, keepdims=True) + RMS_EPS)
        xh = xv * r
        dnv = dn_ref[...].astype(F32)
        dxh = dnv * g_ref[...]
        dx = dres_ref[...] + r * (dxh - xh * jnp.mean(dxh * xh, axis=-1, keepdims=True))
        dx_ref[...] = dx
        dxb_ref[...] = dx.astype(dxb_ref.dtype)
        part = jnp.sum(dnv * xh, axis=0, keepdims=True)

        @pl.when(i == 0)
        def _():
            dg_ref[...] = part

        @pl.when(i > 0)
        def _():
            dg_ref[...] += part

    row = pl.BlockSpec((tm, d), lambda i: (i, 0))
    vec = pl.BlockSpec((1, d), lambda i: (0, 0))
    return pl.pallas_call(
        body, grid=(t // tm,),
        in_specs=[row, row, vec, row], out_specs=[row, row, vec],
        out_shape=[jax.ShapeDtypeStruct((t, d), F32), jax.ShapeDtypeStruct((t, d), BF16),
                   jax.ShapeDtypeStruct((1, d), F32)],
        compiler_params=_cp(("arbitrary",)), name=name,
    )(dn, x, g, dres)


def _final_loss(h, g, tgt, *, name, tm=512):
    t, d = h.shape
    tm = _blk(t, tm)

    def body(h_ref, g_ref, t_ref, dh_ref, dhb_ref, dg_ref, loss_ref):
        i = pl.program_id(0)
        hv = h_ref[...]
        r = lax.rsqrt(jnp.mean(hv * hv, axis=-1, keepdims=True) + RMS_EPS)
        xh = hv * r
        e = xh * g_ref[...] - t_ref[...]
        dy = e * (1.0 / d)
        dxh = dy * g_ref[...]
        dh = r * (dxh - xh * jnp.mean(dxh * xh, axis=-1, keepdims=True))
        dh_ref[...] = dh
        dhb_ref[...] = dh.astype(dhb_ref.dtype)
        dg_part = jnp.sum(dy * xh, axis=0, keepdims=True)
        row_loss = jnp.sum(e * e, axis=-1, keepdims=True) * (0.5 / d)
        loss_part = jnp.sum(row_loss, axis=0, keepdims=True)

        @pl.when(i == 0)
        def _():
            dg_ref[...] = dg_part
            loss_ref[...] = jnp.broadcast_to(loss_part, loss_ref.shape)

        @pl.when(i > 0)
        def _():
            dg_ref[...] += dg_part
            loss_ref[...] += jnp.broadcast_to(loss_part, loss_ref.shape)

    row = pl.BlockSpec((tm, d), lambda i: (i, 0))
    vec = pl.BlockSpec((1, d), lambda i: (0, 0))
    return pl.pallas_call(
        body, grid=(t // tm,),
        in_specs=[row, vec, row], out_specs=[row, row, vec, pl.BlockSpec((1, LANES), lambda i: (0, 0))],
        out_shape=[jax.ShapeDtypeStruct((t, d), F32), jax.ShapeDtypeStruct((t, d), BF16),
                   jax.ShapeDtypeStruct((1, d), F32), jax.ShapeDtypeStruct((1, LANES), F32)],
        compiler_params=_cp(("arbitrary",)), name=name,
    )(h, g, tgt)


SWA_SUB = 64


def _swa_mask(n, rows, row0):
    w = SWA_WINDOW
    qi = (lax.broadcasted_iota(jnp.int32, (rows, 2 * w), 0) + row0) & (w - 1)
    kj = lax.broadcasted_iota(jnp.int32, (rows, 2 * w), 1)
    return (kj > qi) & (kj <= qi + w) & ((n > 0) | (kj >= w))


def _stack_heads(ref, heads, width):
    return jnp.concatenate([ref[:, h * width:(h + 1) * width] for h in heads], axis=0)


def _stack_scalars(ref, heads, rows):
    return jnp.concatenate([jnp.broadcast_to(ref[0:1, h:h + 1], (rows, 1)) for h in heads], axis=0)


def _swa_fwd(q, kv, sinks, *, name, exchange=None):
    t = q.shape[0]
    w, hd, hq, hkv = SWA_WINDOW, SWA_HEAD_DIM, SWA_Q_HEADS, SWA_KV_HEADS
    grp = hq // hkv
    kvw = hkv * hd
    nb = t // w
    host = _ExchangeHost(exchange)
    assert not (exchange and exchange.aliases)

    def body(*refs):
        q_ref, kvp_ref, kvc_ref, s_ref = refs[:4]
        o_ref, lse_ref = refs[4 + host.n_in:6 + host.n_in]
        host.start(refs, 4, 6 + host.n_in, (nb,))
        n = pl.program_id(0)
        mask = _swa_mask(n, grp * w, 0)
        kvcat = jnp.concatenate([kvp_ref[...], kvc_ref[...]], axis=0)
        outs, lses = [], []
        for hk in range(hkv):
            heads = range(hk * grp, (hk + 1) * grp)
            qs = _stack_heads(q_ref, heads, hd)
            kh = kvcat[:, hk * hd:(hk + 1) * hd]
            vh = kvcat[:, kvw + hk * hd:kvw + (hk + 1) * hd]
            sk = _stack_scalars(s_ref, heads, w)
            s = jnp.where(mask, _dot(qs * SWA_SCALE, kh, NT), -jnp.inf)
            m = jnp.maximum(jnp.max(s, axis=-1, keepdims=True), sk)
            p = jnp.exp(s - m)
            den = jnp.sum(p, axis=-1, keepdims=True) + jnp.exp(sk - m)
            o = _dot((p * (1.0 / den)).astype(BF16), vh)
            lse = m + jnp.log(den)
            outs += [o[j * w:(j + 1) * w] for j in range(grp)]
            lses += [lse[j * w:(j + 1) * w] for j in range(grp)]
        o_ref[...] = jnp.concatenate(outs, axis=1).astype(o_ref.dtype)
        lse_ref[...] = jnp.concatenate(lses, axis=1)
        host.wait(refs, 4, 6 + host.n_in, (nb,))

    outs = pl.pallas_call(
        body, grid=(nb,),
        in_specs=[pl.BlockSpec((w, hq * hd), lambda i: (i, 0)),
                  pl.BlockSpec((w, 2 * kvw), lambda i: (jnp.maximum(i - 1, 0), 0)),
                  pl.BlockSpec((w, 2 * kvw), lambda i: (i, 0)),
                  pl.BlockSpec((1, hq), lambda i: (0, 0))] + host.in_specs,
        out_specs=[pl.BlockSpec((w, hq * hd), lambda i: (i, 0)), pl.BlockSpec((w, hq), lambda i: (i, 0))] + host.out_specs,
        out_shape=[jax.ShapeDtypeStruct((t, hq * hd), BF16), jax.ShapeDtypeStruct((t, hq), F32)] + host.out_shapes,
        scratch_shapes=host.scratch,
        compiler_params=_cp(host.semantics(("parallel",))), name=name,
    )(q, kv, kv, sinks, *host.ins)
    return (outs[0], outs[1], list(outs[2:])) if exchange is not None else outs


ANY_SPEC = pl.BlockSpec(memory_space=pl.ANY)


def _swa_bwd(q, kv, sinks, o, lse, do, dp, dp_col, *, name):
    t = q.shape[0]
    w, hd, hq, hkv = SWA_WINDOW, SWA_HEAD_DIM, SWA_Q_HEADS, SWA_KV_HEADS
    grp = hq // hkv
    kvw = hkv * hd
    nb = t // w
    assert dp_col % (hq * hd) == 0
    dq_blk = dp_col // (hq * hd)

    def body(q_ref, kvp_ref, kvc_ref, s_ref, o_ref, lse_ref, do_ref, _, dq_ref, dkv_ref, ds_ref, carry_ref, s_scr, dp_scr,
             p_scr, ds_scr):
        n = pl.program_id(0)

        @pl.when(n == 0)
        def _():
            ds_ref[...] = jnp.zeros_like(ds_ref)
            carry_ref[...] = jnp.zeros_like(carry_ref)

        @pl.when(n < nb)
        def _():
            kvcat = jnp.concatenate([kvp_ref[...], kvc_ref[...]], axis=0)
            dqs, dsk, dks, dvs = [], [], [], []
            for hk in range(hkv):
                heads = range(hk * grp, (hk + 1) * grp)
                qs = _stack_heads(q_ref, heads, hd)
                dos = _stack_heads(do_ref, heads, hd)
                os_ = _stack_heads(o_ref, heads, hd)
                lse = _stack_heads(lse_ref, heads, 1)
                kh = kvcat[:, hk * hd:(hk + 1) * hd]
                vh = kvcat[:, kvw + hk * hd:kvw + (hk + 1) * hd]
                delta = jnp.sum(dos.astype(F32) * os_.astype(F32), axis=-1, keepdims=True)
                s_scr[...] = _dot(qs * SWA_SCALE, kh, NT)
                dp_scr[...] = _dot(dos, vh, NT)
                for r0 in range(0, grp * w, SWA_SUB):
                    rows = slice(r0, r0 + SWA_SUB)
                    p = jnp.exp(jnp.where(_swa_mask(n, SWA_SUB, r0 % w), s_scr[rows, :], -jnp.inf) - lse[rows])
                    p_scr[rows, :] = p.astype(p_scr.dtype)
                    ds_scr[rows, :] = (p * (dp_scr[rows, :] - delta[rows]) * SWA_SCALE).astype(ds_scr.dtype)
                ds = ds_scr[...]
                dq = _dot(ds, kh)
                dqs += [dq[j * w:(j + 1) * w] for j in range(grp)]
                dks.append(_dot(ds, qs, TN))
                dvs.append(_dot(p_scr[...], dos, TN))
                dsink = -jnp.exp(_stack_scalars(s_ref, heads, w) - lse) * delta
                dsk += [jnp.sum(dsink[j * w:(j + 1) * w], axis=0, keepdims=True) for j in range(grp)]
            dq_ref[...] = jnp.concatenate(dqs, axis=1).astype(dq_ref.dtype)
            ds_ref[...] += jnp.concatenate(dsk, axis=1)
            dkv_cat = jnp.concatenate(dks + dvs, axis=1)
            dkv_ref[...] = (carry_ref[...] + dkv_cat[:w]).astype(dkv_ref.dtype)
            carry_ref[...] = dkv_cat[w:]

        @pl.when(n == nb)
        def _():
            dkv_ref[...] = carry_ref[...].astype(dkv_ref.dtype)

    cur = lambda i: (jnp.minimum(i, nb - 1), 0)
    prev = lambda i: (jnp.clip(i - 1, 0, nb - 1), 0)
    return pl.pallas_call(
        body, grid=(nb + 1,),
        in_specs=[pl.BlockSpec((w, hq * hd), cur), pl.BlockSpec((w, 2 * kvw), prev), pl.BlockSpec((w, 2 * kvw), cur),
                  pl.BlockSpec((1, hq), lambda i: (0, 0)), pl.BlockSpec((w, hq * hd), cur),
                  pl.BlockSpec((w, hq), cur), pl.BlockSpec((w, hq * hd), cur), ANY_SPEC],
        out_specs=[pl.BlockSpec((w, hq * hd), lambda i: (jnp.minimum(i, nb - 1), dq_blk)),
                   pl.BlockSpec((w, 2 * kvw), prev), pl.BlockSpec((1, hq), lambda i: (0, 0))],
        out_shape=[jax.ShapeDtypeStruct(dp.shape, dp.dtype), jax.ShapeDtypeStruct((t, 2 * kvw), BF16),
                   jax.ShapeDtypeStruct((1, hq), F32)],
        scratch_shapes=[pltpu.VMEM((w, 2 * kvw), F32)] + [pltpu.VMEM((grp * w, 2 * w), dt) for dt in (F32, F32, BF16, BF16)],
        input_output_aliases={7: 0},
        compiler_params=_cp(("arbitrary",)), name=name,
    )(q, kv, kv, sinks, o, lse, do, dp)


def _xa_fwd(q, mkv, *, name, tq=512):
    t, xw = q.shape
    nm = mkv.shape[0]
    hd, nh = XA_HEAD_DIM, XA_HEADS
    tq = _blk(t, tq)

    def body(q_ref, mkv_ref, o_ref):
        outs = []
        for h in range(nh):
            qh = q_ref[:, h * hd:(h + 1) * hd]
            kh = mkv_ref[:, h * hd:(h + 1) * hd]
            vh = mkv_ref[:, xw + h * hd:xw + (h + 1) * hd]
            s = _dot(qh, kh, NT) * (hd ** -0.5)
            p = jnp.exp(s - jnp.max(s, axis=-1, keepdims=True))
            p = p * (1.0 / jnp.sum(p, axis=-1, keepdims=True))
            outs.append(_dot(p.astype(BF16), vh))
        o_ref[...] = jnp.concatenate(outs, axis=1).astype(o_ref.dtype)

    return pl.pallas_call(
        body, grid=(t // tq,),
        in_specs=[pl.BlockSpec((tq, xw), lambda i: (i, 0)), pl.BlockSpec((nm, 2 * xw), lambda i: (0, 0))],
        out_specs=pl.BlockSpec((tq, xw), lambda i: (i, 0)),
        out_shape=jax.ShapeDtypeStruct((t, xw), BF16),
        compiler_params=_cp(("parallel",)), name=name,
    )(q, mkv)


def _xa_bwd(q, mkv, do, dp, dp_col, *, name, tq=512):
    t, xw = q.shape
    nm = mkv.shape[0]
    hd, nh = XA_HEAD_DIM, XA_HEADS
    tq = _blk(t, tq)
    assert dp_col % xw == 0

    def body(q_ref, mkv_ref, do_ref, _, dq_ref, dmkv_ref):
        i = pl.program_id(0)
        dqs, dks, dvs = [], [], []
        for h in range(nh):
            qh = q_ref[:, h * hd:(h + 1) * hd]
            kh = mkv_ref[:, h * hd:(h + 1) * hd]
            vh = mkv_ref[:, xw + h * hd:xw + (h + 1) * hd]
            doh = do_ref[:, h * hd:(h + 1) * hd]
            s = _dot(qh, kh, NT) * (hd ** -0.5)
            p = jnp.exp(s - jnp.max(s, axis=-1, keepdims=True))
            p = p * (1.0 / jnp.sum(p, axis=-1, keepdims=True))
            dp = _dot(doh, vh, NT)
            ds = (p * (dp - jnp.sum(p * dp, axis=-1, keepdims=True)) * (hd ** -0.5)).astype(BF16)
            dqs.append(_dot(ds, kh))
            dks.append(_dot(ds, qh, TN))
            dvs.append(_dot(p.astype(BF16), doh, TN))
        dq_ref[...] = jnp.concatenate(dqs, axis=1).astype(dq_ref.dtype)
        part = jnp.concatenate(dks + dvs, axis=1)

        @pl.when(i == 0)
        def _():
            dmkv_ref[...] = part

        @pl.when(i > 0)
        def _():
            dmkv_ref[...] += part

    row = pl.BlockSpec((tq, xw), lambda i: (i, 0))
    full = pl.BlockSpec((nm, 2 * xw), lambda i: (0, 0))
    return pl.pallas_call(
        body, grid=(t // tq,),
        in_specs=[row, full, row, ANY_SPEC],
        out_specs=[pl.BlockSpec((tq, xw), lambda i: (i, dp_col // xw)), full],
        out_shape=[jax.ShapeDtypeStruct(dp.shape, dp.dtype), jax.ShapeDtypeStruct((nm, 2 * xw), F32)],
        input_output_aliases={3: 0}, compiler_params=_cp(("arbitrary",)), name=name,
    )(q, mkv, do, dp)


def _merge_specs(ys, ws, tm):
    y_specs = [pl.BlockSpec((tm, y.shape[1]), lambda i: (i, 0)) for y in ys]
    w_specs = [pl.BlockSpec(w.shape, lambda i: (0, 0, 0)) for w in ws]
    return y_specs, w_specs


def _merge_tiles(ws, tn):
    ns, _, per = ws[0].shape
    tn = _blk(per, tn)
    return tn, [(s, c, s * per + c) for s in range(ns) for c in range(0, per, tn)]


def _merge_fwd(ys, ws, gates, *, name, tm=256, tn=512):
    t, d = ys[0].shape[0], ws[0].shape[0] * ws[0].shape[2]
    tm = _blk(t, tm)
    tn, tiles = _merge_tiles(ws, tn)
    y_specs, w_specs = _merge_specs(ys, ws, tm)

    def body(ya, yb, yc, wa, wb, wc, g_ref, o_ref):
        for s, c, col in tiles:
            acc = None
            for b, (y, w) in enumerate(((ya, wa), (yb, wb), (yc, wc))):
                term = _sigmoid(g_ref[:, b * d + col:b * d + col + tn]) * _dot(y[...], w[s, :, c:c + tn])
                acc = term if acc is None else acc + term
            o_ref[:, col:col + tn] = acc.astype(o_ref.dtype)

    return pl.pallas_call(
        body, grid=(t // tm,),
        in_specs=y_specs + w_specs + [pl.BlockSpec((tm, 3 * d), lambda i: (i, 0))],
        out_specs=pl.BlockSpec((tm, d), lambda i: (i, 0)),
        out_shape=jax.ShapeDtypeStruct((t, d), BF16),
        compiler_params=_cp(("parallel",)), name=name,
    )(*ys, *ws, gates)


def _merge_bwd(ys, ws, gates, dmerged, dp_width, *, name, tm=256, tn=512):
    t, d = ys[0].shape[0], ws[0].shape[0] * ws[0].shape[2]
    tm = _blk(t, tm)
    tn, tiles = _merge_tiles(ws, tn)
    y_specs, w_specs = _merge_specs(ys, ws, tm)
    row = pl.BlockSpec((tm, d), lambda i: (i, 0))
    wide = pl.BlockSpec((tm, 3 * d), lambda i: (i, 0))

    def body(ya, yb, yc, wa, wb, wc, g_ref, dm_ref, dua, dub, duc, dp_ref):
        for s, c, col in tiles:
            dm = dm_ref[:, col:col + tn]
            for b, (y, w, du) in enumerate(((ya, wa, dua), (yb, wb, dub), (yc, wc, duc))):
                sg = _sigmoid(g_ref[:, b * d + col:b * d + col + tn])
                u = _dot(y[...], w[s, :, c:c + tn])
                du[:, col:col + tn] = (dm * sg).astype(du.dtype)
                dp_ref[:, b * d + col:b * d + col + tn] = (dm * u * sg * (1.0 - sg)).astype(dp_ref.dtype)

    return pl.pallas_call(
        body, grid=(t // tm,),
        in_specs=y_specs + w_specs + [wide, row],
        out_specs=[row] * 3 + [wide],
        out_shape=[jax.ShapeDtypeStruct((t, d), BF16)] * 3 + [jax.ShapeDtypeStruct((t, dp_width), BF16)],
        compiler_params=_cp(("parallel",)), name=name,
    )(*ys, *ws, gates, dmerged)


def _adamw(w, g, m, v, *, name, tm=256):
    lead = w.ndim - 2
    assert all(s == 1 for s in w.shape[:lead]) and m.shape == w.shape and v.shape == w.shape
    r, c = w.shape[lead:]
    assert g.shape == (r, c)
    tm = _blk(r, tm) if r % 8 == 0 else r
    tc = c if tm * c * 4 <= (4 << 20) else _blk(c, 256)
    ncb = c // tc
    bc1 = 1.0 - ADAM_B1 ** ADAM_STEP
    bc2 = 1.0 - ADAM_B2 ** ADAM_STEP

    def body(w_ref, g_ref, m_ref, v_ref, go_ref, d_ref, nm_ref, nv_ref):
        gv = g_ref[...]
        go_ref[...] = gv
        nm = ADAM_B1 * m_ref[...] + (1.0 - ADAM_B1) * gv
        nv = ADAM_B2 * v_ref[...] + (1.0 - ADAM_B2) * (gv * gv)
        d_ref[...] = -ADAM_LR * ((nm / bc1) / (jnp.sqrt(nv / bc2) + ADAM_EPS) + ADAM_WD * w_ref[...])
        nm_ref[...] = nm
        nv_ref[...] = nv

    spec = pl.BlockSpec((None,) * lead + (tm, tc), lambda i: (0,) * lead + (i // ncb, i % ncb))
    g_spec = pl.BlockSpec((tm, tc), lambda i: (i // ncb, i % ncb))
    return pl.pallas_call(
        body, grid=(r // tm * ncb,), in_specs=[spec, g_spec, spec, spec], out_specs=[spec] * 4,
        out_shape=[jax.ShapeDtypeStruct(w.shape, F32)] * 4,
        compiler_params=_cp(("parallel",)), name=name,
    )(w, g, m, v)


HALO = 8


def _shift_down(cur, prev, j):
    if j == 0:
        return cur
    y = pltpu.roll(cur, j, 0)
    row = lax.broadcasted_iota(jnp.int32, (HALO, cur.shape[1]), 0)
    top = jnp.where(row < j, pltpu.roll(prev, j, 0), y[:HALO])
    return jnp.concatenate([top, y[HALO:]], axis=0)


def _shift_up(cur, nxt, j):
    if j == 0:
        return cur
    tm = cur.shape[0]
    y = pltpu.roll(cur, tm - j, 0)
    row = lax.broadcasted_iota(jnp.int32, (HALO, cur.shape[1]), 0)
    bot = jnp.where(row >= HALO - j, pltpu.roll(nxt, HALO - j, 0), y[tm - HALO:])
    return jnp.concatenate([y[:tm - HALO], bot], axis=0)


def _softplus(x):
    return jnp.maximum(x, 0.0) + jnp.log(1.0 + jnp.exp(-jnp.abs(x)))


def _gdn_pre_fwd(qkvb, conv_w, ab, alog_pad, dt_pad, *, name, ab_blk=0, tm=256):
    t, cw = qkvb.shape
    hd, nh, ck = GDN_HEAD_DIM, GDN_HEADS, GDN_CHUNK
    gw = nh * hd
    tm = _blk(t, tm)
    hb = tm // HALO

    def body(x_ref, xp_ref, w_ref, ab_ref, al_ref, dt_ref, xc_ref, qkvn_ref, aux_ref):
        i = pl.program_id(0)
        cur = x_ref[...]
        prev = jnp.where(i > 0, xp_ref[...], 0.0)
        xc = None
        for tap in range(GDN_CONV):
            term = w_ref[tap:tap + 1, :] * _shift_down(cur, prev, GDN_CONV - 1 - tap)
            xc = term if xc is None else xc + term
        xc_ref[...] = xc
        s = xc * _sigmoid(xc)
        for h in range(2 * nh):
            xh = s[:, h * hd:(h + 1) * hd]
            r = lax.rsqrt(jnp.sum(xh * xh, axis=-1, keepdims=True) + L2_EPS)
            scale = hd ** -0.5 if h < nh else 1.0
            qkvn_ref[:, h * hd:(h + 1) * hd] = xh * (r * scale)
        qkvn_ref[:, 2 * gw:] = s[:, 2 * gw:]
        abv = ab_ref[...]
        lane = lax.broadcasted_iota(jnp.int32, abv.shape, 1)
        g = jnp.where(lane < nh, -jnp.exp(al_ref[...]) * _softplus(abv + dt_ref[...]), 0.0)
        beta = jnp.where((lane >= nh) & (lane < 2 * nh), _sigmoid(abv), 0.0)
        ii = lax.broadcasted_iota(jnp.int32, (tm, tm), 0)
        jj = lax.broadcasted_iota(jnp.int32, (tm, tm), 1)
        tri = jnp.where((ii >= jj) & ((ii ^ jj) < ck), 1.0, 0.0)
        gcum = _dot(tri, g, precision=HI)
        aux_ref[...] = g + beta + pltpu.roll(gcum, 2 * nh, 1)

    row = lambda c: pl.BlockSpec((tm, c), lambda i: (i, 0))
    vec = lambda r, c: pl.BlockSpec((r, c), lambda i: (0, 0))
    return pl.pallas_call(
        body, grid=(t // tm,),
        in_specs=[row(cw), pl.BlockSpec((HALO, cw), lambda i: (jnp.maximum(i * hb - 1, 0), 0)), vec(GDN_CONV, cw),
                  pl.BlockSpec((tm, LANES), lambda i: (i, ab_blk)), vec(1, LANES), vec(1, LANES)],
        out_specs=[row(cw), row(cw), row(LANES)],
        out_shape=[jax.ShapeDtypeStruct((t, cw), F32), jax.ShapeDtypeStruct((t, cw), F32),
                   jax.ShapeDtypeStruct((t, LANES), F32)],
        compiler_params=_cp(("parallel",)), name=name,
    )(qkvb, qkvb, conv_w, ab, alog_pad, dt_pad)


GDN_STEP_CHUNKS = 4
GDN_ILP_CHUNKS = 4
GDN_ILP_CHUNKS_BWD = 4


def _bdot(a, b, dims=NN):
    return _dot(a.astype(BF16), b.astype(BF16), dims)


def _split_bf16(x):
    hi = x.astype(BF16)
    return hi, (x - hi.astype(F32)).astype(BF16)


def _dot3(a, b, dims=NN):
    ah, al = _split_bf16(a)
    bh, bl = _split_bf16(b)
    return _dot(ah, bh, dims) + (_dot(ah, bl, dims) + _dot(al, bh, dims))


def _dot3_many(lhs, rhs, dims=NN):
    sa = [_split_bf16(a) for a in lhs]
    sb = [_split_bf16(b) for b in rhs]
    hh = [_dot(a[0], b[0], dims) for a, b in zip(sa, sb)]
    hl = [_dot(a[0], b[1], dims) for a, b in zip(sa, sb)]
    lh = [_dot(a[1], b[0], dims) for a, b in zip(sa, sb)]
    return [x + (y + z) for x, y, z in zip(hh, hl, lh)]


def _gdn_local(chains, with_inverse):
    ck = GDN_CHUNK
    ii = lax.broadcasted_iota(jnp.int32, (ck, ck), 0)
    jj = lax.broadcasted_iota(jnp.int32, (ck, ck), 1)
    lower, strict = ii >= jj, ii > jj
    dmat = [jnp.exp(jnp.where(lower, gc - gc_row, -jnp.inf)) for _, _, _, gc, gc_row in chains]
    kk = [_bdot(k, k, NT) for _, k, _, _, _ in chains]
    qk = [_bdot(q, k, NT) for q, k, _, _, _ in chains]
    tinv = [None] * len(chains)
    if with_inverse:
        lmat = [jnp.where(strict, c[2] * kk_i * d_i, 0.0) for c, kk_i, d_i in zip(chains, kk, dmat)]
        eye = jnp.where(ii == jj, 1.0, 0.0)
        tinv = [eye - l_i for l_i in lmat]
        pw = lmat
        for _ in range(int(math.log2(ck)) - 1):
            pw = _dot3_many(pw, pw)
            tinv = [t_i + d_i for t_i, d_i in zip(tinv, _dot3_many(tinv, pw))]
    out = []
    for (q, k, b, gc, gc_row), dmat_i, kk_i, qk_i, tinv_i in zip(chains, dmat, kk, qk, tinv):
        gl = gc[ck - 1:ck, :]
        out.append(dict(lower=lower, strict=strict, dmat=dmat_i, kk=kk_i, tinv=tinv_i, gam=jnp.exp(gc), qk=qk_i,
                        mm=qk_i * dmat_i, kdec=jnp.exp(gl - gc)))
    return out


def _gdn_head_cols(h):
    return slice(h * GDN_HEAD_DIM, (h + 1) * GDN_HEAD_DIM)


def _gdn_chunk_inputs(x_ref, aux_ref, auxt_ref, g, h):
    nh, ck = GDN_HEADS, GDN_CHUNK
    gw = nh * GDN_HEAD_DIM
    rows = slice(g * ck, (g + 1) * ck)
    cols = _gdn_head_cols(h)
    q = x_ref[rows, cols]
    k = x_ref[rows, gw + cols.start:gw + cols.stop]
    v = x_ref[rows, 2 * gw + cols.start:2 * gw + cols.stop]
    b = aux_ref[rows, nh + h:nh + h + 1]
    gc = aux_ref[rows, 2 * nh + h:2 * nh + h + 1]
    gc_row = auxt_ref[g, 2 * nh + h:2 * nh + h + 1, :]
    return q, k, v, b, gc, gc_row


def _gdn_specs(t, widths, *, reverse=False, step_chunks=None):
    rows = (step_chunks or GDN_STEP_CHUNKS) * GDN_CHUNK
    nsteps = t // rows
    idx = (lambda i: (nsteps - 1 - i, 0)) if reverse else (lambda i: (i, 0))
    return [pl.BlockSpec((rows, w), idx) for w in widths]


def _gdn_local_fwd(qkvn, aux, aux_t, *, name, exchange=None):
    t = qkvn.shape[0]
    hd, nh, ck, gs = GDN_HEAD_DIM, GDN_HEADS, GDN_CHUNK, GDN_STEP_CHUNKS
    gw = nh * hd
    host = _ExchangeHost(exchange)
    assert not (exchange and exchange.aliases)
    grid = (t // (gs * ck),)

    def body(*refs):
        x_ref, aux_ref, auxt_ref = refs[:3]
        u_ref, w_ref, qd_ref, kd_ref, mm_ref, tinv_ref = refs[3 + host.n_in:9 + host.n_in]
        host.start(refs, 3, 9 + host.n_in, grid)
        for g0 in range(0, gs, GDN_ILP_CHUNKS):
            where = [(g, h) for g in range(g0, g0 + GDN_ILP_CHUNKS) for h in range(nh)]
            ins = [_gdn_chunk_inputs(x_ref, aux_ref, auxt_ref, g, h) for g, h in where]
            lcs = _gdn_local([(q, k, b, gc, gc_row) for q, k, _, b, gc, gc_row in ins], True)
            tinvs = [lc["tinv"] for lc in lcs]
            us = _dot3_many(tinvs, [b * v for _, _, v, b, _, _ in ins])
            ws = _dot3_many(tinvs, [(b * lc["gam"]) * k for (_, k, _, b, _, _), lc in zip(ins, lcs)])
            for i, ((g, h), (q, k, _, _, _, _), lc) in enumerate(zip(where, ins, lcs)):
                rows, cols = slice(g * ck, (g + 1) * ck), _gdn_head_cols(h)
                u_ref[rows, cols] = us[i]
                w_ref[rows, cols] = ws[i].astype(w_ref.dtype)
                qd_ref[rows, cols] = (lc["gam"] * q).astype(qd_ref.dtype)
                kd_ref[rows, cols] = (lc["kdec"] * k).astype(kd_ref.dtype)
            for g in range(g0, g0 + GDN_ILP_CHUNKS):
                rows = slice(g * ck, (g + 1) * ck)
                mine = [lc for (gg, _), lc in zip(where, lcs) if gg == g]
                mm_ref[rows, :] = jnp.concatenate([lc["mm"] for lc in mine], axis=1).astype(mm_ref.dtype)
                tinv_ref[rows, :] = jnp.concatenate([lc["tinv"] for lc in mine], axis=1)
        host.wait(refs, 3, 9 + host.n_in, grid)

    sq = nh * ck
    outs = pl.pallas_call(
        body, grid=grid,
        in_specs=_gdn_specs(t, (3 * gw, LANES)) + [pl.BlockSpec((gs, 16, ck), lambda i: (i, 0, 0))] + host.in_specs,
        out_specs=_gdn_specs(t, (gw, gw, gw, gw, sq, sq)) + host.out_specs,
        out_shape=[jax.ShapeDtypeStruct((t, gw), F32)] + [jax.ShapeDtypeStruct((t, gw), BF16)] * 3
        + [jax.ShapeDtypeStruct((t, sq), BF16), jax.ShapeDtypeStruct((t, sq), F32)] + host.out_shapes,
        scratch_shapes=host.scratch,
        compiler_params=_cp(host.semantics(("parallel",))), name=name,
    )(qkvn, aux, aux_t, *host.ins)
    return (*outs[:6], list(outs[6:])) if exchange is not None else outs


def _gdn_seq_fwd(u, w, qd, kd, mm, aux, *, name):
    t = u.shape[0]
    hd, nh, ck, gs = GDN_HEAD_DIM, GDN_HEADS, GDN_CHUNK, GDN_STEP_CHUNKS
    gw = nh * hd
    sq = nh * ck

    def body(u_ref, w_ref, qd_ref, kd_ref, mm_ref, aux_ref, o_ref, vn_ref, sall_ref, s_ref):
        @pl.when(pl.program_id(0) == 0)
        def _():
            s_ref[...] = jnp.zeros_like(s_ref)

        heads = range(nh)
        hcols = [_gdn_head_cols(h) for h in heads]
        sts = [s_ref[h] for h in heads]
        for g in range(gs):
            rows = slice(g * ck, (g + 1) * ck)
            last = (g + 1) * ck - 1
            for h in heads:
                sall_ref[g, h] = sts[h]
            stbs = [st.astype(BF16) for st in sts]
            w_s = [_dot(w_ref[rows, c], stb) for c, stb in zip(hcols, stbs)]
            q_s = [_dot(qd_ref[rows, c], stb) for c, stb in zip(hcols, stbs)]
            vnbs = [(u_ref[rows, c] - ws).astype(BF16) for c, ws in zip(hcols, w_s)]
            m_v = [_dot(mm_ref[rows, h * ck:(h + 1) * ck], vnbs[h]) for h in heads]
            k_v = [_dot(kd_ref[rows, c], vnb, TN) for c, vnb in zip(hcols, vnbs)]
            for h, c in zip(heads, hcols):
                vn_ref[rows, c] = vnbs[h]
                o_ref[rows, c] = q_s[h] + m_v[h]
            gam_c = [jnp.exp(aux_ref[last:last + 1, 2 * nh + h:2 * nh + h + 1]) for h in heads]
            sts = [gam_c[h] * sts[h] + k_v[h] for h in heads]
        for h in heads:
            s_ref[h] = sts[h]

    return pl.pallas_call(
        body, grid=(t // (gs * ck),),
        in_specs=_gdn_specs(t, (gw, gw, gw, gw, sq, LANES)),
        out_specs=_gdn_specs(t, (gw, gw)) + [pl.BlockSpec((gs, nh, hd, hd), lambda i: (i, 0, 0, 0))],
        out_shape=[jax.ShapeDtypeStruct((t, gw), F32), jax.ShapeDtypeStruct((t, gw), BF16),
                   jax.ShapeDtypeStruct((t // ck, nh, hd, hd), F32)],
        scratch_shapes=[pltpu.VMEM((nh, hd, hd), F32)],
        compiler_params=_cp(("arbitrary",)), name=name,
    )(u, w, qd, kd, mm, aux)


def _gdn_seq_bwd(do, w, qd, kd, mm, vn, s_all, aux, *, name):
    t = do.shape[0]
    hd, nh, ck, gs = GDN_HEAD_DIM, GDN_HEADS, GDN_CHUNK, GDN_STEP_CHUNKS
    gw = nh * hd
    sq = nh * ck
    nsteps = t // (gs * ck)

    def body(do_ref, w_ref, qd_ref, kd_ref, mm_ref, vn_ref, sall_ref, aux_ref, dvn_ref, dqd_ref, dkd_ref, dw_ref,
             dlast_ref, ds_ref):
        @pl.when(pl.program_id(0) == 0)
        def _():
            ds_ref[...] = jnp.zeros_like(ds_ref)

        lane = lax.broadcasted_iota(jnp.int32, (ck, LANES), 1)
        rowi = lax.broadcasted_iota(jnp.int32, (ck, LANES), 0)
        heads = range(nh)
        hcols = [_gdn_head_cols(h) for h in heads]
        dsns = [ds_ref[h] for h in heads]
        for g in reversed(range(gs)):
            rows = slice(g * ck, (g + 1) * ck)
            last = (g + 1) * ck - 1
            sts = [sall_ref[g, h] for h in heads]
            stbs = [st.astype(BF16) for st in sts]
            dsbs = [dsn.astype(BF16) for dsn in dsns]
            dobs = [do_ref[rows, c].astype(BF16) for c in hcols]
            dvns = [_dot(mm_ref[rows, h * ck:(h + 1) * ck], dobs[h], TN) + _dot(kd_ref[rows, hcols[h]], dsbs[h])
                    for h in heads]
            dqds = [_dot(dob, stb, NT) for dob, stb in zip(dobs, stbs)]
            dkds = [_dot(vn_ref[rows, c], dsb, NT) for c, dsb in zip(hcols, dsbs)]
            q_o = [_dot(qd_ref[rows, c], dob, TN) for c, dob in zip(hcols, dobs)]
            dvbs = [dvn.astype(BF16) for dvn in dvns]
            dws = [_dot(dvb, stb, NT) for dvb, stb in zip(dvbs, stbs)]
            w_v = [_dot(w_ref[rows, c], dvb, TN) for c, dvb in zip(hcols, dvbs)]
            gam_c = [jnp.exp(aux_ref[last:last + 1, 2 * nh + h:2 * nh + h + 1]) for h in heads]
            dlast = jnp.zeros((ck, LANES), F32)
            for h, c in zip(heads, hcols):
                dvn_ref[rows, c] = dvns[h]
                dqd_ref[rows, c] = dqds[h]
                dkd_ref[rows, c] = dkds[h]
                dw_ref[rows, c] = -dws[h]
                dgam_c = jnp.sum(jnp.sum(dsns[h] * sts[h], axis=1, keepdims=True), axis=0, keepdims=True)
                dlast = dlast + jnp.where((rowi == ck - 1) & (lane == h), gam_c[h] * dgam_c, 0.0)
            dlast_ref[rows, :] = dlast
            dsns = [q_o[h] + gam_c[h] * dsns[h] - w_v[h] for h in heads]
        for h in heads:
            ds_ref[h] = dsns[h]

    return pl.pallas_call(
        body, grid=(nsteps,),
        in_specs=_gdn_specs(t, (gw, gw, gw, gw, sq, gw), reverse=True)
        + [pl.BlockSpec((gs, nh, hd, hd), lambda i: (nsteps - 1 - i, 0, 0, 0))] + _gdn_specs(t, (LANES,), reverse=True),
        out_specs=_gdn_specs(t, (gw, gw, gw, gw, LANES), reverse=True),
        out_shape=[jax.ShapeDtypeStruct((t, gw), F32)] * 4 + [jax.ShapeDtypeStruct((t, LANES), F32)],
        scratch_shapes=[pltpu.VMEM((nh, hd, hd), F32)],
        compiler_params=_cp(("arbitrary",)), name=name,
    )(do, w, qd, kd, mm, vn, s_all, aux)


def _gdn_local_bwd(qkvn, aux, aux_t, tinv, u, w, vn, do, dvn, dqd, dkd, dw, dlast, *, name):
    t = qkvn.shape[0]
    hd, nh, ck, gs = GDN_HEAD_DIM, GDN_HEADS, GDN_CHUNK, GDN_STEP_CHUNKS
    gw = nh * hd
    sq = nh * ck

    def body(x_ref, aux_ref, auxt_ref, tinv_ref, u_ref, w_ref, vn_ref, do_ref, dvn_ref, dqd_ref, dkd_ref, dw_ref,
             dlast_ref, dx_ref, daux_ref):
        lane = lax.broadcasted_iota(jnp.int32, (ck, LANES), 1)
        ones = jnp.ones((ck, LANES), F32)
        ii = lax.broadcasted_iota(jnp.int32, (ck, ck), 0)
        jj = lax.broadcasted_iota(jnp.int32, (ck, ck), 1)
        suffix = jnp.where(jj >= ii, 1.0, 0.0)
        for g0 in range(0, gs, GDN_ILP_CHUNKS_BWD):
            where = [(g, h) for g in range(g0, g0 + GDN_ILP_CHUNKS_BWD) for h in range(nh)]
            at = [(slice(g * ck, (g + 1) * ck), _gdn_head_cols(h)) for g, h in where]
            ins = [_gdn_chunk_inputs(x_ref, aux_ref, auxt_ref, g, h) for g, h in where]
            lcs = _gdn_local([(q, k, b, gc, gc_row) for q, k, _, b, gc, gc_row in ins], False)
            tinvs = [tinv_ref[slice(g * ck, (g + 1) * ck), h * ck:(h + 1) * ck] for g, h in where]
            dms = [jnp.where(lc["lower"], _bdot(do_ref[r, c], vn_ref[r, c], NT), 0.0) for lc, (r, c) in zip(lcs, at)]
            drvs = _dot3_many(tinvs, [dvn_ref[r, c] for r, c in at], TN)
            drks = _dot3_many(tinvs, [dw_ref[r, c] for r, c in at], TN)
            das = [jnp.where(lc["strict"], -(_bdot(drv, u_ref[r, c], NT) + _bdot(drk, w_ref[r, c], NT)), 0.0)
                   for lc, (r, c), drv, drk in zip(lcs, at, drvs, drks)]
            f_mats = [da * (i[3] * lc["kk"]) * lc["dmat"] + dm * lc["qk"] * lc["dmat"]
                      for i, lc, da, dm in zip(ins, lcs, das, dms)]
            col_sums = _dot3_many(f_mats, [ones] * len(where), TN)
            dgc_all = {g: dlast_ref[slice(g * ck, (g + 1) * ck), :] for g in range(g0, g0 + GDN_ILP_CHUNKS_BWD)}
            db_all = {g: jnp.zeros((ck, LANES), F32) for g in range(g0, g0 + GDN_ILP_CHUNKS_BWD)}
            e_mats = [da * lc["dmat"] * i[3] for i, lc, da in zip(ins, lcs, das)]
            dmds = [dm * lc["dmat"] for lc, dm in zip(lcs, dms)]
            dq_mm = [_bdot(dmd, i[1]) for i, dmd in zip(ins, dmds)]
            dk_mm = [_bdot(e, i[1]) + _bdot(e, i[1], TN) + _bdot(dmd, i[0], TN) for i, e, dmd in zip(ins, e_mats, dmds)]
            for n, ((g, h), (q, k, v, b, _, _), lc, (rows, cols)) in enumerate(zip(where, ins, lcs, at)):
                dmat, kk, gam, kdec = (lc[key] for key in ("dmat", "kk", "gam", "kdec"))
                drv, drk, da = drvs[n], drks[n], das[n]
                dqd_h, dkd_h = dqd_ref[rows, cols], dkd_ref[rows, cols]
                rs_rk = jnp.sum(drk * k, axis=-1, keepdims=True)
                db = (jnp.sum(drv * v, axis=-1, keepdims=True) + gam * rs_rk
                      + jnp.sum(da * kk * dmat, axis=-1, keepdims=True))
                dx_ref[rows, cols] = dq_mm[n] + gam * dqd_h
                dx_ref[rows, gw + cols.start:gw + cols.stop] = (b * gam) * drk + dk_mm[n] + kdec * dkd_h
                dx_ref[rows, 2 * gw + cols.start:2 * gw + cols.stop] = b * drv
                e_vec = jnp.sum(dkd_h * (kdec * k), axis=-1, keepdims=True)
                dgc = (b * gam * rs_rk + gam * jnp.sum(dqd_h * q, axis=-1, keepdims=True)
                       + jnp.sum(f_mats[n], axis=-1, keepdims=True) - col_sums[n][:, 0:1] - e_vec)
                is_last = lax.broadcasted_iota(jnp.int32, (ck, 1), 0) == ck - 1
                dgc = dgc + jnp.where(is_last, jnp.sum(e_vec, axis=0, keepdims=True), 0.0)
                dgc_all[g] = dgc_all[g] + jnp.where(lane == h, dgc, 0.0)
                db_all[g] = db_all[g] + jnp.where(lane == nh + h, db, 0.0)
            for g in dgc_all:
                daux_ref[slice(g * ck, (g + 1) * ck), :] = _dot3(suffix, dgc_all[g]) + db_all[g]

    return pl.pallas_call(
        body, grid=(t // (gs * ck),),
        in_specs=_gdn_specs(t, (3 * gw, LANES)) + [pl.BlockSpec((gs, 16, ck), lambda i: (i, 0, 0))]
        + _gdn_specs(t, (sq, gw, gw, gw, gw, gw, gw, gw, gw, LANES)),
        out_specs=_gdn_specs(t, (3 * gw, LANES)),
        out_shape=[jax.ShapeDtypeStruct((t, 3 * gw), F32), jax.ShapeDtypeStruct((t, LANES), F32)],
        compiler_params=_cp(("parallel",)), name=name,
    )(qkvn, aux, aux_t, tinv, u, w, vn, do, dvn, dqd, dkd, dw, dlast)


def _gdn_pre_bwd1(xc, dqkvn, daux, ab, alog_pad, dt_pad, dkv, dp, dp_col, *, name, ab_blk=0, tm=256):
    t, cw = xc.shape
    hd, nh = GDN_HEAD_DIM, GDN_HEADS
    gw = nh * hd
    tm = _blk(t, tm)

    kvw = dkv.shape[1]
    seg = kvw + AB_PAD
    assert dp_col % seg == 0

    def body(xc_ref, dy_ref, daux_ref, ab_ref, al_ref, dt_ref, dkv_ref, _, dxc_ref, dab_ref, dal_ref, ddt_ref):
        i = pl.program_id(0)
        xc = xc_ref[...]
        sg = _sigmoid(xc)
        s = xc * sg
        dsilu = sg * (1.0 + xc * (1.0 - sg))
        for h in range(2 * nh):
            xh = s[:, h * hd:(h + 1) * hd]
            scale = hd ** -0.5 if h < nh else 1.0
            dyh = dy_ref[:, h * hd:(h + 1) * hd] * scale
            r = lax.rsqrt(jnp.sum(xh * xh, axis=-1, keepdims=True) + L2_EPS)
            dxh = r * dyh - xh * (r * r * r) * jnp.sum(dyh * xh, axis=-1, keepdims=True)
            dxc_ref[:, h * hd:(h + 1) * hd] = dxh * dsilu[:, h * hd:(h + 1) * hd]
        dxc_ref[:, 2 * gw:] = dy_ref[:, 2 * gw:] * dsilu[:, 2 * gw:]
        abv = ab_ref[...]
        dauxv = daux_ref[...]
        lane = lax.broadcasted_iota(jnp.int32, abv.shape, 1)
        is_a = lane < nh
        is_b = (lane >= nh) & (lane < 2 * nh)
        pre = abv + dt_ref[...]
        neg_ea = -jnp.exp(al_ref[...])
        d_a = jnp.where(is_a, dauxv * neg_ea * _sigmoid(pre), 0.0)
        beta = _sigmoid(abv)
        d_b = jnp.where(is_b, dauxv * beta * (1.0 - beta), 0.0)
        dab_ref[:, :kvw] = dkv_ref[...]
        dab_ref[:, kvw:kvw + LANES] = (d_a + d_b).astype(dab_ref.dtype)
        dab_ref[:, kvw + LANES:] = jnp.zeros((tm, AB_PAD - LANES), dab_ref.dtype)
        dal = jnp.sum(jnp.where(is_a, dauxv * neg_ea * _softplus(pre), 0.0), axis=0, keepdims=True)
        ddt = jnp.sum(d_a, axis=0, keepdims=True)

        @pl.when(i == 0)
        def _():
            dal_ref[...] = dal
            ddt_ref[...] = ddt

        @pl.when(i > 0)
        def _():
            dal_ref[...] += dal
            ddt_ref[...] += ddt

    row = lambda c: pl.BlockSpec((tm, c), lambda i: (i, 0))
    vec = pl.BlockSpec((1, LANES), lambda i: (0, 0))
    return pl.pallas_call(
        body, grid=(t // tm,),
        in_specs=[row(cw), row(cw), row(LANES), pl.BlockSpec((tm, LANES), lambda i: (i, ab_blk)), vec, vec, row(kvw),
                  ANY_SPEC],
        out_specs=[row(cw), pl.BlockSpec((tm, seg), lambda i: (i, dp_col // seg)), vec, vec],
        out_shape=[jax.ShapeDtypeStruct((t, cw), F32), jax.ShapeDtypeStruct(dp.shape, dp.dtype),
                   jax.ShapeDtypeStruct((1, LANES), F32), jax.ShapeDtypeStruct((1, LANES), F32)],
        input_output_aliases={7: 1}, compiler_params=_cp(("arbitrary",)), name=name,
    )(xc, dqkvn, daux, ab, alog_pad, dt_pad, dkv, dp)


def _gdn_pre_bwd2(dxc, qkvb, conv_w, dp, dp_col, *, name, tm=512):
    t, cw = dxc.shape
    tm = _blk(t, tm)
    hb = tm // HALO
    nblk = t // tm
    cg = GDN_HEADS * GDN_HEAD_DIM
    assert cw % cg == 0 and dp_col % cg == 0
    col0 = dp_col // cg

    def body(d_ref, dn_ref, x_ref, xp_ref, w_ref, _, dx_ref, dw_ref):
        i = pl.program_id(1)
        dcur = d_ref[...]
        dnxt = jnp.where(i < nblk - 1, dn_ref[...], 0.0)
        cur = x_ref[...]
        prev = jnp.where(i > 0, xp_ref[...], 0.0)
        dx = None
        dws = []
        for tap in range(GDN_CONV):
            j = GDN_CONV - 1 - tap
            term = w_ref[tap:tap + 1, :] * _shift_up(dcur, dnxt, j)
            dx = term if dx is None else dx + term
            dws.append(jnp.sum(dcur * _shift_down(cur, prev, j), axis=0, keepdims=True))
        dx_ref[...] = dx.astype(dx_ref.dtype)
        dw = jnp.concatenate(dws, axis=0)

        @pl.when(i == 0)
        def _():
            dw_ref[...] = dw

        @pl.when(i > 0)
        def _():
            dw_ref[...] += dw

    row = pl.BlockSpec((tm, cg), lambda c, i: (i, c))
    wsp = pl.BlockSpec((GDN_CONV, cg), lambda c, i: (0, c))
    return pl.pallas_call(
        body, grid=(cw // cg, nblk),
        in_specs=[row, pl.BlockSpec((HALO, cg), lambda c, i: (jnp.minimum((i + 1) * hb, t // HALO - 1), c)),
                  row, pl.BlockSpec((HALO, cg), lambda c, i: (jnp.maximum(i * hb - 1, 0), c)), wsp, ANY_SPEC],
        out_specs=[pl.BlockSpec((tm, cg), lambda c, i: (i, col0 + c)), wsp],
        out_shape=[jax.ShapeDtypeStruct(dp.shape, dp.dtype), jax.ShapeDtypeStruct((GDN_CONV, cw), F32)],
        input_output_aliases={5: 0}, compiler_params=_cp(("arbitrary", "arbitrary")), name=name,
    )(dxc, dxc, qkvb, qkvb, conv_w, dp)


def _gdn_post_fwd(o, z, norm_w, *, name, tm=512):
    t, gw = o.shape
    hd, nh = GDN_HEAD_DIM, GDN_HEADS
    tm = _blk(t, tm)

    def body(o_ref, z_ref, w_ref, y_ref):
        zv = z_ref[...]
        sz = zv * _sigmoid(zv)
        for h in range(nh):
            oh = o_ref[:, h * hd:(h + 1) * hd]
            r = lax.rsqrt(jnp.mean(oh * oh, axis=-1, keepdims=True) + RMS_EPS)
            y_ref[:, h * hd:(h + 1) * hd] = (oh * r * w_ref[...] * sz[:, h * hd:(h + 1) * hd]).astype(y_ref.dtype)

    row = pl.BlockSpec((tm, gw), lambda i: (i, 0))
    return pl.pallas_call(
        body, grid=(t // tm,), in_specs=[row, row, pl.BlockSpec((1, hd), lambda i: (0, 0))], out_specs=row,
        out_shape=jax.ShapeDtypeStruct((t, gw), BF16), compiler_params=_cp(("parallel",)), name=name,
    )(o, z, norm_w)


def _gdn_post_bwd(dy, o, z, norm_w, dp, dp_col, *, name, tm=512):
    t, gw = o.shape
    hd, nh = GDN_HEAD_DIM, GDN_HEADS
    tm = _blk(t, tm)

    def body(dy_ref, o_ref, z_ref, w_ref, _, do_ref, dz_ref, dw_ref):
        i = pl.program_id(0)
        zv = z_ref[...]
        sg = _sigmoid(zv)
        sz = zv * sg
        dsz = sg * (1.0 + zv * (1.0 - sg))
        dw = None
        for h in range(nh):
            sl = slice(h * hd, (h + 1) * hd)
            oh = o_ref[:, sl]
            dyh = dy_ref[:, sl].astype(F32)
            r = lax.rsqrt(jnp.mean(oh * oh, axis=-1, keepdims=True) + RMS_EPS)
            xh = oh * r
            dz_ref[:, sl] = (dyh * xh * w_ref[...] * dsz[:, sl]).astype(dz_ref.dtype)
            dn = dyh * sz[:, sl]
            dxh = dn * w_ref[...]
            do_ref[:, sl] = r * (dxh - xh * jnp.mean(dxh * xh, axis=-1, keepdims=True))
            part = jnp.sum(dn * xh, axis=0, keepdims=True)
            dw = part if dw is None else dw + part

        @pl.when(i == 0)
        def _():
            dw_ref[...] = dw

        @pl.when(i > 0)
        def _():
            dw_ref[...] += dw

    row = pl.BlockSpec((tm, gw), lambda i: (i, 0))
    vec = pl.BlockSpec((1, hd), lambda i: (0, 0))
    return pl.pallas_call(
        body, grid=(t // tm,), in_specs=[row, row, row, vec, ANY_SPEC],
        out_specs=[row, pl.BlockSpec((tm, gw), lambda i: (i, dp_col // gw)), vec],
        out_shape=[jax.ShapeDtypeStruct((t, gw), F32), jax.ShapeDtypeStruct(dp.shape, dp.dtype),
                   jax.ShapeDtypeStruct((1, hd), F32)],
        input_output_aliases={4: 1}, compiler_params=_cp(("arbitrary",)), name=name,
    )(dy, o, z, norm_w, dp)


IN_NAMES = ("q_a", "kv_a", "qkv_b", "ab", "z", "q_c", "gates")
CAT_NAMES = ("gates", "q_a", "qkv_b", "z", "q_c", "kv_a", "ab")
AB_PAD = 256


def _in_widths(d):
    gw = GDN_HEADS * GDN_HEAD_DIM
    return dict(q_a=SWA_Q_HEADS * SWA_HEAD_DIM, kv_a=2 * SWA_KV_HEADS * SWA_HEAD_DIM, qkv_b=3 * gw, ab=2 * GDN_HEADS,
                z=gw, q_c=XA_HEADS * XA_HEAD_DIM, gates=3 * d)


def _ranges(names, widths):
    out, start = {}, 0
    for k in names:
        out[k] = (start, widths[k])
        start += widths[k]
    return out, start


def _cat_ranges(d):
    widths = dict(_in_widths(d), ab=AB_PAD)
    return _ranges(CAT_NAMES, widths)


def _to_cat(shards, *, name="to_cat", tm=256):
    ns, d, n = shards.shape
    src, _ = _ranges(IN_NAMES, _in_widths(d))
    _, cat_w = _cat_ranges(d)
    pieces = []
    for k in CAT_NAMES:
        lo, hi = src[k][0], src[k][0] + src[k][1]
        for s in range(ns):
            a, b = max(lo, s * n), min(hi, (s + 1) * n)
            if a < b:
                pieces.append((s, a - s * n, b - s * n))
    tm = _blk(d, tm)

    def body(s_ref, o_ref):
        cols = [s_ref[s, :, a:b] for s, a, b in pieces]
        cols.append(jnp.zeros((tm, AB_PAD - src["ab"][1]), o_ref.dtype))
        o_ref[...] = jnp.concatenate(cols, axis=1)

    return pl.pallas_call(
        body, grid=(d // tm,),
        in_specs=[pl.BlockSpec((ns, tm, n), lambda i: (0, i, 0))],
        out_specs=pl.BlockSpec((tm, cat_w), lambda i: (i, 0)),
        out_shape=jax.ShapeDtypeStruct((d, cat_w), shards.dtype),
        compiler_params=_cp(("parallel",)), name=name,
    )(shards)


def _from_cat(w_cat, *, name="from_cat", tm=256):
    d, cat_w = w_cat.shape
    src, total = _ranges(IN_NAMES, _in_widths(d))
    cat, _ = _cat_ranges(d)
    n = total // N_SHARDS
    pieces = []
    for s in range(N_SHARDS):
        pieces.append([])
        for k in IN_NAMES:
            a, b = max(s * n, src[k][0]), min((s + 1) * n, src[k][0] + src[k][1])
            if a < b:
                pieces[s].append((cat[k][0] + a - src[k][0], cat[k][0] + b - src[k][0]))
    tm = _blk(d, tm)

    def body(c_ref, o_ref):
        for s in range(N_SHARDS):
            o_ref[s] = jnp.concatenate([c_ref[:, a:b] for a, b in pieces[s]], axis=1)

    return pl.pallas_call(
        body, grid=(d // tm,),
        in_specs=[pl.BlockSpec((tm, cat_w), lambda i: (i, 0))],
        out_specs=pl.BlockSpec((N_SHARDS, tm, n), lambda i: (0, i, 0)),
        out_shape=jax.ShapeDtypeStruct((N_SHARDS, d, n), w_cat.dtype),
        compiler_params=_cp(("parallel",)), name=name,
    )(w_cat)


def _pad_cols(a, width):
    return jnp.pad(a, ((0, 0), (0, width - a.shape[1])))


def _relu2_epilogue(acc):
    r = jnp.maximum(acc, 0.0)
    return acc, r * r


def _add_epilogue(acc, res):
    return (acc + res,)


def _drelu2_epilogue(acc, u):
    return (acc * (2.0 * jnp.maximum(u.astype(F32), 0.0)),)


def _local_step(x, mem, tgt, wts, small, comm=None):
    t, d = x.shape
    nh = GDN_HEADS
    cat, cat_w = _cat_ranges(d)
    alog_pad = _pad_cols(small["a_log"], LANES)
    dt_pad = _pad_cols(small["dt_bias"], LANES)
    kvw = cat["kv_a"][1]
    assert cat["ab"][0] == cat["kv_a"][0] + kvw
    ab_blk = kvw // LANES

    if comm is None:
        n = _rms_fwd(x, small["g_mix"], name="rms_mix")
        w_cat = wts["w_cat"]
    else:
        n, landed = _rms_fwd(x, small["g_mix"], name="rms_mix", exchange=comm.gather_exchange(["w_in"]))
        w_cat = _to_cat(_exchange_call(_gather_pass_on(landed), name="ag_w_in_pass")[0])
    assert w_cat.shape == (d, cat_w)
    q_a = _mm(n, w_cat, b_window=cat["q_a"], out_dtypes=(BF16,), name="in_q_a")
    kv_a, ab = _mm(n, w_cat, b_window=(cat["kv_a"][0], kvw + AB_PAD), out_dtypes=(BF16, F32), name="in_kv_ab")
    qkvb = _mm(n, w_cat, b_window=cat["qkv_b"], tn=512, name="in_qkv_b")
    z = _mm(n, w_cat, b_window=cat["z"], name="in_z")
    q_c = _mm(n, w_cat, b_window=cat["q_c"], out_dtypes=(BF16,), name="in_q_c")
    if comm is None:
        gates = _mm(n, w_cat, b_window=cat["gates"], name="in_gates")
        y_a, lse = _swa_fwd(q_a, kv_a, small["sinks"], name="swa_fwd")
    else:
        gates, landed_mlp = _mm(n, w_cat, b_window=cat["gates"], name="in_gates",
                                exchange=comm.gather_exchange(comm.MLP[1:]))
        y_a, lse, landed = _swa_fwd(q_a, kv_a, small["sinks"], name="swa_fwd", exchange=comm.gather_exchange(comm.MLP[:1]))
        landed_mlp = landed + landed_mlp
    xc, qkvn, aux = _gdn_pre_fwd(qkvb, small["conv_w"], ab, alog_pad, dt_pad, ab_blk=ab_blk, name="gdn_pre_fwd")
    aux_t = aux[:, :16].reshape(t // GDN_CHUNK, GDN_CHUNK, 16).transpose(0, 2, 1)
    if comm is None:
        gdn_u, gdn_w, gdn_qd, gdn_kd, gdn_mm, gdn_tinv = _gdn_local_fwd(qkvn, aux, aux_t, name="gdn_local_fwd")
    else:
        gdn_u, gdn_w, gdn_qd, gdn_kd, gdn_mm, gdn_tinv, landed_mid = _gdn_local_fwd(
            qkvn, aux, aux_t, name="gdn_local_fwd", exchange=comm.gather_exchange(comm.mid))
        wts = dict(wts, **comm.gathered(comm.mid, landed_mid, "mid"))
    o_b, gdn_vn, s_all = _gdn_seq_fwd(gdn_u, gdn_w, gdn_qd, gdn_kd, gdn_mm, aux, name="gdn_seq_fwd")
    y_b = _gdn_post_fwd(o_b, z, small["gdn_norm_w"], name="gdn_post_fwd")
    nmem = _rms_fwd(mem, small["g_mem"], name="rms_mem")
    mkv = _mm(nmem, wts["w_mem_kv"], out_dtypes=(BF16,), name="mem_kv")
    y_c = _xa_fwd(q_c, mkv, name="xa_fwd")
    ys = (y_a, y_b, y_c)
    w_ups = (wts["w_swa_up"], wts["w_gdn_up"], wts["w_xa_up"])
    merged = _merge_fwd(ys, w_ups, gates, name="merge_fwd")
    if comm is None:
        h1 = _mm(merged, wts["w_out"], extras=(x,), epilogue=_add_epilogue, name="out_proj")
    else:
        h1, whole = _mm(merged, wts["w_out"], extras=(x,), epilogue=_add_epilogue, name="out_proj",
                        exchange=_gather_pass_on(landed_mlp))
        wts = dict(wts, **comm.as_weights(comm.MLP, whole))
    n2 = _rms_fwd(h1, small["g_mlp"], name="rms_mlp")
    u, act = _mm(n2, wts["w_mlp_in"], b_sharded=True, out_dtypes=(BF16, BF16), epilogue=_relu2_epilogue, name="mlp_in")
    h2 = _mm(act, wts["w_mlp_out"], extras=(h1,), epilogue=_add_epilogue, name="mlp_out")
    dh2, dh2_b, dg_final, loss = _final_loss(h2, small["g_final"], tgt, name="final_loss")

    grads = {"g_final": dg_final}
    du = _mm(dh2_b, wts["w_mlp_out"], tb=True, out_dtypes=(BF16,), extras=(u,), epilogue=_drelu2_epilogue, name="d_mlp_act")
    grads["w_mlp_out"] = _mm(act, dh2_b, ta=True, out_dtypes=(BF16,), name="dw_mlp_out")
    grads["w_mlp_in"] = _mm(n2, du, ta=True, out_sharded=True, out_dtypes=(BF16,), name="dw_mlp_in")
    if comm is None:
        dn2 = _mm(du, wts["w_mlp_in"], tb=True, b_sharded=True, name="d_mlp_in")
    else:
        g_mlp = [comm.shard_major(k, grads.pop(k)) for k in comm.MLP]
        dn2, sib_mlp = _mm(du, wts["w_mlp_in"], tb=True, b_sharded=True, name="d_mlp_in", exchange=_sibling_halves(g_mlp))
        s1_mlp = comm.pair_sums(g_mlp, "mlp", sib_mlp)
    dh1, dh1_b, grads["g_mlp"] = _rms_bwd(dn2, h1, small["g_mlp"], dh2, name="rms_mlp_bwd")
    dmerged = _mm(dh1_b, wts["w_out"], tb=True, name="d_out_proj")
    grads["w_out"] = _mm(merged, dh1_b, ta=True, out_dtypes=(BF16,), name="dw_out")
    *dus, dp = _merge_bwd(ys, w_ups, gates, dmerged, cat_w, name="merge_bwd")
    dys = []
    for y, du_i, w_up, key in zip(ys, dus, w_ups, ("w_swa_up", "w_gdn_up", "w_xa_up")):
        dys.append(_mm(du_i, w_up, tb=True, b_sharded=True, out_dtypes=(BF16,), name="d_" + key))
        grads[key] = _mm(y, du_i, ta=True, out_sharded=True, out_dtypes=(BF16,), name="dw_" + key[2:])
    dp, dkv_a, grads["sinks"] = _swa_bwd(q_a, kv_a, small["sinks"], y_a, lse, dys[0], dp, cat["q_a"][0], name="swa_bwd")
    do_b, dp, grads["gdn_norm_w"] = _gdn_post_bwd(dys[1], o_b, z, small["gdn_norm_w"], dp, cat["z"][0],
                                                  name="gdn_post_bwd")
    dvn, dqd, dkd, dw_, dlast = _gdn_seq_bwd(do_b, gdn_w, gdn_qd, gdn_kd, gdn_mm, gdn_vn, s_all, aux, name="gdn_seq_bwd")
    dqkvn, daux = _gdn_local_bwd(qkvn, aux, aux_t, gdn_tinv, gdn_u, gdn_w, gdn_vn, do_b, dvn, dqd, dkd, dw_, dlast,
                                 name="gdn_local_bwd")
    dxc, dp, dalog, ddt = _gdn_pre_bwd1(xc, dqkvn, daux, ab, alog_pad, dt_pad, dkv_a, dp, cat["kv_a"][0], ab_blk=ab_blk,
                                        name="gdn_pre_bwd1")
    grads["a_log"], grads["dt_bias"] = dalog[:, :nh], ddt[:, :nh]
    dp, grads["conv_w"] = _gdn_pre_bwd2(dxc, qkvb, small["conv_w"], dp, cat["qkv_b"][0], name="gdn_pre_bwd2")
    dp, dmkv = _xa_bwd(q_c, mkv, dys[2], dp, cat["q_c"][0], name="xa_bwd")
    grads["w_mem_kv"] = _mm(nmem, dmkv, ta=True, out_dtypes=(BF16,), name="dw_mem_kv")
    dnmem = _mm(dmkv, wts["w_mem_kv"], tb=True, name="d_mem_kv")
    _, _, grads["g_mem"] = _rms_bwd(dnmem, mem, small["g_mem"], jnp.zeros_like(mem), name="rms_mem_bwd")
    if comm is None:
        grads["w_cat"] = _mm(n, dp, ta=True, out_dtypes=(BF16,), name="dw_in")
        dn = _mm(dp, w_cat, tb=True, name="d_in_proj")
    else:
        s1_mid = comm.pair_sums([comm.shard_major(k, grads.pop(k)) for k in comm.mid], "mid")
        dw_cat, rcv_mlp = _mm(n, dp, ta=True, out_dtypes=(BF16,), name="dw_in", exchange=_chip_exchange(s1_mlp))
        s1_in = comm.pair_sums([_from_cat(dw_cat)], "in")
        dn, rcv_rest = _mm(dp, w_cat, tb=True, name="d_in_proj", exchange=_chip_exchange(s1_in + s1_mid))
        halves = comm.chip_sums(s1_in + s1_mid + s1_mlp, rcv_rest + rcv_mlp)
        reduced = _exchange_call(_join_halves(halves), name="rs_join_halves")
        grads.update(zip(["w_in"] + comm.mid + list(comm.MLP), reduced))
    dx, _, grads["g_mix"] = _rms_bwd(dn, x, small["g_mix"], dh1, name="rms_mix_bwd")
    return loss, dx, grads


HBM_SPEC = pl.BlockSpec(memory_space=pltpu.HBM)
VMEM_SPEC = pl.BlockSpec(memory_space=pltpu.VMEM)
N_CHIPS = N_SHARDS
N_DEV = 8
DMA_CHUNK_BYTES = 1 << 20


def _place():
    return lax.axis_index("x"), lax.axis_index("y"), lax.axis_index("c")


def _other_chips(x, y):
    return [(1 - x, y), (x, 1 - y), (1 - x, 1 - y)]


def _n_chunks(rows, row_bytes):
    n = 1
    while rows % (2 * n) == 0 and (rows // (2 * n)) % 16 == 0 and (rows // n) * row_bytes > DMA_CHUNK_BYTES:
        n *= 2
    return n


def _sem_scratch(n_remote, n_local):
    return [pltpu.SemaphoreType.DMA((max(n_remote, 1),)), pltpu.SemaphoreType.DMA((max(n_remote, 1),)),
            pltpu.SemaphoreType.DMA((max(n_local, 1),))]


def _gather_over_ici(shards):
    plan = _half_chunks(shards, 0)

    def copies_of(in_refs, out_refs, place):
        x, y, c = place
        remote, local = [], []
        for i, r0, nr in plan:
            rh = shards[i].shape[0] // 2
            mine = pl.ds(c * rh + r0, nr)
            for chip in _other_chips(x, y):
                remote.append((in_refs[i].at[mine], out_refs[i].at[2 * x + y, mine], (*chip, c)))
            for half in range(2):
                rows = pl.ds(half * rh + r0, nr)
                local.append((in_refs[i].at[rows], out_refs[i].at[2 * x + y, rows]))
        return remote, local

    shapes = tuple(jax.ShapeDtypeStruct((N_CHIPS, *s.shape), s.dtype) for s in shards)
    return Exchange(tuple(shards), shapes, 3 * len(plan), 2 * len(plan), copies_of)


def _gather_pass_on(arrived):
    plan = _half_chunks([jax.ShapeDtypeStruct(a.shape[1:], a.dtype) for a in arrived], 0)

    def copies_of(in_refs, out_refs, place):
        x, y, c = place
        remote = []
        for i, r0, nr in plan:
            mine = pl.ds(c * (arrived[i].shape[1] // 2) + r0, nr)
            for chip in _other_chips(x, y):
                rows = out_refs[i].at[2 * chip[0] + chip[1], mine]
                remote.append((rows, rows, (x, y, 1 - c)))
        return remote, []

    shapes = tuple(jax.ShapeDtypeStruct(a.shape, a.dtype) for a in arrived)
    return Exchange(tuple(arrived), shapes, 3 * len(plan), 0, copies_of, tuple((i, i) for i in range(len(arrived))))


def _exchange_call(ex, *, name):
    n_in, n_out = len(ex.ins), len(ex.out_shapes)

    def body(*refs):
        cps = _exchange_copies(ex, refs[:n_in], refs[n_in:n_in + n_out], refs[n_in + n_out:])
        for cp in cps:
            cp.start()
        for cp in cps:
            cp.wait()

    return pl.pallas_call(
        body, out_shape=list(ex.out_shapes), in_specs=[HBM_SPEC] * n_in, out_specs=[HBM_SPEC] * n_out,
        scratch_shapes=_sem_scratch(ex.n_remote, ex.n_local), input_output_aliases=dict(ex.aliases), name=name,
    )(*ex.ins)


def _half_chunks(arrs, row_axis):
    plan = []
    for i, a in enumerate(arrs):
        rh = a.shape[row_axis] // 2
        row_bytes = a.dtype.itemsize * math.prod(a.shape) // a.shape[row_axis]
        nch = _n_chunks(rh, row_bytes)
        plan += [(i, q * (rh // nch), rh // nch) for q in range(nch)]
    return plan


def _sibling_halves(gs):
    plan = _half_chunks(gs, 1)

    def copies_of(in_refs, out_refs, place):
        x, y, c = place
        out = []
        for i, r0, nr in plan:
            rh = gs[i].shape[1] // 2
            out.append((in_refs[i].at[:, pl.ds((1 - c) * rh + r0, nr), :], out_refs[i].at[:, pl.ds(r0, nr), :],
                        (x, y, 1 - c)))
        return out, []

    shapes = tuple(jax.ShapeDtypeStruct((g.shape[0], g.shape[1] // 2, g.shape[2]), g.dtype) for g in gs)
    return Exchange(tuple(gs), shapes, len(plan), 0, copies_of)


def _chip_exchange(s1s):
    plan = _half_chunks([jax.ShapeDtypeStruct((2 * s.shape[1], s.shape[2]), s.dtype) for s in s1s], 0)

    def copies_of(in_refs, out_refs, place):
        x, y, c = place
        out = []
        for i, r0, nr in plan:
            for j, chip in enumerate(_other_chips(x, y)):
                out.append((in_refs[i].at[2 * chip[0] + chip[1], pl.ds(r0, nr), :], out_refs[i].at[j, pl.ds(r0, nr), :],
                            (*chip, c)))
        return out, []

    shapes = tuple(jax.ShapeDtypeStruct((3, *s.shape[1:]), s.dtype) for s in s1s)
    return Exchange(tuple(s1s), shapes, 3 * len(plan), 0, copies_of)


def _join_halves(gs):
    plan = _half_chunks(gs, 0)

    def copies_of(in_refs, out_refs, place):
        x, y, c = place
        out = []
        for i, r0, nr in plan:
            rows = out_refs[i].at[pl.ds(c * (gs[i].shape[0] // 2) + r0, nr), :]
            out.append((rows, rows, (x, y, 1 - c)))
        return out, []

    shapes = tuple(jax.ShapeDtypeStruct(g.shape, g.dtype) for g in gs)
    aliases = tuple((i, i) for i in range(len(gs)))
    return Exchange(tuple(gs), shapes, len(plan), 0, copies_of, aliases)


def _row_block(rows, cols):
    tb = rows
    while tb % 32 == 0 and tb * cols * 4 > (2 << 20):
        tb //= 2
    return tb


def _pair_sum(g, sib, core, *, name):
    ns, r, c = g.shape
    rh = r // 2
    tb = _row_block(rh, c)
    nb = rh // tb

    def body(core_ref, g_ref, s_ref, o_ref):
        o_ref[...] = (g_ref[...].astype(F32) + s_ref[...].astype(F32)).astype(o_ref.dtype)

    mine = pl.BlockSpec((None, tb, c), lambda s, i, core_ref: (s, core_ref[0] * nb + i, 0))
    half = pl.BlockSpec((None, tb, c), lambda s, i, core_ref: (s, i, 0))
    return pl.pallas_call(
        body, grid_spec=pltpu.PrefetchScalarGridSpec(num_scalar_prefetch=1, grid=(ns, nb), in_specs=[mine, half],
                                                     out_specs=half),
        out_shape=jax.ShapeDtypeStruct((ns, rh, c), BF16), compiler_params=_cp(("parallel", "parallel")), name=name,
    )(core, g, sib)


def _chip_sum(s1, rcv, where, *, name):
    _, rh, c = s1.shape
    tb = _row_block(rh, c)
    nb = rh // tb

    def body(where_ref, own_ref, r0_ref, r1_ref, r2_ref, o_ref):
        acc = own_ref[...].astype(F32)
        for r in (r0_ref, r1_ref, r2_ref):
            acc = acc + r[...].astype(F32)
        o_ref[...] = acc

    own = pl.BlockSpec((None, tb, c), lambda i, w: (w[1], i, 0))
    got = [pl.BlockSpec((None, tb, c), functools.partial(lambda i, w, j: (j, i, 0), j=j)) for j in range(3)]
    return pl.pallas_call(
        body, grid_spec=pltpu.PrefetchScalarGridSpec(
            num_scalar_prefetch=1, grid=(nb,), in_specs=[own] + got,
            out_specs=pl.BlockSpec((tb, c), lambda i, w: (w[0] * nb + i, 0))),
        out_shape=jax.ShapeDtypeStruct((2 * rh, c), F32), compiler_params=_cp(("parallel",)), name=name,
    )(where, s1, rcv, rcv, rcv)


def _all_gather_small(blk, *, name):
    r = blk.shape[0]

    def body(b_ref, out_ref, send_sems, recv_sems):
        x, y, c = _place()
        me = 4 * x + 2 * y + c
        out_ref[me] = b_ref[...]
        sends = []
        for k in range(1, N_DEV):
            peer = (x ^ (k >> 2), y ^ ((k >> 1) & 1), c ^ (k & 1))
            sends.append(pltpu.make_async_remote_copy(src_ref=b_ref, dst_ref=out_ref.at[me], send_sem=send_sems.at[k - 1],
                                                      recv_sem=recv_sems.at[k - 1], device_id=peer, device_id_type=MESH))
        for cp in sends:
            cp.start()
        for k in range(1, N_DEV):
            rows = out_ref.at[me ^ k]
            pltpu.make_async_remote_copy(src_ref=rows, dst_ref=rows, send_sem=send_sems.at[k - 1],
                                         recv_sem=recv_sems.at[k - 1], device_id=(x, y, c), device_id_type=MESH).wait_recv()
        for cp in sends:
            cp.wait_send()

    return pl.pallas_call(
        body, out_shape=jax.ShapeDtypeStruct((N_DEV, r, LANES), blk.dtype), in_specs=[VMEM_SPEC], out_specs=VMEM_SPEC,
        scratch_shapes=[pltpu.SemaphoreType.DMA((N_DEV - 1,)), pltpu.SemaphoreType.DMA((N_DEV - 1,))],
        name=name,
    )(blk)


def _sum_rows(parts, out_dtype, *, name, tb=1024):
    rows = parts[0].shape[0]
    tb = _blk(rows, tb)

    def body(*refs):
        acc = refs[0][...].astype(F32)
        for r in refs[1:-1]:
            acc = acc + r[...].astype(F32)
        refs[-1][...] = acc.astype(refs[-1].dtype)

    spec = pl.BlockSpec((tb, LANES), lambda i: (i, 0))
    return pl.pallas_call(
        body, grid=(rows // tb,), in_specs=[spec] * len(parts), out_specs=spec,
        out_shape=jax.ShapeDtypeStruct((rows, LANES), out_dtype), compiler_params=_cp(("parallel",)), name=name,
    )(*parts)


BIG = (
    ("w_in", 1), ("w_mem_kv", 0), ("w_swa_up", 1), ("w_gdn_up", 1), ("w_xa_up", 1), ("w_out", 0), ("w_mlp_in", 1),
    ("w_mlp_out", 0))


class _Comm:
    MLP = ("w_mlp_in", "w_mlp_out")

    def __init__(self, late_shards, core, where):
        self.axis = dict(BIG)
        self.late_shards = late_shards
        self.mid = [k for k in late_shards if k not in self.MLP and k != "w_in"]
        self.core, self.where = core, where

    def gather_exchange(self, names):
        return _gather_over_ici([self.late_shards[k] for k in names])

    def as_weights(self, names, whole):
        return {k: (g.reshape(-1, g.shape[2]) if self.axis[k] == 0 else g) for k, g in zip(names, whole)}

    def gathered(self, names, landed, tag):
        return self.as_weights(names, _exchange_call(_gather_pass_on(landed), name=f"ag_{tag}_pass"))

    def shard_major(self, k, grad):
        return grad.reshape(N_CHIPS, -1, grad.shape[-1]) if self.axis[k] == 0 else grad

    def pair_sums(self, gs, tag, sibs=None):
        if sibs is None:
            sibs = _exchange_call(_sibling_halves(gs), name=f"rs_sibling_{tag}")
        return [_pair_sum(g, s, self.core, name=f"rs_pair_sum_{tag}{i}") for i, (g, s) in enumerate(zip(gs, sibs))]

    def chip_sums(self, s1s, rcvs):
        return [_chip_sum(s1, rcv, self.where, name=f"rs_chip_sum_{i}") for i, (s1, rcv) in enumerate(zip(s1s, rcvs))]
SMALL = ("g_mix", "sinks", "a_log", "dt_bias", "gdn_norm_w", "g_mem", "g_mlp", "g_final")


def _rows128(a, rows):
    flat = a.reshape(-1)
    return jnp.pad(flat, (0, rows * LANES - flat.shape[0])).reshape(rows, LANES)


def kernel(x, mem, g_mix, w_in, sinks, conv_w, a_log, dt_bias, gdn_norm_w, g_mem, w_mem_kv, w_swa_up, w_gdn_up, w_xa_up, w_out, g_mlp, w_mlp_in, w_mlp_out, g_final, loss_target, m_g_mix, m_w_in, m_sinks, m_conv_w, m_a_log, m_dt_bias, m_gdn_norm_w, m_g_mem, m_w_mem_kv, m_w_swa_up, m_w_gdn_up, m_w_xa_up, m_w_out, m_g_mlp, m_w_mlp_in, m_w_mlp_out, m_g_final, v_g_mix, v_w_in, v_sinks, v_conv_w, v_a_log, v_dt_bias, v_gdn_norm_w, v_g_mem, v_w_mem_kv, v_w_swa_up, v_w_gdn_up, v_w_xa_up, v_w_out, v_g_mlp, v_w_mlp_in, v_w_mlp_out, v_g_final):
    given = dict(locals())
    xi, yi, ci = _place()
    chip = 2 * xi + yi
    core = jnp.reshape(ci, (1,)).astype(jnp.int32)
    where = jnp.stack([ci, chip]).astype(jnp.int32)

    comm = _Comm({k: given[k][0].astype(BF16) for k, _ in BIG}, core, where)
    wts = {}
    conv_shard = conv_w[0]
    conv_rows = -(-conv_shard.size // (8 * LANES)) * 8
    conv_all = _all_gather_small(_rows128(conv_shard, conv_rows), name="ag_conv")
    conv_full = jnp.concatenate(
        [conv_all[2 * s].reshape(-1)[:conv_shard.size].reshape(conv_shard.shape) for s in range(N_CHIPS)], axis=1)

    small = {k: given[k].reshape(1, -1) for k in SMALL}
    small["conv_w"] = conv_full
    loss_row, dx, grads = _local_step(x[0], mem[0], loss_target[0], wts, small, comm)
    big_grads = {k: grads[k] for k, _ in BIG}

    layout = [("loss", loss_row[:, :1])] + [(k, grads[k]) for k in SMALL] + [("conv_w", grads["conv_w"])]
    rows = [-(-a.size // LANES) for _, a in layout]
    blk_rows = -(-sum(rows) // 8) * 8
    blk = jnp.concatenate([_rows128(a.astype(F32), n) for (_, a), n in zip(layout, rows)]
                          + [jnp.zeros((blk_rows - sum(rows), LANES), F32)], axis=0)
    gathered = _all_gather_small(blk, name="ag_small_grads")
    reduced = _sum_rows([gathered[i] for i in range(N_DEV)], F32, name="small_grad_sum")
    small_grads, start = {}, 0
    for (k, a), n in zip(layout, rows):
        small_grads[k] = reduced[start:start + n].reshape(-1)[:a.size].reshape(a.shape)
        start += n
    loss = small_grads["loss"].reshape(())
    cw = conv_shard.shape[1]
    conv_grad = lax.dynamic_slice_in_dim(small_grads["conv_w"], chip * cw, cw, axis=1)

    names = ["g_mix", "w_in", "sinks", "conv_w", "a_log", "dt_bias", "gdn_norm_w", "g_mem", "w_mem_kv", "w_swa_up",
             "w_gdn_up", "w_xa_up", "w_out", "g_mlp", "w_mlp_in", "w_mlp_out", "g_final"]
    out_g, out_d, out_m, out_v = [], [], [], []
    for k in names:
        w, m, v = given[k], given["m_" + k], given["v_" + k]
        if k in big_grads:
            g2 = big_grads[k]
        elif k == "conv_w":
            g2 = conv_grad
        else:
            g2 = small_grads[k]
        as_given = (lambda a: a.reshape(1, -1)) if w.ndim == 1 else (lambda a: a)
        if w.shape[-1] % LANES and w.shape[-1] > LANES:
            tr = lambda a: jnp.swapaxes(a, -1, -2)
            g_out, delta, new_m, new_v = (tr(a) for a in _adamw(tr(w), tr(g2), tr(m), tr(v), name="adamw_" + k))
        else:
            g_out, delta, new_m, new_v = _adamw(as_given(w), g2, as_given(m), as_given(v), name="adamw_" + k)
        out_g.append(g_out.reshape(w.shape))
        out_d.append(delta.reshape(w.shape))
        out_m.append(new_m.reshape(w.shape))
        out_v.append(new_v.reshape(w.shape))
    return (loss, dx[None], *out_g, *out_d, *out_m, *out_v)
```

```python
import functools
import math
from typing import Callable, NamedTuple

import jax
import jax.numpy as jnp
from jax import lax
from jax.experimental import pallas as pl
from jax.experimental.pallas import tpu as pltpu

F32 = jnp.float32
BF16 = jnp.bfloat16
HI = lax.Precision.HIGHEST
MESH = pl.DeviceIdType.MESH

SWA_Q_HEADS = 16
SWA_KV_HEADS = 2
SWA_HEAD_DIM = 64
SWA_WINDOW = 128
SWA_SCALE = SWA_HEAD_DIM ** -0.5
assert math.frexp(SWA_SCALE)[0] == 0.5
GDN_HEADS = 4
GDN_HEAD_DIM = 128
GDN_CONV = 4
GDN_CHUNK = 64
XA_HEADS = 4
XA_HEAD_DIM = 128
RMS_EPS = 1e-6
L2_EPS = 1e-6
ADAM_LR = 0.001
ADAM_B1 = 0.9
ADAM_B2 = 0.999
ADAM_EPS = 1e-08
ADAM_WD = 0.01
ADAM_STEP = 10

LANES = 128
N_SHARDS = 4
VMEM_LIMIT = 56 * 1024 * 1024

NT = (((1,), (1,)), ((), ()))
TN = (((0,), (0,)), ((), ()))
NN = (((1,), (0,)), ((), ()))


def _cp(sem=None):
    return pltpu.CompilerParams(dimension_semantics=sem, vmem_limit_bytes=VMEM_LIMIT)


def _blk(dim, pref):
    if dim <= pref:
        return dim
    b = (pref // LANES) * LANES
    while dim % b:
        b -= LANES
    assert b > 0, (dim, pref)
    return b


def _dot(a, b, dims=NN, precision=None):
    return lax.dot_general(a, b, dims, precision=precision, preferred_element_type=F32)


def _sigmoid(x):
    return 0.5 * jnp.tanh(0.5 * x) + 0.5


MM_TK_BYTES = 4096


def _mm(a, b, *, name, ta=False, tb=False, out_dtypes=(F32,), epilogue=None, extras=(), tm=1024, tn=1024, tk=None,
        b_sharded=False, out_sharded=False, b_window=None, exchange=None):
    (kdim, m) = a.shape if ta else a.shape[::-1]
    col0 = 0
    n_lim = k_lim = None
    if b_sharded:
        ns, rows_w, per = b.shape
        if tb:
            kb, n, k_lim = ns * per, rows_w, per
        else:
            kb, n, n_lim = rows_w, ns * per, per
    else:
        (kb, n) = b.shape[::-1] if tb else b.shape
        if b_window is not None:
            assert not tb
            col0, n = b_window
    assert kdim == kb, (a.shape, b.shape, ta, tb)
    if out_sharded:
        assert n % N_SHARDS == 0
        n_lim = n // N_SHARDS if n_lim is None else n_lim
        assert n_lim == n // N_SHARDS
    if tk is None:
        tk = MM_TK_BYTES // max(a.dtype.itemsize, b.dtype.itemsize)
    tm, tn, tk = _blk(m, tm), _blk(n_lim or n, tn), _blk(k_lim or kdim, tk)
    assert col0 % tn == 0, (col0, tn)
    nk = kdim // tk
    a_spec = pl.BlockSpec((tk, tm), lambda i, j, k: (k, i)) if ta else pl.BlockSpec((tm, tk), lambda i, j, k: (i, k))
    if b_sharded and tb:
        kpb = k_lim // tk
        b_spec = pl.BlockSpec((None, tn, tk), lambda i, j, k: (k // kpb, j, k % kpb))
    elif b_sharded:
        bpb = n_lim // tn
        b_spec = pl.BlockSpec((None, tk, tn), lambda i, j, k: (j // bpb, k, j % bpb))
    elif tb:
        b_spec = pl.BlockSpec((tn, tk), lambda i, j, k: (j, k))
    else:
        b_spec = pl.BlockSpec((tk, tn), lambda i, j, k: (k, j + col0 // tn))
    x_spec = pl.BlockSpec((tm, tn), lambda i, j, k: (i, j))
    if out_sharded:
        opb = n_lim // tn
        o_spec = pl.BlockSpec((None, tm, tn), lambda i, j, k: (j // opb, i, j % opb))
        out_shape = (N_SHARDS, m, n_lim)
    else:
        o_spec, out_shape = x_spec, (m, n)
    dims = ((((0 if ta else 1),), ((1 if tb else 0),)), ((), ()))
    n_extra, n_out = len(extras), len(out_dtypes)

    host = _ExchangeHost(exchange)
    grid = (m // tm, n // tn, nk)

    def body(*refs):
        a_ref, b_ref = refs[:2]
        extra_refs = refs[2:2 + n_extra]
        out_refs = refs[2 + n_extra + host.n_in:2 + n_extra + host.n_in + n_out]
        host.start(refs, 2 + n_extra, 2 + n_extra + host.n_in + n_out, grid)
        part = _dot(a_ref[...].astype(BF16), b_ref[...].astype(BF16), dims)

        def finish(acc):
            vals = epilogue(acc, *[r[...] for r in extra_refs]) if epilogue is not None else (acc,) * n_out
            assert len(vals) == n_out
            for r, v in zip(out_refs, vals):
                r[...] = v.astype(r.dtype)

        if nk == 1:
            finish(part)
        else:
            acc_ref = refs[2 + n_extra + host.n_in + n_out + host.n_out]
            k = pl.program_id(2)

            @pl.when(k == 0)
            def _():
                acc_ref[...] = part

            @pl.when((k > 0) & (k < nk - 1))
            def _():
                acc_ref[...] += part

            @pl.when(k == nk - 1)
            def _():
                finish(acc_ref[...] + part)

        host.wait(refs, 2 + n_extra, 2 + n_extra + host.n_in + n_out, grid)

    outs = pl.pallas_call(
        body,
        grid=grid,
        in_specs=[a_spec, b_spec] + [x_spec] * n_extra + host.in_specs,
        out_specs=[o_spec] * n_out + host.out_specs,
        out_shape=[jax.ShapeDtypeStruct(out_shape, d) for d in out_dtypes] + host.out_shapes,
        scratch_shapes=([pltpu.VMEM((tm, tn), F32)] if nk > 1 else []) + host.scratch,
        input_output_aliases=host.aliases(2 + n_extra, n_out),
        compiler_params=_cp(host.semantics(("parallel", "parallel", "arbitrary"))),
        name=name,
    )(a, b, *extras, *host.ins)
    mine, landed = outs[:n_out], list(outs[n_out:])
    mine = mine[0] if n_out == 1 else mine
    return (mine, landed) if exchange is not None else mine


class Exchange(NamedTuple):
    ins: tuple
    out_shapes: tuple
    n_remote: int
    n_local: int
    copies_of: Callable
    aliases: tuple = ()


def _exchange_copies(ex, in_refs, out_refs, sem_refs):
    send_sems, recv_sems, local_sems = sem_refs
    remote, local = ex.copies_of(in_refs, out_refs, _place())
    assert len(remote) == ex.n_remote and len(local) == ex.n_local, (len(remote), len(local))
    cps = [pltpu.make_async_remote_copy(src_ref=src, dst_ref=dst, send_sem=send_sems.at[k], recv_sem=recv_sems.at[k],
                                        device_id=to, device_id_type=MESH) for k, (src, dst, to) in enumerate(remote)]
    cps += [pltpu.make_async_copy(src, dst, local_sems.at[k]) for k, (src, dst) in enumerate(local)]
    return cps


class _ExchangeHost:
    def __init__(self, ex):
        self.ex = ex
        self.ins = list(ex.ins) if ex else []
        self.out_shapes = list(ex.out_shapes) if ex else []
        self.n_in, self.n_out = len(self.ins), len(self.out_shapes)
        self.in_specs = [HBM_SPEC] * self.n_in
        self.out_specs = [HBM_SPEC] * self.n_out
        self.scratch = _sem_scratch(ex.n_remote, ex.n_local) if ex else []

    def semantics(self, sem):
        return tuple("arbitrary" for _ in sem) if self.ex else sem

    def aliases(self, in_at, out_at):
        return {in_at + i: out_at + o for i, o in self.ex.aliases} if self.ex else {}

    def _refs(self, refs, in_at, out_at):
        return refs[in_at:in_at + self.n_in], refs[out_at:out_at + self.n_out], refs[len(refs) - 3:]

    def _when(self, grid, last):
        cond = None
        for d, size in enumerate(grid):
            c = pl.program_id(d) == (size - 1 if last else 0)
            cond = c if cond is None else cond & c
        return cond

    def start(self, refs, in_at, out_at, grid):
        if self.ex:
            @pl.when(self._when(grid, False))
            def _():
                for cp in _exchange_copies(self.ex, *self._refs(refs, in_at, out_at)):
                    cp.start()

    def wait(self, refs, in_at, out_at, grid):
        if self.ex:
            @pl.when(self._when(grid, True))
            def _():
                for cp in _exchange_copies(self.ex, *self._refs(refs, in_at, out_at)):
                    cp.wait()


def _rms_fwd(x, g, *, name, tm=512, exchange=None):
    t, d = x.shape
    tm = _blk(t, tm)
    host = _ExchangeHost(exchange)
    grid = (t // tm,)

    def body(*refs):
        x_ref, g_ref, n_ref = refs[0], refs[1], refs[2 + host.n_in]
        host.start(refs, 2, 3 + host.n_in, grid)
        xv = x_ref[...]
        r = lax.rsqrt(jnp.mean(xv * xv, axis=-1, keepdims=True) + RMS_EPS)
        n_ref[...] = (xv * r * g_ref[...]).astype(n_ref.dtype)
        host.wait(refs, 2, 3 + host.n_in, grid)

    outs = pl.pallas_call(
        body, grid=grid,
        in_specs=[pl.BlockSpec((tm, d), lambda i: (i, 0)), pl.BlockSpec((1, d), lambda i: (0, 0))] + host.in_specs,
        out_specs=[pl.BlockSpec((tm, d), lambda i: (i, 0))] + host.out_specs,
        out_shape=[jax.ShapeDtypeStruct((t, d), BF16)] + host.out_shapes,
        scratch_shapes=host.scratch, input_output_aliases=host.aliases(2, 1),
        compiler_params=_cp(host.semantics(("parallel",))), name=name,
    )(x, g, *host.ins)
    return (outs[0], list(outs[1:])) if exchange is not None else outs[0]


def _rms_bwd(dn, x, g, dres, *, name, tm=512):
    t, d = x.shape
    tm = _blk(t, tm)

    def body(dn_ref, x_ref, g_ref, dres_ref, dx_ref, dxb_ref, dg_ref):
        i = pl.program_id(0)
        xv = x_ref[...]
        r = lax.rsqrt(jnp.mean(xv * xv, axis=-1, keepdims=True) + RMS_EPS)
        xh = xv * r
        dnv = dn_ref[...].astype(F32)
        dxh = dnv * g_ref[...]
        dx = dres_ref[...] + r * (dxh - xh * jnp.mean(dxh * xh, axis=-1, keepdims=True))
        dx_ref[...] = dx
        dxb_ref[...] = dx.astype(dxb_ref.dtype)
        part = jnp.sum(dnv * xh, axis=0, keepdims=True)

        @pl.when(i == 0)
        def _():
            dg_ref[...] = part

        @pl.when(i > 0)
        def _():
            dg_ref[...] += part

    row = pl.BlockSpec((tm, d), lambda i: (i, 0))
    vec = pl.BlockSpec((1, d), lambda i: (0, 0))
    return pl.pallas_call(
        body, grid=(t // tm,),
        in_specs=[row, row, vec, row], out_specs=[row, row, vec],
        out_shape=[jax.ShapeDtypeStruct((t, d), F32), jax.ShapeDtypeStruct((t, d), BF16),
                   jax.ShapeDtypeStruct((1, d), F32)],
        compiler_params=_cp(("arbitrary",)), name=name,
    )(dn, x, g, dres)


def _final_loss(h, g, tgt, *, name, tm=512):
    t, d = h.shape
    tm = _blk(t, tm)

    def body(h_ref, g_ref, t_ref, dh_ref, dhb_ref, dg_ref, loss_ref):
        i = pl.program_id(0)
        hv = h_ref[...]
        r = lax.rsqrt(jnp.mean(hv * hv, axis=-1, keepdims=True) + RMS_EPS)
        xh = hv * r
        e = xh * g_ref[...] - t_ref[...]
        dy = e * (1.0 / d)
        dxh = dy * g_ref[...]
        dh = r * (dxh - xh * jnp.mean(dxh * xh, axis=-1, keepdims=True))
        dh_ref[...] = dh
        dhb_ref[...] = dh.astype(dhb_ref.dtype)
        dg_part = jnp.sum(dy * xh, axis=0, keepdims=True)
        row_loss = jnp.sum(e * e, axis=-1, keepdims=True) * (0.5 / d)
        loss_part = jnp.sum(row_loss, axis=0, keepdims=True)

        @pl.when(i == 0)
        def _():
            dg_ref[...] = dg_part
            loss_ref[...] = jnp.broadcast_to(loss_part, loss_ref.shape)

        @pl.when(i > 0)
        def _():
            dg_ref[...] += dg_part
            loss_ref[...] += jnp.broadcast_to(loss_part, loss_ref.shape)

    row = pl.BlockSpec((tm, d), lambda i: (i, 0))
    vec = pl.BlockSpec((1, d), lambda i: (0, 0))
    return pl.pallas_call(
        body, grid=(t // tm,),
        in_specs=[row, vec, row], out_specs=[row, row, vec, pl.BlockSpec((1, LANES), lambda i: (0, 0))],
        out_shape=[jax.ShapeDtypeStruct((t, d), F32), jax.ShapeDtypeStruct((t, d), BF16),
                   jax.ShapeDtypeStruct((1, d), F32), jax.ShapeDtypeStruct((1, LANES), F32)],
        compiler_params=_cp(("arbitrary",)), name=name,
    )(h, g, tgt)


SWA_SUB = 64


def _swa_mask(n, rows, row0):
    w = SWA_WINDOW
    qi = (lax.broadcasted_iota(jnp.int32, (rows, 2 * w), 0) + row0) & (w - 1)
    kj = lax.broadcasted_iota(jnp.int32, (rows, 2 * w), 1)
    return (kj > qi) & (kj <= qi + w) & ((n > 0) | (kj >= w))


def _stack_heads(ref, heads, width):
    return jnp.concatenate([ref[:, h * width:(h + 1) * width] for h in heads], axis=0)


def _stack_scalars(ref, heads, rows):
    return jnp.concatenate([jnp.broadcast_to(ref[0:1, h:h + 1], (rows, 1)) for h in heads], axis=0)


def _swa_fwd(q, kv, sinks, *, name, exchange=None):
    t = q.shape[0]
    w, hd, hq, hkv = SWA_WINDOW, SWA_HEAD_DIM, SWA_Q_HEADS, SWA_KV_HEADS
    grp = hq // hkv
    kvw = hkv * hd
    nb = t // w
    host = _ExchangeHost(exchange)
    assert not (exchange and exchange.aliases)

    def body(*refs):
        q_ref, kvp_ref, kvc_ref, s_ref = refs[:4]
        o_ref, lse_ref = refs[4 + host.n_in:6 + host.n_in]
        host.start(refs, 4, 6 + host.n_in, (nb,))
        n = pl.program_id(0)
        mask = _swa_mask(n, grp * w, 0)
        kvcat = jnp.concatenate([kvp_ref[...], kvc_ref[...]], axis=0)
        kvs = range(hkv)
        heads = [range(hk * grp, (hk + 1) * grp) for hk in kvs]
        sks = [_stack_scalars(s_ref, hs, w) for hs in heads]
        ss = [jnp.where(mask, _dot(_stack_heads(q_ref, heads[hk], hd) * SWA_SCALE, kvcat[:, hk * hd:(hk + 1) * hd], NT),
                        -jnp.inf) for hk in kvs]
        ms = [jnp.maximum(jnp.max(s, axis=-1, keepdims=True), sk) for s, sk in zip(ss, sks)]
        ps = [jnp.exp(s - m) for s, m in zip(ss, ms)]
        dens = [jnp.sum(p, axis=-1, keepdims=True) + jnp.exp(sk - m) for p, sk, m in zip(ps, sks, ms)]
        os_ = [_dot((p * (1.0 / den)).astype(BF16), kvcat[:, kvw + hk * hd:kvw + (hk + 1) * hd])
               for hk, p, den in zip(kvs, ps, dens)]
        outs, lses = [], []
        for o, m, den in zip(os_, ms, dens):
            lse = m + jnp.log(den)
            outs += [o[j * w:(j + 1) * w] for j in range(grp)]
            lses += [lse[j * w:(j + 1) * w] for j in range(grp)]
        o_ref[...] = jnp.concatenate(outs, axis=1).astype(o_ref.dtype)
        lse_ref[...] = jnp.concatenate(lses, axis=1)
        host.wait(refs, 4, 6 + host.n_in, (nb,))

    outs = pl.pallas_call(
        body, grid=(nb,),
        in_specs=[pl.BlockSpec((w, hq * hd), lambda i: (i, 0)),
                  pl.BlockSpec((w, 2 * kvw), lambda i: (jnp.maximum(i - 1, 0), 0)),
                  pl.BlockSpec((w, 2 * kvw), lambda i: (i, 0)),
                  pl.BlockSpec((1, hq), lambda i: (0, 0))] + host.in_specs,
        out_specs=[pl.BlockSpec((w, hq * hd), lambda i: (i, 0)), pl.BlockSpec((w, hq), lambda i: (i, 0))] + host.out_specs,
        out_shape=[jax.ShapeDtypeStruct((t, hq * hd), BF16), jax.ShapeDtypeStruct((t, hq), F32)] + host.out_shapes,
        scratch_shapes=host.scratch,
        compiler_params=_cp(host.semantics(("parallel",))), name=name,
    )(q, kv, kv, sinks, *host.ins)
    return (outs[0], outs[1], list(outs[2:])) if exchange is not None else outs


ANY_SPEC = pl.BlockSpec(memory_space=pl.ANY)


def _swa_bwd(q, kv, sinks, o, lse, do, dp, dp_col, *, name):
    t = q.shape[0]
    w, hd, hq, hkv = SWA_WINDOW, SWA_HEAD_DIM, SWA_Q_HEADS, SWA_KV_HEADS
    grp = hq // hkv
    kvw = hkv * hd
    nb = t // w
    assert dp_col % (hq * hd) == 0
    dq_blk = dp_col // (hq * hd)

    def body(q_ref, kvp_ref, kvc_ref, s_ref, o_ref, lse_ref, do_ref, _, dq_ref, dkv_ref, ds_ref, carry_ref, s_scr, dp_scr,
             p_scr, ds_scr):
        n = pl.program_id(0)

        @pl.when(n == 0)
        def _():
            ds_ref[...] = jnp.zeros_like(ds_ref)
            carry_ref[...] = jnp.zeros_like(carry_ref)

        @pl.when(n < nb)
        def _():
            kvcat = jnp.concatenate([kvp_ref[...], kvc_ref[...]], axis=0)
            dqs, dsk, dks, dvs = [], [], [], []
            for hk in range(hkv):
                heads = range(hk * grp, (hk + 1) * grp)
                qs = _stack_heads(q_ref, heads, hd)
                dos = _stack_heads(do_ref, heads, hd)
                os_ = _stack_heads(o_ref, heads, hd)
                lse = _stack_heads(lse_ref, heads, 1)
                kh = kvcat[:, hk * hd:(hk + 1) * hd]
                vh = kvcat[:, kvw + hk * hd:kvw + (hk + 1) * hd]
                delta = jnp.sum(dos.astype(F32) * os_.astype(F32), axis=-1, keepdims=True)
                s_scr[...] = _dot(qs * SWA_SCALE, kh, NT)
                dp_scr[...] = _dot(dos, vh, NT)
                for r0 in range(0, grp * w, SWA_SUB):
                    rows = slice(r0, r0 + SWA_SUB)
                    p = jnp.exp(jnp.where(_swa_mask(n, SWA_SUB, r0 % w), s_scr[rows, :], -jnp.inf) - lse[rows])
                    p_scr[rows, :] = p.astype(p_scr.dtype)
                    ds_scr[rows, :] = (p * (dp_scr[rows, :] - delta[rows]) * SWA_SCALE).astype(ds_scr.dtype)
                ds = ds_scr[...]
                dq = _dot(ds, kh)
                dqs += [dq[j * w:(j + 1) * w] for j in range(grp)]
                dks.append(_dot(ds, qs, TN))
                dvs.append(_dot(p_scr[...], dos, TN))
                dsink = -jnp.exp(_stack_scalars(s_ref, heads, w) - lse) * delta
                dsk += [jnp.sum(dsink[j * w:(j + 1) * w], axis=0, keepdims=True) for j in range(grp)]
            dq_ref[...] = jnp.concatenate(dqs, axis=1).astype(dq_ref.dtype)
            ds_ref[...] += jnp.concatenate(dsk, axis=1)
            dkv_cat = jnp.concatenate(dks + dvs, axis=1)
            dkv_ref[...] = (carry_ref[...] + dkv_cat[:w]).astype(dkv_ref.dtype)
            carry_ref[...] = dkv_cat[w:]

        @pl.when(n == nb)
        def _():
            dkv_ref[...] = carry_ref[...].astype(dkv_ref.dtype)

    cur = lambda i: (jnp.minimum(i, nb - 1), 0)
    prev = lambda i: (jnp.clip(i - 1, 0, nb - 1), 0)
    return pl.pallas_call(
        body, grid=(nb + 1,),
        in_specs=[pl.BlockSpec((w, hq * hd), cur), pl.BlockSpec((w, 2 * kvw), prev), pl.BlockSpec((w, 2 * kvw), cur),
                  pl.BlockSpec((1, hq), lambda i: (0, 0)), pl.BlockSpec((w, hq * hd), cur),
                  pl.BlockSpec((w, hq), cur), pl.BlockSpec((w, hq * hd), cur), ANY_SPEC],
        out_specs=[pl.BlockSpec((w, hq * hd), lambda i: (jnp.minimum(i, nb - 1), dq_blk)),
                   pl.BlockSpec((w, 2 * kvw), prev), pl.BlockSpec((1, hq), lambda i: (0, 0))],
        out_shape=[jax.ShapeDtypeStruct(dp.shape, dp.dtype), jax.ShapeDtypeStruct((t, 2 * kvw), BF16),
                   jax.ShapeDtypeStruct((1, hq), F32)],
        scratch_shapes=[pltpu.VMEM((w, 2 * kvw), F32)] + [pltpu.VMEM((grp * w, 2 * w), dt) for dt in (F32, F32, BF16, BF16)],
        input_output_aliases={7: 0},
        compiler_params=_cp(("arbitrary",)), name=name,
    )(q, kv, kv, sinks, o, lse, do, dp)


def _xa_fwd(q, mkv, *, name, tq=512):
    t, xw = q.shape
    nm = mkv.shape[0]
    hd, nh = XA_HEAD_DIM, XA_HEADS
    tq = _blk(t, tq)

    def body(q_ref, mkv_ref, o_ref):
        outs = []
        for h in range(nh):
            qh = q_ref[:, h * hd:(h + 1) * hd]
            kh = mkv_ref[:, h * hd:(h + 1) * hd]
            vh = mkv_ref[:, xw + h * hd:xw + (h + 1) * hd]
            s = _dot(qh, kh, NT) * (hd ** -0.5)
            p = jnp.exp(s - jnp.max(s, axis=-1, keepdims=True))
            p = p * (1.0 / jnp.sum(p, axis=-1, keepdims=True))
            outs.append(_dot(p.astype(BF16), vh))
        o_ref[...] = jnp.concatenate(outs, axis=1).astype(o_ref.dtype)

    return pl.pallas_call(
        body, grid=(t // tq,),
        in_specs=[pl.BlockSpec((tq, xw), lambda i: (i, 0)), pl.BlockSpec((nm, 2 * xw), lambda i: (0, 0))],
        out_specs=pl.BlockSpec((tq, xw), lambda i: (i, 0)),
        out_shape=jax.ShapeDtypeStruct((t, xw), BF16),
        compiler_params=_cp(("parallel",)), name=name,
    )(q, mkv)


def _xa_bwd(q, mkv, do, dp, dp_col, *, name, tq=512):
    t, xw = q.shape
    nm = mkv.shape[0]
    hd, nh = XA_HEAD_DIM, XA_HEADS
    tq = _blk(t, tq)
    assert dp_col % xw == 0

    def body(q_ref, mkv_ref, do_ref, _, dq_ref, dmkv_ref):
        i = pl.program_id(0)
        dqs, dks, dvs = [], [], []
        for h in range(nh):
            qh = q_ref[:, h * hd:(h + 1) * hd]
            kh = mkv_ref[:, h * hd:(h + 1) * hd]
            vh = mkv_ref[:, xw + h * hd:xw + (h + 1) * hd]
            doh = do_ref[:, h * hd:(h + 1) * hd]
            s = _dot(qh, kh, NT) * (hd ** -0.5)
            p = jnp.exp(s - jnp.max(s, axis=-1, keepdims=True))
            p = p * (1.0 / jnp.sum(p, axis=-1, keepdims=True))
            dp = _dot(doh, vh, NT)
            ds = (p * (dp - jnp.sum(p * dp, axis=-1, keepdims=True)) * (hd ** -0.5)).astype(BF16)
            dqs.append(_dot(ds, kh))
            dks.append(_dot(ds, qh, TN))
            dvs.append(_dot(p.astype(BF16), doh, TN))
        dq_ref[...] = jnp.concatenate(dqs, axis=1).astype(dq_ref.dtype)
        part = jnp.concatenate(dks + dvs, axis=1)

        @pl.when(i == 0)
        def _():
            dmkv_ref[...] = part

        @pl.when(i > 0)
        def _():
            dmkv_ref[...] += part

    row = pl.BlockSpec((tq, xw), lambda i: (i, 0))
    full = pl.BlockSpec((nm, 2 * xw), lambda i: (0, 0))
    return pl.pallas_call(
        body, grid=(t // tq,),
        in_specs=[row, full, row, ANY_SPEC],
        out_specs=[pl.BlockSpec((tq, xw), lambda i: (i, dp_col // xw)), full],
        out_shape=[jax.ShapeDtypeStruct(dp.shape, dp.dtype), jax.ShapeDtypeStruct((nm, 2 * xw), F32)],
        input_output_aliases={3: 0}, compiler_params=_cp(("arbitrary",)), name=name,
    )(q, mkv, do, dp)


def _merge_specs(ys, ws, tm):
    y_specs = [pl.BlockSpec((tm, y.shape[1]), lambda i: (i, 0)) for y in ys]
    w_specs = [pl.BlockSpec(w.shape, lambda i: (0, 0, 0)) for w in ws]
    return y_specs, w_specs


def _merge_tiles(ws, tn):
    ns, _, per = ws[0].shape
    tn = _blk(per, tn)
    return tn, [(s, c, s * per + c) for s in range(ns) for c in range(0, per, tn)]


def _merge_fwd(ys, ws, gates, *, name, tm=256, tn=512):
    t, d = ys[0].shape[0], ws[0].shape[0] * ws[0].shape[2]
    tm = _blk(t, tm)
    tn, tiles = _merge_tiles(ws, tn)
    y_specs, w_specs = _merge_specs(ys, ws, tm)

    def body(ya, yb, yc, wa, wb, wc, g_ref, o_ref):
        for s, c, col in tiles:
            acc = None
            for b, (y, w) in enumerate(((ya, wa), (yb, wb), (yc, wc))):
                term = _sigmoid(g_ref[:, b * d + col:b * d + col + tn]) * _dot(y[...], w[s, :, c:c + tn])
                acc = term if acc is None else acc + term
            o_ref[:, col:col + tn] = acc.astype(o_ref.dtype)

    return pl.pallas_call(
        body, grid=(t // tm,),
        in_specs=y_specs + w_specs + [pl.BlockSpec((tm, 3 * d), lambda i: (i, 0))],
        out_specs=pl.BlockSpec((tm, d), lambda i: (i, 0)),
        out_shape=jax.ShapeDtypeStruct((t, d), BF16),
        compiler_params=_cp(("parallel",)), name=name,
    )(*ys, *ws, gates)


def _merge_bwd(ys, ws, gates, dmerged, dp_width, *, name, tm=256, tn=512):
    t, d = ys[0].shape[0], ws[0].shape[0] * ws[0].shape[2]
    tm = _blk(t, tm)
    tn, tiles = _merge_tiles(ws, tn)
    y_specs, w_specs = _merge_specs(ys, ws, tm)
    row = pl.BlockSpec((tm, d), lambda i: (i, 0))
    wide = pl.BlockSpec((tm, 3 * d), lambda i: (i, 0))

    def body(ya, yb, yc, wa, wb, wc, g_ref, dm_ref, dua, dub, duc, dp_ref):
        for s, c, col in tiles:
            dm = dm_ref[:, col:col + tn]
            for b, (y, w, du) in enumerate(((ya, wa, dua), (yb, wb, dub), (yc, wc, duc))):
                sg = _sigmoid(g_ref[:, b * d + col:b * d + col + tn])
                u = _dot(y[...], w[s, :, c:c + tn])
                du[:, col:col + tn] = (dm * sg).astype(du.dtype)
                dp_ref[:, b * d + col:b * d + col + tn] = (dm * u * sg * (1.0 - sg)).astype(dp_ref.dtype)

    return pl.pallas_call(
        body, grid=(t // tm,),
        in_specs=y_specs + w_specs + [wide, row],
        out_specs=[row] * 3 + [wide],
        out_shape=[jax.ShapeDtypeStruct((t, d), BF16)] * 3 + [jax.ShapeDtypeStruct((t, dp_width), BF16)],
        compiler_params=_cp(("parallel",)), name=name,
    )(*ys, *ws, gates, dmerged)


def _adamw(w, g, m, v, *, name, tm=256):
    lead = w.ndim - 2
    assert all(s == 1 for s in w.shape[:lead]) and m.shape == w.shape and v.shape == w.shape
    r, c = w.shape[lead:]
    assert g.shape == (r, c)
    tm = _blk(r, tm) if r % 8 == 0 else r
    tc = c if tm * c * 4 <= (4 << 20) else _blk(c, 256)
    ncb = c // tc
    bc1 = 1.0 - ADAM_B1 ** ADAM_STEP
    bc2 = 1.0 - ADAM_B2 ** ADAM_STEP

    def body(w_ref, g_ref, m_ref, v_ref, go_ref, d_ref, nm_ref, nv_ref):
        gv = g_ref[...]
        go_ref[...] = gv
        nm = ADAM_B1 * m_ref[...] + (1.0 - ADAM_B1) * gv
        nv = ADAM_B2 * v_ref[...] + (1.0 - ADAM_B2) * (gv * gv)
        d_ref[...] = -ADAM_LR * ((nm / bc1) / (jnp.sqrt(nv / bc2) + ADAM_EPS) + ADAM_WD * w_ref[...])
        nm_ref[...] = nm
        nv_ref[...] = nv

    spec = pl.BlockSpec((None,) * lead + (tm, tc), lambda i: (0,) * lead + (i // ncb, i % ncb))
    g_spec = pl.BlockSpec((tm, tc), lambda i: (i // ncb, i % ncb))
    return pl.pallas_call(
        body, grid=(r // tm * ncb,), in_specs=[spec, g_spec, spec, spec], out_specs=[spec] * 4,
        out_shape=[jax.ShapeDtypeStruct(w.shape, F32)] * 4,
        compiler_params=_cp(("parallel",)), name=name,
    )(w, g, m, v)


HALO = 8


def _shift_down(cur, prev, j):
    if j == 0:
        return cur
    y = pltpu.roll(cur, j, 0)
    row = lax.broadcasted_iota(jnp.int32, (HALO, cur.shape[1]), 0)
    top = jnp.where(row < j, pltpu.roll(prev, j, 0), y[:HALO])
    return jnp.concatenate([top, y[HALO:]], axis=0)


def _shift_up(cur, nxt, j):
    if j == 0:
        return cur
    tm = cur.shape[0]
    y = pltpu.roll(cur, tm - j, 0)
    row = lax.broadcasted_iota(jnp.int32, (HALO, cur.shape[1]), 0)
    bot = jnp.where(row >= HALO - j, pltpu.roll(nxt, HALO - j, 0), y[tm - HALO:])
    return jnp.concatenate([y[:tm - HALO], bot], axis=0)


def _softplus(x):
    return jnp.maximum(x, 0.0) + jnp.log(1.0 + jnp.exp(-jnp.abs(x)))


def _gdn_pre_fwd(qkvb, conv_w, ab, alog_pad, dt_pad, *, name, ab_blk=0, tm=256):
    t, cw = qkvb.shape
    hd, nh, ck = GDN_HEAD_DIM, GDN_HEADS, GDN_CHUNK
    gw = nh * hd
    tm = _blk(t, tm)
    hb = tm // HALO

    def body(x_ref, xp_ref, w_ref, ab_ref, al_ref, dt_ref, xc_ref, qkvn_ref, aux_ref):
        i = pl.program_id(0)
        cur = x_ref[...]
        prev = jnp.where(i > 0, xp_ref[...], 0.0)
        xc = None
        for tap in range(GDN_CONV):
            term = w_ref[tap:tap + 1, :] * _shift_down(cur, prev, GDN_CONV - 1 - tap)
            xc = term if xc is None else xc + term
        xc_ref[...] = xc
        s = xc * _sigmoid(xc)
        for h in range(2 * nh):
            xh = s[:, h * hd:(h + 1) * hd]
            r = lax.rsqrt(jnp.sum(xh * xh, axis=-1, keepdims=True) + L2_EPS)
            scale = hd ** -0.5 if h < nh else 1.0
            qkvn_ref[:, h * hd:(h + 1) * hd] = xh * (r * scale)
        qkvn_ref[:, 2 * gw:] = s[:, 2 * gw:]
        abv = ab_ref[...]
        lane = lax.broadcasted_iota(jnp.int32, abv.shape, 1)
        g = jnp.where(lane < nh, -jnp.exp(al_ref[...]) * _softplus(abv + dt_ref[...]), 0.0)
        beta = jnp.where((lane >= nh) & (lane < 2 * nh), _sigmoid(abv), 0.0)
        ii = lax.broadcasted_iota(jnp.int32, (tm, tm), 0)
        jj = lax.broadcasted_iota(jnp.int32, (tm, tm), 1)
        tri = jnp.where((ii >= jj) & ((ii ^ jj) < ck), 1.0, 0.0)
        gcum = _dot(tri, g, precision=HI)
        aux_ref[...] = g + beta + pltpu.roll(gcum, 2 * nh, 1)

    row = lambda c: pl.BlockSpec((tm, c), lambda i: (i, 0))
    vec = lambda r, c: pl.BlockSpec((r, c), lambda i: (0, 0))
    return pl.pallas_call(
        body, grid=(t // tm,),
        in_specs=[row(cw), pl.BlockSpec((HALO, cw), lambda i: (jnp.maximum(i * hb - 1, 0), 0)), vec(GDN_CONV, cw),
                  pl.BlockSpec((tm, LANES), lambda i: (i, ab_blk)), vec(1, LANES), vec(1, LANES)],
        out_specs=[row(cw), row(cw), row(LANES)],
        out_shape=[jax.ShapeDtypeStruct((t, cw), F32), jax.ShapeDtypeStruct((t, cw), F32),
                   jax.ShapeDtypeStruct((t, LANES), F32)],
        compiler_params=_cp(("parallel",)), name=name,
    )(qkvb, qkvb, conv_w, ab, alog_pad, dt_pad)


GDN_STEP_CHUNKS = 4
GDN_ILP_CHUNKS = 4
GDN_ILP_CHUNKS_BWD = 4


def _bdot(a, b, dims=NN):
    return _dot(a.astype(BF16), b.astype(BF16), dims)


def _split_bf16(x):
    hi = x.astype(BF16)
    return hi, (x - hi.astype(F32)).astype(BF16)


def _dot3(a, b, dims=NN):
    ah, al = _split_bf16(a)
    bh, bl = _split_bf16(b)
    return _dot(ah, bh, dims) + (_dot(ah, bl, dims) + _dot(al, bh, dims))


def _dot3_many(lhs, rhs, dims=NN):
    sa = [_split_bf16(a) for a in lhs]
    sb = [_split_bf16(b) for b in rhs]
    hh = [_dot(a[0], b[0], dims) for a, b in zip(sa, sb)]
    hl = [_dot(a[0], b[1], dims) for a, b in zip(sa, sb)]
    lh = [_dot(a[1], b[0], dims) for a, b in zip(sa, sb)]
    return [x + (y + z) for x, y, z in zip(hh, hl, lh)]


def _gdn_local(chains, with_inverse):
    ck = GDN_CHUNK
    ii = lax.broadcasted_iota(jnp.int32, (ck, ck), 0)
    jj = lax.broadcasted_iota(jnp.int32, (ck, ck), 1)
    lower, strict = ii >= jj, ii > jj
    dmat = [jnp.exp(jnp.where(lower, gc - gc_row, -jnp.inf)) for _, _, _, gc, gc_row in chains]
    kk = [_bdot(k, k, NT) for _, k, _, _, _ in chains]
    qk = [_bdot(q, k, NT) for q, k, _, _, _ in chains]
    tinv = [None] * len(chains)
    if with_inverse:
        lmat = [jnp.where(strict, c[2] * kk_i * d_i, 0.0) for c, kk_i, d_i in zip(chains, kk, dmat)]
        eye = jnp.where(ii == jj, 1.0, 0.0)
        tinv = [eye - l_i for l_i in lmat]
        pw = lmat
        for _ in range(int(math.log2(ck)) - 1):
            pw = _dot3_many(pw, pw)
            tinv = [t_i + d_i for t_i, d_i in zip(tinv, _dot3_many(tinv, pw))]
    out = []
    for (q, k, b, gc, gc_row), dmat_i, kk_i, qk_i, tinv_i in zip(chains, dmat, kk, qk, tinv):
        gl = gc[ck - 1:ck, :]
        out.append(dict(lower=lower, strict=strict, dmat=dmat_i, kk=kk_i, tinv=tinv_i, gam=jnp.exp(gc), qk=qk_i,
                        mm=qk_i * dmat_i, kdec=jnp.exp(gl - gc)))
    return out


def _gdn_head_cols(h):
    return slice(h * GDN_HEAD_DIM, (h + 1) * GDN_HEAD_DIM)


def _gdn_chunk_inputs(x_ref, aux_ref, auxt_ref, g, h):
    nh, ck = GDN_HEADS, GDN_CHUNK
    gw = nh * GDN_HEAD_DIM
    rows = slice(g * ck, (g + 1) * ck)
    cols = _gdn_head_cols(h)
    q = x_ref[rows, cols]
    k = x_ref[rows, gw + cols.start:gw + cols.stop]
    v = x_ref[rows, 2 * gw + cols.start:2 * gw + cols.stop]
    b = aux_ref[rows, nh + h:nh + h + 1]
    gc = aux_ref[rows, 2 * nh + h:2 * nh + h + 1]
    gc_row = auxt_ref[g, 2 * nh + h:2 * nh + h + 1, :]
    return q, k, v, b, gc, gc_row


def _gdn_specs(t, widths, *, reverse=False, step_chunks=None):
    rows = (step_chunks or GDN_STEP_CHUNKS) * GDN_CHUNK
    nsteps = t // rows
    idx = (lambda i: (nsteps - 1 - i, 0)) if reverse else (lambda i: (i, 0))
    return [pl.BlockSpec((rows, w), idx) for w in widths]


def _gdn_local_fwd(qkvn, aux, aux_t, *, name, exchange=None):
    t = qkvn.shape[0]
    hd, nh, ck, gs = GDN_HEAD_DIM, GDN_HEADS, GDN_CHUNK, GDN_STEP_CHUNKS
    gw = nh * hd
    host = _ExchangeHost(exchange)
    assert not (exchange and exchange.aliases)
    grid = (t // (gs * ck),)

    def body(*refs):
        x_ref, aux_ref, auxt_ref = refs[:3]
        u_ref, w_ref, qd_ref, kd_ref, mm_ref, tinv_ref = refs[3 + host.n_in:9 + host.n_in]
        host.start(refs, 3, 9 + host.n_in, grid)
        for g0 in range(0, gs, GDN_ILP_CHUNKS):
            where = [(g, h) for g in range(g0, g0 + GDN_ILP_CHUNKS) for h in range(nh)]
            ins = [_gdn_chunk_inputs(x_ref, aux_ref, auxt_ref, g, h) for g, h in where]
            lcs = _gdn_local([(q, k, b, gc, gc_row) for q, k, _, b, gc, gc_row in ins], True)
            tinvs = [lc["tinv"] for lc in lcs]
            us = _dot3_many(tinvs, [b * v for _, _, v, b, _, _ in ins])
            ws = _dot3_many(tinvs, [(b * lc["gam"]) * k for (_, k, _, b, _, _), lc in zip(ins, lcs)])
            for i, ((g, h), (q, k, _, _, _, _), lc) in enumerate(zip(where, ins, lcs)):
                rows, cols = slice(g * ck, (g + 1) * ck), _gdn_head_cols(h)
                u_ref[rows, cols] = us[i]
                w_ref[rows, cols] = ws[i].astype(w_ref.dtype)
                qd_ref[rows, cols] = (lc["gam"] * q).astype(qd_ref.dtype)
                kd_ref[rows, cols] = (lc["kdec"] * k).astype(kd_ref.dtype)
            for g in range(g0, g0 + GDN_ILP_CHUNKS):
                rows = slice(g * ck, (g + 1) * ck)
                mine = [lc for (gg, _), lc in zip(where, lcs) if gg == g]
                mm_ref[rows, :] = jnp.concatenate([lc["mm"] for lc in mine], axis=1).astype(mm_ref.dtype)
                tinv_ref[rows, :] = jnp.concatenate([lc["tinv"] for lc in mine], axis=1)
        host.wait(refs, 3, 9 + host.n_in, grid)

    sq = nh * ck
    outs = pl.pallas_call(
        body, grid=grid,
        in_specs=_gdn_specs(t, (3 * gw, LANES)) + [pl.BlockSpec((gs, 16, ck), lambda i: (i, 0, 0))] + host.in_specs,
        out_specs=_gdn_specs(t, (gw, gw, gw, gw, sq, sq)) + host.out_specs,
        out_shape=[jax.ShapeDtypeStruct((t, gw), F32)] + [jax.ShapeDtypeStruct((t, gw), BF16)] * 3
        + [jax.ShapeDtypeStruct((t, sq), BF16), jax.ShapeDtypeStruct((t, sq), F32)] + host.out_shapes,
        scratch_shapes=host.scratch,
        compiler_params=_cp(host.semantics(("parallel",))), name=name,
    )(qkvn, aux, aux_t, *host.ins)
    return (*outs[:6], list(outs[6:])) if exchange is not None else outs


def _gdn_seq_fwd(u, w, qd, kd, mm, aux, *, name):
    t = u.shape[0]
    hd, nh, ck, gs = GDN_HEAD_DIM, GDN_HEADS, GDN_CHUNK, GDN_STEP_CHUNKS
    gw = nh * hd
    sq = nh * ck

    def body(u_ref, w_ref, qd_ref, kd_ref, mm_ref, aux_ref, o_ref, vn_ref, sall_ref, s_ref):
        @pl.when(pl.program_id(0) == 0)
        def _():
            s_ref[...] = jnp.zeros_like(s_ref)

        heads = range(nh)
        hcols = [_gdn_head_cols(h) for h in heads]
        sts = [s_ref[h] for h in heads]
        for g in range(gs):
            rows = slice(g * ck, (g + 1) * ck)
            last = (g + 1) * ck - 1
            for h in heads:
                sall_ref[g, h] = sts[h]
            stbs = [st.astype(BF16) for st in sts]
            w_s = [_dot(w_ref[rows, c], stb) for c, stb in zip(hcols, stbs)]
            q_s = [_dot(qd_ref[rows, c], stb) for c, stb in zip(hcols, stbs)]
            vnbs = [(u_ref[rows, c] - ws).astype(BF16) for c, ws in zip(hcols, w_s)]
            m_v = [_dot(mm_ref[rows, h * ck:(h + 1) * ck], vnbs[h]) for h in heads]
            k_v = [_dot(kd_ref[rows, c], vnb, TN) for c, vnb in zip(hcols, vnbs)]
            for h, c in zip(heads, hcols):
                vn_ref[rows, c] = vnbs[h]
                o_ref[rows, c] = q_s[h] + m_v[h]
            gam_c = [jnp.exp(aux_ref[last:last + 1, 2 * nh + h:2 * nh + h + 1]) for h in heads]
            sts = [gam_c[h] * sts[h] + k_v[h] for h in heads]
        for h in heads:
            s_ref[h] = sts[h]

    return pl.pallas_call(
        body, grid=(t // (gs * ck),),
        in_specs=_gdn_specs(t, (gw, gw, gw, gw, sq, LANES)),
        out_specs=_gdn_specs(t, (gw, gw)) + [pl.BlockSpec((gs, nh, hd, hd), lambda i: (i, 0, 0, 0))],
        out_shape=[jax.ShapeDtypeStruct((t, gw), F32), jax.ShapeDtypeStruct((t, gw), BF16),
                   jax.ShapeDtypeStruct((t // ck, nh, hd, hd), F32)],
        scratch_shapes=[pltpu.VMEM((nh, hd, hd), F32)],
        compiler_params=_cp(("arbitrary",)), name=name,
    )(u, w, qd, kd, mm, aux)


def _gdn_seq_bwd(do, w, qd, kd, mm, vn, s_all, aux, *, name):
    t = do.shape[0]
    hd, nh, ck, gs = GDN_HEAD_DIM, GDN_HEADS, GDN_CHUNK, GDN_STEP_CHUNKS
    gw = nh * hd
    sq = nh * ck
    nsteps = t // (gs * ck)

    def body(do_ref, w_ref, qd_ref, kd_ref, mm_ref, vn_ref, sall_ref, aux_ref, dvn_ref, dqd_ref, dkd_ref, dw_ref,
             dlast_ref, ds_ref):
        @pl.when(pl.program_id(0) == 0)
        def _():
            ds_ref[...] = jnp.zeros_like(ds_ref)

        lane = lax.broadcasted_iota(jnp.int32, (ck, LANES), 1)
        rowi = lax.broadcasted_iota(jnp.int32, (ck, LANES), 0)
        heads = range(nh)
        hcols = [_gdn_head_cols(h) for h in heads]
        dsns = [ds_ref[h] for h in heads]
        for g in reversed(range(gs)):
            rows = slice(g * ck, (g + 1) * ck)
            last = (g + 1) * ck - 1
            sts = [sall_ref[g, h] for h in heads]
            stbs = [st.astype(BF16) for st in sts]
            dsbs = [dsn.astype(BF16) for dsn in dsns]
            dobs = [do_ref[rows, c].astype(BF16) for c in hcols]
            dvns = [_dot(mm_ref[rows, h * ck:(h + 1) * ck], dobs[h], TN) + _dot(kd_ref[rows, hcols[h]], dsbs[h])
                    for h in heads]
            dqds = [_dot(dob, stb, NT) for dob, stb in zip(dobs, stbs)]
            dkds = [_dot(vn_ref[rows, c], dsb, NT) for c, dsb in zip(hcols, dsbs)]
            q_o = [_dot(qd_ref[rows, c], dob, TN) for c, dob in zip(hcols, dobs)]
            dvbs = [dvn.astype(BF16) for dvn in dvns]
            dws = [_dot(dvb, stb, NT) for dvb, stb in zip(dvbs, stbs)]
            w_v = [_dot(w_ref[rows, c], dvb, TN) for c, dvb in zip(hcols, dvbs)]
            gam_c = [jnp.exp(aux_ref[last:last + 1, 2 * nh + h:2 * nh + h + 1]) for h in heads]
            dlast = jnp.zeros((ck, LANES), F32)
            for h, c in zip(heads, hcols):
                dvn_ref[rows, c] = dvns[h]
                dqd_ref[rows, c] = dqds[h]
                dkd_ref[rows, c] = dkds[h]
                dw_ref[rows, c] = -dws[h]
                dgam_c = jnp.sum(jnp.sum(dsns[h] * sts[h], axis=1, keepdims=True), axis=0, keepdims=True)
                dlast = dlast + jnp.where((rowi == ck - 1) & (lane == h), gam_c[h] * dgam_c, 0.0)
            dlast_ref[rows, :] = dlast
            dsns = [q_o[h] + gam_c[h] * dsns[h] - w_v[h] for h in heads]
        for h in heads:
            ds_ref[h] = dsns[h]

    return pl.pallas_call(
        body, grid=(nsteps,),
        in_specs=_gdn_specs(t, (gw, gw, gw, gw, sq, gw), reverse=True)
        + [pl.BlockSpec((gs, nh, hd, hd), lambda i: (nsteps - 1 - i, 0, 0, 0))] + _gdn_specs(t, (LANES,), reverse=True),
        out_specs=_gdn_specs(t, (gw, gw, gw, gw, LANES), reverse=True),
        out_shape=[jax.ShapeDtypeStruct((t, gw), F32)] * 4 + [jax.ShapeDtypeStruct((t, LANES), F32)],
        scratch_shapes=[pltpu.VMEM((nh, hd, hd), F32)],
        compiler_params=_cp(("arbitrary",)), name=name,
    )(do, w, qd, kd, mm, vn, s_all, aux)


def _gdn_local_bwd(qkvn, aux, aux_t, tinv, u, w, vn, do, dvn, dqd, dkd, dw, dlast, *, name):
    t = qkvn.shape[0]
    hd, nh, ck, gs = GDN_HEAD_DIM, GDN_HEADS, GDN_CHUNK, GDN_STEP_CHUNKS
    gw = nh * hd
    sq = nh * ck

    def body(x_ref, aux_ref, auxt_ref, tinv_ref, u_ref, w_ref, vn_ref, do_ref, dvn_ref, dqd_ref, dkd_ref, dw_ref,
             dlast_ref, dx_ref, daux_ref):
        lane = lax.broadcasted_iota(jnp.int32, (ck, LANES), 1)
        ones = jnp.ones((ck, LANES), F32)
        ii = lax.broadcasted_iota(jnp.int32, (ck, ck), 0)
        jj = lax.broadcasted_iota(jnp.int32, (ck, ck), 1)
        suffix = jnp.where(jj >= ii, 1.0, 0.0)
        for g0 in range(0, gs, GDN_ILP_CHUNKS_BWD):
            where = [(g, h) for g in range(g0, g0 + GDN_ILP_CHUNKS_BWD) for h in range(nh)]
            at = [(slice(g * ck, (g + 1) * ck), _gdn_head_cols(h)) for g, h in where]
            ins = [_gdn_chunk_inputs(x_ref, aux_ref, auxt_ref, g, h) for g, h in where]
            lcs = _gdn_local([(q, k, b, gc, gc_row) for q, k, _, b, gc, gc_row in ins], False)
            tinvs = [tinv_ref[slice(g * ck, (g + 1) * ck), h * ck:(h + 1) * ck] for g, h in where]
            dms = [jnp.where(lc["lower"], _bdot(do_ref[r, c], vn_ref[r, c], NT), 0.0) for lc, (r, c) in zip(lcs, at)]
            drvs = _dot3_many(tinvs, [dvn_ref[r, c] for r, c in at], TN)
            drks = _dot3_many(tinvs, [dw_ref[r, c] for r, c in at], TN)
            das = [jnp.where(lc["strict"], -(_bdot(drv, u_ref[r, c], NT) + _bdot(drk, w_ref[r, c], NT)), 0.0)
                   for lc, (r, c), drv, drk in zip(lcs, at, drvs, drks)]
            f_mats = [da * (i[3] * lc["kk"]) * lc["dmat"] + dm * lc["qk"] * lc["dmat"]
                      for i, lc, da, dm in zip(ins, lcs, das, dms)]
            col_sums = _dot3_many(f_mats, [ones] * len(where), TN)
            dgc_all = {g: dlast_ref[slice(g * ck, (g + 1) * ck), :] for g in range(g0, g0 + GDN_ILP_CHUNKS_BWD)}
            db_all = {g: jnp.zeros((ck, LANES), F32) for g in range(g0, g0 + GDN_ILP_CHUNKS_BWD)}
            e_mats = [da * lc["dmat"] * i[3] for i, lc, da in zip(ins, lcs, das)]
            dmds = [dm * lc["dmat"] for lc, dm in zip(lcs, dms)]
            dq_mm = [_bdot(dmd, i[1]) for i, dmd in zip(ins, dmds)]
            dk_mm = [_bdot(e, i[1]) + _bdot(e, i[1], TN) + _bdot(dmd, i[0], TN) for i, e, dmd in zip(ins, e_mats, dmds)]
            for n, ((g, h), (q, k, v, b, _, _), lc, (rows, cols)) in enumerate(zip(where, ins, lcs, at)):
                dmat, kk, gam, kdec = (lc[key] for key in ("dmat", "kk", "gam", "kdec"))
                drv, drk, da = drvs[n], drks[n], das[n]
                dqd_h, dkd_h = dqd_ref[rows, cols], dkd_ref[rows, cols]
                rs_rk = jnp.sum(drk * k, axis=-1, keepdims=True)
                db = (jnp.sum(drv * v, axis=-1, keepdims=True) + gam * rs_rk
                      + jnp.sum(da * kk * dmat, axis=-1, keepdims=True))
                dx_ref[rows, cols] = dq_mm[n] + gam * dqd_h
                dx_ref[rows, gw + cols.start:gw + cols.stop] = (b * gam) * drk + dk_mm[n] + kdec * dkd_h
                dx_ref[rows, 2 * gw + cols.start:2 * gw + cols.stop] = b * drv
                e_vec = jnp.sum(dkd_h * (kdec * k), axis=-1, keepdims=True)
                dgc = (b * gam * rs_rk + gam * jnp.sum(dqd_h * q, axis=-1, keepdims=True)
                       + jnp.sum(f_mats[n], axis=-1, keepdims=True) - col_sums[n][:, 0:1] - e_vec)
                is_last = lax.broadcasted_iota(jnp.int32, (ck, 1), 0) == ck - 1
                dgc = dgc + jnp.where(is_last, jnp.sum(e_vec, axis=0, keepdims=True), 0.0)
                dgc_all[g] = dgc_all[g] + jnp.where(lane == h, dgc, 0.0)
                db_all[g] = db_all[g] + jnp.where(lane == nh + h, db, 0.0)
            for g in dgc_all:
                daux_ref[slice(g * ck, (g + 1) * ck), :] = _dot3(suffix, dgc_all[g]) + db_all[g]

    return pl.pallas_call(
        body, grid=(t // (gs * ck),),
        in_specs=_gdn_specs(t, (3 * gw, LANES)) + [pl.BlockSpec((gs, 16, ck), lambda i: (i, 0, 0))]
        + _gdn_specs(t, (sq, gw, gw, gw, gw, gw, gw, gw, gw, LANES)),
        out_specs=_gdn_specs(t, (3 * gw, LANES)),
        out_shape=[jax.ShapeDtypeStruct((t, 3 * gw), F32), jax.ShapeDtypeStruct((t, LANES), F32)],
        compiler_params=_cp(("parallel",)), name=name,
    )(qkvn, aux, aux_t, tinv, u, w, vn, do, dvn, dqd, dkd, dw, dlast)


def _gdn_pre_bwd1(xc, dqkvn, daux, ab, alog_pad, dt_pad, dkv, dp, dp_col, *, name, ab_blk=0, tm=256):
    t, cw = xc.shape
    hd, nh = GDN_HEAD_DIM, GDN_HEADS
    gw = nh * hd
    tm = _blk(t, tm)

    kvw = dkv.shape[1]
    seg = kvw + AB_PAD
    assert dp_col % seg == 0

    def body(xc_ref, dy_ref, daux_ref, ab_ref, al_ref, dt_ref, dkv_ref, _, dxc_ref, dab_ref, dal_ref, ddt_ref):
        i = pl.program_id(0)
        xc = xc_ref[...]
        sg = _sigmoid(xc)
        s = xc * sg
        dsilu = sg * (1.0 + xc * (1.0 - sg))
        for h in range(2 * nh):
            xh = s[:, h * hd:(h + 1) * hd]
            scale = hd ** -0.5 if h < nh else 1.0
            dyh = dy_ref[:, h * hd:(h + 1) * hd] * scale
            r = lax.rsqrt(jnp.sum(xh * xh, axis=-1, keepdims=True) + L2_EPS)
            dxh = r * dyh - xh * (r * r * r) * jnp.sum(dyh * xh, axis=-1, keepdims=True)
            dxc_ref[:, h * hd:(h + 1) * hd] = dxh * dsilu[:, h * hd:(h + 1) * hd]
        dxc_ref[:, 2 * gw:] = dy_ref[:, 2 * gw:] * dsilu[:, 2 * gw:]
        abv = ab_ref[...]
        dauxv = daux_ref[...]
        lane = lax.broadcasted_iota(jnp.int32, abv.shape, 1)
        is_a = lane < nh
        is_b = (lane >= nh) & (lane < 2 * nh)
        pre = abv + dt_ref[...]
        neg_ea = -jnp.exp(al_ref[...])
        d_a = jnp.where(is_a, dauxv * neg_ea * _sigmoid(pre), 0.0)
        beta = _sigmoid(abv)
        d_b = jnp.where(is_b, dauxv * beta * (1.0 - beta), 0.0)
        dab_ref[:, :kvw] = dkv_ref[...]
        dab_ref[:, kvw:kvw + LANES] = (d_a + d_b).astype(dab_ref.dtype)
        dab_ref[:, kvw + LANES:] = jnp.zeros((tm, AB_PAD - LANES), dab_ref.dtype)
        dal = jnp.sum(jnp.where(is_a, dauxv * neg_ea * _softplus(pre), 0.0), axis=0, keepdims=True)
        ddt = jnp.sum(d_a, axis=0, keepdims=True)

        @pl.when(i == 0)
        def _():
            dal_ref[...] = dal
            ddt_ref[...] = ddt

        @pl.when(i > 0)
        def _():
            dal_ref[...] += dal
            ddt_ref[...] += ddt

    row = lambda c: pl.BlockSpec((tm, c), lambda i: (i, 0))
    vec = pl.BlockSpec((1, LANES), lambda i: (0, 0))
    return pl.pallas_call(
        body, grid=(t // tm,),
        in_specs=[row(cw), row(cw), row(LANES), pl.BlockSpec((tm, LANES), lambda i: (i, ab_blk)), vec, vec, row(kvw),
                  ANY_SPEC],
        out_specs=[row(cw), pl.BlockSpec((tm, seg), lambda i: (i, dp_col // seg)), vec, vec],
        out_shape=[jax.ShapeDtypeStruct((t, cw), F32), jax.ShapeDtypeStruct(dp.shape, dp.dtype),
                   jax.ShapeDtypeStruct((1, LANES), F32), jax.ShapeDtypeStruct((1, LANES), F32)],
        input_output_aliases={7: 1}, compiler_params=_cp(("arbitrary",)), name=name,
    )(xc, dqkvn, daux, ab, alog_pad, dt_pad, dkv, dp)


def _gdn_pre_bwd2(dxc, qkvb, conv_w, dp, dp_col, *, name, tm=512):
    t, cw = dxc.shape
    tm = _blk(t, tm)
    hb = tm // HALO
    nblk = t // tm
    cg = GDN_HEADS * GDN_HEAD_DIM
    assert cw % cg == 0 and dp_col % cg == 0
    col0 = dp_col // cg

    def body(d_ref, dn_ref, x_ref, xp_ref, w_ref, _, dx_ref, dw_ref):
        i = pl.program_id(1)
        dcur = d_ref[...]
        dnxt = jnp.where(i < nblk - 1, dn_ref[...], 0.0)
        cur = x_ref[...]
        prev = jnp.where(i > 0, xp_ref[...], 0.0)
        dx = None
        dws = []
        for tap in range(GDN_CONV):
            j = GDN_CONV - 1 - tap
            term = w_ref[tap:tap + 1, :] * _shift_up(dcur, dnxt, j)
            dx = term if dx is None else dx + term
            dws.append(jnp.sum(dcur * _shift_down(cur, prev, j), axis=0, keepdims=True))
        dx_ref[...] = dx.astype(dx_ref.dtype)
        dw = jnp.concatenate(dws, axis=0)

        @pl.when(i == 0)
        def _():
            dw_ref[...] = dw

        @pl.when(i > 0)
        def _():
            dw_ref[...] += dw

    row = pl.BlockSpec((tm, cg), lambda c, i: (i, c))
    wsp = pl.BlockSpec((GDN_CONV, cg), lambda c, i: (0, c))
    return pl.pallas_call(
        body, grid=(cw // cg, nblk),
        in_specs=[row, pl.BlockSpec((HALO, cg), lambda c, i: (jnp.minimum((i + 1) * hb, t // HALO - 1), c)),
                  row, pl.BlockSpec((HALO, cg), lambda c, i: (jnp.maximum(i * hb - 1, 0), c)), wsp, ANY_SPEC],
        out_specs=[pl.BlockSpec((tm, cg), lambda c, i: (i, col0 + c)), wsp],
        out_shape=[jax.ShapeDtypeStruct(dp.shape, dp.dtype), jax.ShapeDtypeStruct((GDN_CONV, cw), F32)],
        input_output_aliases={5: 0}, compiler_params=_cp(("arbitrary", "arbitrary")), name=name,
    )(dxc, dxc, qkvb, qkvb, conv_w, dp)


def _gdn_post_fwd(o, z, norm_w, *, name, tm=512):
    t, gw = o.shape
    hd, nh = GDN_HEAD_DIM, GDN_HEADS
    tm = _blk(t, tm)

    def body(o_ref, z_ref, w_ref, y_ref):
        zv = z_ref[...]
        sz = zv * _sigmoid(zv)
        for h in range(nh):
            oh = o_ref[:, h * hd:(h + 1) * hd]
            r = lax.rsqrt(jnp.mean(oh * oh, axis=-1, keepdims=True) + RMS_EPS)
            y_ref[:, h * hd:(h + 1) * hd] = (oh * r * w_ref[...] * sz[:, h * hd:(h + 1) * hd]).astype(y_ref.dtype)

    row = pl.BlockSpec((tm, gw), lambda i: (i, 0))
    return pl.pallas_call(
        body, grid=(t // tm,), in_specs=[row, row, pl.BlockSpec((1, hd), lambda i: (0, 0))], out_specs=row,
        out_shape=jax.ShapeDtypeStruct((t, gw), BF16), compiler_params=_cp(("parallel",)), name=name,
    )(o, z, norm_w)


def _gdn_post_bwd(dy, o, z, norm_w, dp, dp_col, *, name, tm=512):
    t, gw = o.shape
    hd, nh = GDN_HEAD_DIM, GDN_HEADS
    tm = _blk(t, tm)

    def body(dy_ref, o_ref, z_ref, w_ref, _, do_ref, dz_ref, dw_ref):
        i = pl.program_id(0)
        zv = z_ref[...]
        sg = _sigmoid(zv)
        sz = zv * sg
        dsz = sg * (1.0 + zv * (1.0 - sg))
        dw = None
        for h in range(nh):
            sl = slice(h * hd, (h + 1) * hd)
            oh = o_ref[:, sl]
            dyh = dy_ref[:, sl].astype(F32)
            r = lax.rsqrt(jnp.mean(oh * oh, axis=-1, keepdims=True) + RMS_EPS)
            xh = oh * r
            dz_ref[:, sl] = (dyh * xh * w_ref[...] * dsz[:, sl]).astype(dz_ref.dtype)
            dn = dyh * sz[:, sl]
            dxh = dn * w_ref[...]
            do_ref[:, sl] = r * (dxh - xh * jnp.mean(dxh * xh, axis=-1, keepdims=True))
            part = jnp.sum(dn * xh, axis=0, keepdims=True)
            dw = part if dw is None else dw + part

        @pl.when(i == 0)
        def _():
            dw_ref[...] = dw

        @pl.when(i > 0)
        def _():
            dw_ref[...] += dw

    row = pl.BlockSpec((tm, gw), lambda i: (i, 0))
    vec = pl.BlockSpec((1, hd), lambda i: (0, 0))
    return pl.pallas_call(
        body, grid=(t // tm,), in_specs=[row, row, row, vec, ANY_SPEC],
        out_specs=[row, pl.BlockSpec((tm, gw), lambda i: (i, dp_col // gw)), vec],
        out_shape=[jax.ShapeDtypeStruct((t, gw), F32), jax.ShapeDtypeStruct(dp.shape, dp.dtype),
                   jax.ShapeDtypeStruct((1, hd), F32)],
        input_output_aliases={4: 1}, compiler_params=_cp(("arbitrary",)), name=name,
    )(dy, o, z, norm_w, dp)


IN_NAMES = ("q_a", "kv_a", "qkv_b", "ab", "z", "q_c", "gates")
CAT_NAMES = ("gates", "q_a", "qkv_b", "z", "q_c", "kv_a", "ab")
AB_PAD = 256


def _in_widths(d):
    gw = GDN_HEADS * GDN_HEAD_DIM
    return dict(q_a=SWA_Q_HEADS * SWA_HEAD_DIM, kv_a=2 * SWA_KV_HEADS * SWA_HEAD_DIM, qkv_b=3 * gw, ab=2 * GDN_HEADS,
                z=gw, q_c=XA_HEADS * XA_HEAD_DIM, gates=3 * d)


def _ranges(names, widths):
    out, start = {}, 0
    for k in names:
        out[k] = (start, widths[k])
        start += widths[k]
    return out, start


def _cat_ranges(d):
    widths = dict(_in_widths(d), ab=AB_PAD)
    return _ranges(CAT_NAMES, widths)


def _to_cat(shards, *, name="to_cat", tm=256):
    ns, d, n = shards.shape
    src, _ = _ranges(IN_NAMES, _in_widths(d))
    _, cat_w = _cat_ranges(d)
    pieces = []
    for k in CAT_NAMES:
        lo, hi = src[k][0], src[k][0] + src[k][1]
        for s in range(ns):
            a, b = max(lo, s * n), min(hi, (s + 1) * n)
            if a < b:
                pieces.append((s, a - s * n, b - s * n))
    tm = _blk(d, tm)

    def body(s_ref, o_ref):
        cols = [s_ref[s, :, a:b] for s, a, b in pieces]
        cols.append(jnp.zeros((tm, AB_PAD - src["ab"][1]), o_ref.dtype))
        o_ref[...] = jnp.concatenate(cols, axis=1)

    return pl.pallas_call(
        body, grid=(d // tm,),
        in_specs=[pl.BlockSpec((ns, tm, n), lambda i: (0, i, 0))],
        out_specs=pl.BlockSpec((tm, cat_w), lambda i: (i, 0)),
        out_shape=jax.ShapeDtypeStruct((d, cat_w), shards.dtype),
        compiler_params=_cp(("parallel",)), name=name,
    )(shards)


def _from_cat(w_cat, *, name="from_cat", tm=256):
    d, cat_w = w_cat.shape
    src, total = _ranges(IN_NAMES, _in_widths(d))
    cat, _ = _cat_ranges(d)
    n = total // N_SHARDS
    pieces = []
    for s in range(N_SHARDS):
        pieces.append([])
        for k in IN_NAMES:
            a, b = max(s * n, src[k][0]), min((s + 1) * n, src[k][0] + src[k][1])
            if a < b:
                pieces[s].append((cat[k][0] + a - src[k][0], cat[k][0] + b - src[k][0]))
    tm = _blk(d, tm)

    def body(c_ref, o_ref):
        for s in range(N_SHARDS):
            o_ref[s] = jnp.concatenate([c_ref[:, a:b] for a, b in pieces[s]], axis=1)

    return pl.pallas_call(
        body, grid=(d // tm,),
        in_specs=[pl.BlockSpec((tm, cat_w), lambda i: (i, 0))],
        out_specs=pl.BlockSpec((N_SHARDS, tm, n), lambda i: (0, i, 0)),
        out_shape=jax.ShapeDtypeStruct((N_SHARDS, d, n), w_cat.dtype),
        compiler_params=_cp(("parallel",)), name=name,
    )(w_cat)


def _pad_cols(a, width):
    return jnp.pad(a, ((0, 0), (0, width - a.shape[1])))


def _relu2_epilogue(acc):
    r = jnp.maximum(acc, 0.0)
    return acc, r * r


def _add_epilogue(acc, res):
    return (acc + res,)


def _drelu2_epilogue(acc, u):
    return (acc * (2.0 * jnp.maximum(u.astype(F32), 0.0)),)


def _local_step(x, mem, tgt, wts, small, comm=None):
    t, d = x.shape
    nh = GDN_HEADS
    cat, cat_w = _cat_ranges(d)
    alog_pad = _pad_cols(small["a_log"], LANES)
    dt_pad = _pad_cols(small["dt_bias"], LANES)
    kvw = cat["kv_a"][1]
    assert cat["ab"][0] == cat["kv_a"][0] + kvw
    ab_blk = kvw // LANES

    if comm is None:
        n = _rms_fwd(x, small["g_mix"], name="rms_mix")
        w_cat = wts["w_cat"]
    else:
        n, landed = _rms_fwd(x, small["g_mix"], name="rms_mix", exchange=comm.gather_exchange(["w_in"]))
        w_cat = _to_cat(_exchange_call(_gather_pass_on(landed), name="ag_w_in_pass")[0])
    assert w_cat.shape == (d, cat_w)
    q_a = _mm(n, w_cat, b_window=cat["q_a"], out_dtypes=(BF16,), name="in_q_a")
    kv_a, ab = _mm(n, w_cat, b_window=(cat["kv_a"][0], kvw + AB_PAD), out_dtypes=(BF16, F32), name="in_kv_ab")
    qkvb = _mm(n, w_cat, b_window=cat["qkv_b"], tn=512, name="in_qkv_b")
    z = _mm(n, w_cat, b_window=cat["z"], name="in_z")
    q_c = _mm(n, w_cat, b_window=cat["q_c"], out_dtypes=(BF16,), name="in_q_c")
    if comm is None:
        gates = _mm(n, w_cat, b_window=cat["gates"], name="in_gates")
        y_a, lse = _swa_fwd(q_a, kv_a, small["sinks"], name="swa_fwd")
    else:
        gates, landed_mlp = _mm(n, w_cat, b_window=cat["gates"], name="in_gates",
                                exchange=comm.gather_exchange(comm.MLP[1:]))
        y_a, lse, landed = _swa_fwd(q_a, kv_a, small["sinks"], name="swa_fwd", exchange=comm.gather_exchange(comm.MLP[:1]))
        landed_mlp = landed + landed_mlp
    xc, qkvn, aux = _gdn_pre_fwd(qkvb, small["conv_w"], ab, alog_pad, dt_pad, ab_blk=ab_blk, name="gdn_pre_fwd")
    aux_t = aux[:, :16].reshape(t // GDN_CHUNK, GDN_CHUNK, 16).transpose(0, 2, 1)
    if comm is None:
        gdn_u, gdn_w, gdn_qd, gdn_kd, gdn_mm, gdn_tinv = _gdn_local_fwd(qkvn, aux, aux_t, name="gdn_local_fwd")
    else:
        gdn_u, gdn_w, gdn_qd, gdn_kd, gdn_mm, gdn_tinv, landed_mid = _gdn_local_fwd(
            qkvn, aux, aux_t, name="gdn_local_fwd", exchange=comm.gather_exchange(comm.mid))
        wts = dict(wts, **comm.gathered(comm.mid, landed_mid, "mid"))
    o_b, gdn_vn, s_all = _gdn_seq_fwd(gdn_u, gdn_w, gdn_qd, gdn_kd, gdn_mm, aux, name="gdn_seq_fwd")
    y_b = _gdn_post_fwd(o_b, z, small["gdn_norm_w"], name="gdn_post_fwd")
    nmem = _rms_fwd(mem, small["g_mem"], name="rms_mem")
    mkv = _mm(nmem, wts["w_mem_kv"], out_dtypes=(BF16,), name="mem_kv")
    y_c = _xa_fwd(q_c, mkv, name="xa_fwd")
    ys = (y_a, y_b, y_c)
    w_ups = (wts["w_swa_up"], wts["w_gdn_up"], wts["w_xa_up"])
    merged = _merge_fwd(ys, w_ups, gates, name="merge_fwd")
    if comm is None:
        h1 = _mm(merged, wts["w_out"], extras=(x,), epilogue=_add_epilogue, name="out_proj")
    else:
        h1, whole = _mm(merged, wts["w_out"], extras=(x,), epilogue=_add_epilogue, name="out_proj",
                        exchange=_gather_pass_on(landed_mlp))
        wts = dict(wts, **comm.as_weights(comm.MLP, whole))
    n2 = _rms_fwd(h1, small["g_mlp"], name="rms_mlp")
    u, act = _mm(n2, wts["w_mlp_in"], b_sharded=True, out_dtypes=(BF16, BF16), epilogue=_relu2_epilogue, name="mlp_in")
    h2 = _mm(act, wts["w_mlp_out"], extras=(h1,), epilogue=_add_epilogue, name="mlp_out")
    dh2, dh2_b, dg_final, loss = _final_loss(h2, small["g_final"], tgt, name="final_loss")

    grads = {"g_final": dg_final}
    du = _mm(dh2_b, wts["w_mlp_out"], tb=True, out_dtypes=(BF16,), extras=(u,), epilogue=_drelu2_epilogue, name="d_mlp_act")
    grads["w_mlp_out"] = _mm(act, dh2_b, ta=True, out_dtypes=(BF16,), name="dw_mlp_out")
    grads["w_mlp_in"] = _mm(n2, du, ta=True, out_sharded=True, out_dtypes=(BF16,), name="dw_mlp_in")
    if comm is None:
        dn2 = _mm(du, wts["w_mlp_in"], tb=True, b_sharded=True, name="d_mlp_in")
    else:
        g_mlp = [comm.shard_major(k, grads.pop(k)) for k in comm.MLP]
        dn2, sib_mlp = _mm(du, wts["w_mlp_in"], tb=True, b_sharded=True, name="d_mlp_in", exchange=_sibling_halves(g_mlp))
        s1_mlp = comm.pair_sums(g_mlp, "mlp", sib_mlp)
    dh1, dh1_b, grads["g_mlp"] = _rms_bwd(dn2, h1, small["g_mlp"], dh2, name="rms_mlp_bwd")
    dmerged = _mm(dh1_b, wts["w_out"], tb=True, name="d_out_proj")
    grads["w_out"] = _mm(merged, dh1_b, ta=True, out_dtypes=(BF16,), name="dw_out")
    *dus, dp = _merge_bwd(ys, w_ups, gates, dmerged, cat_w, name="merge_bwd")
    dys = []
    for y, du_i, w_up, key in zip(ys, dus, w_ups, ("w_swa_up", "w_gdn_up", "w_xa_up")):
        dys.append(_mm(du_i, w_up, tb=True, b_sharded=True, out_dtypes=(BF16,), name="d_" + key))
        grads[key] = _mm(y, du_i, ta=True, out_sharded=True, out_dtypes=(BF16,), name="dw_" + key[2:])
    dp, dkv_a, grads["sinks"] = _swa_bwd(q_a, kv_a, small["sinks"], y_a, lse, dys[0], dp, cat["q_a"][0], name="swa_bwd")
    do_b, dp, grads["gdn_norm_w"] = _gdn_post_bwd(dys[1], o_b, z, small["gdn_norm_w"], dp, cat["z"][0],
                                                  name="gdn_post_bwd")
    dvn, dqd, dkd, dw_, dlast = _gdn_seq_bwd(do_b, gdn_w, gdn_qd, gdn_kd, gdn_mm, gdn_vn, s_all, aux, name="gdn_seq_bwd")
    dqkvn, daux = _gdn_local_bwd(qkvn, aux, aux_t, gdn_tinv, gdn_u, gdn_w, gdn_vn, do_b, dvn, dqd, dkd, dw_, dlast,
                                 name="gdn_local_bwd")
    dxc, dp, dalog, ddt = _gdn_pre_bwd1(xc, dqkvn, daux, ab, alog_pad, dt_pad, dkv_a, dp, cat["kv_a"][0], ab_blk=ab_blk,
                                        name="gdn_pre_bwd1")
    grads["a_log"], grads["dt_bias"] = dalog[:, :nh], ddt[:, :nh]
    dp, grads["conv_w"] = _gdn_pre_bwd2(dxc, qkvb, small["conv_w"], dp, cat["qkv_b"][0], name="gdn_pre_bwd2")
    dp, dmkv = _xa_bwd(q_c, mkv, dys[2], dp, cat["q_c"][0], name="xa_bwd")
    grads["w_mem_kv"] = _mm(nmem, dmkv, ta=True, out_dtypes=(BF16,), name="dw_mem_kv")
    dnmem = _mm(dmkv, wts["w_mem_kv"], tb=True, name="d_mem_kv")
    _, _, grads["g_mem"] = _rms_bwd(dnmem, mem, small["g_mem"], jnp.zeros_like(mem), name="rms_mem_bwd")
    if comm is None:
        grads["w_cat"] = _mm(n, dp, ta=True, out_dtypes=(BF16,), name="dw_in")
        dn = _mm(dp, w_cat, tb=True, name="d_in_proj")
    else:
        s1_mid = comm.pair_sums([comm.shard_major(k, grads.pop(k)) for k in comm.mid], "mid")
        dw_cat, rcv_mlp = _mm(n, dp, ta=True, out_dtypes=(BF16,), name="dw_in", exchange=_chip_exchange(s1_mlp))
        s1_in = comm.pair_sums([_from_cat(dw_cat)], "in")
        dn, rcv_rest = _mm(dp, w_cat, tb=True, name="d_in_proj", exchange=_chip_exchange(s1_in + s1_mid))
        halves = comm.chip_sums(s1_in + s1_mid + s1_mlp, rcv_rest + rcv_mlp)
        reduced = _exchange_call(_join_halves(halves), name="rs_join_halves")
        grads.update(zip(["w_in"] + comm.mid + list(comm.MLP), reduced))
    dx, _, grads["g_mix"] = _rms_bwd(dn, x, small["g_mix"], dh1, name="rms_mix_bwd")
    return loss, dx, grads


HBM_SPEC = pl.BlockSpec(memory_space=pltpu.HBM)
VMEM_SPEC = pl.BlockSpec(memory_space=pltpu.VMEM)
N_CHIPS = N_SHARDS
N_DEV = 8
DMA_CHUNK_BYTES = 1 << 20


def _place():
    return lax.axis_index("x"), lax.axis_index("y"), lax.axis_index("c")


def _other_chips(x, y):
    return [(1 - x, y), (x, 1 - y), (1 - x, 1 - y)]


def _n_chunks(rows, row_bytes):
    n = 1
    while rows % (2 * n) == 0 and (rows // (2 * n)) % 16 == 0 and (rows // n) * row_bytes > DMA_CHUNK_BYTES:
        n *= 2
    return n


def _sem_scratch(n_remote, n_local):
    return [pltpu.SemaphoreType.DMA((max(n_remote, 1),)), pltpu.SemaphoreType.DMA((max(n_remote, 1),)),
            pltpu.SemaphoreType.DMA((max(n_local, 1),))]


def _gather_over_ici(shards):
    plan = _half_chunks(shards, 0)

    def copies_of(in_refs, out_refs, place):
        x, y, c = place
        remote, local = [], []
        for i, r0, nr in plan:
            rh = shards[i].shape[0] // 2
            mine = pl.ds(c * rh + r0, nr)
            for chip in _other_chips(x, y):
                remote.append((in_refs[i].at[mine], out_refs[i].at[2 * x + y, mine], (*chip, c)))
            for half in range(2):
                rows = pl.ds(half * rh + r0, nr)
                local.append((in_refs[i].at[rows], out_refs[i].at[2 * x + y, rows]))
        return remote, local

    shapes = tuple(jax.ShapeDtypeStruct((N_CHIPS, *s.shape), s.dtype) for s in shards)
    return Exchange(tuple(shards), shapes, 3 * len(plan), 2 * len(plan), copies_of)


def _gather_pass_on(arrived):
    plan = _half_chunks([jax.ShapeDtypeStruct(a.shape[1:], a.dtype) for a in arrived], 0)

    def copies_of(in_refs, out_refs, place):
        x, y, c = place
        remote = []
        for i, r0, nr in plan:
            mine = pl.ds(c * (arrived[i].shape[1] // 2) + r0, nr)
            for chip in _other_chips(x, y):
                rows = out_refs[i].at[2 * chip[0] + chip[1], mine]
                remote.append((rows, rows, (x, y, 1 - c)))
        return remote, []

    shapes = tuple(jax.ShapeDtypeStruct(a.shape, a.dtype) for a in arrived)
    return Exchange(tuple(arrived), shapes, 3 * len(plan), 0, copies_of, tuple((i, i) for i in range(len(arrived))))


def _exchange_call(ex, *, name):
    n_in, n_out = len(ex.ins), len(ex.out_shapes)

    def body(*refs):
        cps = _exchange_copies(ex, refs[:n_in], refs[n_in:n_in + n_out], refs[n_in + n_out:])
        for cp in cps:
            cp.start()
        for cp in cps:
            cp.wait()

    return pl.pallas_call(
        body, out_shape=list(ex.out_shapes), in_specs=[HBM_SPEC] * n_in, out_specs=[HBM_SPEC] * n_out,
        scratch_shapes=_sem_scratch(ex.n_remote, ex.n_local), input_output_aliases=dict(ex.aliases), name=name,
    )(*ex.ins)


def _half_chunks(arrs, row_axis):
    plan = []
    for i, a in enumerate(arrs):
        rh = a.shape[row_axis] // 2
        row_bytes = a.dtype.itemsize * math.prod(a.shape) // a.shape[row_axis]
        nch = _n_chunks(rh, row_bytes)
        plan += [(i, q * (rh // nch), rh // nch) for q in range(nch)]
    return plan


def _sibling_halves(gs):
    plan = _half_chunks(gs, 1)

    def copies_of(in_refs, out_refs, place):
        x, y, c = place
        out = []
        for i, r0, nr in plan:
            rh = gs[i].shape[1] // 2
            out.append((in_refs[i].at[:, pl.ds((1 - c) * rh + r0, nr), :], out_refs[i].at[:, pl.ds(r0, nr), :],
                        (x, y, 1 - c)))
        return out, []

    shapes = tuple(jax.ShapeDtypeStruct((g.shape[0], g.shape[1] // 2, g.shape[2]), g.dtype) for g in gs)
    return Exchange(tuple(gs), shapes, len(plan), 0, copies_of)


def _chip_exchange(s1s):
    plan = _half_chunks([jax.ShapeDtypeStruct((2 * s.shape[1], s.shape[2]), s.dtype) for s in s1s], 0)

    def copies_of(in_refs, out_refs, place):
        x, y, c = place
        out = []
        for i, r0, nr in plan:
            for j, chip in enumerate(_other_chips(x, y)):
                out.append((in_refs[i].at[2 * chip[0] + chip[1], pl.ds(r0, nr), :], out_refs[i].at[j, pl.ds(r0, nr), :],
                            (*chip, c)))
        return out, []

    shapes = tuple(jax.ShapeDtypeStruct((3, *s.shape[1:]), s.dtype) for s in s1s)
    return Exchange(tuple(s1s), shapes, 3 * len(plan), 0, copies_of)


def _join_halves(gs):
    plan = _half_chunks(gs, 0)

    def copies_of(in_refs, out_refs, place):
        x, y, c = place
        out = []
        for i, r0, nr in plan:
            rows = out_refs[i].at[pl.ds(c * (gs[i].shape[0] // 2) + r0, nr), :]
            out.append((rows, rows, (x, y, 1 - c)))
        return out, []

    shapes = tuple(jax.ShapeDtypeStruct(g.shape, g.dtype) for g in gs)
    aliases = tuple((i, i) for i in range(len(gs)))
    return Exchange(tuple(gs), shapes, len(plan), 0, copies_of, aliases)


def _row_block(rows, cols):
    tb = rows
    while tb % 32 == 0 and tb * cols * 4 > (2 << 20):
        tb //= 2
    return tb


def _pair_sum(g, sib, core, *, name):
    ns, r, c = g.shape
    rh = r // 2
    tb = _row_block(rh, c)
    nb = rh // tb

    def body(core_ref, g_ref, s_ref, o_ref):
        o_ref[...] = (g_ref[...].astype(F32) + s_ref[...].astype(F32)).astype(o_ref.dtype)

    mine = pl.BlockSpec((None, tb, c), lambda s, i, core_ref: (s, core_ref[0] * nb + i, 0))
    half = pl.BlockSpec((None, tb, c), lambda s, i, core_ref: (s, i, 0))
    return pl.pallas_call(
        body, grid_spec=pltpu.PrefetchScalarGridSpec(num_scalar_prefetch=1, grid=(ns, nb), in_specs=[mine, half],
                                                     out_specs=half),
        out_shape=jax.ShapeDtypeStruct((ns, rh, c), BF16), compiler_params=_cp(("parallel", "parallel")), name=name,
    )(core, g, sib)


def _chip_sum(s1, rcv, where, *, name):
    _, rh, c = s1.shape
    tb = _row_block(rh, c)
    nb = rh // tb

    def body(where_ref, own_ref, r0_ref, r1_ref, r2_ref, o_ref):
        acc = own_ref[...].astype(F32)
        for r in (r0_ref, r1_ref, r2_ref):
            acc = acc + r[...].astype(F32)
        o_ref[...] = acc

    own = pl.BlockSpec((None, tb, c), lambda i, w: (w[1], i, 0))
    got = [pl.BlockSpec((None, tb, c), functools.partial(lambda i, w, j: (j, i, 0), j=j)) for j in range(3)]
    return pl.pallas_call(
        body, grid_spec=pltpu.PrefetchScalarGridSpec(
            num_scalar_prefetch=1, grid=(nb,), in_specs=[own] + got,
            out_specs=pl.BlockSpec((tb, c), lambda i, w: (w[0] * nb + i, 0))),
        out_shape=jax.ShapeDtypeStruct((2 * rh, c), F32), compiler_params=_cp(("parallel",)), name=name,
    )(where, s1, rcv, rcv, rcv)


def _all_gather_small(blk, *, name):
    r = blk.shape[0]

    def body(b_ref, out_ref, send_sems, recv_sems):
        x, y, c = _place()
        me = 4 * x + 2 * y + c
        out_ref[me] = b_ref[...]
        sends = []
        for k in range(1, N_DEV):
            peer = (x ^ (k >> 2), y ^ ((k >> 1) & 1), c ^ (k & 1))
            sends.append(pltpu.make_async_remote_copy(src_ref=b_ref, dst_ref=out_ref.at[me], send_sem=send_sems.at[k - 1],
                                                      recv_sem=recv_sems.at[k - 1], device_id=peer, device_id_type=MESH))
        for cp in sends:
            cp.start()
        for k in range(1, N_DEV):
            rows = out_ref.at[me ^ k]
            pltpu.make_async_remote_copy(src_ref=rows, dst_ref=rows, send_sem=send_sems.at[k - 1],
                                         recv_sem=recv_sems.at[k - 1], device_id=(x, y, c), device_id_type=MESH).wait_recv()
        for cp in sends:
            cp.wait_send()

    return pl.pallas_call(
        body, out_shape=jax.ShapeDtypeStruct((N_DEV, r, LANES), blk.dtype), in_specs=[VMEM_SPEC], out_specs=VMEM_SPEC,
        scratch_shapes=[pltpu.SemaphoreType.DMA((N_DEV - 1,)), pltpu.SemaphoreType.DMA((N_DEV - 1,))],
        name=name,
    )(blk)


def _sum_rows(parts, out_dtype, *, name, tb=1024):
    rows = parts[0].shape[0]
    tb = _blk(rows, tb)

    def body(*refs):
        acc = refs[0][...].astype(F32)
        for r in refs[1:-1]:
            acc = acc + r[...].astype(F32)
        refs[-1][...] = acc.astype(refs[-1].dtype)

    spec = pl.BlockSpec((tb, LANES), lambda i: (i, 0))
    return pl.pallas_call(
        body, grid=(rows // tb,), in_specs=[spec] * len(parts), out_specs=spec,
        out_shape=jax.ShapeDtypeStruct((rows, LANES), out_dtype), compiler_params=_cp(("parallel",)), name=name,
    )(*parts)


BIG = (
    ("w_in", 1), ("w_mem_kv", 0), ("w_swa_up", 1), ("w_gdn_up", 1), ("w_xa_up", 1), ("w_out", 0), ("w_mlp_in", 1),
    ("w_mlp_out", 0))


class _Comm:
    MLP = ("w_mlp_in", "w_mlp_out")

    def __init__(self, late_shards, core, where):
        self.axis = dict(BIG)
        self.late_shards = late_shards
        self.mid = [k for k in late_shards if k not in self.MLP and k != "w_in"]
        self.core, self.where = core, where

    def gather_exchange(self, names):
        return _gather_over_ici([self.late_shards[k] for k in names])

    def as_weights(self, names, whole):
        return {k: (g.reshape(-1, g.shape[2]) if self.axis[k] == 0 else g) for k, g in zip(names, whole)}

    def gathered(self, names, landed, tag):
        return self.as_weights(names, _exchange_call(_gather_pass_on(landed), name=f"ag_{tag}_pass"))

    def shard_major(self, k, grad):
        return grad.reshape(N_CHIPS, -1, grad.shape[-1]) if self.axis[k] == 0 else grad

    def pair_sums(self, gs, tag, sibs=None):
        if sibs is None:
            sibs = _exchange_call(_sibling_halves(gs), name=f"rs_sibling_{tag}")
        return [_pair_sum(g, s, self.core, name=f"rs_pair_sum_{tag}{i}") for i, (g, s) in enumerate(zip(gs, sibs))]

    def chip_sums(self, s1s, rcvs):
        return [_chip_sum(s1, rcv, self.where, name=f"rs_chip_sum_{i}") for i, (s1, rcv) in enumerate(zip(s1s, rcvs))]
SMALL = ("g_mix", "sinks", "a_log", "dt_bias", "gdn_norm_w", "g_mem", "g_mlp", "g_final")


def _rows128(a, rows):
    flat = a.reshape(-1)
    return jnp.pad(flat, (0, rows * LANES - flat.shape[0])).reshape(rows, LANES)


def kernel(x, mem, g_mix, w_in, sinks, conv_w, a_log, dt_bias, gdn_norm_w, g_mem, w_mem_kv, w_swa_up, w_gdn_up, w_xa_up, w_out, g_mlp, w_mlp_in, w_mlp_out, g_final, loss_target, m_g_mix, m_w_in, m_sinks, m_conv_w, m_a_log, m_dt_bias, m_gdn_norm_w, m_g_mem, m_w_mem_kv, m_w_swa_up, m_w_gdn_up, m_w_xa_up, m_w_out, m_g_mlp, m_w_mlp_in, m_w_mlp_out, m_g_final, v_g_mix, v_w_in, v_sinks, v_conv_w, v_a_log, v_dt_bias, v_gdn_norm_w, v_g_mem, v_w_mem_kv, v_w_swa_up, v_w_gdn_up, v_w_xa_up, v_w_out, v_g_mlp, v_w_mlp_in, v_w_mlp_out, v_g_final):
    given = dict(locals())
    xi, yi, ci = _place()
    chip = 2 * xi + yi
    core = jnp.reshape(ci, (1,)).astype(jnp.int32)
    where = jnp.stack([ci, chip]).astype(jnp.int32)

    comm = _Comm({k: given[k][0].astype(BF16) for k, _ in BIG}, core, where)
    wts = {}
    conv_shard = conv_w[0]
    conv_rows = -(-conv_shard.size // (8 * LANES)) * 8
    conv_all = _all_gather_small(_rows128(conv_shard, conv_rows), name="ag_conv")
    conv_full = jnp.concatenate(
        [conv_all[2 * s].reshape(-1)[:conv_shard.size].reshape(conv_shard.shape) for s in range(N_CHIPS)], axis=1)

    small = {k: given[k].reshape(1, -1) for k in SMALL}
    small["conv_w"] = conv_full
    loss_row, dx, grads = _local_step(x[0], mem[0], loss_target[0], wts, small, comm)
    big_grads = {k: grads[k] for k, _ in BIG}

    layout = [("loss", loss_row[:, :1])] + [(k, grads[k]) for k in SMALL] + [("conv_w", grads["conv_w"])]
    rows = [-(-a.size // LANES) for _, a in layout]
    blk_rows = -(-sum(rows) // 8) * 8
    blk = jnp.concatenate([_rows128(a.astype(F32), n) for (_, a), n in zip(layout, rows)]
                          + [jnp.zeros((blk_rows - sum(rows), LANES), F32)], axis=0)
    gathered = _all_gather_small(blk, name="ag_small_grads")
    reduced = _sum_rows([gathered[i] for i in range(N_DEV)], F32, name="small_grad_sum")
    small_grads, start = {}, 0
    for (k, a), n in zip(layout, rows):
        small_grads[k] = reduced[start:start + n].reshape(-1)[:a.size].reshape(a.shape)
        start += n
    loss = small_grads["loss"].reshape(())
    cw = conv_shard.shape[1]
    conv_grad = lax.dynamic_slice_in_dim(small_grads["conv_w"], chip * cw, cw, axis=1)

    names = ["g_mix", "w_in", "sinks", "conv_w", "a_log", "dt_bias", "gdn_norm_w", "g_mem", "w_mem_kv", "w_swa_up",
             "w_gdn_up", "w_xa_up", "w_out", "g_mlp", "w_mlp_in", "w_mlp_out", "g_final"]
    out_g, out_d, out_m, out_v = [], [], [], []
    for k in names:
        w, m, v = given[k], given["m_" + k], given["v_" + k]
        if k in big_grads:
            g2 = big_grads[k]
        elif k == "conv_w":
            g2 = conv_grad
        else:
            g2 = small_grads[k]
        as_given = (lambda a: a.reshape(1, -1)) if w.ndim == 1 else (lambda a: a)
        if w.shape[-1] % LANES and w.shape[-1] > LANES:
            tr = lambda a: jnp.swapaxes(a, -1, -2)
            g_out, delta, new_m, new_v = (tr(a) for a in _adamw(tr(w), tr(g2), tr(m), tr(v), name="adamw_" + k))
        else:
            g_out, delta, new_m, new_v = _adamw(as_given(w), g2, as_given(m), as_given(v), name="adamw_" + k)
        out_g.append(g_out.reshape(w.shape))
        out_d.append(delta.reshape(w.shape))
        out_m.append(new_m.reshape(w.shape))
        out_v.append(new_v.reshape(w.shape))
    return (loss, dx[None], *out_g, *out_d, *out_m, *out_v)
```

```python
import functools
import math
from typing import Callable, NamedTuple

import jax
import jax.numpy as jnp
from jax import lax
from jax.experimental import pallas as pl
from jax.experimental.pallas import tpu as pltpu

F32 = jnp.float32
BF16 = jnp.bfloat16
HI = lax.Precision.HIGHEST
MESH = pl.DeviceIdType.MESH

SWA_Q_HEADS = 16
SWA_KV_HEADS = 2
SWA_HEAD_DIM = 64
SWA_WINDOW = 128
SWA_SCALE = SWA_HEAD_DIM ** -0.5
assert math.frexp(SWA_SCALE)[0] == 0.5
GDN_HEADS = 4
GDN_HEAD_DIM = 128
GDN_CONV = 4
GDN_CHUNK = 64
XA_HEADS = 4
XA_HEAD_DIM = 128
RMS_EPS = 1e-6
L2_EPS = 1e-6
ADAM_LR = 0.001
ADAM_B1 = 0.9
ADAM_B2 = 0.999
ADAM_EPS = 1e-08
ADAM_WD = 0.01
ADAM_STEP = 10

LANES = 128
N_SHARDS = 4
VMEM_LIMIT = 56 * 1024 * 1024

NT = (((1,), (1,)), ((), ()))
TN = (((0,), (0,)), ((), ()))
NN = (((1,), (0,)), ((), ()))


def _cp(sem=None):
    return pltpu.CompilerParams(dimension_semantics=sem, vmem_limit_bytes=VMEM_LIMIT)


def _blk(dim, pref):
    if dim <= pref:
        return dim
    b = (pref // LANES) * LANES
    while dim % b:
        b -= LANES
    assert b > 0, (dim, pref)
    return b


def _dot(a, b, dims=NN, precision=None):
    return lax.dot_general(a, b, dims, precision=precision, preferred_element_type=F32)


def _sigmoid(x):
    return 0.5 * jnp.tanh(0.5 * x) + 0.5


MM_TK_BYTES = 4096


def _mm(a, b, *, name, ta=False, tb=False, out_dtypes=(F32,), epilogue=None, extras=(), tm=1024, tn=1024, tk=None,
        b_sharded=False, out_sharded=False, b_window=None, exchange=None):
    (kdim, m) = a.shape if ta else a.shape[::-1]
    col0 = 0
    n_lim = k_lim = None
    if b_sharded:
        ns, rows_w, per = b.shape
        if tb:
            kb, n, k_lim = ns * per, rows_w, per
        else:
            kb, n, n_lim = rows_w, ns * per, per
    else:
        (kb, n) = b.shape[::-1] if tb else b.shape
        if b_window is not None:
            assert not tb
            col0, n = b_window
    assert kdim == kb, (a.shape, b.shape, ta, tb)
    if out_sharded:
        assert n % N_SHARDS == 0
        n_lim = n // N_SHARDS if n_lim is None else n_lim
        assert n_lim == n // N_SHARDS
    if tk is None:
        tk = MM_TK_BYTES // max(a.dtype.itemsize, b.dtype.itemsize)
    tm, tn, tk = _blk(m, tm), _blk(n_lim or n, tn), _blk(k_lim or kdim, tk)
    assert col0 % tn == 0, (col0, tn)
    nk = kdim // tk
    a_spec = pl.BlockSpec((tk, tm), lambda i, j, k: (k, i)) if ta else pl.BlockSpec((tm, tk), lambda i, j, k: (i, k))
    if b_sharded and tb:
        kpb = k_lim // tk
        b_spec = pl.BlockSpec((None, tn, tk), lambda i, j, k: (k // kpb, j, k % kpb))
    elif b_sharded:
        bpb = n_lim // tn
        b_spec = pl.BlockSpec((None, tk, tn), lambda i, j, k: (j // bpb, k, j % bpb))
    elif tb:
        b_spec = pl.BlockSpec((tn, tk), lambda i, j, k: (j, k))
    else:
        b_spec = pl.BlockSpec((tk, tn), lambda i, j, k: (k, j + col0 // tn))
    x_spec = pl.BlockSpec((tm, tn), lambda i, j, k: (i, j))
    if out_sharded:
        opb = n_lim // tn
        o_spec = pl.BlockSpec((None, tm, tn), lambda i, j, k: (j // opb, i, j % opb))
        out_shape = (N_SHARDS, m, n_lim)
    else:
        o_spec, out_shape = x_spec, (m, n)
    dims = ((((0 if ta else 1),), ((1 if tb else 0),)), ((), ()))
    n_extra, n_out = len(extras), len(out_dtypes)

    host = _ExchangeHost(exchange)
    grid = (m // tm, n // tn, nk)

    def body(*refs):
        a_ref, b_ref = refs[:2]
        extra_refs = refs[2:2 + n_extra]
        out_refs = refs[2 + n_extra + host.n_in:2 + n_extra + host.n_in + n_out]
        host.start(refs, 2 + n_extra, 2 + n_extra + host.n_in + n_out, grid)
        part = _dot(a_ref[...].astype(BF16), b_ref[...].astype(BF16), dims)

        def finish(acc):
            vals = epilogue(acc, *[r[...] for r in extra_refs]) if epilogue is not None else (acc,) * n_out
            assert len(vals) == n_out
            for r, v in zip(out_refs, vals):
                r[...] = v.astype(r.dtype)

        if nk == 1:
            finish(part)
        else:
            acc_ref = refs[2 + n_extra + host.n_in + n_out + host.n_out]
            k = pl.program_id(2)

            @pl.when(k == 0)
            def _():
                acc_ref[...] = part

            @pl.when((k > 0) & (k < nk - 1))
            def _():
                acc_ref[...] += part

            @pl.when(k == nk - 1)
            def _():
                finish(acc_ref[...] + part)

        host.wait(refs, 2 + n_extra, 2 + n_extra + host.n_in + n_out, grid)

    outs = pl.pallas_call(
        body,
        grid=grid,
        in_specs=[a_spec, b_spec] + [x_spec] * n_extra + host.in_specs,
        out_specs=[o_spec] * n_out + host.out_specs,
        out_shape=[jax.ShapeDtypeStruct(out_shape, d) for d in out_dtypes] + host.out_shapes,
        scratch_shapes=([pltpu.VMEM((tm, tn), F32)] if nk > 1 else []) + host.scratch,
        input_output_aliases=host.aliases(2 + n_extra, n_out),
        compiler_params=_cp(host.semantics(("parallel", "parallel", "arbitrary"))),
        name=name,
    )(a, b, *extras, *host.ins)
    mine, landed = outs[:n_out], list(outs[n_out:])
    mine = mine[0] if n_out == 1 else mine
    return (mine, landed) if exchange is not None else mine


class Exchange(NamedTuple):
    ins: tuple
    out_shapes: tuple
    n_remote: int
    n_local: int
    copies_of: Callable
    aliases: tuple = ()


def _exchange_copies(ex, in_refs, out_refs, sem_refs):
    send_sems, recv_sems, local_sems = sem_refs
    remote, local = ex.copies_of(in_refs, out_refs, _place())
    assert len(remote) == ex.n_remote and len(local) == ex.n_local, (len(remote), len(local))
    cps = [pltpu.make_async_remote_copy(src_ref=src, dst_ref=dst, send_sem=send_sems.at[k], recv_sem=recv_sems.at[k],
                                        device_id=to, device_id_type=MESH) for k, (src, dst, to) in enumerate(remote)]
    cps += [pltpu.make_async_copy(src, dst, local_sems.at[k]) for k, (src, dst) in enumerate(local)]
    return cps


class _ExchangeHost:
    def __init__(self, ex):
        self.ex = ex
        self.ins = list(ex.ins) if ex else []
        self.out_shapes = list(ex.out_shapes) if ex else []
        self.n_in, self.n_out = len(self.ins), len(self.out_shapes)
        self.in_specs = [HBM_SPEC] * self.n_in
        self.out_specs = [HBM_SPEC] * self.n_out
        self.scratch = _sem_scratch(ex.n_remote, ex.n_local) if ex else []

    def semantics(self, sem):
        return tuple("arbitrary" for _ in sem) if self.ex else sem

    def aliases(self, in_at, out_at):
        return {in_at + i: out_at + o for i, o in self.ex.aliases} if self.ex else {}

    def _refs(self, refs, in_at, out_at):
        return refs[in_at:in_at + self.n_in], refs[out_at:out_at + self.n_out], refs[len(refs) - 3:]

    def _when(self, grid, last):
        cond = None
        for d, size in enumerate(grid):
            c = pl.program_id(d) == (size - 1 if last else 0)
            cond = c if cond is None else cond & c
        return cond

    def start(self, refs, in_at, out_at, grid):
        if self.ex:
            @pl.when(self._when(grid, False))
            def _():
                for cp in _exchange_copies(self.ex, *self._refs(refs, in_at, out_at)):
                    cp.start()

    def wait(self, refs, in_at, out_at, grid):
        if self.ex:
            @pl.when(self._when(grid, True))
            def _():
                for cp in _exchange_copies(self.ex, *self._refs(refs, in_at, out_at)):
                    cp.wait()


def _rms_fwd(x, g, *, name, tm=512, exchange=None):
    t, d = x.shape
    tm = _blk(t, tm)
    host = _ExchangeHost(exchange)
    grid = (t // tm,)

    def body(*refs):
        x_ref, g_ref, n_ref = refs[0], refs[1], refs[2 + host.n_in]
        host.start(refs, 2, 3 + host.n_in, grid)
        xv = x_ref[...]
        r = lax.rsqrt(jnp.mean(xv * xv, axis=-1, keepdims=True) + RMS_EPS)
        n_ref[...] = (xv * r * g_ref[...]).astype(n_ref.dtype)
        host.wait(refs, 2, 3 + host.n_in, grid)

    outs = pl.pallas_call(
        body, grid=grid,
        in_specs=[pl.BlockSpec((tm, d), lambda i: (i, 0)), pl.BlockSpec((1, d), lambda i: (0, 0))] + host.in_specs,
        out_specs=[pl.BlockSpec((tm, d), lambda i: (i, 0))] + host.out_specs,
        out_shape=[jax.ShapeDtypeStruct((t, d), BF16)] + host.out_shapes,
        scratch_shapes=host.scratch, input_output_aliases=host.aliases(2, 1),
        compiler_params=_cp(host.semantics(("parallel",))), name=name,
    )(x, g, *host.ins)
    return (outs[0], list(outs[1:])) if exchange is not None else outs[0]


def _rms_bwd(dn, x, g, dres, *, name, tm=512):
    t, d = x.shape
    tm = _blk(t, tm)

    def body(dn_ref, x_ref, g_ref, dres_ref, dx_ref, dxb_ref, dg_ref):
        i = pl.program_id(0)
        xv = x_ref[...]
        r = lax.rsqrt(jnp.mean(xv * xv, axis=-1, keepdims=True) + RMS_EPS)
        xh = xv * r
        dnv = dn_ref[...].astype(F32)
        dxh = dnv * g_ref[...]
        dx = dres_ref[...] + r * (dxh - xh * jnp.mean(dxh * xh, axis=-1, keepdims=True))
        dx_ref[...] = dx
        dxb_ref[...] = dx.astype(dxb_ref.dtype)
        part = jnp.sum(dnv * xh, axis=0, keepdims=True)

        @pl.when(i == 0)
        def _():
            dg_ref[...] = part

        @pl.when(i > 0)
        def _():
            dg_ref[...] += part

    row = pl.BlockSpec((tm, d), lambda i: (i, 0))
    vec = pl.BlockSpec((1, d), lambda i: (0, 0))
    return pl.pallas_call(
        body, grid=(t // tm,),
        in_specs=[row, row, vec, row], out_specs=[row, row, vec],
        out_shape=[jax.ShapeDtypeStruct((t, d), F32), jax.ShapeDtypeStruct((t, d), BF16),
                   jax.ShapeDtypeStruct((1, d), F32)],
        compiler_params=_cp(("arbitrary",)), name=name,
    )(dn, x, g, dres)


def _final_loss(h, g, tgt, *, name, tm=512):
    t, d = h.shape
    tm = _blk(t, tm)

    def body(h_ref, g_ref, t_ref, dh_ref, dhb_ref, dg_ref, loss_ref):
        i = pl.program_id(0)
        hv = h_ref[...]
        r = lax.rsqrt(jnp.mean(hv * hv, axis=-1, keepdims=True) + RMS_EPS)
        xh = hv * r
        e = xh * g_ref[...] - t_ref[...]
        dy = e * (1.0 / d)
        dxh = dy * g_ref[...]
        dh = r * (dxh - xh * jnp.mean(dxh * xh, axis=-1, keepdims=True))
        dh_ref[...] = dh
        dhb_ref[...] = dh.astype(dhb_ref.dtype)
        dg_part = jnp.sum(dy * xh, axis=0, keepdims=True)
        row_loss = jnp.sum(e * e, axis=-1, keepdims=True) * (0.5 / d)
        loss_part = jnp.sum(row_loss, axis=0, keepdims=True)

        @pl.when(i == 0)
        def _():
            dg_ref[...] = dg_part
            loss_ref[...] = jnp.broadcast_to(loss_part, loss_ref.shape)

        @pl.when(i > 0)
        def _():
            dg_ref[...] += dg_part
            loss_ref[...] += jnp.broadcast_to(loss_part, loss_ref.shape)

    row = pl.BlockSpec((tm, d), lambda i: (i, 0))
    vec = pl.BlockSpec((1, d), lambda i: (0, 0))
    return pl.pallas_call(
        body, grid=(t // tm,),
        in_specs=[row, vec, row], out_specs=[row, row, vec, pl.BlockSpec((1, LANES), lambda i: (0, 0))],
        out_shape=[jax.ShapeDtypeStruct((t, d), F32), jax.ShapeDtypeStruct((t, d), BF16),
                   jax.ShapeDtypeStruct((1, d), F32), jax.ShapeDtypeStruct((1, LANES), F32)],
        compiler_params=_cp(("arbitrary",)), name=name,
    )(h, g, tgt)


SWA_SUB = 64


def _swa_mask(n, rows, row0):
    w = SWA_WINDOW
    qi = (lax.broadcasted_iota(jnp.int32, (rows, 2 * w), 0) + row0) & (w - 1)
    kj = lax.broadcasted_iota(jnp.int32, (rows, 2 * w), 1)
    return (kj > qi) & (kj <= qi + w) & ((n > 0) | (kj >= w))


def _stack_heads(ref, heads, width):
    return jnp.concatenate([ref[:, h * width:(h + 1) * width] for h in heads], axis=0)


def _stack_scalars(ref, heads, rows):
    return jnp.concatenate([jnp.broadcast_to(ref[0:1, h:h + 1], (rows, 1)) for h in heads], axis=0)


def _swa_fwd(q, kv, sinks, *, name, exchange=None):
    t = q.shape[0]
    w, hd, hq, hkv = SWA_WINDOW, SWA_HEAD_DIM, SWA_Q_HEADS, SWA_KV_HEADS
    grp = hq // hkv
    kvw = hkv * hd
    nb = t // w
    host = _ExchangeHost(exchange)
    assert not (exchange and exchange.aliases)

    def body(*refs):
        q_ref, kvp_ref, kvc_ref, s_ref = refs[:4]
        o_ref, lse_ref = refs[4 + host.n_in:6 + host.n_in]
        host.start(refs, 4, 6 + host.n_in, (nb,))
        n = pl.program_id(0)
        mask = _swa_mask(n, grp * w, 0)
        kvcat = jnp.concatenate([kvp_ref[...], kvc_ref[...]], axis=0)
        kvs = range(hkv)
        heads = [range(hk * grp, (hk + 1) * grp) for hk in kvs]
        sks = [_stack_scalars(s_ref, hs, w) for hs in heads]
        ss = [jnp.where(mask, _dot(_stack_heads(q_ref, heads[hk], hd) * SWA_SCALE, kvcat[:, hk * hd:(hk + 1) * hd], NT),
                        -jnp.inf) for hk in kvs]
        ms = [jnp.maximum(jnp.max(s, axis=-1, keepdims=True), sk) for s, sk in zip(ss, sks)]
        ps = [jnp.exp(s - m) for s, m in zip(ss, ms)]
        dens = [jnp.sum(p, axis=-1, keepdims=True) + jnp.exp(sk - m) for p, sk, m in zip(ps, sks, ms)]
        os_ = [_dot((p * (1.0 / den)).astype(BF16), kvcat[:, kvw + hk * hd:kvw + (hk + 1) * hd])
               for hk, p, den in zip(kvs, ps, dens)]
        outs, lses = [], []
        for o, m, den in zip(os_, ms, dens):
            lse = m + jnp.log(den)
            outs += [o[j * w:(j + 1) * w] for j in range(grp)]
            lses += [lse[j * w:(j + 1) * w] for j in range(grp)]
        o_ref[...] = jnp.concatenate(outs, axis=1).astype(o_ref.dtype)
        lse_ref[...] = jnp.concatenate(lses, axis=1)
        host.wait(refs, 4, 6 + host.n_in, (nb,))

    outs = pl.pallas_call(
        body, grid=(nb,),
        in_specs=[pl.BlockSpec((w, hq * hd), lambda i: (i, 0)),
                  pl.BlockSpec((w, 2 * kvw), lambda i: (jnp.maximum(i - 1, 0), 0)),
                  pl.BlockSpec((w, 2 * kvw), lambda i: (i, 0)),
                  pl.BlockSpec((1, hq), lambda i: (0, 0))] + host.in_specs,
        out_specs=[pl.BlockSpec((w, hq * hd), lambda i: (i, 0)), pl.BlockSpec((w, hq), lambda i: (i, 0))] + host.out_specs,
        out_shape=[jax.ShapeDtypeStruct((t, hq * hd), BF16), jax.ShapeDtypeStruct((t, hq), F32)] + host.out_shapes,
        scratch_shapes=host.scratch,
        compiler_params=_cp(host.semantics(("parallel",))), name=name,
    )(q, kv, kv, sinks, *host.ins)
    return (outs[0], outs[1], list(outs[2:])) if exchange is not None else outs


ANY_SPEC = pl.BlockSpec(memory_space=pl.ANY)


def _swa_bwd(q, kv, sinks, o, lse, do, dp, dp_col, *, name):
    t = q.shape[0]
    w, hd, hq, hkv = SWA_WINDOW, SWA_HEAD_DIM, SWA_Q_HEADS, SWA_KV_HEADS
    grp = hq // hkv
    kvw = hkv * hd
    nb = t // w
    assert dp_col % (hq * hd) == 0
    dq_blk = dp_col // (hq * hd)

    def body(q_ref, kvp_ref, kvc_ref, s_ref, o_ref, lse_ref, do_ref, _, dq_ref, dkv_ref, ds_ref, carry_ref, s_scr, dp_scr,
             p_scr, ds_scr):
        n = pl.program_id(0)

        @pl.when(n == 0)
        def _():
            ds_ref[...] = jnp.zeros_like(ds_ref)
            carry_ref[...] = jnp.zeros_like(carry_ref)

        @pl.when(n < nb)
        def _():
            kvcat = jnp.concatenate([kvp_ref[...], kvc_ref[...]], axis=0)
            dqs, dsk, dks, dvs = [], [], [], []
            for hk in range(hkv):
                heads = range(hk * grp, (hk + 1) * grp)
                qs = _stack_heads(q_ref, heads, hd)
                dos = _stack_heads(do_ref, heads, hd)
                os_ = _stack_heads(o_ref, heads, hd)
                lse = _stack_heads(lse_ref, heads, 1)
                kh = kvcat[:, hk * hd:(hk + 1) * hd]
                vh = kvcat[:, kvw + hk * hd:kvw + (hk + 1) * hd]
                delta = jnp.sum(dos.astype(F32) * os_.astype(F32), axis=-1, keepdims=True)
                s_scr[...] = _dot(qs * SWA_SCALE, kh, NT)
                dp_scr[...] = _dot(dos, vh, NT)
                for r0 in range(0, grp * w, SWA_SUB):
                    rows = slice(r0, r0 + SWA_SUB)
                    p = jnp.exp(jnp.where(_swa_mask(n, SWA_SUB, r0 % w), s_scr[rows, :], -jnp.inf) - lse[rows])
                    p_scr[rows, :] = p.astype(p_scr.dtype)
                    ds_scr[rows, :] = (p * (dp_scr[rows, :] - delta[rows]) * SWA_SCALE).astype(ds_scr.dtype)
                ds = ds_scr[...]
                dq = _dot(ds, kh)
                dqs += [dq[j * w:(j + 1) * w] for j in range(grp)]
                dks.append(_dot(ds, qs, TN))
                dvs.append(_dot(p_scr[...], dos, TN))
                dsink = -jnp.exp(_stack_scalars(s_ref, heads, w) - lse) * delta
                dsk += [jnp.sum(dsink[j * w:(j + 1) * w], axis=0, keepdims=True) for j in range(grp)]
            dq_ref[...] = jnp.concatenate(dqs, axis=1).astype(dq_ref.dtype)
            ds_ref[...] += jnp.concatenate(dsk, axis=1)
            dkv_cat = jnp.concatenate(dks + dvs, axis=1)
            dkv_ref[...] = (carry_ref[...] + dkv_cat[:w]).astype(dkv_ref.dtype)
            carry_ref[...] = dkv_cat[w:]

        @pl.when(n == nb)
        def _():
            dkv_ref[...] = carry_ref[...].astype(dkv_ref.dtype)

    cur = lambda i: (jnp.minimum(i, nb - 1), 0)
    prev = lambda i: (jnp.clip(i - 1, 0, nb - 1), 0)
    return pl.pallas_call(
        body, grid=(nb + 1,),
        in_specs=[pl.BlockSpec((w, hq * hd), cur), pl.BlockSpec((w, 2 * kvw), prev), pl.BlockSpec((w, 2 * kvw), cur),
                  pl.BlockSpec((1, hq), lambda i: (0, 0)), pl.BlockSpec((w, hq * hd), cur),
                  pl.BlockSpec((w, hq), cur), pl.BlockSpec((w, hq * hd), cur), ANY_SPEC],
        out_specs=[pl.BlockSpec((w, hq * hd), lambda i: (jnp.minimum(i, nb - 1), dq_blk)),
                   pl.BlockSpec((w, 2 * kvw), prev), pl.BlockSpec((1, hq), lambda i: (0, 0))],
        out_shape=[jax.ShapeDtypeStruct(dp.shape, dp.dtype), jax.ShapeDtypeStruct((t, 2 * kvw), BF16),
                   jax.ShapeDtypeStruct((1, hq), F32)],
        scratch_shapes=[pltpu.VMEM((w, 2 * kvw), F32)] + [pltpu.VMEM((grp * w, 2 * w), dt) for dt in (F32, F32, BF16, BF16)],
        input_output_aliases={7: 0},
        compiler_params=_cp(("arbitrary",)), name=name,
    )(q, kv, kv, sinks, o, lse, do, dp)


def _xa_fwd(q, mkv, *, name, tq=512):
    t, xw = q.shape
    nm = mkv.shape[0]
    hd, nh = XA_HEAD_DIM, XA_HEADS
    tq = _blk(t, tq)

    def body(q_ref, mkv_ref, o_ref):
        outs = []
        for h in range(nh):
            qh = q_ref[:, h * hd:(h + 1) * hd]
            kh = mkv_ref[:, h * hd:(h + 1) * hd]
            vh = mkv_ref[:, xw + h * hd:xw + (h + 1) * hd]
            s = _dot(qh, kh, NT) * (hd ** -0.5)
            p = jnp.exp(s - jnp.max(s, axis=-1, keepdims=True))
            p = p * (1.0 / jnp.sum(p, axis=-1, keepdims=True))
            outs.append(_dot(p.astype(BF16), vh))
        o_ref[...] = jnp.concatenate(outs, axis=1).astype(o_ref.dtype)

    return pl.pallas_call(
        body, grid=(t // tq,),
        in_specs=[pl.BlockSpec((tq, xw), lambda i: (i, 0)), pl.BlockSpec((nm, 2 * xw), lambda i: (0, 0))],
        out_specs=pl.BlockSpec((tq, xw), lambda i: (i, 0)),
        out_shape=jax.ShapeDtypeStruct((t, xw), BF16),
        compiler_params=_cp(("parallel",)), name=name,
    )(q, mkv)


def _xa_bwd(q, mkv, do, dp, dp_col, *, name, tq=512):
    t, xw = q.shape
    nm = mkv.shape[0]
    hd, nh = XA_HEAD_DIM, XA_HEADS
    tq = _blk(t, tq)
    assert dp_col % xw == 0

    def body(q_ref, mkv_ref, do_ref, _, dq_ref, dmkv_ref):
        i = pl.program_id(0)
        dqs, dks, dvs = [], [], []
        for h in range(nh):
            qh = q_ref[:, h * hd:(h + 1) * hd]
            kh = mkv_ref[:, h * hd:(h + 1) * hd]
            vh = mkv_ref[:, xw + h * hd:xw + (h + 1) * hd]
            doh = do_ref[:, h * hd:(h + 1) * hd]
            s = _dot(qh, kh, NT) * (hd ** -0.5)
            p = jnp.exp(s - jnp.max(s, axis=-1, keepdims=True))
            p = p * (1.0 / jnp.sum(p, axis=-1, keepdims=True))
            dp = _dot(doh, vh, NT)
            ds = (p * (dp - jnp.sum(p * dp, axis=-1, keepdims=True)) * (hd ** -0.5)).astype(BF16)
            dqs.append(_dot(ds, kh))
            dks.append(_dot(ds, qh, TN))
            dvs.append(_dot(p.astype(BF16), doh, TN))
        dq_ref[...] = jnp.concatenate(dqs, axis=1).astype(dq_ref.dtype)
        part = jnp.concatenate(dks + dvs, axis=1)

        @pl.when(i == 0)
        def _():
            dmkv_ref[...] = part

        @pl.when(i > 0)
        def _():
            dmkv_ref[...] += part

    row = pl.BlockSpec((tq, xw), lambda i: (i, 0))
    full = pl.BlockSpec((nm, 2 * xw), lambda i: (0, 0))
    return pl.pallas_call(
        body, grid=(t // tq,),
        in_specs=[row, full, row, ANY_SPEC],
        out_specs=[pl.BlockSpec((tq, xw), lambda i: (i, dp_col // xw)), full],
        out_shape=[jax.ShapeDtypeStruct(dp.shape, dp.dtype), jax.ShapeDtypeStruct((nm, 2 * xw), F32)],
        input_output_aliases={3: 0}, compiler_params=_cp(("arbitrary",)), name=name,
    )(q, mkv, do, dp)


def _merge_specs(ys, ws, tm):
    y_specs = [pl.BlockSpec((tm, y.shape[1]), lambda i: (i, 0)) for y in ys]
    w_specs = [pl.BlockSpec(w.shape, lambda i: (0, 0, 0)) for w in ws]
    return y_specs, w_specs


def _merge_tiles(ws, tn):
    ns, _, per = ws[0].shape
    tn = _blk(per, tn)
    return tn, [(s, c, s * per + c) for s in range(ns) for c in range(0, per, tn)]


def _merge_fwd(ys, ws, gates, *, name, tm=256, tn=512):
    t, d = ys[0].shape[0], ws[0].shape[0] * ws[0].shape[2]
    tm = _blk(t, tm)
    tn, tiles = _merge_tiles(ws, tn)
    y_specs, w_specs = _merge_specs(ys, ws, tm)

    def body(ya, yb, yc, wa, wb, wc, g_ref, o_ref):
        for s, c, col in tiles:
            acc = None
            for b, (y, w) in enumerate(((ya, wa), (yb, wb), (yc, wc))):
                term = _sigmoid(g_ref[:, b * d + col:b * d + col + tn]) * _dot(y[...], w[s, :, c:c + tn])
                acc = term if acc is None else acc + term
            o_ref[:, col:col + tn] = acc.astype(o_ref.dtype)

    return pl.pallas_call(
        body, grid=(t // tm,),
        in_specs=y_specs + w_specs + [pl.BlockSpec((tm, 3 * d), lambda i: (i, 0))],
        out_specs=pl.BlockSpec((tm, d), lambda i: (i, 0)),
        out_shape=jax.ShapeDtypeStruct((t, d), BF16),
        compiler_params=_cp(("parallel",)), name=name,
    )(*ys, *ws, gates)


def _merge_bwd(ys, ws, gates, dmerged, dp_width, *, name, tm=256, tn=512):
    t, d = ys[0].shape[0], ws[0].shape[0] * ws[0].shape[2]
    tm = _blk(t, tm)
    tn, tiles = _merge_tiles(ws, tn)
    y_specs, w_specs = _merge_specs(ys, ws, tm)
    row = pl.BlockSpec((tm, d), lambda i: (i, 0))
    wide = pl.BlockSpec((tm, 3 * d), lambda i: (i, 0))

    def body(ya, yb, yc, wa, wb, wc, g_ref, dm_ref, dua, dub, duc, dp_ref):
        for s, c, col in tiles:
            dm = dm_ref[:, col:col + tn]
            for b, (y, w, du) in enumerate(((ya, wa, dua), (yb, wb, dub), (yc, wc, duc))):
                sg = _sigmoid(g_ref[:, b * d + col:b * d + col + tn])
                u = _dot(y[...], w[s, :, c:c + tn])
                du[:, col:col + tn] = (dm * sg).astype(du.dtype)
                dp_ref[:, b * d + col:b * d + col + tn] = (dm * u * sg * (1.0 - sg)).astype(dp_ref.dtype)

    return pl.pallas_call(
        body, grid=(t // tm,),
        in_specs=y_specs + w_specs + [wide, row],
        out_specs=[row] * 3 + [wide],
        out_shape=[jax.ShapeDtypeStruct((t, d), BF16)] * 3 + [jax.ShapeDtypeStruct((t, dp_width), BF16)],
        compiler_params=_cp(("parallel",)), name=name,
    )(*ys, *ws, gates, dmerged)


def _adamw(w, g, m, v, *, name, tm=256):
    lead = w.ndim - 2
    assert all(s == 1 for s in w.shape[:lead]) and m.shape == w.shape and v.shape == w.shape
    r, c = w.shape[lead:]
    assert g.shape == (r, c)
    tm = _blk(r, tm) if r % 8 == 0 else r
    tc = c if tm * c * 4 <= (4 << 20) else _blk(c, 256)
    ncb = c // tc
    bc1 = 1.0 - ADAM_B1 ** ADAM_STEP
    bc2 = 1.0 - ADAM_B2 ** ADAM_STEP

    def body(w_ref, g_ref, m_ref, v_ref, go_ref, d_ref, nm_ref, nv_ref):
        gv = g_ref[...]
        go_ref[...] = gv
        nm = ADAM_B1 * m_ref[...] + (1.0 - ADAM_B1) * gv
        nv = ADAM_B2 * v_ref[...] + (1.0 - ADAM_B2) * (gv * gv)
        d_ref[...] = -ADAM_LR * ((nm / bc1) / (jnp.sqrt(nv / bc2) + ADAM_EPS) + ADAM_WD * w_ref[...])
        nm_ref[...] = nm
        nv_ref[...] = nv

    spec = pl.BlockSpec((None,) * lead + (tm, tc), lambda i: (0,) * lead + (i // ncb, i % ncb))
    g_spec = pl.BlockSpec((tm, tc), lambda i: (i // ncb, i % ncb))
    return pl.pallas_call(
        body, grid=(r // tm * ncb,), in_specs=[spec, g_spec, spec, spec], out_specs=[spec] * 4,
        out_shape=[jax.ShapeDtypeStruct(w.shape, F32)] * 4,
        compiler_params=_cp(("parallel",)), name=name,
    )(w, g, m, v)


HALO = 8


def _shift_down(cur, prev, j):
    if j == 0:
        return cur
    y = pltpu.roll(cur, j, 0)
    row = lax.broadcasted_iota(jnp.int32, (HALO, cur.shape[1]), 0)
    top = jnp.where(row < j, pltpu.roll(prev, j, 0), y[:HALO])
    return jnp.concatenate([top, y[HALO:]], axis=0)


def _shift_up(cur, nxt, j):
    if j == 0:
        return cur
    tm = cur.shape[0]
    y = pltpu.roll(cur, tm - j, 0)
    row = lax.broadcasted_iota(jnp.int32, (HALO, cur.shape[1]), 0)
    bot = jnp.where(row >= HALO - j, pltpu.roll(nxt, HALO - j, 0), y[tm - HALO:])
    return jnp.concatenate([y[:tm - HALO], bot], axis=0)


def _softplus(x):
    return jnp.maximum(x, 0.0) + jnp.log(1.0 + jnp.exp(-jnp.abs(x)))


def _gdn_pre_fwd(qkvb, conv_w, ab, alog_pad, dt_pad, *, name, ab_blk=0, tm=256):
    t, cw = qkvb.shape
    hd, nh, ck = GDN_HEAD_DIM, GDN_HEADS, GDN_CHUNK
    gw = nh * hd
    tm = _blk(t, tm)
    hb = tm // HALO

    def body(x_ref, xp_ref, w_ref, ab_ref, al_ref, dt_ref, xc_ref, qkvn_ref, aux_ref):
        i = pl.program_id(0)
        cur = x_ref[...]
        prev = jnp.where(i > 0, xp_ref[...], 0.0)
        xc = None
        for tap in range(GDN_CONV):
            term = w_ref[tap:tap + 1, :] * _shift_down(cur, prev, GDN_CONV - 1 - tap)
            xc = term if xc is None else xc + term
        xc_ref[...] = xc
        s = xc * _sigmoid(xc)
        for h in range(2 * nh):
            xh = s[:, h * hd:(h + 1) * hd]
            r = lax.rsqrt(jnp.sum(xh * xh, axis=-1, keepdims=True) + L2_EPS)
            scale = hd ** -0.5 if h < nh else 1.0
            qkvn_ref[:, h * hd:(h + 1) * hd] = xh * (r * scale)
        qkvn_ref[:, 2 * gw:] = s[:, 2 * gw:]
        abv = ab_ref[...]
        lane = lax.broadcasted_iota(jnp.int32, abv.shape, 1)
        g = jnp.where(lane < nh, -jnp.exp(al_ref[...]) * _softplus(abv + dt_ref[...]), 0.0)
        beta = jnp.where((lane >= nh) & (lane < 2 * nh), _sigmoid(abv), 0.0)
        ii = lax.broadcasted_iota(jnp.int32, (tm, tm), 0)
        jj = lax.broadcasted_iota(jnp.int32, (tm, tm), 1)
        tri = jnp.where((ii >= jj) & ((ii ^ jj) < ck), 1.0, 0.0)
        gcum = _dot(tri, g, precision=HI)
        aux_ref[...] = g + beta + pltpu.roll(gcum, 2 * nh, 1)

    row = lambda c: pl.BlockSpec((tm, c), lambda i: (i, 0))
    vec = lambda r, c: pl.BlockSpec((r, c), lambda i: (0, 0))
    return pl.pallas_call(
        body, grid=(t // tm,),
        in_specs=[row(cw), pl.BlockSpec((HALO, cw), lambda i: (jnp.maximum(i * hb - 1, 0), 0)), vec(GDN_CONV, cw),
                  pl.BlockSpec((tm, LANES), lambda i: (i, ab_blk)), vec(1, LANES), vec(1, LANES)],
        out_specs=[row(cw), row(cw), row(LANES)],
        out_shape=[jax.ShapeDtypeStruct((t, cw), F32), jax.ShapeDtypeStruct((t, cw), F32),
                   jax.ShapeDtypeStruct((t, LANES), F32)],
        compiler_params=_cp(("parallel",)), name=name,
    )(qkvb, qkvb, conv_w, ab, alog_pad, dt_pad)


GDN_STEP_CHUNKS = 4
GDN_ILP_CHUNKS = 4
GDN_ILP_CHUNKS_BWD = 4


def _bdot(a, b, dims=NN):
    return _dot(a.astype(BF16), b.astype(BF16), dims)


def _split_bf16(x):
    hi = x.astype(BF16)
    return hi, (x - hi.astype(F32)).astype(BF16)


def _dot3(a, b, dims=NN):
    ah, al = _split_bf16(a)
    bh, bl = _split_bf16(b)
    return _dot(ah, bh, dims) + (_dot(ah, bl, dims) + _dot(al, bh, dims))


def _dot3_many(lhs, rhs, dims=NN):
    sa = [_split_bf16(a) for a in lhs]
    sb = [_split_bf16(b) for b in rhs]
    hh = [_dot(a[0], b[0], dims) for a, b in zip(sa, sb)]
    hl = [_dot(a[0], b[1], dims) for a, b in zip(sa, sb)]
    lh = [_dot(a[1], b[0], dims) for a, b in zip(sa, sb)]
    return [x + (y + z) for x, y, z in zip(hh, hl, lh)]


def _gdn_local(chains, with_inverse):
    ck = GDN_CHUNK
    ii = lax.broadcasted_iota(jnp.int32, (ck, ck), 0)
    jj = lax.broadcasted_iota(jnp.int32, (ck, ck), 1)
    lower, strict = ii >= jj, ii > jj
    dmat = [jnp.exp(jnp.where(lower, gc - gc_row, -jnp.inf)) for _, _, _, gc, gc_row in chains]
    kk = [_bdot(k, k, NT) for _, k, _, _, _ in chains]
    qk = [_bdot(q, k, NT) for q, k, _, _, _ in chains]
    tinv = [None] * len(chains)
    if with_inverse:
        lmat = [jnp.where(strict, c[2] * kk_i * d_i, 0.0) for c, kk_i, d_i in zip(chains, kk, dmat)]
        eye = jnp.where(ii == jj, 1.0, 0.0)
        tinv = [eye - l_i for l_i in lmat]
        pw = lmat
        for _ in range(int(math.log2(ck)) - 1):
            pw = _dot3_many(pw, pw)
            tinv = [t_i + d_i for t_i, d_i in zip(tinv, _dot3_many(tinv, pw))]
    out = []
    for (q, k, b, gc, gc_row), dmat_i, kk_i, qk_i, tinv_i in zip(chains, dmat, kk, qk, tinv):
        gl = gc[ck - 1:ck, :]
        out.append(dict(lower=lower, strict=strict, dmat=dmat_i, kk=kk_i, tinv=tinv_i, gam=jnp.exp(gc), qk=qk_i,
                        mm=qk_i * dmat_i, kdec=jnp.exp(gl - gc)))
    return out


def _gdn_head_cols(h):
    return slice(h * GDN_HEAD_DIM, (h + 1) * GDN_HEAD_DIM)


def _gdn_chunk_inputs(x_ref, aux_ref, auxt_ref, g, h):
    nh, ck = GDN_HEADS, GDN_CHUNK
    gw = nh * GDN_HEAD_DIM
    rows = slice(g * ck, (g + 1) * ck)
    cols = _gdn_head_cols(h)
    q = x_ref[rows, cols]
    k = x_ref[rows, gw + cols.start:gw + cols.stop]
    v = x_ref[rows, 2 * gw + cols.start:2 * gw + cols.stop]
    b = aux_ref[rows, nh + h:nh + h + 1]
    gc = aux_ref[rows, 2 * nh + h:2 * nh + h + 1]
    gc_row = auxt_ref[g, 2 * nh + h:2 * nh + h + 1, :]
    return q, k, v, b, gc, gc_row


def _gdn_specs(t, widths, *, reverse=False, step_chunks=None):
    rows = (step_chunks or GDN_STEP_CHUNKS) * GDN_CHUNK
    nsteps = t // rows
    idx = (lambda i: (nsteps - 1 - i, 0)) if reverse else (lambda i: (i, 0))
    return [pl.BlockSpec((rows, w), idx) for w in widths]


def _gdn_local_fwd(qkvn, aux, aux_t, *, name, exchange=None):
    t = qkvn.shape[0]
    hd, nh, ck, gs = GDN_HEAD_DIM, GDN_HEADS, GDN_CHUNK, GDN_STEP_CHUNKS
    gw = nh * hd
    host = _ExchangeHost(exchange)
    assert not (exchange and exchange.aliases)
    grid = (t // (gs * ck),)

    def body(*refs):
        x_ref, aux_ref, auxt_ref = refs[:3]
        u_ref, w_ref, qd_ref, kd_ref, mm_ref, tinv_ref = refs[3 + host.n_in:9 + host.n_in]
        host.start(refs, 3, 9 + host.n_in, grid)
        for g0 in range(0, gs, GDN_ILP_CHUNKS):
            where = [(g, h) for g in range(g0, g0 + GDN_ILP_CHUNKS) for h in range(nh)]
            ins = [_gdn_chunk_inputs(x_ref, aux_ref, auxt_ref, g, h) for g, h in where]
            lcs = _gdn_local([(q, k, b, gc, gc_row) for q, k, _, b, gc, gc_row in ins], True)
            tinvs = [lc["tinv"] for lc in lcs]
            us = _dot3_many(tinvs, [b * v for _, _, v, b, _, _ in ins])
            ws = _dot3_many(tinvs, [(b * lc["gam"]) * k for (_, k, _, b, _, _), lc in zip(ins, lcs)])
            for i, ((g, h), (q, k, _, _, _, _), lc) in enumerate(zip(where, ins, lcs)):
                rows, cols = slice(g * ck, (g + 1) * ck), _gdn_head_cols(h)
                u_ref[rows, cols] = us[i]
                w_ref[rows, cols] = ws[i].astype(w_ref.dtype)
                qd_ref[rows, cols] = (lc["gam"] * q).astype(qd_ref.dtype)
                kd_ref[rows, cols] = (lc["kdec"] * k).astype(kd_ref.dtype)
            for g in range(g0, g0 + GDN_ILP_CHUNKS):
                rows = slice(g * ck, (g + 1) * ck)
                mine = [lc for (gg, _), lc in zip(where, lcs) if gg == g]
                mm_ref[rows, :] = jnp.concatenate([lc["mm"] for lc in mine], axis=1).astype(mm_ref.dtype)
                tinv_ref[rows, :] = jnp.concatenate([lc["tinv"] for lc in mine], axis=1)
        host.wait(refs, 3, 9 + host.n_in, grid)

    sq = nh * ck
    outs = pl.pallas_call(
        body, grid=grid,
        in_specs=_gdn_specs(t, (3 * gw, LANES)) + [pl.BlockSpec((gs, 16, ck), lambda i: (i, 0, 0))] + host.in_specs,
        out_specs=_gdn_specs(t, (gw, gw, gw, gw, sq, sq)) + host.out_specs,
        out_shape=[jax.ShapeDtypeStruct((t, gw), F32)] + [jax.ShapeDtypeStruct((t, gw), BF16)] * 3
        + [jax.ShapeDtypeStruct((t, sq), BF16), jax.ShapeDtypeStruct((t, sq), F32)] + host.out_shapes,
        scratch_shapes=host.scratch,
        compiler_params=_cp(host.semantics(("parallel",))), name=name,
    )(qkvn, aux, aux_t, *host.ins)
    return (*outs[:6], list(outs[6:])) if exchange is not None else outs


def _gdn_seq_fwd(u, w, qd, kd, mm, aux, *, name):
    t = u.shape[0]
    hd, nh, ck, gs = GDN_HEAD_DIM, GDN_HEADS, GDN_CHUNK, GDN_STEP_CHUNKS
    gw = nh * hd
    sq = nh * ck

    def body(u_ref, w_ref, qd_ref, kd_ref, mm_ref, aux_ref, o_ref, vn_ref, sall_ref, s_ref):
        @pl.when(pl.program_id(0) == 0)
        def _():
            s_ref[...] = jnp.zeros_like(s_ref)

        heads = range(nh)
        hcols = [_gdn_head_cols(h) for h in heads]
        sts = [s_ref[h] for h in heads]
        for g in range(gs):
            rows = slice(g * ck, (g + 1) * ck)
            last = (g + 1) * ck - 1
            for h in heads:
                sall_ref[g, h] = sts[h]
            stbs = [st.astype(BF16) for st in sts]
            w_s = [_dot(w_ref[rows, c], stb) for c, stb in zip(hcols, stbs)]
            q_s = [_dot(qd_ref[rows, c], stb) for c, stb in zip(hcols, stbs)]
            vnbs = [(u_ref[rows, c] - ws).astype(BF16) for c, ws in zip(hcols, w_s)]
            m_v = [_dot(mm_ref[rows, h * ck:(h + 1) * ck], vnbs[h]) for h in heads]
            k_v = [_dot(kd_ref[rows, c], vnb, TN) for c, vnb in zip(hcols, vnbs)]
            for h, c in zip(heads, hcols):
                vn_ref[rows, c] = vnbs[h]
                o_ref[rows, c] = q_s[h] + m_v[h]
            gam_c = [jnp.exp(aux_ref[last:last + 1, 2 * nh + h:2 * nh + h + 1]) for h in heads]
            sts = [gam_c[h] * sts[h] + k_v[h] for h in heads]
        for h in heads:
            s_ref[h] = sts[h]

    return pl.pallas_call(
        body, grid=(t // (gs * ck),),
        in_specs=_gdn_specs(t, (gw, gw, gw, gw, sq, LANES)),
        out_specs=_gdn_specs(t, (gw, gw)) + [pl.BlockSpec((gs, nh, hd, hd), lambda i: (i, 0, 0, 0))],
        out_shape=[jax.ShapeDtypeStruct((t, gw), F32), jax.ShapeDtypeStruct((t, gw), BF16),
                   jax.ShapeDtypeStruct((t // ck, nh, hd, hd), F32)],
        scratch_shapes=[pltpu.VMEM((nh, hd, hd), F32)],
        compiler_params=_cp(("arbitrary",)), name=name,
    )(u, w, qd, kd, mm, aux)


def _gdn_seq_bwd(do, w, qd, kd, mm, vn, s_all, aux, *, name):
    t = do.shape[0]
    hd, nh, ck, gs = GDN_HEAD_DIM, GDN_HEADS, GDN_CHUNK, GDN_STEP_CHUNKS
    gw = nh * hd
    sq = nh * ck
    nsteps = t // (gs * ck)

    def body(do_ref, w_ref, qd_ref, kd_ref, mm_ref, vn_ref, sall_ref, aux_ref, dvn_ref, dqd_ref, dkd_ref, dw_ref,
             dlast_ref, ds_ref):
        @pl.when(pl.program_id(0) == 0)
        def _():
            ds_ref[...] = jnp.zeros_like(ds_ref)

        lane = lax.broadcasted_iota(jnp.int32, (ck, LANES), 1)
        rowi = lax.broadcasted_iota(jnp.int32, (ck, LANES), 0)
        heads = range(nh)
        hcols = [_gdn_head_cols(h) for h in heads]
        dsns = [ds_ref[h] for h in heads]
        for g in reversed(range(gs)):
            rows = slice(g * ck, (g + 1) * ck)
            last = (g + 1) * ck - 1
            sts = [sall_ref[g, h] for h in heads]
            stbs = [st.astype(BF16) for st in sts]
            dsbs = [dsn.astype(BF16) for dsn in dsns]
            dobs = [do_ref[rows, c].astype(BF16) for c in hcols]
            dvns = [_dot(mm_ref[rows, h * ck:(h + 1) * ck], dobs[h], TN) + _dot(kd_ref[rows, hcols[h]], dsbs[h])
                    for h in heads]
            dqds = [_dot(dob, stb, NT) for dob, stb in zip(dobs, stbs)]
            dkds = [_dot(vn_ref[rows, c], dsb, NT) for c, dsb in zip(hcols, dsbs)]
            q_o = [_dot(qd_ref[rows, c], dob, TN) for c, dob in zip(hcols, dobs)]
            dvbs = [dvn.astype(BF16) for dvn in dvns]
            dws = [_dot(dvb, stb, NT) for dvb, stb in zip(dvbs, stbs)]
            w_v = [_dot(w_ref[rows, c], dvb, TN) for c, dvb in zip(hcols, dvbs)]
            gam_c = [jnp.exp(aux_ref[last:last + 1, 2 * nh + h:2 * nh + h + 1]) for h in heads]
            dlast = jnp.zeros((ck, LANES), F32)
            for h, c in zip(heads, hcols):
                dvn_ref[rows, c] = dvns[h]
                dqd_ref[rows, c] = dqds[h]
                dkd_ref[rows, c] = dkds[h]
                dw_ref[rows, c] = -dws[h]
                dgam_c = jnp.sum(jnp.sum(dsns[h] * sts[h], axis=1, keepdims=True), axis=0, keepdims=True)
                dlast = dlast + jnp.where((rowi == ck - 1) & (lane == h), gam_c[h] * dgam_c, 0.0)
            dlast_ref[rows, :] = dlast
            dsns = [q_o[h] + gam_c[h] * dsns[h] - w_v[h] for h in heads]
        for h in heads:
            ds_ref[h] = dsns[h]

    return pl.pallas_call(
        body, grid=(nsteps,),
        in_specs=_gdn_specs(t, (gw, gw, gw, gw, sq, gw), reverse=True)
        + [pl.BlockSpec((gs, nh, hd, hd), lambda i: (nsteps - 1 - i, 0, 0, 0))] + _gdn_specs(t, (LANES,), reverse=True),
        out_specs=_gdn_specs(t, (gw, gw, gw, gw, LANES), reverse=True),
        out_shape=[jax.ShapeDtypeStruct((t, gw), F32)] * 4 + [jax.ShapeDtypeStruct((t, LANES), F32)],
        scratch_shapes=[pltpu.VMEM((nh, hd, hd), F32)],
        compiler_params=_cp(("arbitrary",)), name=name,
    )(do, w, qd, kd, mm, vn, s_all, aux)


def _gdn_local_bwd(qkvn, aux, aux_t, tinv, u, w, vn, do, dvn, dqd, dkd, dw, dlast, *, name):
    t = qkvn.shape[0]
    hd, nh, ck, gs = GDN_HEAD_DIM, GDN_HEADS, GDN_CHUNK, GDN_STEP_CHUNKS
    gw = nh * hd
    sq = nh * ck

    def body(x_ref, aux_ref, auxt_ref, tinv_ref, u_ref, w_ref, vn_ref, do_ref, dvn_ref, dqd_ref, dkd_ref, dw_ref,
             dlast_ref, dx_ref, daux_ref):
        lane = lax.broadcasted_iota(jnp.int32, (ck, LANES), 1)
        ones = jnp.ones((ck, LANES), F32)
        ii = lax.broadcasted_iota(jnp.int32, (ck, ck), 0)
        jj = lax.broadcasted_iota(jnp.int32, (ck, ck), 1)
        suffix = jnp.where(jj >= ii, 1.0, 0.0)
        for g0 in range(0, gs, GDN_ILP_CHUNKS_BWD):
            where = [(g, h) for g in range(g0, g0 + GDN_ILP_CHUNKS_BWD) for h in range(nh)]
            at = [(slice(g * ck, (g + 1) * ck), _gdn_head_cols(h)) for g, h in where]
            ins = [_gdn_chunk_inputs(x_ref, aux_ref, auxt_ref, g, h) for g, h in where]
            lcs = _gdn_local([(q, k, b, gc, gc_row) for q, k, _, b, gc, gc_row in ins], False)
            tinvs = [tinv_ref[slice(g * ck, (g + 1) * ck), h * ck:(h + 1) * ck] for g, h in where]
            dms = [jnp.where(lc["lower"], _bdot(do_ref[r, c], vn_ref[r, c], NT), 0.0) for lc, (r, c) in zip(lcs, at)]
            drvs = _dot3_many(tinvs, [dvn_ref[r, c] for r, c in at], TN)
            drks = _dot3_many(tinvs, [dw_ref[r, c] for r, c in at], TN)
            das = [jnp.where(lc["strict"], -(_bdot(drv, u_ref[r, c], NT) + _bdot(drk, w_ref[r, c], NT)), 0.0)
                   for lc, (r, c), drv, drk in zip(lcs, at, drvs, drks)]
            f_mats = [da * (i[3] * lc["kk"]) * lc["dmat"] + dm * lc["qk"] * lc["dmat"]
                      for i, lc, da, dm in zip(ins, lcs, das, dms)]
            col_sums = _dot3_many(f_mats, [ones] * len(where), TN)
            dgc_all = {g: dlast_ref[slice(g * ck, (g + 1) * ck), :] for g in range(g0, g0 + GDN_ILP_CHUNKS_BWD)}
            db_all = {g: jnp.zeros((ck, LANES), F32) for g in range(g0, g0 + GDN_ILP_CHUNKS_BWD)}
            e_mats = [da * lc["dmat"] * i[3] for i, lc, da in zip(ins, lcs, das)]
            dmds = [dm * lc["dmat"] for lc, dm in zip(lcs, dms)]
            dq_mm = [_bdot(dmd, i[1]) for i, dmd in zip(ins, dmds)]
            dk_mm = [_bdot(e, i[1]) + _bdot(e, i[1], TN) + _bdot(dmd, i[0], TN) for i, e, dmd in zip(ins, e_mats, dmds)]
            for n, ((g, h), (q, k, v, b, _, _), lc, (rows, cols)) in enumerate(zip(where, ins, lcs, at)):
                dmat, kk, gam, kdec = (lc[key] for key in ("dmat", "kk", "gam", "kdec"))
                drv, drk, da = drvs[n], drks[n], das[n]
                dqd_h, dkd_h = dqd_ref[rows, cols], dkd_ref[rows, cols]
                rs_rk = jnp.sum(drk * k, axis=-1, keepdims=True)
                db = (jnp.sum(drv * v, axis=-1, keepdims=True) + gam * rs_rk
                      + jnp.sum(da * kk * dmat, axis=-1, keepdims=True))
                dx_ref[rows, cols] = dq_mm[n] + gam * dqd_h
                dx_ref[rows, gw + cols.start:gw + cols.stop] = (b * gam) * drk + dk_mm[n] + kdec * dkd_h
                dx_ref[rows, 2 * gw + cols.start:2 * gw + cols.stop] = b * drv
                e_vec = jnp.sum(dkd_h * (kdec * k), axis=-1, keepdims=True)
                dgc = (b * gam * rs_rk + gam * jnp.sum(dqd_h * q, axis=-1, keepdims=True)
                       + jnp.sum(f_mats[n], axis=-1, keepdims=True) - col_sums[n][:, 0:1] - e_vec)
                is_last = lax.broadcasted_iota(jnp.int32, (ck, 1), 0) == ck - 1
                dgc = dgc + jnp.where(is_last, jnp.sum(e_vec, axis=0, keepdims=True), 0.0)
                dgc_all[g] = dgc_all[g] + jnp.where(lane == h, dgc, 0.0)
                db_all[g] = db_all[g] + jnp.where(lane == nh + h, db, 0.0)
            for g in dgc_all:
                daux_ref[slice(g * ck, (g + 1) * ck), :] = _dot3(suffix, dgc_all[g]) + db_all[g]

    return pl.pallas_call(
        body, grid=(t // (gs * ck),),
        in_specs=_gdn_specs(t, (3 * gw, LANES)) + [pl.BlockSpec((gs, 16, ck), lambda i: (i, 0, 0))]
        + _gdn_specs(t, (sq, gw, gw, gw, gw, gw, gw, gw, gw, LANES)),
        out_specs=_gdn_specs(t, (3 * gw, LANES)),
        out_shape=[jax.ShapeDtypeStruct((t, 3 * gw), F32), jax.ShapeDtypeStruct((t, LANES), F32)],
        compiler_params=_cp(("parallel",)), name=name,
    )(qkvn, aux, aux_t, tinv, u, w, vn, do, dvn, dqd, dkd, dw, dlast)


def _gdn_pre_bwd1(xc, dqkvn, daux, ab, alog_pad, dt_pad, dkv, dp, dp_col, *, name, ab_blk=0, tm=256):
    t, cw = xc.shape
    hd, nh = GDN_HEAD_DIM, GDN_HEADS
    gw = nh * hd
    tm = _blk(t, tm)

    kvw = dkv.shape[1]
    seg = kvw + AB_PAD
    assert dp_col % seg == 0

    def body(xc_ref, dy_ref, daux_ref, ab_ref, al_ref, dt_ref, dkv_ref, _, dxc_ref, dab_ref, dal_ref, ddt_ref):
        i = pl.program_id(0)
        xc = xc_ref[...]
        sg = _sigmoid(xc)
        s = xc * sg
        dsilu = sg * (1.0 + xc * (1.0 - sg))
        for h in range(2 * nh):
            xh = s[:, h * hd:(h + 1) * hd]
            scale = hd ** -0.5 if h < nh else 1.0
            dyh = dy_ref[:, h * hd:(h + 1) * hd] * scale
            r = lax.rsqrt(jnp.sum(xh * xh, axis=-1, keepdims=True) + L2_EPS)
            dxh = r * dyh - xh * (r * r * r) * jnp.sum(dyh * xh, axis=-1, keepdims=True)
            dxc_ref[:, h * hd:(h + 1) * hd] = dxh * dsilu[:, h * hd:(h + 1) * hd]
        dxc_ref[:, 2 * gw:] = dy_ref[:, 2 * gw:] * dsilu[:, 2 * gw:]
        abv = ab_ref[...]
        dauxv = daux_ref[...]
        lane = lax.broadcasted_iota(jnp.int32, abv.shape, 1)
        is_a = lane < nh
        is_b = (lane >= nh) & (lane < 2 * nh)
        pre = abv + dt_ref[...]
        neg_ea = -jnp.exp(al_ref[...])
        d_a = jnp.where(is_a, dauxv * neg_ea * _sigmoid(pre), 0.0)
        beta = _sigmoid(abv)
        d_b = jnp.where(is_b, dauxv * beta * (1.0 - beta), 0.0)
        dab_ref[:, :kvw] = dkv_ref[...]
        dab_ref[:, kvw:kvw + LANES] = (d_a + d_b).astype(dab_ref.dtype)
        dab_ref[:, kvw + LANES:] = jnp.zeros((tm, AB_PAD - LANES), dab_ref.dtype)
        dal = jnp.sum(jnp.where(is_a, dauxv * neg_ea * _softplus(pre), 0.0), axis=0, keepdims=True)
        ddt = jnp.sum(d_a, axis=0, keepdims=True)

        @pl.when(i == 0)
        def _():
            dal_ref[...] = dal
            ddt_ref[...] = ddt

        @pl.when(i > 0)
        def _():
            dal_ref[...] += dal
            ddt_ref[...] += ddt

    row = lambda c: pl.BlockSpec((tm, c), lambda i: (i, 0))
    vec = pl.BlockSpec((1, LANES), lambda i: (0, 0))
    return pl.pallas_call(
        body, grid=(t // tm,),
        in_specs=[row(cw), row(cw), row(LANES), pl.BlockSpec((tm, LANES), lambda i: (i, ab_blk)), vec, vec, row(kvw),
                  ANY_SPEC],
        out_specs=[row(cw), pl.BlockSpec((tm, seg), lambda i: (i, dp_col // seg)), vec, vec],
        out_shape=[jax.ShapeDtypeStruct((t, cw), F32), jax.ShapeDtypeStruct(dp.shape, dp.dtype),
                   jax.ShapeDtypeStruct((1, LANES), F32), jax.ShapeDtypeStruct((1, LANES), F32)],
        input_output_aliases={7: 1}, compiler_params=_cp(("arbitrary",)), name=name,
    )(xc, dqkvn, daux, ab, alog_pad, dt_pad, dkv, dp)


def _gdn_pre_bwd2(dxc, qkvb, conv_w, dp, dp_col, *, name, tm=512):
    t, cw = dxc.shape
    tm = _blk(t, tm)
    hb = tm // HALO
    nblk = t // tm
    cg = GDN_HEADS * GDN_HEAD_DIM
    assert cw % cg == 0 and dp_col % cg == 0
    col0 = dp_col // cg

    def body(d_ref, dn_ref, x_ref, xp_ref, w_ref, _, dx_ref, dw_ref):
        i = pl.program_id(1)
        dcur = d_ref[...]
        dnxt = jnp.where(i < nblk - 1, dn_ref[...], 0.0)
        cur = x_ref[...]
        prev = jnp.where(i > 0, xp_ref[...], 0.0)
        dx = None
        dws = []
        for tap in range(GDN_CONV):
            j = GDN_CONV - 1 - tap
            term = w_ref[tap:tap + 1, :] * _shift_up(dcur, dnxt, j)
            dx = term if dx is None else dx + term
            dws.append(jnp.sum(dcur * _shift_down(cur, prev, j), axis=0, keepdims=True))
        dx_ref[...] = dx.astype(dx_ref.dtype)
        dw = jnp.concatenate(dws, axis=0)

        @pl.when(i == 0)
        def _():
            dw_ref[...] = dw

        @pl.when(i > 0)
        def _():
            dw_ref[...] += dw

    row = pl.BlockSpec((tm, cg), lambda c, i: (i, c))
    wsp = pl.BlockSpec((GDN_CONV, cg), lambda c, i: (0, c))
    return pl.pallas_call(
        body, grid=(cw // cg, nblk),
        in_specs=[row, pl.BlockSpec((HALO, cg), lambda c, i: (jnp.minimum((i + 1) * hb, t // HALO - 1), c)),
                  row, pl.BlockSpec((HALO, cg), lambda c, i: (jnp.maximum(i * hb - 1, 0), c)), wsp, ANY_SPEC],
        out_specs=[pl.BlockSpec((tm, cg), lambda c, i: (i, col0 + c)), wsp],
        out_shape=[jax.ShapeDtypeStruct(dp.shape, dp.dtype), jax.ShapeDtypeStruct((GDN_CONV, cw), F32)],
        input_output_aliases={5: 0}, compiler_params=_cp(("arbitrary", "arbitrary")), name=name,
    )(dxc, dxc, qkvb, qkvb, conv_w, dp)


def _gdn_post_fwd(o, z, norm_w, *, name, tm=512):
    t, gw = o.shape
    hd, nh = GDN_HEAD_DIM, GDN_HEADS
    tm = _blk(t, tm)

    def body(o_ref, z_ref, w_ref, y_ref):
        zv = z_ref[...]
        sz = zv * _sigmoid(zv)
        for h in range(nh):
            oh = o_ref[:, h * hd:(h + 1) * hd]
            r = lax.rsqrt(jnp.mean(oh * oh, axis=-1, keepdims=True) + RMS_EPS)
            y_ref[:, h * hd:(h + 1) * hd] = (oh * r * w_ref[...] * sz[:, h * hd:(h + 1) * hd]).astype(y_ref.dtype)

    row = pl.BlockSpec((tm, gw), lambda i: (i, 0))
    return pl.pallas_call(
        body, grid=(t // tm,), in_specs=[row, row, pl.BlockSpec((1, hd), lambda i: (0, 0))], out_specs=row,
        out_shape=jax.ShapeDtypeStruct((t, gw), BF16), compiler_params=_cp(("parallel",)), name=name,
    )(o, z, norm_w)


def _gdn_post_bwd(dy, o, z, norm_w, dp, dp_col, *, name, tm=512):
    t, gw = o.shape
    hd, nh = GDN_HEAD_DIM, GDN_HEADS
    tm = _blk(t, tm)

    def body(dy_ref, o_ref, z_ref, w_ref, _, do_ref, dz_ref, dw_ref):
        i = pl.program_id(0)
        zv = z_ref[...]
        sg = _sigmoid(zv)
        sz = zv * sg
        dsz = sg * (1.0 + zv * (1.0 - sg))
        dw = None
        for h in range(nh):
            sl = slice(h * hd, (h + 1) * hd)
            oh = o_ref[:, sl]
            dyh = dy_ref[:, sl].astype(F32)
            r = lax.rsqrt(jnp.mean(oh * oh, axis=-1, keepdims=True) + RMS_EPS)
            xh = oh * r
            dz_ref[:, sl] = (dyh * xh * w_ref[...] * dsz[:, sl]).astype(dz_ref.dtype)
            dn = dyh * sz[:, sl]
            dxh = dn * w_ref[...]
            do_ref[:, sl] = r * (dxh - xh * jnp.mean(dxh * xh, axis=-1, keepdims=True))
            part = jnp.sum(dn * xh, axis=0, keepdims=True)
            dw = part if dw is None else dw + part

        @pl.when(i == 0)
        def _():
            dw_ref[...] = dw

        @pl.when(i > 0)
        def _():
            dw_ref[...] += dw

    row = pl.BlockSpec((tm, gw), lambda i: (i, 0))
    vec = pl.BlockSpec((1, hd), lambda i: (0, 0))
    return pl.pallas_call(
        body, grid=(t // tm,), in_specs=[row, row, row, vec, ANY_SPEC],
        out_specs=[row, pl.BlockSpec((tm, gw), lambda i: (i, dp_col // gw)), vec],
        out_shape=[jax.ShapeDtypeStruct((t, gw), F32), jax.ShapeDtypeStruct(dp.shape, dp.dtype),
                   jax.ShapeDtypeStruct((1, hd), F32)],
        input_output_aliases={4: 1}, compiler_params=_cp(("arbitrary",)), name=name,
    )(dy, o, z, norm_w, dp)


IN_NAMES = ("q_a", "kv_a", "qkv_b", "ab", "z", "q_c", "gates")
CAT_NAMES = ("gates", "q_a", "qkv_b", "z", "q_c", "kv_a", "ab")
AB_PAD = 256


def _in_widths(d):
    gw = GDN_HEADS * GDN_HEAD_DIM
    return dict(q_a=SWA_Q_HEADS * SWA_HEAD_DIM, kv_a=2 * SWA_KV_HEADS * SWA_HEAD_DIM, qkv_b=3 * gw, ab=2 * GDN_HEADS,
                z=gw, q_c=XA_HEADS * XA_HEAD_DIM, gates=3 * d)


def _ranges(names, widths):
    out, start = {}, 0
    for k in names:
        out[k] = (start, widths[k])
        start += widths[k]
    return out, start


def _cat_ranges(d):
    widths = dict(_in_widths(d), ab=AB_PAD)
    return _ranges(CAT_NAMES, widths)


def _to_cat(shards, *, name="to_cat", tm=256):
    ns, d, n = shards.shape
    src, _ = _ranges(IN_NAMES, _in_widths(d))
    _, cat_w = _cat_ranges(d)
    pieces = []
    for k in CAT_NAMES:
        lo, hi = src[k][0], src[k][0] + src[k][1]
        for s in range(ns):
            a, b = max(lo, s * n), min(hi, (s + 1) * n)
            if a < b:
                pieces.append((s, a - s * n, b - s * n))
    tm = _blk(d, tm)

    def body(s_ref, o_ref):
        cols = [s_ref[s, :, a:b] for s, a, b in pieces]
        cols.append(jnp.zeros((tm, AB_PAD - src["ab"][1]), o_ref.dtype))
        o_ref[...] = jnp.concatenate(cols, axis=1)

    return pl.pallas_call(
        body, grid=(d // tm,),
        in_specs=[pl.BlockSpec((ns, tm, n), lambda i: (0, i, 0))],
        out_specs=pl.BlockSpec((tm, cat_w), lambda i: (i, 0)),
        out_shape=jax.ShapeDtypeStruct((d, cat_w), shards.dtype),
        compiler_params=_cp(("parallel",)), name=name,
    )(shards)


def _from_cat(w_cat, *, name="from_cat", tm=256):
    d, cat_w = w_cat.shape
    src, total = _ranges(IN_NAMES, _in_widths(d))
    cat, _ = _cat_ranges(d)
    n = total // N_SHARDS
    pieces = []
    for s in range(N_SHARDS):
        pieces.append([])
        for k in IN_NAMES:
            a, b = max(s * n, src[k][0]), min((s + 1) * n, src[k][0] + src[k][1])
            if a < b:
                pieces[s].append((cat[k][0] + a - src[k][0], cat[k][0] + b - src[k][0]))
    tm = _blk(d, tm)

    def body(c_ref, o_ref):
        for s in range(N_SHARDS):
            o_ref[s] = jnp.concatenate([c_ref[:, a:b] for a, b in pieces[s]], axis=1)

    return pl.pallas_call(
        body, grid=(d // tm,),
        in_specs=[pl.BlockSpec((tm, cat_w), lambda i: (i, 0))],
        out_specs=pl.BlockSpec((N_SHARDS, tm, n), lambda i: (0, i, 0)),
        out_shape=jax.ShapeDtypeStruct((N_SHARDS, d, n), w_cat.dtype),
        compiler_params=_cp(("parallel",)), name=name,
    )(w_cat)


def _pad_cols(a, width):
    return jnp.pad(a, ((0, 0), (0, width - a.shape[1])))


def _relu2_epilogue(acc):
    r = jnp.maximum(acc, 0.0)
    return acc, r * r


def _add_epilogue(acc, res):
    return (acc + res,)


def _drelu2_epilogue(acc, u):
    return (acc * (2.0 * jnp.maximum(u.astype(F32), 0.0)),)


def _local_step(x, mem, tgt, wts, small, comm=None):
    t, d = x.shape
    nh = GDN_HEADS
    cat, cat_w = _cat_ranges(d)
    alog_pad = _pad_cols(small["a_log"], LANES)
    dt_pad = _pad_cols(small["dt_bias"], LANES)
    kvw = cat["kv_a"][1]
    assert cat["ab"][0] == cat["kv_a"][0] + kvw
    ab_blk = kvw // LANES

    if comm is None:
        n = _rms_fwd(x, small["g_mix"], name="rms_mix")
        w_cat = wts["w_cat"]
    else:
        n, landed = _rms_fwd(x, small["g_mix"], name="rms_mix", exchange=comm.gather_exchange(["w_in"]))
        w_cat = _to_cat(_exchange_call(comm.pass_on(["w_in"], landed), name="ag_w_in_pass")[0])
    assert w_cat.shape == (d, cat_w)
    q_a = _mm(n, w_cat, b_window=cat["q_a"], out_dtypes=(BF16,), name="in_q_a")
    kv_a, ab = _mm(n, w_cat, b_window=(cat["kv_a"][0], kvw + AB_PAD), out_dtypes=(BF16, F32), name="in_kv_ab")
    qkvb = _mm(n, w_cat, b_window=cat["qkv_b"], tn=512, name="in_qkv_b")
    z = _mm(n, w_cat, b_window=cat["z"], name="in_z")
    q_c = _mm(n, w_cat, b_window=cat["q_c"], out_dtypes=(BF16,), name="in_q_c")
    if comm is None:
        gates = _mm(n, w_cat, b_window=cat["gates"], name="in_gates")
        y_a, lse = _swa_fwd(q_a, kv_a, small["sinks"], name="swa_fwd")
    else:
        gates, landed_mlp = _mm(n, w_cat, b_window=cat["gates"], name="in_gates",
                                exchange=comm.gather_exchange(comm.MLP[1:]))
        y_a, lse, landed = _swa_fwd(q_a, kv_a, small["sinks"], name="swa_fwd", exchange=comm.gather_exchange(comm.MLP[:1]))
        landed_mlp = landed + landed_mlp
    xc, qkvn, aux = _gdn_pre_fwd(qkvb, small["conv_w"], ab, alog_pad, dt_pad, ab_blk=ab_blk, name="gdn_pre_fwd")
    aux_t = aux[:, :16].reshape(t // GDN_CHUNK, GDN_CHUNK, 16).transpose(0, 2, 1)
    if comm is None:
        gdn_u, gdn_w, gdn_qd, gdn_kd, gdn_mm, gdn_tinv = _gdn_local_fwd(qkvn, aux, aux_t, name="gdn_local_fwd")
    else:
        gdn_u, gdn_w, gdn_qd, gdn_kd, gdn_mm, gdn_tinv, landed_mid = _gdn_local_fwd(
            qkvn, aux, aux_t, name="gdn_local_fwd", exchange=comm.gather_exchange(comm.mid))
        wts = dict(wts, **comm.gathered(comm.mid, landed_mid, "mid"))
    o_b, gdn_vn, s_all = _gdn_seq_fwd(gdn_u, gdn_w, gdn_qd, gdn_kd, gdn_mm, aux, name="gdn_seq_fwd")
    y_b = _gdn_post_fwd(o_b, z, small["gdn_norm_w"], name="gdn_post_fwd")
    nmem = _rms_fwd(mem, small["g_mem"], name="rms_mem")
    mkv = _mm(nmem, wts["w_mem_kv"], out_dtypes=(BF16,), name="mem_kv")
    y_c = _xa_fwd(q_c, mkv, name="xa_fwd")
    ys = (y_a, y_b, y_c)
    w_ups = (wts["w_swa_up"], wts["w_gdn_up"], wts["w_xa_up"])
    merged = _merge_fwd(ys, w_ups, gates, name="merge_fwd")
    if comm is None:
        h1 = _mm(merged, wts["w_out"], extras=(x,), epilogue=_add_epilogue, name="out_proj")
    else:
        h1, whole = _mm(merged, wts["w_out"], extras=(x,), epilogue=_add_epilogue, name="out_proj",
                        exchange=comm.pass_on(comm.MLP, landed_mlp))
        wts = dict(wts, **comm.as_weights(comm.MLP, whole))
    n2 = _rms_fwd(h1, small["g_mlp"], name="rms_mlp")
    u, act = _mm(n2, wts["w_mlp_in"], b_sharded=True, out_dtypes=(BF16, BF16), epilogue=_relu2_epilogue, name="mlp_in")
    h2 = _mm(act, wts["w_mlp_out"], extras=(h1,), epilogue=_add_epilogue, name="mlp_out")
    dh2, dh2_b, dg_final, loss = _final_loss(h2, small["g_final"], tgt, name="final_loss")

    grads = {"g_final": dg_final}
    du = _mm(dh2_b, wts["w_mlp_out"], tb=True, out_dtypes=(BF16,), extras=(u,), epilogue=_drelu2_epilogue, name="d_mlp_act")
    grads["w_mlp_out"] = _mm(act, dh2_b, ta=True, out_dtypes=(BF16,), name="dw_mlp_out")
    grads["w_mlp_in"] = _mm(n2, du, ta=True, out_sharded=True, out_dtypes=(BF16,), name="dw_mlp_in")
    if comm is None:
        dn2 = _mm(du, wts["w_mlp_in"], tb=True, b_sharded=True, name="d_mlp_in")
    else:
        g_mlp = [comm.shard_major(k, grads.pop(k)) for k in comm.MLP]
        dn2, sib_mlp = _mm(du, wts["w_mlp_in"], tb=True, b_sharded=True, name="d_mlp_in", exchange=_sibling_halves(g_mlp))
        s1_mlp = comm.pair_sums(g_mlp, "mlp", sib_mlp)
    dh1, dh1_b, grads["g_mlp"] = _rms_bwd(dn2, h1, small["g_mlp"], dh2, name="rms_mlp_bwd")
    dmerged = _mm(dh1_b, wts["w_out"], tb=True, name="d_out_proj")
    grads["w_out"] = _mm(merged, dh1_b, ta=True, out_dtypes=(BF16,), name="dw_out")
    *dus, dp = _merge_bwd(ys, w_ups, gates, dmerged, cat_w, name="merge_bwd")
    dys = []
    for y, du_i, w_up, key in zip(ys, dus, w_ups, ("w_swa_up", "w_gdn_up", "w_xa_up")):
        dys.append(_mm(du_i, w_up, tb=True, b_sharded=True, out_dtypes=(BF16,), name="d_" + key))
        grads[key] = _mm(y, du_i, ta=True, out_sharded=True, out_dtypes=(BF16,), name="dw_" + key[2:])
    dp, dkv_a, grads["sinks"] = _swa_bwd(q_a, kv_a, small["sinks"], y_a, lse, dys[0], dp, cat["q_a"][0], name="swa_bwd")
    do_b, dp, grads["gdn_norm_w"] = _gdn_post_bwd(dys[1], o_b, z, small["gdn_norm_w"], dp, cat["z"][0],
                                                  name="gdn_post_bwd")
    dvn, dqd, dkd, dw_, dlast = _gdn_seq_bwd(do_b, gdn_w, gdn_qd, gdn_kd, gdn_mm, gdn_vn, s_all, aux, name="gdn_seq_bwd")
    dqkvn, daux = _gdn_local_bwd(qkvn, aux, aux_t, gdn_tinv, gdn_u, gdn_w, gdn_vn, do_b, dvn, dqd, dkd, dw_, dlast,
                                 name="gdn_local_bwd")
    dxc, dp, dalog, ddt = _gdn_pre_bwd1(xc, dqkvn, daux, ab, alog_pad, dt_pad, dkv_a, dp, cat["kv_a"][0], ab_blk=ab_blk,
                                        name="gdn_pre_bwd1")
    grads["a_log"], grads["dt_bias"] = dalog[:, :nh], ddt[:, :nh]
    dp, grads["conv_w"] = _gdn_pre_bwd2(dxc, qkvb, small["conv_w"], dp, cat["qkv_b"][0], name="gdn_pre_bwd2")
    dp, dmkv = _xa_bwd(q_c, mkv, dys[2], dp, cat["q_c"][0], name="xa_bwd")
    grads["w_mem_kv"] = _mm(nmem, dmkv, ta=True, out_dtypes=(BF16,), name="dw_mem_kv")
    dnmem = _mm(dmkv, wts["w_mem_kv"], tb=True, name="d_mem_kv")
    _, _, grads["g_mem"] = _rms_bwd(dnmem, mem, small["g_mem"], jnp.zeros_like(mem), name="rms_mem_bwd")
    if comm is None:
        grads["w_cat"] = _mm(n, dp, ta=True, out_dtypes=(BF16,), name="dw_in")
        dn = _mm(dp, w_cat, tb=True, name="d_in_proj")
    else:
        s1_mid = comm.pair_sums([comm.shard_major(k, grads.pop(k)) for k in comm.mid], "mid")
        dw_cat, rcv_mlp = _mm(n, dp, ta=True, out_dtypes=(BF16,), name="dw_in", exchange=_chip_exchange(s1_mlp))
        s1_in = comm.pair_sums([_from_cat(dw_cat)], "in")
        dn, rcv_rest = _mm(dp, w_cat, tb=True, name="d_in_proj", exchange=_chip_exchange(s1_in + s1_mid))
        halves = comm.chip_sums(s1_in + s1_mid + s1_mlp, rcv_rest + rcv_mlp)
        reduced = _exchange_call(_join_halves(halves), name="rs_join_halves")
        grads.update(zip(["w_in"] + comm.mid + list(comm.MLP), reduced))
    dx, _, grads["g_mix"] = _rms_bwd(dn, x, small["g_mix"], dh1, name="rms_mix_bwd")
    return loss, dx, grads


HBM_SPEC = pl.BlockSpec(memory_space=pltpu.HBM)
VMEM_SPEC = pl.BlockSpec(memory_space=pltpu.VMEM)
N_CHIPS = N_SHARDS
N_DEV = 8
DMA_CHUNK_BYTES = 1 << 20


def _place():
    return lax.axis_index("x"), lax.axis_index("y"), lax.axis_index("c")


def _other_chips(x, y):
    return [(1 - x, y), (x, 1 - y), (1 - x, 1 - y)]


def _n_chunks(rows, row_bytes):
    n = 1
    while rows % (2 * n) == 0 and (rows // (2 * n)) % 16 == 0 and (rows // n) * row_bytes > DMA_CHUNK_BYTES:
        n *= 2
    return n


def _sem_scratch(n_remote, n_local):
    return [pltpu.SemaphoreType.DMA((max(n_remote, 1),)), pltpu.SemaphoreType.DMA((max(n_remote, 1),)),
            pltpu.SemaphoreType.DMA((max(n_local, 1),))]


def _gather_over_ici(shards):
    plan = _half_chunks(shards, 0)

    def copies_of(in_refs, out_refs, place):
        x, y, c = place
        remote = []
        for i, r0, nr in plan:
            mine = pl.ds(c * (shards[i].shape[0] // 2) + r0, nr)
            for chip in _other_chips(x, y):
                remote.append((in_refs[i].at[mine], out_refs[i].at[2 * x + y, mine], (*chip, c)))
        return remote, []

    shapes = tuple(jax.ShapeDtypeStruct((N_CHIPS, *s.shape), s.dtype) for s in shards)
    return Exchange(tuple(shards), shapes, 3 * len(plan), 0, copies_of)


def _gather_pass_on(arrived, shards):
    plan = _half_chunks(shards, 0)
    n = len(arrived)

    def copies_of(in_refs, out_refs, place):
        x, y, c = place
        remote, local = [], []
        for i, r0, nr in plan:
            rh = shards[i].shape[0] // 2
            mine = pl.ds(c * rh + r0, nr)
            for chip in _other_chips(x, y):
                rows = out_refs[i].at[2 * chip[0] + chip[1], mine]
                remote.append((rows, rows, (x, y, 1 - c)))
            for half in range(2):
                rows = pl.ds(half * rh + r0, nr)
                local.append((in_refs[n + i].at[rows], out_refs[i].at[2 * x + y, rows]))
        return remote, local

    shapes = tuple(jax.ShapeDtypeStruct(a.shape, a.dtype) for a in arrived)
    return Exchange(tuple(arrived) + tuple(shards), shapes, 3 * len(plan), 2 * len(plan), copies_of,
                    tuple((i, i) for i in range(n)))


def _exchange_call(ex, *, name):
    n_in, n_out = len(ex.ins), len(ex.out_shapes)

    def body(*refs):
        cps = _exchange_copies(ex, refs[:n_in], refs[n_in:n_in + n_out], refs[n_in + n_out:])
        for cp in cps:
            cp.start()
        for cp in cps:
            cp.wait()

    return pl.pallas_call(
        body, out_shape=list(ex.out_shapes), in_specs=[HBM_SPEC] * n_in, out_specs=[HBM_SPEC] * n_out,
        scratch_shapes=_sem_scratch(ex.n_remote, ex.n_local), input_output_aliases=dict(ex.aliases), name=name,
    )(*ex.ins)


def _half_chunks(arrs, row_axis):
    plan = []
    for i, a in enumerate(arrs):
        rh = a.shape[row_axis] // 2
        row_bytes = a.dtype.itemsize * math.prod(a.shape) // a.shape[row_axis]
        nch = _n_chunks(rh, row_bytes)
        plan += [(i, q * (rh // nch), rh // nch) for q in range(nch)]
    return plan


def _sibling_halves(gs):
    plan = _half_chunks(gs, 1)

    def copies_of(in_refs, out_refs, place):
        x, y, c = place
        out = []
        for i, r0, nr in plan:
            rh = gs[i].shape[1] // 2
            out.append((in_refs[i].at[:, pl.ds((1 - c) * rh + r0, nr), :], out_refs[i].at[:, pl.ds(r0, nr), :],
                        (x, y, 1 - c)))
        return out, []

    shapes = tuple(jax.ShapeDtypeStruct((g.shape[0], g.shape[1] // 2, g.shape[2]), g.dtype) for g in gs)
    return Exchange(tuple(gs), shapes, len(plan), 0, copies_of)


def _chip_exchange(s1s):
    plan = _half_chunks([jax.ShapeDtypeStruct((2 * s.shape[1], s.shape[2]), s.dtype) for s in s1s], 0)

    def copies_of(in_refs, out_refs, place):
        x, y, c = place
        out = []
        for i, r0, nr in plan:
            for j, chip in enumerate(_other_chips(x, y)):
                out.append((in_refs[i].at[2 * chip[0] + chip[1], pl.ds(r0, nr), :], out_refs[i].at[j, pl.ds(r0, nr), :],
                            (*chip, c)))
        return out, []

    shapes = tuple(jax.ShapeDtypeStruct((3, *s.shape[1:]), s.dtype) for s in s1s)
    return Exchange(tuple(s1s), shapes, 3 * len(plan), 0, copies_of)


def _join_halves(gs):
    plan = _half_chunks(gs, 0)

    def copies_of(in_refs, out_refs, place):
        x, y, c = place
        out = []
        for i, r0, nr in plan:
            rows = out_refs[i].at[pl.ds(c * (gs[i].shape[0] // 2) + r0, nr), :]
            out.append((rows, rows, (x, y, 1 - c)))
        return out, []

    shapes = tuple(jax.ShapeDtypeStruct(g.shape, g.dtype) for g in gs)
    aliases = tuple((i, i) for i in range(len(gs)))
    return Exchange(tuple(gs), shapes, len(plan), 0, copies_of, aliases)


def _row_block(rows, cols):
    tb = rows
    while tb % 32 == 0 and tb * cols * 4 > (2 << 20):
        tb //= 2
    return tb


def _pair_sum(g, sib, core, *, name):
    ns, r, c = g.shape
    rh = r // 2
    tb = _row_block(rh, c)
    nb = rh // tb

    def body(core_ref, g_ref, s_ref, o_ref):
        o_ref[...] = (g_ref[...].astype(F32) + s_ref[...].astype(F32)).astype(o_ref.dtype)

    mine = pl.BlockSpec((None, tb, c), lambda s, i, core_ref: (s, core_ref[0] * nb + i, 0))
    half = pl.BlockSpec((None, tb, c), lambda s, i, core_ref: (s, i, 0))
    return pl.pallas_call(
        body, grid_spec=pltpu.PrefetchScalarGridSpec(num_scalar_prefetch=1, grid=(ns, nb), in_specs=[mine, half],
                                                     out_specs=half),
        out_shape=jax.ShapeDtypeStruct((ns, rh, c), BF16), compiler_params=_cp(("parallel", "parallel")), name=name,
    )(core, g, sib)


def _chip_sum(s1, rcv, where, *, name):
    _, rh, c = s1.shape
    tb = _row_block(rh, c)
    nb = rh // tb

    def body(where_ref, own_ref, r0_ref, r1_ref, r2_ref, o_ref):
        acc = own_ref[...].astype(F32)
        for r in (r0_ref, r1_ref, r2_ref):
            acc = acc + r[...].astype(F32)
        o_ref[...] = acc

    own = pl.BlockSpec((None, tb, c), lambda i, w: (w[1], i, 0))
    got = [pl.BlockSpec((None, tb, c), functools.partial(lambda i, w, j: (j, i, 0), j=j)) for j in range(3)]
    return pl.pallas_call(
        body, grid_spec=pltpu.PrefetchScalarGridSpec(
            num_scalar_prefetch=1, grid=(nb,), in_specs=[own] + got,
            out_specs=pl.BlockSpec((tb, c), lambda i, w: (w[0] * nb + i, 0))),
        out_shape=jax.ShapeDtypeStruct((2 * rh, c), F32), compiler_params=_cp(("parallel",)), name=name,
    )(where, s1, rcv, rcv, rcv)


def _all_gather_small(blk, *, name):
    r = blk.shape[0]

    def body(b_ref, out_ref, send_sems, recv_sems):
        x, y, c = _place()
        me = 4 * x + 2 * y + c
        out_ref[me] = b_ref[...]
        sends = []
        for k in range(1, N_DEV):
            peer = (x ^ (k >> 2), y ^ ((k >> 1) & 1), c ^ (k & 1))
            sends.append(pltpu.make_async_remote_copy(src_ref=b_ref, dst_ref=out_ref.at[me], send_sem=send_sems.at[k - 1],
                                                      recv_sem=recv_sems.at[k - 1], device_id=peer, device_id_type=MESH))
        for cp in sends:
            cp.start()
        for k in range(1, N_DEV):
            rows = out_ref.at[me ^ k]
            pltpu.make_async_remote_copy(src_ref=rows, dst_ref=rows, send_sem=send_sems.at[k - 1],
                                         recv_sem=recv_sems.at[k - 1], device_id=(x, y, c), device_id_type=MESH).wait_recv()
        for cp in sends:
            cp.wait_send()

    return pl.pallas_call(
        body, out_shape=jax.ShapeDtypeStruct((N_DEV, r, LANES), blk.dtype), in_specs=[VMEM_SPEC], out_specs=VMEM_SPEC,
        scratch_shapes=[pltpu.SemaphoreType.DMA((N_DEV - 1,)), pltpu.SemaphoreType.DMA((N_DEV - 1,))],
        name=name,
    )(blk)


def _sum_rows(parts, out_dtype, *, name, tb=1024):
    rows = parts[0].shape[0]
    tb = _blk(rows, tb)

    def body(*refs):
        acc = refs[0][...].astype(F32)
        for r in refs[1:-1]:
            acc = acc + r[...].astype(F32)
        refs[-1][...] = acc.astype(refs[-1].dtype)

    spec = pl.BlockSpec((tb, LANES), lambda i: (i, 0))
    return pl.pallas_call(
        body, grid=(rows // tb,), in_specs=[spec] * len(parts), out_specs=spec,
        out_shape=jax.ShapeDtypeStruct((rows, LANES), out_dtype), compiler_params=_cp(("parallel",)), name=name,
    )(*parts)


BIG = (
    ("w_in", 1), ("w_mem_kv", 0), ("w_swa_up", 1), ("w_gdn_up", 1), ("w_xa_up", 1), ("w_out", 0), ("w_mlp_in", 1),
    ("w_mlp_out", 0))


class _Comm:
    MLP = ("w_mlp_in", "w_mlp_out")

    def __init__(self, late_shards, core, where):
        self.axis = dict(BIG)
        self.late_shards = late_shards
        self.mid = [k for k in late_shards if k not in self.MLP and k != "w_in"]
        self.core, self.where = core, where

    def gather_exchange(self, names):
        return _gather_over_ici([self.late_shards[k] for k in names])

    def as_weights(self, names, whole):
        return {k: (g.reshape(-1, g.shape[2]) if self.axis[k] == 0 else g) for k, g in zip(names, whole)}

    def gathered(self, names, landed, tag):
        return self.as_weights(names, _exchange_call(self.pass_on(names, landed), name=f"ag_{tag}_pass"))

    def pass_on(self, names, landed):
        return _gather_pass_on(landed, [self.late_shards[k] for k in names])

    def shard_major(self, k, grad):
        return grad.reshape(N_CHIPS, -1, grad.shape[-1]) if self.axis[k] == 0 else grad

    def pair_sums(self, gs, tag, sibs=None):
        if sibs is None:
            sibs = _exchange_call(_sibling_halves(gs), name=f"rs_sibling_{tag}")
        return [_pair_sum(g, s, self.core, name=f"rs_pair_sum_{tag}{i}") for i, (g, s) in enumerate(zip(gs, sibs))]

    def chip_sums(self, s1s, rcvs):
        return [_chip_sum(s1, rcv, self.where, name=f"rs_chip_sum_{i}") for i, (s1, rcv) in enumerate(zip(s1s, rcvs))]
SMALL = ("g_mix", "sinks", "a_log", "dt_bias", "gdn_norm_w", "g_mem", "g_mlp", "g_final")


def _rows128(a, rows):
    flat = a.reshape(-1)
    return jnp.pad(flat, (0, rows * LANES - flat.shape[0])).reshape(rows, LANES)


def kernel(x, mem, g_mix, w_in, sinks, conv_w, a_log, dt_bias, gdn_norm_w, g_mem, w_mem_kv, w_swa_up, w_gdn_up, w_xa_up, w_out, g_mlp, w_mlp_in, w_mlp_out, g_final, loss_target, m_g_mix, m_w_in, m_sinks, m_conv_w, m_a_log, m_dt_bias, m_gdn_norm_w, m_g_mem, m_w_mem_kv, m_w_swa_up, m_w_gdn_up, m_w_xa_up, m_w_out, m_g_mlp, m_w_mlp_in, m_w_mlp_out, m_g_final, v_g_mix, v_w_in, v_sinks, v_conv_w, v_a_log, v_dt_bias, v_gdn_norm_w, v_g_mem, v_w_mem_kv, v_w_swa_up, v_w_gdn_up, v_w_xa_up, v_w_out, v_g_mlp, v_w_mlp_in, v_w_mlp_out, v_g_final):
    given = dict(locals())
    xi, yi, ci = _place()
    chip = 2 * xi + yi
    core = jnp.reshape(ci, (1,)).astype(jnp.int32)
    where = jnp.stack([ci, chip]).astype(jnp.int32)

    comm = _Comm({k: given[k][0].astype(BF16) for k, _ in BIG}, core, where)
    wts = {}
    conv_shard = conv_w[0]
    conv_rows = -(-conv_shard.size // (8 * LANES)) * 8
    conv_all = _all_gather_small(_rows128(conv_shard, conv_rows), name="ag_conv")
    conv_full = jnp.concatenate(
        [conv_all[2 * s].reshape(-1)[:conv_shard.size].reshape(conv_shard.shape) for s in range(N_CHIPS)], axis=1)

    small = {k: given[k].reshape(1, -1) for k in SMALL}
    small["conv_w"] = conv_full
    loss_row, dx, grads = _local_step(x[0], mem[0], loss_target[0], wts, small, comm)
    big_grads = {k: grads[k] for k, _ in BIG}

    layout = [("loss", loss_row[:, :1])] + [(k, grads[k]) for k in SMALL] + [("conv_w", grads["conv_w"])]
    rows = [-(-a.size // LANES) for _, a in layout]
    blk_rows = -(-sum(rows) // 8) * 8
    blk = jnp.concatenate([_rows128(a.astype(F32), n) for (_, a), n in zip(layout, rows)]
                          + [jnp.zeros((blk_rows - sum(rows), LANES), F32)], axis=0)
    gathered = _all_gather_small(blk, name="ag_small_grads")
    reduced = _sum_rows([gathered[i] for i in range(N_DEV)], F32, name="small_grad_sum")
    small_grads, start = {}, 0
    for (k, a), n in zip(layout, rows):
        small_grads[k] = reduced[start:start + n].reshape(-1)[:a.size].reshape(a.shape)
        start += n
    loss = small_grads["loss"].reshape(())
    cw = conv_shard.shape[1]
    conv_grad = lax.dynamic_slice_in_dim(small_grads["conv_w"], chip * cw, cw, axis=1)

    names = ["g_mix", "w_in", "sinks", "conv_w", "a_log", "dt_bias", "gdn_norm_w", "g_mem", "w_mem_kv", "w_swa_up",
             "w_gdn_up", "w_xa_up", "w_out", "g_mlp", "w_mlp_in", "w_mlp_out", "g_final"]
    out_g, out_d, out_m, out_v = [], [], [], []
    for k in names:
        w, m, v = given[k], given["m_" + k], given["v_" + k]
        if k in big_grads:
            g2 = big_grads[k]
        elif k == "conv_w":
            g2 = conv_grad
        else:
            g2 = small_grads[k]
        as_given = (lambda a: a.reshape(1, -1)) if w.ndim == 1 else (lambda a: a)
        if w.shape[-1] % LANES and w.shape[-1] > LANES:
            tr = lambda a: jnp.swapaxes(a, -1, -2)
            g_out, delta, new_m, new_v = (tr(a) for a in _adamw(tr(w), tr(g2), tr(m), tr(v), name="adamw_" + k))
        else:
            g_out, delta, new_m, new_v = _adamw(as_given(w), g2, as_given(m), as_given(v), name="adamw_" + k)
        out_g.append(g_out.reshape(w.shape))
        out_d.append(delta.reshape(w.shape))
        out_m.append(new_m.reshape(w.shape))
        out_v.append(new_v.reshape(w.shape))
    return (loss, dx[None], *out_g, *out_d, *out_m, *out_v)
```

```python
import functools
import math
from typing import Callable, NamedTuple

import jax
import jax.numpy as jnp
from jax import lax
from jax.experimental import pallas as pl
from jax.experimental.pallas import tpu as pltpu

F32 = jnp.float32
BF16 = jnp.bfloat16
HI = lax.Precision.HIGHEST
MESH = pl.DeviceIdType.MESH

SWA_Q_HEADS = 16
SWA_KV_HEADS = 2
SWA_HEAD_DIM = 64
SWA_WINDOW = 128
SWA_SCALE = SWA_HEAD_DIM ** -0.5
assert math.frexp(SWA_SCALE)[0] == 0.5
GDN_HEADS = 4
GDN_HEAD_DIM = 128
GDN_CONV = 4
GDN_CHUNK = 64
XA_HEADS = 4
XA_HEAD_DIM = 128
RMS_EPS = 1e-6
L2_EPS = 1e-6
ADAM_LR = 0.001
ADAM_B1 = 0.9
ADAM_B2 = 0.999
ADAM_EPS = 1e-08
ADAM_WD = 0.01
ADAM_STEP = 10

LANES = 128
N_SHARDS = 4
VMEM_LIMIT = 56 * 1024 * 1024

NT = (((1,), (1,)), ((), ()))
TN = (((0,), (0,)), ((), ()))
NN = (((1,), (0,)), ((), ()))


def _cp(sem=None):
    return pltpu.CompilerParams(dimension_semantics=sem, vmem_limit_bytes=VMEM_LIMIT)


def _blk(dim, pref):
    if dim <= pref:
        return dim
    b = (pref // LANES) * LANES
    while dim % b:
        b -= LANES
    assert b > 0, (dim, pref)
    return b


def _dot(a, b, dims=NN, precision=None):
    return lax.dot_general(a, b, dims, precision=precision, preferred_element_type=F32)


def _sigmoid(x):
    return 0.5 * jnp.tanh(0.5 * x) + 0.5


MM_TK_BYTES = 4096


def _mm(a, b, *, name, ta=False, tb=False, out_dtypes=(F32,), epilogue=None, extras=(), tm=1024, tn=1024, tk=None,
        b_sharded=False, out_sharded=False, b_window=None, exchange=None):
    (kdim, m) = a.shape if ta else a.shape[::-1]
    col0 = 0
    n_lim = k_lim = None
    if b_sharded:
        ns, rows_w, per = b.shape
        if tb:
            kb, n, k_lim = ns * per, rows_w, per
        else:
            kb, n, n_lim = rows_w, ns * per, per
    else:
        (kb, n) = b.shape[::-1] if tb else b.shape
        if b_window is not None:
            assert not tb
            col0, n = b_window
    assert kdim == kb, (a.shape, b.shape, ta, tb)
    if out_sharded:
        assert n % N_SHARDS == 0
        n_lim = n // N_SHARDS if n_lim is None else n_lim
        assert n_lim == n // N_SHARDS
    if tk is None:
        tk = MM_TK_BYTES // max(a.dtype.itemsize, b.dtype.itemsize)
    tm, tn, tk = _blk(m, tm), _blk(n_lim or n, tn), _blk(k_lim or kdim, tk)
    assert col0 % tn == 0, (col0, tn)
    nk = kdim // tk
    a_spec = pl.BlockSpec((tk, tm), lambda i, j, k: (k, i)) if ta else pl.BlockSpec((tm, tk), lambda i, j, k: (i, k))
    if b_sharded and tb:
        kpb = k_lim // tk
        b_spec = pl.BlockSpec((None, tn, tk), lambda i, j, k: (k // kpb, j, k % kpb))
    elif b_sharded:
        bpb = n_lim // tn
        b_spec = pl.BlockSpec((None, tk, tn), lambda i, j, k: (j // bpb, k, j % bpb))
    elif tb:
        b_spec = pl.BlockSpec((tn, tk), lambda i, j, k: (j, k))
    else:
        b_spec = pl.BlockSpec((tk, tn), lambda i, j, k: (k, j + col0 // tn))
    x_spec = pl.BlockSpec((tm, tn), lambda i, j, k: (i, j))
    if out_sharded:
        opb = n_lim // tn
        o_spec = pl.BlockSpec((None, tm, tn), lambda i, j, k: (j // opb, i, j % opb))
        out_shape = (N_SHARDS, m, n_lim)
    else:
        o_spec, out_shape = x_spec, (m, n)
    dims = ((((0 if ta else 1),), ((1 if tb else 0),)), ((), ()))
    n_extra, n_out = len(extras), len(out_dtypes)

    host = _ExchangeHost(exchange)
    grid = (m // tm, n // tn, nk)

    def body(*refs):
        a_ref, b_ref = refs[:2]
        extra_refs = refs[2:2 + n_extra]
        out_refs = refs[2 + n_extra + host.n_in:2 + n_extra + host.n_in + n_out]
        host.start(refs, 2 + n_extra, 2 + n_extra + host.n_in + n_out, grid)
        part = _dot(a_ref[...].astype(BF16), b_ref[...].astype(BF16), dims)

        def finish(acc):
            vals = epilogue(acc, *[r[...] for r in extra_refs]) if epilogue is not None else (acc,) * n_out
            assert len(vals) == n_out
            for r, v in zip(out_refs, vals):
                r[...] = v.astype(r.dtype)

        if nk == 1:
            finish(part)
        else:
            acc_ref = refs[2 + n_extra + host.n_in + n_out + host.n_out]
            k = pl.program_id(2)

            @pl.when(k == 0)
            def _():
                acc_ref[...] = part

            @pl.when((k > 0) & (k < nk - 1))
            def _():
                acc_ref[...] += part

            @pl.when(k == nk - 1)
            def _():
                finish(acc_ref[...] + part)

        host.wait(refs, 2 + n_extra, 2 + n_extra + host.n_in + n_out, grid)

    outs = pl.pallas_call(
        body,
        grid=grid,
        in_specs=[a_spec, b_spec] + [x_spec] * n_extra + host.in_specs,
        out_specs=[o_spec] * n_out + host.out_specs,
        out_shape=[jax.ShapeDtypeStruct(out_shape, d) for d in out_dtypes] + host.out_shapes,
        scratch_shapes=([pltpu.VMEM((tm, tn), F32)] if nk > 1 else []) + host.scratch,
        input_output_aliases=host.aliases(2 + n_extra, n_out),
        compiler_params=_cp(host.semantics(("parallel", "parallel", "arbitrary"))),
        name=name,
    )(a, b, *extras, *host.ins)
    mine, landed = outs[:n_out], list(outs[n_out:])
    mine = mine[0] if n_out == 1 else mine
    return (mine, landed) if exchange is not None else mine


class Exchange(NamedTuple):
    ins: tuple
    out_shapes: tuple
    n_remote: int
    n_local: int
    copies_of: Callable
    aliases: tuple = ()


def _exchange_copies(ex, in_refs, out_refs, sem_refs):
    send_sems, recv_sems, local_sems = sem_refs
    remote, local = ex.copies_of(in_refs, out_refs, _place())
    assert len(remote) == ex.n_remote and len(local) == ex.n_local, (len(remote), len(local))
    cps = [pltpu.make_async_remote_copy(src_ref=src, dst_ref=dst, send_sem=send_sems.at[k], recv_sem=recv_sems.at[k],
                                        device_id=to, device_id_type=MESH) for k, (src, dst, to) in enumerate(remote)]
    cps += [pltpu.make_async_copy(src, dst, local_sems.at[k]) for k, (src, dst) in enumerate(local)]
    return cps


class _ExchangeHost:
    def __init__(self, ex):
        self.ex = ex
        self.ins = list(ex.ins) if ex else []
        self.out_shapes = list(ex.out_shapes) if ex else []
        self.n_in, self.n_out = len(self.ins), len(self.out_shapes)
        self.in_specs = [HBM_SPEC] * self.n_in
        self.out_specs = [HBM_SPEC] * self.n_out
        self.scratch = _sem_scratch(ex.n_remote, ex.n_local) if ex else []

    def semantics(self, sem):
        return tuple("arbitrary" for _ in sem) if self.ex else sem

    def aliases(self, in_at, out_at):
        return {in_at + i: out_at + o for i, o in self.ex.aliases} if self.ex else {}

    def _refs(self, refs, in_at, out_at):
        return refs[in_at:in_at + self.n_in], refs[out_at:out_at + self.n_out], refs[len(refs) - 3:]

    def _when(self, grid, last):
        cond = None
        for d, size in enumerate(grid):
            c = pl.program_id(d) == (size - 1 if last else 0)
            cond = c if cond is None else cond & c
        return cond

    def start(self, refs, in_at, out_at, grid):
        if self.ex:
            @pl.when(self._when(grid, False))
            def _():
                for cp in _exchange_copies(self.ex, *self._refs(refs, in_at, out_at)):
                    cp.start()

    def wait(self, refs, in_at, out_at, grid):
        if self.ex:
            @pl.when(self._when(grid, True))
            def _():
                for cp in _exchange_copies(self.ex, *self._refs(refs, in_at, out_at)):
                    cp.wait()


def _rms_fwd(x, g, *, name, tm=512, exchange=None):
    t, d = x.shape
    tm = _blk(t, tm)
    host = _ExchangeHost(exchange)
    grid = (t // tm,)

    def body(*refs):
        x_ref, g_ref, n_ref = refs[0], refs[1], refs[2 + host.n_in]
        host.start(refs, 2, 3 + host.n_in, grid)
        xv = x_ref[...]
        r = lax.rsqrt(jnp.mean(xv * xv, axis=-1, keepdims=True) + RMS_EPS)
        n_ref[...] = (xv * r * g_ref[...]).astype(n_ref.dtype)
        host.wait(refs, 2, 3 + host.n_in, grid)

    outs = pl.pallas_call(
        body, grid=grid,
        in_specs=[pl.BlockSpec((tm, d), lambda i: (i, 0)), pl.BlockSpec((1, d), lambda i: (0, 0))] + host.in_specs,
        out_specs=[pl.BlockSpec((tm, d), lambda i: (i, 0))] + host.out_specs,
        out_shape=[jax.ShapeDtypeStruct((t, d), BF16)] + host.out_shapes,
        scratch_shapes=host.scratch, input_output_aliases=host.aliases(2, 1),
        compiler_params=_cp(host.semantics(("parallel",))), name=name,
    )(x, g, *host.ins)
    return (outs[0], list(outs[1:])) if exchange is not None else outs[0]


def _rms_bwd(dn, x, g, dres, *, name, tm=512):
    t, d = x.shape
    tm = _blk(t, tm)

    def body(dn_ref, x_ref, g_ref, dres_ref, dx_ref, dxb_ref, dg_ref):
        i = pl.program_id(0)
        xv = x_ref[...]
        r = lax.rsqrt(jnp.mean(xv * xv, axis=-1, keepdims=True) + RMS_EPS)
        xh = xv * r
        dnv = dn_ref[...].astype(F32)
        dxh = dnv * g_ref[...]
        dx = dres_ref[...] + r * (dxh - xh * jnp.mean(dxh * xh, axis=-1, keepdims=True))
        dx_ref[...] = dx
        dxb_ref[...] = dx.astype(dxb_ref.dtype)
        part = jnp.sum(dnv * xh, axis=0, keepdims=True)

        @pl.when(i == 0)
        def _():
            dg_ref[...] = part

        @pl.when(i > 0)
        def _():
            dg_ref[...] += part

    row = pl.BlockSpec((tm, d), lambda i: (i, 0))
    vec = pl.BlockSpec((1, d), lambda i: (0, 0))
    return pl.pallas_call(
        body, grid=(t // tm,),
        in_specs=[row, row, vec, row], out_specs=[row, row, vec],
        out_shape=[jax.ShapeDtypeStruct((t, d), F32), jax.ShapeDtypeStruct((t, d), BF16),
                   jax.ShapeDtypeStruct((1, d), F32)],
        compiler_params=_cp(("arbitrary",)), name=name,
    )(dn, x, g, dres)


def _final_loss(h, g, tgt, *, name, tm=512):
    t, d = h.shape
    tm = _blk(t, tm)

    def body(h_ref, g_ref, t_ref, dh_ref, dhb_ref, dg_ref, loss_ref):
        i = pl.program_id(0)
        hv = h_ref[...]
        r = lax.rsqrt(jnp.mean(hv * hv, axis=-1, keepdims=True) + RMS_EPS)
        xh = hv * r
        e = xh * g_ref[...] - t_ref[...]
        dy = e * (1.0 / d)
        dxh = dy * g_ref[...]
        dh = r * (dxh - xh * jnp.mean(dxh * xh, axis=-1, keepdims=True))
        dh_ref[...] = dh
        dhb_ref[...] = dh.astype(dhb_ref.dtype)
        dg_part = jnp.sum(dy * xh, axis=0, keepdims=True)
        row_loss = jnp.sum(e * e, axis=-1, keepdims=True) * (0.5 / d)
        loss_part = jnp.sum(row_loss, axis=0, keepdims=True)

        @pl.when(i == 0)
        def _():
            dg_ref[...] = dg_part
            loss_ref[...] = jnp.broadcast_to(loss_part, loss_ref.shape)

        @pl.when(i > 0)
        def _():
            dg_ref[...] += dg_part
            loss_ref[...] += jnp.broadcast_to(loss_part, loss_ref.shape)

    row = pl.BlockSpec((tm, d), lambda i: (i, 0))
    vec = pl.BlockSpec((1, d), lambda i: (0, 0))
    return pl.pallas_call(
        body, grid=(t // tm,),
        in_specs=[row, vec, row], out_specs=[row, row, vec, pl.BlockSpec((1, LANES), lambda i: (0, 0))],
        out_shape=[jax.ShapeDtypeStruct((t, d), F32), jax.ShapeDtypeStruct((t, d), BF16),
                   jax.ShapeDtypeStruct((1, d), F32), jax.ShapeDtypeStruct((1, LANES), F32)],
        compiler_params=_cp(("arbitrary",)), name=name,
    )(h, g, tgt)


SWA_SUB = 64


def _swa_mask(n, rows, row0):
    w = SWA_WINDOW
    qi = (lax.broadcasted_iota(jnp.int32, (rows, 2 * w), 0) + row0) & (w - 1)
    kj = lax.broadcasted_iota(jnp.int32, (rows, 2 * w), 1)
    return (kj > qi) & (kj <= qi + w) & ((n > 0) | (kj >= w))


def _stack_heads(ref, heads, width):
    return jnp.concatenate([ref[:, h * width:(h + 1) * width] for h in heads], axis=0)


def _stack_scalars(ref, heads, rows):
    return jnp.concatenate([jnp.broadcast_to(ref[0:1, h:h + 1], (rows, 1)) for h in heads], axis=0)


def _swa_fwd(q, kv, sinks, *, name, exchange=None):
    t = q.shape[0]
    w, hd, hq, hkv = SWA_WINDOW, SWA_HEAD_DIM, SWA_Q_HEADS, SWA_KV_HEADS
    grp = hq // hkv
    kvw = hkv * hd
    nb = t // w
    host = _ExchangeHost(exchange)
    assert not (exchange and exchange.aliases)

    def body(*refs):
        q_ref, kvp_ref, kvc_ref, s_ref = refs[:4]
        o_ref, lse_ref = refs[4 + host.n_in:6 + host.n_in]
        host.start(refs, 4, 6 + host.n_in, (nb,))
        n = pl.program_id(0)
        mask = _swa_mask(n, grp * w, 0)
        kvcat = jnp.concatenate([kvp_ref[...], kvc_ref[...]], axis=0)
        kvs = range(hkv)
        heads = [range(hk * grp, (hk + 1) * grp) for hk in kvs]
        sks = [_stack_scalars(s_ref, hs, w) for hs in heads]
        ss = [jnp.where(mask, _dot(_stack_heads(q_ref, heads[hk], hd) * SWA_SCALE, kvcat[:, hk * hd:(hk + 1) * hd], NT),
                        -jnp.inf) for hk in kvs]
        ms = [jnp.maximum(jnp.max(s, axis=-1, keepdims=True), sk) for s, sk in zip(ss, sks)]
        ps = [jnp.exp(s - m) for s, m in zip(ss, ms)]
        dens = [jnp.sum(p, axis=-1, keepdims=True) + jnp.exp(sk - m) for p, sk, m in zip(ps, sks, ms)]
        os_ = [_dot((p * (1.0 / den)).astype(BF16), kvcat[:, kvw + hk * hd:kvw + (hk + 1) * hd])
               for hk, p, den in zip(kvs, ps, dens)]
        outs, lses = [], []
        for o, m, den in zip(os_, ms, dens):
            lse = m + jnp.log(den)
            outs += [o[j * w:(j + 1) * w] for j in range(grp)]
            lses += [lse[j * w:(j + 1) * w] for j in range(grp)]
        o_ref[...] = jnp.concatenate(outs, axis=1).astype(o_ref.dtype)
        lse_ref[...] = jnp.concatenate(lses, axis=1)
        host.wait(refs, 4, 6 + host.n_in, (nb,))

    outs = pl.pallas_call(
        body, grid=(nb,),
        in_specs=[pl.BlockSpec((w, hq * hd), lambda i: (i, 0)),
                  pl.BlockSpec((w, 2 * kvw), lambda i: (jnp.maximum(i - 1, 0), 0)),
                  pl.BlockSpec((w, 2 * kvw), lambda i: (i, 0)),
                  pl.BlockSpec((1, hq), lambda i: (0, 0))] + host.in_specs,
        out_specs=[pl.BlockSpec((w, hq * hd), lambda i: (i, 0)), pl.BlockSpec((w, hq), lambda i: (i, 0))] + host.out_specs,
        out_shape=[jax.ShapeDtypeStruct((t, hq * hd), BF16), jax.ShapeDtypeStruct((t, hq), F32)] + host.out_shapes,
        scratch_shapes=host.scratch,
        compiler_params=_cp(host.semantics(("parallel",))), name=name,
    )(q, kv, kv, sinks, *host.ins)
    return (outs[0], outs[1], list(outs[2:])) if exchange is not None else outs


ANY_SPEC = pl.BlockSpec(memory_space=pl.ANY)


def _swa_bwd(q, kv, sinks, o, lse, do, dp, dp_col, *, name):
    t = q.shape[0]
    w, hd, hq, hkv = SWA_WINDOW, SWA_HEAD_DIM, SWA_Q_HEADS, SWA_KV_HEADS
    grp = hq // hkv
    kvw = hkv * hd
    nb = t // w
    assert dp_col % (hq * hd) == 0
    dq_blk = dp_col // (hq * hd)

    def body(q_ref, kvp_ref, kvc_ref, s_ref, o_ref, lse_ref, do_ref, _, dq_ref, dkv_ref, ds_ref, carry_ref, s_scr, dp_scr,
             p_scr, ds_scr):
        n = pl.program_id(0)

        @pl.when(n == 0)
        def _():
            ds_ref[...] = jnp.zeros_like(ds_ref)
            carry_ref[...] = jnp.zeros_like(carry_ref)

        @pl.when(n < nb)
        def _():
            kvcat = jnp.concatenate([kvp_ref[...], kvc_ref[...]], axis=0)
            dqs, dsk, dks, dvs = [], [], [], []
            for hk in range(hkv):
                heads = range(hk * grp, (hk + 1) * grp)
                qs = _stack_heads(q_ref, heads, hd)
                dos = _stack_heads(do_ref, heads, hd)
                os_ = _stack_heads(o_ref, heads, hd)
                lse = _stack_heads(lse_ref, heads, 1)
                kh = kvcat[:, hk * hd:(hk + 1) * hd]
                vh = kvcat[:, kvw + hk * hd:kvw + (hk + 1) * hd]
                delta = jnp.sum(dos.astype(F32) * os_.astype(F32), axis=-1, keepdims=True)
                s_scr[...] = _dot(qs * SWA_SCALE, kh, NT)
                dp_scr[...] = _dot(dos, vh, NT)
                for r0 in range(0, grp * w, SWA_SUB):
                    rows = slice(r0, r0 + SWA_SUB)
                    p = jnp.exp(jnp.where(_swa_mask(n, SWA_SUB, r0 % w), s_scr[rows, :], -jnp.inf) - lse[rows])
                    p_scr[rows, :] = p.astype(p_scr.dtype)
                    ds_scr[rows, :] = (p * (dp_scr[rows, :] - delta[rows]) * SWA_SCALE).astype(ds_scr.dtype)
                ds = ds_scr[...]
                dq = _dot(ds, kh)
                dqs += [dq[j * w:(j + 1) * w] for j in range(grp)]
                dks.append(_dot(ds, qs, TN))
                dvs.append(_dot(p_scr[...], dos, TN))
                dsink = -jnp.exp(_stack_scalars(s_ref, heads, w) - lse) * delta
                dsk += [jnp.sum(dsink[j * w:(j + 1) * w], axis=0, keepdims=True) for j in range(grp)]
            dq_ref[...] = jnp.concatenate(dqs, axis=1).astype(dq_ref.dtype)
            ds_ref[...] += jnp.concatenate(dsk, axis=1)
            dkv_cat = jnp.concatenate(dks + dvs, axis=1)
            dkv_ref[...] = (carry_ref[...] + dkv_cat[:w]).astype(dkv_ref.dtype)
            carry_ref[...] = dkv_cat[w:]

        @pl.when(n == nb)
        def _():
            dkv_ref[...] = carry_ref[...].astype(dkv_ref.dtype)

    cur = lambda i: (jnp.minimum(i, nb - 1), 0)
    prev = lambda i: (jnp.clip(i - 1, 0, nb - 1), 0)
    return pl.pallas_call(
        body, grid=(nb + 1,),
        in_specs=[pl.BlockSpec((w, hq * hd), cur), pl.BlockSpec((w, 2 * kvw), prev), pl.BlockSpec((w, 2 * kvw), cur),
                  pl.BlockSpec((1, hq), lambda i: (0, 0)), pl.BlockSpec((w, hq * hd), cur),
                  pl.BlockSpec((w, hq), cur), pl.BlockSpec((w, hq * hd), cur), ANY_SPEC],
        out_specs=[pl.BlockSpec((w, hq * hd), lambda i: (jnp.minimum(i, nb - 1), dq_blk)),
                   pl.BlockSpec((w, 2 * kvw), prev), pl.BlockSpec((1, hq), lambda i: (0, 0))],
        out_shape=[jax.ShapeDtypeStruct(dp.shape, dp.dtype), jax.ShapeDtypeStruct((t, 2 * kvw), BF16),
                   jax.ShapeDtypeStruct((1, hq), F32)],
        scratch_shapes=[pltpu.VMEM((w, 2 * kvw), F32)] + [pltpu.VMEM((grp * w, 2 * w), dt) for dt in (F32, F32, BF16, BF16)],
        input_output_aliases={7: 0},
        compiler_params=_cp(("arbitrary",)), name=name,
    )(q, kv, kv, sinks, o, lse, do, dp)


def _xa_fwd(q, mkv, *, name, tq=512):
    t, xw = q.shape
    nm = mkv.shape[0]
    hd, nh = XA_HEAD_DIM, XA_HEADS
    tq = _blk(t, tq)

    def body(q_ref, mkv_ref, o_ref):
        outs = []
        for h in range(nh):
            qh = q_ref[:, h * hd:(h + 1) * hd]
            kh = mkv_ref[:, h * hd:(h + 1) * hd]
            vh = mkv_ref[:, xw + h * hd:xw + (h + 1) * hd]
            s = _dot(qh, kh, NT) * (hd ** -0.5)
            p = jnp.exp(s - jnp.max(s, axis=-1, keepdims=True))
            p = p * (1.0 / jnp.sum(p, axis=-1, keepdims=True))
            outs.append(_dot(p.astype(BF16), vh))
        o_ref[...] = jnp.concatenate(outs, axis=1).astype(o_ref.dtype)

    return pl.pallas_call(
        body, grid=(t // tq,),
        in_specs=[pl.BlockSpec((tq, xw), lambda i: (i, 0)), pl.BlockSpec((nm, 2 * xw), lambda i: (0, 0))],
        out_specs=pl.BlockSpec((tq, xw), lambda i: (i, 0)),
        out_shape=jax.ShapeDtypeStruct((t, xw), BF16),
        compiler_params=_cp(("parallel",)), name=name,
    )(q, mkv)


def _xa_bwd(q, mkv, do, dp, dp_col, *, name, tq=512):
    t, xw = q.shape
    nm = mkv.shape[0]
    hd, nh = XA_HEAD_DIM, XA_HEADS
    tq = _blk(t, tq)
    assert dp_col % xw == 0

    def body(q_ref, mkv_ref, do_ref, _, dq_ref, dmkv_ref):
        i = pl.program_id(0)
        dqs, dks, dvs = [], [], []
        for h in range(nh):
            qh = q_ref[:, h * hd:(h + 1) * hd]
            kh = mkv_ref[:, h * hd:(h + 1) * hd]
            vh = mkv_ref[:, xw + h * hd:xw + (h + 1) * hd]
            doh = do_ref[:, h * hd:(h + 1) * hd]
            s = _dot(qh, kh, NT) * (hd ** -0.5)
            p = jnp.exp(s - jnp.max(s, axis=-1, keepdims=True))
            p = p * (1.0 / jnp.sum(p, axis=-1, keepdims=True))
            dp = _dot(doh, vh, NT)
            ds = (p * (dp - jnp.sum(p * dp, axis=-1, keepdims=True)) * (hd ** -0.5)).astype(BF16)
            dqs.append(_dot(ds, kh))
            dks.append(_dot(ds, qh, TN))
            dvs.append(_dot(p.astype(BF16), doh, TN))
        dq_ref[...] = jnp.concatenate(dqs, axis=1).astype(dq_ref.dtype)
        part = jnp.concatenate(dks + dvs, axis=1)

        @pl.when(i == 0)
        def _():
            dmkv_ref[...] = part

        @pl.when(i > 0)
        def _():
            dmkv_ref[...] += part

    row = pl.BlockSpec((tq, xw), lambda i: (i, 0))
    full = pl.BlockSpec((nm, 2 * xw), lambda i: (0, 0))
    return pl.pallas_call(
        body, grid=(t // tq,),
        in_specs=[row, full, row, ANY_SPEC],
        out_specs=[pl.BlockSpec((tq, xw), lambda i: (i, dp_col // xw)), full],
        out_shape=[jax.ShapeDtypeStruct(dp.shape, dp.dtype), jax.ShapeDtypeStruct((nm, 2 * xw), F32)],
        input_output_aliases={3: 0}, compiler_params=_cp(("arbitrary",)), name=name,
    )(q, mkv, do, dp)


def _merge_specs(ys, ws, tm):
    y_specs = [pl.BlockSpec((tm, y.shape[1]), lambda i: (i, 0)) for y in ys]
    w_specs = [pl.BlockSpec(w.shape, lambda i: (0, 0, 0)) for w in ws]
    return y_specs, w_specs


def _merge_tiles(ws, tn):
    ns, _, per = ws[0].shape
    tn = _blk(per, tn)
    return tn, [(s, c, s * per + c) for s in range(ns) for c in range(0, per, tn)]


def _merge_fwd(ys, ws, gates, *, name, tm=256, tn=512):
    t, d = ys[0].shape[0], ws[0].shape[0] * ws[0].shape[2]
    tm = _blk(t, tm)
    tn, tiles = _merge_tiles(ws, tn)
    y_specs, w_specs = _merge_specs(ys, ws, tm)

    def body(ya, yb, yc, wa, wb, wc, g_ref, o_ref):
        for s, c, col in tiles:
            acc = None
            for b, (y, w) in enumerate(((ya, wa), (yb, wb), (yc, wc))):
                term = _sigmoid(g_ref[:, b * d + col:b * d + col + tn]) * _dot(y[...], w[s, :, c:c + tn])
                acc = term if acc is None else acc + term
            o_ref[:, col:col + tn] = acc.astype(o_ref.dtype)

    return pl.pallas_call(
        body, grid=(t // tm,),
        in_specs=y_specs + w_specs + [pl.BlockSpec((tm, 3 * d), lambda i: (i, 0))],
        out_specs=pl.BlockSpec((tm, d), lambda i: (i, 0)),
        out_shape=jax.ShapeDtypeStruct((t, d), BF16),
        compiler_params=_cp(("parallel",)), name=name,
    )(*ys, *ws, gates)


def _merge_bwd(ys, ws, gates, dmerged, dp_width, *, name, tm=256, tn=512):
    t, d = ys[0].shape[0], ws[0].shape[0] * ws[0].shape[2]
    tm = _blk(t, tm)
    tn, tiles = _merge_tiles(ws, tn)
    y_specs, w_specs = _merge_specs(ys, ws, tm)
    row = pl.BlockSpec((tm, d), lambda i: (i, 0))
    wide = pl.BlockSpec((tm, 3 * d), lambda i: (i, 0))

    def body(ya, yb, yc, wa, wb, wc, g_ref, dm_ref, dua, dub, duc, dp_ref):
        for s, c, col in tiles:
            dm = dm_ref[:, col:col + tn]
            for b, (y, w, du) in enumerate(((ya, wa, dua), (yb, wb, dub), (yc, wc, duc))):
                sg = _sigmoid(g_ref[:, b * d + col:b * d + col + tn])
                u = _dot(y[...], w[s, :, c:c + tn])
                du[:, col:col + tn] = (dm * sg).astype(du.dtype)
                dp_ref[:, b * d + col:b * d + col + tn] = (dm * u * sg * (1.0 - sg)).astype(dp_ref.dtype)

    return pl.pallas_call(
        body, grid=(t // tm,),
        in_specs=y_specs + w_specs + [wide, row],
        out_specs=[row] * 3 + [wide],
        out_shape=[jax.ShapeDtypeStruct((t, d), BF16)] * 3 + [jax.ShapeDtypeStruct((t, dp_width), BF16)],
        compiler_params=_cp(("parallel",)), name=name,
    )(*ys, *ws, gates, dmerged)


def _adamw(w, g, m, v, *, name, tm=256):
    lead = w.ndim - 2
    assert all(s == 1 for s in w.shape[:lead]) and m.shape == w.shape and v.shape == w.shape
    r, c = w.shape[lead:]
    assert g.shape == (r, c)
    tm = _blk(r, tm) if r % 8 == 0 else r
    tc = c if tm * c * 4 <= (4 << 20) else _blk(c, 256)
    ncb = c // tc
    bc1 = 1.0 - ADAM_B1 ** ADAM_STEP
    bc2 = 1.0 - ADAM_B2 ** ADAM_STEP

    def body(w_ref, g_ref, m_ref, v_ref, go_ref, d_ref, nm_ref, nv_ref):
        gv = g_ref[...]
        go_ref[...] = gv
        nm = ADAM_B1 * m_ref[...] + (1.0 - ADAM_B1) * gv
        nv = ADAM_B2 * v_ref[...] + (1.0 - ADAM_B2) * (gv * gv)
        d_ref[...] = -ADAM_LR * ((nm / bc1) / (jnp.sqrt(nv / bc2) + ADAM_EPS) + ADAM_WD * w_ref[...])
        nm_ref[...] = nm
        nv_ref[...] = nv

    spec = pl.BlockSpec((None,) * lead + (tm, tc), lambda i: (0,) * lead + (i // ncb, i % ncb))
    g_spec = pl.BlockSpec((tm, tc), lambda i: (i // ncb, i % ncb))
    return pl.pallas_call(
        body, grid=(r // tm * ncb,), in_specs=[spec, g_spec, spec, spec], out_specs=[spec] * 4,
        out_shape=[jax.ShapeDtypeStruct(w.shape, F32)] * 4,
        compiler_params=_cp(("parallel",)), name=name,
    )(w, g, m, v)


HALO = 8


def _shift_down(cur, prev, j):
    if j == 0:
        return cur
    y = pltpu.roll(cur, j, 0)
    row = lax.broadcasted_iota(jnp.int32, (HALO, cur.shape[1]), 0)
    top = jnp.where(row < j, pltpu.roll(prev, j, 0), y[:HALO])
    return jnp.concatenate([top, y[HALO:]], axis=0)


def _shift_up(cur, nxt, j):
    if j == 0:
        return cur
    tm = cur.shape[0]
    y = pltpu.roll(cur, tm - j, 0)
    row = lax.broadcasted_iota(jnp.int32, (HALO, cur.shape[1]), 0)
    bot = jnp.where(row >= HALO - j, pltpu.roll(nxt, HALO - j, 0), y[tm - HALO:])
    return jnp.concatenate([y[:tm - HALO], bot], axis=0)


def _softplus(x):
    return jnp.maximum(x, 0.0) + jnp.log(1.0 + jnp.exp(-jnp.abs(x)))


def _gdn_pre_fwd(qkvb, conv_w, ab, alog_pad, dt_pad, *, name, ab_blk=0, tm=256):
    t, cw = qkvb.shape
    hd, nh, ck = GDN_HEAD_DIM, GDN_HEADS, GDN_CHUNK
    gw = nh * hd
    tm = _blk(t, tm)
    hb = tm // HALO

    def body(x_ref, xp_ref, w_ref, ab_ref, al_ref, dt_ref, xc_ref, qkvn_ref, aux_ref):
        i = pl.program_id(0)
        cur = x_ref[...]
        prev = jnp.where(i > 0, xp_ref[...], 0.0)
        xc = None
        for tap in range(GDN_CONV):
            term = w_ref[tap:tap + 1, :] * _shift_down(cur, prev, GDN_CONV - 1 - tap)
            xc = term if xc is None else xc + term
        xc_ref[...] = xc
        s = xc * _sigmoid(xc)
        for h in range(2 * nh):
            xh = s[:, h * hd:(h + 1) * hd]
            r = lax.rsqrt(jnp.sum(xh * xh, axis=-1, keepdims=True) + L2_EPS)
            scale = hd ** -0.5 if h < nh else 1.0
            qkvn_ref[:, h * hd:(h + 1) * hd] = xh * (r * scale)
        qkvn_ref[:, 2 * gw:] = s[:, 2 * gw:]
        abv = ab_ref[...]
        lane = lax.broadcasted_iota(jnp.int32, abv.shape, 1)
        g = jnp.where(lane < nh, -jnp.exp(al_ref[...]) * _softplus(abv + dt_ref[...]), 0.0)
        beta = jnp.where((lane >= nh) & (lane < 2 * nh), _sigmoid(abv), 0.0)
        ii = lax.broadcasted_iota(jnp.int32, (tm, tm), 0)
        jj = lax.broadcasted_iota(jnp.int32, (tm, tm), 1)
        tri = jnp.where((ii >= jj) & ((ii ^ jj) < ck), 1.0, 0.0)
        gcum = _dot(tri, g, precision=HI)
        aux_ref[...] = g + beta + pltpu.roll(gcum, 2 * nh, 1)

    row = lambda c: pl.BlockSpec((tm, c), lambda i: (i, 0))
    vec = lambda r, c: pl.BlockSpec((r, c), lambda i: (0, 0))
    return pl.pallas_call(
        body, grid=(t // tm,),
        in_specs=[row(cw), pl.BlockSpec((HALO, cw), lambda i: (jnp.maximum(i * hb - 1, 0), 0)), vec(GDN_CONV, cw),
                  pl.BlockSpec((tm, LANES), lambda i: (i, ab_blk)), vec(1, LANES), vec(1, LANES)],
        out_specs=[row(cw), row(cw), row(LANES)],
        out_shape=[jax.ShapeDtypeStruct((t, cw), F32), jax.ShapeDtypeStruct((t, cw), F32),
                   jax.ShapeDtypeStruct((t, LANES), F32)],
        compiler_params=_cp(("parallel",)), name=name,
    )(qkvb, qkvb, conv_w, ab, alog_pad, dt_pad)


GDN_STEP_CHUNKS = 4
GDN_ILP_CHUNKS = 4
GDN_ILP_CHUNKS_BWD = 4


def _bdot(a, b, dims=NN):
    return _dot(a.astype(BF16), b.astype(BF16), dims)


def _split_bf16(x):
    hi = x.astype(BF16)
    return hi, (x - hi.astype(F32)).astype(BF16)


def _dot3(a, b, dims=NN):
    ah, al = _split_bf16(a)
    bh, bl = _split_bf16(b)
    return _dot(ah, bh, dims) + (_dot(ah, bl, dims) + _dot(al, bh, dims))


def _dot3_many(lhs, rhs, dims=NN):
    sa = [_split_bf16(a) for a in lhs]
    sb = [_split_bf16(b) for b in rhs]
    hh = [_dot(a[0], b[0], dims) for a, b in zip(sa, sb)]
    hl = [_dot(a[0], b[1], dims) for a, b in zip(sa, sb)]
    lh = [_dot(a[1], b[0], dims) for a, b in zip(sa, sb)]
    return [x + (y + z) for x, y, z in zip(hh, hl, lh)]


def _gdn_local(chains, with_inverse):
    ck = GDN_CHUNK
    ii = lax.broadcasted_iota(jnp.int32, (ck, ck), 0)
    jj = lax.broadcasted_iota(jnp.int32, (ck, ck), 1)
    lower, strict = ii >= jj, ii > jj
    dmat = [jnp.exp(jnp.where(lower, gc - gc_row, -jnp.inf)) for _, _, _, gc, gc_row in chains]
    kk = [_bdot(k, k, NT) for _, k, _, _, _ in chains]
    qk = [_bdot(q, k, NT) for q, k, _, _, _ in chains]
    tinv = [None] * len(chains)
    if with_inverse:
        lmat = [jnp.where(strict, c[2] * kk_i * d_i, 0.0) for c, kk_i, d_i in zip(chains, kk, dmat)]
        eye = jnp.where(ii == jj, 1.0, 0.0)
        tinv = [eye - l_i for l_i in lmat]
        pw = lmat
        for _ in range(int(math.log2(ck)) - 1):
            pw = _dot3_many(pw, pw)
            tinv = [t_i + d_i for t_i, d_i in zip(tinv, _dot3_many(tinv, pw))]
    out = []
    for (q, k, b, gc, gc_row), dmat_i, kk_i, qk_i, tinv_i in zip(chains, dmat, kk, qk, tinv):
        gl = gc[ck - 1:ck, :]
        out.append(dict(lower=lower, strict=strict, dmat=dmat_i, kk=kk_i, tinv=tinv_i, gam=jnp.exp(gc), qk=qk_i,
                        mm=qk_i * dmat_i, kdec=jnp.exp(gl - gc)))
    return out


def _gdn_head_cols(h):
    return slice(h * GDN_HEAD_DIM, (h + 1) * GDN_HEAD_DIM)


def _gdn_chunk_inputs(x_ref, aux_ref, auxt_ref, g, h):
    nh, ck = GDN_HEADS, GDN_CHUNK
    gw = nh * GDN_HEAD_DIM
    rows = slice(g * ck, (g + 1) * ck)
    cols = _gdn_head_cols(h)
    q = x_ref[rows, cols]
    k = x_ref[rows, gw + cols.start:gw + cols.stop]
    v = x_ref[rows, 2 * gw + cols.start:2 * gw + cols.stop]
    b = aux_ref[rows, nh + h:nh + h + 1]
    gc = aux_ref[rows, 2 * nh + h:2 * nh + h + 1]
    gc_row = auxt_ref[g, 2 * nh + h:2 * nh + h + 1, :]
    return q, k, v, b, gc, gc_row


def _gdn_specs(t, widths, *, reverse=False, step_chunks=None):
    rows = (step_chunks or GDN_STEP_CHUNKS) * GDN_CHUNK
    nsteps = t // rows
    idx = (lambda i: (nsteps - 1 - i, 0)) if reverse else (lambda i: (i, 0))
    return [pl.BlockSpec((rows, w), idx) for w in widths]


def _gdn_local_fwd(qkvn, aux, aux_t, *, name, exchange=None):
    t = qkvn.shape[0]
    hd, nh, ck, gs = GDN_HEAD_DIM, GDN_HEADS, GDN_CHUNK, GDN_STEP_CHUNKS
    gw = nh * hd
    host = _ExchangeHost(exchange)
    assert not (exchange and exchange.aliases)
    grid = (t // (gs * ck),)

    def body(*refs):
        x_ref, aux_ref, auxt_ref = refs[:3]
        u_ref, w_ref, qd_ref, kd_ref, mm_ref, tinv_ref = refs[3 + host.n_in:9 + host.n_in]
        host.start(refs, 3, 9 + host.n_in, grid)
        for g0 in range(0, gs, GDN_ILP_CHUNKS):
            where = [(g, h) for g in range(g0, g0 + GDN_ILP_CHUNKS) for h in range(nh)]
            ins = [_gdn_chunk_inputs(x_ref, aux_ref, auxt_ref, g, h) for g, h in where]
            lcs = _gdn_local([(q, k, b, gc, gc_row) for q, k, _, b, gc, gc_row in ins], True)
            tinvs = [lc["tinv"] for lc in lcs]
            us = _dot3_many(tinvs, [b * v for _, _, v, b, _, _ in ins])
            ws = _dot3_many(tinvs, [(b * lc["gam"]) * k for (_, k, _, b, _, _), lc in zip(ins, lcs)])
            for i, ((g, h), (q, k, _, _, _, _), lc) in enumerate(zip(where, ins, lcs)):
                rows, cols = slice(g * ck, (g + 1) * ck), _gdn_head_cols(h)
                u_ref[rows, cols] = us[i]
                w_ref[rows, cols] = ws[i].astype(w_ref.dtype)
                qd_ref[rows, cols] = (lc["gam"] * q).astype(qd_ref.dtype)
                kd_ref[rows, cols] = (lc["kdec"] * k).astype(kd_ref.dtype)
            for g in range(g0, g0 + GDN_ILP_CHUNKS):
                rows = slice(g * ck, (g + 1) * ck)
                mine = [lc for (gg, _), lc in zip(where, lcs) if gg == g]
                mm_ref[rows, :] = jnp.concatenate([lc["mm"] for lc in mine], axis=1).astype(mm_ref.dtype)
                tinv_ref[rows, :] = jnp.concatenate([lc["tinv"] for lc in mine], axis=1)
        host.wait(refs, 3, 9 + host.n_in, grid)

    sq = nh * ck
    outs = pl.pallas_call(
        body, grid=grid,
        in_specs=_gdn_specs(t, (3 * gw, LANES)) + [pl.BlockSpec((gs, 16, ck), lambda i: (i, 0, 0))] + host.in_specs,
        out_specs=_gdn_specs(t, (gw, gw, gw, gw, sq, sq)) + host.out_specs,
        out_shape=[jax.ShapeDtypeStruct((t, gw), F32)] + [jax.ShapeDtypeStruct((t, gw), BF16)] * 3
        + [jax.ShapeDtypeStruct((t, sq), BF16), jax.ShapeDtypeStruct((t, sq), F32)] + host.out_shapes,
        scratch_shapes=host.scratch,
        compiler_params=_cp(host.semantics(("parallel",))), name=name,
    )(qkvn, aux, aux_t, *host.ins)
    return (*outs[:6], list(outs[6:])) if exchange is not None else outs


def _gdn_seq_fwd(u, w, qd, kd, mm, aux, *, name):
    t = u.shape[0]
    hd, nh, ck, gs = GDN_HEAD_DIM, GDN_HEADS, GDN_CHUNK, GDN_STEP_CHUNKS
    gw = nh * hd
    sq = nh * ck

    def body(u_ref, w_ref, qd_ref, kd_ref, mm_ref, aux_ref, o_ref, vn_ref, sall_ref, s_ref):
        @pl.when(pl.program_id(0) == 0)
        def _():
            s_ref[...] = jnp.zeros_like(s_ref)

        heads = range(nh)
        hcols = [_gdn_head_cols(h) for h in heads]
        sts = [s_ref[h] for h in heads]
        for g in range(gs):
            rows = slice(g * ck, (g + 1) * ck)
            last = (g + 1) * ck - 1
            for h in heads:
                sall_ref[g, h] = sts[h]
            stbs = [st.astype(BF16) for st in sts]
            w_s = [_dot(w_ref[rows, c], stb) for c, stb in zip(hcols, stbs)]
            q_s = [_dot(qd_ref[rows, c], stb) for c, stb in zip(hcols, stbs)]
            vnbs = [(u_ref[rows, c] - ws).astype(BF16) for c, ws in zip(hcols, w_s)]
            m_v = [_dot(mm_ref[rows, h * ck:(h + 1) * ck], vnbs[h]) for h in heads]
            k_v = [_dot(kd_ref[rows, c], vnb, TN) for c, vnb in zip(hcols, vnbs)]
            for h, c in zip(heads, hcols):
                vn_ref[rows, c] = vnbs[h]
                o_ref[rows, c] = q_s[h] + m_v[h]
            gam_c = [jnp.exp(aux_ref[last:last + 1, 2 * nh + h:2 * nh + h + 1]) for h in heads]
            sts = [gam_c[h] * sts[h] + k_v[h] for h in heads]
        for h in heads:
            s_ref[h] = sts[h]

    return pl.pallas_call(
        body, grid=(t // (gs * ck),),
        in_specs=_gdn_specs(t, (gw, gw, gw, gw, sq, LANES)),
        out_specs=_gdn_specs(t, (gw, gw)) + [pl.BlockSpec((gs, nh, hd, hd), lambda i: (i, 0, 0, 0))],
        out_shape=[jax.ShapeDtypeStruct((t, gw), F32), jax.ShapeDtypeStruct((t, gw), BF16),
                   jax.ShapeDtypeStruct((t // ck, nh, hd, hd), F32)],
        scratch_shapes=[pltpu.VMEM((nh, hd, hd), F32)],
        compiler_params=_cp(("arbitrary",)), name=name,
    )(u, w, qd, kd, mm, aux)


def _gdn_seq_bwd(do, w, qd, kd, mm, vn, s_all, aux, *, name):
    t = do.shape[0]
    hd, nh, ck, gs = GDN_HEAD_DIM, GDN_HEADS, GDN_CHUNK, GDN_STEP_CHUNKS
    gw = nh * hd
    sq = nh * ck
    nsteps = t // (gs * ck)

    def body(do_ref, w_ref, qd_ref, kd_ref, mm_ref, vn_ref, sall_ref, aux_ref, dvn_ref, dqd_ref, dkd_ref, dw_ref,
             dlast_ref, ds_ref):
        @pl.when(pl.program_id(0) == 0)
        def _():
            ds_ref[...] = jnp.zeros_like(ds_ref)

        lane = lax.broadcasted_iota(jnp.int32, (ck, LANES), 1)
        rowi = lax.broadcasted_iota(jnp.int32, (ck, LANES), 0)
        heads = range(nh)
        hcols = [_gdn_head_cols(h) for h in heads]
        dsns = [ds_ref[h] for h in heads]
        for g in reversed(range(gs)):
            rows = slice(g * ck, (g + 1) * ck)
            last = (g + 1) * ck - 1
            sts = [sall_ref[g, h] for h in heads]
            stbs = [st.astype(BF16) for st in sts]
            dsbs = [dsn.astype(BF16) for dsn in dsns]
            dobs = [do_ref[rows, c].astype(BF16) for c in hcols]
            dvns = [_dot(mm_ref[rows, h * ck:(h + 1) * ck], dobs[h], TN) + _dot(kd_ref[rows, hcols[h]], dsbs[h])
                    for h in heads]
            dqds = [_dot(dob, stb, NT) for dob, stb in zip(dobs, stbs)]
            dkds = [_dot(vn_ref[rows, c], dsb, NT) for c, dsb in zip(hcols, dsbs)]
            q_o = [_dot(qd_ref[rows, c], dob, TN) for c, dob in zip(hcols, dobs)]
            dvbs = [dvn.astype(BF16) for dvn in dvns]
            dws = [_dot(dvb, stb, NT) for dvb, stb in zip(dvbs, stbs)]
            w_v = [_dot(w_ref[rows, c], dvb, TN) for c, dvb in zip(hcols, dvbs)]
            gam_c = [jnp.exp(aux_ref[last:last + 1, 2 * nh + h:2 * nh + h + 1]) for h in heads]
            dlast = jnp.zeros((ck, LANES), F32)
            for h, c in zip(heads, hcols):
                dvn_ref[rows, c] = dvns[h]
                dqd_ref[rows, c] = dqds[h]
                dkd_ref[rows, c] = dkds[h]
                dw_ref[rows, c] = -dws[h]
                dgam_c = jnp.sum(jnp.sum(dsns[h] * sts[h], axis=1, keepdims=True), axis=0, keepdims=True)
                dlast = dlast + jnp.where((rowi == ck - 1) & (lane == h), gam_c[h] * dgam_c, 0.0)
            dlast_ref[rows, :] = dlast
            dsns = [q_o[h] + gam_c[h] * dsns[h] - w_v[h] for h in heads]
        for h in heads:
            ds_ref[h] = dsns[h]

    return pl.pallas_call(
        body, grid=(nsteps,),
        in_specs=_gdn_specs(t, (gw, gw, gw, gw, sq, gw), reverse=True)
        + [pl.BlockSpec((gs, nh, hd, hd), lambda i: (nsteps - 1 - i, 0, 0, 0))] + _gdn_specs(t, (LANES,), reverse=True),
        out_specs=_gdn_specs(t, (gw, gw, gw, gw, LANES), reverse=True),
        out_shape=[jax.ShapeDtypeStruct((t, gw), F32)] * 4 + [jax.ShapeDtypeStruct((t, LANES), F32)],
        scratch_shapes=[pltpu.VMEM((nh, hd, hd), F32)],
        compiler_params=_cp(("arbitrary",)), name=name,
    )(do, w, qd, kd, mm, vn, s_all, aux)


def _gdn_local_bwd(qkvn, aux, aux_t, tinv, u, w, vn, do, dvn, dqd, dkd, dw, dlast, *, name):
    t = qkvn.shape[0]
    hd, nh, ck, gs = GDN_HEAD_DIM, GDN_HEADS, GDN_CHUNK, GDN_STEP_CHUNKS
    gw = nh * hd
    sq = nh * ck

    def body(x_ref, aux_ref, auxt_ref, tinv_ref, u_ref, w_ref, vn_ref, do_ref, dvn_ref, dqd_ref, dkd_ref, dw_ref,
             dlast_ref, dx_ref, daux_ref):
        lane = lax.broadcasted_iota(jnp.int32, (ck, LANES), 1)
        ones = jnp.ones((ck, LANES), F32)
        ii = lax.broadcasted_iota(jnp.int32, (ck, ck), 0)
        jj = lax.broadcasted_iota(jnp.int32, (ck, ck), 1)
        suffix = jnp.where(jj >= ii, 1.0, 0.0)
        for g0 in range(0, gs, GDN_ILP_CHUNKS_BWD):
            where = [(g, h) for g in range(g0, g0 + GDN_ILP_CHUNKS_BWD) for h in range(nh)]
            at = [(slice(g * ck, (g + 1) * ck), _gdn_head_cols(h)) for g, h in where]
            ins = [_gdn_chunk_inputs(x_ref, aux_ref, auxt_ref, g, h) for g, h in where]
            lcs = _gdn_local([(q, k, b, gc, gc_row) for q, k, _, b, gc, gc_row in ins], False)
            tinvs = [tinv_ref[slice(g * ck, (g + 1) * ck), h * ck:(h + 1) * ck] for g, h in where]
            dms = [jnp.where(lc["lower"], _bdot(do_ref[r, c], vn_ref[r, c], NT), 0.0) for lc, (r, c) in zip(lcs, at)]
            drvs = _dot3_many(tinvs, [dvn_ref[r, c] for r, c in at], TN)
            drks = _dot3_many(tinvs, [dw_ref[r, c] for r, c in at], TN)
            das = [jnp.where(lc["strict"], -(_bdot(drv, u_ref[r, c], NT) + _bdot(drk, w_ref[r, c], NT)), 0.0)
                   for lc, (r, c), drv, drk in zip(lcs, at, drvs, drks)]
            f_mats = [da * (i[3] * lc["kk"]) * lc["dmat"] + dm * lc["qk"] * lc["dmat"]
                      for i, lc, da, dm in zip(ins, lcs, das, dms)]
            col_sums = _dot3_many(f_mats, [ones] * len(where), TN)
            dgc_all = {g: dlast_ref[slice(g * ck, (g + 1) * ck), :] for g in range(g0, g0 + GDN_ILP_CHUNKS_BWD)}
            db_all = {g: jnp.zeros((ck, LANES), F32) for g in range(g0, g0 + GDN_ILP_CHUNKS_BWD)}
            e_mats = [da * lc["dmat"] * i[3] for i, lc, da in zip(ins, lcs, das)]
            dmds = [dm * lc["dmat"] for lc, dm in zip(lcs, dms)]
            dq_mm = [_bdot(dmd, i[1]) for i, dmd in zip(ins, dmds)]
            dk_mm = [_bdot(e, i[1]) + _bdot(e, i[1], TN) + _bdot(dmd, i[0], TN) for i, e, dmd in zip(ins, e_mats, dmds)]
            for n, ((g, h), (q, k, v, b, _, _), lc, (rows, cols)) in enumerate(zip(where, ins, lcs, at)):
                dmat, kk, gam, kdec = (lc[key] for key in ("dmat", "kk", "gam", "kdec"))
                drv, drk, da = drvs[n], drks[n], das[n]
                dqd_h, dkd_h = dqd_ref[rows, cols], dkd_ref[rows, cols]
                rs_rk = jnp.sum(drk * k, axis=-1, keepdims=True)
                db = (jnp.sum(drv * v, axis=-1, keepdims=True) + gam * rs_rk
                      + jnp.sum(da * kk * dmat, axis=-1, keepdims=True))
                dx_ref[rows, cols] = dq_mm[n] + gam * dqd_h
                dx_ref[rows, gw + cols.start:gw + cols.stop] = (b * gam) * drk + dk_mm[n] + kdec * dkd_h
                dx_ref[rows, 2 * gw + cols.start:2 * gw + cols.stop] = b * drv
                e_vec = jnp.sum(dkd_h * (kdec * k), axis=-1, keepdims=True)
                dgc = (b * gam * rs_rk + gam * jnp.sum(dqd_h * q, axis=-1, keepdims=True)
                       + jnp.sum(f_mats[n], axis=-1, keepdims=True) - col_sums[n][:, 0:1] - e_vec)
                is_last = lax.broadcasted_iota(jnp.int32, (ck, 1), 0) == ck - 1
                dgc = dgc + jnp.where(is_last, jnp.sum(e_vec, axis=0, keepdims=True), 0.0)
                dgc_all[g] = dgc_all[g] + jnp.where(lane == h, dgc, 0.0)
                db_all[g] = db_all[g] + jnp.where(lane == nh + h, db, 0.0)
            for g in dgc_all:
                daux_ref[slice(g * ck, (g + 1) * ck), :] = _dot3(suffix, dgc_all[g]) + db_all[g]

    return pl.pallas_call(
        body, grid=(t // (gs * ck),),
        in_specs=_gdn_specs(t, (3 * gw, LANES)) + [pl.BlockSpec((gs, 16, ck), lambda i: (i, 0, 0))]
        + _gdn_specs(t, (sq, gw, gw, gw, gw, gw, gw, gw, gw, LANES)),
        out_specs=_gdn_specs(t, (3 * gw, LANES)),
        out_shape=[jax.ShapeDtypeStruct((t, 3 * gw), F32), jax.ShapeDtypeStruct((t, LANES), F32)],
        compiler_params=_cp(("parallel",)), name=name,
    )(qkvn, aux, aux_t, tinv, u, w, vn, do, dvn, dqd, dkd, dw, dlast)


def _gdn_pre_bwd1(xc, dqkvn, daux, ab, alog_pad, dt_pad, dkv, dp, dp_col, *, name, ab_blk=0, tm=256):
    t, cw = xc.shape
    hd, nh = GDN_HEAD_DIM, GDN_HEADS
    gw = nh * hd
    tm = _blk(t, tm)

    kvw = dkv.shape[1]
    seg = kvw + AB_PAD
    assert dp_col % seg == 0

    def body(xc_ref, dy_ref, daux_ref, ab_ref, al_ref, dt_ref, dkv_ref, _, dxc_ref, dab_ref, dal_ref, ddt_ref):
        i = pl.program_id(0)
        xc = xc_ref[...]
        sg = _sigmoid(xc)
        s = xc * sg
        dsilu = sg * (1.0 + xc * (1.0 - sg))
        for h in range(2 * nh):
            xh = s[:, h * hd:(h + 1) * hd]
            scale = hd ** -0.5 if h < nh else 1.0
            dyh = dy_ref[:, h * hd:(h + 1) * hd] * scale
            r = lax.rsqrt(jnp.sum(xh * xh, axis=-1, keepdims=True) + L2_EPS)
            dxh = r * dyh - xh * (r * r * r) * jnp.sum(dyh * xh, axis=-1, keepdims=True)
            dxc_ref[:, h * hd:(h + 1) * hd] = dxh * dsilu[:, h * hd:(h + 1) * hd]
        dxc_ref[:, 2 * gw:] = dy_ref[:, 2 * gw:] * dsilu[:, 2 * gw:]
        abv = ab_ref[...]
        dauxv = daux_ref[...]
        lane = lax.broadcasted_iota(jnp.int32, abv.shape, 1)
        is_a = lane < nh
        is_b = (lane >= nh) & (lane < 2 * nh)
        pre = abv + dt_ref[...]
        neg_ea = -jnp.exp(al_ref[...])
        d_a = jnp.where(is_a, dauxv * neg_ea * _sigmoid(pre), 0.0)
        beta = _sigmoid(abv)
        d_b = jnp.where(is_b, dauxv * beta * (1.0 - beta), 0.0)
        dab_ref[:, :kvw] = dkv_ref[...]
        dab_ref[:, kvw:kvw + LANES] = (d_a + d_b).astype(dab_ref.dtype)
        dab_ref[:, kvw + LANES:] = jnp.zeros((tm, AB_PAD - LANES), dab_ref.dtype)
        dal = jnp.sum(jnp.where(is_a, dauxv * neg_ea * _softplus(pre), 0.0), axis=0, keepdims=True)
        ddt = jnp.sum(d_a, axis=0, keepdims=True)

        @pl.when(i == 0)
        def _():
            dal_ref[...] = dal
            ddt_ref[...] = ddt

        @pl.when(i > 0)
        def _():
            dal_ref[...] += dal
            ddt_ref[...] += ddt

    row = lambda c: pl.BlockSpec((tm, c), lambda i: (i, 0))
    vec = pl.BlockSpec((1, LANES), lambda i: (0, 0))
    return pl.pallas_call(
        body, grid=(t // tm,),
        in_specs=[row(cw), row(cw), row(LANES), pl.BlockSpec((tm, LANES), lambda i: (i, ab_blk)), vec, vec, row(kvw),
                  ANY_SPEC],
        out_specs=[row(cw), pl.BlockSpec((tm, seg), lambda i: (i, dp_col // seg)), vec, vec],
        out_shape=[jax.ShapeDtypeStruct((t, cw), F32), jax.ShapeDtypeStruct(dp.shape, dp.dtype),
                   jax.ShapeDtypeStruct((1, LANES), F32), jax.ShapeDtypeStruct((1, LANES), F32)],
        input_output_aliases={7: 1}, compiler_params=_cp(("arbitrary",)), name=name,
    )(xc, dqkvn, daux, ab, alog_pad, dt_pad, dkv, dp)


def _gdn_pre_bwd2(dxc, qkvb, conv_w, dp, dp_col, *, name, tm=512):
    t, cw = dxc.shape
    tm = _blk(t, tm)
    hb = tm // HALO
    nblk = t // tm
    cg = GDN_HEADS * GDN_HEAD_DIM
    assert cw % cg == 0 and dp_col % cg == 0
    col0 = dp_col // cg

    def body(d_ref, dn_ref, x_ref, xp_ref, w_ref, _, dx_ref, dw_ref):
        i = pl.program_id(1)
        dcur = d_ref[...]
        dnxt = jnp.where(i < nblk - 1, dn_ref[...], 0.0)
        cur = x_ref[...]
        prev = jnp.where(i > 0, xp_ref[...], 0.0)
        dx = None
        dws = []
        for tap in range(GDN_CONV):
            j = GDN_CONV - 1 - tap
            term = w_ref[tap:tap + 1, :] * _shift_up(dcur, dnxt, j)
            dx = term if dx is None else dx + term
            dws.append(jnp.sum(dcur * _shift_down(cur, prev, j), axis=0, keepdims=True))
        dx_ref[...] = dx.astype(dx_ref.dtype)
        dw = jnp.concatenate(dws, axis=0)

        @pl.when(i == 0)
        def _():
            dw_ref[...] = dw

        @pl.when(i > 0)
        def _():
            dw_ref[...] += dw

    row = pl.BlockSpec((tm, cg), lambda c, i: (i, c))
    wsp = pl.BlockSpec((GDN_CONV, cg), lambda c, i: (0, c))
    return pl.pallas_call(
        body, grid=(cw // cg, nblk),
        in_specs=[row, pl.BlockSpec((HALO, cg), lambda c, i: (jnp.minimum((i + 1) * hb, t // HALO - 1), c)),
                  row, pl.BlockSpec((HALO, cg), lambda c, i: (jnp.maximum(i * hb - 1, 0), c)), wsp, ANY_SPEC],
        out_specs=[pl.BlockSpec((tm, cg), lambda c, i: (i, col0 + c)), wsp],
        out_shape=[jax.ShapeDtypeStruct(dp.shape, dp.dtype), jax.ShapeDtypeStruct((GDN_CONV, cw), F32)],
        input_output_aliases={5: 0}, compiler_params=_cp(("arbitrary", "arbitrary")), name=name,
    )(dxc, dxc, qkvb, qkvb, conv_w, dp)


def _gdn_post_fwd(o, z, norm_w, *, name, tm=512):
    t, gw = o.shape
    hd, nh = GDN_HEAD_DIM, GDN_HEADS
    tm = _blk(t, tm)

    def body(o_ref, z_ref, w_ref, y_ref):
        zv = z_ref[...]
        sz = zv * _sigmoid(zv)
        for h in range(nh):
            oh = o_ref[:, h * hd:(h + 1) * hd]
            r = lax.rsqrt(jnp.mean(oh * oh, axis=-1, keepdims=True) + RMS_EPS)
            y_ref[:, h * hd:(h + 1) * hd] = (oh * r * w_ref[...] * sz[:, h * hd:(h + 1) * hd]).astype(y_ref.dtype)

    row = pl.BlockSpec((tm, gw), lambda i: (i, 0))
    return pl.pallas_call(
        body, grid=(t // tm,), in_specs=[row, row, pl.BlockSpec((1, hd), lambda i: (0, 0))], out_specs=row,
        out_shape=jax.ShapeDtypeStruct((t, gw), BF16), compiler_params=_cp(("parallel",)), name=name,
    )(o, z, norm_w)


def _gdn_post_bwd(dy, o, z, norm_w, dp, dp_col, *, name, tm=512):
    t, gw = o.shape
    hd, nh = GDN_HEAD_DIM, GDN_HEADS
    tm = _blk(t, tm)

    def body(dy_ref, o_ref, z_ref, w_ref, _, do_ref, dz_ref, dw_ref):
        i = pl.program_id(0)
        zv = z_ref[...]
        sg = _sigmoid(zv)
        sz = zv * sg
        dsz = sg * (1.0 + zv * (1.0 - sg))
        dw = None
        for h in range(nh):
            sl = slice(h * hd, (h + 1) * hd)
            oh = o_ref[:, sl]
            dyh = dy_ref[:, sl].astype(F32)
            r = lax.rsqrt(jnp.mean(oh * oh, axis=-1, keepdims=True) + RMS_EPS)
            xh = oh * r
            dz_ref[:, sl] = (dyh * xh * w_ref[...] * dsz[:, sl]).astype(dz_ref.dtype)
            dn = dyh * sz[:, sl]
            dxh = dn * w_ref[...]
            do_ref[:, sl] = r * (dxh - xh * jnp.mean(dxh * xh, axis=-1, keepdims=True))
            part = jnp.sum(dn * xh, axis=0, keepdims=True)
            dw = part if dw is None else dw + part

        @pl.when(i == 0)
        def _():
            dw_ref[...] = dw

        @pl.when(i > 0)
        def _():
            dw_ref[...] += dw

    row = pl.BlockSpec((tm, gw), lambda i: (i, 0))
    vec = pl.BlockSpec((1, hd), lambda i: (0, 0))
    return pl.pallas_call(
        body, grid=(t // tm,), in_specs=[row, row, row, vec, ANY_SPEC],
        out_specs=[row, pl.BlockSpec((tm, gw), lambda i: (i, dp_col // gw)), vec],
        out_shape=[jax.ShapeDtypeStruct((t, gw), F32), jax.ShapeDtypeStruct(dp.shape, dp.dtype),
                   jax.ShapeDtypeStruct((1, hd), F32)],
        input_output_aliases={4: 1}, compiler_params=_cp(("arbitrary",)), name=name,
    )(dy, o, z, norm_w, dp)


IN_NAMES = ("q_a", "kv_a", "qkv_b", "ab", "z", "q_c", "gates")
CAT_NAMES = ("gates", "q_a", "qkv_b", "z", "q_c", "kv_a", "ab")
AB_PAD = 256


def _in_widths(d):
    gw = GDN_HEADS * GDN_HEAD_DIM
    return dict(q_a=SWA_Q_HEADS * SWA_HEAD_DIM, kv_a=2 * SWA_KV_HEADS * SWA_HEAD_DIM, qkv_b=3 * gw, ab=2 * GDN_HEADS,
                z=gw, q_c=XA_HEADS * XA_HEAD_DIM, gates=3 * d)


def _ranges(names, widths):
    out, start = {}, 0
    for k in names:
        out[k] = (start, widths[k])
        start += widths[k]
    return out, start


def _cat_ranges(d):
    widths = dict(_in_widths(d), ab=AB_PAD)
    return _ranges(CAT_NAMES, widths)


def _to_cat(shards, *, name="to_cat", tm=256):
    ns, d, n = shards.shape
    src, _ = _ranges(IN_NAMES, _in_widths(d))
    _, cat_w = _cat_ranges(d)
    pieces = []
    for k in CAT_NAMES:
        lo, hi = src[k][0], src[k][0] + src[k][1]
        for s in range(ns):
            a, b = max(lo, s * n), min(hi, (s + 1) * n)
            if a < b:
                pieces.append((s, a - s * n, b - s * n))
    tm = _blk(d, tm)

    def body(s_ref, o_ref):
        cols = [s_ref[s, :, a:b] for s, a, b in pieces]
        cols.append(jnp.zeros((tm, AB_PAD - src["ab"][1]), o_ref.dtype))
        o_ref[...] = jnp.concatenate(cols, axis=1)

    return pl.pallas_call(
        body, grid=(d // tm,),
        in_specs=[pl.BlockSpec((ns, tm, n), lambda i: (0, i, 0))],
        out_specs=pl.BlockSpec((tm, cat_w), lambda i: (i, 0)),
        out_shape=jax.ShapeDtypeStruct((d, cat_w), shards.dtype),
        compiler_params=_cp(("parallel",)), name=name,
    )(shards)


def _from_cat(w_cat, *, name="from_cat", tm=256):
    d, cat_w = w_cat.shape
    src, total = _ranges(IN_NAMES, _in_widths(d))
    cat, _ = _cat_ranges(d)
    n = total // N_SHARDS
    pieces = []
    for s in range(N_SHARDS):
        pieces.append([])
        for k in IN_NAMES:
            a, b = max(s * n, src[k][0]), min((s + 1) * n, src[k][0] + src[k][1])
            if a < b:
                pieces[s].append((cat[k][0] + a - src[k][0], cat[k][0] + b - src[k][0]))
    tm = _blk(d, tm)

    def body(c_ref, o_ref):
        for s in range(N_SHARDS):
            o_ref[s] = jnp.concatenate([c_ref[:, a:b] for a, b in pieces[s]], axis=1)

    return pl.pallas_call(
        body, grid=(d // tm,),
        in_specs=[pl.BlockSpec((tm, cat_w), lambda i: (i, 0))],
        out_specs=pl.BlockSpec((N_SHARDS, tm, n), lambda i: (0, i, 0)),
        out_shape=jax.ShapeDtypeStruct((N_SHARDS, d, n), w_cat.dtype),
        compiler_params=_cp(("parallel",)), name=name,
    )(w_cat)


def _pad_cols(a, width):
    return jnp.pad(a, ((0, 0), (0, width - a.shape[1])))


def _relu2_epilogue(acc):
    r = jnp.maximum(acc, 0.0)
    return acc, r * r


def _add_epilogue(acc, res):
    return (acc + res,)


def _drelu2_epilogue(acc, u):
    return (acc * (2.0 * jnp.maximum(u.astype(F32), 0.0)),)


def _local_step(x, mem, tgt, wts, small, comm=None):
    t, d = x.shape
    nh = GDN_HEADS
    cat, cat_w = _cat_ranges(d)
    alog_pad = _pad_cols(small["a_log"], LANES)
    dt_pad = _pad_cols(small["dt_bias"], LANES)
    kvw = cat["kv_a"][1]
    assert cat["ab"][0] == cat["kv_a"][0] + kvw
    ab_blk = kvw // LANES

    if comm is None:
        n = _rms_fwd(x, small["g_mix"], name="rms_mix")
        w_cat = wts["w_cat"]
    else:
        n, landed = _rms_fwd(x, small["g_mix"], name="rms_mix", exchange=comm.gather_exchange(["w_in"]))
        w_cat = _to_cat(_exchange_call(comm.pass_on(["w_in"], landed), name="ag_w_in_pass")[0])
    assert w_cat.shape == (d, cat_w)
    q_a = _mm(n, w_cat, b_window=cat["q_a"], out_dtypes=(BF16,), name="in_q_a")
    kv_a, ab = _mm(n, w_cat, b_window=(cat["kv_a"][0], kvw + AB_PAD), out_dtypes=(BF16, F32), name="in_kv_ab")
    qkvb = _mm(n, w_cat, b_window=cat["qkv_b"], tn=512, name="in_qkv_b")
    z = _mm(n, w_cat, b_window=cat["z"], name="in_z")
    q_c = _mm(n, w_cat, b_window=cat["q_c"], out_dtypes=(BF16,), name="in_q_c")
    if comm is None:
        gates = _mm(n, w_cat, b_window=cat["gates"], name="in_gates")
        y_a, lse = _swa_fwd(q_a, kv_a, small["sinks"], name="swa_fwd")
    else:
        gates, landed_mlp = _mm(n, w_cat, b_window=cat["gates"], name="in_gates",
                                exchange=comm.gather_exchange(comm.MLP[1:]))
        y_a, lse, landed = _swa_fwd(q_a, kv_a, small["sinks"], name="swa_fwd", exchange=comm.gather_exchange(comm.MLP[:1]))
        landed_mlp = landed + landed_mlp
    xc, qkvn, aux = _gdn_pre_fwd(qkvb, small["conv_w"], ab, alog_pad, dt_pad, ab_blk=ab_blk, name="gdn_pre_fwd")
    aux_t = aux[:, :16].reshape(t // GDN_CHUNK, GDN_CHUNK, 16).transpose(0, 2, 1)
    if comm is None:
        gdn_u, gdn_w, gdn_qd, gdn_kd, gdn_mm, gdn_tinv = _gdn_local_fwd(qkvn, aux, aux_t, name="gdn_local_fwd")
    else:
        gdn_u, gdn_w, gdn_qd, gdn_kd, gdn_mm, gdn_tinv, landed_mid = _gdn_local_fwd(
            qkvn, aux, aux_t, name="gdn_local_fwd", exchange=comm.gather_exchange(comm.mid))
        wts = dict(wts, **comm.gathered(comm.mid, landed_mid, "mid"))
    o_b, gdn_vn, s_all = _gdn_seq_fwd(gdn_u, gdn_w, gdn_qd, gdn_kd, gdn_mm, aux, name="gdn_seq_fwd")
    y_b = _gdn_post_fwd(o_b, z, small["gdn_norm_w"], name="gdn_post_fwd")
    nmem = _rms_fwd(mem, small["g_mem"], name="rms_mem")
    mkv = _mm(nmem, wts["w_mem_kv"], out_dtypes=(BF16,), name="mem_kv")
    y_c = _xa_fwd(q_c, mkv, name="xa_fwd")
    ys = (y_a, y_b, y_c)
    w_ups = (wts["w_swa_up"], wts["w_gdn_up"], wts["w_xa_up"])
    merged = _merge_fwd(ys, w_ups, gates, name="merge_fwd")
    if comm is None:
        h1 = _mm(merged, wts["w_out"], extras=(x,), epilogue=_add_epilogue, name="out_proj")
    else:
        h1, whole = _mm(merged, wts["w_out"], extras=(x,), epilogue=_add_epilogue, name="out_proj",
                        exchange=comm.pass_on(comm.MLP, landed_mlp))
        wts = dict(wts, **comm.as_weights(comm.MLP, whole))
    n2 = _rms_fwd(h1, small["g_mlp"], name="rms_mlp")
    u, act = _mm(n2, wts["w_mlp_in"], b_sharded=True, out_dtypes=(BF16, BF16), epilogue=_relu2_epilogue, name="mlp_in")
    h2 = _mm(act, wts["w_mlp_out"], extras=(h1,), epilogue=_add_epilogue, name="mlp_out")
    dh2, dh2_b, dg_final, loss = _final_loss(h2, small["g_final"], tgt, name="final_loss")

    grads = {"g_final": dg_final}
    du = _mm(dh2_b, wts["w_mlp_out"], tb=True, out_dtypes=(BF16,), extras=(u,), epilogue=_drelu2_epilogue, name="d_mlp_act")
    grads["w_mlp_out"] = _mm(act, dh2_b, ta=True, out_dtypes=(BF16,), name="dw_mlp_out")
    grads["w_mlp_in"] = _mm(n2, du, ta=True, out_sharded=True, out_dtypes=(BF16,), name="dw_mlp_in")
    if comm is None:
        dn2 = _mm(du, wts["w_mlp_in"], tb=True, b_sharded=True, name="d_mlp_in")
    else:
        g_mlp = [comm.shard_major(k, grads.pop(k)) for k in comm.MLP]
        dn2, sib_mlp = _mm(du, wts["w_mlp_in"], tb=True, b_sharded=True, name="d_mlp_in", exchange=_sibling_halves(g_mlp))
        s1_mlp = comm.pair_sums(g_mlp, "mlp", sib_mlp)
    dh1, dh1_b, grads["g_mlp"] = _rms_bwd(dn2, h1, small["g_mlp"], dh2, name="rms_mlp_bwd")
    dmerged = _mm(dh1_b, wts["w_out"], tb=True, name="d_out_proj")
    grads["w_out"] = _mm(merged, dh1_b, ta=True, out_dtypes=(BF16,), name="dw_out")
    *dus, dp = _merge_bwd(ys, w_ups, gates, dmerged, cat_w, name="merge_bwd")
    dys = []
    for y, du_i, w_up, key in zip(ys, dus, w_ups, ("w_swa_up", "w_gdn_up", "w_xa_up")):
        dys.append(_mm(du_i, w_up, tb=True, b_sharded=True, out_dtypes=(BF16,), name="d_" + key))
        grads[key] = _mm(y, du_i, ta=True, out_sharded=True, out_dtypes=(BF16,), name="dw_" + key[2:])
    dp, dkv_a, grads["sinks"] = _swa_bwd(q_a, kv_a, small["sinks"], y_a, lse, dys[0], dp, cat["q_a"][0], name="swa_bwd")
    do_b, dp, grads["gdn_norm_w"] = _gdn_post_bwd(dys[1], o_b, z, small["gdn_norm_w"], dp, cat["z"][0],
                                                  name="gdn_post_bwd")
    dvn, dqd, dkd, dw_, dlast = _gdn_seq_bwd(do_b, gdn_w, gdn_qd, gdn_kd, gdn_mm, gdn_vn, s_all, aux, name="gdn_seq_bwd")
    dqkvn, daux = _gdn_local_bwd(qkvn, aux, aux_t, gdn_tinv, gdn_u, gdn_w, gdn_vn, do_b, dvn, dqd, dkd, dw_, dlast,
                                 name="gdn_local_bwd")
    dxc, dp, dalog, ddt = _gdn_pre_bwd1(xc, dqkvn, daux, ab, alog_pad, dt_pad, dkv_a, dp, cat["kv_a"][0], ab_blk=ab_blk,
                                        name="gdn_pre_bwd1")
    grads["a_log"], grads["dt_bias"] = dalog[:, :nh], ddt[:, :nh]
    dp, grads["conv_w"] = _gdn_pre_bwd2(dxc, qkvb, small["conv_w"], dp, cat["qkv_b"][0], name="gdn_pre_bwd2")
    dp, dmkv = _xa_bwd(q_c, mkv, dys[2], dp, cat["q_c"][0], name="xa_bwd")
    grads["w_mem_kv"] = _mm(nmem, dmkv, ta=True, out_dtypes=(BF16,), name="dw_mem_kv")
    dnmem = _mm(dmkv, wts["w_mem_kv"], tb=True, name="d_mem_kv")
    _, _, grads["g_mem"] = _rms_bwd(dnmem, mem, small["g_mem"], jnp.zeros_like(mem), name="rms_mem_bwd")
    if comm is None:
        grads["w_cat"] = _mm(n, dp, ta=True, out_dtypes=(BF16,), name="dw_in")
        dn = _mm(dp, w_cat, tb=True, name="d_in_proj")
    else:
        s1_mid = comm.pair_sums([comm.shard_major(k, grads.pop(k)) for k in comm.mid], "mid")
        dw_cat, rcv_mlp = _mm(n, dp, ta=True, out_dtypes=(BF16,), name="dw_in", exchange=_chip_exchange(s1_mlp))
        s1_in = comm.pair_sums([_from_cat(dw_cat)], "in")
        dn, rcv_rest = _mm(dp, w_cat, tb=True, name="d_in_proj", exchange=_chip_exchange(s1_in + s1_mid))
        halves = comm.chip_sums(s1_in + s1_mid + s1_mlp, rcv_rest + rcv_mlp)
        reduced = _exchange_call(_join_halves(halves), name="rs_join_halves")
        grads.update(zip(["w_in"] + comm.mid + list(comm.MLP), reduced))
    dx, _, grads["g_mix"] = _rms_bwd(dn, x, small["g_mix"], dh1, name="rms_mix_bwd")
    return loss, dx, grads


HBM_SPEC = pl.BlockSpec(memory_space=pltpu.HBM)
VMEM_SPEC = pl.BlockSpec(memory_space=pltpu.VMEM)
N_CHIPS = N_SHARDS
N_DEV = 8
DMA_CHUNK_BYTES = 1 << 20


def _place():
    return lax.axis_index("x"), lax.axis_index("y"), lax.axis_index("c")


def _other_chips(x, y):
    return [(1 - x, y), (x, 1 - y), (1 - x, 1 - y)]


def _n_chunks(rows, row_bytes):
    n = 1
    while rows % (2 * n) == 0 and (rows // (2 * n)) % 16 == 0 and (rows // n) * row_bytes > DMA_CHUNK_BYTES:
        n *= 2
    return n


def _sem_scratch(n_remote, n_local):
    return [pltpu.SemaphoreType.DMA((max(n_remote, 1),)), pltpu.SemaphoreType.DMA((max(n_remote, 1),)),
            pltpu.SemaphoreType.DMA((max(n_local, 1),))]


def _gather_over_ici(shards):
    plan = _half_chunks(shards, 0)

    def copies_of(in_refs, out_refs, place):
        x, y, c = place
        remote = []
        for i, r0, nr in plan:
            mine = pl.ds(c * (shards[i].shape[0] // 2) + r0, nr)
            for chip in _other_chips(x, y):
                remote.append((in_refs[i].at[mine], out_refs[i].at[2 * x + y, mine], (*chip, c)))
        return remote, []

    shapes = tuple(jax.ShapeDtypeStruct((N_CHIPS, *s.shape), s.dtype) for s in shards)
    return Exchange(tuple(shards), shapes, 3 * len(plan), 0, copies_of)


def _gather_pass_on(arrived):
    plan = _half_chunks([jax.ShapeDtypeStruct(a.shape[1:], a.dtype) for a in arrived], 0)

    def copies_of(in_refs, out_refs, place):
        x, y, c = place
        remote = []
        for i, r0, nr in plan:
            mine = pl.ds(c * (arrived[i].shape[1] // 2) + r0, nr)
            for chip in _other_chips(x, y):
                rows = out_refs[i].at[2 * chip[0] + chip[1], mine]
                remote.append((rows, rows, (x, y, 1 - c)))
        return remote, []

    shapes = tuple(jax.ShapeDtypeStruct(a.shape, a.dtype) for a in arrived)
    return Exchange(tuple(arrived), shapes, 3 * len(plan), 0, copies_of, tuple((i, i) for i in range(len(arrived))))


def _place_own(arrived, shard, chip, *, name):
    r, c = shard.shape
    tb = _row_block(r, c)

    def body(chip_ref, s_ref, _, o_ref):
        o_ref[...] = s_ref[...]

    return pl.pallas_call(
        body, grid_spec=pltpu.PrefetchScalarGridSpec(
            num_scalar_prefetch=1, grid=(r // tb,),
            in_specs=[pl.BlockSpec((tb, c), lambda i, chip_ref: (i, 0)), ANY_SPEC],
            out_specs=pl.BlockSpec((None, tb, c), lambda i, chip_ref: (chip_ref[0], i, 0))),
        out_shape=jax.ShapeDtypeStruct(arrived.shape, arrived.dtype), input_output_aliases={2: 0},
        compiler_params=_cp(("parallel",)), name=name,
    )(chip, shard, arrived)


def _exchange_call(ex, *, name):
    n_in, n_out = len(ex.ins), len(ex.out_shapes)

    def body(*refs):
        cps = _exchange_copies(ex, refs[:n_in], refs[n_in:n_in + n_out], refs[n_in + n_out:])
        for cp in cps:
            cp.start()
        for cp in cps:
            cp.wait()

    return pl.pallas_call(
        body, out_shape=list(ex.out_shapes), in_specs=[HBM_SPEC] * n_in, out_specs=[HBM_SPEC] * n_out,
        scratch_shapes=_sem_scratch(ex.n_remote, ex.n_local), input_output_aliases=dict(ex.aliases), name=name,
    )(*ex.ins)


def _half_chunks(arrs, row_axis):
    plan = []
    for i, a in enumerate(arrs):
        rh = a.shape[row_axis] // 2
        row_bytes = a.dtype.itemsize * math.prod(a.shape) // a.shape[row_axis]
        nch = _n_chunks(rh, row_bytes)
        plan += [(i, q * (rh // nch), rh // nch) for q in range(nch)]
    return plan


def _sibling_halves(gs):
    plan = _half_chunks(gs, 1)

    def copies_of(in_refs, out_refs, place):
        x, y, c = place
        out = []
        for i, r0, nr in plan:
            rh = gs[i].shape[1] // 2
            out.append((in_refs[i].at[:, pl.ds((1 - c) * rh + r0, nr), :], out_refs[i].at[:, pl.ds(r0, nr), :],
                        (x, y, 1 - c)))
        return out, []

    shapes = tuple(jax.ShapeDtypeStruct((g.shape[0], g.shape[1] // 2, g.shape[2]), g.dtype) for g in gs)
    return Exchange(tuple(gs), shapes, len(plan), 0, copies_of)


def _chip_exchange(s1s):
    plan = _half_chunks([jax.ShapeDtypeStruct((2 * s.shape[1], s.shape[2]), s.dtype) for s in s1s], 0)

    def copies_of(in_refs, out_refs, place):
        x, y, c = place
        out = []
        for i, r0, nr in plan:
            for j, chip in enumerate(_other_chips(x, y)):
                out.append((in_refs[i].at[2 * chip[0] + chip[1], pl.ds(r0, nr), :], out_refs[i].at[j, pl.ds(r0, nr), :],
                            (*chip, c)))
        return out, []

    shapes = tuple(jax.ShapeDtypeStruct((3, *s.shape[1:]), s.dtype) for s in s1s)
    return Exchange(tuple(s1s), shapes, 3 * len(plan), 0, copies_of)


def _join_halves(gs):
    plan = _half_chunks(gs, 0)

    def copies_of(in_refs, out_refs, place):
        x, y, c = place
        out = []
        for i, r0, nr in plan:
            rows = out_refs[i].at[pl.ds(c * (gs[i].shape[0] // 2) + r0, nr), :]
            out.append((rows, rows, (x, y, 1 - c)))
        return out, []

    shapes = tuple(jax.ShapeDtypeStruct(g.shape, g.dtype) for g in gs)
    aliases = tuple((i, i) for i in range(len(gs)))
    return Exchange(tuple(gs), shapes, len(plan), 0, copies_of, aliases)


def _row_block(rows, cols):
    tb = rows
    while tb % 32 == 0 and tb * cols * 4 > (2 << 20):
        tb //= 2
    return tb


def _pair_sum(g, sib, core, *, name):
    ns, r, c = g.shape
    rh = r // 2
    tb = _row_block(rh, c)
    nb = rh // tb

    def body(core_ref, g_ref, s_ref, o_ref):
        o_ref[...] = (g_ref[...].astype(F32) + s_ref[...].astype(F32)).astype(o_ref.dtype)

    mine = pl.BlockSpec((None, tb, c), lambda s, i, core_ref: (s, core_ref[0] * nb + i, 0))
    half = pl.BlockSpec((None, tb, c), lambda s, i, core_ref: (s, i, 0))
    return pl.pallas_call(
        body, grid_spec=pltpu.PrefetchScalarGridSpec(num_scalar_prefetch=1, grid=(ns, nb), in_specs=[mine, half],
                                                     out_specs=half),
        out_shape=jax.ShapeDtypeStruct((ns, rh, c), BF16), compiler_params=_cp(("parallel", "parallel")), name=name,
    )(core, g, sib)


def _chip_sum(s1, rcv, where, *, name):
    _, rh, c = s1.shape
    tb = _row_block(rh, c)
    nb = rh // tb

    def body(where_ref, own_ref, r0_ref, r1_ref, r2_ref, o_ref):
        acc = own_ref[...].astype(F32)
        for r in (r0_ref, r1_ref, r2_ref):
            acc = acc + r[...].astype(F32)
        o_ref[...] = acc

    own = pl.BlockSpec((None, tb, c), lambda i, w: (w[1], i, 0))
    got = [pl.BlockSpec((None, tb, c), functools.partial(lambda i, w, j: (j, i, 0), j=j)) for j in range(3)]
    return pl.pallas_call(
        body, grid_spec=pltpu.PrefetchScalarGridSpec(
            num_scalar_prefetch=1, grid=(nb,), in_specs=[own] + got,
            out_specs=pl.BlockSpec((tb, c), lambda i, w: (w[0] * nb + i, 0))),
        out_shape=jax.ShapeDtypeStruct((2 * rh, c), F32), compiler_params=_cp(("parallel",)), name=name,
    )(where, s1, rcv, rcv, rcv)


def _all_gather_small(blk, *, name):
    r = blk.shape[0]

    def body(b_ref, out_ref, send_sems, recv_sems):
        x, y, c = _place()
        me = 4 * x + 2 * y + c
        out_ref[me] = b_ref[...]
        sends = []
        for k in range(1, N_DEV):
            peer = (x ^ (k >> 2), y ^ ((k >> 1) & 1), c ^ (k & 1))
            sends.append(pltpu.make_async_remote_copy(src_ref=b_ref, dst_ref=out_ref.at[me], send_sem=send_sems.at[k - 1],
                                                      recv_sem=recv_sems.at[k - 1], device_id=peer, device_id_type=MESH))
        for cp in sends:
            cp.start()
        for k in range(1, N_DEV):
            rows = out_ref.at[me ^ k]
            pltpu.make_async_remote_copy(src_ref=rows, dst_ref=rows, send_sem=send_sems.at[k - 1],
                                         recv_sem=recv_sems.at[k - 1], device_id=(x, y, c), device_id_type=MESH).wait_recv()
        for cp in sends:
            cp.wait_send()

    return pl.pallas_call(
        body, out_shape=jax.ShapeDtypeStruct((N_DEV, r, LANES), blk.dtype), in_specs=[VMEM_SPEC], out_specs=VMEM_SPEC,
        scratch_shapes=[pltpu.SemaphoreType.DMA((N_DEV - 1,)), pltpu.SemaphoreType.DMA((N_DEV - 1,))],
        name=name,
    )(blk)


def _sum_rows(parts, out_dtype, *, name, tb=1024):
    rows = parts[0].shape[0]
    tb = _blk(rows, tb)

    def body(*refs):
        acc = refs[0][...].astype(F32)
        for r in refs[1:-1]:
            acc = acc + r[...].astype(F32)
        refs[-1][...] = acc.astype(refs[-1].dtype)

    spec = pl.BlockSpec((tb, LANES), lambda i: (i, 0))
    return pl.pallas_call(
        body, grid=(rows // tb,), in_specs=[spec] * len(parts), out_specs=spec,
        out_shape=jax.ShapeDtypeStruct((rows, LANES), out_dtype), compiler_params=_cp(("parallel",)), name=name,
    )(*parts)


BIG = (
    ("w_in", 1), ("w_mem_kv", 0), ("w_swa_up", 1), ("w_gdn_up", 1), ("w_xa_up", 1), ("w_out", 0), ("w_mlp_in", 1),
    ("w_mlp_out", 0))


class _Comm:
    MLP = ("w_mlp_in", "w_mlp_out")

    def __init__(self, late_shards, core, where):
        self.axis = dict(BIG)
        self.late_shards = late_shards
        self.mid = [k for k in late_shards if k not in self.MLP and k != "w_in"]
        self.core, self.where = core, where

    def gather_exchange(self, names):
        return _gather_over_ici([self.late_shards[k] for k in names])

    def as_weights(self, names, whole):
        return {k: (g.reshape(-1, g.shape[2]) if self.axis[k] == 0 else g) for k, g in zip(names, whole)}

    def gathered(self, names, landed, tag):
        return self.as_weights(names, _exchange_call(self.pass_on(names, landed), name=f"ag_{tag}_pass"))

    def pass_on(self, names, landed):
        chip = self.where[1:2]
        return _gather_pass_on([_place_own(a, self.late_shards[k], chip, name=f"ag_own_{k}") for k, a in zip(names, landed)])

    def shard_major(self, k, grad):
        return grad.reshape(N_CHIPS, -1, grad.shape[-1]) if self.axis[k] == 0 else grad

    def pair_sums(self, gs, tag, sibs=None):
        if sibs is None:
            sibs = _exchange_call(_sibling_halves(gs), name=f"rs_sibling_{tag}")
        return [_pair_sum(g, s, self.core, name=f"rs_pair_sum_{tag}{i}") for i, (g, s) in enumerate(zip(gs, sibs))]

    def chip_sums(self, s1s, rcvs):
        return [_chip_sum(s1, rcv, self.where, name=f"rs_chip_sum_{i}") for i, (s1, rcv) in enumerate(zip(s1s, rcvs))]
SMALL = ("g_mix", "sinks", "a_log", "dt_bias", "gdn_norm_w", "g_mem", "g_mlp", "g_final")


def _rows128(a, rows):
    flat = a.reshape(-1)
    return jnp.pad(flat, (0, rows * LANES - flat.shape[0])).reshape(rows, LANES)


def kernel(x, mem, g_mix, w_in, sinks, conv_w, a_log, dt_bias, gdn_norm_w, g_mem, w_mem_kv, w_swa_up, w_gdn_up, w_xa_up, w_out, g_mlp, w_mlp_in, w_mlp_out, g_final, loss_target, m_g_mix, m_w_in, m_sinks, m_conv_w, m_a_log, m_dt_bias, m_gdn_norm_w, m_g_mem, m_w_mem_kv, m_w_swa_up, m_w_gdn_up, m_w_xa_up, m_w_out, m_g_mlp, m_w_mlp_in, m_w_mlp_out, m_g_final, v_g_mix, v_w_in, v_sinks, v_conv_w, v_a_log, v_dt_bias, v_gdn_norm_w, v_g_mem, v_w_mem_kv, v_w_swa_up, v_w_gdn_up, v_w_xa_up, v_w_out, v_g_mlp, v_w_mlp_in, v_w_mlp_out, v_g_final):
    given = dict(locals())
    xi, yi, ci = _place()
    chip = 2 * xi + yi
    core = jnp.reshape(ci, (1,)).astype(jnp.int32)
    where = jnp.stack([ci, chip]).astype(jnp.int32)

    comm = _Comm({k: given[k][0].astype(BF16) for k, _ in BIG}, core, where)
    wts = {}
    conv_shard = conv_w[0]
    conv_rows = -(-conv_shard.size // (8 * LANES)) * 8
    conv_all = _all_gather_small(_rows128(conv_shard, conv_rows), name="ag_conv")
    conv_full = jnp.concatenate(
        [conv_all[2 * s].reshape(-1)[:conv_shard.size].reshape(conv_shard.shape) for s in range(N_CHIPS)], axis=1)

    small = {k: given[k].reshape(1, -1) for k in SMALL}
    small["conv_w"] = conv_full
    loss_row, dx, grads = _local_step(x[0], mem[0], loss_target[0], wts, small, comm)
    big_grads = {k: grads[k] for k, _ in BIG}

    layout = [("loss", loss_row[:, :1])] + [(k, grads[k]) for k in SMALL] + [("conv_w", grads["conv_w"])]
    rows = [-(-a.size // LANES) for _, a in layout]
    blk_rows = -(-sum(rows) // 8) * 8
    blk = jnp.concatenate([_rows128(a.astype(F32), n) for (_, a), n in zip(layout, rows)]
                          + [jnp.zeros((blk_rows - sum(rows), LANES), F32)], axis=0)
    gathered = _all_gather_small(blk, name="ag_small_grads")
    reduced = _sum_rows([gathered[i] for i in range(N_DEV)], F32, name="small_grad_sum")
    small_grads, start = {}, 0
    for (k, a), n in zip(layout, rows):
        small_grads[k] = reduced[start:start + n].reshape(-1)[:a.size].reshape(a.shape)
        start += n
    loss = small_grads["loss"].reshape(())
    cw = conv_shard.shape[1]
    conv_grad = lax.dynamic_slice_in_dim(small_grads["conv_w"], chip * cw, cw, axis=1)

    names = ["g_mix", "w_in", "sinks", "conv_w", "a_log", "dt_bias", "gdn_norm_w", "g_mem", "w_mem_kv", "w_swa_up",
             "w_gdn_up", "w_xa_up", "w_out", "g_mlp", "w_mlp_in", "w_mlp_out", "g_final"]
    out_g, out_d, out_m, out_v = [], [], [], []
    for k in names:
        w, m, v = given[k], given["m_" + k], given["v_" + k]
        if k in big_grads:
            g2 = big_grads[k]
        elif k == "conv_w":
            g2 = conv_grad
        else:
            g2 = small_grads[k]
        as_given = (lambda a: a.reshape(1, -1)) if w.ndim == 1 else (lambda a: a)
        if w.shape[-1] % LANES and w.shape[-1] > LANES:
            tr = lambda a: jnp.swapaxes(a, -1, -2)
            g_out, delta, new_m, new_v = (tr(a) for a in _adamw(tr(w), tr(g2), tr(m), tr(v), name="adamw_" + k))
        else:
            g_out, delta, new_m, new_v = _adamw(as_given(w), g2, as_given(m), as_given(v), name="adamw_" + k)
        out_g.append(g_out.reshape(w.shape))
        out_d.append(delta.reshape(w.shape))
        out_m.append(new_m.reshape(w.shape))
        out_v.append(new_v.reshape(w.shape))
    return (loss, dx[None], *out_g, *out_d, *out_m, *out_v)
```

```python
import functools
import math
from typing import Callable, NamedTuple

import jax
import jax.numpy as jnp
from jax import lax
from jax.experimental import pallas as pl
from jax.experimental.pallas import tpu as pltpu

F32 = jnp.float32
BF16 = jnp.bfloat16
HI = lax.Precision.HIGHEST
MESH = pl.DeviceIdType.MESH

SWA_Q_HEADS = 16
SWA_KV_HEADS = 2
SWA_HEAD_DIM = 64
SWA_WINDOW = 128
SWA_SCALE = SWA_HEAD_DIM ** -0.5
assert math.frexp(SWA_SCALE)[0] == 0.5
GDN_HEADS = 4
GDN_HEAD_DIM = 128
GDN_CONV = 4
GDN_CHUNK = 64
XA_HEADS = 4
XA_HEAD_DIM = 128
RMS_EPS = 1e-6
L2_EPS = 1e-6
ADAM_LR = 0.001
ADAM_B1 = 0.9
ADAM_B2 = 0.999
ADAM_EPS = 1e-08
ADAM_WD = 0.01
ADAM_STEP = 10

LANES = 128
N_SHARDS = 4
VMEM_LIMIT = 56 * 1024 * 1024

NT = (((1,), (1,)), ((), ()))
TN = (((0,), (0,)), ((), ()))
NN = (((1,), (0,)), ((), ()))


def _cp(sem=None):
    return pltpu.CompilerParams(dimension_semantics=sem, vmem_limit_bytes=VMEM_LIMIT)


def _blk(dim, pref):
    if dim <= pref:
        return dim
    b = (pref // LANES) * LANES
    while dim % b:
        b -= LANES
    assert b > 0, (dim, pref)
    return b


def _dot(a, b, dims=NN, precision=None):
    return lax.dot_general(a, b, dims, precision=precision, preferred_element_type=F32)


def _sigmoid(x):
    return 0.5 * jnp.tanh(0.5 * x) + 0.5


MM_TK_BYTES = 4096


def _mm(a, b, *, name, ta=False, tb=False, out_dtypes=(F32,), epilogue=None, extras=(), tm=1024, tn=1024, tk=None,
        b_sharded=False, out_sharded=False, b_window=None, exchange=None):
    (kdim, m) = a.shape if ta else a.shape[::-1]
    col0 = 0
    n_lim = k_lim = None
    if b_sharded:
        ns, rows_w, per = b.shape
        if tb:
            kb, n, k_lim = ns * per, rows_w, per
        else:
            kb, n, n_lim = rows_w, ns * per, per
    else:
        (kb, n) = b.shape[::-1] if tb else b.shape
        if b_window is not None:
            assert not tb
            col0, n = b_window
    assert kdim == kb, (a.shape, b.shape, ta, tb)
    if out_sharded:
        assert n % N_SHARDS == 0
        n_lim = n // N_SHARDS if n_lim is None else n_lim
        assert n_lim == n // N_SHARDS
    if tk is None:
        tk = MM_TK_BYTES // max(a.dtype.itemsize, b.dtype.itemsize)
    tm, tn, tk = _blk(m, tm), _blk(n_lim or n, tn), _blk(k_lim or kdim, tk)
    assert col0 % tn == 0, (col0, tn)
    nk = kdim // tk
    a_spec = pl.BlockSpec((tk, tm), lambda i, j, k: (k, i)) if ta else pl.BlockSpec((tm, tk), lambda i, j, k: (i, k))
    if b_sharded and tb:
        kpb = k_lim // tk
        b_spec = pl.BlockSpec((None, tn, tk), lambda i, j, k: (k // kpb, j, k % kpb))
    elif b_sharded:
        bpb = n_lim // tn
        b_spec = pl.BlockSpec((None, tk, tn), lambda i, j, k: (j // bpb, k, j % bpb))
    elif tb:
        b_spec = pl.BlockSpec((tn, tk), lambda i, j, k: (j, k))
    else:
        b_spec = pl.BlockSpec((tk, tn), lambda i, j, k: (k, j + col0 // tn))
    x_spec = pl.BlockSpec((tm, tn), lambda i, j, k: (i, j))
    if out_sharded:
        opb = n_lim // tn
        o_spec = pl.BlockSpec((None, tm, tn), lambda i, j, k: (j // opb, i, j % opb))
        out_shape = (N_SHARDS, m, n_lim)
    else:
        o_spec, out_shape = x_spec, (m, n)
    dims = ((((0 if ta else 1),), ((1 if tb else 0),)), ((), ()))
    n_extra, n_out = len(extras), len(out_dtypes)

    host = _ExchangeHost(exchange)
    grid = (m // tm, n // tn, nk)

    def body(*refs):
        a_ref, b_ref = refs[:2]
        extra_refs = refs[2:2 + n_extra]
        out_refs = refs[2 + n_extra + host.n_in:2 + n_extra + host.n_in + n_out]
        host.start(refs, 2 + n_extra, 2 + n_extra + host.n_in + n_out, grid)
        part = _dot(a_ref[...].astype(BF16), b_ref[...].astype(BF16), dims)

        def finish(acc):
            vals = epilogue(acc, *[r[...] for r in extra_refs]) if epilogue is not None else (acc,) * n_out
            assert len(vals) == n_out
            for r, v in zip(out_refs, vals):
                r[...] = v.astype(r.dtype)

        if nk == 1:
            finish(part)
        else:
            acc_ref = refs[2 + n_extra + host.n_in + n_out + host.n_out]
            k = pl.program_id(2)

            @pl.when(k == 0)
            def _():
                acc_ref[...] = part

            @pl.when((k > 0) & (k < nk - 1))
            def _():
                acc_ref[...] += part

            @pl.when(k == nk - 1)
            def _():
                finish(acc_ref[...] + part)

        host.wait(refs, 2 + n_extra, 2 + n_extra + host.n_in + n_out, grid)

    outs = pl.pallas_call(
        body,
        grid=grid,
        in_specs=[a_spec, b_spec] + [x_spec] * n_extra + host.in_specs,
        out_specs=[o_spec] * n_out + host.out_specs,
        out_shape=[jax.ShapeDtypeStruct(out_shape, d) for d in out_dtypes] + host.out_shapes,
        scratch_shapes=([pltpu.VMEM((tm, tn), F32)] if nk > 1 else []) + host.scratch,
        input_output_aliases=host.aliases(2 + n_extra, n_out),
        compiler_params=_cp(host.semantics(("parallel", "parallel", "arbitrary"))),
        name=name,
    )(a, b, *extras, *host.ins)
    mine, landed = outs[:n_out], list(outs[n_out:])
    mine = mine[0] if n_out == 1 else mine
    return (mine, landed) if exchange is not None else mine


class Exchange(NamedTuple):
    ins: tuple
    out_shapes: tuple
    n_remote: int
    n_local: int
    copies_of: Callable
    aliases: tuple = ()


def _exchange_copies(ex, in_refs, out_refs, sem_refs):
    send_sems, recv_sems, local_sems = sem_refs
    remote, local = ex.copies_of(in_refs, out_refs, _place())
    assert len(remote) == ex.n_remote and len(local) == ex.n_local, (len(remote), len(local))
    cps = [pltpu.make_async_remote_copy(src_ref=src, dst_ref=dst, send_sem=send_sems.at[k], recv_sem=recv_sems.at[k],
                                        device_id=to, device_id_type=MESH) for k, (src, dst, to) in enumerate(remote)]
    cps += [pltpu.make_async_copy(src, dst, local_sems.at[k]) for k, (src, dst) in enumerate(local)]
    return cps


class _ExchangeHost:
    def __init__(self, ex):
        self.ex = ex
        self.ins = list(ex.ins) if ex else []
        self.out_shapes = list(ex.out_shapes) if ex else []
        self.n_in, self.n_out = len(self.ins), len(self.out_shapes)
        self.in_specs = [HBM_SPEC] * self.n_in
        self.out_specs = [HBM_SPEC] * self.n_out
        self.scratch = _sem_scratch(ex.n_remote, ex.n_local) if ex else []

    def semantics(self, sem):
        return tuple("arbitrary" for _ in sem) if self.ex else sem

    def aliases(self, in_at, out_at):
        return {in_at + i: out_at + o for i, o in self.ex.aliases} if self.ex else {}

    def _refs(self, refs, in_at, out_at):
        return refs[in_at:in_at + self.n_in], refs[out_at:out_at + self.n_out], refs[len(refs) - 3:]

    def _when(self, grid, last):
        cond = None
        for d, size in enumerate(grid):
            c = pl.program_id(d) == (size - 1 if last else 0)
            cond = c if cond is None else cond & c
        return cond

    def start(self, refs, in_at, out_at, grid):
        if self.ex:
            @pl.when(self._when(grid, False))
            def _():
                for cp in _exchange_copies(self.ex, *self._refs(refs, in_at, out_at)):
                    cp.start()

    def wait(self, refs, in_at, out_at, grid):
        if self.ex:
            @pl.when(self._when(grid, True))
            def _():
                for cp in _exchange_copies(self.ex, *self._refs(refs, in_at, out_at)):
                    cp.wait()


def _rms_fwd(x, g, *, name, tm=512, exchange=None):
    t, d = x.shape
    tm = _blk(t, tm)
    host = _ExchangeHost(exchange)
    grid = (t // tm,)

    def body(*refs):
        x_ref, g_ref, n_ref = refs[0], refs[1], refs[2 + host.n_in]
        host.start(refs, 2, 3 + host.n_in, grid)
        xv = x_ref[...]
        r = lax.rsqrt(jnp.mean(xv * xv, axis=-1, keepdims=True) + RMS_EPS)
        n_ref[...] = (xv * r * g_ref[...]).astype(n_ref.dtype)
        host.wait(refs, 2, 3 + host.n_in, grid)

    outs = pl.pallas_call(
        body, grid=grid,
        in_specs=[pl.BlockSpec((tm, d), lambda i: (i, 0)), pl.BlockSpec((1, d), lambda i: (0, 0))] + host.in_specs,
        out_specs=[pl.BlockSpec((tm, d), lambda i: (i, 0))] + host.out_specs,
        out_shape=[jax.ShapeDtypeStruct((t, d), BF16)] + host.out_shapes,
        scratch_shapes=host.scratch, input_output_aliases=host.aliases(2, 1),
        compiler_params=_cp(host.semantics(("parallel",))), name=name,
    )(x, g, *host.ins)
    return (outs[0], list(outs[1:])) if exchange is not None else outs[0]


def _rms_bwd(dn, x, g, dres, *, name, tm=512):
    t, d = x.shape
    tm = _blk(t, tm)

    def body(dn_ref, x_ref, g_ref, dres_ref, dx_ref, dxb_ref, dg_ref):
        i = pl.program_id(0)
        xv = x_ref[...]
        r = lax.rsqrt(jnp.mean(xv * xv, axis=-1, keepdims=True) + RMS_EPS)
        xh = xv * r
        dnv = dn_ref[...].astype(F32)
        dxh = dnv * g_ref[...]
        dx = dres_ref[...] + r * (dxh - xh * jnp.mean(dxh * xh, axis=-1, keepdims=True))
        dx_ref[...] = dx
        dxb_ref[...] = dx.astype(dxb_ref.dtype)
        part = jnp.sum(dnv * xh, axis=0, keepdims=True)

        @pl.when(i == 0)
        def _():
            dg_ref[...] = part

        @pl.when(i > 0)
        def _():
            dg_ref[...] += part

    row = pl.BlockSpec((tm, d), lambda i: (i, 0))
    vec = pl.BlockSpec((1, d), lambda i: (0, 0))
    return pl.pallas_call(
        body, grid=(t // tm,),
        in_specs=[row, row, vec, row], out_specs=[row, row, vec],
        out_shape=[jax.ShapeDtypeStruct((t, d), F32), jax.ShapeDtypeStruct((t, d), BF16),
                   jax.ShapeDtypeStruct((1, d), F32)],
        compiler_params=_cp(("arbitrary",)), name=name,
    )(dn, x, g, dres)


def _final_loss(h, g, tgt, *, name, tm=512):
    t, d = h.shape
    tm = _blk(t, tm)

    def body(h_ref, g_ref, t_ref, dh_ref, dhb_ref, dg_ref, loss_ref):
        i = pl.program_id(0)
        hv = h_ref[...]
        r = lax.rsqrt(jnp.mean(hv * hv, axis=-1, keepdims=True) + RMS_EPS)
        xh = hv * r
        e = xh * g_ref[...] - t_ref[...]
        dy = e * (1.0 / d)
        dxh = dy * g_ref[...]
        dh = r * (dxh - xh * jnp.mean(dxh * xh, axis=-1, keepdims=True))
        dh_ref[...] = dh
        dhb_ref[...] = dh.astype(dhb_ref.dtype)
        dg_part = jnp.sum(dy * xh, axis=0, keepdims=True)
        row_loss = jnp.sum(e * e, axis=-1, keepdims=True) * (0.5 / d)
        loss_part = jnp.sum(row_loss, axis=0, keepdims=True)

        @pl.when(i == 0)
        def _():
            dg_ref[...] = dg_part
            loss_ref[...] = jnp.broadcast_to(loss_part, loss_ref.shape)

        @pl.when(i > 0)
        def _():
            dg_ref[...] += dg_part
            loss_ref[...] += jnp.broadcast_to(loss_part, loss_ref.shape)

    row = pl.BlockSpec((tm, d), lambda i: (i, 0))
    vec = pl.BlockSpec((1, d), lambda i: (0, 0))
    return pl.pallas_call(
        body, grid=(t // tm,),
        in_specs=[row, vec, row], out_specs=[row, row, vec, pl.BlockSpec((1, LANES), lambda i: (0, 0))],
        out_shape=[jax.ShapeDtypeStruct((t, d), F32), jax.ShapeDtypeStruct((t, d), BF16),
                   jax.ShapeDtypeStruct((1, d), F32), jax.ShapeDtypeStruct((1, LANES), F32)],
        compiler_params=_cp(("arbitrary",)), name=name,
    )(h, g, tgt)


SWA_SUB = 64


def _swa_mask(n, rows, row0):
    w = SWA_WINDOW
    qi = (lax.broadcasted_iota(jnp.int32, (rows, 2 * w), 0) + row0) & (w - 1)
    kj = lax.broadcasted_iota(jnp.int32, (rows, 2 * w), 1)
    return (kj > qi) & (kj <= qi + w) & ((n > 0) | (kj >= w))


def _stack_heads(ref, heads, width):
    return jnp.concatenate([ref[:, h * width:(h + 1) * width] for h in heads], axis=0)


def _stack_scalars(ref, heads, rows):
    return jnp.concatenate([jnp.broadcast_to(ref[0:1, h:h + 1], (rows, 1)) for h in heads], axis=0)


def _swa_fwd(q, kv, sinks, *, name, exchange=None):
    t = q.shape[0]
    w, hd, hq, hkv = SWA_WINDOW, SWA_HEAD_DIM, SWA_Q_HEADS, SWA_KV_HEADS
    grp = hq // hkv
    kvw = hkv * hd
    nb = t // w
    host = _ExchangeHost(exchange)
    assert not (exchange and exchange.aliases)

    def body(*refs):
        q_ref, kvp_ref, kvc_ref, s_ref = refs[:4]
        o_ref, lse_ref = refs[4 + host.n_in:6 + host.n_in]
        host.start(refs, 4, 6 + host.n_in, (nb,))
        n = pl.program_id(0)
        mask = _swa_mask(n, grp * w, 0)
        kvcat = jnp.concatenate([kvp_ref[...], kvc_ref[...]], axis=0)
        kvs = range(hkv)
        heads = [range(hk * grp, (hk + 1) * grp) for hk in kvs]
        sks = [_stack_scalars(s_ref, hs, w) for hs in heads]
        ss = [jnp.where(mask, _dot(_stack_heads(q_ref, heads[hk], hd) * SWA_SCALE, kvcat[:, hk * hd:(hk + 1) * hd], NT),
                        -jnp.inf) for hk in kvs]
        ms = [jnp.maximum(jnp.max(s, axis=-1, keepdims=True), sk) for s, sk in zip(ss, sks)]
        ps = [jnp.exp(s - m) for s, m in zip(ss, ms)]
        dens = [jnp.sum(p, axis=-1, keepdims=True) + jnp.exp(sk - m) for p, sk, m in zip(ps, sks, ms)]
        os_ = [_dot((p * (1.0 / den)).astype(BF16), kvcat[:, kvw + hk * hd:kvw + (hk + 1) * hd])
               for hk, p, den in zip(kvs, ps, dens)]
        outs, lses = [], []
        for o, m, den in zip(os_, ms, dens):
            lse = m + jnp.log(den)
            outs += [o[j * w:(j + 1) * w] for j in range(grp)]
            lses += [lse[j * w:(j + 1) * w] for j in range(grp)]
        o_ref[...] = jnp.concatenate(outs, axis=1).astype(o_ref.dtype)
        lse_ref[...] = jnp.concatenate(lses, axis=1)
        host.wait(refs, 4, 6 + host.n_in, (nb,))

    outs = pl.pallas_call(
        body, grid=(nb,),
        in_specs=[pl.BlockSpec((w, hq * hd), lambda i: (i, 0)),
                  pl.BlockSpec((w, 2 * kvw), lambda i: (jnp.maximum(i - 1, 0), 0)),
                  pl.BlockSpec((w, 2 * kvw), lambda i: (i, 0)),
                  pl.BlockSpec((1, hq), lambda i: (0, 0))] + host.in_specs,
        out_specs=[pl.BlockSpec((w, hq * hd), lambda i: (i, 0)), pl.BlockSpec((w, hq), lambda i: (i, 0))] + host.out_specs,
        out_shape=[jax.ShapeDtypeStruct((t, hq * hd), BF16), jax.ShapeDtypeStruct((t, hq), F32)] + host.out_shapes,
        scratch_shapes=host.scratch,
        compiler_params=_cp(host.semantics(("parallel",))), name=name,
    )(q, kv, kv, sinks, *host.ins)
    return (outs[0], outs[1], list(outs[2:])) if exchange is not None else outs


ANY_SPEC = pl.BlockSpec(memory_space=pl.ANY)


def _swa_bwd(q, kv, sinks, o, lse, do, dp, dp_col, *, name):
    t = q.shape[0]
    w, hd, hq, hkv = SWA_WINDOW, SWA_HEAD_DIM, SWA_Q_HEADS, SWA_KV_HEADS
    grp = hq // hkv
    kvw = hkv * hd
    nb = t // w
    assert dp_col % (hq * hd) == 0
    dq_blk = dp_col // (hq * hd)

    def body(q_ref, kvp_ref, kvc_ref, s_ref, o_ref, lse_ref, do_ref, _, dq_ref, dkv_ref, ds_ref, carry_ref, s_scr, dp_scr,
             p_scr, ds_scr):
        n = pl.program_id(0)

        @pl.when(n == 0)
        def _():
            ds_ref[...] = jnp.zeros_like(ds_ref)
            carry_ref[...] = jnp.zeros_like(carry_ref)

        @pl.when(n < nb)
        def _():
            kvcat = jnp.concatenate([kvp_ref[...], kvc_ref[...]], axis=0)
            dqs, dsk, dks, dvs = [], [], [], []
            for hk in range(hkv):
                heads = range(hk * grp, (hk + 1) * grp)
                qs = _stack_heads(q_ref, heads, hd)
                dos = _stack_heads(do_ref, heads, hd)
                os_ = _stack_heads(o_ref, heads, hd)
                lse = _stack_heads(lse_ref, heads, 1)
                kh = kvcat[:, hk * hd:(hk + 1) * hd]
                vh = kvcat[:, kvw + hk * hd:kvw + (hk + 1) * hd]
                delta = jnp.sum(dos.astype(F32) * os_.astype(F32), axis=-1, keepdims=True)
                s_scr[...] = _dot(qs * SWA_SCALE, kh, NT)
                dp_scr[...] = _dot(dos, vh, NT)
                for r0 in range(0, grp * w, SWA_SUB):
                    rows = slice(r0, r0 + SWA_SUB)
                    p = jnp.exp(jnp.where(_swa_mask(n, SWA_SUB, r0 % w), s_scr[rows, :], -jnp.inf) - lse[rows])
                    p_scr[rows, :] = p.astype(p_scr.dtype)
                    ds_scr[rows, :] = (p * (dp_scr[rows, :] - delta[rows]) * SWA_SCALE).astype(ds_scr.dtype)
                ds = ds_scr[...]
                dq = _dot(ds, kh)
                dqs += [dq[j * w:(j + 1) * w] for j in range(grp)]
                dks.append(_dot(ds, qs, TN))
                dvs.append(_dot(p_scr[...], dos, TN))
                dsink = -jnp.exp(_stack_scalars(s_ref, heads, w) - lse) * delta
                dsk += [jnp.sum(dsink[j * w:(j + 1) * w], axis=0, keepdims=True) for j in range(grp)]
            dq_ref[...] = jnp.concatenate(dqs, axis=1).astype(dq_ref.dtype)
            ds_ref[...] += jnp.concatenate(dsk, axis=1)
            dkv_cat = jnp.concatenate(dks + dvs, axis=1)
            dkv_ref[...] = (carry_ref[...] + dkv_cat[:w]).astype(dkv_ref.dtype)
            carry_ref[...] = dkv_cat[w:]

        @pl.when(n == nb)
        def _():
            dkv_ref[...] = carry_ref[...].astype(dkv_ref.dtype)

    cur = lambda i: (jnp.minimum(i, nb - 1), 0)
    prev = lambda i: (jnp.clip(i - 1, 0, nb - 1), 0)
    return pl.pallas_call(
        body, grid=(nb + 1,),
        in_specs=[pl.BlockSpec((w, hq * hd), cur), pl.BlockSpec((w, 2 * kvw), prev), pl.BlockSpec((w, 2 * kvw), cur),
                  pl.BlockSpec((1, hq), lambda i: (0, 0)), pl.BlockSpec((w, hq * hd), cur),
                  pl.BlockSpec((w, hq), cur), pl.BlockSpec((w, hq * hd), cur), ANY_SPEC],
        out_specs=[pl.BlockSpec((w, hq * hd), lambda i: (jnp.minimum(i, nb - 1), dq_blk)),
                   pl.BlockSpec((w, 2 * kvw), prev), pl.BlockSpec((1, hq), lambda i: (0, 0))],
        out_shape=[jax.ShapeDtypeStruct(dp.shape, dp.dtype), jax.ShapeDtypeStruct((t, 2 * kvw), BF16),
                   jax.ShapeDtypeStruct((1, hq), F32)],
        scratch_shapes=[pltpu.VMEM((w, 2 * kvw), F32)] + [pltpu.VMEM((grp * w, 2 * w), dt) for dt in (F32, F32, BF16, BF16)],
        input_output_aliases={7: 0},
        compiler_params=_cp(("arbitrary",)), name=name,
    )(q, kv, kv, sinks, o, lse, do, dp)


def _xa_fwd(q, mkv, *, name, tq=512):
    t, xw = q.shape
    nm = mkv.shape[0]
    hd, nh = XA_HEAD_DIM, XA_HEADS
    tq = _blk(t, tq)

    def body(q_ref, mkv_ref, o_ref):
        cols = [slice(h * hd, (h + 1) * hd) for h in range(nh)]
        ss = [_dot(q_ref[:, c], mkv_ref[:, c], NT) * (hd ** -0.5) for c in cols]
        ps = [jnp.exp(s - jnp.max(s, axis=-1, keepdims=True)) for s in ss]
        ps = [p * (1.0 / jnp.sum(p, axis=-1, keepdims=True)) for p in ps]
        outs = [_dot(p.astype(BF16), mkv_ref[:, xw + c.start:xw + c.stop]) for p, c in zip(ps, cols)]
        o_ref[...] = jnp.concatenate(outs, axis=1).astype(o_ref.dtype)

    return pl.pallas_call(
        body, grid=(t // tq,),
        in_specs=[pl.BlockSpec((tq, xw), lambda i: (i, 0)), pl.BlockSpec((nm, 2 * xw), lambda i: (0, 0))],
        out_specs=pl.BlockSpec((tq, xw), lambda i: (i, 0)),
        out_shape=jax.ShapeDtypeStruct((t, xw), BF16),
        compiler_params=_cp(("parallel",)), name=name,
    )(q, mkv)


def _xa_bwd(q, mkv, do, dp, dp_col, *, name, tq=512):
    t, xw = q.shape
    nm = mkv.shape[0]
    hd, nh = XA_HEAD_DIM, XA_HEADS
    tq = _blk(t, tq)
    assert dp_col % xw == 0

    def body(q_ref, mkv_ref, do_ref, _, dq_ref, dmkv_ref):
        i = pl.program_id(0)
        cols = [slice(h * hd, (h + 1) * hd) for h in range(nh)]
        vcols = [slice(xw + c.start, xw + c.stop) for c in cols]
        ss = [_dot(q_ref[:, c], mkv_ref[:, c], NT) * (hd ** -0.5) for c in cols]
        dps = [_dot(do_ref[:, c], mkv_ref[:, v], NT) for c, v in zip(cols, vcols)]
        ps = [jnp.exp(s - jnp.max(s, axis=-1, keepdims=True)) for s in ss]
        ps = [p * (1.0 / jnp.sum(p, axis=-1, keepdims=True)) for p in ps]
        dss = [(p * (dp - jnp.sum(p * dp, axis=-1, keepdims=True)) * (hd ** -0.5)).astype(BF16) for p, dp in zip(ps, dps)]
        dqs = [_dot(ds, mkv_ref[:, c]) for ds, c in zip(dss, cols)]
        dks = [_dot(ds, q_ref[:, c], TN) for ds, c in zip(dss, cols)]
        dvs = [_dot(p.astype(BF16), do_ref[:, c], TN) for p, c in zip(ps, cols)]
        dq_ref[...] = jnp.concatenate(dqs, axis=1).astype(dq_ref.dtype)
        part = jnp.concatenate(dks + dvs, axis=1)

        @pl.when(i == 0)
        def _():
            dmkv_ref[...] = part

        @pl.when(i > 0)
        def _():
            dmkv_ref[...] += part

    row = pl.BlockSpec((tq, xw), lambda i: (i, 0))
    full = pl.BlockSpec((nm, 2 * xw), lambda i: (0, 0))
    return pl.pallas_call(
        body, grid=(t // tq,),
        in_specs=[row, full, row, ANY_SPEC],
        out_specs=[pl.BlockSpec((tq, xw), lambda i: (i, dp_col // xw)), full],
        out_shape=[jax.ShapeDtypeStruct(dp.shape, dp.dtype), jax.ShapeDtypeStruct((nm, 2 * xw), F32)],
        input_output_aliases={3: 0}, compiler_params=_cp(("arbitrary",)), name=name,
    )(q, mkv, do, dp)


def _merge_specs(ys, ws, tm):
    y_specs = [pl.BlockSpec((tm, y.shape[1]), lambda i: (i, 0)) for y in ys]
    w_specs = [pl.BlockSpec(w.shape, lambda i: (0, 0, 0)) for w in ws]
    return y_specs, w_specs


def _merge_tiles(ws, tn):
    ns, _, per = ws[0].shape
    tn = _blk(per, tn)
    return tn, [(s, c, s * per + c) for s in range(ns) for c in range(0, per, tn)]


def _merge_fwd(ys, ws, gates, *, name, tm=256, tn=512):
    t, d = ys[0].shape[0], ws[0].shape[0] * ws[0].shape[2]
    tm = _blk(t, tm)
    tn, tiles = _merge_tiles(ws, tn)
    y_specs, w_specs = _merge_specs(ys, ws, tm)

    def body(ya, yb, yc, wa, wb, wc, g_ref, o_ref):
        for s, c, col in tiles:
            acc = None
            for b, (y, w) in enumerate(((ya, wa), (yb, wb), (yc, wc))):
                term = _sigmoid(g_ref[:, b * d + col:b * d + col + tn]) * _dot(y[...], w[s, :, c:c + tn])
                acc = term if acc is None else acc + term
            o_ref[:, col:col + tn] = acc.astype(o_ref.dtype)

    return pl.pallas_call(
        body, grid=(t // tm,),
        in_specs=y_specs + w_specs + [pl.BlockSpec((tm, 3 * d), lambda i: (i, 0))],
        out_specs=pl.BlockSpec((tm, d), lambda i: (i, 0)),
        out_shape=jax.ShapeDtypeStruct((t, d), BF16),
        compiler_params=_cp(("parallel",)), name=name,
    )(*ys, *ws, gates)


def _merge_bwd(ys, ws, gates, dmerged, dp_width, *, name, tm=256, tn=512):
    t, d = ys[0].shape[0], ws[0].shape[0] * ws[0].shape[2]
    tm = _blk(t, tm)
    tn, tiles = _merge_tiles(ws, tn)
    y_specs, w_specs = _merge_specs(ys, ws, tm)
    row = pl.BlockSpec((tm, d), lambda i: (i, 0))
    wide = pl.BlockSpec((tm, 3 * d), lambda i: (i, 0))

    def body(ya, yb, yc, wa, wb, wc, g_ref, dm_ref, dua, dub, duc, dp_ref):
        for s, c, col in tiles:
            dm = dm_ref[:, col:col + tn]
            for b, (y, w, du) in enumerate(((ya, wa, dua), (yb, wb, dub), (yc, wc, duc))):
                sg = _sigmoid(g_ref[:, b * d + col:b * d + col + tn])
                u = _dot(y[...], w[s, :, c:c + tn])
                du[:, col:col + tn] = (dm * sg).astype(du.dtype)
                dp_ref[:, b * d + col:b * d + col + tn] = (dm * u * sg * (1.0 - sg)).astype(dp_ref.dtype)

    return pl.pallas_call(
        body, grid=(t // tm,),
        in_specs=y_specs + w_specs + [wide, row],
        out_specs=[row] * 3 + [wide],
        out_shape=[jax.ShapeDtypeStruct((t, d), BF16)] * 3 + [jax.ShapeDtypeStruct((t, dp_width), BF16)],
        compiler_params=_cp(("parallel",)), name=name,
    )(*ys, *ws, gates, dmerged)


def _adamw(w, g, m, v, *, name, tm=256):
    lead = w.ndim - 2
    assert all(s == 1 for s in w.shape[:lead]) and m.shape == w.shape and v.shape == w.shape
    r, c = w.shape[lead:]
    assert g.shape == (r, c)
    tm = _blk(r, tm) if r % 8 == 0 else r
    tc = c if tm * c * 4 <= (4 << 20) else _blk(c, 256)
    ncb = c // tc
    bc1 = 1.0 - ADAM_B1 ** ADAM_STEP
    bc2 = 1.0 - ADAM_B2 ** ADAM_STEP

    def body(w_ref, g_ref, m_ref, v_ref, go_ref, d_ref, nm_ref, nv_ref):
        gv = g_ref[...]
        go_ref[...] = gv
        nm = ADAM_B1 * m_ref[...] + (1.0 - ADAM_B1) * gv
        nv = ADAM_B2 * v_ref[...] + (1.0 - ADAM_B2) * (gv * gv)
        d_ref[...] = -ADAM_LR * ((nm / bc1) / (jnp.sqrt(nv / bc2) + ADAM_EPS) + ADAM_WD * w_ref[...])
        nm_ref[...] = nm
        nv_ref[...] = nv

    spec = pl.BlockSpec((None,) * lead + (tm, tc), lambda i: (0,) * lead + (i // ncb, i % ncb))
    g_spec = pl.BlockSpec((tm, tc), lambda i: (i // ncb, i % ncb))
    return pl.pallas_call(
        body, grid=(r // tm * ncb,), in_specs=[spec, g_spec, spec, spec], out_specs=[spec] * 4,
        out_shape=[jax.ShapeDtypeStruct(w.shape, F32)] * 4,
        compiler_params=_cp(("parallel",)), name=name,
    )(w, g, m, v)


HALO = 8


def _shift_down(cur, prev, j):
    if j == 0:
        return cur
    y = pltpu.roll(cur, j, 0)
    row = lax.broadcasted_iota(jnp.int32, (HALO, cur.shape[1]), 0)
    top = jnp.where(row < j, pltpu.roll(prev, j, 0), y[:HALO])
    return jnp.concatenate([top, y[HALO:]], axis=0)


def _shift_up(cur, nxt, j):
    if j == 0:
        return cur
    tm = cur.shape[0]
    y = pltpu.roll(cur, tm - j, 0)
    row = lax.broadcasted_iota(jnp.int32, (HALO, cur.shape[1]), 0)
    bot = jnp.where(row >= HALO - j, pltpu.roll(nxt, HALO - j, 0), y[tm - HALO:])
    return jnp.concatenate([y[:tm - HALO], bot], axis=0)


def _softplus(x):
    return jnp.maximum(x, 0.0) + jnp.log(1.0 + jnp.exp(-jnp.abs(x)))


def _gdn_pre_fwd(qkvb, conv_w, ab, alog_pad, dt_pad, *, name, ab_blk=0, tm=256):
    t, cw = qkvb.shape
    hd, nh, ck = GDN_HEAD_DIM, GDN_HEADS, GDN_CHUNK
    gw = nh * hd
    tm = _blk(t, tm)
    hb = tm // HALO

    def body(x_ref, xp_ref, w_ref, ab_ref, al_ref, dt_ref, xc_ref, qkvn_ref, aux_ref):
        i = pl.program_id(0)
        cur = x_ref[...]
        prev = jnp.where(i > 0, xp_ref[...], 0.0)
        xc = None
        for tap in range(GDN_CONV):
            term = w_ref[tap:tap + 1, :] * _shift_down(cur, prev, GDN_CONV - 1 - tap)
            xc = term if xc is None else xc + term
        xc_ref[...] = xc
        s = xc * _sigmoid(xc)
        for h in range(2 * nh):
            xh = s[:, h * hd:(h + 1) * hd]
            r = lax.rsqrt(jnp.sum(xh * xh, axis=-1, keepdims=True) + L2_EPS)
            scale = hd ** -0.5 if h < nh else 1.0
            qkvn_ref[:, h * hd:(h + 1) * hd] = xh * (r * scale)
        qkvn_ref[:, 2 * gw:] = s[:, 2 * gw:]
        abv = ab_ref[...]
        lane = lax.broadcasted_iota(jnp.int32, abv.shape, 1)
        g = jnp.where(lane < nh, -jnp.exp(al_ref[...]) * _softplus(abv + dt_ref[...]), 0.0)
        beta = jnp.where((lane >= nh) & (lane < 2 * nh), _sigmoid(abv), 0.0)
        ii = lax.broadcasted_iota(jnp.int32, (tm, tm), 0)
        jj = lax.broadcasted_iota(jnp.int32, (tm, tm), 1)
        tri = jnp.where((ii >= jj) & ((ii ^ jj) < ck), 1.0, 0.0)
        gcum = _dot(tri, g, precision=HI)
        aux_ref[...] = g + beta + pltpu.roll(gcum, 2 * nh, 1)

    row = lambda c: pl.BlockSpec((tm, c), lambda i: (i, 0))
    vec = lambda r, c: pl.BlockSpec((r, c), lambda i: (0, 0))
    return pl.pallas_call(
        body, grid=(t // tm,),
        in_specs=[row(cw), pl.BlockSpec((HALO, cw), lambda i: (jnp.maximum(i * hb - 1, 0), 0)), vec(GDN_CONV, cw),
                  pl.BlockSpec((tm, LANES), lambda i: (i, ab_blk)), vec(1, LANES), vec(1, LANES)],
        out_specs=[row(cw), row(cw), row(LANES)],
        out_shape=[jax.ShapeDtypeStruct((t, cw), F32), jax.ShapeDtypeStruct((t, cw), F32),
                   jax.ShapeDtypeStruct((t, LANES), F32)],
        compiler_params=_cp(("parallel",)), name=name,
    )(qkvb, qkvb, conv_w, ab, alog_pad, dt_pad)


GDN_STEP_CHUNKS = 4
GDN_ILP_CHUNKS = 4
GDN_ILP_CHUNKS_BWD = 4


def _bdot(a, b, dims=NN):
    return _dot(a.astype(BF16), b.astype(BF16), dims)


def _split_bf16(x):
    hi = x.astype(BF16)
    return hi, (x - hi.astype(F32)).astype(BF16)


def _dot3(a, b, dims=NN):
    ah, al = _split_bf16(a)
    bh, bl = _split_bf16(b)
    return _dot(ah, bh, dims) + (_dot(ah, bl, dims) + _dot(al, bh, dims))


def _dot3_many(lhs, rhs, dims=NN):
    sa = [_split_bf16(a) for a in lhs]
    sb = [_split_bf16(b) for b in rhs]
    hh = [_dot(a[0], b[0], dims) for a, b in zip(sa, sb)]
    hl = [_dot(a[0], b[1], dims) for a, b in zip(sa, sb)]
    lh = [_dot(a[1], b[0], dims) for a, b in zip(sa, sb)]
    return [x + (y + z) for x, y, z in zip(hh, hl, lh)]


def _gdn_local(chains, with_inverse):
    ck = GDN_CHUNK
    ii = lax.broadcasted_iota(jnp.int32, (ck, ck), 0)
    jj = lax.broadcasted_iota(jnp.int32, (ck, ck), 1)
    lower, strict = ii >= jj, ii > jj
    dmat = [jnp.exp(jnp.where(lower, gc - gc_row, -jnp.inf)) for _, _, _, gc, gc_row in chains]
    kk = [_bdot(k, k, NT) for _, k, _, _, _ in chains]
    qk = [_bdot(q, k, NT) for q, k, _, _, _ in chains]
    tinv = [None] * len(chains)
    if with_inverse:
        lmat = [jnp.where(strict, c[2] * kk_i * d_i, 0.0) for c, kk_i, d_i in zip(chains, kk, dmat)]
        eye = jnp.where(ii == jj, 1.0, 0.0)
        tinv = [eye - l_i for l_i in lmat]
        pw = lmat
        for _ in range(int(math.log2(ck)) - 1):
            pw = _dot3_many(pw, pw)
            tinv = [t_i + d_i for t_i, d_i in zip(tinv, _dot3_many(tinv, pw))]
    out = []
    for (q, k, b, gc, gc_row), dmat_i, kk_i, qk_i, tinv_i in zip(chains, dmat, kk, qk, tinv):
        gl = gc[ck - 1:ck, :]
        out.append(dict(lower=lower, strict=strict, dmat=dmat_i, kk=kk_i, tinv=tinv_i, gam=jnp.exp(gc), qk=qk_i,
                        mm=qk_i * dmat_i, kdec=jnp.exp(gl - gc)))
    return out


def _gdn_head_cols(h):
    return slice(h * GDN_HEAD_DIM, (h + 1) * GDN_HEAD_DIM)


def _gdn_chunk_inputs(x_ref, aux_ref, auxt_ref, g, h):
    nh, ck = GDN_HEADS, GDN_CHUNK
    gw = nh * GDN_HEAD_DIM
    rows = slice(g * ck, (g + 1) * ck)
    cols = _gdn_head_cols(h)
    q = x_ref[rows, cols]
    k = x_ref[rows, gw + cols.start:gw + cols.stop]
    v = x_ref[rows, 2 * gw + cols.start:2 * gw + cols.stop]
    b = aux_ref[rows, nh + h:nh + h + 1]
    gc = aux_ref[rows, 2 * nh + h:2 * nh + h + 1]
    gc_row = auxt_ref[g, 2 * nh + h:2 * nh + h + 1, :]
    return q, k, v, b, gc, gc_row


def _gdn_specs(t, widths, *, reverse=False, step_chunks=None):
    rows = (step_chunks or GDN_STEP_CHUNKS) * GDN_CHUNK
    nsteps = t // rows
    idx = (lambda i: (nsteps - 1 - i, 0)) if reverse else (lambda i: (i, 0))
    return [pl.BlockSpec((rows, w), idx) for w in widths]


def _gdn_local_fwd(qkvn, aux, aux_t, *, name, exchange=None):
    t = qkvn.shape[0]
    hd, nh, ck, gs = GDN_HEAD_DIM, GDN_HEADS, GDN_CHUNK, GDN_STEP_CHUNKS
    gw = nh * hd
    host = _ExchangeHost(exchange)
    assert not (exchange and exchange.aliases)
    grid = (t // (gs * ck),)

    def body(*refs):
        x_ref, aux_ref, auxt_ref = refs[:3]
        u_ref, w_ref, qd_ref, kd_ref, mm_ref, tinv_ref = refs[3 + host.n_in:9 + host.n_in]
        host.start(refs, 3, 9 + host.n_in, grid)
        for g0 in range(0, gs, GDN_ILP_CHUNKS):
            where = [(g, h) for g in range(g0, g0 + GDN_ILP_CHUNKS) for h in range(nh)]
            ins = [_gdn_chunk_inputs(x_ref, aux_ref, auxt_ref, g, h) for g, h in where]
            lcs = _gdn_local([(q, k, b, gc, gc_row) for q, k, _, b, gc, gc_row in ins], True)
            tinvs = [lc["tinv"] for lc in lcs]
            us = _dot3_many(tinvs, [b * v for _, _, v, b, _, _ in ins])
            ws = _dot3_many(tinvs, [(b * lc["gam"]) * k for (_, k, _, b, _, _), lc in zip(ins, lcs)])
            for i, ((g, h), (q, k, _, _, _, _), lc) in enumerate(zip(where, ins, lcs)):
                rows, cols = slice(g * ck, (g + 1) * ck), _gdn_head_cols(h)
                u_ref[rows, cols] = us[i]
                w_ref[rows, cols] = ws[i].astype(w_ref.dtype)
                qd_ref[rows, cols] = (lc["gam"] * q).astype(qd_ref.dtype)
                kd_ref[rows, cols] = (lc["kdec"] * k).astype(kd_ref.dtype)
            for g in range(g0, g0 + GDN_ILP_CHUNKS):
                rows = slice(g * ck, (g + 1) * ck)
                mine = [lc for (gg, _), lc in zip(where, lcs) if gg == g]
                mm_ref[rows, :] = jnp.concatenate([lc["mm"] for lc in mine], axis=1).astype(mm_ref.dtype)
                tinv_ref[rows, :] = jnp.concatenate([lc["tinv"] for lc in mine], axis=1)
        host.wait(refs, 3, 9 + host.n_in, grid)

    sq = nh * ck
    outs = pl.pallas_call(
        body, grid=grid,
        in_specs=_gdn_specs(t, (3 * gw, LANES)) + [pl.BlockSpec((gs, 16, ck), lambda i: (i, 0, 0))] + host.in_specs,
        out_specs=_gdn_specs(t, (gw, gw, gw, gw, sq, sq)) + host.out_specs,
        out_shape=[jax.ShapeDtypeStruct((t, gw), F32)] + [jax.ShapeDtypeStruct((t, gw), BF16)] * 3
        + [jax.ShapeDtypeStruct((t, sq), BF16), jax.ShapeDtypeStruct((t, sq), F32)] + host.out_shapes,
        scratch_shapes=host.scratch,
        compiler_params=_cp(host.semantics(("parallel",))), name=name,
    )(qkvn, aux, aux_t, *host.ins)
    return (*outs[:6], list(outs[6:])) if exchange is not None else outs


def _gdn_seq_fwd(u, w, qd, kd, mm, aux, *, name):
    t = u.shape[0]
    hd, nh, ck, gs = GDN_HEAD_DIM, GDN_HEADS, GDN_CHUNK, GDN_STEP_CHUNKS
    gw = nh * hd
    sq = nh * ck

    def body(u_ref, w_ref, qd_ref, kd_ref, mm_ref, aux_ref, o_ref, vn_ref, sall_ref, s_ref):
        @pl.when(pl.program_id(0) == 0)
        def _():
            s_ref[...] = jnp.zeros_like(s_ref)

        heads = range(nh)
        hcols = [_gdn_head_cols(h) for h in heads]
        sts = [s_ref[h] for h in heads]
        for g in range(gs):
            rows = slice(g * ck, (g + 1) * ck)
            last = (g + 1) * ck - 1
            for h in heads:
                sall_ref[g, h] = sts[h]
            stbs = [st.astype(BF16) for st in sts]
            w_s = [_dot(w_ref[rows, c], stb) for c, stb in zip(hcols, stbs)]
            q_s = [_dot(qd_ref[rows, c], stb) for c, stb in zip(hcols, stbs)]
            vnbs = [(u_ref[rows, c] - ws).astype(BF16) for c, ws in zip(hcols, w_s)]
            m_v = [_dot(mm_ref[rows, h * ck:(h + 1) * ck], vnbs[h]) for h in heads]
            k_v = [_dot(kd_ref[rows, c], vnb, TN) for c, vnb in zip(hcols, vnbs)]
            for h, c in zip(heads, hcols):
                vn_ref[rows, c] = vnbs[h]
                o_ref[rows, c] = q_s[h] + m_v[h]
            gam_c = [jnp.exp(aux_ref[last:last + 1, 2 * nh + h:2 * nh + h + 1]) for h in heads]
            sts = [gam_c[h] * sts[h] + k_v[h] for h in heads]
        for h in heads:
            s_ref[h] = sts[h]

    return pl.pallas_call(
        body, grid=(t // (gs * ck),),
        in_specs=_gdn_specs(t, (gw, gw, gw, gw, sq, LANES)),
        out_specs=_gdn_specs(t, (gw, gw)) + [pl.BlockSpec((gs, nh, hd, hd), lambda i: (i, 0, 0, 0))],
        out_shape=[jax.ShapeDtypeStruct((t, gw), F32), jax.ShapeDtypeStruct((t, gw), BF16),
                   jax.ShapeDtypeStruct((t // ck, nh, hd, hd), F32)],
        scratch_shapes=[pltpu.VMEM((nh, hd, hd), F32)],
        compiler_params=_cp(("arbitrary",)), name=name,
    )(u, w, qd, kd, mm, aux)


def _gdn_seq_bwd(do, w, qd, kd, mm, vn, s_all, aux, *, name):
    t = do.shape[0]
    hd, nh, ck, gs = GDN_HEAD_DIM, GDN_HEADS, GDN_CHUNK, GDN_STEP_CHUNKS
    gw = nh * hd
    sq = nh * ck
    nsteps = t // (gs * ck)

    def body(do_ref, w_ref, qd_ref, kd_ref, mm_ref, vn_ref, sall_ref, aux_ref, dvn_ref, dqd_ref, dkd_ref, dw_ref,
             dlast_ref, ds_ref):
        @pl.when(pl.program_id(0) == 0)
        def _():
            ds_ref[...] = jnp.zeros_like(ds_ref)

        lane = lax.broadcasted_iota(jnp.int32, (ck, LANES), 1)
        rowi = lax.broadcasted_iota(jnp.int32, (ck, LANES), 0)
        heads = range(nh)
        hcols = [_gdn_head_cols(h) for h in heads]
        dsns = [ds_ref[h] for h in heads]
        for g in reversed(range(gs)):
            rows = slice(g * ck, (g + 1) * ck)
            last = (g + 1) * ck - 1
            sts = [sall_ref[g, h] for h in heads]
            stbs = [st.astype(BF16) for st in sts]
            dsbs = [dsn.astype(BF16) for dsn in dsns]
            dobs = [do_ref[rows, c].astype(BF16) for c in hcols]
            dvns = [_dot(mm_ref[rows, h * ck:(h + 1) * ck], dobs[h], TN) + _dot(kd_ref[rows, hcols[h]], dsbs[h])
                    for h in heads]
            dqds = [_dot(dob, stb, NT) for dob, stb in zip(dobs, stbs)]
            dkds = [_dot(vn_ref[rows, c], dsb, NT) for c, dsb in zip(hcols, dsbs)]
            q_o = [_dot(qd_ref[rows, c], dob, TN) for c, dob in zip(hcols, dobs)]
            dvbs = [dvn.astype(BF16) for dvn in dvns]
            dws = [_dot(dvb, stb, NT) for dvb, stb in zip(dvbs, stbs)]
            w_v = [_dot(w_ref[rows, c], dvb, TN) for c, dvb in zip(hcols, dvbs)]
            gam_c = [jnp.exp(aux_ref[last:last + 1, 2 * nh + h:2 * nh + h + 1]) for h in heads]
            dlast = jnp.zeros((ck, LANES), F32)
            for h, c in zip(heads, hcols):
                dvn_ref[rows, c] = dvns[h]
                dqd_ref[rows, c] = dqds[h]
                dkd_ref[rows, c] = dkds[h]
                dw_ref[rows, c] = -dws[h]
                dgam_c = jnp.sum(jnp.sum(dsns[h] * sts[h], axis=1, keepdims=True), axis=0, keepdims=True)
                dlast = dlast + jnp.where((rowi == ck - 1) & (lane == h), gam_c[h] * dgam_c, 0.0)
            dlast_ref[rows, :] = dlast
            dsns = [q_o[h] + gam_c[h] * dsns[h] - w_v[h] for h in heads]
        for h in heads:
            ds_ref[h] = dsns[h]

    return pl.pallas_call(
        body, grid=(nsteps,),
        in_specs=_gdn_specs(t, (gw, gw, gw, gw, sq, gw), reverse=True)
        + [pl.BlockSpec((gs, nh, hd, hd), lambda i: (nsteps - 1 - i, 0, 0, 0))] + _gdn_specs(t, (LANES,), reverse=True),
        out_specs=_gdn_specs(t, (gw, gw, gw, gw, LANES), reverse=True),
        out_shape=[jax.ShapeDtypeStruct((t, gw), F32)] * 4 + [jax.ShapeDtypeStruct((t, LANES), F32)],
        scratch_shapes=[pltpu.VMEM((nh, hd, hd), F32)],
        compiler_params=_cp(("arbitrary",)), name=name,
    )(do, w, qd, kd, mm, vn, s_all, aux)


def _gdn_local_bwd(qkvn, aux, aux_t, tinv, u, w, vn, do, dvn, dqd, dkd, dw, dlast, *, name):
    t = qkvn.shape[0]
    hd, nh, ck, gs = GDN_HEAD_DIM, GDN_HEADS, GDN_CHUNK, GDN_STEP_CHUNKS
    gw = nh * hd
    sq = nh * ck

    def body(x_ref, aux_ref, auxt_ref, tinv_ref, u_ref, w_ref, vn_ref, do_ref, dvn_ref, dqd_ref, dkd_ref, dw_ref,
             dlast_ref, dx_ref, daux_ref):
        lane = lax.broadcasted_iota(jnp.int32, (ck, LANES), 1)
        ones = jnp.ones((ck, LANES), F32)
        ii = lax.broadcasted_iota(jnp.int32, (ck, ck), 0)
        jj = lax.broadcasted_iota(jnp.int32, (ck, ck), 1)
        suffix = jnp.where(jj >= ii, 1.0, 0.0)
        for g0 in range(0, gs, GDN_ILP_CHUNKS_BWD):
            where = [(g, h) for g in range(g0, g0 + GDN_ILP_CHUNKS_BWD) for h in range(nh)]
            at = [(slice(g * ck, (g + 1) * ck), _gdn_head_cols(h)) for g, h in where]
            ins = [_gdn_chunk_inputs(x_ref, aux_ref, auxt_ref, g, h) for g, h in where]
            lcs = _gdn_local([(q, k, b, gc, gc_row) for q, k, _, b, gc, gc_row in ins], False)
            tinvs = [tinv_ref[slice(g * ck, (g + 1) * ck), h * ck:(h + 1) * ck] for g, h in where]
            dms = [jnp.where(lc["lower"], _bdot(do_ref[r, c], vn_ref[r, c], NT), 0.0) for lc, (r, c) in zip(lcs, at)]
            drvs = _dot3_many(tinvs, [dvn_ref[r, c] for r, c in at], TN)
            drks = _dot3_many(tinvs, [dw_ref[r, c] for r, c in at], TN)
            das = [jnp.where(lc["strict"], -(_bdot(drv, u_ref[r, c], NT) + _bdot(drk, w_ref[r, c], NT)), 0.0)
                   for lc, (r, c), drv, drk in zip(lcs, at, drvs, drks)]
            f_mats = [da * (i[3] * lc["kk"]) * lc["dmat"] + dm * lc["qk"] * lc["dmat"]
                      for i, lc, da, dm in zip(ins, lcs, das, dms)]
            col_sums = _dot3_many(f_mats, [ones] * len(where), TN)
            dgc_all = {g: dlast_ref[slice(g * ck, (g + 1) * ck), :] for g in range(g0, g0 + GDN_ILP_CHUNKS_BWD)}
            db_all = {g: jnp.zeros((ck, LANES), F32) for g in range(g0, g0 + GDN_ILP_CHUNKS_BWD)}
            e_mats = [da * lc["dmat"] * i[3] for i, lc, da in zip(ins, lcs, das)]
            dmds = [dm * lc["dmat"] for lc, dm in zip(lcs, dms)]
            dq_mm = [_bdot(dmd, i[1]) for i, dmd in zip(ins, dmds)]
            dk_mm = [_bdot(e, i[1]) + _bdot(e, i[1], TN) + _bdot(dmd, i[0], TN) for i, e, dmd in zip(ins, e_mats, dmds)]
            for n, ((g, h), (q, k, v, b, _, _), lc, (rows, cols)) in enumerate(zip(where, ins, lcs, at)):
                dmat, kk, gam, kdec = (lc[key] for key in ("dmat", "kk", "gam", "kdec"))
                drv, drk, da = drvs[n], drks[n], das[n]
                dqd_h, dkd_h = dqd_ref[rows, cols], dkd_ref[rows, cols]
                rs_rk = jnp.sum(drk * k, axis=-1, keepdims=True)
                db = (jnp.sum(drv * v, axis=-1, keepdims=True) + gam * rs_rk
                      + jnp.sum(da * kk * dmat, axis=-1, keepdims=True))
                dx_ref[rows, cols] = dq_mm[n] + gam * dqd_h
                dx_ref[rows, gw + cols.start:gw + cols.stop] = (b * gam) * drk + dk_mm[n] + kdec * dkd_h
                dx_ref[rows, 2 * gw + cols.start:2 * gw + cols.stop] = b * drv
                e_vec = jnp.sum(dkd_h * (kdec * k), axis=-1, keepdims=True)
                dgc = (b * gam * rs_rk + gam * jnp.sum(dqd_h * q, axis=-1, keepdims=True)
                       + jnp.sum(f_mats[n], axis=-1, keepdims=True) - col_sums[n][:, 0:1] - e_vec)
                is_last = lax.broadcasted_iota(jnp.int32, (ck, 1), 0) == ck - 1
                dgc = dgc + jnp.where(is_last, jnp.sum(e_vec, axis=0, keepdims=True), 0.0)
                dgc_all[g] = dgc_all[g] + jnp.where(lane == h, dgc, 0.0)
                db_all[g] = db_all[g] + jnp.where(lane == nh + h, db, 0.0)
            for g in dgc_all:
                daux_ref[slice(g * ck, (g + 1) * ck), :] = _dot3(suffix, dgc_all[g]) + db_all[g]

    return pl.pallas_call(
        body, grid=(t // (gs * ck),),
        in_specs=_gdn_specs(t, (3 * gw, LANES)) + [pl.BlockSpec((gs, 16, ck), lambda i: (i, 0, 0))]
        + _gdn_specs(t, (sq, gw, gw, gw, gw, gw, gw, gw, gw, LANES)),
        out_specs=_gdn_specs(t, (3 * gw, LANES)),
        out_shape=[jax.ShapeDtypeStruct((t, 3 * gw), F32), jax.ShapeDtypeStruct((t, LANES), F32)],
        compiler_params=_cp(("parallel",)), name=name,
    )(qkvn, aux, aux_t, tinv, u, w, vn, do, dvn, dqd, dkd, dw, dlast)


def _gdn_pre_bwd1(xc, dqkvn, daux, ab, alog_pad, dt_pad, dkv, dp, dp_col, *, name, ab_blk=0, tm=256):
    t, cw = xc.shape
    hd, nh = GDN_HEAD_DIM, GDN_HEADS
    gw = nh * hd
    tm = _blk(t, tm)

    kvw = dkv.shape[1]
    seg = kvw + AB_PAD
    assert dp_col % seg == 0

    def body(xc_ref, dy_ref, daux_ref, ab_ref, al_ref, dt_ref, dkv_ref, _, dxc_ref, dab_ref, dal_ref, ddt_ref):
        i = pl.program_id(0)
        xc = xc_ref[...]
        sg = _sigmoid(xc)
        s = xc * sg
        dsilu = sg * (1.0 + xc * (1.0 - sg))
        for h in range(2 * nh):
            xh = s[:, h * hd:(h + 1) * hd]
            scale = hd ** -0.5 if h < nh else 1.0
            dyh = dy_ref[:, h * hd:(h + 1) * hd] * scale
            r = lax.rsqrt(jnp.sum(xh * xh, axis=-1, keepdims=True) + L2_EPS)
            dxh = r * dyh - xh * (r * r * r) * jnp.sum(dyh * xh, axis=-1, keepdims=True)
            dxc_ref[:, h * hd:(h + 1) * hd] = dxh * dsilu[:, h * hd:(h + 1) * hd]
        dxc_ref[:, 2 * gw:] = dy_ref[:, 2 * gw:] * dsilu[:, 2 * gw:]
        abv = ab_ref[...]
        dauxv = daux_ref[...]
        lane = lax.broadcasted_iota(jnp.int32, abv.shape, 1)
        is_a = lane < nh
        is_b = (lane >= nh) & (lane < 2 * nh)
        pre = abv + dt_ref[...]
        neg_ea = -jnp.exp(al_ref[...])
        d_a = jnp.where(is_a, dauxv * neg_ea * _sigmoid(pre), 0.0)
        beta = _sigmoid(abv)
        d_b = jnp.where(is_b, dauxv * beta * (1.0 - beta), 0.0)
        dab_ref[:, :kvw] = dkv_ref[...]
        dab_ref[:, kvw:kvw + LANES] = (d_a + d_b).astype(dab_ref.dtype)
        dab_ref[:, kvw + LANES:] = jnp.zeros((tm, AB_PAD - LANES), dab_ref.dtype)
        dal = jnp.sum(jnp.where(is_a, dauxv * neg_ea * _softplus(pre), 0.0), axis=0, keepdims=True)
        ddt = jnp.sum(d_a, axis=0, keepdims=True)

        @pl.when(i == 0)
        def _():
            dal_ref[...] = dal
            ddt_ref[...] = ddt

        @pl.when(i > 0)
        def _():
            dal_ref[...] += dal
            ddt_ref[...] += ddt

    row = lambda c: pl.BlockSpec((tm, c), lambda i: (i, 0))
    vec = pl.BlockSpec((1, LANES), lambda i: (0, 0))
    return pl.pallas_call(
        body, grid=(t // tm,),
        in_specs=[row(cw), row(cw), row(LANES), pl.BlockSpec((tm, LANES), lambda i: (i, ab_blk)), vec, vec, row(kvw),
                  ANY_SPEC],
        out_specs=[row(cw), pl.BlockSpec((tm, seg), lambda i: (i, dp_col // seg)), vec, vec],
        out_shape=[jax.ShapeDtypeStruct((t, cw), F32), jax.ShapeDtypeStruct(dp.shape, dp.dtype),
                   jax.ShapeDtypeStruct((1, LANES), F32), jax.ShapeDtypeStruct((1, LANES), F32)],
        input_output_aliases={7: 1}, compiler_params=_cp(("arbitrary",)), name=name,
    )(xc, dqkvn, daux, ab, alog_pad, dt_pad, dkv, dp)


def _gdn_pre_bwd2(dxc, qkvb, conv_w, dp, dp_col, *, name, tm=512):
    t, cw = dxc.shape
    tm = _blk(t, tm)
    hb = tm // HALO
    nblk = t // tm
    cg = GDN_HEADS * GDN_HEAD_DIM
    assert cw % cg == 0 and dp_col % cg == 0
    col0 = dp_col // cg

    def body(d_ref, dn_ref, x_ref, xp_ref, w_ref, _, dx_ref, dw_ref):
        i = pl.program_id(1)
        dcur = d_ref[...]
        dnxt = jnp.where(i < nblk - 1, dn_ref[...], 0.0)
        cur = x_ref[...]
        prev = jnp.where(i > 0, xp_ref[...], 0.0)
        dx = None
        dws = []
        for tap in range(GDN_CONV):
            j = GDN_CONV - 1 - tap
            term = w_ref[tap:tap + 1, :] * _shift_up(dcur, dnxt, j)
            dx = term if dx is None else dx + term
            dws.append(jnp.sum(dcur * _shift_down(cur, prev, j), axis=0, keepdims=True))
        dx_ref[...] = dx.astype(dx_ref.dtype)
        dw = jnp.concatenate(dws, axis=0)

        @pl.when(i == 0)
        def _():
            dw_ref[...] = dw

        @pl.when(i > 0)
        def _():
            dw_ref[...] += dw

    row = pl.BlockSpec((tm, cg), lambda c, i: (i, c))
    wsp = pl.BlockSpec((GDN_CONV, cg), lambda c, i: (0, c))
    return pl.pallas_call(
        body, grid=(cw // cg, nblk),
        in_specs=[row, pl.BlockSpec((HALO, cg), lambda c, i: (jnp.minimum((i + 1) * hb, t // HALO - 1), c)),
                  row, pl.BlockSpec((HALO, cg), lambda c, i: (jnp.maximum(i * hb - 1, 0), c)), wsp, ANY_SPEC],
        out_specs=[pl.BlockSpec((tm, cg), lambda c, i: (i, col0 + c)), wsp],
        out_shape=[jax.ShapeDtypeStruct(dp.shape, dp.dtype), jax.ShapeDtypeStruct((GDN_CONV, cw), F32)],
        input_output_aliases={5: 0}, compiler_params=_cp(("arbitrary", "arbitrary")), name=name,
    )(dxc, dxc, qkvb, qkvb, conv_w, dp)


def _gdn_post_fwd(o, z, norm_w, *, name, tm=512):
    t, gw = o.shape
    hd, nh = GDN_HEAD_DIM, GDN_HEADS
    tm = _blk(t, tm)

    def body(o_ref, z_ref, w_ref, y_ref):
        zv = z_ref[...]
        sz = zv * _sigmoid(zv)
        for h in range(nh):
            oh = o_ref[:, h * hd:(h + 1) * hd]
            r = lax.rsqrt(jnp.mean(oh * oh, axis=-1, keepdims=True) + RMS_EPS)
            y_ref[:, h * hd:(h + 1) * hd] = (oh * r * w_ref[...] * sz[:, h * hd:(h + 1) * hd]).astype(y_ref.dtype)

    row = pl.BlockSpec((tm, gw), lambda i: (i, 0))
    return pl.pallas_call(
        body, grid=(t // tm,), in_specs=[row, row, pl.BlockSpec((1, hd), lambda i: (0, 0))], out_specs=row,
        out_shape=jax.ShapeDtypeStruct((t, gw), BF16), compiler_params=_cp(("parallel",)), name=name,
    )(o, z, norm_w)


def _gdn_post_bwd(dy, o, z, norm_w, dp, dp_col, *, name, tm=512):
    t, gw = o.shape
    hd, nh = GDN_HEAD_DIM, GDN_HEADS
    tm = _blk(t, tm)

    def body(dy_ref, o_ref, z_ref, w_ref, _, do_ref, dz_ref, dw_ref):
        i = pl.program_id(0)
        zv = z_ref[...]
        sg = _sigmoid(zv)
        sz = zv * sg
        dsz = sg * (1.0 + zv * (1.0 - sg))
        dw = None
        for h in range(nh):
            sl = slice(h * hd, (h + 1) * hd)
            oh = o_ref[:, sl]
            dyh = dy_ref[:, sl].astype(F32)
            r = lax.rsqrt(jnp.mean(oh * oh, axis=-1, keepdims=True) + RMS_EPS)
            xh = oh * r
            dz_ref[:, sl] = (dyh * xh * w_ref[...] * dsz[:, sl]).astype(dz_ref.dtype)
            dn = dyh * sz[:, sl]
            dxh = dn * w_ref[...]
            do_ref[:, sl] = r * (dxh - xh * jnp.mean(dxh * xh, axis=-1, keepdims=True))
            part = jnp.sum(dn * xh, axis=0, keepdims=True)
            dw = part if dw is None else dw + part

        @pl.when(i == 0)
        def _():
            dw_ref[...] = dw

        @pl.when(i > 0)
        def _():
            dw_ref[...] += dw

    row = pl.BlockSpec((tm, gw), lambda i: (i, 0))
    vec = pl.BlockSpec((1, hd), lambda i: (0, 0))
    return pl.pallas_call(
        body, grid=(t // tm,), in_specs=[row, row, row, vec, ANY_SPEC],
        out_specs=[row, pl.BlockSpec((tm, gw), lambda i: (i, dp_col // gw)), vec],
        out_shape=[jax.ShapeDtypeStruct((t, gw), F32), jax.ShapeDtypeStruct(dp.shape, dp.dtype),
                   jax.ShapeDtypeStruct((1, hd), F32)],
        input_output_aliases={4: 1}, compiler_params=_cp(("arbitrary",)), name=name,
    )(dy, o, z, norm_w, dp)


IN_NAMES = ("q_a", "kv_a", "qkv_b", "ab", "z", "q_c", "gates")
CAT_NAMES = ("gates", "q_a", "qkv_b", "z", "q_c", "kv_a", "ab")
AB_PAD = 256


def _in_widths(d):
    gw = GDN_HEADS * GDN_HEAD_DIM
    return dict(q_a=SWA_Q_HEADS * SWA_HEAD_DIM, kv_a=2 * SWA_KV_HEADS * SWA_HEAD_DIM, qkv_b=3 * gw, ab=2 * GDN_HEADS,
                z=gw, q_c=XA_HEADS * XA_HEAD_DIM, gates=3 * d)


def _ranges(names, widths):
    out, start = {}, 0
    for k in names:
        out[k] = (start, widths[k])
        start += widths[k]
    return out, start


def _cat_ranges(d):
    widths = dict(_in_widths(d), ab=AB_PAD)
    return _ranges(CAT_NAMES, widths)


def _to_cat(shards, *, name="to_cat", tm=256):
    ns, d, n = shards.shape
    src, _ = _ranges(IN_NAMES, _in_widths(d))
    _, cat_w = _cat_ranges(d)
    pieces = []
    for k in CAT_NAMES:
        lo, hi = src[k][0], src[k][0] + src[k][1]
        for s in range(ns):
            a, b = max(lo, s * n), min(hi, (s + 1) * n)
            if a < b:
                pieces.append((s, a - s * n, b - s * n))
    tm = _blk(d, tm)

    def body(s_ref, o_ref):
        cols = [s_ref[s, :, a:b] for s, a, b in pieces]
        cols.append(jnp.zeros((tm, AB_PAD - src["ab"][1]), o_ref.dtype))
        o_ref[...] = jnp.concatenate(cols, axis=1)

    return pl.pallas_call(
        body, grid=(d // tm,),
        in_specs=[pl.BlockSpec((ns, tm, n), lambda i: (0, i, 0))],
        out_specs=pl.BlockSpec((tm, cat_w), lambda i: (i, 0)),
        out_shape=jax.ShapeDtypeStruct((d, cat_w), shards.dtype),
        compiler_params=_cp(("parallel",)), name=name,
    )(shards)


def _from_cat(w_cat, *, name="from_cat", tm=256):
    d, cat_w = w_cat.shape
    src, total = _ranges(IN_NAMES, _in_widths(d))
    cat, _ = _cat_ranges(d)
    n = total // N_SHARDS
    pieces = []
    for s in range(N_SHARDS):
        pieces.append([])
        for k in IN_NAMES:
            a, b = max(s * n, src[k][0]), min((s + 1) * n, src[k][0] + src[k][1])
            if a < b:
                pieces[s].append((cat[k][0] + a - src[k][0], cat[k][0] + b - src[k][0]))
    tm = _blk(d, tm)

    def body(c_ref, o_ref):
        for s in range(N_SHARDS):
            o_ref[s] = jnp.concatenate([c_ref[:, a:b] for a, b in pieces[s]], axis=1)

    return pl.pallas_call(
        body, grid=(d // tm,),
        in_specs=[pl.BlockSpec((tm, cat_w), lambda i: (i, 0))],
        out_specs=pl.BlockSpec((N_SHARDS, tm, n), lambda i: (0, i, 0)),
        out_shape=jax.ShapeDtypeStruct((N_SHARDS, d, n), w_cat.dtype),
        compiler_params=_cp(("parallel",)), name=name,
    )(w_cat)


def _pad_cols(a, width):
    return jnp.pad(a, ((0, 0), (0, width - a.shape[1])))


def _relu2_epilogue(acc):
    r = jnp.maximum(acc, 0.0)
    return acc, r * r


def _add_epilogue(acc, res):
    return (acc + res,)


def _drelu2_epilogue(acc, u):
    return (acc * (2.0 * jnp.maximum(u.astype(F32), 0.0)),)


def _local_step(x, mem, tgt, wts, small, comm=None):
    t, d = x.shape
    nh = GDN_HEADS
    cat, cat_w = _cat_ranges(d)
    alog_pad = _pad_cols(small["a_log"], LANES)
    dt_pad = _pad_cols(small["dt_bias"], LANES)
    kvw = cat["kv_a"][1]
    assert cat["ab"][0] == cat["kv_a"][0] + kvw
    ab_blk = kvw // LANES

    if comm is None:
        n = _rms_fwd(x, small["g_mix"], name="rms_mix")
        w_cat = wts["w_cat"]
    else:
        n, landed = _rms_fwd(x, small["g_mix"], name="rms_mix", exchange=comm.gather_exchange(["w_in"]))
        w_cat = _to_cat(_exchange_call(comm.pass_on(["w_in"], landed), name="ag_w_in_pass")[0])
    assert w_cat.shape == (d, cat_w)
    q_a = _mm(n, w_cat, b_window=cat["q_a"], out_dtypes=(BF16,), name="in_q_a")
    kv_a, ab = _mm(n, w_cat, b_window=(cat["kv_a"][0], kvw + AB_PAD), out_dtypes=(BF16, F32), name="in_kv_ab")
    qkvb = _mm(n, w_cat, b_window=cat["qkv_b"], tn=512, name="in_qkv_b")
    z = _mm(n, w_cat, b_window=cat["z"], name="in_z")
    q_c = _mm(n, w_cat, b_window=cat["q_c"], out_dtypes=(BF16,), name="in_q_c")
    if comm is None:
        gates = _mm(n, w_cat, b_window=cat["gates"], name="in_gates")
        y_a, lse = _swa_fwd(q_a, kv_a, small["sinks"], name="swa_fwd")
    else:
        gates, landed_mlp = _mm(n, w_cat, b_window=cat["gates"], name="in_gates",
                                exchange=comm.gather_exchange(comm.MLP[1:]))
        y_a, lse, landed = _swa_fwd(q_a, kv_a, small["sinks"], name="swa_fwd", exchange=comm.gather_exchange(comm.MLP[:1]))
        landed_mlp = landed + landed_mlp
    xc, qkvn, aux = _gdn_pre_fwd(qkvb, small["conv_w"], ab, alog_pad, dt_pad, ab_blk=ab_blk, name="gdn_pre_fwd")
    aux_t = aux[:, :16].reshape(t // GDN_CHUNK, GDN_CHUNK, 16).transpose(0, 2, 1)
    if comm is None:
        gdn_u, gdn_w, gdn_qd, gdn_kd, gdn_mm, gdn_tinv = _gdn_local_fwd(qkvn, aux, aux_t, name="gdn_local_fwd")
    else:
        gdn_u, gdn_w, gdn_qd, gdn_kd, gdn_mm, gdn_tinv, landed_mid = _gdn_local_fwd(
            qkvn, aux, aux_t, name="gdn_local_fwd", exchange=comm.gather_exchange(comm.mid))
        wts = dict(wts, **comm.gathered(comm.mid, landed_mid, "mid"))
    o_b, gdn_vn, s_all = _gdn_seq_fwd(gdn_u, gdn_w, gdn_qd, gdn_kd, gdn_mm, aux, name="gdn_seq_fwd")
    y_b = _gdn_post_fwd(o_b, z, small["gdn_norm_w"], name="gdn_post_fwd")
    nmem = _rms_fwd(mem, small["g_mem"], name="rms_mem")
    mkv = _mm(nmem, wts["w_mem_kv"], out_dtypes=(BF16,), name="mem_kv")
    y_c = _xa_fwd(q_c, mkv, name="xa_fwd")
    ys = (y_a, y_b, y_c)
    w_ups = (wts["w_swa_up"], wts["w_gdn_up"], wts["w_xa_up"])
    merged = _merge_fwd(ys, w_ups, gates, name="merge_fwd")
    if comm is None:
        h1 = _mm(merged, wts["w_out"], extras=(x,), epilogue=_add_epilogue, name="out_proj")
    else:
        h1, whole = _mm(merged, wts["w_out"], extras=(x,), epilogue=_add_epilogue, name="out_proj",
                        exchange=comm.pass_on(comm.MLP, landed_mlp))
        wts = dict(wts, **comm.as_weights(comm.MLP, whole))
    n2 = _rms_fwd(h1, small["g_mlp"], name="rms_mlp")
    u, act = _mm(n2, wts["w_mlp_in"], b_sharded=True, out_dtypes=(BF16, BF16), epilogue=_relu2_epilogue, name="mlp_in")
    h2 = _mm(act, wts["w_mlp_out"], extras=(h1,), epilogue=_add_epilogue, name="mlp_out")
    dh2, dh2_b, dg_final, loss = _final_loss(h2, small["g_final"], tgt, name="final_loss")

    grads = {"g_final": dg_final}
    du = _mm(dh2_b, wts["w_mlp_out"], tb=True, out_dtypes=(BF16,), extras=(u,), epilogue=_drelu2_epilogue, name="d_mlp_act")
    grads["w_mlp_out"] = _mm(act, dh2_b, ta=True, out_dtypes=(BF16,), name="dw_mlp_out")
    grads["w_mlp_in"] = _mm(n2, du, ta=True, out_sharded=True, out_dtypes=(BF16,), name="dw_mlp_in")
    if comm is None:
        dn2 = _mm(du, wts["w_mlp_in"], tb=True, b_sharded=True, name="d_mlp_in")
    else:
        g_mlp = [comm.shard_major(k, grads.pop(k)) for k in comm.MLP]
        dn2, sib_mlp = _mm(du, wts["w_mlp_in"], tb=True, b_sharded=True, name="d_mlp_in", exchange=_sibling_halves(g_mlp))
        s1_mlp = comm.pair_sums(g_mlp, "mlp", sib_mlp)
    dh1, dh1_b, grads["g_mlp"] = _rms_bwd(dn2, h1, small["g_mlp"], dh2, name="rms_mlp_bwd")
    dmerged = _mm(dh1_b, wts["w_out"], tb=True, name="d_out_proj")
    grads["w_out"] = _mm(merged, dh1_b, ta=True, out_dtypes=(BF16,), name="dw_out")
    *dus, dp = _merge_bwd(ys, w_ups, gates, dmerged, cat_w, name="merge_bwd")
    dys = []
    for y, du_i, w_up, key in zip(ys, dus, w_ups, ("w_swa_up", "w_gdn_up", "w_xa_up")):
        dys.append(_mm(du_i, w_up, tb=True, b_sharded=True, out_dtypes=(BF16,), name="d_" + key))
        grads[key] = _mm(y, du_i, ta=True, out_sharded=True, out_dtypes=(BF16,), name="dw_" + key[2:])
    dp, dkv_a, grads["sinks"] = _swa_bwd(q_a, kv_a, small["sinks"], y_a, lse, dys[0], dp, cat["q_a"][0], name="swa_bwd")
    do_b, dp, grads["gdn_norm_w"] = _gdn_post_bwd(dys[1], o_b, z, small["gdn_norm_w"], dp, cat["z"][0],
                                                  name="gdn_post_bwd")
    dvn, dqd, dkd, dw_, dlast = _gdn_seq_bwd(do_b, gdn_w, gdn_qd, gdn_kd, gdn_mm, gdn_vn, s_all, aux, name="gdn_seq_bwd")
    dqkvn, daux = _gdn_local_bwd(qkvn, aux, aux_t, gdn_tinv, gdn_u, gdn_w, gdn_vn, do_b, dvn, dqd, dkd, dw_, dlast,
                                 name="gdn_local_bwd")
    dxc, dp, dalog, ddt = _gdn_pre_bwd1(xc, dqkvn, daux, ab, alog_pad, dt_pad, dkv_a, dp, cat["kv_a"][0], ab_blk=ab_blk,
                                        name="gdn_pre_bwd1")
    grads["a_log"], grads["dt_bias"] = dalog[:, :nh], ddt[:, :nh]
    dp, grads["conv_w"] = _gdn_pre_bwd2(dxc, qkvb, small["conv_w"], dp, cat["qkv_b"][0], name="gdn_pre_bwd2")
    dp, dmkv = _xa_bwd(q_c, mkv, dys[2], dp, cat["q_c"][0], name="xa_bwd")
    grads["w_mem_kv"] = _mm(nmem, dmkv, ta=True, out_dtypes=(BF16,), name="dw_mem_kv")
    dnmem = _mm(dmkv, wts["w_mem_kv"], tb=True, name="d_mem_kv")
    _, _, grads["g_mem"] = _rms_bwd(dnmem, mem, small["g_mem"], jnp.zeros_like(mem), name="rms_mem_bwd")
    if comm is None:
        grads["w_cat"] = _mm(n, dp, ta=True, out_dtypes=(BF16,), name="dw_in")
        dn = _mm(dp, w_cat, tb=True, name="d_in_proj")
    else:
        s1_mid = comm.pair_sums([comm.shard_major(k, grads.pop(k)) for k in comm.mid], "mid")
        dw_cat, rcv_mlp = _mm(n, dp, ta=True, out_dtypes=(BF16,), name="dw_in", exchange=_chip_exchange(s1_mlp))
        s1_in = comm.pair_sums([_from_cat(dw_cat)], "in")
        dn, rcv_rest = _mm(dp, w_cat, tb=True, name="d_in_proj", exchange=_chip_exchange(s1_in + s1_mid))
        halves = comm.chip_sums(s1_in + s1_mid + s1_mlp, rcv_rest + rcv_mlp)
        reduced = _exchange_call(_join_halves(halves), name="rs_join_halves")
        grads.update(zip(["w_in"] + comm.mid + list(comm.MLP), reduced))
    dx, _, grads["g_mix"] = _rms_bwd(dn, x, small["g_mix"], dh1, name="rms_mix_bwd")
    return loss, dx, grads


HBM_SPEC = pl.BlockSpec(memory_space=pltpu.HBM)
VMEM_SPEC = pl.BlockSpec(memory_space=pltpu.VMEM)
N_CHIPS = N_SHARDS
N_DEV = 8
DMA_CHUNK_BYTES = 1 << 20


def _place():
    return lax.axis_index("x"), lax.axis_index("y"), lax.axis_index("c")


def _other_chips(x, y):
    return [(1 - x, y), (x, 1 - y), (1 - x, 1 - y)]


def _n_chunks(rows, row_bytes):
    n = 1
    while rows % (2 * n) == 0 and (rows // (2 * n)) % 16 == 0 and (rows // n) * row_bytes > DMA_CHUNK_BYTES:
        n *= 2
    return n


def _sem_scratch(n_remote, n_local):
    return [pltpu.SemaphoreType.DMA((max(n_remote, 1),)), pltpu.SemaphoreType.DMA((max(n_remote, 1),)),
            pltpu.SemaphoreType.DMA((max(n_local, 1),))]


def _gather_over_ici(shards):
    plan = _half_chunks(shards, 0)

    def copies_of(in_refs, out_refs, place):
        x, y, c = place
        remote = []
        for i, r0, nr in plan:
            mine = pl.ds(c * (shards[i].shape[0] // 2) + r0, nr)
            for chip in _other_chips(x, y):
                remote.append((in_refs[i].at[mine], out_refs[i].at[2 * x + y, mine], (*chip, c)))
        return remote, []

    shapes = tuple(jax.ShapeDtypeStruct((N_CHIPS, *s.shape), s.dtype) for s in shards)
    return Exchange(tuple(shards), shapes, 3 * len(plan), 0, copies_of)


def _gather_pass_on(arrived):
    plan = _half_chunks([jax.ShapeDtypeStruct(a.shape[1:], a.dtype) for a in arrived], 0)

    def copies_of(in_refs, out_refs, place):
        x, y, c = place
        remote = []
        for i, r0, nr in plan:
            mine = pl.ds(c * (arrived[i].shape[1] // 2) + r0, nr)
            for chip in _other_chips(x, y):
                rows = out_refs[i].at[2 * chip[0] + chip[1], mine]
                remote.append((rows, rows, (x, y, 1 - c)))
        return remote, []

    shapes = tuple(jax.ShapeDtypeStruct(a.shape, a.dtype) for a in arrived)
    return Exchange(tuple(arrived), shapes, 3 * len(plan), 0, copies_of, tuple((i, i) for i in range(len(arrived))))


def _place_own(arrived, shard, chip, *, name):
    r, c = shard.shape
    tb = _row_block(r, c)

    def body(chip_ref, s_ref, _, o_ref):
        o_ref[...] = s_ref[...]

    return pl.pallas_call(
        body, grid_spec=pltpu.PrefetchScalarGridSpec(
            num_scalar_prefetch=1, grid=(r // tb,),
            in_specs=[pl.BlockSpec((tb, c), lambda i, chip_ref: (i, 0)), ANY_SPEC],
            out_specs=pl.BlockSpec((None, tb, c), lambda i, chip_ref: (chip_ref[0], i, 0))),
        out_shape=jax.ShapeDtypeStruct(arrived.shape, arrived.dtype), input_output_aliases={2: 0},
        compiler_params=_cp(("parallel",)), name=name,
    )(chip, shard, arrived)


def _exchange_call(ex, *, name):
    n_in, n_out = len(ex.ins), len(ex.out_shapes)

    def body(*refs):
        cps = _exchange_copies(ex, refs[:n_in], refs[n_in:n_in + n_out], refs[n_in + n_out:])
        for cp in cps:
            cp.start()
        for cp in cps:
            cp.wait()

    return pl.pallas_call(
        body, out_shape=list(ex.out_shapes), in_specs=[HBM_SPEC] * n_in, out_specs=[HBM_SPEC] * n_out,
        scratch_shapes=_sem_scratch(ex.n_remote, ex.n_local), input_output_aliases=dict(ex.aliases), name=name,
    )(*ex.ins)


def _half_chunks(arrs, row_axis):
    plan = []
    for i, a in enumerate(arrs):
        rh = a.shape[row_axis] // 2
        row_bytes = a.dtype.itemsize * math.prod(a.shape) // a.shape[row_axis]
        nch = _n_chunks(rh, row_bytes)
        plan += [(i, q * (rh // nch), rh // nch) for q in range(nch)]
    return plan


def _sibling_halves(gs):
    plan = _half_chunks(gs, 1)

    def copies_of(in_refs, out_refs, place):
        x, y, c = place
        out = []
        for i, r0, nr in plan:
            rh = gs[i].shape[1] // 2
            out.append((in_refs[i].at[:, pl.ds((1 - c) * rh + r0, nr), :], out_refs[i].at[:, pl.ds(r0, nr), :],
                        (x, y, 1 - c)))
        return out, []

    shapes = tuple(jax.ShapeDtypeStruct((g.shape[0], g.shape[1] // 2, g.shape[2]), g.dtype) for g in gs)
    return Exchange(tuple(gs), shapes, len(plan), 0, copies_of)


def _chip_exchange(s1s):
    plan = _half_chunks([jax.ShapeDtypeStruct((2 * s.shape[1], s.shape[2]), s.dtype) for s in s1s], 0)

    def copies_of(in_refs, out_refs, place):
        x, y, c = place
        out = []
        for i, r0, nr in plan:
            for j, chip in enumerate(_other_chips(x, y)):
                out.append((in_refs[i].at[2 * chip[0] + chip[1], pl.ds(r0, nr), :], out_refs[i].at[j, pl.ds(r0, nr), :],
                            (*chip, c)))
        return out, []

    shapes = tuple(jax.ShapeDtypeStruct((3, *s.shape[1:]), s.dtype) for s in s1s)
    return Exchange(tuple(s1s), shapes, 3 * len(plan), 0, copies_of)


def _join_halves(gs):
    plan = _half_chunks(gs, 0)

    def copies_of(in_refs, out_refs, place):
        x, y, c = place
        out = []
        for i, r0, nr in plan:
            rows = out_refs[i].at[pl.ds(c * (gs[i].shape[0] // 2) + r0, nr), :]
            out.append((rows, rows, (x, y, 1 - c)))
        return out, []

    shapes = tuple(jax.ShapeDtypeStruct(g.shape, g.dtype) for g in gs)
    aliases = tuple((i, i) for i in range(len(gs)))
    return Exchange(tuple(gs), shapes, len(plan), 0, copies_of, aliases)


def _row_block(rows, cols):
    tb = rows
    while tb % 32 == 0 and tb * cols * 4 > (2 << 20):
        tb //= 2
    return tb


def _pair_sum(g, sib, core, *, name):
    ns, r, c = g.shape
    rh = r // 2
    tb = _row_block(rh, c)
    nb = rh // tb

    def body(core_ref, g_ref, s_ref, o_ref):
        o_ref[...] = (g_ref[...].astype(F32) + s_ref[...].astype(F32)).astype(o_ref.dtype)

    mine = pl.BlockSpec((None, tb, c), lambda s, i, core_ref: (s, core_ref[0] * nb + i, 0))
    half = pl.BlockSpec((None, tb, c), lambda s, i, core_ref: (s, i, 0))
    return pl.pallas_call(
        body, grid_spec=pltpu.PrefetchScalarGridSpec(num_scalar_prefetch=1, grid=(ns, nb), in_specs=[mine, half],
                                                     out_specs=half),
        out_shape=jax.ShapeDtypeStruct((ns, rh, c), BF16), compiler_params=_cp(("parallel", "parallel")), name=name,
    )(core, g, sib)


def _chip_sum(s1, rcv, where, *, name):
    _, rh, c = s1.shape
    tb = _row_block(rh, c)
    nb = rh // tb

    def body(where_ref, own_ref, r0_ref, r1_ref, r2_ref, o_ref):
        acc = own_ref[...].astype(F32)
        for r in (r0_ref, r1_ref, r2_ref):
            acc = acc + r[...].astype(F32)
        o_ref[...] = acc

    own = pl.BlockSpec((None, tb, c), lambda i, w: (w[1], i, 0))
    got = [pl.BlockSpec((None, tb, c), functools.partial(lambda i, w, j: (j, i, 0), j=j)) for j in range(3)]
    return pl.pallas_call(
        body, grid_spec=pltpu.PrefetchScalarGridSpec(
            num_scalar_prefetch=1, grid=(nb,), in_specs=[own] + got,
            out_specs=pl.BlockSpec((tb, c), lambda i, w: (w[0] * nb + i, 0))),
        out_shape=jax.ShapeDtypeStruct((2 * rh, c), F32), compiler_params=_cp(("parallel",)), name=name,
    )(where, s1, rcv, rcv, rcv)


def _all_gather_small(blk, *, name):
    r = blk.shape[0]

    def body(b_ref, out_ref, send_sems, recv_sems):
        x, y, c = _place()
        me = 4 * x + 2 * y + c
        out_ref[me] = b_ref[...]
        sends = []
        for k in range(1, N_DEV):
            peer = (x ^ (k >> 2), y ^ ((k >> 1) & 1), c ^ (k & 1))
            sends.append(pltpu.make_async_remote_copy(src_ref=b_ref, dst_ref=out_ref.at[me], send_sem=send_sems.at[k - 1],
                                                      recv_sem=recv_sems.at[k - 1], device_id=peer, device_id_type=MESH))
        for cp in sends:
            cp.start()
        for k in range(1, N_DEV):
            rows = out_ref.at[me ^ k]
            pltpu.make_async_remote_copy(src_ref=rows, dst_ref=rows, send_sem=send_sems.at[k - 1],
                                         recv_sem=recv_sems.at[k - 1], device_id=(x, y, c), device_id_type=MESH).wait_recv()
        for cp in sends:
            cp.wait_send()

    return pl.pallas_call(
        body, out_shape=jax.ShapeDtypeStruct((N_DEV, r, LANES), blk.dtype), in_specs=[VMEM_SPEC], out_specs=VMEM_SPEC,
        scratch_shapes=[pltpu.SemaphoreType.DMA((N_DEV - 1,)), pltpu.SemaphoreType.DMA((N_DEV - 1,))],
        name=name,
    )(blk)


def _sum_rows(parts, out_dtype, *, name, tb=1024):
    rows = parts[0].shape[0]
    tb = _blk(rows, tb)

    def body(*refs):
        acc = refs[0][...].astype(F32)
        for r in refs[1:-1]:
            acc = acc + r[...].astype(F32)
        refs[-1][...] = acc.astype(refs[-1].dtype)

    spec = pl.BlockSpec((tb, LANES), lambda i: (i, 0))
    return pl.pallas_call(
        body, grid=(rows // tb,), in_specs=[spec] * len(parts), out_specs=spec,
        out_shape=jax.ShapeDtypeStruct((rows, LANES), out_dtype), compiler_params=_cp(("parallel",)), name=name,
    )(*parts)


BIG = (
    ("w_in", 1), ("w_mem_kv", 0), ("w_swa_up", 1), ("w_gdn_up", 1), ("w_xa_up", 1), ("w_out", 0), ("w_mlp_in", 1),
    ("w_mlp_out", 0))


class _Comm:
    MLP = ("w_mlp_in", "w_mlp_out")

    def __init__(self, late_shards, core, where):
        self.axis = dict(BIG)
        self.late_shards = late_shards
        self.mid = [k for k in late_shards if k not in self.MLP and k != "w_in"]
        self.core, self.where = core, where

    def gather_exchange(self, names):
        return _gather_over_ici([self.late_shards[k] for k in names])

    def as_weights(self, names, whole):
        return {k: (g.reshape(-1, g.shape[2]) if self.axis[k] == 0 else g) for k, g in zip(names, whole)}

    def gathered(self, names, landed, tag):
        return self.as_weights(names, _exchange_call(self.pass_on(names, landed), name=f"ag_{tag}_pass"))

    def pass_on(self, names, landed):
        chip = self.where[1:2]
        return _gather_pass_on([_place_own(a, self.late_shards[k], chip, name=f"ag_own_{k}") for k, a in zip(names, landed)])

    def shard_major(self, k, grad):
        return grad.reshape(N_CHIPS, -1, grad.shape[-1]) if self.axis[k] == 0 else grad

    def pair_sums(self, gs, tag, sibs=None):
        if sibs is None:
            sibs = _exchange_call(_sibling_halves(gs), name=f"rs_sibling_{tag}")
        return [_pair_sum(g, s, self.core, name=f"rs_pair_sum_{tag}{i}") for i, (g, s) in enumerate(zip(gs, sibs))]

    def chip_sums(self, s1s, rcvs):
        return [_chip_sum(s1, rcv, self.where, name=f"rs_chip_sum_{i}") for i, (s1, rcv) in enumerate(zip(s1s, rcvs))]
SMALL = ("g_mix", "sinks", "a_log", "dt_bias", "gdn_norm_w", "g_mem", "g_mlp", "g_final")


def _rows128(a, rows):
    flat = a.reshape(-1)
    return jnp.pad(flat, (0, rows * LANES - flat.shape[0])).reshape(rows, LANES)


def kernel(x, mem, g_mix, w_in, sinks, conv_w, a_log, dt_bias, gdn_norm_w, g_mem, w_mem_kv, w_swa_up, w_gdn_up, w_xa_up, w_out, g_mlp, w_mlp_in, w_mlp_out, g_final, loss_target, m_g_mix, m_w_in, m_sinks, m_conv_w, m_a_log, m_dt_bias, m_gdn_norm_w, m_g_mem, m_w_mem_kv, m_w_swa_up, m_w_gdn_up, m_w_xa_up, m_w_out, m_g_mlp, m_w_mlp_in, m_w_mlp_out, m_g_final, v_g_mix, v_w_in, v_sinks, v_conv_w, v_a_log, v_dt_bias, v_gdn_norm_w, v_g_mem, v_w_mem_kv, v_w_swa_up, v_w_gdn_up, v_w_xa_up, v_w_out, v_g_mlp, v_w_mlp_in, v_w_mlp_out, v_g_final):
    given = dict(locals())
    xi, yi, ci = _place()
    chip = 2 * xi + yi
    core = jnp.reshape(ci, (1,)).astype(jnp.int32)
    where = jnp.stack([ci, chip]).astype(jnp.int32)

    comm = _Comm({k: given[k][0].astype(BF16) for k, _ in BIG}, core, where)
    wts = {}
    conv_shard = conv_w[0]
    conv_rows = -(-conv_shard.size // (8 * LANES)) * 8
    conv_all = _all_gather_small(_rows128(conv_shard, conv_rows), name="ag_conv")
    conv_full = jnp.concatenate(
        [conv_all[2 * s].reshape(-1)[:conv_shard.size].reshape(conv_shard.shape) for s in range(N_CHIPS)], axis=1)

    small = {k: given[k].reshape(1, -1) for k in SMALL}
    small["conv_w"] = conv_full
    loss_row, dx, grads = _local_step(x[0], mem[0], loss_target[0], wts, small, comm)
    big_grads = {k: grads[k] for k, _ in BIG}

    layout = [("loss", loss_row[:, :1])] + [(k, grads[k]) for k in SMALL] + [("conv_w", grads["conv_w"])]
    rows = [-(-a.size // LANES) for _, a in layout]
    blk_rows = -(-sum(rows) // 8) * 8
    blk = jnp.concatenate([_rows128(a.astype(F32), n) for (_, a), n in zip(layout, rows)]
                          + [jnp.zeros((blk_rows - sum(rows), LANES), F32)], axis=0)
    gathered = _all_gather_small(blk, name="ag_small_grads")
    reduced = _sum_rows([gathered[i] for i in range(N_DEV)], F32, name="small_grad_sum")
    small_grads, start = {}, 0
    for (k, a), n in zip(layout, rows):
        small_grads[k] = reduced[start:start + n].reshape(-1)[:a.size].reshape(a.shape)
        start += n
    loss = small_grads["loss"].reshape(())
    cw = conv_shard.shape[1]
    conv_grad = lax.dynamic_slice_in_dim(small_grads["conv_w"], chip * cw, cw, axis=1)

    names = ["g_mix", "w_in", "sinks", "conv_w", "a_log", "dt_bias", "gdn_norm_w", "g_mem", "w_mem_kv", "w_swa_up",
             "w_gdn_up", "w_xa_up", "w_out", "g_mlp", "w_mlp_in", "w_mlp_out", "g_final"]
    out_g, out_d, out_m, out_v = [], [], [], []
    for k in names:
        w, m, v = given[k], given["m_" + k], given["v_" + k]
        if k in big_grads:
            g2 = big_grads[k]
        elif k == "conv_w":
            g2 = conv_grad
        else:
            g2 = small_grads[k]
        as_given = (lambda a: a.reshape(1, -1)) if w.ndim == 1 else (lambda a: a)
        if w.shape[-1] % LANES and w.shape[-1] > LANES:
            tr = lambda a: jnp.swapaxes(a, -1, -2)
            g_out, delta, new_m, new_v = (tr(a) for a in _adamw(tr(w), tr(g2), tr(m), tr(v), name="adamw_" + k))
        else:
            g_out, delta, new_m, new_v = _adamw(as_given(w), g2, as_given(m), as_given(v), name="adamw_" + k)
        out_g.append(g_out.reshape(w.shape))
        out_d.append(delta.reshape(w.shape))
        out_m.append(new_m.reshape(w.shape))
        out_v.append(new_v.reshape(w.shape))
    return (loss, dx[None], *out_g, *out_d, *out_m, *out_v)
```

```python
import functools
import math
from typing import Callable, NamedTuple

import jax
import jax.numpy as jnp
from jax import lax
from jax.experimental import pallas as pl
from jax.experimental.pallas import tpu as pltpu

F32 = jnp.float32
BF16 = jnp.bfloat16
HI = lax.Precision.HIGHEST
MESH = pl.DeviceIdType.MESH

SWA_Q_HEADS = 16
SWA_KV_HEADS = 2
SWA_HEAD_DIM = 64
SWA_WINDOW = 128
SWA_SCALE = SWA_HEAD_DIM ** -0.5
assert math.frexp(SWA_SCALE)[0] == 0.5
GDN_HEADS = 4
GDN_HEAD_DIM = 128
GDN_CONV = 4
GDN_CHUNK = 64
XA_HEADS = 4
XA_HEAD_DIM = 128
RMS_EPS = 1e-6
L2_EPS = 1e-6
ADAM_LR = 0.001
ADAM_B1 = 0.9
ADAM_B2 = 0.999
ADAM_EPS = 1e-08
ADAM_WD = 0.01
ADAM_STEP = 10

LANES = 128
N_SHARDS = 4
VMEM_LIMIT = 56 * 1024 * 1024

NT = (((1,), (1,)), ((), ()))
TN = (((0,), (0,)), ((), ()))
NN = (((1,), (0,)), ((), ()))


def _cp(sem=None):
    return pltpu.CompilerParams(dimension_semantics=sem, vmem_limit_bytes=VMEM_LIMIT)


def _blk(dim, pref):
    if dim <= pref:
        return dim
    b = (pref // LANES) * LANES
    while dim % b:
        b -= LANES
    assert b > 0, (dim, pref)
    return b


def _dot(a, b, dims=NN, precision=None):
    return lax.dot_general(a, b, dims, precision=precision, preferred_element_type=F32)


def _sigmoid(x):
    return 0.5 * jnp.tanh(0.5 * x) + 0.5


MM_TK_BYTES = 4096


def _mm(a, b, *, name, ta=False, tb=False, out_dtypes=(F32,), epilogue=None, extras=(), tm=1024, tn=1024, tk=None,
        b_sharded=False, out_sharded=False, b_window=None, exchange=None):
    (kdim, m) = a.shape if ta else a.shape[::-1]
    col0 = 0
    n_lim = k_lim = None
    if b_sharded:
        ns, rows_w, per = b.shape
        if tb:
            kb, n, k_lim = ns * per, rows_w, per
        else:
            kb, n, n_lim = rows_w, ns * per, per
    else:
        (kb, n) = b.shape[::-1] if tb else b.shape
        if b_window is not None:
            assert not tb
            col0, n = b_window
    assert kdim == kb, (a.shape, b.shape, ta, tb)
    if out_sharded:
        assert n % N_SHARDS == 0
        n_lim = n // N_SHARDS if n_lim is None else n_lim
        assert n_lim == n // N_SHARDS
    if tk is None:
        tk = MM_TK_BYTES // max(a.dtype.itemsize, b.dtype.itemsize)
    tm, tn, tk = _blk(m, tm), _blk(n_lim or n, tn), _blk(k_lim or kdim, tk)
    assert col0 % tn == 0, (col0, tn)
    nk = kdim // tk
    a_spec = pl.BlockSpec((tk, tm), lambda i, j, k: (k, i)) if ta else pl.BlockSpec((tm, tk), lambda i, j, k: (i, k))
    if b_sharded and tb:
        kpb = k_lim // tk
        b_spec = pl.BlockSpec((None, tn, tk), lambda i, j, k: (k // kpb, j, k % kpb))
    elif b_sharded:
        bpb = n_lim // tn
        b_spec = pl.BlockSpec((None, tk, tn), lambda i, j, k: (j // bpb, k, j % bpb))
    elif tb:
        b_spec = pl.BlockSpec((tn, tk), lambda i, j, k: (j, k))
    else:
        b_spec = pl.BlockSpec((tk, tn), lambda i, j, k: (k, j + col0 // tn))
    x_spec = pl.BlockSpec((tm, tn), lambda i, j, k: (i, j))
    if out_sharded:
        opb = n_lim // tn
        o_spec = pl.BlockSpec((None, tm, tn), lambda i, j, k: (j // opb, i, j % opb))
        out_shape = (N_SHARDS, m, n_lim)
    else:
        o_spec, out_shape = x_spec, (m, n)
    dims = ((((0 if ta else 1),), ((1 if tb else 0),)), ((), ()))
    n_extra, n_out = len(extras), len(out_dtypes)

    host = _ExchangeHost(exchange)
    grid = (m // tm, n // tn, nk)

    def body(*refs):
        a_ref, b_ref = refs[:2]
        extra_refs = refs[2:2 + n_extra]
        out_refs = refs[2 + n_extra + host.n_in:2 + n_extra + host.n_in + n_out]
        host.start(refs, 2 + n_extra, 2 + n_extra + host.n_in + n_out, grid)
        part = _dot(a_ref[...].astype(BF16), b_ref[...].astype(BF16), dims)

        def finish(acc):
            vals = epilogue(acc, *[r[...] for r in extra_refs]) if epilogue is not None else (acc,) * n_out
            assert len(vals) == n_out
            for r, v in zip(out_refs, vals):
                r[...] = v.astype(r.dtype)

        if nk == 1:
            finish(part)
        else:
            acc_ref = refs[2 + n_extra + host.n_in + n_out + host.n_out]
            k = pl.program_id(2)

            @pl.when(k == 0)
            def _():
                acc_ref[...] = part

            @pl.when((k > 0) & (k < nk - 1))
            def _():
                acc_ref[...] += part

            @pl.when(k == nk - 1)
            def _():
                finish(acc_ref[...] + part)

        host.wait(refs, 2 + n_extra, 2 + n_extra + host.n_in + n_out, grid)

    outs = pl.pallas_call(
        body,
        grid=grid,
        in_specs=[a_spec, b_spec] + [x_spec] * n_extra + host.in_specs,
        out_specs=[o_spec] * n_out + host.out_specs,
        out_shape=[jax.ShapeDtypeStruct(out_shape, d) for d in out_dtypes] + host.out_shapes,
        scratch_shapes=([pltpu.VMEM((tm, tn), F32)] if nk > 1 else []) + host.scratch,
        input_output_aliases=host.aliases(2 + n_extra, n_out),
        compiler_params=_cp(host.semantics(("parallel", "parallel", "arbitrary"))),
        name=name,
    )(a, b, *extras, *host.ins)
    mine, landed = outs[:n_out], list(outs[n_out:])
    mine = mine[0] if n_out == 1 else mine
    return (mine, landed) if exchange is not None else mine


class Exchange(NamedTuple):
    ins: tuple
    out_shapes: tuple
    n_remote: int
    n_local: int
    copies_of: Callable
    aliases: tuple = ()


def _exchange_copies(ex, in_refs, out_refs, sem_refs):
    send_sems, recv_sems, local_sems = sem_refs
    remote, local = ex.copies_of(in_refs, out_refs, _place())
    assert len(remote) == ex.n_remote and len(local) == ex.n_local, (len(remote), len(local))
    cps = [pltpu.make_async_remote_copy(src_ref=src, dst_ref=dst, send_sem=send_sems.at[k], recv_sem=recv_sems.at[k],
                                        device_id=to, device_id_type=MESH) for k, (src, dst, to) in enumerate(remote)]
    cps += [pltpu.make_async_copy(src, dst, local_sems.at[k]) for k, (src, dst) in enumerate(local)]
    return cps


class _ExchangeHost:
    def __init__(self, ex):
        self.ex = ex
        self.ins = list(ex.ins) if ex else []
        self.out_shapes = list(ex.out_shapes) if ex else []
        self.n_in, self.n_out = len(self.ins), len(self.out_shapes)
        self.in_specs = [HBM_SPEC] * self.n_in
        self.out_specs = [HBM_SPEC] * self.n_out
        self.scratch = _sem_scratch(ex.n_remote, ex.n_local) if ex else []

    def semantics(self, sem):
        return tuple("arbitrary" for _ in sem) if self.ex else sem

    def aliases(self, in_at, out_at):
        return {in_at + i: out_at + o for i, o in self.ex.aliases} if self.ex else {}

    def _refs(self, refs, in_at, out_at):
        return refs[in_at:in_at + self.n_in], refs[out_at:out_at + self.n_out], refs[len(refs) - 3:]

    def _when(self, grid, last):
        cond = None
        for d, size in enumerate(grid):
            c = pl.program_id(d) == (size - 1 if last else 0)
            cond = c if cond is None else cond & c
        return cond

    def start(self, refs, in_at, out_at, grid):
        if self.ex:
            @pl.when(self._when(grid, False))
            def _():
                for cp in _exchange_copies(self.ex, *self._refs(refs, in_at, out_at)):
                    cp.start()

    def wait(self, refs, in_at, out_at, grid):
        if self.ex:
            @pl.when(self._when(grid, True))
            def _():
                for cp in _exchange_copies(self.ex, *self._refs(refs, in_at, out_at)):
                    cp.wait()


def _rms_fwd(x, g, *, name, tm=512, exchange=None):
    t, d = x.shape
    tm = _blk(t, tm)
    host = _ExchangeHost(exchange)
    grid = (t // tm,)

    def body(*refs):
        x_ref, g_ref, n_ref = refs[0], refs[1], refs[2 + host.n_in]
        host.start(refs, 2, 3 + host.n_in, grid)
        xv = x_ref[...]
        r = lax.rsqrt(jnp.mean(xv * xv, axis=-1, keepdims=True) + RMS_EPS)
        n_ref[...] = (xv * r * g_ref[...]).astype(n_ref.dtype)
        host.wait(refs, 2, 3 + host.n_in, grid)

    outs = pl.pallas_call(
        body, grid=grid,
        in_specs=[pl.BlockSpec((tm, d), lambda i: (i, 0)), pl.BlockSpec((1, d), lambda i: (0, 0))] + host.in_specs,
        out_specs=[pl.BlockSpec((tm, d), lambda i: (i, 0))] + host.out_specs,
        out_shape=[jax.ShapeDtypeStruct((t, d), BF16)] + host.out_shapes,
        scratch_shapes=host.scratch, input_output_aliases=host.aliases(2, 1),
        compiler_params=_cp(host.semantics(("parallel",))), name=name,
    )(x, g, *host.ins)
    return (outs[0], list(outs[1:])) if exchange is not None else outs[0]


def _rms_bwd(dn, x, g, dres, *, name, tm=512):
    t, d = x.shape
    tm = _blk(t, tm)

    def body(dn_ref, x_ref, g_ref, dres_ref, dx_ref, dxb_ref, dg_ref):
        i = pl.program_id(0)
        xv = x_ref[...]
        r = lax.rsqrt(jnp.mean(xv * xv, axis=-1, keepdims=True) + RMS_EPS)
        xh = xv * r
        dnv = dn_ref[...].astype(F32)
        dxh = dnv * g_ref[...]
        dx = dres_ref[...] + r * (dxh - xh * jnp.mean(dxh * xh, axis=-1, keepdims=True))
        dx_ref[...] = dx
        dxb_ref[...] = dx.astype(dxb_ref.dtype)
        part = jnp.sum(dnv * xh, axis=0, keepdims=True)

        @pl.when(i == 0)
        def _():
            dg_ref[...] = part

        @pl.when(i > 0)
        def _():
            dg_ref[...] += part

    row = pl.BlockSpec((tm, d), lambda i: (i, 0))
    vec = pl.BlockSpec((1, d), lambda i: (0, 0))
    return pl.pallas_call(
        body, grid=(t // tm,),
        in_specs=[row, row, vec, row], out_specs=[row, row, vec],
        out_shape=[jax.ShapeDtypeStruct((t, d), F32), jax.ShapeDtypeStruct((t, d), BF16),
                   jax.ShapeDtypeStruct((1, d), F32)],
        compiler_params=_cp(("arbitrary",)), name=name,
    )(dn, x, g, dres)


def _final_loss(h, g, tgt, *, name, tm=512):
    t, d = h.shape
    tm = _blk(t, tm)

    def body(h_ref, g_ref, t_ref, dh_ref, dhb_ref, dg_ref, loss_ref):
        i = pl.program_id(0)
        hv = h_ref[...]
        r = lax.rsqrt(jnp.mean(hv * hv, axis=-1, keepdims=True) + RMS_EPS)
        xh = hv * r
        e = xh * g_ref[...] - t_ref[...]
        dy = e * (1.0 / d)
        dxh = dy * g_ref[...]
        dh = r * (dxh - xh * jnp.mean(dxh * xh, axis=-1, keepdims=True))
        dh_ref[...] = dh
        dhb_ref[...] = dh.astype(dhb_ref.dtype)
        dg_part = jnp.sum(dy * xh, axis=0, keepdims=True)
        row_loss = jnp.sum(e * e, axis=-1, keepdims=True) * (0.5 / d)
        loss_part = jnp.sum(row_loss, axis=0, keepdims=True)

        @pl.when(i == 0)
        def _():
            dg_ref[...] = dg_part
            loss_ref[...] = jnp.broadcast_to(loss_part, loss_ref.shape)

        @pl.when(i > 0)
        def _():
            dg_ref[...] += dg_part
            loss_ref[...] += jnp.broadcast_to(loss_part, loss_ref.shape)

    row = pl.BlockSpec((tm, d), lambda i: (i, 0))
    vec = pl.BlockSpec((1, d), lambda i: (0, 0))
    return pl.pallas_call(
        body, grid=(t // tm,),
        in_specs=[row, vec, row], out_specs=[row, row, vec, pl.BlockSpec((1, LANES), lambda i: (0, 0))],
        out_shape=[jax.ShapeDtypeStruct((t, d), F32), jax.ShapeDtypeStruct((t, d), BF16),
                   jax.ShapeDtypeStruct((1, d), F32), jax.ShapeDtypeStruct((1, LANES), F32)],
        compiler_params=_cp(("arbitrary",)), name=name,
    )(h, g, tgt)


SWA_SUB = 64


def _swa_mask(n, rows, row0):
    w = SWA_WINDOW
    qi = (lax.broadcasted_iota(jnp.int32, (rows, 2 * w), 0) + row0) & (w - 1)
    kj = lax.broadcasted_iota(jnp.int32, (rows, 2 * w), 1)
    return (kj > qi) & (kj <= qi + w) & ((n > 0) | (kj >= w))


def _stack_heads(ref, heads, width):
    return jnp.concatenate([ref[:, h * width:(h + 1) * width] for h in heads], axis=0)


def _stack_scalars(ref, heads, rows):
    return jnp.concatenate([jnp.broadcast_to(ref[0:1, h:h + 1], (rows, 1)) for h in heads], axis=0)


def _swa_fwd(q, kv, sinks, *, name, exchange=None):
    t = q.shape[0]
    w, hd, hq, hkv = SWA_WINDOW, SWA_HEAD_DIM, SWA_Q_HEADS, SWA_KV_HEADS
    grp = hq // hkv
    kvw = hkv * hd
    nb = t // w
    host = _ExchangeHost(exchange)
    assert not (exchange and exchange.aliases)

    def body(*refs):
        q_ref, kvp_ref, kvc_ref, s_ref = refs[:4]
        o_ref, lse_ref = refs[4 + host.n_in:6 + host.n_in]
        host.start(refs, 4, 6 + host.n_in, (nb,))
        n = pl.program_id(0)
        mask = _swa_mask(n, grp * w, 0)
        kvcat = jnp.concatenate([kvp_ref[...], kvc_ref[...]], axis=0)
        kvs = range(hkv)
        heads = [range(hk * grp, (hk + 1) * grp) for hk in kvs]
        sks = [_stack_scalars(s_ref, hs, w) for hs in heads]
        ss = [jnp.where(mask, _dot(_stack_heads(q_ref, heads[hk], hd) * SWA_SCALE, kvcat[:, hk * hd:(hk + 1) * hd], NT),
                        -jnp.inf) for hk in kvs]
        ms = [jnp.maximum(jnp.max(s, axis=-1, keepdims=True), sk) for s, sk in zip(ss, sks)]
        ps = [jnp.exp(s - m) for s, m in zip(ss, ms)]
        dens = [jnp.sum(p, axis=-1, keepdims=True) + jnp.exp(sk - m) for p, sk, m in zip(ps, sks, ms)]
        os_ = [_dot((p * (1.0 / den)).astype(BF16), kvcat[:, kvw + hk * hd:kvw + (hk + 1) * hd])
               for hk, p, den in zip(kvs, ps, dens)]
        outs, lses = [], []
        for o, m, den in zip(os_, ms, dens):
            lse = m + jnp.log(den)
            outs += [o[j * w:(j + 1) * w] for j in range(grp)]
            lses += [lse[j * w:(j + 1) * w] for j in range(grp)]
        o_ref[...] = jnp.concatenate(outs, axis=1).astype(o_ref.dtype)
        lse_ref[...] = jnp.concatenate(lses, axis=1)
        host.wait(refs, 4, 6 + host.n_in, (nb,))

    outs = pl.pallas_call(
        body, grid=(nb,),
        in_specs=[pl.BlockSpec((w, hq * hd), lambda i: (i, 0)),
                  pl.BlockSpec((w, 2 * kvw), lambda i: (jnp.maximum(i - 1, 0), 0)),
                  pl.BlockSpec((w, 2 * kvw), lambda i: (i, 0)),
                  pl.BlockSpec((1, hq), lambda i: (0, 0))] + host.in_specs,
        out_specs=[pl.BlockSpec((w, hq * hd), lambda i: (i, 0)), pl.BlockSpec((w, hq), lambda i: (i, 0))] + host.out_specs,
        out_shape=[jax.ShapeDtypeStruct((t, hq * hd), BF16), jax.ShapeDtypeStruct((t, hq), F32)] + host.out_shapes,
        scratch_shapes=host.scratch,
        compiler_params=_cp(host.semantics(("parallel",))), name=name,
    )(q, kv, kv, sinks, *host.ins)
    return (outs[0], outs[1], list(outs[2:])) if exchange is not None else outs


ANY_SPEC = pl.BlockSpec(memory_space=pl.ANY)


def _swa_bwd(q, kv, sinks, o, lse, do, dp, dp_col, *, name):
    t = q.shape[0]
    w, hd, hq, hkv = SWA_WINDOW, SWA_HEAD_DIM, SWA_Q_HEADS, SWA_KV_HEADS
    grp = hq // hkv
    kvw = hkv * hd
    nb = t // w
    assert dp_col % (hq * hd) == 0
    dq_blk = dp_col // (hq * hd)

    def body(q_ref, kvp_ref, kvc_ref, s_ref, o_ref, lse_ref, do_ref, _, dq_ref, dkv_ref, ds_ref, carry_ref, s_scr, dp_scr,
             p_scr, ds_scr):
        n = pl.program_id(0)

        @pl.when(n == 0)
        def _():
            ds_ref[...] = jnp.zeros_like(ds_ref)
            carry_ref[...] = jnp.zeros_like(carry_ref)

        @pl.when(n < nb)
        def _():
            kvcat = jnp.concatenate([kvp_ref[...], kvc_ref[...]], axis=0)
            dqs, dsk, dks, dvs = [], [], [], []
            for hk in range(hkv):
                heads = range(hk * grp, (hk + 1) * grp)
                qs = _stack_heads(q_ref, heads, hd)
                dos = _stack_heads(do_ref, heads, hd)
                os_ = _stack_heads(o_ref, heads, hd)
                lse = _stack_heads(lse_ref, heads, 1)
                kh = kvcat[:, hk * hd:(hk + 1) * hd]
                vh = kvcat[:, kvw + hk * hd:kvw + (hk + 1) * hd]
                delta = jnp.sum(dos.astype(F32) * os_.astype(F32), axis=-1, keepdims=True)
                s_scr[...] = _dot(qs * SWA_SCALE, kh, NT)
                dp_scr[...] = _dot(dos, vh, NT)
                for r0 in range(0, grp * w, SWA_SUB):
                    rows = slice(r0, r0 + SWA_SUB)
                    p = jnp.exp(jnp.where(_swa_mask(n, SWA_SUB, r0 % w), s_scr[rows, :], -jnp.inf) - lse[rows])
                    p_scr[rows, :] = p.astype(p_scr.dtype)
                    ds_scr[rows, :] = (p * (dp_scr[rows, :] - delta[rows]) * SWA_SCALE).astype(ds_scr.dtype)
                ds = ds_scr[...]
                dq = _dot(ds, kh)
                dqs += [dq[j * w:(j + 1) * w] for j in range(grp)]
                dks.append(_dot(ds, qs, TN))
                dvs.append(_dot(p_scr[...], dos, TN))
                dsink = -jnp.exp(_stack_scalars(s_ref, heads, w) - lse) * delta
                dsk += [jnp.sum(dsink[j * w:(j + 1) * w], axis=0, keepdims=True) for j in range(grp)]
            dq_ref[...] = jnp.concatenate(dqs, axis=1).astype(dq_ref.dtype)
            ds_ref[...] += jnp.concatenate(dsk, axis=1)
            dkv_cat = jnp.concatenate(dks + dvs, axis=1)
            dkv_ref[...] = (carry_ref[...] + dkv_cat[:w]).astype(dkv_ref.dtype)
            carry_ref[...] = dkv_cat[w:]

        @pl.when(n == nb)
        def _():
            dkv_ref[...] = carry_ref[...].astype(dkv_ref.dtype)

    cur = lambda i: (jnp.minimum(i, nb - 1), 0)
    prev = lambda i: (jnp.clip(i - 1, 0, nb - 1), 0)
    return pl.pallas_call(
        body, grid=(nb + 1,),
        in_specs=[pl.BlockSpec((w, hq * hd), cur), pl.BlockSpec((w, 2 * kvw), prev), pl.BlockSpec((w, 2 * kvw), cur),
                  pl.BlockSpec((1, hq), lambda i: (0, 0)), pl.BlockSpec((w, hq * hd), cur),
                  pl.BlockSpec((w, hq), cur), pl.BlockSpec((w, hq * hd), cur), ANY_SPEC],
        out_specs=[pl.BlockSpec((w, hq * hd), lambda i: (jnp.minimum(i, nb - 1), dq_blk)),
                   pl.BlockSpec((w, 2 * kvw), prev), pl.BlockSpec((1, hq), lambda i: (0, 0))],
        out_shape=[jax.ShapeDtypeStruct(dp.shape, dp.dtype), jax.ShapeDtypeStruct((t, 2 * kvw), BF16),
                   jax.ShapeDtypeStruct((1, hq), F32)],
        scratch_shapes=[pltpu.VMEM((w, 2 * kvw), F32)] + [pltpu.VMEM((grp * w, 2 * w), dt) for dt in (F32, F32, BF16, BF16)],
        input_output_aliases={7: 0},
        compiler_params=_cp(("arbitrary",)), name=name,
    )(q, kv, kv, sinks, o, lse, do, dp)


def _xa_fwd(q, mkv, *, name, tq=512):
    t, xw = q.shape
    nm = mkv.shape[0]
    hd, nh = XA_HEAD_DIM, XA_HEADS
    tq = _blk(t, tq)

    def body(q_ref, mkv_ref, o_ref):
        cols = [slice(h * hd, (h + 1) * hd) for h in range(nh)]
        ss = [_dot(q_ref[:, c], mkv_ref[:, c], NT) * (hd ** -0.5) for c in cols]
        ps = [jnp.exp(s - jnp.max(s, axis=-1, keepdims=True)) for s in ss]
        ps = [p * (1.0 / jnp.sum(p, axis=-1, keepdims=True)) for p in ps]
        outs = [_dot(p.astype(BF16), mkv_ref[:, xw + c.start:xw + c.stop]) for p, c in zip(ps, cols)]
        o_ref[...] = jnp.concatenate(outs, axis=1).astype(o_ref.dtype)

    return pl.pallas_call(
        body, grid=(t // tq,),
        in_specs=[pl.BlockSpec((tq, xw), lambda i: (i, 0)), pl.BlockSpec((nm, 2 * xw), lambda i: (0, 0))],
        out_specs=pl.BlockSpec((tq, xw), lambda i: (i, 0)),
        out_shape=jax.ShapeDtypeStruct((t, xw), BF16),
        compiler_params=_cp(("parallel",)), name=name,
    )(q, mkv)


def _xa_bwd(q, mkv, do, dp, dp_col, *, name, tq=512):
    t, xw = q.shape
    nm = mkv.shape[0]
    hd, nh = XA_HEAD_DIM, XA_HEADS
    tq = _blk(t, tq)
    assert dp_col % xw == 0

    def body(q_ref, mkv_ref, do_ref, _, dq_ref, dmkv_ref):
        i = pl.program_id(0)
        cols = [slice(h * hd, (h + 1) * hd) for h in range(nh)]
        vcols = [slice(xw + c.start, xw + c.stop) for c in cols]
        ss = [_dot(q_ref[:, c], mkv_ref[:, c], NT) * (hd ** -0.5) for c in cols]
        dps = [_dot(do_ref[:, c], mkv_ref[:, v], NT) for c, v in zip(cols, vcols)]
        ps = [jnp.exp(s - jnp.max(s, axis=-1, keepdims=True)) for s in ss]
        ps = [p * (1.0 / jnp.sum(p, axis=-1, keepdims=True)) for p in ps]
        dss = [(p * (dp - jnp.sum(p * dp, axis=-1, keepdims=True)) * (hd ** -0.5)).astype(BF16) for p, dp in zip(ps, dps)]
        dqs = [_dot(ds, mkv_ref[:, c]) for ds, c in zip(dss, cols)]
        dks = [_dot(ds, q_ref[:, c], TN) for ds, c in zip(dss, cols)]
        dvs = [_dot(p.astype(BF16), do_ref[:, c], TN) for p, c in zip(ps, cols)]
        dq_ref[...] = jnp.concatenate(dqs, axis=1).astype(dq_ref.dtype)
        part = jnp.concatenate(dks + dvs, axis=1)

        @pl.when(i == 0)
        def _():
            dmkv_ref[...] = part

        @pl.when(i > 0)
        def _():
            dmkv_ref[...] += part

    row = pl.BlockSpec((tq, xw), lambda i: (i, 0))
    full = pl.BlockSpec((nm, 2 * xw), lambda i: (0, 0))
    return pl.pallas_call(
        body, grid=(t // tq,),
        in_specs=[row, full, row, ANY_SPEC],
        out_specs=[pl.BlockSpec((tq, xw), lambda i: (i, dp_col // xw)), full],
        out_shape=[jax.ShapeDtypeStruct(dp.shape, dp.dtype), jax.ShapeDtypeStruct((nm, 2 * xw), F32)],
        input_output_aliases={3: 0}, compiler_params=_cp(("arbitrary",)), name=name,
    )(q, mkv, do, dp)


def _merge_specs(ys, ws, tm):
    y_specs = [pl.BlockSpec((tm, y.shape[1]), lambda i: (i, 0)) for y in ys]
    w_specs = [pl.BlockSpec(w.shape, lambda i: (0, 0, 0)) for w in ws]
    return y_specs, w_specs


def _merge_tiles(ws, tn):
    ns, _, per = ws[0].shape
    tn = _blk(per, tn)
    return tn, [(s, c, s * per + c) for s in range(ns) for c in range(0, per, tn)]


def _merge_fwd(ys, ws, gates, *, name, tm=256, tn=512):
    t, d = ys[0].shape[0], ws[0].shape[0] * ws[0].shape[2]
    tm = _blk(t, tm)
    tn, tiles = _merge_tiles(ws, tn)
    y_specs, w_specs = _merge_specs(ys, ws, tm)

    def body(ya, yb, yc, wa, wb, wc, g_ref, o_ref):
        for s, c, col in tiles:
            acc = None
            for b, (y, w) in enumerate(((ya, wa), (yb, wb), (yc, wc))):
                term = _sigmoid(g_ref[:, b * d + col:b * d + col + tn]) * _dot(y[...], w[s, :, c:c + tn])
                acc = term if acc is None else acc + term
            o_ref[:, col:col + tn] = acc.astype(o_ref.dtype)

    return pl.pallas_call(
        body, grid=(t // tm,),
        in_specs=y_specs + w_specs + [pl.BlockSpec((tm, 3 * d), lambda i: (i, 0))],
        out_specs=pl.BlockSpec((tm, d), lambda i: (i, 0)),
        out_shape=jax.ShapeDtypeStruct((t, d), BF16),
        compiler_params=_cp(("parallel",)), name=name,
    )(*ys, *ws, gates)


def _merge_bwd(ys, ws, gates, dmerged, dp_width, *, name, tm=256, tn=512):
    t, d = ys[0].shape[0], ws[0].shape[0] * ws[0].shape[2]
    tm = _blk(t, tm)
    tn, tiles = _merge_tiles(ws, tn)
    y_specs, w_specs = _merge_specs(ys, ws, tm)
    row = pl.BlockSpec((tm, d), lambda i: (i, 0))
    wide = pl.BlockSpec((tm, 3 * d), lambda i: (i, 0))

    def body(ya, yb, yc, wa, wb, wc, g_ref, dm_ref, dua, dub, duc, dp_ref):
        for s, c, col in tiles:
            dm = dm_ref[:, col:col + tn]
            for b, (y, w, du) in enumerate(((ya, wa, dua), (yb, wb, dub), (yc, wc, duc))):
                sg = _sigmoid(g_ref[:, b * d + col:b * d + col + tn])
                u = _dot(y[...], w[s, :, c:c + tn])
                du[:, col:col + tn] = (dm * sg).astype(du.dtype)
                dp_ref[:, b * d + col:b * d + col + tn] = (dm * u * sg * (1.0 - sg)).astype(dp_ref.dtype)

    return pl.pallas_call(
        body, grid=(t // tm,),
        in_specs=y_specs + w_specs + [wide, row],
        out_specs=[row] * 3 + [wide],
        out_shape=[jax.ShapeDtypeStruct((t, d), BF16)] * 3 + [jax.ShapeDtypeStruct((t, dp_width), BF16)],
        compiler_params=_cp(("parallel",)), name=name,
    )(*ys, *ws, gates, dmerged)


def _adamw(w, g, m, v, *, name, tm=256):
    lead = w.ndim - 2
    assert all(s == 1 for s in w.shape[:lead]) and m.shape == w.shape and v.shape == w.shape
    r, c = w.shape[lead:]
    assert g.shape == (r, c)
    tm = _blk(r, tm) if r % 8 == 0 else r
    tc = c if tm * c * 4 <= ROW_BLOCK_BYTES else _blk(c, 256)
    ncb = c // tc
    bc1 = 1.0 - ADAM_B1 ** ADAM_STEP
    bc2 = 1.0 - ADAM_B2 ** ADAM_STEP

    def body(w_ref, g_ref, m_ref, v_ref, go_ref, d_ref, nm_ref, nv_ref):
        gv = g_ref[...]
        go_ref[...] = gv
        nm = ADAM_B1 * m_ref[...] + (1.0 - ADAM_B1) * gv
        nv = ADAM_B2 * v_ref[...] + (1.0 - ADAM_B2) * (gv * gv)
        d_ref[...] = -ADAM_LR * ((nm / bc1) / (jnp.sqrt(nv / bc2) + ADAM_EPS) + ADAM_WD * w_ref[...])
        nm_ref[...] = nm
        nv_ref[...] = nv

    spec = pl.BlockSpec((None,) * lead + (tm, tc), lambda i: (0,) * lead + (i // ncb, i % ncb))
    g_spec = pl.BlockSpec((tm, tc), lambda i: (i // ncb, i % ncb))
    return pl.pallas_call(
        body, grid=(r // tm * ncb,), in_specs=[spec, g_spec, spec, spec], out_specs=[spec] * 4,
        out_shape=[jax.ShapeDtypeStruct(w.shape, F32)] * 4,
        compiler_params=_cp(("parallel",)), name=name,
    )(w, g, m, v)


HALO = 8


def _shift_down(cur, prev, j):
    if j == 0:
        return cur
    y = pltpu.roll(cur, j, 0)
    row = lax.broadcasted_iota(jnp.int32, (HALO, cur.shape[1]), 0)
    top = jnp.where(row < j, pltpu.roll(prev, j, 0), y[:HALO])
    return jnp.concatenate([top, y[HALO:]], axis=0)


def _shift_up(cur, nxt, j):
    if j == 0:
        return cur
    tm = cur.shape[0]
    y = pltpu.roll(cur, tm - j, 0)
    row = lax.broadcasted_iota(jnp.int32, (HALO, cur.shape[1]), 0)
    bot = jnp.where(row >= HALO - j, pltpu.roll(nxt, HALO - j, 0), y[tm - HALO:])
    return jnp.concatenate([y[:tm - HALO], bot], axis=0)


def _softplus(x):
    return jnp.maximum(x, 0.0) + jnp.log(1.0 + jnp.exp(-jnp.abs(x)))


def _gdn_pre_fwd(qkvb, conv_w, ab, alog_pad, dt_pad, *, name, ab_blk=0, tm=256):
    t, cw = qkvb.shape
    hd, nh, ck = GDN_HEAD_DIM, GDN_HEADS, GDN_CHUNK
    gw = nh * hd
    tm = _blk(t, tm)
    hb = tm // HALO

    def body(x_ref, xp_ref, w_ref, ab_ref, al_ref, dt_ref, xc_ref, qkvn_ref, aux_ref):
        i = pl.program_id(0)
        cur = x_ref[...]
        prev = jnp.where(i > 0, xp_ref[...], 0.0)
        xc = None
        for tap in range(GDN_CONV):
            term = w_ref[tap:tap + 1, :] * _shift_down(cur, prev, GDN_CONV - 1 - tap)
            xc = term if xc is None else xc + term
        xc_ref[...] = xc
        s = xc * _sigmoid(xc)
        for h in range(2 * nh):
            xh = s[:, h * hd:(h + 1) * hd]
            r = lax.rsqrt(jnp.sum(xh * xh, axis=-1, keepdims=True) + L2_EPS)
            scale = hd ** -0.5 if h < nh else 1.0
            qkvn_ref[:, h * hd:(h + 1) * hd] = xh * (r * scale)
        qkvn_ref[:, 2 * gw:] = s[:, 2 * gw:]
        abv = ab_ref[...]
        lane = lax.broadcasted_iota(jnp.int32, abv.shape, 1)
        g = jnp.where(lane < nh, -jnp.exp(al_ref[...]) * _softplus(abv + dt_ref[...]), 0.0)
        beta = jnp.where((lane >= nh) & (lane < 2 * nh), _sigmoid(abv), 0.0)
        ii = lax.broadcasted_iota(jnp.int32, (tm, tm), 0)
        jj = lax.broadcasted_iota(jnp.int32, (tm, tm), 1)
        tri = jnp.where((ii >= jj) & ((ii ^ jj) < ck), 1.0, 0.0)
        gcum = _dot(tri, g, precision=HI)
        aux_ref[...] = g + beta + pltpu.roll(gcum, 2 * nh, 1)

    row = lambda c: pl.BlockSpec((tm, c), lambda i: (i, 0))
    vec = lambda r, c: pl.BlockSpec((r, c), lambda i: (0, 0))
    return pl.pallas_call(
        body, grid=(t // tm,),
        in_specs=[row(cw), pl.BlockSpec((HALO, cw), lambda i: (jnp.maximum(i * hb - 1, 0), 0)), vec(GDN_CONV, cw),
                  pl.BlockSpec((tm, LANES), lambda i: (i, ab_blk)), vec(1, LANES), vec(1, LANES)],
        out_specs=[row(cw), row(cw), row(LANES)],
        out_shape=[jax.ShapeDtypeStruct((t, cw), F32), jax.ShapeDtypeStruct((t, cw), F32),
                   jax.ShapeDtypeStruct((t, LANES), F32)],
        compiler_params=_cp(("parallel",)), name=name,
    )(qkvb, qkvb, conv_w, ab, alog_pad, dt_pad)


GDN_STEP_CHUNKS = 4
GDN_ILP_CHUNKS = 4
GDN_ILP_CHUNKS_BWD = 4


def _bdot(a, b, dims=NN):
    return _dot(a.astype(BF16), b.astype(BF16), dims)


def _split_bf16(x):
    hi = x.astype(BF16)
    return hi, (x - hi.astype(F32)).astype(BF16)


def _dot3(a, b, dims=NN):
    ah, al = _split_bf16(a)
    bh, bl = _split_bf16(b)
    return _dot(ah, bh, dims) + (_dot(ah, bl, dims) + _dot(al, bh, dims))


def _dot3_many(lhs, rhs, dims=NN):
    sa = [_split_bf16(a) for a in lhs]
    sb = [_split_bf16(b) for b in rhs]
    hh = [_dot(a[0], b[0], dims) for a, b in zip(sa, sb)]
    hl = [_dot(a[0], b[1], dims) for a, b in zip(sa, sb)]
    lh = [_dot(a[1], b[0], dims) for a, b in zip(sa, sb)]
    return [x + (y + z) for x, y, z in zip(hh, hl, lh)]


def _gdn_local(chains, with_inverse):
    ck = GDN_CHUNK
    ii = lax.broadcasted_iota(jnp.int32, (ck, ck), 0)
    jj = lax.broadcasted_iota(jnp.int32, (ck, ck), 1)
    lower, strict = ii >= jj, ii > jj
    dmat = [jnp.exp(jnp.where(lower, gc - gc_row, -jnp.inf)) for _, _, _, gc, gc_row in chains]
    kk = [_bdot(k, k, NT) for _, k, _, _, _ in chains]
    qk = [_bdot(q, k, NT) for q, k, _, _, _ in chains]
    tinv = [None] * len(chains)
    if with_inverse:
        lmat = [jnp.where(strict, c[2] * kk_i * d_i, 0.0) for c, kk_i, d_i in zip(chains, kk, dmat)]
        eye = jnp.where(ii == jj, 1.0, 0.0)
        tinv = [eye - l_i for l_i in lmat]
        pw = lmat
        for _ in range(int(math.log2(ck)) - 1):
            pw = _dot3_many(pw, pw)
            tinv = [t_i + d_i for t_i, d_i in zip(tinv, _dot3_many(tinv, pw))]
    out = []
    for (q, k, b, gc, gc_row), dmat_i, kk_i, qk_i, tinv_i in zip(chains, dmat, kk, qk, tinv):
        gl = gc[ck - 1:ck, :]
        out.append(dict(lower=lower, strict=strict, dmat=dmat_i, kk=kk_i, tinv=tinv_i, gam=jnp.exp(gc), qk=qk_i,
                        mm=qk_i * dmat_i, kdec=jnp.exp(gl - gc)))
    return out


def _gdn_head_cols(h):
    return slice(h * GDN_HEAD_DIM, (h + 1) * GDN_HEAD_DIM)


def _gdn_chunk_inputs(x_ref, aux_ref, auxt_ref, g, h):
    nh, ck = GDN_HEADS, GDN_CHUNK
    gw = nh * GDN_HEAD_DIM
    rows = slice(g * ck, (g + 1) * ck)
    cols = _gdn_head_cols(h)
    q = x_ref[rows, cols]
    k = x_ref[rows, gw + cols.start:gw + cols.stop]
    v = x_ref[rows, 2 * gw + cols.start:2 * gw + cols.stop]
    b = aux_ref[rows, nh + h:nh + h + 1]
    gc = aux_ref[rows, 2 * nh + h:2 * nh + h + 1]
    gc_row = auxt_ref[g, 2 * nh + h:2 * nh + h + 1, :]
    return q, k, v, b, gc, gc_row


def _gdn_specs(t, widths, *, reverse=False, step_chunks=None):
    rows = (step_chunks or GDN_STEP_CHUNKS) * GDN_CHUNK
    nsteps = t // rows
    idx = (lambda i: (nsteps - 1 - i, 0)) if reverse else (lambda i: (i, 0))
    return [pl.BlockSpec((rows, w), idx) for w in widths]


def _gdn_local_fwd(qkvn, aux, aux_t, *, name, exchange=None):
    t = qkvn.shape[0]
    hd, nh, ck, gs = GDN_HEAD_DIM, GDN_HEADS, GDN_CHUNK, GDN_STEP_CHUNKS
    gw = nh * hd
    host = _ExchangeHost(exchange)
    assert not (exchange and exchange.aliases)
    grid = (t // (gs * ck),)

    def body(*refs):
        x_ref, aux_ref, auxt_ref = refs[:3]
        u_ref, w_ref, qd_ref, kd_ref, mm_ref, tinv_ref = refs[3 + host.n_in:9 + host.n_in]
        host.start(refs, 3, 9 + host.n_in, grid)
        for g0 in range(0, gs, GDN_ILP_CHUNKS):
            where = [(g, h) for g in range(g0, g0 + GDN_ILP_CHUNKS) for h in range(nh)]
            ins = [_gdn_chunk_inputs(x_ref, aux_ref, auxt_ref, g, h) for g, h in where]
            lcs = _gdn_local([(q, k, b, gc, gc_row) for q, k, _, b, gc, gc_row in ins], True)
            tinvs = [lc["tinv"] for lc in lcs]
            us = _dot3_many(tinvs, [b * v for _, _, v, b, _, _ in ins])
            ws = _dot3_many(tinvs, [(b * lc["gam"]) * k for (_, k, _, b, _, _), lc in zip(ins, lcs)])
            for i, ((g, h), (q, k, _, _, _, _), lc) in enumerate(zip(where, ins, lcs)):
                rows, cols = slice(g * ck, (g + 1) * ck), _gdn_head_cols(h)
                u_ref[rows, cols] = us[i]
                w_ref[rows, cols] = ws[i].astype(w_ref.dtype)
                qd_ref[rows, cols] = (lc["gam"] * q).astype(qd_ref.dtype)
                kd_ref[rows, cols] = (lc["kdec"] * k).astype(kd_ref.dtype)
            for g in range(g0, g0 + GDN_ILP_CHUNKS):
                rows = slice(g * ck, (g + 1) * ck)
                mine = [lc for (gg, _), lc in zip(where, lcs) if gg == g]
                mm_ref[rows, :] = jnp.concatenate([lc["mm"] for lc in mine], axis=1).astype(mm_ref.dtype)
                tinv_ref[rows, :] = jnp.concatenate([lc["tinv"] for lc in mine], axis=1)
        host.wait(refs, 3, 9 + host.n_in, grid)

    sq = nh * ck
    outs = pl.pallas_call(
        body, grid=grid,
        in_specs=_gdn_specs(t, (3 * gw, LANES)) + [pl.BlockSpec((gs, 16, ck), lambda i: (i, 0, 0))] + host.in_specs,
        out_specs=_gdn_specs(t, (gw, gw, gw, gw, sq, sq)) + host.out_specs,
        out_shape=[jax.ShapeDtypeStruct((t, gw), F32)] + [jax.ShapeDtypeStruct((t, gw), BF16)] * 3
        + [jax.ShapeDtypeStruct((t, sq), BF16), jax.ShapeDtypeStruct((t, sq), F32)] + host.out_shapes,
        scratch_shapes=host.scratch,
        compiler_params=_cp(host.semantics(("parallel",))), name=name,
    )(qkvn, aux, aux_t, *host.ins)
    return (*outs[:6], list(outs[6:])) if exchange is not None else outs


def _gdn_seq_fwd(u, w, qd, kd, mm, aux, *, name):
    t = u.shape[0]
    hd, nh, ck, gs = GDN_HEAD_DIM, GDN_HEADS, GDN_CHUNK, GDN_STEP_CHUNKS
    gw = nh * hd
    sq = nh * ck

    def body(u_ref, w_ref, qd_ref, kd_ref, mm_ref, aux_ref, o_ref, vn_ref, sall_ref, s_ref):
        @pl.when(pl.program_id(0) == 0)
        def _():
            s_ref[...] = jnp.zeros_like(s_ref)

        heads = range(nh)
        hcols = [_gdn_head_cols(h) for h in heads]
        sts = [s_ref[h] for h in heads]
        for g in range(gs):
            rows = slice(g * ck, (g + 1) * ck)
            last = (g + 1) * ck - 1
            for h in heads:
                sall_ref[g, h] = sts[h]
            stbs = [st.astype(BF16) for st in sts]
            w_s = [_dot(w_ref[rows, c], stb) for c, stb in zip(hcols, stbs)]
            q_s = [_dot(qd_ref[rows, c], stb) for c, stb in zip(hcols, stbs)]
            vnbs = [(u_ref[rows, c] - ws).astype(BF16) for c, ws in zip(hcols, w_s)]
            m_v = [_dot(mm_ref[rows, h * ck:(h + 1) * ck], vnbs[h]) for h in heads]
            k_v = [_dot(kd_ref[rows, c], vnb, TN) for c, vnb in zip(hcols, vnbs)]
            for h, c in zip(heads, hcols):
                vn_ref[rows, c] = vnbs[h]
                o_ref[rows, c] = q_s[h] + m_v[h]
            gam_c = [jnp.exp(aux_ref[last:last + 1, 2 * nh + h:2 * nh + h + 1]) for h in heads]
            sts = [gam_c[h] * sts[h] + k_v[h] for h in heads]
        for h in heads:
            s_ref[h] = sts[h]

    return pl.pallas_call(
        body, grid=(t // (gs * ck),),
        in_specs=_gdn_specs(t, (gw, gw, gw, gw, sq, LANES)),
        out_specs=_gdn_specs(t, (gw, gw)) + [pl.BlockSpec((gs, nh, hd, hd), lambda i: (i, 0, 0, 0))],
        out_shape=[jax.ShapeDtypeStruct((t, gw), F32), jax.ShapeDtypeStruct((t, gw), BF16),
                   jax.ShapeDtypeStruct((t // ck, nh, hd, hd), F32)],
        scratch_shapes=[pltpu.VMEM((nh, hd, hd), F32)],
        compiler_params=_cp(("arbitrary",)), name=name,
    )(u, w, qd, kd, mm, aux)


def _gdn_seq_bwd(do, w, qd, kd, mm, vn, s_all, aux, *, name):
    t = do.shape[0]
    hd, nh, ck, gs = GDN_HEAD_DIM, GDN_HEADS, GDN_CHUNK, GDN_STEP_CHUNKS
    gw = nh * hd
    sq = nh * ck
    nsteps = t // (gs * ck)

    def body(do_ref, w_ref, qd_ref, kd_ref, mm_ref, vn_ref, sall_ref, aux_ref, dvn_ref, dqd_ref, dkd_ref, dw_ref,
             dlast_ref, ds_ref):
        @pl.when(pl.program_id(0) == 0)
        def _():
            ds_ref[...] = jnp.zeros_like(ds_ref)

        lane = lax.broadcasted_iota(jnp.int32, (ck, LANES), 1)
        rowi = lax.broadcasted_iota(jnp.int32, (ck, LANES), 0)
        heads = range(nh)
        hcols = [_gdn_head_cols(h) for h in heads]
        dsns = [ds_ref[h] for h in heads]
        for g in reversed(range(gs)):
            rows = slice(g * ck, (g + 1) * ck)
            last = (g + 1) * ck - 1
            sts = [sall_ref[g, h] for h in heads]
            stbs = [st.astype(BF16) for st in sts]
            dsbs = [dsn.astype(BF16) for dsn in dsns]
            dobs = [do_ref[rows, c].astype(BF16) for c in hcols]
            dvns = [_dot(mm_ref[rows, h * ck:(h + 1) * ck], dobs[h], TN) + _dot(kd_ref[rows, hcols[h]], dsbs[h])
                    for h in heads]
            dqds = [_dot(dob, stb, NT) for dob, stb in zip(dobs, stbs)]
            dkds = [_dot(vn_ref[rows, c], dsb, NT) for c, dsb in zip(hcols, dsbs)]
            q_o = [_dot(qd_ref[rows, c], dob, TN) for c, dob in zip(hcols, dobs)]
            dvbs = [dvn.astype(BF16) for dvn in dvns]
            dws = [_dot(dvb, stb, NT) for dvb, stb in zip(dvbs, stbs)]
            w_v = [_dot(w_ref[rows, c], dvb, TN) for c, dvb in zip(hcols, dvbs)]
            gam_c = [jnp.exp(aux_ref[last:last + 1, 2 * nh + h:2 * nh + h + 1]) for h in heads]
            dlast = jnp.zeros((ck, LANES), F32)
            for h, c in zip(heads, hcols):
                dvn_ref[rows, c] = dvns[h]
                dqd_ref[rows, c] = dqds[h]
                dkd_ref[rows, c] = dkds[h]
                dw_ref[rows, c] = -dws[h]
                dgam_c = jnp.sum(jnp.sum(dsns[h] * sts[h], axis=1, keepdims=True), axis=0, keepdims=True)
                dlast = dlast + jnp.where((rowi == ck - 1) & (lane == h), gam_c[h] * dgam_c, 0.0)
            dlast_ref[rows, :] = dlast
            dsns = [q_o[h] + gam_c[h] * dsns[h] - w_v[h] for h in heads]
        for h in heads:
            ds_ref[h] = dsns[h]

    return pl.pallas_call(
        body, grid=(nsteps,),
        in_specs=_gdn_specs(t, (gw, gw, gw, gw, sq, gw), reverse=True)
        + [pl.BlockSpec((gs, nh, hd, hd), lambda i: (nsteps - 1 - i, 0, 0, 0))] + _gdn_specs(t, (LANES,), reverse=True),
        out_specs=_gdn_specs(t, (gw, gw, gw, gw, LANES), reverse=True),
        out_shape=[jax.ShapeDtypeStruct((t, gw), F32)] * 4 + [jax.ShapeDtypeStruct((t, LANES), F32)],
        scratch_shapes=[pltpu.VMEM((nh, hd, hd), F32)],
        compiler_params=_cp(("arbitrary",)), name=name,
    )(do, w, qd, kd, mm, vn, s_all, aux)


def _gdn_local_bwd(qkvn, aux, aux_t, tinv, u, w, vn, do, dvn, dqd, dkd, dw, dlast, *, name):
    t = qkvn.shape[0]
    hd, nh, ck, gs = GDN_HEAD_DIM, GDN_HEADS, GDN_CHUNK, GDN_STEP_CHUNKS
    gw = nh * hd
    sq = nh * ck

    def body(x_ref, aux_ref, auxt_ref, tinv_ref, u_ref, w_ref, vn_ref, do_ref, dvn_ref, dqd_ref, dkd_ref, dw_ref,
             dlast_ref, dx_ref, daux_ref):
        lane = lax.broadcasted_iota(jnp.int32, (ck, LANES), 1)
        ones = jnp.ones((ck, LANES), F32)
        ii = lax.broadcasted_iota(jnp.int32, (ck, ck), 0)
        jj = lax.broadcasted_iota(jnp.int32, (ck, ck), 1)
        suffix = jnp.where(jj >= ii, 1.0, 0.0)
        for g0 in range(0, gs, GDN_ILP_CHUNKS_BWD):
            where = [(g, h) for g in range(g0, g0 + GDN_ILP_CHUNKS_BWD) for h in range(nh)]
            at = [(slice(g * ck, (g + 1) * ck), _gdn_head_cols(h)) for g, h in where]
            ins = [_gdn_chunk_inputs(x_ref, aux_ref, auxt_ref, g, h) for g, h in where]
            lcs = _gdn_local([(q, k, b, gc, gc_row) for q, k, _, b, gc, gc_row in ins], False)
            tinvs = [tinv_ref[slice(g * ck, (g + 1) * ck), h * ck:(h + 1) * ck] for g, h in where]
            dms = [jnp.where(lc["lower"], _bdot(do_ref[r, c], vn_ref[r, c], NT), 0.0) for lc, (r, c) in zip(lcs, at)]
            drvs = _dot3_many(tinvs, [dvn_ref[r, c] for r, c in at], TN)
            drks = _dot3_many(tinvs, [dw_ref[r, c] for r, c in at], TN)
            das = [jnp.where(lc["strict"], -(_bdot(drv, u_ref[r, c], NT) + _bdot(drk, w_ref[r, c], NT)), 0.0)
                   for lc, (r, c), drv, drk in zip(lcs, at, drvs, drks)]
            f_mats = [da * (i[3] * lc["kk"]) * lc["dmat"] + dm * lc["qk"] * lc["dmat"]
                      for i, lc, da, dm in zip(ins, lcs, das, dms)]
            col_sums = _dot3_many(f_mats, [ones] * len(where), TN)
            dgc_all = {g: dlast_ref[slice(g * ck, (g + 1) * ck), :] for g in range(g0, g0 + GDN_ILP_CHUNKS_BWD)}
            db_all = {g: jnp.zeros((ck, LANES), F32) for g in range(g0, g0 + GDN_ILP_CHUNKS_BWD)}
            e_mats = [da * lc["dmat"] * i[3] for i, lc, da in zip(ins, lcs, das)]
            dmds = [dm * lc["dmat"] for lc, dm in zip(lcs, dms)]
            dq_mm = [_bdot(dmd, i[1]) for i, dmd in zip(ins, dmds)]
            dk_mm = [_bdot(e, i[1]) + _bdot(e, i[1], TN) + _bdot(dmd, i[0], TN) for i, e, dmd in zip(ins, e_mats, dmds)]
            for n, ((g, h), (q, k, v, b, _, _), lc, (rows, cols)) in enumerate(zip(where, ins, lcs, at)):
                dmat, kk, gam, kdec = (lc[key] for key in ("dmat", "kk", "gam", "kdec"))
                drv, drk, da = drvs[n], drks[n], das[n]
                dqd_h, dkd_h = dqd_ref[rows, cols], dkd_ref[rows, cols]
                rs_rk = jnp.sum(drk * k, axis=-1, keepdims=True)
                db = (jnp.sum(drv * v, axis=-1, keepdims=True) + gam * rs_rk
                      + jnp.sum(da * kk * dmat, axis=-1, keepdims=True))
                dx_ref[rows, cols] = dq_mm[n] + gam * dqd_h
                dx_ref[rows, gw + cols.start:gw + cols.stop] = (b * gam) * drk + dk_mm[n] + kdec * dkd_h
                dx_ref[rows, 2 * gw + cols.start:2 * gw + cols.stop] = b * drv
                e_vec = jnp.sum(dkd_h * (kdec * k), axis=-1, keepdims=True)
                dgc = (b * gam * rs_rk + gam * jnp.sum(dqd_h * q, axis=-1, keepdims=True)
                       + jnp.sum(f_mats[n], axis=-1, keepdims=True) - col_sums[n][:, 0:1] - e_vec)
                is_last = lax.broadcasted_iota(jnp.int32, (ck, 1), 0) == ck - 1
                dgc = dgc + jnp.where(is_last, jnp.sum(e_vec, axis=0, keepdims=True), 0.0)
                dgc_all[g] = dgc_all[g] + jnp.where(lane == h, dgc, 0.0)
                db_all[g] = db_all[g] + jnp.where(lane == nh + h, db, 0.0)
            for g in dgc_all:
                daux_ref[slice(g * ck, (g + 1) * ck), :] = _dot3(suffix, dgc_all[g]) + db_all[g]

    return pl.pallas_call(
        body, grid=(t // (gs * ck),),
        in_specs=_gdn_specs(t, (3 * gw, LANES)) + [pl.BlockSpec((gs, 16, ck), lambda i: (i, 0, 0))]
        + _gdn_specs(t, (sq, gw, gw, gw, gw, gw, gw, gw, gw, LANES)),
        out_specs=_gdn_specs(t, (3 * gw, LANES)),
        out_shape=[jax.ShapeDtypeStruct((t, 3 * gw), F32), jax.ShapeDtypeStruct((t, LANES), F32)],
        compiler_params=_cp(("parallel",)), name=name,
    )(qkvn, aux, aux_t, tinv, u, w, vn, do, dvn, dqd, dkd, dw, dlast)


def _gdn_pre_bwd1(xc, dqkvn, daux, ab, alog_pad, dt_pad, dkv, dp, dp_col, *, name, ab_blk=0, tm=256):
    t, cw = xc.shape
    hd, nh = GDN_HEAD_DIM, GDN_HEADS
    gw = nh * hd
    tm = _blk(t, tm)

    kvw = dkv.shape[1]
    seg = kvw + AB_PAD
    assert dp_col % seg == 0

    def body(xc_ref, dy_ref, daux_ref, ab_ref, al_ref, dt_ref, dkv_ref, _, dxc_ref, dab_ref, dal_ref, ddt_ref):
        i = pl.program_id(0)
        xc = xc_ref[...]
        sg = _sigmoid(xc)
        s = xc * sg
        dsilu = sg * (1.0 + xc * (1.0 - sg))
        for h in range(2 * nh):
            xh = s[:, h * hd:(h + 1) * hd]
            scale = hd ** -0.5 if h < nh else 1.0
            dyh = dy_ref[:, h * hd:(h + 1) * hd] * scale
            r = lax.rsqrt(jnp.sum(xh * xh, axis=-1, keepdims=True) + L2_EPS)
            dxh = r * dyh - xh * (r * r * r) * jnp.sum(dyh * xh, axis=-1, keepdims=True)
            dxc_ref[:, h * hd:(h + 1) * hd] = dxh * dsilu[:, h * hd:(h + 1) * hd]
        dxc_ref[:, 2 * gw:] = dy_ref[:, 2 * gw:] * dsilu[:, 2 * gw:]
        abv = ab_ref[...]
        dauxv = daux_ref[...]
        lane = lax.broadcasted_iota(jnp.int32, abv.shape, 1)
        is_a = lane < nh
        is_b = (lane >= nh) & (lane < 2 * nh)
        pre = abv + dt_ref[...]
        neg_ea = -jnp.exp(al_ref[...])
        d_a = jnp.where(is_a, dauxv * neg_ea * _sigmoid(pre), 0.0)
        beta = _sigmoid(abv)
        d_b = jnp.where(is_b, dauxv * beta * (1.0 - beta), 0.0)
        dab_ref[:, :kvw] = dkv_ref[...]
        dab_ref[:, kvw:kvw + LANES] = (d_a + d_b).astype(dab_ref.dtype)
        dab_ref[:, kvw + LANES:] = jnp.zeros((tm, AB_PAD - LANES), dab_ref.dtype)
        dal = jnp.sum(jnp.where(is_a, dauxv * neg_ea * _softplus(pre), 0.0), axis=0, keepdims=True)
        ddt = jnp.sum(d_a, axis=0, keepdims=True)

        @pl.when(i == 0)
        def _():
            dal_ref[...] = dal
            ddt_ref[...] = ddt

        @pl.when(i > 0)
        def _():
            dal_ref[...] += dal
            ddt_ref[...] += ddt

    row = lambda c: pl.BlockSpec((tm, c), lambda i: (i, 0))
    vec = pl.BlockSpec((1, LANES), lambda i: (0, 0))
    return pl.pallas_call(
        body, grid=(t // tm,),
        in_specs=[row(cw), row(cw), row(LANES), pl.BlockSpec((tm, LANES), lambda i: (i, ab_blk)), vec, vec, row(kvw),
                  ANY_SPEC],
        out_specs=[row(cw), pl.BlockSpec((tm, seg), lambda i: (i, dp_col // seg)), vec, vec],
        out_shape=[jax.ShapeDtypeStruct((t, cw), F32), jax.ShapeDtypeStruct(dp.shape, dp.dtype),
                   jax.ShapeDtypeStruct((1, LANES), F32), jax.ShapeDtypeStruct((1, LANES), F32)],
        input_output_aliases={7: 1}, compiler_params=_cp(("arbitrary",)), name=name,
    )(xc, dqkvn, daux, ab, alog_pad, dt_pad, dkv, dp)


def _gdn_pre_bwd2(dxc, qkvb, conv_w, dp, dp_col, *, name, tm=512):
    t, cw = dxc.shape
    tm = _blk(t, tm)
    hb = tm // HALO
    nblk = t // tm
    cg = GDN_HEADS * GDN_HEAD_DIM
    assert cw % cg == 0 and dp_col % cg == 0
    col0 = dp_col // cg

    def body(d_ref, dn_ref, x_ref, xp_ref, w_ref, _, dx_ref, dw_ref):
        i = pl.program_id(1)
        dcur = d_ref[...]
        dnxt = jnp.where(i < nblk - 1, dn_ref[...], 0.0)
        cur = x_ref[...]
        prev = jnp.where(i > 0, xp_ref[...], 0.0)
        dx = None
        dws = []
        for tap in range(GDN_CONV):
            j = GDN_CONV - 1 - tap
            term = w_ref[tap:tap + 1, :] * _shift_up(dcur, dnxt, j)
            dx = term if dx is None else dx + term
            dws.append(jnp.sum(dcur * _shift_down(cur, prev, j), axis=0, keepdims=True))
        dx_ref[...] = dx.astype(dx_ref.dtype)
        dw = jnp.concatenate(dws, axis=0)

        @pl.when(i == 0)
        def _():
            dw_ref[...] = dw

        @pl.when(i > 0)
        def _():
            dw_ref[...] += dw

    row = pl.BlockSpec((tm, cg), lambda c, i: (i, c))
    wsp = pl.BlockSpec((GDN_CONV, cg), lambda c, i: (0, c))
    return pl.pallas_call(
        body, grid=(cw // cg, nblk),
        in_specs=[row, pl.BlockSpec((HALO, cg), lambda c, i: (jnp.minimum((i + 1) * hb, t // HALO - 1), c)),
                  row, pl.BlockSpec((HALO, cg), lambda c, i: (jnp.maximum(i * hb - 1, 0), c)), wsp, ANY_SPEC],
        out_specs=[pl.BlockSpec((tm, cg), lambda c, i: (i, col0 + c)), wsp],
        out_shape=[jax.ShapeDtypeStruct(dp.shape, dp.dtype), jax.ShapeDtypeStruct((GDN_CONV, cw), F32)],
        input_output_aliases={5: 0}, compiler_params=_cp(("arbitrary", "arbitrary")), name=name,
    )(dxc, dxc, qkvb, qkvb, conv_w, dp)


def _gdn_post_fwd(o, z, norm_w, *, name, tm=512):
    t, gw = o.shape
    hd, nh = GDN_HEAD_DIM, GDN_HEADS
    tm = _blk(t, tm)

    def body(o_ref, z_ref, w_ref, y_ref):
        zv = z_ref[...]
        sz = zv * _sigmoid(zv)
        for h in range(nh):
            oh = o_ref[:, h * hd:(h + 1) * hd]
            r = lax.rsqrt(jnp.mean(oh * oh, axis=-1, keepdims=True) + RMS_EPS)
            y_ref[:, h * hd:(h + 1) * hd] = (oh * r * w_ref[...] * sz[:, h * hd:(h + 1) * hd]).astype(y_ref.dtype)

    row = pl.BlockSpec((tm, gw), lambda i: (i, 0))
    return pl.pallas_call(
        body, grid=(t // tm,), in_specs=[row, row, pl.BlockSpec((1, hd), lambda i: (0, 0))], out_specs=row,
        out_shape=jax.ShapeDtypeStruct((t, gw), BF16), compiler_params=_cp(("parallel",)), name=name,
    )(o, z, norm_w)


def _gdn_post_bwd(dy, o, z, norm_w, dp, dp_col, *, name, tm=512):
    t, gw = o.shape
    hd, nh = GDN_HEAD_DIM, GDN_HEADS
    tm = _blk(t, tm)

    def body(dy_ref, o_ref, z_ref, w_ref, _, do_ref, dz_ref, dw_ref):
        i = pl.program_id(0)
        zv = z_ref[...]
        sg = _sigmoid(zv)
        sz = zv * sg
        dsz = sg * (1.0 + zv * (1.0 - sg))
        dw = None
        for h in range(nh):
            sl = slice(h * hd, (h + 1) * hd)
            oh = o_ref[:, sl]
            dyh = dy_ref[:, sl].astype(F32)
            r = lax.rsqrt(jnp.mean(oh * oh, axis=-1, keepdims=True) + RMS_EPS)
            xh = oh * r
            dz_ref[:, sl] = (dyh * xh * w_ref[...] * dsz[:, sl]).astype(dz_ref.dtype)
            dn = dyh * sz[:, sl]
            dxh = dn * w_ref[...]
            do_ref[:, sl] = r * (dxh - xh * jnp.mean(dxh * xh, axis=-1, keepdims=True))
            part = jnp.sum(dn * xh, axis=0, keepdims=True)
            dw = part if dw is None else dw + part

        @pl.when(i == 0)
        def _():
            dw_ref[...] = dw

        @pl.when(i > 0)
        def _():
            dw_ref[...] += dw

    row = pl.BlockSpec((tm, gw), lambda i: (i, 0))
    vec = pl.BlockSpec((1, hd), lambda i: (0, 0))
    return pl.pallas_call(
        body, grid=(t // tm,), in_specs=[row, row, row, vec, ANY_SPEC],
        out_specs=[row, pl.BlockSpec((tm, gw), lambda i: (i, dp_col // gw)), vec],
        out_shape=[jax.ShapeDtypeStruct((t, gw), F32), jax.ShapeDtypeStruct(dp.shape, dp.dtype),
                   jax.ShapeDtypeStruct((1, hd), F32)],
        input_output_aliases={4: 1}, compiler_params=_cp(("arbitrary",)), name=name,
    )(dy, o, z, norm_w, dp)


IN_NAMES = ("q_a", "kv_a", "qkv_b", "ab", "z", "q_c", "gates")
CAT_NAMES = ("gates", "q_a", "qkv_b", "z", "q_c", "kv_a", "ab")
AB_PAD = 256


def _in_widths(d):
    gw = GDN_HEADS * GDN_HEAD_DIM
    return dict(q_a=SWA_Q_HEADS * SWA_HEAD_DIM, kv_a=2 * SWA_KV_HEADS * SWA_HEAD_DIM, qkv_b=3 * gw, ab=2 * GDN_HEADS,
                z=gw, q_c=XA_HEADS * XA_HEAD_DIM, gates=3 * d)


def _ranges(names, widths):
    out, start = {}, 0
    for k in names:
        out[k] = (start, widths[k])
        start += widths[k]
    return out, start


def _cat_ranges(d):
    widths = dict(_in_widths(d), ab=AB_PAD)
    return _ranges(CAT_NAMES, widths)


def _to_cat(shards, *, name="to_cat", tm=256):
    ns, d, n = shards.shape
    src, _ = _ranges(IN_NAMES, _in_widths(d))
    _, cat_w = _cat_ranges(d)
    pieces = []
    for k in CAT_NAMES:
        lo, hi = src[k][0], src[k][0] + src[k][1]
        for s in range(ns):
            a, b = max(lo, s * n), min(hi, (s + 1) * n)
            if a < b:
                pieces.append((s, a - s * n, b - s * n))
    tm = _blk(d, tm)

    def body(s_ref, o_ref):
        cols = [s_ref[s, :, a:b] for s, a, b in pieces]
        cols.append(jnp.zeros((tm, AB_PAD - src["ab"][1]), o_ref.dtype))
        o_ref[...] = jnp.concatenate(cols, axis=1)

    return pl.pallas_call(
        body, grid=(d // tm,),
        in_specs=[pl.BlockSpec((ns, tm, n), lambda i: (0, i, 0))],
        out_specs=pl.BlockSpec((tm, cat_w), lambda i: (i, 0)),
        out_shape=jax.ShapeDtypeStruct((d, cat_w), shards.dtype),
        compiler_params=_cp(("parallel",)), name=name,
    )(shards)


def _from_cat(w_cat, *, name="from_cat", tm=256):
    d, cat_w = w_cat.shape
    src, total = _ranges(IN_NAMES, _in_widths(d))
    cat, _ = _cat_ranges(d)
    n = total // N_SHARDS
    pieces = []
    for s in range(N_SHARDS):
        pieces.append([])
        for k in IN_NAMES:
            a, b = max(s * n, src[k][0]), min((s + 1) * n, src[k][0] + src[k][1])
            if a < b:
                pieces[s].append((cat[k][0] + a - src[k][0], cat[k][0] + b - src[k][0]))
    tm = _blk(d, tm)

    def body(c_ref, o_ref):
        for s in range(N_SHARDS):
            o_ref[s] = jnp.concatenate([c_ref[:, a:b] for a, b in pieces[s]], axis=1)

    return pl.pallas_call(
        body, grid=(d // tm,),
        in_specs=[pl.BlockSpec((tm, cat_w), lambda i: (i, 0))],
        out_specs=pl.BlockSpec((N_SHARDS, tm, n), lambda i: (0, i, 0)),
        out_shape=jax.ShapeDtypeStruct((N_SHARDS, d, n), w_cat.dtype),
        compiler_params=_cp(("parallel",)), name=name,
    )(w_cat)


def _pad_cols(a, width):
    return jnp.pad(a, ((0, 0), (0, width - a.shape[1])))


def _relu2_epilogue(acc):
    r = jnp.maximum(acc, 0.0)
    return acc, r * r


def _add_epilogue(acc, res):
    return (acc + res,)


def _drelu2_epilogue(acc, u):
    return (acc * (2.0 * jnp.maximum(u.astype(F32), 0.0)),)


def _local_step(x, mem, tgt, wts, small, comm=None):
    t, d = x.shape
    nh = GDN_HEADS
    cat, cat_w = _cat_ranges(d)
    alog_pad = _pad_cols(small["a_log"], LANES)
    dt_pad = _pad_cols(small["dt_bias"], LANES)
    kvw = cat["kv_a"][1]
    assert cat["ab"][0] == cat["kv_a"][0] + kvw
    ab_blk = kvw // LANES

    if comm is None:
        n = _rms_fwd(x, small["g_mix"], name="rms_mix")
        w_cat = wts["w_cat"]
    else:
        n, landed = _rms_fwd(x, small["g_mix"], name="rms_mix", exchange=comm.gather_exchange(["w_in"]))
        w_cat = _to_cat(_exchange_call(comm.pass_on(["w_in"], landed), name="ag_w_in_pass")[0])
    assert w_cat.shape == (d, cat_w)
    q_a = _mm(n, w_cat, b_window=cat["q_a"], out_dtypes=(BF16,), name="in_q_a")
    kv_a, ab = _mm(n, w_cat, b_window=(cat["kv_a"][0], kvw + AB_PAD), out_dtypes=(BF16, F32), name="in_kv_ab")
    qkvb = _mm(n, w_cat, b_window=cat["qkv_b"], tn=512, name="in_qkv_b")
    z = _mm(n, w_cat, b_window=cat["z"], name="in_z")
    q_c = _mm(n, w_cat, b_window=cat["q_c"], out_dtypes=(BF16,), name="in_q_c")
    if comm is None:
        gates = _mm(n, w_cat, b_window=cat["gates"], name="in_gates")
        y_a, lse = _swa_fwd(q_a, kv_a, small["sinks"], name="swa_fwd")
    else:
        gates, landed_mlp = _mm(n, w_cat, b_window=cat["gates"], name="in_gates",
                                exchange=comm.gather_exchange(comm.MLP[1:]))
        y_a, lse, landed = _swa_fwd(q_a, kv_a, small["sinks"], name="swa_fwd", exchange=comm.gather_exchange(comm.MLP[:1]))
        landed_mlp = landed + landed_mlp
    xc, qkvn, aux = _gdn_pre_fwd(qkvb, small["conv_w"], ab, alog_pad, dt_pad, ab_blk=ab_blk, name="gdn_pre_fwd")
    aux_t = aux[:, :16].reshape(t // GDN_CHUNK, GDN_CHUNK, 16).transpose(0, 2, 1)
    if comm is None:
        gdn_u, gdn_w, gdn_qd, gdn_kd, gdn_mm, gdn_tinv = _gdn_local_fwd(qkvn, aux, aux_t, name="gdn_local_fwd")
    else:
        gdn_u, gdn_w, gdn_qd, gdn_kd, gdn_mm, gdn_tinv, landed_mid = _gdn_local_fwd(
            qkvn, aux, aux_t, name="gdn_local_fwd", exchange=comm.gather_exchange(comm.mid))
        wts = dict(wts, **comm.gathered(comm.mid, landed_mid, "mid"))
    o_b, gdn_vn, s_all = _gdn_seq_fwd(gdn_u, gdn_w, gdn_qd, gdn_kd, gdn_mm, aux, name="gdn_seq_fwd")
    y_b = _gdn_post_fwd(o_b, z, small["gdn_norm_w"], name="gdn_post_fwd")
    nmem = _rms_fwd(mem, small["g_mem"], name="rms_mem")
    mkv = _mm(nmem, wts["w_mem_kv"], out_dtypes=(BF16,), name="mem_kv")
    y_c = _xa_fwd(q_c, mkv, name="xa_fwd")
    ys = (y_a, y_b, y_c)
    w_ups = (wts["w_swa_up"], wts["w_gdn_up"], wts["w_xa_up"])
    merged = _merge_fwd(ys, w_ups, gates, name="merge_fwd")
    if comm is None:
        h1 = _mm(merged, wts["w_out"], extras=(x,), epilogue=_add_epilogue, name="out_proj")
    else:
        h1, whole = _mm(merged, wts["w_out"], extras=(x,), epilogue=_add_epilogue, name="out_proj",
                        exchange=comm.pass_on(comm.MLP, landed_mlp))
        wts = dict(wts, **comm.as_weights(comm.MLP, whole))
    n2 = _rms_fwd(h1, small["g_mlp"], name="rms_mlp")
    u, act = _mm(n2, wts["w_mlp_in"], b_sharded=True, out_dtypes=(BF16, BF16), epilogue=_relu2_epilogue, name="mlp_in")
    h2 = _mm(act, wts["w_mlp_out"], extras=(h1,), epilogue=_add_epilogue, name="mlp_out")
    dh2, dh2_b, dg_final, loss = _final_loss(h2, small["g_final"], tgt, name="final_loss")

    grads = {"g_final": dg_final}
    du = _mm(dh2_b, wts["w_mlp_out"], tb=True, out_dtypes=(BF16,), extras=(u,), epilogue=_drelu2_epilogue, name="d_mlp_act")
    grads["w_mlp_out"] = _mm(act, dh2_b, ta=True, out_dtypes=(BF16,), name="dw_mlp_out")
    grads["w_mlp_in"] = _mm(n2, du, ta=True, out_sharded=True, out_dtypes=(BF16,), name="dw_mlp_in")
    if comm is None:
        dn2 = _mm(du, wts["w_mlp_in"], tb=True, b_sharded=True, name="d_mlp_in")
    else:
        g_mlp = [comm.shard_major(k, grads.pop(k)) for k in comm.MLP]
        dn2, sib_mlp = _mm(du, wts["w_mlp_in"], tb=True, b_sharded=True, name="d_mlp_in", exchange=_sibling_halves(g_mlp))
        s1_mlp = comm.pair_sums(g_mlp, "mlp", sib_mlp)
    dh1, dh1_b, grads["g_mlp"] = _rms_bwd(dn2, h1, small["g_mlp"], dh2, name="rms_mlp_bwd")
    dmerged = _mm(dh1_b, wts["w_out"], tb=True, name="d_out_proj")
    grads["w_out"] = _mm(merged, dh1_b, ta=True, out_dtypes=(BF16,), name="dw_out")
    *dus, dp = _merge_bwd(ys, w_ups, gates, dmerged, cat_w, name="merge_bwd")
    dys = []
    for y, du_i, w_up, key in zip(ys, dus, w_ups, ("w_swa_up", "w_gdn_up", "w_xa_up")):
        dys.append(_mm(du_i, w_up, tb=True, b_sharded=True, out_dtypes=(BF16,), name="d_" + key))
        grads[key] = _mm(y, du_i, ta=True, out_sharded=True, out_dtypes=(BF16,), name="dw_" + key[2:])
    dp, dkv_a, grads["sinks"] = _swa_bwd(q_a, kv_a, small["sinks"], y_a, lse, dys[0], dp, cat["q_a"][0], name="swa_bwd")
    do_b, dp, grads["gdn_norm_w"] = _gdn_post_bwd(dys[1], o_b, z, small["gdn_norm_w"], dp, cat["z"][0],
                                                  name="gdn_post_bwd")
    dvn, dqd, dkd, dw_, dlast = _gdn_seq_bwd(do_b, gdn_w, gdn_qd, gdn_kd, gdn_mm, gdn_vn, s_all, aux, name="gdn_seq_bwd")
    dqkvn, daux = _gdn_local_bwd(qkvn, aux, aux_t, gdn_tinv, gdn_u, gdn_w, gdn_vn, do_b, dvn, dqd, dkd, dw_, dlast,
                                 name="gdn_local_bwd")
    dxc, dp, dalog, ddt = _gdn_pre_bwd1(xc, dqkvn, daux, ab, alog_pad, dt_pad, dkv_a, dp, cat["kv_a"][0], ab_blk=ab_blk,
                                        name="gdn_pre_bwd1")
    grads["a_log"], grads["dt_bias"] = dalog[:, :nh], ddt[:, :nh]
    dp, grads["conv_w"] = _gdn_pre_bwd2(dxc, qkvb, small["conv_w"], dp, cat["qkv_b"][0], name="gdn_pre_bwd2")
    dp, dmkv = _xa_bwd(q_c, mkv, dys[2], dp, cat["q_c"][0], name="xa_bwd")
    grads["w_mem_kv"] = _mm(nmem, dmkv, ta=True, out_dtypes=(BF16,), name="dw_mem_kv")
    dnmem = _mm(dmkv, wts["w_mem_kv"], tb=True, name="d_mem_kv")
    _, _, grads["g_mem"] = _rms_bwd(dnmem, mem, small["g_mem"], jnp.zeros_like(mem), name="rms_mem_bwd")
    if comm is None:
        grads["w_cat"] = _mm(n, dp, ta=True, out_dtypes=(BF16,), name="dw_in")
        dn = _mm(dp, w_cat, tb=True, name="d_in_proj")
    else:
        s1_mid = comm.pair_sums([comm.shard_major(k, grads.pop(k)) for k in comm.mid], "mid")
        dw_cat, rcv_mlp = _mm(n, dp, ta=True, out_dtypes=(BF16,), name="dw_in", exchange=_chip_exchange(s1_mlp))
        s1_in = comm.pair_sums([_from_cat(dw_cat)], "in")
        dn, rcv_rest = _mm(dp, w_cat, tb=True, name="d_in_proj", exchange=_chip_exchange(s1_in + s1_mid))
        halves = comm.chip_sums(s1_in + s1_mid + s1_mlp, rcv_rest + rcv_mlp)
        reduced = _exchange_call(_join_halves(halves), name="rs_join_halves")
        grads.update(zip(["w_in"] + comm.mid + list(comm.MLP), reduced))
    dx, _, grads["g_mix"] = _rms_bwd(dn, x, small["g_mix"], dh1, name="rms_mix_bwd")
    return loss, dx, grads


HBM_SPEC = pl.BlockSpec(memory_space=pltpu.HBM)
VMEM_SPEC = pl.BlockSpec(memory_space=pltpu.VMEM)
N_CHIPS = N_SHARDS
N_DEV = 8
DMA_CHUNK_BYTES = 1 << 20


def _place():
    return lax.axis_index("x"), lax.axis_index("y"), lax.axis_index("c")


def _other_chips(x, y):
    return [(1 - x, y), (x, 1 - y), (1 - x, 1 - y)]


def _n_chunks(rows, row_bytes):
    n = 1
    while rows % (2 * n) == 0 and (rows // (2 * n)) % 16 == 0 and (rows // n) * row_bytes > DMA_CHUNK_BYTES:
        n *= 2
    return n


def _sem_scratch(n_remote, n_local):
    return [pltpu.SemaphoreType.DMA((max(n_remote, 1),)), pltpu.SemaphoreType.DMA((max(n_remote, 1),)),
            pltpu.SemaphoreType.DMA((max(n_local, 1),))]


def _gather_over_ici(shards):
    plan = _half_chunks(shards, 0)

    def copies_of(in_refs, out_refs, place):
        x, y, c = place
        remote = []
        for i, r0, nr in plan:
            mine = pl.ds(c * (shards[i].shape[0] // 2) + r0, nr)
            for chip in _other_chips(x, y):
                remote.append((in_refs[i].at[mine], out_refs[i].at[2 * x + y, mine], (*chip, c)))
        return remote, []

    shapes = tuple(jax.ShapeDtypeStruct((N_CHIPS, *s.shape), s.dtype) for s in shards)
    return Exchange(tuple(shards), shapes, 3 * len(plan), 0, copies_of)


def _gather_pass_on(arrived):
    plan = _half_chunks([jax.ShapeDtypeStruct(a.shape[1:], a.dtype) for a in arrived], 0)

    def copies_of(in_refs, out_refs, place):
        x, y, c = place
        remote = []
        for i, r0, nr in plan:
            mine = pl.ds(c * (arrived[i].shape[1] // 2) + r0, nr)
            for chip in _other_chips(x, y):
                rows = out_refs[i].at[2 * chip[0] + chip[1], mine]
                remote.append((rows, rows, (x, y, 1 - c)))
        return remote, []

    shapes = tuple(jax.ShapeDtypeStruct(a.shape, a.dtype) for a in arrived)
    return Exchange(tuple(arrived), shapes, 3 * len(plan), 0, copies_of, tuple((i, i) for i in range(len(arrived))))


def _place_own(arrived, shard, chip, *, name):
    r, c = shard.shape
    tb = _row_block(r, c, shard.dtype.itemsize)

    def body(chip_ref, s_ref, _, o_ref):
        o_ref[...] = s_ref[...]

    return pl.pallas_call(
        body, grid_spec=pltpu.PrefetchScalarGridSpec(
            num_scalar_prefetch=1, grid=(r // tb,),
            in_specs=[pl.BlockSpec((tb, c), lambda i, chip_ref: (i, 0)), ANY_SPEC],
            out_specs=pl.BlockSpec((None, tb, c), lambda i, chip_ref: (chip_ref[0], i, 0))),
        out_shape=jax.ShapeDtypeStruct(arrived.shape, arrived.dtype), input_output_aliases={2: 0},
        compiler_params=_cp(("parallel",)), name=name,
    )(chip, shard, arrived)


def _exchange_call(ex, *, name):
    n_in, n_out = len(ex.ins), len(ex.out_shapes)

    def body(*refs):
        cps = _exchange_copies(ex, refs[:n_in], refs[n_in:n_in + n_out], refs[n_in + n_out:])
        for cp in cps:
            cp.start()
        for cp in cps:
            cp.wait()

    return pl.pallas_call(
        body, out_shape=list(ex.out_shapes), in_specs=[HBM_SPEC] * n_in, out_specs=[HBM_SPEC] * n_out,
        scratch_shapes=_sem_scratch(ex.n_remote, ex.n_local), input_output_aliases=dict(ex.aliases), name=name,
    )(*ex.ins)


def _half_chunks(arrs, row_axis):
    plan = []
    for i, a in enumerate(arrs):
        rh = a.shape[row_axis] // 2
        row_bytes = a.dtype.itemsize * math.prod(a.shape) // a.shape[row_axis]
        nch = _n_chunks(rh, row_bytes)
        plan += [(i, q * (rh // nch), rh // nch) for q in range(nch)]
    return plan


def _sibling_halves(gs):
    plan = _half_chunks(gs, 1)

    def copies_of(in_refs, out_refs, place):
        x, y, c = place
        out = []
        for i, r0, nr in plan:
            rh = gs[i].shape[1] // 2
            out.append((in_refs[i].at[:, pl.ds((1 - c) * rh + r0, nr), :], out_refs[i].at[:, pl.ds(r0, nr), :],
                        (x, y, 1 - c)))
        return out, []

    shapes = tuple(jax.ShapeDtypeStruct((g.shape[0], g.shape[1] // 2, g.shape[2]), g.dtype) for g in gs)
    return Exchange(tuple(gs), shapes, len(plan), 0, copies_of)


def _chip_exchange(s1s):
    plan = _half_chunks([jax.ShapeDtypeStruct((2 * s.shape[1], s.shape[2]), s.dtype) for s in s1s], 0)

    def copies_of(in_refs, out_refs, place):
        x, y, c = place
        out = []
        for i, r0, nr in plan:
            for j, chip in enumerate(_other_chips(x, y)):
                out.append((in_refs[i].at[2 * chip[0] + chip[1], pl.ds(r0, nr), :], out_refs[i].at[j, pl.ds(r0, nr), :],
                            (*chip, c)))
        return out, []

    shapes = tuple(jax.ShapeDtypeStruct((3, *s.shape[1:]), s.dtype) for s in s1s)
    return Exchange(tuple(s1s), shapes, 3 * len(plan), 0, copies_of)


def _join_halves(gs):
    plan = _half_chunks(gs, 0)

    def copies_of(in_refs, out_refs, place):
        x, y, c = place
        out = []
        for i, r0, nr in plan:
            rows = out_refs[i].at[pl.ds(c * (gs[i].shape[0] // 2) + r0, nr), :]
            out.append((rows, rows, (x, y, 1 - c)))
        return out, []

    shapes = tuple(jax.ShapeDtypeStruct(g.shape, g.dtype) for g in gs)
    aliases = tuple((i, i) for i in range(len(gs)))
    return Exchange(tuple(gs), shapes, len(plan), 0, copies_of, aliases)


ROW_BLOCK_BYTES = 4 << 20


def _row_block(rows, cols, itemsize=4):
    tb = rows
    while tb % 32 == 0 and tb * cols * itemsize > ROW_BLOCK_BYTES:
        tb //= 2
    return tb


def _pair_sum(g, sib, core, *, name):
    ns, r, c = g.shape
    rh = r // 2
    tb = _row_block(rh, c, g.dtype.itemsize)
    nb = rh // tb

    def body(core_ref, g_ref, s_ref, o_ref):
        o_ref[...] = (g_ref[...].astype(F32) + s_ref[...].astype(F32)).astype(o_ref.dtype)

    mine = pl.BlockSpec((None, tb, c), lambda s, i, core_ref: (s, core_ref[0] * nb + i, 0))
    half = pl.BlockSpec((None, tb, c), lambda s, i, core_ref: (s, i, 0))
    return pl.pallas_call(
        body, grid_spec=pltpu.PrefetchScalarGridSpec(num_scalar_prefetch=1, grid=(ns, nb), in_specs=[mine, half],
                                                     out_specs=half),
        out_shape=jax.ShapeDtypeStruct((ns, rh, c), BF16), compiler_params=_cp(("parallel", "parallel")), name=name,
    )(core, g, sib)


def _chip_sum(s1, rcv, where, *, name):
    _, rh, c = s1.shape
    tb = _row_block(rh, c)
    nb = rh // tb

    def body(where_ref, own_ref, r0_ref, r1_ref, r2_ref, o_ref):
        acc = own_ref[...].astype(F32)
        for r in (r0_ref, r1_ref, r2_ref):
            acc = acc + r[...].astype(F32)
        o_ref[...] = acc

    own = pl.BlockSpec((None, tb, c), lambda i, w: (w[1], i, 0))
    got = [pl.BlockSpec((None, tb, c), functools.partial(lambda i, w, j: (j, i, 0), j=j)) for j in range(3)]
    return pl.pallas_call(
        body, grid_spec=pltpu.PrefetchScalarGridSpec(
            num_scalar_prefetch=1, grid=(nb,), in_specs=[own] + got,
            out_specs=pl.BlockSpec((tb, c), lambda i, w: (w[0] * nb + i, 0))),
        out_shape=jax.ShapeDtypeStruct((2 * rh, c), F32), compiler_params=_cp(("parallel",)), name=name,
    )(where, s1, rcv, rcv, rcv)


def _all_gather_small(blk, *, name):
    r = blk.shape[0]

    def body(b_ref, out_ref, send_sems, recv_sems):
        x, y, c = _place()
        me = 4 * x + 2 * y + c
        out_ref[me] = b_ref[...]
        sends = []
        for k in range(1, N_DEV):
            peer = (x ^ (k >> 2), y ^ ((k >> 1) & 1), c ^ (k & 1))
            sends.append(pltpu.make_async_remote_copy(src_ref=b_ref, dst_ref=out_ref.at[me], send_sem=send_sems.at[k - 1],
                                                      recv_sem=recv_sems.at[k - 1], device_id=peer, device_id_type=MESH))
        for cp in sends:
            cp.start()
        for k in range(1, N_DEV):
            rows = out_ref.at[me ^ k]
            pltpu.make_async_remote_copy(src_ref=rows, dst_ref=rows, send_sem=send_sems.at[k - 1],
                                         recv_sem=recv_sems.at[k - 1], device_id=(x, y, c), device_id_type=MESH).wait_recv()
        for cp in sends:
            cp.wait_send()

    return pl.pallas_call(
        body, out_shape=jax.ShapeDtypeStruct((N_DEV, r, LANES), blk.dtype), in_specs=[VMEM_SPEC], out_specs=VMEM_SPEC,
        scratch_shapes=[pltpu.SemaphoreType.DMA((N_DEV - 1,)), pltpu.SemaphoreType.DMA((N_DEV - 1,))],
        name=name,
    )(blk)


def _sum_rows(parts, out_dtype, *, name, tb=1024):
    rows = parts[0].shape[0]
    tb = _blk(rows, tb)

    def body(*refs):
        acc = refs[0][...].astype(F32)
        for r in refs[1:-1]:
            acc = acc + r[...].astype(F32)
        refs[-1][...] = acc.astype(refs[-1].dtype)

    spec = pl.BlockSpec((tb, LANES), lambda i: (i, 0))
    return pl.pallas_call(
        body, grid=(rows // tb,), in_specs=[spec] * len(parts), out_specs=spec,
        out_shape=jax.ShapeDtypeStruct((rows, LANES), out_dtype), compiler_params=_cp(("parallel",)), name=name,
    )(*parts)


BIG = (
    ("w_in", 1), ("w_mem_kv", 0), ("w_swa_up", 1), ("w_gdn_up", 1), ("w_xa_up", 1), ("w_out", 0), ("w_mlp_in", 1),
    ("w_mlp_out", 0))


class _Comm:
    MLP = ("w_mlp_in", "w_mlp_out")

    def __init__(self, late_shards, core, where):
        self.axis = dict(BIG)
        self.late_shards = late_shards
        self.mid = [k for k in late_shards if k not in self.MLP and k != "w_in"]
        self.core, self.where = core, where

    def gather_exchange(self, names):
        return _gather_over_ici([self.late_shards[k] for k in names])

    def as_weights(self, names, whole):
        return {k: (g.reshape(-1, g.shape[2]) if self.axis[k] == 0 else g) for k, g in zip(names, whole)}

    def gathered(self, names, landed, tag):
        return self.as_weights(names, _exchange_call(self.pass_on(names, landed), name=f"ag_{tag}_pass"))

    def pass_on(self, names, landed):
        chip = self.where[1:2]
        return _gather_pass_on([_place_own(a, self.late_shards[k], chip, name=f"ag_own_{k}") for k, a in zip(names, landed)])

    def shard_major(self, k, grad):
        return grad.reshape(N_CHIPS, -1, grad.shape[-1]) if self.axis[k] == 0 else grad

    def pair_sums(self, gs, tag, sibs=None):
        if sibs is None:
            sibs = _exchange_call(_sibling_halves(gs), name=f"rs_sibling_{tag}")
        return [_pair_sum(g, s, self.core, name=f"rs_pair_sum_{tag}{i}") for i, (g, s) in enumerate(zip(gs, sibs))]

    def chip_sums(self, s1s, rcvs):
        return [_chip_sum(s1, rcv, self.where, name=f"rs_chip_sum_{i}") for i, (s1, rcv) in enumerate(zip(s1s, rcvs))]
SMALL = ("g_mix", "sinks", "a_log", "dt_bias", "gdn_norm_w", "g_mem", "g_mlp", "g_final")


def _rows128(a, rows):
    flat = a.reshape(-1)
    return jnp.pad(flat, (0, rows * LANES - flat.shape[0])).reshape(rows, LANES)


def kernel(x, mem, g_mix, w_in, sinks, conv_w, a_log, dt_bias, gdn_norm_w, g_mem, w_mem_kv, w_swa_up, w_gdn_up, w_xa_up, w_out, g_mlp, w_mlp_in, w_mlp_out, g_final, loss_target, m_g_mix, m_w_in, m_sinks, m_conv_w, m_a_log, m_dt_bias, m_gdn_norm_w, m_g_mem, m_w_mem_kv, m_w_swa_up, m_w_gdn_up, m_w_xa_up, m_w_out, m_g_mlp, m_w_mlp_in, m_w_mlp_out, m_g_final, v_g_mix, v_w_in, v_sinks, v_conv_w, v_a_log, v_dt_bias, v_gdn_norm_w, v_g_mem, v_w_mem_kv, v_w_swa_up, v_w_gdn_up, v_w_xa_up, v_w_out, v_g_mlp, v_w_mlp_in, v_w_mlp_out, v_g_final):
    given = dict(locals())
    xi, yi, ci = _place()
    chip = 2 * xi + yi
    core = jnp.reshape(ci, (1,)).astype(jnp.int32)
    where = jnp.stack([ci, chip]).astype(jnp.int32)

    comm = _Comm({k: given[k][0].astype(BF16) for k, _ in BIG}, core, where)
    wts = {}
    conv_shard = conv_w[0]
    conv_rows = -(-conv_shard.size // (8 * LANES)) * 8
    conv_all = _all_gather_small(_rows128(conv_shard, conv_rows), name="ag_conv")
    conv_full = jnp.concatenate(
        [conv_all[2 * s].reshape(-1)[:conv_shard.size].reshape(conv_shard.shape) for s in range(N_CHIPS)], axis=1)

    small = {k: given[k].reshape(1, -1) for k in SMALL}
    small["conv_w"] = conv_full
    loss_row, dx, grads = _local_step(x[0], mem[0], loss_target[0], wts, small, comm)
    big_grads = {k: grads[k] for k, _ in BIG}

    layout = [("loss", loss_row[:, :1])] + [(k, grads[k]) for k in SMALL] + [("conv_w", grads["conv_w"])]
    rows = [-(-a.size // LANES) for _, a in layout]
    blk_rows = -(-sum(rows) // 8) * 8
    blk = jnp.concatenate([_rows128(a.astype(F32), n) for (_, a), n in zip(layout, rows)]
                          + [jnp.zeros((blk_rows - sum(rows), LANES), F32)], axis=0)
    gathered = _all_gather_small(blk, name="ag_small_grads")
    reduced = _sum_rows([gathered[i] for i in range(N_DEV)], F32, name="small_grad_sum")
    small_grads, start = {}, 0
    for (k, a), n in zip(layout, rows):
        small_grads[k] = reduced[start:start + n].reshape(-1)[:a.size].reshape(a.shape)
        start += n
    loss = small_grads["loss"].reshape(())
    cw = conv_shard.shape[1]
    conv_grad = lax.dynamic_slice_in_dim(small_grads["conv_w"], chip * cw, cw, axis=1)

    names = ["g_mix", "w_in", "sinks", "conv_w", "a_log", "dt_bias", "gdn_norm_w", "g_mem", "w_mem_kv", "w_swa_up",
             "w_gdn_up", "w_xa_up", "w_out", "g_mlp", "w_mlp_in", "w_mlp_out", "g_final"]
    out_g, out_d, out_m, out_v = [], [], [], []
    for k in names:
        w, m, v = given[k], given["m_" + k], given["v_" + k]
        if k in big_grads:
            g2 = big_grads[k]
        elif k == "conv_w":
            g2 = conv_grad
        else:
            g2 = small_grads[k]
        as_given = (lambda a: a.reshape(1, -1)) if w.ndim == 1 else (lambda a: a)
        if w.shape[-1] % LANES and w.shape[-1] > LANES:
            tr = lambda a: jnp.swapaxes(a, -1, -2)
            g_out, delta, new_m, new_v = (tr(a) for a in _adamw(tr(w), tr(g2), tr(m), tr(v), name="adamw_" + k))
        else:
            g_out, delta, new_m, new_v = _adamw(as_given(w), g2, as_given(m), as_given(v), name="adamw_" + k)
        out_g.append(g_out.reshape(w.shape))
        out_d.append(delta.reshape(w.shape))
        out_m.append(new_m.reshape(w.shape))
        out_v.append(new_v.reshape(w.shape))
    return (loss, dx[None], *out_g, *out_d, *out_m, *out_v)
```

```python
import functools
import math
from typing import Callable, NamedTuple

import jax
import jax.numpy as jnp
from jax import lax
from jax.experimental import pallas as pl
from jax.experimental.pallas import tpu as pltpu

F32 = jnp.float32
BF16 = jnp.bfloat16
HI = lax.Precision.HIGHEST
MESH = pl.DeviceIdType.MESH

SWA_Q_HEADS = 16
SWA_KV_HEADS = 2
SWA_HEAD_DIM = 64
SWA_WINDOW = 128
SWA_SCALE = SWA_HEAD_DIM ** -0.5
assert math.frexp(SWA_SCALE)[0] == 0.5
GDN_HEADS = 4
GDN_HEAD_DIM = 128
GDN_CONV = 4
GDN_CHUNK = 64
XA_HEADS = 4
XA_HEAD_DIM = 128
RMS_EPS = 1e-6
L2_EPS = 1e-6
ADAM_LR = 0.001
ADAM_B1 = 0.9
ADAM_B2 = 0.999
ADAM_EPS = 1e-08
ADAM_WD = 0.01
ADAM_STEP = 10

LANES = 128
N_SHARDS = 4
VMEM_LIMIT = 56 * 1024 * 1024

NT = (((1,), (1,)), ((), ()))
TN = (((0,), (0,)), ((), ()))
NN = (((1,), (0,)), ((), ()))


def _cp(sem=None):
    return pltpu.CompilerParams(dimension_semantics=sem, vmem_limit_bytes=VMEM_LIMIT)


def _blk(dim, pref):
    if dim <= pref:
        return dim
    b = (pref // LANES) * LANES
    while dim % b:
        b -= LANES
    assert b > 0, (dim, pref)
    return b


def _dot(a, b, dims=NN, precision=None):
    return lax.dot_general(a, b, dims, precision=precision, preferred_element_type=F32)


def _sigmoid(x):
    return 0.5 * jnp.tanh(0.5 * x) + 0.5


MM_TK_BYTES = 4096


def _mm(a, b, *, name, ta=False, tb=False, out_dtypes=(F32,), epilogue=None, extras=(), tm=1024, tn=1024, tk=None,
        b_sharded=False, out_sharded=False, b_window=None, exchange=None):
    (kdim, m) = a.shape if ta else a.shape[::-1]
    col0 = 0
    n_lim = k_lim = None
    if b_sharded:
        ns, rows_w, per = b.shape
        if tb:
            kb, n, k_lim = ns * per, rows_w, per
        else:
            kb, n, n_lim = rows_w, ns * per, per
    else:
        (kb, n) = b.shape[::-1] if tb else b.shape
        if b_window is not None:
            assert not tb
            col0, n = b_window
    assert kdim == kb, (a.shape, b.shape, ta, tb)
    if out_sharded:
        assert n % N_SHARDS == 0
        n_lim = n // N_SHARDS if n_lim is None else n_lim
        assert n_lim == n // N_SHARDS
    if tk is None:
        tk = MM_TK_BYTES // max(a.dtype.itemsize, b.dtype.itemsize)
    tm, tn, tk = _blk(m, tm), _blk(n_lim or n, tn), _blk(k_lim or kdim, tk)
    assert col0 % tn == 0, (col0, tn)
    nk = kdim // tk
    a_spec = pl.BlockSpec((tk, tm), lambda i, j, k: (k, i)) if ta else pl.BlockSpec((tm, tk), lambda i, j, k: (i, k))
    if b_sharded and tb:
        kpb = k_lim // tk
        b_spec = pl.BlockSpec((None, tn, tk), lambda i, j, k: (k // kpb, j, k % kpb))
    elif b_sharded:
        bpb = n_lim // tn
        b_spec = pl.BlockSpec((None, tk, tn), lambda i, j, k: (j // bpb, k, j % bpb))
    elif tb:
        b_spec = pl.BlockSpec((tn, tk), lambda i, j, k: (j, k))
    else:
        b_spec = pl.BlockSpec((tk, tn), lambda i, j, k: (k, j + col0 // tn))
    x_spec = pl.BlockSpec((tm, tn), lambda i, j, k: (i, j))
    if out_sharded:
        opb = n_lim // tn
        o_spec = pl.BlockSpec((None, tm, tn), lambda i, j, k: (j // opb, i, j % opb))
        out_shape = (N_SHARDS, m, n_lim)
    else:
        o_spec, out_shape = x_spec, (m, n)
    dims = ((((0 if ta else 1),), ((1 if tb else 0),)), ((), ()))
    n_extra, n_out = len(extras), len(out_dtypes)

    host = _ExchangeHost(exchange)
    grid = (m // tm, n // tn, nk)

    def body(*refs):
        a_ref, b_ref = refs[:2]
        extra_refs = refs[2:2 + n_extra]
        out_refs = refs[2 + n_extra + host.n_in:2 + n_extra + host.n_in + n_out]
        host.start(refs, 2 + n_extra, 2 + n_extra + host.n_in + n_out, grid)
        part = _dot(a_ref[...].astype(BF16), b_ref[...].astype(BF16), dims)

        def finish(acc):
            vals = epilogue(acc, *[r[...] for r in extra_refs]) if epilogue is not None else (acc,) * n_out
            assert len(vals) == n_out
            for r, v in zip(out_refs, vals):
                r[...] = v.astype(r.dtype)

        if nk == 1:
            finish(part)
        else:
            acc_ref = refs[2 + n_extra + host.n_in + n_out + host.n_out]
            k = pl.program_id(2)

            @pl.when(k == 0)
            def _():
                acc_ref[...] = part

            @pl.when((k > 0) & (k < nk - 1))
            def _():
                acc_ref[...] += part

            @pl.when(k == nk - 1)
            def _():
                finish(acc_ref[...] + part)

        host.wait(refs, 2 + n_extra, 2 + n_extra + host.n_in + n_out, grid)

    outs = pl.pallas_call(
        body,
        grid=grid,
        in_specs=[a_spec, b_spec] + [x_spec] * n_extra + host.in_specs,
        out_specs=[o_spec] * n_out + host.out_specs,
        out_shape=[jax.ShapeDtypeStruct(out_shape, d) for d in out_dtypes] + host.out_shapes,
        scratch_shapes=([pltpu.VMEM((tm, tn), F32)] if nk > 1 else []) + host.scratch,
        input_output_aliases=host.aliases(2 + n_extra, n_out),
        compiler_params=_cp(host.semantics(("parallel", "parallel", "arbitrary"))),
        name=name,
    )(a, b, *extras, *host.ins)
    mine, landed = outs[:n_out], list(outs[n_out:])
    mine = mine[0] if n_out == 1 else mine
    return (mine, landed) if exchange is not None else mine


class Exchange(NamedTuple):
    ins: tuple
    out_shapes: tuple
    n_remote: int
    n_local: int
    copies_of: Callable
    aliases: tuple = ()


def _exchange_copies(ex, in_refs, out_refs, sem_refs):
    send_sems, recv_sems, local_sems = sem_refs
    remote, local = ex.copies_of(in_refs, out_refs, _place())
    assert len(remote) == ex.n_remote and len(local) == ex.n_local, (len(remote), len(local))
    cps = [pltpu.make_async_remote_copy(src_ref=src, dst_ref=dst, send_sem=send_sems.at[k], recv_sem=recv_sems.at[k],
                                        device_id=to, device_id_type=MESH) for k, (src, dst, to) in enumerate(remote)]
    cps += [pltpu.make_async_copy(src, dst, local_sems.at[k]) for k, (src, dst) in enumerate(local)]
    return cps


class _ExchangeHost:
    def __init__(self, ex):
        self.ex = ex
        self.ins = list(ex.ins) if ex else []
        self.out_shapes = list(ex.out_shapes) if ex else []
        self.n_in, self.n_out = len(self.ins), len(self.out_shapes)
        self.in_specs = [HBM_SPEC] * self.n_in
        self.out_specs = [HBM_SPEC] * self.n_out
        self.scratch = _sem_scratch(ex.n_remote, ex.n_local) if ex else []

    def semantics(self, sem):
        return tuple("arbitrary" for _ in sem) if self.ex else sem

    def aliases(self, in_at, out_at):
        return {in_at + i: out_at + o for i, o in self.ex.aliases} if self.ex else {}

    def _refs(self, refs, in_at, out_at):
        return refs[in_at:in_at + self.n_in], refs[out_at:out_at + self.n_out], refs[len(refs) - 3:]

    def _when(self, grid, last):
        cond = None
        for d, size in enumerate(grid):
            c = pl.program_id(d) == (size - 1 if last else 0)
            cond = c if cond is None else cond & c
        return cond

    def start(self, refs, in_at, out_at, grid):
        if self.ex:
            @pl.when(self._when(grid, False))
            def _():
                for cp in _exchange_copies(self.ex, *self._refs(refs, in_at, out_at)):
                    cp.start()

    def wait(self, refs, in_at, out_at, grid):
        if self.ex:
            @pl.when(self._when(grid, True))
            def _():
                for cp in _exchange_copies(self.ex, *self._refs(refs, in_at, out_at)):
                    cp.wait()


def _rms_fwd(x, g, *, name, tm=512, exchange=None):
    t, d = x.shape
    tm = _blk(t, tm)
    host = _ExchangeHost(exchange)
    grid = (t // tm,)

    def body(*refs):
        x_ref, g_ref, n_ref = refs[0], refs[1], refs[2 + host.n_in]
        host.start(refs, 2, 3 + host.n_in, grid)
        xv = x_ref[...]
        r = lax.rsqrt(jnp.mean(xv * xv, axis=-1, keepdims=True) + RMS_EPS)
        n_ref[...] = (xv * r * g_ref[...]).astype(n_ref.dtype)
        host.wait(refs, 2, 3 + host.n_in, grid)

    outs = pl.pallas_call(
        body, grid=grid,
        in_specs=[pl.BlockSpec((tm, d), lambda i: (i, 0)), pl.BlockSpec((1, d), lambda i: (0, 0))] + host.in_specs,
        out_specs=[pl.BlockSpec((tm, d), lambda i: (i, 0))] + host.out_specs,
        out_shape=[jax.ShapeDtypeStruct((t, d), BF16)] + host.out_shapes,
        scratch_shapes=host.scratch, input_output_aliases=host.aliases(2, 1),
        compiler_params=_cp(host.semantics(("parallel",))), name=name,
    )(x, g, *host.ins)
    return (outs[0], list(outs[1:])) if exchange is not None else outs[0]


def _rms_bwd(dn, x, g, dres, *, name, tm=512):
    t, d = x.shape
    tm = _blk(t, tm)

    def body(dn_ref, x_ref, g_ref, dres_ref, dx_ref, dxb_ref, dg_ref):
        i = pl.program_id(0)
        xv = x_ref[...]
        r = lax.rsqrt(jnp.mean(xv * xv, axis=-1, keepdims=True) + RMS_EPS)
        xh = xv * r
        dnv = dn_ref[...].astype(F32)
        dxh = dnv * g_ref[...]
        dx = dres_ref[...] + r * (dxh - xh * jnp.mean(dxh * xh, axis=-1, keepdims=True))
        dx_ref[...] = dx
        dxb_ref[...] = dx.astype(dxb_ref.dtype)
        part = jnp.sum(dnv * xh, axis=0, keepdims=True)

        @pl.when(i == 0)
        def _():
            dg_ref[...] = part

        @pl.when(i > 0)
        def _():
            dg_ref[...] += part

    row = pl.BlockSpec((tm, d), lambda i: (i, 0))
    vec = pl.BlockSpec((1, d), lambda i: (0, 0))
    return pl.pallas_call(
        body, grid=(t // tm,),
        in_specs=[row, row, vec, row], out_specs=[row, row, vec],
        out_shape=[jax.ShapeDtypeStruct((t, d), F32), jax.ShapeDtypeStruct((t, d), BF16),
                   jax.ShapeDtypeStruct((1, d), F32)],
        compiler_params=_cp(("arbitrary",)), name=name,
    )(dn, x, g, dres)


def _final_loss(h, g, tgt, *, name, tm=512):
    t, d = h.shape
    tm = _blk(t, tm)

    def body(h_ref, g_ref, t_ref, dh_ref, dhb_ref, dg_ref, loss_ref):
        i = pl.program_id(0)
        hv = h_ref[...]
        r = lax.rsqrt(jnp.mean(hv * hv, axis=-1, keepdims=True) + RMS_EPS)
        xh = hv * r
        e = xh * g_ref[...] - t_ref[...]
        dy = e * (1.0 / d)
        dxh = dy * g_ref[...]
        dh = r * (dxh - xh * jnp.mean(dxh * xh, axis=-1, keepdims=True))
        dh_ref[...] = dh
        dhb_ref[...] = dh.astype(dhb_ref.dtype)
        dg_part = jnp.sum(dy * xh, axis=0, keepdims=True)
        row_loss = jnp.sum(e * e, axis=-1, keepdims=True) * (0.5 / d)
        loss_part = jnp.sum(row_loss, axis=0, keepdims=True)

        @pl.when(i == 0)
        def _():
            dg_ref[...] = dg_part
            loss_ref[...] = jnp.broadcast_to(loss_part, loss_ref.shape)

        @pl.when(i > 0)
        def _():
            dg_ref[...] += dg_part
            loss_ref[...] += jnp.broadcast_to(loss_part, loss_ref.shape)

    row = pl.BlockSpec((tm, d), lambda i: (i, 0))
    vec = pl.BlockSpec((1, d), lambda i: (0, 0))
    return pl.pallas_call(
        body, grid=(t // tm,),
        in_specs=[row, vec, row], out_specs=[row, row, vec, pl.BlockSpec((1, LANES), lambda i: (0, 0))],
        out_shape=[jax.ShapeDtypeStruct((t, d), F32), jax.ShapeDtypeStruct((t, d), BF16),
                   jax.ShapeDtypeStruct((1, d), F32), jax.ShapeDtypeStruct((1, LANES), F32)],
        compiler_params=_cp(("arbitrary",)), name=name,
    )(h, g, tgt)


SWA_SUB = 64


def _swa_mask(n, rows, row0):
    w = SWA_WINDOW
    qi = (lax.broadcasted_iota(jnp.int32, (rows, 2 * w), 0) + row0) & (w - 1)
    kj = lax.broadcasted_iota(jnp.int32, (rows, 2 * w), 1)
    return (kj > qi) & (kj <= qi + w) & ((n > 0) | (kj >= w))


def _stack_heads(ref, heads, width):
    return jnp.concatenate([ref[:, h * width:(h + 1) * width] for h in heads], axis=0)


def _stack_scalars(ref, heads, rows):
    return jnp.concatenate([jnp.broadcast_to(ref[0:1, h:h + 1], (rows, 1)) for h in heads], axis=0)


def _swa_fwd(q, kv, sinks, *, name, exchange=None):
    t = q.shape[0]
    w, hd, hq, hkv = SWA_WINDOW, SWA_HEAD_DIM, SWA_Q_HEADS, SWA_KV_HEADS
    grp = hq // hkv
    kvw = hkv * hd
    nb = t // w
    host = _ExchangeHost(exchange)
    assert not (exchange and exchange.aliases)

    def body(*refs):
        q_ref, kvp_ref, kvc_ref, s_ref = refs[:4]
        o_ref, lse_ref = refs[4 + host.n_in:6 + host.n_in]
        host.start(refs, 4, 6 + host.n_in, (nb,))
        n = pl.program_id(0)
        mask = _swa_mask(n, grp * w, 0)
        kvcat = jnp.concatenate([kvp_ref[...], kvc_ref[...]], axis=0)
        kvs = range(hkv)
        heads = [range(hk * grp, (hk + 1) * grp) for hk in kvs]
        sks = [_stack_scalars(s_ref, hs, w) for hs in heads]
        ss = [jnp.where(mask, _dot(_stack_heads(q_ref, heads[hk], hd) * SWA_SCALE, kvcat[:, hk * hd:(hk + 1) * hd], NT),
                        -jnp.inf) for hk in kvs]
        ms = [jnp.maximum(jnp.max(s, axis=-1, keepdims=True), sk) for s, sk in zip(ss, sks)]
        ps = [jnp.exp(s - m) for s, m in zip(ss, ms)]
        dens = [jnp.sum(p, axis=-1, keepdims=True) + jnp.exp(sk - m) for p, sk, m in zip(ps, sks, ms)]
        os_ = [_dot((p * (1.0 / den)).astype(BF16), kvcat[:, kvw + hk * hd:kvw + (hk + 1) * hd])
               for hk, p, den in zip(kvs, ps, dens)]
        outs, lses = [], []
        for o, m, den in zip(os_, ms, dens):
            lse = m + jnp.log(den)
            outs += [o[j * w:(j + 1) * w] for j in range(grp)]
            lses += [lse[j * w:(j + 1) * w] for j in range(grp)]
        o_ref[...] = jnp.concatenate(outs, axis=1).astype(o_ref.dtype)
        lse_ref[...] = jnp.concatenate(lses, axis=1)
        host.wait(refs, 4, 6 + host.n_in, (nb,))

    outs = pl.pallas_call(
        body, grid=(nb,),
        in_specs=[pl.BlockSpec((w, hq * hd), lambda i: (i, 0)),
                  pl.BlockSpec((w, 2 * kvw), lambda i: (jnp.maximum(i - 1, 0), 0)),
                  pl.BlockSpec((w, 2 * kvw), lambda i: (i, 0)),
                  pl.BlockSpec((1, hq), lambda i: (0, 0))] + host.in_specs,
        out_specs=[pl.BlockSpec((w, hq * hd), lambda i: (i, 0)), pl.BlockSpec((w, hq), lambda i: (i, 0))] + host.out_specs,
        out_shape=[jax.ShapeDtypeStruct((t, hq * hd), BF16), jax.ShapeDtypeStruct((t, hq), F32)] + host.out_shapes,
        scratch_shapes=host.scratch,
        compiler_params=_cp(host.semantics(("parallel",))), name=name,
    )(q, kv, kv, sinks, *host.ins)
    return (outs[0], outs[1], list(outs[2:])) if exchange is not None else outs


ANY_SPEC = pl.BlockSpec(memory_space=pl.ANY)


def _swa_bwd(q, kv, sinks, o, lse, do, dp, dp_col, *, name):
    t = q.shape[0]
    w, hd, hq, hkv = SWA_WINDOW, SWA_HEAD_DIM, SWA_Q_HEADS, SWA_KV_HEADS
    grp = hq // hkv
    kvw = hkv * hd
    nb = t // w
    assert dp_col % (hq * hd) == 0
    dq_blk = dp_col // (hq * hd)

    def body(q_ref, kvp_ref, kvc_ref, s_ref, o_ref, lse_ref, do_ref, _, dq_ref, dkv_ref, ds_ref, carry_ref, s_scr, dp_scr,
             p_scr, ds_scr):
        n = pl.program_id(0)

        @pl.when(n == 0)
        def _():
            ds_ref[...] = jnp.zeros_like(ds_ref)
            carry_ref[...] = jnp.zeros_like(carry_ref)

        @pl.when(n < nb)
        def _():
            kvcat = jnp.concatenate([kvp_ref[...], kvc_ref[...]], axis=0)
            dqs, dsk, dks, dvs = [], [], [], []
            for hk in range(hkv):
                heads = range(hk * grp, (hk + 1) * grp)
                qs = _stack_heads(q_ref, heads, hd)
                dos = _stack_heads(do_ref, heads, hd)
                os_ = _stack_heads(o_ref, heads, hd)
                lse = _stack_heads(lse_ref, heads, 1)
                kh = kvcat[:, hk * hd:(hk + 1) * hd]
                vh = kvcat[:, kvw + hk * hd:kvw + (hk + 1) * hd]
                delta = jnp.sum(dos.astype(F32) * os_.astype(F32), axis=-1, keepdims=True)
                s_scr[...] = _dot(qs * SWA_SCALE, kh, NT)
                dp_scr[...] = _dot(dos, vh, NT)
                for r0 in range(0, grp * w, SWA_SUB):
                    rows = slice(r0, r0 + SWA_SUB)
                    p = jnp.exp(jnp.where(_swa_mask(n, SWA_SUB, r0 % w), s_scr[rows, :], -jnp.inf) - lse[rows])
                    p_scr[rows, :] = p.astype(p_scr.dtype)
                    ds_scr[rows, :] = (p * (dp_scr[rows, :] - delta[rows]) * SWA_SCALE).astype(ds_scr.dtype)
                ds = ds_scr[...]
                dq = _dot(ds, kh)
                dqs += [dq[j * w:(j + 1) * w] for j in range(grp)]
                dks.append(_dot(ds, qs, TN))
                dvs.append(_dot(p_scr[...], dos, TN))
                dsink = -jnp.exp(_stack_scalars(s_ref, heads, w) - lse) * delta
                dsk += [jnp.sum(dsink[j * w:(j + 1) * w], axis=0, keepdims=True) for j in range(grp)]
            dq_ref[...] = jnp.concatenate(dqs, axis=1).astype(dq_ref.dtype)
            ds_ref[...] += jnp.concatenate(dsk, axis=1)
            dkv_cat = jnp.concatenate(dks + dvs, axis=1)
            dkv_ref[...] = (carry_ref[...] + dkv_cat[:w]).astype(dkv_ref.dtype)
            carry_ref[...] = dkv_cat[w:]

        @pl.when(n == nb)
        def _():
            dkv_ref[...] = carry_ref[...].astype(dkv_ref.dtype)

    cur = lambda i: (jnp.minimum(i, nb - 1), 0)
    prev = lambda i: (jnp.clip(i - 1, 0, nb - 1), 0)
    return pl.pallas_call(
        body, grid=(nb + 1,),
        in_specs=[pl.BlockSpec((w, hq * hd), cur), pl.BlockSpec((w, 2 * kvw), prev), pl.BlockSpec((w, 2 * kvw), cur),
                  pl.BlockSpec((1, hq), lambda i: (0, 0)), pl.BlockSpec((w, hq * hd), cur),
                  pl.BlockSpec((w, hq), cur), pl.BlockSpec((w, hq * hd), cur), ANY_SPEC],
        out_specs=[pl.BlockSpec((w, hq * hd), lambda i: (jnp.minimum(i, nb - 1), dq_blk)),
                   pl.BlockSpec((w, 2 * kvw), prev), pl.BlockSpec((1, hq), lambda i: (0, 0))],
        out_shape=[jax.ShapeDtypeStruct(dp.shape, dp.dtype), jax.ShapeDtypeStruct((t, 2 * kvw), BF16),
                   jax.ShapeDtypeStruct((1, hq), F32)],
        scratch_shapes=[pltpu.VMEM((w, 2 * kvw), F32)] + [pltpu.VMEM((grp * w, 2 * w), dt) for dt in (F32, F32, BF16, BF16)],
        input_output_aliases={7: 0},
        compiler_params=_cp(("arbitrary",)), name=name,
    )(q, kv, kv, sinks, o, lse, do, dp)


def _xa_fwd(q, mkv, *, name, tq=512):
    t, xw = q.shape
    nm = mkv.shape[0]
    hd, nh = XA_HEAD_DIM, XA_HEADS
    tq = _blk(t, tq)

    def body(q_ref, mkv_ref, o_ref):
        cols = [slice(h * hd, (h + 1) * hd) for h in range(nh)]
        ss = [_dot(q_ref[:, c], mkv_ref[:, c], NT) * (hd ** -0.5) for c in cols]
        ps = [jnp.exp(s - jnp.max(s, axis=-1, keepdims=True)) for s in ss]
        ps = [p * (1.0 / jnp.sum(p, axis=-1, keepdims=True)) for p in ps]
        outs = [_dot(p.astype(BF16), mkv_ref[:, xw + c.start:xw + c.stop]) for p, c in zip(ps, cols)]
        o_ref[...] = jnp.concatenate(outs, axis=1).astype(o_ref.dtype)

    return pl.pallas_call(
        body, grid=(t // tq,),
        in_specs=[pl.BlockSpec((tq, xw), lambda i: (i, 0)), pl.BlockSpec((nm, 2 * xw), lambda i: (0, 0))],
        out_specs=pl.BlockSpec((tq, xw), lambda i: (i, 0)),
        out_shape=jax.ShapeDtypeStruct((t, xw), BF16),
        compiler_params=_cp(("parallel",)), name=name,
    )(q, mkv)


def _xa_bwd(q, mkv, do, dp, dp_col, *, name, tq=512):
    t, xw = q.shape
    nm = mkv.shape[0]
    hd, nh = XA_HEAD_DIM, XA_HEADS
    tq = _blk(t, tq)
    assert dp_col % xw == 0

    def body(q_ref, mkv_ref, do_ref, _, dq_ref, dmkv_ref):
        i = pl.program_id(0)
        cols = [slice(h * hd, (h + 1) * hd) for h in range(nh)]
        vcols = [slice(xw + c.start, xw + c.stop) for c in cols]
        ss = [_dot(q_ref[:, c], mkv_ref[:, c], NT) * (hd ** -0.5) for c in cols]
        dps = [_dot(do_ref[:, c], mkv_ref[:, v], NT) for c, v in zip(cols, vcols)]
        ps = [jnp.exp(s - jnp.max(s, axis=-1, keepdims=True)) for s in ss]
        ps = [p * (1.0 / jnp.sum(p, axis=-1, keepdims=True)) for p in ps]
        dss = [(p * (dp - jnp.sum(p * dp, axis=-1, keepdims=True)) * (hd ** -0.5)).astype(BF16) for p, dp in zip(ps, dps)]
        dqs = [_dot(ds, mkv_ref[:, c]) for ds, c in zip(dss, cols)]
        dks = [_dot(ds, q_ref[:, c], TN) for ds, c in zip(dss, cols)]
        dvs = [_dot(p.astype(BF16), do_ref[:, c], TN) for p, c in zip(ps, cols)]
        dq_ref[...] = jnp.concatenate(dqs, axis=1).astype(dq_ref.dtype)
        part = jnp.concatenate(dks + dvs, axis=1)

        @pl.when(i == 0)
        def _():
            dmkv_ref[...] = part

        @pl.when(i > 0)
        def _():
            dmkv_ref[...] += part

    row = pl.BlockSpec((tq, xw), lambda i: (i, 0))
    full = pl.BlockSpec((nm, 2 * xw), lambda i: (0, 0))
    return pl.pallas_call(
        body, grid=(t // tq,),
        in_specs=[row, full, row, ANY_SPEC],
        out_specs=[pl.BlockSpec((tq, xw), lambda i: (i, dp_col // xw)), full],
        out_shape=[jax.ShapeDtypeStruct(dp.shape, dp.dtype), jax.ShapeDtypeStruct((nm, 2 * xw), F32)],
        input_output_aliases={3: 0}, compiler_params=_cp(("arbitrary",)), name=name,
    )(q, mkv, do, dp)


def _merge_specs(ys, ws, tm):
    y_specs = [pl.BlockSpec((tm, y.shape[1]), lambda i: (i, 0)) for y in ys]
    w_specs = [pl.BlockSpec(w.shape, lambda i: (0, 0, 0)) for w in ws]
    return y_specs, w_specs


def _merge_tiles(ws, tn):
    ns, _, per = ws[0].shape
    tn = _blk(per, tn)
    return tn, [(s, c, s * per + c) for s in range(ns) for c in range(0, per, tn)]


def _merge_fwd(ys, ws, gates, *, name, tm=256, tn=512):
    t, d = ys[0].shape[0], ws[0].shape[0] * ws[0].shape[2]
    tm = _blk(t, tm)
    tn, tiles = _merge_tiles(ws, tn)
    y_specs, w_specs = _merge_specs(ys, ws, tm)

    def body(ya, yb, yc, wa, wb, wc, g_ref, o_ref):
        for s, c, col in tiles:
            acc = None
            for b, (y, w) in enumerate(((ya, wa), (yb, wb), (yc, wc))):
                term = _sigmoid(g_ref[:, b * d + col:b * d + col + tn]) * _dot(y[...], w[s, :, c:c + tn])
                acc = term if acc is None else acc + term
            o_ref[:, col:col + tn] = acc.astype(o_ref.dtype)

    return pl.pallas_call(
        body, grid=(t // tm,),
        in_specs=y_specs + w_specs + [pl.BlockSpec((tm, 3 * d), lambda i: (i, 0))],
        out_specs=pl.BlockSpec((tm, d), lambda i: (i, 0)),
        out_shape=jax.ShapeDtypeStruct((t, d), BF16),
        compiler_params=_cp(("parallel",)), name=name,
    )(*ys, *ws, gates)


def _merge_bwd(ys, ws, gates, dmerged, dp_width, *, name, tm=256, tn=512):
    t, d = ys[0].shape[0], ws[0].shape[0] * ws[0].shape[2]
    tm = _blk(t, tm)
    tn, tiles = _merge_tiles(ws, tn)
    y_specs, w_specs = _merge_specs(ys, ws, tm)
    row = pl.BlockSpec((tm, d), lambda i: (i, 0))
    wide = pl.BlockSpec((tm, 3 * d), lambda i: (i, 0))

    def body(ya, yb, yc, wa, wb, wc, g_ref, dm_ref, dua, dub, duc, dp_ref):
        for s, c, col in tiles:
            dm = dm_ref[:, col:col + tn]
            for b, (y, w, du) in enumerate(((ya, wa, dua), (yb, wb, dub), (yc, wc, duc))):
                sg = _sigmoid(g_ref[:, b * d + col:b * d + col + tn])
                u = _dot(y[...], w[s, :, c:c + tn])
                du[:, col:col + tn] = (dm * sg).astype(du.dtype)
                dp_ref[:, b * d + col:b * d + col + tn] = (dm * u * sg * (1.0 - sg)).astype(dp_ref.dtype)

    return pl.pallas_call(
        body, grid=(t // tm,),
        in_specs=y_specs + w_specs + [wide, row],
        out_specs=[row] * 3 + [wide],
        out_shape=[jax.ShapeDtypeStruct((t, d), BF16)] * 3 + [jax.ShapeDtypeStruct((t, dp_width), BF16)],
        compiler_params=_cp(("parallel",)), name=name,
    )(*ys, *ws, gates, dmerged)


def _adamw(w, g, m, v, *, name, tm=256):
    lead = w.ndim - 2
    assert all(s == 1 for s in w.shape[:lead]) and m.shape == w.shape and v.shape == w.shape
    r, c = w.shape[lead:]
    assert g.shape == (r, c)
    tm = _blk(r, tm) if r % 8 == 0 else r
    tc = c if tm * c * 4 <= ROW_BLOCK_BYTES else _blk(c, 256)
    ncb = c // tc
    bc1 = 1.0 - ADAM_B1 ** ADAM_STEP
    bc2 = 1.0 - ADAM_B2 ** ADAM_STEP

    def body(w_ref, g_ref, m_ref, v_ref, go_ref, d_ref, nm_ref, nv_ref):
        gv = g_ref[...]
        go_ref[...] = gv
        nm = ADAM_B1 * m_ref[...] + (1.0 - ADAM_B1) * gv
        nv = ADAM_B2 * v_ref[...] + (1.0 - ADAM_B2) * (gv * gv)
        d_ref[...] = -ADAM_LR * ((nm / bc1) / (jnp.sqrt(nv / bc2) + ADAM_EPS) + ADAM_WD * w_ref[...])
        nm_ref[...] = nm
        nv_ref[...] = nv

    spec = pl.BlockSpec((None,) * lead + (tm, tc), lambda i: (0,) * lead + (i // ncb, i % ncb))
    g_spec = pl.BlockSpec((tm, tc), lambda i: (i // ncb, i % ncb))
    return pl.pallas_call(
        body, grid=(r // tm * ncb,), in_specs=[spec, g_spec, spec, spec], out_specs=[spec] * 4,
        out_shape=[jax.ShapeDtypeStruct(w.shape, F32)] * 4,
        compiler_params=_cp(("parallel",)), name=name,
    )(w, g, m, v)


HALO = 8


def _shift_down(cur, prev, j):
    if j == 0:
        return cur
    y = pltpu.roll(cur, j, 0)
    row = lax.broadcasted_iota(jnp.int32, (HALO, cur.shape[1]), 0)
    top = jnp.where(row < j, pltpu.roll(prev, j, 0), y[:HALO])
    return jnp.concatenate([top, y[HALO:]], axis=0)


def _shift_up(cur, nxt, j):
    if j == 0:
        return cur
    tm = cur.shape[0]
    y = pltpu.roll(cur, tm - j, 0)
    row = lax.broadcasted_iota(jnp.int32, (HALO, cur.shape[1]), 0)
    bot = jnp.where(row >= HALO - j, pltpu.roll(nxt, HALO - j, 0), y[tm - HALO:])
    return jnp.concatenate([y[:tm - HALO], bot], axis=0)


def _softplus(x):
    return jnp.maximum(x, 0.0) + jnp.log(1.0 + jnp.exp(-jnp.abs(x)))


def _gdn_pre_fwd(qkvb, conv_w, ab, alog_pad, dt_pad, *, name, ab_blk=0, tm=256):
    t, cw = qkvb.shape
    hd, nh, ck = GDN_HEAD_DIM, GDN_HEADS, GDN_CHUNK
    gw = nh * hd
    tm = _blk(t, tm)
    hb = tm // HALO

    def body(x_ref, xp_ref, w_ref, ab_ref, al_ref, dt_ref, xc_ref, qkvn_ref, aux_ref):
        i = pl.program_id(0)
        cur = x_ref[...]
        prev = jnp.where(i > 0, xp_ref[...], 0.0)
        xc = None
        for tap in range(GDN_CONV):
            term = w_ref[tap:tap + 1, :] * _shift_down(cur, prev, GDN_CONV - 1 - tap)
            xc = term if xc is None else xc + term
        xc_ref[...] = xc
        s = xc * _sigmoid(xc)
        for h in range(2 * nh):
            xh = s[:, h * hd:(h + 1) * hd]
            r = lax.rsqrt(jnp.sum(xh * xh, axis=-1, keepdims=True) + L2_EPS)
            scale = hd ** -0.5 if h < nh else 1.0
            qkvn_ref[:, h * hd:(h + 1) * hd] = xh * (r * scale)
        qkvn_ref[:, 2 * gw:] = s[:, 2 * gw:]
        abv = ab_ref[...]
        lane = lax.broadcasted_iota(jnp.int32, abv.shape, 1)
        g = jnp.where(lane < nh, -jnp.exp(al_ref[...]) * _softplus(abv + dt_ref[...]), 0.0)
        beta = jnp.where((lane >= nh) & (lane < 2 * nh), _sigmoid(abv), 0.0)
        ii = lax.broadcasted_iota(jnp.int32, (tm, tm), 0)
        jj = lax.broadcasted_iota(jnp.int32, (tm, tm), 1)
        tri = jnp.where((ii >= jj) & ((ii ^ jj) < ck), 1.0, 0.0)
        gcum = _dot(tri, g, precision=HI)
        aux_ref[...] = g + beta + pltpu.roll(gcum, 2 * nh, 1)

    row = lambda c: pl.BlockSpec((tm, c), lambda i: (i, 0))
    vec = lambda r, c: pl.BlockSpec((r, c), lambda i: (0, 0))
    return pl.pallas_call(
        body, grid=(t // tm,),
        in_specs=[row(cw), pl.BlockSpec((HALO, cw), lambda i: (jnp.maximum(i * hb - 1, 0), 0)), vec(GDN_CONV, cw),
                  pl.BlockSpec((tm, LANES), lambda i: (i, ab_blk)), vec(1, LANES), vec(1, LANES)],
        out_specs=[row(cw), row(cw), row(LANES)],
        out_shape=[jax.ShapeDtypeStruct((t, cw), F32), jax.ShapeDtypeStruct((t, cw), F32),
                   jax.ShapeDtypeStruct((t, LANES), F32)],
        compiler_params=_cp(("parallel",)), name=name,
    )(qkvb, qkvb, conv_w, ab, alog_pad, dt_pad)


GDN_STEP_CHUNKS = 8
GDN_ILP_CHUNKS = 4
GDN_ILP_CHUNKS_BWD = 4


def _bdot(a, b, dims=NN):
    return _dot(a.astype(BF16), b.astype(BF16), dims)


def _split_bf16(x):
    hi = x.astype(BF16)
    return hi, (x - hi.astype(F32)).astype(BF16)


def _dot3(a, b, dims=NN):
    ah, al = _split_bf16(a)
    bh, bl = _split_bf16(b)
    return _dot(ah, bh, dims) + (_dot(ah, bl, dims) + _dot(al, bh, dims))


def _dot3_many(lhs, rhs, dims=NN):
    sa = [_split_bf16(a) for a in lhs]
    sb = [_split_bf16(b) for b in rhs]
    hh = [_dot(a[0], b[0], dims) for a, b in zip(sa, sb)]
    hl = [_dot(a[0], b[1], dims) for a, b in zip(sa, sb)]
    lh = [_dot(a[1], b[0], dims) for a, b in zip(sa, sb)]
    return [x + (y + z) for x, y, z in zip(hh, hl, lh)]


def _gdn_local(chains, with_inverse):
    ck = GDN_CHUNK
    ii = lax.broadcasted_iota(jnp.int32, (ck, ck), 0)
    jj = lax.broadcasted_iota(jnp.int32, (ck, ck), 1)
    lower, strict = ii >= jj, ii > jj
    dmat = [jnp.exp(jnp.where(lower, gc - gc_row, -jnp.inf)) for _, _, _, gc, gc_row in chains]
    kk = [_bdot(k, k, NT) for _, k, _, _, _ in chains]
    qk = [_bdot(q, k, NT) for q, k, _, _, _ in chains]
    tinv = [None] * len(chains)
    if with_inverse:
        lmat = [jnp.where(strict, c[2] * kk_i * d_i, 0.0) for c, kk_i, d_i in zip(chains, kk, dmat)]
        eye = jnp.where(ii == jj, 1.0, 0.0)
        tinv = [eye - l_i for l_i in lmat]
        pw = lmat
        for _ in range(int(math.log2(ck)) - 1):
            pw = _dot3_many(pw, pw)
            tinv = [t_i + d_i for t_i, d_i in zip(tinv, _dot3_many(tinv, pw))]
    out = []
    for (q, k, b, gc, gc_row), dmat_i, kk_i, qk_i, tinv_i in zip(chains, dmat, kk, qk, tinv):
        gl = gc[ck - 1:ck, :]
        out.append(dict(lower=lower, strict=strict, dmat=dmat_i, kk=kk_i, tinv=tinv_i, gam=jnp.exp(gc), qk=qk_i,
                        mm=qk_i * dmat_i, kdec=jnp.exp(gl - gc)))
    return out


def _gdn_head_cols(h):
    return slice(h * GDN_HEAD_DIM, (h + 1) * GDN_HEAD_DIM)


def _gdn_chunk_inputs(x_ref, aux_ref, auxt_ref, g, h):
    nh, ck = GDN_HEADS, GDN_CHUNK
    gw = nh * GDN_HEAD_DIM
    rows = slice(g * ck, (g + 1) * ck)
    cols = _gdn_head_cols(h)
    q = x_ref[rows, cols]
    k = x_ref[rows, gw + cols.start:gw + cols.stop]
    v = x_ref[rows, 2 * gw + cols.start:2 * gw + cols.stop]
    b = aux_ref[rows, nh + h:nh + h + 1]
    gc = aux_ref[rows, 2 * nh + h:2 * nh + h + 1]
    gc_row = auxt_ref[g, 2 * nh + h:2 * nh + h + 1, :]
    return q, k, v, b, gc, gc_row


def _gdn_specs(t, widths, *, reverse=False, step_chunks=None):
    rows = (step_chunks or GDN_STEP_CHUNKS) * GDN_CHUNK
    nsteps = t // rows
    idx = (lambda i: (nsteps - 1 - i, 0)) if reverse else (lambda i: (i, 0))
    return [pl.BlockSpec((rows, w), idx) for w in widths]


def _gdn_local_fwd(qkvn, aux, aux_t, *, name, exchange=None):
    t = qkvn.shape[0]
    hd, nh, ck, gs = GDN_HEAD_DIM, GDN_HEADS, GDN_CHUNK, GDN_STEP_CHUNKS
    gw = nh * hd
    host = _ExchangeHost(exchange)
    assert not (exchange and exchange.aliases)
    grid = (t // (gs * ck),)

    def body(*refs):
        x_ref, aux_ref, auxt_ref = refs[:3]
        u_ref, w_ref, qd_ref, kd_ref, mm_ref, tinv_ref = refs[3 + host.n_in:9 + host.n_in]
        host.start(refs, 3, 9 + host.n_in, grid)
        for g0 in range(0, gs, GDN_ILP_CHUNKS):
            where = [(g, h) for g in range(g0, g0 + GDN_ILP_CHUNKS) for h in range(nh)]
            ins = [_gdn_chunk_inputs(x_ref, aux_ref, auxt_ref, g, h) for g, h in where]
            lcs = _gdn_local([(q, k, b, gc, gc_row) for q, k, _, b, gc, gc_row in ins], True)
            tinvs = [lc["tinv"] for lc in lcs]
            us = _dot3_many(tinvs, [b * v for _, _, v, b, _, _ in ins])
            ws = _dot3_many(tinvs, [(b * lc["gam"]) * k for (_, k, _, b, _, _), lc in zip(ins, lcs)])
            for i, ((g, h), (q, k, _, _, _, _), lc) in enumerate(zip(where, ins, lcs)):
                rows, cols = slice(g * ck, (g + 1) * ck), _gdn_head_cols(h)
                u_ref[rows, cols] = us[i]
                w_ref[rows, cols] = ws[i].astype(w_ref.dtype)
                qd_ref[rows, cols] = (lc["gam"] * q).astype(qd_ref.dtype)
                kd_ref[rows, cols] = (lc["kdec"] * k).astype(kd_ref.dtype)
            for g in range(g0, g0 + GDN_ILP_CHUNKS):
                rows = slice(g * ck, (g + 1) * ck)
                mine = [lc for (gg, _), lc in zip(where, lcs) if gg == g]
                mm_ref[rows, :] = jnp.concatenate([lc["mm"] for lc in mine], axis=1).astype(mm_ref.dtype)
                tinv_ref[rows, :] = jnp.concatenate([lc["tinv"] for lc in mine], axis=1)
        host.wait(refs, 3, 9 + host.n_in, grid)

    sq = nh * ck
    outs = pl.pallas_call(
        body, grid=grid,
        in_specs=_gdn_specs(t, (3 * gw, LANES)) + [pl.BlockSpec((gs, 16, ck), lambda i: (i, 0, 0))] + host.in_specs,
        out_specs=_gdn_specs(t, (gw, gw, gw, gw, sq, sq)) + host.out_specs,
        out_shape=[jax.ShapeDtypeStruct((t, gw), F32)] + [jax.ShapeDtypeStruct((t, gw), BF16)] * 3
        + [jax.ShapeDtypeStruct((t, sq), BF16), jax.ShapeDtypeStruct((t, sq), F32)] + host.out_shapes,
        scratch_shapes=host.scratch,
        compiler_params=_cp(host.semantics(("parallel",))), name=name,
    )(qkvn, aux, aux_t, *host.ins)
    return (*outs[:6], list(outs[6:])) if exchange is not None else outs


def _gdn_seq_fwd(u, w, qd, kd, mm, aux, *, name):
    t = u.shape[0]
    hd, nh, ck, gs = GDN_HEAD_DIM, GDN_HEADS, GDN_CHUNK, GDN_STEP_CHUNKS
    gw = nh * hd
    sq = nh * ck

    def body(u_ref, w_ref, qd_ref, kd_ref, mm_ref, aux_ref, o_ref, vn_ref, sall_ref, s_ref):
        @pl.when(pl.program_id(0) == 0)
        def _():
            s_ref[...] = jnp.zeros_like(s_ref)

        heads = range(nh)
        hcols = [_gdn_head_cols(h) for h in heads]
        sts = [s_ref[h] for h in heads]
        for g in range(gs):
            rows = slice(g * ck, (g + 1) * ck)
            last = (g + 1) * ck - 1
            for h in heads:
                sall_ref[g, h] = sts[h]
            stbs = [st.astype(BF16) for st in sts]
            w_s = [_dot(w_ref[rows, c], stb) for c, stb in zip(hcols, stbs)]
            q_s = [_dot(qd_ref[rows, c], stb) for c, stb in zip(hcols, stbs)]
            vnbs = [(u_ref[rows, c] - ws).astype(BF16) for c, ws in zip(hcols, w_s)]
            m_v = [_dot(mm_ref[rows, h * ck:(h + 1) * ck], vnbs[h]) for h in heads]
            k_v = [_dot(kd_ref[rows, c], vnb, TN) for c, vnb in zip(hcols, vnbs)]
            for h, c in zip(heads, hcols):
                vn_ref[rows, c] = vnbs[h]
                o_ref[rows, c] = q_s[h] + m_v[h]
            gam_c = [jnp.exp(aux_ref[last:last + 1, 2 * nh + h:2 * nh + h + 1]) for h in heads]
            sts = [gam_c[h] * sts[h] + k_v[h] for h in heads]
        for h in heads:
            s_ref[h] = sts[h]

    return pl.pallas_call(
        body, grid=(t // (gs * ck),),
        in_specs=_gdn_specs(t, (gw, gw, gw, gw, sq, LANES)),
        out_specs=_gdn_specs(t, (gw, gw)) + [pl.BlockSpec((gs, nh, hd, hd), lambda i: (i, 0, 0, 0))],
        out_shape=[jax.ShapeDtypeStruct((t, gw), F32), jax.ShapeDtypeStruct((t, gw), BF16),
                   jax.ShapeDtypeStruct((t // ck, nh, hd, hd), F32)],
        scratch_shapes=[pltpu.VMEM((nh, hd, hd), F32)],
        compiler_params=_cp(("arbitrary",)), name=name,
    )(u, w, qd, kd, mm, aux)


def _gdn_seq_bwd(do, w, qd, kd, mm, vn, s_all, aux, *, name):
    t = do.shape[0]
    hd, nh, ck, gs = GDN_HEAD_DIM, GDN_HEADS, GDN_CHUNK, GDN_STEP_CHUNKS
    gw = nh * hd
    sq = nh * ck
    nsteps = t // (gs * ck)

    def body(do_ref, w_ref, qd_ref, kd_ref, mm_ref, vn_ref, sall_ref, aux_ref, dvn_ref, dqd_ref, dkd_ref, dw_ref,
             dlast_ref, ds_ref):
        @pl.when(pl.program_id(0) == 0)
        def _():
            ds_ref[...] = jnp.zeros_like(ds_ref)

        lane = lax.broadcasted_iota(jnp.int32, (ck, LANES), 1)
        rowi = lax.broadcasted_iota(jnp.int32, (ck, LANES), 0)
        heads = range(nh)
        hcols = [_gdn_head_cols(h) for h in heads]
        dsns = [ds_ref[h] for h in heads]
        for g in reversed(range(gs)):
            rows = slice(g * ck, (g + 1) * ck)
            last = (g + 1) * ck - 1
            sts = [sall_ref[g, h] for h in heads]
            stbs = [st.astype(BF16) for st in sts]
            dsbs = [dsn.astype(BF16) for dsn in dsns]
            dobs = [do_ref[rows, c].astype(BF16) for c in hcols]
            dvns = [_dot(mm_ref[rows, h * ck:(h + 1) * ck], dobs[h], TN) + _dot(kd_ref[rows, hcols[h]], dsbs[h])
                    for h in heads]
            dqds = [_dot(dob, stb, NT) for dob, stb in zip(dobs, stbs)]
            dkds = [_dot(vn_ref[rows, c], dsb, NT) for c, dsb in zip(hcols, dsbs)]
            q_o = [_dot(qd_ref[rows, c], dob, TN) for c, dob in zip(hcols, dobs)]
            dvbs = [dvn.astype(BF16) for dvn in dvns]
            dws = [_dot(dvb, stb, NT) for dvb, stb in zip(dvbs, stbs)]
            w_v = [_dot(w_ref[rows, c], dvb, TN) for c, dvb in zip(hcols, dvbs)]
            gam_c = [jnp.exp(aux_ref[last:last + 1, 2 * nh + h:2 * nh + h + 1]) for h in heads]
            dlast = jnp.zeros((ck, LANES), F32)
            for h, c in zip(heads, hcols):
                dvn_ref[rows, c] = dvns[h]
                dqd_ref[rows, c] = dqds[h]
                dkd_ref[rows, c] = dkds[h]
                dw_ref[rows, c] = -dws[h]
                dgam_c = jnp.sum(jnp.sum(dsns[h] * sts[h], axis=1, keepdims=True), axis=0, keepdims=True)
                dlast = dlast + jnp.where((rowi == ck - 1) & (lane == h), gam_c[h] * dgam_c, 0.0)
            dlast_ref[rows, :] = dlast
            dsns = [q_o[h] + gam_c[h] * dsns[h] - w_v[h] for h in heads]
        for h in heads:
            ds_ref[h] = dsns[h]

    return pl.pallas_call(
        body, grid=(nsteps,),
        in_specs=_gdn_specs(t, (gw, gw, gw, gw, sq, gw), reverse=True)
        + [pl.BlockSpec((gs, nh, hd, hd), lambda i: (nsteps - 1 - i, 0, 0, 0))] + _gdn_specs(t, (LANES,), reverse=True),
        out_specs=_gdn_specs(t, (gw, gw, gw, gw, LANES), reverse=True),
        out_shape=[jax.ShapeDtypeStruct((t, gw), F32)] * 4 + [jax.ShapeDtypeStruct((t, LANES), F32)],
        scratch_shapes=[pltpu.VMEM((nh, hd, hd), F32)],
        compiler_params=_cp(("arbitrary",)), name=name,
    )(do, w, qd, kd, mm, vn, s_all, aux)


def _gdn_local_bwd(qkvn, aux, aux_t, tinv, u, w, vn, do, dvn, dqd, dkd, dw, dlast, *, name):
    t = qkvn.shape[0]
    hd, nh, ck, gs = GDN_HEAD_DIM, GDN_HEADS, GDN_CHUNK, GDN_STEP_CHUNKS
    gw = nh * hd
    sq = nh * ck

    def body(x_ref, aux_ref, auxt_ref, tinv_ref, u_ref, w_ref, vn_ref, do_ref, dvn_ref, dqd_ref, dkd_ref, dw_ref,
             dlast_ref, dx_ref, daux_ref):
        lane = lax.broadcasted_iota(jnp.int32, (ck, LANES), 1)
        ones = jnp.ones((ck, LANES), F32)
        ii = lax.broadcasted_iota(jnp.int32, (ck, ck), 0)
        jj = lax.broadcasted_iota(jnp.int32, (ck, ck), 1)
        suffix = jnp.where(jj >= ii, 1.0, 0.0)
        for g0 in range(0, gs, GDN_ILP_CHUNKS_BWD):
            where = [(g, h) for g in range(g0, g0 + GDN_ILP_CHUNKS_BWD) for h in range(nh)]
            at = [(slice(g * ck, (g + 1) * ck), _gdn_head_cols(h)) for g, h in where]
            ins = [_gdn_chunk_inputs(x_ref, aux_ref, auxt_ref, g, h) for g, h in where]
            lcs = _gdn_local([(q, k, b, gc, gc_row) for q, k, _, b, gc, gc_row in ins], False)
            tinvs = [tinv_ref[slice(g * ck, (g + 1) * ck), h * ck:(h + 1) * ck] for g, h in where]
            dms = [jnp.where(lc["lower"], _bdot(do_ref[r, c], vn_ref[r, c], NT), 0.0) for lc, (r, c) in zip(lcs, at)]
            drvs = _dot3_many(tinvs, [dvn_ref[r, c] for r, c in at], TN)
            drks = _dot3_many(tinvs, [dw_ref[r, c] for r, c in at], TN)
            das = [jnp.where(lc["strict"], -(_bdot(drv, u_ref[r, c], NT) + _bdot(drk, w_ref[r, c], NT)), 0.0)
                   for lc, (r, c), drv, drk in zip(lcs, at, drvs, drks)]
            f_mats = [da * (i[3] * lc["kk"]) * lc["dmat"] + dm * lc["qk"] * lc["dmat"]
                      for i, lc, da, dm in zip(ins, lcs, das, dms)]
            col_sums = _dot3_many(f_mats, [ones] * len(where), TN)
            dgc_all = {g: dlast_ref[slice(g * ck, (g + 1) * ck), :] for g in range(g0, g0 + GDN_ILP_CHUNKS_BWD)}
            db_all = {g: jnp.zeros((ck, LANES), F32) for g in range(g0, g0 + GDN_ILP_CHUNKS_BWD)}
            e_mats = [da * lc["dmat"] * i[3] for i, lc, da in zip(ins, lcs, das)]
            dmds = [dm * lc["dmat"] for lc, dm in zip(lcs, dms)]
            dq_mm = [_bdot(dmd, i[1]) for i, dmd in zip(ins, dmds)]
            dk_mm = [_bdot(e, i[1]) + _bdot(e, i[1], TN) + _bdot(dmd, i[0], TN) for i, e, dmd in zip(ins, e_mats, dmds)]
            for n, ((g, h), (q, k, v, b, _, _), lc, (rows, cols)) in enumerate(zip(where, ins, lcs, at)):
                dmat, kk, gam, kdec = (lc[key] for key in ("dmat", "kk", "gam", "kdec"))
                drv, drk, da = drvs[n], drks[n], das[n]
                dqd_h, dkd_h = dqd_ref[rows, cols], dkd_ref[rows, cols]
                rs_rk = jnp.sum(drk * k, axis=-1, keepdims=True)
                db = (jnp.sum(drv * v, axis=-1, keepdims=True) + gam * rs_rk
                      + jnp.sum(da * kk * dmat, axis=-1, keepdims=True))
                dx_ref[rows, cols] = dq_mm[n] + gam * dqd_h
                dx_ref[rows, gw + cols.start:gw + cols.stop] = (b * gam) * drk + dk_mm[n] + kdec * dkd_h
                dx_ref[rows, 2 * gw + cols.start:2 * gw + cols.stop] = b * drv
                e_vec = jnp.sum(dkd_h * (kdec * k), axis=-1, keepdims=True)
                dgc = (b * gam * rs_rk + gam * jnp.sum(dqd_h * q, axis=-1, keepdims=True)
                       + jnp.sum(f_mats[n], axis=-1, keepdims=True) - col_sums[n][:, 0:1] - e_vec)
                is_last = lax.broadcasted_iota(jnp.int32, (ck, 1), 0) == ck - 1
                dgc = dgc + jnp.where(is_last, jnp.sum(e_vec, axis=0, keepdims=True), 0.0)
                dgc_all[g] = dgc_all[g] + jnp.where(lane == h, dgc, 0.0)
                db_all[g] = db_all[g] + jnp.where(lane == nh + h, db, 0.0)
            for g in dgc_all:
                daux_ref[slice(g * ck, (g + 1) * ck), :] = _dot3(suffix, dgc_all[g]) + db_all[g]

    return pl.pallas_call(
        body, grid=(t // (gs * ck),),
        in_specs=_gdn_specs(t, (3 * gw, LANES)) + [pl.BlockSpec((gs, 16, ck), lambda i: (i, 0, 0))]
        + _gdn_specs(t, (sq, gw, gw, gw, gw, gw, gw, gw, gw, LANES)),
        out_specs=_gdn_specs(t, (3 * gw, LANES)),
        out_shape=[jax.ShapeDtypeStruct((t, 3 * gw), F32), jax.ShapeDtypeStruct((t, LANES), F32)],
        compiler_params=_cp(("parallel",)), name=name,
    )(qkvn, aux, aux_t, tinv, u, w, vn, do, dvn, dqd, dkd, dw, dlast)


def _gdn_pre_bwd1(xc, dqkvn, daux, ab, alog_pad, dt_pad, dkv, dp, dp_col, *, name, ab_blk=0, tm=256):
    t, cw = xc.shape
    hd, nh = GDN_HEAD_DIM, GDN_HEADS
    gw = nh * hd
    tm = _blk(t, tm)

    kvw = dkv.shape[1]
    seg = kvw + AB_PAD
    assert dp_col % seg == 0

    def body(xc_ref, dy_ref, daux_ref, ab_ref, al_ref, dt_ref, dkv_ref, _, dxc_ref, dab_ref, dal_ref, ddt_ref):
        i = pl.program_id(0)
        xc = xc_ref[...]
        sg = _sigmoid(xc)
        s = xc * sg
        dsilu = sg * (1.0 + xc * (1.0 - sg))
        for h in range(2 * nh):
            xh = s[:, h * hd:(h + 1) * hd]
            scale = hd ** -0.5 if h < nh else 1.0
            dyh = dy_ref[:, h * hd:(h + 1) * hd] * scale
            r = lax.rsqrt(jnp.sum(xh * xh, axis=-1, keepdims=True) + L2_EPS)
            dxh = r * dyh - xh * (r * r * r) * jnp.sum(dyh * xh, axis=-1, keepdims=True)
            dxc_ref[:, h * hd:(h + 1) * hd] = dxh * dsilu[:, h * hd:(h + 1) * hd]
        dxc_ref[:, 2 * gw:] = dy_ref[:, 2 * gw:] * dsilu[:, 2 * gw:]
        abv = ab_ref[...]
        dauxv = daux_ref[...]
        lane = lax.broadcasted_iota(jnp.int32, abv.shape, 1)
        is_a = lane < nh
        is_b = (lane >= nh) & (lane < 2 * nh)
        pre = abv + dt_ref[...]
        neg_ea = -jnp.exp(al_ref[...])
        d_a = jnp.where(is_a, dauxv * neg_ea * _sigmoid(pre), 0.0)
        beta = _sigmoid(abv)
        d_b = jnp.where(is_b, dauxv * beta * (1.0 - beta), 0.0)
        dab_ref[:, :kvw] = dkv_ref[...]
        dab_ref[:, kvw:kvw + LANES] = (d_a + d_b).astype(dab_ref.dtype)
        dab_ref[:, kvw + LANES:] = jnp.zeros((tm, AB_PAD - LANES), dab_ref.dtype)
        dal = jnp.sum(jnp.where(is_a, dauxv * neg_ea * _softplus(pre), 0.0), axis=0, keepdims=True)
        ddt = jnp.sum(d_a, axis=0, keepdims=True)

        @pl.when(i == 0)
        def _():
            dal_ref[...] = dal
            ddt_ref[...] = ddt

        @pl.when(i > 0)
        def _():
            dal_ref[...] += dal
            ddt_ref[...] += ddt

    row = lambda c: pl.BlockSpec((tm, c), lambda i: (i, 0))
    vec = pl.BlockSpec((1, LANES), lambda i: (0, 0))
    return pl.pallas_call(
        body, grid=(t // tm,),
        in_specs=[row(cw), row(cw), row(LANES), pl.BlockSpec((tm, LANES), lambda i: (i, ab_blk)), vec, vec, row(kvw),
                  ANY_SPEC],
        out_specs=[row(cw), pl.BlockSpec((tm, seg), lambda i: (i, dp_col // seg)), vec, vec],
        out_shape=[jax.ShapeDtypeStruct((t, cw), F32), jax.ShapeDtypeStruct(dp.shape, dp.dtype),
                   jax.ShapeDtypeStruct((1, LANES), F32), jax.ShapeDtypeStruct((1, LANES), F32)],
        input_output_aliases={7: 1}, compiler_params=_cp(("arbitrary",)), name=name,
    )(xc, dqkvn, daux, ab, alog_pad, dt_pad, dkv, dp)


def _gdn_pre_bwd2(dxc, qkvb, conv_w, dp, dp_col, *, name, tm=512):
    t, cw = dxc.shape
    tm = _blk(t, tm)
    hb = tm // HALO
    nblk = t // tm
    cg = GDN_HEADS * GDN_HEAD_DIM
    assert cw % cg == 0 and dp_col % cg == 0
    col0 = dp_col // cg

    def body(d_ref, dn_ref, x_ref, xp_ref, w_ref, _, dx_ref, dw_ref):
        i = pl.program_id(1)
        dcur = d_ref[...]
        dnxt = jnp.where(i < nblk - 1, dn_ref[...], 0.0)
        cur = x_ref[...]
        prev = jnp.where(i > 0, xp_ref[...], 0.0)
        dx = None
        dws = []
        for tap in range(GDN_CONV):
            j = GDN_CONV - 1 - tap
            term = w_ref[tap:tap + 1, :] * _shift_up(dcur, dnxt, j)
            dx = term if dx is None else dx + term
            dws.append(jnp.sum(dcur * _shift_down(cur, prev, j), axis=0, keepdims=True))
        dx_ref[...] = dx.astype(dx_ref.dtype)
        dw = jnp.concatenate(dws, axis=0)

        @pl.when(i == 0)
        def _():
            dw_ref[...] = dw

        @pl.when(i > 0)
        def _():
            dw_ref[...] += dw

    row = pl.BlockSpec((tm, cg), lambda c, i: (i, c))
    wsp = pl.BlockSpec((GDN_CONV, cg), lambda c, i: (0, c))
    return pl.pallas_call(
        body, grid=(cw // cg, nblk),
        in_specs=[row, pl.BlockSpec((HALO, cg), lambda c, i: (jnp.minimum((i + 1) * hb, t // HALO - 1), c)),
                  row, pl.BlockSpec((HALO, cg), lambda c, i: (jnp.maximum(i * hb - 1, 0), c)), wsp, ANY_SPEC],
        out_specs=[pl.BlockSpec((tm, cg), lambda c, i: (i, col0 + c)), wsp],
        out_shape=[jax.ShapeDtypeStruct(dp.shape, dp.dtype), jax.ShapeDtypeStruct((GDN_CONV, cw), F32)],
        input_output_aliases={5: 0}, compiler_params=_cp(("arbitrary", "arbitrary")), name=name,
    )(dxc, dxc, qkvb, qkvb, conv_w, dp)


def _gdn_post_fwd(o, z, norm_w, *, name, tm=512):
    t, gw = o.shape
    hd, nh = GDN_HEAD_DIM, GDN_HEADS
    tm = _blk(t, tm)

    def body(o_ref, z_ref, w_ref, y_ref):
        zv = z_ref[...]
        sz = zv * _sigmoid(zv)
        for h in range(nh):
            oh = o_ref[:, h * hd:(h + 1) * hd]
            r = lax.rsqrt(jnp.mean(oh * oh, axis=-1, keepdims=True) + RMS_EPS)
            y_ref[:, h * hd:(h + 1) * hd] = (oh * r * w_ref[...] * sz[:, h * hd:(h + 1) * hd]).astype(y_ref.dtype)

    row = pl.BlockSpec((tm, gw), lambda i: (i, 0))
    return pl.pallas_call(
        body, grid=(t // tm,), in_specs=[row, row, pl.BlockSpec((1, hd), lambda i: (0, 0))], out_specs=row,
        out_shape=jax.ShapeDtypeStruct((t, gw), BF16), compiler_params=_cp(("parallel",)), name=name,
    )(o, z, norm_w)


def _gdn_post_bwd(dy, o, z, norm_w, dp, dp_col, *, name, tm=512):
    t, gw = o.shape
    hd, nh = GDN_HEAD_DIM, GDN_HEADS
    tm = _blk(t, tm)

    def body(dy_ref, o_ref, z_ref, w_ref, _, do_ref, dz_ref, dw_ref):
        i = pl.program_id(0)
        zv = z_ref[...]
        sg = _sigmoid(zv)
        sz = zv * sg
        dsz = sg * (1.0 + zv * (1.0 - sg))
        dw = None
        for h in range(nh):
            sl = slice(h * hd, (h + 1) * hd)
            oh = o_ref[:, sl]
            dyh = dy_ref[:, sl].astype(F32)
            r = lax.rsqrt(jnp.mean(oh * oh, axis=-1, keepdims=True) + RMS_EPS)
            xh = oh * r
            dz_ref[:, sl] = (dyh * xh * w_ref[...] * dsz[:, sl]).astype(dz_ref.dtype)
            dn = dyh * sz[:, sl]
            dxh = dn * w_ref[...]
            do_ref[:, sl] = r * (dxh - xh * jnp.mean(dxh * xh, axis=-1, keepdims=True))
            part = jnp.sum(dn * xh, axis=0, keepdims=True)
            dw = part if dw is None else dw + part

        @pl.when(i == 0)
        def _():
            dw_ref[...] = dw

        @pl.when(i > 0)
        def _():
            dw_ref[...] += dw

    row = pl.BlockSpec((tm, gw), lambda i: (i, 0))
    vec = pl.BlockSpec((1, hd), lambda i: (0, 0))
    return pl.pallas_call(
        body, grid=(t // tm,), in_specs=[row, row, row, vec, ANY_SPEC],
        out_specs=[row, pl.BlockSpec((tm, gw), lambda i: (i, dp_col // gw)), vec],
        out_shape=[jax.ShapeDtypeStruct((t, gw), F32), jax.ShapeDtypeStruct(dp.shape, dp.dtype),
                   jax.ShapeDtypeStruct((1, hd), F32)],
        input_output_aliases={4: 1}, compiler_params=_cp(("arbitrary",)), name=name,
    )(dy, o, z, norm_w, dp)


IN_NAMES = ("q_a", "kv_a", "qkv_b", "ab", "z", "q_c", "gates")
CAT_NAMES = ("gates", "q_a", "qkv_b", "z", "q_c", "kv_a", "ab")
AB_PAD = 256


def _in_widths(d):
    gw = GDN_HEADS * GDN_HEAD_DIM
    return dict(q_a=SWA_Q_HEADS * SWA_HEAD_DIM, kv_a=2 * SWA_KV_HEADS * SWA_HEAD_DIM, qkv_b=3 * gw, ab=2 * GDN_HEADS,
                z=gw, q_c=XA_HEADS * XA_HEAD_DIM, gates=3 * d)


def _ranges(names, widths):
    out, start = {}, 0
    for k in names:
        out[k] = (start, widths[k])
        start += widths[k]
    return out, start


def _cat_ranges(d):
    widths = dict(_in_widths(d), ab=AB_PAD)
    return _ranges(CAT_NAMES, widths)


def _to_cat(shards, *, name="to_cat", tm=256):
    ns, d, n = shards.shape
    src, _ = _ranges(IN_NAMES, _in_widths(d))
    _, cat_w = _cat_ranges(d)
    pieces = []
    for k in CAT_NAMES:
        lo, hi = src[k][0], src[k][0] + src[k][1]
        for s in range(ns):
            a, b = max(lo, s * n), min(hi, (s + 1) * n)
            if a < b:
                pieces.append((s, a - s * n, b - s * n))
    tm = _blk(d, tm)

    def body(s_ref, o_ref):
        cols = [s_ref[s, :, a:b] for s, a, b in pieces]
        cols.append(jnp.zeros((tm, AB_PAD - src["ab"][1]), o_ref.dtype))
        o_ref[...] = jnp.concatenate(cols, axis=1)

    return pl.pallas_call(
        body, grid=(d // tm,),
        in_specs=[pl.BlockSpec((ns, tm, n), lambda i: (0, i, 0))],
        out_specs=pl.BlockSpec((tm, cat_w), lambda i: (i, 0)),
        out_shape=jax.ShapeDtypeStruct((d, cat_w), shards.dtype),
        compiler_params=_cp(("parallel",)), name=name,
    )(shards)


def _from_cat(w_cat, *, name="from_cat", tm=256):
    d, cat_w = w_cat.shape
    src, total = _ranges(IN_NAMES, _in_widths(d))
    cat, _ = _cat_ranges(d)
    n = total // N_SHARDS
    pieces = []
    for s in range(N_SHARDS):
        pieces.append([])
        for k in IN_NAMES:
            a, b = max(s * n, src[k][0]), min((s + 1) * n, src[k][0] + src[k][1])
            if a < b:
                pieces[s].append((cat[k][0] + a - src[k][0], cat[k][0] + b - src[k][0]))
    tm = _blk(d, tm)

    def body(c_ref, o_ref):
        for s in range(N_SHARDS):
            o_ref[s] = jnp.concatenate([c_ref[:, a:b] for a, b in pieces[s]], axis=1)

    return pl.pallas_call(
        body, grid=(d // tm,),
        in_specs=[pl.BlockSpec((tm, cat_w), lambda i: (i, 0))],
        out_specs=pl.BlockSpec((N_SHARDS, tm, n), lambda i: (0, i, 0)),
        out_shape=jax.ShapeDtypeStruct((N_SHARDS, d, n), w_cat.dtype),
        compiler_params=_cp(("parallel",)), name=name,
    )(w_cat)


def _pad_cols(a, width):
    return jnp.pad(a, ((0, 0), (0, width - a.shape[1])))


def _relu2_epilogue(acc):
    r = jnp.maximum(acc, 0.0)
    return acc, r * r


def _add_epilogue(acc, res):
    return (acc + res,)


def _drelu2_epilogue(acc, u):
    return (acc * (2.0 * jnp.maximum(u.astype(F32), 0.0)),)


def _local_step(x, mem, tgt, wts, small, comm=None):
    t, d = x.shape
    nh = GDN_HEADS
    cat, cat_w = _cat_ranges(d)
    alog_pad = _pad_cols(small["a_log"], LANES)
    dt_pad = _pad_cols(small["dt_bias"], LANES)
    kvw = cat["kv_a"][1]
    assert cat["ab"][0] == cat["kv_a"][0] + kvw
    ab_blk = kvw // LANES

    if comm is None:
        n = _rms_fwd(x, small["g_mix"], name="rms_mix")
        w_cat = wts["w_cat"]
    else:
        n, landed = _rms_fwd(x, small["g_mix"], name="rms_mix", exchange=comm.gather_exchange(["w_in"]))
        w_cat = _to_cat(_exchange_call(comm.pass_on(["w_in"], landed), name="ag_w_in_pass")[0])
    assert w_cat.shape == (d, cat_w)
    q_a = _mm(n, w_cat, b_window=cat["q_a"], out_dtypes=(BF16,), name="in_q_a")
    kv_a, ab = _mm(n, w_cat, b_window=(cat["kv_a"][0], kvw + AB_PAD), out_dtypes=(BF16, F32), name="in_kv_ab")
    qkvb = _mm(n, w_cat, b_window=cat["qkv_b"], tn=512, name="in_qkv_b")
    z = _mm(n, w_cat, b_window=cat["z"], name="in_z")
    q_c = _mm(n, w_cat, b_window=cat["q_c"], out_dtypes=(BF16,), name="in_q_c")
    if comm is None:
        gates = _mm(n, w_cat, b_window=cat["gates"], name="in_gates")
        y_a, lse = _swa_fwd(q_a, kv_a, small["sinks"], name="swa_fwd")
    else:
        gates, landed_mlp = _mm(n, w_cat, b_window=cat["gates"], name="in_gates",
                                exchange=comm.gather_exchange(comm.MLP[1:]))
        y_a, lse, landed = _swa_fwd(q_a, kv_a, small["sinks"], name="swa_fwd", exchange=comm.gather_exchange(comm.MLP[:1]))
        landed_mlp = landed + landed_mlp
    xc, qkvn, aux = _gdn_pre_fwd(qkvb, small["conv_w"], ab, alog_pad, dt_pad, ab_blk=ab_blk, name="gdn_pre_fwd")
    aux_t = aux[:, :16].reshape(t // GDN_CHUNK, GDN_CHUNK, 16).transpose(0, 2, 1)
    if comm is None:
        gdn_u, gdn_w, gdn_qd, gdn_kd, gdn_mm, gdn_tinv = _gdn_local_fwd(qkvn, aux, aux_t, name="gdn_local_fwd")
    else:
        gdn_u, gdn_w, gdn_qd, gdn_kd, gdn_mm, gdn_tinv, landed_mid = _gdn_local_fwd(
            qkvn, aux, aux_t, name="gdn_local_fwd", exchange=comm.gather_exchange(comm.mid))
        wts = dict(wts, **comm.gathered(comm.mid, landed_mid, "mid"))
    o_b, gdn_vn, s_all = _gdn_seq_fwd(gdn_u, gdn_w, gdn_qd, gdn_kd, gdn_mm, aux, name="gdn_seq_fwd")
    y_b = _gdn_post_fwd(o_b, z, small["gdn_norm_w"], name="gdn_post_fwd")
    nmem = _rms_fwd(mem, small["g_mem"], name="rms_mem")
    mkv = _mm(nmem, wts["w_mem_kv"], out_dtypes=(BF16,), name="mem_kv")
    y_c = _xa_fwd(q_c, mkv, name="xa_fwd")
    ys = (y_a, y_b, y_c)
    w_ups = (wts["w_swa_up"], wts["w_gdn_up"], wts["w_xa_up"])
    merged = _merge_fwd(ys, w_ups, gates, name="merge_fwd")
    if comm is None:
        h1 = _mm(merged, wts["w_out"], extras=(x,), epilogue=_add_epilogue, name="out_proj")
    else:
        h1, whole = _mm(merged, wts["w_out"], extras=(x,), epilogue=_add_epilogue, name="out_proj",
                        exchange=comm.pass_on(comm.MLP, landed_mlp))
        wts = dict(wts, **comm.as_weights(comm.MLP, whole))
    n2 = _rms_fwd(h1, small["g_mlp"], name="rms_mlp")
    u, act = _mm(n2, wts["w_mlp_in"], b_sharded=True, out_dtypes=(BF16, BF16), epilogue=_relu2_epilogue, name="mlp_in")
    h2 = _mm(act, wts["w_mlp_out"], extras=(h1,), epilogue=_add_epilogue, name="mlp_out")
    dh2, dh2_b, dg_final, loss = _final_loss(h2, small["g_final"], tgt, name="final_loss")

    grads = {"g_final": dg_final}
    du = _mm(dh2_b, wts["w_mlp_out"], tb=True, out_dtypes=(BF16,), extras=(u,), epilogue=_drelu2_epilogue, name="d_mlp_act")
    grads["w_mlp_out"] = _mm(act, dh2_b, ta=True, out_dtypes=(BF16,), name="dw_mlp_out")
    grads["w_mlp_in"] = _mm(n2, du, ta=True, out_sharded=True, out_dtypes=(BF16,), name="dw_mlp_in")
    if comm is None:
        dn2 = _mm(du, wts["w_mlp_in"], tb=True, b_sharded=True, name="d_mlp_in")
    else:
        g_mlp = [comm.shard_major(k, grads.pop(k)) for k in comm.MLP]
        dn2, sib_mlp = _mm(du, wts["w_mlp_in"], tb=True, b_sharded=True, name="d_mlp_in", exchange=_sibling_halves(g_mlp))
        s1_mlp = comm.pair_sums(g_mlp, "mlp", sib_mlp)
    dh1, dh1_b, grads["g_mlp"] = _rms_bwd(dn2, h1, small["g_mlp"], dh2, name="rms_mlp_bwd")
    dmerged = _mm(dh1_b, wts["w_out"], tb=True, name="d_out_proj")
    grads["w_out"] = _mm(merged, dh1_b, ta=True, out_dtypes=(BF16,), name="dw_out")
    *dus, dp = _merge_bwd(ys, w_ups, gates, dmerged, cat_w, name="merge_bwd")
    dys = []
    for y, du_i, w_up, key in zip(ys, dus, w_ups, ("w_swa_up", "w_gdn_up", "w_xa_up")):
        dys.append(_mm(du_i, w_up, tb=True, b_sharded=True, out_dtypes=(BF16,), name="d_" + key))
        grads[key] = _mm(y, du_i, ta=True, out_sharded=True, out_dtypes=(BF16,), name="dw_" + key[2:])
    dp, dkv_a, grads["sinks"] = _swa_bwd(q_a, kv_a, small["sinks"], y_a, lse, dys[0], dp, cat["q_a"][0], name="swa_bwd")
    do_b, dp, grads["gdn_norm_w"] = _gdn_post_bwd(dys[1], o_b, z, small["gdn_norm_w"], dp, cat["z"][0],
                                                  name="gdn_post_bwd")
    dvn, dqd, dkd, dw_, dlast = _gdn_seq_bwd(do_b, gdn_w, gdn_qd, gdn_kd, gdn_mm, gdn_vn, s_all, aux, name="gdn_seq_bwd")
    dqkvn, daux = _gdn_local_bwd(qkvn, aux, aux_t, gdn_tinv, gdn_u, gdn_w, gdn_vn, do_b, dvn, dqd, dkd, dw_, dlast,
                                 name="gdn_local_bwd")
    dxc, dp, dalog, ddt = _gdn_pre_bwd1(xc, dqkvn, daux, ab, alog_pad, dt_pad, dkv_a, dp, cat["kv_a"][0], ab_blk=ab_blk,
                                        name="gdn_pre_bwd1")
    grads["a_log"], grads["dt_bias"] = dalog[:, :nh], ddt[:, :nh]
    dp, grads["conv_w"] = _gdn_pre_bwd2(dxc, qkvb, small["conv_w"], dp, cat["qkv_b"][0], name="gdn_pre_bwd2")
    dp, dmkv = _xa_bwd(q_c, mkv, dys[2], dp, cat["q_c"][0], name="xa_bwd")
    grads["w_mem_kv"] = _mm(nmem, dmkv, ta=True, out_dtypes=(BF16,), name="dw_mem_kv")
    dnmem = _mm(dmkv, wts["w_mem_kv"], tb=True, name="d_mem_kv")
    _, _, grads["g_mem"] = _rms_bwd(dnmem, mem, small["g_mem"], jnp.zeros_like(mem), name="rms_mem_bwd")
    if comm is None:
        grads["w_cat"] = _mm(n, dp, ta=True, out_dtypes=(BF16,), name="dw_in")
        dn = _mm(dp, w_cat, tb=True, name="d_in_proj")
    else:
        s1_mid = comm.pair_sums([comm.shard_major(k, grads.pop(k)) for k in comm.mid], "mid")
        dw_cat, rcv_mlp = _mm(n, dp, ta=True, out_dtypes=(BF16,), name="dw_in", exchange=_chip_exchange(s1_mlp))
        s1_in = comm.pair_sums([_from_cat(dw_cat)], "in")
        dn, rcv_rest = _mm(dp, w_cat, tb=True, name="d_in_proj", exchange=_chip_exchange(s1_in + s1_mid))
        halves = comm.chip_sums(s1_in + s1_mid + s1_mlp, rcv_rest + rcv_mlp)
        reduced = _exchange_call(_join_halves(halves), name="rs_join_halves")
        grads.update(zip(["w_in"] + comm.mid + list(comm.MLP), reduced))
    dx, _, grads["g_mix"] = _rms_bwd(dn, x, small["g_mix"], dh1, name="rms_mix_bwd")
    return loss, dx, grads


HBM_SPEC = pl.BlockSpec(memory_space=pltpu.HBM)
VMEM_SPEC = pl.BlockSpec(memory_space=pltpu.VMEM)
N_CHIPS = N_SHARDS
N_DEV = 8
DMA_CHUNK_BYTES = 1 << 20


def _place():
    return lax.axis_index("x"), lax.axis_index("y"), lax.axis_index("c")


def _other_chips(x, y):
    return [(1 - x, y), (x, 1 - y), (1 - x, 1 - y)]


def _n_chunks(rows, row_bytes):
    n = 1
    while rows % (2 * n) == 0 and (rows // (2 * n)) % 16 == 0 and (rows // n) * row_bytes > DMA_CHUNK_BYTES:
        n *= 2
    return n


def _sem_scratch(n_remote, n_local):
    return [pltpu.SemaphoreType.DMA((max(n_remote, 1),)), pltpu.SemaphoreType.DMA((max(n_remote, 1),)),
            pltpu.SemaphoreType.DMA((max(n_local, 1),))]


def _gather_over_ici(shards):
    plan = _half_chunks(shards, 0)

    def copies_of(in_refs, out_refs, place):
        x, y, c = place
        remote = []
        for i, r0, nr in plan:
            mine = pl.ds(c * (shards[i].shape[0] // 2) + r0, nr)
            for chip in _other_chips(x, y):
                remote.append((in_refs[i].at[mine], out_refs[i].at[2 * x + y, mine], (*chip, c)))
        return remote, []

    shapes = tuple(jax.ShapeDtypeStruct((N_CHIPS, *s.shape), s.dtype) for s in shards)
    return Exchange(tuple(shards), shapes, 3 * len(plan), 0, copies_of)


def _gather_pass_on(arrived):
    plan = _half_chunks([jax.ShapeDtypeStruct(a.shape[1:], a.dtype) for a in arrived], 0)

    def copies_of(in_refs, out_refs, place):
        x, y, c = place
        remote = []
        for i, r0, nr in plan:
            mine = pl.ds(c * (arrived[i].shape[1] // 2) + r0, nr)
            for chip in _other_chips(x, y):
                rows = out_refs[i].at[2 * chip[0] + chip[1], mine]
                remote.append((rows, rows, (x, y, 1 - c)))
        return remote, []

    shapes = tuple(jax.ShapeDtypeStruct(a.shape, a.dtype) for a in arrived)
    return Exchange(tuple(arrived), shapes, 3 * len(plan), 0, copies_of, tuple((i, i) for i in range(len(arrived))))


def _place_own(arrived, shard, chip, *, name):
    r, c = shard.shape
    tb = _row_block(r, c, shard.dtype.itemsize)

    def body(chip_ref, s_ref, _, o_ref):
        o_ref[...] = s_ref[...]

    return pl.pallas_call(
        body, grid_spec=pltpu.PrefetchScalarGridSpec(
            num_scalar_prefetch=1, grid=(r // tb,),
            in_specs=[pl.BlockSpec((tb, c), lambda i, chip_ref: (i, 0)), ANY_SPEC],
            out_specs=pl.BlockSpec((None, tb, c), lambda i, chip_ref: (chip_ref[0], i, 0))),
        out_shape=jax.ShapeDtypeStruct(arrived.shape, arrived.dtype), input_output_aliases={2: 0},
        compiler_params=_cp(("parallel",)), name=name,
    )(chip, shard, arrived)


def _exchange_call(ex, *, name):
    n_in, n_out = len(ex.ins), len(ex.out_shapes)

    def body(*refs):
        cps = _exchange_copies(ex, refs[:n_in], refs[n_in:n_in + n_out], refs[n_in + n_out:])
        for cp in cps:
            cp.start()
        for cp in cps:
            cp.wait()

    return pl.pallas_call(
        body, out_shape=list(ex.out_shapes), in_specs=[HBM_SPEC] * n_in, out_specs=[HBM_SPEC] * n_out,
        scratch_shapes=_sem_scratch(ex.n_remote, ex.n_local), input_output_aliases=dict(ex.aliases), name=name,
    )(*ex.ins)


def _half_chunks(arrs, row_axis):
    plan = []
    for i, a in enumerate(arrs):
        rh = a.shape[row_axis] // 2
        row_bytes = a.dtype.itemsize * math.prod(a.shape) // a.shape[row_axis]
        nch = _n_chunks(rh, row_bytes)
        plan += [(i, q * (rh // nch), rh // nch) for q in range(nch)]
    return plan


def _sibling_halves(gs):
    plan = _half_chunks(gs, 1)

    def copies_of(in_refs, out_refs, place):
        x, y, c = place
        out = []
        for i, r0, nr in plan:
            rh = gs[i].shape[1] // 2
            out.append((in_refs[i].at[:, pl.ds((1 - c) * rh + r0, nr), :], out_refs[i].at[:, pl.ds(r0, nr), :],
                        (x, y, 1 - c)))
        return out, []

    shapes = tuple(jax.ShapeDtypeStruct((g.shape[0], g.shape[1] // 2, g.shape[2]), g.dtype) for g in gs)
    return Exchange(tuple(gs), shapes, len(plan), 0, copies_of)


def _chip_exchange(s1s):
    plan = _half_chunks([jax.ShapeDtypeStruct((2 * s.shape[1], s.shape[2]), s.dtype) for s in s1s], 0)

    def copies_of(in_refs, out_refs, place):
        x, y, c = place
        out = []
        for i, r0, nr in plan:
            for j, chip in enumerate(_other_chips(x, y)):
                out.append((in_refs[i].at[2 * chip[0] + chip[1], pl.ds(r0, nr), :], out_refs[i].at[j, pl.ds(r0, nr), :],
                            (*chip, c)))
        return out, []

    shapes = tuple(jax.ShapeDtypeStruct((3, *s.shape[1:]), s.dtype) for s in s1s)
    return Exchange(tuple(s1s), shapes, 3 * len(plan), 0, copies_of)


def _join_halves(gs):
    plan = _half_chunks(gs, 0)

    def copies_of(in_refs, out_refs, place):
        x, y, c = place
        out = []
        for i, r0, nr in plan:
            rows = out_refs[i].at[pl.ds(c * (gs[i].shape[0] // 2) + r0, nr), :]
            out.append((rows, rows, (x, y, 1 - c)))
        return out, []

    shapes = tuple(jax.ShapeDtypeStruct(g.shape, g.dtype) for g in gs)
    aliases = tuple((i, i) for i in range(len(gs)))
    return Exchange(tuple(gs), shapes, len(plan), 0, copies_of, aliases)


ROW_BLOCK_BYTES = 4 << 20


def _row_block(rows, cols, itemsize=4):
    tb = rows
    while tb % 32 == 0 and tb * cols * itemsize > ROW_BLOCK_BYTES:
        tb //= 2
    return tb


def _pair_sum(g, sib, core, *, name):
    ns, r, c = g.shape
    rh = r // 2
    tb = _row_block(rh, c, g.dtype.itemsize)
    nb = rh // tb

    def body(core_ref, g_ref, s_ref, o_ref):
        o_ref[...] = (g_ref[...].astype(F32) + s_ref[...].astype(F32)).astype(o_ref.dtype)

    mine = pl.BlockSpec((None, tb, c), lambda s, i, core_ref: (s, core_ref[0] * nb + i, 0))
    half = pl.BlockSpec((None, tb, c), lambda s, i, core_ref: (s, i, 0))
    return pl.pallas_call(
        body, grid_spec=pltpu.PrefetchScalarGridSpec(num_scalar_prefetch=1, grid=(ns, nb), in_specs=[mine, half],
                                                     out_specs=half),
        out_shape=jax.ShapeDtypeStruct((ns, rh, c), BF16), compiler_params=_cp(("parallel", "parallel")), name=name,
    )(core, g, sib)


def _chip_sum(s1, rcv, where, *, name):
    _, rh, c = s1.shape
    tb = _row_block(rh, c)
    nb = rh // tb

    def body(where_ref, own_ref, r0_ref, r1_ref, r2_ref, o_ref):
        acc = own_ref[...].astype(F32)
        for r in (r0_ref, r1_ref, r2_ref):
            acc = acc + r[...].astype(F32)
        o_ref[...] = acc

    own = pl.BlockSpec((None, tb, c), lambda i, w: (w[1], i, 0))
    got = [pl.BlockSpec((None, tb, c), functools.partial(lambda i, w, j: (j, i, 0), j=j)) for j in range(3)]
    return pl.pallas_call(
        body, grid_spec=pltpu.PrefetchScalarGridSpec(
            num_scalar_prefetch=1, grid=(nb,), in_specs=[own] + got,
            out_specs=pl.BlockSpec((tb, c), lambda i, w: (w[0] * nb + i, 0))),
        out_shape=jax.ShapeDtypeStruct((2 * rh, c), F32), compiler_params=_cp(("parallel",)), name=name,
    )(where, s1, rcv, rcv, rcv)


def _all_gather_small(blk, *, name):
    r = blk.shape[0]

    def body(b_ref, out_ref, send_sems, recv_sems):
        x, y, c = _place()
        me = 4 * x + 2 * y + c
        out_ref[me] = b_ref[...]
        sends = []
        for k in range(1, N_DEV):
            peer = (x ^ (k >> 2), y ^ ((k >> 1) & 1), c ^ (k & 1))
            sends.append(pltpu.make_async_remote_copy(src_ref=b_ref, dst_ref=out_ref.at[me], send_sem=send_sems.at[k - 1],
                                                      recv_sem=recv_sems.at[k - 1], device_id=peer, device_id_type=MESH))
        for cp in sends:
            cp.start()
        for k in range(1, N_DEV):
            rows = out_ref.at[me ^ k]
            pltpu.make_async_remote_copy(src_ref=rows, dst_ref=rows, send_sem=send_sems.at[k - 1],
                                         recv_sem=recv_sems.at[k - 1], device_id=(x, y, c), device_id_type=MESH).wait_recv()
        for cp in sends:
            cp.wait_send()

    return pl.pallas_call(
        body, out_shape=jax.ShapeDtypeStruct((N_DEV, r, LANES), blk.dtype), in_specs=[VMEM_SPEC], out_specs=VMEM_SPEC,
        scratch_shapes=[pltpu.SemaphoreType.DMA((N_DEV - 1,)), pltpu.SemaphoreType.DMA((N_DEV - 1,))],
        name=name,
    )(blk)


def _sum_rows(parts, out_dtype, *, name, tb=1024):
    rows = parts[0].shape[0]
    tb = _blk(rows, tb)

    def body(*refs):
        acc = refs[0][...].astype(F32)
        for r in refs[1:-1]:
            acc = acc + r[...].astype(F32)
        refs[-1][...] = acc.astype(refs[-1].dtype)

    spec = pl.BlockSpec((tb, LANES), lambda i: (i, 0))
    return pl.pallas_call(
        body, grid=(rows // tb,), in_specs=[spec] * len(parts), out_specs=spec,
        out_shape=jax.ShapeDtypeStruct((rows, LANES), out_dtype), compiler_params=_cp(("parallel",)), name=name,
    )(*parts)


BIG = (
    ("w_in", 1), ("w_mem_kv", 0), ("w_swa_up", 1), ("w_gdn_up", 1), ("w_xa_up", 1), ("w_out", 0), ("w_mlp_in", 1),
    ("w_mlp_out", 0))


class _Comm:
    MLP = ("w_mlp_in", "w_mlp_out")

    def __init__(self, late_shards, core, where):
        self.axis = dict(BIG)
        self.late_shards = late_shards
        self.mid = [k for k in late_shards if k not in self.MLP and k != "w_in"]
        self.core, self.where = core, where

    def gather_exchange(self, names):
        return _gather_over_ici([self.late_shards[k] for k in names])

    def as_weights(self, names, whole):
        return {k: (g.reshape(-1, g.shape[2]) if self.axis[k] == 0 else g) for k, g in zip(names, whole)}

    def gathered(self, names, landed, tag):
        return self.as_weights(names, _exchange_call(self.pass_on(names, landed), name=f"ag_{tag}_pass"))

    def pass_on(self, names, landed):
        chip = self.where[1:2]
        return _gather_pass_on([_place_own(a, self.late_shards[k], chip, name=f"ag_own_{k}") for k, a in zip(names, landed)])

    def shard_major(self, k, grad):
        return grad.reshape(N_CHIPS, -1, grad.shape[-1]) if self.axis[k] == 0 else grad

    def pair_sums(self, gs, tag, sibs=None):
        if sibs is None:
            sibs = _exchange_call(_sibling_halves(gs), name=f"rs_sibling_{tag}")
        return [_pair_sum(g, s, self.core, name=f"rs_pair_sum_{tag}{i}") for i, (g, s) in enumerate(zip(gs, sibs))]

    def chip_sums(self, s1s, rcvs):
        return [_chip_sum(s1, rcv, self.where, name=f"rs_chip_sum_{i}") for i, (s1, rcv) in enumerate(zip(s1s, rcvs))]
SMALL = ("g_mix", "sinks", "a_log", "dt_bias", "gdn_norm_w", "g_mem", "g_mlp", "g_final")


def _rows128(a, rows):
    flat = a.reshape(-1)
    return jnp.pad(flat, (0, rows * LANES - flat.shape[0])).reshape(rows, LANES)


def kernel(x, mem, g_mix, w_in, sinks, conv_w, a_log, dt_bias, gdn_norm_w, g_mem, w_mem_kv, w_swa_up, w_gdn_up, w_xa_up, w_out, g_mlp, w_mlp_in, w_mlp_out, g_final, loss_target, m_g_mix, m_w_in, m_sinks, m_conv_w, m_a_log, m_dt_bias, m_gdn_norm_w, m_g_mem, m_w_mem_kv, m_w_swa_up, m_w_gdn_up, m_w_xa_up, m_w_out, m_g_mlp, m_w_mlp_in, m_w_mlp_out, m_g_final, v_g_mix, v_w_in, v_sinks, v_conv_w, v_a_log, v_dt_bias, v_gdn_norm_w, v_g_mem, v_w_mem_kv, v_w_swa_up, v_w_gdn_up, v_w_xa_up, v_w_out, v_g_mlp, v_w_mlp_in, v_w_mlp_out, v_g_final):
    given = dict(locals())
    xi, yi, ci = _place()
    chip = 2 * xi + yi
    core = jnp.reshape(ci, (1,)).astype(jnp.int32)
    where = jnp.stack([ci, chip]).astype(jnp.int32)

    comm = _Comm({k: given[k][0].astype(BF16) for k, _ in BIG}, core, where)
    wts = {}
    conv_shard = conv_w[0]
    conv_rows = -(-conv_shard.size // (8 * LANES)) * 8
    conv_all = _all_gather_small(_rows128(conv_shard, conv_rows), name="ag_conv")
    conv_full = jnp.concatenate(
        [conv_all[2 * s].reshape(-1)[:conv_shard.size].reshape(conv_shard.shape) for s in range(N_CHIPS)], axis=1)

    small = {k: given[k].reshape(1, -1) for k in SMALL}
    small["conv_w"] = conv_full
    loss_row, dx, grads = _local_step(x[0], mem[0], loss_target[0], wts, small, comm)
    big_grads = {k: grads[k] for k, _ in BIG}

    layout = [("loss", loss_row[:, :1])] + [(k, grads[k]) for k in SMALL] + [("conv_w", grads["conv_w"])]
    rows = [-(-a.size // LANES) for _, a in layout]
    blk_rows = -(-sum(rows) // 8) * 8
    blk = jnp.concatenate([_rows128(a.astype(F32), n) for (_, a), n in zip(layout, rows)]
                          + [jnp.zeros((blk_rows - sum(rows), LANES), F32)], axis=0)
    gathered = _all_gather_small(blk, name="ag_small_grads")
    reduced = _sum_rows([gathered[i] for i in range(N_DEV)], F32, name="small_grad_sum")
    small_grads, start = {}, 0
    for (k, a), n in zip(layout, rows):
        small_grads[k] = reduced[start:start + n].reshape(-1)[:a.size].reshape(a.shape)
        start += n
    loss = small_grads["loss"].reshape(())
    cw = conv_shard.shape[1]
    conv_grad = lax.dynamic_slice_in_dim(small_grads["conv_w"], chip * cw, cw, axis=1)

    names = ["g_mix", "w_in", "sinks", "conv_w", "a_log", "dt_bias", "gdn_norm_w", "g_mem", "w_mem_kv", "w_swa_up",
             "w_gdn_up", "w_xa_up", "w_out", "g_mlp", "w_mlp_in", "w_mlp_out", "g_final"]
    out_g, out_d, out_m, out_v = [], [], [], []
    for k in names:
        w, m, v = given[k], given["m_" + k], given["v_" + k]
        if k in big_grads:
            g2 = big_grads[k]
        elif k == "conv_w":
            g2 = conv_grad
        else:
            g2 = small_grads[k]
        as_given = (lambda a: a.reshape(1, -1)) if w.ndim == 1 else (lambda a: a)
        if w.shape[-1] % LANES and w.shape[-1] > LANES:
            tr = lambda a: jnp.swapaxes(a, -1, -2)
            g_out, delta, new_m, new_v = (tr(a) for a in _adamw(tr(w), tr(g2), tr(m), tr(v), name="adamw_" + k))
        else:
            g_out, delta, new_m, new_v = _adamw(as_given(w), g2, as_given(m), as_given(v), name="adamw_" + k)
        out_g.append(g_out.reshape(w.shape))
        out_d.append(delta.reshape(w.shape))
        out_m.append(new_m.reshape(w.shape))
        out_v.append(new_v.reshape(w.shape))
    return (loss, dx[None], *out_g, *out_d, *out_m, *out_v)
```

```python
import functools
import math
from typing import Callable, NamedTuple

import jax
import jax.numpy as jnp
from jax import lax
from jax.experimental import pallas as pl
from jax.experimental.pallas import tpu as pltpu

F32 = jnp.float32
BF16 = jnp.bfloat16
HI = lax.Precision.HIGHEST
MESH = pl.DeviceIdType.MESH

SWA_Q_HEADS = 16
SWA_KV_HEADS = 2
SWA_HEAD_DIM = 64
SWA_WINDOW = 128
SWA_SCALE = SWA_HEAD_DIM ** -0.5
assert math.frexp(SWA_SCALE)[0] == 0.5
GDN_HEADS = 4
GDN_HEAD_DIM = 128
GDN_CONV = 4
GDN_CHUNK = 64
XA_HEADS = 4
XA_HEAD_DIM = 128
RMS_EPS = 1e-6
L2_EPS = 1e-6
ADAM_LR = 0.001
ADAM_B1 = 0.9
ADAM_B2 = 0.999
ADAM_EPS = 1e-08
ADAM_WD = 0.01
ADAM_STEP = 10

LANES = 128
N_SHARDS = 4
VMEM_LIMIT = 56 * 1024 * 1024

NT = (((1,), (1,)), ((), ()))
TN = (((0,), (0,)), ((), ()))
NN = (((1,), (0,)), ((), ()))


def _cp(sem=None):
    return pltpu.CompilerParams(dimension_semantics=sem, vmem_limit_bytes=VMEM_LIMIT)


def _blk(dim, pref):
    if dim <= pref:
        return dim
    b = (pref // LANES) * LANES
    while dim % b:
        b -= LANES
    assert b > 0, (dim, pref)
    return b


def _dot(a, b, dims=NN, precision=None):
    return lax.dot_general(a, b, dims, precision=precision, preferred_element_type=F32)


def _sigmoid(x):
    return 0.5 * jnp.tanh(0.5 * x) + 0.5


MM_TK_BYTES = 4096


def _mm(a, b, *, name, ta=False, tb=False, out_dtypes=(F32,), epilogue=None, extras=(), tm=1024, tn=1024, tk=None,
        b_sharded=False, out_sharded=False, b_window=None, exchange=None):
    (kdim, m) = a.shape if ta else a.shape[::-1]
    col0 = 0
    n_lim = k_lim = None
    if b_sharded:
        ns, rows_w, per = b.shape
        if tb:
            kb, n, k_lim = ns * per, rows_w, per
        else:
            kb, n, n_lim = rows_w, ns * per, per
    else:
        (kb, n) = b.shape[::-1] if tb else b.shape
        if b_window is not None:
            assert not tb
            col0, n = b_window
    assert kdim == kb, (a.shape, b.shape, ta, tb)
    if out_sharded:
        assert n % N_SHARDS == 0
        n_lim = n // N_SHARDS if n_lim is None else n_lim
        assert n_lim == n // N_SHARDS
    if tk is None:
        tk = MM_TK_BYTES // max(a.dtype.itemsize, b.dtype.itemsize)
    tm, tn, tk = _blk(m, tm), _blk(n_lim or n, tn), _blk(k_lim or kdim, tk)
    assert col0 % tn == 0, (col0, tn)
    nk = kdim // tk
    a_spec = pl.BlockSpec((tk, tm), lambda i, j, k: (k, i)) if ta else pl.BlockSpec((tm, tk), lambda i, j, k: (i, k))
    if b_sharded and tb:
        kpb = k_lim // tk
        b_spec = pl.BlockSpec((None, tn, tk), lambda i, j, k: (k // kpb, j, k % kpb))
    elif b_sharded:
        bpb = n_lim // tn
        b_spec = pl.BlockSpec((None, tk, tn), lambda i, j, k: (j // bpb, k, j % bpb))
    elif tb:
        b_spec = pl.BlockSpec((tn, tk), lambda i, j, k: (j, k))
    else:
        b_spec = pl.BlockSpec((tk, tn), lambda i, j, k: (k, j + col0 // tn))
    x_spec = pl.BlockSpec((tm, tn), lambda i, j, k: (i, j))
    r_spec = pl.BlockSpec((1, tn), lambda i, j, k: (0, j))
    if out_sharded:
        opb = n_lim // tn
        o_spec = pl.BlockSpec((None, tm, tn), lambda i, j, k: (j // opb, i, j % opb))
        out_shape = (N_SHARDS, m, n_lim)
    else:
        o_spec, out_shape = x_spec, (m, n)
    dims = ((((0 if ta else 1),), ((1 if tb else 0),)), ((), ()))
    n_extra, n_out = len(extras), len(out_dtypes)

    host = _ExchangeHost(exchange)
    grid = (m // tm, n // tn, nk)

    def body(*refs):
        a_ref, b_ref = refs[:2]
        extra_refs = refs[2:2 + n_extra]
        out_refs = refs[2 + n_extra + host.n_in:2 + n_extra + host.n_in + n_out]
        host.start(refs, 2 + n_extra, 2 + n_extra + host.n_in + n_out, grid)
        part = _dot(a_ref[...].astype(BF16), b_ref[...].astype(BF16), dims)

        def finish(acc):
            vals = epilogue(acc, *[r[...] for r in extra_refs]) if epilogue is not None else (acc,) * n_out
            assert len(vals) == n_out
            for r, v in zip(out_refs, vals):
                r[...] = v.astype(r.dtype)

        if nk == 1:
            finish(part)
        else:
            acc_ref = refs[2 + n_extra + host.n_in + n_out + host.n_out]
            k = pl.program_id(2)

            @pl.when(k == 0)
            def _():
                acc_ref[...] = part

            @pl.when((k > 0) & (k < nk - 1))
            def _():
                acc_ref[...] += part

            @pl.when(k == nk - 1)
            def _():
                finish(acc_ref[...] + part)

        host.wait(refs, 2 + n_extra, 2 + n_extra + host.n_in + n_out, grid)

    outs = pl.pallas_call(
        body,
        grid=grid,
        in_specs=[a_spec, b_spec] + [x_spec if e.shape[0] == m else r_spec for e in extras] + host.in_specs,
        out_specs=[o_spec] * n_out + host.out_specs,
        out_shape=[jax.ShapeDtypeStruct(out_shape, d) for d in out_dtypes] + host.out_shapes,
        scratch_shapes=([pltpu.VMEM((tm, tn), F32)] if nk > 1 else []) + host.scratch,
        input_output_aliases=host.aliases(2 + n_extra, n_out),
        compiler_params=_cp(host.semantics(("parallel", "parallel", "arbitrary"))),
        name=name,
    )(a, b, *extras, *host.ins)
    mine, landed = outs[:n_out], list(outs[n_out:])
    mine = mine[0] if n_out == 1 else mine
    return (mine, landed) if exchange is not None else mine


class Exchange(NamedTuple):
    ins: tuple
    out_shapes: tuple
    n_remote: int
    n_local: int
    copies_of: Callable
    aliases: tuple = ()


def _exchange_copies(ex, in_refs, out_refs, sem_refs):
    send_sems, recv_sems, local_sems = sem_refs
    remote, local = ex.copies_of(in_refs, out_refs, _place())
    assert len(remote) == ex.n_remote and len(local) == ex.n_local, (len(remote), len(local))
    cps = [pltpu.make_async_remote_copy(src_ref=src, dst_ref=dst, send_sem=send_sems.at[k], recv_sem=recv_sems.at[k],
                                        device_id=to, device_id_type=MESH) for k, (src, dst, to) in enumerate(remote)]
    cps += [pltpu.make_async_copy(src, dst, local_sems.at[k]) for k, (src, dst) in enumerate(local)]
    return cps


class _ExchangeHost:
    def __init__(self, ex):
        self.ex = ex
        self.ins = list(ex.ins) if ex else []
        self.out_shapes = list(ex.out_shapes) if ex else []
        self.n_in, self.n_out = len(self.ins), len(self.out_shapes)
        self.in_specs = [HBM_SPEC] * self.n_in
        self.out_specs = [HBM_SPEC] * self.n_out
        self.scratch = _sem_scratch(ex.n_remote, ex.n_local) if ex else []

    def semantics(self, sem):
        return tuple("arbitrary" for _ in sem) if self.ex else sem

    def aliases(self, in_at, out_at):
        return {in_at + i: out_at + o for i, o in self.ex.aliases} if self.ex else {}

    def _refs(self, refs, in_at, out_at):
        return refs[in_at:in_at + self.n_in], refs[out_at:out_at + self.n_out], refs[len(refs) - 3:]

    def _when(self, grid, last):
        cond = None
        for d, size in enumerate(grid):
            c = pl.program_id(d) == (size - 1 if last else 0)
            cond = c if cond is None else cond & c
        return cond

    def start(self, refs, in_at, out_at, grid):
        if self.ex:
            @pl.when(self._when(grid, False))
            def _():
                for cp in _exchange_copies(self.ex, *self._refs(refs, in_at, out_at)):
                    cp.start()

    def wait(self, refs, in_at, out_at, grid):
        if self.ex:
            @pl.when(self._when(grid, True))
            def _():
                for cp in _exchange_copies(self.ex, *self._refs(refs, in_at, out_at)):
                    cp.wait()


def _rms_fwd(x, g, *, name, tm=512, exchange=None):
    t, d = x.shape
    tm = _blk(t, tm)
    host = _ExchangeHost(exchange)
    grid = (t // tm,)

    def body(*refs):
        x_ref, g_ref, n_ref = refs[0], refs[1], refs[2 + host.n_in]
        host.start(refs, 2, 3 + host.n_in, grid)
        xv = x_ref[...]
        r = lax.rsqrt(jnp.mean(xv * xv, axis=-1, keepdims=True) + RMS_EPS)
        n_ref[...] = (xv * r * g_ref[...]).astype(n_ref.dtype)
        host.wait(refs, 2, 3 + host.n_in, grid)

    outs = pl.pallas_call(
        body, grid=grid,
        in_specs=[pl.BlockSpec((tm, d), lambda i: (i, 0)), pl.BlockSpec((1, d), lambda i: (0, 0))] + host.in_specs,
        out_specs=[pl.BlockSpec((tm, d), lambda i: (i, 0))] + host.out_specs,
        out_shape=[jax.ShapeDtypeStruct((t, d), BF16)] + host.out_shapes,
        scratch_shapes=host.scratch, input_output_aliases=host.aliases(2, 1),
        compiler_params=_cp(host.semantics(("parallel",))), name=name,
    )(x, g, *host.ins)
    return (outs[0], list(outs[1:])) if exchange is not None else outs[0]


def _rms_bwd(dn, x, g, dres, *, name, tm=512):
    t, d = x.shape
    tm = _blk(t, tm)

    def body(dn_ref, x_ref, g_ref, dres_ref, dx_ref, dxb_ref, dg_ref):
        i = pl.program_id(0)
        xv = x_ref[...]
        r = lax.rsqrt(jnp.mean(xv * xv, axis=-1, keepdims=True) + RMS_EPS)
        xh = xv * r
        dnv = dn_ref[...].astype(F32)
        dxh = dnv * g_ref[...]
        dx = dres_ref[...] + r * (dxh - xh * jnp.mean(dxh * xh, axis=-1, keepdims=True))
        dx_ref[...] = dx
        dxb_ref[...] = dx.astype(dxb_ref.dtype)
        part = jnp.sum(dnv * xh, axis=0, keepdims=True)

        @pl.when(i == 0)
        def _():
            dg_ref[...] = part

        @pl.when(i > 0)
        def _():
            dg_ref[...] += part

    row = pl.BlockSpec((tm, d), lambda i: (i, 0))
    vec = pl.BlockSpec((1, d), lambda i: (0, 0))
    return pl.pallas_call(
        body, grid=(t // tm,),
        in_specs=[row, row, vec, row], out_specs=[row, row, vec],
        out_shape=[jax.ShapeDtypeStruct((t, d), F32), jax.ShapeDtypeStruct((t, d), BF16),
                   jax.ShapeDtypeStruct((1, d), F32)],
        compiler_params=_cp(("arbitrary",)), name=name,
    )(dn, x, g, dres)


def _final_loss(h, g, tgt, *, name, tm=512):
    t, d = h.shape
    tm = _blk(t, tm)

    def body(h_ref, g_ref, t_ref, dh_ref, dhb_ref, dg_ref, loss_ref):
        i = pl.program_id(0)
        hv = h_ref[...]
        r = lax.rsqrt(jnp.mean(hv * hv, axis=-1, keepdims=True) + RMS_EPS)
        xh = hv * r
        e = xh * g_ref[...] - t_ref[...]
        dy = e * (1.0 / d)
        dxh = dy * g_ref[...]
        dh = r * (dxh - xh * jnp.mean(dxh * xh, axis=-1, keepdims=True))
        dh_ref[...] = dh
        dhb_ref[...] = dh.astype(dhb_ref.dtype)
        dg_part = jnp.sum(dy * xh, axis=0, keepdims=True)
        row_loss = jnp.sum(e * e, axis=-1, keepdims=True) * (0.5 / d)
        loss_part = jnp.sum(row_loss, axis=0, keepdims=True)

        @pl.when(i == 0)
        def _():
            dg_ref[...] = dg_part
            loss_ref[...] = jnp.broadcast_to(loss_part, loss_ref.shape)

        @pl.when(i > 0)
        def _():
            dg_ref[...] += dg_part
            loss_ref[...] += jnp.broadcast_to(loss_part, loss_ref.shape)

    row = pl.BlockSpec((tm, d), lambda i: (i, 0))
    vec = pl.BlockSpec((1, d), lambda i: (0, 0))
    return pl.pallas_call(
        body, grid=(t // tm,),
        in_specs=[row, vec, row], out_specs=[row, row, vec, pl.BlockSpec((1, LANES), lambda i: (0, 0))],
        out_shape=[jax.ShapeDtypeStruct((t, d), F32), jax.ShapeDtypeStruct((t, d), BF16),
                   jax.ShapeDtypeStruct((1, d), F32), jax.ShapeDtypeStruct((1, LANES), F32)],
        compiler_params=_cp(("arbitrary",)), name=name,
    )(h, g, tgt)


SWA_SUB = 64


def _swa_mask(n, rows, row0):
    w = SWA_WINDOW
    qi = (lax.broadcasted_iota(jnp.int32, (rows, 2 * w), 0) + row0) & (w - 1)
    kj = lax.broadcasted_iota(jnp.int32, (rows, 2 * w), 1)
    return (kj > qi) & (kj <= qi + w) & ((n > 0) | (kj >= w))


def _stack_heads(ref, heads, width):
    return jnp.concatenate([ref[:, h * width:(h + 1) * width] for h in heads], axis=0)


def _stack_scalars(ref, heads, rows):
    return jnp.concatenate([jnp.broadcast_to(ref[0:1, h:h + 1], (rows, 1)) for h in heads], axis=0)


def _swa_fwd(q, kv, sinks, *, name, exchange=None):
    t = q.shape[0]
    w, hd, hq, hkv = SWA_WINDOW, SWA_HEAD_DIM, SWA_Q_HEADS, SWA_KV_HEADS
    grp = hq // hkv
    kvw = hkv * hd
    nb = t // w
    host = _ExchangeHost(exchange)
    assert not (exchange and exchange.aliases)

    def body(*refs):
        q_ref, kvp_ref, kvc_ref, s_ref = refs[:4]
        o_ref, lse_ref = refs[4 + host.n_in:6 + host.n_in]
        host.start(refs, 4, 6 + host.n_in, (nb,))
        n = pl.program_id(0)
        mask = _swa_mask(n, grp * w, 0)
        kvcat = jnp.concatenate([kvp_ref[...], kvc_ref[...]], axis=0)
        kvs = range(hkv)
        heads = [range(hk * grp, (hk + 1) * grp) for hk in kvs]
        sks = [_stack_scalars(s_ref, hs, w) for hs in heads]
        ss = [jnp.where(mask, _dot(_stack_heads(q_ref, heads[hk], hd) * SWA_SCALE, kvcat[:, hk * hd:(hk + 1) * hd], NT),
                        -jnp.inf) for hk in kvs]
        ms = [jnp.maximum(jnp.max(s, axis=-1, keepdims=True), sk) for s, sk in zip(ss, sks)]
        ps = [jnp.exp(s - m) for s, m in zip(ss, ms)]
        dens = [jnp.sum(p, axis=-1, keepdims=True) + jnp.exp(sk - m) for p, sk, m in zip(ps, sks, ms)]
        os_ = [_dot((p * (1.0 / den)).astype(BF16), kvcat[:, kvw + hk * hd:kvw + (hk + 1) * hd])
               for hk, p, den in zip(kvs, ps, dens)]
        outs, lses = [], []
        for o, m, den in zip(os_, ms, dens):
            lse = m + jnp.log(den)
            outs += [o[j * w:(j + 1) * w] for j in range(grp)]
            lses += [lse[j * w:(j + 1) * w] for j in range(grp)]
        o_ref[...] = jnp.concatenate(outs, axis=1).astype(o_ref.dtype)
        lse_ref[...] = jnp.concatenate(lses, axis=1)
        host.wait(refs, 4, 6 + host.n_in, (nb,))

    outs = pl.pallas_call(
        body, grid=(nb,),
        in_specs=[pl.BlockSpec((w, hq * hd), lambda i: (i, 0)),
                  pl.BlockSpec((w, 2 * kvw), lambda i: (jnp.maximum(i - 1, 0), 0)),
                  pl.BlockSpec((w, 2 * kvw), lambda i: (i, 0)),
                  pl.BlockSpec((1, hq), lambda i: (0, 0))] + host.in_specs,
        out_specs=[pl.BlockSpec((w, hq * hd), lambda i: (i, 0)), pl.BlockSpec((w, hq), lambda i: (i, 0))] + host.out_specs,
        out_shape=[jax.ShapeDtypeStruct((t, hq * hd), BF16), jax.ShapeDtypeStruct((t, hq), F32)] + host.out_shapes,
        scratch_shapes=host.scratch,
        compiler_params=_cp(host.semantics(("parallel",))), name=name,
    )(q, kv, kv, sinks, *host.ins)
    return (outs[0], outs[1], list(outs[2:])) if exchange is not None else outs


ANY_SPEC = pl.BlockSpec(memory_space=pl.ANY)


def _swa_bwd(q, kv, sinks, o, lse, do, dp, dp_col, *, name):
    t = q.shape[0]
    w, hd, hq, hkv = SWA_WINDOW, SWA_HEAD_DIM, SWA_Q_HEADS, SWA_KV_HEADS
    grp = hq // hkv
    kvw = hkv * hd
    nb = t // w
    assert dp_col % (hq * hd) == 0
    dq_blk = dp_col // (hq * hd)

    def body(q_ref, kvp_ref, kvc_ref, s_ref, o_ref, lse_ref, do_ref, _, dq_ref, dkv_ref, ds_ref, carry_ref, s_scr, dp_scr,
             p_scr, ds_scr):
        n = pl.program_id(0)

        @pl.when(n == 0)
        def _():
            ds_ref[...] = jnp.zeros_like(ds_ref)
            carry_ref[...] = jnp.zeros_like(carry_ref)

        @pl.when(n < nb)
        def _():
            kvcat = jnp.concatenate([kvp_ref[...], kvc_ref[...]], axis=0)
            dqs, dsk, dks, dvs = [], [], [], []
            for hk in range(hkv):
                heads = range(hk * grp, (hk + 1) * grp)
                qs = _stack_heads(q_ref, heads, hd)
                dos = _stack_heads(do_ref, heads, hd)
                os_ = _stack_heads(o_ref, heads, hd)
                lse = _stack_heads(lse_ref, heads, 1)
                kh = kvcat[:, hk * hd:(hk + 1) * hd]
                vh = kvcat[:, kvw + hk * hd:kvw + (hk + 1) * hd]
                delta = jnp.sum(dos.astype(F32) * os_.astype(F32), axis=-1, keepdims=True)
                s_scr[...] = _dot(qs * SWA_SCALE, kh, NT)
                dp_scr[...] = _dot(dos, vh, NT)
                for r0 in range(0, grp * w, SWA_SUB):
                    rows = slice(r0, r0 + SWA_SUB)
                    p = jnp.exp(jnp.where(_swa_mask(n, SWA_SUB, r0 % w), s_scr[rows, :], -jnp.inf) - lse[rows])
                    p_scr[rows, :] = p.astype(p_scr.dtype)
                    ds_scr[rows, :] = (p * (dp_scr[rows, :] - delta[rows]) * SWA_SCALE).astype(ds_scr.dtype)
                ds = ds_scr[...]
                dq = _dot(ds, kh)
                dqs += [dq[j * w:(j + 1) * w] for j in range(grp)]
                dks.append(_dot(ds, qs, TN))
                dvs.append(_dot(p_scr[...], dos, TN))
                dsink = -jnp.exp(_stack_scalars(s_ref, heads, w) - lse) * delta
                dsk += [jnp.sum(dsink[j * w:(j + 1) * w], axis=0, keepdims=True) for j in range(grp)]
            dq_ref[...] = jnp.concatenate(dqs, axis=1).astype(dq_ref.dtype)
            ds_ref[...] += jnp.concatenate(dsk, axis=1)
            dkv_cat = jnp.concatenate(dks + dvs, axis=1)
            dkv_ref[...] = (carry_ref[...] + dkv_cat[:w]).astype(dkv_ref.dtype)
            carry_ref[...] = dkv_cat[w:]

        @pl.when(n == nb)
        def _():
            dkv_ref[...] = carry_ref[...].astype(dkv_ref.dtype)

    cur = lambda i: (jnp.minimum(i, nb - 1), 0)
    prev = lambda i: (jnp.clip(i - 1, 0, nb - 1), 0)
    return pl.pallas_call(
        body, grid=(nb + 1,),
        in_specs=[pl.BlockSpec((w, hq * hd), cur), pl.BlockSpec((w, 2 * kvw), prev), pl.BlockSpec((w, 2 * kvw), cur),
                  pl.BlockSpec((1, hq), lambda i: (0, 0)), pl.BlockSpec((w, hq * hd), cur),
                  pl.BlockSpec((w, hq), cur), pl.BlockSpec((w, hq * hd), cur), ANY_SPEC],
        out_specs=[pl.BlockSpec((w, hq * hd), lambda i: (jnp.minimum(i, nb - 1), dq_blk)),
                   pl.BlockSpec((w, 2 * kvw), prev), pl.BlockSpec((1, hq), lambda i: (0, 0))],
        out_shape=[jax.ShapeDtypeStruct(dp.shape, dp.dtype), jax.ShapeDtypeStruct((t, 2 * kvw), BF16),
                   jax.ShapeDtypeStruct((1, hq), F32)],
        scratch_shapes=[pltpu.VMEM((w, 2 * kvw), F32)] + [pltpu.VMEM((grp * w, 2 * w), dt) for dt in (F32, F32, BF16, BF16)],
        input_output_aliases={7: 0},
        compiler_params=_cp(("arbitrary",)), name=name,
    )(q, kv, kv, sinks, o, lse, do, dp)


def _xa_fwd(q, mkv, *, name, tq=512):
    t, xw = q.shape
    nm = mkv.shape[0]
    hd, nh = XA_HEAD_DIM, XA_HEADS
    tq = _blk(t, tq)

    def body(q_ref, mkv_ref, o_ref):
        cols = [slice(h * hd, (h + 1) * hd) for h in range(nh)]
        ss = [_dot(q_ref[:, c], mkv_ref[:, c], NT) * (hd ** -0.5) for c in cols]
        ps = [jnp.exp(s - jnp.max(s, axis=-1, keepdims=True)) for s in ss]
        ps = [p * (1.0 / jnp.sum(p, axis=-1, keepdims=True)) for p in ps]
        outs = [_dot(p.astype(BF16), mkv_ref[:, xw + c.start:xw + c.stop]) for p, c in zip(ps, cols)]
        o_ref[...] = jnp.concatenate(outs, axis=1).astype(o_ref.dtype)

    return pl.pallas_call(
        body, grid=(t // tq,),
        in_specs=[pl.BlockSpec((tq, xw), lambda i: (i, 0)), pl.BlockSpec((nm, 2 * xw), lambda i: (0, 0))],
        out_specs=pl.BlockSpec((tq, xw), lambda i: (i, 0)),
        out_shape=jax.ShapeDtypeStruct((t, xw), BF16),
        compiler_params=_cp(("parallel",)), name=name,
    )(q, mkv)


def _xa_bwd(q, mkv, do, dp, dp_col, *, name, tq=512):
    t, xw = q.shape
    nm = mkv.shape[0]
    hd, nh = XA_HEAD_DIM, XA_HEADS
    tq = _blk(t, tq)
    assert dp_col % xw == 0

    def body(q_ref, mkv_ref, do_ref, _, dq_ref, dmkv_ref):
        i = pl.program_id(0)
        cols = [slice(h * hd, (h + 1) * hd) for h in range(nh)]
        vcols = [slice(xw + c.start, xw + c.stop) for c in cols]
        ss = [_dot(q_ref[:, c], mkv_ref[:, c], NT) * (hd ** -0.5) for c in cols]
        dps = [_dot(do_ref[:, c], mkv_ref[:, v], NT) for c, v in zip(cols, vcols)]
        ps = [jnp.exp(s - jnp.max(s, axis=-1, keepdims=True)) for s in ss]
        ps = [p * (1.0 / jnp.sum(p, axis=-1, keepdims=True)) for p in ps]
        dss = [(p * (dp - jnp.sum(p * dp, axis=-1, keepdims=True)) * (hd ** -0.5)).astype(BF16) for p, dp in zip(ps, dps)]
        dqs = [_dot(ds, mkv_ref[:, c]) for ds, c in zip(dss, cols)]
        dks = [_dot(ds, q_ref[:, c], TN) for ds, c in zip(dss, cols)]
        dvs = [_dot(p.astype(BF16), do_ref[:, c], TN) for p, c in zip(ps, cols)]
        dq_ref[...] = jnp.concatenate(dqs, axis=1).astype(dq_ref.dtype)
        part = jnp.concatenate(dks + dvs, axis=1)

        @pl.when(i == 0)
        def _():
            dmkv_ref[...] = part

        @pl.when(i > 0)
        def _():
            dmkv_ref[...] += part

    row = pl.BlockSpec((tq, xw), lambda i: (i, 0))
    full = pl.BlockSpec((nm, 2 * xw), lambda i: (0, 0))
    return pl.pallas_call(
        body, grid=(t // tq,),
        in_specs=[row, full, row, ANY_SPEC],
        out_specs=[pl.BlockSpec((tq, xw), lambda i: (i, dp_col // xw)), full],
        out_shape=[jax.ShapeDtypeStruct(dp.shape, dp.dtype), jax.ShapeDtypeStruct((nm, 2 * xw), F32)],
        input_output_aliases={3: 0}, compiler_params=_cp(("arbitrary",)), name=name,
    )(q, mkv, do, dp)


def _merge_specs(ys, ws, tm):
    y_specs = [pl.BlockSpec((tm, y.shape[1]), lambda i: (i, 0)) for y in ys]
    w_specs = [pl.BlockSpec(w.shape, lambda i: (0, 0, 0)) for w in ws]
    return y_specs, w_specs


def _merge_tiles(ws, tn):
    ns, _, per = ws[0].shape
    tn = _blk(per, tn)
    return tn, [(s, c, s * per + c) for s in range(ns) for c in range(0, per, tn)]


def _merge_fwd(ys, ws, gates, *, name, tm=256, tn=512):
    t, d = ys[0].shape[0], ws[0].shape[0] * ws[0].shape[2]
    tm = _blk(t, tm)
    tn, tiles = _merge_tiles(ws, tn)
    y_specs, w_specs = _merge_specs(ys, ws, tm)

    def body(ya, yb, yc, wa, wb, wc, g_ref, o_ref):
        for s, c, col in tiles:
            acc = None
            for b, (y, w) in enumerate(((ya, wa), (yb, wb), (yc, wc))):
                term = _sigmoid(g_ref[:, b * d + col:b * d + col + tn]) * _dot(y[...], w[s, :, c:c + tn])
                acc = term if acc is None else acc + term
            o_ref[:, col:col + tn] = acc.astype(o_ref.dtype)

    return pl.pallas_call(
        body, grid=(t // tm,),
        in_specs=y_specs + w_specs + [pl.BlockSpec((tm, 3 * d), lambda i: (i, 0))],
        out_specs=pl.BlockSpec((tm, d), lambda i: (i, 0)),
        out_shape=jax.ShapeDtypeStruct((t, d), BF16),
        compiler_params=_cp(("parallel",)), name=name,
    )(*ys, *ws, gates)


def _merge_bwd(ys, ws, gates, dmerged, dp_width, *, name, tm=256, tn=512):
    t, d = ys[0].shape[0], ws[0].shape[0] * ws[0].shape[2]
    tm = _blk(t, tm)
    tn, tiles = _merge_tiles(ws, tn)
    y_specs, w_specs = _merge_specs(ys, ws, tm)
    row = pl.BlockSpec((tm, d), lambda i: (i, 0))
    wide = pl.BlockSpec((tm, 3 * d), lambda i: (i, 0))

    def body(ya, yb, yc, wa, wb, wc, g_ref, dm_ref, dua, dub, duc, dp_ref):
        for s, c, col in tiles:
            dm = dm_ref[:, col:col + tn]
            for b, (y, w, du) in enumerate(((ya, wa, dua), (yb, wb, dub), (yc, wc, duc))):
                sg = _sigmoid(g_ref[:, b * d + col:b * d + col + tn])
                u = _dot(y[...], w[s, :, c:c + tn])
                du[:, col:col + tn] = (dm * sg).astype(du.dtype)
                dp_ref[:, b * d + col:b * d + col + tn] = (dm * u * sg * (1.0 - sg)).astype(dp_ref.dtype)

    return pl.pallas_call(
        body, grid=(t // tm,),
        in_specs=y_specs + w_specs + [wide, row],
        out_specs=[row] * 3 + [wide],
        out_shape=[jax.ShapeDtypeStruct((t, d), BF16)] * 3 + [jax.ShapeDtypeStruct((t, dp_width), BF16)],
        compiler_params=_cp(("parallel",)), name=name,
    )(*ys, *ws, gates, dmerged)


def _adamw(w, g, m, v, *, name, tm=256):
    lead = w.ndim - 2
    assert all(s == 1 for s in w.shape[:lead]) and m.shape == w.shape and v.shape == w.shape
    r, c = w.shape[lead:]
    assert g.shape == (r, c)
    tm = _blk(r, tm) if r % 8 == 0 else r
    tc = c if tm * c * 4 <= ROW_BLOCK_BYTES else _blk(c, 256)
    ncb = c // tc
    bc1 = 1.0 - ADAM_B1 ** ADAM_STEP
    bc2 = 1.0 - ADAM_B2 ** ADAM_STEP

    def body(w_ref, g_ref, m_ref, v_ref, go_ref, d_ref, nm_ref, nv_ref):
        gv = g_ref[...]
        go_ref[...] = gv
        nm = ADAM_B1 * m_ref[...] + (1.0 - ADAM_B1) * gv
        nv = ADAM_B2 * v_ref[...] + (1.0 - ADAM_B2) * (gv * gv)
        d_ref[...] = -ADAM_LR * ((nm / bc1) / (jnp.sqrt(nv / bc2) + ADAM_EPS) + ADAM_WD * w_ref[...])
        nm_ref[...] = nm
        nv_ref[...] = nv

    spec = pl.BlockSpec((None,) * lead + (tm, tc), lambda i: (0,) * lead + (i // ncb, i % ncb))
    g_spec = pl.BlockSpec((tm, tc), lambda i: (i // ncb, i % ncb))
    return pl.pallas_call(
        body, grid=(r // tm * ncb,), in_specs=[spec, g_spec, spec, spec], out_specs=[spec] * 4,
        out_shape=[jax.ShapeDtypeStruct(w.shape, F32)] * 4,
        compiler_params=_cp(("parallel",)), name=name,
    )(w, g, m, v)


HALO = 8


def _shift_down(cur, prev, j):
    if j == 0:
        return cur
    y = pltpu.roll(cur, j, 0)
    row = lax.broadcasted_iota(jnp.int32, (HALO, cur.shape[1]), 0)
    top = jnp.where(row < j, pltpu.roll(prev, j, 0), y[:HALO])
    return jnp.concatenate([top, y[HALO:]], axis=0)


def _shift_up(cur, nxt, j):
    if j == 0:
        return cur
    tm = cur.shape[0]
    y = pltpu.roll(cur, tm - j, 0)
    row = lax.broadcasted_iota(jnp.int32, (HALO, cur.shape[1]), 0)
    bot = jnp.where(row >= HALO - j, pltpu.roll(nxt, HALO - j, 0), y[tm - HALO:])
    return jnp.concatenate([y[:tm - HALO], bot], axis=0)


def _softplus(x):
    return jnp.maximum(x, 0.0) + jnp.log(1.0 + jnp.exp(-jnp.abs(x)))


def _gdn_pre_fwd(qkvb, conv_w, ab, alog_pad, dt_pad, *, name, ab_blk=0, tm=256):
    t, cw = qkvb.shape
    hd, nh, ck = GDN_HEAD_DIM, GDN_HEADS, GDN_CHUNK
    gw = nh * hd
    tm = _blk(t, tm)
    hb = tm // HALO

    def body(x_ref, xp_ref, w_ref, ab_ref, al_ref, dt_ref, xc_ref, qkvn_ref, aux_ref):
        i = pl.program_id(0)
        cur = x_ref[...]
        prev = jnp.where(i > 0, xp_ref[...], 0.0)
        xc = None
        for tap in range(GDN_CONV):
            term = w_ref[tap:tap + 1, :] * _shift_down(cur, prev, GDN_CONV - 1 - tap)
            xc = term if xc is None else xc + term
        xc_ref[...] = xc
        s = xc * _sigmoid(xc)
        for h in range(2 * nh):
            xh = s[:, h * hd:(h + 1) * hd]
            r = lax.rsqrt(jnp.sum(xh * xh, axis=-1, keepdims=True) + L2_EPS)
            scale = hd ** -0.5 if h < nh else 1.0
            qkvn_ref[:, h * hd:(h + 1) * hd] = xh * (r * scale)
        qkvn_ref[:, 2 * gw:] = s[:, 2 * gw:]
        abv = ab_ref[...]
        lane = lax.broadcasted_iota(jnp.int32, abv.shape, 1)
        g = jnp.where(lane < nh, -jnp.exp(al_ref[...]) * _softplus(abv + dt_ref[...]), 0.0)
        beta = jnp.where((lane >= nh) & (lane < 2 * nh), _sigmoid(abv), 0.0)
        ii = lax.broadcasted_iota(jnp.int32, (tm, tm), 0)
        jj = lax.broadcasted_iota(jnp.int32, (tm, tm), 1)
        tri = jnp.where((ii >= jj) & ((ii ^ jj) < ck), 1.0, 0.0)
        gcum = _dot(tri, g, precision=HI)
        aux_ref[...] = g + beta + pltpu.roll(gcum, 2 * nh, 1)

    row = lambda c: pl.BlockSpec((tm, c), lambda i: (i, 0))
    vec = lambda r, c: pl.BlockSpec((r, c), lambda i: (0, 0))
    return pl.pallas_call(
        body, grid=(t // tm,),
        in_specs=[row(cw), pl.BlockSpec((HALO, cw), lambda i: (jnp.maximum(i * hb - 1, 0), 0)), vec(GDN_CONV, cw),
                  pl.BlockSpec((tm, LANES), lambda i: (i, ab_blk)), vec(1, LANES), vec(1, LANES)],
        out_specs=[row(cw), row(cw), row(LANES)],
        out_shape=[jax.ShapeDtypeStruct((t, cw), F32), jax.ShapeDtypeStruct((t, cw), F32),
                   jax.ShapeDtypeStruct((t, LANES), F32)],
        compiler_params=_cp(("parallel",)), name=name,
    )(qkvb, qkvb, conv_w, ab, alog_pad, dt_pad)


GDN_STEP_CHUNKS = 8
GDN_ILP_CHUNKS = 4
GDN_ILP_CHUNKS_BWD = 4


def _bdot(a, b, dims=NN):
    return _dot(a.astype(BF16), b.astype(BF16), dims)


def _split_bf16(x):
    hi = x.astype(BF16)
    return hi, (x - hi.astype(F32)).astype(BF16)


def _dot3(a, b, dims=NN):
    ah, al = _split_bf16(a)
    bh, bl = _split_bf16(b)
    return _dot(ah, bh, dims) + (_dot(ah, bl, dims) + _dot(al, bh, dims))


def _dot3_many(lhs, rhs, dims=NN):
    sa = [_split_bf16(a) for a in lhs]
    sb = [_split_bf16(b) for b in rhs]
    hh = [_dot(a[0], b[0], dims) for a, b in zip(sa, sb)]
    hl = [_dot(a[0], b[1], dims) for a, b in zip(sa, sb)]
    lh = [_dot(a[1], b[0], dims) for a, b in zip(sa, sb)]
    return [x + (y + z) for x, y, z in zip(hh, hl, lh)]


def _gdn_local(chains, with_inverse):
    ck = GDN_CHUNK
    ii = lax.broadcasted_iota(jnp.int32, (ck, ck), 0)
    jj = lax.broadcasted_iota(jnp.int32, (ck, ck), 1)
    lower, strict = ii >= jj, ii > jj
    dmat = [jnp.exp(jnp.where(lower, gc - gc_row, -jnp.inf)) for _, _, _, gc, gc_row in chains]
    kk = [_bdot(k, k, NT) for _, k, _, _, _ in chains]
    qk = [_bdot(q, k, NT) for q, k, _, _, _ in chains]
    tinv = [None] * len(chains)
    if with_inverse:
        lmat = [jnp.where(strict, c[2] * kk_i * d_i, 0.0) for c, kk_i, d_i in zip(chains, kk, dmat)]
        eye = jnp.where(ii == jj, 1.0, 0.0)
        tinv = [eye - l_i for l_i in lmat]
        pw = lmat
        for _ in range(int(math.log2(ck)) - 1):
            pw = _dot3_many(pw, pw)
            tinv = [t_i + d_i for t_i, d_i in zip(tinv, _dot3_many(tinv, pw))]
    out = []
    for (q, k, b, gc, gc_row), dmat_i, kk_i, qk_i, tinv_i in zip(chains, dmat, kk, qk, tinv):
        gl = gc[ck - 1:ck, :]
        out.append(dict(lower=lower, strict=strict, dmat=dmat_i, kk=kk_i, tinv=tinv_i, gam=jnp.exp(gc), qk=qk_i,
                        mm=qk_i * dmat_i, kdec=jnp.exp(gl - gc)))
    return out


def _gdn_head_cols(h):
    return slice(h * GDN_HEAD_DIM, (h + 1) * GDN_HEAD_DIM)


def _gdn_chunk_inputs(x_ref, aux_ref, auxt_ref, g, h):
    nh, ck = GDN_HEADS, GDN_CHUNK
    gw = nh * GDN_HEAD_DIM
    rows = slice(g * ck, (g + 1) * ck)
    cols = _gdn_head_cols(h)
    q = x_ref[rows, cols]
    k = x_ref[rows, gw + cols.start:gw + cols.stop]
    v = x_ref[rows, 2 * gw + cols.start:2 * gw + cols.stop]
    b = aux_ref[rows, nh + h:nh + h + 1]
    gc = aux_ref[rows, 2 * nh + h:2 * nh + h + 1]
    gc_row = auxt_ref[g, 2 * nh + h:2 * nh + h + 1, :]
    return q, k, v, b, gc, gc_row


def _gdn_specs(t, widths, *, reverse=False, step_chunks=None):
    rows = (step_chunks or GDN_STEP_CHUNKS) * GDN_CHUNK
    nsteps = t // rows
    idx = (lambda i: (nsteps - 1 - i, 0)) if reverse else (lambda i: (i, 0))
    return [pl.BlockSpec((rows, w), idx) for w in widths]


def _gdn_local_fwd(qkvn, aux, aux_t, *, name, exchange=None):
    t = qkvn.shape[0]
    hd, nh, ck, gs = GDN_HEAD_DIM, GDN_HEADS, GDN_CHUNK, GDN_STEP_CHUNKS
    gw = nh * hd
    host = _ExchangeHost(exchange)
    assert not (exchange and exchange.aliases)
    grid = (t // (gs * ck),)

    def body(*refs):
        x_ref, aux_ref, auxt_ref = refs[:3]
        u_ref, w_ref, qd_ref, kd_ref, mm_ref, tinv_ref = refs[3 + host.n_in:9 + host.n_in]
        host.start(refs, 3, 9 + host.n_in, grid)
        for g0 in range(0, gs, GDN_ILP_CHUNKS):
            where = [(g, h) for g in range(g0, g0 + GDN_ILP_CHUNKS) for h in range(nh)]
            ins = [_gdn_chunk_inputs(x_ref, aux_ref, auxt_ref, g, h) for g, h in where]
            lcs = _gdn_local([(q, k, b, gc, gc_row) for q, k, _, b, gc, gc_row in ins], True)
            tinvs = [lc["tinv"] for lc in lcs]
            us = _dot3_many(tinvs, [b * v for _, _, v, b, _, _ in ins])
            ws = _dot3_many(tinvs, [(b * lc["gam"]) * k for (_, k, _, b, _, _), lc in zip(ins, lcs)])
            for i, ((g, h), (q, k, _, _, _, _), lc) in enumerate(zip(where, ins, lcs)):
                rows, cols = slice(g * ck, (g + 1) * ck), _gdn_head_cols(h)
                u_ref[rows, cols] = us[i]
                w_ref[rows, cols] = ws[i].astype(w_ref.dtype)
                qd_ref[rows, cols] = (lc["gam"] * q).astype(qd_ref.dtype)
                kd_ref[rows, cols] = (lc["kdec"] * k).astype(kd_ref.dtype)
            for g in range(g0, g0 + GDN_ILP_CHUNKS):
                rows = slice(g * ck, (g + 1) * ck)
                mine = [lc for (gg, _), lc in zip(where, lcs) if gg == g]
                mm_ref[rows, :] = jnp.concatenate([lc["mm"] for lc in mine], axis=1).astype(mm_ref.dtype)
                tinv_ref[rows, :] = jnp.concatenate([lc["tinv"] for lc in mine], axis=1)
        host.wait(refs, 3, 9 + host.n_in, grid)

    sq = nh * ck
    outs = pl.pallas_call(
        body, grid=grid,
        in_specs=_gdn_specs(t, (3 * gw, LANES)) + [pl.BlockSpec((gs, 16, ck), lambda i: (i, 0, 0))] + host.in_specs,
        out_specs=_gdn_specs(t, (gw, gw, gw, gw, sq, sq)) + host.out_specs,
        out_shape=[jax.ShapeDtypeStruct((t, gw), F32)] + [jax.ShapeDtypeStruct((t, gw), BF16)] * 3
        + [jax.ShapeDtypeStruct((t, sq), BF16), jax.ShapeDtypeStruct((t, sq), F32)] + host.out_shapes,
        scratch_shapes=host.scratch,
        compiler_params=_cp(host.semantics(("parallel",))), name=name,
    )(qkvn, aux, aux_t, *host.ins)
    return (*outs[:6], list(outs[6:])) if exchange is not None else outs


def _gdn_seq_fwd(u, w, qd, kd, mm, aux, *, name):
    t = u.shape[0]
    hd, nh, ck, gs = GDN_HEAD_DIM, GDN_HEADS, GDN_CHUNK, GDN_STEP_CHUNKS
    gw = nh * hd
    sq = nh * ck

    def body(u_ref, w_ref, qd_ref, kd_ref, mm_ref, aux_ref, o_ref, vn_ref, sall_ref, s_ref):
        @pl.when(pl.program_id(0) == 0)
        def _():
            s_ref[...] = jnp.zeros_like(s_ref)

        heads = range(nh)
        hcols = [_gdn_head_cols(h) for h in heads]
        sts = [s_ref[h] for h in heads]
        for g in range(gs):
            rows = slice(g * ck, (g + 1) * ck)
            last = (g + 1) * ck - 1
            for h in heads:
                sall_ref[g, h] = sts[h]
            stbs = [st.astype(BF16) for st in sts]
            w_s = [_dot(w_ref[rows, c], stb) for c, stb in zip(hcols, stbs)]
            q_s = [_dot(qd_ref[rows, c], stb) for c, stb in zip(hcols, stbs)]
            vnbs = [(u_ref[rows, c] - ws).astype(BF16) for c, ws in zip(hcols, w_s)]
            m_v = [_dot(mm_ref[rows, h * ck:(h + 1) * ck], vnbs[h]) for h in heads]
            k_v = [_dot(kd_ref[rows, c], vnb, TN) for c, vnb in zip(hcols, vnbs)]
            for h, c in zip(heads, hcols):
                vn_ref[rows, c] = vnbs[h]
                o_ref[rows, c] = q_s[h] + m_v[h]
            gam_c = [jnp.exp(aux_ref[last:last + 1, 2 * nh + h:2 * nh + h + 1]) for h in heads]
            sts = [gam_c[h] * sts[h] + k_v[h] for h in heads]
        for h in heads:
            s_ref[h] = sts[h]

    return pl.pallas_call(
        body, grid=(t // (gs * ck),),
        in_specs=_gdn_specs(t, (gw, gw, gw, gw, sq, LANES)),
        out_specs=_gdn_specs(t, (gw, gw)) + [pl.BlockSpec((gs, nh, hd, hd), lambda i: (i, 0, 0, 0))],
        out_shape=[jax.ShapeDtypeStruct((t, gw), F32), jax.ShapeDtypeStruct((t, gw), BF16),
                   jax.ShapeDtypeStruct((t // ck, nh, hd, hd), F32)],
        scratch_shapes=[pltpu.VMEM((nh, hd, hd), F32)],
        compiler_params=_cp(("arbitrary",)), name=name,
    )(u, w, qd, kd, mm, aux)


def _gdn_seq_bwd(do, w, qd, kd, mm, vn, s_all, aux, *, name):
    t = do.shape[0]
    hd, nh, ck, gs = GDN_HEAD_DIM, GDN_HEADS, GDN_CHUNK, GDN_STEP_CHUNKS
    gw = nh * hd
    sq = nh * ck
    nsteps = t // (gs * ck)

    def body(do_ref, w_ref, qd_ref, kd_ref, mm_ref, vn_ref, sall_ref, aux_ref, dvn_ref, dqd_ref, dkd_ref, dw_ref,
             dlast_ref, ds_ref):
        @pl.when(pl.program_id(0) == 0)
        def _():
            ds_ref[...] = jnp.zeros_like(ds_ref)

        lane = lax.broadcasted_iota(jnp.int32, (ck, LANES), 1)
        rowi = lax.broadcasted_iota(jnp.int32, (ck, LANES), 0)
        heads = range(nh)
        hcols = [_gdn_head_cols(h) for h in heads]
        dsns = [ds_ref[h] for h in heads]
        for g in reversed(range(gs)):
            rows = slice(g * ck, (g + 1) * ck)
            last = (g + 1) * ck - 1
            sts = [sall_ref[g, h] for h in heads]
            stbs = [st.astype(BF16) for st in sts]
            dsbs = [dsn.astype(BF16) for dsn in dsns]
            dobs = [do_ref[rows, c].astype(BF16) for c in hcols]
            dvns = [_dot(mm_ref[rows, h * ck:(h + 1) * ck], dobs[h], TN) + _dot(kd_ref[rows, hcols[h]], dsbs[h])
                    for h in heads]
            dqds = [_dot(dob, stb, NT) for dob, stb in zip(dobs, stbs)]
            dkds = [_dot(vn_ref[rows, c], dsb, NT) for c, dsb in zip(hcols, dsbs)]
            q_o = [_dot(qd_ref[rows, c], dob, TN) for c, dob in zip(hcols, dobs)]
            dvbs = [dvn.astype(BF16) for dvn in dvns]
            dws = [_dot(dvb, stb, NT) for dvb, stb in zip(dvbs, stbs)]
            w_v = [_dot(w_ref[rows, c], dvb, TN) for c, dvb in zip(hcols, dvbs)]
            gam_c = [jnp.exp(aux_ref[last:last + 1, 2 * nh + h:2 * nh + h + 1]) for h in heads]
            dlast = jnp.zeros((ck, LANES), F32)
            for h, c in zip(heads, hcols):
                dvn_ref[rows, c] = dvns[h]
                dqd_ref[rows, c] = dqds[h]
                dkd_ref[rows, c] = dkds[h]
                dw_ref[rows, c] = -dws[h]
                dgam_c = jnp.sum(jnp.sum(dsns[h] * sts[h], axis=1, keepdims=True), axis=0, keepdims=True)
                dlast = dlast + jnp.where((rowi == ck - 1) & (lane == h), gam_c[h] * dgam_c, 0.0)
            dlast_ref[rows, :] = dlast
            dsns = [q_o[h] + gam_c[h] * dsns[h] - w_v[h] for h in heads]
        for h in heads:
            ds_ref[h] = dsns[h]

    return pl.pallas_call(
        body, grid=(nsteps,),
        in_specs=_gdn_specs(t, (gw, gw, gw, gw, sq, gw), reverse=True)
        + [pl.BlockSpec((gs, nh, hd, hd), lambda i: (nsteps - 1 - i, 0, 0, 0))] + _gdn_specs(t, (LANES,), reverse=True),
        out_specs=_gdn_specs(t, (gw, gw, gw, gw, LANES), reverse=True),
        out_shape=[jax.ShapeDtypeStruct((t, gw), F32)] * 4 + [jax.ShapeDtypeStruct((t, LANES), F32)],
        scratch_shapes=[pltpu.VMEM((nh, hd, hd), F32)],
        compiler_params=_cp(("arbitrary",)), name=name,
    )(do, w, qd, kd, mm, vn, s_all, aux)


def _gdn_local_bwd(qkvn, aux, aux_t, tinv, u, w, vn, do, dvn, dqd, dkd, dw, dlast, *, name):
    t = qkvn.shape[0]
    hd, nh, ck, gs = GDN_HEAD_DIM, GDN_HEADS, GDN_CHUNK, GDN_STEP_CHUNKS
    gw = nh * hd
    sq = nh * ck

    def body(x_ref, aux_ref, auxt_ref, tinv_ref, u_ref, w_ref, vn_ref, do_ref, dvn_ref, dqd_ref, dkd_ref, dw_ref,
             dlast_ref, dx_ref, daux_ref):
        lane = lax.broadcasted_iota(jnp.int32, (ck, LANES), 1)
        ones = jnp.ones((ck, LANES), F32)
        ii = lax.broadcasted_iota(jnp.int32, (ck, ck), 0)
        jj = lax.broadcasted_iota(jnp.int32, (ck, ck), 1)
        suffix = jnp.where(jj >= ii, 1.0, 0.0)
        for g0 in range(0, gs, GDN_ILP_CHUNKS_BWD):
            where = [(g, h) for g in range(g0, g0 + GDN_ILP_CHUNKS_BWD) for h in range(nh)]
            at = [(slice(g * ck, (g + 1) * ck), _gdn_head_cols(h)) for g, h in where]
            ins = [_gdn_chunk_inputs(x_ref, aux_ref, auxt_ref, g, h) for g, h in where]
            lcs = _gdn_local([(q, k, b, gc, gc_row) for q, k, _, b, gc, gc_row in ins], False)
            tinvs = [tinv_ref[slice(g * ck, (g + 1) * ck), h * ck:(h + 1) * ck] for g, h in where]
            dms = [jnp.where(lc["lower"], _bdot(do_ref[r, c], vn_ref[r, c], NT), 0.0) for lc, (r, c) in zip(lcs, at)]
            drvs = _dot3_many(tinvs, [dvn_ref[r, c] for r, c in at], TN)
            drks = _dot3_many(tinvs, [dw_ref[r, c] for r, c in at], TN)
            das = [jnp.where(lc["strict"], -(_bdot(drv, u_ref[r, c], NT) + _bdot(drk, w_ref[r, c], NT)), 0.0)
                   for lc, (r, c), drv, drk in zip(lcs, at, drvs, drks)]
            f_mats = [da * (i[3] * lc["kk"]) * lc["dmat"] + dm * lc["qk"] * lc["dmat"]
                      for i, lc, da, dm in zip(ins, lcs, das, dms)]
            col_sums = _dot3_many(f_mats, [ones] * len(where), TN)
            dgc_all = {g: dlast_ref[slice(g * ck, (g + 1) * ck), :] for g in range(g0, g0 + GDN_ILP_CHUNKS_BWD)}
            db_all = {g: jnp.zeros((ck, LANES), F32) for g in range(g0, g0 + GDN_ILP_CHUNKS_BWD)}
            e_mats = [da * lc["dmat"] * i[3] for i, lc, da in zip(ins, lcs, das)]
            dmds = [dm * lc["dmat"] for lc, dm in zip(lcs, dms)]
            dq_mm = [_bdot(dmd, i[1]) for i, dmd in zip(ins, dmds)]
            dk_mm = [_bdot(e, i[1]) + _bdot(e, i[1], TN) + _bdot(dmd, i[0], TN) for i, e, dmd in zip(ins, e_mats, dmds)]
            for n, ((g, h), (q, k, v, b, _, _), lc, (rows, cols)) in enumerate(zip(where, ins, lcs, at)):
                dmat, kk, gam, kdec = (lc[key] for key in ("dmat", "kk", "gam", "kdec"))
                drv, drk, da = drvs[n], drks[n], das[n]
                dqd_h, dkd_h = dqd_ref[rows, cols], dkd_ref[rows, cols]
                rs_rk = jnp.sum(drk * k, axis=-1, keepdims=True)
                db = (jnp.sum(drv * v, axis=-1, keepdims=True) + gam * rs_rk
                      + jnp.sum(da * kk * dmat, axis=-1, keepdims=True))
                dx_ref[rows, cols] = dq_mm[n] + gam * dqd_h
                dx_ref[rows, gw + cols.start:gw + cols.stop] = (b * gam) * drk + dk_mm[n] + kdec * dkd_h
                dx_ref[rows, 2 * gw + cols.start:2 * gw + cols.stop] = b * drv
                e_vec = jnp.sum(dkd_h * (kdec * k), axis=-1, keepdims=True)
                dgc = (b * gam * rs_rk + gam * jnp.sum(dqd_h * q, axis=-1, keepdims=True)
                       + jnp.sum(f_mats[n], axis=-1, keepdims=True) - col_sums[n][:, 0:1] - e_vec)
                is_last = lax.broadcasted_iota(jnp.int32, (ck, 1), 0) == ck - 1
                dgc = dgc + jnp.where(is_last, jnp.sum(e_vec, axis=0, keepdims=True), 0.0)
                dgc_all[g] = dgc_all[g] + jnp.where(lane == h, dgc, 0.0)
                db_all[g] = db_all[g] + jnp.where(lane == nh + h, db, 0.0)
            for g in dgc_all:
                daux_ref[slice(g * ck, (g + 1) * ck), :] = _dot3(suffix, dgc_all[g]) + db_all[g]

    return pl.pallas_call(
        body, grid=(t // (gs * ck),),
        in_specs=_gdn_specs(t, (3 * gw, LANES)) + [pl.BlockSpec((gs, 16, ck), lambda i: (i, 0, 0))]
        + _gdn_specs(t, (sq, gw, gw, gw, gw, gw, gw, gw, gw, LANES)),
        out_specs=_gdn_specs(t, (3 * gw, LANES)),
        out_shape=[jax.ShapeDtypeStruct((t, 3 * gw), F32), jax.ShapeDtypeStruct((t, LANES), F32)],
        compiler_params=_cp(("parallel",)), name=name,
    )(qkvn, aux, aux_t, tinv, u, w, vn, do, dvn, dqd, dkd, dw, dlast)


def _gdn_pre_bwd1(xc, dqkvn, daux, ab, alog_pad, dt_pad, dkv, dp, dp_col, *, name, ab_blk=0, tm=256):
    t, cw = xc.shape
    hd, nh = GDN_HEAD_DIM, GDN_HEADS
    gw = nh * hd
    tm = _blk(t, tm)

    kvw = dkv.shape[1]
    seg = kvw + AB_PAD
    assert dp_col % seg == 0

    def body(xc_ref, dy_ref, daux_ref, ab_ref, al_ref, dt_ref, dkv_ref, _, dxc_ref, dab_ref, dal_ref, ddt_ref):
        i = pl.program_id(0)
        xc = xc_ref[...]
        sg = _sigmoid(xc)
        s = xc * sg
        dsilu = sg * (1.0 + xc * (1.0 - sg))
        for h in range(2 * nh):
            xh = s[:, h * hd:(h + 1) * hd]
            scale = hd ** -0.5 if h < nh else 1.0
            dyh = dy_ref[:, h * hd:(h + 1) * hd] * scale
            r = lax.rsqrt(jnp.sum(xh * xh, axis=-1, keepdims=True) + L2_EPS)
            dxh = r * dyh - xh * (r * r * r) * jnp.sum(dyh * xh, axis=-1, keepdims=True)
            dxc_ref[:, h * hd:(h + 1) * hd] = dxh * dsilu[:, h * hd:(h + 1) * hd]
        dxc_ref[:, 2 * gw:] = dy_ref[:, 2 * gw:] * dsilu[:, 2 * gw:]
        abv = ab_ref[...]
        dauxv = daux_ref[...]
        lane = lax.broadcasted_iota(jnp.int32, abv.shape, 1)
        is_a = lane < nh
        is_b = (lane >= nh) & (lane < 2 * nh)
        pre = abv + dt_ref[...]
        neg_ea = -jnp.exp(al_ref[...])
        d_a = jnp.where(is_a, dauxv * neg_ea * _sigmoid(pre), 0.0)
        beta = _sigmoid(abv)
        d_b = jnp.where(is_b, dauxv * beta * (1.0 - beta), 0.0)
        dab_ref[:, :kvw] = dkv_ref[...]
        dab_ref[:, kvw:kvw + LANES] = (d_a + d_b).astype(dab_ref.dtype)
        dab_ref[:, kvw + LANES:] = jnp.zeros((tm, AB_PAD - LANES), dab_ref.dtype)
        dal = jnp.sum(jnp.where(is_a, dauxv * neg_ea * _softplus(pre), 0.0), axis=0, keepdims=True)
        ddt = jnp.sum(d_a, axis=0, keepdims=True)

        @pl.when(i == 0)
        def _():
            dal_ref[...] = dal
            ddt_ref[...] = ddt

        @pl.when(i > 0)
        def _():
            dal_ref[...] += dal
            ddt_ref[...] += ddt

    row = lambda c: pl.BlockSpec((tm, c), lambda i: (i, 0))
    vec = pl.BlockSpec((1, LANES), lambda i: (0, 0))
    return pl.pallas_call(
        body, grid=(t // tm,),
        in_specs=[row(cw), row(cw), row(LANES), pl.BlockSpec((tm, LANES), lambda i: (i, ab_blk)), vec, vec, row(kvw),
                  ANY_SPEC],
        out_specs=[row(cw), pl.BlockSpec((tm, seg), lambda i: (i, dp_col // seg)), vec, vec],
        out_shape=[jax.ShapeDtypeStruct((t, cw), F32), jax.ShapeDtypeStruct(dp.shape, dp.dtype),
                   jax.ShapeDtypeStruct((1, LANES), F32), jax.ShapeDtypeStruct((1, LANES), F32)],
        input_output_aliases={7: 1}, compiler_params=_cp(("arbitrary",)), name=name,
    )(xc, dqkvn, daux, ab, alog_pad, dt_pad, dkv, dp)


def _gdn_pre_bwd2(dxc, qkvb, conv_w, dp, dp_col, *, name, tm=512):
    t, cw = dxc.shape
    tm = _blk(t, tm)
    hb = tm // HALO
    nblk = t // tm
    cg = GDN_HEADS * GDN_HEAD_DIM
    assert cw % cg == 0 and dp_col % cg == 0
    col0 = dp_col // cg

    def body(d_ref, dn_ref, x_ref, xp_ref, w_ref, _, dx_ref, dw_ref):
        i = pl.program_id(1)
        dcur = d_ref[...]
        dnxt = jnp.where(i < nblk - 1, dn_ref[...], 0.0)
        cur = x_ref[...]
        prev = jnp.where(i > 0, xp_ref[...], 0.0)
        dx = None
        dws = []
        for tap in range(GDN_CONV):
            j = GDN_CONV - 1 - tap
            term = w_ref[tap:tap + 1, :] * _shift_up(dcur, dnxt, j)
            dx = term if dx is None else dx + term
            dws.append(jnp.sum(dcur * _shift_down(cur, prev, j), axis=0, keepdims=True))
        dx_ref[...] = dx.astype(dx_ref.dtype)
        dw = jnp.concatenate(dws, axis=0)

        @pl.when(i == 0)
        def _():
            dw_ref[...] = dw

        @pl.when(i > 0)
        def _():
            dw_ref[...] += dw

    row = pl.BlockSpec((tm, cg), lambda c, i: (i, c))
    wsp = pl.BlockSpec((GDN_CONV, cg), lambda c, i: (0, c))
    return pl.pallas_call(
        body, grid=(cw // cg, nblk),
        in_specs=[row, pl.BlockSpec((HALO, cg), lambda c, i: (jnp.minimum((i + 1) * hb, t // HALO - 1), c)),
                  row, pl.BlockSpec((HALO, cg), lambda c, i: (jnp.maximum(i * hb - 1, 0), c)), wsp, ANY_SPEC],
        out_specs=[pl.BlockSpec((tm, cg), lambda c, i: (i, col0 + c)), wsp],
        out_shape=[jax.ShapeDtypeStruct(dp.shape, dp.dtype), jax.ShapeDtypeStruct((GDN_CONV, cw), F32)],
        input_output_aliases={5: 0}, compiler_params=_cp(("arbitrary", "arbitrary")), name=name,
    )(dxc, dxc, qkvb, qkvb, conv_w, dp)


def _gdn_post_fwd(o, z, norm_w, *, name, tm=512):
    t, gw = o.shape
    hd, nh = GDN_HEAD_DIM, GDN_HEADS
    tm = _blk(t, tm)

    def body(o_ref, z_ref, w_ref, y_ref):
        zv = z_ref[...]
        sz = zv * _sigmoid(zv)
        for h in range(nh):
            oh = o_ref[:, h * hd:(h + 1) * hd]
            r = lax.rsqrt(jnp.mean(oh * oh, axis=-1, keepdims=True) + RMS_EPS)
            y_ref[:, h * hd:(h + 1) * hd] = (oh * r * w_ref[...] * sz[:, h * hd:(h + 1) * hd]).astype(y_ref.dtype)

    row = pl.BlockSpec((tm, gw), lambda i: (i, 0))
    return pl.pallas_call(
        body, grid=(t // tm,), in_specs=[row, row, pl.BlockSpec((1, hd), lambda i: (0, 0))], out_specs=row,
        out_shape=jax.ShapeDtypeStruct((t, gw), BF16), compiler_params=_cp(("parallel",)), name=name,
    )(o, z, norm_w)


def _gdn_post_bwd(dy, o, z, norm_w, dp, dp_col, *, name, tm=512):
    t, gw = o.shape
    hd, nh = GDN_HEAD_DIM, GDN_HEADS
    tm = _blk(t, tm)

    def body(dy_ref, o_ref, z_ref, w_ref, _, do_ref, dz_ref, dw_ref):
        i = pl.program_id(0)
        zv = z_ref[...]
        sg = _sigmoid(zv)
        sz = zv * sg
        dsz = sg * (1.0 + zv * (1.0 - sg))
        dw = None
        for h in range(nh):
            sl = slice(h * hd, (h + 1) * hd)
            oh = o_ref[:, sl]
            dyh = dy_ref[:, sl].astype(F32)
            r = lax.rsqrt(jnp.mean(oh * oh, axis=-1, keepdims=True) + RMS_EPS)
            xh = oh * r
            dz_ref[:, sl] = (dyh * xh * w_ref[...] * dsz[:, sl]).astype(dz_ref.dtype)
            dn = dyh * sz[:, sl]
            dxh = dn * w_ref[...]
            do_ref[:, sl] = r * (dxh - xh * jnp.mean(dxh * xh, axis=-1, keepdims=True))
            part = jnp.sum(dn * xh, axis=0, keepdims=True)
            dw = part if dw is None else dw + part

        @pl.when(i == 0)
        def _():
            dw_ref[...] = dw

        @pl.when(i > 0)
        def _():
            dw_ref[...] += dw

    row = pl.BlockSpec((tm, gw), lambda i: (i, 0))
    vec = pl.BlockSpec((1, hd), lambda i: (0, 0))
    return pl.pallas_call(
        body, grid=(t // tm,), in_specs=[row, row, row, vec, ANY_SPEC],
        out_specs=[row, pl.BlockSpec((tm, gw), lambda i: (i, dp_col // gw)), vec],
        out_shape=[jax.ShapeDtypeStruct((t, gw), F32), jax.ShapeDtypeStruct(dp.shape, dp.dtype),
                   jax.ShapeDtypeStruct((1, hd), F32)],
        input_output_aliases={4: 1}, compiler_params=_cp(("arbitrary",)), name=name,
    )(dy, o, z, norm_w, dp)


IN_NAMES = ("q_a", "kv_a", "qkv_b", "ab", "z", "q_c", "gates")
CAT_NAMES = ("gates", "q_a", "qkv_b", "z", "q_c", "kv_a", "ab")
AB_PAD = 256


def _in_widths(d):
    gw = GDN_HEADS * GDN_HEAD_DIM
    return dict(q_a=SWA_Q_HEADS * SWA_HEAD_DIM, kv_a=2 * SWA_KV_HEADS * SWA_HEAD_DIM, qkv_b=3 * gw, ab=2 * GDN_HEADS,
                z=gw, q_c=XA_HEADS * XA_HEAD_DIM, gates=3 * d)


def _ranges(names, widths):
    out, start = {}, 0
    for k in names:
        out[k] = (start, widths[k])
        start += widths[k]
    return out, start


def _cat_ranges(d):
    widths = dict(_in_widths(d), ab=AB_PAD)
    return _ranges(CAT_NAMES, widths)


def _to_cat(shards, *, name="to_cat", tm=256):
    ns, d, n = shards.shape
    src, _ = _ranges(IN_NAMES, _in_widths(d))
    _, cat_w = _cat_ranges(d)
    pieces = []
    for k in CAT_NAMES:
        lo, hi = src[k][0], src[k][0] + src[k][1]
        for s in range(ns):
            a, b = max(lo, s * n), min(hi, (s + 1) * n)
            if a < b:
                pieces.append((s, a - s * n, b - s * n))
    tm = _blk(d, tm)

    def body(s_ref, o_ref):
        cols = [s_ref[s, :, a:b] for s, a, b in pieces]
        cols.append(jnp.zeros((tm, AB_PAD - src["ab"][1]), o_ref.dtype))
        o_ref[...] = jnp.concatenate(cols, axis=1)

    return pl.pallas_call(
        body, grid=(d // tm,),
        in_specs=[pl.BlockSpec((ns, tm, n), lambda i: (0, i, 0))],
        out_specs=pl.BlockSpec((tm, cat_w), lambda i: (i, 0)),
        out_shape=jax.ShapeDtypeStruct((d, cat_w), shards.dtype),
        compiler_params=_cp(("parallel",)), name=name,
    )(shards)


def _from_cat(w_cat, *, name="from_cat", tm=256):
    d, cat_w = w_cat.shape
    src, total = _ranges(IN_NAMES, _in_widths(d))
    cat, _ = _cat_ranges(d)
    n = total // N_SHARDS
    pieces = []
    for s in range(N_SHARDS):
        pieces.append([])
        for k in IN_NAMES:
            a, b = max(s * n, src[k][0]), min((s + 1) * n, src[k][0] + src[k][1])
            if a < b:
                pieces[s].append((cat[k][0] + a - src[k][0], cat[k][0] + b - src[k][0]))
    tm = _blk(d, tm)

    def body(c_ref, o_ref):
        for s in range(N_SHARDS):
            o_ref[s] = jnp.concatenate([c_ref[:, a:b] for a, b in pieces[s]], axis=1)

    return pl.pallas_call(
        body, grid=(d // tm,),
        in_specs=[pl.BlockSpec((tm, cat_w), lambda i: (i, 0))],
        out_specs=pl.BlockSpec((N_SHARDS, tm, n), lambda i: (0, i, 0)),
        out_shape=jax.ShapeDtypeStruct((N_SHARDS, d, n), w_cat.dtype),
        compiler_params=_cp(("parallel",)), name=name,
    )(w_cat)


def _pad_cols(a, width):
    return jnp.pad(a, ((0, 0), (0, width - a.shape[1])))


def _relu2_epilogue(acc):
    r = jnp.maximum(acc, 0.0)
    return acc, r * r


def _add_epilogue(acc, res):
    return (acc + res,)


def _add_norm_epilogue(acc, res, gain):
    h = acc + res
    r = lax.rsqrt(jnp.mean(h * h, axis=-1, keepdims=True) + RMS_EPS)
    return h, h * r * gain


def _drelu2_epilogue(acc, u):
    return (acc * (2.0 * jnp.maximum(u.astype(F32), 0.0)),)


def _local_step(x, mem, tgt, wts, small, comm=None):
    t, d = x.shape
    nh = GDN_HEADS
    cat, cat_w = _cat_ranges(d)
    alog_pad = _pad_cols(small["a_log"], LANES)
    dt_pad = _pad_cols(small["dt_bias"], LANES)
    kvw = cat["kv_a"][1]
    assert cat["ab"][0] == cat["kv_a"][0] + kvw
    ab_blk = kvw // LANES

    if comm is None:
        n = _rms_fwd(x, small["g_mix"], name="rms_mix")
        w_cat = wts["w_cat"]
    else:
        n, landed = _rms_fwd(x, small["g_mix"], name="rms_mix", exchange=comm.gather_exchange(["w_in"]))
        w_cat = _to_cat(_exchange_call(comm.pass_on(["w_in"], landed), name="ag_w_in_pass")[0])
    assert w_cat.shape == (d, cat_w)
    q_a = _mm(n, w_cat, b_window=cat["q_a"], out_dtypes=(BF16,), name="in_q_a")
    kv_a, ab = _mm(n, w_cat, b_window=(cat["kv_a"][0], kvw + AB_PAD), out_dtypes=(BF16, F32), name="in_kv_ab")
    qkvb = _mm(n, w_cat, b_window=cat["qkv_b"], tn=512, name="in_qkv_b")
    z = _mm(n, w_cat, b_window=cat["z"], name="in_z")
    q_c = _mm(n, w_cat, b_window=cat["q_c"], out_dtypes=(BF16,), name="in_q_c")
    if comm is None:
        gates = _mm(n, w_cat, b_window=cat["gates"], name="in_gates")
        y_a, lse = _swa_fwd(q_a, kv_a, small["sinks"], name="swa_fwd")
    else:
        gates, landed_mlp = _mm(n, w_cat, b_window=cat["gates"], name="in_gates",
                                exchange=comm.gather_exchange(comm.MLP[1:]))
        y_a, lse, landed = _swa_fwd(q_a, kv_a, small["sinks"], name="swa_fwd", exchange=comm.gather_exchange(comm.MLP[:1]))
        landed_mlp = landed + landed_mlp
    xc, qkvn, aux = _gdn_pre_fwd(qkvb, small["conv_w"], ab, alog_pad, dt_pad, ab_blk=ab_blk, name="gdn_pre_fwd")
    aux_t = aux[:, :16].reshape(t // GDN_CHUNK, GDN_CHUNK, 16).transpose(0, 2, 1)
    if comm is None:
        gdn_u, gdn_w, gdn_qd, gdn_kd, gdn_mm, gdn_tinv = _gdn_local_fwd(qkvn, aux, aux_t, name="gdn_local_fwd")
    else:
        gdn_u, gdn_w, gdn_qd, gdn_kd, gdn_mm, gdn_tinv, landed_mid = _gdn_local_fwd(
            qkvn, aux, aux_t, name="gdn_local_fwd", exchange=comm.gather_exchange(comm.mid))
        wts = dict(wts, **comm.gathered(comm.mid, landed_mid, "mid"))
    o_b, gdn_vn, s_all = _gdn_seq_fwd(gdn_u, gdn_w, gdn_qd, gdn_kd, gdn_mm, aux, name="gdn_seq_fwd")
    y_b = _gdn_post_fwd(o_b, z, small["gdn_norm_w"], name="gdn_post_fwd")
    nmem = _rms_fwd(mem, small["g_mem"], name="rms_mem")
    mkv = _mm(nmem, wts["w_mem_kv"], out_dtypes=(BF16,), name="mem_kv")
    y_c = _xa_fwd(q_c, mkv, name="xa_fwd")
    ys = (y_a, y_b, y_c)
    w_ups = (wts["w_swa_up"], wts["w_gdn_up"], wts["w_xa_up"])
    merged = _merge_fwd(ys, w_ups, gates, name="merge_fwd")
    proj = dict(extras=(x, small["g_mlp"]), epilogue=_add_norm_epilogue, out_dtypes=(F32, BF16), tm=512, tn=d,
                name="out_proj")
    if comm is None:
        h1, n2 = _mm(merged, wts["w_out"], **proj)
    else:
        (h1, n2), whole = _mm(merged, wts["w_out"], exchange=comm.pass_on(comm.MLP, landed_mlp), **proj)
        wts = dict(wts, **comm.as_weights(comm.MLP, whole))
    u, act = _mm(n2, wts["w_mlp_in"], b_sharded=True, out_dtypes=(BF16, BF16), epilogue=_relu2_epilogue, name="mlp_in")
    h2 = _mm(act, wts["w_mlp_out"], extras=(h1,), epilogue=_add_epilogue, name="mlp_out")
    dh2, dh2_b, dg_final, loss = _final_loss(h2, small["g_final"], tgt, name="final_loss")

    grads = {"g_final": dg_final}
    du = _mm(dh2_b, wts["w_mlp_out"], tb=True, out_dtypes=(BF16,), extras=(u,), epilogue=_drelu2_epilogue, name="d_mlp_act")
    grads["w_mlp_out"] = _mm(act, dh2_b, ta=True, out_dtypes=(BF16,), name="dw_mlp_out")
    grads["w_mlp_in"] = _mm(n2, du, ta=True, out_sharded=True, out_dtypes=(BF16,), name="dw_mlp_in")
    if comm is None:
        dn2 = _mm(du, wts["w_mlp_in"], tb=True, b_sharded=True, name="d_mlp_in")
    else:
        g_mlp = [comm.shard_major(k, grads.pop(k)) for k in comm.MLP]
        dn2, sib_mlp = _mm(du, wts["w_mlp_in"], tb=True, b_sharded=True, name="d_mlp_in", exchange=_sibling_halves(g_mlp))
        s1_mlp = comm.pair_sums(g_mlp, "mlp", sib_mlp)
    dh1, dh1_b, grads["g_mlp"] = _rms_bwd(dn2, h1, small["g_mlp"], dh2, name="rms_mlp_bwd")
    dmerged = _mm(dh1_b, wts["w_out"], tb=True, name="d_out_proj")
    grads["w_out"] = _mm(merged, dh1_b, ta=True, out_dtypes=(BF16,), name="dw_out")
    *dus, dp = _merge_bwd(ys, w_ups, gates, dmerged, cat_w, name="merge_bwd")
    dys = []
    for y, du_i, w_up, key in zip(ys, dus, w_ups, ("w_swa_up", "w_gdn_up", "w_xa_up")):
        dys.append(_mm(du_i, w_up, tb=True, b_sharded=True, out_dtypes=(BF16,), name="d_" + key))
        grads[key] = _mm(y, du_i, ta=True, out_sharded=True, out_dtypes=(BF16,), name="dw_" + key[2:])
    dp, dkv_a, grads["sinks"] = _swa_bwd(q_a, kv_a, small["sinks"], y_a, lse, dys[0], dp, cat["q_a"][0], name="swa_bwd")
    do_b, dp, grads["gdn_norm_w"] = _gdn_post_bwd(dys[1], o_b, z, small["gdn_norm_w"], dp, cat["z"][0],
                                                  name="gdn_post_bwd")
    dvn, dqd, dkd, dw_, dlast = _gdn_seq_bwd(do_b, gdn_w, gdn_qd, gdn_kd, gdn_mm, gdn_vn, s_all, aux, name="gdn_seq_bwd")
    dqkvn, daux = _gdn_local_bwd(qkvn, aux, aux_t, gdn_tinv, gdn_u, gdn_w, gdn_vn, do_b, dvn, dqd, dkd, dw_, dlast,
                                 name="gdn_local_bwd")
    dxc, dp, dalog, ddt = _gdn_pre_bwd1(xc, dqkvn, daux, ab, alog_pad, dt_pad, dkv_a, dp, cat["kv_a"][0], ab_blk=ab_blk,
                                        name="gdn_pre_bwd1")
    grads["a_log"], grads["dt_bias"] = dalog[:, :nh], ddt[:, :nh]
    dp, grads["conv_w"] = _gdn_pre_bwd2(dxc, qkvb, small["conv_w"], dp, cat["qkv_b"][0], name="gdn_pre_bwd2")
    dp, dmkv = _xa_bwd(q_c, mkv, dys[2], dp, cat["q_c"][0], name="xa_bwd")
    grads["w_mem_kv"] = _mm(nmem, dmkv, ta=True, out_dtypes=(BF16,), name="dw_mem_kv")
    dnmem = _mm(dmkv, wts["w_mem_kv"], tb=True, name="d_mem_kv")
    _, _, grads["g_mem"] = _rms_bwd(dnmem, mem, small["g_mem"], jnp.zeros_like(mem), name="rms_mem_bwd")
    if comm is None:
        grads["w_cat"] = _mm(n, dp, ta=True, out_dtypes=(BF16,), name="dw_in")
        dn = _mm(dp, w_cat, tb=True, name="d_in_proj")
    else:
        s1_mid = comm.pair_sums([comm.shard_major(k, grads.pop(k)) for k in comm.mid], "mid")
        dw_cat, rcv_mlp = _mm(n, dp, ta=True, out_dtypes=(BF16,), name="dw_in", exchange=_chip_exchange(s1_mlp))
        s1_in = comm.pair_sums([_from_cat(dw_cat)], "in")
        dn, rcv_rest = _mm(dp, w_cat, tb=True, name="d_in_proj", exchange=_chip_exchange(s1_in + s1_mid))
        halves = comm.chip_sums(s1_in + s1_mid + s1_mlp, rcv_rest + rcv_mlp)
        reduced = _exchange_call(_join_halves(halves), name="rs_join_halves")
        grads.update(zip(["w_in"] + comm.mid + list(comm.MLP), reduced))
    dx, _, grads["g_mix"] = _rms_bwd(dn, x, small["g_mix"], dh1, name="rms_mix_bwd")
    return loss, dx, grads


HBM_SPEC = pl.BlockSpec(memory_space=pltpu.HBM)
VMEM_SPEC = pl.BlockSpec(memory_space=pltpu.VMEM)
N_CHIPS = N_SHARDS
N_DEV = 8
DMA_CHUNK_BYTES = 1 << 20


def _place():
    return lax.axis_index("x"), lax.axis_index("y"), lax.axis_index("c")


def _other_chips(x, y):
    return [(1 - x, y), (x, 1 - y), (1 - x, 1 - y)]


def _n_chunks(rows, row_bytes):
    n = 1
    while rows % (2 * n) == 0 and (rows // (2 * n)) % 16 == 0 and (rows // n) * row_bytes > DMA_CHUNK_BYTES:
        n *= 2
    return n


def _sem_scratch(n_remote, n_local):
    return [pltpu.SemaphoreType.DMA((max(n_remote, 1),)), pltpu.SemaphoreType.DMA((max(n_remote, 1),)),
            pltpu.SemaphoreType.DMA((max(n_local, 1),))]


def _gather_over_ici(shards):
    plan = _half_chunks(shards, 0)

    def copies_of(in_refs, out_refs, place):
        x, y, c = place
        remote = []
        for i, r0, nr in plan:
            mine = pl.ds(c * (shards[i].shape[0] // 2) + r0, nr)
            for chip in _other_chips(x, y):
                remote.append((in_refs[i].at[mine], out_refs[i].at[2 * x + y, mine], (*chip, c)))
        return remote, []

    shapes = tuple(jax.ShapeDtypeStruct((N_CHIPS, *s.shape), s.dtype) for s in shards)
    return Exchange(tuple(shards), shapes, 3 * len(plan), 0, copies_of)


def _gather_pass_on(arrived):
    plan = _half_chunks([jax.ShapeDtypeStruct(a.shape[1:], a.dtype) for a in arrived], 0)

    def copies_of(in_refs, out_refs, place):
        x, y, c = place
        remote = []
        for i, r0, nr in plan:
            mine = pl.ds(c * (arrived[i].shape[1] // 2) + r0, nr)
            for chip in _other_chips(x, y):
                rows = out_refs[i].at[2 * chip[0] + chip[1], mine]
                remote.append((rows, rows, (x, y, 1 - c)))
        return remote, []

    shapes = tuple(jax.ShapeDtypeStruct(a.shape, a.dtype) for a in arrived)
    return Exchange(tuple(arrived), shapes, 3 * len(plan), 0, copies_of, tuple((i, i) for i in range(len(arrived))))


def _place_own(arrived, shard, chip, *, name):
    r, c = shard.shape
    tb = _row_block(r, c, shard.dtype.itemsize)

    def body(chip_ref, s_ref, _, o_ref):
        o_ref[...] = s_ref[...]

    return pl.pallas_call(
        body, grid_spec=pltpu.PrefetchScalarGridSpec(
            num_scalar_prefetch=1, grid=(r // tb,),
            in_specs=[pl.BlockSpec((tb, c), lambda i, chip_ref: (i, 0)), ANY_SPEC],
            out_specs=pl.BlockSpec((None, tb, c), lambda i, chip_ref: (chip_ref[0], i, 0))),
        out_shape=jax.ShapeDtypeStruct(arrived.shape, arrived.dtype), input_output_aliases={2: 0},
        compiler_params=_cp(("parallel",)), name=name,
    )(chip, shard, arrived)


def _exchange_call(ex, *, name):
    n_in, n_out = len(ex.ins), len(ex.out_shapes)

    def body(*refs):
        cps = _exchange_copies(ex, refs[:n_in], refs[n_in:n_in + n_out], refs[n_in + n_out:])
        for cp in cps:
            cp.start()
        for cp in cps:
            cp.wait()

    return pl.pallas_call(
        body, out_shape=list(ex.out_shapes), in_specs=[HBM_SPEC] * n_in, out_specs=[HBM_SPEC] * n_out,
        scratch_shapes=_sem_scratch(ex.n_remote, ex.n_local), input_output_aliases=dict(ex.aliases), name=name,
    )(*ex.ins)


def _half_chunks(arrs, row_axis):
    plan = []
    for i, a in enumerate(arrs):
        rh = a.shape[row_axis] // 2
        row_bytes = a.dtype.itemsize * math.prod(a.shape) // a.shape[row_axis]
        nch = _n_chunks(rh, row_bytes)
        plan += [(i, q * (rh // nch), rh // nch) for q in range(nch)]
    return plan


def _sibling_halves(gs):
    plan = _half_chunks(gs, 1)

    def copies_of(in_refs, out_refs, place):
        x, y, c = place
        out = []
        for i, r0, nr in plan:
            rh = gs[i].shape[1] // 2
            out.append((in_refs[i].at[:, pl.ds((1 - c) * rh + r0, nr), :], out_refs[i].at[:, pl.ds(r0, nr), :],
                        (x, y, 1 - c)))
        return out, []

    shapes = tuple(jax.ShapeDtypeStruct((g.shape[0], g.shape[1] // 2, g.shape[2]), g.dtype) for g in gs)
    return Exchange(tuple(gs), shapes, len(plan), 0, copies_of)


def _chip_exchange(s1s):
    plan = _half_chunks([jax.ShapeDtypeStruct((2 * s.shape[1], s.shape[2]), s.dtype) for s in s1s], 0)

    def copies_of(in_refs, out_refs, place):
        x, y, c = place
        out = []
        for i, r0, nr in plan:
            for j, chip in enumerate(_other_chips(x, y)):
                out.append((in_refs[i].at[2 * chip[0] + chip[1], pl.ds(r0, nr), :], out_refs[i].at[j, pl.ds(r0, nr), :],
                            (*chip, c)))
        return out, []

    shapes = tuple(jax.ShapeDtypeStruct((3, *s.shape[1:]), s.dtype) for s in s1s)
    return Exchange(tuple(s1s), shapes, 3 * len(plan), 0, copies_of)


def _join_halves(gs):
    plan = _half_chunks(gs, 0)

    def copies_of(in_refs, out_refs, place):
        x, y, c = place
        out = []
        for i, r0, nr in plan:
            rows = out_refs[i].at[pl.ds(c * (gs[i].shape[0] // 2) + r0, nr), :]
            out.append((rows, rows, (x, y, 1 - c)))
        return out, []

    shapes = tuple(jax.ShapeDtypeStruct(g.shape, g.dtype) for g in gs)
    aliases = tuple((i, i) for i in range(len(gs)))
    return Exchange(tuple(gs), shapes, len(plan), 0, copies_of, aliases)


ROW_BLOCK_BYTES = 4 << 20


def _row_block(rows, cols, itemsize=4):
    tb = rows
    while tb % 32 == 0 and tb * cols * itemsize > ROW_BLOCK_BYTES:
        tb //= 2
    return tb


def _pair_sum(g, sib, core, *, name):
    ns, r, c = g.shape
    rh = r // 2
    tb = _row_block(rh, c, g.dtype.itemsize)
    nb = rh // tb

    def body(core_ref, g_ref, s_ref, o_ref):
        o_ref[...] = (g_ref[...].astype(F32) + s_ref[...].astype(F32)).astype(o_ref.dtype)

    mine = pl.BlockSpec((None, tb, c), lambda s, i, core_ref: (s, core_ref[0] * nb + i, 0))
    half = pl.BlockSpec((None, tb, c), lambda s, i, core_ref: (s, i, 0))
    return pl.pallas_call(
        body, grid_spec=pltpu.PrefetchScalarGridSpec(num_scalar_prefetch=1, grid=(ns, nb), in_specs=[mine, half],
                                                     out_specs=half),
        out_shape=jax.ShapeDtypeStruct((ns, rh, c), BF16), compiler_params=_cp(("parallel", "parallel")), name=name,
    )(core, g, sib)


def _chip_sum(s1, rcv, where, *, name):
    _, rh, c = s1.shape
    tb = _row_block(rh, c)
    nb = rh // tb

    def body(where_ref, own_ref, r0_ref, r1_ref, r2_ref, o_ref):
        acc = own_ref[...].astype(F32)
        for r in (r0_ref, r1_ref, r2_ref):
            acc = acc + r[...].astype(F32)
        o_ref[...] = acc

    own = pl.BlockSpec((None, tb, c), lambda i, w: (w[1], i, 0))
    got = [pl.BlockSpec((None, tb, c), functools.partial(lambda i, w, j: (j, i, 0), j=j)) for j in range(3)]
    return pl.pallas_call(
        body, grid_spec=pltpu.PrefetchScalarGridSpec(
            num_scalar_prefetch=1, grid=(nb,), in_specs=[own] + got,
            out_specs=pl.BlockSpec((tb, c), lambda i, w: (w[0] * nb + i, 0))),
        out_shape=jax.ShapeDtypeStruct((2 * rh, c), F32), compiler_params=_cp(("parallel",)), name=name,
    )(where, s1, rcv, rcv, rcv)


def _all_gather_small(blk, *, name):
    r = blk.shape[0]

    def body(b_ref, out_ref, send_sems, recv_sems):
        x, y, c = _place()
        me = 4 * x + 2 * y + c
        out_ref[me] = b_ref[...]
        sends = []
        for k in range(1, N_DEV):
            peer = (x ^ (k >> 2), y ^ ((k >> 1) & 1), c ^ (k & 1))
            sends.append(pltpu.make_async_remote_copy(src_ref=b_ref, dst_ref=out_ref.at[me], send_sem=send_sems.at[k - 1],
                                                      recv_sem=recv_sems.at[k - 1], device_id=peer, device_id_type=MESH))
        for cp in sends:
            cp.start()
        for k in range(1, N_DEV):
            rows = out_ref.at[me ^ k]
            pltpu.make_async_remote_copy(src_ref=rows, dst_ref=rows, send_sem=send_sems.at[k - 1],
                                         recv_sem=recv_sems.at[k - 1], device_id=(x, y, c), device_id_type=MESH).wait_recv()
        for cp in sends:
            cp.wait_send()

    return pl.pallas_call(
        body, out_shape=jax.ShapeDtypeStruct((N_DEV, r, LANES), blk.dtype), in_specs=[VMEM_SPEC], out_specs=VMEM_SPEC,
        scratch_shapes=[pltpu.SemaphoreType.DMA((N_DEV - 1,)), pltpu.SemaphoreType.DMA((N_DEV - 1,))],
        name=name,
    )(blk)


def _sum_rows(parts, out_dtype, *, name, tb=1024):
    rows = parts[0].shape[0]
    tb = _blk(rows, tb)

    def body(*refs):
        acc = refs[0][...].astype(F32)
        for r in refs[1:-1]:
            acc = acc + r[...].astype(F32)
        refs[-1][...] = acc.astype(refs[-1].dtype)

    spec = pl.BlockSpec((tb, LANES), lambda i: (i, 0))
    return pl.pallas_call(
        body, grid=(rows // tb,), in_specs=[spec] * len(parts), out_specs=spec,
        out_shape=jax.ShapeDtypeStruct((rows, LANES), out_dtype), compiler_params=_cp(("parallel",)), name=name,
    )(*parts)


BIG = (
    ("w_in", 1), ("w_mem_kv", 0), ("w_swa_up", 1), ("w_gdn_up", 1), ("w_xa_up", 1), ("w_out", 0), ("w_mlp_in", 1),
    ("w_mlp_out", 0))


class _Comm:
    MLP = ("w_mlp_in", "w_mlp_out")

    def __init__(self, late_shards, core, where):
        self.axis = dict(BIG)
        self.late_shards = late_shards
        self.mid = [k for k in late_shards if k not in self.MLP and k != "w_in"]
        self.core, self.where = core, where

    def gather_exchange(self, names):
        return _gather_over_ici([self.late_shards[k] for k in names])

    def as_weights(self, names, whole):
        return {k: (g.reshape(-1, g.shape[2]) if self.axis[k] == 0 else g) for k, g in zip(names, whole)}

    def gathered(self, names, landed, tag):
        return self.as_weights(names, _exchange_call(self.pass_on(names, landed), name=f"ag_{tag}_pass"))

    def pass_on(self, names, landed):
        chip = self.where[1:2]
        return _gather_pass_on([_place_own(a, self.late_shards[k], chip, name=f"ag_own_{k}") for k, a in zip(names, landed)])

    def shard_major(self, k, grad):
        return grad.reshape(N_CHIPS, -1, grad.shape[-1]) if self.axis[k] == 0 else grad

    def pair_sums(self, gs, tag, sibs=None):
        if sibs is None:
            sibs = _exchange_call(_sibling_halves(gs), name=f"rs_sibling_{tag}")
        return [_pair_sum(g, s, self.core, name=f"rs_pair_sum_{tag}{i}") for i, (g, s) in enumerate(zip(gs, sibs))]

    def chip_sums(self, s1s, rcvs):
        return [_chip_sum(s1, rcv, self.where, name=f"rs_chip_sum_{i}") for i, (s1, rcv) in enumerate(zip(s1s, rcvs))]
SMALL = ("g_mix", "sinks", "a_log", "dt_bias", "gdn_norm_w", "g_mem", "g_mlp", "g_final")


def _rows128(a, rows):
    flat = a.reshape(-1)
    return jnp.pad(flat, (0, rows * LANES - flat.shape[0])).reshape(rows, LANES)


def kernel(x, mem, g_mix, w_in, sinks, conv_w, a_log, dt_bias, gdn_norm_w, g_mem, w_mem_kv, w_swa_up, w_gdn_up, w_xa_up, w_out, g_mlp, w_mlp_in, w_mlp_out, g_final, loss_target, m_g_mix, m_w_in, m_sinks, m_conv_w, m_a_log, m_dt_bias, m_gdn_norm_w, m_g_mem, m_w_mem_kv, m_w_swa_up, m_w_gdn_up, m_w_xa_up, m_w_out, m_g_mlp, m_w_mlp_in, m_w_mlp_out, m_g_final, v_g_mix, v_w_in, v_sinks, v_conv_w, v_a_log, v_dt_bias, v_gdn_norm_w, v_g_mem, v_w_mem_kv, v_w_swa_up, v_w_gdn_up, v_w_xa_up, v_w_out, v_g_mlp, v_w_mlp_in, v_w_mlp_out, v_g_final):
    given = dict(locals())
    xi, yi, ci = _place()
    chip = 2 * xi + yi
    core = jnp.reshape(ci, (1,)).astype(jnp.int32)
    where = jnp.stack([ci, chip]).astype(jnp.int32)

    comm = _Comm({k: given[k][0].astype(BF16) for k, _ in BIG}, core, where)
    wts = {}
    conv_shard = conv_w[0]
    conv_rows = -(-conv_shard.size // (8 * LANES)) * 8
    conv_all = _all_gather_small(_rows128(conv_shard, conv_rows), name="ag_conv")
    conv_full = jnp.concatenate(
        [conv_all[2 * s].reshape(-1)[:conv_shard.size].reshape(conv_shard.shape) for s in range(N_CHIPS)], axis=1)

    small = {k: given[k].reshape(1, -1) for k in SMALL}
    small["conv_w"] = conv_full
    loss_row, dx, grads = _local_step(x[0], mem[0], loss_target[0], wts, small, comm)
    big_grads = {k: grads[k] for k, _ in BIG}

    layout = [("loss", loss_row[:, :1])] + [(k, grads[k]) for k in SMALL] + [("conv_w", grads["conv_w"])]
    rows = [-(-a.size // LANES) for _, a in layout]
    blk_rows = -(-sum(rows) // 8) * 8
    blk = jnp.concatenate([_rows128(a.astype(F32), n) for (_, a), n in zip(layout, rows)]
                          + [jnp.zeros((blk_rows - sum(rows), LANES), F32)], axis=0)
    gathered = _all_gather_small(blk, name="ag_small_grads")
    reduced = _sum_rows([gathered[i] for i in range(N_DEV)], F32, name="small_grad_sum")
    small_grads, start = {}, 0
    for (k, a), n in zip(layout, rows):
        small_grads[k] = reduced[start:start + n].reshape(-1)[:a.size].reshape(a.shape)
        start += n
    loss = small_grads["loss"].reshape(())
    cw = conv_shard.shape[1]
    conv_grad = lax.dynamic_slice_in_dim(small_grads["conv_w"], chip * cw, cw, axis=1)

    names = ["g_mix", "w_in", "sinks", "conv_w", "a_log", "dt_bias", "gdn_norm_w", "g_mem", "w_mem_kv", "w_swa_up",
             "w_gdn_up", "w_xa_up", "w_out", "g_mlp", "w_mlp_in", "w_mlp_out", "g_final"]
    out_g, out_d, out_m, out_v = [], [], [], []
    for k in names:
        w, m, v = given[k], given["m_" + k], given["v_" + k]
        if k in big_grads:
            g2 = big_grads[k]
        elif k == "conv_w":
            g2 = conv_grad
        else:
            g2 = small_grads[k]
        as_given = (lambda a: a.reshape(1, -1)) if w.ndim == 1 else (lambda a: a)
        if w.shape[-1] % LANES and w.shape[-1] > LANES:
            tr = lambda a: jnp.swapaxes(a, -1, -2)
            g_out, delta, new_m, new_v = (tr(a) for a in _adamw(tr(w), tr(g2), tr(m), tr(v), name="adamw_" + k))
        else:
            g_out, delta, new_m, new_v = _adamw(as_given(w), g2, as_given(m), as_given(v), name="adamw_" + k)
        out_g.append(g_out.reshape(w.shape))
        out_d.append(delta.reshape(w.shape))
        out_m.append(new_m.reshape(w.shape))
        out_v.append(new_v.reshape(w.shape))
    return (loss, dx[None], *out_g, *out_d, *out_m, *out_v)
```

```python
import functools
import math
from typing import Callable, NamedTuple

import jax
import jax.numpy as jnp
from jax import lax
from jax.experimental import pallas as pl
from jax.experimental.pallas import tpu as pltpu

F32 = jnp.float32
BF16 = jnp.bfloat16
HI = lax.Precision.HIGHEST
MESH = pl.DeviceIdType.MESH

SWA_Q_HEADS = 16
SWA_KV_HEADS = 2
SWA_HEAD_DIM = 64
SWA_WINDOW = 128
SWA_SCALE = SWA_HEAD_DIM ** -0.5
assert math.frexp(SWA_SCALE)[0] == 0.5
GDN_HEADS = 4
GDN_HEAD_DIM = 128
GDN_CONV = 4
GDN_CHUNK = 64
XA_HEADS = 4
XA_HEAD_DIM = 128
RMS_EPS = 1e-6
L2_EPS = 1e-6
ADAM_LR = 0.001
ADAM_B1 = 0.9
ADAM_B2 = 0.999
ADAM_EPS = 1e-08
ADAM_WD = 0.01
ADAM_STEP = 10

LANES = 128
N_SHARDS = 4
VMEM_LIMIT = 56 * 1024 * 1024

NT = (((1,), (1,)), ((), ()))
TN = (((0,), (0,)), ((), ()))
NN = (((1,), (0,)), ((), ()))


def _cp(sem=None):
    return pltpu.CompilerParams(dimension_semantics=sem, vmem_limit_bytes=VMEM_LIMIT)


def _blk(dim, pref):
    if dim <= pref:
        return dim
    b = (pref // LANES) * LANES
    while dim % b:
        b -= LANES
    assert b > 0, (dim, pref)
    return b


def _dot(a, b, dims=NN, precision=None):
    return lax.dot_general(a, b, dims, precision=precision, preferred_element_type=F32)


def _sigmoid(x):
    return 0.5 * jnp.tanh(0.5 * x) + 0.5


MM_TK_BYTES = 4096


def _mm(a, b, *, name, ta=False, tb=False, out_dtypes=(F32,), epilogue=None, extras=(), tm=1024, tn=1024, tk=None,
        b_sharded=False, out_sharded=False, b_window=None, exchange=None):
    (kdim, m) = a.shape if ta else a.shape[::-1]
    col0 = 0
    n_lim = k_lim = None
    if b_sharded:
        ns, rows_w, per = b.shape
        if tb:
            kb, n, k_lim = ns * per, rows_w, per
        else:
            kb, n, n_lim = rows_w, ns * per, per
    else:
        (kb, n) = b.shape[::-1] if tb else b.shape
        if b_window is not None:
            assert not tb
            col0, n = b_window
    assert kdim == kb, (a.shape, b.shape, ta, tb)
    if out_sharded:
        assert n % N_SHARDS == 0
        n_lim = n // N_SHARDS if n_lim is None else n_lim
        assert n_lim == n // N_SHARDS
    if tk is None:
        tk = MM_TK_BYTES // max(a.dtype.itemsize, b.dtype.itemsize)
    tm, tn, tk = _blk(m, tm), _blk(n_lim or n, tn), _blk(k_lim or kdim, tk)
    assert col0 % tn == 0, (col0, tn)
    nk = kdim // tk
    a_spec = pl.BlockSpec((tk, tm), lambda i, j, k: (k, i)) if ta else pl.BlockSpec((tm, tk), lambda i, j, k: (i, k))
    if b_sharded and tb:
        kpb = k_lim // tk
        b_spec = pl.BlockSpec((None, tn, tk), lambda i, j, k: (k // kpb, j, k % kpb))
    elif b_sharded:
        bpb = n_lim // tn
        b_spec = pl.BlockSpec((None, tk, tn), lambda i, j, k: (j // bpb, k, j % bpb))
    elif tb:
        b_spec = pl.BlockSpec((tn, tk), lambda i, j, k: (j, k))
    else:
        b_spec = pl.BlockSpec((tk, tn), lambda i, j, k: (k, j + col0 // tn))
    x_spec = pl.BlockSpec((tm, tn), lambda i, j, k: (i, j))
    r_spec = pl.BlockSpec((1, tn), lambda i, j, k: (0, j))
    if out_sharded:
        opb = n_lim // tn
        o_spec = pl.BlockSpec((None, tm, tn), lambda i, j, k: (j // opb, i, j % opb))
        out_shape = (N_SHARDS, m, n_lim)
    else:
        o_spec, out_shape = x_spec, (m, n)
    dims = ((((0 if ta else 1),), ((1 if tb else 0),)), ((), ()))
    n_extra, n_out = len(extras), len(out_dtypes)

    host = _ExchangeHost(exchange)
    grid = (m // tm, n // tn, nk)

    def body(*refs):
        a_ref, b_ref = refs[:2]
        extra_refs = refs[2:2 + n_extra]
        out_refs = refs[2 + n_extra + host.n_in:2 + n_extra + host.n_in + n_out]
        host.start(refs, 2 + n_extra, 2 + n_extra + host.n_in + n_out, grid)
        part = _dot(a_ref[...].astype(BF16), b_ref[...].astype(BF16), dims)

        def finish(acc):
            vals = epilogue(acc, *[r[...] for r in extra_refs]) if epilogue is not None else (acc,) * n_out
            assert len(vals) == n_out
            for r, v in zip(out_refs, vals):
                r[...] = v.astype(r.dtype)

        if nk == 1:
            finish(part)
        else:
            acc_ref = refs[2 + n_extra + host.n_in + n_out + host.n_out]
            k = pl.program_id(2)

            @pl.when(k == 0)
            def _():
                acc_ref[...] = part

            @pl.when((k > 0) & (k < nk - 1))
            def _():
                acc_ref[...] += part

            @pl.when(k == nk - 1)
            def _():
                finish(acc_ref[...] + part)

        host.wait(refs, 2 + n_extra, 2 + n_extra + host.n_in + n_out, grid)

    outs = pl.pallas_call(
        body,
        grid=grid,
        in_specs=[a_spec, b_spec] + [x_spec if e.shape[0] == m else r_spec for e in extras] + host.in_specs,
        out_specs=[o_spec] * n_out + host.out_specs,
        out_shape=[jax.ShapeDtypeStruct(out_shape, d) for d in out_dtypes] + host.out_shapes,
        scratch_shapes=([pltpu.VMEM((tm, tn), F32)] if nk > 1 else []) + host.scratch,
        input_output_aliases=host.aliases(2 + n_extra, n_out),
        compiler_params=_cp(host.semantics(("parallel", "parallel", "arbitrary"))),
        name=name,
    )(a, b, *extras, *host.ins)
    mine, landed = outs[:n_out], list(outs[n_out:])
    mine = mine[0] if n_out == 1 else mine
    return (mine, landed) if exchange is not None else mine


class Exchange(NamedTuple):
    ins: tuple
    out_shapes: tuple
    n_remote: int
    n_local: int
    copies_of: Callable
    aliases: tuple = ()


def _exchange_copies(ex, in_refs, out_refs, sem_refs):
    send_sems, recv_sems, local_sems = sem_refs
    remote, local = ex.copies_of(in_refs, out_refs, _place())
    assert len(remote) == ex.n_remote and len(local) == ex.n_local, (len(remote), len(local))
    cps = [pltpu.make_async_remote_copy(src_ref=src, dst_ref=dst, send_sem=send_sems.at[k], recv_sem=recv_sems.at[k],
                                        device_id=to, device_id_type=MESH) for k, (src, dst, to) in enumerate(remote)]
    cps += [pltpu.make_async_copy(src, dst, local_sems.at[k]) for k, (src, dst) in enumerate(local)]
    return cps


class _ExchangeHost:
    def __init__(self, ex):
        self.ex = ex
        self.ins = list(ex.ins) if ex else []
        self.out_shapes = list(ex.out_shapes) if ex else []
        self.n_in, self.n_out = len(self.ins), len(self.out_shapes)
        self.in_specs = [HBM_SPEC] * self.n_in
        self.out_specs = [HBM_SPEC] * self.n_out
        self.scratch = _sem_scratch(ex.n_remote, ex.n_local) if ex else []

    def semantics(self, sem):
        return tuple("arbitrary" for _ in sem) if self.ex else sem

    def aliases(self, in_at, out_at):
        return {in_at + i: out_at + o for i, o in self.ex.aliases} if self.ex else {}

    def _refs(self, refs, in_at, out_at):
        return refs[in_at:in_at + self.n_in], refs[out_at:out_at + self.n_out], refs[len(refs) - 3:]

    def _when(self, grid, last):
        cond = None
        for d, size in enumerate(grid):
            c = pl.program_id(d) == (size - 1 if last else 0)
            cond = c if cond is None else cond & c
        return cond

    def start(self, refs, in_at, out_at, grid):
        if self.ex:
            @pl.when(self._when(grid, False))
            def _():
                for cp in _exchange_copies(self.ex, *self._refs(refs, in_at, out_at)):
                    cp.start()

    def wait(self, refs, in_at, out_at, grid):
        if self.ex:
            @pl.when(self._when(grid, True))
            def _():
                for cp in _exchange_copies(self.ex, *self._refs(refs, in_at, out_at)):
                    cp.wait()


def _rms_fwd(x, g, *, name, tm=512, exchange=None):
    t, d = x.shape
    tm = _blk(t, tm)
    host = _ExchangeHost(exchange)
    grid = (t // tm,)

    def body(*refs):
        x_ref, g_ref, n_ref = refs[0], refs[1], refs[2 + host.n_in]
        host.start(refs, 2, 3 + host.n_in, grid)
        xv = x_ref[...]
        r = lax.rsqrt(jnp.mean(xv * xv, axis=-1, keepdims=True) + RMS_EPS)
        n_ref[...] = (xv * r * g_ref[...]).astype(n_ref.dtype)
        host.wait(refs, 2, 3 + host.n_in, grid)

    outs = pl.pallas_call(
        body, grid=grid,
        in_specs=[pl.BlockSpec((tm, d), lambda i: (i, 0)), pl.BlockSpec((1, d), lambda i: (0, 0))] + host.in_specs,
        out_specs=[pl.BlockSpec((tm, d), lambda i: (i, 0))] + host.out_specs,
        out_shape=[jax.ShapeDtypeStruct((t, d), BF16)] + host.out_shapes,
        scratch_shapes=host.scratch, input_output_aliases=host.aliases(2, 1),
        compiler_params=_cp(host.semantics(("parallel",))), name=name,
    )(x, g, *host.ins)
    return (outs[0], list(outs[1:])) if exchange is not None else outs[0]


def _rms_bwd(dn, x, g, dres, *, name, tm=512):
    t, d = x.shape
    tm = _blk(t, tm)

    def body(dn_ref, x_ref, g_ref, dres_ref, dx_ref, dxb_ref, dg_ref):
        i = pl.program_id(0)
        xv = x_ref[...]
        r = lax.rsqrt(jnp.mean(xv * xv, axis=-1, keepdims=True) + RMS_EPS)
        xh = xv * r
        dnv = dn_ref[...].astype(F32)
        dxh = dnv * g_ref[...]
        dx = dres_ref[...] + r * (dxh - xh * jnp.mean(dxh * xh, axis=-1, keepdims=True))
        dx_ref[...] = dx
        dxb_ref[...] = dx.astype(dxb_ref.dtype)
        part = jnp.sum(dnv * xh, axis=0, keepdims=True)

        @pl.when(i == 0)
        def _():
            dg_ref[...] = part

        @pl.when(i > 0)
        def _():
            dg_ref[...] += part

    row = pl.BlockSpec((tm, d), lambda i: (i, 0))
    vec = pl.BlockSpec((1, d), lambda i: (0, 0))
    return pl.pallas_call(
        body, grid=(t // tm,),
        in_specs=[row, row, vec, row], out_specs=[row, row, vec],
        out_shape=[jax.ShapeDtypeStruct((t, d), F32), jax.ShapeDtypeStruct((t, d), BF16),
                   jax.ShapeDtypeStruct((1, d), F32)],
        compiler_params=_cp(("arbitrary",)), name=name,
    )(dn, x, g, dres)


def _final_loss(h, g, tgt, *, name, tm=512):
    t, d = h.shape
    tm = _blk(t, tm)

    def body(h_ref, g_ref, t_ref, dh_ref, dhb_ref, dg_ref, loss_ref):
        i = pl.program_id(0)
        hv = h_ref[...]
        r = lax.rsqrt(jnp.mean(hv * hv, axis=-1, keepdims=True) + RMS_EPS)
        xh = hv * r
        e = xh * g_ref[...] - t_ref[...]
        dy = e * (1.0 / d)
        dxh = dy * g_ref[...]
        dh = r * (dxh - xh * jnp.mean(dxh * xh, axis=-1, keepdims=True))
        dh_ref[...] = dh
        dhb_ref[...] = dh.astype(dhb_ref.dtype)
        dg_part = jnp.sum(dy * xh, axis=0, keepdims=True)
        row_loss = jnp.sum(e * e, axis=-1, keepdims=True) * (0.5 / d)
        loss_part = jnp.sum(row_loss, axis=0, keepdims=True)

        @pl.when(i == 0)
        def _():
            dg_ref[...] = dg_part
            loss_ref[...] = jnp.broadcast_to(loss_part, loss_ref.shape)

        @pl.when(i > 0)
        def _():
            dg_ref[...] += dg_part
            loss_ref[...] += jnp.broadcast_to(loss_part, loss_ref.shape)

    row = pl.BlockSpec((tm, d), lambda i: (i, 0))
    vec = pl.BlockSpec((1, d), lambda i: (0, 0))
    return pl.pallas_call(
        body, grid=(t // tm,),
        in_specs=[row, vec, row], out_specs=[row, row, vec, pl.BlockSpec((1, LANES), lambda i: (0, 0))],
        out_shape=[jax.ShapeDtypeStruct((t, d), F32), jax.ShapeDtypeStruct((t, d), BF16),
                   jax.ShapeDtypeStruct((1, d), F32), jax.ShapeDtypeStruct((1, LANES), F32)],
        compiler_params=_cp(("arbitrary",)), name=name,
    )(h, g, tgt)


SWA_SUB = 64


def _swa_mask(n, rows, row0):
    w = SWA_WINDOW
    qi = (lax.broadcasted_iota(jnp.int32, (rows, 2 * w), 0) + row0) & (w - 1)
    kj = lax.broadcasted_iota(jnp.int32, (rows, 2 * w), 1)
    return (kj > qi) & (kj <= qi + w) & ((n > 0) | (kj >= w))


def _stack_heads(ref, heads, width):
    return jnp.concatenate([ref[:, h * width:(h + 1) * width] for h in heads], axis=0)


def _stack_scalars(ref, heads, rows):
    return jnp.concatenate([jnp.broadcast_to(ref[0:1, h:h + 1], (rows, 1)) for h in heads], axis=0)


def _swa_fwd(q, kv, sinks, *, name, exchange=None):
    t = q.shape[0]
    w, hd, hq, hkv = SWA_WINDOW, SWA_HEAD_DIM, SWA_Q_HEADS, SWA_KV_HEADS
    grp = hq // hkv
    kvw = hkv * hd
    nb = t // w
    host = _ExchangeHost(exchange)
    assert not (exchange and exchange.aliases)

    def body(*refs):
        q_ref, kvp_ref, kvc_ref, s_ref = refs[:4]
        o_ref, lse_ref = refs[4 + host.n_in:6 + host.n_in]
        host.start(refs, 4, 6 + host.n_in, (nb,))
        n = pl.program_id(0)
        mask = _swa_mask(n, grp * w, 0)
        kvcat = jnp.concatenate([kvp_ref[...], kvc_ref[...]], axis=0)
        kvs = range(hkv)
        heads = [range(hk * grp, (hk + 1) * grp) for hk in kvs]
        sks = [_stack_scalars(s_ref, hs, w) for hs in heads]
        ss = [jnp.where(mask, _dot(_stack_heads(q_ref, heads[hk], hd) * SWA_SCALE, kvcat[:, hk * hd:(hk + 1) * hd], NT),
                        -jnp.inf) for hk in kvs]
        ms = [jnp.maximum(jnp.max(s, axis=-1, keepdims=True), sk) for s, sk in zip(ss, sks)]
        ps = [jnp.exp(s - m) for s, m in zip(ss, ms)]
        dens = [jnp.sum(p, axis=-1, keepdims=True) + jnp.exp(sk - m) for p, sk, m in zip(ps, sks, ms)]
        os_ = [_dot((p * (1.0 / den)).astype(BF16), kvcat[:, kvw + hk * hd:kvw + (hk + 1) * hd])
               for hk, p, den in zip(kvs, ps, dens)]
        outs, lses = [], []
        for o, m, den in zip(os_, ms, dens):
            lse = m + jnp.log(den)
            outs += [o[j * w:(j + 1) * w] for j in range(grp)]
            lses += [lse[j * w:(j + 1) * w] for j in range(grp)]
        o_ref[...] = jnp.concatenate(outs, axis=1).astype(o_ref.dtype)
        lse_ref[...] = jnp.concatenate(lses, axis=1)
        host.wait(refs, 4, 6 + host.n_in, (nb,))

    outs = pl.pallas_call(
        body, grid=(nb,),
        in_specs=[pl.BlockSpec((w, hq * hd), lambda i: (i, 0)),
                  pl.BlockSpec((w, 2 * kvw), lambda i: (jnp.maximum(i - 1, 0), 0)),
                  pl.BlockSpec((w, 2 * kvw), lambda i: (i, 0)),
                  pl.BlockSpec((1, hq), lambda i: (0, 0))] + host.in_specs,
        out_specs=[pl.BlockSpec((w, hq * hd), lambda i: (i, 0)), pl.BlockSpec((w, hq), lambda i: (i, 0))] + host.out_specs,
        out_shape=[jax.ShapeDtypeStruct((t, hq * hd), BF16), jax.ShapeDtypeStruct((t, hq), F32)] + host.out_shapes,
        scratch_shapes=host.scratch,
        compiler_params=_cp(host.semantics(("parallel",))), name=name,
    )(q, kv, kv, sinks, *host.ins)
    return (outs[0], outs[1], list(outs[2:])) if exchange is not None else outs


ANY_SPEC = pl.BlockSpec(memory_space=pl.ANY)


def _swa_bwd(q, kv, sinks, o, lse, do, dp, dp_col, *, name):
    t = q.shape[0]
    w, hd, hq, hkv = SWA_WINDOW, SWA_HEAD_DIM, SWA_Q_HEADS, SWA_KV_HEADS
    grp = hq // hkv
    kvw = hkv * hd
    nb = t // w
    assert dp_col % (hq * hd) == 0
    dq_blk = dp_col // (hq * hd)

    def body(q_ref, kvp_ref, kvc_ref, s_ref, o_ref, lse_ref, do_ref, _, dq_ref, dkv_ref, ds_ref, carry_ref, s_scr, dp_scr,
             p_scr, ds_scr):
        n = pl.program_id(0)

        @pl.when(n == 0)
        def _():
            ds_ref[...] = jnp.zeros_like(ds_ref)
            carry_ref[...] = jnp.zeros_like(carry_ref)

        @pl.when(n < nb)
        def _():
            kvcat = jnp.concatenate([kvp_ref[...], kvc_ref[...]], axis=0)
            dqs, dsk, dks, dvs = [], [], [], []
            for hk in range(hkv):
                heads = range(hk * grp, (hk + 1) * grp)
                qs = _stack_heads(q_ref, heads, hd)
                dos = _stack_heads(do_ref, heads, hd)
                os_ = _stack_heads(o_ref, heads, hd)
                lse = _stack_heads(lse_ref, heads, 1)
                kh = kvcat[:, hk * hd:(hk + 1) * hd]
                vh = kvcat[:, kvw + hk * hd:kvw + (hk + 1) * hd]
                delta = jnp.sum(dos.astype(F32) * os_.astype(F32), axis=-1, keepdims=True)
                s_scr[...] = _dot(qs * SWA_SCALE, kh, NT)
                dp_scr[...] = _dot(dos, vh, NT)
                for r0 in range(0, grp * w, SWA_SUB):
                    rows = slice(r0, r0 + SWA_SUB)
                    p = jnp.exp(jnp.where(_swa_mask(n, SWA_SUB, r0 % w), s_scr[rows, :], -jnp.inf) - lse[rows])
                    p_scr[rows, :] = p.astype(p_scr.dtype)
                    ds_scr[rows, :] = (p * (dp_scr[rows, :] - delta[rows]) * SWA_SCALE).astype(ds_scr.dtype)
                ds = ds_scr[...]
                dq = _dot(ds, kh)
                dqs += [dq[j * w:(j + 1) * w] for j in range(grp)]
                dks.append(_dot(ds, qs, TN))
                dvs.append(_dot(p_scr[...], dos, TN))
                dsink = -jnp.exp(_stack_scalars(s_ref, heads, w) - lse) * delta
                dsk += [jnp.sum(dsink[j * w:(j + 1) * w], axis=0, keepdims=True) for j in range(grp)]
            dq_ref[...] = jnp.concatenate(dqs, axis=1).astype(dq_ref.dtype)
            ds_ref[...] += jnp.concatenate(dsk, axis=1)
            dkv_cat = jnp.concatenate(dks + dvs, axis=1)
            dkv_ref[...] = (carry_ref[...] + dkv_cat[:w]).astype(dkv_ref.dtype)
            carry_ref[...] = dkv_cat[w:]

        @pl.when(n == nb)
        def _():
            dkv_ref[...] = carry_ref[...].astype(dkv_ref.dtype)

    cur = lambda i: (jnp.minimum(i, nb - 1), 0)
    prev = lambda i: (jnp.clip(i - 1, 0, nb - 1), 0)
    return pl.pallas_call(
        body, grid=(nb + 1,),
        in_specs=[pl.BlockSpec((w, hq * hd), cur), pl.BlockSpec((w, 2 * kvw), prev), pl.BlockSpec((w, 2 * kvw), cur),
                  pl.BlockSpec((1, hq), lambda i: (0, 0)), pl.BlockSpec((w, hq * hd), cur),
                  pl.BlockSpec((w, hq), cur), pl.BlockSpec((w, hq * hd), cur), ANY_SPEC],
        out_specs=[pl.BlockSpec((w, hq * hd), lambda i: (jnp.minimum(i, nb - 1), dq_blk)),
                   pl.BlockSpec((w, 2 * kvw), prev), pl.BlockSpec((1, hq), lambda i: (0, 0))],
        out_shape=[jax.ShapeDtypeStruct(dp.shape, dp.dtype), jax.ShapeDtypeStruct((t, 2 * kvw), BF16),
                   jax.ShapeDtypeStruct((1, hq), F32)],
        scratch_shapes=[pltpu.VMEM((w, 2 * kvw), F32)] + [pltpu.VMEM((grp * w, 2 * w), dt) for dt in (F32, F32, BF16, BF16)],
        input_output_aliases={7: 0},
        compiler_params=_cp(("arbitrary",)), name=name,
    )(q, kv, kv, sinks, o, lse, do, dp)


def _xa_fwd(q, mkv, *, name, tq=512):
    t, xw = q.shape
    nm = mkv.shape[0]
    hd, nh = XA_HEAD_DIM, XA_HEADS
    tq = _blk(t, tq)

    def body(q_ref, mkv_ref, o_ref):
        cols = [slice(h * hd, (h + 1) * hd) for h in range(nh)]
        ss = [_dot(q_ref[:, c], mkv_ref[:, c], NT) * (hd ** -0.5) for c in cols]
        ps = [jnp.exp(s - jnp.max(s, axis=-1, keepdims=True)) for s in ss]
        ps = [p * (1.0 / jnp.sum(p, axis=-1, keepdims=True)) for p in ps]
        outs = [_dot(p.astype(BF16), mkv_ref[:, xw + c.start:xw + c.stop]) for p, c in zip(ps, cols)]
        o_ref[...] = jnp.concatenate(outs, axis=1).astype(o_ref.dtype)

    return pl.pallas_call(
        body, grid=(t // tq,),
        in_specs=[pl.BlockSpec((tq, xw), lambda i: (i, 0)), pl.BlockSpec((nm, 2 * xw), lambda i: (0, 0))],
        out_specs=pl.BlockSpec((tq, xw), lambda i: (i, 0)),
        out_shape=jax.ShapeDtypeStruct((t, xw), BF16),
        compiler_params=_cp(("parallel",)), name=name,
    )(q, mkv)


def _xa_bwd(q, mkv, do, dp, dp_col, *, name, tq=512):
    t, xw = q.shape
    nm = mkv.shape[0]
    hd, nh = XA_HEAD_DIM, XA_HEADS
    tq = _blk(t, tq)
    assert dp_col % xw == 0

    def body(q_ref, mkv_ref, do_ref, _, dq_ref, dmkv_ref):
        i = pl.program_id(0)
        cols = [slice(h * hd, (h + 1) * hd) for h in range(nh)]
        vcols = [slice(xw + c.start, xw + c.stop) for c in cols]
        ss = [_dot(q_ref[:, c], mkv_ref[:, c], NT) * (hd ** -0.5) for c in cols]
        dps = [_dot(do_ref[:, c], mkv_ref[:, v], NT) for c, v in zip(cols, vcols)]
        ps = [jnp.exp(s - jnp.max(s, axis=-1, keepdims=True)) for s in ss]
        ps = [p * (1.0 / jnp.sum(p, axis=-1, keepdims=True)) for p in ps]
        dss = [(p * (dp - jnp.sum(p * dp, axis=-1, keepdims=True)) * (hd ** -0.5)).astype(BF16) for p, dp in zip(ps, dps)]
        dqs = [_dot(ds, mkv_ref[:, c]) for ds, c in zip(dss, cols)]
        dks = [_dot(ds, q_ref[:, c], TN) for ds, c in zip(dss, cols)]
        dvs = [_dot(p.astype(BF16), do_ref[:, c], TN) for p, c in zip(ps, cols)]
        dq_ref[...] = jnp.concatenate(dqs, axis=1).astype(dq_ref.dtype)
        part = jnp.concatenate(dks + dvs, axis=1)

        @pl.when(i == 0)
        def _():
            dmkv_ref[...] = part

        @pl.when(i > 0)
        def _():
            dmkv_ref[...] += part

    row = pl.BlockSpec((tq, xw), lambda i: (i, 0))
    full = pl.BlockSpec((nm, 2 * xw), lambda i: (0, 0))
    return pl.pallas_call(
        body, grid=(t // tq,),
        in_specs=[row, full, row, ANY_SPEC],
        out_specs=[pl.BlockSpec((tq, xw), lambda i: (i, dp_col // xw)), full],
        out_shape=[jax.ShapeDtypeStruct(dp.shape, dp.dtype), jax.ShapeDtypeStruct((nm, 2 * xw), F32)],
        input_output_aliases={3: 0}, compiler_params=_cp(("arbitrary",)), name=name,
    )(q, mkv, do, dp)


def _merge_specs(ys, ws, tm):
    y_specs = [pl.BlockSpec((tm, y.shape[1]), lambda i: (i, 0)) for y in ys]
    w_specs = [pl.BlockSpec(w.shape, lambda i: (0, 0, 0)) for w in ws]
    return y_specs, w_specs


def _merge_tiles(ws, tn):
    ns, _, per = ws[0].shape
    tn = _blk(per, tn)
    return tn, [(s, c, s * per + c) for s in range(ns) for c in range(0, per, tn)]


def _merge_fwd(ys, ws, gates, *, name, tm=256, tn=512):
    t, d = ys[0].shape[0], ws[0].shape[0] * ws[0].shape[2]
    tm = _blk(t, tm)
    tn, tiles = _merge_tiles(ws, tn)
    y_specs, w_specs = _merge_specs(ys, ws, tm)

    def body(ya, yb, yc, wa, wb, wc, g_ref, o_ref):
        for s, c, col in tiles:
            acc = None
            for b, (y, w) in enumerate(((ya, wa), (yb, wb), (yc, wc))):
                term = _sigmoid(g_ref[:, b * d + col:b * d + col + tn]) * _dot(y[...], w[s, :, c:c + tn])
                acc = term if acc is None else acc + term
            o_ref[:, col:col + tn] = acc.astype(o_ref.dtype)

    return pl.pallas_call(
        body, grid=(t // tm,),
        in_specs=y_specs + w_specs + [pl.BlockSpec((tm, 3 * d), lambda i: (i, 0))],
        out_specs=pl.BlockSpec((tm, d), lambda i: (i, 0)),
        out_shape=jax.ShapeDtypeStruct((t, d), BF16),
        compiler_params=_cp(("parallel",)), name=name,
    )(*ys, *ws, gates)


def _merge_bwd(ys, ws, gates, dmerged, dp_width, *, name, tm=256, tn=512):
    t, d = ys[0].shape[0], ws[0].shape[0] * ws[0].shape[2]
    tm = _blk(t, tm)
    tn, tiles = _merge_tiles(ws, tn)
    y_specs, w_specs = _merge_specs(ys, ws, tm)
    row = pl.BlockSpec((tm, d), lambda i: (i, 0))
    wide = pl.BlockSpec((tm, 3 * d), lambda i: (i, 0))

    def body(ya, yb, yc, wa, wb, wc, g_ref, dm_ref, dua, dub, duc, dp_ref):
        for s, c, col in tiles:
            dm = dm_ref[:, col:col + tn]
            for b, (y, w, du) in enumerate(((ya, wa, dua), (yb, wb, dub), (yc, wc, duc))):
                sg = _sigmoid(g_ref[:, b * d + col:b * d + col + tn])
                u = _dot(y[...], w[s, :, c:c + tn])
                du[:, col:col + tn] = (dm * sg).astype(du.dtype)
                dp_ref[:, b * d + col:b * d + col + tn] = (dm * u * sg * (1.0 - sg)).astype(dp_ref.dtype)

    return pl.pallas_call(
        body, grid=(t // tm,),
        in_specs=y_specs + w_specs + [wide, row],
        out_specs=[row] * 3 + [wide],
        out_shape=[jax.ShapeDtypeStruct((t, d), BF16)] * 3 + [jax.ShapeDtypeStruct((t, dp_width), BF16)],
        compiler_params=_cp(("parallel",)), name=name,
    )(*ys, *ws, gates, dmerged)


def _adamw(w, g, m, v, *, name, tm=256):
    lead = w.ndim - 2
    assert all(s == 1 for s in w.shape[:lead]) and m.shape == w.shape and v.shape == w.shape
    r, c = w.shape[lead:]
    assert g.shape == (r, c)
    tm = _blk(r, tm) if r % 8 == 0 else r
    tc = c if tm * c * 4 <= ROW_BLOCK_BYTES else _blk(c, 256)
    ncb = c // tc
    bc1 = 1.0 - ADAM_B1 ** ADAM_STEP
    bc2 = 1.0 - ADAM_B2 ** ADAM_STEP

    def body(w_ref, g_ref, m_ref, v_ref, go_ref, d_ref, nm_ref, nv_ref):
        gv = g_ref[...]
        go_ref[...] = gv
        nm = ADAM_B1 * m_ref[...] + (1.0 - ADAM_B1) * gv
        nv = ADAM_B2 * v_ref[...] + (1.0 - ADAM_B2) * (gv * gv)
        d_ref[...] = -ADAM_LR * ((nm / bc1) / (jnp.sqrt(nv / bc2) + ADAM_EPS) + ADAM_WD * w_ref[...])
        nm_ref[...] = nm
        nv_ref[...] = nv

    spec = pl.BlockSpec((None,) * lead + (tm, tc), lambda i: (0,) * lead + (i // ncb, i % ncb))
    g_spec = pl.BlockSpec((tm, tc), lambda i: (i // ncb, i % ncb))
    return pl.pallas_call(
        body, grid=(r // tm * ncb,), in_specs=[spec, g_spec, spec, spec], out_specs=[spec] * 4,
        out_shape=[jax.ShapeDtypeStruct(w.shape, F32)] * 4,
        compiler_params=_cp(("parallel",)), name=name,
    )(w, g, m, v)


HALO = 8


def _shift_down(cur, prev, j):
    if j == 0:
        return cur
    y = pltpu.roll(cur, j, 0)
    row = lax.broadcasted_iota(jnp.int32, (HALO, cur.shape[1]), 0)
    top = jnp.where(row < j, pltpu.roll(prev, j, 0), y[:HALO])
    return jnp.concatenate([top, y[HALO:]], axis=0)


def _shift_up(cur, nxt, j):
    if j == 0:
        return cur
    tm = cur.shape[0]
    y = pltpu.roll(cur, tm - j, 0)
    row = lax.broadcasted_iota(jnp.int32, (HALO, cur.shape[1]), 0)
    bot = jnp.where(row >= HALO - j, pltpu.roll(nxt, HALO - j, 0), y[tm - HALO:])
    return jnp.concatenate([y[:tm - HALO], bot], axis=0)


def _softplus(x):
    return jnp.maximum(x, 0.0) + jnp.log(1.0 + jnp.exp(-jnp.abs(x)))


def _gdn_pre_fwd(qkvb, conv_w, ab, alog_pad, dt_pad, *, name, ab_blk=0, tm=256):
    t, cw = qkvb.shape
    hd, nh, ck = GDN_HEAD_DIM, GDN_HEADS, GDN_CHUNK
    gw = nh * hd
    tm = _blk(t, tm)
    hb = tm // HALO

    def body(x_ref, xp_ref, w_ref, ab_ref, al_ref, dt_ref, xc_ref, qkvn_ref, aux_ref):
        i = pl.program_id(0)
        cur = x_ref[...]
        prev = jnp.where(i > 0, xp_ref[...], 0.0)
        xc = None
        for tap in range(GDN_CONV):
            term = w_ref[tap:tap + 1, :] * _shift_down(cur, prev, GDN_CONV - 1 - tap)
            xc = term if xc is None else xc + term
        xc_ref[...] = xc
        s = xc * _sigmoid(xc)
        for h in range(2 * nh):
            xh = s[:, h * hd:(h + 1) * hd]
            r = lax.rsqrt(jnp.sum(xh * xh, axis=-1, keepdims=True) + L2_EPS)
            scale = hd ** -0.5 if h < nh else 1.0
            qkvn_ref[:, h * hd:(h + 1) * hd] = xh * (r * scale)
        qkvn_ref[:, 2 * gw:] = s[:, 2 * gw:]
        abv = ab_ref[...]
        lane = lax.broadcasted_iota(jnp.int32, abv.shape, 1)
        g = jnp.where(lane < nh, -jnp.exp(al_ref[...]) * _softplus(abv + dt_ref[...]), 0.0)
        beta = jnp.where((lane >= nh) & (lane < 2 * nh), _sigmoid(abv), 0.0)
        ii = lax.broadcasted_iota(jnp.int32, (tm, tm), 0)
        jj = lax.broadcasted_iota(jnp.int32, (tm, tm), 1)
        tri = jnp.where((ii >= jj) & ((ii ^ jj) < ck), 1.0, 0.0)
        gcum = _dot(tri, g, precision=HI)
        aux_ref[...] = g + beta + pltpu.roll(gcum, 2 * nh, 1)

    row = lambda c: pl.BlockSpec((tm, c), lambda i: (i, 0))
    vec = lambda r, c: pl.BlockSpec((r, c), lambda i: (0, 0))
    return pl.pallas_call(
        body, grid=(t // tm,),
        in_specs=[row(cw), pl.BlockSpec((HALO, cw), lambda i: (jnp.maximum(i * hb - 1, 0), 0)), vec(GDN_CONV, cw),
                  pl.BlockSpec((tm, LANES), lambda i: (i, ab_blk)), vec(1, LANES), vec(1, LANES)],
        out_specs=[row(cw), row(cw), row(LANES)],
        out_shape=[jax.ShapeDtypeStruct((t, cw), F32), jax.ShapeDtypeStruct((t, cw), F32),
                   jax.ShapeDtypeStruct((t, LANES), F32)],
        compiler_params=_cp(("parallel",)), name=name,
    )(qkvb, qkvb, conv_w, ab, alog_pad, dt_pad)


GDN_STEP_CHUNKS = 8
GDN_ILP_CHUNKS = 4
GDN_ILP_CHUNKS_BWD = 4


def _bdot(a, b, dims=NN):
    return _dot(a.astype(BF16), b.astype(BF16), dims)


def _split_bf16(x):
    hi = x.astype(BF16)
    return hi, (x - hi.astype(F32)).astype(BF16)


def _dot3(a, b, dims=NN):
    ah, al = _split_bf16(a)
    bh, bl = _split_bf16(b)
    return _dot(ah, bh, dims) + (_dot(ah, bl, dims) + _dot(al, bh, dims))


def _dot3_many(lhs, rhs, dims=NN):
    sa = [_split_bf16(a) for a in lhs]
    sb = [_split_bf16(b) for b in rhs]
    hh = [_dot(a[0], b[0], dims) for a, b in zip(sa, sb)]
    hl = [_dot(a[0], b[1], dims) for a, b in zip(sa, sb)]
    lh = [_dot(a[1], b[0], dims) for a, b in zip(sa, sb)]
    return [x + (y + z) for x, y, z in zip(hh, hl, lh)]


def _gdn_local(chains, with_inverse):
    ck = GDN_CHUNK
    ii = lax.broadcasted_iota(jnp.int32, (ck, ck), 0)
    jj = lax.broadcasted_iota(jnp.int32, (ck, ck), 1)
    lower, strict = ii >= jj, ii > jj
    dmat = [jnp.exp(jnp.where(lower, gc - gc_row, -jnp.inf)) for _, _, _, gc, gc_row in chains]
    kk = [_bdot(k, k, NT) for _, k, _, _, _ in chains]
    qk = [_bdot(q, k, NT) for q, k, _, _, _ in chains]
    tinv = [None] * len(chains)
    if with_inverse:
        lmat = [jnp.where(strict, c[2] * kk_i * d_i, 0.0) for c, kk_i, d_i in zip(chains, kk, dmat)]
        eye = jnp.where(ii == jj, 1.0, 0.0)
        tinv = [eye - l_i for l_i in lmat]
        pw = lmat
        for _ in range(int(math.log2(ck)) - 1):
            pw = _dot3_many(pw, pw)
            tinv = [t_i + d_i for t_i, d_i in zip(tinv, _dot3_many(tinv, pw))]
    out = []
    for (q, k, b, gc, gc_row), dmat_i, kk_i, qk_i, tinv_i in zip(chains, dmat, kk, qk, tinv):
        gl = gc[ck - 1:ck, :]
        out.append(dict(lower=lower, strict=strict, dmat=dmat_i, kk=kk_i, tinv=tinv_i, gam=jnp.exp(gc), qk=qk_i,
                        mm=qk_i * dmat_i, kdec=jnp.exp(gl - gc)))
    return out


def _gdn_head_cols(h):
    return slice(h * GDN_HEAD_DIM, (h + 1) * GDN_HEAD_DIM)


def _gdn_chunk_inputs(x_ref, aux_ref, auxt_ref, g, h):
    nh, ck = GDN_HEADS, GDN_CHUNK
    gw = nh * GDN_HEAD_DIM
    rows = slice(g * ck, (g + 1) * ck)
    cols = _gdn_head_cols(h)
    q = x_ref[rows, cols]
    k = x_ref[rows, gw + cols.start:gw + cols.stop]
    v = x_ref[rows, 2 * gw + cols.start:2 * gw + cols.stop]
    b = aux_ref[rows, nh + h:nh + h + 1]
    gc = aux_ref[rows, 2 * nh + h:2 * nh + h + 1]
    gc_row = auxt_ref[g, 2 * nh + h:2 * nh + h + 1, :]
    return q, k, v, b, gc, gc_row


def _gdn_specs(t, widths, *, reverse=False, step_chunks=None):
    rows = (step_chunks or GDN_STEP_CHUNKS) * GDN_CHUNK
    nsteps = t // rows
    idx = (lambda i: (nsteps - 1 - i, 0)) if reverse else (lambda i: (i, 0))
    return [pl.BlockSpec((rows, w), idx) for w in widths]


def _gdn_local_fwd(qkvn, aux, aux_t, *, name, exchange=None):
    t = qkvn.shape[0]
    hd, nh, ck, gs = GDN_HEAD_DIM, GDN_HEADS, GDN_CHUNK, GDN_STEP_CHUNKS
    gw = nh * hd
    host = _ExchangeHost(exchange)
    assert not (exchange and exchange.aliases)
    grid = (t // (gs * ck),)

    def body(*refs):
        x_ref, aux_ref, auxt_ref = refs[:3]
        u_ref, w_ref, qd_ref, kd_ref, mm_ref, tinv_ref = refs[3 + host.n_in:9 + host.n_in]
        host.start(refs, 3, 9 + host.n_in, grid)
        for g0 in range(0, gs, GDN_ILP_CHUNKS):
            where = [(g, h) for g in range(g0, g0 + GDN_ILP_CHUNKS) for h in range(nh)]
            ins = [_gdn_chunk_inputs(x_ref, aux_ref, auxt_ref, g, h) for g, h in where]
            lcs = _gdn_local([(q, k, b, gc, gc_row) for q, k, _, b, gc, gc_row in ins], True)
            tinvs = [lc["tinv"] for lc in lcs]
            us = _dot3_many(tinvs, [b * v for _, _, v, b, _, _ in ins])
            ws = _dot3_many(tinvs, [(b * lc["gam"]) * k for (_, k, _, b, _, _), lc in zip(ins, lcs)])
            for i, ((g, h), (q, k, _, _, _, _), lc) in enumerate(zip(where, ins, lcs)):
                rows, cols = slice(g * ck, (g + 1) * ck), _gdn_head_cols(h)
                u_ref[rows, cols] = us[i]
                w_ref[rows, cols] = ws[i].astype(w_ref.dtype)
                qd_ref[rows, cols] = (lc["gam"] * q).astype(qd_ref.dtype)
                kd_ref[rows, cols] = (lc["kdec"] * k).astype(kd_ref.dtype)
            for g in range(g0, g0 + GDN_ILP_CHUNKS):
                rows = slice(g * ck, (g + 1) * ck)
                mine = [lc for (gg, _), lc in zip(where, lcs) if gg == g]
                mm_ref[rows, :] = jnp.concatenate([lc["mm"] for lc in mine], axis=1).astype(mm_ref.dtype)
                tinv_ref[rows, :] = jnp.concatenate([lc["tinv"] for lc in mine], axis=1)
        host.wait(refs, 3, 9 + host.n_in, grid)

    sq = nh * ck
    outs = pl.pallas_call(
        body, grid=grid,
        in_specs=_gdn_specs(t, (3 * gw, LANES)) + [pl.BlockSpec((gs, 16, ck), lambda i: (i, 0, 0))] + host.in_specs,
        out_specs=_gdn_specs(t, (gw, gw, gw, gw, sq, sq)) + host.out_specs,
        out_shape=[jax.ShapeDtypeStruct((t, gw), F32)] + [jax.ShapeDtypeStruct((t, gw), BF16)] * 3
        + [jax.ShapeDtypeStruct((t, sq), BF16), jax.ShapeDtypeStruct((t, sq), F32)] + host.out_shapes,
        scratch_shapes=host.scratch,
        compiler_params=_cp(host.semantics(("parallel",))), name=name,
    )(qkvn, aux, aux_t, *host.ins)
    return (*outs[:6], list(outs[6:])) if exchange is not None else outs


def _gdn_seq_fwd(u, w, qd, kd, mm, aux, *, name):
    t = u.shape[0]
    hd, nh, ck, gs = GDN_HEAD_DIM, GDN_HEADS, GDN_CHUNK, GDN_STEP_CHUNKS
    gw = nh * hd
    sq = nh * ck

    def body(u_ref, w_ref, qd_ref, kd_ref, mm_ref, aux_ref, o_ref, vn_ref, sall_ref, s_ref):
        @pl.when(pl.program_id(0) == 0)
        def _():
            s_ref[...] = jnp.zeros_like(s_ref)

        heads = range(nh)
        hcols = [_gdn_head_cols(h) for h in heads]
        sts = [s_ref[h] for h in heads]
        for g in range(gs):
            rows = slice(g * ck, (g + 1) * ck)
            last = (g + 1) * ck - 1
            for h in heads:
                sall_ref[g, h] = sts[h]
            stbs = [st.astype(BF16) for st in sts]
            w_s = [_dot(w_ref[rows, c], stb) for c, stb in zip(hcols, stbs)]
            q_s = [_dot(qd_ref[rows, c], stb) for c, stb in zip(hcols, stbs)]
            vnbs = [(u_ref[rows, c] - ws).astype(BF16) for c, ws in zip(hcols, w_s)]
            m_v = [_dot(mm_ref[rows, h * ck:(h + 1) * ck], vnbs[h]) for h in heads]
            k_v = [_dot(kd_ref[rows, c], vnb, TN) for c, vnb in zip(hcols, vnbs)]
            for h, c in zip(heads, hcols):
                vn_ref[rows, c] = vnbs[h]
                o_ref[rows, c] = q_s[h] + m_v[h]
            gam_c = [jnp.exp(aux_ref[last:last + 1, 2 * nh + h:2 * nh + h + 1]) for h in heads]
            sts = [gam_c[h] * sts[h] + k_v[h] for h in heads]
        for h in heads:
            s_ref[h] = sts[h]

    return pl.pallas_call(
        body, grid=(t // (gs * ck),),
        in_specs=_gdn_specs(t, (gw, gw, gw, gw, sq, LANES)),
        out_specs=_gdn_specs(t, (gw, gw)) + [pl.BlockSpec((gs, nh, hd, hd), lambda i: (i, 0, 0, 0))],
        out_shape=[jax.ShapeDtypeStruct((t, gw), F32), jax.ShapeDtypeStruct((t, gw), BF16),
                   jax.ShapeDtypeStruct((t // ck, nh, hd, hd), F32)],
        scratch_shapes=[pltpu.VMEM((nh, hd, hd), F32)],
        compiler_params=_cp(("arbitrary",)), name=name,
    )(u, w, qd, kd, mm, aux)


def _gdn_seq_bwd(do, w, qd, kd, mm, vn, s_all, aux, *, name):
    t = do.shape[0]
    hd, nh, ck, gs = GDN_HEAD_DIM, GDN_HEADS, GDN_CHUNK, GDN_STEP_CHUNKS
    gw = nh * hd
    sq = nh * ck
    nsteps = t // (gs * ck)

    def body(do_ref, w_ref, qd_ref, kd_ref, mm_ref, vn_ref, sall_ref, aux_ref, dvn_ref, dqd_ref, dkd_ref, dw_ref,
             dlast_ref, ds_ref):
        @pl.when(pl.program_id(0) == 0)
        def _():
            ds_ref[...] = jnp.zeros_like(ds_ref)

        lane = lax.broadcasted_iota(jnp.int32, (ck, LANES), 1)
        rowi = lax.broadcasted_iota(jnp.int32, (ck, LANES), 0)
        heads = range(nh)
        hcols = [_gdn_head_cols(h) for h in heads]
        dsns = [ds_ref[h] for h in heads]
        for g in reversed(range(gs)):
            rows = slice(g * ck, (g + 1) * ck)
            last = (g + 1) * ck - 1
            sts = [sall_ref[g, h] for h in heads]
            stbs = [st.astype(BF16) for st in sts]
            dsbs = [dsn.astype(BF16) for dsn in dsns]
            dobs = [do_ref[rows, c].astype(BF16) for c in hcols]
            dvns = [_dot(mm_ref[rows, h * ck:(h + 1) * ck], dobs[h], TN) + _dot(kd_ref[rows, hcols[h]], dsbs[h])
                    for h in heads]
            dqds = [_dot(dob, stb, NT) for dob, stb in zip(dobs, stbs)]
            dkds = [_dot(vn_ref[rows, c], dsb, NT) for c, dsb in zip(hcols, dsbs)]
            q_o = [_dot(qd_ref[rows, c], dob, TN) for c, dob in zip(hcols, dobs)]
            dvbs = [dvn.astype(BF16) for dvn in dvns]
            dws = [_dot(dvb, stb, NT) for dvb, stb in zip(dvbs, stbs)]
            w_v = [_dot(w_ref[rows, c], dvb, TN) for c, dvb in zip(hcols, dvbs)]
            gam_c = [jnp.exp(aux_ref[last:last + 1, 2 * nh + h:2 * nh + h + 1]) for h in heads]
            dlast = jnp.zeros((ck, LANES), F32)
            for h, c in zip(heads, hcols):
                dvn_ref[rows, c] = dvns[h]
                dqd_ref[rows, c] = dqds[h]
                dkd_ref[rows, c] = dkds[h]
                dw_ref[rows, c] = -dws[h]
                dgam_c = jnp.sum(jnp.sum(dsns[h] * sts[h], axis=1, keepdims=True), axis=0, keepdims=True)
                dlast = dlast + jnp.where((rowi == ck - 1) & (lane == h), gam_c[h] * dgam_c, 0.0)
            dlast_ref[rows, :] = dlast
            dsns = [q_o[h] + gam_c[h] * dsns[h] - w_v[h] for h in heads]
        for h in heads:
            ds_ref[h] = dsns[h]

    return pl.pallas_call(
        body, grid=(nsteps,),
        in_specs=_gdn_specs(t, (gw, gw, gw, gw, sq, gw), reverse=True)
        + [pl.BlockSpec((gs, nh, hd, hd), lambda i: (nsteps - 1 - i, 0, 0, 0))] + _gdn_specs(t, (LANES,), reverse=True),
        out_specs=_gdn_specs(t, (gw, gw, gw, gw, LANES), reverse=True),
        out_shape=[jax.ShapeDtypeStruct((t, gw), F32)] * 4 + [jax.ShapeDtypeStruct((t, LANES), F32)],
        scratch_shapes=[pltpu.VMEM((nh, hd, hd), F32)],
        compiler_params=_cp(("arbitrary",)), name=name,
    )(do, w, qd, kd, mm, vn, s_all, aux)


def _gdn_local_bwd(qkvn, aux, aux_t, tinv, u, w, vn, do, dvn, dqd, dkd, dw, dlast, *, name):
    t = qkvn.shape[0]
    hd, nh, ck, gs = GDN_HEAD_DIM, GDN_HEADS, GDN_CHUNK, GDN_STEP_CHUNKS
    gw = nh * hd
    sq = nh * ck

    def body(x_ref, aux_ref, auxt_ref, tinv_ref, u_ref, w_ref, vn_ref, do_ref, dvn_ref, dqd_ref, dkd_ref, dw_ref,
             dlast_ref, dx_ref, daux_ref):
        lane = lax.broadcasted_iota(jnp.int32, (ck, LANES), 1)
        ones = jnp.ones((ck, LANES), F32)
        ii = lax.broadcasted_iota(jnp.int32, (ck, ck), 0)
        jj = lax.broadcasted_iota(jnp.int32, (ck, ck), 1)
        suffix = jnp.where(jj >= ii, 1.0, 0.0)
        for g0 in range(0, gs, GDN_ILP_CHUNKS_BWD):
            where = [(g, h) for g in range(g0, g0 + GDN_ILP_CHUNKS_BWD) for h in range(nh)]
            at = [(slice(g * ck, (g + 1) * ck), _gdn_head_cols(h)) for g, h in where]
            ins = [_gdn_chunk_inputs(x_ref, aux_ref, auxt_ref, g, h) for g, h in where]
            lcs = _gdn_local([(q, k, b, gc, gc_row) for q, k, _, b, gc, gc_row in ins], False)
            tinvs = [tinv_ref[slice(g * ck, (g + 1) * ck), h * ck:(h + 1) * ck] for g, h in where]
            dms = [jnp.where(lc["lower"], _bdot(do_ref[r, c], vn_ref[r, c], NT), 0.0) for lc, (r, c) in zip(lcs, at)]
            drvs = _dot3_many(tinvs, [dvn_ref[r, c] for r, c in at], TN)
            drks = _dot3_many(tinvs, [dw_ref[r, c] for r, c in at], TN)
            das = [jnp.where(lc["strict"], -(_bdot(drv, u_ref[r, c], NT) + _bdot(drk, w_ref[r, c], NT)), 0.0)
                   for lc, (r, c), drv, drk in zip(lcs, at, drvs, drks)]
            f_mats = [da * (i[3] * lc["kk"]) * lc["dmat"] + dm * lc["qk"] * lc["dmat"]
                      for i, lc, da, dm in zip(ins, lcs, das, dms)]
            col_sums = _dot3_many(f_mats, [ones] * len(where), TN)
            dgc_all = {g: dlast_ref[slice(g * ck, (g + 1) * ck), :] for g in range(g0, g0 + GDN_ILP_CHUNKS_BWD)}
            db_all = {g: jnp.zeros((ck, LANES), F32) for g in range(g0, g0 + GDN_ILP_CHUNKS_BWD)}
            e_mats = [da * lc["dmat"] * i[3] for i, lc, da in zip(ins, lcs, das)]
            dmds = [dm * lc["dmat"] for lc, dm in zip(lcs, dms)]
            dq_mm = [_bdot(dmd, i[1]) for i, dmd in zip(ins, dmds)]
            dk_mm = [_bdot(e, i[1]) + _bdot(e, i[1], TN) + _bdot(dmd, i[0], TN) for i, e, dmd in zip(ins, e_mats, dmds)]
            for n, ((g, h), (q, k, v, b, _, _), lc, (rows, cols)) in enumerate(zip(where, ins, lcs, at)):
                dmat, kk, gam, kdec = (lc[key] for key in ("dmat", "kk", "gam", "kdec"))
                drv, drk, da = drvs[n], drks[n], das[n]
                dqd_h, dkd_h = dqd_ref[rows, cols], dkd_ref[rows, cols]
                rs_rk = jnp.sum(drk * k, axis=-1, keepdims=True)
                db = (jnp.sum(drv * v, axis=-1, keepdims=True) + gam * rs_rk
                      + jnp.sum(da * kk * dmat, axis=-1, keepdims=True))
                dx_ref[rows, cols] = dq_mm[n] + gam * dqd_h
                dx_ref[rows, gw + cols.start:gw + cols.stop] = (b * gam) * drk + dk_mm[n] + kdec * dkd_h
                dx_ref[rows, 2 * gw + cols.start:2 * gw + cols.stop] = b * drv
                e_vec = jnp.sum(dkd_h * (kdec * k), axis=-1, keepdims=True)
                dgc = (b * gam * rs_rk + gam * jnp.sum(dqd_h * q, axis=-1, keepdims=True)
                       + jnp.sum(f_mats[n], axis=-1, keepdims=True) - col_sums[n][:, 0:1] - e_vec)
                is_last = lax.broadcasted_iota(jnp.int32, (ck, 1), 0) == ck - 1
                dgc = dgc + jnp.where(is_last, jnp.sum(e_vec, axis=0, keepdims=True), 0.0)
                dgc_all[g] = dgc_all[g] + jnp.where(lane == h, dgc, 0.0)
                db_all[g] = db_all[g] + jnp.where(lane == nh + h, db, 0.0)
            for g in dgc_all:
                daux_ref[slice(g * ck, (g + 1) * ck), :] = _dot3(suffix, dgc_all[g]) + db_all[g]

    return pl.pallas_call(
        body, grid=(t // (gs * ck),),
        in_specs=_gdn_specs(t, (3 * gw, LANES)) + [pl.BlockSpec((gs, 16, ck), lambda i: (i, 0, 0))]
        + _gdn_specs(t, (sq, gw, gw, gw, gw, gw, gw, gw, gw, LANES)),
        out_specs=_gdn_specs(t, (3 * gw, LANES)),
        out_shape=[jax.ShapeDtypeStruct((t, 3 * gw), F32), jax.ShapeDtypeStruct((t, LANES), F32)],
        compiler_params=_cp(("parallel",)), name=name,
    )(qkvn, aux, aux_t, tinv, u, w, vn, do, dvn, dqd, dkd, dw, dlast)


def _gdn_pre_bwd1(xc, dqkvn, daux, ab, alog_pad, dt_pad, dkv, dp, dp_col, *, name, ab_blk=0, tm=256):
    t, cw = xc.shape
    hd, nh = GDN_HEAD_DIM, GDN_HEADS
    gw = nh * hd
    tm = _blk(t, tm)

    kvw = dkv.shape[1]
    seg = kvw + AB_PAD
    assert dp_col % seg == 0

    def body(xc_ref, dy_ref, daux_ref, ab_ref, al_ref, dt_ref, dkv_ref, _, dxc_ref, dab_ref, dal_ref, ddt_ref):
        i = pl.program_id(0)
        xc = xc_ref[...]
        sg = _sigmoid(xc)
        s = xc * sg
        dsilu = sg * (1.0 + xc * (1.0 - sg))
        for h in range(2 * nh):
            xh = s[:, h * hd:(h + 1) * hd]
            scale = hd ** -0.5 if h < nh else 1.0
            dyh = dy_ref[:, h * hd:(h + 1) * hd] * scale
            r = lax.rsqrt(jnp.sum(xh * xh, axis=-1, keepdims=True) + L2_EPS)
            dxh = r * dyh - xh * (r * r * r) * jnp.sum(dyh * xh, axis=-1, keepdims=True)
            dxc_ref[:, h * hd:(h + 1) * hd] = dxh * dsilu[:, h * hd:(h + 1) * hd]
        dxc_ref[:, 2 * gw:] = dy_ref[:, 2 * gw:] * dsilu[:, 2 * gw:]
        abv = ab_ref[...]
        dauxv = daux_ref[...]
        lane = lax.broadcasted_iota(jnp.int32, abv.shape, 1)
        is_a = lane < nh
        is_b = (lane >= nh) & (lane < 2 * nh)
        pre = abv + dt_ref[...]
        neg_ea = -jnp.exp(al_ref[...])
        d_a = jnp.where(is_a, dauxv * neg_ea * _sigmoid(pre), 0.0)
        beta = _sigmoid(abv)
        d_b = jnp.where(is_b, dauxv * beta * (1.0 - beta), 0.0)
        dab_ref[:, :kvw] = dkv_ref[...]
        dab_ref[:, kvw:kvw + LANES] = (d_a + d_b).astype(dab_ref.dtype)
        dab_ref[:, kvw + LANES:] = jnp.zeros((tm, AB_PAD - LANES), dab_ref.dtype)
        dal = jnp.sum(jnp.where(is_a, dauxv * neg_ea * _softplus(pre), 0.0), axis=0, keepdims=True)
        ddt = jnp.sum(d_a, axis=0, keepdims=True)

        @pl.when(i == 0)
        def _():
            dal_ref[...] = dal
            ddt_ref[...] = ddt

        @pl.when(i > 0)
        def _():
            dal_ref[...] += dal
            ddt_ref[...] += ddt

    row = lambda c: pl.BlockSpec((tm, c), lambda i: (i, 0))
    vec = pl.BlockSpec((1, LANES), lambda i: (0, 0))
    return pl.pallas_call(
        body, grid=(t // tm,),
        in_specs=[row(cw), row(cw), row(LANES), pl.BlockSpec((tm, LANES), lambda i: (i, ab_blk)), vec, vec, row(kvw),
                  ANY_SPEC],
        out_specs=[row(cw), pl.BlockSpec((tm, seg), lambda i: (i, dp_col // seg)), vec, vec],
        out_shape=[jax.ShapeDtypeStruct((t, cw), F32), jax.ShapeDtypeStruct(dp.shape, dp.dtype),
                   jax.ShapeDtypeStruct((1, LANES), F32), jax.ShapeDtypeStruct((1, LANES), F32)],
        input_output_aliases={7: 1}, compiler_params=_cp(("arbitrary",)), name=name,
    )(xc, dqkvn, daux, ab, alog_pad, dt_pad, dkv, dp)


def _gdn_pre_bwd2(dxc, qkvb, conv_w, dp, dp_col, *, name, tm=512):
    t, cw = dxc.shape
    tm = _blk(t, tm)
    hb = tm // HALO
    nblk = t // tm
    cg = GDN_HEADS * GDN_HEAD_DIM
    assert cw % cg == 0 and dp_col % cg == 0
    col0 = dp_col // cg

    def body(d_ref, dn_ref, x_ref, xp_ref, w_ref, _, dx_ref, dw_ref):
        i = pl.program_id(1)
        dcur = d_ref[...]
        dnxt = jnp.where(i < nblk - 1, dn_ref[...], 0.0)
        cur = x_ref[...]
        prev = jnp.where(i > 0, xp_ref[...], 0.0)
        dx = None
        dws = []
        for tap in range(GDN_CONV):
            j = GDN_CONV - 1 - tap
            term = w_ref[tap:tap + 1, :] * _shift_up(dcur, dnxt, j)
            dx = term if dx is None else dx + term
            dws.append(jnp.sum(dcur * _shift_down(cur, prev, j), axis=0, keepdims=True))
        dx_ref[...] = dx.astype(dx_ref.dtype)
        dw = jnp.concatenate(dws, axis=0)

        @pl.when(i == 0)
        def _():
            dw_ref[...] = dw

        @pl.when(i > 0)
        def _():
            dw_ref[...] += dw

    row = pl.BlockSpec((tm, cg), lambda c, i: (i, c))
    wsp = pl.BlockSpec((GDN_CONV, cg), lambda c, i: (0, c))
    return pl.pallas_call(
        body, grid=(cw // cg, nblk),
        in_specs=[row, pl.BlockSpec((HALO, cg), lambda c, i: (jnp.minimum((i + 1) * hb, t // HALO - 1), c)),
                  row, pl.BlockSpec((HALO, cg), lambda c, i: (jnp.maximum(i * hb - 1, 0), c)), wsp, ANY_SPEC],
        out_specs=[pl.BlockSpec((tm, cg), lambda c, i: (i, col0 + c)), wsp],
        out_shape=[jax.ShapeDtypeStruct(dp.shape, dp.dtype), jax.ShapeDtypeStruct((GDN_CONV, cw), F32)],
        input_output_aliases={5: 0}, compiler_params=_cp(("arbitrary", "arbitrary")), name=name,
    )(dxc, dxc, qkvb, qkvb, conv_w, dp)


def _gdn_post_fwd(o, z, norm_w, *, name, tm=512):
    t, gw = o.shape
    hd, nh = GDN_HEAD_DIM, GDN_HEADS
    tm = _blk(t, tm)

    def body(o_ref, z_ref, w_ref, y_ref):
        zv = z_ref[...]
        sz = zv * _sigmoid(zv)
        for h in range(nh):
            oh = o_ref[:, h * hd:(h + 1) * hd]
            r = lax.rsqrt(jnp.mean(oh * oh, axis=-1, keepdims=True) + RMS_EPS)
            y_ref[:, h * hd:(h + 1) * hd] = (oh * r * w_ref[...] * sz[:, h * hd:(h + 1) * hd]).astype(y_ref.dtype)

    row = pl.BlockSpec((tm, gw), lambda i: (i, 0))
    return pl.pallas_call(
        body, grid=(t // tm,), in_specs=[row, row, pl.BlockSpec((1, hd), lambda i: (0, 0))], out_specs=row,
        out_shape=jax.ShapeDtypeStruct((t, gw), BF16), compiler_params=_cp(("parallel",)), name=name,
    )(o, z, norm_w)


def _gdn_post_bwd(dy, o, z, norm_w, dp, dp_col, *, name, tm=512):
    t, gw = o.shape
    hd, nh = GDN_HEAD_DIM, GDN_HEADS
    tm = _blk(t, tm)

    def body(dy_ref, o_ref, z_ref, w_ref, _, do_ref, dz_ref, dw_ref):
        i = pl.program_id(0)
        zv = z_ref[...]
        sg = _sigmoid(zv)
        sz = zv * sg
        dsz = sg * (1.0 + zv * (1.0 - sg))
        dw = None
        for h in range(nh):
            sl = slice(h * hd, (h + 1) * hd)
            oh = o_ref[:, sl]
            dyh = dy_ref[:, sl].astype(F32)
            r = lax.rsqrt(jnp.mean(oh * oh, axis=-1, keepdims=True) + RMS_EPS)
            xh = oh * r
            dz_ref[:, sl] = (dyh * xh * w_ref[...] * dsz[:, sl]).astype(dz_ref.dtype)
            dn = dyh * sz[:, sl]
            dxh = dn * w_ref[...]
            do_ref[:, sl] = r * (dxh - xh * jnp.mean(dxh * xh, axis=-1, keepdims=True))
            part = jnp.sum(dn * xh, axis=0, keepdims=True)
            dw = part if dw is None else dw + part

        @pl.when(i == 0)
        def _():
            dw_ref[...] = dw

        @pl.when(i > 0)
        def _():
            dw_ref[...] += dw

    row = pl.BlockSpec((tm, gw), lambda i: (i, 0))
    vec = pl.BlockSpec((1, hd), lambda i: (0, 0))
    return pl.pallas_call(
        body, grid=(t // tm,), in_specs=[row, row, row, vec, ANY_SPEC],
        out_specs=[row, pl.BlockSpec((tm, gw), lambda i: (i, dp_col // gw)), vec],
        out_shape=[jax.ShapeDtypeStruct((t, gw), F32), jax.ShapeDtypeStruct(dp.shape, dp.dtype),
                   jax.ShapeDtypeStruct((1, hd), F32)],
        input_output_aliases={4: 1}, compiler_params=_cp(("arbitrary",)), name=name,
    )(dy, o, z, norm_w, dp)


IN_NAMES = ("q_a", "kv_a", "qkv_b", "ab", "z", "q_c", "gates")
CAT_NAMES = ("gates", "q_a", "qkv_b", "z", "q_c", "kv_a", "ab")
AB_PAD = 256


def _in_widths(d):
    gw = GDN_HEADS * GDN_HEAD_DIM
    return dict(q_a=SWA_Q_HEADS * SWA_HEAD_DIM, kv_a=2 * SWA_KV_HEADS * SWA_HEAD_DIM, qkv_b=3 * gw, ab=2 * GDN_HEADS,
                z=gw, q_c=XA_HEADS * XA_HEAD_DIM, gates=3 * d)


def _ranges(names, widths):
    out, start = {}, 0
    for k in names:
        out[k] = (start, widths[k])
        start += widths[k]
    return out, start


def _cat_ranges(d):
    widths = dict(_in_widths(d), ab=AB_PAD)
    return _ranges(CAT_NAMES, widths)


def _to_cat(shards, *, name="to_cat", tm=256):
    ns, d, n = shards.shape
    src, _ = _ranges(IN_NAMES, _in_widths(d))
    _, cat_w = _cat_ranges(d)
    pieces = []
    for k in CAT_NAMES:
        lo, hi = src[k][0], src[k][0] + src[k][1]
        for s in range(ns):
            a, b = max(lo, s * n), min(hi, (s + 1) * n)
            if a < b:
                pieces.append((s, a - s * n, b - s * n))
    tm = _blk(d, tm)

    def body(s_ref, o_ref):
        cols = [s_ref[s, :, a:b] for s, a, b in pieces]
        cols.append(jnp.zeros((tm, AB_PAD - src["ab"][1]), o_ref.dtype))
        o_ref[...] = jnp.concatenate(cols, axis=1)

    return pl.pallas_call(
        body, grid=(d // tm,),
        in_specs=[pl.BlockSpec((ns, tm, n), lambda i: (0, i, 0))],
        out_specs=pl.BlockSpec((tm, cat_w), lambda i: (i, 0)),
        out_shape=jax.ShapeDtypeStruct((d, cat_w), shards.dtype),
        compiler_params=_cp(("parallel",)), name=name,
    )(shards)


def _from_cat(w_cat, *, name="from_cat", tm=256):
    d, cat_w = w_cat.shape
    src, total = _ranges(IN_NAMES, _in_widths(d))
    cat, _ = _cat_ranges(d)
    n = total // N_SHARDS
    pieces = []
    for s in range(N_SHARDS):
        pieces.append([])
        for k in IN_NAMES:
            a, b = max(s * n, src[k][0]), min((s + 1) * n, src[k][0] + src[k][1])
            if a < b:
                pieces[s].append((cat[k][0] + a - src[k][0], cat[k][0] + b - src[k][0]))
    tm = _blk(d, tm)

    def body(c_ref, o_ref):
        for s in range(N_SHARDS):
            o_ref[s] = jnp.concatenate([c_ref[:, a:b] for a, b in pieces[s]], axis=1)

    return pl.pallas_call(
        body, grid=(d // tm,),
        in_specs=[pl.BlockSpec((tm, cat_w), lambda i: (i, 0))],
        out_specs=pl.BlockSpec((N_SHARDS, tm, n), lambda i: (0, i, 0)),
        out_shape=jax.ShapeDtypeStruct((N_SHARDS, d, n), w_cat.dtype),
        compiler_params=_cp(("parallel",)), name=name,
    )(w_cat)


def _pad_cols(a, width):
    return jnp.pad(a, ((0, 0), (0, width - a.shape[1])))


def _relu2_epilogue(acc):
    r = jnp.maximum(acc, 0.0)
    return acc, r * r


def _add_epilogue(acc, res):
    return (acc + res,)


def _add_norm_epilogue(acc, res, gain):
    h = acc + res
    r = lax.rsqrt(jnp.mean(h * h, axis=-1, keepdims=True) + RMS_EPS)
    return h, h * r * gain


def _drelu2_epilogue(acc, u):
    return (acc * (2.0 * jnp.maximum(u.astype(F32), 0.0)),)


def _local_step(x, mem, tgt, wts, small, comm=None):
    t, d = x.shape
    nh = GDN_HEADS
    cat, cat_w = _cat_ranges(d)
    alog_pad = _pad_cols(small["a_log"], LANES)
    dt_pad = _pad_cols(small["dt_bias"], LANES)
    kvw = cat["kv_a"][1]
    assert cat["ab"][0] == cat["kv_a"][0] + kvw
    ab_blk = kvw // LANES

    if comm is None:
        n = _rms_fwd(x, small["g_mix"], name="rms_mix")
        w_cat = wts["w_cat"]
    else:
        n, landed = _rms_fwd(x, small["g_mix"], name="rms_mix", exchange=comm.gather_exchange(["w_in"]))
        w_cat = _to_cat(_exchange_call(comm.pass_on(["w_in"], landed), name="ag_w_in_pass")[0])
    assert w_cat.shape == (d, cat_w)
    landed_mlp = None

    def proj(k, **kw):
        nonlocal landed_mlp
        if comm is None:
            return _mm(n, w_cat, **kw)
        out, landed_mlp = _mm(n, w_cat, exchange=comm.gather_exchange(comm.MLP[1:], (k, 4), landed_mlp), **kw)
        return out

    q_a = proj(0, b_window=cat["q_a"], out_dtypes=(BF16,), name="in_q_a")
    kv_a, ab = _mm(n, w_cat, b_window=(cat["kv_a"][0], kvw + AB_PAD), out_dtypes=(BF16, F32), name="in_kv_ab")
    qkvb = proj(1, b_window=cat["qkv_b"], tn=512, name="in_qkv_b")
    z = _mm(n, w_cat, b_window=cat["z"], name="in_z")
    q_c = proj(2, b_window=cat["q_c"], out_dtypes=(BF16,), name="in_q_c")
    gates = proj(3, b_window=cat["gates"], name="in_gates")
    if comm is None:
        y_a, lse = _swa_fwd(q_a, kv_a, small["sinks"], name="swa_fwd")
    else:
        y_a, lse, landed = _swa_fwd(q_a, kv_a, small["sinks"], name="swa_fwd", exchange=comm.gather_exchange(comm.MLP[:1]))
        landed_mlp = landed + landed_mlp
    xc, qkvn, aux = _gdn_pre_fwd(qkvb, small["conv_w"], ab, alog_pad, dt_pad, ab_blk=ab_blk, name="gdn_pre_fwd")
    aux_t = aux[:, :16].reshape(t // GDN_CHUNK, GDN_CHUNK, 16).transpose(0, 2, 1)
    if comm is None:
        gdn_u, gdn_w, gdn_qd, gdn_kd, gdn_mm, gdn_tinv = _gdn_local_fwd(qkvn, aux, aux_t, name="gdn_local_fwd")
    else:
        gdn_u, gdn_w, gdn_qd, gdn_kd, gdn_mm, gdn_tinv, landed_mid = _gdn_local_fwd(
            qkvn, aux, aux_t, name="gdn_local_fwd", exchange=comm.gather_exchange(comm.mid))
        wts = dict(wts, **comm.gathered(comm.mid, landed_mid, "mid"))
    o_b, gdn_vn, s_all = _gdn_seq_fwd(gdn_u, gdn_w, gdn_qd, gdn_kd, gdn_mm, aux, name="gdn_seq_fwd")
    y_b = _gdn_post_fwd(o_b, z, small["gdn_norm_w"], name="gdn_post_fwd")
    nmem = _rms_fwd(mem, small["g_mem"], name="rms_mem")
    mkv = _mm(nmem, wts["w_mem_kv"], out_dtypes=(BF16,), name="mem_kv")
    y_c = _xa_fwd(q_c, mkv, name="xa_fwd")
    ys = (y_a, y_b, y_c)
    w_ups = (wts["w_swa_up"], wts["w_gdn_up"], wts["w_xa_up"])
    merged = _merge_fwd(ys, w_ups, gates, name="merge_fwd")
    proj = dict(extras=(x, small["g_mlp"]), epilogue=_add_norm_epilogue, out_dtypes=(F32, BF16), tm=512, tn=d,
                name="out_proj")
    if comm is None:
        h1, n2 = _mm(merged, wts["w_out"], **proj)
    else:
        (h1, n2), whole = _mm(merged, wts["w_out"], exchange=comm.pass_on(comm.MLP, landed_mlp), **proj)
        wts = dict(wts, **comm.as_weights(comm.MLP, whole))
    u, act = _mm(n2, wts["w_mlp_in"], b_sharded=True, out_dtypes=(BF16, BF16), epilogue=_relu2_epilogue, name="mlp_in")
    h2 = _mm(act, wts["w_mlp_out"], extras=(h1,), epilogue=_add_epilogue, name="mlp_out")
    dh2, dh2_b, dg_final, loss = _final_loss(h2, small["g_final"], tgt, name="final_loss")

    grads = {"g_final": dg_final}
    du = _mm(dh2_b, wts["w_mlp_out"], tb=True, out_dtypes=(BF16,), extras=(u,), epilogue=_drelu2_epilogue, name="d_mlp_act")
    grads["w_mlp_out"] = _mm(act, dh2_b, ta=True, out_dtypes=(BF16,), name="dw_mlp_out")
    grads["w_mlp_in"] = _mm(n2, du, ta=True, out_sharded=True, out_dtypes=(BF16,), name="dw_mlp_in")
    if comm is None:
        dn2 = _mm(du, wts["w_mlp_in"], tb=True, b_sharded=True, name="d_mlp_in")
    else:
        g_mlp = [comm.shard_major(k, grads.pop(k)) for k in comm.MLP]
        dn2, sib_mlp = _mm(du, wts["w_mlp_in"], tb=True, b_sharded=True, name="d_mlp_in", exchange=_sibling_halves(g_mlp))
        s1_mlp = comm.pair_sums(g_mlp, "mlp", sib_mlp)
    dh1, dh1_b, grads["g_mlp"] = _rms_bwd(dn2, h1, small["g_mlp"], dh2, name="rms_mlp_bwd")
    dmerged = _mm(dh1_b, wts["w_out"], tb=True, name="d_out_proj")
    grads["w_out"] = _mm(merged, dh1_b, ta=True, out_dtypes=(BF16,), name="dw_out")
    *dus, dp = _merge_bwd(ys, w_ups, gates, dmerged, cat_w, name="merge_bwd")
    dys = []
    for y, du_i, w_up, key in zip(ys, dus, w_ups, ("w_swa_up", "w_gdn_up", "w_xa_up")):
        dys.append(_mm(du_i, w_up, tb=True, b_sharded=True, out_dtypes=(BF16,), name="d_" + key))
        grads[key] = _mm(y, du_i, ta=True, out_sharded=True, out_dtypes=(BF16,), name="dw_" + key[2:])
    dp, dkv_a, grads["sinks"] = _swa_bwd(q_a, kv_a, small["sinks"], y_a, lse, dys[0], dp, cat["q_a"][0], name="swa_bwd")
    do_b, dp, grads["gdn_norm_w"] = _gdn_post_bwd(dys[1], o_b, z, small["gdn_norm_w"], dp, cat["z"][0],
                                                  name="gdn_post_bwd")
    dvn, dqd, dkd, dw_, dlast = _gdn_seq_bwd(do_b, gdn_w, gdn_qd, gdn_kd, gdn_mm, gdn_vn, s_all, aux, name="gdn_seq_bwd")
    dqkvn, daux = _gdn_local_bwd(qkvn, aux, aux_t, gdn_tinv, gdn_u, gdn_w, gdn_vn, do_b, dvn, dqd, dkd, dw_, dlast,
                                 name="gdn_local_bwd")
    dxc, dp, dalog, ddt = _gdn_pre_bwd1(xc, dqkvn, daux, ab, alog_pad, dt_pad, dkv_a, dp, cat["kv_a"][0], ab_blk=ab_blk,
                                        name="gdn_pre_bwd1")
    grads["a_log"], grads["dt_bias"] = dalog[:, :nh], ddt[:, :nh]
    dp, grads["conv_w"] = _gdn_pre_bwd2(dxc, qkvb, small["conv_w"], dp, cat["qkv_b"][0], name="gdn_pre_bwd2")
    dp, dmkv = _xa_bwd(q_c, mkv, dys[2], dp, cat["q_c"][0], name="xa_bwd")
    grads["w_mem_kv"] = _mm(nmem, dmkv, ta=True, out_dtypes=(BF16,), name="dw_mem_kv")
    dnmem = _mm(dmkv, wts["w_mem_kv"], tb=True, name="d_mem_kv")
    _, _, grads["g_mem"] = _rms_bwd(dnmem, mem, small["g_mem"], jnp.zeros_like(mem), name="rms_mem_bwd")
    if comm is None:
        grads["w_cat"] = _mm(n, dp, ta=True, out_dtypes=(BF16,), name="dw_in")
        dn = _mm(dp, w_cat, tb=True, name="d_in_proj")
    else:
        s1_mid = comm.pair_sums([comm.shard_major(k, grads.pop(k)) for k in comm.mid], "mid")
        dw_cat, rcv_mlp = _mm(n, dp, ta=True, out_dtypes=(BF16,), name="dw_in", exchange=_chip_exchange(s1_mlp))
        s1_in = comm.pair_sums([_from_cat(dw_cat)], "in")
        dn, rcv_rest = _mm(dp, w_cat, tb=True, name="d_in_proj", exchange=_chip_exchange(s1_in + s1_mid))
        halves = comm.chip_sums(s1_in + s1_mid + s1_mlp, rcv_rest + rcv_mlp)
        reduced = _exchange_call(_join_halves(halves), name="rs_join_halves")
        grads.update(zip(["w_in"] + comm.mid + list(comm.MLP), reduced))
    dx, _, grads["g_mix"] = _rms_bwd(dn, x, small["g_mix"], dh1, name="rms_mix_bwd")
    return loss, dx, grads


HBM_SPEC = pl.BlockSpec(memory_space=pltpu.HBM)
VMEM_SPEC = pl.BlockSpec(memory_space=pltpu.VMEM)
N_CHIPS = N_SHARDS
N_DEV = 8
DMA_CHUNK_BYTES = 1 << 20


def _place():
    return lax.axis_index("x"), lax.axis_index("y"), lax.axis_index("c")


def _other_chips(x, y):
    return [(1 - x, y), (x, 1 - y), (1 - x, 1 - y)]


def _n_chunks(rows, row_bytes):
    n = 1
    while rows % (2 * n) == 0 and (rows // (2 * n)) % 16 == 0 and (rows // n) * row_bytes > DMA_CHUNK_BYTES:
        n *= 2
    return n


def _sem_scratch(n_remote, n_local):
    return [pltpu.SemaphoreType.DMA((max(n_remote, 1),)), pltpu.SemaphoreType.DMA((max(n_remote, 1),)),
            pltpu.SemaphoreType.DMA((max(n_local, 1),))]


def _gather_over_ici(shards, part=(0, 1), into=None):
    plan = _half_chunks(shards, 0)[part[0]::part[1]]
    assert plan, part

    def copies_of(in_refs, out_refs, place):
        x, y, c = place
        remote = []
        for i, r0, nr in plan:
            mine = pl.ds(c * (shards[i].shape[0] // 2) + r0, nr)
            for chip in _other_chips(x, y):
                remote.append((in_refs[i].at[mine], out_refs[i].at[2 * x + y, mine], (*chip, c)))
        return remote, []

    shapes = tuple(jax.ShapeDtypeStruct((N_CHIPS, *s.shape), s.dtype) for s in shards)
    if into is None:
        return Exchange(tuple(shards), shapes, 3 * len(plan), 0, copies_of)
    aliases = tuple((len(shards) + i, i) for i in range(len(shards)))
    return Exchange(tuple(shards) + tuple(into), shapes, 3 * len(plan), 0, copies_of, aliases)


def _gather_pass_on(arrived):
    plan = _half_chunks([jax.ShapeDtypeStruct(a.shape[1:], a.dtype) for a in arrived], 0)

    def copies_of(in_refs, out_refs, place):
        x, y, c = place
        remote = []
        for i, r0, nr in plan:
            mine = pl.ds(c * (arrived[i].shape[1] // 2) + r0, nr)
            for chip in _other_chips(x, y):
                rows = out_refs[i].at[2 * chip[0] + chip[1], mine]
                remote.append((rows, rows, (x, y, 1 - c)))
        return remote, []

    shapes = tuple(jax.ShapeDtypeStruct(a.shape, a.dtype) for a in arrived)
    return Exchange(tuple(arrived), shapes, 3 * len(plan), 0, copies_of, tuple((i, i) for i in range(len(arrived))))


def _place_own(arrived, shard, chip, *, name):
    r, c = shard.shape
    tb = _row_block(r, c, shard.dtype.itemsize)

    def body(chip_ref, s_ref, _, o_ref):
        o_ref[...] = s_ref[...]

    return pl.pallas_call(
        body, grid_spec=pltpu.PrefetchScalarGridSpec(
            num_scalar_prefetch=1, grid=(r // tb,),
            in_specs=[pl.BlockSpec((tb, c), lambda i, chip_ref: (i, 0)), ANY_SPEC],
            out_specs=pl.BlockSpec((None, tb, c), lambda i, chip_ref: (chip_ref[0], i, 0))),
        out_shape=jax.ShapeDtypeStruct(arrived.shape, arrived.dtype), input_output_aliases={2: 0},
        compiler_params=_cp(("parallel",)), name=name,
    )(chip, shard, arrived)


def _exchange_call(ex, *, name):
    n_in, n_out = len(ex.ins), len(ex.out_shapes)

    def body(*refs):
        cps = _exchange_copies(ex, refs[:n_in], refs[n_in:n_in + n_out], refs[n_in + n_out:])
        for cp in cps:
            cp.start()
        for cp in cps:
            cp.wait()

    return pl.pallas_call(
        body, out_shape=list(ex.out_shapes), in_specs=[HBM_SPEC] * n_in, out_specs=[HBM_SPEC] * n_out,
        scratch_shapes=_sem_scratch(ex.n_remote, ex.n_local), input_output_aliases=dict(ex.aliases), name=name,
    )(*ex.ins)


def _half_chunks(arrs, row_axis):
    plan = []
    for i, a in enumerate(arrs):
        rh = a.shape[row_axis] // 2
        row_bytes = a.dtype.itemsize * math.prod(a.shape) // a.shape[row_axis]
        nch = _n_chunks(rh, row_bytes)
        plan += [(i, q * (rh // nch), rh // nch) for q in range(nch)]
    return plan


def _sibling_halves(gs):
    plan = _half_chunks(gs, 1)

    def copies_of(in_refs, out_refs, place):
        x, y, c = place
        out = []
        for i, r0, nr in plan:
            rh = gs[i].shape[1] // 2
            out.append((in_refs[i].at[:, pl.ds((1 - c) * rh + r0, nr), :], out_refs[i].at[:, pl.ds(r0, nr), :],
                        (x, y, 1 - c)))
        return out, []

    shapes = tuple(jax.ShapeDtypeStruct((g.shape[0], g.shape[1] // 2, g.shape[2]), g.dtype) for g in gs)
    return Exchange(tuple(gs), shapes, len(plan), 0, copies_of)


def _chip_exchange(s1s):
    plan = _half_chunks([jax.ShapeDtypeStruct((2 * s.shape[1], s.shape[2]), s.dtype) for s in s1s], 0)

    def copies_of(in_refs, out_refs, place):
        x, y, c = place
        out = []
        for i, r0, nr in plan:
            for j, chip in enumerate(_other_chips(x, y)):
                out.append((in_refs[i].at[2 * chip[0] + chip[1], pl.ds(r0, nr), :], out_refs[i].at[j, pl.ds(r0, nr), :],
                            (*chip, c)))
        return out, []

    shapes = tuple(jax.ShapeDtypeStruct((3, *s.shape[1:]), s.dtype) for s in s1s)
    return Exchange(tuple(s1s), shapes, 3 * len(plan), 0, copies_of)


def _join_halves(gs):
    plan = _half_chunks(gs, 0)

    def copies_of(in_refs, out_refs, place):
        x, y, c = place
        out = []
        for i, r0, nr in plan:
            rows = out_refs[i].at[pl.ds(c * (gs[i].shape[0] // 2) + r0, nr), :]
            out.append((rows, rows, (x, y, 1 - c)))
        return out, []

    shapes = tuple(jax.ShapeDtypeStruct(g.shape, g.dtype) for g in gs)
    aliases = tuple((i, i) for i in range(len(gs)))
    return Exchange(tuple(gs), shapes, len(plan), 0, copies_of, aliases)


ROW_BLOCK_BYTES = 4 << 20


def _row_block(rows, cols, itemsize=4):
    tb = rows
    while tb % 32 == 0 and tb * cols * itemsize > ROW_BLOCK_BYTES:
        tb //= 2
    return tb


def _pair_sum(g, sib, core, *, name):
    ns, r, c = g.shape
    rh = r // 2
    tb = _row_block(rh, c, g.dtype.itemsize)
    nb = rh // tb

    def body(core_ref, g_ref, s_ref, o_ref):
        o_ref[...] = (g_ref[...].astype(F32) + s_ref[...].astype(F32)).astype(o_ref.dtype)

    mine = pl.BlockSpec((None, tb, c), lambda s, i, core_ref: (s, core_ref[0] * nb + i, 0))
    half = pl.BlockSpec((None, tb, c), lambda s, i, core_ref: (s, i, 0))
    return pl.pallas_call(
        body, grid_spec=pltpu.PrefetchScalarGridSpec(num_scalar_prefetch=1, grid=(ns, nb), in_specs=[mine, half],
                                                     out_specs=half),
        out_shape=jax.ShapeDtypeStruct((ns, rh, c), BF16), compiler_params=_cp(("parallel", "parallel")), name=name,
    )(core, g, sib)


def _chip_sum(s1, rcv, where, *, name):
    _, rh, c = s1.shape
    tb = _row_block(rh, c)
    nb = rh // tb

    def body(where_ref, own_ref, r0_ref, r1_ref, r2_ref, o_ref):
        acc = own_ref[...].astype(F32)
        for r in (r0_ref, r1_ref, r2_ref):
            acc = acc + r[...].astype(F32)
        o_ref[...] = acc

    own = pl.BlockSpec((None, tb, c), lambda i, w: (w[1], i, 0))
    got = [pl.BlockSpec((None, tb, c), functools.partial(lambda i, w, j: (j, i, 0), j=j)) for j in range(3)]
    return pl.pallas_call(
        body, grid_spec=pltpu.PrefetchScalarGridSpec(
            num_scalar_prefetch=1, grid=(nb,), in_specs=[own] + got,
            out_specs=pl.BlockSpec((tb, c), lambda i, w: (w[0] * nb + i, 0))),
        out_shape=jax.ShapeDtypeStruct((2 * rh, c), F32), compiler_params=_cp(("parallel",)), name=name,
    )(where, s1, rcv, rcv, rcv)


def _all_gather_small(blk, *, name):
    r = blk.shape[0]

    def body(b_ref, out_ref, send_sems, recv_sems):
        x, y, c = _place()
        me = 4 * x + 2 * y + c
        out_ref[me] = b_ref[...]
        sends = []
        for k in range(1, N_DEV):
            peer = (x ^ (k >> 2), y ^ ((k >> 1) & 1), c ^ (k & 1))
            sends.append(pltpu.make_async_remote_copy(src_ref=b_ref, dst_ref=out_ref.at[me], send_sem=send_sems.at[k - 1],
                                                      recv_sem=recv_sems.at[k - 1], device_id=peer, device_id_type=MESH))
        for cp in sends:
            cp.start()
        for k in range(1, N_DEV):
            rows = out_ref.at[me ^ k]
            pltpu.make_async_remote_copy(src_ref=rows, dst_ref=rows, send_sem=send_sems.at[k - 1],
                                         recv_sem=recv_sems.at[k - 1], device_id=(x, y, c), device_id_type=MESH).wait_recv()
        for cp in sends:
            cp.wait_send()

    return pl.pallas_call(
        body, out_shape=jax.ShapeDtypeStruct((N_DEV, r, LANES), blk.dtype), in_specs=[VMEM_SPEC], out_specs=VMEM_SPEC,
        scratch_shapes=[pltpu.SemaphoreType.DMA((N_DEV - 1,)), pltpu.SemaphoreType.DMA((N_DEV - 1,))],
        name=name,
    )(blk)


def _sum_rows(parts, out_dtype, *, name, tb=1024):
    rows = parts[0].shape[0]
    tb = _blk(rows, tb)

    def body(*refs):
        acc = refs[0][...].astype(F32)
        for r in refs[1:-1]:
            acc = acc + r[...].astype(F32)
        refs[-1][...] = acc.astype(refs[-1].dtype)

    spec = pl.BlockSpec((tb, LANES), lambda i: (i, 0))
    return pl.pallas_call(
        body, grid=(rows // tb,), in_specs=[spec] * len(parts), out_specs=spec,
        out_shape=jax.ShapeDtypeStruct((rows, LANES), out_dtype), compiler_params=_cp(("parallel",)), name=name,
    )(*parts)


BIG = (
    ("w_in", 1), ("w_mem_kv", 0), ("w_swa_up", 1), ("w_gdn_up", 1), ("w_xa_up", 1), ("w_out", 0), ("w_mlp_in", 1),
    ("w_mlp_out", 0))


class _Comm:
    MLP = ("w_mlp_in", "w_mlp_out")

    def __init__(self, late_shards, core, where):
        self.axis = dict(BIG)
        self.late_shards = late_shards
        self.mid = [k for k in late_shards if k not in self.MLP and k != "w_in"]
        self.core, self.where = core, where

    def gather_exchange(self, names, part=(0, 1), into=None):
        return _gather_over_ici([self.late_shards[k] for k in names], part, into)

    def as_weights(self, names, whole):
        return {k: (g.reshape(-1, g.shape[2]) if self.axis[k] == 0 else g) for k, g in zip(names, whole)}

    def gathered(self, names, landed, tag):
        return self.as_weights(names, _exchange_call(self.pass_on(names, landed), name=f"ag_{tag}_pass"))

    def pass_on(self, names, landed):
        chip = self.where[1:2]
        return _gather_pass_on([_place_own(a, self.late_shards[k], chip, name=f"ag_own_{k}") for k, a in zip(names, landed)])

    def shard_major(self, k, grad):
        return grad.reshape(N_CHIPS, -1, grad.shape[-1]) if self.axis[k] == 0 else grad

    def pair_sums(self, gs, tag, sibs=None):
        if sibs is None:
            sibs = _exchange_call(_sibling_halves(gs), name=f"rs_sibling_{tag}")
        return [_pair_sum(g, s, self.core, name=f"rs_pair_sum_{tag}{i}") for i, (g, s) in enumerate(zip(gs, sibs))]

    def chip_sums(self, s1s, rcvs):
        return [_chip_sum(s1, rcv, self.where, name=f"rs_chip_sum_{i}") for i, (s1, rcv) in enumerate(zip(s1s, rcvs))]
SMALL = ("g_mix", "sinks", "a_log", "dt_bias", "gdn_norm_w", "g_mem", "g_mlp", "g_final")


def _rows128(a, rows):
    flat = a.reshape(-1)
    return jnp.pad(flat, (0, rows * LANES - flat.shape[0])).reshape(rows, LANES)


def kernel(x, mem, g_mix, w_in, sinks, conv_w, a_log, dt_bias, gdn_norm_w, g_mem, w_mem_kv, w_swa_up, w_gdn_up, w_xa_up, w_out, g_mlp, w_mlp_in, w_mlp_out, g_final, loss_target, m_g_mix, m_w_in, m_sinks, m_conv_w, m_a_log, m_dt_bias, m_gdn_norm_w, m_g_mem, m_w_mem_kv, m_w_swa_up, m_w_gdn_up, m_w_xa_up, m_w_out, m_g_mlp, m_w_mlp_in, m_w_mlp_out, m_g_final, v_g_mix, v_w_in, v_sinks, v_conv_w, v_a_log, v_dt_bias, v_gdn_norm_w, v_g_mem, v_w_mem_kv, v_w_swa_up, v_w_gdn_up, v_w_xa_up, v_w_out, v_g_mlp, v_w_mlp_in, v_w_mlp_out, v_g_final):
    given = dict(locals())
    xi, yi, ci = _place()
    chip = 2 * xi + yi
    core = jnp.reshape(ci, (1,)).astype(jnp.int32)
    where = jnp.stack([ci, chip]).astype(jnp.int32)

    comm = _Comm({k: given[k][0].astype(BF16) for k, _ in BIG}, core, where)
    wts = {}
    conv_shard = conv_w[0]
    conv_rows = -(-conv_shard.size // (8 * LANES)) * 8
    conv_all = _all_gather_small(_rows128(conv_shard, conv_rows), name="ag_conv")
    conv_full = jnp.concatenate(
        [conv_all[2 * s].reshape(-1)[:conv_shard.size].reshape(conv_shard.shape) for s in range(N_CHIPS)], axis=1)

    small = {k: given[k].reshape(1, -1) for k in SMALL}
    small["conv_w"] = conv_full
    loss_row, dx, grads = _local_step(x[0], mem[0], loss_target[0], wts, small, comm)
    big_grads = {k: grads[k] for k, _ in BIG}

    layout = [("loss", loss_row[:, :1])] + [(k, grads[k]) for k in SMALL] + [("conv_w", grads["conv_w"])]
    rows = [-(-a.size // LANES) for _, a in layout]
    blk_rows = -(-sum(rows) // 8) * 8
    blk = jnp.concatenate([_rows128(a.astype(F32), n) for (_, a), n in zip(layout, rows)]
                          + [jnp.zeros((blk_rows - sum(rows), LANES), F32)], axis=0)
    gathered = _all_gather_small(blk, name="ag_small_grads")
    reduced = _sum_rows([gathered[i] for i in range(N_DEV)], F32, name="small_grad_sum")
    small_grads, start = {}, 0
    for (k, a), n in zip(layout, rows):
        small_grads[k] = reduced[start:start + n].reshape(-1)[:a.size].reshape(a.shape)
        start += n
    loss = small_grads["loss"].reshape(())
    cw = conv_shard.shape[1]
    conv_grad = lax.dynamic_slice_in_dim(small_grads["conv_w"], chip * cw, cw, axis=1)

    names = ["g_mix", "w_in", "sinks", "conv_w", "a_log", "dt_bias", "gdn_norm_w", "g_mem", "w_mem_kv", "w_swa_up",
             "w_gdn_up", "w_xa_up", "w_out", "g_mlp", "w_mlp_in", "w_mlp_out", "g_final"]
    out_g, out_d, out_m, out_v = [], [], [], []
    for k in names:
        w, m, v = given[k], given["m_" + k], given["v_" + k]
        if k in big_grads:
            g2 = big_grads[k]
        elif k == "conv_w":
            g2 = conv_grad
        else:
            g2 = small_grads[k]
        as_given = (lambda a: a.reshape(1, -1)) if w.ndim == 1 else (lambda a: a)
        if w.shape[-1] % LANES and w.shape[-1] > LANES:
            tr = lambda a: jnp.swapaxes(a, -1, -2)
            g_out, delta, new_m, new_v = (tr(a) for a in _adamw(tr(w), tr(g2), tr(m), tr(v), name="adamw_" + k))
        else:
            g_out, delta, new_m, new_v = _adamw(as_given(w), g2, as_given(m), as_given(v), name="adamw_" + k)
        out_g.append(g_out.reshape(w.shape))
        out_d.append(delta.reshape(w.shape))
        out_m.append(new_m.reshape(w.shape))
        out_v.append(new_v.reshape(w.shape))
    return (loss, dx[None], *out_g, *out_d, *out_m, *out_v)
```
